```python
import math
import jax
import jax.numpy as jnp
from jax import lax
import numpy as np

D_MODEL = 1024
BATCH = 8
SEQ = 8192
DEPTH = 2

SSM_HEADS = 16
SSM_HEAD_DIM = 64
SSM_INNER = SSM_HEADS * SSM_HEAD_DIM
SSM_GROUPS = 2
SSM_STATE = 128
SSM_CONV = 4
SSM_CHUNK = 128
SSM_CONV_DIM = SSM_INNER + 2 * SSM_GROUPS * SSM_STATE

GMLP_GROUPS = 8
GMLP_GROUP_DIM = 128
GMLP_INNER = GMLP_GROUPS * GMLP_GROUP_DIM
GMLP_CHUNK = 128

MIX_WIDTH = SSM_INNER + GMLP_INNER
IN_EVEN = SSM_INNER + SSM_CONV_DIM + SSM_HEADS + 2 * GMLP_INNER

ATTN_HEADS = 16
ATTN_KV_HEADS = 2
ATTN_HEAD_DIM = 64
ATTN_Q_PER_KV = ATTN_HEADS // ATTN_KV_HEADS
WINDOW = 128
ATTN_BLOCK = 128
QKV_DIM = (ATTN_HEADS + 2 * ATTN_KV_HEADS) * ATTN_HEAD_DIM
REL_BUCKETS = 32
REL_MAX_DIST = 128

FFN_HIDDEN = -(-8 * D_MODEL // (3 * 256)) * 256

N_EVEN = (DEPTH + 1) // 2
N_ODD = DEPTH // 2
EPS = 1e-6
NEG_INF = -1e30

kernel_name = "hybrid_ssd_gmlp_swa_adaln_trunk"


def rms_norm(x, w):
    xf = x.astype(jnp.float32)
    y = xf * lax.rsqrt(jnp.mean(xf * xf, axis=-1, keepdims=True) + EPS)
    return (y * w.astype(jnp.float32)).astype(x.dtype)


def layer_norm(x, w, b):
    xf = x.astype(jnp.float32)
    mu = jnp.mean(xf, axis=-1, keepdims=True)
    var = jnp.mean(jnp.square(xf - mu), axis=-1, keepdims=True)
    y = (xf - mu) * lax.rsqrt(var + EPS)
    return (y * w.astype(jnp.float32) + b.astype(jnp.float32)).astype(x.dtype)


def modulate(h, shift, scale):
    return h * (1 + scale[:, None, :]) + shift[:, None, :]


def causal_dwconv(x, w, b):
    out = lax.conv_general_dilated(
        x, w[:, None, :].astype(x.dtype), window_strides=(1,),
        padding=[(SSM_CONV - 1, 0)], dimension_numbers=('NWC', 'WIO', 'NWC'),
        feature_group_count=x.shape[-1])
    return out + b


def ssd_chunked(x, dt, a, bmat, cmat):
    f32 = jnp.float32
    bsz, seq, nh, hd = x.shape
    ng, ns = bmat.shape[-2:]
    hpg = nh // ng
    nc = seq // SSM_CHUNK
    L = SSM_CHUNK
    xd = (x.astype(f32) * dt[..., None]).reshape(bsz, nc, L, ng, hpg, hd)
    la = jnp.moveaxis((dt * a).reshape(bsz, nc, L, ng, hpg), 2, -1)
    la_cum = jnp.cumsum(la, axis=-1)
    bc = bmat.astype(f32).reshape(bsz, nc, L, ng, ns)
    cc = cmat.astype(f32).reshape(bsz, nc, L, ng, ns)
    causal = jnp.tril(jnp.ones((L, L), dtype=bool))
    seg = la_cum[..., :, None] - la_cum[..., None, :]
    decay = jnp.where(causal, jnp.exp(jnp.where(causal, seg, 0.0)), 0.0)
    cb = jnp.einsum('bclgn,bcsgn->bcgls', cc, bc)
    y_diag = jnp.einsum('bcgjls,bcsgjp->bclgjp', cb[:, :, :, None] * decay, xd)
    decay_to_end = jnp.exp(la_cum[..., -1:] - la_cum)
    states = jnp.einsum('bcsgn,bcgjs,bcsgjp->bcgjpn', bc, decay_to_end, xd)
    chunk_decay = jnp.exp(la_cum[..., -1])

    def step(carry, inp):
        st, dec = inp
        return carry * dec[..., None, None] + st, carry

    init = jnp.zeros((bsz, ng, hpg, hd, ns), f32)
    _, prev = lax.scan(step, init, (jnp.moveaxis(states, 1, 0), jnp.moveaxis(chunk_decay, 1, 0)))
    prev = jnp.moveaxis(prev, 0, 1)
    y_off = jnp.einsum('bclgn,bcgjpn,bcgjl->bclgjp', cc, prev, jnp.exp(la_cum))
    return (y_diag + y_off).reshape(bsz, seq, nh, hd)


def ssd_branch(z, xbc, dt_raw, conv_w, conv_b, dt_bias, a_log, d_skip, norm_w):
    f32 = jnp.float32
    bsz, seq, _ = z.shape
    xbc = jax.nn.silu(causal_dwconv(xbc, conv_w, conv_b))
    xs, bm, cm = jnp.split(xbc, [SSM_INNER, SSM_INNER + SSM_GROUPS * SSM_STATE], axis=-1)
    xs = xs.reshape(bsz, seq, SSM_HEADS, SSM_HEAD_DIM)
    bm = bm.reshape(bsz, seq, SSM_GROUPS, SSM_STATE)
    cm = cm.reshape(bsz, seq, SSM_GROUPS, SSM_STATE)
    dt = jax.nn.softplus(dt_raw.astype(f32) + dt_bias.astype(f32))
    a = -jnp.exp(a_log.astype(f32))
    y = ssd_chunked(xs, dt, a, bm, cm) + d_skip.astype(f32)[:, None] * xs.astype(f32)
    y = y.reshape(bsz, seq, SSM_INNER) * jax.nn.silu(z.astype(f32))
    y = y.reshape(bsz, seq, SSM_GROUPS, SSM_INNER // SSM_GROUPS)
    y = y * lax.rsqrt(jnp.mean(y * y, axis=-1, keepdims=True) + EPS)
    return (y.reshape(bsz, seq, SSM_INNER) * norm_w.astype(f32)).astype(z.dtype)


def spatial_gating_branch(u, v, ln_w, ln_b, w_s, b_s):
    bsz, seq, _ = u.shape
    nc = seq // GMLP_CHUNK
    u = jax.nn.gelu(u, approximate=False)
    v = layer_norm(jax.nn.gelu(v, approximate=False), ln_w, ln_b)
    v = v.reshape(bsz, nc, GMLP_CHUNK, GMLP_GROUPS, GMLP_GROUP_DIM)
    w = w_s * jnp.tril(jnp.ones((GMLP_CHUNK, GMLP_CHUNK), w_s.dtype))
    sv = jnp.einsum('gts,bcsgd->bctgd', w, v) + b_s.T[None, None, :, :, None]
    return u * sv.reshape(bsz, seq, GMLP_INNER)


def even_mixer(h, in_w, conv_w, conv_b, dt_bias, a_log, d_skip, ssm_norm_w,
               ln_w, ln_b, w_s, b_s, out_w):
    proj = h @ in_w
    o1 = SSM_INNER
    o2 = o1 + SSM_CONV_DIM
    o3 = o2 + SSM_HEADS
    o4 = o3 + GMLP_INNER
    z, xbc, dt_raw, u, v = jnp.split(proj, [o1, o2, o3, o4], axis=-1)
    ya = ssd_branch(z, xbc, dt_raw, conv_w, conv_b, dt_bias, a_log, d_skip, ssm_norm_w)
    yb = spatial_gating_branch(u, v, ln_w, ln_b, w_s, b_s)
    return jnp.concatenate([ya, yb], axis=-1) @ out_w


def t5_relative_bias(table):
    qi = jnp.arange(ATTN_BLOCK)[:, None]
    sj = jnp.arange(2 * ATTN_BLOCK)[None, :]
    dist = jnp.maximum(qi + ATTN_BLOCK - sj, 0)
    max_exact = REL_BUCKETS // 2
    log_ratio = (jnp.log(jnp.maximum(dist, 1).astype(jnp.float32) / max_exact)
                 / math.log(REL_MAX_DIST / max_exact))
    large = max_exact + (log_ratio * (REL_BUCKETS - max_exact)).astype(jnp.int32)
    bucket = jnp.where(dist < max_exact, dist, jnp.minimum(large, REL_BUCKETS - 1))
    bias = table[bucket]
    return jnp.transpose(bias, (2, 0, 1)).reshape(
        ATTN_KV_HEADS, ATTN_Q_PER_KV, ATTN_BLOCK, 2 * ATTN_BLOCK)


def window_attention(h, w_qkv, b_qkv, w_o, b_o, sinks, rel_table):
    f32 = jnp.float32
    bsz, seq, _ = h.shape
    nb = seq // ATTN_BLOCK
    qkv = h @ w_qkv + b_qkv
    q, k, v = jnp.split(qkv, [ATTN_HEADS * ATTN_HEAD_DIM,
                              (ATTN_HEADS + ATTN_KV_HEADS) * ATTN_HEAD_DIM], axis=-1)
    q = q.reshape(bsz, nb, ATTN_BLOCK, ATTN_KV_HEADS, ATTN_Q_PER_KV, ATTN_HEAD_DIM)

    def band(t):
        t = t.reshape(bsz, nb, ATTN_BLOCK, ATTN_KV_HEADS, ATTN_HEAD_DIM)
        prev = jnp.pad(t, ((0, 0), (1, 0), (0, 0), (0, 0), (0, 0)))[:, :-1]
        return jnp.concatenate([prev, t], axis=2)

    kb, vb = band(k), band(v)
    scale = ATTN_HEAD_DIM ** -0.5
    logits = jnp.einsum('bnqkgd,bnskd->bnkgqs', q.astype(f32), kb.astype(f32)) * scale
    logits = logits + t5_relative_bias(rel_table).astype(f32)
    qi = jnp.arange(ATTN_BLOCK)[:, None]
    sj = jnp.arange(2 * ATTN_BLOCK)[None, :]
    rel = qi + ATTN_BLOCK - sj
    in_window = (rel >= 0) & (rel < WINDOW)
    key_pos = jnp.arange(nb)[:, None, None] * ATTN_BLOCK - ATTN_BLOCK + sj[None]
    mask = in_window[None] & (key_pos >= 0)
    logits = jnp.where(mask[None, :, None, None], logits, NEG_INF)
    sink = jnp.broadcast_to(
        sinks.astype(f32).reshape(ATTN_KV_HEADS, ATTN_Q_PER_KV)[None, None, :, :, None, None],
        logits.shape[:-1] + (1,))
    probs = jax.nn.softmax(jnp.concatenate([logits, sink], axis=-1), axis=-1)[..., :-1]
    out = jnp.einsum('bnkgqs,bnskd->bnqkgd', probs.astype(vb.dtype), vb)
    return out.reshape(bsz, seq, ATTN_HEADS * ATTN_HEAD_DIM) @ w_o + b_o


def swiglu(h, w_gate, w_up, w_down):
    return (jax.nn.silu(h @ w_gate) * (h @ w_up)) @ w_down


def _fwd_setup_inputs(seed: int = 0) -> dict:
    key = jax.random.key(seed)
    ks = iter(jax.random.split(key, 40))
    nrm = lambda shape, s: jax.random.normal(next(ks), shape, jnp.float32) * s
    ones_n = lambda shape: 1.0 + nrm(shape, 0.02)
    D = D_MODEL
    dt0 = jnp.exp(jax.random.uniform(next(ks), (N_EVEN, SSM_HEADS), jnp.float32,
                                     math.log(1e-3), math.log(1e-1)))
    return {
        "x": nrm((BATCH, SEQ, D), 1.0),
        "c": nrm((BATCH, D), 1.0),
        "ada_w": nrm((DEPTH, D, 6 * D), 0.5 * D ** -0.5),
        "ada_b": nrm((DEPTH, 6 * D), 0.02),
        "norm_mix_w": ones_n((DEPTH, D)),
        "norm_ffn_w": ones_n((DEPTH, D)),
        "in_w_even": nrm((N_EVEN, D, IN_EVEN), D ** -0.5),
        "conv_w": nrm((N_EVEN, SSM_CONV, SSM_CONV_DIM), SSM_CONV ** -0.5),
        "conv_b": nrm((N_EVEN, SSM_CONV_DIM), 0.02),
        "dt_bias": dt0 + jnp.log(-jnp.expm1(-dt0)),
        "a_log": jnp.log(jax.random.uniform(next(ks), (N_EVEN, SSM_HEADS), jnp.float32, 1.0, 16.0)),
        "d_skip": 1.0 + nrm((N_EVEN, SSM_HEADS), 0.1),
        "ssm_norm_w": ones_n((N_EVEN, SSM_INNER)),
        "gmlp_ln_w": ones_n((N_EVEN, GMLP_INNER)),
        "gmlp_ln_b": nrm((N_EVEN, GMLP_INNER), 0.02),
        "gmlp_ws": nrm((N_EVEN, GMLP_GROUPS, GMLP_CHUNK, GMLP_CHUNK), GMLP_CHUNK ** -0.5),
        "gmlp_bs": 1.0 + nrm((N_EVEN, GMLP_GROUPS, GMLP_CHUNK), 0.02),
        "out_w_even": nrm((N_EVEN, MIX_WIDTH, D), MIX_WIDTH ** -0.5),
        "qkv_w": nrm((N_ODD, D, QKV_DIM), D ** -0.5),
        "qkv_b": nrm((N_ODD, QKV_DIM), 0.02),
        "o_w": nrm((N_ODD, ATTN_HEADS * ATTN_HEAD_DIM, D), (ATTN_HEADS * ATTN_HEAD_DIM) ** -0.5),
        "o_b": nrm((N_ODD, D), 0.02),
        "sinks": nrm((N_ODD, ATTN_HEADS), 1.0),
        "rel_table": nrm((REL_BUCKETS, ATTN_HEADS), 0.5),
        "ffn_gate_w": nrm((DEPTH, D, FFN_HIDDEN), D ** -0.5),
        "ffn_up_w": nrm((DEPTH, D, FFN_HIDDEN), D ** -0.5),
        "ffn_down_w": nrm((DEPTH, FFN_HIDDEN, D), FFN_HIDDEN ** -0.5),
        "final_norm_w": ones_n((D,)),
    }


def _fwd_reference(x, c, ada_w, ada_b, norm_mix_w, norm_ffn_w, in_w_even, conv_w, conv_b,
              dt_bias, a_log, d_skip, ssm_norm_w, gmlp_ln_w, gmlp_ln_b, gmlp_ws, gmlp_bs,
              out_w_even, qkv_w, qkv_b, o_w, o_b, sinks, rel_table,
              ffn_gate_w, ffn_up_w, ffn_down_w, final_norm_w):
    cond = jax.nn.silu(c)
    for layer in range(DEPTH):
        mod = cond @ ada_w[layer] + ada_b[layer]
        sh1, sc1, g1, sh2, sc2, g2 = jnp.split(mod, 6, axis=-1)
        h = modulate(rms_norm(x, norm_mix_w[layer]), sh1, sc1)
        i = layer // 2
        if layer % 2 == 0:
            mix = even_mixer(h, in_w_even[i], conv_w[i], conv_b[i], dt_bias[i], a_log[i],
                             d_skip[i], ssm_norm_w[i], gmlp_ln_w[i], gmlp_ln_b[i],
                             gmlp_ws[i], gmlp_bs[i], out_w_even[i])
        else:
            mix = window_attention(h, qkv_w[i], qkv_b[i], o_w[i], o_b[i], sinks[i], rel_table)
        x = x + g1[:, None, :] * mix
        h = modulate(rms_norm(x, norm_ffn_w[layer]), sh2, sc2)
        x = x + g2[:, None, :] * swiglu(h, ffn_gate_w[layer], ffn_up_w[layer], ffn_down_w[layer])
    return rms_norm(x, final_norm_w)


import jax as _jax
import jax.numpy as _jnp

TWIN_FORMAT = 'train_step'
FWD_PARAMS = ['x', 'c', 'ada_w', 'ada_b', 'norm_mix_w', 'norm_ffn_w', 'in_w_even', 'conv_w', 'conv_b', 'dt_bias', 'a_log', 'd_skip', 'ssm_norm_w', 'gmlp_ln_w', 'gmlp_ln_b', 'gmlp_ws', 'gmlp_bs', 'out_w_even', 'qkv_w', 'qkv_b', 'o_w', 'o_b', 'sinks', 'rel_table', 'ffn_gate_w', 'ffn_up_w', 'ffn_down_w', 'final_norm_w']
TWIN_WEIGHTS = ['ada_w', 'ada_b', 'norm_mix_w', 'norm_ffn_w', 'in_w_even', 'conv_w', 'conv_b', 'dt_bias', 'a_log', 'd_skip', 'ssm_norm_w', 'gmlp_ln_w', 'gmlp_ln_b', 'gmlp_ws', 'gmlp_bs', 'out_w_even', 'qkv_w', 'qkv_b', 'o_w', 'o_b', 'sinks', 'rel_table', 'ffn_gate_w', 'ffn_up_w', 'ffn_down_w', 'final_norm_w']
TWIN_DIFF_INPUT = 'x'
TWIN_INPUTS = ['x', 'c', 'ada_w', 'ada_b', 'norm_mix_w', 'norm_ffn_w', 'in_w_even', 'conv_w', 'conv_b', 'dt_bias', 'a_log', 'd_skip', 'ssm_norm_w', 'gmlp_ln_w', 'gmlp_ln_b', 'gmlp_ws', 'gmlp_bs', 'out_w_even', 'qkv_w', 'qkv_b', 'o_w', 'o_b', 'sinks', 'rel_table', 'ffn_gate_w', 'ffn_up_w', 'ffn_down_w', 'final_norm_w', 'loss_target', 'm_ada_w', 'm_ada_b', 'm_norm_mix_w', 'm_norm_ffn_w', 'm_in_w_even', 'm_conv_w', 'm_conv_b', 'm_dt_bias', 'm_a_log', 'm_d_skip', 'm_ssm_norm_w', 'm_gmlp_ln_w', 'm_gmlp_ln_b', 'm_gmlp_ws', 'm_gmlp_bs', 'm_out_w_even', 'm_qkv_w', 'm_qkv_b', 'm_o_w', 'm_o_b', 'm_sinks', 'm_rel_table', 'm_ffn_gate_w', 'm_ffn_up_w', 'm_ffn_down_w', 'm_final_norm_w', 'v_ada_w', 'v_ada_b', 'v_norm_mix_w', 'v_norm_ffn_w', 'v_in_w_even', 'v_conv_w', 'v_conv_b', 'v_dt_bias', 'v_a_log', 'v_d_skip', 'v_ssm_norm_w', 'v_gmlp_ln_w', 'v_gmlp_ln_b', 'v_gmlp_ws', 'v_gmlp_bs', 'v_out_w_even', 'v_qkv_w', 'v_qkv_b', 'v_o_w', 'v_o_b', 'v_sinks', 'v_rel_table', 'v_ffn_gate_w', 'v_ffn_up_w', 'v_ffn_down_w', 'v_final_norm_w']
TWIN_OUTPUTS = ['loss', 'grad_x', 'grad_ada_w', 'grad_ada_b', 'grad_norm_mix_w', 'grad_norm_ffn_w', 'grad_in_w_even', 'grad_conv_w', 'grad_conv_b', 'grad_dt_bias', 'grad_a_log', 'grad_d_skip', 'grad_ssm_norm_w', 'grad_gmlp_ln_w', 'grad_gmlp_ln_b', 'grad_gmlp_ws', 'grad_gmlp_bs', 'grad_out_w_even', 'grad_qkv_w', 'grad_qkv_b', 'grad_o_w', 'grad_o_b', 'grad_sinks', 'grad_rel_table', 'grad_ffn_gate_w', 'grad_ffn_up_w', 'grad_ffn_down_w', 'grad_final_norm_w', 'delta_ada_w', 'delta_ada_b', 'delta_norm_mix_w', 'delta_norm_ffn_w', 'delta_in_w_even', 'delta_conv_w', 'delta_conv_b', 'delta_dt_bias', 'delta_a_log', 'delta_d_skip', 'delta_ssm_norm_w', 'delta_gmlp_ln_w', 'delta_gmlp_ln_b', 'delta_gmlp_ws', 'delta_gmlp_bs', 'delta_out_w_even', 'delta_qkv_w', 'delta_qkv_b', 'delta_o_w', 'delta_o_b', 'delta_sinks', 'delta_rel_table', 'delta_ffn_gate_w', 'delta_ffn_up_w', 'delta_ffn_down_w', 'delta_final_norm_w', 'new_m_ada_w', 'new_m_ada_b', 'new_m_norm_mix_w', 'new_m_norm_ffn_w', 'new_m_in_w_even', 'new_m_conv_w', 'new_m_conv_b', 'new_m_dt_bias', 'new_m_a_log', 'new_m_d_skip', 'new_m_ssm_norm_w', 'new_m_gmlp_ln_w', 'new_m_gmlp_ln_b', 'new_m_gmlp_ws', 'new_m_gmlp_bs', 'new_m_out_w_even', 'new_m_qkv_w', 'new_m_qkv_b', 'new_m_o_w', 'new_m_o_b', 'new_m_sinks', 'new_m_rel_table', 'new_m_ffn_gate_w', 'new_m_ffn_up_w', 'new_m_ffn_down_w', 'new_m_final_norm_w', 'new_v_ada_w', 'new_v_ada_b', 'new_v_norm_mix_w', 'new_v_norm_ffn_w', 'new_v_in_w_even', 'new_v_conv_w', 'new_v_conv_b', 'new_v_dt_bias', 'new_v_a_log', 'new_v_d_skip', 'new_v_ssm_norm_w', 'new_v_gmlp_ln_w', 'new_v_gmlp_ln_b', 'new_v_gmlp_ws', 'new_v_gmlp_bs', 'new_v_out_w_even', 'new_v_qkv_w', 'new_v_qkv_b', 'new_v_o_w', 'new_v_o_b', 'new_v_sinks', 'new_v_rel_table', 'new_v_ffn_gate_w', 'new_v_ffn_up_w', 'new_v_ffn_down_w', 'new_v_final_norm_w']
TWIN_LEAF_KINDS = {'loss': 'loss', 'grad_x': 'grad_x', 'grad_ada_w': 'grad_w', 'grad_ada_b': 'grad_w', 'grad_norm_mix_w': 'grad_w', 'grad_norm_ffn_w': 'grad_w', 'grad_in_w_even': 'grad_w', 'grad_conv_w': 'grad_w', 'grad_conv_b': 'grad_w', 'grad_dt_bias': 'grad_w', 'grad_a_log': 'grad_w', 'grad_d_skip': 'grad_w', 'grad_ssm_norm_w': 'grad_w', 'grad_gmlp_ln_w': 'grad_w', 'grad_gmlp_ln_b': 'grad_w', 'grad_gmlp_ws': 'grad_w', 'grad_gmlp_bs': 'grad_w', 'grad_out_w_even': 'grad_w', 'grad_qkv_w': 'grad_w', 'grad_qkv_b': 'grad_w', 'grad_o_w': 'grad_w', 'grad_o_b': 'grad_w', 'grad_sinks': 'grad_w', 'grad_rel_table': 'grad_w', 'grad_ffn_gate_w': 'grad_w', 'grad_ffn_up_w': 'grad_w', 'grad_ffn_down_w': 'grad_w', 'grad_final_norm_w': 'grad_w', 'delta_ada_w': 'delta_w', 'delta_ada_b': 'delta_w', 'delta_norm_mix_w': 'delta_w', 'delta_norm_ffn_w': 'delta_w', 'delta_in_w_even': 'delta_w', 'delta_conv_w': 'delta_w', 'delta_conv_b': 'delta_w', 'delta_dt_bias': 'delta_w', 'delta_a_log': 'delta_w', 'delta_d_skip': 'delta_w', 'delta_ssm_norm_w': 'delta_w', 'delta_gmlp_ln_w': 'delta_w', 'delta_gmlp_ln_b': 'delta_w', 'delta_gmlp_ws': 'delta_w', 'delta_gmlp_bs': 'delta_w', 'delta_out_w_even': 'delta_w', 'delta_qkv_w': 'delta_w', 'delta_qkv_b': 'delta_w', 'delta_o_w': 'delta_w', 'delta_o_b': 'delta_w', 'delta_sinks': 'delta_w', 'delta_rel_table': 'delta_w', 'delta_ffn_gate_w': 'delta_w', 'delta_ffn_up_w': 'delta_w', 'delta_ffn_down_w': 'delta_w', 'delta_final_norm_w': 'delta_w', 'new_m_ada_w': 'new_m', 'new_m_ada_b': 'new_m', 'new_m_norm_mix_w': 'new_m', 'new_m_norm_ffn_w': 'new_m', 'new_m_in_w_even': 'new_m', 'new_m_conv_w': 'new_m', 'new_m_conv_b': 'new_m', 'new_m_dt_bias': 'new_m', 'new_m_a_log': 'new_m', 'new_m_d_skip': 'new_m', 'new_m_ssm_norm_w': 'new_m', 'new_m_gmlp_ln_w': 'new_m', 'new_m_gmlp_ln_b': 'new_m', 'new_m_gmlp_ws': 'new_m', 'new_m_gmlp_bs': 'new_m', 'new_m_out_w_even': 'new_m', 'new_m_qkv_w': 'new_m', 'new_m_qkv_b': 'new_m', 'new_m_o_w': 'new_m', 'new_m_o_b': 'new_m', 'new_m_sinks': 'new_m', 'new_m_rel_table': 'new_m', 'new_m_ffn_gate_w': 'new_m', 'new_m_ffn_up_w': 'new_m', 'new_m_ffn_down_w': 'new_m', 'new_m_final_norm_w': 'new_m', 'new_v_ada_w': 'new_v', 'new_v_ada_b': 'new_v', 'new_v_norm_mix_w': 'new_v', 'new_v_norm_ffn_w': 'new_v', 'new_v_in_w_even': 'new_v', 'new_v_conv_w': 'new_v', 'new_v_conv_b': 'new_v', 'new_v_dt_bias': 'new_v', 'new_v_a_log': 'new_v', 'new_v_d_skip': 'new_v', 'new_v_ssm_norm_w': 'new_v', 'new_v_gmlp_ln_w': 'new_v', 'new_v_gmlp_ln_b': 'new_v', 'new_v_gmlp_ws': 'new_v', 'new_v_gmlp_bs': 'new_v', 'new_v_out_w_even': 'new_v', 'new_v_qkv_w': 'new_v', 'new_v_qkv_b': 'new_v', 'new_v_o_w': 'new_v', 'new_v_o_b': 'new_v', 'new_v_sinks': 'new_v', 'new_v_rel_table': 'new_v', 'new_v_ffn_gate_w': 'new_v', 'new_v_ffn_up_w': 'new_v', 'new_v_ffn_down_w': 'new_v', 'new_v_final_norm_w': 'new_v'}


def _forward(args):
    return _fwd_reference(*[args[k] for k in FWD_PARAMS])


def _output_shape():
    def fwd():
        inp = _fwd_setup_inputs(0)
        return _fwd_reference(*[inp[k] for k in FWD_PARAMS])
    out = _jax.eval_shape(fwd)
    return out.shape, out.dtype

N_MICROBATCH = 1
ADAM_LR = 0.001
ADAM_B1 = 0.9
ADAM_B2 = 0.999
ADAM_EPS = 1e-08
ADAM_WD = 0.01
ADAM_STEP = 10
PER_EXAMPLE_BATCH_AXIS = {'x': 0, 'c': 0, 'loss_target': 0}
SHARED_INPUTS = []
_WEIGHT_DTYPES = {'ada_w': _jnp.float32, 'ada_b': _jnp.float32, 'norm_mix_w': _jnp.float32, 'norm_ffn_w': _jnp.float32, 'in_w_even': _jnp.float32, 'conv_w': _jnp.float32, 'conv_b': _jnp.float32, 'dt_bias': _jnp.float32, 'a_log': _jnp.float32, 'd_skip': _jnp.float32, 'ssm_norm_w': _jnp.float32, 'gmlp_ln_w': _jnp.float32, 'gmlp_ln_b': _jnp.float32, 'gmlp_ws': _jnp.float32, 'gmlp_bs': _jnp.float32, 'out_w_even': _jnp.float32, 'qkv_w': _jnp.float32, 'qkv_b': _jnp.float32, 'o_w': _jnp.float32, 'o_b': _jnp.float32, 'sinks': _jnp.float32, 'rel_table': _jnp.float32, 'ffn_gate_w': _jnp.float32, 'ffn_up_w': _jnp.float32, 'ffn_down_w': _jnp.float32, 'final_norm_w': _jnp.float32}
MOMENT_SCALE = {'ada_w': 7.197753e-02, 'ada_b': 1.236266e-01, 'norm_mix_w': 6.916837e-02, 'norm_ffn_w': 6.980327e-02, 'in_w_even': 4.629515e-02, 'conv_w': 4.895041e-02, 'conv_b': 6.006475e-02, 'dt_bias': 1.143842e-01, 'a_log': 1.450530e-01, 'd_skip': 3.322512e-01, 'ssm_norm_w': 5.395079e-02, 'gmlp_ln_w': 2.784694e-02, 'gmlp_ln_b': 2.985333e-02, 'gmlp_ws': 2.686203e-02, 'gmlp_bs': 3.817151e-02, 'out_w_even': 7.094424e-02, 'qkv_w': 3.226173e-02, 'qkv_b': 8.195065e-02, 'o_w': 2.763079e-02, 'o_b': 7.128871e-02, 'sinks': 1.240172e-02, 'rel_table': 2.009051e-02, 'ffn_gate_w': 3.114739e-02, 'ffn_up_w': 3.019386e-02, 'ffn_down_w': 4.995029e-02, 'final_norm_w': 6.407269e+01}


def _to_microbatches(a, axis):
    t = _jnp.moveaxis(a, axis, 0)
    t = t.reshape((N_MICROBATCH, t.shape[0] // N_MICROBATCH) + t.shape[1:])
    return _jnp.moveaxis(t, 1, axis + 1)


def setup_inputs(seed: int = 0) -> dict:
    inp = _fwd_setup_inputs(seed)
    key = _jax.random.fold_in(_jax.random.key(seed), 7919)
    shape, _ = _output_shape()
    out = dict(inp)
    out["loss_target"] = _jax.random.normal(_jax.random.fold_in(key, 0), shape, _jnp.float32)
    for i, name in enumerate(TWIN_WEIGHTS):
        w = inp[name].astype(_jnp.float32)
        if MOMENT_SCALE is None:
            s = _jnp.sqrt(_jnp.mean(_jnp.square(w)) + 1e-30)
        else:
            s = MOMENT_SCALE[name]
        km, kv = _jax.random.split(_jax.random.fold_in(key, i + 1))
        out[name] = w
        out["m_" + name] = s * _jax.random.normal(km, w.shape, _jnp.float32)
        out["v_" + name] = (s * s) * _jax.random.uniform(kv, w.shape, _jnp.float32, 0.5, 1.5)
    if N_MICROBATCH > 1:
        for name, axis in PER_EXAMPLE_BATCH_AXIS.items():
            out[name] = _to_microbatches(out[name], axis)
    return {'x': out['x'], 'c': out['c'], 'ada_w': out['ada_w'], 'ada_b': out['ada_b'], 'norm_mix_w': out['norm_mix_w'], 'norm_ffn_w': out['norm_ffn_w'], 'in_w_even': out['in_w_even'], 'conv_w': out['conv_w'], 'conv_b': out['conv_b'], 'dt_bias': out['dt_bias'], 'a_log': out['a_log'], 'd_skip': out['d_skip'], 'ssm_norm_w': out['ssm_norm_w'], 'gmlp_ln_w': out['gmlp_ln_w'], 'gmlp_ln_b': out['gmlp_ln_b'], 'gmlp_ws': out['gmlp_ws'], 'gmlp_bs': out['gmlp_bs'], 'out_w_even': out['out_w_even'], 'qkv_w': out['qkv_w'], 'qkv_b': out['qkv_b'], 'o_w': out['o_w'], 'o_b': out['o_b'], 'sinks': out['sinks'], 'rel_table': out['rel_table'], 'ffn_gate_w': out['ffn_gate_w'], 'ffn_up_w': out['ffn_up_w'], 'ffn_down_w': out['ffn_down_w'], 'final_norm_w': out['final_norm_w'], 'loss_target': out['loss_target'], 'm_ada_w': out['m_ada_w'], 'm_ada_b': out['m_ada_b'], 'm_norm_mix_w': out['m_norm_mix_w'], 'm_norm_ffn_w': out['m_norm_ffn_w'], 'm_in_w_even': out['m_in_w_even'], 'm_conv_w': out['m_conv_w'], 'm_conv_b': out['m_conv_b'], 'm_dt_bias': out['m_dt_bias'], 'm_a_log': out['m_a_log'], 'm_d_skip': out['m_d_skip'], 'm_ssm_norm_w': out['m_ssm_norm_w'], 'm_gmlp_ln_w': out['m_gmlp_ln_w'], 'm_gmlp_ln_b': out['m_gmlp_ln_b'], 'm_gmlp_ws': out['m_gmlp_ws'], 'm_gmlp_bs': out['m_gmlp_bs'], 'm_out_w_even': out['m_out_w_even'], 'm_qkv_w': out['m_qkv_w'], 'm_qkv_b': out['m_qkv_b'], 'm_o_w': out['m_o_w'], 'm_o_b': out['m_o_b'], 'm_sinks': out['m_sinks'], 'm_rel_table': out['m_rel_table'], 'm_ffn_gate_w': out['m_ffn_gate_w'], 'm_ffn_up_w': out['m_ffn_up_w'], 'm_ffn_down_w': out['m_ffn_down_w'], 'm_final_norm_w': out['m_final_norm_w'], 'v_ada_w': out['v_ada_w'], 'v_ada_b': out['v_ada_b'], 'v_norm_mix_w': out['v_norm_mix_w'], 'v_norm_ffn_w': out['v_norm_ffn_w'], 'v_in_w_even': out['v_in_w_even'], 'v_conv_w': out['v_conv_w'], 'v_conv_b': out['v_conv_b'], 'v_dt_bias': out['v_dt_bias'], 'v_a_log': out['v_a_log'], 'v_d_skip': out['v_d_skip'], 'v_ssm_norm_w': out['v_ssm_norm_w'], 'v_gmlp_ln_w': out['v_gmlp_ln_w'], 'v_gmlp_ln_b': out['v_gmlp_ln_b'], 'v_gmlp_ws': out['v_gmlp_ws'], 'v_gmlp_bs': out['v_gmlp_bs'], 'v_out_w_even': out['v_out_w_even'], 'v_qkv_w': out['v_qkv_w'], 'v_qkv_b': out['v_qkv_b'], 'v_o_w': out['v_o_w'], 'v_o_b': out['v_o_b'], 'v_sinks': out['v_sinks'], 'v_rel_table': out['v_rel_table'], 'v_ffn_gate_w': out['v_ffn_gate_w'], 'v_ffn_up_w': out['v_ffn_up_w'], 'v_ffn_down_w': out['v_ffn_down_w'], 'v_final_norm_w': out['v_final_norm_w']}


def _loss(weights, diff, rest, loss_target):
    with _jax.named_scope("forward"):
        args = {**rest, TWIN_DIFF_INPUT: diff, **{k: w.astype(_WEIGHT_DTYPES[k]) for k, w in weights.items()}}
        y = _forward(args)
    with _jax.named_scope("loss_head"):
        err = _jnp.square(y.astype(_jnp.float32) - loss_target)
        return 0.5 * _jnp.sum(_jnp.mean(err, axis=-1)) if err.ndim else 0.5 * err


def _adamw(w, g, m, v):
    m = ADAM_B1 * m + (1.0 - ADAM_B1) * g
    v = ADAM_B2 * v + (1.0 - ADAM_B2) * _jnp.square(g)
    m_hat = m / (1.0 - ADAM_B1 ** ADAM_STEP)
    v_hat = v / (1.0 - ADAM_B2 ** ADAM_STEP)
    delta = -ADAM_LR * (m_hat / (_jnp.sqrt(v_hat) + ADAM_EPS) + ADAM_WD * w)
    return delta, m, v


def reference(x, c, ada_w, ada_b, norm_mix_w, norm_ffn_w, in_w_even, conv_w, conv_b, dt_bias, a_log, d_skip, ssm_norm_w, gmlp_ln_w, gmlp_ln_b, gmlp_ws, gmlp_bs, out_w_even, qkv_w, qkv_b, o_w, o_b, sinks, rel_table, ffn_gate_w, ffn_up_w, ffn_down_w, final_norm_w, loss_target, m_ada_w, m_ada_b, m_norm_mix_w, m_norm_ffn_w, m_in_w_even, m_conv_w, m_conv_b, m_dt_bias, m_a_log, m_d_skip, m_ssm_norm_w, m_gmlp_ln_w, m_gmlp_ln_b, m_gmlp_ws, m_gmlp_bs, m_out_w_even, m_qkv_w, m_qkv_b, m_o_w, m_o_b, m_sinks, m_rel_table, m_ffn_gate_w, m_ffn_up_w, m_ffn_down_w, m_final_norm_w, v_ada_w, v_ada_b, v_norm_mix_w, v_norm_ffn_w, v_in_w_even, v_conv_w, v_conv_b, v_dt_bias, v_a_log, v_d_skip, v_ssm_norm_w, v_gmlp_ln_w, v_gmlp_ln_b, v_gmlp_ws, v_gmlp_bs, v_out_w_even, v_qkv_w, v_qkv_b, v_o_w, v_o_b, v_sinks, v_rel_table, v_ffn_gate_w, v_ffn_up_w, v_ffn_down_w, v_final_norm_w):
    given = dict(x=x, c=c, ada_w=ada_w, ada_b=ada_b, norm_mix_w=norm_mix_w, norm_ffn_w=norm_ffn_w, in_w_even=in_w_even, conv_w=conv_w, conv_b=conv_b, dt_bias=dt_bias, a_log=a_log, d_skip=d_skip, ssm_norm_w=ssm_norm_w, gmlp_ln_w=gmlp_ln_w, gmlp_ln_b=gmlp_ln_b, gmlp_ws=gmlp_ws, gmlp_bs=gmlp_bs, out_w_even=out_w_even, qkv_w=qkv_w, qkv_b=qkv_b, o_w=o_w, o_b=o_b, sinks=sinks, rel_table=rel_table, ffn_gate_w=ffn_gate_w, ffn_up_w=ffn_up_w, ffn_down_w=ffn_down_w, final_norm_w=final_norm_w, loss_target=loss_target, m_ada_w=m_ada_w, m_ada_b=m_ada_b, m_norm_mix_w=m_norm_mix_w, m_norm_ffn_w=m_norm_ffn_w, m_in_w_even=m_in_w_even, m_conv_w=m_conv_w, m_conv_b=m_conv_b, m_dt_bias=m_dt_bias, m_a_log=m_a_log, m_d_skip=m_d_skip, m_ssm_norm_w=m_ssm_norm_w, m_gmlp_ln_w=m_gmlp_ln_w, m_gmlp_ln_b=m_gmlp_ln_b, m_gmlp_ws=m_gmlp_ws, m_gmlp_bs=m_gmlp_bs, m_out_w_even=m_out_w_even, m_qkv_w=m_qkv_w, m_qkv_b=m_qkv_b, m_o_w=m_o_w, m_o_b=m_o_b, m_sinks=m_sinks, m_rel_table=m_rel_table, m_ffn_gate_w=m_ffn_gate_w, m_ffn_up_w=m_ffn_up_w, m_ffn_down_w=m_ffn_down_w, m_final_norm_w=m_final_norm_w, v_ada_w=v_ada_w, v_ada_b=v_ada_b, v_norm_mix_w=v_norm_mix_w, v_norm_ffn_w=v_norm_ffn_w, v_in_w_even=v_in_w_even, v_conv_w=v_conv_w, v_conv_b=v_conv_b, v_dt_bias=v_dt_bias, v_a_log=v_a_log, v_d_skip=v_d_skip, v_ssm_norm_w=v_ssm_norm_w, v_gmlp_ln_w=v_gmlp_ln_w, v_gmlp_ln_b=v_gmlp_ln_b, v_gmlp_ws=v_gmlp_ws, v_gmlp_bs=v_gmlp_bs, v_out_w_even=v_out_w_even, v_qkv_w=v_qkv_w, v_qkv_b=v_qkv_b, v_o_w=v_o_w, v_o_b=v_o_b, v_sinks=v_sinks, v_rel_table=v_rel_table, v_ffn_gate_w=v_ffn_gate_w, v_ffn_up_w=v_ffn_up_w, v_ffn_down_w=v_ffn_down_w, v_final_norm_w=v_final_norm_w)
    weights = {n: given[n] for n in TWIN_WEIGHTS}
    shared = {n: given[n] for n in SHARED_INPUTS}
    per_example = {n: given[n] for n in ['x', 'c']}
    grad_fn = _jax.value_and_grad(_loss, argnums=(0, 1))

    def one_microbatch(ex, loss_target):
        ex = dict(ex)
        diff = ex.pop(TWIN_DIFF_INPUT)
        return grad_fn(weights, diff, {**shared, **ex}, loss_target)

    if N_MICROBATCH == 1:
        loss, (grad_w, grad_x) = one_microbatch(per_example, given["loss_target"])
    else:
        def body(carry, xs):
            loss_sum, grad_sum = carry
            l_k, (gw_k, gx_k) = one_microbatch(xs[0], xs[1])
            with _jax.named_scope("update"):
                return (loss_sum + l_k, _jax.tree.map(_jnp.add, grad_sum, gw_k)), gx_k

        init = (_jnp.zeros((), _jnp.float32), _jax.tree.map(_jnp.zeros_like, weights))
        (loss, grad_w), grad_x = _jax.lax.scan(body, init, (per_example, given["loss_target"]))
    with _jax.named_scope("update"):
        delta_w, new_m, new_v = {}, {}, {}
        for n in TWIN_WEIGHTS:
            delta_w[n], new_m[n], new_v[n] = _adamw(weights[n], grad_w[n], given["m_" + n], given["v_" + n])
    return (loss, grad_x, *[grad_w[n] for n in TWIN_WEIGHTS], *[delta_w[n] for n in TWIN_WEIGHTS],
            *[new_m[n] for n in TWIN_WEIGHTS], *[new_v[n] for n in TWIN_WEIGHTS])
```

```python
import functools
import math

import numpy as np
import jax
import jax.numpy as jnp
from jax import lax
from jax.experimental import pallas as pl
from jax.experimental.pallas import tpu as pltpu

F32 = jnp.float32
BF16 = jnp.bfloat16
HIGHEST = lax.Precision.HIGHEST
MESH = pl.DeviceIdType.MESH

N_DEV = 8
D = 1024
DEPTH = 2
SSM_HEADS = 16
SSM_HEAD_DIM = 64
SSM_INNER = 1024
SSM_GROUPS = 2
SSM_STATE = 128
SSM_CONV = 4
CHUNK = 128
CONV_DIM = SSM_INNER + 2 * SSM_GROUPS * SSM_STATE
GMLP_GROUPS = 8
GMLP_INNER = 1024
IN_EVEN = 4624
ATTN_HEADS = 16
ATTN_KV = 2
ATTN_DH = 64
QKV_DIM = 1280
REL_BUCKETS = 32
REL_MAX_DIST = 128
FFN = 2816
EPS = 1e-6
NEG_INF = -1e30
LANES = 128

ADAM_LR = 0.001
ADAM_B1 = 0.9
ADAM_B2 = 0.999
ADAM_EPS = 1e-08
ADAM_WD = 0.01
ADAM_STEP = 10

VMEM_LIMIT_BYTES = 56 * 1024 * 1024
ROW_TILE = 512


def _pcall(body, *, name, out_shape, grid=(), in_specs=None, out_specs=None, scratch=(), sem=None):
    params = dict(vmem_limit_bytes=VMEM_LIMIT_BYTES)
    if sem is not None:
        params["dimension_semantics"] = sem
    specs = {} if in_specs is None else dict(in_specs=in_specs, out_specs=out_specs)
    return pl.pallas_call(
        body, name=name, out_shape=out_shape, grid=grid, **specs,
        scratch_shapes=list(scratch), compiler_params=pltpu.CompilerParams(**params))


def _tile(n, pref):
    if n <= pref:
        return n
    best = None
    for t in range(LANES, pref + 1, LANES):
        if n % t == 0:
            best = t
    assert best is not None, (n, pref)
    return best


def _rows(T):
    return min(ROW_TILE, T)


def _sds(shape, dtype=F32):
    return jax.ShapeDtypeStruct(shape, dtype)


def _row_spec(tm, c, col=0):
    return pl.BlockSpec((tm, c), lambda i, col=col: (i, col))


def _vec_spec(c, r=1):
    return pl.BlockSpec((r, c), lambda i: (0, 0))


def _sigmoid(x):
    return jax.nn.sigmoid(x)


def _silu(x):
    return x * _sigmoid(x)


def _dsilu(x):
    s = _sigmoid(x)
    return s * (1.0 + x * (1.0 - s))


def _gelu(x):
    return 0.5 * x * (1.0 + lax.erf(x * 0.7071067811865476))


def _dgelu(x):
    return 0.5 * (1.0 + lax.erf(x * 0.7071067811865476)) + x * jnp.exp(-0.5 * x * x) * 0.3989422804014327


def _dot(a, b, dims, precision=None):
    return lax.dot_general(a, b, (dims, ((), ())), precision=precision, preferred_element_type=F32)


def _nn(a, b, precision=None):
    return _dot(a, b, ((1,), (0,)), precision)


def _nt(a, b, precision=None):
    return _dot(a, b, ((1,), (1,)), precision)


def _tn(a, b, precision=None):
    return _dot(a, b, ((0,), (0,)), precision)


def _bf(x):
    return x.astype(BF16)


def _colsum(x):
    return jnp.sum(x, axis=0, keepdims=True)


def _rowsum(x):
    return jnp.sum(x, axis=1, keepdims=True)


def _allsum(x):
    return _colsum(_rowsum(x))


def _all_gather(x, name):
    R, C = x.shape

    def body(x_ref, out_ref, send_sems, recv_sems, local_sem):
        mx, my, mc = lax.axis_index("x"), lax.axis_index("y"), lax.axis_index("c")
        me, sibling = (mx, my, mc), (mx, my, 1 - mc)
        chips = [(1 - mx, my), (mx, 1 - my), (1 - mx, 1 - my)]

        def slot(px, py, pc):
            return out_ref.at[4 * px + 2 * py + pc]

        def copy(k, block, to, src=None):
            return pltpu.make_async_remote_copy(
                src_ref=slot(*block) if src is None else src, dst_ref=slot(*block),
                send_sem=send_sems.at[k], recv_sem=recv_sems.at[k], device_id=to, device_id_type=MESH)

        mine = pltpu.make_async_copy(x_ref, slot(*me), local_sem)
        mine.start()
        first = [copy(0, me, sibling, src=x_ref)]
        first += [copy(1 + j, me, (*chip, mc), src=x_ref) for j, chip in enumerate(chips)]
        for cp in first:
            cp.start()
        passed = [copy(4 + j, (*chip, mc), sibling) for j, chip in enumerate(chips)]
        for j, chip in enumerate(chips):
            copy(1 + j, (*chip, mc), me).wait_recv()
            passed[j].start()
        copy(0, sibling, me).wait_recv()
        for j, chip in enumerate(chips):
            copy(4 + j, (*chip, 1 - mc), me).wait_recv()
        for cp in first + passed:
            cp.wait_send()
        mine.wait()

    return pl.pallas_call(
        body, name=name, out_shape=_sds((N_DEV, R, C), x.dtype),
        in_specs=[pl.BlockSpec(memory_space=pl.ANY)], out_specs=pl.BlockSpec(memory_space=pl.ANY),
        scratch_shapes=[pltpu.SemaphoreType.DMA((7,)), pltpu.SemaphoreType.DMA((7,)), pltpu.SemaphoreType.DMA(())],
    )(x)


def _exchange_partials(p, name):
    _, R, C = p.shape

    def body(p_ref, out_ref, send_sems, recv_sems, local_sem):
        mx, my, mc = lax.axis_index("x"), lax.axis_index("y"), lax.axis_index("c")
        me = 4 * mx + 2 * my + mc
        local = pltpu.make_async_copy(p_ref.at[me], out_ref.at[me], local_sem)
        local.start()
        copies = []
        for r in range(1, N_DEV):
            px = 1 - mx if (r >> 2) & 1 else mx
            py = 1 - my if (r >> 1) & 1 else my
            pc = 1 - mc if r & 1 else mc
            cp = pltpu.make_async_remote_copy(
                src_ref=p_ref.at[4 * px + 2 * py + pc], dst_ref=out_ref.at[me],
                send_sem=send_sems.at[r - 1], recv_sem=recv_sems.at[r - 1],
                device_id=(px, py, pc), device_id_type=MESH)
            cp.start()
            copies.append(cp)
        for cp in copies:
            cp.wait()
        local.wait()

    return pl.pallas_call(
        body, name=name, out_shape=_sds((N_DEV, R, C), p.dtype),
        in_specs=[pl.BlockSpec(memory_space=pl.ANY)], out_specs=pl.BlockSpec(memory_space=pl.ANY),
        scratch_shapes=[pltpu.SemaphoreType.DMA((7,)), pltpu.SemaphoreType.DMA((7,)), pltpu.SemaphoreType.DMA(())],
    )(p)


def _adamw(parts, w, m, v, name):
    P, R, C = parts.shape
    tr = R if R <= 256 else _tile_rows(R, 256)

    def body(p_ref, w_ref, m_ref, v_ref, g_ref, d_ref, nm_ref, nv_ref):
        g = p_ref[0]
        for k in range(1, P):
            g = g + p_ref[k]
        nm = ADAM_B1 * m_ref[...] + (1.0 - ADAM_B1) * g
        nv = ADAM_B2 * v_ref[...] + (1.0 - ADAM_B2) * (g * g)
        m_hat = nm / (1.0 - ADAM_B1 ** ADAM_STEP)
        v_hat = nv / (1.0 - ADAM_B2 ** ADAM_STEP)
        g_ref[...] = g
        d_ref[...] = -ADAM_LR * (m_hat / (jnp.sqrt(v_hat) + ADAM_EPS) + ADAM_WD * w_ref[...])
        nm_ref[...] = nm
        nv_ref[...] = nv

    blk = pl.BlockSpec((tr, C), lambda i: (i, 0))
    return _pcall(
        body, name=name, out_shape=[_sds((R, C))] * 4, grid=(R // tr,),
        in_specs=[pl.BlockSpec((P, tr, C), lambda i: (0, i, 0)), blk, blk, blk],
        out_specs=[blk] * 4, sem=("parallel",))(parts, w, m, v)


def _tile_rows(n, pref):
    best = None
    for t in range(8, pref + 1, 8):
        if n % t == 0:
            best = t
    assert best is not None, (n, pref)
    return best


def _mm(pairs, mode, *, name, out_dtype=F32, bias=None, resid=None, gvec=None, keep=False, tn_pref=1024):
    M = pairs[0][0].shape[0]
    N = pairs[0][1].shape[1] if mode == "nn" else pairs[0][1].shape[0]
    tm, tn = _rows(M), _tile(N, tn_pref)
    n_pairs = len(pairs)
    has_bias, has_res = bias is not None, resid is not None

    def body(*refs):
        ab = refs[:2 * n_pairs]
        pos = 2 * n_pairs
        b_ref = refs[pos] if has_bias else None
        pos += has_bias
        r_ref, g_ref = (refs[pos], refs[pos + 1]) if has_res else (None, None)
        pos += 2 * has_res
        outs = refs[pos:]
        acc = None
        for p in range(n_pairs):
            a, b = _bf(ab[2 * p][...]), _bf(ab[2 * p + 1][...])
            d = _nn(a, b) if mode == "nn" else _nt(a, b)
            acc = d if acc is None else acc + d
        if has_bias:
            acc = acc + b_ref[...]
        if has_res:
            outs[0][...] = (r_ref[...] + g_ref[...] * acc).astype(outs[0].dtype)
            if keep:
                outs[1][...] = acc
        else:
            outs[0][...] = acc.astype(outs[0].dtype)

    in_specs, args = [], []
    for a, b in pairs:
        K = a.shape[1]
        in_specs.append(pl.BlockSpec((tm, K), lambda i, j: (i, 0)))
        if mode == "nn":
            in_specs.append(pl.BlockSpec((K, tn), lambda i, j: (0, j)))
        else:
            in_specs.append(pl.BlockSpec((tn, K), lambda i, j: (j, 0)))
        args += [a, b]
    if has_bias:
        in_specs.append(pl.BlockSpec((1, tn), lambda i, j: (0, j)))
        args.append(bias)
    if has_res:
        in_specs.append(pl.BlockSpec((tm, tn), lambda i, j: (i, j)))
        in_specs.append(pl.BlockSpec((1, tn), lambda i, j: (0, j)))
        args += [resid, gvec]
    o_spec = pl.BlockSpec((tm, tn), lambda i, j: (i, j))
    n_out = 2 if (has_res and keep) else 1
    out_shape = [_sds((M, N), out_dtype)] + ([_sds((M, N), F32)] if n_out == 2 else [])
    res = _pcall(body, name=name, out_shape=out_shape, grid=(M // tm, N // tn), in_specs=in_specs,
                 out_specs=[o_spec] * n_out, sem=("parallel", "parallel"))(*args)
    return res if n_out == 2 else res[0]


def _mm_tn(a, b, *, name, tm_pref=1408, tn_pref=1536):
    K, M = a.shape
    N = b.shape[1]
    tm, tn = _tile(M, tm_pref), _tile(N, tn_pref)
    tk = K if K <= ROW_TILE else ROW_TILE

    def body(a_ref, b_ref, o_ref):
        @pl.when(pl.program_id(2) == 0)
        def _():
            o_ref[...] = jnp.zeros_like(o_ref)
        o_ref[...] += _tn(_bf(a_ref[...]), _bf(b_ref[...]))

    return _pcall(
        body, name=name, out_shape=_sds((M, N)), grid=(M // tm, N // tn, K // tk),
        in_specs=[pl.BlockSpec((tk, tm), lambda i, j, k: (k, i)), pl.BlockSpec((tk, tn), lambda i, j, k: (k, j))],
        out_specs=pl.BlockSpec((tm, tn), lambda i, j, k: (i, j)),
        sem=("parallel", "parallel", "arbitrary"))(a, b)


def _mm_swiglu(h, wg, wu, name):
    M, K = h.shape
    N = wg.shape[1]
    tm, tn = _rows(M), _tile(N, 1408)

    def body(h_ref, wg_ref, wu_ref, gate_ref, up_ref, act_ref):
        hv = _bf(h_ref[...])
        gate = _nn(hv, wg_ref[...])
        up = _nn(hv, wu_ref[...])
        gate_ref[...] = gate
        up_ref[...] = up
        act_ref[...] = (_silu(gate) * up).astype(BF16)

    w_spec = pl.BlockSpec((K, tn), lambda i, j: (0, j))
    o_spec = pl.BlockSpec((tm, tn), lambda i, j: (i, j))
    return _pcall(
        body, name=name, out_shape=[_sds((M, N)), _sds((M, N)), _sds((M, N), BF16)], grid=(M // tm, N // tn),
        in_specs=[pl.BlockSpec((tm, K), lambda i, j: (i, 0)), w_spec, w_spec], out_specs=[o_spec] * 3,
        sem=("parallel", "parallel"))(h, wg, wu)


def _mm_swiglu_bwd(dout, wd, gate, up, name):
    M, K = dout.shape
    N = wd.shape[0]
    tm, tn = _rows(M), _tile(N, 1408)

    def body(d_ref, wd_ref, gate_ref, up_ref, dg_ref, du_ref):
        dact = _nt(_bf(d_ref[...]), wd_ref[...])
        g = gate_ref[...]
        dg_ref[...] = (dact * up_ref[...] * _dsilu(g)).astype(BF16)
        du_ref[...] = (dact * _silu(g)).astype(BF16)

    t_spec = pl.BlockSpec((tm, tn), lambda i, j: (i, j))
    return _pcall(
        body, name=name, out_shape=[_sds((M, N), BF16)] * 2, grid=(M // tm, N // tn),
        in_specs=[pl.BlockSpec((tm, K), lambda i, j: (i, 0)), pl.BlockSpec((tn, K), lambda i, j: (j, 0)), t_spec, t_spec],
        out_specs=[t_spec] * 2, sem=("parallel", "parallel"))(dout, wd, gate, up)


def _norm_mod(x, w, sc, sh, name):
    T = x.shape[0]
    tm = _rows(T)

    def body(x_ref, w_ref, sc_ref, sh_ref, o_ref):
        xv = x_ref[...]
        r = lax.rsqrt(jnp.mean(xv * xv, axis=-1, keepdims=True) + EPS)
        o_ref[...] = ((xv * r * w_ref[...]) * (1.0 + sc_ref[...]) + sh_ref[...]).astype(BF16)

    return _pcall(body, name=name, out_shape=_sds((T, D), BF16), grid=(T // tm,),
                  in_specs=[_row_spec(tm, D), _vec_spec(D), _vec_spec(D), _vec_spec(D)],
                  out_specs=_row_spec(tm, D), sem=("parallel",))(x, w, sc, sh)


def _norm_mod_bwd(x, dh, dres, w, sc, name):
    T = x.shape[0]
    tm = _rows(T)

    def body(x_ref, dh_ref, dres_ref, w_ref, sc_ref, dx_ref, acc_ref):
        @pl.when(pl.program_id(0) == 0)
        def _():
            acc_ref[...] = jnp.zeros_like(acc_ref)
        xv, dh_v, wv = x_ref[...], dh_ref[...], w_ref[...]
        r = lax.rsqrt(jnp.mean(xv * xv, axis=-1, keepdims=True) + EPS)
        n = xv * r
        dnw = dh_v * (1.0 + sc_ref[...])
        dn = dnw * wv
        dx_ref[...] = dres_ref[...] + r * (dn - n * jnp.mean(dn * n, axis=-1, keepdims=True))
        acc_ref[0:1, :] += _colsum(dh_v * (n * wv))
        acc_ref[1:2, :] += _colsum(dh_v)
        acc_ref[2:3, :] += _colsum(dnw * n)

    return _pcall(body, name=name, out_shape=[_sds((T, D)), _sds((8, D))], grid=(T // tm,),
                  in_specs=[_row_spec(tm, D), _row_spec(tm, D), _row_spec(tm, D), _vec_spec(D), _vec_spec(D)],
                  out_specs=[_row_spec(tm, D), _vec_spec(D, 8)], sem=("arbitrary",))(x, dh, dres, w, sc)


def _gate_bwd(dx, branch, g, name):
    T = dx.shape[0]
    tm = _rows(T)

    def body(dx_ref, br_ref, g_ref, db_ref, acc_ref):
        @pl.when(pl.program_id(0) == 0)
        def _():
            acc_ref[...] = jnp.zeros_like(acc_ref)
        dxv = dx_ref[...]
        db = g_ref[...] * dxv
        db_ref[...] = db.astype(BF16)
        acc_ref[0:1, :] += _colsum(dxv * br_ref[...])
        acc_ref[1:2, :] += _colsum(db)

    return _pcall(body, name=name, out_shape=[_sds((T, D), BF16), _sds((8, D))], grid=(T // tm,),
                  in_specs=[_row_spec(tm, D), _row_spec(tm, D), _vec_spec(D)],
                  out_specs=[_row_spec(tm, D), _vec_spec(D, 8)], sem=("arbitrary",))(dx, branch, g)


def _final_loss(x, wf, target, name):
    T = x.shape[0]
    tm = _rows(T)

    def body(x_ref, w_ref, t_ref, dx_ref, acc_ref):
        @pl.when(pl.program_id(0) == 0)
        def _():
            acc_ref[...] = jnp.zeros_like(acc_ref)
        xv, wv = x_ref[...], w_ref[...]
        r = lax.rsqrt(jnp.mean(xv * xv, axis=-1, keepdims=True) + EPS)
        n = xv * r
        err = n * wv - t_ref[...]
        dy = err * (1.0 / D)
        dn = dy * wv
        dx_ref[...] = r * (dn - n * jnp.mean(dn * n, axis=-1, keepdims=True))
        acc_ref[0:1, :] += _colsum(dy * n)
        acc_ref[1:2, :] += jnp.broadcast_to(_allsum(err * err) * (0.5 / D), (1, D))

    return _pcall(body, name=name, out_shape=[_sds((T, D)), _sds((8, D))], grid=(T // tm,),
                  in_specs=[_row_spec(tm, D), _vec_spec(D), _row_spec(tm, D)],
                  out_specs=[_row_spec(tm, D), _vec_spec(D, 8)], sem=("arbitrary",))(x, wf, target)


def _colsum_call(x, name):
    T, C = x.shape
    tm = _rows(T)

    def body(x_ref, o_ref):
        @pl.when(pl.program_id(0) == 0)
        def _():
            o_ref[...] = jnp.zeros_like(o_ref)
        o_ref[...] += _colsum(x_ref[...].astype(F32))

    return _pcall(body, name=name, out_shape=_sds((1, C)), grid=(T // tm,), in_specs=[_row_spec(tm, C)],
                  out_specs=_vec_spec(C), sem=("arbitrary",))(x)


def _mod_matmul(c_all, ada_w, name):
    n = ada_w.shape[2]

    def body(c_ref, w_ref, cond_ref, o_ref):
        cond = _silu(c_ref[...])
        cond_ref[...] = cond
        o_ref[0] = _nn(cond, w_ref[0])

    return _pcall(body, name=name, out_shape=[_sds((N_DEV, D)), _sds((DEPTH, N_DEV, n))], grid=(DEPTH,),
                  in_specs=[pl.BlockSpec((N_DEV, D), lambda l: (0, 0)), pl.BlockSpec((1, D, n), lambda l: (l, 0, 0))],
                  out_specs=[pl.BlockSpec((N_DEV, D), lambda l: (0, 0)), pl.BlockSpec((1, N_DEV, n), lambda l: (l, 0, 0))],
                  sem=("arbitrary",))(c_all, ada_w)


def _add_rows(a, b, name):
    def body(a_ref, b_ref, o_ref):
        o_ref[...] = a_ref[...] + b_ref[...]

    return _pcall(body, name=name, out_shape=_sds(a.shape))(a, b)


def _ada_w_grad(cond, dmod_cols, name):
    n = dmod_cols.shape[2]

    def body(c_ref, d_ref, o_ref):
        o_ref[0] = _tn(c_ref[...], d_ref[0])

    return _pcall(body, name=name, out_shape=_sds((DEPTH, D, n)), grid=(DEPTH,),
                  in_specs=[pl.BlockSpec((N_DEV, D), lambda l: (0, 0)), pl.BlockSpec((1, N_DEV, n), lambda l: (l, 0, 0))],
                  out_specs=pl.BlockSpec((1, D, n), lambda l: (l, 0, 0)), sem=("parallel",))(cond, dmod_cols)


def _conv_fwd(pm, conv_w, conv_b, name):
    T = pm.shape[0]
    tm = _rows(T)
    C = CONV_DIM

    def body(x_ref, prev_ref, w_ref, b_ref, o_ref):
        cur = x_ref[...]
        prev = jnp.where(pl.program_id(0) > 0, prev_ref[...], 0.0)
        cur8 = cur[0:8]
        row8 = lax.broadcasted_iota(jnp.int32, (8, C), 0)
        full = w_ref[3:4, :] * cur
        head = w_ref[3:4, :] * cur8
        for k in range(1, SSM_CONV):
            wk = w_ref[3 - k:4 - k, :]
            full = full + wk * pltpu.roll(cur, k, 0)
            head = head + wk * jnp.where(row8 < k, pltpu.roll(prev, k, 0), pltpu.roll(cur8, k, 0))
        o_ref[...] = full + b_ref[...]
        o_ref[0:8, :] = head + b_ref[...]

    return _pcall(
        body, name=name, out_shape=_sds((T, C)), grid=(T // tm,),
        in_specs=[pl.BlockSpec((tm, C), lambda i: (i, 2)),
                  pl.BlockSpec((8, C), lambda i: (jnp.maximum(i * (tm // 8) - 1, 0), 2)),
                  _vec_spec(C, SSM_CONV), _vec_spec(C)],
        out_specs=_row_spec(tm, C), sem=("parallel",))(pm, pm, conv_w, conv_b)


def _conv_bwd(dc, pm, conv_w, name):
    T = dc.shape[0]
    tm = _rows(T)
    C = CONV_DIM
    nt = T // tm

    def body(dc_ref, nxt_ref, x_ref, prev_ref, w_ref, dx_ref, acc_ref):
        i = pl.program_id(0)

        @pl.when(i == 0)
        def _():
            acc_ref[...] = jnp.zeros_like(acc_ref)
        dcv = dc_ref[...]
        nxt = jnp.where(i < nt - 1, nxt_ref[...], 0.0)
        xc = x_ref[...]
        prev = jnp.where(i > 0, prev_ref[...], 0.0)
        dc8h, dc8t, x8 = dcv[0:8], dcv[tm - 8:tm], xc[0:8]
        row8 = lax.broadcasted_iota(jnp.int32, (8, C), 0)
        full = w_ref[3:4, :] * dcv
        tail = w_ref[3:4, :] * dc8t
        acc_ref[3:4, :] += _colsum(dcv * xc)
        for k in range(1, SSM_CONV):
            wk = w_ref[3 - k:4 - k, :]
            full = full + wk * pltpu.roll(dcv, tm - k, 0)
            tail = tail + wk * jnp.where(row8 + k >= 8, pltpu.roll(nxt, 8 - k, 0), pltpu.roll(dc8t, 8 - k, 0))
            xs_head = jnp.where(row8 < k, pltpu.roll(prev, k, 0), pltpu.roll(x8, k, 0))
            prod = dcv * pltpu.roll(xc, k, 0)
            acc_ref[3 - k:4 - k, :] += _colsum(prod) - _colsum(prod[0:8]) + _colsum(dc8h * xs_head)
        acc_ref[4:5, :] += _colsum(dcv)
        dx_ref[...] = full
        dx_ref[tm - 8:tm, :] = tail

    return _pcall(
        body, name=name, out_shape=[_sds((T, C)), _sds((8, C))], grid=(nt,),
        in_specs=[_row_spec(tm, C),
                  pl.BlockSpec((8, C), lambda i: (jnp.minimum((i + 1) * (tm // 8), T // 8 - 1), 0)),
                  pl.BlockSpec((tm, C), lambda i: (i, 2)),
                  pl.BlockSpec((8, C), lambda i: (jnp.maximum(i * (tm // 8) - 1, 0), 2)),
                  _vec_spec(C, SSM_CONV)],
        out_specs=[_row_spec(tm, C), _vec_spec(C, 8)], sem=("arbitrary",))(dc, dc, pm, pm, conv_w)


def _ssd_prologue(cpre, dtr, dtb, alog):
    L = CHUNK
    xc = _silu(cpre)
    pre = dtr + dtb
    dt = jnp.maximum(pre, 0.0) + jnp.log1p(jnp.exp(-jnp.abs(pre)))
    a = -jnp.exp(alog)
    la = dt * a
    row = lax.broadcasted_iota(jnp.int32, (L, L), 0)
    col = lax.broadcasted_iota(jnp.int32, (L, L), 1)
    causal = row >= col
    tri = causal.astype(F32)
    lc = _nn(tri, la, HIGHEST)
    return xc, pre, dt, a, causal, tri, lc, row, col


def _ssd_fwd(cpre, dtr, pm, dtb, alog, dskip, normw, name):
    T = cpre.shape[0]
    nc = T // CHUNK
    L, P, H, HPG = CHUNK, SSM_HEAD_DIM, SSM_HEADS, SSM_HEADS // SSM_GROUPS

    def body(cp_ref, dtr_ref, z_ref, dtb_ref, alog_ref, dskip_ref, nw_ref, ya_ref, y_ref, sp_ref, s_ref):
        @pl.when(pl.program_id(0) == 0)
        def _():
            s_ref[...] = jnp.zeros_like(s_ref)
        xc, _, dt, _, causal, _, lc, _, _ = _ssd_prologue(cp_ref[...], dtr_ref[...], dtb_ref[...], alog_ref[...])
        lct = lc.T
        sp_ref[0] = s_ref[...]
        for g in range(SSM_GROUPS):
            bm = _bf(xc[:, SSM_INNER + g * SSM_STATE:SSM_INNER + (g + 1) * SSM_STATE])
            cm = _bf(xc[:, SSM_INNER + (SSM_GROUPS + g) * SSM_STATE:SSM_INNER + (SSM_GROUPS + g + 1) * SSM_STATE])
            cb = _nt(cm, bm)
            for j in range(HPG):
                h = g * HPG + j
                xs = xc[:, h * P:(h + 1) * P]
                lcol, lrow = lc[:, h:h + 1], lct[h:h + 1, :]
                llast = lc[L - 1:L, h:h + 1]
                decay = jnp.where(causal, jnp.exp(jnp.where(causal, lcol - lrow, 0.0)), 0.0)
                xd = xs * dt[:, h:h + 1]
                s_prev = s_ref[h]
                y = _nn(_bf(cb * decay), _bf(xd))
                y = y + jnp.exp(lcol) * _nt(cm, _bf(s_prev))
                y = y + dskip_ref[:, h * P:(h + 1) * P] * xs
                st = _tn(_bf(xd * jnp.exp(llast - lcol)), bm)
                s_ref[h] = s_prev * jnp.exp(llast) + st
                y_ref[:, h * P:(h + 1) * P] = y
        y2 = y_ref[...] * _silu(z_ref[...])
        half = SSM_INNER // SSM_GROUPS
        for g in range(SSM_GROUPS):
            yg = y2[:, g * half:(g + 1) * half]
            r = lax.rsqrt(jnp.mean(yg * yg, axis=-1, keepdims=True) + EPS)
            ya_ref[:, g * half:(g + 1) * half] = (yg * r * nw_ref[:, g * half:(g + 1) * half]).astype(BF16)

    return _pcall(
        body, name=name,
        out_shape=[_sds((T, SSM_INNER), BF16), _sds((T, SSM_INNER)), _sds((nc, H, P, SSM_STATE))], grid=(nc,),
        in_specs=[_row_spec(L, CONV_DIM), _row_spec(L, LANES), _row_spec(L, SSM_INNER, 0),
                  _vec_spec(LANES), _vec_spec(LANES), _vec_spec(SSM_INNER), _vec_spec(SSM_INNER)],
        out_specs=[_row_spec(L, SSM_INNER), _row_spec(L, SSM_INNER),
                   pl.BlockSpec((1, H, P, SSM_STATE), lambda i: (i, 0, 0, 0))],
        scratch=[pltpu.VMEM((H, P, SSM_STATE), F32)], sem=("arbitrary",))(cpre, dtr, pm, dtb, alog, dskip, normw)


def _ssd_bwd(cpre, dtr, pm, ypre, sprev, dya, dtb, alog, dskip, normw, name):
    T = cpre.shape[0]
    nc = T // CHUNK
    L, P, H, HPG, N = CHUNK, SSM_HEAD_DIM, SSM_HEADS, SSM_HEADS // SSM_GROUPS, SSM_STATE
    half = SSM_INNER // SSM_GROUPS

    def body(cp_ref, dtr_ref, z_ref, y_ref, sp_ref, dya_ref, dtb_ref, alog_ref, dskip_ref, nw_ref,
             dz_ref, dcp_ref, ddtr_ref, acc_ref, dnw_ref, ds_ref, dy_ref):
        @pl.when(pl.program_id(0) == 0)
        def _():
            ds_ref[...] = jnp.zeros_like(ds_ref)
            acc_ref[...] = jnp.zeros_like(acc_ref)
            dnw_ref[...] = jnp.zeros_like(dnw_ref)
        cpre_v = cp_ref[...]
        xc, pre, dt, a, causal, tri, lc, row, col = _ssd_prologue(cpre_v, dtr_ref[...], dtb_ref[...], alog_ref[...])
        lct = lc.T
        zv, yv = z_ref[...], y_ref[...]
        sz = _silu(zv)
        y2 = yv * sz
        dya_v = dya_ref[...]
        nwv = nw_ref[...]
        for g in range(SSM_GROUPS):
            sl = slice(g * half, (g + 1) * half)
            yg = y2[:, sl]
            r = lax.rsqrt(jnp.mean(yg * yg, axis=-1, keepdims=True) + EPS)
            nrm = yg * r
            dnw_ref[:, sl] += _colsum(dya_v[:, sl] * nrm)
            dn = dya_v[:, sl] * nwv[:, sl]
            dy2 = r * (dn - nrm * jnp.mean(dn * nrm, axis=-1, keepdims=True))
            dy_ref[:, sl] = dy2 * sz[:, sl]
            dz_ref[:, sl] = dy2 * yv[:, sl] * _dsilu(zv[:, sl])
        dlc_c = jnp.zeros((L, LANES), F32)
        dlc_r = jnp.zeros((LANES, L), F32)
        ddt = jnp.zeros((L, LANES), F32)
        dskip_acc = jnp.zeros((1, LANES), F32)
        lane_c = lax.broadcasted_iota(jnp.int32, (L, LANES), 1)
        row_c = lax.broadcasted_iota(jnp.int32, (L, LANES), 0)
        sub_r = lax.broadcasted_iota(jnp.int32, (LANES, L), 0)
        lane1 = lax.broadcasted_iota(jnp.int32, (1, LANES), 1)
        for g in range(SSM_GROUPS):
            b_lo = SSM_INNER + g * N
            c_lo = SSM_INNER + (SSM_GROUPS + g) * N
            bmf, cmf = xc[:, b_lo:b_lo + N], xc[:, c_lo:c_lo + N]
            bm, cm = _bf(bmf), _bf(cmf)
            cb = _nt(cm, bm)
            dcb = jnp.zeros((L, L), F32)
            db = jnp.zeros((L, N), F32)
            dcm = jnp.zeros((L, N), F32)
            for j in range(HPG):
                h = g * HPG + j
                xs = xc[:, h * P:(h + 1) * P]
                dyh = dy_ref[:, h * P:(h + 1) * P]
                lcol, lrow = lc[:, h:h + 1], lct[h:h + 1, :]
                llast = lc[L - 1:L, h:h + 1]
                dtc = dt[:, h:h + 1]
                decay = jnp.where(causal, jnp.exp(jnp.where(causal, lcol - lrow, 0.0)), 0.0)
                e_col = jnp.exp(lcol)
                dte = jnp.exp(llast - lcol)
                cd = jnp.exp(llast)
                xd = xs * dtc
                s_prev = sp_ref[0, h]
                ds_new = ds_ref[h]
                m = cb * decay
                dyb = _bf(dyh)
                dxd = _tn(_bf(m), dyb)
                dm = _nt(dyb, _bf(xd))
                gm = dm * m
                dcb = dcb + dm * decay
                dl_col = _rowsum(gm)
                dl_row = -_colsum(gm)
                y0 = _nt(cm, _bf(s_prev))
                dy0 = _bf(e_col * dyh)
                dl_col = dl_col + _rowsum(dyh * y0) * e_col
                dcm = dcm + _nn(dy0, _bf(s_prev))
                ds_prev = _tn(dy0, cm) + ds_new * cd
                rr = _nt(bm, _bf(ds_new))
                dxd = dxd + dte * rr
                tt = _rowsum(rr * xd) * dte
                dl_col = dl_col - tt
                db = db + _nn(_bf(dte * xd), _bf(ds_new))
                dl_last = _colsum(tt) + _allsum(ds_new * s_prev) * cd
                ds_ref[h] = ds_prev
                dcp_ref[:, h * P:(h + 1) * P] = dxd * dtc + dskip_ref[:, h * P:(h + 1) * P] * dyh
                ddt = ddt + jnp.where(lane_c == h, _rowsum(dxd * xs), 0.0)
                dskip_acc = dskip_acc + jnp.where(lane1 == h, _allsum(dyh * xs), 0.0)
                dlc_c = dlc_c + jnp.where(lane_c == h, dl_col, 0.0)
                dlc_c = dlc_c + jnp.where((lane_c == h) & (row_c == L - 1), dl_last, 0.0)
                dlc_r = dlc_r + jnp.where(sub_r == h, dl_row, 0.0)
            dcbb = _bf(dcb)
            dcp_ref[:, c_lo:c_lo + N] = dcm + _nn(dcbb, bm)
            dcp_ref[:, b_lo:b_lo + N] = db + _tn(dcbb, cm)
        dlc = dlc_c + dlc_r.T
        dla = _tn(tri, dlc, HIGHEST)
        ddt = ddt + dla * a
        head_lane = lane_c < H
        ddtr = jnp.where(head_lane, ddt * _sigmoid(pre), 0.0)
        ddtr_ref[...] = ddtr
        acc_ref[0:1, :] += _colsum(ddtr)
        acc_ref[1:2, :] += jnp.where(lane1 < H, _colsum(dla * dt) * a, 0.0)
        acc_ref[2:3, :] += dskip_acc
        dcp_ref[...] = dcp_ref[...] * _dsilu(cpre_v)

    rev = lambda i: (nc - 1 - i, 0)
    rspec = lambda c: pl.BlockSpec((L, c), rev)
    return _pcall(
        body, name=name,
        out_shape=[_sds((T, SSM_INNER)), _sds((T, CONV_DIM)), _sds((T, LANES)), _sds((8, LANES)), _sds((1, SSM_INNER))],
        grid=(nc,),
        in_specs=[rspec(CONV_DIM), rspec(LANES), rspec(SSM_INNER), rspec(SSM_INNER),
                  pl.BlockSpec((1, H, P, N), lambda i: (nc - 1 - i, 0, 0, 0)), rspec(SSM_INNER),
                  _vec_spec(LANES), _vec_spec(LANES), _vec_spec(SSM_INNER), _vec_spec(SSM_INNER)],
        out_specs=[rspec(SSM_INNER), rspec(CONV_DIM), rspec(LANES), _vec_spec(LANES, 8), _vec_spec(SSM_INNER)],
        scratch=[pltpu.VMEM((H, P, N), F32), pltpu.VMEM((L, SSM_INNER), F32)],
        sem=("arbitrary",))(cpre, dtr, pm, ypre, sprev, dya, dtb, alog, dskip, normw)


def _gmlp_common(u, v, lnw, lnb):
    ug = _gelu(u)
    vg = _gelu(v)
    mu = jnp.mean(vg, axis=-1, keepdims=True)
    cen = vg - mu
    rstd = lax.rsqrt(jnp.mean(cen * cen, axis=-1, keepdims=True) + EPS)
    vhat = cen * rstd
    return ug, rstd, vhat, vhat * lnw + lnb


def _causal_mask():
    row = lax.broadcasted_iota(jnp.int32, (CHUNK, CHUNK), 0)
    col = lax.broadcasted_iota(jnp.int32, (CHUNK, CHUNK), 1)
    return row >= col


def _gmlp_fwd(pm, lnw, lnb, ws, bs_exp, name):
    T = pm.shape[0]
    nc = T // CHUNK
    L, G = CHUNK, GMLP_GROUPS

    def body(u_ref, v_ref, lnw_ref, lnb_ref, ws_ref, bs_ref, o_ref):
        ug, _, _, vn = _gmlp_common(u_ref[...], v_ref[...], lnw_ref[...], lnb_ref[...])
        causal = _causal_mask()
        for g in range(G):
            sl = slice(g * L, (g + 1) * L)
            wm = _bf(jnp.where(causal, ws_ref[g], 0.0))
            sv = _nn(wm, _bf(vn[:, sl])) + bs_ref[:, sl]
            o_ref[:, sl] = (ug[:, sl] * sv).astype(BF16)

    return _pcall(
        body, name=name, out_shape=_sds((T, GMLP_INNER), BF16), grid=(nc,),
        in_specs=[_row_spec(L, GMLP_INNER, 1), _row_spec(L, GMLP_INNER, 2), _vec_spec(GMLP_INNER), _vec_spec(GMLP_INNER),
                  pl.BlockSpec((G, L, L), lambda i: (0, 0, 0)), _vec_spec(GMLP_INNER, L)],
        out_specs=_row_spec(L, GMLP_INNER), sem=("parallel",))(pm, pm, lnw, lnb, ws, bs_exp)


def _gmlp_bwd(pm, dyb, lnw, lnb, ws, bs_exp, name):
    T = pm.shape[0]
    nc = T // CHUNK
    L, G = CHUNK, GMLP_GROUPS

    def body(u_ref, v_ref, dy_ref, lnw_ref, lnb_ref, ws_ref, bs_ref, du_ref, dv_ref, dws_ref, dbs_ref, acc_ref, dvn_ref):
        @pl.when(pl.program_id(0) == 0)
        def _():
            dws_ref[...] = jnp.zeros_like(dws_ref)
            dbs_ref[...] = jnp.zeros_like(dbs_ref)
            acc_ref[...] = jnp.zeros_like(acc_ref)
        uv, vv, dyv, lnwv = u_ref[...], v_ref[...], dy_ref[...], lnw_ref[...]
        ug, rstd, vhat, vn = _gmlp_common(uv, vv, lnwv, lnb_ref[...])
        causal = _causal_mask()
        lane = lax.broadcasted_iota(jnp.int32, (L, LANES), 1)
        dbs = jnp.zeros((L, LANES), F32)
        for g in range(G):
            sl = slice(g * L, (g + 1) * L)
            wm = _bf(jnp.where(causal, ws_ref[g], 0.0))
            vng = _bf(vn[:, sl])
            sv = _nn(wm, vng) + bs_ref[:, sl]
            du_ref[:, sl] = dyv[:, sl] * sv * _dgelu(uv[:, sl])
            dsv = dyv[:, sl] * ug[:, sl]
            dsvb = _bf(dsv)
            dws_ref[g] += jnp.where(causal, _nt(dsvb, vng), 0.0)
            dbs = dbs + jnp.where(lane == g, _rowsum(dsv), 0.0)
            dvn_ref[:, sl] = _tn(wm, dsvb)
        dbs_ref[...] += dbs
        dvn = dvn_ref[...]
        acc_ref[0:1, :] += _colsum(dvn * vhat)
        acc_ref[1:2, :] += _colsum(dvn)
        dvh = dvn * lnwv
        dvg = rstd * (dvh - jnp.mean(dvh, axis=-1, keepdims=True) - vhat * jnp.mean(dvh * vhat, axis=-1, keepdims=True))
        dv_ref[...] = dvg * _dgelu(vv)

    return _pcall(
        body, name=name,
        out_shape=[_sds((T, GMLP_INNER)), _sds((T, GMLP_INNER)), _sds((G, L, L)), _sds((L, LANES)), _sds((8, GMLP_INNER))],
        grid=(nc,),
        in_specs=[_row_spec(L, GMLP_INNER, 1), _row_spec(L, GMLP_INNER, 2), _row_spec(L, GMLP_INNER),
                  _vec_spec(GMLP_INNER), _vec_spec(GMLP_INNER), pl.BlockSpec((G, L, L), lambda i: (0, 0, 0)),
                  _vec_spec(GMLP_INNER, L)],
        out_specs=[_row_spec(L, GMLP_INNER), _row_spec(L, GMLP_INNER), pl.BlockSpec((G, L, L), lambda i: (0, 0, 0)),
                   _vec_spec(LANES, L), _vec_spec(GMLP_INNER, 8)],
        scratch=[pltpu.VMEM((L, GMLP_INNER), F32)], sem=("arbitrary",))(pm, pm, dyb, lnw, lnb, ws, bs_exp)


def _rel_buckets():
    qi = np.arange(CHUNK)[:, None]
    sj = np.arange(2 * CHUNK)[None, :]
    dist = np.maximum(qi + CHUNK - sj, 0)
    max_exact = REL_BUCKETS // 2
    log_ratio = (np.log(np.maximum(dist, 1).astype(np.float32) / np.float32(max_exact))
                 / np.float32(math.log(REL_MAX_DIST / max_exact))).astype(np.float32)
    large = max_exact + (log_ratio * np.float32(REL_BUCKETS - max_exact)).astype(np.int32)
    return np.where(dist < max_exact, dist, np.minimum(large, REL_BUCKETS - 1))


def _bucket_onehot_t():
    bucket = _rel_buckets().reshape(-1)
    return jnp.asarray((np.arange(REL_BUCKETS)[:, None] == bucket[None, :]).astype(np.float32))


def _bias_from_table(table_t, onehot_t, name):
    def body(t_ref, o_ref, out_ref):
        out_ref[...] = _nn(t_ref[...], o_ref[...], HIGHEST)

    return _pcall(body, name=name, out_shape=_sds((ATTN_HEADS, onehot_t.shape[1])))(table_t, onehot_t)


def _table_from_dbias(dbias, onehot_t, name):
    def body(d_ref, o_ref, out_ref):
        out_ref[...] = _nt(d_ref[...], o_ref[...], HIGHEST)

    return _pcall(body, name=name, out_shape=_sds((ATTN_HEADS, REL_BUCKETS)))(dbias, onehot_t)


def _attn_probs(qh, kband, bias_h, sink, mask, scale):
    logits = _nt(qh, kband) * scale + bias_h
    logits = jnp.where(mask, logits, NEG_INF)
    mx = jnp.maximum(jnp.max(logits, axis=-1, keepdims=True), sink)
    e = jnp.exp(logits - mx)
    es = jnp.exp(sink - mx)
    inv = 1.0 / (_rowsum(e) + es)
    return e * inv, es * inv


def _attn_mask(n):
    qi = lax.broadcasted_iota(jnp.int32, (CHUNK, 2 * CHUNK), 0)
    sj = lax.broadcasted_iota(jnp.int32, (CHUNK, 2 * CHUNK), 1)
    rel = qi + CHUNK - sj
    return (rel >= 0) & (rel < CHUNK) & ((sj >= CHUNK) | (n > 0))


def _attn_fwd(qkv, bias, sinks, name):
    T = qkv.shape[0]
    nb = T // CHUNK
    L, DH, HPK = CHUNK, ATTN_DH, ATTN_HEADS // ATTN_KV
    scale = DH ** -0.5
    kcol, vcol = ATTN_HEADS * DH // LANES, ATTN_HEADS * DH // LANES + 1

    def body(q_ref, k_ref, v_ref, kp_ref, vp_ref, bias_ref, sink_ref, o_ref):
        n = pl.program_id(0)
        mask = _attn_mask(n)
        kband = _bf(jnp.concatenate([kp_ref[...], k_ref[...]], axis=0))
        vband = _bf(jnp.concatenate([vp_ref[...], v_ref[...]], axis=0))
        for h in range(ATTN_HEADS):
            kv = h // HPK
            qh = _bf(q_ref[:, h * DH:(h + 1) * DH])
            p, _ = _attn_probs(qh, kband[:, kv * DH:(kv + 1) * DH], bias_ref[h], sink_ref[h], mask, scale)
            o_ref[:, h * DH:(h + 1) * DH] = _nn(_bf(p), vband[:, kv * DH:(kv + 1) * DH]).astype(BF16)

    prev = lambda i: jnp.maximum(i - 1, 0)
    return _pcall(
        body, name=name, out_shape=_sds((T, ATTN_HEADS * DH), BF16), grid=(nb,),
        in_specs=[_row_spec(L, ATTN_HEADS * DH, 0), _row_spec(L, LANES, kcol), _row_spec(L, LANES, vcol),
                  pl.BlockSpec((L, LANES), lambda i: (prev(i), kcol)), pl.BlockSpec((L, LANES), lambda i: (prev(i), vcol)),
                  pl.BlockSpec((ATTN_HEADS, L, 2 * L), lambda i: (0, 0, 0)),
                  pl.BlockSpec(memory_space=pltpu.SMEM)],
        out_specs=_row_spec(L, ATTN_HEADS * DH), sem=("parallel",))(qkv, qkv, qkv, qkv, qkv, bias, sinks)


def _attn_bwd(qkv, datt, bias, sinks, name):
    T = qkv.shape[0]
    nb = T // CHUNK
    L, DH, HPK = CHUNK, ATTN_DH, ATTN_HEADS // ATTN_KV
    scale = DH ** -0.5
    kcol, vcol = ATTN_HEADS * DH // LANES, ATTN_HEADS * DH // LANES + 1

    def body(q_ref, k_ref, v_ref, kp_ref, vp_ref, do_ref, bias_ref, sink_ref,
             dq_ref, dk_ref, dv_ref, dbias_ref, dsink_ref, pend_k, pend_v, band_k, band_v):
        n = pl.program_id(0)

        @pl.when(n == 0)
        def _():
            dbias_ref[...] = jnp.zeros_like(dbias_ref)
            dsink_ref[...] = jnp.zeros_like(dsink_ref)

        @pl.when(n < nb)
        def _():
            mask = _attn_mask(n)
            kband = _bf(jnp.concatenate([kp_ref[...], k_ref[...]], axis=0))
            vband = _bf(jnp.concatenate([vp_ref[...], v_ref[...]], axis=0))
            lane1 = lax.broadcasted_iota(jnp.int32, (1, LANES), 1)
            dsink = jnp.zeros((1, LANES), F32)
            for kv in range(ATTN_KV):
                kb, vb = kband[:, kv * DH:(kv + 1) * DH], vband[:, kv * DH:(kv + 1) * DH]
                dkb = jnp.zeros((2 * L, DH), F32)
                dvb = jnp.zeros((2 * L, DH), F32)
                for j in range(HPK):
                    h = kv * HPK + j
                    qh = _bf(q_ref[:, h * DH:(h + 1) * DH])
                    p, ps = _attn_probs(qh, kb, bias_ref[h], sink_ref[h], mask, scale)
                    doh = _bf(do_ref[:, h * DH:(h + 1) * DH])
                    dp = _nt(doh, vb)
                    dvb = dvb + _tn(_bf(p), doh)
                    delta = _rowsum(p * dp)
                    dl = p * (dp - delta)
                    dsink = dsink + jnp.where(lane1 == h, -_colsum(ps * delta), 0.0)
                    dbias_ref[h] += dl
                    dlb = _bf(dl)
                    dq_ref[:, h * DH:(h + 1) * DH] = _nn(dlb, kb) * scale
                    dkb = dkb + _tn(dlb, qh) * scale
                band_k[:, kv * DH:(kv + 1) * DH] = dkb
                band_v[:, kv * DH:(kv + 1) * DH] = dvb
            dsink_ref[...] += dsink

            @pl.when(n > 0)
            def _():
                dk_ref[...] = pend_k[...] + band_k[0:L, :]
                dv_ref[...] = pend_v[...] + band_v[0:L, :]
            pend_k[...] = band_k[L:2 * L, :]
            pend_v[...] = band_v[L:2 * L, :]

        @pl.when(n == nb)
        def _():
            dk_ref[...] = pend_k[...]
            dv_ref[...] = pend_v[...]

    cur = lambda i: jnp.minimum(i, nb - 1)
    prev = lambda i: jnp.maximum(jnp.minimum(i, nb - 1) - 1, 0)
    lag = lambda i: jnp.maximum(i - 1, 0)
    return _pcall(
        body, name=name,
        out_shape=[_sds((T, ATTN_HEADS * DH)), _sds((T, LANES)), _sds((T, LANES)), _sds((ATTN_HEADS, L, 2 * L)), _sds((1, LANES))],
        grid=(nb + 1,),
        in_specs=[pl.BlockSpec((L, ATTN_HEADS * DH), lambda i: (cur(i), 0)),
                  pl.BlockSpec((L, LANES), lambda i: (cur(i), kcol)), pl.BlockSpec((L, LANES), lambda i: (cur(i), vcol)),
                  pl.BlockSpec((L, LANES), lambda i: (prev(i), kcol)), pl.BlockSpec((L, LANES), lambda i: (prev(i), vcol)),
                  pl.BlockSpec((L, ATTN_HEADS * DH), lambda i: (cur(i), 0)),
                  pl.BlockSpec((ATTN_HEADS, L, 2 * L), lambda i: (0, 0, 0)),
                  pl.BlockSpec(memory_space=pltpu.SMEM)],
        out_specs=[pl.BlockSpec((L, ATTN_HEADS * DH), lambda i: (cur(i), 0)),
                   pl.BlockSpec((L, LANES), lambda i: (lag(i), 0)), pl.BlockSpec((L, LANES), lambda i: (lag(i), 0)),
                   pl.BlockSpec((ATTN_HEADS, L, 2 * L), lambda i: (0, 0, 0)), _vec_spec(LANES)],
        scratch=[pltpu.VMEM((L, LANES), F32), pltpu.VMEM((L, LANES), F32),
                 pltpu.VMEM((2 * L, LANES), F32), pltpu.VMEM((2 * L, LANES), F32)],
        sem=("arbitrary",))(qkv, qkv, qkv, qkv, qkv, datt, bias, sinks)


def _pad_rows(a, mult):
    pad = (-a.shape[-2]) % mult
    if pad == 0:
        return a
    cfg = [(0, 0)] * (a.ndim - 2) + [(0, pad), (0, 0)]
    return jnp.pad(a, cfg)


class _Pack:
    def __init__(self, width, mult, total_mult):
        self.width, self.mult, self.total_mult = width, mult, total_mult
        self.entries = []
        self.rows = 0

    def add(self, key, shape):
        n = int(np.prod(shape))
        rows = -(-n // self.width)
        self.entries.append((key, self.rows, rows, tuple(shape)))
        self.rows += -(-rows // self.mult) * self.mult

    @property
    def total(self):
        return -(-self.rows // self.total_mult) * self.total_mult

    def pack(self, pieces, dtype, lead=()):
        parts = []
        for key, _, rows, shape in self.entries:
            a = pieces[key].astype(dtype).reshape(lead + (-1,))
            n = int(np.prod(shape))
            a = jnp.pad(a, [(0, 0)] * len(lead) + [(0, rows * self.width - n)])
            a = a.reshape(lead + (rows, self.width))
            parts.append(_pad_rows(a, self.mult))
        out = jnp.concatenate(parts, axis=len(lead))
        return _pad_rows(out, self.total_mult)

    def unpack(self, packed, lead=()):
        out = {}
        for key, off, rows, shape in self.entries:
            a = lax.slice_in_dim(packed, off, off + rows, axis=len(lead))
            a = a.reshape(lead + (-1,))
            n = int(np.prod(shape))
            out[key] = lax.slice_in_dim(a, 0, n, axis=len(lead)).reshape(lead + shape)
        return out


def _to_dest_major(full, kind):
    if kind == "col":
        K, N = full.shape
        return jnp.transpose(full.reshape(K, N_DEV, N // N_DEV), (1, 0, 2))
    N, K = full.shape
    return full.reshape(N_DEV, N // N_DEV, K)


def _from_source_major(sh, kind):
    if kind == "col":
        _, K, n = sh.shape
        return jnp.transpose(sh, (1, 0, 2)).reshape(K, N_DEV * n)
    _, n, K = sh.shape
    return sh.reshape(N_DEV * n, K)


def _ffn_fwd(x, mod, norm_w, wg, wu, wd, tag):
    h = _norm_mod(x, norm_w, mod[4:5], mod[3:4], f"ffn_norm_{tag}")
    gate, up, act = _mm_swiglu(h, wg, wu, f"ffn_gateup_{tag}")
    x_out, ffn_out = _mm([(act, wd)], "nn", name=f"ffn_down_{tag}", resid=x, gvec=mod[5:6], keep=True)
    return x_out, dict(h=h, gate=gate, up=up, act=act, out=ffn_out)


def _ffn_bwd(dx_out, x_in, saved, mod, norm_w, wg, wu, wd, tag):
    dffn, acc_g = _gate_bwd(dx_out, saved["out"], mod[5:6], f"ffn_gate_bwd_{tag}")
    dgate, dup = _mm_swiglu_bwd(dffn, wd, saved["gate"], saved["up"], f"ffn_act_bwd_{tag}")
    d_wd = _mm_tn(saved["act"], dffn, name=f"ffn_dwd_{tag}")
    d_wg = _mm_tn(saved["h"], dgate, name=f"ffn_dwg_{tag}")
    d_wu = _mm_tn(saved["h"], dup, name=f"ffn_dwu_{tag}")
    dh = _mm([(dgate, wg), (dup, wu)], "nt", name=f"ffn_dh_{tag}")
    dx, acc_n = _norm_mod_bwd(x_in, dh, dx_out, norm_w, mod[4:5], f"ffn_norm_bwd_{tag}")
    return dx, dict(d_wg=d_wg, d_wu=d_wu, d_wd=d_wd, d_g=acc_g[0], d_sc=acc_n[0], d_sh=acc_n[1], d_nw=acc_n[2])


_BIG = [
    ("in_w", "in_w_even", 0, "col"), ("out_w", "out_w_even", 0, "row"), ("qkv_w", "qkv_w", 0, "col"), ("o_w", "o_w", 0, "row"),
    ("gate0", "ffn_gate_w", 0, "col"), ("up0", "ffn_up_w", 0, "col"), ("down0", "ffn_down_w", 0, "row"),
    ("gate1", "ffn_gate_w", 1, "col"), ("up1", "ffn_up_w", 1, "col"), ("down1", "ffn_down_w", 1, "row"),
]

_REPLICATED = ["ada_b", "norm_mix_w", "norm_ffn_w", "conv_b", "dt_bias", "a_log", "d_skip", "ssm_norm_w", "gmlp_ln_w",
               "gmlp_ln_b", "gmlp_ws", "gmlp_bs", "sinks", "rel_table", "final_norm_w"]
_TINY_SHARDED = ["conv_w", "qkv_b", "o_b"]

_WEIGHTS = ['ada_w', 'ada_b', 'norm_mix_w', 'norm_ffn_w', 'in_w_even', 'conv_w', 'conv_b', 'dt_bias', 'a_log', 'd_skip',
            'ssm_norm_w', 'gmlp_ln_w', 'gmlp_ln_b', 'gmlp_ws', 'gmlp_bs', 'out_w_even', 'qkv_w', 'qkv_b', 'o_w', 'o_b',
            'sinks', 'rel_table', 'ffn_gate_w', 'ffn_up_w', 'ffn_down_w', 'final_norm_w']


def _step(x, c, loss_target, W, M, V):
    T = x.shape[1]
    x0 = x[0]
    target = loss_target[0]
    me = 4 * lax.axis_index("x") + 2 * lax.axis_index("y") + lax.axis_index("c")

    big = _Pack(D, 16, 128)
    for key, name, layer, kind in _BIG:
        big.add(key, W[name][layer].shape)
    w_big_local = {key: W[name][layer] for key, name, layer, kind in _BIG}
    gathered = _all_gather(big.pack(w_big_local, BF16), "gather_weights")
    shards = big.unpack(gathered, lead=(N_DEV,))
    full = {key: _from_source_major(shards[key], kind) for key, name, layer, kind in _BIG}

    small_in = _Pack(D, 8, 8)
    small_in.add("c", (1, D))
    small_in.add("conv_w", W["conv_w"][0].shape)
    small_in.add("qkv_b", W["qkv_b"][0].shape)
    small_in.add("o_b", W["o_b"][0].shape)
    sm = small_in.unpack(_all_gather(small_in.pack(
        dict(c=c, conv_w=W["conv_w"][0], qkv_b=W["qkv_b"][0], o_b=W["o_b"][0]), F32), "gather_small"), lead=(N_DEV,))
    c_all = sm["c"].reshape(N_DEV, D)
    conv_w_full = jnp.transpose(sm["conv_w"], (1, 0, 2)).reshape(SSM_CONV, CONV_DIM)
    qkv_b_full = sm["qkv_b"].reshape(1, QKV_DIM)
    o_b_full = sm["o_b"].reshape(1, D)

    ncol = W["ada_w"].shape[2]
    cond, mod_cols = _mod_matmul(c_all, W["ada_w"], "mod_matmul")
    mod_g = _all_gather(mod_cols.reshape(DEPTH * N_DEV, ncol), "gather_mod").reshape(N_DEV, DEPTH, N_DEV, ncol)
    mod_me = lax.dynamic_index_in_dim(mod_g, me, axis=2, keepdims=False)
    mod_me = jnp.transpose(mod_me, (1, 0, 2)).reshape(DEPTH, 6, D)
    mod_me = jnp.pad(mod_me, ((0, 0), (0, 2), (0, 0))).reshape(DEPTH * 8, D)
    ada_b_rows = jnp.pad(W["ada_b"].reshape(DEPTH, 6, D), ((0, 0), (0, 2), (0, 0))).reshape(DEPTH * 8, D)
    mod_all = _add_rows(mod_me, ada_b_rows, "mod_bias").reshape(DEPTH, 8, D)
    mod0, mod1 = mod_all[0], mod_all[1]

    in_w = full["in_w"]
    o1, o2, o3, o4 = SSM_INNER, SSM_INNER + CONV_DIM, SSM_INNER + CONV_DIM + SSM_HEADS, SSM_INNER + CONV_DIM + SSM_HEADS + GMLP_INNER
    w_z, w_xbc, w_dt, w_u, w_v = in_w[:, :o1], in_w[:, o1:o2], in_w[:, o2:o3], in_w[:, o3:o4], in_w[:, o4:]
    w_main = jnp.concatenate([w_z, w_u, w_v, w_xbc], axis=1)
    w_dtp = jnp.pad(w_dt, ((0, 0), (0, LANES - SSM_HEADS)))
    out_w = full["out_w"]
    qkv_w, o_w = full["qkv_w"], full["o_w"]
    w_q, w_k, w_v_att = qkv_w[:, :D], qkv_w[:, D:D + LANES], qkv_w[:, D + LANES:]

    pad16 = lambda a: jnp.pad(a.reshape(1, SSM_HEADS), ((0, 0), (0, LANES - SSM_HEADS)))
    dtb, alog = pad16(W["dt_bias"][0]), pad16(W["a_log"][0])
    dskip = jnp.repeat(W["d_skip"][0], SSM_HEAD_DIM).reshape(1, SSM_INNER)
    ssm_nw = W["ssm_norm_w"]
    lnw, lnb = W["gmlp_ln_w"], W["gmlp_ln_b"]
    ws = W["gmlp_ws"][0]
    bs_exp = jnp.repeat(W["gmlp_bs"][0].T, CHUNK, axis=1)
    conv_b = W["conv_b"]
    nmw, nfw = W["norm_mix_w"], W["norm_ffn_w"]
    onehot_t = _bucket_onehot_t()
    bias = _bias_from_table(W["rel_table"].T, onehot_t, "rel_bias").reshape(ATTN_HEADS, CHUNK, 2 * CHUNK)
    sinks = W["sinks"][0]

    h0 = _norm_mod(x0, nmw[0:1], mod0[1:2], mod0[0:1], "mix_norm_0")
    pm = _mm([(h0, w_main)], "nn", name="in_proj", tn_pref=1536)
    dtr = _mm([(h0, w_dtp)], "nn", name="in_proj_dt")
    cpre = _conv_fwd(pm, conv_w_full, conv_b, "conv_fwd")
    ya, ypre, sprev = _ssd_fwd(cpre, dtr, pm, dtb, alog, dskip, ssm_nw, "ssd_fwd")
    yb = _gmlp_fwd(pm, lnw, lnb, ws, bs_exp, "gmlp_fwd")
    x1, mix0 = _mm([(ya, out_w[:SSM_INNER]), (yb, out_w[SSM_INNER:])], "nn", name="out_proj", resid=x0, gvec=mod0[2:3], keep=True)
    x2, ffn0 = _ffn_fwd(x1, mod0, nfw[0:1], full["gate0"], full["up0"], full["down0"], "0")

    h1 = _norm_mod(x2, nmw[1:2], mod1[1:2], mod1[0:1], "mix_norm_1")
    qkv = _mm([(h1, qkv_w)], "nn", name="qkv_proj", bias=qkv_b_full, tn_pref=1280)
    att = _attn_fwd(qkv, bias, sinks, "attn_fwd")
    x3, mix1 = _mm([(att, o_w)], "nn", name="o_proj", bias=o_b_full, resid=x2, gvec=mod1[2:3], keep=True)
    x4, ffn1 = _ffn_fwd(x3, mod1, nfw[1:2], full["gate1"], full["up1"], full["down1"], "1")

    dx4, acc_f = _final_loss(x4, W["final_norm_w"].reshape(1, D), target, "final_loss")
    loss = lax.psum(acc_f[1, 0], ("x", "y", "c"))
    dx3, gf1 = _ffn_bwd(dx4, x3, ffn1, mod1, nfw[1:2], full["gate1"], full["up1"], full["down1"], "1")

    dmix1, acc_m1 = _gate_bwd(dx3, mix1, mod1[2:3], "mix_gate_bwd_1")
    datt = _mm([(dmix1, o_w)], "nt", name="o_proj_dx")
    d_o_w = _mm_tn(att, dmix1, name="o_proj_dw")
    dq, dk, dv, dbias, dsinks = _attn_bwd(qkv, datt, bias, sinks, "attn_bwd")
    d_table = _table_from_dbias(dbias.reshape(ATTN_HEADS, -1), onehot_t, "rel_table_grad").T
    d_qkv_w = jnp.concatenate([_mm_tn(h1, dq, name="qkv_dw_q"), _mm_tn(h1, dk, name="qkv_dw_k"), _mm_tn(h1, dv, name="qkv_dw_v")], axis=1)
    d_qkv_b = jnp.concatenate([_colsum_call(dq, "qkv_db_q"), _colsum_call(dk, "qkv_db_k"), _colsum_call(dv, "qkv_db_v")], axis=1)
    dh1 = _mm([(dq, w_q), (dk, w_k), (dv, w_v_att)], "nt", name="qkv_proj_dx")
    dx2, acc_n1 = _norm_mod_bwd(x2, dh1, dx3, nmw[1:2], mod1[1:2], "mix_norm_bwd_1")

    dx1, gf0 = _ffn_bwd(dx2, x1, ffn0, mod0, nfw[0:1], full["gate0"], full["up0"], full["down0"], "0")

    dmix0, acc_m0 = _gate_bwd(dx1, mix0, mod0[2:3], "mix_gate_bwd_0")
    dya = _mm([(dmix0, out_w[:SSM_INNER])], "nt", name="out_proj_dx_a")
    dyb = _mm([(dmix0, out_w[SSM_INNER:])], "nt", name="out_proj_dx_b")
    d_out_w = jnp.concatenate([_mm_tn(ya, dmix0, name="out_proj_dw_a"), _mm_tn(yb, dmix0, name="out_proj_dw_b")], axis=0)
    du, dvg, d_ws, d_bs, acc_ln = _gmlp_bwd(pm, dyb, lnw, lnb, ws, bs_exp, "gmlp_bwd")
    dz, dcpre, ddtr, acc_ssd, d_ssm_nw = _ssd_bwd(cpre, dtr, pm, ypre, sprev, dya, dtb, alog, dskip, ssm_nw, "ssd_bwd")
    dxbc, acc_conv = _conv_bwd(dcpre, pm, conv_w_full, "conv_bwd")
    d_in_w = jnp.concatenate([
        _mm_tn(h0, dz, name="in_dw_z"), _mm_tn(h0, dxbc, name="in_dw_xbc"),
        _mm_tn(h0, ddtr, name="in_dw_dt")[:, :SSM_HEADS], _mm_tn(h0, du, name="in_dw_u"), _mm_tn(h0, dvg, name="in_dw_v")], axis=1)
    dh0 = _mm([(dz, w_z), (dxbc, w_xbc), (ddtr, w_dtp), (du, w_u), (dvg, w_v)], "nt", name="in_proj_dx")
    grad_x, acc_n0 = _norm_mod_bwd(x0, dh0, dx1, nmw[0:1], mod0[1:2], "mix_norm_bwd_0")

    g_full = dict(in_w=d_in_w, out_w=d_out_w, qkv_w=d_qkv_w, o_w=d_o_w,
                  gate0=gf0["d_wg"], up0=gf0["d_wu"], down0=gf0["d_wd"], gate1=gf1["d_wg"], up1=gf1["d_wu"], down1=gf1["d_wd"])
    g_dest = {key: _to_dest_major(g_full[key], kind) for key, name, layer, kind in _BIG}
    parts_big = _exchange_partials(big.pack(g_dest, F32, lead=(N_DEV,)), "exchange_grads")
    pk = lambda S: big.pack({key: S[name][layer] for key, name, layer, kind in _BIG}, F32)
    res_big = [big.unpack(r) for r in _adamw(parts_big, pk(W), pk(M), pk(V), "adamw_big")]

    d_mod = jnp.stack([
        jnp.stack([acc_n0[1], acc_n0[0], acc_m0[0], gf0["d_sh"], gf0["d_sc"], gf0["d_g"]]),
        jnp.stack([acc_n1[1], acc_n1[0], acc_m1[0], gf1["d_sh"], gf1["d_sc"], gf1["d_g"]])])
    g_small = dict(
        ada_b=d_mod.reshape(DEPTH, 6 * D),
        norm_mix_w=jnp.stack([acc_n0[2], acc_n1[2]]), norm_ffn_w=jnp.stack([gf0["d_nw"], gf1["d_nw"]]),
        conv_b=acc_conv[4:5], dt_bias=acc_ssd[0:1, :SSM_HEADS], a_log=acc_ssd[1:2, :SSM_HEADS], d_skip=acc_ssd[2:3, :SSM_HEADS],
        ssm_norm_w=d_ssm_nw, gmlp_ln_w=acc_ln[0:1], gmlp_ln_b=acc_ln[1:2], gmlp_ws=d_ws[None],
        gmlp_bs=d_bs[:, :GMLP_GROUPS].T[None], sinks=dsinks[:, :ATTN_HEADS], rel_table=d_table, final_norm_w=acc_f[0],
        conv_w=acc_conv[0:SSM_CONV], qkv_b=d_qkv_b, o_b=acc_m1[1:2])
    small = _Pack(D, 8, 8)
    for name in _REPLICATED:
        small.add(name, W[name].shape)
    small.add("conv_w", (SSM_CONV, CONV_DIM))
    small.add("qkv_b", (1, QKV_DIM))
    small.add("o_b", (1, D))
    parts_small = _all_gather(small.pack(g_small, F32), "gather_small_grads")
    zeros_tiny = dict(conv_w=jnp.zeros((SSM_CONV, CONV_DIM), F32), qkv_b=jnp.zeros((1, QKV_DIM), F32), o_b=jnp.zeros((1, D), F32))
    pks = lambda S: small.pack({**{name: S[name] for name in _REPLICATED}, **zeros_tiny}, F32)
    res_small = [small.unpack(r) for r in _adamw(parts_small, pks(W), pks(M), pks(V), "adamw_small")]
    g_small_sum = res_small[0]

    n_cw, n_qb, n_ob = W["conv_w"].shape[2], W["qkv_b"].shape[1], W["o_b"].shape[1]
    g_tiny = dict(conv_w=lax.dynamic_slice_in_dim(g_small_sum["conv_w"], me * n_cw, n_cw, axis=1)[None],
                  qkv_b=lax.dynamic_slice_in_dim(g_small_sum["qkv_b"], me * n_qb, n_qb, axis=1),
                  o_b=lax.dynamic_slice_in_dim(g_small_sum["o_b"], me * n_ob, n_ob, axis=1))
    tiny = _Pack(D, 8, 8)
    for name in _TINY_SHARDED:
        tiny.add(name, W[name].shape)
    pkt = lambda S: tiny.pack({name: S[name] for name in _TINY_SHARDED}, F32)
    res_tiny = [tiny.unpack(r) for r in _adamw(pkt(g_tiny)[None], pkt(W), pkt(M), pkt(V), "adamw_tiny")]

    dmod_all = parts_small[:, small.entries[0][1]:small.entries[0][1] + small.entries[0][2]].reshape(N_DEV, DEPTH, 6 * D)
    dmod_cols = jnp.transpose(lax.dynamic_slice_in_dim(dmod_all, me * ncol, ncol, axis=2), (1, 0, 2))
    g_ada_w = _ada_w_grad(cond, dmod_cols, "ada_w_grad")
    flat = lambda a: a.reshape(DEPTH * D, ncol)
    res_ada = [r.reshape(DEPTH, D, ncol) for r in _adamw(flat(g_ada_w)[None], flat(W["ada_w"]), flat(M["ada_w"]), flat(V["ada_w"]), "adamw_ada_w")]

    def result(kind_idx, name):
        if name == "ada_w":
            return res_ada[kind_idx]
        if name in _REPLICATED:
            return res_small[kind_idx][name]
        if name in _TINY_SHARDED:
            return res_tiny[kind_idx][name]
        pieces = [res_big[kind_idx][key] for key, nm, layer, kind in _BIG if nm == name]
        return jnp.stack(pieces)

    outs = [loss, grad_x[None]]
    for kind_idx in range(4):
        outs += [result(kind_idx, name) for name in _WEIGHTS]
    return tuple(outs)


def kernel(x, c, ada_w, ada_b, norm_mix_w, norm_ffn_w, in_w_even, conv_w, conv_b, dt_bias, a_log, d_skip, ssm_norm_w, gmlp_ln_w, gmlp_ln_b, gmlp_ws, gmlp_bs, out_w_even, qkv_w, qkv_b, o_w, o_b, sinks, rel_table, ffn_gate_w, ffn_up_w, ffn_down_w, final_norm_w, loss_target, m_ada_w, m_ada_b, m_norm_mix_w, m_norm_ffn_w, m_in_w_even, m_conv_w, m_conv_b, m_dt_bias, m_a_log, m_d_skip, m_ssm_norm_w, m_gmlp_ln_w, m_gmlp_ln_b, m_gmlp_ws, m_gmlp_bs, m_out_w_even, m_qkv_w, m_qkv_b, m_o_w, m_o_b, m_sinks, m_rel_table, m_ffn_gate_w, m_ffn_up_w, m_ffn_down_w, m_final_norm_w, v_ada_w, v_ada_b, v_norm_mix_w, v_norm_ffn_w, v_in_w_even, v_conv_w, v_conv_b, v_dt_bias, v_a_log, v_d_skip, v_ssm_norm_w, v_gmlp_ln_w, v_gmlp_ln_b, v_gmlp_ws, v_gmlp_bs, v_out_w_even, v_qkv_w, v_qkv_b, v_o_w, v_o_b, v_sinks, v_rel_table, v_ffn_gate_w, v_ffn_up_w, v_ffn_down_w, v_final_norm_w):
    args = locals()
    W = {n: args[n] for n in _WEIGHTS}
    M = {n: args["m_" + n] for n in _WEIGHTS}
    V = {n: args["v_" + n] for n in _WEIGHTS}
    return _step(x, c, loss_target, W, M, V)
```

```python
import functools
import math

import numpy as np
import jax
import jax.numpy as jnp
from jax import lax
from jax.experimental import pallas as pl
from jax.experimental.pallas import tpu as pltpu

F32 = jnp.float32
BF16 = jnp.bfloat16
HIGHEST = lax.Precision.HIGHEST
MESH = pl.DeviceIdType.MESH

N_DEV = 8
D = 1024
DEPTH = 2
SSM_HEADS = 16
SSM_HEAD_DIM = 64
SSM_INNER = 1024
SSM_GROUPS = 2
SSM_STATE = 128
SSM_CONV = 4
CHUNK = 128
CONV_DIM = SSM_INNER + 2 * SSM_GROUPS * SSM_STATE
GMLP_GROUPS = 8
GMLP_INNER = 1024
IN_EVEN = 4624
ATTN_HEADS = 16
ATTN_KV = 2
ATTN_DH = 64
QKV_DIM = 1280
REL_BUCKETS = 32
REL_MAX_DIST = 128
FFN = 2816
EPS = 1e-6
NEG_INF = -1e30
LANES = 128

ADAM_LR = 0.001
ADAM_B1 = 0.9
ADAM_B2 = 0.999
ADAM_EPS = 1e-08
ADAM_WD = 0.01
ADAM_STEP = 10

VMEM_LIMIT_BYTES = 56 * 1024 * 1024
ROW_TILE = 512


def _pcall(body, *, name, out_shape, grid=(), in_specs=None, out_specs=None, scratch=(), sem=None):
    params = dict(vmem_limit_bytes=VMEM_LIMIT_BYTES)
    if sem is not None:
        params["dimension_semantics"] = sem
    specs = {} if in_specs is None else dict(in_specs=in_specs, out_specs=out_specs)
    return pl.pallas_call(
        body, name=name, out_shape=out_shape, grid=grid, **specs,
        scratch_shapes=list(scratch), compiler_params=pltpu.CompilerParams(**params))


def _tile(n, pref):
    if n <= pref:
        return n
    best = None
    for t in range(LANES, pref + 1, LANES):
        if n % t == 0:
            best = t
    assert best is not None, (n, pref)
    return best


def _rows(T):
    return min(ROW_TILE, T)


def _sds(shape, dtype=F32):
    return jax.ShapeDtypeStruct(shape, dtype)


def _row_spec(tm, c, col=0):
    return pl.BlockSpec((tm, c), lambda i, col=col: (i, col))


def _vec_spec(c, r=1):
    return pl.BlockSpec((r, c), lambda i: (0, 0))


def _sigmoid(x):
    return jax.nn.sigmoid(x)


def _silu(x):
    return x * _sigmoid(x)


def _dsilu(x):
    s = _sigmoid(x)
    return s * (1.0 + x * (1.0 - s))


def _gelu(x):
    return 0.5 * x * (1.0 + lax.erf(x * 0.7071067811865476))


def _dgelu(x):
    return 0.5 * (1.0 + lax.erf(x * 0.7071067811865476)) + x * jnp.exp(-0.5 * x * x) * 0.3989422804014327


def _dot(a, b, dims, precision=None):
    return lax.dot_general(a, b, (dims, ((), ())), precision=precision, preferred_element_type=F32)


def _nn(a, b, precision=None):
    return _dot(a, b, ((1,), (0,)), precision)


def _nt(a, b, precision=None):
    return _dot(a, b, ((1,), (1,)), precision)


def _tn(a, b, precision=None):
    return _dot(a, b, ((0,), (0,)), precision)


def _bf(x):
    return x.astype(BF16)


def _colsum(x):
    return jnp.sum(x, axis=0, keepdims=True)


def _rowsum(x):
    return jnp.sum(x, axis=1, keepdims=True)


def _allsum(x):
    return _colsum(_rowsum(x))


def _all_gather(x, name):
    R, C = x.shape

    def body(x_ref, out_ref, send_sems, recv_sems, local_sem):
        mx, my, mc = lax.axis_index("x"), lax.axis_index("y"), lax.axis_index("c")
        me, sibling = (mx, my, mc), (mx, my, 1 - mc)
        chips = [(1 - mx, my), (mx, 1 - my), (1 - mx, 1 - my)]

        def slot(px, py, pc):
            return out_ref.at[4 * px + 2 * py + pc]

        def copy(k, block, to, src=None):
            return pltpu.make_async_remote_copy(
                src_ref=slot(*block) if src is None else src, dst_ref=slot(*block),
                send_sem=send_sems.at[k], recv_sem=recv_sems.at[k], device_id=to, device_id_type=MESH)

        mine = pltpu.make_async_copy(x_ref, slot(*me), local_sem)
        mine.start()
        first = [copy(0, me, sibling, src=x_ref)]
        first += [copy(1 + j, me, (*chip, mc), src=x_ref) for j, chip in enumerate(chips)]
        for cp in first:
            cp.start()
        passed = [copy(4 + j, (*chip, mc), sibling) for j, chip in enumerate(chips)]
        for j, chip in enumerate(chips):
            copy(1 + j, (*chip, mc), me).wait_recv()
            passed[j].start()
        copy(0, sibling, me).wait_recv()
        for j, chip in enumerate(chips):
            copy(4 + j, (*chip, 1 - mc), me).wait_recv()
        for cp in first + passed:
            cp.wait_send()
        mine.wait()

    return pl.pallas_call(
        body, name=name, out_shape=_sds((N_DEV, R, C), x.dtype),
        in_specs=[pl.BlockSpec(memory_space=pl.ANY)], out_specs=pl.BlockSpec(memory_space=pl.ANY),
        scratch_shapes=[pltpu.SemaphoreType.DMA((7,)), pltpu.SemaphoreType.DMA((7,)), pltpu.SemaphoreType.DMA(())],
    )(x)


N_CHIP = 4


def _exchange_sibling(p, name):
    _, R, C = p.shape

    def body(p_ref, theirs_ref, mine_ref, send_sems, recv_sems, local_sems):
        mx, my, mc = lax.axis_index("x"), lax.axis_index("y"), lax.axis_index("c")
        copies = []
        for chip in range(N_CHIP):
            local = pltpu.make_async_copy(p_ref.at[2 * chip + mc], mine_ref.at[chip], local_sems.at[chip])
            remote = pltpu.make_async_remote_copy(
                src_ref=p_ref.at[2 * chip + 1 - mc], dst_ref=theirs_ref.at[chip],
                send_sem=send_sems.at[chip], recv_sem=recv_sems.at[chip],
                device_id=(mx, my, 1 - mc), device_id_type=MESH)
            local.start()
            remote.start()
            copies += [local, remote]
        for cp in copies:
            cp.wait()

    return pl.pallas_call(
        body, name=name, out_shape=[_sds((N_CHIP, R, C), p.dtype)] * 2,
        in_specs=[pl.BlockSpec(memory_space=pl.ANY)], out_specs=[pl.BlockSpec(memory_space=pl.ANY)] * 2,
        scratch_shapes=[pltpu.SemaphoreType.DMA((N_CHIP,))] * 3,
    )(p)


def _exchange_chips(q, name):
    _, R, C = q.shape

    def body(q_ref, out_ref, send_sems, recv_sems, local_sem):
        mx, my, mc = lax.axis_index("x"), lax.axis_index("y"), lax.axis_index("c")
        me = 2 * mx + my
        local = pltpu.make_async_copy(q_ref.at[me], out_ref.at[me], local_sem)
        local.start()
        copies = []
        for r in range(1, N_CHIP):
            px = 1 - mx if r & 2 else mx
            py = 1 - my if r & 1 else my
            cp = pltpu.make_async_remote_copy(
                src_ref=q_ref.at[2 * px + py], dst_ref=out_ref.at[me],
                send_sem=send_sems.at[r - 1], recv_sem=recv_sems.at[r - 1],
                device_id=(px, py, mc), device_id_type=MESH)
            cp.start()
            copies.append(cp)
        for cp in copies:
            cp.wait()
        local.wait()

    return pl.pallas_call(
        body, name=name, out_shape=_sds((N_CHIP, R, C), q.dtype),
        in_specs=[pl.BlockSpec(memory_space=pl.ANY)], out_specs=pl.BlockSpec(memory_space=pl.ANY),
        scratch_shapes=[pltpu.SemaphoreType.DMA((3,)), pltpu.SemaphoreType.DMA((3,)), pltpu.SemaphoreType.DMA(())],
    )(q)


def _pair_sum(a, b, name):
    n, R, C = a.shape
    tr = _tile_rows(R, 256)

    def body(a_ref, b_ref, o_ref):
        o_ref[...] = (a_ref[...] + b_ref[...]).astype(BF16)

    blk = pl.BlockSpec((1, tr, C), lambda s, i: (s, i, 0))
    return _pcall(body, name=name, out_shape=_sds((n, R, C), BF16), grid=(n, R // tr), in_specs=[blk, blk],
                  out_specs=blk, sem=("parallel", "parallel"))(a, b)


def _sum_parts(parts, name):
    P, R, C = parts.shape
    tr = _tile_rows(R, 256)

    def body(p_ref, o_ref):
        g = p_ref[0].astype(F32)
        for k in range(1, P):
            g = g + p_ref[k].astype(F32)
        o_ref[...] = g

    return _pcall(body, name=name, out_shape=_sds((R, C)), grid=(R // tr,),
                  in_specs=[pl.BlockSpec((P, tr, C), lambda i: (0, i, 0))],
                  out_specs=pl.BlockSpec((tr, C), lambda i: (i, 0)), sem=("parallel",))(parts)


def _adamw(parts, w, m, v, name):
    P, R, C = parts.shape
    tr = R if R <= 256 else _tile_rows(R, 256)

    def body(p_ref, w_ref, m_ref, v_ref, g_ref, d_ref, nm_ref, nv_ref):
        g = p_ref[0]
        for k in range(1, P):
            g = g + p_ref[k]
        nm = ADAM_B1 * m_ref[...] + (1.0 - ADAM_B1) * g
        nv = ADAM_B2 * v_ref[...] + (1.0 - ADAM_B2) * (g * g)
        m_hat = nm / (1.0 - ADAM_B1 ** ADAM_STEP)
        v_hat = nv / (1.0 - ADAM_B2 ** ADAM_STEP)
        g_ref[...] = g
        d_ref[...] = -ADAM_LR * (m_hat / (jnp.sqrt(v_hat) + ADAM_EPS) + ADAM_WD * w_ref[...])
        nm_ref[...] = nm
        nv_ref[...] = nv

    blk = pl.BlockSpec((tr, C), lambda i: (i, 0))
    return _pcall(
        body, name=name, out_shape=[_sds((R, C))] * 4, grid=(R // tr,),
        in_specs=[pl.BlockSpec((P, tr, C), lambda i: (0, i, 0)), blk, blk, blk],
        out_specs=[blk] * 4, sem=("parallel",))(parts, w, m, v)


def _tile_rows(n, pref):
    best = None
    for t in range(8, pref + 1, 8):
        if n % t == 0:
            best = t
    assert best is not None, (n, pref)
    return best


def _mm(pairs, mode, *, name, out_dtype=F32, bias=None, resid=None, gvec=None, keep=False, tn_pref=1024):
    M = pairs[0][0].shape[0]
    N = pairs[0][1].shape[1] if mode == "nn" else pairs[0][1].shape[0]
    tm, tn = _rows(M), _tile(N, tn_pref)
    n_pairs = len(pairs)
    has_bias, has_res = bias is not None, resid is not None

    def body(*refs):
        ab = refs[:2 * n_pairs]
        pos = 2 * n_pairs
        b_ref = refs[pos] if has_bias else None
        pos += has_bias
        r_ref, g_ref = (refs[pos], refs[pos + 1]) if has_res else (None, None)
        pos += 2 * has_res
        outs = refs[pos:]
        acc = None
        for p in range(n_pairs):
            a, b = _bf(ab[2 * p][...]), _bf(ab[2 * p + 1][...])
            d = _nn(a, b) if mode == "nn" else _nt(a, b)
            acc = d if acc is None else acc + d
        if has_bias:
            acc = acc + b_ref[...]
        if has_res:
            outs[0][...] = (r_ref[...] + g_ref[...] * acc).astype(outs[0].dtype)
            if keep:
                outs[1][...] = acc
        else:
            outs[0][...] = acc.astype(outs[0].dtype)

    in_specs, args = [], []
    for a, b in pairs:
        K = a.shape[1]
        in_specs.append(pl.BlockSpec((tm, K), lambda i, j: (i, 0)))
        if mode == "nn":
            in_specs.append(pl.BlockSpec((K, tn), lambda i, j: (0, j)))
        else:
            in_specs.append(pl.BlockSpec((tn, K), lambda i, j: (j, 0)))
        args += [a, b]
    if has_bias:
        in_specs.append(pl.BlockSpec((1, tn), lambda i, j: (0, j)))
        args.append(bias)
    if has_res:
        in_specs.append(pl.BlockSpec((tm, tn), lambda i, j: (i, j)))
        in_specs.append(pl.BlockSpec((1, tn), lambda i, j: (0, j)))
        args += [resid, gvec]
    o_spec = pl.BlockSpec((tm, tn), lambda i, j: (i, j))
    n_out = 2 if (has_res and keep) else 1
    out_shape = [_sds((M, N), out_dtype)] + ([_sds((M, N), F32)] if n_out == 2 else [])
    res = _pcall(body, name=name, out_shape=out_shape, grid=(M // tm, N // tn), in_specs=in_specs,
                 out_specs=[o_spec] * n_out, sem=("parallel", "parallel"))(*args)
    return res if n_out == 2 else res[0]


def _mm_tn(a, b, *, name, tm_pref=1408, tn_pref=1536):
    K, M = a.shape
    N = b.shape[1]
    tm, tn = _tile(M, tm_pref), _tile(N, tn_pref)
    tk = K if K <= ROW_TILE else ROW_TILE

    def body(a_ref, b_ref, o_ref):
        @pl.when(pl.program_id(2) == 0)
        def _():
            o_ref[...] = jnp.zeros_like(o_ref)
        o_ref[...] += _tn(_bf(a_ref[...]), _bf(b_ref[...]))

    return _pcall(
        body, name=name, out_shape=_sds((M, N)), grid=(M // tm, N // tn, K // tk),
        in_specs=[pl.BlockSpec((tk, tm), lambda i, j, k: (k, i)), pl.BlockSpec((tk, tn), lambda i, j, k: (k, j))],
        out_specs=pl.BlockSpec((tm, tn), lambda i, j, k: (i, j)),
        sem=("parallel", "parallel", "arbitrary"))(a, b)


def _mm_swiglu(h, wg_t, wu_t, name):
    M, K = h.shape
    N = wg_t.shape[0]
    tm, tn = _rows(M), _tile(N, 1408)

    def body(h_ref, wg_ref, wu_ref, gate_ref, up_ref, act_ref):
        hv = _bf(h_ref[...])
        gate = _nt(hv, wg_ref[...])
        up = _nt(hv, wu_ref[...])
        gate_ref[...] = gate
        up_ref[...] = up
        act_ref[...] = (_silu(gate) * up).astype(BF16)

    w_spec = pl.BlockSpec((tn, K), lambda i, j: (j, 0))
    o_spec = pl.BlockSpec((tm, tn), lambda i, j: (i, j))
    return _pcall(
        body, name=name, out_shape=[_sds((M, N)), _sds((M, N)), _sds((M, N), BF16)], grid=(M // tm, N // tn),
        in_specs=[pl.BlockSpec((tm, K), lambda i, j: (i, 0)), w_spec, w_spec], out_specs=[o_spec] * 3,
        sem=("parallel", "parallel"))(h, wg_t, wu_t)


def _mm_swiglu_bwd(dout, wd, gate, up, name):
    M, K = dout.shape
    N = wd.shape[0]
    tm, tn = _rows(M), _tile(N, 1408)

    def body(d_ref, wd_ref, gate_ref, up_ref, dg_ref, du_ref):
        dact = _nt(_bf(d_ref[...]), wd_ref[...])
        g = gate_ref[...]
        dg_ref[...] = (dact * up_ref[...] * _dsilu(g)).astype(BF16)
        du_ref[...] = (dact * _silu(g)).astype(BF16)

    t_spec = pl.BlockSpec((tm, tn), lambda i, j: (i, j))
    return _pcall(
        body, name=name, out_shape=[_sds((M, N), BF16)] * 2, grid=(M // tm, N // tn),
        in_specs=[pl.BlockSpec((tm, K), lambda i, j: (i, 0)), pl.BlockSpec((tn, K), lambda i, j: (j, 0)), t_spec, t_spec],
        out_specs=[t_spec] * 2, sem=("parallel", "parallel"))(dout, wd, gate, up)


def _norm_mod(x, w, sc, sh, name):
    T = x.shape[0]
    tm = _rows(T)

    def body(x_ref, w_ref, sc_ref, sh_ref, o_ref):
        xv = x_ref[...]
        r = lax.rsqrt(jnp.mean(xv * xv, axis=-1, keepdims=True) + EPS)
        o_ref[...] = ((xv * r * w_ref[...]) * (1.0 + sc_ref[...]) + sh_ref[...]).astype(BF16)

    return _pcall(body, name=name, out_shape=_sds((T, D), BF16), grid=(T // tm,),
                  in_specs=[_row_spec(tm, D), _vec_spec(D), _vec_spec(D), _vec_spec(D)],
                  out_specs=_row_spec(tm, D), sem=("parallel",))(x, w, sc, sh)


def _norm_mod_bwd(x, dh, dres, w, sc, name):
    T = x.shape[0]
    tm = _rows(T)

    def body(x_ref, dh_ref, dres_ref, w_ref, sc_ref, dx_ref, acc_ref):
        @pl.when(pl.program_id(0) == 0)
        def _():
            acc_ref[...] = jnp.zeros_like(acc_ref)
        xv, dh_v, wv = x_ref[...], dh_ref[...], w_ref[...]
        r = lax.rsqrt(jnp.mean(xv * xv, axis=-1, keepdims=True) + EPS)
        n = xv * r
        dnw = dh_v * (1.0 + sc_ref[...])
        dn = dnw * wv
        dx_ref[...] = dres_ref[...] + r * (dn - n * jnp.mean(dn * n, axis=-1, keepdims=True))
        acc_ref[0:1, :] += _colsum(dh_v * (n * wv))
        acc_ref[1:2, :] += _colsum(dh_v)
        acc_ref[2:3, :] += _colsum(dnw * n)

    return _pcall(body, name=name, out_shape=[_sds((T, D)), _sds((8, D))], grid=(T // tm,),
                  in_specs=[_row_spec(tm, D), _row_spec(tm, D), _row_spec(tm, D), _vec_spec(D), _vec_spec(D)],
                  out_specs=[_row_spec(tm, D), _vec_spec(D, 8)], sem=("arbitrary",))(x, dh, dres, w, sc)


def _gate_bwd(dx, branch, g, name):
    T = dx.shape[0]
    tm = _rows(T)

    def body(dx_ref, br_ref, g_ref, db_ref, acc_ref):
        @pl.when(pl.program_id(0) == 0)
        def _():
            acc_ref[...] = jnp.zeros_like(acc_ref)
        dxv = dx_ref[...]
        db = g_ref[...] * dxv
        db_ref[...] = db.astype(BF16)
        acc_ref[0:1, :] += _colsum(dxv * br_ref[...])
        acc_ref[1:2, :] += _colsum(db)

    return _pcall(body, name=name, out_shape=[_sds((T, D), BF16), _sds((8, D))], grid=(T // tm,),
                  in_specs=[_row_spec(tm, D), _row_spec(tm, D), _vec_spec(D)],
                  out_specs=[_row_spec(tm, D), _vec_spec(D, 8)], sem=("arbitrary",))(dx, branch, g)


def _final_loss(x, wf, target, name):
    T = x.shape[0]
    tm = _rows(T)

    def body(x_ref, w_ref, t_ref, dx_ref, acc_ref):
        @pl.when(pl.program_id(0) == 0)
        def _():
            acc_ref[...] = jnp.zeros_like(acc_ref)
        xv, wv = x_ref[...], w_ref[...]
        r = lax.rsqrt(jnp.mean(xv * xv, axis=-1, keepdims=True) + EPS)
        n = xv * r
        err = n * wv - t_ref[...]
        dy = err * (1.0 / D)
        dn = dy * wv
        dx_ref[...] = r * (dn - n * jnp.mean(dn * n, axis=-1, keepdims=True))
        acc_ref[0:1, :] += _colsum(dy * n)
        acc_ref[1:2, :] += jnp.broadcast_to(_allsum(err * err) * (0.5 / D), (1, D))

    return _pcall(body, name=name, out_shape=[_sds((T, D)), _sds((8, D))], grid=(T // tm,),
                  in_specs=[_row_spec(tm, D), _vec_spec(D), _row_spec(tm, D)],
                  out_specs=[_row_spec(tm, D), _vec_spec(D, 8)], sem=("arbitrary",))(x, wf, target)


def _colsum_call(x, name):
    T, C = x.shape
    tm = _rows(T)

    def body(x_ref, o_ref):
        @pl.when(pl.program_id(0) == 0)
        def _():
            o_ref[...] = jnp.zeros_like(o_ref)
        o_ref[...] += _colsum(x_ref[...].astype(F32))

    return _pcall(body, name=name, out_shape=_sds((1, C)), grid=(T // tm,), in_specs=[_row_spec(tm, C)],
                  out_specs=_vec_spec(C), sem=("arbitrary",))(x)


def _mod_matmul(c_all, ada_w, name):
    n = ada_w.shape[2]

    def body(c_ref, w_ref, cond_ref, o_ref):
        cond = _silu(c_ref[...])
        cond_ref[...] = cond
        o_ref[0] = _nn(cond, w_ref[0])

    return _pcall(body, name=name, out_shape=[_sds((N_DEV, D)), _sds((DEPTH, N_DEV, n))], grid=(DEPTH,),
                  in_specs=[pl.BlockSpec((N_DEV, D), lambda l: (0, 0)), pl.BlockSpec((1, D, n), lambda l: (l, 0, 0))],
                  out_specs=[pl.BlockSpec((N_DEV, D), lambda l: (0, 0)), pl.BlockSpec((1, N_DEV, n), lambda l: (l, 0, 0))],
                  sem=("arbitrary",))(c_all, ada_w)


def _add_rows(a, b, name):
    def body(a_ref, b_ref, o_ref):
        o_ref[...] = a_ref[...] + b_ref[...]

    return _pcall(body, name=name, out_shape=_sds(a.shape))(a, b)


def _ada_w_grad(cond, dmod_cols, name):
    n = dmod_cols.shape[2]

    def body(c_ref, d_ref, o_ref):
        o_ref[0] = _tn(c_ref[...], d_ref[0])

    return _pcall(body, name=name, out_shape=_sds((DEPTH, D, n)), grid=(DEPTH,),
                  in_specs=[pl.BlockSpec((N_DEV, D), lambda l: (0, 0)), pl.BlockSpec((1, N_DEV, n), lambda l: (l, 0, 0))],
                  out_specs=pl.BlockSpec((1, D, n), lambda l: (l, 0, 0)), sem=("parallel",))(cond, dmod_cols)


def _conv_fwd(pm, conv_w, conv_b, name):
    T = pm.shape[0]
    tm = _rows(T)
    C = CONV_DIM

    def body(x_ref, prev_ref, w_ref, b_ref, o_ref):
        cur = x_ref[...]
        prev = jnp.where(pl.program_id(0) > 0, prev_ref[...], 0.0)
        cur8 = cur[0:8]
        row8 = lax.broadcasted_iota(jnp.int32, (8, C), 0)
        full = w_ref[3:4, :] * cur
        head = w_ref[3:4, :] * cur8
        for k in range(1, SSM_CONV):
            wk = w_ref[3 - k:4 - k, :]
            full = full + wk * pltpu.roll(cur, k, 0)
            head = head + wk * jnp.where(row8 < k, pltpu.roll(prev, k, 0), pltpu.roll(cur8, k, 0))
        o_ref[...] = full + b_ref[...]
        o_ref[0:8, :] = head + b_ref[...]

    return _pcall(
        body, name=name, out_shape=_sds((T, C)), grid=(T // tm,),
        in_specs=[pl.BlockSpec((tm, C), lambda i: (i, 2)),
                  pl.BlockSpec((8, C), lambda i: (jnp.maximum(i * (tm // 8) - 1, 0), 2)),
                  _vec_spec(C, SSM_CONV), _vec_spec(C)],
        out_specs=_row_spec(tm, C), sem=("parallel",))(pm, pm, conv_w, conv_b)


def _conv_bwd(dc, pm, conv_w, name):
    T = dc.shape[0]
    tm = _rows(T)
    C = CONV_DIM
    nt = T // tm

    def body(dc_ref, nxt_ref, x_ref, prev_ref, w_ref, dx_ref, acc_ref):
        i = pl.program_id(0)

        @pl.when(i == 0)
        def _():
            acc_ref[...] = jnp.zeros_like(acc_ref)
        dcv = dc_ref[...]
        nxt = jnp.where(i < nt - 1, nxt_ref[...], 0.0)
        xc = x_ref[...]
        prev = jnp.where(i > 0, prev_ref[...], 0.0)
        dc8h, dc8t, x8 = dcv[0:8], dcv[tm - 8:tm], xc[0:8]
        row8 = lax.broadcasted_iota(jnp.int32, (8, C), 0)
        full = w_ref[3:4, :] * dcv
        tail = w_ref[3:4, :] * dc8t
        acc_ref[3:4, :] += _colsum(dcv * xc)
        for k in range(1, SSM_CONV):
            wk = w_ref[3 - k:4 - k, :]
            full = full + wk * pltpu.roll(dcv, tm - k, 0)
            tail = tail + wk * jnp.where(row8 + k >= 8, pltpu.roll(nxt, 8 - k, 0), pltpu.roll(dc8t, 8 - k, 0))
            xs_head = jnp.where(row8 < k, pltpu.roll(prev, k, 0), pltpu.roll(x8, k, 0))
            prod = dcv * pltpu.roll(xc, k, 0)
            acc_ref[3 - k:4 - k, :] += _colsum(prod) - _colsum(prod[0:8]) + _colsum(dc8h * xs_head)
        acc_ref[4:5, :] += _colsum(dcv)
        dx_ref[...] = full
        dx_ref[tm - 8:tm, :] = tail

    return _pcall(
        body, name=name, out_shape=[_sds((T, C)), _sds((8, C))], grid=(nt,),
        in_specs=[_row_spec(tm, C),
                  pl.BlockSpec((8, C), lambda i: (jnp.minimum((i + 1) * (tm // 8), T // 8 - 1), 0)),
                  pl.BlockSpec((tm, C), lambda i: (i, 2)),
                  pl.BlockSpec((8, C), lambda i: (jnp.maximum(i * (tm // 8) - 1, 0), 2)),
                  _vec_spec(C, SSM_CONV)],
        out_specs=[_row_spec(tm, C), _vec_spec(C, 8)], sem=("arbitrary",))(dc, dc, pm, pm, conv_w)


def _ssd_prologue(cpre, dtr, dtb, alog):
    L = CHUNK
    xc = _silu(cpre)
    pre = dtr + dtb
    dt = jnp.maximum(pre, 0.0) + jnp.log1p(jnp.exp(-jnp.abs(pre)))
    a = -jnp.exp(alog)
    la = dt * a
    row = lax.broadcasted_iota(jnp.int32, (L, L), 0)
    col = lax.broadcasted_iota(jnp.int32, (L, L), 1)
    causal = row >= col
    tri = causal.astype(F32)
    lc = _nn(tri, la, HIGHEST)
    return xc, pre, dt, a, causal, tri, lc, row, col


def _ssd_fwd(cpre, dtr, pm, dtb, alog, dskip, normw, name):
    T = cpre.shape[0]
    nc = T // CHUNK
    L, P, H, HPG = CHUNK, SSM_HEAD_DIM, SSM_HEADS, SSM_HEADS // SSM_GROUPS

    def body(cp_ref, dtr_ref, z_ref, dtb_ref, alog_ref, dskip_ref, nw_ref, ya_ref, y_ref, sp_ref, s_ref):
        @pl.when(pl.program_id(0) == 0)
        def _():
            s_ref[...] = jnp.zeros_like(s_ref)
        xc, _, dt, _, causal, _, lc, _, _ = _ssd_prologue(cp_ref[...], dtr_ref[...], dtb_ref[...], alog_ref[...])
        lct = lc.T
        sp_ref[0] = s_ref[...]
        for g in range(SSM_GROUPS):
            bm = _bf(xc[:, SSM_INNER + g * SSM_STATE:SSM_INNER + (g + 1) * SSM_STATE])
            cm = _bf(xc[:, SSM_INNER + (SSM_GROUPS + g) * SSM_STATE:SSM_INNER + (SSM_GROUPS + g + 1) * SSM_STATE])
            cb = _nt(cm, bm)
            for j in range(HPG):
                h = g * HPG + j
                xs = xc[:, h * P:(h + 1) * P]
                lcol, lrow = lc[:, h:h + 1], lct[h:h + 1, :]
                llast = lc[L - 1:L, h:h + 1]
                decay = jnp.where(causal, jnp.exp(jnp.where(causal, lcol - lrow, 0.0)), 0.0)
                xd = xs * dt[:, h:h + 1]
                s_prev = s_ref[h]
                y = _nn(_bf(cb * decay), _bf(xd))
                y = y + jnp.exp(lcol) * _nt(cm, _bf(s_prev))
                y = y + dskip_ref[:, h * P:(h + 1) * P] * xs
                st = _tn(_bf(xd * jnp.exp(llast - lcol)), bm)
                s_ref[h] = s_prev * jnp.exp(llast) + st
                y_ref[:, h * P:(h + 1) * P] = y
        y2 = y_ref[...] * _silu(z_ref[...])
        half = SSM_INNER // SSM_GROUPS
        for g in range(SSM_GROUPS):
            yg = y2[:, g * half:(g + 1) * half]
            r = lax.rsqrt(jnp.mean(yg * yg, axis=-1, keepdims=True) + EPS)
            ya_ref[:, g * half:(g + 1) * half] = (yg * r * nw_ref[:, g * half:(g + 1) * half]).astype(BF16)

    return _pcall(
        body, name=name,
        out_shape=[_sds((T, SSM_INNER), BF16), _sds((T, SSM_INNER)), _sds((nc, H, P, SSM_STATE))], grid=(nc,),
        in_specs=[_row_spec(L, CONV_DIM), _row_spec(L, LANES), _row_spec(L, SSM_INNER, 0),
                  _vec_spec(LANES), _vec_spec(LANES), _vec_spec(SSM_INNER), _vec_spec(SSM_INNER)],
        out_specs=[_row_spec(L, SSM_INNER), _row_spec(L, SSM_INNER),
                   pl.BlockSpec((1, H, P, SSM_STATE), lambda i: (i, 0, 0, 0))],
        scratch=[pltpu.VMEM((H, P, SSM_STATE), F32)], sem=("arbitrary",))(cpre, dtr, pm, dtb, alog, dskip, normw)


def _ssd_bwd(cpre, dtr, pm, ypre, sprev, dya, dtb, alog, dskip, normw, name):
    T = cpre.shape[0]
    nc = T // CHUNK
    L, P, H, HPG, N = CHUNK, SSM_HEAD_DIM, SSM_HEADS, SSM_HEADS // SSM_GROUPS, SSM_STATE
    half = SSM_INNER // SSM_GROUPS

    def body(cp_ref, dtr_ref, z_ref, y_ref, sp_ref, dya_ref, dtb_ref, alog_ref, dskip_ref, nw_ref,
             dz_ref, dcp_ref, ddtr_ref, acc_ref, dnw_ref, ds_ref, dy_ref):
        @pl.when(pl.program_id(0) == 0)
        def _():
            ds_ref[...] = jnp.zeros_like(ds_ref)
            acc_ref[...] = jnp.zeros_like(acc_ref)
            dnw_ref[...] = jnp.zeros_like(dnw_ref)
        cpre_v = cp_ref[...]
        xc, pre, dt, a, causal, tri, lc, row, col = _ssd_prologue(cpre_v, dtr_ref[...], dtb_ref[...], alog_ref[...])
        lct = lc.T
        zv, yv = z_ref[...], y_ref[...]
        sz = _silu(zv)
        y2 = yv * sz
        dya_v = dya_ref[...]
        nwv = nw_ref[...]
        for g in range(SSM_GROUPS):
            sl = slice(g * half, (g + 1) * half)
            yg = y2[:, sl]
            r = lax.rsqrt(jnp.mean(yg * yg, axis=-1, keepdims=True) + EPS)
            nrm = yg * r
            dnw_ref[:, sl] += _colsum(dya_v[:, sl] * nrm)
            dn = dya_v[:, sl] * nwv[:, sl]
            dy2 = r * (dn - nrm * jnp.mean(dn * nrm, axis=-1, keepdims=True))
            dy_ref[:, sl] = dy2 * sz[:, sl]
            dz_ref[:, sl] = dy2 * yv[:, sl] * _dsilu(zv[:, sl])
        dlc_c = jnp.zeros((L, LANES), F32)
        dlc_r = jnp.zeros((LANES, L), F32)
        ddt = jnp.zeros((L, LANES), F32)
        dskip_acc = jnp.zeros((1, LANES), F32)
        lane_c = lax.broadcasted_iota(jnp.int32, (L, LANES), 1)
        row_c = lax.broadcasted_iota(jnp.int32, (L, LANES), 0)
        sub_r = lax.broadcasted_iota(jnp.int32, (LANES, L), 0)
        lane1 = lax.broadcasted_iota(jnp.int32, (1, LANES), 1)
        for g in range(SSM_GROUPS):
            b_lo = SSM_INNER + g * N
            c_lo = SSM_INNER + (SSM_GROUPS + g) * N
            bmf, cmf = xc[:, b_lo:b_lo + N], xc[:, c_lo:c_lo + N]
            bm, cm = _bf(bmf), _bf(cmf)
            cb = _nt(cm, bm)
            dcb = jnp.zeros((L, L), F32)
            db = jnp.zeros((L, N), F32)
            dcm = jnp.zeros((L, N), F32)
            for j in range(HPG):
                h = g * HPG + j
                xs = xc[:, h * P:(h + 1) * P]
                dyh = dy_ref[:, h * P:(h + 1) * P]
                lcol, lrow = lc[:, h:h + 1], lct[h:h + 1, :]
                llast = lc[L - 1:L, h:h + 1]
                dtc = dt[:, h:h + 1]
                decay = jnp.where(causal, jnp.exp(jnp.where(causal, lcol - lrow, 0.0)), 0.0)
                e_col = jnp.exp(lcol)
                dte = jnp.exp(llast - lcol)
                cd = jnp.exp(llast)
                xd = xs * dtc
                s_prev = sp_ref[0, h]
                ds_new = ds_ref[h]
                m = cb * decay
                dyb = _bf(dyh)
                dxd = _tn(_bf(m), dyb)
                dm = _nt(dyb, _bf(xd))
                gm = dm * m
                dcb = dcb + dm * decay
                dl_col = _rowsum(gm)
                dl_row = -_colsum(gm)
                y0 = _nt(cm, _bf(s_prev))
                dy0 = _bf(e_col * dyh)
                dl_col = dl_col + _rowsum(dyh * y0) * e_col
                dcm = dcm + _nn(dy0, _bf(s_prev))
                ds_prev = _tn(dy0, cm) + ds_new * cd
                rr = _nt(bm, _bf(ds_new))
                dxd = dxd + dte * rr
                tt = _rowsum(rr * xd) * dte
                dl_col = dl_col - tt
                db = db + _nn(_bf(dte * xd), _bf(ds_new))
                dl_last = _colsum(tt) + _allsum(ds_new * s_prev) * cd
                ds_ref[h] = ds_prev
                dcp_ref[:, h * P:(h + 1) * P] = dxd * dtc + dskip_ref[:, h * P:(h + 1) * P] * dyh
                ddt = ddt + jnp.where(lane_c == h, _rowsum(dxd * xs), 0.0)
                dskip_acc = dskip_acc + jnp.where(lane1 == h, _allsum(dyh * xs), 0.0)
                dlc_c = dlc_c + jnp.where(lane_c == h, dl_col, 0.0)
                dlc_c = dlc_c + jnp.where((lane_c == h) & (row_c == L - 1), dl_last, 0.0)
                dlc_r = dlc_r + jnp.where(sub_r == h, dl_row, 0.0)
            dcbb = _bf(dcb)
            dcp_ref[:, c_lo:c_lo + N] = dcm + _nn(dcbb, bm)
            dcp_ref[:, b_lo:b_lo + N] = db + _tn(dcbb, cm)
        dlc = dlc_c + dlc_r.T
        dla = _tn(tri, dlc, HIGHEST)
        ddt = ddt + dla * a
        head_lane = lane_c < H
        ddtr = jnp.where(head_lane, ddt * _sigmoid(pre), 0.0)
        ddtr_ref[...] = ddtr
        acc_ref[0:1, :] += _colsum(ddtr)
        acc_ref[1:2, :] += jnp.where(lane1 < H, _colsum(dla * dt) * a, 0.0)
        acc_ref[2:3, :] += dskip_acc
        dcp_ref[...] = dcp_ref[...] * _dsilu(cpre_v)

    rev = lambda i: (nc - 1 - i, 0)
    rspec = lambda c: pl.BlockSpec((L, c), rev)
    return _pcall(
        body, name=name,
        out_shape=[_sds((T, SSM_INNER)), _sds((T, CONV_DIM)), _sds((T, LANES)), _sds((8, LANES)), _sds((1, SSM_INNER))],
        grid=(nc,),
        in_specs=[rspec(CONV_DIM), rspec(LANES), rspec(SSM_INNER), rspec(SSM_INNER),
                  pl.BlockSpec((1, H, P, N), lambda i: (nc - 1 - i, 0, 0, 0)), rspec(SSM_INNER),
                  _vec_spec(LANES), _vec_spec(LANES), _vec_spec(SSM_INNER), _vec_spec(SSM_INNER)],
        out_specs=[rspec(SSM_INNER), rspec(CONV_DIM), rspec(LANES), _vec_spec(LANES, 8), _vec_spec(SSM_INNER)],
        scratch=[pltpu.VMEM((H, P, N), F32), pltpu.VMEM((L, SSM_INNER), F32)],
        sem=("arbitrary",))(cpre, dtr, pm, ypre, sprev, dya, dtb, alog, dskip, normw)


def _gmlp_common(u, v, lnw, lnb):
    ug = _gelu(u)
    vg = _gelu(v)
    mu = jnp.mean(vg, axis=-1, keepdims=True)
    cen = vg - mu
    rstd = lax.rsqrt(jnp.mean(cen * cen, axis=-1, keepdims=True) + EPS)
    vhat = cen * rstd
    return ug, rstd, vhat, vhat * lnw + lnb


def _causal_mask():
    row = lax.broadcasted_iota(jnp.int32, (CHUNK, CHUNK), 0)
    col = lax.broadcasted_iota(jnp.int32, (CHUNK, CHUNK), 1)
    return row >= col


def _gmlp_fwd(pm, lnw, lnb, ws, bs_exp, name):
    T = pm.shape[0]
    nc = T // CHUNK
    L, G = CHUNK, GMLP_GROUPS

    def body(u_ref, v_ref, lnw_ref, lnb_ref, ws_ref, bs_ref, o_ref):
        ug, _, _, vn = _gmlp_common(u_ref[...], v_ref[...], lnw_ref[...], lnb_ref[...])
        causal = _causal_mask()
        for g in range(G):
            sl = slice(g * L, (g + 1) * L)
            wm = _bf(jnp.where(causal, ws_ref[g], 0.0))
            sv = _nn(wm, _bf(vn[:, sl])) + bs_ref[:, sl]
            o_ref[:, sl] = (ug[:, sl] * sv).astype(BF16)

    return _pcall(
        body, name=name, out_shape=_sds((T, GMLP_INNER), BF16), grid=(nc,),
        in_specs=[_row_spec(L, GMLP_INNER, 1), _row_spec(L, GMLP_INNER, 2), _vec_spec(GMLP_INNER), _vec_spec(GMLP_INNER),
                  pl.BlockSpec((G, L, L), lambda i: (0, 0, 0)), _vec_spec(GMLP_INNER, L)],
        out_specs=_row_spec(L, GMLP_INNER), sem=("parallel",))(pm, pm, lnw, lnb, ws, bs_exp)


def _gmlp_bwd(pm, dyb, lnw, lnb, ws, bs_exp, name):
    T = pm.shape[0]
    nc = T // CHUNK
    L, G = CHUNK, GMLP_GROUPS

    def body(u_ref, v_ref, dy_ref, lnw_ref, lnb_ref, ws_ref, bs_ref, du_ref, dv_ref, dws_ref, dbs_ref, acc_ref, dvn_ref):
        @pl.when(pl.program_id(0) == 0)
        def _():
            dws_ref[...] = jnp.zeros_like(dws_ref)
            dbs_ref[...] = jnp.zeros_like(dbs_ref)
            acc_ref[...] = jnp.zeros_like(acc_ref)
        uv, vv, dyv, lnwv = u_ref[...], v_ref[...], dy_ref[...], lnw_ref[...]
        ug, rstd, vhat, vn = _gmlp_common(uv, vv, lnwv, lnb_ref[...])
        causal = _causal_mask()
        lane = lax.broadcasted_iota(jnp.int32, (L, LANES), 1)
        dbs = jnp.zeros((L, LANES), F32)
        for g in range(G):
            sl = slice(g * L, (g + 1) * L)
            wm = _bf(jnp.where(causal, ws_ref[g], 0.0))
            vng = _bf(vn[:, sl])
            sv = _nn(wm, vng) + bs_ref[:, sl]
            du_ref[:, sl] = dyv[:, sl] * sv * _dgelu(uv[:, sl])
            dsv = dyv[:, sl] * ug[:, sl]
            dsvb = _bf(dsv)
            dws_ref[g] += jnp.where(causal, _nt(dsvb, vng), 0.0)
            dbs = dbs + jnp.where(lane == g, _rowsum(dsv), 0.0)
            dvn_ref[:, sl] = _tn(wm, dsvb)
        dbs_ref[...] += dbs
        dvn = dvn_ref[...]
        acc_ref[0:1, :] += _colsum(dvn * vhat)
        acc_ref[1:2, :] += _colsum(dvn)
        dvh = dvn * lnwv
        dvg = rstd * (dvh - jnp.mean(dvh, axis=-1, keepdims=True) - vhat * jnp.mean(dvh * vhat, axis=-1, keepdims=True))
        dv_ref[...] = dvg * _dgelu(vv)

    return _pcall(
        body, name=name,
        out_shape=[_sds((T, GMLP_INNER)), _sds((T, GMLP_INNER)), _sds((G, L, L)), _sds((L, LANES)), _sds((8, GMLP_INNER))],
        grid=(nc,),
        in_specs=[_row_spec(L, GMLP_INNER, 1), _row_spec(L, GMLP_INNER, 2), _row_spec(L, GMLP_INNER),
                  _vec_spec(GMLP_INNER), _vec_spec(GMLP_INNER), pl.BlockSpec((G, L, L), lambda i: (0, 0, 0)),
                  _vec_spec(GMLP_INNER, L)],
        out_specs=[_row_spec(L, GMLP_INNER), _row_spec(L, GMLP_INNER), pl.BlockSpec((G, L, L), lambda i: (0, 0, 0)),
                   _vec_spec(LANES, L), _vec_spec(GMLP_INNER, 8)],
        scratch=[pltpu.VMEM((L, GMLP_INNER), F32)], sem=("arbitrary",))(pm, pm, dyb, lnw, lnb, ws, bs_exp)


def _rel_buckets():
    qi = np.arange(CHUNK)[:, None]
    sj = np.arange(2 * CHUNK)[None, :]
    dist = np.maximum(qi + CHUNK - sj, 0)
    max_exact = REL_BUCKETS // 2
    log_ratio = (np.log(np.maximum(dist, 1).astype(np.float32) / np.float32(max_exact))
                 / np.float32(math.log(REL_MAX_DIST / max_exact))).astype(np.float32)
    large = max_exact + (log_ratio * np.float32(REL_BUCKETS - max_exact)).astype(np.int32)
    return np.where(dist < max_exact, dist, np.minimum(large, REL_BUCKETS - 1))


def _bucket_onehot_t():
    bucket = _rel_buckets().reshape(-1)
    return jnp.asarray((np.arange(REL_BUCKETS)[:, None] == bucket[None, :]).astype(np.float32))


def _bias_from_table(table_t, onehot_t, name):
    def body(t_ref, o_ref, out_ref):
        out_ref[...] = _nn(t_ref[...], o_ref[...], HIGHEST)

    return _pcall(body, name=name, out_shape=_sds((ATTN_HEADS, onehot_t.shape[1])))(table_t, onehot_t)


def _table_from_dbias(dbias, onehot_t, name):
    def body(d_ref, o_ref, out_ref):
        out_ref[...] = _nt(d_ref[...], o_ref[...], HIGHEST)

    return _pcall(body, name=name, out_shape=_sds((ATTN_HEADS, REL_BUCKETS)))(dbias, onehot_t)


def _attn_probs(qh, kband, bias_h, sink, mask, scale):
    logits = _nt(qh, kband) * scale + bias_h
    logits = jnp.where(mask, logits, NEG_INF)
    mx = jnp.maximum(jnp.max(logits, axis=-1, keepdims=True), sink)
    e = jnp.exp(logits - mx)
    es = jnp.exp(sink - mx)
    inv = 1.0 / (_rowsum(e) + es)
    return e * inv, es * inv


def _attn_mask(n):
    qi = lax.broadcasted_iota(jnp.int32, (CHUNK, 2 * CHUNK), 0)
    sj = lax.broadcasted_iota(jnp.int32, (CHUNK, 2 * CHUNK), 1)
    rel = qi + CHUNK - sj
    return (rel >= 0) & (rel < CHUNK) & ((sj >= CHUNK) | (n > 0))


def _attn_fwd(qkv, bias, sinks, name):
    T = qkv.shape[0]
    nb = T // CHUNK
    L, DH, HPK = CHUNK, ATTN_DH, ATTN_HEADS // ATTN_KV
    scale = DH ** -0.5
    kcol, vcol = ATTN_HEADS * DH // LANES, ATTN_HEADS * DH // LANES + 1

    def body(q_ref, k_ref, v_ref, kp_ref, vp_ref, bias_ref, sink_ref, o_ref):
        n = pl.program_id(0)
        mask = _attn_mask(n)
        kband = _bf(jnp.concatenate([kp_ref[...], k_ref[...]], axis=0))
        vband = _bf(jnp.concatenate([vp_ref[...], v_ref[...]], axis=0))
        for h in range(ATTN_HEADS):
            kv = h // HPK
            qh = _bf(q_ref[:, h * DH:(h + 1) * DH])
            p, _ = _attn_probs(qh, kband[:, kv * DH:(kv + 1) * DH], bias_ref[h], sink_ref[h], mask, scale)
            o_ref[:, h * DH:(h + 1) * DH] = _nn(_bf(p), vband[:, kv * DH:(kv + 1) * DH]).astype(BF16)

    prev = lambda i: jnp.maximum(i - 1, 0)
    return _pcall(
        body, name=name, out_shape=_sds((T, ATTN_HEADS * DH), BF16), grid=(nb,),
        in_specs=[_row_spec(L, ATTN_HEADS * DH, 0), _row_spec(L, LANES, kcol), _row_spec(L, LANES, vcol),
                  pl.BlockSpec((L, LANES), lambda i: (prev(i), kcol)), pl.BlockSpec((L, LANES), lambda i: (prev(i), vcol)),
                  pl.BlockSpec((ATTN_HEADS, L, 2 * L), lambda i: (0, 0, 0)),
                  pl.BlockSpec(memory_space=pltpu.SMEM)],
        out_specs=_row_spec(L, ATTN_HEADS * DH), sem=("parallel",))(qkv, qkv, qkv, qkv, qkv, bias, sinks)


def _attn_bwd(qkv, datt, bias, sinks, name):
    T = qkv.shape[0]
    nb = T // CHUNK
    L, DH, HPK = CHUNK, ATTN_DH, ATTN_HEADS // ATTN_KV
    scale = DH ** -0.5
    kcol, vcol = ATTN_HEADS * DH // LANES, ATTN_HEADS * DH // LANES + 1

    def body(q_ref, k_ref, v_ref, kp_ref, vp_ref, do_ref, bias_ref, sink_ref,
             dq_ref, dk_ref, dv_ref, dbias_ref, dsink_ref, pend_k, pend_v, band_k, band_v):
        n = pl.program_id(0)

        @pl.when(n == 0)
        def _():
            dbias_ref[...] = jnp.zeros_like(dbias_ref)
            dsink_ref[...] = jnp.zeros_like(dsink_ref)

        @pl.when(n < nb)
        def _():
            mask = _attn_mask(n)
            kband = _bf(jnp.concatenate([kp_ref[...], k_ref[...]], axis=0))
            vband = _bf(jnp.concatenate([vp_ref[...], v_ref[...]], axis=0))
            lane1 = lax.broadcasted_iota(jnp.int32, (1, LANES), 1)
            dsink = jnp.zeros((1, LANES), F32)
            for kv in range(ATTN_KV):
                kb, vb = kband[:, kv * DH:(kv + 1) * DH], vband[:, kv * DH:(kv + 1) * DH]
                dkb = jnp.zeros((2 * L, DH), F32)
                dvb = jnp.zeros((2 * L, DH), F32)
                for j in range(HPK):
                    h = kv * HPK + j
                    qh = _bf(q_ref[:, h * DH:(h + 1) * DH])
                    p, ps = _attn_probs(qh, kb, bias_ref[h], sink_ref[h], mask, scale)
                    doh = _bf(do_ref[:, h * DH:(h + 1) * DH])
                    dp = _nt(doh, vb)
                    dvb = dvb + _tn(_bf(p), doh)
                    delta = _rowsum(p * dp)
                    dl = p * (dp - delta)
                    dsink = dsink + jnp.where(lane1 == h, -_colsum(ps * delta), 0.0)
                    dbias_ref[h] += dl
                    dlb = _bf(dl)
                    dq_ref[:, h * DH:(h + 1) * DH] = _nn(dlb, kb) * scale
                    dkb = dkb + _tn(dlb, qh) * scale
                band_k[:, kv * DH:(kv + 1) * DH] = dkb
                band_v[:, kv * DH:(kv + 1) * DH] = dvb
            dsink_ref[...] += dsink

            @pl.when(n > 0)
            def _():
                dk_ref[...] = pend_k[...] + band_k[0:L, :]
                dv_ref[...] = pend_v[...] + band_v[0:L, :]
            pend_k[...] = band_k[L:2 * L, :]
            pend_v[...] = band_v[L:2 * L, :]

        @pl.when(n == nb)
        def _():
            dk_ref[...] = pend_k[...]
            dv_ref[...] = pend_v[...]

    cur = lambda i: jnp.minimum(i, nb - 1)
    prev = lambda i: jnp.maximum(jnp.minimum(i, nb - 1) - 1, 0)
    lag = lambda i: jnp.maximum(i - 1, 0)
    return _pcall(
        body, name=name,
        out_shape=[_sds((T, ATTN_HEADS * DH)), _sds((T, LANES)), _sds((T, LANES)), _sds((ATTN_HEADS, L, 2 * L)), _sds((1, LANES))],
        grid=(nb + 1,),
        in_specs=[pl.BlockSpec((L, ATTN_HEADS * DH), lambda i: (cur(i), 0)),
                  pl.BlockSpec((L, LANES), lambda i: (cur(i), kcol)), pl.BlockSpec((L, LANES), lambda i: (cur(i), vcol)),
                  pl.BlockSpec((L, LANES), lambda i: (prev(i), kcol)), pl.BlockSpec((L, LANES), lambda i: (prev(i), vcol)),
                  pl.BlockSpec((L, ATTN_HEADS * DH), lambda i: (cur(i), 0)),
                  pl.BlockSpec((ATTN_HEADS, L, 2 * L), lambda i: (0, 0, 0)),
                  pl.BlockSpec(memory_space=pltpu.SMEM)],
        out_specs=[pl.BlockSpec((L, ATTN_HEADS * DH), lambda i: (cur(i), 0)),
                   pl.BlockSpec((L, LANES), lambda i: (lag(i), 0)), pl.BlockSpec((L, LANES), lambda i: (lag(i), 0)),
                   pl.BlockSpec((ATTN_HEADS, L, 2 * L), lambda i: (0, 0, 0)), _vec_spec(LANES)],
        scratch=[pltpu.VMEM((L, LANES), F32), pltpu.VMEM((L, LANES), F32),
                 pltpu.VMEM((2 * L, LANES), F32), pltpu.VMEM((2 * L, LANES), F32)],
        sem=("arbitrary",))(qkv, qkv, qkv, qkv, qkv, datt, bias, sinks)


def _pad_rows(a, mult):
    pad = (-a.shape[-2]) % mult
    if pad == 0:
        return a
    cfg = [(0, 0)] * (a.ndim - 2) + [(0, pad), (0, 0)]
    return jnp.pad(a, cfg)


class _Pack:
    def __init__(self, width, mult, total_mult):
        self.width, self.mult, self.total_mult = width, mult, total_mult
        self.entries = []
        self.rows = 0

    def add(self, key, shape):
        n = int(np.prod(shape))
        rows = -(-n // self.width)
        self.entries.append((key, self.rows, rows, tuple(shape)))
        self.rows += -(-rows // self.mult) * self.mult

    @property
    def total(self):
        return -(-self.rows // self.total_mult) * self.total_mult

    def pack(self, pieces, dtype, lead=()):
        parts = []
        for key, _, rows, shape in self.entries:
            a = pieces[key].astype(dtype).reshape(lead + (-1,))
            n = int(np.prod(shape))
            a = jnp.pad(a, [(0, 0)] * len(lead) + [(0, rows * self.width - n)])
            a = a.reshape(lead + (rows, self.width))
            parts.append(_pad_rows(a, self.mult))
        out = jnp.concatenate(parts, axis=len(lead))
        return _pad_rows(out, self.total_mult)

    def unpack(self, packed, lead=()):
        out = {}
        for key, off, rows, shape in self.entries:
            a = lax.slice_in_dim(packed, off, off + rows, axis=len(lead))
            a = a.reshape(lead + (-1,))
            n = int(np.prod(shape))
            out[key] = lax.slice_in_dim(a, 0, n, axis=len(lead)).reshape(lead + shape)
        return out


def _ffn_fwd(x, mod, norm_w, wg_t, wu_t, wd, tag):
    h = _norm_mod(x, norm_w, mod[4:5], mod[3:4], f"ffn_norm_{tag}")
    gate, up, act = _mm_swiglu(h, wg_t, wu_t, f"ffn_gateup_{tag}")
    x_out, ffn_out = _mm([(act, wd)], "nn", name=f"ffn_down_{tag}", resid=x, gvec=mod[5:6], keep=True)
    return x_out, dict(h=h, gate=gate, up=up, act=act, out=ffn_out)


def _ffn_bwd(dx_out, x_in, saved, mod, norm_w, wg_t, wu_t, wd, tag):
    dffn, acc_g = _gate_bwd(dx_out, saved["out"], mod[5:6], f"ffn_gate_bwd_{tag}")
    dgate, dup = _mm_swiglu_bwd(dffn, wd, saved["gate"], saved["up"], f"ffn_act_bwd_{tag}")
    d_wd = _mm_tn(saved["act"], dffn, name=f"ffn_dwd_{tag}")
    d_wg_t = _mm_tn(dgate, saved["h"], name=f"ffn_dwg_{tag}")
    d_wu_t = _mm_tn(dup, saved["h"], name=f"ffn_dwu_{tag}")
    dh = _mm([(dgate, wg_t), (dup, wu_t)], "nn", name=f"ffn_dh_{tag}")
    dx, acc_n = _norm_mod_bwd(x_in, dh, dx_out, norm_w, mod[4:5], f"ffn_norm_bwd_{tag}")
    return dx, dict(d_wg=d_wg_t, d_wu=d_wu_t, d_wd=d_wd, d_g=acc_g[0], d_sc=acc_n[0], d_sh=acc_n[1], d_nw=acc_n[2])


_BIG = [
    ("out_w", "out_w_even", 0, "row"), ("qkv_w", "qkv_w", 0, "col"), ("o_w", "o_w", 0, "row"),
    ("gate0", "ffn_gate_w", 0, "col"), ("up0", "ffn_up_w", 0, "col"), ("down0", "ffn_down_w", 0, "row"),
    ("gate1", "ffn_gate_w", 1, "col"), ("up1", "ffn_up_w", 1, "col"), ("down1", "ffn_down_w", 1, "row"),
    ("in_w", "in_w_even", 0, "col"),
]


def _to_wire(a, kind):
    return a.T if kind == "col" else a

_REPLICATED = ["ada_b", "norm_mix_w", "norm_ffn_w", "conv_b", "dt_bias", "a_log", "d_skip", "ssm_norm_w", "gmlp_ln_w",
               "gmlp_ln_b", "gmlp_ws", "gmlp_bs", "sinks", "rel_table", "final_norm_w"]
_TINY_SHARDED = ["conv_w", "qkv_b", "o_b"]

_WEIGHTS = ['ada_w', 'ada_b', 'norm_mix_w', 'norm_ffn_w', 'in_w_even', 'conv_w', 'conv_b', 'dt_bias', 'a_log', 'd_skip',
            'ssm_norm_w', 'gmlp_ln_w', 'gmlp_ln_b', 'gmlp_ws', 'gmlp_bs', 'out_w_even', 'qkv_w', 'qkv_b', 'o_w', 'o_b',
            'sinks', 'rel_table', 'ffn_gate_w', 'ffn_up_w', 'ffn_down_w', 'final_norm_w']


def _step(x, c, loss_target, W, M, V):
    T = x.shape[1]
    x0 = x[0]
    target = loss_target[0]
    me = 4 * lax.axis_index("x") + 2 * lax.axis_index("y") + lax.axis_index("c")

    big = _Pack(D, 16, 128)
    wire = _Pack(D, 1, 128)
    for key, name, layer, kind in _BIG:
        big.add(key, W[name][layer].shape)
        wire.add(key, _to_wire(W[name][layer], kind).shape)
    w_wire_local = {key: _to_wire(W[name][layer].astype(BF16), kind) for key, name, layer, kind in _BIG}
    gathered = _all_gather(wire.pack(w_wire_local, BF16), "gather_weights")
    shards = wire.unpack(gathered, lead=(N_DEV,))
    full = {key: shards[key].reshape(-1, D) for key, name, layer, kind in _BIG}

    small_in = _Pack(D, 8, 8)
    small_in.add("c", (1, D))
    small_in.add("conv_w", W["conv_w"][0].shape)
    small_in.add("qkv_b", W["qkv_b"][0].shape)
    small_in.add("o_b", W["o_b"][0].shape)
    sm = small_in.unpack(_all_gather(small_in.pack(
        dict(c=c, conv_w=W["conv_w"][0], qkv_b=W["qkv_b"][0], o_b=W["o_b"][0]), F32), "gather_small"), lead=(N_DEV,))
    c_all = sm["c"].reshape(N_DEV, D)
    conv_w_full = jnp.transpose(sm["conv_w"], (1, 0, 2)).reshape(SSM_CONV, CONV_DIM)
    qkv_b_full = sm["qkv_b"].reshape(1, QKV_DIM)
    o_b_full = sm["o_b"].reshape(1, D)

    ncol = W["ada_w"].shape[2]
    cond, mod_cols = _mod_matmul(c_all, W["ada_w"], "mod_matmul")
    mod_g = _all_gather(mod_cols.reshape(DEPTH * N_DEV, ncol), "gather_mod").reshape(N_DEV, DEPTH, N_DEV, ncol)
    mod_me = lax.dynamic_index_in_dim(mod_g, me, axis=2, keepdims=False)
    mod_me = jnp.transpose(mod_me, (1, 0, 2)).reshape(DEPTH, 6, D)
    mod_me = jnp.pad(mod_me, ((0, 0), (0, 2), (0, 0))).reshape(DEPTH * 8, D)
    ada_b_rows = jnp.pad(W["ada_b"].reshape(DEPTH, 6, D), ((0, 0), (0, 2), (0, 0))).reshape(DEPTH * 8, D)
    mod_all = _add_rows(mod_me, ada_b_rows, "mod_bias").reshape(DEPTH, 8, D)
    mod0, mod1 = mod_all[0], mod_all[1]

    in_t = full["in_w"]
    o1, o2, o3, o4 = SSM_INNER, SSM_INNER + CONV_DIM, SSM_INNER + CONV_DIM + SSM_HEADS, SSM_INNER + CONV_DIM + SSM_HEADS + GMLP_INNER
    w_z, w_xbc, w_dt, w_u, w_v = in_t[:o1], in_t[o1:o2], in_t[o2:o3], in_t[o3:o4], in_t[o4:]
    w_main = jnp.concatenate([w_z, w_u, w_v, w_xbc], axis=0)
    w_dtp = jnp.pad(w_dt, ((0, LANES - SSM_HEADS), (0, 0)))
    out_w = full["out_w"]
    qkv_t, o_w = full["qkv_w"], full["o_w"]
    w_q, w_k, w_v_att = qkv_t[:D], qkv_t[D:D + LANES], qkv_t[D + LANES:]

    pad16 = lambda a: jnp.pad(a.reshape(1, SSM_HEADS), ((0, 0), (0, LANES - SSM_HEADS)))
    dtb, alog = pad16(W["dt_bias"][0]), pad16(W["a_log"][0])
    dskip = jnp.repeat(W["d_skip"][0], SSM_HEAD_DIM).reshape(1, SSM_INNER)
    ssm_nw = W["ssm_norm_w"]
    lnw, lnb = W["gmlp_ln_w"], W["gmlp_ln_b"]
    ws = W["gmlp_ws"][0]
    bs_exp = jnp.repeat(W["gmlp_bs"][0].T, CHUNK, axis=1)
    conv_b = W["conv_b"]
    nmw, nfw = W["norm_mix_w"], W["norm_ffn_w"]
    onehot_t = _bucket_onehot_t()
    bias = _bias_from_table(W["rel_table"].T, onehot_t, "rel_bias").reshape(ATTN_HEADS, CHUNK, 2 * CHUNK)
    sinks = W["sinks"][0]

    h0 = _norm_mod(x0, nmw[0:1], mod0[1:2], mod0[0:1], "mix_norm_0")
    pm = _mm([(h0, w_main)], "nt", name="in_proj", tn_pref=1536)
    dtr = _mm([(h0, w_dtp)], "nt", name="in_proj_dt")
    cpre = _conv_fwd(pm, conv_w_full, conv_b, "conv_fwd")
    ya, ypre, sprev = _ssd_fwd(cpre, dtr, pm, dtb, alog, dskip, ssm_nw, "ssd_fwd")
    yb = _gmlp_fwd(pm, lnw, lnb, ws, bs_exp, "gmlp_fwd")
    x1, mix0 = _mm([(ya, out_w[:SSM_INNER]), (yb, out_w[SSM_INNER:])], "nn", name="out_proj", resid=x0, gvec=mod0[2:3], keep=True)
    x2, ffn0 = _ffn_fwd(x1, mod0, nfw[0:1], full["gate0"], full["up0"], full["down0"], "0")

    h1 = _norm_mod(x2, nmw[1:2], mod1[1:2], mod1[0:1], "mix_norm_1")
    qkv = _mm([(h1, qkv_t)], "nt", name="qkv_proj", bias=qkv_b_full, tn_pref=1280)
    att = _attn_fwd(qkv, bias, sinks, "attn_fwd")
    x3, mix1 = _mm([(att, o_w)], "nn", name="o_proj", bias=o_b_full, resid=x2, gvec=mod1[2:3], keep=True)
    x4, ffn1 = _ffn_fwd(x3, mod1, nfw[1:2], full["gate1"], full["up1"], full["down1"], "1")

    dx4, acc_f = _final_loss(x4, W["final_norm_w"].reshape(1, D), target, "final_loss")
    loss = lax.psum(acc_f[1, 0], ("x", "y", "c"))
    dx3, gf1 = _ffn_bwd(dx4, x3, ffn1, mod1, nfw[1:2], full["gate1"], full["up1"], full["down1"], "1")

    dmix1, acc_m1 = _gate_bwd(dx3, mix1, mod1[2:3], "mix_gate_bwd_1")
    datt = _mm([(dmix1, o_w)], "nt", name="o_proj_dx")
    d_o_w = _mm_tn(att, dmix1, name="o_proj_dw")
    dq, dk, dv, dbias, dsinks = _attn_bwd(qkv, datt, bias, sinks, "attn_bwd")
    d_table = _table_from_dbias(dbias.reshape(ATTN_HEADS, -1), onehot_t, "rel_table_grad").T
    d_qkv_t = jnp.concatenate([_mm_tn(dq, h1, name="qkv_dw_q"), _mm_tn(dk, h1, name="qkv_dw_k"), _mm_tn(dv, h1, name="qkv_dw_v")], axis=0)
    d_qkv_b = jnp.concatenate([_colsum_call(dq, "qkv_db_q"), _colsum_call(dk, "qkv_db_k"), _colsum_call(dv, "qkv_db_v")], axis=1)
    dh1 = _mm([(dq, w_q), (dk, w_k), (dv, w_v_att)], "nn", name="qkv_proj_dx")
    dx2, acc_n1 = _norm_mod_bwd(x2, dh1, dx3, nmw[1:2], mod1[1:2], "mix_norm_bwd_1")

    dx1, gf0 = _ffn_bwd(dx2, x1, ffn0, mod0, nfw[0:1], full["gate0"], full["up0"], full["down0"], "0")

    dmix0, acc_m0 = _gate_bwd(dx1, mix0, mod0[2:3], "mix_gate_bwd_0")
    dya = _mm([(dmix0, out_w[:SSM_INNER])], "nt", name="out_proj_dx_a")
    dyb = _mm([(dmix0, out_w[SSM_INNER:])], "nt", name="out_proj_dx_b")
    d_out_w = jnp.concatenate([_mm_tn(ya, dmix0, name="out_proj_dw_a"), _mm_tn(yb, dmix0, name="out_proj_dw_b")], axis=0)
    du, dvg, d_ws, d_bs, acc_ln = _gmlp_bwd(pm, dyb, lnw, lnb, ws, bs_exp, "gmlp_bwd")
    dz, dcpre, ddtr, acc_ssd, d_ssm_nw = _ssd_bwd(cpre, dtr, pm, ypre, sprev, dya, dtb, alog, dskip, ssm_nw, "ssd_bwd")
    dxbc, acc_conv = _conv_bwd(dcpre, pm, conv_w_full, "conv_bwd")
    d_in_t = jnp.concatenate([
        _mm_tn(dz, h0, name="in_dw_z"), _mm_tn(dxbc, h0, name="in_dw_xbc"),
        _mm_tn(ddtr, h0, name="in_dw_dt")[:SSM_HEADS], _mm_tn(du, h0, name="in_dw_u"), _mm_tn(dvg, h0, name="in_dw_v")], axis=0)
    dh0 = _mm([(dz, w_z), (dxbc, w_xbc), (ddtr, w_dtp), (du, w_u), (dvg, w_v)], "nn", name="in_proj_dx")
    grad_x, acc_n0 = _norm_mod_bwd(x0, dh0, dx1, nmw[0:1], mod0[1:2], "mix_norm_bwd_0")

    g_wire = dict(in_w=d_in_t, out_w=d_out_w, qkv_w=d_qkv_t, o_w=d_o_w,
                  gate0=gf0["d_wg"], up0=gf0["d_wu"], down0=gf0["d_wd"], gate1=gf1["d_wg"], up1=gf1["d_wu"], down1=gf1["d_wd"])
    g_dest = {key: g_wire[key].reshape(N_DEV, -1, D) for key, name, layer, kind in _BIG}
    theirs, mine = _exchange_sibling(wire.pack(g_dest, F32, lead=(N_DEV,)), "exchange_grads_sibling")
    from_chips = _exchange_chips(_pair_sum(mine, theirs, "grads_pair_sum"), "exchange_grads_chips")
    g_mine = wire.unpack(_sum_parts(from_chips, "grads_chip_sum"))
    g_nat = {key: _to_wire(g_mine[key], kind) for key, name, layer, kind in _BIG}
    pk = lambda S: big.pack({key: S[name][layer] for key, name, layer, kind in _BIG}, F32)
    res_big = [big.unpack(r) for r in _adamw(big.pack(g_nat, F32)[None], pk(W), pk(M), pk(V), "adamw_big")]

    d_mod = jnp.stack([
        jnp.stack([acc_n0[1], acc_n0[0], acc_m0[0], gf0["d_sh"], gf0["d_sc"], gf0["d_g"]]),
        jnp.stack([acc_n1[1], acc_n1[0], acc_m1[0], gf1["d_sh"], gf1["d_sc"], gf1["d_g"]])])
    g_small = dict(
        ada_b=d_mod.reshape(DEPTH, 6 * D),
        norm_mix_w=jnp.stack([acc_n0[2], acc_n1[2]]), norm_ffn_w=jnp.stack([gf0["d_nw"], gf1["d_nw"]]),
        conv_b=acc_conv[4:5], dt_bias=acc_ssd[0:1, :SSM_HEADS], a_log=acc_ssd[1:2, :SSM_HEADS], d_skip=acc_ssd[2:3, :SSM_HEADS],
        ssm_norm_w=d_ssm_nw, gmlp_ln_w=acc_ln[0:1], gmlp_ln_b=acc_ln[1:2], gmlp_ws=d_ws[None],
        gmlp_bs=d_bs[:, :GMLP_GROUPS].T[None], sinks=dsinks[:, :ATTN_HEADS], rel_table=d_table, final_norm_w=acc_f[0],
        conv_w=acc_conv[0:SSM_CONV], qkv_b=d_qkv_b, o_b=acc_m1[1:2])
    small = _Pack(D, 8, 8)
    for name in _REPLICATED:
        small.add(name, W[name].shape)
    small.add("conv_w", (SSM_CONV, CONV_DIM))
    small.add("qkv_b", (1, QKV_DIM))
    small.add("o_b", (1, D))
    parts_small = _all_gather(small.pack(g_small, F32), "gather_small_grads")
    zeros_tiny = dict(conv_w=jnp.zeros((SSM_CONV, CONV_DIM), F32), qkv_b=jnp.zeros((1, QKV_DIM), F32), o_b=jnp.zeros((1, D), F32))
    pks = lambda S: small.pack({**{name: S[name] for name in _REPLICATED}, **zeros_tiny}, F32)
    res_small = [small.unpack(r) for r in _adamw(parts_small, pks(W), pks(M), pks(V), "adamw_small")]
    g_small_sum = res_small[0]

    n_cw, n_qb, n_ob = W["conv_w"].shape[2], W["qkv_b"].shape[1], W["o_b"].shape[1]
    g_tiny = dict(conv_w=lax.dynamic_slice_in_dim(g_small_sum["conv_w"], me * n_cw, n_cw, axis=1)[None],
                  qkv_b=lax.dynamic_slice_in_dim(g_small_sum["qkv_b"], me * n_qb, n_qb, axis=1),
                  o_b=lax.dynamic_slice_in_dim(g_small_sum["o_b"], me * n_ob, n_ob, axis=1))
    tiny = _Pack(D, 8, 8)
    for name in _TINY_SHARDED:
        tiny.add(name, W[name].shape)
    pkt = lambda S: tiny.pack({name: S[name] for name in _TINY_SHARDED}, F32)
    res_tiny = [tiny.unpack(r) for r in _adamw(pkt(g_tiny)[None], pkt(W), pkt(M), pkt(V), "adamw_tiny")]

    dmod_all = parts_small[:, small.entries[0][1]:small.entries[0][1] + small.entries[0][2]].reshape(N_DEV, DEPTH, 6 * D)
    dmod_cols = jnp.transpose(lax.dynamic_slice_in_dim(dmod_all, me * ncol, ncol, axis=2), (1, 0, 2))
    g_ada_w = _ada_w_grad(cond, dmod_cols, "ada_w_grad")
    flat = lambda a: a.reshape(DEPTH * D, ncol)
    res_ada = [r.reshape(DEPTH, D, ncol) for r in _adamw(flat(g_ada_w)[None], flat(W["ada_w"]), flat(M["ada_w"]), flat(V["ada_w"]), "adamw_ada_w")]

    def result(kind_idx, name):
        if name == "ada_w":
            return res_ada[kind_idx]
        if name in _REPLICATED:
            return res_small[kind_idx][name]
        if name in _TINY_SHARDED:
            return res_tiny[kind_idx][name]
        pieces = [res_big[kind_idx][key] for key, nm, layer, kind in _BIG if nm == name]
        return jnp.stack(pieces)

    outs = [loss, grad_x[None]]
    for kind_idx in range(4):
        outs += [result(kind_idx, name) for name in _WEIGHTS]
    return tuple(outs)


def kernel(x, c, ada_w, ada_b, norm_mix_w, norm_ffn_w, in_w_even, conv_w, conv_b, dt_bias, a_log, d_skip, ssm_norm_w, gmlp_ln_w, gmlp_ln_b, gmlp_ws, gmlp_bs, out_w_even, qkv_w, qkv_b, o_w, o_b, sinks, rel_table, ffn_gate_w, ffn_up_w, ffn_down_w, final_norm_w, loss_target, m_ada_w, m_ada_b, m_norm_mix_w, m_norm_ffn_w, m_in_w_even, m_conv_w, m_conv_b, m_dt_bias, m_a_log, m_d_skip, m_ssm_norm_w, m_gmlp_ln_w, m_gmlp_ln_b, m_gmlp_ws, m_gmlp_bs, m_out_w_even, m_qkv_w, m_qkv_b, m_o_w, m_o_b, m_sinks, m_rel_table, m_ffn_gate_w, m_ffn_up_w, m_ffn_down_w, m_final_norm_w, v_ada_w, v_ada_b, v_norm_mix_w, v_norm_ffn_w, v_in_w_even, v_conv_w, v_conv_b, v_dt_bias, v_a_log, v_d_skip, v_ssm_norm_w, v_gmlp_ln_w, v_gmlp_ln_b, v_gmlp_ws, v_gmlp_bs, v_out_w_even, v_qkv_w, v_qkv_b, v_o_w, v_o_b, v_sinks, v_rel_table, v_ffn_gate_w, v_ffn_up_w, v_ffn_down_w, v_final_norm_w):
    args = locals()
    W = {n: args[n] for n in _WEIGHTS}
    M = {n: args["m_" + n] for n in _WEIGHTS}
    V = {n: args["v_" + n] for n in _WEIGHTS}
    return _step(x, c, loss_target, W, M, V)
```

```python
import functools
import math

import numpy as np
import jax
import jax.numpy as jnp
from jax import lax
from jax.experimental import pallas as pl
from jax.experimental.pallas import tpu as pltpu

F32 = jnp.float32
BF16 = jnp.bfloat16
HIGHEST = lax.Precision.HIGHEST
MESH = pl.DeviceIdType.MESH

N_DEV = 8
D = 1024
DEPTH = 2
SSM_HEADS = 16
SSM_HEAD_DIM = 64
SSM_INNER = 1024
SSM_GROUPS = 2
SSM_STATE = 128
SSM_CONV = 4
CHUNK = 128
CONV_DIM = SSM_INNER + 2 * SSM_GROUPS * SSM_STATE
GMLP_GROUPS = 8
GMLP_INNER = 1024
IN_EVEN = 4624
ATTN_HEADS = 16
ATTN_KV = 2
ATTN_DH = 64
QKV_DIM = 1280
REL_BUCKETS = 32
REL_MAX_DIST = 128
FFN = 2816
EPS = 1e-6
NEG_INF = -1e30
LANES = 128

ADAM_LR = 0.001
ADAM_B1 = 0.9
ADAM_B2 = 0.999
ADAM_EPS = 1e-08
ADAM_WD = 0.01
ADAM_STEP = 10

VMEM_LIMIT_BYTES = 56 * 1024 * 1024
ROW_TILE = 512


def _pcall(body, *, name, out_shape, grid=(), in_specs=None, out_specs=None, scratch=(), sem=None):
    params = dict(vmem_limit_bytes=VMEM_LIMIT_BYTES)
    if sem is not None:
        params["dimension_semantics"] = sem
    specs = {} if in_specs is None else dict(in_specs=in_specs, out_specs=out_specs)
    return pl.pallas_call(
        body, name=name, out_shape=out_shape, grid=grid, **specs,
        scratch_shapes=list(scratch), compiler_params=pltpu.CompilerParams(**params))


def _tile(n, pref):
    if n <= pref:
        return n
    best = None
    for t in range(LANES, pref + 1, LANES):
        if n % t == 0:
            best = t
    assert best is not None, (n, pref)
    return best


def _rows(T):
    return min(ROW_TILE, T)


def _sds(shape, dtype=F32):
    return jax.ShapeDtypeStruct(shape, dtype)


def _row_spec(tm, c, col=0):
    return pl.BlockSpec((tm, c), lambda i, col=col: (i, col))


def _vec_spec(c, r=1):
    return pl.BlockSpec((r, c), lambda i: (0, 0))


def _sigmoid(x):
    return jax.nn.sigmoid(x)


def _silu(x):
    return x * _sigmoid(x)


def _dsilu(x):
    s = _sigmoid(x)
    return s * (1.0 + x * (1.0 - s))


def _gelu(x):
    return 0.5 * x * (1.0 + lax.erf(x * 0.7071067811865476))


def _dgelu(x):
    return 0.5 * (1.0 + lax.erf(x * 0.7071067811865476)) + x * jnp.exp(-0.5 * x * x) * 0.3989422804014327


def _dot(a, b, dims, precision=None):
    return lax.dot_general(a, b, (dims, ((), ())), precision=precision, preferred_element_type=F32)


def _nn(a, b, precision=None):
    return _dot(a, b, ((1,), (0,)), precision)


def _nt(a, b, precision=None):
    return _dot(a, b, ((1,), (1,)), precision)


def _tn(a, b, precision=None):
    return _dot(a, b, ((0,), (0,)), precision)


def _bf(x):
    return x.astype(BF16)


def _colsum(x):
    return jnp.sum(x, axis=0, keepdims=True)


def _rowsum(x):
    return jnp.sum(x, axis=1, keepdims=True)


def _allsum(x):
    return _colsum(_rowsum(x))


def _all_gather(x, name):
    R, C = x.shape

    def body(x_ref, out_ref, send_sems, recv_sems, local_sem):
        mx, my, mc = lax.axis_index("x"), lax.axis_index("y"), lax.axis_index("c")
        me, sibling = (mx, my, mc), (mx, my, 1 - mc)
        chips = [(1 - mx, my), (mx, 1 - my), (1 - mx, 1 - my)]

        def slot(px, py, pc):
            return out_ref.at[4 * px + 2 * py + pc]

        def copy(k, block, to, src=None):
            return pltpu.make_async_remote_copy(
                src_ref=slot(*block) if src is None else src, dst_ref=slot(*block),
                send_sem=send_sems.at[k], recv_sem=recv_sems.at[k], device_id=to, device_id_type=MESH)

        mine = pltpu.make_async_copy(x_ref, slot(*me), local_sem)
        mine.start()
        first = [copy(0, me, sibling, src=x_ref)]
        first += [copy(1 + j, me, (*chip, mc), src=x_ref) for j, chip in enumerate(chips)]
        for cp in first:
            cp.start()
        passed = [copy(4 + j, (*chip, mc), sibling) for j, chip in enumerate(chips)]
        for j, chip in enumerate(chips):
            copy(1 + j, (*chip, mc), me).wait_recv()
            passed[j].start()
        copy(0, sibling, me).wait_recv()
        for j, chip in enumerate(chips):
            copy(4 + j, (*chip, 1 - mc), me).wait_recv()
        for cp in first + passed:
            cp.wait_send()
        mine.wait()

    return pl.pallas_call(
        body, name=name, out_shape=_sds((N_DEV, R, C), x.dtype),
        in_specs=[pl.BlockSpec(memory_space=pl.ANY)], out_specs=pl.BlockSpec(memory_space=pl.ANY),
        scratch_shapes=[pltpu.SemaphoreType.DMA((7,)), pltpu.SemaphoreType.DMA((7,)), pltpu.SemaphoreType.DMA(())],
    )(x)


N_CHIP = 4


def _exchange_sibling(p, name):
    _, R, C = p.shape

    def body(p_ref, theirs_ref, send_sems, recv_sems):
        mx, my, mc = lax.axis_index("x"), lax.axis_index("y"), lax.axis_index("c")
        copies = []
        for chip in range(N_CHIP):
            cp = pltpu.make_async_remote_copy(
                src_ref=p_ref.at[2 * chip + 1 - mc], dst_ref=theirs_ref.at[chip],
                send_sem=send_sems.at[chip], recv_sem=recv_sems.at[chip],
                device_id=(mx, my, 1 - mc), device_id_type=MESH)
            cp.start()
            copies.append(cp)
        for cp in copies:
            cp.wait()

    return pl.pallas_call(
        body, name=name, out_shape=_sds((N_CHIP, R, C), p.dtype),
        in_specs=[pl.BlockSpec(memory_space=pl.ANY)], out_specs=pl.BlockSpec(memory_space=pl.ANY),
        scratch_shapes=[pltpu.SemaphoreType.DMA((N_CHIP,))] * 2,
    )(p)


def _exchange_chips(q, name):
    _, R, C = q.shape

    def body(q_ref, out_ref, send_sems, recv_sems, local_sem):
        mx, my, mc = lax.axis_index("x"), lax.axis_index("y"), lax.axis_index("c")
        me = 2 * mx + my
        local = pltpu.make_async_copy(q_ref.at[me], out_ref.at[me], local_sem)
        local.start()
        copies = []
        for r in range(1, N_CHIP):
            px = 1 - mx if r & 2 else mx
            py = 1 - my if r & 1 else my
            cp = pltpu.make_async_remote_copy(
                src_ref=q_ref.at[2 * px + py], dst_ref=out_ref.at[me],
                send_sem=send_sems.at[r - 1], recv_sem=recv_sems.at[r - 1],
                device_id=(px, py, mc), device_id_type=MESH)
            cp.start()
            copies.append(cp)
        for cp in copies:
            cp.wait()
        local.wait()

    return pl.pallas_call(
        body, name=name, out_shape=_sds((N_CHIP, R, C), q.dtype),
        in_specs=[pl.BlockSpec(memory_space=pl.ANY)], out_specs=pl.BlockSpec(memory_space=pl.ANY),
        scratch_shapes=[pltpu.SemaphoreType.DMA((3,)), pltpu.SemaphoreType.DMA((3,)), pltpu.SemaphoreType.DMA(())],
    )(q)


def _pair_sum(p, theirs, name):
    n, R, C = theirs.shape
    tr = _tile_rows(R, 256)

    def body(p_ref, t_ref, o_ref):
        mc = lax.axis_index("c")
        o_ref[0] = (p_ref[0, mc] + t_ref[0]).astype(BF16)

    blk = pl.BlockSpec((1, tr, C), lambda s, i: (s, i, 0))
    return _pcall(body, name=name, out_shape=_sds((n, R, C), BF16), grid=(n, R // tr),
                  in_specs=[pl.BlockSpec((1, 2, tr, C), lambda s, i: (s, 0, i, 0)), blk],
                  out_specs=blk, sem=("parallel", "parallel"))(p.reshape(n, 2, R, C), theirs)


def _sum_parts(parts, name):
    P, R, C = parts.shape
    tr = _tile_rows(R, 256)

    def body(p_ref, o_ref):
        g = p_ref[0].astype(F32)
        for k in range(1, P):
            g = g + p_ref[k].astype(F32)
        o_ref[...] = g

    return _pcall(body, name=name, out_shape=_sds((R, C)), grid=(R // tr,),
                  in_specs=[pl.BlockSpec((P, tr, C), lambda i: (0, i, 0))],
                  out_specs=pl.BlockSpec((tr, C), lambda i: (i, 0)), sem=("parallel",))(parts)


def _adamw(parts, w, m, v, name):
    P, R, C = parts.shape
    tr = R if R <= 256 else _tile_rows(R, 256)

    def body(p_ref, w_ref, m_ref, v_ref, g_ref, d_ref, nm_ref, nv_ref):
        g = p_ref[0]
        for k in range(1, P):
            g = g + p_ref[k]
        nm = ADAM_B1 * m_ref[...] + (1.0 - ADAM_B1) * g
        nv = ADAM_B2 * v_ref[...] + (1.0 - ADAM_B2) * (g * g)
        m_hat = nm / (1.0 - ADAM_B1 ** ADAM_STEP)
        v_hat = nv / (1.0 - ADAM_B2 ** ADAM_STEP)
        g_ref[...] = g
        d_ref[...] = -ADAM_LR * (m_hat / (jnp.sqrt(v_hat) + ADAM_EPS) + ADAM_WD * w_ref[...])
        nm_ref[...] = nm
        nv_ref[...] = nv

    blk = pl.BlockSpec((tr, C), lambda i: (i, 0))
    return _pcall(
        body, name=name, out_shape=[_sds((R, C))] * 4, grid=(R // tr,),
        in_specs=[pl.BlockSpec((P, tr, C), lambda i: (0, i, 0)), blk, blk, blk],
        out_specs=[blk] * 4, sem=("parallel",))(parts, w, m, v)


def _tile_rows(n, pref):
    best = None
    for t in range(8, pref + 1, 8):
        if n % t == 0:
            best = t
    assert best is not None, (n, pref)
    return best


def _mm(pairs, mode, *, name, out_dtype=F32, bias=None, resid=None, gvec=None, keep=False, tn_pref=1024):
    M = pairs[0][0].shape[0]
    N = pairs[0][1].shape[1] if mode == "nn" else pairs[0][1].shape[0]
    tm, tn = _rows(M), _tile(N, tn_pref)
    n_pairs = len(pairs)
    has_bias, has_res = bias is not None, resid is not None

    def body(*refs):
        ab = refs[:2 * n_pairs]
        pos = 2 * n_pairs
        b_ref = refs[pos] if has_bias else None
        pos += has_bias
        r_ref, g_ref = (refs[pos], refs[pos + 1]) if has_res else (None, None)
        pos += 2 * has_res
        outs = refs[pos:]
        acc = None
        for p in range(n_pairs):
            a, b = _bf(ab[2 * p][...]), _bf(ab[2 * p + 1][...])
            d = _nn(a, b) if mode == "nn" else _nt(a, b)
            acc = d if acc is None else acc + d
        if has_bias:
            acc = acc + b_ref[...]
        if has_res:
            outs[0][...] = (r_ref[...] + g_ref[...] * acc).astype(outs[0].dtype)
            if keep:
                outs[1][...] = acc
        else:
            outs[0][...] = acc.astype(outs[0].dtype)

    in_specs, args = [], []
    for a, b in pairs:
        K = a.shape[1]
        in_specs.append(pl.BlockSpec((tm, K), lambda i, j: (i, 0)))
        if mode == "nn":
            in_specs.append(pl.BlockSpec((K, tn), lambda i, j: (0, j)))
        else:
            in_specs.append(pl.BlockSpec((tn, K), lambda i, j: (j, 0)))
        args += [a, b]
    if has_bias:
        in_specs.append(pl.BlockSpec((1, tn), lambda i, j: (0, j)))
        args.append(bias)
    if has_res:
        in_specs.append(pl.BlockSpec((tm, tn), lambda i, j: (i, j)))
        in_specs.append(pl.BlockSpec((1, tn), lambda i, j: (0, j)))
        args += [resid, gvec]
    o_spec = pl.BlockSpec((tm, tn), lambda i, j: (i, j))
    n_out = 2 if (has_res and keep) else 1
    out_shape = [_sds((M, N), out_dtype)] + ([_sds((M, N), F32)] if n_out == 2 else [])
    res = _pcall(body, name=name, out_shape=out_shape, grid=(M // tm, N // tn), in_specs=in_specs,
                 out_specs=[o_spec] * n_out, sem=("parallel", "parallel"))(*args)
    return res if n_out == 2 else res[0]


def _mm_tn(a, b, *, name, tm_pref=1408, tn_pref=1536):
    K, M = a.shape
    N = b.shape[1]
    tm, tn = _tile(M, tm_pref), _tile(N, tn_pref)
    tk = K if K <= ROW_TILE else ROW_TILE

    def body(a_ref, b_ref, o_ref):
        @pl.when(pl.program_id(2) == 0)
        def _():
            o_ref[...] = jnp.zeros_like(o_ref)
        o_ref[...] += _tn(_bf(a_ref[...]), _bf(b_ref[...]))

    return _pcall(
        body, name=name, out_shape=_sds((M, N)), grid=(M // tm, N // tn, K // tk),
        in_specs=[pl.BlockSpec((tk, tm), lambda i, j, k: (k, i)), pl.BlockSpec((tk, tn), lambda i, j, k: (k, j))],
        out_specs=pl.BlockSpec((tm, tn), lambda i, j, k: (i, j)),
        sem=("parallel", "parallel", "arbitrary"))(a, b)


def _mm_swiglu(h, wg_t, wu_t, name):
    M, K = h.shape
    N = wg_t.shape[0]
    tm, tn = _rows(M), _tile(N, 1408)

    def body(h_ref, wg_ref, wu_ref, gate_ref, up_ref, act_ref):
        hv = _bf(h_ref[...])
        gate = _nt(hv, wg_ref[...])
        up = _nt(hv, wu_ref[...])
        gate_ref[...] = gate
        up_ref[...] = up
        act_ref[...] = (_silu(gate) * up).astype(BF16)

    w_spec = pl.BlockSpec((tn, K), lambda i, j: (j, 0))
    o_spec = pl.BlockSpec((tm, tn), lambda i, j: (i, j))
    return _pcall(
        body, name=name, out_shape=[_sds((M, N)), _sds((M, N)), _sds((M, N), BF16)], grid=(M // tm, N // tn),
        in_specs=[pl.BlockSpec((tm, K), lambda i, j: (i, 0)), w_spec, w_spec], out_specs=[o_spec] * 3,
        sem=("parallel", "parallel"))(h, wg_t, wu_t)


def _mm_swiglu_bwd(dout, wd, gate, up, name):
    M, K = dout.shape
    N = wd.shape[0]
    tm, tn = _rows(M), _tile(N, 1408)

    def body(d_ref, wd_ref, gate_ref, up_ref, dg_ref, du_ref):
        dact = _nt(_bf(d_ref[...]), wd_ref[...])
        g = gate_ref[...]
        dg_ref[...] = (dact * up_ref[...] * _dsilu(g)).astype(BF16)
        du_ref[...] = (dact * _silu(g)).astype(BF16)

    t_spec = pl.BlockSpec((tm, tn), lambda i, j: (i, j))
    return _pcall(
        body, name=name, out_shape=[_sds((M, N), BF16)] * 2, grid=(M // tm, N // tn),
        in_specs=[pl.BlockSpec((tm, K), lambda i, j: (i, 0)), pl.BlockSpec((tn, K), lambda i, j: (j, 0)), t_spec, t_spec],
        out_specs=[t_spec] * 2, sem=("parallel", "parallel"))(dout, wd, gate, up)


def _norm_mod(x, w, sc, sh, name):
    T = x.shape[0]
    tm = _rows(T)

    def body(x_ref, w_ref, sc_ref, sh_ref, o_ref):
        xv = x_ref[...]
        r = lax.rsqrt(jnp.mean(xv * xv, axis=-1, keepdims=True) + EPS)
        o_ref[...] = ((xv * r * w_ref[...]) * (1.0 + sc_ref[...]) + sh_ref[...]).astype(BF16)

    return _pcall(body, name=name, out_shape=_sds((T, D), BF16), grid=(T // tm,),
                  in_specs=[_row_spec(tm, D), _vec_spec(D), _vec_spec(D), _vec_spec(D)],
                  out_specs=_row_spec(tm, D), sem=("parallel",))(x, w, sc, sh)


def _norm_mod_bwd(x, dh, dres, w, sc, name):
    T = x.shape[0]
    tm = _rows(T)

    def body(x_ref, dh_ref, dres_ref, w_ref, sc_ref, dx_ref, acc_ref):
        @pl.when(pl.program_id(0) == 0)
        def _():
            acc_ref[...] = jnp.zeros_like(acc_ref)
        xv, dh_v, wv = x_ref[...], dh_ref[...], w_ref[...]
        r = lax.rsqrt(jnp.mean(xv * xv, axis=-1, keepdims=True) + EPS)
        n = xv * r
        dnw = dh_v * (1.0 + sc_ref[...])
        dn = dnw * wv
        dx_ref[...] = dres_ref[...] + r * (dn - n * jnp.mean(dn * n, axis=-1, keepdims=True))
        acc_ref[0:1, :] += _colsum(dh_v * (n * wv))
        acc_ref[1:2, :] += _colsum(dh_v)
        acc_ref[2:3, :] += _colsum(dnw * n)

    return _pcall(body, name=name, out_shape=[_sds((T, D)), _sds((8, D))], grid=(T // tm,),
                  in_specs=[_row_spec(tm, D), _row_spec(tm, D), _row_spec(tm, D), _vec_spec(D), _vec_spec(D)],
                  out_specs=[_row_spec(tm, D), _vec_spec(D, 8)], sem=("arbitrary",))(x, dh, dres, w, sc)


def _gate_bwd(dx, branch, g, name):
    T = dx.shape[0]
    tm = _rows(T)

    def body(dx_ref, br_ref, g_ref, db_ref, acc_ref):
        @pl.when(pl.program_id(0) == 0)
        def _():
            acc_ref[...] = jnp.zeros_like(acc_ref)
        dxv = dx_ref[...]
        db = g_ref[...] * dxv
        db_ref[...] = db.astype(BF16)
        acc_ref[0:1, :] += _colsum(dxv * br_ref[...])
        acc_ref[1:2, :] += _colsum(db)

    return _pcall(body, name=name, out_shape=[_sds((T, D), BF16), _sds((8, D))], grid=(T // tm,),
                  in_specs=[_row_spec(tm, D), _row_spec(tm, D), _vec_spec(D)],
                  out_specs=[_row_spec(tm, D), _vec_spec(D, 8)], sem=("arbitrary",))(dx, branch, g)


def _final_loss(x, wf, target, name):
    T = x.shape[0]
    tm = _rows(T)

    def body(x_ref, w_ref, t_ref, dx_ref, acc_ref):
        @pl.when(pl.program_id(0) == 0)
        def _():
            acc_ref[...] = jnp.zeros_like(acc_ref)
        xv, wv = x_ref[...], w_ref[...]
        r = lax.rsqrt(jnp.mean(xv * xv, axis=-1, keepdims=True) + EPS)
        n = xv * r
        err = n * wv - t_ref[...]
        dy = err * (1.0 / D)
        dn = dy * wv
        dx_ref[...] = r * (dn - n * jnp.mean(dn * n, axis=-1, keepdims=True))
        acc_ref[0:1, :] += _colsum(dy * n)
        acc_ref[1:2, :] += jnp.broadcast_to(_allsum(err * err) * (0.5 / D), (1, D))

    return _pcall(body, name=name, out_shape=[_sds((T, D)), _sds((8, D))], grid=(T // tm,),
                  in_specs=[_row_spec(tm, D), _vec_spec(D), _row_spec(tm, D)],
                  out_specs=[_row_spec(tm, D), _vec_spec(D, 8)], sem=("arbitrary",))(x, wf, target)


def _colsum_call(x, name):
    T, C = x.shape
    tm = _rows(T)

    def body(x_ref, o_ref):
        @pl.when(pl.program_id(0) == 0)
        def _():
            o_ref[...] = jnp.zeros_like(o_ref)
        o_ref[...] += _colsum(x_ref[...].astype(F32))

    return _pcall(body, name=name, out_shape=_sds((1, C)), grid=(T // tm,), in_specs=[_row_spec(tm, C)],
                  out_specs=_vec_spec(C), sem=("arbitrary",))(x)


def _mod_matmul(c_all, ada_w, name):
    n = ada_w.shape[2]

    def body(c_ref, w_ref, cond_ref, o_ref):
        cond = _silu(c_ref[...])
        cond_ref[...] = cond
        o_ref[0] = _nn(cond, w_ref[0])

    return _pcall(body, name=name, out_shape=[_sds((N_DEV, D)), _sds((DEPTH, N_DEV, n))], grid=(DEPTH,),
                  in_specs=[pl.BlockSpec((N_DEV, D), lambda l: (0, 0)), pl.BlockSpec((1, D, n), lambda l: (l, 0, 0))],
                  out_specs=[pl.BlockSpec((N_DEV, D), lambda l: (0, 0)), pl.BlockSpec((1, N_DEV, n), lambda l: (l, 0, 0))],
                  sem=("arbitrary",))(c_all, ada_w)


def _add_rows(a, b, name):
    def body(a_ref, b_ref, o_ref):
        o_ref[...] = a_ref[...] + b_ref[...]

    return _pcall(body, name=name, out_shape=_sds(a.shape))(a, b)


def _ada_w_grad(cond, dmod_cols, name):
    n = dmod_cols.shape[2]

    def body(c_ref, d_ref, o_ref):
        o_ref[0] = _tn(c_ref[...], d_ref[0])

    return _pcall(body, name=name, out_shape=_sds((DEPTH, D, n)), grid=(DEPTH,),
                  in_specs=[pl.BlockSpec((N_DEV, D), lambda l: (0, 0)), pl.BlockSpec((1, N_DEV, n), lambda l: (l, 0, 0))],
                  out_specs=pl.BlockSpec((1, D, n), lambda l: (l, 0, 0)), sem=("parallel",))(cond, dmod_cols)


def _conv_fwd(pm, conv_w, conv_b, name):
    T = pm.shape[0]
    tm = _rows(T)
    C = CONV_DIM

    def body(x_ref, prev_ref, w_ref, b_ref, o_ref):
        cur = x_ref[...]
        prev = jnp.where(pl.program_id(0) > 0, prev_ref[...], 0.0)
        cur8 = cur[0:8]
        row8 = lax.broadcasted_iota(jnp.int32, (8, C), 0)
        full = w_ref[3:4, :] * cur
        head = w_ref[3:4, :] * cur8
        for k in range(1, SSM_CONV):
            wk = w_ref[3 - k:4 - k, :]
            full = full + wk * pltpu.roll(cur, k, 0)
            head = head + wk * jnp.where(row8 < k, pltpu.roll(prev, k, 0), pltpu.roll(cur8, k, 0))
        o_ref[...] = full + b_ref[...]
        o_ref[0:8, :] = head + b_ref[...]

    return _pcall(
        body, name=name, out_shape=_sds((T, C)), grid=(T // tm,),
        in_specs=[pl.BlockSpec((tm, C), lambda i: (i, 2)),
                  pl.BlockSpec((8, C), lambda i: (jnp.maximum(i * (tm // 8) - 1, 0), 2)),
                  _vec_spec(C, SSM_CONV), _vec_spec(C)],
        out_specs=_row_spec(tm, C), sem=("parallel",))(pm, pm, conv_w, conv_b)


def _conv_bwd(dc, pm, conv_w, name):
    T = dc.shape[0]
    tm = _rows(T)
    C = CONV_DIM
    nt = T // tm

    def body(dc_ref, nxt_ref, x_ref, prev_ref, w_ref, dx_ref, acc_ref):
        i = pl.program_id(0)

        @pl.when(i == 0)
        def _():
            acc_ref[...] = jnp.zeros_like(acc_ref)
        dcv = dc_ref[...]
        nxt = jnp.where(i < nt - 1, nxt_ref[...], 0.0)
        xc = x_ref[...]
        prev = jnp.where(i > 0, prev_ref[...], 0.0)
        dc8h, dc8t, x8 = dcv[0:8], dcv[tm - 8:tm], xc[0:8]
        row8 = lax.broadcasted_iota(jnp.int32, (8, C), 0)
        full = w_ref[3:4, :] * dcv
        tail = w_ref[3:4, :] * dc8t
        acc_ref[3:4, :] += _colsum(dcv * xc)
        for k in range(1, SSM_CONV):
            wk = w_ref[3 - k:4 - k, :]
            full = full + wk * pltpu.roll(dcv, tm - k, 0)
            tail = tail + wk * jnp.where(row8 + k >= 8, pltpu.roll(nxt, 8 - k, 0), pltpu.roll(dc8t, 8 - k, 0))
            xs_head = jnp.where(row8 < k, pltpu.roll(prev, k, 0), pltpu.roll(x8, k, 0))
            prod = dcv * pltpu.roll(xc, k, 0)
            acc_ref[3 - k:4 - k, :] += _colsum(prod) - _colsum(prod[0:8]) + _colsum(dc8h * xs_head)
        acc_ref[4:5, :] += _colsum(dcv)
        dx_ref[...] = full
        dx_ref[tm - 8:tm, :] = tail

    return _pcall(
        body, name=name, out_shape=[_sds((T, C)), _sds((8, C))], grid=(nt,),
        in_specs=[_row_spec(tm, C),
                  pl.BlockSpec((8, C), lambda i: (jnp.minimum((i + 1) * (tm // 8), T // 8 - 1), 0)),
                  pl.BlockSpec((tm, C), lambda i: (i, 2)),
                  pl.BlockSpec((8, C), lambda i: (jnp.maximum(i * (tm // 8) - 1, 0), 2)),
                  _vec_spec(C, SSM_CONV)],
        out_specs=[_row_spec(tm, C), _vec_spec(C, 8)], sem=("arbitrary",))(dc, dc, pm, pm, conv_w)


def _ssd_prologue(cpre, dtr, dtb, alog):
    L = CHUNK
    xc = _silu(cpre)
    pre = dtr + dtb
    dt = jnp.maximum(pre, 0.0) + jnp.log1p(jnp.exp(-jnp.abs(pre)))
    a = -jnp.exp(alog)
    la = dt * a
    row = lax.broadcasted_iota(jnp.int32, (L, L), 0)
    col = lax.broadcasted_iota(jnp.int32, (L, L), 1)
    causal = row >= col
    tri = causal.astype(F32)
    lc = _nn(tri, la, HIGHEST)
    return xc, pre, dt, a, causal, tri, lc, row, col


def _ssd_fwd(cpre, dtr, pm, dtb, alog, dskip, normw, name):
    T = cpre.shape[0]
    nc = T // CHUNK
    L, P, H, HPG = CHUNK, SSM_HEAD_DIM, SSM_HEADS, SSM_HEADS // SSM_GROUPS

    def body(cp_ref, dtr_ref, z_ref, dtb_ref, alog_ref, dskip_ref, nw_ref, ya_ref, y_ref, sp_ref, s_ref):
        @pl.when(pl.program_id(0) == 0)
        def _():
            s_ref[...] = jnp.zeros_like(s_ref)
        xc, _, dt, _, causal, _, lc, _, _ = _ssd_prologue(cp_ref[...], dtr_ref[...], dtb_ref[...], alog_ref[...])
        lct = lc.T
        sp_ref[0] = s_ref[...]
        for g in range(SSM_GROUPS):
            bm = _bf(xc[:, SSM_INNER + g * SSM_STATE:SSM_INNER + (g + 1) * SSM_STATE])
            cm = _bf(xc[:, SSM_INNER + (SSM_GROUPS + g) * SSM_STATE:SSM_INNER + (SSM_GROUPS + g + 1) * SSM_STATE])
            cb = _nt(cm, bm)
            for j in range(HPG):
                h = g * HPG + j
                xs = xc[:, h * P:(h + 1) * P]
                lcol, lrow = lc[:, h:h + 1], lct[h:h + 1, :]
                llast = lc[L - 1:L, h:h + 1]
                decay = jnp.where(causal, jnp.exp(jnp.where(causal, lcol - lrow, 0.0)), 0.0)
                xd = xs * dt[:, h:h + 1]
                s_prev = s_ref[h]
                y = _nn(_bf(cb * decay), _bf(xd))
                y = y + jnp.exp(lcol) * _nt(cm, _bf(s_prev))
                y = y + dskip_ref[:, h * P:(h + 1) * P] * xs
                st = _tn(_bf(xd * jnp.exp(llast - lcol)), bm)
                s_ref[h] = s_prev * jnp.exp(llast) + st
                y_ref[:, h * P:(h + 1) * P] = y
        y2 = y_ref[...] * _silu(z_ref[...])
        half = SSM_INNER // SSM_GROUPS
        for g in range(SSM_GROUPS):
            yg = y2[:, g * half:(g + 1) * half]
            r = lax.rsqrt(jnp.mean(yg * yg, axis=-1, keepdims=True) + EPS)
            ya_ref[:, g * half:(g + 1) * half] = (yg * r * nw_ref[:, g * half:(g + 1) * half]).astype(BF16)

    return _pcall(
        body, name=name,
        out_shape=[_sds((T, SSM_INNER), BF16), _sds((T, SSM_INNER)), _sds((nc, H, P, SSM_STATE))], grid=(nc,),
        in_specs=[_row_spec(L, CONV_DIM), _row_spec(L, LANES), _row_spec(L, SSM_INNER, 0),
                  _vec_spec(LANES), _vec_spec(LANES), _vec_spec(SSM_INNER), _vec_spec(SSM_INNER)],
        out_specs=[_row_spec(L, SSM_INNER), _row_spec(L, SSM_INNER),
                   pl.BlockSpec((1, H, P, SSM_STATE), lambda i: (i, 0, 0, 0))],
        scratch=[pltpu.VMEM((H, P, SSM_STATE), F32)], sem=("arbitrary",))(cpre, dtr, pm, dtb, alog, dskip, normw)


def _ssd_bwd(cpre, dtr, pm, ypre, sprev, dya, dtb, alog, dskip, normw, name):
    T = cpre.shape[0]
    nc = T // CHUNK
    L, P, H, HPG, N = CHUNK, SSM_HEAD_DIM, SSM_HEADS, SSM_HEADS // SSM_GROUPS, SSM_STATE
    half = SSM_INNER // SSM_GROUPS

    def body(cp_ref, dtr_ref, z_ref, y_ref, sp_ref, dya_ref, dtb_ref, alog_ref, dskip_ref, nw_ref,
             dz_ref, dcp_ref, ddtr_ref, acc_ref, dnw_ref, ds_ref, dy_ref):
        @pl.when(pl.program_id(0) == 0)
        def _():
            ds_ref[...] = jnp.zeros_like(ds_ref)
            acc_ref[...] = jnp.zeros_like(acc_ref)
            dnw_ref[...] = jnp.zeros_like(dnw_ref)
        cpre_v = cp_ref[...]
        xc, pre, dt, a, causal, tri, lc, row, col = _ssd_prologue(cpre_v, dtr_ref[...], dtb_ref[...], alog_ref[...])
        lct = lc.T
        zv, yv = z_ref[...], y_ref[...]
        sz = _silu(zv)
        y2 = yv * sz
        dya_v = dya_ref[...]
        nwv = nw_ref[...]
        for g in range(SSM_GROUPS):
            sl = slice(g * half, (g + 1) * half)
            yg = y2[:, sl]
            r = lax.rsqrt(jnp.mean(yg * yg, axis=-1, keepdims=True) + EPS)
            nrm = yg * r
            dnw_ref[:, sl] += _colsum(dya_v[:, sl] * nrm)
            dn = dya_v[:, sl] * nwv[:, sl]
            dy2 = r * (dn - nrm * jnp.mean(dn * nrm, axis=-1, keepdims=True))
            dy_ref[:, sl] = dy2 * sz[:, sl]
            dz_ref[:, sl] = dy2 * yv[:, sl] * _dsilu(zv[:, sl])
        dlc_c = jnp.zeros((L, LANES), F32)
        dlc_r = jnp.zeros((LANES, L), F32)
        ddt = jnp.zeros((L, LANES), F32)
        dskip_acc = jnp.zeros((1, LANES), F32)
        lane_c = lax.broadcasted_iota(jnp.int32, (L, LANES), 1)
        row_c = lax.broadcasted_iota(jnp.int32, (L, LANES), 0)
        sub_r = lax.broadcasted_iota(jnp.int32, (LANES, L), 0)
        lane1 = lax.broadcasted_iota(jnp.int32, (1, LANES), 1)
        for g in range(SSM_GROUPS):
            b_lo = SSM_INNER + g * N
            c_lo = SSM_INNER + (SSM_GROUPS + g) * N
            bmf, cmf = xc[:, b_lo:b_lo + N], xc[:, c_lo:c_lo + N]
            bm, cm = _bf(bmf), _bf(cmf)
            cb = _nt(cm, bm)
            dcb = jnp.zeros((L, L), F32)
            db = jnp.zeros((L, N), F32)
            dcm = jnp.zeros((L, N), F32)
            for j in range(HPG):
                h = g * HPG + j
                xs = xc[:, h * P:(h + 1) * P]
                dyh = dy_ref[:, h * P:(h + 1) * P]
                lcol, lrow = lc[:, h:h + 1], lct[h:h + 1, :]
                llast = lc[L - 1:L, h:h + 1]
                dtc = dt[:, h:h + 1]
                decay = jnp.where(causal, jnp.exp(jnp.where(causal, lcol - lrow, 0.0)), 0.0)
                e_col = jnp.exp(lcol)
                dte = jnp.exp(llast - lcol)
                cd = jnp.exp(llast)
                xd = xs * dtc
                s_prev = sp_ref[0, h]
                ds_new = ds_ref[h]
                m = cb * decay
                dyb = _bf(dyh)
                dxd = _tn(_bf(m), dyb)
                dm = _nt(dyb, _bf(xd))
                gm = dm * m
                dcb = dcb + dm * decay
                dl_col = _rowsum(gm)
                dl_row = -_colsum(gm)
                y0 = _nt(cm, _bf(s_prev))
                dy0 = _bf(e_col * dyh)
                dl_col = dl_col + _rowsum(dyh * y0) * e_col
                dcm = dcm + _nn(dy0, _bf(s_prev))
                ds_prev = _tn(dy0, cm) + ds_new * cd
                rr = _nt(bm, _bf(ds_new))
                dxd = dxd + dte * rr
                tt = _rowsum(rr * xd) * dte
                dl_col = dl_col - tt
                db = db + _nn(_bf(dte * xd), _bf(ds_new))
                dl_last = _colsum(tt) + _allsum(ds_new * s_prev) * cd
                ds_ref[h] = ds_prev
                dcp_ref[:, h * P:(h + 1) * P] = dxd * dtc + dskip_ref[:, h * P:(h + 1) * P] * dyh
                ddt = ddt + jnp.where(lane_c == h, _rowsum(dxd * xs), 0.0)
                dskip_acc = dskip_acc + jnp.where(lane1 == h, _allsum(dyh * xs), 0.0)
                dlc_c = dlc_c + jnp.where(lane_c == h, dl_col, 0.0)
                dlc_c = dlc_c + jnp.where((lane_c == h) & (row_c == L - 1), dl_last, 0.0)
                dlc_r = dlc_r + jnp.where(sub_r == h, dl_row, 0.0)
            dcbb = _bf(dcb)
            dcp_ref[:, c_lo:c_lo + N] = dcm + _nn(dcbb, bm)
            dcp_ref[:, b_lo:b_lo + N] = db + _tn(dcbb, cm)
        dlc = dlc_c + dlc_r.T
        dla = _tn(tri, dlc, HIGHEST)
        ddt = ddt + dla * a
        head_lane = lane_c < H
        ddtr = jnp.where(head_lane, ddt * _sigmoid(pre), 0.0)
        ddtr_ref[...] = ddtr
        acc_ref[0:1, :] += _colsum(ddtr)
        acc_ref[1:2, :] += jnp.where(lane1 < H, _colsum(dla * dt) * a, 0.0)
        acc_ref[2:3, :] += dskip_acc
        dcp_ref[...] = dcp_ref[...] * _dsilu(cpre_v)

    rev = lambda i: (nc - 1 - i, 0)
    rspec = lambda c: pl.BlockSpec((L, c), rev)
    return _pcall(
        body, name=name,
        out_shape=[_sds((T, SSM_INNER)), _sds((T, CONV_DIM)), _sds((T, LANES)), _sds((8, LANES)), _sds((1, SSM_INNER))],
        grid=(nc,),
        in_specs=[rspec(CONV_DIM), rspec(LANES), rspec(SSM_INNER), rspec(SSM_INNER),
                  pl.BlockSpec((1, H, P, N), lambda i: (nc - 1 - i, 0, 0, 0)), rspec(SSM_INNER),
                  _vec_spec(LANES), _vec_spec(LANES), _vec_spec(SSM_INNER), _vec_spec(SSM_INNER)],
        out_specs=[rspec(SSM_INNER), rspec(CONV_DIM), rspec(LANES), _vec_spec(LANES, 8), _vec_spec(SSM_INNER)],
        scratch=[pltpu.VMEM((H, P, N), F32), pltpu.VMEM((L, SSM_INNER), F32)],
        sem=("arbitrary",))(cpre, dtr, pm, ypre, sprev, dya, dtb, alog, dskip, normw)


def _gmlp_common(u, v, lnw, lnb):
    ug = _gelu(u)
    vg = _gelu(v)
    mu = jnp.mean(vg, axis=-1, keepdims=True)
    cen = vg - mu
    rstd = lax.rsqrt(jnp.mean(cen * cen, axis=-1, keepdims=True) + EPS)
    vhat = cen * rstd
    return ug, rstd, vhat, vhat * lnw + lnb


def _causal_mask():
    row = lax.broadcasted_iota(jnp.int32, (CHUNK, CHUNK), 0)
    col = lax.broadcasted_iota(jnp.int32, (CHUNK, CHUNK), 1)
    return row >= col


def _gmlp_fwd(pm, lnw, lnb, ws, bs_exp, name):
    T = pm.shape[0]
    nc = T // CHUNK
    L, G = CHUNK, GMLP_GROUPS

    def body(u_ref, v_ref, lnw_ref, lnb_ref, ws_ref, bs_ref, o_ref):
        ug, _, _, vn = _gmlp_common(u_ref[...], v_ref[...], lnw_ref[...], lnb_ref[...])
        causal = _causal_mask()
        for g in range(G):
            sl = slice(g * L, (g + 1) * L)
            wm = _bf(jnp.where(causal, ws_ref[g], 0.0))
            sv = _nn(wm, _bf(vn[:, sl])) + bs_ref[:, sl]
            o_ref[:, sl] = (ug[:, sl] * sv).astype(BF16)

    return _pcall(
        body, name=name, out_shape=_sds((T, GMLP_INNER), BF16), grid=(nc,),
        in_specs=[_row_spec(L, GMLP_INNER, 1), _row_spec(L, GMLP_INNER, 2), _vec_spec(GMLP_INNER), _vec_spec(GMLP_INNER),
                  pl.BlockSpec((G, L, L), lambda i: (0, 0, 0)), _vec_spec(GMLP_INNER, L)],
        out_specs=_row_spec(L, GMLP_INNER), sem=("parallel",))(pm, pm, lnw, lnb, ws, bs_exp)


def _gmlp_bwd(pm, dyb, lnw, lnb, ws, bs_exp, name):
    T = pm.shape[0]
    nc = T // CHUNK
    L, G = CHUNK, GMLP_GROUPS

    def body(u_ref, v_ref, dy_ref, lnw_ref, lnb_ref, ws_ref, bs_ref, du_ref, dv_ref, dws_ref, dbs_ref, acc_ref, dvn_ref):
        @pl.when(pl.program_id(0) == 0)
        def _():
            dws_ref[...] = jnp.zeros_like(dws_ref)
            dbs_ref[...] = jnp.zeros_like(dbs_ref)
            acc_ref[...] = jnp.zeros_like(acc_ref)
        uv, vv, dyv, lnwv = u_ref[...], v_ref[...], dy_ref[...], lnw_ref[...]
        ug, rstd, vhat, vn = _gmlp_common(uv, vv, lnwv, lnb_ref[...])
        causal = _causal_mask()
        lane = lax.broadcasted_iota(jnp.int32, (L, LANES), 1)
        dbs = jnp.zeros((L, LANES), F32)
        for g in range(G):
            sl = slice(g * L, (g + 1) * L)
            wm = _bf(jnp.where(causal, ws_ref[g], 0.0))
            vng = _bf(vn[:, sl])
            sv = _nn(wm, vng) + bs_ref[:, sl]
            du_ref[:, sl] = dyv[:, sl] * sv * _dgelu(uv[:, sl])
            dsv = dyv[:, sl] * ug[:, sl]
            dsvb = _bf(dsv)
            dws_ref[g] += jnp.where(causal, _nt(dsvb, vng), 0.0)
            dbs = dbs + jnp.where(lane == g, _rowsum(dsv), 0.0)
            dvn_ref[:, sl] = _tn(wm, dsvb)
        dbs_ref[...] += dbs
        dvn = dvn_ref[...]
        acc_ref[0:1, :] += _colsum(dvn * vhat)
        acc_ref[1:2, :] += _colsum(dvn)
        dvh = dvn * lnwv
        dvg = rstd * (dvh - jnp.mean(dvh, axis=-1, keepdims=True) - vhat * jnp.mean(dvh * vhat, axis=-1, keepdims=True))
        dv_ref[...] = dvg * _dgelu(vv)

    return _pcall(
        body, name=name,
        out_shape=[_sds((T, GMLP_INNER)), _sds((T, GMLP_INNER)), _sds((G, L, L)), _sds((L, LANES)), _sds((8, GMLP_INNER))],
        grid=(nc,),
        in_specs=[_row_spec(L, GMLP_INNER, 1), _row_spec(L, GMLP_INNER, 2), _row_spec(L, GMLP_INNER),
                  _vec_spec(GMLP_INNER), _vec_spec(GMLP_INNER), pl.BlockSpec((G, L, L), lambda i: (0, 0, 0)),
                  _vec_spec(GMLP_INNER, L)],
        out_specs=[_row_spec(L, GMLP_INNER), _row_spec(L, GMLP_INNER), pl.BlockSpec((G, L, L), lambda i: (0, 0, 0)),
                   _vec_spec(LANES, L), _vec_spec(GMLP_INNER, 8)],
        scratch=[pltpu.VMEM((L, GMLP_INNER), F32)], sem=("arbitrary",))(pm, pm, dyb, lnw, lnb, ws, bs_exp)


def _rel_buckets():
    qi = np.arange(CHUNK)[:, None]
    sj = np.arange(2 * CHUNK)[None, :]
    dist = np.maximum(qi + CHUNK - sj, 0)
    max_exact = REL_BUCKETS // 2
    log_ratio = (np.log(np.maximum(dist, 1).astype(np.float32) / np.float32(max_exact))
                 / np.float32(math.log(REL_MAX_DIST / max_exact))).astype(np.float32)
    large = max_exact + (log_ratio * np.float32(REL_BUCKETS - max_exact)).astype(np.int32)
    return np.where(dist < max_exact, dist, np.minimum(large, REL_BUCKETS - 1))


def _bucket_onehot_t():
    bucket = _rel_buckets().reshape(-1)
    return jnp.asarray((np.arange(REL_BUCKETS)[:, None] == bucket[None, :]).astype(np.float32))


def _bias_from_table(table_t, onehot_t, name):
    def body(t_ref, o_ref, out_ref):
        out_ref[...] = _nn(t_ref[...], o_ref[...], HIGHEST)

    return _pcall(body, name=name, out_shape=_sds((ATTN_HEADS, onehot_t.shape[1])))(table_t, onehot_t)


def _table_from_dbias(dbias, onehot_t, name):
    def body(d_ref, o_ref, out_ref):
        out_ref[...] = _nt(d_ref[...], o_ref[...], HIGHEST)

    return _pcall(body, name=name, out_shape=_sds((ATTN_HEADS, REL_BUCKETS)))(dbias, onehot_t)


def _attn_probs(qh, kband, bias_h, sink, mask, scale):
    logits = _nt(qh, kband) * scale + bias_h
    logits = jnp.where(mask, logits, NEG_INF)
    mx = jnp.maximum(jnp.max(logits, axis=-1, keepdims=True), sink)
    e = jnp.exp(logits - mx)
    es = jnp.exp(sink - mx)
    inv = 1.0 / (_rowsum(e) + es)
    return e * inv, es * inv


def _attn_mask(n):
    qi = lax.broadcasted_iota(jnp.int32, (CHUNK, 2 * CHUNK), 0)
    sj = lax.broadcasted_iota(jnp.int32, (CHUNK, 2 * CHUNK), 1)
    rel = qi + CHUNK - sj
    return (rel >= 0) & (rel < CHUNK) & ((sj >= CHUNK) | (n > 0))


def _attn_fwd(qkv, bias, sinks, name):
    T = qkv.shape[0]
    nb = T // CHUNK
    L, DH, HPK = CHUNK, ATTN_DH, ATTN_HEADS // ATTN_KV
    scale = DH ** -0.5
    kcol, vcol = ATTN_HEADS * DH // LANES, ATTN_HEADS * DH // LANES + 1

    def body(q_ref, k_ref, v_ref, kp_ref, vp_ref, bias_ref, sink_ref, o_ref):
        n = pl.program_id(0)
        mask = _attn_mask(n)
        kband = _bf(jnp.concatenate([kp_ref[...], k_ref[...]], axis=0))
        vband = _bf(jnp.concatenate([vp_ref[...], v_ref[...]], axis=0))
        for h in range(ATTN_HEADS):
            kv = h // HPK
            qh = _bf(q_ref[:, h * DH:(h + 1) * DH])
            p, _ = _attn_probs(qh, kband[:, kv * DH:(kv + 1) * DH], bias_ref[h], sink_ref[h], mask, scale)
            o_ref[:, h * DH:(h + 1) * DH] = _nn(_bf(p), vband[:, kv * DH:(kv + 1) * DH]).astype(BF16)

    prev = lambda i: jnp.maximum(i - 1, 0)
    return _pcall(
        body, name=name, out_shape=_sds((T, ATTN_HEADS * DH), BF16), grid=(nb,),
        in_specs=[_row_spec(L, ATTN_HEADS * DH, 0), _row_spec(L, LANES, kcol), _row_spec(L, LANES, vcol),
                  pl.BlockSpec((L, LANES), lambda i: (prev(i), kcol)), pl.BlockSpec((L, LANES), lambda i: (prev(i), vcol)),
                  pl.BlockSpec((ATTN_HEADS, L, 2 * L), lambda i: (0, 0, 0)),
                  pl.BlockSpec(memory_space=pltpu.SMEM)],
        out_specs=_row_spec(L, ATTN_HEADS * DH), sem=("parallel",))(qkv, qkv, qkv, qkv, qkv, bias, sinks)


def _attn_bwd(qkv, datt, bias, sinks, name):
    T = qkv.shape[0]
    nb = T // CHUNK
    L, DH, HPK = CHUNK, ATTN_DH, ATTN_HEADS // ATTN_KV
    scale = DH ** -0.5
    kcol, vcol = ATTN_HEADS * DH // LANES, ATTN_HEADS * DH // LANES + 1

    def body(q_ref, k_ref, v_ref, kp_ref, vp_ref, do_ref, bias_ref, sink_ref,
             dq_ref, dk_ref, dv_ref, dbias_ref, dsink_ref, pend_k, pend_v, band_k, band_v):
        n = pl.program_id(0)

        @pl.when(n == 0)
        def _():
            dbias_ref[...] = jnp.zeros_like(dbias_ref)
            dsink_ref[...] = jnp.zeros_like(dsink_ref)

        @pl.when(n < nb)
        def _():
            mask = _attn_mask(n)
            kband = _bf(jnp.concatenate([kp_ref[...], k_ref[...]], axis=0))
            vband = _bf(jnp.concatenate([vp_ref[...], v_ref[...]], axis=0))
            lane1 = lax.broadcasted_iota(jnp.int32, (1, LANES), 1)
            dsink = jnp.zeros((1, LANES), F32)
            for kv in range(ATTN_KV):
                kb, vb = kband[:, kv * DH:(kv + 1) * DH], vband[:, kv * DH:(kv + 1) * DH]
                dkb = jnp.zeros((2 * L, DH), F32)
                dvb = jnp.zeros((2 * L, DH), F32)
                for j in range(HPK):
                    h = kv * HPK + j
                    qh = _bf(q_ref[:, h * DH:(h + 1) * DH])
                    p, ps = _attn_probs(qh, kb, bias_ref[h], sink_ref[h], mask, scale)
                    doh = _bf(do_ref[:, h * DH:(h + 1) * DH])
                    dp = _nt(doh, vb)
                    dvb = dvb + _tn(_bf(p), doh)
                    delta = _rowsum(p * dp)
                    dl = p * (dp - delta)
                    dsink = dsink + jnp.where(lane1 == h, -_colsum(ps * delta), 0.0)
                    dbias_ref[h] += dl
                    dlb = _bf(dl)
                    dq_ref[:, h * DH:(h + 1) * DH] = _nn(dlb, kb) * scale
                    dkb = dkb + _tn(dlb, qh) * scale
                band_k[:, kv * DH:(kv + 1) * DH] = dkb
                band_v[:, kv * DH:(kv + 1) * DH] = dvb
            dsink_ref[...] += dsink

            @pl.when(n > 0)
            def _():
                dk_ref[...] = pend_k[...] + band_k[0:L, :]
                dv_ref[...] = pend_v[...] + band_v[0:L, :]
            pend_k[...] = band_k[L:2 * L, :]
            pend_v[...] = band_v[L:2 * L, :]

        @pl.when(n == nb)
        def _():
            dk_ref[...] = pend_k[...]
            dv_ref[...] = pend_v[...]

    cur = lambda i: jnp.minimum(i, nb - 1)
    prev = lambda i: jnp.maximum(jnp.minimum(i, nb - 1) - 1, 0)
    lag = lambda i: jnp.maximum(i - 1, 0)
    return _pcall(
        body, name=name,
        out_shape=[_sds((T, ATTN_HEADS * DH)), _sds((T, LANES)), _sds((T, LANES)), _sds((ATTN_HEADS, L, 2 * L)), _sds((1, LANES))],
        grid=(nb + 1,),
        in_specs=[pl.BlockSpec((L, ATTN_HEADS * DH), lambda i: (cur(i), 0)),
                  pl.BlockSpec((L, LANES), lambda i: (cur(i), kcol)), pl.BlockSpec((L, LANES), lambda i: (cur(i), vcol)),
                  pl.BlockSpec((L, LANES), lambda i: (prev(i), kcol)), pl.BlockSpec((L, LANES), lambda i: (prev(i), vcol)),
                  pl.BlockSpec((L, ATTN_HEADS * DH), lambda i: (cur(i), 0)),
                  pl.BlockSpec((ATTN_HEADS, L, 2 * L), lambda i: (0, 0, 0)),
                  pl.BlockSpec(memory_space=pltpu.SMEM)],
        out_specs=[pl.BlockSpec((L, ATTN_HEADS * DH), lambda i: (cur(i), 0)),
                   pl.BlockSpec((L, LANES), lambda i: (lag(i), 0)), pl.BlockSpec((L, LANES), lambda i: (lag(i), 0)),
                   pl.BlockSpec((ATTN_HEADS, L, 2 * L), lambda i: (0, 0, 0)), _vec_spec(LANES)],
        scratch=[pltpu.VMEM((L, LANES), F32), pltpu.VMEM((L, LANES), F32),
                 pltpu.VMEM((2 * L, LANES), F32), pltpu.VMEM((2 * L, LANES), F32)],
        sem=("arbitrary",))(qkv, qkv, qkv, qkv, qkv, datt, bias, sinks)


def _pad_rows(a, mult):
    pad = (-a.shape[-2]) % mult
    if pad == 0:
        return a
    cfg = [(0, 0)] * (a.ndim - 2) + [(0, pad), (0, 0)]
    return jnp.pad(a, cfg)


class _Pack:
    def __init__(self, width, mult, total_mult):
        self.width, self.mult, self.total_mult = width, mult, total_mult
        self.entries = []
        self.rows = 0

    def add(self, key, shape):
        n = int(np.prod(shape))
        rows = -(-n // self.width)
        self.entries.append((key, self.rows, rows, tuple(shape)))
        self.rows += -(-rows // self.mult) * self.mult

    @property
    def total(self):
        return -(-self.rows // self.total_mult) * self.total_mult

    def pack(self, pieces, dtype, lead=()):
        parts = []
        for key, _, rows, shape in self.entries:
            a = pieces[key].astype(dtype).reshape(lead + (-1,))
            n = int(np.prod(shape))
            a = jnp.pad(a, [(0, 0)] * len(lead) + [(0, rows * self.width - n)])
            a = a.reshape(lead + (rows, self.width))
            parts.append(_pad_rows(a, self.mult))
        out = jnp.concatenate(parts, axis=len(lead))
        return _pad_rows(out, self.total_mult)

    def unpack(self, packed, lead=()):
        out = {}
        for key, off, rows, shape in self.entries:
            a = lax.slice_in_dim(packed, off, off + rows, axis=len(lead))
            a = a.reshape(lead + (-1,))
            n = int(np.prod(shape))
            out[key] = lax.slice_in_dim(a, 0, n, axis=len(lead)).reshape(lead + shape)
        return out


def _ffn_fwd(x, mod, norm_w, wg_t, wu_t, wd, tag):
    h = _norm_mod(x, norm_w, mod[4:5], mod[3:4], f"ffn_norm_{tag}")
    gate, up, act = _mm_swiglu(h, wg_t, wu_t, f"ffn_gateup_{tag}")
    x_out, ffn_out = _mm([(act, wd)], "nn", name=f"ffn_down_{tag}", resid=x, gvec=mod[5:6], keep=True)
    return x_out, dict(h=h, gate=gate, up=up, act=act, out=ffn_out)


def _ffn_bwd(dx_out, x_in, saved, mod, norm_w, wg_t, wu_t, wd, tag):
    dffn, acc_g = _gate_bwd(dx_out, saved["out"], mod[5:6], f"ffn_gate_bwd_{tag}")
    dgate, dup = _mm_swiglu_bwd(dffn, wd, saved["gate"], saved["up"], f"ffn_act_bwd_{tag}")
    d_wd = _mm_tn(saved["act"], dffn, name=f"ffn_dwd_{tag}")
    d_wg_t = _mm_tn(dgate, saved["h"], name=f"ffn_dwg_{tag}")
    d_wu_t = _mm_tn(dup, saved["h"], name=f"ffn_dwu_{tag}")
    dh = _mm([(dgate, wg_t), (dup, wu_t)], "nn", name=f"ffn_dh_{tag}")
    dx, acc_n = _norm_mod_bwd(x_in, dh, dx_out, norm_w, mod[4:5], f"ffn_norm_bwd_{tag}")
    return dx, dict(d_wg=d_wg_t, d_wu=d_wu_t, d_wd=d_wd, d_g=acc_g[0], d_sc=acc_n[0], d_sh=acc_n[1], d_nw=acc_n[2])


_BIG = [
    ("out_w", "out_w_even", 0, "row"), ("qkv_w", "qkv_w", 0, "col"), ("o_w", "o_w", 0, "row"),
    ("gate0", "ffn_gate_w", 0, "col"), ("up0", "ffn_up_w", 0, "col"), ("down0", "ffn_down_w", 0, "row"),
    ("gate1", "ffn_gate_w", 1, "col"), ("up1", "ffn_up_w", 1, "col"), ("down1", "ffn_down_w", 1, "row"),
    ("in_w", "in_w_even", 0, "col"),
]


def _to_wire(a, kind):
    return a.T if kind == "col" else a

_REPLICATED = ["ada_b", "norm_mix_w", "norm_ffn_w", "conv_b", "dt_bias", "a_log", "d_skip", "ssm_norm_w", "gmlp_ln_w",
               "gmlp_ln_b", "gmlp_ws", "gmlp_bs", "sinks", "rel_table", "final_norm_w"]
_TINY_SHARDED = ["conv_w", "qkv_b", "o_b"]

_WEIGHTS = ['ada_w', 'ada_b', 'norm_mix_w', 'norm_ffn_w', 'in_w_even', 'conv_w', 'conv_b', 'dt_bias', 'a_log', 'd_skip',
            'ssm_norm_w', 'gmlp_ln_w', 'gmlp_ln_b', 'gmlp_ws', 'gmlp_bs', 'out_w_even', 'qkv_w', 'qkv_b', 'o_w', 'o_b',
            'sinks', 'rel_table', 'ffn_gate_w', 'ffn_up_w', 'ffn_down_w', 'final_norm_w']


def _step(x, c, loss_target, W, M, V):
    T = x.shape[1]
    x0 = x[0]
    target = loss_target[0]
    me = 4 * lax.axis_index("x") + 2 * lax.axis_index("y") + lax.axis_index("c")

    big = _Pack(D, 16, 128)
    wire = _Pack(D, 1, 128)
    for key, name, layer, kind in _BIG:
        big.add(key, W[name][layer].shape)
        wire.add(key, _to_wire(W[name][layer], kind).shape)
    w_wire_local = {key: _to_wire(W[name][layer].astype(BF16), kind) for key, name, layer, kind in _BIG}
    gathered = _all_gather(wire.pack(w_wire_local, BF16), "gather_weights")
    shards = wire.unpack(gathered, lead=(N_DEV,))
    full = {key: shards[key].reshape(-1, D) for key, name, layer, kind in _BIG}

    small_in = _Pack(D, 8, 8)
    small_in.add("c", (1, D))
    small_in.add("conv_w", W["conv_w"][0].shape)
    small_in.add("qkv_b", W["qkv_b"][0].shape)
    small_in.add("o_b", W["o_b"][0].shape)
    sm = small_in.unpack(_all_gather(small_in.pack(
        dict(c=c, conv_w=W["conv_w"][0], qkv_b=W["qkv_b"][0], o_b=W["o_b"][0]), F32), "gather_small"), lead=(N_DEV,))
    c_all = sm["c"].reshape(N_DEV, D)
    conv_w_full = jnp.transpose(sm["conv_w"], (1, 0, 2)).reshape(SSM_CONV, CONV_DIM)
    qkv_b_full = sm["qkv_b"].reshape(1, QKV_DIM)
    o_b_full = sm["o_b"].reshape(1, D)

    ncol = W["ada_w"].shape[2]
    cond, mod_cols = _mod_matmul(c_all, W["ada_w"], "mod_matmul")
    mod_g = _all_gather(mod_cols.reshape(DEPTH * N_DEV, ncol), "gather_mod").reshape(N_DEV, DEPTH, N_DEV, ncol)
    mod_me = lax.dynamic_index_in_dim(mod_g, me, axis=2, keepdims=False)
    mod_me = jnp.transpose(mod_me, (1, 0, 2)).reshape(DEPTH, 6, D)
    mod_me = jnp.pad(mod_me, ((0, 0), (0, 2), (0, 0))).reshape(DEPTH * 8, D)
    ada_b_rows = jnp.pad(W["ada_b"].reshape(DEPTH, 6, D), ((0, 0), (0, 2), (0, 0))).reshape(DEPTH * 8, D)
    mod_all = _add_rows(mod_me, ada_b_rows, "mod_bias").reshape(DEPTH, 8, D)
    mod0, mod1 = mod_all[0], mod_all[1]

    in_t = full["in_w"]
    o1, o2, o3, o4 = SSM_INNER, SSM_INNER + CONV_DIM, SSM_INNER + CONV_DIM + SSM_HEADS, SSM_INNER + CONV_DIM + SSM_HEADS + GMLP_INNER
    w_z, w_xbc, w_dt, w_u, w_v = in_t[:o1], in_t[o1:o2], in_t[o2:o3], in_t[o3:o4], in_t[o4:]
    w_main = jnp.concatenate([w_z, w_u, w_v, w_xbc], axis=0)
    w_dtp = jnp.pad(w_dt, ((0, LANES - SSM_HEADS), (0, 0)))
    out_w = full["out_w"]
    qkv_t, o_w = full["qkv_w"], full["o_w"]
    w_q, w_k, w_v_att = qkv_t[:D], qkv_t[D:D + LANES], qkv_t[D + LANES:]

    pad16 = lambda a: jnp.pad(a.reshape(1, SSM_HEADS), ((0, 0), (0, LANES - SSM_HEADS)))
    dtb, alog = pad16(W["dt_bias"][0]), pad16(W["a_log"][0])
    dskip = jnp.repeat(W["d_skip"][0], SSM_HEAD_DIM).reshape(1, SSM_INNER)
    ssm_nw = W["ssm_norm_w"]
    lnw, lnb = W["gmlp_ln_w"], W["gmlp_ln_b"]
    ws = W["gmlp_ws"][0]
    bs_exp = jnp.repeat(W["gmlp_bs"][0].T, CHUNK, axis=1)
    conv_b = W["conv_b"]
    nmw, nfw = W["norm_mix_w"], W["norm_ffn_w"]
    onehot_t = _bucket_onehot_t()
    bias = _bias_from_table(W["rel_table"].T, onehot_t, "rel_bias").reshape(ATTN_HEADS, CHUNK, 2 * CHUNK)
    sinks = W["sinks"][0]

    h0 = _norm_mod(x0, nmw[0:1], mod0[1:2], mod0[0:1], "mix_norm_0")
    pm = _mm([(h0, w_main)], "nt", name="in_proj", tn_pref=1536)
    dtr = _mm([(h0, w_dtp)], "nt", name="in_proj_dt")
    cpre = _conv_fwd(pm, conv_w_full, conv_b, "conv_fwd")
    ya, ypre, sprev = _ssd_fwd(cpre, dtr, pm, dtb, alog, dskip, ssm_nw, "ssd_fwd")
    yb = _gmlp_fwd(pm, lnw, lnb, ws, bs_exp, "gmlp_fwd")
    x1, mix0 = _mm([(ya, out_w[:SSM_INNER]), (yb, out_w[SSM_INNER:])], "nn", name="out_proj", resid=x0, gvec=mod0[2:3], keep=True)
    x2, ffn0 = _ffn_fwd(x1, mod0, nfw[0:1], full["gate0"], full["up0"], full["down0"], "0")

    h1 = _norm_mod(x2, nmw[1:2], mod1[1:2], mod1[0:1], "mix_norm_1")
    qkv = _mm([(h1, qkv_t)], "nt", name="qkv_proj", bias=qkv_b_full, tn_pref=1280)
    att = _attn_fwd(qkv, bias, sinks, "attn_fwd")
    x3, mix1 = _mm([(att, o_w)], "nn", name="o_proj", bias=o_b_full, resid=x2, gvec=mod1[2:3], keep=True)
    x4, ffn1 = _ffn_fwd(x3, mod1, nfw[1:2], full["gate1"], full["up1"], full["down1"], "1")

    dx4, acc_f = _final_loss(x4, W["final_norm_w"].reshape(1, D), target, "final_loss")
    loss = lax.psum(acc_f[1, 0], ("x", "y", "c"))
    dx3, gf1 = _ffn_bwd(dx4, x3, ffn1, mod1, nfw[1:2], full["gate1"], full["up1"], full["down1"], "1")

    dmix1, acc_m1 = _gate_bwd(dx3, mix1, mod1[2:3], "mix_gate_bwd_1")
    datt = _mm([(dmix1, o_w)], "nt", name="o_proj_dx")
    d_o_w = _mm_tn(att, dmix1, name="o_proj_dw")
    dq, dk, dv, dbias, dsinks = _attn_bwd(qkv, datt, bias, sinks, "attn_bwd")
    d_table = _table_from_dbias(dbias.reshape(ATTN_HEADS, -1), onehot_t, "rel_table_grad").T
    d_qkv_t = jnp.concatenate([_mm_tn(dq, h1, name="qkv_dw_q"), _mm_tn(dk, h1, name="qkv_dw_k"), _mm_tn(dv, h1, name="qkv_dw_v")], axis=0)
    d_qkv_b = jnp.concatenate([_colsum_call(dq, "qkv_db_q"), _colsum_call(dk, "qkv_db_k"), _colsum_call(dv, "qkv_db_v")], axis=1)
    dh1 = _mm([(dq, w_q), (dk, w_k), (dv, w_v_att)], "nn", name="qkv_proj_dx")
    dx2, acc_n1 = _norm_mod_bwd(x2, dh1, dx3, nmw[1:2], mod1[1:2], "mix_norm_bwd_1")

    dx1, gf0 = _ffn_bwd(dx2, x1, ffn0, mod0, nfw[0:1], full["gate0"], full["up0"], full["down0"], "0")

    dmix0, acc_m0 = _gate_bwd(dx1, mix0, mod0[2:3], "mix_gate_bwd_0")
    dya = _mm([(dmix0, out_w[:SSM_INNER])], "nt", name="out_proj_dx_a")
    dyb = _mm([(dmix0, out_w[SSM_INNER:])], "nt", name="out_proj_dx_b")
    d_out_w = jnp.concatenate([_mm_tn(ya, dmix0, name="out_proj_dw_a"), _mm_tn(yb, dmix0, name="out_proj_dw_b")], axis=0)
    du, dvg, d_ws, d_bs, acc_ln = _gmlp_bwd(pm, dyb, lnw, lnb, ws, bs_exp, "gmlp_bwd")
    dz, dcpre, ddtr, acc_ssd, d_ssm_nw = _ssd_bwd(cpre, dtr, pm, ypre, sprev, dya, dtb, alog, dskip, ssm_nw, "ssd_bwd")
    dxbc, acc_conv = _conv_bwd(dcpre, pm, conv_w_full, "conv_bwd")
    d_in_t = jnp.concatenate([
        _mm_tn(dz, h0, name="in_dw_z"), _mm_tn(dxbc, h0, name="in_dw_xbc"),
        _mm_tn(ddtr, h0, name="in_dw_dt")[:SSM_HEADS], _mm_tn(du, h0, name="in_dw_u"), _mm_tn(dvg, h0, name="in_dw_v")], axis=0)
    dh0 = _mm([(dz, w_z), (dxbc, w_xbc), (ddtr, w_dtp), (du, w_u), (dvg, w_v)], "nn", name="in_proj_dx")
    grad_x, acc_n0 = _norm_mod_bwd(x0, dh0, dx1, nmw[0:1], mod0[1:2], "mix_norm_bwd_0")

    g_wire = dict(in_w=d_in_t, out_w=d_out_w, qkv_w=d_qkv_t, o_w=d_o_w,
                  gate0=gf0["d_wg"], up0=gf0["d_wu"], down0=gf0["d_wd"], gate1=gf1["d_wg"], up1=gf1["d_wu"], down1=gf1["d_wd"])
    g_dest = {key: g_wire[key].reshape(N_DEV, -1, D) for key, name, layer, kind in _BIG}
    g_packed = wire.pack(g_dest, F32, lead=(N_DEV,))
    theirs = _exchange_sibling(g_packed, "exchange_grads_sibling")
    from_chips = _exchange_chips(_pair_sum(g_packed, theirs, "grads_pair_sum"), "exchange_grads_chips")
    g_mine = wire.unpack(_sum_parts(from_chips, "grads_chip_sum"))
    g_nat = {key: _to_wire(g_mine[key], kind) for key, name, layer, kind in _BIG}
    pk = lambda S: big.pack({key: S[name][layer] for key, name, layer, kind in _BIG}, F32)
    res_big = [big.unpack(r) for r in _adamw(big.pack(g_nat, F32)[None], pk(W), pk(M), pk(V), "adamw_big")]

    d_mod = jnp.stack([
        jnp.stack([acc_n0[1], acc_n0[0], acc_m0[0], gf0["d_sh"], gf0["d_sc"], gf0["d_g"]]),
        jnp.stack([acc_n1[1], acc_n1[0], acc_m1[0], gf1["d_sh"], gf1["d_sc"], gf1["d_g"]])])
    g_small = dict(
        ada_b=d_mod.reshape(DEPTH, 6 * D),
        norm_mix_w=jnp.stack([acc_n0[2], acc_n1[2]]), norm_ffn_w=jnp.stack([gf0["d_nw"], gf1["d_nw"]]),
        conv_b=acc_conv[4:5], dt_bias=acc_ssd[0:1, :SSM_HEADS], a_log=acc_ssd[1:2, :SSM_HEADS], d_skip=acc_ssd[2:3, :SSM_HEADS],
        ssm_norm_w=d_ssm_nw, gmlp_ln_w=acc_ln[0:1], gmlp_ln_b=acc_ln[1:2], gmlp_ws=d_ws[None],
        gmlp_bs=d_bs[:, :GMLP_GROUPS].T[None], sinks=dsinks[:, :ATTN_HEADS], rel_table=d_table, final_norm_w=acc_f[0],
        conv_w=acc_conv[0:SSM_CONV], qkv_b=d_qkv_b, o_b=acc_m1[1:2])
    small = _Pack(D, 8, 8)
    for name in _REPLICATED:
        small.add(name, W[name].shape)
    small.add("conv_w", (SSM_CONV, CONV_DIM))
    small.add("qkv_b", (1, QKV_DIM))
    small.add("o_b", (1, D))
    parts_small = _all_gather(small.pack(g_small, F32), "gather_small_grads")
    zeros_tiny = dict(conv_w=jnp.zeros((SSM_CONV, CONV_DIM), F32), qkv_b=jnp.zeros((1, QKV_DIM), F32), o_b=jnp.zeros((1, D), F32))
    pks = lambda S: small.pack({**{name: S[name] for name in _REPLICATED}, **zeros_tiny}, F32)
    res_small = [small.unpack(r) for r in _adamw(parts_small, pks(W), pks(M), pks(V), "adamw_small")]
    g_small_sum = res_small[0]

    n_cw, n_qb, n_ob = W["conv_w"].shape[2], W["qkv_b"].shape[1], W["o_b"].shape[1]
    g_tiny = dict(conv_w=lax.dynamic_slice_in_dim(g_small_sum["conv_w"], me * n_cw, n_cw, axis=1)[None],
                  qkv_b=lax.dynamic_slice_in_dim(g_small_sum["qkv_b"], me * n_qb, n_qb, axis=1),
                  o_b=lax.dynamic_slice_in_dim(g_small_sum["o_b"], me * n_ob, n_ob, axis=1))
    tiny = _Pack(D, 8, 8)
    for name in _TINY_SHARDED:
        tiny.add(name, W[name].shape)
    pkt = lambda S: tiny.pack({name: S[name] for name in _TINY_SHARDED}, F32)
    res_tiny = [tiny.unpack(r) for r in _adamw(pkt(g_tiny)[None], pkt(W), pkt(M), pkt(V), "adamw_tiny")]

    dmod_all = parts_small[:, small.entries[0][1]:small.entries[0][1] + small.entries[0][2]].reshape(N_DEV, DEPTH, 6 * D)
    dmod_cols = jnp.transpose(lax.dynamic_slice_in_dim(dmod_all, me * ncol, ncol, axis=2), (1, 0, 2))
    g_ada_w = _ada_w_grad(cond, dmod_cols, "ada_w_grad")
    flat = lambda a: a.reshape(DEPTH * D, ncol)
    res_ada = [r.reshape(DEPTH, D, ncol) for r in _adamw(flat(g_ada_w)[None], flat(W["ada_w"]), flat(M["ada_w"]), flat(V["ada_w"]), "adamw_ada_w")]

    def result(kind_idx, name):
        if name == "ada_w":
            return res_ada[kind_idx]
        if name in _REPLICATED:
            return res_small[kind_idx][name]
        if name in _TINY_SHARDED:
            return res_tiny[kind_idx][name]
        pieces = [res_big[kind_idx][key] for key, nm, layer, kind in _BIG if nm == name]
        return jnp.stack(pieces)

    outs = [loss, grad_x[None]]
    for kind_idx in range(4):
        outs += [result(kind_idx, name) for name in _WEIGHTS]
    return tuple(outs)


def kernel(x, c, ada_w, ada_b, norm_mix_w, norm_ffn_w, in_w_even, conv_w, conv_b, dt_bias, a_log, d_skip, ssm_norm_w, gmlp_ln_w, gmlp_ln_b, gmlp_ws, gmlp_bs, out_w_even, qkv_w, qkv_b, o_w, o_b, sinks, rel_table, ffn_gate_w, ffn_up_w, ffn_down_w, final_norm_w, loss_target, m_ada_w, m_ada_b, m_norm_mix_w, m_norm_ffn_w, m_in_w_even, m_conv_w, m_conv_b, m_dt_bias, m_a_log, m_d_skip, m_ssm_norm_w, m_gmlp_ln_w, m_gmlp_ln_b, m_gmlp_ws, m_gmlp_bs, m_out_w_even, m_qkv_w, m_qkv_b, m_o_w, m_o_b, m_sinks, m_rel_table, m_ffn_gate_w, m_ffn_up_w, m_ffn_down_w, m_final_norm_w, v_ada_w, v_ada_b, v_norm_mix_w, v_norm_ffn_w, v_in_w_even, v_conv_w, v_conv_b, v_dt_bias, v_a_log, v_d_skip, v_ssm_norm_w, v_gmlp_ln_w, v_gmlp_ln_b, v_gmlp_ws, v_gmlp_bs, v_out_w_even, v_qkv_w, v_qkv_b, v_o_w, v_o_b, v_sinks, v_rel_table, v_ffn_gate_w, v_ffn_up_w, v_ffn_down_w, v_final_norm_w):
    args = locals()
    W = {n: args[n] for n in _WEIGHTS}
    M = {n: args["m_" + n] for n in _WEIGHTS}
    V = {n: args["v_" + n] for n in _WEIGHTS}
    return _step(x, c, loss_target, W, M, V)
```

```python
import functools
import math

import numpy as np
import jax
import jax.numpy as jnp
from jax import lax
from jax.experimental import pallas as pl
from jax.experimental.pallas import tpu as pltpu

F32 = jnp.float32
BF16 = jnp.bfloat16
HIGHEST = lax.Precision.HIGHEST
MESH = pl.DeviceIdType.MESH

N_DEV = 8
D = 1024
DEPTH = 2
SSM_HEADS = 16
SSM_HEAD_DIM = 64
SSM_INNER = 1024
SSM_GROUPS = 2
SSM_STATE = 128
SSM_CONV = 4
CHUNK = 128
CONV_DIM = SSM_INNER + 2 * SSM_GROUPS * SSM_STATE
GMLP_GROUPS = 8
GMLP_INNER = 1024
IN_EVEN = 4624
ATTN_HEADS = 16
ATTN_KV = 2
ATTN_DH = 64
QKV_DIM = 1280
REL_BUCKETS = 32
REL_MAX_DIST = 128
FFN = 2816
EPS = 1e-6
NEG_INF = -1e30
LANES = 128

ADAM_LR = 0.001
ADAM_B1 = 0.9
ADAM_B2 = 0.999
ADAM_EPS = 1e-08
ADAM_WD = 0.01
ADAM_STEP = 10

VMEM_LIMIT_BYTES = 56 * 1024 * 1024
ROW_TILE = 512


def _pcall(body, *, name, out_shape, grid=(), in_specs=None, out_specs=None, scratch=(), sem=None):
    params = dict(vmem_limit_bytes=VMEM_LIMIT_BYTES)
    if sem is not None:
        params["dimension_semantics"] = sem
    specs = {} if in_specs is None else dict(in_specs=in_specs, out_specs=out_specs)
    return pl.pallas_call(
        body, name=name, out_shape=out_shape, grid=grid, **specs,
        scratch_shapes=list(scratch), compiler_params=pltpu.CompilerParams(**params))


def _tile(n, pref):
    if n <= pref:
        return n
    best = None
    for t in range(LANES, pref + 1, LANES):
        if n % t == 0:
            best = t
    assert best is not None, (n, pref)
    return best


def _rows(T):
    return min(ROW_TILE, T)


def _sds(shape, dtype=F32):
    return jax.ShapeDtypeStruct(shape, dtype)


def _row_spec(tm, c, col=0):
    return pl.BlockSpec((tm, c), lambda i, col=col: (i, col))


def _vec_spec(c, r=1):
    return pl.BlockSpec((r, c), lambda i: (0, 0))


def _sigmoid(x):
    return jax.nn.sigmoid(x)


def _silu(x):
    return x * _sigmoid(x)


def _dsilu(x):
    s = _sigmoid(x)
    return s * (1.0 + x * (1.0 - s))


def _gelu(x):
    return 0.5 * x * (1.0 + lax.erf(x * 0.7071067811865476))


def _dgelu(x):
    return 0.5 * (1.0 + lax.erf(x * 0.7071067811865476)) + x * jnp.exp(-0.5 * x * x) * 0.3989422804014327


def _dot(a, b, dims, precision=None):
    return lax.dot_general(a, b, (dims, ((), ())), precision=precision, preferred_element_type=F32)


def _nn(a, b, precision=None):
    return _dot(a, b, ((1,), (0,)), precision)


def _nt(a, b, precision=None):
    return _dot(a, b, ((1,), (1,)), precision)


def _tn(a, b, precision=None):
    return _dot(a, b, ((0,), (0,)), precision)


def _bf(x):
    return x.astype(BF16)


def _colsum(x):
    return jnp.sum(x, axis=0, keepdims=True)


def _rowsum(x):
    return jnp.sum(x, axis=1, keepdims=True)


def _allsum(x):
    return _colsum(_rowsum(x))


def _all_gather(x, name):
    R, C = x.shape

    def body(x_ref, out_ref, send_sems, recv_sems, local_sem):
        mx, my, mc = lax.axis_index("x"), lax.axis_index("y"), lax.axis_index("c")
        me, sibling = (mx, my, mc), (mx, my, 1 - mc)
        chips = [(1 - mx, my), (mx, 1 - my), (1 - mx, 1 - my)]

        def slot(px, py, pc):
            return out_ref.at[4 * px + 2 * py + pc]

        def copy(k, block, to, src=None):
            return pltpu.make_async_remote_copy(
                src_ref=slot(*block) if src is None else src, dst_ref=slot(*block),
                send_sem=send_sems.at[k], recv_sem=recv_sems.at[k], device_id=to, device_id_type=MESH)

        mine = pltpu.make_async_copy(x_ref, slot(*me), local_sem)
        mine.start()
        first = [copy(0, me, sibling, src=x_ref)]
        first += [copy(1 + j, me, (*chip, mc), src=x_ref) for j, chip in enumerate(chips)]
        for cp in first:
            cp.start()
        passed = [copy(4 + j, (*chip, mc), sibling) for j, chip in enumerate(chips)]
        for j, chip in enumerate(chips):
            copy(1 + j, (*chip, mc), me).wait_recv()
            passed[j].start()
        copy(0, sibling, me).wait_recv()
        for j, chip in enumerate(chips):
            copy(4 + j, (*chip, 1 - mc), me).wait_recv()
        for cp in first + passed:
            cp.wait_send()
        mine.wait()

    return pl.pallas_call(
        body, name=name, out_shape=_sds((N_DEV, R, C), x.dtype),
        in_specs=[pl.BlockSpec(memory_space=pl.ANY)], out_specs=pl.BlockSpec(memory_space=pl.ANY),
        scratch_shapes=[pltpu.SemaphoreType.DMA((7,)), pltpu.SemaphoreType.DMA((7,)), pltpu.SemaphoreType.DMA(())],
    )(x)


N_CHIP = 4


def _exchange_sibling(p, name):
    _, R, C = p.shape

    def body(p_ref, theirs_ref, send_sems, recv_sems):
        mx, my, mc = lax.axis_index("x"), lax.axis_index("y"), lax.axis_index("c")
        copies = []
        for chip in range(N_CHIP):
            cp = pltpu.make_async_remote_copy(
                src_ref=p_ref.at[2 * chip + 1 - mc], dst_ref=theirs_ref.at[chip],
                send_sem=send_sems.at[chip], recv_sem=recv_sems.at[chip],
                device_id=(mx, my, 1 - mc), device_id_type=MESH)
            cp.start()
            copies.append(cp)
        for cp in copies:
            cp.wait()

    return pl.pallas_call(
        body, name=name, out_shape=_sds((N_CHIP, R, C), p.dtype),
        in_specs=[pl.BlockSpec(memory_space=pl.ANY)], out_specs=pl.BlockSpec(memory_space=pl.ANY),
        scratch_shapes=[pltpu.SemaphoreType.DMA((N_CHIP,))] * 2,
    )(p)


def _exchange_chips(q, name):
    _, R, C = q.shape

    def body(q_ref, out_ref, send_sems, recv_sems, local_sem):
        mx, my, mc = lax.axis_index("x"), lax.axis_index("y"), lax.axis_index("c")
        me = 2 * mx + my
        local = pltpu.make_async_copy(q_ref.at[me], out_ref.at[me], local_sem)
        local.start()
        copies = []
        for r in range(1, N_CHIP):
            px = 1 - mx if r & 2 else mx
            py = 1 - my if r & 1 else my
            cp = pltpu.make_async_remote_copy(
                src_ref=q_ref.at[2 * px + py], dst_ref=out_ref.at[me],
                send_sem=send_sems.at[r - 1], recv_sem=recv_sems.at[r - 1],
                device_id=(px, py, mc), device_id_type=MESH)
            cp.start()
            copies.append(cp)
        for cp in copies:
            cp.wait()
        local.wait()

    return pl.pallas_call(
        body, name=name, out_shape=_sds((N_CHIP, R, C), q.dtype),
        in_specs=[pl.BlockSpec(memory_space=pl.ANY)], out_specs=pl.BlockSpec(memory_space=pl.ANY),
        scratch_shapes=[pltpu.SemaphoreType.DMA((3,)), pltpu.SemaphoreType.DMA((3,)), pltpu.SemaphoreType.DMA(())],
    )(q)


def _pair_sum(p, theirs, name):
    n, R, C = theirs.shape
    tr = _tile_rows(R, 256)

    def body(p_ref, t_ref, o_ref):
        mc = lax.axis_index("c")
        o_ref[0] = (p_ref[0, mc] + t_ref[0]).astype(BF16)

    blk = pl.BlockSpec((1, tr, C), lambda s, i: (s, i, 0))
    return _pcall(body, name=name, out_shape=_sds((n, R, C), BF16), grid=(n, R // tr),
                  in_specs=[pl.BlockSpec((1, 2, tr, C), lambda s, i: (s, 0, i, 0)), blk],
                  out_specs=blk, sem=("parallel", "parallel"))(p.reshape(n, 2, R, C), theirs)


def _sum_parts(parts, name):
    P, R, C = parts.shape
    tr = _tile_rows(R, 256)

    def body(p_ref, o_ref):
        g = p_ref[0].astype(F32)
        for k in range(1, P):
            g = g + p_ref[k].astype(F32)
        o_ref[...] = g

    return _pcall(body, name=name, out_shape=_sds((R, C)), grid=(R // tr,),
                  in_specs=[pl.BlockSpec((P, tr, C), lambda i: (0, i, 0))],
                  out_specs=pl.BlockSpec((tr, C), lambda i: (i, 0)), sem=("parallel",))(parts)


def _adamw(parts, w, m, v, name):
    P, R, C = parts.shape
    tr = R if R <= 256 else _tile_rows(R, 256)

    def body(p_ref, w_ref, m_ref, v_ref, g_ref, d_ref, nm_ref, nv_ref):
        g = p_ref[0]
        for k in range(1, P):
            g = g + p_ref[k]
        nm = ADAM_B1 * m_ref[...] + (1.0 - ADAM_B1) * g
        nv = ADAM_B2 * v_ref[...] + (1.0 - ADAM_B2) * (g * g)
        m_hat = nm / (1.0 - ADAM_B1 ** ADAM_STEP)
        v_hat = nv / (1.0 - ADAM_B2 ** ADAM_STEP)
        g_ref[...] = g
        d_ref[...] = -ADAM_LR * (m_hat / (jnp.sqrt(v_hat) + ADAM_EPS) + ADAM_WD * w_ref[...])
        nm_ref[...] = nm
        nv_ref[...] = nv

    blk = pl.BlockSpec((tr, C), lambda i: (i, 0))
    return _pcall(
        body, name=name, out_shape=[_sds((R, C))] * 4, grid=(R // tr,),
        in_specs=[pl.BlockSpec((P, tr, C), lambda i: (0, i, 0)), blk, blk, blk],
        out_specs=[blk] * 4, sem=("parallel",))(parts, w, m, v)


def _tile_rows(n, pref):
    best = None
    for t in range(8, pref + 1, 8):
        if n % t == 0:
            best = t
    assert best is not None, (n, pref)
    return best


def _mm(pairs, mode, *, name, out_dtype=F32, bias=None, resid=None, gvec=None, keep=False, tn_pref=1024):
    M = pairs[0][0].shape[0]
    N = pairs[0][1].shape[1] if mode == "nn" else pairs[0][1].shape[0]
    tm, tn = _rows(M), _tile(N, tn_pref)
    n_pairs = len(pairs)
    has_bias, has_res = bias is not None, resid is not None

    def body(*refs):
        ab = refs[:2 * n_pairs]
        pos = 2 * n_pairs
        b_ref = refs[pos] if has_bias else None
        pos += has_bias
        r_ref, g_ref = (refs[pos], refs[pos + 1]) if has_res else (None, None)
        pos += 2 * has_res
        outs = refs[pos:]
        acc = None
        for p in range(n_pairs):
            a, b = _bf(ab[2 * p][...]), _bf(ab[2 * p + 1][...])
            d = _nn(a, b) if mode == "nn" else _nt(a, b)
            acc = d if acc is None else acc + d
        if has_bias:
            acc = acc + b_ref[...]
        if has_res:
            outs[0][...] = (r_ref[...] + g_ref[...] * acc).astype(outs[0].dtype)
            if keep:
                outs[1][...] = acc
        else:
            outs[0][...] = acc.astype(outs[0].dtype)

    in_specs, args = [], []
    for a, b in pairs:
        K = a.shape[1]
        in_specs.append(pl.BlockSpec((tm, K), lambda i, j: (i, 0)))
        if mode == "nn":
            in_specs.append(pl.BlockSpec((K, tn), lambda i, j: (0, j)))
        else:
            in_specs.append(pl.BlockSpec((tn, K), lambda i, j: (j, 0)))
        args += [a, b]
    if has_bias:
        in_specs.append(pl.BlockSpec((1, tn), lambda i, j: (0, j)))
        args.append(bias)
    if has_res:
        in_specs.append(pl.BlockSpec((tm, tn), lambda i, j: (i, j)))
        in_specs.append(pl.BlockSpec((1, tn), lambda i, j: (0, j)))
        args += [resid, gvec]
    o_spec = pl.BlockSpec((tm, tn), lambda i, j: (i, j))
    n_out = 2 if (has_res and keep) else 1
    out_shape = [_sds((M, N), out_dtype)] + ([_sds((M, N), F32)] if n_out == 2 else [])
    res = _pcall(body, name=name, out_shape=out_shape, grid=(M // tm, N // tn), in_specs=in_specs,
                 out_specs=[o_spec] * n_out, sem=("parallel", "parallel"))(*args)
    return res if n_out == 2 else res[0]


def _mm_tn(a, b, *, name, tm_pref=1408, tn_pref=1536):
    K, M = a.shape
    N = b.shape[1]
    tm, tn = _tile(M, tm_pref), _tile(N, tn_pref)
    tk = K if K <= ROW_TILE else ROW_TILE

    def body(a_ref, b_ref, o_ref):
        @pl.when(pl.program_id(2) == 0)
        def _():
            o_ref[...] = jnp.zeros_like(o_ref)
        o_ref[...] += _tn(_bf(a_ref[...]), _bf(b_ref[...]))

    return _pcall(
        body, name=name, out_shape=_sds((M, N)), grid=(M // tm, N // tn, K // tk),
        in_specs=[pl.BlockSpec((tk, tm), lambda i, j, k: (k, i)), pl.BlockSpec((tk, tn), lambda i, j, k: (k, j))],
        out_specs=pl.BlockSpec((tm, tn), lambda i, j, k: (i, j)),
        sem=("parallel", "parallel", "arbitrary"))(a, b)


def _mm_swiglu(h, wg_t, wu_t, name):
    M, K = h.shape
    N = wg_t.shape[0]
    tm, tn = _rows(M), _tile(N, 1408)

    def body(h_ref, wg_ref, wu_ref, gate_ref, up_ref, act_ref):
        hv = _bf(h_ref[...])
        gate = _nt(hv, wg_ref[...])
        up = _nt(hv, wu_ref[...])
        gate_ref[...] = gate
        up_ref[...] = up
        act_ref[...] = (_silu(gate) * up).astype(BF16)

    w_spec = pl.BlockSpec((tn, K), lambda i, j: (j, 0))
    o_spec = pl.BlockSpec((tm, tn), lambda i, j: (i, j))
    return _pcall(
        body, name=name, out_shape=[_sds((M, N)), _sds((M, N)), _sds((M, N), BF16)], grid=(M // tm, N // tn),
        in_specs=[pl.BlockSpec((tm, K), lambda i, j: (i, 0)), w_spec, w_spec], out_specs=[o_spec] * 3,
        sem=("parallel", "parallel"))(h, wg_t, wu_t)


def _mm_swiglu_bwd(dout, wd, gate, up, name):
    M, K = dout.shape
    N = wd.shape[0]
    tm, tn = _rows(M), _tile(N, 1408)

    def body(d_ref, wd_ref, gate_ref, up_ref, dg_ref, du_ref):
        dact = _nt(_bf(d_ref[...]), wd_ref[...])
        g = gate_ref[...]
        dg_ref[...] = (dact * up_ref[...] * _dsilu(g)).astype(BF16)
        du_ref[...] = (dact * _silu(g)).astype(BF16)

    t_spec = pl.BlockSpec((tm, tn), lambda i, j: (i, j))
    return _pcall(
        body, name=name, out_shape=[_sds((M, N), BF16)] * 2, grid=(M // tm, N // tn),
        in_specs=[pl.BlockSpec((tm, K), lambda i, j: (i, 0)), pl.BlockSpec((tn, K), lambda i, j: (j, 0)), t_spec, t_spec],
        out_specs=[t_spec] * 2, sem=("parallel", "parallel"))(dout, wd, gate, up)


def _norm_mod(x, w, sc, sh, name):
    T = x.shape[0]
    tm = _rows(T)

    def body(x_ref, w_ref, sc_ref, sh_ref, o_ref):
        xv = x_ref[...]
        r = lax.rsqrt(jnp.mean(xv * xv, axis=-1, keepdims=True) + EPS)
        o_ref[...] = ((xv * r * w_ref[...]) * (1.0 + sc_ref[...]) + sh_ref[...]).astype(BF16)

    return _pcall(body, name=name, out_shape=_sds((T, D), BF16), grid=(T // tm,),
                  in_specs=[_row_spec(tm, D), _vec_spec(D), _vec_spec(D), _vec_spec(D)],
                  out_specs=_row_spec(tm, D), sem=("parallel",))(x, w, sc, sh)


def _norm_mod_bwd(x, dh, dres, w, sc, name):
    T = x.shape[0]
    tm = _rows(T)

    def body(x_ref, dh_ref, dres_ref, w_ref, sc_ref, dx_ref, acc_ref):
        @pl.when(pl.program_id(0) == 0)
        def _():
            acc_ref[...] = jnp.zeros_like(acc_ref)
        xv, dh_v, wv = x_ref[...], dh_ref[...], w_ref[...]
        r = lax.rsqrt(jnp.mean(xv * xv, axis=-1, keepdims=True) + EPS)
        n = xv * r
        dnw = dh_v * (1.0 + sc_ref[...])
        dn = dnw * wv
        dx_ref[...] = dres_ref[...] + r * (dn - n * jnp.mean(dn * n, axis=-1, keepdims=True))
        acc_ref[0:1, :] += _colsum(dh_v * (n * wv))
        acc_ref[1:2, :] += _colsum(dh_v)
        acc_ref[2:3, :] += _colsum(dnw * n)

    return _pcall(body, name=name, out_shape=[_sds((T, D)), _sds((8, D))], grid=(T // tm,),
                  in_specs=[_row_spec(tm, D), _row_spec(tm, D), _row_spec(tm, D), _vec_spec(D), _vec_spec(D)],
                  out_specs=[_row_spec(tm, D), _vec_spec(D, 8)], sem=("arbitrary",))(x, dh, dres, w, sc)


def _gate_bwd(dx, branch, g, name):
    T = dx.shape[0]
    tm = _rows(T)

    def body(dx_ref, br_ref, g_ref, db_ref, acc_ref):
        @pl.when(pl.program_id(0) == 0)
        def _():
            acc_ref[...] = jnp.zeros_like(acc_ref)
        dxv = dx_ref[...]
        db = g_ref[...] * dxv
        db_ref[...] = db.astype(BF16)
        acc_ref[0:1, :] += _colsum(dxv * br_ref[...])
        acc_ref[1:2, :] += _colsum(db)

    return _pcall(body, name=name, out_shape=[_sds((T, D), BF16), _sds((8, D))], grid=(T // tm,),
                  in_specs=[_row_spec(tm, D), _row_spec(tm, D), _vec_spec(D)],
                  out_specs=[_row_spec(tm, D), _vec_spec(D, 8)], sem=("arbitrary",))(dx, branch, g)


def _final_loss(x, wf, target, name):
    T = x.shape[0]
    tm = _rows(T)

    def body(x_ref, w_ref, t_ref, dx_ref, acc_ref):
        @pl.when(pl.program_id(0) == 0)
        def _():
            acc_ref[...] = jnp.zeros_like(acc_ref)
        xv, wv = x_ref[...], w_ref[...]
        r = lax.rsqrt(jnp.mean(xv * xv, axis=-1, keepdims=True) + EPS)
        n = xv * r
        err = n * wv - t_ref[...]
        dy = err * (1.0 / D)
        dn = dy * wv
        dx_ref[...] = r * (dn - n * jnp.mean(dn * n, axis=-1, keepdims=True))
        acc_ref[0:1, :] += _colsum(dy * n)
        acc_ref[1:2, :] += jnp.broadcast_to(_allsum(err * err) * (0.5 / D), (1, D))

    return _pcall(body, name=name, out_shape=[_sds((T, D)), _sds((8, D))], grid=(T // tm,),
                  in_specs=[_row_spec(tm, D), _vec_spec(D), _row_spec(tm, D)],
                  out_specs=[_row_spec(tm, D), _vec_spec(D, 8)], sem=("arbitrary",))(x, wf, target)


def _colsum_call(x, name):
    T, C = x.shape
    tm = _rows(T)

    def body(x_ref, o_ref):
        @pl.when(pl.program_id(0) == 0)
        def _():
            o_ref[...] = jnp.zeros_like(o_ref)
        o_ref[...] += _colsum(x_ref[...].astype(F32))

    return _pcall(body, name=name, out_shape=_sds((1, C)), grid=(T // tm,), in_specs=[_row_spec(tm, C)],
                  out_specs=_vec_spec(C), sem=("arbitrary",))(x)


def _mod_matmul(c_all, ada_w, name):
    n = ada_w.shape[2]

    def body(c_ref, w_ref, cond_ref, o_ref):
        cond = _silu(c_ref[...])
        cond_ref[...] = cond
        o_ref[0] = _nn(cond, w_ref[0])

    return _pcall(body, name=name, out_shape=[_sds((N_DEV, D)), _sds((DEPTH, N_DEV, n))], grid=(DEPTH,),
                  in_specs=[pl.BlockSpec((N_DEV, D), lambda l: (0, 0)), pl.BlockSpec((1, D, n), lambda l: (l, 0, 0))],
                  out_specs=[pl.BlockSpec((N_DEV, D), lambda l: (0, 0)), pl.BlockSpec((1, N_DEV, n), lambda l: (l, 0, 0))],
                  sem=("arbitrary",))(c_all, ada_w)


def _add_rows(a, b, name):
    def body(a_ref, b_ref, o_ref):
        o_ref[...] = a_ref[...] + b_ref[...]

    return _pcall(body, name=name, out_shape=_sds(a.shape))(a, b)


def _ada_w_grad(cond, dmod_cols, name):
    n = dmod_cols.shape[2]

    def body(c_ref, d_ref, o_ref):
        o_ref[0] = _tn(c_ref[...], d_ref[0])

    return _pcall(body, name=name, out_shape=_sds((DEPTH, D, n)), grid=(DEPTH,),
                  in_specs=[pl.BlockSpec((N_DEV, D), lambda l: (0, 0)), pl.BlockSpec((1, N_DEV, n), lambda l: (l, 0, 0))],
                  out_specs=pl.BlockSpec((1, D, n), lambda l: (l, 0, 0)), sem=("parallel",))(cond, dmod_cols)


def _conv_fwd(pm, conv_w, conv_b, name):
    T = pm.shape[0]
    tm = _rows(T)
    C = CONV_DIM

    def body(x_ref, prev_ref, w_ref, b_ref, o_ref):
        cur = x_ref[...]
        prev = jnp.where(pl.program_id(0) > 0, prev_ref[...], 0.0)
        cur8 = cur[0:8]
        row8 = lax.broadcasted_iota(jnp.int32, (8, C), 0)
        full = w_ref[3:4, :] * cur
        head = w_ref[3:4, :] * cur8
        for k in range(1, SSM_CONV):
            wk = w_ref[3 - k:4 - k, :]
            full = full + wk * pltpu.roll(cur, k, 0)
            head = head + wk * jnp.where(row8 < k, pltpu.roll(prev, k, 0), pltpu.roll(cur8, k, 0))
        o_ref[...] = full + b_ref[...]
        o_ref[0:8, :] = head + b_ref[...]

    return _pcall(
        body, name=name, out_shape=_sds((T, C)), grid=(T // tm,),
        in_specs=[pl.BlockSpec((tm, C), lambda i: (i, 2)),
                  pl.BlockSpec((8, C), lambda i: (jnp.maximum(i * (tm // 8) - 1, 0), 2)),
                  _vec_spec(C, SSM_CONV), _vec_spec(C)],
        out_specs=_row_spec(tm, C), sem=("parallel",))(pm, pm, conv_w, conv_b)


def _conv_bwd(dc, pm, conv_w, name):
    T = dc.shape[0]
    tm = _rows(T)
    C = CONV_DIM
    nt = T // tm

    def body(dc_ref, nxt_ref, x_ref, prev_ref, w_ref, dx_ref, acc_ref):
        i = pl.program_id(0)

        @pl.when(i == 0)
        def _():
            acc_ref[...] = jnp.zeros_like(acc_ref)
        dcv = dc_ref[...]
        nxt = jnp.where(i < nt - 1, nxt_ref[...], 0.0)
        xc = x_ref[...]
        prev = jnp.where(i > 0, prev_ref[...], 0.0)
        dc8h, dc8t, x8 = dcv[0:8], dcv[tm - 8:tm], xc[0:8]
        row8 = lax.broadcasted_iota(jnp.int32, (8, C), 0)
        full = w_ref[3:4, :] * dcv
        tail = w_ref[3:4, :] * dc8t
        acc_ref[3:4, :] += _colsum(dcv * xc)
        for k in range(1, SSM_CONV):
            wk = w_ref[3 - k:4 - k, :]
            full = full + wk * pltpu.roll(dcv, tm - k, 0)
            tail = tail + wk * jnp.where(row8 + k >= 8, pltpu.roll(nxt, 8 - k, 0), pltpu.roll(dc8t, 8 - k, 0))
            xs_head = jnp.where(row8 < k, pltpu.roll(prev, k, 0), pltpu.roll(x8, k, 0))
            prod = dcv * pltpu.roll(xc, k, 0)
            acc_ref[3 - k:4 - k, :] += _colsum(prod) - _colsum(prod[0:8]) + _colsum(dc8h * xs_head)
        acc_ref[4:5, :] += _colsum(dcv)
        dx_ref[...] = full
        dx_ref[tm - 8:tm, :] = tail

    return _pcall(
        body, name=name, out_shape=[_sds((T, C)), _sds((8, C))], grid=(nt,),
        in_specs=[_row_spec(tm, C),
                  pl.BlockSpec((8, C), lambda i: (jnp.minimum((i + 1) * (tm // 8), T // 8 - 1), 0)),
                  pl.BlockSpec((tm, C), lambda i: (i, 2)),
                  pl.BlockSpec((8, C), lambda i: (jnp.maximum(i * (tm // 8) - 1, 0), 2)),
                  _vec_spec(C, SSM_CONV)],
        out_specs=[_row_spec(tm, C), _vec_spec(C, 8)], sem=("arbitrary",))(dc, dc, pm, pm, conv_w)


def _ssd_prologue(cpre, dtr, dtb, alog):
    L = CHUNK
    xc = _silu(cpre)
    pre = dtr + dtb
    dt = jnp.maximum(pre, 0.0) + jnp.log1p(jnp.exp(-jnp.abs(pre)))
    a = -jnp.exp(alog)
    la = dt * a
    row = lax.broadcasted_iota(jnp.int32, (L, L), 0)
    col = lax.broadcasted_iota(jnp.int32, (L, L), 1)
    causal = row >= col
    tri = causal.astype(F32)
    lc = _nn(tri, la, HIGHEST)
    return xc, pre, dt, a, causal, tri, lc, row, col


def _head_indicator():
    m = np.zeros((LANES, SSM_INNER), np.float32)
    for h in range(SSM_HEADS):
        m[h, h * SSM_HEAD_DIM:(h + 1) * SSM_HEAD_DIM] = 1.0
    return jnp.asarray(m, dtype=BF16)


def _split_dot(x, ind, dims):
    hi = x.astype(BF16)
    lo = (x - hi.astype(F32)).astype(BF16)
    return _dot(hi, ind, dims) + _dot(lo, ind, dims)


def _expand(x16, ind):
    return _split_dot(x16, ind, ((1,), (0,)))


def _headsum(x, ind):
    return _split_dot(x, ind, ((1,), (1,)))


def _ssd_fwd(cpre, dtr, pm, dtb, alog, dskip, normw, ind, name):
    T = cpre.shape[0]
    nc = T // CHUNK
    L, P, H, HPG, N = CHUNK, SSM_HEAD_DIM, SSM_HEADS, SSM_HEADS // SSM_GROUPS, SSM_STATE
    half = SSM_INNER // SSM_GROUPS

    def body(cp_ref, dtr_ref, z_ref, dtb_ref, alog_ref, dskip_ref, nw_ref, ind_ref, ya_ref, y_ref, sp_ref, st_ref):
        @pl.when(pl.program_id(0) == 0)
        def _():
            st_ref[...] = jnp.zeros_like(st_ref)
        xc, _, dt, _, causal, _, lc, _, _ = _ssd_prologue(cp_ref[...], dtr_ref[...], dtb_ref[...], alog_ref[...])
        lct = lc.T
        ind = ind_ref[...]
        llast = lc[L - 1:L, :]
        xs = xc[:, :SSM_INNER]
        xd = xs * _expand(dt, ind)
        ex = _expand(jnp.exp(lc), ind)
        xd_end = _bf(xd * _expand(jnp.exp(llast - lc), ind))
        cdx = _expand(jnp.broadcast_to(jnp.exp(llast), (8, LANES)), ind)[0:1]
        xdb = _bf(xd)
        sp_ref[0] = st_ref[...]
        for g in range(SSM_GROUPS):
            sl = slice(g * half, (g + 1) * half)
            bm = _bf(xc[:, SSM_INNER + g * N:SSM_INNER + (g + 1) * N])
            cm = _bf(xc[:, SSM_INNER + (SSM_GROUPS + g) * N:SSM_INNER + (SSM_GROUPS + g + 1) * N])
            cb = _nt(cm, bm)
            st = st_ref[g]
            y_ref[:, sl] = ex[:, sl] * _nn(cm, _bf(st)) + dskip_ref[:, sl] * xs[:, sl]
            st_ref[g] = st * cdx[:, sl] + _tn(bm, xd_end[:, sl])
            for j in range(HPG):
                h = g * HPG + j
                decay = jnp.where(causal, jnp.exp(jnp.where(causal, lc[:, h:h + 1] - lct[h:h + 1, :], 0.0)), 0.0)
                y_ref[:, h * P:(h + 1) * P] += _nn(_bf(cb * decay), xdb[:, h * P:(h + 1) * P])
        y2 = y_ref[...] * _silu(z_ref[...])
        for g in range(SSM_GROUPS):
            yg = y2[:, g * half:(g + 1) * half]
            r = lax.rsqrt(jnp.mean(yg * yg, axis=-1, keepdims=True) + EPS)
            ya_ref[:, g * half:(g + 1) * half] = (yg * r * nw_ref[:, g * half:(g + 1) * half]).astype(BF16)

    return _pcall(
        body, name=name,
        out_shape=[_sds((T, SSM_INNER), BF16), _sds((T, SSM_INNER)), _sds((nc, SSM_GROUPS, N, half))], grid=(nc,),
        in_specs=[_row_spec(L, CONV_DIM), _row_spec(L, LANES), _row_spec(L, SSM_INNER, 0),
                  _vec_spec(LANES), _vec_spec(LANES), _vec_spec(SSM_INNER), _vec_spec(SSM_INNER), _vec_spec(SSM_INNER, LANES)],
        out_specs=[_row_spec(L, SSM_INNER), _row_spec(L, SSM_INNER),
                   pl.BlockSpec((1, SSM_GROUPS, N, half), lambda i: (i, 0, 0, 0))],
        scratch=[pltpu.VMEM((SSM_GROUPS, N, half), F32)], sem=("arbitrary",))(cpre, dtr, pm, dtb, alog, dskip, normw, ind)


def _ssd_bwd(cpre, dtr, pm, ypre, sprev, dya, dtb, alog, dskip, normw, ind, name):
    T = cpre.shape[0]
    nc = T // CHUNK
    L, P, H, HPG, N = CHUNK, SSM_HEAD_DIM, SSM_HEADS, SSM_HEADS // SSM_GROUPS, SSM_STATE
    half = SSM_INNER // SSM_GROUPS

    def body(cp_ref, dtr_ref, z_ref, y_ref, sp_ref, dya_ref, dtb_ref, alog_ref, dskip_ref, nw_ref, ind_ref,
             dz_ref, dcp_ref, ddtr_ref, acc_ref, dnw_ref, ds_ref, dy_ref, dxd_ref, rr_ref, yoff_ref, dcd_ref):
        @pl.when(pl.program_id(0) == 0)
        def _():
            ds_ref[...] = jnp.zeros_like(ds_ref)
            acc_ref[...] = jnp.zeros_like(acc_ref)
            dnw_ref[...] = jnp.zeros_like(dnw_ref)
        cpre_v = cp_ref[...]
        xc, pre, dt, a, causal, tri, lc, row, col = _ssd_prologue(cpre_v, dtr_ref[...], dtb_ref[...], alog_ref[...])
        lct = lc.T
        zv, yv = z_ref[...], y_ref[...]
        sz = _silu(zv)
        y2 = yv * sz
        dya_v = dya_ref[...]
        nwv = nw_ref[...]
        for g in range(SSM_GROUPS):
            sl = slice(g * half, (g + 1) * half)
            yg = y2[:, sl]
            r = lax.rsqrt(jnp.mean(yg * yg, axis=-1, keepdims=True) + EPS)
            nrm = yg * r
            dnw_ref[:, sl] += _colsum(dya_v[:, sl] * nrm)
            dn = dya_v[:, sl] * nwv[:, sl]
            dy2 = r * (dn - nrm * jnp.mean(dn * nrm, axis=-1, keepdims=True))
            dy_ref[:, sl] = dy2 * sz[:, sl]
            dz_ref[:, sl] = dy2 * yv[:, sl] * _dsilu(zv[:, sl])
        ind = ind_ref[...]
        llast = lc[L - 1:L, :]
        dte16 = jnp.exp(llast - lc)
        cd16 = jnp.exp(llast)
        xs = xc[:, :SSM_INNER]
        dtx = _expand(dt, ind)
        ex = _expand(jnp.exp(lc), ind)
        dtex = _expand(dte16, ind)
        cdx = _expand(jnp.broadcast_to(cd16, (8, LANES)), ind)[0:1]
        xd = xs * dtx
        xdb = _bf(xd)
        xd_end = _bf(xd * dtex)
        dyv = dy_ref[...]
        dy_off = _bf(ex * dyv)
        dyb = _bf(dyv)
        dskx = dskip_ref[...]
        lane_c = lax.broadcasted_iota(jnp.int32, (L, LANES), 1)
        lane1 = lax.broadcasted_iota(jnp.int32, (1, LANES), 1)
        sub16 = lax.broadcasted_iota(jnp.int32, (H, L), 0)
        dlc_c = jnp.zeros((L, LANES), F32)
        dlc_r = jnp.zeros((H, L), F32)
        for g in range(SSM_GROUPS):
            sl = slice(g * half, (g + 1) * half)
            b_lo = SSM_INNER + g * N
            c_lo = SSM_INNER + (SSM_GROUPS + g) * N
            bm, cm = _bf(xc[:, b_lo:b_lo + N]), _bf(xc[:, c_lo:c_lo + N])
            cb = _nt(cm, bm)
            st, dst = sp_ref[0, g], ds_ref[g]
            stb, dstb = _bf(st), _bf(dst)
            dcm = _nt(dy_off[:, sl], stb)
            ds_ref[g] = _tn(cm, dy_off[:, sl]) + dst * cdx[:, sl]
            rr_ref[:, sl] = _nn(bm, dstb)
            yoff_ref[:, sl] = ex[:, sl] * _nn(cm, stb)
            db = _nt(xd_end[:, sl], dstb)
            dcd_ref[:, sl] = _colsum(dst * st)
            dcb = jnp.zeros((L, L), F32)
            for j in range(HPG):
                h = g * HPG + j
                hs = slice(h * P, (h + 1) * P)
                decay = jnp.where(causal, jnp.exp(jnp.where(causal, lc[:, h:h + 1] - lct[h:h + 1, :], 0.0)), 0.0)
                m = cb * decay
                dxd_ref[:, hs] = _tn(_bf(m), dyb[:, hs])
                dm = _nt(dyb[:, hs], xdb[:, hs])
                dcb = dcb + dm * decay
                gm = dm * m
                dlc_c = dlc_c + jnp.where(lane_c == h, _rowsum(gm), 0.0)
                dlc_r = dlc_r + jnp.where(sub16 == h, _colsum(gm), 0.0)
            dcbb = _bf(dcb)
            dcp_ref[:, c_lo:c_lo + N] = dcm + _nn(dcbb, bm)
            dcp_ref[:, b_lo:b_lo + N] = db + _tn(dcbb, cm)
        dxd_diag, rr = dxd_ref[...], rr_ref[...]
        tt = _headsum(rr * xd, ind) * dte16
        dlc_rt = jnp.concatenate([dlc_r, jnp.zeros((LANES - H, L), F32)], axis=0).T
        dlc = dlc_c - dlc_rt + _headsum(dyv * yoff_ref[...], ind) - tt
        dcd = _headsum(jnp.broadcast_to(dcd_ref[...], (8, SSM_INNER)), ind)[0:1]
        dlc = dlc + jnp.where(row == L - 1, _colsum(tt) + dcd * cd16, 0.0)
        dla = _tn(tri, dlc, HIGHEST)
        dxd = dxd_diag + dtex * rr
        ddt = _headsum(dxd * xs, ind) + dla * a
        ddtr = jnp.where(lane_c < H, ddt * _sigmoid(pre), 0.0)
        ddtr_ref[...] = ddtr
        acc_ref[0:1, :] += _colsum(ddtr)
        acc_ref[1:2, :] += jnp.where(lane1 < H, _colsum(dla * dt) * a, 0.0)
        acc_ref[2:3, :] += _headsum(jnp.broadcast_to(_colsum(dyv * xs), (8, SSM_INNER)), ind)[0:1]
        dcp_ref[:, 0:SSM_INNER] = dxd * dtx + dskx * dyv
        dcp_ref[...] = dcp_ref[...] * _dsilu(cpre_v)

    rev = lambda i: (nc - 1 - i, 0)
    rspec = lambda c: pl.BlockSpec((L, c), rev)
    return _pcall(
        body, name=name,
        out_shape=[_sds((T, SSM_INNER)), _sds((T, CONV_DIM)), _sds((T, LANES)), _sds((8, LANES)), _sds((1, SSM_INNER))],
        grid=(nc,),
        in_specs=[rspec(CONV_DIM), rspec(LANES), rspec(SSM_INNER), rspec(SSM_INNER),
                  pl.BlockSpec((1, SSM_GROUPS, N, half), lambda i: (nc - 1 - i, 0, 0, 0)), rspec(SSM_INNER),
                  _vec_spec(LANES), _vec_spec(LANES), _vec_spec(SSM_INNER), _vec_spec(SSM_INNER), _vec_spec(SSM_INNER, LANES)],
        out_specs=[rspec(SSM_INNER), rspec(CONV_DIM), rspec(LANES), _vec_spec(LANES, 8), _vec_spec(SSM_INNER)],
        scratch=[pltpu.VMEM((SSM_GROUPS, N, half), F32), pltpu.VMEM((L, SSM_INNER), F32), pltpu.VMEM((L, SSM_INNER), F32),
                 pltpu.VMEM((L, SSM_INNER), F32), pltpu.VMEM((L, SSM_INNER), F32), pltpu.VMEM((1, SSM_INNER), F32)],
        sem=("arbitrary",))(cpre, dtr, pm, ypre, sprev, dya, dtb, alog, dskip, normw, ind)


def _gmlp_common(u, v, lnw, lnb):
    ug = _gelu(u)
    vg = _gelu(v)
    mu = jnp.mean(vg, axis=-1, keepdims=True)
    cen = vg - mu
    rstd = lax.rsqrt(jnp.mean(cen * cen, axis=-1, keepdims=True) + EPS)
    vhat = cen * rstd
    return ug, rstd, vhat, vhat * lnw + lnb


def _causal_mask():
    row = lax.broadcasted_iota(jnp.int32, (CHUNK, CHUNK), 0)
    col = lax.broadcasted_iota(jnp.int32, (CHUNK, CHUNK), 1)
    return row >= col


def _gmlp_fwd(pm, lnw, lnb, ws, bs_exp, name):
    T = pm.shape[0]
    nc = T // CHUNK
    L, G = CHUNK, GMLP_GROUPS

    def body(u_ref, v_ref, lnw_ref, lnb_ref, ws_ref, bs_ref, o_ref):
        ug, _, _, vn = _gmlp_common(u_ref[...], v_ref[...], lnw_ref[...], lnb_ref[...])
        causal = _causal_mask()
        for g in range(G):
            sl = slice(g * L, (g + 1) * L)
            wm = _bf(jnp.where(causal, ws_ref[g], 0.0))
            sv = _nn(wm, _bf(vn[:, sl])) + bs_ref[:, sl]
            o_ref[:, sl] = (ug[:, sl] * sv).astype(BF16)

    return _pcall(
        body, name=name, out_shape=_sds((T, GMLP_INNER), BF16), grid=(nc,),
        in_specs=[_row_spec(L, GMLP_INNER, 1), _row_spec(L, GMLP_INNER, 2), _vec_spec(GMLP_INNER), _vec_spec(GMLP_INNER),
                  pl.BlockSpec((G, L, L), lambda i: (0, 0, 0)), _vec_spec(GMLP_INNER, L)],
        out_specs=_row_spec(L, GMLP_INNER), sem=("parallel",))(pm, pm, lnw, lnb, ws, bs_exp)


def _gmlp_bwd(pm, dyb, lnw, lnb, ws, bs_exp, name):
    T = pm.shape[0]
    nc = T // CHUNK
    L, G = CHUNK, GMLP_GROUPS

    def body(u_ref, v_ref, dy_ref, lnw_ref, lnb_ref, ws_ref, bs_ref, du_ref, dv_ref, dws_ref, dbs_ref, acc_ref, dvn_ref):
        @pl.when(pl.program_id(0) == 0)
        def _():
            dws_ref[...] = jnp.zeros_like(dws_ref)
            dbs_ref[...] = jnp.zeros_like(dbs_ref)
            acc_ref[...] = jnp.zeros_like(acc_ref)
        uv, vv, dyv, lnwv = u_ref[...], v_ref[...], dy_ref[...], lnw_ref[...]
        ug, rstd, vhat, vn = _gmlp_common(uv, vv, lnwv, lnb_ref[...])
        causal = _causal_mask()
        lane = lax.broadcasted_iota(jnp.int32, (L, LANES), 1)
        dbs = jnp.zeros((L, LANES), F32)
        for g in range(G):
            sl = slice(g * L, (g + 1) * L)
            wm = _bf(jnp.where(causal, ws_ref[g], 0.0))
            vng = _bf(vn[:, sl])
            sv = _nn(wm, vng) + bs_ref[:, sl]
            du_ref[:, sl] = dyv[:, sl] * sv * _dgelu(uv[:, sl])
            dsv = dyv[:, sl] * ug[:, sl]
            dsvb = _bf(dsv)
            dws_ref[g] += jnp.where(causal, _nt(dsvb, vng), 0.0)
            dbs = dbs + jnp.where(lane == g, _rowsum(dsv), 0.0)
            dvn_ref[:, sl] = _tn(wm, dsvb)
        dbs_ref[...] += dbs
        dvn = dvn_ref[...]
        acc_ref[0:1, :] += _colsum(dvn * vhat)
        acc_ref[1:2, :] += _colsum(dvn)
        dvh = dvn * lnwv
        dvg = rstd * (dvh - jnp.mean(dvh, axis=-1, keepdims=True) - vhat * jnp.mean(dvh * vhat, axis=-1, keepdims=True))
        dv_ref[...] = dvg * _dgelu(vv)

    return _pcall(
        body, name=name,
        out_shape=[_sds((T, GMLP_INNER)), _sds((T, GMLP_INNER)), _sds((G, L, L)), _sds((L, LANES)), _sds((8, GMLP_INNER))],
        grid=(nc,),
        in_specs=[_row_spec(L, GMLP_INNER, 1), _row_spec(L, GMLP_INNER, 2), _row_spec(L, GMLP_INNER),
                  _vec_spec(GMLP_INNER), _vec_spec(GMLP_INNER), pl.BlockSpec((G, L, L), lambda i: (0, 0, 0)),
                  _vec_spec(GMLP_INNER, L)],
        out_specs=[_row_spec(L, GMLP_INNER), _row_spec(L, GMLP_INNER), pl.BlockSpec((G, L, L), lambda i: (0, 0, 0)),
                   _vec_spec(LANES, L), _vec_spec(GMLP_INNER, 8)],
        scratch=[pltpu.VMEM((L, GMLP_INNER), F32)], sem=("arbitrary",))(pm, pm, dyb, lnw, lnb, ws, bs_exp)


def _rel_buckets():
    qi = np.arange(CHUNK)[:, None]
    sj = np.arange(2 * CHUNK)[None, :]
    dist = np.maximum(qi + CHUNK - sj, 0)
    max_exact = REL_BUCKETS // 2
    log_ratio = (np.log(np.maximum(dist, 1).astype(np.float32) / np.float32(max_exact))
                 / np.float32(math.log(REL_MAX_DIST / max_exact))).astype(np.float32)
    large = max_exact + (log_ratio * np.float32(REL_BUCKETS - max_exact)).astype(np.int32)
    return np.where(dist < max_exact, dist, np.minimum(large, REL_BUCKETS - 1))


def _bucket_onehot_t():
    bucket = _rel_buckets().reshape(-1)
    return jnp.asarray((np.arange(REL_BUCKETS)[:, None] == bucket[None, :]).astype(np.float32))


def _bias_from_table(table_t, onehot_t, name):
    def body(t_ref, o_ref, out_ref):
        out_ref[...] = _nn(t_ref[...], o_ref[...], HIGHEST)

    return _pcall(body, name=name, out_shape=_sds((ATTN_HEADS, onehot_t.shape[1])))(table_t, onehot_t)


def _table_from_dbias(dbias, onehot_t, name):
    def body(d_ref, o_ref, out_ref):
        out_ref[...] = _nt(d_ref[...], o_ref[...], HIGHEST)

    return _pcall(body, name=name, out_shape=_sds((ATTN_HEADS, REL_BUCKETS)))(dbias, onehot_t)


def _attn_probs(qh, kband, bias_h, sink, mask, scale):
    logits = _nt(qh, kband) * scale + bias_h
    logits = jnp.where(mask, logits, NEG_INF)
    mx = jnp.maximum(jnp.max(logits, axis=-1, keepdims=True), sink)
    e = jnp.exp(logits - mx)
    es = jnp.exp(sink - mx)
    inv = 1.0 / (_rowsum(e) + es)
    return e * inv, es * inv


def _attn_mask(n):
    qi = lax.broadcasted_iota(jnp.int32, (CHUNK, 2 * CHUNK), 0)
    sj = lax.broadcasted_iota(jnp.int32, (CHUNK, 2 * CHUNK), 1)
    rel = qi + CHUNK - sj
    return (rel >= 0) & (rel < CHUNK) & ((sj >= CHUNK) | (n > 0))


def _attn_fwd(qkv, bias, sinks, name):
    T = qkv.shape[0]
    nb = T // CHUNK
    L, DH, HPK = CHUNK, ATTN_DH, ATTN_HEADS // ATTN_KV
    scale = DH ** -0.5
    kcol, vcol = ATTN_HEADS * DH // LANES, ATTN_HEADS * DH // LANES + 1

    def body(q_ref, k_ref, v_ref, kp_ref, vp_ref, bias_ref, sink_ref, o_ref):
        n = pl.program_id(0)
        mask = _attn_mask(n)
        kband = _bf(jnp.concatenate([kp_ref[...], k_ref[...]], axis=0))
        vband = _bf(jnp.concatenate([vp_ref[...], v_ref[...]], axis=0))
        for h in range(ATTN_HEADS):
            kv = h // HPK
            qh = _bf(q_ref[:, h * DH:(h + 1) * DH])
            p, _ = _attn_probs(qh, kband[:, kv * DH:(kv + 1) * DH], bias_ref[h], sink_ref[h], mask, scale)
            o_ref[:, h * DH:(h + 1) * DH] = _nn(_bf(p), vband[:, kv * DH:(kv + 1) * DH]).astype(BF16)

    prev = lambda i: jnp.maximum(i - 1, 0)
    return _pcall(
        body, name=name, out_shape=_sds((T, ATTN_HEADS * DH), BF16), grid=(nb,),
        in_specs=[_row_spec(L, ATTN_HEADS * DH, 0), _row_spec(L, LANES, kcol), _row_spec(L, LANES, vcol),
                  pl.BlockSpec((L, LANES), lambda i: (prev(i), kcol)), pl.BlockSpec((L, LANES), lambda i: (prev(i), vcol)),
                  pl.BlockSpec((ATTN_HEADS, L, 2 * L), lambda i: (0, 0, 0)),
                  pl.BlockSpec(memory_space=pltpu.SMEM)],
        out_specs=_row_spec(L, ATTN_HEADS * DH), sem=("parallel",))(qkv, qkv, qkv, qkv, qkv, bias, sinks)


def _attn_bwd(qkv, datt, bias, sinks, name):
    T = qkv.shape[0]
    nb = T // CHUNK
    L, DH, HPK = CHUNK, ATTN_DH, ATTN_HEADS // ATTN_KV
    scale = DH ** -0.5
    kcol, vcol = ATTN_HEADS * DH // LANES, ATTN_HEADS * DH // LANES + 1

    def body(q_ref, k_ref, v_ref, kp_ref, vp_ref, do_ref, bias_ref, sink_ref,
             dq_ref, dk_ref, dv_ref, dbias_ref, dsink_ref, pend_k, pend_v, band_k, band_v):
        n = pl.program_id(0)

        @pl.when(n == 0)
        def _():
            dbias_ref[...] = jnp.zeros_like(dbias_ref)
            dsink_ref[...] = jnp.zeros_like(dsink_ref)

        @pl.when(n < nb)
        def _():
            mask = _attn_mask(n)
            kband = _bf(jnp.concatenate([kp_ref[...], k_ref[...]], axis=0))
            vband = _bf(jnp.concatenate([vp_ref[...], v_ref[...]], axis=0))
            lane1 = lax.broadcasted_iota(jnp.int32, (1, LANES), 1)
            dsink = jnp.zeros((1, LANES), F32)
            for kv in range(ATTN_KV):
                kb, vb = kband[:, kv * DH:(kv + 1) * DH], vband[:, kv * DH:(kv + 1) * DH]
                dkb = jnp.zeros((2 * L, DH), F32)
                dvb = jnp.zeros((2 * L, DH), F32)
                for j in range(HPK):
                    h = kv * HPK + j
                    qh = _bf(q_ref[:, h * DH:(h + 1) * DH])
                    p, ps = _attn_probs(qh, kb, bias_ref[h], sink_ref[h], mask, scale)
                    doh = _bf(do_ref[:, h * DH:(h + 1) * DH])
                    dp = _nt(doh, vb)
                    dvb = dvb + _tn(_bf(p), doh)
                    delta = _rowsum(p * dp)
                    dl = p * (dp - delta)
                    dsink = dsink + jnp.where(lane1 == h, -_colsum(ps * delta), 0.0)
                    dbias_ref[h] += dl
                    dlb = _bf(dl)
                    dq_ref[:, h * DH:(h + 1) * DH] = _nn(dlb, kb) * scale
                    dkb = dkb + _tn(dlb, qh) * scale
                band_k[:, kv * DH:(kv + 1) * DH] = dkb
                band_v[:, kv * DH:(kv + 1) * DH] = dvb
            dsink_ref[...] += dsink

            @pl.when(n > 0)
            def _():
                dk_ref[...] = pend_k[...] + band_k[0:L, :]
                dv_ref[...] = pend_v[...] + band_v[0:L, :]
            pend_k[...] = band_k[L:2 * L, :]
            pend_v[...] = band_v[L:2 * L, :]

        @pl.when(n == nb)
        def _():
            dk_ref[...] = pend_k[...]
            dv_ref[...] = pend_v[...]

    cur = lambda i: jnp.minimum(i, nb - 1)
    prev = lambda i: jnp.maximum(jnp.minimum(i, nb - 1) - 1, 0)
    lag = lambda i: jnp.maximum(i - 1, 0)
    return _pcall(
        body, name=name,
        out_shape=[_sds((T, ATTN_HEADS * DH)), _sds((T, LANES)), _sds((T, LANES)), _sds((ATTN_HEADS, L, 2 * L)), _sds((1, LANES))],
        grid=(nb + 1,),
        in_specs=[pl.BlockSpec((L, ATTN_HEADS * DH), lambda i: (cur(i), 0)),
                  pl.BlockSpec((L, LANES), lambda i: (cur(i), kcol)), pl.BlockSpec((L, LANES), lambda i: (cur(i), vcol)),
                  pl.BlockSpec((L, LANES), lambda i: (prev(i), kcol)), pl.BlockSpec((L, LANES), lambda i: (prev(i), vcol)),
                  pl.BlockSpec((L, ATTN_HEADS * DH), lambda i: (cur(i), 0)),
                  pl.BlockSpec((ATTN_HEADS, L, 2 * L), lambda i: (0, 0, 0)),
                  pl.BlockSpec(memory_space=pltpu.SMEM)],
        out_specs=[pl.BlockSpec((L, ATTN_HEADS * DH), lambda i: (cur(i), 0)),
                   pl.BlockSpec((L, LANES), lambda i: (lag(i), 0)), pl.BlockSpec((L, LANES), lambda i: (lag(i), 0)),
                   pl.BlockSpec((ATTN_HEADS, L, 2 * L), lambda i: (0, 0, 0)), _vec_spec(LANES)],
        scratch=[pltpu.VMEM((L, LANES), F32), pltpu.VMEM((L, LANES), F32),
                 pltpu.VMEM((2 * L, LANES), F32), pltpu.VMEM((2 * L, LANES), F32)],
        sem=("arbitrary",))(qkv, qkv, qkv, qkv, qkv, datt, bias, sinks)


def _pad_rows(a, mult):
    pad = (-a.shape[-2]) % mult
    if pad == 0:
        return a
    cfg = [(0, 0)] * (a.ndim - 2) + [(0, pad), (0, 0)]
    return jnp.pad(a, cfg)


class _Pack:
    def __init__(self, width, mult, total_mult):
        self.width, self.mult, self.total_mult = width, mult, total_mult
        self.entries = []
        self.rows = 0

    def add(self, key, shape):
        n = int(np.prod(shape))
        rows = -(-n // self.width)
        self.entries.append((key, self.rows, rows, tuple(shape)))
        self.rows += -(-rows // self.mult) * self.mult

    @property
    def total(self):
        return -(-self.rows // self.total_mult) * self.total_mult

    def pack(self, pieces, dtype, lead=()):
        parts = []
        for key, _, rows, shape in self.entries:
            a = pieces[key].astype(dtype).reshape(lead + (-1,))
            n = int(np.prod(shape))
            a = jnp.pad(a, [(0, 0)] * len(lead) + [(0, rows * self.width - n)])
            a = a.reshape(lead + (rows, self.width))
            parts.append(_pad_rows(a, self.mult))
        out = jnp.concatenate(parts, axis=len(lead))
        return _pad_rows(out, self.total_mult)

    def unpack(self, packed, lead=()):
        out = {}
        for key, off, rows, shape in self.entries:
            a = lax.slice_in_dim(packed, off, off + rows, axis=len(lead))
            a = a.reshape(lead + (-1,))
            n = int(np.prod(shape))
            out[key] = lax.slice_in_dim(a, 0, n, axis=len(lead)).reshape(lead + shape)
        return out


def _ffn_fwd(x, mod, norm_w, wg_t, wu_t, wd, tag):
    h = _norm_mod(x, norm_w, mod[4:5], mod[3:4], f"ffn_norm_{tag}")
    gate, up, act = _mm_swiglu(h, wg_t, wu_t, f"ffn_gateup_{tag}")
    x_out, ffn_out = _mm([(act, wd)], "nn", name=f"ffn_down_{tag}", resid=x, gvec=mod[5:6], keep=True)
    return x_out, dict(h=h, gate=gate, up=up, act=act, out=ffn_out)


def _ffn_bwd(dx_out, x_in, saved, mod, norm_w, wg_t, wu_t, wd, tag):
    dffn, acc_g = _gate_bwd(dx_out, saved["out"], mod[5:6], f"ffn_gate_bwd_{tag}")
    dgate, dup = _mm_swiglu_bwd(dffn, wd, saved["gate"], saved["up"], f"ffn_act_bwd_{tag}")
    d_wd = _mm_tn(saved["act"], dffn, name=f"ffn_dwd_{tag}")
    d_wg_t = _mm_tn(dgate, saved["h"], name=f"ffn_dwg_{tag}")
    d_wu_t = _mm_tn(dup, saved["h"], name=f"ffn_dwu_{tag}")
    dh = _mm([(dgate, wg_t), (dup, wu_t)], "nn", name=f"ffn_dh_{tag}")
    dx, acc_n = _norm_mod_bwd(x_in, dh, dx_out, norm_w, mod[4:5], f"ffn_norm_bwd_{tag}")
    return dx, dict(d_wg=d_wg_t, d_wu=d_wu_t, d_wd=d_wd, d_g=acc_g[0], d_sc=acc_n[0], d_sh=acc_n[1], d_nw=acc_n[2])


_BIG = [
    ("out_w", "out_w_even", 0, "row"), ("qkv_w", "qkv_w", 0, "col"), ("o_w", "o_w", 0, "row"),
    ("gate0", "ffn_gate_w", 0, "col"), ("up0", "ffn_up_w", 0, "col"), ("down0", "ffn_down_w", 0, "row"),
    ("gate1", "ffn_gate_w", 1, "col"), ("up1", "ffn_up_w", 1, "col"), ("down1", "ffn_down_w", 1, "row"),
    ("in_w", "in_w_even", 0, "col"),
]


def _to_wire(a, kind):
    return a.T if kind == "col" else a

_REPLICATED = ["ada_b", "norm_mix_w", "norm_ffn_w", "conv_b", "dt_bias", "a_log", "d_skip", "ssm_norm_w", "gmlp_ln_w",
               "gmlp_ln_b", "gmlp_ws", "gmlp_bs", "sinks", "rel_table", "final_norm_w"]
_TINY_SHARDED = ["conv_w", "qkv_b", "o_b"]

_WEIGHTS = ['ada_w', 'ada_b', 'norm_mix_w', 'norm_ffn_w', 'in_w_even', 'conv_w', 'conv_b', 'dt_bias', 'a_log', 'd_skip',
            'ssm_norm_w', 'gmlp_ln_w', 'gmlp_ln_b', 'gmlp_ws', 'gmlp_bs', 'out_w_even', 'qkv_w', 'qkv_b', 'o_w', 'o_b',
            'sinks', 'rel_table', 'ffn_gate_w', 'ffn_up_w', 'ffn_down_w', 'final_norm_w']


def _step(x, c, loss_target, W, M, V):
    T = x.shape[1]
    x0 = x[0]
    target = loss_target[0]
    me = 4 * lax.axis_index("x") + 2 * lax.axis_index("y") + lax.axis_index("c")

    big = _Pack(D, 16, 128)
    wire = _Pack(D, 1, 128)
    for key, name, layer, kind in _BIG:
        big.add(key, W[name][layer].shape)
        wire.add(key, _to_wire(W[name][layer], kind).shape)
    w_wire_local = {key: _to_wire(W[name][layer].astype(BF16), kind) for key, name, layer, kind in _BIG}
    gathered = _all_gather(wire.pack(w_wire_local, BF16), "gather_weights")
    shards = wire.unpack(gathered, lead=(N_DEV,))
    full = {key: shards[key].reshape(-1, D) for key, name, layer, kind in _BIG}

    small_in = _Pack(D, 8, 8)
    small_in.add("c", (1, D))
    small_in.add("conv_w", W["conv_w"][0].shape)
    small_in.add("qkv_b", W["qkv_b"][0].shape)
    small_in.add("o_b", W["o_b"][0].shape)
    sm = small_in.unpack(_all_gather(small_in.pack(
        dict(c=c, conv_w=W["conv_w"][0], qkv_b=W["qkv_b"][0], o_b=W["o_b"][0]), F32), "gather_small"), lead=(N_DEV,))
    c_all = sm["c"].reshape(N_DEV, D)
    conv_w_full = jnp.transpose(sm["conv_w"], (1, 0, 2)).reshape(SSM_CONV, CONV_DIM)
    qkv_b_full = sm["qkv_b"].reshape(1, QKV_DIM)
    o_b_full = sm["o_b"].reshape(1, D)

    ncol = W["ada_w"].shape[2]
    cond, mod_cols = _mod_matmul(c_all, W["ada_w"], "mod_matmul")
    mod_g = _all_gather(mod_cols.reshape(DEPTH * N_DEV, ncol), "gather_mod").reshape(N_DEV, DEPTH, N_DEV, ncol)
    mod_me = lax.dynamic_index_in_dim(mod_g, me, axis=2, keepdims=False)
    mod_me = jnp.transpose(mod_me, (1, 0, 2)).reshape(DEPTH, 6, D)
    mod_me = jnp.pad(mod_me, ((0, 0), (0, 2), (0, 0))).reshape(DEPTH * 8, D)
    ada_b_rows = jnp.pad(W["ada_b"].reshape(DEPTH, 6, D), ((0, 0), (0, 2), (0, 0))).reshape(DEPTH * 8, D)
    mod_all = _add_rows(mod_me, ada_b_rows, "mod_bias").reshape(DEPTH, 8, D)
    mod0, mod1 = mod_all[0], mod_all[1]

    in_t = full["in_w"]
    o1, o2, o3, o4 = SSM_INNER, SSM_INNER + CONV_DIM, SSM_INNER + CONV_DIM + SSM_HEADS, SSM_INNER + CONV_DIM + SSM_HEADS + GMLP_INNER
    w_z, w_xbc, w_dt, w_u, w_v = in_t[:o1], in_t[o1:o2], in_t[o2:o3], in_t[o3:o4], in_t[o4:]
    w_main = jnp.concatenate([w_z, w_u, w_v, w_xbc], axis=0)
    w_dtp = jnp.pad(w_dt, ((0, LANES - SSM_HEADS), (0, 0)))
    out_w = full["out_w"]
    qkv_t, o_w = full["qkv_w"], full["o_w"]
    w_q, w_k, w_v_att = qkv_t[:D], qkv_t[D:D + LANES], qkv_t[D + LANES:]

    pad16 = lambda a: jnp.pad(a.reshape(1, SSM_HEADS), ((0, 0), (0, LANES - SSM_HEADS)))
    dtb, alog = pad16(W["dt_bias"][0]), pad16(W["a_log"][0])
    dskip = jnp.repeat(W["d_skip"][0], SSM_HEAD_DIM).reshape(1, SSM_INNER)
    ssm_nw = W["ssm_norm_w"]
    lnw, lnb = W["gmlp_ln_w"], W["gmlp_ln_b"]
    ws = W["gmlp_ws"][0]
    bs_exp = jnp.repeat(W["gmlp_bs"][0].T, CHUNK, axis=1)
    conv_b = W["conv_b"]
    nmw, nfw = W["norm_mix_w"], W["norm_ffn_w"]
    onehot_t = _bucket_onehot_t()
    head_ind = _head_indicator()
    bias = _bias_from_table(W["rel_table"].T, onehot_t, "rel_bias").reshape(ATTN_HEADS, CHUNK, 2 * CHUNK)
    sinks = W["sinks"][0]

    h0 = _norm_mod(x0, nmw[0:1], mod0[1:2], mod0[0:1], "mix_norm_0")
    pm = _mm([(h0, w_main)], "nt", name="in_proj", tn_pref=1536)
    dtr = _mm([(h0, w_dtp)], "nt", name="in_proj_dt")
    cpre = _conv_fwd(pm, conv_w_full, conv_b, "conv_fwd")
    ya, ypre, sprev = _ssd_fwd(cpre, dtr, pm, dtb, alog, dskip, ssm_nw, head_ind, "ssd_fwd")
    yb = _gmlp_fwd(pm, lnw, lnb, ws, bs_exp, "gmlp_fwd")
    x1, mix0 = _mm([(ya, out_w[:SSM_INNER]), (yb, out_w[SSM_INNER:])], "nn", name="out_proj", resid=x0, gvec=mod0[2:3], keep=True)
    x2, ffn0 = _ffn_fwd(x1, mod0, nfw[0:1], full["gate0"], full["up0"], full["down0"], "0")

    h1 = _norm_mod(x2, nmw[1:2], mod1[1:2], mod1[0:1], "mix_norm_1")
    qkv = _mm([(h1, qkv_t)], "nt", name="qkv_proj", bias=qkv_b_full, tn_pref=1280)
    att = _attn_fwd(qkv, bias, sinks, "attn_fwd")
    x3, mix1 = _mm([(att, o_w)], "nn", name="o_proj", bias=o_b_full, resid=x2, gvec=mod1[2:3], keep=True)
    x4, ffn1 = _ffn_fwd(x3, mod1, nfw[1:2], full["gate1"], full["up1"], full["down1"], "1")

    dx4, acc_f = _final_loss(x4, W["final_norm_w"].reshape(1, D), target, "final_loss")
    loss = lax.psum(acc_f[1, 0], ("x", "y", "c"))
    dx3, gf1 = _ffn_bwd(dx4, x3, ffn1, mod1, nfw[1:2], full["gate1"], full["up1"], full["down1"], "1")

    dmix1, acc_m1 = _gate_bwd(dx3, mix1, mod1[2:3], "mix_gate_bwd_1")
    datt = _mm([(dmix1, o_w)], "nt", name="o_proj_dx")
    d_o_w = _mm_tn(att, dmix1, name="o_proj_dw")
    dq, dk, dv, dbias, dsinks = _attn_bwd(qkv, datt, bias, sinks, "attn_bwd")
    d_table = _table_from_dbias(dbias.reshape(ATTN_HEADS, -1), onehot_t, "rel_table_grad").T
    d_qkv_t = jnp.concatenate([_mm_tn(dq, h1, name="qkv_dw_q"), _mm_tn(dk, h1, name="qkv_dw_k"), _mm_tn(dv, h1, name="qkv_dw_v")], axis=0)
    d_qkv_b = jnp.concatenate([_colsum_call(dq, "qkv_db_q"), _colsum_call(dk, "qkv_db_k"), _colsum_call(dv, "qkv_db_v")], axis=1)
    dh1 = _mm([(dq, w_q), (dk, w_k), (dv, w_v_att)], "nn", name="qkv_proj_dx")
    dx2, acc_n1 = _norm_mod_bwd(x2, dh1, dx3, nmw[1:2], mod1[1:2], "mix_norm_bwd_1")

    dx1, gf0 = _ffn_bwd(dx2, x1, ffn0, mod0, nfw[0:1], full["gate0"], full["up0"], full["down0"], "0")

    dmix0, acc_m0 = _gate_bwd(dx1, mix0, mod0[2:3], "mix_gate_bwd_0")
    dya = _mm([(dmix0, out_w[:SSM_INNER])], "nt", name="out_proj_dx_a")
    dyb = _mm([(dmix0, out_w[SSM_INNER:])], "nt", name="out_proj_dx_b")
    d_out_w = jnp.concatenate([_mm_tn(ya, dmix0, name="out_proj_dw_a"), _mm_tn(yb, dmix0, name="out_proj_dw_b")], axis=0)
    du, dvg, d_ws, d_bs, acc_ln = _gmlp_bwd(pm, dyb, lnw, lnb, ws, bs_exp, "gmlp_bwd")
    dz, dcpre, ddtr, acc_ssd, d_ssm_nw = _ssd_bwd(cpre, dtr, pm, ypre, sprev, dya, dtb, alog, dskip, ssm_nw, head_ind, "ssd_bwd")
    dxbc, acc_conv = _conv_bwd(dcpre, pm, conv_w_full, "conv_bwd")
    d_in_t = jnp.concatenate([
        _mm_tn(dz, h0, name="in_dw_z"), _mm_tn(dxbc, h0, name="in_dw_xbc"),
        _mm_tn(ddtr, h0, name="in_dw_dt")[:SSM_HEADS], _mm_tn(du, h0, name="in_dw_u"), _mm_tn(dvg, h0, name="in_dw_v")], axis=0)
    dh0 = _mm([(dz, w_z), (dxbc, w_xbc), (ddtr, w_dtp), (du, w_u), (dvg, w_v)], "nn", name="in_proj_dx")
    grad_x, acc_n0 = _norm_mod_bwd(x0, dh0, dx1, nmw[0:1], mod0[1:2], "mix_norm_bwd_0")

    g_wire = dict(in_w=d_in_t, out_w=d_out_w, qkv_w=d_qkv_t, o_w=d_o_w,
                  gate0=gf0["d_wg"], up0=gf0["d_wu"], down0=gf0["d_wd"], gate1=gf1["d_wg"], up1=gf1["d_wu"], down1=gf1["d_wd"])
    g_dest = {key: g_wire[key].reshape(N_DEV, -1, D) for key, name, layer, kind in _BIG}
    g_packed = wire.pack(g_dest, F32, lead=(N_DEV,))
    theirs = _exchange_sibling(g_packed, "exchange_grads_sibling")
    from_chips = _exchange_chips(_pair_sum(g_packed, theirs, "grads_pair_sum"), "exchange_grads_chips")
    g_mine = wire.unpack(_sum_parts(from_chips, "grads_chip_sum"))
    g_nat = {key: _to_wire(g_mine[key], kind) for key, name, layer, kind in _BIG}
    pk = lambda S: big.pack({key: S[name][layer] for key, name, layer, kind in _BIG}, F32)
    res_big = [big.unpack(r) for r in _adamw(big.pack(g_nat, F32)[None], pk(W), pk(M), pk(V), "adamw_big")]

    d_mod = jnp.stack([
        jnp.stack([acc_n0[1], acc_n0[0], acc_m0[0], gf0["d_sh"], gf0["d_sc"], gf0["d_g"]]),
        jnp.stack([acc_n1[1], acc_n1[0], acc_m1[0], gf1["d_sh"], gf1["d_sc"], gf1["d_g"]])])
    g_small = dict(
        ada_b=d_mod.reshape(DEPTH, 6 * D),
        norm_mix_w=jnp.stack([acc_n0[2], acc_n1[2]]), norm_ffn_w=jnp.stack([gf0["d_nw"], gf1["d_nw"]]),
        conv_b=acc_conv[4:5], dt_bias=acc_ssd[0:1, :SSM_HEADS], a_log=acc_ssd[1:2, :SSM_HEADS], d_skip=acc_ssd[2:3, :SSM_HEADS],
        ssm_norm_w=d_ssm_nw, gmlp_ln_w=acc_ln[0:1], gmlp_ln_b=acc_ln[1:2], gmlp_ws=d_ws[None],
        gmlp_bs=d_bs[:, :GMLP_GROUPS].T[None], sinks=dsinks[:, :ATTN_HEADS], rel_table=d_table, final_norm_w=acc_f[0],
        conv_w=acc_conv[0:SSM_CONV], qkv_b=d_qkv_b, o_b=acc_m1[1:2])
    small = _Pack(D, 8, 8)
    for name in _REPLICATED:
        small.add(name, W[name].shape)
    small.add("conv_w", (SSM_CONV, CONV_DIM))
    small.add("qkv_b", (1, QKV_DIM))
    small.add("o_b", (1, D))
    parts_small = _all_gather(small.pack(g_small, F32), "gather_small_grads")
    zeros_tiny = dict(conv_w=jnp.zeros((SSM_CONV, CONV_DIM), F32), qkv_b=jnp.zeros((1, QKV_DIM), F32), o_b=jnp.zeros((1, D), F32))
    pks = lambda S: small.pack({**{name: S[name] for name in _REPLICATED}, **zeros_tiny}, F32)
    res_small = [small.unpack(r) for r in _adamw(parts_small, pks(W), pks(M), pks(V), "adamw_small")]
    g_small_sum = res_small[0]

    n_cw, n_qb, n_ob = W["conv_w"].shape[2], W["qkv_b"].shape[1], W["o_b"].shape[1]
    g_tiny = dict(conv_w=lax.dynamic_slice_in_dim(g_small_sum["conv_w"], me * n_cw, n_cw, axis=1)[None],
                  qkv_b=lax.dynamic_slice_in_dim(g_small_sum["qkv_b"], me * n_qb, n_qb, axis=1),
                  o_b=lax.dynamic_slice_in_dim(g_small_sum["o_b"], me * n_ob, n_ob, axis=1))
    tiny = _Pack(D, 8, 8)
    for name in _TINY_SHARDED:
        tiny.add(name, W[name].shape)
    pkt = lambda S: tiny.pack({name: S[name] for name in _TINY_SHARDED}, F32)
    res_tiny = [tiny.unpack(r) for r in _adamw(pkt(g_tiny)[None], pkt(W), pkt(M), pkt(V), "adamw_tiny")]

    dmod_all = parts_small[:, small.entries[0][1]:small.entries[0][1] + small.entries[0][2]].reshape(N_DEV, DEPTH, 6 * D)
    dmod_cols = jnp.transpose(lax.dynamic_slice_in_dim(dmod_all, me * ncol, ncol, axis=2), (1, 0, 2))
    g_ada_w = _ada_w_grad(cond, dmod_cols, "ada_w_grad")
    flat = lambda a: a.reshape(DEPTH * D, ncol)
    res_ada = [r.reshape(DEPTH, D, ncol) for r in _adamw(flat(g_ada_w)[None], flat(W["ada_w"]), flat(M["ada_w"]), flat(V["ada_w"]), "adamw_ada_w")]

    def result(kind_idx, name):
        if name == "ada_w":
            return res_ada[kind_idx]
        if name in _REPLICATED:
            return res_small[kind_idx][name]
        if name in _TINY_SHARDED:
            return res_tiny[kind_idx][name]
        pieces = [res_big[kind_idx][key] for key, nm, layer, kind in _BIG if nm == name]
        return jnp.stack(pieces)

    outs = [loss, grad_x[None]]
    for kind_idx in range(4):
        outs += [result(kind_idx, name) for name in _WEIGHTS]
    return tuple(outs)


def kernel(x, c, ada_w, ada_b, norm_mix_w, norm_ffn_w, in_w_even, conv_w, conv_b, dt_bias, a_log, d_skip, ssm_norm_w, gmlp_ln_w, gmlp_ln_b, gmlp_ws, gmlp_bs, out_w_even, qkv_w, qkv_b, o_w, o_b, sinks, rel_table, ffn_gate_w, ffn_up_w, ffn_down_w, final_norm_w, loss_target, m_ada_w, m_ada_b, m_norm_mix_w, m_norm_ffn_w, m_in_w_even, m_conv_w, m_conv_b, m_dt_bias, m_a_log, m_d_skip, m_ssm_norm_w, m_gmlp_ln_w, m_gmlp_ln_b, m_gmlp_ws, m_gmlp_bs, m_out_w_even, m_qkv_w, m_qkv_b, m_o_w, m_o_b, m_sinks, m_rel_table, m_ffn_gate_w, m_ffn_up_w, m_ffn_down_w, m_final_norm_w, v_ada_w, v_ada_b, v_norm_mix_w, v_norm_ffn_w, v_in_w_even, v_conv_w, v_conv_b, v_dt_bias, v_a_log, v_d_skip, v_ssm_norm_w, v_gmlp_ln_w, v_gmlp_ln_b, v_gmlp_ws, v_gmlp_bs, v_out_w_even, v_qkv_w, v_qkv_b, v_o_w, v_o_b, v_sinks, v_rel_table, v_ffn_gate_w, v_ffn_up_w, v_ffn_down_w, v_final_norm_w):
    args = locals()
    W = {n: args[n] for n in _WEIGHTS}
    M = {n: args["m_" + n] for n in _WEIGHTS}
    V = {n: args["v_" + n] for n in _WEIGHTS}
    return _step(x, c, loss_target, W, M, V)
```

```python
import functools
import math

import numpy as np
import jax
import jax.numpy as jnp
from jax import lax
from jax.experimental import pallas as pl
from jax.experimental.pallas import tpu as pltpu

F32 = jnp.float32
BF16 = jnp.bfloat16
HIGHEST = lax.Precision.HIGHEST
MESH = pl.DeviceIdType.MESH

N_DEV = 8
D = 1024
DEPTH = 2
SSM_HEADS = 16
SSM_HEAD_DIM = 64
SSM_INNER = 1024
SSM_GROUPS = 2
SSM_STATE = 128
SSM_CONV = 4
CHUNK = 128
CONV_DIM = SSM_INNER + 2 * SSM_GROUPS * SSM_STATE
GMLP_GROUPS = 8
GMLP_INNER = 1024
IN_EVEN = 4624
ATTN_HEADS = 16
ATTN_KV = 2
ATTN_DH = 64
QKV_DIM = 1280
REL_BUCKETS = 32
REL_MAX_DIST = 128
FFN = 2816
EPS = 1e-6
NEG_INF = -1e30
LANES = 128

ADAM_LR = 0.001
ADAM_B1 = 0.9
ADAM_B2 = 0.999
ADAM_EPS = 1e-08
ADAM_WD = 0.01
ADAM_STEP = 10

VMEM_LIMIT_BYTES = 56 * 1024 * 1024
ROW_TILE = 512


def _pcall(body, *, name, out_shape, grid=(), in_specs=None, out_specs=None, scratch=(), sem=None):
    params = dict(vmem_limit_bytes=VMEM_LIMIT_BYTES)
    if sem is not None:
        params["dimension_semantics"] = sem
    specs = {} if in_specs is None else dict(in_specs=in_specs, out_specs=out_specs)
    return pl.pallas_call(
        body, name=name, out_shape=out_shape, grid=grid, **specs,
        scratch_shapes=list(scratch), compiler_params=pltpu.CompilerParams(**params))


def _tile(n, pref):
    if n <= pref:
        return n
    best = None
    for t in range(LANES, pref + 1, LANES):
        if n % t == 0:
            best = t
    assert best is not None, (n, pref)
    return best


def _rows(T):
    return min(ROW_TILE, T)


def _sds(shape, dtype=F32):
    return jax.ShapeDtypeStruct(shape, dtype)


def _row_spec(tm, c, col=0):
    return pl.BlockSpec((tm, c), lambda i, col=col: (i, col))


def _vec_spec(c, r=1):
    return pl.BlockSpec((r, c), lambda i: (0, 0))


def _sigmoid(x):
    return jax.nn.sigmoid(x)


def _silu(x):
    return x * _sigmoid(x)


def _dsilu(x):
    s = _sigmoid(x)
    return s * (1.0 + x * (1.0 - s))


def _gelu(x):
    return 0.5 * x * (1.0 + lax.erf(x * 0.7071067811865476))


def _dgelu(x):
    return 0.5 * (1.0 + lax.erf(x * 0.7071067811865476)) + x * jnp.exp(-0.5 * x * x) * 0.3989422804014327


def _dot(a, b, dims, precision=None):
    return lax.dot_general(a, b, (dims, ((), ())), precision=precision, preferred_element_type=F32)


def _nn(a, b, precision=None):
    return _dot(a, b, ((1,), (0,)), precision)


def _nt(a, b, precision=None):
    return _dot(a, b, ((1,), (1,)), precision)


def _tn(a, b, precision=None):
    return _dot(a, b, ((0,), (0,)), precision)


def _bf(x):
    return x.astype(BF16)


def _colsum(x):
    return jnp.sum(x, axis=0, keepdims=True)


def _rowsum(x):
    return jnp.sum(x, axis=1, keepdims=True)


def _allsum(x):
    return _colsum(_rowsum(x))


def _all_gather(x, name):
    R, C = x.shape

    def body(x_ref, out_ref, send_sems, recv_sems, local_sem):
        mx, my, mc = lax.axis_index("x"), lax.axis_index("y"), lax.axis_index("c")
        me, sibling = (mx, my, mc), (mx, my, 1 - mc)
        chips = [(1 - mx, my), (mx, 1 - my), (1 - mx, 1 - my)]

        def slot(px, py, pc):
            return out_ref.at[4 * px + 2 * py + pc]

        def copy(k, block, to, src=None):
            return pltpu.make_async_remote_copy(
                src_ref=slot(*block) if src is None else src, dst_ref=slot(*block),
                send_sem=send_sems.at[k], recv_sem=recv_sems.at[k], device_id=to, device_id_type=MESH)

        mine = pltpu.make_async_copy(x_ref, slot(*me), local_sem)
        mine.start()
        first = [copy(0, me, sibling, src=x_ref)]
        first += [copy(1 + j, me, (*chip, mc), src=x_ref) for j, chip in enumerate(chips)]
        for cp in first:
            cp.start()
        passed = [copy(4 + j, (*chip, mc), sibling) for j, chip in enumerate(chips)]
        for j, chip in enumerate(chips):
            copy(1 + j, (*chip, mc), me).wait_recv()
            passed[j].start()
        copy(0, sibling, me).wait_recv()
        for j, chip in enumerate(chips):
            copy(4 + j, (*chip, 1 - mc), me).wait_recv()
        for cp in first + passed:
            cp.wait_send()
        mine.wait()

    return pl.pallas_call(
        body, name=name, out_shape=_sds((N_DEV, R, C), x.dtype),
        in_specs=[pl.BlockSpec(memory_space=pl.ANY)], out_specs=pl.BlockSpec(memory_space=pl.ANY),
        scratch_shapes=[pltpu.SemaphoreType.DMA((7,)), pltpu.SemaphoreType.DMA((7,)), pltpu.SemaphoreType.DMA(())],
    )(x)


N_CHIP = 4


def _exchange_sibling(p, name):
    _, R, C = p.shape

    def body(p_ref, theirs_ref, send_sems, recv_sems):
        mx, my, mc = lax.axis_index("x"), lax.axis_index("y"), lax.axis_index("c")
        copies = []
        for chip in range(N_CHIP):
            cp = pltpu.make_async_remote_copy(
                src_ref=p_ref.at[2 * chip + 1 - mc], dst_ref=theirs_ref.at[chip],
                send_sem=send_sems.at[chip], recv_sem=recv_sems.at[chip],
                device_id=(mx, my, 1 - mc), device_id_type=MESH)
            cp.start()
            copies.append(cp)
        for cp in copies:
            cp.wait()

    return pl.pallas_call(
        body, name=name, out_shape=_sds((N_CHIP, R, C), p.dtype),
        in_specs=[pl.BlockSpec(memory_space=pl.ANY)], out_specs=pl.BlockSpec(memory_space=pl.ANY),
        scratch_shapes=[pltpu.SemaphoreType.DMA((N_CHIP,))] * 2,
    )(p)


def _exchange_chips(q, name):
    _, R, C = q.shape

    def body(q_ref, out_ref, send_sems, recv_sems, local_sem):
        mx, my, mc = lax.axis_index("x"), lax.axis_index("y"), lax.axis_index("c")
        me = 2 * mx + my
        local = pltpu.make_async_copy(q_ref.at[me], out_ref.at[me], local_sem)
        local.start()
        copies = []
        for r in range(1, N_CHIP):
            px = 1 - mx if r & 2 else mx
            py = 1 - my if r & 1 else my
            cp = pltpu.make_async_remote_copy(
                src_ref=q_ref.at[2 * px + py], dst_ref=out_ref.at[me],
                send_sem=send_sems.at[r - 1], recv_sem=recv_sems.at[r - 1],
                device_id=(px, py, mc), device_id_type=MESH)
            cp.start()
            copies.append(cp)
        for cp in copies:
            cp.wait()
        local.wait()

    return pl.pallas_call(
        body, name=name, out_shape=_sds((N_CHIP, R, C), q.dtype),
        in_specs=[pl.BlockSpec(memory_space=pl.ANY)], out_specs=pl.BlockSpec(memory_space=pl.ANY),
        scratch_shapes=[pltpu.SemaphoreType.DMA((3,)), pltpu.SemaphoreType.DMA((3,)), pltpu.SemaphoreType.DMA(())],
    )(q)


def _pair_sum(p, theirs, name):
    n, R, C = theirs.shape
    tr = _tile_rows(R, 256)

    def body(p_ref, t_ref, o_ref):
        mc = lax.axis_index("c")
        o_ref[0] = (p_ref[0, mc] + t_ref[0]).astype(BF16)

    blk = pl.BlockSpec((1, tr, C), lambda s, i: (s, i, 0))
    return _pcall(body, name=name, out_shape=_sds((n, R, C), BF16), grid=(n, R // tr),
                  in_specs=[pl.BlockSpec((1, 2, tr, C), lambda s, i: (s, 0, i, 0)), blk],
                  out_specs=blk, sem=("parallel", "parallel"))(p.reshape(n, 2, R, C), theirs)


def _sum_parts(parts, name):
    P, R, C = parts.shape
    tr = _tile_rows(R, 256)

    def body(p_ref, o_ref):
        g = p_ref[0].astype(F32)
        for k in range(1, P):
            g = g + p_ref[k].astype(F32)
        o_ref[...] = g

    return _pcall(body, name=name, out_shape=_sds((R, C)), grid=(R // tr,),
                  in_specs=[pl.BlockSpec((P, tr, C), lambda i: (0, i, 0))],
                  out_specs=pl.BlockSpec((tr, C), lambda i: (i, 0)), sem=("parallel",))(parts)


def _adamw(parts, w, m, v, name):
    P, R, C = parts.shape
    tr = R if R <= 256 else _tile_rows(R, 256)

    def body(p_ref, w_ref, m_ref, v_ref, g_ref, d_ref, nm_ref, nv_ref):
        g = p_ref[0]
        for k in range(1, P):
            g = g + p_ref[k]
        nm = ADAM_B1 * m_ref[...] + (1.0 - ADAM_B1) * g
        nv = ADAM_B2 * v_ref[...] + (1.0 - ADAM_B2) * (g * g)
        m_hat = nm / (1.0 - ADAM_B1 ** ADAM_STEP)
        v_hat = nv / (1.0 - ADAM_B2 ** ADAM_STEP)
        g_ref[...] = g
        d_ref[...] = -ADAM_LR * (m_hat / (jnp.sqrt(v_hat) + ADAM_EPS) + ADAM_WD * w_ref[...])
        nm_ref[...] = nm
        nv_ref[...] = nv

    blk = pl.BlockSpec((tr, C), lambda i: (i, 0))
    return _pcall(
        body, name=name, out_shape=[_sds((R, C))] * 4, grid=(R // tr,),
        in_specs=[pl.BlockSpec((P, tr, C), lambda i: (0, i, 0)), blk, blk, blk],
        out_specs=[blk] * 4, sem=("parallel",))(parts, w, m, v)


def _tile_rows(n, pref):
    best = None
    for t in range(8, pref + 1, 8):
        if n % t == 0:
            best = t
    assert best is not None, (n, pref)
    return best


def _mm(pairs, mode, *, name, out_dtype=F32, bias=None, resid=None, gvec=None, keep=False, tn_pref=1024):
    M = pairs[0][0].shape[0]
    N = pairs[0][1].shape[1] if mode == "nn" else pairs[0][1].shape[0]
    tm, tn = _rows(M), _tile(N, tn_pref)
    n_pairs = len(pairs)
    has_bias, has_res = bias is not None, resid is not None

    def body(*refs):
        ab = refs[:2 * n_pairs]
        pos = 2 * n_pairs
        b_ref = refs[pos] if has_bias else None
        pos += has_bias
        r_ref, g_ref = (refs[pos], refs[pos + 1]) if has_res else (None, None)
        pos += 2 * has_res
        outs = refs[pos:]
        acc = None
        for p in range(n_pairs):
            a, b = _bf(ab[2 * p][...]), _bf(ab[2 * p + 1][...])
            d = _nn(a, b) if mode == "nn" else _nt(a, b)
            acc = d if acc is None else acc + d
        if has_bias:
            acc = acc + b_ref[...]
        if has_res:
            outs[0][...] = (r_ref[...] + g_ref[...] * acc).astype(outs[0].dtype)
            if keep:
                outs[1][...] = acc
        else:
            outs[0][...] = acc.astype(outs[0].dtype)

    in_specs, args = [], []
    for a, b in pairs:
        K = a.shape[1]
        in_specs.append(pl.BlockSpec((tm, K), lambda j, i: (i, 0)))
        if mode == "nn":
            in_specs.append(pl.BlockSpec((K, tn), lambda j, i: (0, j)))
        else:
            in_specs.append(pl.BlockSpec((tn, K), lambda j, i: (j, 0)))
        args += [a, b]
    if has_bias:
        in_specs.append(pl.BlockSpec((1, tn), lambda j, i: (0, j)))
        args.append(bias)
    if has_res:
        in_specs.append(pl.BlockSpec((tm, tn), lambda j, i: (i, j)))
        in_specs.append(pl.BlockSpec((1, tn), lambda j, i: (0, j)))
        args += [resid, gvec]
    o_spec = pl.BlockSpec((tm, tn), lambda j, i: (i, j))
    n_out = 2 if (has_res and keep) else 1
    out_shape = [_sds((M, N), out_dtype)] + ([_sds((M, N), F32)] if n_out == 2 else [])
    res = _pcall(body, name=name, out_shape=out_shape, grid=(N // tn, M // tm), in_specs=in_specs,
                 out_specs=[o_spec] * n_out, sem=("parallel", "parallel"))(*args)
    return res if n_out == 2 else res[0]


def _mm_tn(a, b, *, name, tm_pref=1408, tn_pref=1536):
    K, M = a.shape
    N = b.shape[1]
    tm, tn = _tile(M, tm_pref), _tile(N, tn_pref)
    tk = K if K <= ROW_TILE else ROW_TILE

    def body(a_ref, b_ref, o_ref):
        @pl.when(pl.program_id(2) == 0)
        def _():
            o_ref[...] = jnp.zeros_like(o_ref)
        o_ref[...] += _tn(_bf(a_ref[...]), _bf(b_ref[...]))

    return _pcall(
        body, name=name, out_shape=_sds((M, N)), grid=(M // tm, N // tn, K // tk),
        in_specs=[pl.BlockSpec((tk, tm), lambda i, j, k: (k, i)), pl.BlockSpec((tk, tn), lambda i, j, k: (k, j))],
        out_specs=pl.BlockSpec((tm, tn), lambda i, j, k: (i, j)),
        sem=("parallel", "parallel", "arbitrary"))(a, b)


def _mm_swiglu(h, wg_t, wu_t, name):
    M, K = h.shape
    N = wg_t.shape[0]
    tm, tn = _rows(M), _tile(N, 1408)

    def body(h_ref, wg_ref, wu_ref, gate_ref, up_ref, act_ref):
        hv = _bf(h_ref[...])
        gate = _nt(hv, wg_ref[...])
        up = _nt(hv, wu_ref[...])
        gate_ref[...] = gate
        up_ref[...] = up
        act_ref[...] = (_silu(gate) * up).astype(BF16)

    w_spec = pl.BlockSpec((tn, K), lambda j, i: (j, 0))
    o_spec = pl.BlockSpec((tm, tn), lambda j, i: (i, j))
    return _pcall(
        body, name=name, out_shape=[_sds((M, N)), _sds((M, N)), _sds((M, N), BF16)], grid=(N // tn, M // tm),
        in_specs=[pl.BlockSpec((tm, K), lambda j, i: (i, 0)), w_spec, w_spec], out_specs=[o_spec] * 3,
        sem=("parallel", "parallel"))(h, wg_t, wu_t)


def _mm_swiglu_bwd(dout, wd, gate, up, name):
    M, K = dout.shape
    N = wd.shape[0]
    tm, tn = _rows(M), _tile(N, 1408)

    def body(d_ref, wd_ref, gate_ref, up_ref, dg_ref, du_ref):
        dact = _nt(_bf(d_ref[...]), wd_ref[...])
        g = gate_ref[...]
        dg_ref[...] = (dact * up_ref[...] * _dsilu(g)).astype(BF16)
        du_ref[...] = (dact * _silu(g)).astype(BF16)

    t_spec = pl.BlockSpec((tm, tn), lambda j, i: (i, j))
    return _pcall(
        body, name=name, out_shape=[_sds((M, N), BF16)] * 2, grid=(N // tn, M // tm),
        in_specs=[pl.BlockSpec((tm, K), lambda j, i: (i, 0)), pl.BlockSpec((tn, K), lambda j, i: (j, 0)), t_spec, t_spec],
        out_specs=[t_spec] * 2, sem=("parallel", "parallel"))(dout, wd, gate, up)


def _norm_mod(x, w, sc, sh, name):
    T = x.shape[0]
    tm = _rows(T)

    def body(x_ref, w_ref, sc_ref, sh_ref, o_ref):
        xv = x_ref[...]
        r = lax.rsqrt(jnp.mean(xv * xv, axis=-1, keepdims=True) + EPS)
        o_ref[...] = ((xv * r * w_ref[...]) * (1.0 + sc_ref[...]) + sh_ref[...]).astype(BF16)

    return _pcall(body, name=name, out_shape=_sds((T, D), BF16), grid=(T // tm,),
                  in_specs=[_row_spec(tm, D), _vec_spec(D), _vec_spec(D), _vec_spec(D)],
                  out_specs=_row_spec(tm, D), sem=("parallel",))(x, w, sc, sh)


def _norm_mod_bwd(x, dh, dres, w, sc, name):
    T = x.shape[0]
    tm = _rows(T)

    def body(x_ref, dh_ref, dres_ref, w_ref, sc_ref, dx_ref, acc_ref):
        @pl.when(pl.program_id(0) == 0)
        def _():
            acc_ref[...] = jnp.zeros_like(acc_ref)
        xv, dh_v, wv = x_ref[...], dh_ref[...], w_ref[...]
        r = lax.rsqrt(jnp.mean(xv * xv, axis=-1, keepdims=True) + EPS)
        n = xv * r
        dnw = dh_v * (1.0 + sc_ref[...])
        dn = dnw * wv
        dx_ref[...] = dres_ref[...] + r * (dn - n * jnp.mean(dn * n, axis=-1, keepdims=True))
        acc_ref[0:1, :] += _colsum(dh_v * (n * wv))
        acc_ref[1:2, :] += _colsum(dh_v)
        acc_ref[2:3, :] += _colsum(dnw * n)

    return _pcall(body, name=name, out_shape=[_sds((T, D)), _sds((8, D))], grid=(T // tm,),
                  in_specs=[_row_spec(tm, D), _row_spec(tm, D), _row_spec(tm, D), _vec_spec(D), _vec_spec(D)],
                  out_specs=[_row_spec(tm, D), _vec_spec(D, 8)], sem=("arbitrary",))(x, dh, dres, w, sc)


def _gate_bwd(dx, branch, g, name):
    T = dx.shape[0]
    tm = _rows(T)

    def body(dx_ref, br_ref, g_ref, db_ref, acc_ref):
        @pl.when(pl.program_id(0) == 0)
        def _():
            acc_ref[...] = jnp.zeros_like(acc_ref)
        dxv = dx_ref[...]
        db = g_ref[...] * dxv
        db_ref[...] = db.astype(BF16)
        acc_ref[0:1, :] += _colsum(dxv * br_ref[...])
        acc_ref[1:2, :] += _colsum(db)

    return _pcall(body, name=name, out_shape=[_sds((T, D), BF16), _sds((8, D))], grid=(T // tm,),
                  in_specs=[_row_spec(tm, D), _row_spec(tm, D), _vec_spec(D)],
                  out_specs=[_row_spec(tm, D), _vec_spec(D, 8)], sem=("arbitrary",))(dx, branch, g)


def _final_loss(x, wf, target, name):
    T = x.shape[0]
    tm = _rows(T)

    def body(x_ref, w_ref, t_ref, dx_ref, acc_ref):
        @pl.when(pl.program_id(0) == 0)
        def _():
            acc_ref[...] = jnp.zeros_like(acc_ref)
        xv, wv = x_ref[...], w_ref[...]
        r = lax.rsqrt(jnp.mean(xv * xv, axis=-1, keepdims=True) + EPS)
        n = xv * r
        err = n * wv - t_ref[...]
        dy = err * (1.0 / D)
        dn = dy * wv
        dx_ref[...] = r * (dn - n * jnp.mean(dn * n, axis=-1, keepdims=True))
        acc_ref[0:1, :] += _colsum(dy * n)
        acc_ref[1:2, :] += jnp.broadcast_to(_allsum(err * err) * (0.5 / D), (1, D))

    return _pcall(body, name=name, out_shape=[_sds((T, D)), _sds((8, D))], grid=(T // tm,),
                  in_specs=[_row_spec(tm, D), _vec_spec(D), _row_spec(tm, D)],
                  out_specs=[_row_spec(tm, D), _vec_spec(D, 8)], sem=("arbitrary",))(x, wf, target)


def _colsum_call(x, name):
    T, C = x.shape
    tm = _rows(T)

    def body(x_ref, o_ref):
        @pl.when(pl.program_id(0) == 0)
        def _():
            o_ref[...] = jnp.zeros_like(o_ref)
        o_ref[...] += _colsum(x_ref[...].astype(F32))

    return _pcall(body, name=name, out_shape=_sds((1, C)), grid=(T // tm,), in_specs=[_row_spec(tm, C)],
                  out_specs=_vec_spec(C), sem=("arbitrary",))(x)


def _mod_matmul(c_all, ada_w, name):
    n = ada_w.shape[2]

    def body(c_ref, w_ref, cond_ref, o_ref):
        cond = _silu(c_ref[...])
        cond_ref[...] = cond
        o_ref[0] = _nn(cond, w_ref[0])

    return _pcall(body, name=name, out_shape=[_sds((N_DEV, D)), _sds((DEPTH, N_DEV, n))], grid=(DEPTH,),
                  in_specs=[pl.BlockSpec((N_DEV, D), lambda l: (0, 0)), pl.BlockSpec((1, D, n), lambda l: (l, 0, 0))],
                  out_specs=[pl.BlockSpec((N_DEV, D), lambda l: (0, 0)), pl.BlockSpec((1, N_DEV, n), lambda l: (l, 0, 0))],
                  sem=("arbitrary",))(c_all, ada_w)


def _add_rows(a, b, name):
    def body(a_ref, b_ref, o_ref):
        o_ref[...] = a_ref[...] + b_ref[...]

    return _pcall(body, name=name, out_shape=_sds(a.shape))(a, b)


def _ada_w_grad(cond, dmod_cols, name):
    n = dmod_cols.shape[2]

    def body(c_ref, d_ref, o_ref):
        o_ref[0] = _tn(c_ref[...], d_ref[0])

    return _pcall(body, name=name, out_shape=_sds((DEPTH, D, n)), grid=(DEPTH,),
                  in_specs=[pl.BlockSpec((N_DEV, D), lambda l: (0, 0)), pl.BlockSpec((1, N_DEV, n), lambda l: (l, 0, 0))],
                  out_specs=pl.BlockSpec((1, D, n), lambda l: (l, 0, 0)), sem=("parallel",))(cond, dmod_cols)


def _conv_fwd(pm, conv_w, conv_b, name):
    T = pm.shape[0]
    tm = _rows(T)
    C = CONV_DIM

    def body(x_ref, prev_ref, w_ref, b_ref, o_ref):
        cur = x_ref[...]
        prev = jnp.where(pl.program_id(0) > 0, prev_ref[...], 0.0)
        cur8 = cur[0:8]
        row8 = lax.broadcasted_iota(jnp.int32, (8, C), 0)
        full = w_ref[3:4, :] * cur
        head = w_ref[3:4, :] * cur8
        for k in range(1, SSM_CONV):
            wk = w_ref[3 - k:4 - k, :]
            full = full + wk * pltpu.roll(cur, k, 0)
            head = head + wk * jnp.where(row8 < k, pltpu.roll(prev, k, 0), pltpu.roll(cur8, k, 0))
        o_ref[...] = full + b_ref[...]
        o_ref[0:8, :] = head + b_ref[...]

    return _pcall(
        body, name=name, out_shape=_sds((T, C)), grid=(T // tm,),
        in_specs=[pl.BlockSpec((tm, C), lambda i: (i, 2)),
                  pl.BlockSpec((8, C), lambda i: (jnp.maximum(i * (tm // 8) - 1, 0), 2)),
                  _vec_spec(C, SSM_CONV), _vec_spec(C)],
        out_specs=_row_spec(tm, C), sem=("parallel",))(pm, pm, conv_w, conv_b)


def _conv_bwd(dc, pm, conv_w, name):
    T = dc.shape[0]
    tm = _rows(T)
    C = CONV_DIM
    nt = T // tm

    def body(dc_ref, nxt_ref, x_ref, prev_ref, w_ref, dx_ref, acc_ref):
        i = pl.program_id(0)

        @pl.when(i == 0)
        def _():
            acc_ref[...] = jnp.zeros_like(acc_ref)
        dcv = dc_ref[...]
        nxt = jnp.where(i < nt - 1, nxt_ref[...], 0.0)
        xc = x_ref[...]
        prev = jnp.where(i > 0, prev_ref[...], 0.0)
        dc8h, dc8t, x8 = dcv[0:8], dcv[tm - 8:tm], xc[0:8]
        row8 = lax.broadcasted_iota(jnp.int32, (8, C), 0)
        full = w_ref[3:4, :] * dcv
        tail = w_ref[3:4, :] * dc8t
        acc_ref[3:4, :] += _colsum(dcv * xc)
        for k in range(1, SSM_CONV):
            wk = w_ref[3 - k:4 - k, :]
            full = full + wk * pltpu.roll(dcv, tm - k, 0)
            tail = tail + wk * jnp.where(row8 + k >= 8, pltpu.roll(nxt, 8 - k, 0), pltpu.roll(dc8t, 8 - k, 0))
            xs_head = jnp.where(row8 < k, pltpu.roll(prev, k, 0), pltpu.roll(x8, k, 0))
            prod = dcv * pltpu.roll(xc, k, 0)
            acc_ref[3 - k:4 - k, :] += _colsum(prod) - _colsum(prod[0:8]) + _colsum(dc8h * xs_head)
        acc_ref[4:5, :] += _colsum(dcv)
        dx_ref[...] = full
        dx_ref[tm - 8:tm, :] = tail

    return _pcall(
        body, name=name, out_shape=[_sds((T, C)), _sds((8, C))], grid=(nt,),
        in_specs=[_row_spec(tm, C),
                  pl.BlockSpec((8, C), lambda i: (jnp.minimum((i + 1) * (tm // 8), T // 8 - 1), 0)),
                  pl.BlockSpec((tm, C), lambda i: (i, 2)),
                  pl.BlockSpec((8, C), lambda i: (jnp.maximum(i * (tm // 8) - 1, 0), 2)),
                  _vec_spec(C, SSM_CONV)],
        out_specs=[_row_spec(tm, C), _vec_spec(C, 8)], sem=("arbitrary",))(dc, dc, pm, pm, conv_w)


def _ssd_prologue(cpre, dtr, dtb, alog):
    L = CHUNK
    xc = _silu(cpre)
    pre = dtr + dtb
    dt = jnp.maximum(pre, 0.0) + jnp.log1p(jnp.exp(-jnp.abs(pre)))
    a = -jnp.exp(alog)
    la = dt * a
    row = lax.broadcasted_iota(jnp.int32, (L, L), 0)
    col = lax.broadcasted_iota(jnp.int32, (L, L), 1)
    causal = row >= col
    tri = causal.astype(F32)
    lc = _nn(tri, la, HIGHEST)
    return xc, pre, dt, a, causal, tri, lc, row, col


def _head_indicator():
    m = np.zeros((LANES, SSM_INNER), np.float32)
    for h in range(SSM_HEADS):
        m[h, h * SSM_HEAD_DIM:(h + 1) * SSM_HEAD_DIM] = 1.0
    return jnp.asarray(m, dtype=BF16)


def _split_dot(x, ind, dims):
    hi = x.astype(BF16)
    lo = (x - hi.astype(F32)).astype(BF16)
    return _dot(hi, ind, dims) + _dot(lo, ind, dims)


def _expand(x16, ind):
    return _split_dot(x16, ind, ((1,), (0,)))


def _headsum(x, ind):
    return _split_dot(x, ind, ((1,), (1,)))


def _ssd_fwd(cpre, dtr, pm, dtb, alog, dskip, normw, ind, name):
    T = cpre.shape[0]
    nc = T // CHUNK
    L, P, H, HPG, N = CHUNK, SSM_HEAD_DIM, SSM_HEADS, SSM_HEADS // SSM_GROUPS, SSM_STATE
    half = SSM_INNER // SSM_GROUPS

    def body(cp_ref, dtr_ref, z_ref, dtb_ref, alog_ref, dskip_ref, nw_ref, ind_ref, ya_ref, y_ref, sp_ref, st_ref):
        @pl.when(pl.program_id(0) == 0)
        def _():
            st_ref[...] = jnp.zeros_like(st_ref)
        xc, _, dt, _, causal, _, lc, _, _ = _ssd_prologue(cp_ref[...], dtr_ref[...], dtb_ref[...], alog_ref[...])
        lct = lc.T
        ind = ind_ref[...]
        llast = lc[L - 1:L, :]
        xs = xc[:, :SSM_INNER]
        xd = xs * _expand(dt, ind)
        ex = _expand(jnp.exp(lc), ind)
        xd_end = _bf(xd * _expand(jnp.exp(llast - lc), ind))
        cdx = _expand(jnp.broadcast_to(jnp.exp(llast), (8, LANES)), ind)[0:1]
        xdb = _bf(xd)
        sp_ref[0] = st_ref[...]
        for g in range(SSM_GROUPS):
            sl = slice(g * half, (g + 1) * half)
            bm = _bf(xc[:, SSM_INNER + g * N:SSM_INNER + (g + 1) * N])
            cm = _bf(xc[:, SSM_INNER + (SSM_GROUPS + g) * N:SSM_INNER + (SSM_GROUPS + g + 1) * N])
            cb = _nt(cm, bm)
            st = st_ref[g]
            y_ref[:, sl] = ex[:, sl] * _nn(cm, _bf(st)) + dskip_ref[:, sl] * xs[:, sl]
            st_ref[g] = st * cdx[:, sl] + _tn(bm, xd_end[:, sl])
            for j in range(HPG):
                h = g * HPG + j
                decay = jnp.where(causal, jnp.exp(jnp.where(causal, lc[:, h:h + 1] - lct[h:h + 1, :], 0.0)), 0.0)
                y_ref[:, h * P:(h + 1) * P] += _nn(_bf(cb * decay), xdb[:, h * P:(h + 1) * P])
        y2 = y_ref[...] * _silu(z_ref[...])
        for g in range(SSM_GROUPS):
            yg = y2[:, g * half:(g + 1) * half]
            r = lax.rsqrt(jnp.mean(yg * yg, axis=-1, keepdims=True) + EPS)
            ya_ref[:, g * half:(g + 1) * half] = (yg * r * nw_ref[:, g * half:(g + 1) * half]).astype(BF16)

    return _pcall(
        body, name=name,
        out_shape=[_sds((T, SSM_INNER), BF16), _sds((T, SSM_INNER)), _sds((nc, SSM_GROUPS, N, half))], grid=(nc,),
        in_specs=[_row_spec(L, CONV_DIM), _row_spec(L, LANES), _row_spec(L, SSM_INNER, 0),
                  _vec_spec(LANES), _vec_spec(LANES), _vec_spec(SSM_INNER), _vec_spec(SSM_INNER), _vec_spec(SSM_INNER, LANES)],
        out_specs=[_row_spec(L, SSM_INNER), _row_spec(L, SSM_INNER),
                   pl.BlockSpec((1, SSM_GROUPS, N, half), lambda i: (i, 0, 0, 0))],
        scratch=[pltpu.VMEM((SSM_GROUPS, N, half), F32)], sem=("arbitrary",))(cpre, dtr, pm, dtb, alog, dskip, normw, ind)


def _ssd_bwd(cpre, dtr, pm, ypre, sprev, dya, dtb, alog, dskip, normw, ind, name):
    T = cpre.shape[0]
    nc = T // CHUNK
    L, P, H, HPG, N = CHUNK, SSM_HEAD_DIM, SSM_HEADS, SSM_HEADS // SSM_GROUPS, SSM_STATE
    half = SSM_INNER // SSM_GROUPS

    def body(cp_ref, dtr_ref, z_ref, y_ref, sp_ref, dya_ref, dtb_ref, alog_ref, dskip_ref, nw_ref, ind_ref,
             dz_ref, dcp_ref, ddtr_ref, acc_ref, dnw_ref, ds_ref, dy_ref, dxd_ref, rr_ref, yoff_ref, dcd_ref):
        @pl.when(pl.program_id(0) == 0)
        def _():
            ds_ref[...] = jnp.zeros_like(ds_ref)
            acc_ref[...] = jnp.zeros_like(acc_ref)
            dnw_ref[...] = jnp.zeros_like(dnw_ref)
        cpre_v = cp_ref[...]
        xc, pre, dt, a, causal, tri, lc, row, col = _ssd_prologue(cpre_v, dtr_ref[...], dtb_ref[...], alog_ref[...])
        lct = lc.T
        zv, yv = z_ref[...], y_ref[...]
        sz = _silu(zv)
        y2 = yv * sz
        dya_v = dya_ref[...]
        nwv = nw_ref[...]
        for g in range(SSM_GROUPS):
            sl = slice(g * half, (g + 1) * half)
            yg = y2[:, sl]
            r = lax.rsqrt(jnp.mean(yg * yg, axis=-1, keepdims=True) + EPS)
            nrm = yg * r
            dnw_ref[:, sl] += _colsum(dya_v[:, sl] * nrm)
            dn = dya_v[:, sl] * nwv[:, sl]
            dy2 = r * (dn - nrm * jnp.mean(dn * nrm, axis=-1, keepdims=True))
            dy_ref[:, sl] = dy2 * sz[:, sl]
            dz_ref[:, sl] = dy2 * yv[:, sl] * _dsilu(zv[:, sl])
        ind = ind_ref[...]
        llast = lc[L - 1:L, :]
        dte16 = jnp.exp(llast - lc)
        cd16 = jnp.exp(llast)
        xs = xc[:, :SSM_INNER]
        dtx = _expand(dt, ind)
        ex = _expand(jnp.exp(lc), ind)
        dtex = _expand(dte16, ind)
        cdx = _expand(jnp.broadcast_to(cd16, (8, LANES)), ind)[0:1]
        xd = xs * dtx
        xdb = _bf(xd)
        xd_end = _bf(xd * dtex)
        dyv = dy_ref[...]
        dy_off = _bf(ex * dyv)
        dyb = _bf(dyv)
        dskx = dskip_ref[...]
        lane_c = lax.broadcasted_iota(jnp.int32, (L, LANES), 1)
        lane1 = lax.broadcasted_iota(jnp.int32, (1, LANES), 1)
        sub16 = lax.broadcasted_iota(jnp.int32, (H, L), 0)
        dlc_c = jnp.zeros((L, LANES), F32)
        dlc_r = jnp.zeros((H, L), F32)
        for g in range(SSM_GROUPS):
            sl = slice(g * half, (g + 1) * half)
            b_lo = SSM_INNER + g * N
            c_lo = SSM_INNER + (SSM_GROUPS + g) * N
            bm, cm = _bf(xc[:, b_lo:b_lo + N]), _bf(xc[:, c_lo:c_lo + N])
            cb = _nt(cm, bm)
            st, dst = sp_ref[0, g], ds_ref[g]
            stb, dstb = _bf(st), _bf(dst)
            dcm = _nt(dy_off[:, sl], stb)
            ds_ref[g] = _tn(cm, dy_off[:, sl]) + dst * cdx[:, sl]
            rr_ref[:, sl] = _nn(bm, dstb)
            yoff_ref[:, sl] = ex[:, sl] * _nn(cm, stb)
            db = _nt(xd_end[:, sl], dstb)
            dcd_ref[:, sl] = _colsum(dst * st)
            dcb = jnp.zeros((L, L), F32)
            for j in range(HPG):
                h = g * HPG + j
                hs = slice(h * P, (h + 1) * P)
                decay = jnp.where(causal, jnp.exp(jnp.where(causal, lc[:, h:h + 1] - lct[h:h + 1, :], 0.0)), 0.0)
                m = cb * decay
                dxd_ref[:, hs] = _tn(_bf(m), dyb[:, hs])
                dm = _nt(dyb[:, hs], xdb[:, hs])
                dcb = dcb + dm * decay
                gm = dm * m
                dlc_c = dlc_c + jnp.where(lane_c == h, _rowsum(gm), 0.0)
                dlc_r = dlc_r + jnp.where(sub16 == h, _colsum(gm), 0.0)
            dcbb = _bf(dcb)
            dcp_ref[:, c_lo:c_lo + N] = dcm + _nn(dcbb, bm)
            dcp_ref[:, b_lo:b_lo + N] = db + _tn(dcbb, cm)
        dxd_diag, rr = dxd_ref[...], rr_ref[...]
        tt = _headsum(rr * xd, ind) * dte16
        dlc_rt = jnp.concatenate([dlc_r, jnp.zeros((LANES - H, L), F32)], axis=0).T
        dlc = dlc_c - dlc_rt + _headsum(dyv * yoff_ref[...], ind) - tt
        dcd = _headsum(jnp.broadcast_to(dcd_ref[...], (8, SSM_INNER)), ind)[0:1]
        dlc = dlc + jnp.where(row == L - 1, _colsum(tt) + dcd * cd16, 0.0)
        dla = _tn(tri, dlc, HIGHEST)
        dxd = dxd_diag + dtex * rr
        ddt = _headsum(dxd * xs, ind) + dla * a
        ddtr = jnp.where(lane_c < H, ddt * _sigmoid(pre), 0.0)
        ddtr_ref[...] = ddtr
        acc_ref[0:1, :] += _colsum(ddtr)
        acc_ref[1:2, :] += jnp.where(lane1 < H, _colsum(dla * dt) * a, 0.0)
        acc_ref[2:3, :] += _headsum(jnp.broadcast_to(_colsum(dyv * xs), (8, SSM_INNER)), ind)[0:1]
        dcp_ref[:, 0:SSM_INNER] = dxd * dtx + dskx * dyv
        dcp_ref[...] = dcp_ref[...] * _dsilu(cpre_v)

    rev = lambda i: (nc - 1 - i, 0)
    rspec = lambda c: pl.BlockSpec((L, c), rev)
    return _pcall(
        body, name=name,
        out_shape=[_sds((T, SSM_INNER)), _sds((T, CONV_DIM)), _sds((T, LANES)), _sds((8, LANES)), _sds((1, SSM_INNER))],
        grid=(nc,),
        in_specs=[rspec(CONV_DIM), rspec(LANES), rspec(SSM_INNER), rspec(SSM_INNER),
                  pl.BlockSpec((1, SSM_GROUPS, N, half), lambda i: (nc - 1 - i, 0, 0, 0)), rspec(SSM_INNER),
                  _vec_spec(LANES), _vec_spec(LANES), _vec_spec(SSM_INNER), _vec_spec(SSM_INNER), _vec_spec(SSM_INNER, LANES)],
        out_specs=[rspec(SSM_INNER), rspec(CONV_DIM), rspec(LANES), _vec_spec(LANES, 8), _vec_spec(SSM_INNER)],
        scratch=[pltpu.VMEM((SSM_GROUPS, N, half), F32), pltpu.VMEM((L, SSM_INNER), F32), pltpu.VMEM((L, SSM_INNER), F32),
                 pltpu.VMEM((L, SSM_INNER), F32), pltpu.VMEM((L, SSM_INNER), F32), pltpu.VMEM((1, SSM_INNER), F32)],
        sem=("arbitrary",))(cpre, dtr, pm, ypre, sprev, dya, dtb, alog, dskip, normw, ind)


def _gmlp_common(u, v, lnw, lnb):
    ug = _gelu(u)
    vg = _gelu(v)
    mu = jnp.mean(vg, axis=-1, keepdims=True)
    cen = vg - mu
    rstd = lax.rsqrt(jnp.mean(cen * cen, axis=-1, keepdims=True) + EPS)
    vhat = cen * rstd
    return ug, rstd, vhat, vhat * lnw + lnb


def _causal_mask():
    row = lax.broadcasted_iota(jnp.int32, (CHUNK, CHUNK), 0)
    col = lax.broadcasted_iota(jnp.int32, (CHUNK, CHUNK), 1)
    return row >= col


def _gmlp_fwd(pm, lnw, lnb, ws, bs_exp, name):
    T = pm.shape[0]
    nc = T // CHUNK
    L, G = CHUNK, GMLP_GROUPS

    def body(u_ref, v_ref, lnw_ref, lnb_ref, ws_ref, bs_ref, o_ref):
        ug, _, _, vn = _gmlp_common(u_ref[...], v_ref[...], lnw_ref[...], lnb_ref[...])
        causal = _causal_mask()
        for g in range(G):
            sl = slice(g * L, (g + 1) * L)
            wm = _bf(jnp.where(causal, ws_ref[g], 0.0))
            sv = _nn(wm, _bf(vn[:, sl])) + bs_ref[:, sl]
            o_ref[:, sl] = (ug[:, sl] * sv).astype(BF16)

    return _pcall(
        body, name=name, out_shape=_sds((T, GMLP_INNER), BF16), grid=(nc,),
        in_specs=[_row_spec(L, GMLP_INNER, 1), _row_spec(L, GMLP_INNER, 2), _vec_spec(GMLP_INNER), _vec_spec(GMLP_INNER),
                  pl.BlockSpec((G, L, L), lambda i: (0, 0, 0)), _vec_spec(GMLP_INNER, L)],
        out_specs=_row_spec(L, GMLP_INNER), sem=("parallel",))(pm, pm, lnw, lnb, ws, bs_exp)


def _gmlp_bwd(pm, dyb, lnw, lnb, ws, bs_exp, name):
    T = pm.shape[0]
    nc = T // CHUNK
    L, G = CHUNK, GMLP_GROUPS

    def body(u_ref, v_ref, dy_ref, lnw_ref, lnb_ref, ws_ref, bs_ref, du_ref, dv_ref, dws_ref, dbs_ref, acc_ref, dvn_ref):
        @pl.when(pl.program_id(0) == 0)
        def _():
            dws_ref[...] = jnp.zeros_like(dws_ref)
            dbs_ref[...] = jnp.zeros_like(dbs_ref)
            acc_ref[...] = jnp.zeros_like(acc_ref)
        uv, vv, dyv, lnwv = u_ref[...], v_ref[...], dy_ref[...], lnw_ref[...]
        ug, rstd, vhat, vn = _gmlp_common(uv, vv, lnwv, lnb_ref[...])
        causal = _causal_mask()
        lane = lax.broadcasted_iota(jnp.int32, (L, LANES), 1)
        dbs = jnp.zeros((L, LANES), F32)
        for g in range(G):
            sl = slice(g * L, (g + 1) * L)
            wm = _bf(jnp.where(causal, ws_ref[g], 0.0))
            vng = _bf(vn[:, sl])
            sv = _nn(wm, vng) + bs_ref[:, sl]
            du_ref[:, sl] = dyv[:, sl] * sv * _dgelu(uv[:, sl])
            dsv = dyv[:, sl] * ug[:, sl]
            dsvb = _bf(dsv)
            dws_ref[g] += jnp.where(causal, _nt(dsvb, vng), 0.0)
            dbs = dbs + jnp.where(lane == g, _rowsum(dsv), 0.0)
            dvn_ref[:, sl] = _tn(wm, dsvb)
        dbs_ref[...] += dbs
        dvn = dvn_ref[...]
        acc_ref[0:1, :] += _colsum(dvn * vhat)
        acc_ref[1:2, :] += _colsum(dvn)
        dvh = dvn * lnwv
        dvg = rstd * (dvh - jnp.mean(dvh, axis=-1, keepdims=True) - vhat * jnp.mean(dvh * vhat, axis=-1, keepdims=True))
        dv_ref[...] = dvg * _dgelu(vv)

    return _pcall(
        body, name=name,
        out_shape=[_sds((T, GMLP_INNER)), _sds((T, GMLP_INNER)), _sds((G, L, L)), _sds((L, LANES)), _sds((8, GMLP_INNER))],
        grid=(nc,),
        in_specs=[_row_spec(L, GMLP_INNER, 1), _row_spec(L, GMLP_INNER, 2), _row_spec(L, GMLP_INNER),
                  _vec_spec(GMLP_INNER), _vec_spec(GMLP_INNER), pl.BlockSpec((G, L, L), lambda i: (0, 0, 0)),
                  _vec_spec(GMLP_INNER, L)],
        out_specs=[_row_spec(L, GMLP_INNER), _row_spec(L, GMLP_INNER), pl.BlockSpec((G, L, L), lambda i: (0, 0, 0)),
                   _vec_spec(LANES, L), _vec_spec(GMLP_INNER, 8)],
        scratch=[pltpu.VMEM((L, GMLP_INNER), F32)], sem=("arbitrary",))(pm, pm, dyb, lnw, lnb, ws, bs_exp)


def _rel_buckets():
    qi = np.arange(CHUNK)[:, None]
    sj = np.arange(2 * CHUNK)[None, :]
    dist = np.maximum(qi + CHUNK - sj, 0)
    max_exact = REL_BUCKETS // 2
    log_ratio = (np.log(np.maximum(dist, 1).astype(np.float32) / np.float32(max_exact))
                 / np.float32(math.log(REL_MAX_DIST / max_exact))).astype(np.float32)
    large = max_exact + (log_ratio * np.float32(REL_BUCKETS - max_exact)).astype(np.int32)
    return np.where(dist < max_exact, dist, np.minimum(large, REL_BUCKETS - 1))


def _bucket_onehot_t():
    bucket = _rel_buckets().reshape(-1)
    return jnp.asarray((np.arange(REL_BUCKETS)[:, None] == bucket[None, :]).astype(np.float32))


def _bias_from_table(table_t, onehot_t, name):
    def body(t_ref, o_ref, out_ref):
        out_ref[...] = _nn(t_ref[...], o_ref[...], HIGHEST)

    return _pcall(body, name=name, out_shape=_sds((ATTN_HEADS, onehot_t.shape[1])))(table_t, onehot_t)


def _table_from_dbias(dbias, onehot_t, name):
    def body(d_ref, o_ref, out_ref):
        out_ref[...] = _nt(d_ref[...], o_ref[...], HIGHEST)

    return _pcall(body, name=name, out_shape=_sds((ATTN_HEADS, REL_BUCKETS)))(dbias, onehot_t)


def _attn_probs(qh, kband, bias_h, sink, mask, scale):
    logits = _nt(qh, kband) * scale + bias_h
    logits = jnp.where(mask, logits, NEG_INF)
    mx = jnp.maximum(jnp.max(logits, axis=-1, keepdims=True), sink)
    e = jnp.exp(logits - mx)
    es = jnp.exp(sink - mx)
    inv = 1.0 / (_rowsum(e) + es)
    return e * inv, es * inv


def _attn_mask(n, heads):
    qi = lax.broadcasted_iota(jnp.int32, (heads * CHUNK, 2 * CHUNK), 0) & (CHUNK - 1)
    sj = lax.broadcasted_iota(jnp.int32, (heads * CHUNK, 2 * CHUNK), 1)
    rel = qi + CHUNK - sj
    return (rel >= 0) & (rel < CHUNK) & ((sj >= CHUNK) | (n > 0))


def _stack_heads(ref, first, count, width):
    return jnp.concatenate([_bf(ref[:, (first + j) * width:(first + j + 1) * width]) for j in range(count)], axis=0)


def _sink_column(sink_ref, first, count):
    return jnp.concatenate([jnp.full((CHUNK, 1), sink_ref[first + j], F32) for j in range(count)], axis=0)


def _attn_fwd(qkv, bias, sinks, name):
    T = qkv.shape[0]
    nb = T // CHUNK
    L, DH, HPK = CHUNK, ATTN_DH, ATTN_HEADS // ATTN_KV
    scale = DH ** -0.5
    kcol, vcol = ATTN_HEADS * DH // LANES, ATTN_HEADS * DH // LANES + 1

    def body(q_ref, k_ref, v_ref, kp_ref, vp_ref, bias_ref, sink_ref, o_ref):
        n = pl.program_id(0)
        mask = _attn_mask(n, HPK)
        kband = _bf(jnp.concatenate([kp_ref[...], k_ref[...]], axis=0))
        vband = _bf(jnp.concatenate([vp_ref[...], v_ref[...]], axis=0))
        for kv in range(ATTN_KV):
            qg = _stack_heads(q_ref, kv * HPK, HPK, DH)
            bias_g = bias_ref[kv * HPK:(kv + 1) * HPK].reshape(HPK * L, 2 * L)
            p, _ = _attn_probs(qg, kband[:, kv * DH:(kv + 1) * DH], bias_g, _sink_column(sink_ref, kv * HPK, HPK), mask, scale)
            og = _nn(_bf(p), vband[:, kv * DH:(kv + 1) * DH])
            for j in range(HPK):
                h = kv * HPK + j
                o_ref[:, h * DH:(h + 1) * DH] = og[j * L:(j + 1) * L].astype(BF16)

    prev = lambda i: jnp.maximum(i - 1, 0)
    return _pcall(
        body, name=name, out_shape=_sds((T, ATTN_HEADS * DH), BF16), grid=(nb,),
        in_specs=[_row_spec(L, ATTN_HEADS * DH, 0), _row_spec(L, LANES, kcol), _row_spec(L, LANES, vcol),
                  pl.BlockSpec((L, LANES), lambda i: (prev(i), kcol)), pl.BlockSpec((L, LANES), lambda i: (prev(i), vcol)),
                  pl.BlockSpec((ATTN_HEADS, L, 2 * L), lambda i: (0, 0, 0)),
                  pl.BlockSpec(memory_space=pltpu.SMEM)],
        out_specs=_row_spec(L, ATTN_HEADS * DH), sem=("parallel",))(qkv, qkv, qkv, qkv, qkv, bias, sinks)


def _attn_bwd(qkv, datt, bias, sinks, name):
    T = qkv.shape[0]
    nb = T // CHUNK
    L, DH, HPK = CHUNK, ATTN_DH, ATTN_HEADS // ATTN_KV
    scale = DH ** -0.5
    kcol, vcol = ATTN_HEADS * DH // LANES, ATTN_HEADS * DH // LANES + 1

    def body(q_ref, k_ref, v_ref, kp_ref, vp_ref, do_ref, bias_ref, sink_ref,
             dq_ref, dk_ref, dv_ref, dbias_ref, dsink_ref, pend_k, pend_v, band_k, band_v):
        n = pl.program_id(0)

        @pl.when(n == 0)
        def _():
            dbias_ref[...] = jnp.zeros_like(dbias_ref)
            dsink_ref[...] = jnp.zeros_like(dsink_ref)

        @pl.when(n < nb)
        def _():
            mask = _attn_mask(n, HPK)
            kband = _bf(jnp.concatenate([kp_ref[...], k_ref[...]], axis=0))
            vband = _bf(jnp.concatenate([vp_ref[...], v_ref[...]], axis=0))
            lane1 = lax.broadcasted_iota(jnp.int32, (1, LANES), 1)
            dsink = jnp.zeros((1, LANES), F32)
            for kv in range(ATTN_KV):
                kb, vb = kband[:, kv * DH:(kv + 1) * DH], vband[:, kv * DH:(kv + 1) * DH]
                qg = _stack_heads(q_ref, kv * HPK, HPK, DH)
                dog = _stack_heads(do_ref, kv * HPK, HPK, DH)
                bias_g = bias_ref[kv * HPK:(kv + 1) * HPK].reshape(HPK * L, 2 * L)
                p, ps = _attn_probs(qg, kb, bias_g, _sink_column(sink_ref, kv * HPK, HPK), mask, scale)
                dp = _nt(dog, vb)
                band_v[:, kv * DH:(kv + 1) * DH] = _tn(_bf(p), dog)
                delta = _rowsum(p * dp)
                dl = p * (dp - delta)
                dbias_ref[kv * HPK:(kv + 1) * HPK] += dl.reshape(HPK, L, 2 * L)
                dlb = _bf(dl)
                dqg = _nn(dlb, kb) * scale
                band_k[:, kv * DH:(kv + 1) * DH] = _tn(dlb, qg) * scale
                sd = ps * delta
                for j in range(HPK):
                    h = kv * HPK + j
                    dq_ref[:, h * DH:(h + 1) * DH] = dqg[j * L:(j + 1) * L]
                    dsink = dsink + jnp.where(lane1 == h, -_colsum(sd[j * L:(j + 1) * L]), 0.0)
            dsink_ref[...] += dsink

            @pl.when(n > 0)
            def _():
                dk_ref[...] = pend_k[...] + band_k[0:L, :]
                dv_ref[...] = pend_v[...] + band_v[0:L, :]
            pend_k[...] = band_k[L:2 * L, :]
            pend_v[...] = band_v[L:2 * L, :]

        @pl.when(n == nb)
        def _():
            dk_ref[...] = pend_k[...]
            dv_ref[...] = pend_v[...]

    cur = lambda i: jnp.minimum(i, nb - 1)
    prev = lambda i: jnp.maximum(jnp.minimum(i, nb - 1) - 1, 0)
    lag = lambda i: jnp.maximum(i - 1, 0)
    return _pcall(
        body, name=name,
        out_shape=[_sds((T, ATTN_HEADS * DH)), _sds((T, LANES)), _sds((T, LANES)), _sds((ATTN_HEADS, L, 2 * L)), _sds((1, LANES))],
        grid=(nb + 1,),
        in_specs=[pl.BlockSpec((L, ATTN_HEADS * DH), lambda i: (cur(i), 0)),
                  pl.BlockSpec((L, LANES), lambda i: (cur(i), kcol)), pl.BlockSpec((L, LANES), lambda i: (cur(i), vcol)),
                  pl.BlockSpec((L, LANES), lambda i: (prev(i), kcol)), pl.BlockSpec((L, LANES), lambda i: (prev(i), vcol)),
                  pl.BlockSpec((L, ATTN_HEADS * DH), lambda i: (cur(i), 0)),
                  pl.BlockSpec((ATTN_HEADS, L, 2 * L), lambda i: (0, 0, 0)),
                  pl.BlockSpec(memory_space=pltpu.SMEM)],
        out_specs=[pl.BlockSpec((L, ATTN_HEADS * DH), lambda i: (cur(i), 0)),
                   pl.BlockSpec((L, LANES), lambda i: (lag(i), 0)), pl.BlockSpec((L, LANES), lambda i: (lag(i), 0)),
                   pl.BlockSpec((ATTN_HEADS, L, 2 * L), lambda i: (0, 0, 0)), _vec_spec(LANES)],
        scratch=[pltpu.VMEM((L, LANES), F32), pltpu.VMEM((L, LANES), F32),
                 pltpu.VMEM((2 * L, LANES), F32), pltpu.VMEM((2 * L, LANES), F32)],
        sem=("arbitrary",))(qkv, qkv, qkv, qkv, qkv, datt, bias, sinks)


def _pad_rows(a, mult):
    pad = (-a.shape[-2]) % mult
    if pad == 0:
        return a
    cfg = [(0, 0)] * (a.ndim - 2) + [(0, pad), (0, 0)]
    return jnp.pad(a, cfg)


class _Pack:
    def __init__(self, width, mult, total_mult):
        self.width, self.mult, self.total_mult = width, mult, total_mult
        self.entries = []
        self.rows = 0

    def add(self, key, shape):
        n = int(np.prod(shape))
        rows = -(-n // self.width)
        self.entries.append((key, self.rows, rows, tuple(shape)))
        self.rows += -(-rows // self.mult) * self.mult

    @property
    def total(self):
        return -(-self.rows // self.total_mult) * self.total_mult

    def pack(self, pieces, dtype, lead=()):
        parts = []
        for key, _, rows, shape in self.entries:
            a = pieces[key].astype(dtype).reshape(lead + (-1,))
            n = int(np.prod(shape))
            a = jnp.pad(a, [(0, 0)] * len(lead) + [(0, rows * self.width - n)])
            a = a.reshape(lead + (rows, self.width))
            parts.append(_pad_rows(a, self.mult))
        out = jnp.concatenate(parts, axis=len(lead))
        return _pad_rows(out, self.total_mult)

    def unpack(self, packed, lead=()):
        out = {}
        for key, off, rows, shape in self.entries:
            a = lax.slice_in_dim(packed, off, off + rows, axis=len(lead))
            a = a.reshape(lead + (-1,))
            n = int(np.prod(shape))
            out[key] = lax.slice_in_dim(a, 0, n, axis=len(lead)).reshape(lead + shape)
        return out


def _ffn_fwd(x, mod, norm_w, wg_t, wu_t, wd, tag):
    h = _norm_mod(x, norm_w, mod[4:5], mod[3:4], f"ffn_norm_{tag}")
    gate, up, act = _mm_swiglu(h, wg_t, wu_t, f"ffn_gateup_{tag}")
    x_out, ffn_out = _mm([(act, wd)], "nn", name=f"ffn_down_{tag}", resid=x, gvec=mod[5:6], keep=True)
    return x_out, dict(h=h, gate=gate, up=up, act=act, out=ffn_out)


def _ffn_bwd(dx_out, x_in, saved, mod, norm_w, wg_t, wu_t, wd, tag):
    dffn, acc_g = _gate_bwd(dx_out, saved["out"], mod[5:6], f"ffn_gate_bwd_{tag}")
    dgate, dup = _mm_swiglu_bwd(dffn, wd, saved["gate"], saved["up"], f"ffn_act_bwd_{tag}")
    d_wd = _mm_tn(saved["act"], dffn, name=f"ffn_dwd_{tag}")
    d_wg_t = _mm_tn(dgate, saved["h"], name=f"ffn_dwg_{tag}")
    d_wu_t = _mm_tn(dup, saved["h"], name=f"ffn_dwu_{tag}")
    dh = _mm([(dgate, wg_t), (dup, wu_t)], "nn", name=f"ffn_dh_{tag}")
    dx, acc_n = _norm_mod_bwd(x_in, dh, dx_out, norm_w, mod[4:5], f"ffn_norm_bwd_{tag}")
    return dx, dict(d_wg=d_wg_t, d_wu=d_wu_t, d_wd=d_wd, d_g=acc_g[0], d_sc=acc_n[0], d_sh=acc_n[1], d_nw=acc_n[2])


_BIG = [
    ("out_w", "out_w_even", 0, "row"), ("qkv_w", "qkv_w", 0, "col"), ("o_w", "o_w", 0, "row"),
    ("gate0", "ffn_gate_w", 0, "col"), ("up0", "ffn_up_w", 0, "col"), ("down0", "ffn_down_w", 0, "row"),
    ("gate1", "ffn_gate_w", 1, "col"), ("up1", "ffn_up_w", 1, "col"), ("down1", "ffn_down_w", 1, "row"),
    ("in_w", "in_w_even", 0, "col"),
]


def _to_wire(a, kind):
    return a.T if kind == "col" else a

_REPLICATED = ["ada_b", "norm_mix_w", "norm_ffn_w", "conv_b", "dt_bias", "a_log", "d_skip", "ssm_norm_w", "gmlp_ln_w",
               "gmlp_ln_b", "gmlp_ws", "gmlp_bs", "sinks", "rel_table", "final_norm_w"]
_TINY_SHARDED = ["conv_w", "qkv_b", "o_b"]

_WEIGHTS = ['ada_w', 'ada_b', 'norm_mix_w', 'norm_ffn_w', 'in_w_even', 'conv_w', 'conv_b', 'dt_bias', 'a_log', 'd_skip',
            'ssm_norm_w', 'gmlp_ln_w', 'gmlp_ln_b', 'gmlp_ws', 'gmlp_bs', 'out_w_even', 'qkv_w', 'qkv_b', 'o_w', 'o_b',
            'sinks', 'rel_table', 'ffn_gate_w', 'ffn_up_w', 'ffn_down_w', 'final_norm_w']


def _step(x, c, loss_target, W, M, V):
    T = x.shape[1]
    x0 = x[0]
    target = loss_target[0]
    me = 4 * lax.axis_index("x") + 2 * lax.axis_index("y") + lax.axis_index("c")

    big = _Pack(D, 16, 128)
    wire = _Pack(D, 1, 128)
    for key, name, layer, kind in _BIG:
        big.add(key, W[name][layer].shape)
        wire.add(key, _to_wire(W[name][layer], kind).shape)
    w_wire_local = {key: _to_wire(W[name][layer].astype(BF16), kind) for key, name, layer, kind in _BIG}
    gathered = _all_gather(wire.pack(w_wire_local, BF16), "gather_weights")
    shards = wire.unpack(gathered, lead=(N_DEV,))
    full = {key: shards[key].reshape(-1, D) for key, name, layer, kind in _BIG}

    small_in = _Pack(D, 8, 8)
    small_in.add("c", (1, D))
    small_in.add("conv_w", W["conv_w"][0].shape)
    small_in.add("qkv_b", W["qkv_b"][0].shape)
    small_in.add("o_b", W["o_b"][0].shape)
    sm = small_in.unpack(_all_gather(small_in.pack(
        dict(c=c, conv_w=W["conv_w"][0], qkv_b=W["qkv_b"][0], o_b=W["o_b"][0]), F32), "gather_small"), lead=(N_DEV,))
    c_all = sm["c"].reshape(N_DEV, D)
    conv_w_full = jnp.transpose(sm["conv_w"], (1, 0, 2)).reshape(SSM_CONV, CONV_DIM)
    qkv_b_full = sm["qkv_b"].reshape(1, QKV_DIM)
    o_b_full = sm["o_b"].reshape(1, D)

    ncol = W["ada_w"].shape[2]
    cond, mod_cols = _mod_matmul(c_all, W["ada_w"], "mod_matmul")
    mod_g = _all_gather(mod_cols.reshape(DEPTH * N_DEV, ncol), "gather_mod").reshape(N_DEV, DEPTH, N_DEV, ncol)
    mod_me = lax.dynamic_index_in_dim(mod_g, me, axis=2, keepdims=False)
    mod_me = jnp.transpose(mod_me, (1, 0, 2)).reshape(DEPTH, 6, D)
    mod_me = jnp.pad(mod_me, ((0, 0), (0, 2), (0, 0))).reshape(DEPTH * 8, D)
    ada_b_rows = jnp.pad(W["ada_b"].reshape(DEPTH, 6, D), ((0, 0), (0, 2), (0, 0))).reshape(DEPTH * 8, D)
    mod_all = _add_rows(mod_me, ada_b_rows, "mod_bias").reshape(DEPTH, 8, D)
    mod0, mod1 = mod_all[0], mod_all[1]

    in_t = full["in_w"]
    o1, o2, o3, o4 = SSM_INNER, SSM_INNER + CONV_DIM, SSM_INNER + CONV_DIM + SSM_HEADS, SSM_INNER + CONV_DIM + SSM_HEADS + GMLP_INNER
    w_z, w_xbc, w_dt, w_u, w_v = in_t[:o1], in_t[o1:o2], in_t[o2:o3], in_t[o3:o4], in_t[o4:]
    w_main = jnp.concatenate([w_z, w_u, w_v, w_xbc], axis=0)
    w_dtp = jnp.pad(w_dt, ((0, LANES - SSM_HEADS), (0, 0)))
    out_w = full["out_w"]
    qkv_t, o_w = full["qkv_w"], full["o_w"]
    w_q, w_k, w_v_att = qkv_t[:D], qkv_t[D:D + LANES], qkv_t[D + LANES:]

    pad16 = lambda a: jnp.pad(a.reshape(1, SSM_HEADS), ((0, 0), (0, LANES - SSM_HEADS)))
    dtb, alog = pad16(W["dt_bias"][0]), pad16(W["a_log"][0])
    dskip = jnp.repeat(W["d_skip"][0], SSM_HEAD_DIM).reshape(1, SSM_INNER)
    ssm_nw = W["ssm_norm_w"]
    lnw, lnb = W["gmlp_ln_w"], W["gmlp_ln_b"]
    ws = W["gmlp_ws"][0]
    bs_exp = jnp.repeat(W["gmlp_bs"][0].T, CHUNK, axis=1)
    conv_b = W["conv_b"]
    nmw, nfw = W["norm_mix_w"], W["norm_ffn_w"]
    onehot_t = _bucket_onehot_t()
    head_ind = _head_indicator()
    bias = _bias_from_table(W["rel_table"].T, onehot_t, "rel_bias").reshape(ATTN_HEADS, CHUNK, 2 * CHUNK)
    sinks = W["sinks"][0]

    h0 = _norm_mod(x0, nmw[0:1], mod0[1:2], mod0[0:1], "mix_norm_0")
    pm = _mm([(h0, w_main)], "nt", name="in_proj", tn_pref=1536)
    dtr = _mm([(h0, w_dtp)], "nt", name="in_proj_dt")
    cpre = _conv_fwd(pm, conv_w_full, conv_b, "conv_fwd")
    ya, ypre, sprev = _ssd_fwd(cpre, dtr, pm, dtb, alog, dskip, ssm_nw, head_ind, "ssd_fwd")
    yb = _gmlp_fwd(pm, lnw, lnb, ws, bs_exp, "gmlp_fwd")
    x1, mix0 = _mm([(ya, out_w[:SSM_INNER]), (yb, out_w[SSM_INNER:])], "nn", name="out_proj", resid=x0, gvec=mod0[2:3], keep=True)
    x2, ffn0 = _ffn_fwd(x1, mod0, nfw[0:1], full["gate0"], full["up0"], full["down0"], "0")

    h1 = _norm_mod(x2, nmw[1:2], mod1[1:2], mod1[0:1], "mix_norm_1")
    qkv = _mm([(h1, qkv_t)], "nt", name="qkv_proj", bias=qkv_b_full, tn_pref=1280)
    att = _attn_fwd(qkv, bias, sinks, "attn_fwd")
    x3, mix1 = _mm([(att, o_w)], "nn", name="o_proj", bias=o_b_full, resid=x2, gvec=mod1[2:3], keep=True)
    x4, ffn1 = _ffn_fwd(x3, mod1, nfw[1:2], full["gate1"], full["up1"], full["down1"], "1")

    dx4, acc_f = _final_loss(x4, W["final_norm_w"].reshape(1, D), target, "final_loss")
    loss = lax.psum(acc_f[1, 0], ("x", "y", "c"))
    dx3, gf1 = _ffn_bwd(dx4, x3, ffn1, mod1, nfw[1:2], full["gate1"], full["up1"], full["down1"], "1")

    dmix1, acc_m1 = _gate_bwd(dx3, mix1, mod1[2:3], "mix_gate_bwd_1")
    datt = _mm([(dmix1, o_w)], "nt", name="o_proj_dx")
    d_o_w = _mm_tn(att, dmix1, name="o_proj_dw")
    dq, dk, dv, dbias, dsinks = _attn_bwd(qkv, datt, bias, sinks, "attn_bwd")
    d_table = _table_from_dbias(dbias.reshape(ATTN_HEADS, -1), onehot_t, "rel_table_grad").T
    d_qkv_t = jnp.concatenate([_mm_tn(dq, h1, name="qkv_dw_q"), _mm_tn(dk, h1, name="qkv_dw_k"), _mm_tn(dv, h1, name="qkv_dw_v")], axis=0)
    d_qkv_b = jnp.concatenate([_colsum_call(dq, "qkv_db_q"), _colsum_call(dk, "qkv_db_k"), _colsum_call(dv, "qkv_db_v")], axis=1)
    dh1 = _mm([(dq, w_q), (dk, w_k), (dv, w_v_att)], "nn", name="qkv_proj_dx")
    dx2, acc_n1 = _norm_mod_bwd(x2, dh1, dx3, nmw[1:2], mod1[1:2], "mix_norm_bwd_1")

    dx1, gf0 = _ffn_bwd(dx2, x1, ffn0, mod0, nfw[0:1], full["gate0"], full["up0"], full["down0"], "0")

    dmix0, acc_m0 = _gate_bwd(dx1, mix0, mod0[2:3], "mix_gate_bwd_0")
    dya = _mm([(dmix0, out_w[:SSM_INNER])], "nt", name="out_proj_dx_a")
    dyb = _mm([(dmix0, out_w[SSM_INNER:])], "nt", name="out_proj_dx_b")
    d_out_w = jnp.concatenate([_mm_tn(ya, dmix0, name="out_proj_dw_a"), _mm_tn(yb, dmix0, name="out_proj_dw_b")], axis=0)
    du, dvg, d_ws, d_bs, acc_ln = _gmlp_bwd(pm, dyb, lnw, lnb, ws, bs_exp, "gmlp_bwd")
    dz, dcpre, ddtr, acc_ssd, d_ssm_nw = _ssd_bwd(cpre, dtr, pm, ypre, sprev, dya, dtb, alog, dskip, ssm_nw, head_ind, "ssd_bwd")
    dxbc, acc_conv = _conv_bwd(dcpre, pm, conv_w_full, "conv_bwd")
    d_in_t = jnp.concatenate([
        _mm_tn(dz, h0, name="in_dw_z"), _mm_tn(dxbc, h0, name="in_dw_xbc"),
        _mm_tn(ddtr, h0, name="in_dw_dt")[:SSM_HEADS], _mm_tn(du, h0, name="in_dw_u"), _mm_tn(dvg, h0, name="in_dw_v")], axis=0)
    dh0 = _mm([(dz, w_z), (dxbc, w_xbc), (ddtr, w_dtp), (du, w_u), (dvg, w_v)], "nn", name="in_proj_dx")
    grad_x, acc_n0 = _norm_mod_bwd(x0, dh0, dx1, nmw[0:1], mod0[1:2], "mix_norm_bwd_0")

    g_wire = dict(in_w=d_in_t, out_w=d_out_w, qkv_w=d_qkv_t, o_w=d_o_w,
                  gate0=gf0["d_wg"], up0=gf0["d_wu"], down0=gf0["d_wd"], gate1=gf1["d_wg"], up1=gf1["d_wu"], down1=gf1["d_wd"])
    g_dest = {key: g_wire[key].reshape(N_DEV, -1, D) for key, name, layer, kind in _BIG}
    g_packed = wire.pack(g_dest, F32, lead=(N_DEV,))
    theirs = _exchange_sibling(g_packed, "exchange_grads_sibling")
    from_chips = _exchange_chips(_pair_sum(g_packed, theirs, "grads_pair_sum"), "exchange_grads_chips")
    g_mine = wire.unpack(_sum_parts(from_chips, "grads_chip_sum"))
    g_nat = {key: _to_wire(g_mine[key], kind) for key, name, layer, kind in _BIG}
    pk = lambda S: big.pack({key: S[name][layer] for key, name, layer, kind in _BIG}, F32)
    res_big = [big.unpack(r) for r in _adamw(big.pack(g_nat, F32)[None], pk(W), pk(M), pk(V), "adamw_big")]

    d_mod = jnp.stack([
        jnp.stack([acc_n0[1], acc_n0[0], acc_m0[0], gf0["d_sh"], gf0["d_sc"], gf0["d_g"]]),
        jnp.stack([acc_n1[1], acc_n1[0], acc_m1[0], gf1["d_sh"], gf1["d_sc"], gf1["d_g"]])])
    g_small = dict(
        ada_b=d_mod.reshape(DEPTH, 6 * D),
        norm_mix_w=jnp.stack([acc_n0[2], acc_n1[2]]), norm_ffn_w=jnp.stack([gf0["d_nw"], gf1["d_nw"]]),
        conv_b=acc_conv[4:5], dt_bias=acc_ssd[0:1, :SSM_HEADS], a_log=acc_ssd[1:2, :SSM_HEADS], d_skip=acc_ssd[2:3, :SSM_HEADS],
        ssm_norm_w=d_ssm_nw, gmlp_ln_w=acc_ln[0:1], gmlp_ln_b=acc_ln[1:2], gmlp_ws=d_ws[None],
        gmlp_bs=d_bs[:, :GMLP_GROUPS].T[None], sinks=dsinks[:, :ATTN_HEADS], rel_table=d_table, final_norm_w=acc_f[0],
        conv_w=acc_conv[0:SSM_CONV], qkv_b=d_qkv_b, o_b=acc_m1[1:2])
    small = _Pack(D, 8, 8)
    for name in _REPLICATED:
        small.add(name, W[name].shape)
    small.add("conv_w", (SSM_CONV, CONV_DIM))
    small.add("qkv_b", (1, QKV_DIM))
    small.add("o_b", (1, D))
    parts_small = _all_gather(small.pack(g_small, F32), "gather_small_grads")
    zeros_tiny = dict(conv_w=jnp.zeros((SSM_CONV, CONV_DIM), F32), qkv_b=jnp.zeros((1, QKV_DIM), F32), o_b=jnp.zeros((1, D), F32))
    pks = lambda S: small.pack({**{name: S[name] for name in _REPLICATED}, **zeros_tiny}, F32)
    res_small = [small.unpack(r) for r in _adamw(parts_small, pks(W), pks(M), pks(V), "adamw_small")]
    g_small_sum = res_small[0]

    n_cw, n_qb, n_ob = W["conv_w"].shape[2], W["qkv_b"].shape[1], W["o_b"].shape[1]
    g_tiny = dict(conv_w=lax.dynamic_slice_in_dim(g_small_sum["conv_w"], me * n_cw, n_cw, axis=1)[None],
                  qkv_b=lax.dynamic_slice_in_dim(g_small_sum["qkv_b"], me * n_qb, n_qb, axis=1),
                  o_b=lax.dynamic_slice_in_dim(g_small_sum["o_b"], me * n_ob, n_ob, axis=1))
    tiny = _Pack(D, 8, 8)
    for name in _TINY_SHARDED:
        tiny.add(name, W[name].shape)
    pkt = lambda S: tiny.pack({name: S[name] for name in _TINY_SHARDED}, F32)
    res_tiny = [tiny.unpack(r) for r in _adamw(pkt(g_tiny)[None], pkt(W), pkt(M), pkt(V), "adamw_tiny")]

    dmod_all = parts_small[:, small.entries[0][1]:small.entries[0][1] + small.entries[0][2]].reshape(N_DEV, DEPTH, 6 * D)
    dmod_cols = jnp.transpose(lax.dynamic_slice_in_dim(dmod_all, me * ncol, ncol, axis=2), (1, 0, 2))
    g_ada_w = _ada_w_grad(cond, dmod_cols, "ada_w_grad")
    flat = lambda a: a.reshape(DEPTH * D, ncol)
    res_ada = [r.reshape(DEPTH, D, ncol) for r in _adamw(flat(g_ada_w)[None], flat(W["ada_w"]), flat(M["ada_w"]), flat(V["ada_w"]), "adamw_ada_w")]

    def result(kind_idx, name):
        if name == "ada_w":
            return res_ada[kind_idx]
        if name in _REPLICATED:
            return res_small[kind_idx][name]
        if name in _TINY_SHARDED:
            return res_tiny[kind_idx][name]
        pieces = [res_big[kind_idx][key] for key, nm, layer, kind in _BIG if nm == name]
        return jnp.stack(pieces)

    outs = [loss, grad_x[None]]
    for kind_idx in range(4):
        outs += [result(kind_idx, name) for name in _WEIGHTS]
    return tuple(outs)


def kernel(x, c, ada_w, ada_b, norm_mix_w, norm_ffn_w, in_w_even, conv_w, conv_b, dt_bias, a_log, d_skip, ssm_norm_w, gmlp_ln_w, gmlp_ln_b, gmlp_ws, gmlp_bs, out_w_even, qkv_w, qkv_b, o_w, o_b, sinks, rel_table, ffn_gate_w, ffn_up_w, ffn_down_w, final_norm_w, loss_target, m_ada_w, m_ada_b, m_norm_mix_w, m_norm_ffn_w, m_in_w_even, m_conv_w, m_conv_b, m_dt_bias, m_a_log, m_d_skip, m_ssm_norm_w, m_gmlp_ln_w, m_gmlp_ln_b, m_gmlp_ws, m_gmlp_bs, m_out_w_even, m_qkv_w, m_qkv_b, m_o_w, m_o_b, m_sinks, m_rel_table, m_ffn_gate_w, m_ffn_up_w, m_ffn_down_w, m_final_norm_w, v_ada_w, v_ada_b, v_norm_mix_w, v_norm_ffn_w, v_in_w_even, v_conv_w, v_conv_b, v_dt_bias, v_a_log, v_d_skip, v_ssm_norm_w, v_gmlp_ln_w, v_gmlp_ln_b, v_gmlp_ws, v_gmlp_bs, v_out_w_even, v_qkv_w, v_qkv_b, v_o_w, v_o_b, v_sinks, v_rel_table, v_ffn_gate_w, v_ffn_up_w, v_ffn_down_w, v_final_norm_w):
    args = locals()
    W = {n: args[n] for n in _WEIGHTS}
    M = {n: args["m_" + n] for n in _WEIGHTS}
    V = {n: args["v_" + n] for n in _WEIGHTS}
    return _step(x, c, loss_target, W, M, V)
```

```python
import functools
import math

import numpy as np
import jax
import jax.numpy as jnp
from jax import lax
from jax.experimental import pallas as pl
from jax.experimental.pallas import tpu as pltpu

F32 = jnp.float32
BF16 = jnp.bfloat16
HIGHEST = lax.Precision.HIGHEST
MESH = pl.DeviceIdType.MESH

N_DEV = 8
D = 1024
DEPTH = 2
SSM_HEADS = 16
SSM_HEAD_DIM = 64
SSM_INNER = 1024
SSM_GROUPS = 2
SSM_STATE = 128
SSM_CONV = 4
CHUNK = 128
CONV_DIM = SSM_INNER + 2 * SSM_GROUPS * SSM_STATE
GMLP_GROUPS = 8
GMLP_INNER = 1024
IN_EVEN = 4624
ATTN_HEADS = 16
ATTN_KV = 2
ATTN_DH = 64
QKV_DIM = 1280
REL_BUCKETS = 32
REL_MAX_DIST = 128
FFN = 2816
EPS = 1e-6
NEG_INF = -1e30
LANES = 128

ADAM_LR = 0.001
ADAM_B1 = 0.9
ADAM_B2 = 0.999
ADAM_EPS = 1e-08
ADAM_WD = 0.01
ADAM_STEP = 10

VMEM_LIMIT_BYTES = 56 * 1024 * 1024
ROW_TILE = 512


def _pcall(body, *, name, out_shape, grid=(), in_specs=None, out_specs=None, scratch=(), sem=None):
    params = dict(vmem_limit_bytes=VMEM_LIMIT_BYTES)
    if sem is not None:
        params["dimension_semantics"] = sem
    specs = {} if in_specs is None else dict(in_specs=in_specs, out_specs=out_specs)
    return pl.pallas_call(
        body, name=name, out_shape=out_shape, grid=grid, **specs,
        scratch_shapes=list(scratch), compiler_params=pltpu.CompilerParams(**params))


def _tile(n, pref):
    if n <= pref:
        return n
    best = None
    for t in range(LANES, pref + 1, LANES):
        if n % t == 0:
            best = t
    assert best is not None, (n, pref)
    return best


def _rows(T):
    return min(ROW_TILE, T)


def _sds(shape, dtype=F32):
    return jax.ShapeDtypeStruct(shape, dtype)


def _row_spec(tm, c, col=0):
    return pl.BlockSpec((tm, c), lambda i, col=col: (i, col))


def _vec_spec(c, r=1):
    return pl.BlockSpec((r, c), lambda i: (0, 0))


def _sigmoid(x):
    return jax.nn.sigmoid(x)


def _silu(x):
    return x * _sigmoid(x)


def _dsilu(x):
    s = _sigmoid(x)
    return s * (1.0 + x * (1.0 - s))


def _gelu(x):
    return 0.5 * x * (1.0 + lax.erf(x * 0.7071067811865476))


def _dgelu(x):
    return 0.5 * (1.0 + lax.erf(x * 0.7071067811865476)) + x * jnp.exp(-0.5 * x * x) * 0.3989422804014327


def _dot(a, b, dims, precision=None):
    return lax.dot_general(a, b, (dims, ((), ())), precision=precision, preferred_element_type=F32)


def _nn(a, b, precision=None):
    return _dot(a, b, ((1,), (0,)), precision)


def _nt(a, b, precision=None):
    return _dot(a, b, ((1,), (1,)), precision)


def _tn(a, b, precision=None):
    return _dot(a, b, ((0,), (0,)), precision)


def _bf(x):
    return x.astype(BF16)


def _colsum(x):
    return jnp.sum(x, axis=0, keepdims=True)


def _rowsum(x):
    return jnp.sum(x, axis=1, keepdims=True)


def _allsum(x):
    return _colsum(_rowsum(x))


def _all_gather(x, name):
    R, C = x.shape

    def body(x_ref, out_ref, send_sems, recv_sems, local_sem):
        mx, my, mc = lax.axis_index("x"), lax.axis_index("y"), lax.axis_index("c")
        me, sibling = (mx, my, mc), (mx, my, 1 - mc)
        chips = [(1 - mx, my), (mx, 1 - my), (1 - mx, 1 - my)]

        def slot(px, py, pc):
            return out_ref.at[4 * px + 2 * py + pc]

        def copy(k, block, to, src=None):
            return pltpu.make_async_remote_copy(
                src_ref=slot(*block) if src is None else src, dst_ref=slot(*block),
                send_sem=send_sems.at[k], recv_sem=recv_sems.at[k], device_id=to, device_id_type=MESH)

        mine = pltpu.make_async_copy(x_ref, slot(*me), local_sem)
        mine.start()
        first = [copy(0, me, sibling, src=x_ref)]
        first += [copy(1 + j, me, (*chip, mc), src=x_ref) for j, chip in enumerate(chips)]
        for cp in first:
            cp.start()
        passed = [copy(4 + j, (*chip, mc), sibling) for j, chip in enumerate(chips)]
        for j, chip in enumerate(chips):
            copy(1 + j, (*chip, mc), me).wait_recv()
            passed[j].start()
        copy(0, sibling, me).wait_recv()
        for j, chip in enumerate(chips):
            copy(4 + j, (*chip, 1 - mc), me).wait_recv()
        for cp in first + passed:
            cp.wait_send()
        mine.wait()

    return pl.pallas_call(
        body, name=name, out_shape=_sds((N_DEV, R, C), x.dtype),
        in_specs=[pl.BlockSpec(memory_space=pl.ANY)], out_specs=pl.BlockSpec(memory_space=pl.ANY),
        scratch_shapes=[pltpu.SemaphoreType.DMA((7,)), pltpu.SemaphoreType.DMA((7,)), pltpu.SemaphoreType.DMA(())],
    )(x)


N_CHIP = 4


def _exchange_sibling(p, name):
    _, R, C = p.shape

    def body(p_ref, theirs_ref, send_sems, recv_sems):
        mx, my, mc = lax.axis_index("x"), lax.axis_index("y"), lax.axis_index("c")
        copies = []
        for chip in range(N_CHIP):
            cp = pltpu.make_async_remote_copy(
                src_ref=p_ref.at[2 * chip + 1 - mc], dst_ref=theirs_ref.at[chip],
                send_sem=send_sems.at[chip], recv_sem=recv_sems.at[chip],
                device_id=(mx, my, 1 - mc), device_id_type=MESH)
            cp.start()
            copies.append(cp)
        for cp in copies:
            cp.wait()

    return pl.pallas_call(
        body, name=name, out_shape=_sds((N_CHIP, R, C), p.dtype),
        in_specs=[pl.BlockSpec(memory_space=pl.ANY)], out_specs=pl.BlockSpec(memory_space=pl.ANY),
        scratch_shapes=[pltpu.SemaphoreType.DMA((N_CHIP,))] * 2,
    )(p)


def _exchange_chips(q, name):
    _, R, C = q.shape

    def body(q_ref, out_ref, send_sems, recv_sems, local_sem):
        mx, my, mc = lax.axis_index("x"), lax.axis_index("y"), lax.axis_index("c")
        me = 2 * mx + my
        local = pltpu.make_async_copy(q_ref.at[me], out_ref.at[me], local_sem)
        local.start()
        copies = []
        for r in range(1, N_CHIP):
            px = 1 - mx if r & 2 else mx
            py = 1 - my if r & 1 else my
            cp = pltpu.make_async_remote_copy(
                src_ref=q_ref.at[2 * px + py], dst_ref=out_ref.at[me],
                send_sem=send_sems.at[r - 1], recv_sem=recv_sems.at[r - 1],
                device_id=(px, py, mc), device_id_type=MESH)
            cp.start()
            copies.append(cp)
        for cp in copies:
            cp.wait()
        local.wait()

    return pl.pallas_call(
        body, name=name, out_shape=_sds((N_CHIP, R, C), q.dtype),
        in_specs=[pl.BlockSpec(memory_space=pl.ANY)], out_specs=pl.BlockSpec(memory_space=pl.ANY),
        scratch_shapes=[pltpu.SemaphoreType.DMA((3,)), pltpu.SemaphoreType.DMA((3,)), pltpu.SemaphoreType.DMA(())],
    )(q)


def _pair_sum(p, theirs, name):
    n, R, C = theirs.shape
    tr = _tile_rows(R, 256)

    def body(p_ref, t_ref, o_ref):
        mc = lax.axis_index("c")
        o_ref[0] = (p_ref[0, mc] + t_ref[0]).astype(BF16)

    blk = pl.BlockSpec((1, tr, C), lambda s, i: (s, i, 0))
    return _pcall(body, name=name, out_shape=_sds((n, R, C), BF16), grid=(n, R // tr),
                  in_specs=[pl.BlockSpec((1, 2, tr, C), lambda s, i: (s, 0, i, 0)), blk],
                  out_specs=blk, sem=("parallel", "parallel"))(p.reshape(n, 2, R, C), theirs)


def _sum_parts(parts, name):
    P, R, C = parts.shape
    tr = _tile_rows(R, 256)

    def body(p_ref, o_ref):
        g = p_ref[0].astype(F32)
        for k in range(1, P):
            g = g + p_ref[k].astype(F32)
        o_ref[...] = g

    return _pcall(body, name=name, out_shape=_sds((R, C)), grid=(R // tr,),
                  in_specs=[pl.BlockSpec((P, tr, C), lambda i: (0, i, 0))],
                  out_specs=pl.BlockSpec((tr, C), lambda i: (i, 0)), sem=("parallel",))(parts)


def _adamw(parts, w, m, v, name):
    P, R, C = parts.shape
    tr = R if R <= 256 else _tile_rows(R, 256)

    def body(p_ref, w_ref, m_ref, v_ref, g_ref, d_ref, nm_ref, nv_ref):
        g = p_ref[0]
        for k in range(1, P):
            g = g + p_ref[k]
        nm = ADAM_B1 * m_ref[...] + (1.0 - ADAM_B1) * g
        nv = ADAM_B2 * v_ref[...] + (1.0 - ADAM_B2) * (g * g)
        m_hat = nm / (1.0 - ADAM_B1 ** ADAM_STEP)
        v_hat = nv / (1.0 - ADAM_B2 ** ADAM_STEP)
        g_ref[...] = g
        d_ref[...] = -ADAM_LR * (m_hat / (jnp.sqrt(v_hat) + ADAM_EPS) + ADAM_WD * w_ref[...])
        nm_ref[...] = nm
        nv_ref[...] = nv

    blk = pl.BlockSpec((tr, C), lambda i: (i, 0))
    return _pcall(
        body, name=name, out_shape=[_sds((R, C))] * 4, grid=(R // tr,),
        in_specs=[pl.BlockSpec((P, tr, C), lambda i: (0, i, 0)), blk, blk, blk],
        out_specs=[blk] * 4, sem=("parallel",))(parts, w, m, v)


def _tile_rows(n, pref):
    best = None
    for t in range(8, pref + 1, 8):
        if n % t == 0:
            best = t
    assert best is not None, (n, pref)
    return best


def _mm(pairs, mode, *, name, out_dtype=F32, bias=None, resid=None, gvec=None, keep=False, tn_pref=1024):
    M = pairs[0][0].shape[0]
    N = pairs[0][1].shape[1] if mode == "nn" else pairs[0][1].shape[0]
    tm, tn = _rows(M), _tile(N, tn_pref)
    n_pairs = len(pairs)
    has_bias, has_res = bias is not None, resid is not None

    def body(*refs):
        ab = refs[:2 * n_pairs]
        pos = 2 * n_pairs
        b_ref = refs[pos] if has_bias else None
        pos += has_bias
        r_ref, g_ref = (refs[pos], refs[pos + 1]) if has_res else (None, None)
        pos += 2 * has_res
        outs = refs[pos:]
        acc = None
        for p in range(n_pairs):
            a, b = _bf(ab[2 * p][...]), _bf(ab[2 * p + 1][...])
            d = _nn(a, b) if mode == "nn" else _nt(a, b)
            acc = d if acc is None else acc + d
        if has_bias:
            acc = acc + b_ref[...]
        if has_res:
            outs[0][...] = (r_ref[...] + g_ref[...] * acc).astype(outs[0].dtype)
            if keep:
                outs[1][...] = acc
        else:
            outs[0][...] = acc.astype(outs[0].dtype)

    in_specs, args = [], []
    for a, b in pairs:
        K = a.shape[1]
        in_specs.append(pl.BlockSpec((tm, K), lambda j, i: (i, 0)))
        if mode == "nn":
            in_specs.append(pl.BlockSpec((K, tn), lambda j, i: (0, j)))
        else:
            in_specs.append(pl.BlockSpec((tn, K), lambda j, i: (j, 0)))
        args += [a, b]
    if has_bias:
        in_specs.append(pl.BlockSpec((1, tn), lambda j, i: (0, j)))
        args.append(bias)
    if has_res:
        in_specs.append(pl.BlockSpec((tm, tn), lambda j, i: (i, j)))
        in_specs.append(pl.BlockSpec((1, tn), lambda j, i: (0, j)))
        args += [resid, gvec]
    o_spec = pl.BlockSpec((tm, tn), lambda j, i: (i, j))
    n_out = 2 if (has_res and keep) else 1
    out_shape = [_sds((M, N), out_dtype)] + ([_sds((M, N), F32)] if n_out == 2 else [])
    res = _pcall(body, name=name, out_shape=out_shape, grid=(N // tn, M // tm), in_specs=in_specs,
                 out_specs=[o_spec] * n_out, sem=("parallel", "parallel"))(*args)
    return res if n_out == 2 else res[0]


def _mm_tn(a, b, *, name, tm_pref=1408, tn_pref=1536):
    K, M = a.shape
    N = b.shape[1]
    tm, tn = _tile(M, tm_pref), _tile(N, tn_pref)
    tk = K if K <= ROW_TILE else ROW_TILE

    def body(a_ref, b_ref, o_ref):
        @pl.when(pl.program_id(2) == 0)
        def _():
            o_ref[...] = jnp.zeros_like(o_ref)
        o_ref[...] += _tn(_bf(a_ref[...]), _bf(b_ref[...]))

    return _pcall(
        body, name=name, out_shape=_sds((M, N)), grid=(M // tm, N // tn, K // tk),
        in_specs=[pl.BlockSpec((tk, tm), lambda i, j, k: (k, i)), pl.BlockSpec((tk, tn), lambda i, j, k: (k, j))],
        out_specs=pl.BlockSpec((tm, tn), lambda i, j, k: (i, j)),
        sem=("parallel", "parallel", "arbitrary"))(a, b)


def _mm_swiglu(h, wg_t, wu_t, name):
    M, K = h.shape
    N = wg_t.shape[0]
    tm, tn = _rows(M), _tile(N, 1408)

    def body(h_ref, wg_ref, wu_ref, gate_ref, up_ref, act_ref):
        hv = _bf(h_ref[...])
        gate = _nt(hv, wg_ref[...])
        up = _nt(hv, wu_ref[...])
        gate_ref[...] = gate
        up_ref[...] = up
        act_ref[...] = (_silu(gate) * up).astype(BF16)

    w_spec = pl.BlockSpec((tn, K), lambda j, i: (j, 0))
    o_spec = pl.BlockSpec((tm, tn), lambda j, i: (i, j))
    return _pcall(
        body, name=name, out_shape=[_sds((M, N)), _sds((M, N)), _sds((M, N), BF16)], grid=(N // tn, M // tm),
        in_specs=[pl.BlockSpec((tm, K), lambda j, i: (i, 0)), w_spec, w_spec], out_specs=[o_spec] * 3,
        sem=("parallel", "parallel"))(h, wg_t, wu_t)


def _mm_swiglu_bwd(dout, wd, gate, up, name):
    M, K = dout.shape
    N = wd.shape[0]
    tm, tn = _rows(M), _tile(N, 1408)

    def body(d_ref, wd_ref, gate_ref, up_ref, dg_ref, du_ref):
        dact = _nt(_bf(d_ref[...]), wd_ref[...])
        g = gate_ref[...]
        dg_ref[...] = (dact * up_ref[...] * _dsilu(g)).astype(BF16)
        du_ref[...] = (dact * _silu(g)).astype(BF16)

    t_spec = pl.BlockSpec((tm, tn), lambda j, i: (i, j))
    return _pcall(
        body, name=name, out_shape=[_sds((M, N), BF16)] * 2, grid=(N // tn, M // tm),
        in_specs=[pl.BlockSpec((tm, K), lambda j, i: (i, 0)), pl.BlockSpec((tn, K), lambda j, i: (j, 0)), t_spec, t_spec],
        out_specs=[t_spec] * 2, sem=("parallel", "parallel"))(dout, wd, gate, up)


def _norm_mod(x, w, sc, sh, name):
    T = x.shape[0]
    tm = _rows(T)

    def body(x_ref, w_ref, sc_ref, sh_ref, o_ref):
        xv = x_ref[...]
        r = lax.rsqrt(jnp.mean(xv * xv, axis=-1, keepdims=True) + EPS)
        o_ref[...] = ((xv * r * w_ref[...]) * (1.0 + sc_ref[...]) + sh_ref[...]).astype(BF16)

    return _pcall(body, name=name, out_shape=_sds((T, D), BF16), grid=(T // tm,),
                  in_specs=[_row_spec(tm, D), _vec_spec(D), _vec_spec(D), _vec_spec(D)],
                  out_specs=_row_spec(tm, D), sem=("parallel",))(x, w, sc, sh)


def _norm_mod_bwd(x, dh, dres, w, sc, name):
    T = x.shape[0]
    tm = _rows(T)

    def body(x_ref, dh_ref, dres_ref, w_ref, sc_ref, dx_ref, acc_ref):
        @pl.when(pl.program_id(0) == 0)
        def _():
            acc_ref[...] = jnp.zeros_like(acc_ref)
        xv, dh_v, wv = x_ref[...], dh_ref[...], w_ref[...]
        r = lax.rsqrt(jnp.mean(xv * xv, axis=-1, keepdims=True) + EPS)
        n = xv * r
        dnw = dh_v * (1.0 + sc_ref[...])
        dn = dnw * wv
        dx_ref[...] = dres_ref[...] + r * (dn - n * jnp.mean(dn * n, axis=-1, keepdims=True))
        acc_ref[0:1, :] += _colsum(dh_v * (n * wv))
        acc_ref[1:2, :] += _colsum(dh_v)
        acc_ref[2:3, :] += _colsum(dnw * n)

    return _pcall(body, name=name, out_shape=[_sds((T, D)), _sds((8, D))], grid=(T // tm,),
                  in_specs=[_row_spec(tm, D), _row_spec(tm, D), _row_spec(tm, D), _vec_spec(D), _vec_spec(D)],
                  out_specs=[_row_spec(tm, D), _vec_spec(D, 8)], sem=("arbitrary",))(x, dh, dres, w, sc)


def _gate_bwd(dx, branch, g, name):
    T = dx.shape[0]
    tm = _rows(T)

    def body(dx_ref, br_ref, g_ref, db_ref, acc_ref):
        @pl.when(pl.program_id(0) == 0)
        def _():
            acc_ref[...] = jnp.zeros_like(acc_ref)
        dxv = dx_ref[...]
        db = g_ref[...] * dxv
        db_ref[...] = db.astype(BF16)
        acc_ref[0:1, :] += _colsum(dxv * br_ref[...])
        acc_ref[1:2, :] += _colsum(db)

    return _pcall(body, name=name, out_shape=[_sds((T, D), BF16), _sds((8, D))], grid=(T // tm,),
                  in_specs=[_row_spec(tm, D), _row_spec(tm, D), _vec_spec(D)],
                  out_specs=[_row_spec(tm, D), _vec_spec(D, 8)], sem=("arbitrary",))(dx, branch, g)


def _final_loss(x, wf, target, name):
    T = x.shape[0]
    tm = _rows(T)

    def body(x_ref, w_ref, t_ref, dx_ref, acc_ref):
        @pl.when(pl.program_id(0) == 0)
        def _():
            acc_ref[...] = jnp.zeros_like(acc_ref)
        xv, wv = x_ref[...], w_ref[...]
        r = lax.rsqrt(jnp.mean(xv * xv, axis=-1, keepdims=True) + EPS)
        n = xv * r
        err = n * wv - t_ref[...]
        dy = err * (1.0 / D)
        dn = dy * wv
        dx_ref[...] = r * (dn - n * jnp.mean(dn * n, axis=-1, keepdims=True))
        acc_ref[0:1, :] += _colsum(dy * n)
        acc_ref[1:2, :] += jnp.broadcast_to(_allsum(err * err) * (0.5 / D), (1, D))

    return _pcall(body, name=name, out_shape=[_sds((T, D)), _sds((8, D))], grid=(T // tm,),
                  in_specs=[_row_spec(tm, D), _vec_spec(D), _row_spec(tm, D)],
                  out_specs=[_row_spec(tm, D), _vec_spec(D, 8)], sem=("arbitrary",))(x, wf, target)


def _colsum_call(x, name):
    T, C = x.shape
    tm = _rows(T)

    def body(x_ref, o_ref):
        @pl.when(pl.program_id(0) == 0)
        def _():
            o_ref[...] = jnp.zeros_like(o_ref)
        o_ref[...] += _colsum(x_ref[...].astype(F32))

    return _pcall(body, name=name, out_shape=_sds((1, C)), grid=(T // tm,), in_specs=[_row_spec(tm, C)],
                  out_specs=_vec_spec(C), sem=("arbitrary",))(x)


def _mod_matmul(c_all, ada_w, name):
    n = ada_w.shape[2]

    def body(c_ref, w_ref, cond_ref, o_ref):
        cond = _silu(c_ref[...])
        cond_ref[...] = cond
        o_ref[0] = _nn(cond, w_ref[0])

    return _pcall(body, name=name, out_shape=[_sds((N_DEV, D)), _sds((DEPTH, N_DEV, n))], grid=(DEPTH,),
                  in_specs=[pl.BlockSpec((N_DEV, D), lambda l: (0, 0)), pl.BlockSpec((1, D, n), lambda l: (l, 0, 0))],
                  out_specs=[pl.BlockSpec((N_DEV, D), lambda l: (0, 0)), pl.BlockSpec((1, N_DEV, n), lambda l: (l, 0, 0))],
                  sem=("arbitrary",))(c_all, ada_w)


def _add_rows(a, b, name):
    def body(a_ref, b_ref, o_ref):
        o_ref[...] = a_ref[...] + b_ref[...]

    return _pcall(body, name=name, out_shape=_sds(a.shape))(a, b)


def _ada_w_grad(cond, dmod_cols, name):
    n = dmod_cols.shape[2]

    def body(c_ref, d_ref, o_ref):
        o_ref[0] = _tn(c_ref[...], d_ref[0])

    return _pcall(body, name=name, out_shape=_sds((DEPTH, D, n)), grid=(DEPTH,),
                  in_specs=[pl.BlockSpec((N_DEV, D), lambda l: (0, 0)), pl.BlockSpec((1, N_DEV, n), lambda l: (l, 0, 0))],
                  out_specs=pl.BlockSpec((1, D, n), lambda l: (l, 0, 0)), sem=("parallel",))(cond, dmod_cols)


def _conv_fwd(pm, conv_w, conv_b, name):
    T = pm.shape[0]
    tm = _rows(T)
    C = CONV_DIM

    def body(x_ref, prev_ref, w_ref, b_ref, o_ref):
        cur = x_ref[...]
        prev = jnp.where(pl.program_id(0) > 0, prev_ref[...], 0.0)
        cur8 = cur[0:8]
        row8 = lax.broadcasted_iota(jnp.int32, (8, C), 0)
        full = w_ref[3:4, :] * cur
        head = w_ref[3:4, :] * cur8
        for k in range(1, SSM_CONV):
            wk = w_ref[3 - k:4 - k, :]
            full = full + wk * pltpu.roll(cur, k, 0)
            head = head + wk * jnp.where(row8 < k, pltpu.roll(prev, k, 0), pltpu.roll(cur8, k, 0))
        o_ref[...] = full + b_ref[...]
        o_ref[0:8, :] = head + b_ref[...]

    return _pcall(
        body, name=name, out_shape=_sds((T, C)), grid=(T // tm,),
        in_specs=[pl.BlockSpec((tm, C), lambda i: (i, 2)),
                  pl.BlockSpec((8, C), lambda i: (jnp.maximum(i * (tm // 8) - 1, 0), 2)),
                  _vec_spec(C, SSM_CONV), _vec_spec(C)],
        out_specs=_row_spec(tm, C), sem=("parallel",))(pm, pm, conv_w, conv_b)


def _conv_bwd(dc, pm, conv_w, name):
    T = dc.shape[0]
    tm = _rows(T)
    C = CONV_DIM
    nt = T // tm

    def body(dc_ref, nxt_ref, x_ref, prev_ref, w_ref, dx_ref, acc_ref):
        i = pl.program_id(0)

        @pl.when(i == 0)
        def _():
            acc_ref[...] = jnp.zeros_like(acc_ref)
        dcv = dc_ref[...]
        nxt = jnp.where(i < nt - 1, nxt_ref[...], 0.0)
        xc = x_ref[...]
        prev = jnp.where(i > 0, prev_ref[...], 0.0)
        dc8h, dc8t, x8 = dcv[0:8], dcv[tm - 8:tm], xc[0:8]
        row8 = lax.broadcasted_iota(jnp.int32, (8, C), 0)
        full = w_ref[3:4, :] * dcv
        tail = w_ref[3:4, :] * dc8t
        acc_ref[3:4, :] += _colsum(dcv * xc)
        for k in range(1, SSM_CONV):
            wk = w_ref[3 - k:4 - k, :]
            full = full + wk * pltpu.roll(dcv, tm - k, 0)
            tail = tail + wk * jnp.where(row8 + k >= 8, pltpu.roll(nxt, 8 - k, 0), pltpu.roll(dc8t, 8 - k, 0))
            xs_head = jnp.where(row8 < k, pltpu.roll(prev, k, 0), pltpu.roll(x8, k, 0))
            prod = dcv * pltpu.roll(xc, k, 0)
            acc_ref[3 - k:4 - k, :] += _colsum(prod) - _colsum(prod[0:8]) + _colsum(dc8h * xs_head)
        acc_ref[4:5, :] += _colsum(dcv)
        dx_ref[...] = full
        dx_ref[tm - 8:tm, :] = tail

    return _pcall(
        body, name=name, out_shape=[_sds((T, C)), _sds((8, C))], grid=(nt,),
        in_specs=[_row_spec(tm, C),
                  pl.BlockSpec((8, C), lambda i: (jnp.minimum((i + 1) * (tm // 8), T // 8 - 1), 0)),
                  pl.BlockSpec((tm, C), lambda i: (i, 2)),
                  pl.BlockSpec((8, C), lambda i: (jnp.maximum(i * (tm // 8) - 1, 0), 2)),
                  _vec_spec(C, SSM_CONV)],
        out_specs=[_row_spec(tm, C), _vec_spec(C, 8)], sem=("arbitrary",))(dc, dc, pm, pm, conv_w)


def _ssd_prologue(cpre, dtr, dtb, alog):
    L = CHUNK
    xc = _silu(cpre)
    pre = dtr + dtb
    dt = jnp.maximum(pre, 0.0) + jnp.log1p(jnp.exp(-jnp.abs(pre)))
    a = -jnp.exp(alog)
    la = dt * a
    row = lax.broadcasted_iota(jnp.int32, (L, L), 0)
    col = lax.broadcasted_iota(jnp.int32, (L, L), 1)
    causal = row >= col
    tri = causal.astype(F32)
    lc = _nn(tri, la, HIGHEST)
    return xc, pre, dt, a, causal, tri, lc, row, col


def _head_indicator():
    m = np.zeros((LANES, SSM_INNER), np.float32)
    for h in range(SSM_HEADS):
        m[h, h * SSM_HEAD_DIM:(h + 1) * SSM_HEAD_DIM] = 1.0
    return jnp.asarray(m, dtype=BF16)


def _split_dot(x, ind, dims):
    hi = x.astype(BF16)
    lo = (x - hi.astype(F32)).astype(BF16)
    return _dot(hi, ind, dims) + _dot(lo, ind, dims)


def _expand(x16, ind):
    return _split_dot(x16, ind, ((1,), (0,)))


def _headsum(x, ind):
    return _split_dot(x, ind, ((1,), (1,)))


def _ssd_fwd(cpre, dtr, pm, dtb, alog, dskip, normw, ind, name):
    T = cpre.shape[0]
    nc = T // CHUNK
    L, P, H, HPG, N = CHUNK, SSM_HEAD_DIM, SSM_HEADS, SSM_HEADS // SSM_GROUPS, SSM_STATE
    half = SSM_INNER // SSM_GROUPS

    def body(cp_ref, dtr_ref, z_ref, dtb_ref, alog_ref, dskip_ref, nw_ref, ind_ref, ya_ref, y_ref, sp_ref, st_ref):
        @pl.when(pl.program_id(0) == 0)
        def _():
            st_ref[...] = jnp.zeros_like(st_ref)
        xc, _, dt, _, causal, _, lc, _, _ = _ssd_prologue(cp_ref[...], dtr_ref[...], dtb_ref[...], alog_ref[...])
        lct = lc.T
        ind = ind_ref[...]
        llast = lc[L - 1:L, :]
        xs = xc[:, :SSM_INNER]
        xd = xs * _expand(dt, ind)
        ex = _expand(jnp.exp(lc), ind)
        xd_end = _bf(xd * _expand(jnp.exp(llast - lc), ind))
        cdx = _expand(jnp.broadcast_to(jnp.exp(llast), (8, LANES)), ind)[0:1]
        xdb = _bf(xd)
        sp_ref[0] = st_ref[...]
        for g in range(SSM_GROUPS):
            sl = slice(g * half, (g + 1) * half)
            bm = _bf(xc[:, SSM_INNER + g * N:SSM_INNER + (g + 1) * N])
            cm = _bf(xc[:, SSM_INNER + (SSM_GROUPS + g) * N:SSM_INNER + (SSM_GROUPS + g + 1) * N])
            cb = _nt(cm, bm)
            st = st_ref[g]
            y_ref[:, sl] = ex[:, sl] * _nn(cm, _bf(st)) + dskip_ref[:, sl] * xs[:, sl]
            st_ref[g] = st * cdx[:, sl] + _tn(bm, xd_end[:, sl])
            for j in range(HPG):
                h = g * HPG + j
                decay = jnp.where(causal, jnp.exp(jnp.where(causal, lc[:, h:h + 1] - lct[h:h + 1, :], 0.0)), 0.0)
                y_ref[:, h * P:(h + 1) * P] += _nn(_bf(cb * decay), xdb[:, h * P:(h + 1) * P])
        y2 = y_ref[...] * _silu(z_ref[...])
        for g in range(SSM_GROUPS):
            yg = y2[:, g * half:(g + 1) * half]
            r = lax.rsqrt(jnp.mean(yg * yg, axis=-1, keepdims=True) + EPS)
            ya_ref[:, g * half:(g + 1) * half] = (yg * r * nw_ref[:, g * half:(g + 1) * half]).astype(BF16)

    return _pcall(
        body, name=name,
        out_shape=[_sds((T, SSM_INNER), BF16), _sds((T, SSM_INNER)), _sds((nc, SSM_GROUPS, N, half))], grid=(nc,),
        in_specs=[_row_spec(L, CONV_DIM), _row_spec(L, LANES), _row_spec(L, SSM_INNER, 0),
                  _vec_spec(LANES), _vec_spec(LANES), _vec_spec(SSM_INNER), _vec_spec(SSM_INNER), _vec_spec(SSM_INNER, LANES)],
        out_specs=[_row_spec(L, SSM_INNER), _row_spec(L, SSM_INNER),
                   pl.BlockSpec((1, SSM_GROUPS, N, half), lambda i: (i, 0, 0, 0))],
        scratch=[pltpu.VMEM((SSM_GROUPS, N, half), F32)], sem=("arbitrary",))(cpre, dtr, pm, dtb, alog, dskip, normw, ind)


def _ssd_bwd(cpre, dtr, pm, ypre, sprev, dya, dtb, alog, dskip, normw, ind, name):
    T = cpre.shape[0]
    nc = T // CHUNK
    L, P, H, HPG, N = CHUNK, SSM_HEAD_DIM, SSM_HEADS, SSM_HEADS // SSM_GROUPS, SSM_STATE
    half = SSM_INNER // SSM_GROUPS

    def body(cp_ref, dtr_ref, z_ref, y_ref, sp_ref, dya_ref, dtb_ref, alog_ref, dskip_ref, nw_ref, ind_ref,
             dz_ref, dcp_ref, ddtr_ref, acc_ref, dnw_ref, ds_ref, dy_ref, dxd_ref, rr_ref, yoff_ref, dcd_ref):
        @pl.when(pl.program_id(0) == 0)
        def _():
            ds_ref[...] = jnp.zeros_like(ds_ref)
            acc_ref[...] = jnp.zeros_like(acc_ref)
            dnw_ref[...] = jnp.zeros_like(dnw_ref)
        cpre_v = cp_ref[...]
        xc, pre, dt, a, causal, tri, lc, row, col = _ssd_prologue(cpre_v, dtr_ref[...], dtb_ref[...], alog_ref[...])
        lct = lc.T
        zv, yv = z_ref[...], y_ref[...]
        sz = _silu(zv)
        y2 = yv * sz
        dya_v = dya_ref[...]
        nwv = nw_ref[...]
        for g in range(SSM_GROUPS):
            sl = slice(g * half, (g + 1) * half)
            yg = y2[:, sl]
            r = lax.rsqrt(jnp.mean(yg * yg, axis=-1, keepdims=True) + EPS)
            nrm = yg * r
            dnw_ref[:, sl] += _colsum(dya_v[:, sl] * nrm)
            dn = dya_v[:, sl] * nwv[:, sl]
            dy2 = r * (dn - nrm * jnp.mean(dn * nrm, axis=-1, keepdims=True))
            dy_ref[:, sl] = dy2 * sz[:, sl]
            dz_ref[:, sl] = dy2 * yv[:, sl] * _dsilu(zv[:, sl])
        ind = ind_ref[...]
        llast = lc[L - 1:L, :]
        dte16 = jnp.exp(llast - lc)
        cd16 = jnp.exp(llast)
        xs = xc[:, :SSM_INNER]
        dtx = _expand(dt, ind)
        ex = _expand(jnp.exp(lc), ind)
        dtex = _expand(dte16, ind)
        cdx = _expand(jnp.broadcast_to(cd16, (8, LANES)), ind)[0:1]
        xd = xs * dtx
        xdb = _bf(xd)
        xd_end = _bf(xd * dtex)
        dyv = dy_ref[...]
        dy_off = _bf(ex * dyv)
        dyb = _bf(dyv)
        dskx = dskip_ref[...]
        lane_c = lax.broadcasted_iota(jnp.int32, (L, LANES), 1)
        lane1 = lax.broadcasted_iota(jnp.int32, (1, LANES), 1)
        sub16 = lax.broadcasted_iota(jnp.int32, (H, L), 0)
        dlc_c = jnp.zeros((L, LANES), F32)
        dlc_r = jnp.zeros((H, L), F32)
        for g in range(SSM_GROUPS):
            sl = slice(g * half, (g + 1) * half)
            b_lo = SSM_INNER + g * N
            c_lo = SSM_INNER + (SSM_GROUPS + g) * N
            bm, cm = _bf(xc[:, b_lo:b_lo + N]), _bf(xc[:, c_lo:c_lo + N])
            cb = _nt(cm, bm)
            st, dst = sp_ref[0, g], ds_ref[g]
            stb, dstb = _bf(st), _bf(dst)
            dcm = _nt(dy_off[:, sl], stb)
            ds_ref[g] = _tn(cm, dy_off[:, sl]) + dst * cdx[:, sl]
            rr_ref[:, sl] = _nn(bm, dstb)
            yoff_ref[:, sl] = ex[:, sl] * _nn(cm, stb)
            db = _nt(xd_end[:, sl], dstb)
            dcd_ref[:, sl] = _colsum(dst * st)
            dcb = jnp.zeros((L, L), F32)
            for j in range(HPG):
                h = g * HPG + j
                hs = slice(h * P, (h + 1) * P)
                decay = jnp.where(causal, jnp.exp(jnp.where(causal, lc[:, h:h + 1] - lct[h:h + 1, :], 0.0)), 0.0)
                m = cb * decay
                dxd_ref[:, hs] = _tn(_bf(m), dyb[:, hs])
                dm = _nt(dyb[:, hs], xdb[:, hs])
                dcb = dcb + dm * decay
                gm = dm * m
                dlc_c = dlc_c + jnp.where(lane_c == h, _rowsum(gm), 0.0)
                dlc_r = dlc_r + jnp.where(sub16 == h, _colsum(gm), 0.0)
            dcbb = _bf(dcb)
            dcp_ref[:, c_lo:c_lo + N] = dcm + _nn(dcbb, bm)
            dcp_ref[:, b_lo:b_lo + N] = db + _tn(dcbb, cm)
        dxd_diag, rr = dxd_ref[...], rr_ref[...]
        tt = _headsum(rr * xd, ind) * dte16
        dlc_rt = jnp.concatenate([dlc_r, jnp.zeros((LANES - H, L), F32)], axis=0).T
        dlc = dlc_c - dlc_rt + _headsum(dyv * yoff_ref[...], ind) - tt
        dcd = _headsum(jnp.broadcast_to(dcd_ref[...], (8, SSM_INNER)), ind)[0:1]
        dlc = dlc + jnp.where(row == L - 1, _colsum(tt) + dcd * cd16, 0.0)
        dla = _tn(tri, dlc, HIGHEST)
        dxd = dxd_diag + dtex * rr
        ddt = _headsum(dxd * xs, ind) + dla * a
        ddtr = jnp.where(lane_c < H, ddt * _sigmoid(pre), 0.0)
        ddtr_ref[...] = ddtr
        acc_ref[0:1, :] += _colsum(ddtr)
        acc_ref[1:2, :] += jnp.where(lane1 < H, _colsum(dla * dt) * a, 0.0)
        acc_ref[2:3, :] += _headsum(jnp.broadcast_to(_colsum(dyv * xs), (8, SSM_INNER)), ind)[0:1]
        dcp_ref[:, 0:SSM_INNER] = dxd * dtx + dskx * dyv
        dcp_ref[...] = dcp_ref[...] * _dsilu(cpre_v)

    rev = lambda i: (nc - 1 - i, 0)
    rspec = lambda c: pl.BlockSpec((L, c), rev)
    return _pcall(
        body, name=name,
        out_shape=[_sds((T, SSM_INNER)), _sds((T, CONV_DIM)), _sds((T, LANES)), _sds((8, LANES)), _sds((1, SSM_INNER))],
        grid=(nc,),
        in_specs=[rspec(CONV_DIM), rspec(LANES), rspec(SSM_INNER), rspec(SSM_INNER),
                  pl.BlockSpec((1, SSM_GROUPS, N, half), lambda i: (nc - 1 - i, 0, 0, 0)), rspec(SSM_INNER),
                  _vec_spec(LANES), _vec_spec(LANES), _vec_spec(SSM_INNER), _vec_spec(SSM_INNER), _vec_spec(SSM_INNER, LANES)],
        out_specs=[rspec(SSM_INNER), rspec(CONV_DIM), rspec(LANES), _vec_spec(LANES, 8), _vec_spec(SSM_INNER)],
        scratch=[pltpu.VMEM((SSM_GROUPS, N, half), F32), pltpu.VMEM((L, SSM_INNER), F32), pltpu.VMEM((L, SSM_INNER), F32),
                 pltpu.VMEM((L, SSM_INNER), F32), pltpu.VMEM((L, SSM_INNER), F32), pltpu.VMEM((1, SSM_INNER), F32)],
        sem=("arbitrary",))(cpre, dtr, pm, ypre, sprev, dya, dtb, alog, dskip, normw, ind)


def _gmlp_common(u, v, lnw, lnb):
    ug = _gelu(u)
    vg = _gelu(v)
    mu = jnp.mean(vg, axis=-1, keepdims=True)
    cen = vg - mu
    rstd = lax.rsqrt(jnp.mean(cen * cen, axis=-1, keepdims=True) + EPS)
    vhat = cen * rstd
    return ug, rstd, vhat, vhat * lnw + lnb


def _causal_mask():
    row = lax.broadcasted_iota(jnp.int32, (CHUNK, CHUNK), 0)
    col = lax.broadcasted_iota(jnp.int32, (CHUNK, CHUNK), 1)
    return row >= col


def _gmlp_fwd(pm, lnw, lnb, ws, bs_exp, name):
    T = pm.shape[0]
    nc = T // CHUNK
    L, G = CHUNK, GMLP_GROUPS

    def body(u_ref, v_ref, lnw_ref, lnb_ref, ws_ref, bs_ref, o_ref):
        ug, _, _, vn = _gmlp_common(u_ref[...], v_ref[...], lnw_ref[...], lnb_ref[...])
        causal = _causal_mask()
        for g in range(G):
            sl = slice(g * L, (g + 1) * L)
            wm = _bf(jnp.where(causal, ws_ref[g], 0.0))
            sv = _nn(wm, _bf(vn[:, sl])) + bs_ref[:, sl]
            o_ref[:, sl] = (ug[:, sl] * sv).astype(BF16)

    return _pcall(
        body, name=name, out_shape=_sds((T, GMLP_INNER), BF16), grid=(nc,),
        in_specs=[_row_spec(L, GMLP_INNER, 1), _row_spec(L, GMLP_INNER, 2), _vec_spec(GMLP_INNER), _vec_spec(GMLP_INNER),
                  pl.BlockSpec((G, L, L), lambda i: (0, 0, 0)), _vec_spec(GMLP_INNER, L)],
        out_specs=_row_spec(L, GMLP_INNER), sem=("parallel",))(pm, pm, lnw, lnb, ws, bs_exp)


def _gmlp_bwd(pm, dyb, lnw, lnb, ws, bs_exp, name):
    T = pm.shape[0]
    nc = T // CHUNK
    L, G = CHUNK, GMLP_GROUPS

    def body(u_ref, v_ref, dy_ref, lnw_ref, lnb_ref, ws_ref, bs_ref, du_ref, dv_ref, dws_ref, dbs_ref, acc_ref, dvn_ref):
        @pl.when(pl.program_id(0) == 0)
        def _():
            dws_ref[...] = jnp.zeros_like(dws_ref)
            dbs_ref[...] = jnp.zeros_like(dbs_ref)
            acc_ref[...] = jnp.zeros_like(acc_ref)
        uv, vv, dyv, lnwv = u_ref[...], v_ref[...], dy_ref[...], lnw_ref[...]
        ug, rstd, vhat, vn = _gmlp_common(uv, vv, lnwv, lnb_ref[...])
        causal = _causal_mask()
        lane = lax.broadcasted_iota(jnp.int32, (L, LANES), 1)
        dbs = jnp.zeros((L, LANES), F32)
        for g in range(G):
            sl = slice(g * L, (g + 1) * L)
            wm = _bf(jnp.where(causal, ws_ref[g], 0.0))
            vng = _bf(vn[:, sl])
            sv = _nn(wm, vng) + bs_ref[:, sl]
            du_ref[:, sl] = dyv[:, sl] * sv * _dgelu(uv[:, sl])
            dsv = dyv[:, sl] * ug[:, sl]
            dsvb = _bf(dsv)
            dws_ref[g] += jnp.where(causal, _nt(dsvb, vng), 0.0)
            dbs = dbs + jnp.where(lane == g, _rowsum(dsv), 0.0)
            dvn_ref[:, sl] = _tn(wm, dsvb)
        dbs_ref[...] += dbs
        dvn = dvn_ref[...]
        acc_ref[0:1, :] += _colsum(dvn * vhat)
        acc_ref[1:2, :] += _colsum(dvn)
        dvh = dvn * lnwv
        dvg = rstd * (dvh - jnp.mean(dvh, axis=-1, keepdims=True) - vhat * jnp.mean(dvh * vhat, axis=-1, keepdims=True))
        dv_ref[...] = dvg * _dgelu(vv)

    return _pcall(
        body, name=name,
        out_shape=[_sds((T, GMLP_INNER)), _sds((T, GMLP_INNER)), _sds((G, L, L)), _sds((L, LANES)), _sds((8, GMLP_INNER))],
        grid=(nc,),
        in_specs=[_row_spec(L, GMLP_INNER, 1), _row_spec(L, GMLP_INNER, 2), _row_spec(L, GMLP_INNER),
                  _vec_spec(GMLP_INNER), _vec_spec(GMLP_INNER), pl.BlockSpec((G, L, L), lambda i: (0, 0, 0)),
                  _vec_spec(GMLP_INNER, L)],
        out_specs=[_row_spec(L, GMLP_INNER), _row_spec(L, GMLP_INNER), pl.BlockSpec((G, L, L), lambda i: (0, 0, 0)),
                   _vec_spec(LANES, L), _vec_spec(GMLP_INNER, 8)],
        scratch=[pltpu.VMEM((L, GMLP_INNER), F32)], sem=("arbitrary",))(pm, pm, dyb, lnw, lnb, ws, bs_exp)


def _rel_buckets():
    qi = np.arange(CHUNK)[:, None]
    sj = np.arange(2 * CHUNK)[None, :]
    dist = np.maximum(qi + CHUNK - sj, 0)
    max_exact = REL_BUCKETS // 2
    log_ratio = (np.log(np.maximum(dist, 1).astype(np.float32) / np.float32(max_exact))
                 / np.float32(math.log(REL_MAX_DIST / max_exact))).astype(np.float32)
    large = max_exact + (log_ratio * np.float32(REL_BUCKETS - max_exact)).astype(np.int32)
    return np.where(dist < max_exact, dist, np.minimum(large, REL_BUCKETS - 1))


def _bucket_onehot_t():
    bucket = _rel_buckets().reshape(-1)
    return jnp.asarray((np.arange(REL_BUCKETS)[:, None] == bucket[None, :]).astype(np.float32))


def _bias_from_table(table_t, onehot_t, name):
    def body(t_ref, o_ref, out_ref):
        out_ref[...] = _nn(t_ref[...], o_ref[...], HIGHEST)

    return _pcall(body, name=name, out_shape=_sds((ATTN_HEADS, onehot_t.shape[1])))(table_t, onehot_t)


def _table_from_dbias(dbias, onehot_t, name):
    def body(d_ref, o_ref, out_ref):
        out_ref[...] = _nt(d_ref[...], o_ref[...], HIGHEST)

    return _pcall(body, name=name, out_shape=_sds((ATTN_HEADS, REL_BUCKETS)))(dbias, onehot_t)


def _softmax_sink(logits, sink, mask):
    logits = jnp.where(mask, logits, NEG_INF)
    mx = jnp.maximum(jnp.max(logits, axis=-1, keepdims=True), sink)
    e = jnp.exp(logits - mx)
    es = jnp.exp(sink - mx)
    inv = 1.0 / (_rowsum(e) + es)
    return e * inv, es * inv


def _attn_mask(n, heads):
    qi = lax.broadcasted_iota(jnp.int32, (heads * CHUNK, 2 * CHUNK), 0) & (CHUNK - 1)
    sj = lax.broadcasted_iota(jnp.int32, (heads * CHUNK, 2 * CHUNK), 1)
    rel = qi + CHUNK - sj
    return (rel >= 0) & (rel < CHUNK) & ((sj >= CHUNK) | (n > 0))


def _stack_heads(ref, first, count, width):
    return jnp.concatenate([_bf(ref[:, (first + j) * width:(first + j + 1) * width]) for j in range(count)], axis=0)


def _attn_fwd(qkv, bias, sinks, name):
    T = qkv.shape[0]
    nb = T // CHUNK
    L, DH, HPK = CHUNK, ATTN_DH, ATTN_HEADS // ATTN_KV
    scale = DH ** -0.5
    kcol, vcol = ATTN_HEADS * DH // LANES, ATTN_HEADS * DH // LANES + 1

    def body(q_ref, k_ref, v_ref, kp_ref, vp_ref, bias_ref, sink_ref, o_ref, lg_ref, p_ref):
        n = pl.program_id(0)
        mask = _attn_mask(n, 1)
        kband = _bf(jnp.concatenate([kp_ref[...], k_ref[...]], axis=0))
        vband = _bf(jnp.concatenate([vp_ref[...], v_ref[...]], axis=0))
        for kv in range(ATTN_KV):
            lg_ref[...] = _nt(_stack_heads(q_ref, kv * HPK, HPK, DH), kband[:, kv * DH:(kv + 1) * DH])
            for j in range(HPK):
                h = kv * HPK + j
                p, _ = _softmax_sink(lg_ref[j * L:(j + 1) * L, :] * scale + bias_ref[h], sink_ref[h], mask)
                p_ref[j * L:(j + 1) * L, :] = _bf(p)
            og = _nn(p_ref[...], vband[:, kv * DH:(kv + 1) * DH])
            for j in range(HPK):
                h = kv * HPK + j
                o_ref[:, h * DH:(h + 1) * DH] = og[j * L:(j + 1) * L].astype(BF16)

    prev = lambda i: jnp.maximum(i - 1, 0)
    return _pcall(
        body, name=name, out_shape=_sds((T, ATTN_HEADS * DH), BF16), grid=(nb,),
        in_specs=[_row_spec(L, ATTN_HEADS * DH, 0), _row_spec(L, LANES, kcol), _row_spec(L, LANES, vcol),
                  pl.BlockSpec((L, LANES), lambda i: (prev(i), kcol)), pl.BlockSpec((L, LANES), lambda i: (prev(i), vcol)),
                  pl.BlockSpec((ATTN_HEADS, L, 2 * L), lambda i: (0, 0, 0)),
                  pl.BlockSpec(memory_space=pltpu.SMEM)],
        out_specs=_row_spec(L, ATTN_HEADS * DH),
        scratch=[pltpu.VMEM((HPK * L, 2 * L), F32), pltpu.VMEM((HPK * L, 2 * L), BF16)],
        sem=("parallel",))(qkv, qkv, qkv, qkv, qkv, bias, sinks)


def _attn_bwd(qkv, datt, bias, sinks, name):
    T = qkv.shape[0]
    nb = T // CHUNK
    L, DH, HPK = CHUNK, ATTN_DH, ATTN_HEADS // ATTN_KV
    scale = DH ** -0.5
    kcol, vcol = ATTN_HEADS * DH // LANES, ATTN_HEADS * DH // LANES + 1

    def body(q_ref, k_ref, v_ref, kp_ref, vp_ref, do_ref, bias_ref, sink_ref,
             dq_ref, dk_ref, dv_ref, dbias_ref, dsink_ref, pend_k, pend_v, band_k, band_v, lg_ref, dp_ref, p_ref, dl_ref):
        n = pl.program_id(0)

        @pl.when(n == 0)
        def _():
            dbias_ref[...] = jnp.zeros_like(dbias_ref)
            dsink_ref[...] = jnp.zeros_like(dsink_ref)

        @pl.when(n < nb)
        def _():
            mask = _attn_mask(n, 1)
            kband = _bf(jnp.concatenate([kp_ref[...], k_ref[...]], axis=0))
            vband = _bf(jnp.concatenate([vp_ref[...], v_ref[...]], axis=0))
            lane1 = lax.broadcasted_iota(jnp.int32, (1, LANES), 1)
            dsink = jnp.zeros((1, LANES), F32)
            for kv in range(ATTN_KV):
                kb, vb = kband[:, kv * DH:(kv + 1) * DH], vband[:, kv * DH:(kv + 1) * DH]
                qg = _stack_heads(q_ref, kv * HPK, HPK, DH)
                dog = _stack_heads(do_ref, kv * HPK, HPK, DH)
                lg_ref[...] = _nt(qg, kb)
                dp_ref[...] = _nt(dog, vb)
                for j in range(HPK):
                    h = kv * HPK + j
                    rows = slice(j * L, (j + 1) * L)
                    p, ps = _softmax_sink(lg_ref[rows, :] * scale + bias_ref[h], sink_ref[h], mask)
                    dp = dp_ref[rows, :]
                    delta = _rowsum(p * dp)
                    dl = p * (dp - delta)
                    dbias_ref[h] += dl
                    p_ref[rows, :] = _bf(p)
                    dl_ref[rows, :] = _bf(dl)
                    dsink = dsink + jnp.where(lane1 == h, -_colsum(ps * delta), 0.0)
                band_v[:, kv * DH:(kv + 1) * DH] = _tn(p_ref[...], dog)
                dqg = _nn(dl_ref[...], kb) * scale
                band_k[:, kv * DH:(kv + 1) * DH] = _tn(dl_ref[...], qg) * scale
                for j in range(HPK):
                    h = kv * HPK + j
                    dq_ref[:, h * DH:(h + 1) * DH] = dqg[j * L:(j + 1) * L]
            dsink_ref[...] += dsink

            @pl.when(n > 0)
            def _():
                dk_ref[...] = pend_k[...] + band_k[0:L, :]
                dv_ref[...] = pend_v[...] + band_v[0:L, :]
            pend_k[...] = band_k[L:2 * L, :]
            pend_v[...] = band_v[L:2 * L, :]

        @pl.when(n == nb)
        def _():
            dk_ref[...] = pend_k[...]
            dv_ref[...] = pend_v[...]

    cur = lambda i: jnp.minimum(i, nb - 1)
    prev = lambda i: jnp.maximum(jnp.minimum(i, nb - 1) - 1, 0)
    lag = lambda i: jnp.maximum(i - 1, 0)
    return _pcall(
        body, name=name,
        out_shape=[_sds((T, ATTN_HEADS * DH)), _sds((T, LANES)), _sds((T, LANES)), _sds((ATTN_HEADS, L, 2 * L)), _sds((1, LANES))],
        grid=(nb + 1,),
        in_specs=[pl.BlockSpec((L, ATTN_HEADS * DH), lambda i: (cur(i), 0)),
                  pl.BlockSpec((L, LANES), lambda i: (cur(i), kcol)), pl.BlockSpec((L, LANES), lambda i: (cur(i), vcol)),
                  pl.BlockSpec((L, LANES), lambda i: (prev(i), kcol)), pl.BlockSpec((L, LANES), lambda i: (prev(i), vcol)),
                  pl.BlockSpec((L, ATTN_HEADS * DH), lambda i: (cur(i), 0)),
                  pl.BlockSpec((ATTN_HEADS, L, 2 * L), lambda i: (0, 0, 0)),
                  pl.BlockSpec(memory_space=pltpu.SMEM)],
        out_specs=[pl.BlockSpec((L, ATTN_HEADS * DH), lambda i: (cur(i), 0)),
                   pl.BlockSpec((L, LANES), lambda i: (lag(i), 0)), pl.BlockSpec((L, LANES), lambda i: (lag(i), 0)),
                   pl.BlockSpec((ATTN_HEADS, L, 2 * L), lambda i: (0, 0, 0)), _vec_spec(LANES)],
        scratch=[pltpu.VMEM((L, LANES), F32), pltpu.VMEM((L, LANES), F32),
                 pltpu.VMEM((2 * L, LANES), F32), pltpu.VMEM((2 * L, LANES), F32),
                 pltpu.VMEM((HPK * L, 2 * L), F32), pltpu.VMEM((HPK * L, 2 * L), F32),
                 pltpu.VMEM((HPK * L, 2 * L), BF16), pltpu.VMEM((HPK * L, 2 * L), BF16)],
        sem=("arbitrary",))(qkv, qkv, qkv, qkv, qkv, datt, bias, sinks)


def _pad_rows(a, mult):
    pad = (-a.shape[-2]) % mult
    if pad == 0:
        return a
    cfg = [(0, 0)] * (a.ndim - 2) + [(0, pad), (0, 0)]
    return jnp.pad(a, cfg)


class _Pack:
    def __init__(self, width, mult, total_mult):
        self.width, self.mult, self.total_mult = width, mult, total_mult
        self.entries = []
        self.rows = 0

    def add(self, key, shape):
        n = int(np.prod(shape))
        rows = -(-n // self.width)
        self.entries.append((key, self.rows, rows, tuple(shape)))
        self.rows += -(-rows // self.mult) * self.mult

    @property
    def total(self):
        return -(-self.rows // self.total_mult) * self.total_mult

    def pack(self, pieces, dtype, lead=()):
        parts = []
        for key, _, rows, shape in self.entries:
            a = pieces[key].astype(dtype).reshape(lead + (-1,))
            n = int(np.prod(shape))
            a = jnp.pad(a, [(0, 0)] * len(lead) + [(0, rows * self.width - n)])
            a = a.reshape(lead + (rows, self.width))
            parts.append(_pad_rows(a, self.mult))
        out = jnp.concatenate(parts, axis=len(lead))
        return _pad_rows(out, self.total_mult)

    def unpack(self, packed, lead=()):
        out = {}
        for key, off, rows, shape in self.entries:
            a = lax.slice_in_dim(packed, off, off + rows, axis=len(lead))
            a = a.reshape(lead + (-1,))
            n = int(np.prod(shape))
            out[key] = lax.slice_in_dim(a, 0, n, axis=len(lead)).reshape(lead + shape)
        return out


def _ffn_fwd(x, mod, norm_w, wg_t, wu_t, wd, tag):
    h = _norm_mod(x, norm_w, mod[4:5], mod[3:4], f"ffn_norm_{tag}")
    gate, up, act = _mm_swiglu(h, wg_t, wu_t, f"ffn_gateup_{tag}")
    x_out, ffn_out = _mm([(act, wd)], "nn", name=f"ffn_down_{tag}", resid=x, gvec=mod[5:6], keep=True)
    return x_out, dict(h=h, gate=gate, up=up, act=act, out=ffn_out)


def _ffn_bwd(dx_out, x_in, saved, mod, norm_w, wg_t, wu_t, wd, tag):
    dffn, acc_g = _gate_bwd(dx_out, saved["out"], mod[5:6], f"ffn_gate_bwd_{tag}")
    dgate, dup = _mm_swiglu_bwd(dffn, wd, saved["gate"], saved["up"], f"ffn_act_bwd_{tag}")
    d_wd = _mm_tn(saved["act"], dffn, name=f"ffn_dwd_{tag}")
    d_wg_t = _mm_tn(dgate, saved["h"], name=f"ffn_dwg_{tag}")
    d_wu_t = _mm_tn(dup, saved["h"], name=f"ffn_dwu_{tag}")
    dh = _mm([(dgate, wg_t), (dup, wu_t)], "nn", name=f"ffn_dh_{tag}")
    dx, acc_n = _norm_mod_bwd(x_in, dh, dx_out, norm_w, mod[4:5], f"ffn_norm_bwd_{tag}")
    return dx, dict(d_wg=d_wg_t, d_wu=d_wu_t, d_wd=d_wd, d_g=acc_g[0], d_sc=acc_n[0], d_sh=acc_n[1], d_nw=acc_n[2])


_BIG = [
    ("out_w", "out_w_even", 0, "row"), ("qkv_w", "qkv_w", 0, "col"), ("o_w", "o_w", 0, "row"),
    ("gate0", "ffn_gate_w", 0, "col"), ("up0", "ffn_up_w", 0, "col"), ("down0", "ffn_down_w", 0, "row"),
    ("gate1", "ffn_gate_w", 1, "col"), ("up1", "ffn_up_w", 1, "col"), ("down1", "ffn_down_w", 1, "row"),
    ("in_w", "in_w_even", 0, "col"),
]


def _to_wire(a, kind):
    return a.T if kind == "col" else a

_REPLICATED = ["ada_b", "norm_mix_w", "norm_ffn_w", "conv_b", "dt_bias", "a_log", "d_skip", "ssm_norm_w", "gmlp_ln_w",
               "gmlp_ln_b", "gmlp_ws", "gmlp_bs", "sinks", "rel_table", "final_norm_w"]
_TINY_SHARDED = ["conv_w", "qkv_b", "o_b"]

_WEIGHTS = ['ada_w', 'ada_b', 'norm_mix_w', 'norm_ffn_w', 'in_w_even', 'conv_w', 'conv_b', 'dt_bias', 'a_log', 'd_skip',
            'ssm_norm_w', 'gmlp_ln_w', 'gmlp_ln_b', 'gmlp_ws', 'gmlp_bs', 'out_w_even', 'qkv_w', 'qkv_b', 'o_w', 'o_b',
            'sinks', 'rel_table', 'ffn_gate_w', 'ffn_up_w', 'ffn_down_w', 'final_norm_w']


def _step(x, c, loss_target, W, M, V):
    T = x.shape[1]
    x0 = x[0]
    target = loss_target[0]
    me = 4 * lax.axis_index("x") + 2 * lax.axis_index("y") + lax.axis_index("c")

    big = _Pack(D, 16, 128)
    wire = _Pack(D, 1, 128)
    for key, name, layer, kind in _BIG:
        big.add(key, W[name][layer].shape)
        wire.add(key, _to_wire(W[name][layer], kind).shape)
    w_wire_local = {key: _to_wire(W[name][layer].astype(BF16), kind) for key, name, layer, kind in _BIG}
    gathered = _all_gather(wire.pack(w_wire_local, BF16), "gather_weights")
    shards = wire.unpack(gathered, lead=(N_DEV,))
    full = {key: shards[key].reshape(-1, D) for key, name, layer, kind in _BIG}

    small_in = _Pack(D, 8, 8)
    small_in.add("c", (1, D))
    small_in.add("conv_w", W["conv_w"][0].shape)
    small_in.add("qkv_b", W["qkv_b"][0].shape)
    small_in.add("o_b", W["o_b"][0].shape)
    sm = small_in.unpack(_all_gather(small_in.pack(
        dict(c=c, conv_w=W["conv_w"][0], qkv_b=W["qkv_b"][0], o_b=W["o_b"][0]), F32), "gather_small"), lead=(N_DEV,))
    c_all = sm["c"].reshape(N_DEV, D)
    conv_w_full = jnp.transpose(sm["conv_w"], (1, 0, 2)).reshape(SSM_CONV, CONV_DIM)
    qkv_b_full = sm["qkv_b"].reshape(1, QKV_DIM)
    o_b_full = sm["o_b"].reshape(1, D)

    ncol = W["ada_w"].shape[2]
    cond, mod_cols = _mod_matmul(c_all, W["ada_w"], "mod_matmul")
    mod_g = _all_gather(mod_cols.reshape(DEPTH * N_DEV, ncol), "gather_mod").reshape(N_DEV, DEPTH, N_DEV, ncol)
    mod_me = lax.dynamic_index_in_dim(mod_g, me, axis=2, keepdims=False)
    mod_me = jnp.transpose(mod_me, (1, 0, 2)).reshape(DEPTH, 6, D)
    mod_me = jnp.pad(mod_me, ((0, 0), (0, 2), (0, 0))).reshape(DEPTH * 8, D)
    ada_b_rows = jnp.pad(W["ada_b"].reshape(DEPTH, 6, D), ((0, 0), (0, 2), (0, 0))).reshape(DEPTH * 8, D)
    mod_all = _add_rows(mod_me, ada_b_rows, "mod_bias").reshape(DEPTH, 8, D)
    mod0, mod1 = mod_all[0], mod_all[1]

    in_t = full["in_w"]
    o1, o2, o3, o4 = SSM_INNER, SSM_INNER + CONV_DIM, SSM_INNER + CONV_DIM + SSM_HEADS, SSM_INNER + CONV_DIM + SSM_HEADS + GMLP_INNER
    w_z, w_xbc, w_dt, w_u, w_v = in_t[:o1], in_t[o1:o2], in_t[o2:o3], in_t[o3:o4], in_t[o4:]
    w_main = jnp.concatenate([w_z, w_u, w_v, w_xbc], axis=0)
    w_dtp = jnp.pad(w_dt, ((0, LANES - SSM_HEADS), (0, 0)))
    out_w = full["out_w"]
    qkv_t, o_w = full["qkv_w"], full["o_w"]
    w_q, w_k, w_v_att = qkv_t[:D], qkv_t[D:D + LANES], qkv_t[D + LANES:]

    pad16 = lambda a: jnp.pad(a.reshape(1, SSM_HEADS), ((0, 0), (0, LANES - SSM_HEADS)))
    dtb, alog = pad16(W["dt_bias"][0]), pad16(W["a_log"][0])
    dskip = jnp.repeat(W["d_skip"][0], SSM_HEAD_DIM).reshape(1, SSM_INNER)
    ssm_nw = W["ssm_norm_w"]
    lnw, lnb = W["gmlp_ln_w"], W["gmlp_ln_b"]
    ws = W["gmlp_ws"][0]
    bs_exp = jnp.repeat(W["gmlp_bs"][0].T, CHUNK, axis=1)
    conv_b = W["conv_b"]
    nmw, nfw = W["norm_mix_w"], W["norm_ffn_w"]
    onehot_t = _bucket_onehot_t()
    head_ind = _head_indicator()
    bias = _bias_from_table(W["rel_table"].T, onehot_t, "rel_bias").reshape(ATTN_HEADS, CHUNK, 2 * CHUNK)
    sinks = W["sinks"][0]

    h0 = _norm_mod(x0, nmw[0:1], mod0[1:2], mod0[0:1], "mix_norm_0")
    pm = _mm([(h0, w_main)], "nt", name="in_proj", tn_pref=1536)
    dtr = _mm([(h0, w_dtp)], "nt", name="in_proj_dt")
    cpre = _conv_fwd(pm, conv_w_full, conv_b, "conv_fwd")
    ya, ypre, sprev = _ssd_fwd(cpre, dtr, pm, dtb, alog, dskip, ssm_nw, head_ind, "ssd_fwd")
    yb = _gmlp_fwd(pm, lnw, lnb, ws, bs_exp, "gmlp_fwd")
    x1, mix0 = _mm([(ya, out_w[:SSM_INNER]), (yb, out_w[SSM_INNER:])], "nn", name="out_proj", resid=x0, gvec=mod0[2:3], keep=True)
    x2, ffn0 = _ffn_fwd(x1, mod0, nfw[0:1], full["gate0"], full["up0"], full["down0"], "0")

    h1 = _norm_mod(x2, nmw[1:2], mod1[1:2], mod1[0:1], "mix_norm_1")
    qkv = _mm([(h1, qkv_t)], "nt", name="qkv_proj", bias=qkv_b_full, tn_pref=1280)
    att = _attn_fwd(qkv, bias, sinks, "attn_fwd")
    x3, mix1 = _mm([(att, o_w)], "nn", name="o_proj", bias=o_b_full, resid=x2, gvec=mod1[2:3], keep=True)
    x4, ffn1 = _ffn_fwd(x3, mod1, nfw[1:2], full["gate1"], full["up1"], full["down1"], "1")

    dx4, acc_f = _final_loss(x4, W["final_norm_w"].reshape(1, D), target, "final_loss")
    loss = lax.psum(acc_f[1, 0], ("x", "y", "c"))
    dx3, gf1 = _ffn_bwd(dx4, x3, ffn1, mod1, nfw[1:2], full["gate1"], full["up1"], full["down1"], "1")

    dmix1, acc_m1 = _gate_bwd(dx3, mix1, mod1[2:3], "mix_gate_bwd_1")
    datt = _mm([(dmix1, o_w)], "nt", name="o_proj_dx")
    d_o_w = _mm_tn(att, dmix1, name="o_proj_dw")
    dq, dk, dv, dbias, dsinks = _attn_bwd(qkv, datt, bias, sinks, "attn_bwd")
    d_table = _table_from_dbias(dbias.reshape(ATTN_HEADS, -1), onehot_t, "rel_table_grad").T
    d_qkv_t = jnp.concatenate([_mm_tn(dq, h1, name="qkv_dw_q"), _mm_tn(dk, h1, name="qkv_dw_k"), _mm_tn(dv, h1, name="qkv_dw_v")], axis=0)
    d_qkv_b = jnp.concatenate([_colsum_call(dq, "qkv_db_q"), _colsum_call(dk, "qkv_db_k"), _colsum_call(dv, "qkv_db_v")], axis=1)
    dh1 = _mm([(dq, w_q), (dk, w_k), (dv, w_v_att)], "nn", name="qkv_proj_dx")
    dx2, acc_n1 = _norm_mod_bwd(x2, dh1, dx3, nmw[1:2], mod1[1:2], "mix_norm_bwd_1")

    dx1, gf0 = _ffn_bwd(dx2, x1, ffn0, mod0, nfw[0:1], full["gate0"], full["up0"], full["down0"], "0")

    dmix0, acc_m0 = _gate_bwd(dx1, mix0, mod0[2:3], "mix_gate_bwd_0")
    dya = _mm([(dmix0, out_w[:SSM_INNER])], "nt", name="out_proj_dx_a")
    dyb = _mm([(dmix0, out_w[SSM_INNER:])], "nt", name="out_proj_dx_b")
    d_out_w = jnp.concatenate([_mm_tn(ya, dmix0, name="out_proj_dw_a"), _mm_tn(yb, dmix0, name="out_proj_dw_b")], axis=0)
    du, dvg, d_ws, d_bs, acc_ln = _gmlp_bwd(pm, dyb, lnw, lnb, ws, bs_exp, "gmlp_bwd")
    dz, dcpre, ddtr, acc_ssd, d_ssm_nw = _ssd_bwd(cpre, dtr, pm, ypre, sprev, dya, dtb, alog, dskip, ssm_nw, head_ind, "ssd_bwd")
    dxbc, acc_conv = _conv_bwd(dcpre, pm, conv_w_full, "conv_bwd")
    d_in_t = jnp.concatenate([
        _mm_tn(dz, h0, name="in_dw_z"), _mm_tn(dxbc, h0, name="in_dw_xbc"),
        _mm_tn(ddtr, h0, name="in_dw_dt")[:SSM_HEADS], _mm_tn(du, h0, name="in_dw_u"), _mm_tn(dvg, h0, name="in_dw_v")], axis=0)
    dh0 = _mm([(dz, w_z), (dxbc, w_xbc), (ddtr, w_dtp), (du, w_u), (dvg, w_v)], "nn", name="in_proj_dx")
    grad_x, acc_n0 = _norm_mod_bwd(x0, dh0, dx1, nmw[0:1], mod0[1:2], "mix_norm_bwd_0")

    g_wire = dict(in_w=d_in_t, out_w=d_out_w, qkv_w=d_qkv_t, o_w=d_o_w,
                  gate0=gf0["d_wg"], up0=gf0["d_wu"], down0=gf0["d_wd"], gate1=gf1["d_wg"], up1=gf1["d_wu"], down1=gf1["d_wd"])
    g_dest = {key: g_wire[key].reshape(N_DEV, -1, D) for key, name, layer, kind in _BIG}
    g_packed = wire.pack(g_dest, F32, lead=(N_DEV,))
    theirs = _exchange_sibling(g_packed, "exchange_grads_sibling")
    from_chips = _exchange_chips(_pair_sum(g_packed, theirs, "grads_pair_sum"), "exchange_grads_chips")
    g_mine = wire.unpack(_sum_parts(from_chips, "grads_chip_sum"))
    g_nat = {key: _to_wire(g_mine[key], kind) for key, name, layer, kind in _BIG}
    pk = lambda S: big.pack({key: S[name][layer] for key, name, layer, kind in _BIG}, F32)
    res_big = [big.unpack(r) for r in _adamw(big.pack(g_nat, F32)[None], pk(W), pk(M), pk(V), "adamw_big")]

    d_mod = jnp.stack([
        jnp.stack([acc_n0[1], acc_n0[0], acc_m0[0], gf0["d_sh"], gf0["d_sc"], gf0["d_g"]]),
        jnp.stack([acc_n1[1], acc_n1[0], acc_m1[0], gf1["d_sh"], gf1["d_sc"], gf1["d_g"]])])
    g_small = dict(
        ada_b=d_mod.reshape(DEPTH, 6 * D),
        norm_mix_w=jnp.stack([acc_n0[2], acc_n1[2]]), norm_ffn_w=jnp.stack([gf0["d_nw"], gf1["d_nw"]]),
        conv_b=acc_conv[4:5], dt_bias=acc_ssd[0:1, :SSM_HEADS], a_log=acc_ssd[1:2, :SSM_HEADS], d_skip=acc_ssd[2:3, :SSM_HEADS],
        ssm_norm_w=d_ssm_nw, gmlp_ln_w=acc_ln[0:1], gmlp_ln_b=acc_ln[1:2], gmlp_ws=d_ws[None],
        gmlp_bs=d_bs[:, :GMLP_GROUPS].T[None], sinks=dsinks[:, :ATTN_HEADS], rel_table=d_table, final_norm_w=acc_f[0],
        conv_w=acc_conv[0:SSM_CONV], qkv_b=d_qkv_b, o_b=acc_m1[1:2])
    small = _Pack(D, 8, 8)
    for name in _REPLICATED:
        small.add(name, W[name].shape)
    small.add("conv_w", (SSM_CONV, CONV_DIM))
    small.add("qkv_b", (1, QKV_DIM))
    small.add("o_b", (1, D))
    parts_small = _all_gather(small.pack(g_small, F32), "gather_small_grads")
    zeros_tiny = dict(conv_w=jnp.zeros((SSM_CONV, CONV_DIM), F32), qkv_b=jnp.zeros((1, QKV_DIM), F32), o_b=jnp.zeros((1, D), F32))
    pks = lambda S: small.pack({**{name: S[name] for name in _REPLICATED}, **zeros_tiny}, F32)
    res_small = [small.unpack(r) for r in _adamw(parts_small, pks(W), pks(M), pks(V), "adamw_small")]
    g_small_sum = res_small[0]

    n_cw, n_qb, n_ob = W["conv_w"].shape[2], W["qkv_b"].shape[1], W["o_b"].shape[1]
    g_tiny = dict(conv_w=lax.dynamic_slice_in_dim(g_small_sum["conv_w"], me * n_cw, n_cw, axis=1)[None],
                  qkv_b=lax.dynamic_slice_in_dim(g_small_sum["qkv_b"], me * n_qb, n_qb, axis=1),
                  o_b=lax.dynamic_slice_in_dim(g_small_sum["o_b"], me * n_ob, n_ob, axis=1))
    tiny = _Pack(D, 8, 8)
    for name in _TINY_SHARDED:
        tiny.add(name, W[name].shape)
    pkt = lambda S: tiny.pack({name: S[name] for name in _TINY_SHARDED}, F32)
    res_tiny = [tiny.unpack(r) for r in _adamw(pkt(g_tiny)[None], pkt(W), pkt(M), pkt(V), "adamw_tiny")]

    dmod_all = parts_small[:, small.entries[0][1]:small.entries[0][1] + small.entries[0][2]].reshape(N_DEV, DEPTH, 6 * D)
    dmod_cols = jnp.transpose(lax.dynamic_slice_in_dim(dmod_all, me * ncol, ncol, axis=2), (1, 0, 2))
    g_ada_w = _ada_w_grad(cond, dmod_cols, "ada_w_grad")
    flat = lambda a: a.reshape(DEPTH * D, ncol)
    res_ada = [r.reshape(DEPTH, D, ncol) for r in _adamw(flat(g_ada_w)[None], flat(W["ada_w"]), flat(M["ada_w"]), flat(V["ada_w"]), "adamw_ada_w")]

    def result(kind_idx, name):
        if name == "ada_w":
            return res_ada[kind_idx]
        if name in _REPLICATED:
            return res_small[kind_idx][name]
        if name in _TINY_SHARDED:
            return res_tiny[kind_idx][name]
        pieces = [res_big[kind_idx][key] for key, nm, layer, kind in _BIG if nm == name]
        return jnp.stack(pieces)

    outs = [loss, grad_x[None]]
    for kind_idx in range(4):
        outs += [result(kind_idx, name) for name in _WEIGHTS]
    return tuple(outs)


def kernel(x, c, ada_w, ada_b, norm_mix_w, norm_ffn_w, in_w_even, conv_w, conv_b, dt_bias, a_log, d_skip, ssm_norm_w, gmlp_ln_w, gmlp_ln_b, gmlp_ws, gmlp_bs, out_w_even, qkv_w, qkv_b, o_w, o_b, sinks, rel_table, ffn_gate_w, ffn_up_w, ffn_down_w, final_norm_w, loss_target, m_ada_w, m_ada_b, m_norm_mix_w, m_norm_ffn_w, m_in_w_even, m_conv_w, m_conv_b, m_dt_bias, m_a_log, m_d_skip, m_ssm_norm_w, m_gmlp_ln_w, m_gmlp_ln_b, m_gmlp_ws, m_gmlp_bs, m_out_w_even, m_qkv_w, m_qkv_b, m_o_w, m_o_b, m_sinks, m_rel_table, m_ffn_gate_w, m_ffn_up_w, m_ffn_down_w, m_final_norm_w, v_ada_w, v_ada_b, v_norm_mix_w, v_norm_ffn_w, v_in_w_even, v_conv_w, v_conv_b, v_dt_bias, v_a_log, v_d_skip, v_ssm_norm_w, v_gmlp_ln_w, v_gmlp_ln_b, v_gmlp_ws, v_gmlp_bs, v_out_w_even, v_qkv_w, v_qkv_b, v_o_w, v_o_b, v_sinks, v_rel_table, v_ffn_gate_w, v_ffn_up_w, v_ffn_down_w, v_final_norm_w):
    args = locals()
    W = {n: args[n] for n in _WEIGHTS}
    M = {n: args["m_" + n] for n in _WEIGHTS}
    V = {n: args["v_" + n] for n in _WEIGHTS}
    return _step(x, c, loss_target, W, M, V)
```

```python
import functools
import math

import numpy as np
import jax
import jax.numpy as jnp
from jax import lax
from jax.experimental import pallas as pl
from jax.experimental.pallas import tpu as pltpu

F32 = jnp.float32
BF16 = jnp.bfloat16
HIGHEST = lax.Precision.HIGHEST
MESH = pl.DeviceIdType.MESH

N_DEV = 8
D = 1024
DEPTH = 2
SSM_HEADS = 16
SSM_HEAD_DIM = 64
SSM_INNER = 1024
SSM_GROUPS = 2
SSM_STATE = 128
SSM_CONV = 4
CHUNK = 128
CONV_DIM = SSM_INNER + 2 * SSM_GROUPS * SSM_STATE
GMLP_GROUPS = 8
GMLP_INNER = 1024
IN_EVEN = 4624
ATTN_HEADS = 16
ATTN_KV = 2
ATTN_DH = 64
QKV_DIM = 1280
REL_BUCKETS = 32
REL_MAX_DIST = 128
FFN = 2816
EPS = 1e-6
NEG_INF = -1e30
LANES = 128

ADAM_LR = 0.001
ADAM_B1 = 0.9
ADAM_B2 = 0.999
ADAM_EPS = 1e-08
ADAM_WD = 0.01
ADAM_STEP = 10

VMEM_LIMIT_BYTES = 56 * 1024 * 1024
ROW_TILE = 512


def _pcall(body, *, name, out_shape, grid=(), in_specs=None, out_specs=None, scratch=(), sem=None):
    params = dict(vmem_limit_bytes=VMEM_LIMIT_BYTES)
    if sem is not None:
        params["dimension_semantics"] = sem
    specs = {} if in_specs is None else dict(in_specs=in_specs, out_specs=out_specs)
    return pl.pallas_call(
        body, name=name, out_shape=out_shape, grid=grid, **specs,
        scratch_shapes=list(scratch), compiler_params=pltpu.CompilerParams(**params))


def _pcall_with_gather(body, gather_x, *, name, out_shape, grid, in_specs, out_specs, scratch=()):
    n_in, n_out, n_scr = len(in_specs), len(out_shape), len(scratch)
    steps = int(np.prod(grid))
    hbm = pl.BlockSpec(memory_space=pl.ANY)

    def wrapped(*refs):
        ins, x_ref = refs[:n_in], refs[n_in]
        outs, g_ref = refs[n_in + 1:n_in + 1 + n_out], refs[n_in + 1 + n_out]
        scr, sems = refs[n_in + 2 + n_out:n_in + 2 + n_out + n_scr], refs[n_in + 2 + n_out + n_scr:]
        ops = _GatherOps(x_ref, g_ref, *sems)
        step = pl.program_id(0)
        for axis in range(1, len(grid)):
            step = step * grid[axis] + pl.program_id(axis)
        pl.when(step == 0)(ops.start)
        body(*ins, *outs, *scr)
        pl.when(step == (3 * steps) // 4)(ops.forward)
        pl.when(step == steps - 1)(ops.finish)

    return _pcall(
        wrapped, name=name, out_shape=list(out_shape) + [_sds((N_DEV,) + gather_x.shape, gather_x.dtype)], grid=grid,
        in_specs=list(in_specs) + [hbm], out_specs=list(out_specs) + [hbm],
        scratch=list(scratch) + _GatherOps.scratch(), sem=("arbitrary",) * len(grid))


def _tile(n, pref):
    if n <= pref:
        return n
    best = None
    for t in range(LANES, pref + 1, LANES):
        if n % t == 0:
            best = t
    assert best is not None, (n, pref)
    return best


def _rows(T):
    return min(ROW_TILE, T)


def _sds(shape, dtype=F32):
    return jax.ShapeDtypeStruct(shape, dtype)


def _row_spec(tm, c, col=0):
    return pl.BlockSpec((tm, c), lambda i, col=col: (i, col))


def _vec_spec(c, r=1):
    return pl.BlockSpec((r, c), lambda i: (0, 0))


def _sigmoid(x):
    return jax.nn.sigmoid(x)


def _silu(x):
    return x * _sigmoid(x)


def _dsilu(x):
    s = _sigmoid(x)
    return s * (1.0 + x * (1.0 - s))


def _gelu(x):
    return 0.5 * x * (1.0 + lax.erf(x * 0.7071067811865476))


def _dgelu(x):
    return 0.5 * (1.0 + lax.erf(x * 0.7071067811865476)) + x * jnp.exp(-0.5 * x * x) * 0.3989422804014327


def _dot(a, b, dims, precision=None):
    return lax.dot_general(a, b, (dims, ((), ())), precision=precision, preferred_element_type=F32)


def _nn(a, b, precision=None):
    return _dot(a, b, ((1,), (0,)), precision)


def _nt(a, b, precision=None):
    return _dot(a, b, ((1,), (1,)), precision)


def _tn(a, b, precision=None):
    return _dot(a, b, ((0,), (0,)), precision)


def _bf(x):
    return x.astype(BF16)


def _colsum(x):
    return jnp.sum(x, axis=0, keepdims=True)


def _rowsum(x):
    return jnp.sum(x, axis=1, keepdims=True)


def _allsum(x):
    return _colsum(_rowsum(x))


def _all_gather(x, name):
    R, C = x.shape

    def body(x_ref, out_ref, send_sems, recv_sems, local_sem):
        ops = _GatherOps(x_ref, out_ref, send_sems, recv_sems, local_sem)
        ops.start()
        ops.forward()
        ops.finish()

    return pl.pallas_call(
        body, name=name, out_shape=_sds((N_DEV, R, C), x.dtype),
        in_specs=[pl.BlockSpec(memory_space=pl.ANY)], out_specs=pl.BlockSpec(memory_space=pl.ANY),
        scratch_shapes=_GatherOps.scratch(),
    )(x)


class _GatherOps:
    def __init__(self, x_ref, out_ref, send_sems, recv_sems, local_sem):
        self.x_ref, self.out_ref = x_ref, out_ref
        self.send_sems, self.recv_sems, self.local_sem = send_sems, recv_sems, local_sem
        mx, my, mc = lax.axis_index("x"), lax.axis_index("y"), lax.axis_index("c")
        self.mc = mc
        self.me, self.sibling = (mx, my, mc), (mx, my, 1 - mc)
        self.chips = [(1 - mx, my), (mx, 1 - my), (1 - mx, 1 - my)]

    @staticmethod
    def scratch():
        return [pltpu.SemaphoreType.DMA((7,)), pltpu.SemaphoreType.DMA((7,)), pltpu.SemaphoreType.DMA(())]

    def _slot(self, px, py, pc):
        return self.out_ref.at[4 * px + 2 * py + pc]

    def _copy(self, k, block, to, own=False):
        return pltpu.make_async_remote_copy(
            src_ref=self.x_ref if own else self._slot(*block), dst_ref=self._slot(*block),
            send_sem=self.send_sems.at[k], recv_sem=self.recv_sems.at[k], device_id=to, device_id_type=MESH)

    def _mine(self):
        return pltpu.make_async_copy(self.x_ref, self._slot(*self.me), self.local_sem)

    def _first(self):
        return [self._copy(0, self.me, self.sibling, own=True)] + [
            self._copy(1 + j, self.me, (*chip, self.mc), own=True) for j, chip in enumerate(self.chips)]

    def _passed(self):
        return [self._copy(4 + j, (*chip, self.mc), self.sibling) for j, chip in enumerate(self.chips)]

    def start(self):
        self._mine().start()
        for cp in self._first():
            cp.start()

    def forward(self):
        passed = self._passed()
        for j, chip in enumerate(self.chips):
            self._copy(1 + j, (*chip, self.mc), self.me).wait_recv()
            passed[j].start()

    def finish(self):
        self._copy(0, self.sibling, self.me).wait_recv()
        for j, chip in enumerate(self.chips):
            self._copy(4 + j, (*chip, 1 - self.mc), self.me).wait_recv()
        for cp in self._first() + self._passed():
            cp.wait_send()
        self._mine().wait()


N_CHIP = 4


def _exchange_sibling(p, name):
    _, R, C = p.shape

    def body(p_ref, theirs_ref, send_sems, recv_sems):
        mx, my, mc = lax.axis_index("x"), lax.axis_index("y"), lax.axis_index("c")
        copies = []
        for chip in range(N_CHIP):
            cp = pltpu.make_async_remote_copy(
                src_ref=p_ref.at[2 * chip + 1 - mc], dst_ref=theirs_ref.at[chip],
                send_sem=send_sems.at[chip], recv_sem=recv_sems.at[chip],
                device_id=(mx, my, 1 - mc), device_id_type=MESH)
            cp.start()
            copies.append(cp)
        for cp in copies:
            cp.wait()

    return pl.pallas_call(
        body, name=name, out_shape=_sds((N_CHIP, R, C), p.dtype),
        in_specs=[pl.BlockSpec(memory_space=pl.ANY)], out_specs=pl.BlockSpec(memory_space=pl.ANY),
        scratch_shapes=[pltpu.SemaphoreType.DMA((N_CHIP,))] * 2,
    )(p)


def _exchange_chips(q, name):
    _, R, C = q.shape

    def body(q_ref, out_ref, send_sems, recv_sems, local_sem):
        mx, my, mc = lax.axis_index("x"), lax.axis_index("y"), lax.axis_index("c")
        me = 2 * mx + my
        local = pltpu.make_async_copy(q_ref.at[me], out_ref.at[me], local_sem)
        local.start()
        copies = []
        for r in range(1, N_CHIP):
            px = 1 - mx if r & 2 else mx
            py = 1 - my if r & 1 else my
            cp = pltpu.make_async_remote_copy(
                src_ref=q_ref.at[2 * px + py], dst_ref=out_ref.at[me],
                send_sem=send_sems.at[r - 1], recv_sem=recv_sems.at[r - 1],
                device_id=(px, py, mc), device_id_type=MESH)
            cp.start()
            copies.append(cp)
        for cp in copies:
            cp.wait()
        local.wait()

    return pl.pallas_call(
        body, name=name, out_shape=_sds((N_CHIP, R, C), q.dtype),
        in_specs=[pl.BlockSpec(memory_space=pl.ANY)], out_specs=pl.BlockSpec(memory_space=pl.ANY),
        scratch_shapes=[pltpu.SemaphoreType.DMA((3,)), pltpu.SemaphoreType.DMA((3,)), pltpu.SemaphoreType.DMA(())],
    )(q)


def _pair_sum(p, theirs, name):
    n, R, C = theirs.shape
    tr = _tile_rows(R, 256)

    def body(p_ref, t_ref, o_ref):
        mc = lax.axis_index("c")
        o_ref[0] = (p_ref[0, mc] + t_ref[0]).astype(BF16)

    blk = pl.BlockSpec((1, tr, C), lambda s, i: (s, i, 0))
    return _pcall(body, name=name, out_shape=_sds((n, R, C), BF16), grid=(n, R // tr),
                  in_specs=[pl.BlockSpec((1, 2, tr, C), lambda s, i: (s, 0, i, 0)), blk],
                  out_specs=blk, sem=("parallel", "parallel"))(p.reshape(n, 2, R, C), theirs)


def _sum_parts(parts, name):
    P, R, C = parts.shape
    tr = _tile_rows(R, 256)

    def body(p_ref, o_ref):
        g = p_ref[0].astype(F32)
        for k in range(1, P):
            g = g + p_ref[k].astype(F32)
        o_ref[...] = g

    return _pcall(body, name=name, out_shape=_sds((R, C)), grid=(R // tr,),
                  in_specs=[pl.BlockSpec((P, tr, C), lambda i: (0, i, 0))],
                  out_specs=pl.BlockSpec((tr, C), lambda i: (i, 0)), sem=("parallel",))(parts)


def _adamw(parts, w, m, v, name):
    P, R, C = parts.shape
    tr = R if R <= 256 else _tile_rows(R, 256)

    def body(p_ref, w_ref, m_ref, v_ref, g_ref, d_ref, nm_ref, nv_ref):
        g = p_ref[0]
        for k in range(1, P):
            g = g + p_ref[k]
        nm = ADAM_B1 * m_ref[...] + (1.0 - ADAM_B1) * g
        nv = ADAM_B2 * v_ref[...] + (1.0 - ADAM_B2) * (g * g)
        m_hat = nm / (1.0 - ADAM_B1 ** ADAM_STEP)
        v_hat = nv / (1.0 - ADAM_B2 ** ADAM_STEP)
        g_ref[...] = g
        d_ref[...] = -ADAM_LR * (m_hat / (jnp.sqrt(v_hat) + ADAM_EPS) + ADAM_WD * w_ref[...])
        nm_ref[...] = nm
        nv_ref[...] = nv

    blk = pl.BlockSpec((tr, C), lambda i: (i, 0))
    return _pcall(
        body, name=name, out_shape=[_sds((R, C))] * 4, grid=(R // tr,),
        in_specs=[pl.BlockSpec((P, tr, C), lambda i: (0, i, 0)), blk, blk, blk],
        out_specs=[blk] * 4, sem=("parallel",))(parts, w, m, v)


def _tile_rows(n, pref):
    best = None
    for t in range(8, pref + 1, 8):
        if n % t == 0:
            best = t
    assert best is not None, (n, pref)
    return best


def _mm(pairs, mode, *, name, out_dtype=F32, bias=None, resid=None, gvec=None, keep=False, tn_pref=1024):
    M = pairs[0][0].shape[0]
    N = pairs[0][1].shape[1] if mode == "nn" else pairs[0][1].shape[0]
    tm, tn = _rows(M), _tile(N, tn_pref)
    n_pairs = len(pairs)
    has_bias, has_res = bias is not None, resid is not None

    def body(*refs):
        ab = refs[:2 * n_pairs]
        pos = 2 * n_pairs
        b_ref = refs[pos] if has_bias else None
        pos += has_bias
        r_ref, g_ref = (refs[pos], refs[pos + 1]) if has_res else (None, None)
        pos += 2 * has_res
        outs = refs[pos:]
        acc = None
        for p in range(n_pairs):
            a, b = _bf(ab[2 * p][...]), _bf(ab[2 * p + 1][...])
            d = _nn(a, b) if mode == "nn" else _nt(a, b)
            acc = d if acc is None else acc + d
        if has_bias:
            acc = acc + b_ref[...]
        if has_res:
            outs[0][...] = (r_ref[...] + g_ref[...] * acc).astype(outs[0].dtype)
            if keep:
                outs[1][...] = acc
        else:
            outs[0][...] = acc.astype(outs[0].dtype)

    in_specs, args = [], []
    for a, b in pairs:
        K = a.shape[1]
        in_specs.append(pl.BlockSpec((tm, K), lambda j, i: (i, 0)))
        if mode == "nn":
            in_specs.append(pl.BlockSpec((K, tn), lambda j, i: (0, j)))
        else:
            in_specs.append(pl.BlockSpec((tn, K), lambda j, i: (j, 0)))
        args += [a, b]
    if has_bias:
        in_specs.append(pl.BlockSpec((1, tn), lambda j, i: (0, j)))
        args.append(bias)
    if has_res:
        in_specs.append(pl.BlockSpec((tm, tn), lambda j, i: (i, j)))
        in_specs.append(pl.BlockSpec((1, tn), lambda j, i: (0, j)))
        args += [resid, gvec]
    o_spec = pl.BlockSpec((tm, tn), lambda j, i: (i, j))
    n_out = 2 if (has_res and keep) else 1
    out_shape = [_sds((M, N), out_dtype)] + ([_sds((M, N), F32)] if n_out == 2 else [])
    res = _pcall(body, name=name, out_shape=out_shape, grid=(N // tn, M // tm), in_specs=in_specs,
                 out_specs=[o_spec] * n_out, sem=("parallel", "parallel"))(*args)
    return res if n_out == 2 else res[0]


def _mm_tn(a, b, *, name, tm_pref=1408, tn_pref=1536):
    K, M = a.shape
    N = b.shape[1]
    tm, tn = _tile(M, tm_pref), _tile(N, tn_pref)
    tk = K if K <= ROW_TILE else ROW_TILE

    def body(a_ref, b_ref, o_ref):
        @pl.when(pl.program_id(2) == 0)
        def _():
            o_ref[...] = jnp.zeros_like(o_ref)
        o_ref[...] += _tn(_bf(a_ref[...]), _bf(b_ref[...]))

    return _pcall(
        body, name=name, out_shape=_sds((M, N)), grid=(M // tm, N // tn, K // tk),
        in_specs=[pl.BlockSpec((tk, tm), lambda i, j, k: (k, i)), pl.BlockSpec((tk, tn), lambda i, j, k: (k, j))],
        out_specs=pl.BlockSpec((tm, tn), lambda i, j, k: (i, j)),
        sem=("parallel", "parallel", "arbitrary"))(a, b)


def _mm_swiglu(h, wg_t, wu_t, name, gather=None):
    M, K = h.shape
    N = wg_t.shape[0]
    tm, tn = _rows(M), _tile(N, 1408)

    def body(h_ref, wg_ref, wu_ref, gate_ref, up_ref, act_ref):
        hv = _bf(h_ref[...])
        gate = _nt(hv, wg_ref[...])
        up = _nt(hv, wu_ref[...])
        gate_ref[...] = gate
        up_ref[...] = up
        act_ref[...] = (_silu(gate) * up).astype(BF16)

    w_spec = pl.BlockSpec((tn, K), lambda j, i: (j, 0))
    o_spec = pl.BlockSpec((tm, tn), lambda j, i: (i, j))
    kw = dict(name=name, out_shape=[_sds((M, N)), _sds((M, N)), _sds((M, N), BF16)], grid=(N // tn, M // tm),
              in_specs=[pl.BlockSpec((tm, K), lambda j, i: (i, 0)), w_spec, w_spec], out_specs=[o_spec] * 3)
    if gather is None:
        return _pcall(body, sem=("parallel", "parallel"), **kw)(h, wg_t, wu_t)
    return _pcall_with_gather(body, gather, **kw)(h, wg_t, wu_t, gather)


def _mm_swiglu_bwd(dout, wd, gate, up, name):
    M, K = dout.shape
    N = wd.shape[0]
    tm, tn = _rows(M), _tile(N, 1408)

    def body(d_ref, wd_ref, gate_ref, up_ref, dg_ref, du_ref):
        dact = _nt(_bf(d_ref[...]), wd_ref[...])
        g = gate_ref[...]
        dg_ref[...] = (dact * up_ref[...] * _dsilu(g)).astype(BF16)
        du_ref[...] = (dact * _silu(g)).astype(BF16)

    t_spec = pl.BlockSpec((tm, tn), lambda j, i: (i, j))
    return _pcall(
        body, name=name, out_shape=[_sds((M, N), BF16)] * 2, grid=(N // tn, M // tm),
        in_specs=[pl.BlockSpec((tm, K), lambda j, i: (i, 0)), pl.BlockSpec((tn, K), lambda j, i: (j, 0)), t_spec, t_spec],
        out_specs=[t_spec] * 2, sem=("parallel", "parallel"))(dout, wd, gate, up)


def _norm_mod(x, w, sc, sh, name):
    T = x.shape[0]
    tm = _rows(T)

    def body(x_ref, w_ref, sc_ref, sh_ref, o_ref):
        xv = x_ref[...]
        r = lax.rsqrt(jnp.mean(xv * xv, axis=-1, keepdims=True) + EPS)
        o_ref[...] = ((xv * r * w_ref[...]) * (1.0 + sc_ref[...]) + sh_ref[...]).astype(BF16)

    return _pcall(body, name=name, out_shape=_sds((T, D), BF16), grid=(T // tm,),
                  in_specs=[_row_spec(tm, D), _vec_spec(D), _vec_spec(D), _vec_spec(D)],
                  out_specs=_row_spec(tm, D), sem=("parallel",))(x, w, sc, sh)


def _norm_mod_bwd(x, dh, dres, w, sc, name):
    T = x.shape[0]
    tm = _rows(T)

    def body(x_ref, dh_ref, dres_ref, w_ref, sc_ref, dx_ref, acc_ref):
        @pl.when(pl.program_id(0) == 0)
        def _():
            acc_ref[...] = jnp.zeros_like(acc_ref)
        xv, dh_v, wv = x_ref[...], dh_ref[...], w_ref[...]
        r = lax.rsqrt(jnp.mean(xv * xv, axis=-1, keepdims=True) + EPS)
        n = xv * r
        dnw = dh_v * (1.0 + sc_ref[...])
        dn = dnw * wv
        dx_ref[...] = dres_ref[...] + r * (dn - n * jnp.mean(dn * n, axis=-1, keepdims=True))
        acc_ref[0:1, :] += _colsum(dh_v * (n * wv))
        acc_ref[1:2, :] += _colsum(dh_v)
        acc_ref[2:3, :] += _colsum(dnw * n)

    return _pcall(body, name=name, out_shape=[_sds((T, D)), _sds((8, D))], grid=(T // tm,),
                  in_specs=[_row_spec(tm, D), _row_spec(tm, D), _row_spec(tm, D), _vec_spec(D), _vec_spec(D)],
                  out_specs=[_row_spec(tm, D), _vec_spec(D, 8)], sem=("arbitrary",))(x, dh, dres, w, sc)


def _gate_bwd(dx, branch, g, name):
    T = dx.shape[0]
    tm = _rows(T)

    def body(dx_ref, br_ref, g_ref, db_ref, acc_ref):
        @pl.when(pl.program_id(0) == 0)
        def _():
            acc_ref[...] = jnp.zeros_like(acc_ref)
        dxv = dx_ref[...]
        db = g_ref[...] * dxv
        db_ref[...] = db.astype(BF16)
        acc_ref[0:1, :] += _colsum(dxv * br_ref[...])
        acc_ref[1:2, :] += _colsum(db)

    return _pcall(body, name=name, out_shape=[_sds((T, D), BF16), _sds((8, D))], grid=(T // tm,),
                  in_specs=[_row_spec(tm, D), _row_spec(tm, D), _vec_spec(D)],
                  out_specs=[_row_spec(tm, D), _vec_spec(D, 8)], sem=("arbitrary",))(dx, branch, g)


def _final_loss(x, wf, target, name):
    T = x.shape[0]
    tm = _rows(T)

    def body(x_ref, w_ref, t_ref, dx_ref, acc_ref):
        @pl.when(pl.program_id(0) == 0)
        def _():
            acc_ref[...] = jnp.zeros_like(acc_ref)
        xv, wv = x_ref[...], w_ref[...]
        r = lax.rsqrt(jnp.mean(xv * xv, axis=-1, keepdims=True) + EPS)
        n = xv * r
        err = n * wv - t_ref[...]
        dy = err * (1.0 / D)
        dn = dy * wv
        dx_ref[...] = r * (dn - n * jnp.mean(dn * n, axis=-1, keepdims=True))
        acc_ref[0:1, :] += _colsum(dy * n)
        acc_ref[1:2, :] += jnp.broadcast_to(_allsum(err * err) * (0.5 / D), (1, D))

    return _pcall(body, name=name, out_shape=[_sds((T, D)), _sds((8, D))], grid=(T // tm,),
                  in_specs=[_row_spec(tm, D), _vec_spec(D), _row_spec(tm, D)],
                  out_specs=[_row_spec(tm, D), _vec_spec(D, 8)], sem=("arbitrary",))(x, wf, target)


def _colsum_call(x, name):
    T, C = x.shape
    tm = _rows(T)

    def body(x_ref, o_ref):
        @pl.when(pl.program_id(0) == 0)
        def _():
            o_ref[...] = jnp.zeros_like(o_ref)
        o_ref[...] += _colsum(x_ref[...].astype(F32))

    return _pcall(body, name=name, out_shape=_sds((1, C)), grid=(T // tm,), in_specs=[_row_spec(tm, C)],
                  out_specs=_vec_spec(C), sem=("arbitrary",))(x)


def _mod_matmul(c_all, ada_w, name):
    n = ada_w.shape[2]

    def body(c_ref, w_ref, cond_ref, o_ref):
        cond = _silu(c_ref[...])
        cond_ref[...] = cond
        o_ref[0] = _nn(cond, w_ref[0])

    return _pcall(body, name=name, out_shape=[_sds((N_DEV, D)), _sds((DEPTH, N_DEV, n))], grid=(DEPTH,),
                  in_specs=[pl.BlockSpec((N_DEV, D), lambda l: (0, 0)), pl.BlockSpec((1, D, n), lambda l: (l, 0, 0))],
                  out_specs=[pl.BlockSpec((N_DEV, D), lambda l: (0, 0)), pl.BlockSpec((1, N_DEV, n), lambda l: (l, 0, 0))],
                  sem=("arbitrary",))(c_all, ada_w)


def _add_rows(a, b, name):
    def body(a_ref, b_ref, o_ref):
        o_ref[...] = a_ref[...] + b_ref[...]

    return _pcall(body, name=name, out_shape=_sds(a.shape))(a, b)


def _ada_w_grad(cond, dmod_cols, name):
    n = dmod_cols.shape[2]

    def body(c_ref, d_ref, o_ref):
        o_ref[0] = _tn(c_ref[...], d_ref[0])

    return _pcall(body, name=name, out_shape=_sds((DEPTH, D, n)), grid=(DEPTH,),
                  in_specs=[pl.BlockSpec((N_DEV, D), lambda l: (0, 0)), pl.BlockSpec((1, N_DEV, n), lambda l: (l, 0, 0))],
                  out_specs=pl.BlockSpec((1, D, n), lambda l: (l, 0, 0)), sem=("parallel",))(cond, dmod_cols)


def _conv_fwd(pm, conv_w, conv_b, name):
    T = pm.shape[0]
    tm = _rows(T)
    C = CONV_DIM

    def body(x_ref, prev_ref, w_ref, b_ref, o_ref):
        cur = x_ref[...]
        prev = jnp.where(pl.program_id(0) > 0, prev_ref[...], 0.0)
        cur8 = cur[0:8]
        row8 = lax.broadcasted_iota(jnp.int32, (8, C), 0)
        full = w_ref[3:4, :] * cur
        head = w_ref[3:4, :] * cur8
        for k in range(1, SSM_CONV):
            wk = w_ref[3 - k:4 - k, :]
            full = full + wk * pltpu.roll(cur, k, 0)
            head = head + wk * jnp.where(row8 < k, pltpu.roll(prev, k, 0), pltpu.roll(cur8, k, 0))
        o_ref[...] = full + b_ref[...]
        o_ref[0:8, :] = head + b_ref[...]

    return _pcall(
        body, name=name, out_shape=_sds((T, C)), grid=(T // tm,),
        in_specs=[pl.BlockSpec((tm, C), lambda i: (i, 2)),
                  pl.BlockSpec((8, C), lambda i: (jnp.maximum(i * (tm // 8) - 1, 0), 2)),
                  _vec_spec(C, SSM_CONV), _vec_spec(C)],
        out_specs=_row_spec(tm, C), sem=("parallel",))(pm, pm, conv_w, conv_b)


def _conv_bwd(dc, pm, conv_w, name):
    T = dc.shape[0]
    tm = _rows(T)
    C = CONV_DIM
    nt = T // tm

    def body(dc_ref, nxt_ref, x_ref, prev_ref, w_ref, dx_ref, acc_ref):
        i = pl.program_id(0)

        @pl.when(i == 0)
        def _():
            acc_ref[...] = jnp.zeros_like(acc_ref)
        dcv = dc_ref[...]
        nxt = jnp.where(i < nt - 1, nxt_ref[...], 0.0)
        xc = x_ref[...]
        prev = jnp.where(i > 0, prev_ref[...], 0.0)
        dc8h, dc8t, x8 = dcv[0:8], dcv[tm - 8:tm], xc[0:8]
        row8 = lax.broadcasted_iota(jnp.int32, (8, C), 0)
        full = w_ref[3:4, :] * dcv
        tail = w_ref[3:4, :] * dc8t
        acc_ref[3:4, :] += _colsum(dcv * xc)
        for k in range(1, SSM_CONV):
            wk = w_ref[3 - k:4 - k, :]
            full = full + wk * pltpu.roll(dcv, tm - k, 0)
            tail = tail + wk * jnp.where(row8 + k >= 8, pltpu.roll(nxt, 8 - k, 0), pltpu.roll(dc8t, 8 - k, 0))
            xs_head = jnp.where(row8 < k, pltpu.roll(prev, k, 0), pltpu.roll(x8, k, 0))
            prod = dcv * pltpu.roll(xc, k, 0)
            acc_ref[3 - k:4 - k, :] += _colsum(prod) - _colsum(prod[0:8]) + _colsum(dc8h * xs_head)
        acc_ref[4:5, :] += _colsum(dcv)
        dx_ref[...] = full
        dx_ref[tm - 8:tm, :] = tail

    return _pcall(
        body, name=name, out_shape=[_sds((T, C)), _sds((8, C))], grid=(nt,),
        in_specs=[_row_spec(tm, C),
                  pl.BlockSpec((8, C), lambda i: (jnp.minimum((i + 1) * (tm // 8), T // 8 - 1), 0)),
                  pl.BlockSpec((tm, C), lambda i: (i, 2)),
                  pl.BlockSpec((8, C), lambda i: (jnp.maximum(i * (tm // 8) - 1, 0), 2)),
                  _vec_spec(C, SSM_CONV)],
        out_specs=[_row_spec(tm, C), _vec_spec(C, 8)], sem=("arbitrary",))(dc, dc, pm, pm, conv_w)


def _ssd_prologue(cpre, dtr, dtb, alog):
    L = CHUNK
    xc = _silu(cpre)
    pre = dtr + dtb
    dt = jnp.maximum(pre, 0.0) + jnp.log1p(jnp.exp(-jnp.abs(pre)))
    a = -jnp.exp(alog)
    la = dt * a
    row = lax.broadcasted_iota(jnp.int32, (L, L), 0)
    col = lax.broadcasted_iota(jnp.int32, (L, L), 1)
    causal = row >= col
    tri = causal.astype(F32)
    lc = _nn(tri, la, HIGHEST)
    return xc, pre, dt, a, causal, tri, lc, row, col


def _head_indicator():
    m = np.zeros((LANES, SSM_INNER), np.float32)
    for h in range(SSM_HEADS):
        m[h, h * SSM_HEAD_DIM:(h + 1) * SSM_HEAD_DIM] = 1.0
    return jnp.asarray(m, dtype=BF16)


def _split_dot(x, ind, dims):
    hi = x.astype(BF16)
    lo = (x - hi.astype(F32)).astype(BF16)
    return _dot(hi, ind, dims) + _dot(lo, ind, dims)


def _expand(x16, ind):
    return _split_dot(x16, ind, ((1,), (0,)))


def _headsum(x, ind):
    return _split_dot(x, ind, ((1,), (1,)))


def _ssd_fwd(cpre, dtr, pm, dtb, alog, dskip, normw, ind, name, gather=None):
    T = cpre.shape[0]
    nc = T // CHUNK
    L, P, H, HPG, N = CHUNK, SSM_HEAD_DIM, SSM_HEADS, SSM_HEADS // SSM_GROUPS, SSM_STATE
    half = SSM_INNER // SSM_GROUPS

    def body(cp_ref, dtr_ref, z_ref, dtb_ref, alog_ref, dskip_ref, nw_ref, ind_ref, ya_ref, y_ref, sp_ref, st_ref):
        @pl.when(pl.program_id(0) == 0)
        def _():
            st_ref[...] = jnp.zeros_like(st_ref)
        xc, _, dt, _, causal, _, lc, _, _ = _ssd_prologue(cp_ref[...], dtr_ref[...], dtb_ref[...], alog_ref[...])
        lct = lc.T
        ind = ind_ref[...]
        llast = lc[L - 1:L, :]
        xs = xc[:, :SSM_INNER]
        xd = xs * _expand(dt, ind)
        ex = _expand(jnp.exp(lc), ind)
        xd_end = _bf(xd * _expand(jnp.exp(llast - lc), ind))
        cdx = _expand(jnp.broadcast_to(jnp.exp(llast), (8, LANES)), ind)[0:1]
        xdb = _bf(xd)
        sp_ref[0] = st_ref[...]
        for g in range(SSM_GROUPS):
            sl = slice(g * half, (g + 1) * half)
            bm = _bf(xc[:, SSM_INNER + g * N:SSM_INNER + (g + 1) * N])
            cm = _bf(xc[:, SSM_INNER + (SSM_GROUPS + g) * N:SSM_INNER + (SSM_GROUPS + g + 1) * N])
            cb = _nt(cm, bm)
            st = st_ref[g]
            y_ref[:, sl] = ex[:, sl] * _nn(cm, _bf(st)) + dskip_ref[:, sl] * xs[:, sl]
            st_ref[g] = st * cdx[:, sl] + _tn(bm, xd_end[:, sl])
            for j in range(HPG):
                h = g * HPG + j
                decay = jnp.where(causal, jnp.exp(jnp.where(causal, lc[:, h:h + 1] - lct[h:h + 1, :], 0.0)), 0.0)
                y_ref[:, h * P:(h + 1) * P] += _nn(_bf(cb * decay), xdb[:, h * P:(h + 1) * P])
        y2 = y_ref[...] * _silu(z_ref[...])
        for g in range(SSM_GROUPS):
            yg = y2[:, g * half:(g + 1) * half]
            r = lax.rsqrt(jnp.mean(yg * yg, axis=-1, keepdims=True) + EPS)
            ya_ref[:, g * half:(g + 1) * half] = (yg * r * nw_ref[:, g * half:(g + 1) * half]).astype(BF16)

    kw = dict(
        name=name,
        out_shape=[_sds((T, SSM_INNER), BF16), _sds((T, SSM_INNER)), _sds((nc, SSM_GROUPS, N, half))], grid=(nc,),
        in_specs=[_row_spec(L, CONV_DIM), _row_spec(L, LANES), _row_spec(L, SSM_INNER, 0),
                  _vec_spec(LANES), _vec_spec(LANES), _vec_spec(SSM_INNER), _vec_spec(SSM_INNER), _vec_spec(SSM_INNER, LANES)],
        out_specs=[_row_spec(L, SSM_INNER), _row_spec(L, SSM_INNER),
                   pl.BlockSpec((1, SSM_GROUPS, N, half), lambda i: (i, 0, 0, 0))],
        scratch=[pltpu.VMEM((SSM_GROUPS, N, half), F32)])
    args = (cpre, dtr, pm, dtb, alog, dskip, normw, ind)
    if gather is None:
        return _pcall(body, sem=("arbitrary",), **kw)(*args)
    return _pcall_with_gather(body, gather, **kw)(*args, gather)


def _ssd_bwd(cpre, dtr, pm, ypre, sprev, dya, dtb, alog, dskip, normw, ind, name):
    T = cpre.shape[0]
    nc = T // CHUNK
    L, P, H, HPG, N = CHUNK, SSM_HEAD_DIM, SSM_HEADS, SSM_HEADS // SSM_GROUPS, SSM_STATE
    half = SSM_INNER // SSM_GROUPS

    def body(cp_ref, dtr_ref, z_ref, y_ref, sp_ref, dya_ref, dtb_ref, alog_ref, dskip_ref, nw_ref, ind_ref,
             dz_ref, dcp_ref, ddtr_ref, acc_ref, dnw_ref, ds_ref, dy_ref, dxd_ref, rr_ref, yoff_ref, dcd_ref):
        @pl.when(pl.program_id(0) == 0)
        def _():
            ds_ref[...] = jnp.zeros_like(ds_ref)
            acc_ref[...] = jnp.zeros_like(acc_ref)
            dnw_ref[...] = jnp.zeros_like(dnw_ref)
        cpre_v = cp_ref[...]
        xc, pre, dt, a, causal, tri, lc, row, col = _ssd_prologue(cpre_v, dtr_ref[...], dtb_ref[...], alog_ref[...])
        lct = lc.T
        zv, yv = z_ref[...], y_ref[...]
        sz = _silu(zv)
        y2 = yv * sz
        dya_v = dya_ref[...]
        nwv = nw_ref[...]
        for g in range(SSM_GROUPS):
            sl = slice(g * half, (g + 1) * half)
            yg = y2[:, sl]
            r = lax.rsqrt(jnp.mean(yg * yg, axis=-1, keepdims=True) + EPS)
            nrm = yg * r
            dnw_ref[:, sl] += _colsum(dya_v[:, sl] * nrm)
            dn = dya_v[:, sl] * nwv[:, sl]
            dy2 = r * (dn - nrm * jnp.mean(dn * nrm, axis=-1, keepdims=True))
            dy_ref[:, sl] = dy2 * sz[:, sl]
            dz_ref[:, sl] = dy2 * yv[:, sl] * _dsilu(zv[:, sl])
        ind = ind_ref[...]
        llast = lc[L - 1:L, :]
        dte16 = jnp.exp(llast - lc)
        cd16 = jnp.exp(llast)
        xs = xc[:, :SSM_INNER]
        dtx = _expand(dt, ind)
        ex = _expand(jnp.exp(lc), ind)
        dtex = _expand(dte16, ind)
        cdx = _expand(jnp.broadcast_to(cd16, (8, LANES)), ind)[0:1]
        xd = xs * dtx
        xdb = _bf(xd)
        xd_end = _bf(xd * dtex)
        dyv = dy_ref[...]
        dy_off = _bf(ex * dyv)
        dyb = _bf(dyv)
        dskx = dskip_ref[...]
        lane_c = lax.broadcasted_iota(jnp.int32, (L, LANES), 1)
        lane1 = lax.broadcasted_iota(jnp.int32, (1, LANES), 1)
        sub16 = lax.broadcasted_iota(jnp.int32, (H, L), 0)
        dlc_c = jnp.zeros((L, LANES), F32)
        dlc_r = jnp.zeros((H, L), F32)
        for g in range(SSM_GROUPS):
            sl = slice(g * half, (g + 1) * half)
            b_lo = SSM_INNER + g * N
            c_lo = SSM_INNER + (SSM_GROUPS + g) * N
            bm, cm = _bf(xc[:, b_lo:b_lo + N]), _bf(xc[:, c_lo:c_lo + N])
            cb = _nt(cm, bm)
            st, dst = sp_ref[0, g], ds_ref[g]
            stb, dstb = _bf(st), _bf(dst)
            dcm = _nt(dy_off[:, sl], stb)
            ds_ref[g] = _tn(cm, dy_off[:, sl]) + dst * cdx[:, sl]
            rr_ref[:, sl] = _nn(bm, dstb)
            yoff_ref[:, sl] = ex[:, sl] * _nn(cm, stb)
            db = _nt(xd_end[:, sl], dstb)
            dcd_ref[:, sl] = _colsum(dst * st)
            dcb = jnp.zeros((L, L), F32)
            for j in range(HPG):
                h = g * HPG + j
                hs = slice(h * P, (h + 1) * P)
                decay = jnp.where(causal, jnp.exp(jnp.where(causal, lc[:, h:h + 1] - lct[h:h + 1, :], 0.0)), 0.0)
                m = cb * decay
                dxd_ref[:, hs] = _tn(_bf(m), dyb[:, hs])
                dm = _nt(dyb[:, hs], xdb[:, hs])
                dcb = dcb + dm * decay
                gm = dm * m
                dlc_c = dlc_c + jnp.where(lane_c == h, _rowsum(gm), 0.0)
                dlc_r = dlc_r + jnp.where(sub16 == h, _colsum(gm), 0.0)
            dcbb = _bf(dcb)
            dcp_ref[:, c_lo:c_lo + N] = dcm + _nn(dcbb, bm)
            dcp_ref[:, b_lo:b_lo + N] = db + _tn(dcbb, cm)
        dxd_diag, rr = dxd_ref[...], rr_ref[...]
        tt = _headsum(rr * xd, ind) * dte16
        dlc_rt = jnp.concatenate([dlc_r, jnp.zeros((LANES - H, L), F32)], axis=0).T
        dlc = dlc_c - dlc_rt + _headsum(dyv * yoff_ref[...], ind) - tt
        dcd = _headsum(jnp.broadcast_to(dcd_ref[...], (8, SSM_INNER)), ind)[0:1]
        dlc = dlc + jnp.where(row == L - 1, _colsum(tt) + dcd * cd16, 0.0)
        dla = _tn(tri, dlc, HIGHEST)
        dxd = dxd_diag + dtex * rr
        ddt = _headsum(dxd * xs, ind) + dla * a
        ddtr = jnp.where(lane_c < H, ddt * _sigmoid(pre), 0.0)
        ddtr_ref[...] = ddtr
        acc_ref[0:1, :] += _colsum(ddtr)
        acc_ref[1:2, :] += jnp.where(lane1 < H, _colsum(dla * dt) * a, 0.0)
        acc_ref[2:3, :] += _headsum(jnp.broadcast_to(_colsum(dyv * xs), (8, SSM_INNER)), ind)[0:1]
        dcp_ref[:, 0:SSM_INNER] = dxd * dtx + dskx * dyv
        dcp_ref[...] = dcp_ref[...] * _dsilu(cpre_v)

    rev = lambda i: (nc - 1 - i, 0)
    rspec = lambda c: pl.BlockSpec((L, c), rev)
    return _pcall(
        body, name=name,
        out_shape=[_sds((T, SSM_INNER)), _sds((T, CONV_DIM)), _sds((T, LANES)), _sds((8, LANES)), _sds((1, SSM_INNER))],
        grid=(nc,),
        in_specs=[rspec(CONV_DIM), rspec(LANES), rspec(SSM_INNER), rspec(SSM_INNER),
                  pl.BlockSpec((1, SSM_GROUPS, N, half), lambda i: (nc - 1 - i, 0, 0, 0)), rspec(SSM_INNER),
                  _vec_spec(LANES), _vec_spec(LANES), _vec_spec(SSM_INNER), _vec_spec(SSM_INNER), _vec_spec(SSM_INNER, LANES)],
        out_specs=[rspec(SSM_INNER), rspec(CONV_DIM), rspec(LANES), _vec_spec(LANES, 8), _vec_spec(SSM_INNER)],
        scratch=[pltpu.VMEM((SSM_GROUPS, N, half), F32), pltpu.VMEM((L, SSM_INNER), F32), pltpu.VMEM((L, SSM_INNER), F32),
                 pltpu.VMEM((L, SSM_INNER), F32), pltpu.VMEM((L, SSM_INNER), F32), pltpu.VMEM((1, SSM_INNER), F32)],
        sem=("arbitrary",))(cpre, dtr, pm, ypre, sprev, dya, dtb, alog, dskip, normw, ind)


def _gmlp_common(u, v, lnw, lnb):
    ug = _gelu(u)
    vg = _gelu(v)
    mu = jnp.mean(vg, axis=-1, keepdims=True)
    cen = vg - mu
    rstd = lax.rsqrt(jnp.mean(cen * cen, axis=-1, keepdims=True) + EPS)
    vhat = cen * rstd
    return ug, rstd, vhat, vhat * lnw + lnb


def _causal_mask():
    row = lax.broadcasted_iota(jnp.int32, (CHUNK, CHUNK), 0)
    col = lax.broadcasted_iota(jnp.int32, (CHUNK, CHUNK), 1)
    return row >= col


def _gmlp_fwd(pm, lnw, lnb, ws, bs_exp, name):
    T = pm.shape[0]
    nc = T // CHUNK
    L, G = CHUNK, GMLP_GROUPS

    def body(u_ref, v_ref, lnw_ref, lnb_ref, ws_ref, bs_ref, o_ref):
        ug, _, _, vn = _gmlp_common(u_ref[...], v_ref[...], lnw_ref[...], lnb_ref[...])
        causal = _causal_mask()
        for g in range(G):
            sl = slice(g * L, (g + 1) * L)
            wm = _bf(jnp.where(causal, ws_ref[g], 0.0))
            sv = _nn(wm, _bf(vn[:, sl])) + bs_ref[:, sl]
            o_ref[:, sl] = (ug[:, sl] * sv).astype(BF16)

    return _pcall(
        body, name=name, out_shape=_sds((T, GMLP_INNER), BF16), grid=(nc,),
        in_specs=[_row_spec(L, GMLP_INNER, 1), _row_spec(L, GMLP_INNER, 2), _vec_spec(GMLP_INNER), _vec_spec(GMLP_INNER),
                  pl.BlockSpec((G, L, L), lambda i: (0, 0, 0)), _vec_spec(GMLP_INNER, L)],
        out_specs=_row_spec(L, GMLP_INNER), sem=("parallel",))(pm, pm, lnw, lnb, ws, bs_exp)


def _gmlp_bwd(pm, dyb, lnw, lnb, ws, bs_exp, name):
    T = pm.shape[0]
    nc = T // CHUNK
    L, G = CHUNK, GMLP_GROUPS

    def body(u_ref, v_ref, dy_ref, lnw_ref, lnb_ref, ws_ref, bs_ref, du_ref, dv_ref, dws_ref, dbs_ref, acc_ref, dvn_ref):
        @pl.when(pl.program_id(0) == 0)
        def _():
            dws_ref[...] = jnp.zeros_like(dws_ref)
            dbs_ref[...] = jnp.zeros_like(dbs_ref)
            acc_ref[...] = jnp.zeros_like(acc_ref)
        uv, vv, dyv, lnwv = u_ref[...], v_ref[...], dy_ref[...], lnw_ref[...]
        ug, rstd, vhat, vn = _gmlp_common(uv, vv, lnwv, lnb_ref[...])
        causal = _causal_mask()
        lane = lax.broadcasted_iota(jnp.int32, (L, LANES), 1)
        dbs = jnp.zeros((L, LANES), F32)
        for g in range(G):
            sl = slice(g * L, (g + 1) * L)
            wm = _bf(jnp.where(causal, ws_ref[g], 0.0))
            vng = _bf(vn[:, sl])
            sv = _nn(wm, vng) + bs_ref[:, sl]
            du_ref[:, sl] = dyv[:, sl] * sv * _dgelu(uv[:, sl])
            dsv = dyv[:, sl] * ug[:, sl]
            dsvb = _bf(dsv)
            dws_ref[g] += jnp.where(causal, _nt(dsvb, vng), 0.0)
            dbs = dbs + jnp.where(lane == g, _rowsum(dsv), 0.0)
            dvn_ref[:, sl] = _tn(wm, dsvb)
        dbs_ref[...] += dbs
        dvn = dvn_ref[...]
        acc_ref[0:1, :] += _colsum(dvn * vhat)
        acc_ref[1:2, :] += _colsum(dvn)
        dvh = dvn * lnwv
        dvg = rstd * (dvh - jnp.mean(dvh, axis=-1, keepdims=True) - vhat * jnp.mean(dvh * vhat, axis=-1, keepdims=True))
        dv_ref[...] = dvg * _dgelu(vv)

    return _pcall(
        body, name=name,
        out_shape=[_sds((T, GMLP_INNER)), _sds((T, GMLP_INNER)), _sds((G, L, L)), _sds((L, LANES)), _sds((8, GMLP_INNER))],
        grid=(nc,),
        in_specs=[_row_spec(L, GMLP_INNER, 1), _row_spec(L, GMLP_INNER, 2), _row_spec(L, GMLP_INNER),
                  _vec_spec(GMLP_INNER), _vec_spec(GMLP_INNER), pl.BlockSpec((G, L, L), lambda i: (0, 0, 0)),
                  _vec_spec(GMLP_INNER, L)],
        out_specs=[_row_spec(L, GMLP_INNER), _row_spec(L, GMLP_INNER), pl.BlockSpec((G, L, L), lambda i: (0, 0, 0)),
                   _vec_spec(LANES, L), _vec_spec(GMLP_INNER, 8)],
        scratch=[pltpu.VMEM((L, GMLP_INNER), F32)], sem=("arbitrary",))(pm, pm, dyb, lnw, lnb, ws, bs_exp)


def _rel_buckets():
    qi = np.arange(CHUNK)[:, None]
    sj = np.arange(2 * CHUNK)[None, :]
    dist = np.maximum(qi + CHUNK - sj, 0)
    max_exact = REL_BUCKETS // 2
    log_ratio = (np.log(np.maximum(dist, 1).astype(np.float32) / np.float32(max_exact))
                 / np.float32(math.log(REL_MAX_DIST / max_exact))).astype(np.float32)
    large = max_exact + (log_ratio * np.float32(REL_BUCKETS - max_exact)).astype(np.int32)
    return np.where(dist < max_exact, dist, np.minimum(large, REL_BUCKETS - 1))


def _bucket_onehot_t():
    bucket = _rel_buckets().reshape(-1)
    return jnp.asarray((np.arange(REL_BUCKETS)[:, None] == bucket[None, :]).astype(np.float32))


def _bias_from_table(table_t, onehot_t, name):
    def body(t_ref, o_ref, out_ref):
        out_ref[...] = _nn(t_ref[...], o_ref[...], HIGHEST)

    return _pcall(body, name=name, out_shape=_sds((ATTN_HEADS, onehot_t.shape[1])))(table_t, onehot_t)


def _table_from_dbias(dbias, onehot_t, name):
    def body(d_ref, o_ref, out_ref):
        out_ref[...] = _nt(d_ref[...], o_ref[...], HIGHEST)

    return _pcall(body, name=name, out_shape=_sds((ATTN_HEADS, REL_BUCKETS)))(dbias, onehot_t)


def _softmax_sink(logits, sink, mask):
    logits = jnp.where(mask, logits, NEG_INF)
    mx = jnp.maximum(jnp.max(logits, axis=-1, keepdims=True), sink)
    e = jnp.exp(logits - mx)
    es = jnp.exp(sink - mx)
    inv = 1.0 / (_rowsum(e) + es)
    return e * inv, es * inv


def _attn_mask(n, heads):
    qi = lax.broadcasted_iota(jnp.int32, (heads * CHUNK, 2 * CHUNK), 0) & (CHUNK - 1)
    sj = lax.broadcasted_iota(jnp.int32, (heads * CHUNK, 2 * CHUNK), 1)
    rel = qi + CHUNK - sj
    return (rel >= 0) & (rel < CHUNK) & ((sj >= CHUNK) | (n > 0))


def _stack_heads(ref, first, count, width):
    return jnp.concatenate([_bf(ref[:, (first + j) * width:(first + j + 1) * width]) for j in range(count)], axis=0)


def _attn_fwd(qkv, bias, sinks, name):
    T = qkv.shape[0]
    nb = T // CHUNK
    L, DH, HPK = CHUNK, ATTN_DH, ATTN_HEADS // ATTN_KV
    scale = DH ** -0.5
    kcol, vcol = ATTN_HEADS * DH // LANES, ATTN_HEADS * DH // LANES + 1

    def body(q_ref, k_ref, v_ref, kp_ref, vp_ref, bias_ref, sink_ref, o_ref, lg_ref, p_ref):
        n = pl.program_id(0)
        mask = _attn_mask(n, 1)
        kband = _bf(jnp.concatenate([kp_ref[...], k_ref[...]], axis=0))
        vband = _bf(jnp.concatenate([vp_ref[...], v_ref[...]], axis=0))
        for kv in range(ATTN_KV):
            lg_ref[...] = _nt(_stack_heads(q_ref, kv * HPK, HPK, DH), kband[:, kv * DH:(kv + 1) * DH])
            for j in range(HPK):
                h = kv * HPK + j
                p, _ = _softmax_sink(lg_ref[j * L:(j + 1) * L, :] * scale + bias_ref[h], sink_ref[h], mask)
                p_ref[j * L:(j + 1) * L, :] = _bf(p)
            og = _nn(p_ref[...], vband[:, kv * DH:(kv + 1) * DH])
            for j in range(HPK):
                h = kv * HPK + j
                o_ref[:, h * DH:(h + 1) * DH] = og[j * L:(j + 1) * L].astype(BF16)

    prev = lambda i: jnp.maximum(i - 1, 0)
    return _pcall(
        body, name=name, out_shape=_sds((T, ATTN_HEADS * DH), BF16), grid=(nb,),
        in_specs=[_row_spec(L, ATTN_HEADS * DH, 0), _row_spec(L, LANES, kcol), _row_spec(L, LANES, vcol),
                  pl.BlockSpec((L, LANES), lambda i: (prev(i), kcol)), pl.BlockSpec((L, LANES), lambda i: (prev(i), vcol)),
                  pl.BlockSpec((ATTN_HEADS, L, 2 * L), lambda i: (0, 0, 0)),
                  pl.BlockSpec(memory_space=pltpu.SMEM)],
        out_specs=_row_spec(L, ATTN_HEADS * DH),
        scratch=[pltpu.VMEM((HPK * L, 2 * L), F32), pltpu.VMEM((HPK * L, 2 * L), BF16)],
        sem=("parallel",))(qkv, qkv, qkv, qkv, qkv, bias, sinks)


def _attn_bwd(qkv, datt, bias, sinks, name):
    T = qkv.shape[0]
    nb = T // CHUNK
    L, DH, HPK = CHUNK, ATTN_DH, ATTN_HEADS // ATTN_KV
    scale = DH ** -0.5
    kcol, vcol = ATTN_HEADS * DH // LANES, ATTN_HEADS * DH // LANES + 1

    def body(q_ref, k_ref, v_ref, kp_ref, vp_ref, do_ref, bias_ref, sink_ref,
             dq_ref, dk_ref, dv_ref, dbias_ref, dsink_ref, pend_k, pend_v, band_k, band_v, lg_ref, dp_ref, p_ref, dl_ref):
        n = pl.program_id(0)

        @pl.when(n == 0)
        def _():
            dbias_ref[...] = jnp.zeros_like(dbias_ref)
            dsink_ref[...] = jnp.zeros_like(dsink_ref)

        @pl.when(n < nb)
        def _():
            mask = _attn_mask(n, 1)
            kband = _bf(jnp.concatenate([kp_ref[...], k_ref[...]], axis=0))
            vband = _bf(jnp.concatenate([vp_ref[...], v_ref[...]], axis=0))
            lane1 = lax.broadcasted_iota(jnp.int32, (1, LANES), 1)
            dsink = jnp.zeros((1, LANES), F32)
            for kv in range(ATTN_KV):
                kb, vb = kband[:, kv * DH:(kv + 1) * DH], vband[:, kv * DH:(kv + 1) * DH]
                qg = _stack_heads(q_ref, kv * HPK, HPK, DH)
                dog = _stack_heads(do_ref, kv * HPK, HPK, DH)
                lg_ref[...] = _nt(qg, kb)
                dp_ref[...] = _nt(dog, vb)
                for j in range(HPK):
                    h = kv * HPK + j
                    rows = slice(j * L, (j + 1) * L)
                    p, ps = _softmax_sink(lg_ref[rows, :] * scale + bias_ref[h], sink_ref[h], mask)
                    dp = dp_ref[rows, :]
                    delta = _rowsum(p * dp)
                    dl = p * (dp - delta)
                    dbias_ref[h] += dl
                    p_ref[rows, :] = _bf(p)
                    dl_ref[rows, :] = _bf(dl)
                    dsink = dsink + jnp.where(lane1 == h, -_colsum(ps * delta), 0.0)
                band_v[:, kv * DH:(kv + 1) * DH] = _tn(p_ref[...], dog)
                dqg = _nn(dl_ref[...], kb) * scale
                band_k[:, kv * DH:(kv + 1) * DH] = _tn(dl_ref[...], qg) * scale
                for j in range(HPK):
                    h = kv * HPK + j
                    dq_ref[:, h * DH:(h + 1) * DH] = dqg[j * L:(j + 1) * L]
            dsink_ref[...] += dsink

            @pl.when(n > 0)
            def _():
                dk_ref[...] = pend_k[...] + band_k[0:L, :]
                dv_ref[...] = pend_v[...] + band_v[0:L, :]
            pend_k[...] = band_k[L:2 * L, :]
            pend_v[...] = band_v[L:2 * L, :]

        @pl.when(n == nb)
        def _():
            dk_ref[...] = pend_k[...]
            dv_ref[...] = pend_v[...]

    cur = lambda i: jnp.minimum(i, nb - 1)
    prev = lambda i: jnp.maximum(jnp.minimum(i, nb - 1) - 1, 0)
    lag = lambda i: jnp.maximum(i - 1, 0)
    return _pcall(
        body, name=name,
        out_shape=[_sds((T, ATTN_HEADS * DH)), _sds((T, LANES)), _sds((T, LANES)), _sds((ATTN_HEADS, L, 2 * L)), _sds((1, LANES))],
        grid=(nb + 1,),
        in_specs=[pl.BlockSpec((L, ATTN_HEADS * DH), lambda i: (cur(i), 0)),
                  pl.BlockSpec((L, LANES), lambda i: (cur(i), kcol)), pl.BlockSpec((L, LANES), lambda i: (cur(i), vcol)),
                  pl.BlockSpec((L, LANES), lambda i: (prev(i), kcol)), pl.BlockSpec((L, LANES), lambda i: (prev(i), vcol)),
                  pl.BlockSpec((L, ATTN_HEADS * DH), lambda i: (cur(i), 0)),
                  pl.BlockSpec((ATTN_HEADS, L, 2 * L), lambda i: (0, 0, 0)),
                  pl.BlockSpec(memory_space=pltpu.SMEM)],
        out_specs=[pl.BlockSpec((L, ATTN_HEADS * DH), lambda i: (cur(i), 0)),
                   pl.BlockSpec((L, LANES), lambda i: (lag(i), 0)), pl.BlockSpec((L, LANES), lambda i: (lag(i), 0)),
                   pl.BlockSpec((ATTN_HEADS, L, 2 * L), lambda i: (0, 0, 0)), _vec_spec(LANES)],
        scratch=[pltpu.VMEM((L, LANES), F32), pltpu.VMEM((L, LANES), F32),
                 pltpu.VMEM((2 * L, LANES), F32), pltpu.VMEM((2 * L, LANES), F32),
                 pltpu.VMEM((HPK * L, 2 * L), F32), pltpu.VMEM((HPK * L, 2 * L), F32),
                 pltpu.VMEM((HPK * L, 2 * L), BF16), pltpu.VMEM((HPK * L, 2 * L), BF16)],
        sem=("arbitrary",))(qkv, qkv, qkv, qkv, qkv, datt, bias, sinks)


def _pad_rows(a, mult):
    pad = (-a.shape[-2]) % mult
    if pad == 0:
        return a
    cfg = [(0, 0)] * (a.ndim - 2) + [(0, pad), (0, 0)]
    return jnp.pad(a, cfg)


class _Pack:
    def __init__(self, width, mult, total_mult):
        self.width, self.mult, self.total_mult = width, mult, total_mult
        self.entries = []
        self.rows = 0

    def add(self, key, shape):
        n = int(np.prod(shape))
        rows = -(-n // self.width)
        self.entries.append((key, self.rows, rows, tuple(shape)))
        self.rows += -(-rows // self.mult) * self.mult

    @property
    def total(self):
        return -(-self.rows // self.total_mult) * self.total_mult

    def pack(self, pieces, dtype, lead=()):
        parts = []
        for key, _, rows, shape in self.entries:
            a = pieces[key].astype(dtype).reshape(lead + (-1,))
            n = int(np.prod(shape))
            a = jnp.pad(a, [(0, 0)] * len(lead) + [(0, rows * self.width - n)])
            a = a.reshape(lead + (rows, self.width))
            parts.append(_pad_rows(a, self.mult))
        out = jnp.concatenate(parts, axis=len(lead))
        return _pad_rows(out, self.total_mult)

    def unpack(self, packed, lead=()):
        out = {}
        for key, off, rows, shape in self.entries:
            a = lax.slice_in_dim(packed, off, off + rows, axis=len(lead))
            a = a.reshape(lead + (-1,))
            n = int(np.prod(shape))
            out[key] = lax.slice_in_dim(a, 0, n, axis=len(lead)).reshape(lead + shape)
        return out


def _ffn_fwd(x, mod, norm_w, wg_t, wu_t, wd, tag, gather=None):
    h = _norm_mod(x, norm_w, mod[4:5], mod[3:4], f"ffn_norm_{tag}")
    gate, up, act, *gathered = _mm_swiglu(h, wg_t, wu_t, f"ffn_gateup_{tag}", gather=gather)
    x_out, ffn_out = _mm([(act, wd)], "nn", name=f"ffn_down_{tag}", resid=x, gvec=mod[5:6], keep=True)
    return (x_out, dict(h=h, gate=gate, up=up, act=act, out=ffn_out), *gathered)


def _ffn_bwd(dx_out, x_in, saved, mod, norm_w, wg_t, wu_t, wd, tag):
    dffn, acc_g = _gate_bwd(dx_out, saved["out"], mod[5:6], f"ffn_gate_bwd_{tag}")
    dgate, dup = _mm_swiglu_bwd(dffn, wd, saved["gate"], saved["up"], f"ffn_act_bwd_{tag}")
    d_wd = _mm_tn(saved["act"], dffn, name=f"ffn_dwd_{tag}")
    d_wg_t = _mm_tn(dgate, saved["h"], name=f"ffn_dwg_{tag}")
    d_wu_t = _mm_tn(dup, saved["h"], name=f"ffn_dwu_{tag}")
    dh = _mm([(dgate, wg_t), (dup, wu_t)], "nn", name=f"ffn_dh_{tag}")
    dx, acc_n = _norm_mod_bwd(x_in, dh, dx_out, norm_w, mod[4:5], f"ffn_norm_bwd_{tag}")
    return dx, dict(d_wg=d_wg_t, d_wu=d_wu_t, d_wd=d_wd, d_g=acc_g[0], d_sc=acc_n[0], d_sh=acc_n[1], d_nw=acc_n[2])


_BIG = [
    ("out_w", "out_w_even", 0, "row"), ("qkv_w", "qkv_w", 0, "col"), ("o_w", "o_w", 0, "row"),
    ("gate0", "ffn_gate_w", 0, "col"), ("up0", "ffn_up_w", 0, "col"), ("down0", "ffn_down_w", 0, "row"),
    ("gate1", "ffn_gate_w", 1, "col"), ("up1", "ffn_up_w", 1, "col"), ("down1", "ffn_down_w", 1, "row"),
    ("in_w", "in_w_even", 0, "col"),
]


def _to_wire(a, kind):
    return a.T if kind == "col" else a


_GATHER_GROUPS = [["out_w", "in_w"], ["gate0", "up0", "down0"], ["qkv_w", "o_w", "gate1", "up1", "down1"]]

_REPLICATED = ["ada_b", "norm_mix_w", "norm_ffn_w", "conv_b", "dt_bias", "a_log", "d_skip", "ssm_norm_w", "gmlp_ln_w",
               "gmlp_ln_b", "gmlp_ws", "gmlp_bs", "sinks", "rel_table", "final_norm_w"]
_TINY_SHARDED = ["conv_w", "qkv_b", "o_b"]

_WEIGHTS = ['ada_w', 'ada_b', 'norm_mix_w', 'norm_ffn_w', 'in_w_even', 'conv_w', 'conv_b', 'dt_bias', 'a_log', 'd_skip',
            'ssm_norm_w', 'gmlp_ln_w', 'gmlp_ln_b', 'gmlp_ws', 'gmlp_bs', 'out_w_even', 'qkv_w', 'qkv_b', 'o_w', 'o_b',
            'sinks', 'rel_table', 'ffn_gate_w', 'ffn_up_w', 'ffn_down_w', 'final_norm_w']


def _step(x, c, loss_target, W, M, V):
    T = x.shape[1]
    x0 = x[0]
    target = loss_target[0]
    me = 4 * lax.axis_index("x") + 2 * lax.axis_index("y") + lax.axis_index("c")

    big = _Pack(D, 16, 128)
    wire = _Pack(D, 1, 128)
    for key, name, layer, kind in _BIG:
        big.add(key, W[name][layer].shape)
        wire.add(key, _to_wire(W[name][layer], kind).shape)
    w_wire_local = {key: _to_wire(W[name][layer].astype(BF16), kind) for key, name, layer, kind in _BIG}
    gather_packs = []
    for keys in _GATHER_GROUPS:
        gp = _Pack(D, 1, 16)
        for key in keys:
            gp.add(key, w_wire_local[key].shape)
        gather_packs.append((gp, gp.pack(w_wire_local, BF16)))
    full = {}

    def gathered_weights(group, gathered):
        shards = gather_packs[group][0].unpack(gathered, lead=(N_DEV,))
        full.update({key: a.reshape(-1, D) for key, a in shards.items()})

    gathered_weights(0, _all_gather(gather_packs[0][1], "gather_weights"))

    small_in = _Pack(D, 8, 8)
    small_in.add("c", (1, D))
    small_in.add("conv_w", W["conv_w"][0].shape)
    small_in.add("qkv_b", W["qkv_b"][0].shape)
    small_in.add("o_b", W["o_b"][0].shape)
    sm = small_in.unpack(_all_gather(small_in.pack(
        dict(c=c, conv_w=W["conv_w"][0], qkv_b=W["qkv_b"][0], o_b=W["o_b"][0]), F32), "gather_small"), lead=(N_DEV,))
    c_all = sm["c"].reshape(N_DEV, D)
    conv_w_full = jnp.transpose(sm["conv_w"], (1, 0, 2)).reshape(SSM_CONV, CONV_DIM)
    qkv_b_full = sm["qkv_b"].reshape(1, QKV_DIM)
    o_b_full = sm["o_b"].reshape(1, D)

    ncol = W["ada_w"].shape[2]
    cond, mod_cols = _mod_matmul(c_all, W["ada_w"], "mod_matmul")
    mod_g = _all_gather(mod_cols.reshape(DEPTH * N_DEV, ncol), "gather_mod").reshape(N_DEV, DEPTH, N_DEV, ncol)
    mod_me = lax.dynamic_index_in_dim(mod_g, me, axis=2, keepdims=False)
    mod_me = jnp.transpose(mod_me, (1, 0, 2)).reshape(DEPTH, 6, D)
    mod_me = jnp.pad(mod_me, ((0, 0), (0, 2), (0, 0))).reshape(DEPTH * 8, D)
    ada_b_rows = jnp.pad(W["ada_b"].reshape(DEPTH, 6, D), ((0, 0), (0, 2), (0, 0))).reshape(DEPTH * 8, D)
    mod_all = _add_rows(mod_me, ada_b_rows, "mod_bias").reshape(DEPTH, 8, D)
    mod0, mod1 = mod_all[0], mod_all[1]

    in_t = full["in_w"]
    o1, o2, o3, o4 = SSM_INNER, SSM_INNER + CONV_DIM, SSM_INNER + CONV_DIM + SSM_HEADS, SSM_INNER + CONV_DIM + SSM_HEADS + GMLP_INNER
    w_z, w_xbc, w_dt, w_u, w_v = in_t[:o1], in_t[o1:o2], in_t[o2:o3], in_t[o3:o4], in_t[o4:]
    w_main = jnp.concatenate([w_z, w_u, w_v, w_xbc], axis=0)
    w_dtp = jnp.pad(w_dt, ((0, LANES - SSM_HEADS), (0, 0)))
    out_w = full["out_w"]

    pad16 = lambda a: jnp.pad(a.reshape(1, SSM_HEADS), ((0, 0), (0, LANES - SSM_HEADS)))
    dtb, alog = pad16(W["dt_bias"][0]), pad16(W["a_log"][0])
    dskip = jnp.repeat(W["d_skip"][0], SSM_HEAD_DIM).reshape(1, SSM_INNER)
    ssm_nw = W["ssm_norm_w"]
    lnw, lnb = W["gmlp_ln_w"], W["gmlp_ln_b"]
    ws = W["gmlp_ws"][0]
    bs_exp = jnp.repeat(W["gmlp_bs"][0].T, CHUNK, axis=1)
    conv_b = W["conv_b"]
    nmw, nfw = W["norm_mix_w"], W["norm_ffn_w"]
    onehot_t = _bucket_onehot_t()
    head_ind = _head_indicator()
    bias = _bias_from_table(W["rel_table"].T, onehot_t, "rel_bias").reshape(ATTN_HEADS, CHUNK, 2 * CHUNK)
    sinks = W["sinks"][0]

    h0 = _norm_mod(x0, nmw[0:1], mod0[1:2], mod0[0:1], "mix_norm_0")
    pm = _mm([(h0, w_main)], "nt", name="in_proj", tn_pref=1536)
    dtr = _mm([(h0, w_dtp)], "nt", name="in_proj_dt")
    cpre = _conv_fwd(pm, conv_w_full, conv_b, "conv_fwd")
    ya, ypre, sprev, gathered_b = _ssd_fwd(cpre, dtr, pm, dtb, alog, dskip, ssm_nw, head_ind, "ssd_fwd", gather=gather_packs[1][1])
    gathered_weights(1, gathered_b)
    yb = _gmlp_fwd(pm, lnw, lnb, ws, bs_exp, "gmlp_fwd")
    x1, mix0 = _mm([(ya, out_w[:SSM_INNER]), (yb, out_w[SSM_INNER:])], "nn", name="out_proj", resid=x0, gvec=mod0[2:3], keep=True)
    x2, ffn0, gathered_c = _ffn_fwd(x1, mod0, nfw[0:1], full["gate0"], full["up0"], full["down0"], "0", gather=gather_packs[2][1])
    gathered_weights(2, gathered_c)
    qkv_t, o_w = full["qkv_w"], full["o_w"]
    w_q, w_k, w_v_att = qkv_t[:D], qkv_t[D:D + LANES], qkv_t[D + LANES:]

    h1 = _norm_mod(x2, nmw[1:2], mod1[1:2], mod1[0:1], "mix_norm_1")
    qkv = _mm([(h1, qkv_t)], "nt", name="qkv_proj", bias=qkv_b_full, tn_pref=1280)
    att = _attn_fwd(qkv, bias, sinks, "attn_fwd")
    x3, mix1 = _mm([(att, o_w)], "nn", name="o_proj", bias=o_b_full, resid=x2, gvec=mod1[2:3], keep=True)
    x4, ffn1 = _ffn_fwd(x3, mod1, nfw[1:2], full["gate1"], full["up1"], full["down1"], "1")

    dx4, acc_f = _final_loss(x4, W["final_norm_w"].reshape(1, D), target, "final_loss")
    loss = lax.psum(acc_f[1, 0], ("x", "y", "c"))
    dx3, gf1 = _ffn_bwd(dx4, x3, ffn1, mod1, nfw[1:2], full["gate1"], full["up1"], full["down1"], "1")

    dmix1, acc_m1 = _gate_bwd(dx3, mix1, mod1[2:3], "mix_gate_bwd_1")
    datt = _mm([(dmix1, o_w)], "nt", name="o_proj_dx")
    d_o_w = _mm_tn(att, dmix1, name="o_proj_dw")
    dq, dk, dv, dbias, dsinks = _attn_bwd(qkv, datt, bias, sinks, "attn_bwd")
    d_table = _table_from_dbias(dbias.reshape(ATTN_HEADS, -1), onehot_t, "rel_table_grad").T
    d_qkv_t = jnp.concatenate([_mm_tn(dq, h1, name="qkv_dw_q"), _mm_tn(dk, h1, name="qkv_dw_k"), _mm_tn(dv, h1, name="qkv_dw_v")], axis=0)
    d_qkv_b = jnp.concatenate([_colsum_call(dq, "qkv_db_q"), _colsum_call(dk, "qkv_db_k"), _colsum_call(dv, "qkv_db_v")], axis=1)
    dh1 = _mm([(dq, w_q), (dk, w_k), (dv, w_v_att)], "nn", name="qkv_proj_dx")
    dx2, acc_n1 = _norm_mod_bwd(x2, dh1, dx3, nmw[1:2], mod1[1:2], "mix_norm_bwd_1")

    dx1, gf0 = _ffn_bwd(dx2, x1, ffn0, mod0, nfw[0:1], full["gate0"], full["up0"], full["down0"], "0")

    dmix0, acc_m0 = _gate_bwd(dx1, mix0, mod0[2:3], "mix_gate_bwd_0")
    dya = _mm([(dmix0, out_w[:SSM_INNER])], "nt", name="out_proj_dx_a")
    dyb = _mm([(dmix0, out_w[SSM_INNER:])], "nt", name="out_proj_dx_b")
    d_out_w = jnp.concatenate([_mm_tn(ya, dmix0, name="out_proj_dw_a"), _mm_tn(yb, dmix0, name="out_proj_dw_b")], axis=0)
    du, dvg, d_ws, d_bs, acc_ln = _gmlp_bwd(pm, dyb, lnw, lnb, ws, bs_exp, "gmlp_bwd")
    dz, dcpre, ddtr, acc_ssd, d_ssm_nw = _ssd_bwd(cpre, dtr, pm, ypre, sprev, dya, dtb, alog, dskip, ssm_nw, head_ind, "ssd_bwd")
    dxbc, acc_conv = _conv_bwd(dcpre, pm, conv_w_full, "conv_bwd")
    d_in_t = jnp.concatenate([
        _mm_tn(dz, h0, name="in_dw_z"), _mm_tn(dxbc, h0, name="in_dw_xbc"),
        _mm_tn(ddtr, h0, name="in_dw_dt")[:SSM_HEADS], _mm_tn(du, h0, name="in_dw_u"), _mm_tn(dvg, h0, name="in_dw_v")], axis=0)
    dh0 = _mm([(dz, w_z), (dxbc, w_xbc), (ddtr, w_dtp), (du, w_u), (dvg, w_v)], "nn", name="in_proj_dx")
    grad_x, acc_n0 = _norm_mod_bwd(x0, dh0, dx1, nmw[0:1], mod0[1:2], "mix_norm_bwd_0")

    g_wire = dict(in_w=d_in_t, out_w=d_out_w, qkv_w=d_qkv_t, o_w=d_o_w,
                  gate0=gf0["d_wg"], up0=gf0["d_wu"], down0=gf0["d_wd"], gate1=gf1["d_wg"], up1=gf1["d_wu"], down1=gf1["d_wd"])
    g_dest = {key: g_wire[key].reshape(N_DEV, -1, D) for key, name, layer, kind in _BIG}
    g_packed = wire.pack(g_dest, F32, lead=(N_DEV,))
    theirs = _exchange_sibling(g_packed, "exchange_grads_sibling")
    from_chips = _exchange_chips(_pair_sum(g_packed, theirs, "grads_pair_sum"), "exchange_grads_chips")
    g_mine = wire.unpack(_sum_parts(from_chips, "grads_chip_sum"))
    g_nat = {key: _to_wire(g_mine[key], kind) for key, name, layer, kind in _BIG}
    pk = lambda S: big.pack({key: S[name][layer] for key, name, layer, kind in _BIG}, F32)
    res_big = [big.unpack(r) for r in _adamw(big.pack(g_nat, F32)[None], pk(W), pk(M), pk(V), "adamw_big")]

    d_mod = jnp.stack([
        jnp.stack([acc_n0[1], acc_n0[0], acc_m0[0], gf0["d_sh"], gf0["d_sc"], gf0["d_g"]]),
        jnp.stack([acc_n1[1], acc_n1[0], acc_m1[0], gf1["d_sh"], gf1["d_sc"], gf1["d_g"]])])
    g_small = dict(
        ada_b=d_mod.reshape(DEPTH, 6 * D),
        norm_mix_w=jnp.stack([acc_n0[2], acc_n1[2]]), norm_ffn_w=jnp.stack([gf0["d_nw"], gf1["d_nw"]]),
        conv_b=acc_conv[4:5], dt_bias=acc_ssd[0:1, :SSM_HEADS], a_log=acc_ssd[1:2, :SSM_HEADS], d_skip=acc_ssd[2:3, :SSM_HEADS],
        ssm_norm_w=d_ssm_nw, gmlp_ln_w=acc_ln[0:1], gmlp_ln_b=acc_ln[1:2], gmlp_ws=d_ws[None],
        gmlp_bs=d_bs[:, :GMLP_GROUPS].T[None], sinks=dsinks[:, :ATTN_HEADS], rel_table=d_table, final_norm_w=acc_f[0],
        conv_w=acc_conv[0:SSM_CONV], qkv_b=d_qkv_b, o_b=acc_m1[1:2])
    small = _Pack(D, 8, 8)
    for name in _REPLICATED:
        small.add(name, W[name].shape)
    small.add("conv_w", (SSM_CONV, CONV_DIM))
    small.add("qkv_b", (1, QKV_DIM))
    small.add("o_b", (1, D))
    parts_small = _all_gather(small.pack(g_small, F32), "gather_small_grads")
    zeros_tiny = dict(conv_w=jnp.zeros((SSM_CONV, CONV_DIM), F32), qkv_b=jnp.zeros((1, QKV_DIM), F32), o_b=jnp.zeros((1, D), F32))
    pks = lambda S: small.pack({**{name: S[name] for name in _REPLICATED}, **zeros_tiny}, F32)
    res_small = [small.unpack(r) for r in _adamw(parts_small, pks(W), pks(M), pks(V), "adamw_small")]
    g_small_sum = res_small[0]

    n_cw, n_qb, n_ob = W["conv_w"].shape[2], W["qkv_b"].shape[1], W["o_b"].shape[1]
    g_tiny = dict(conv_w=lax.dynamic_slice_in_dim(g_small_sum["conv_w"], me * n_cw, n_cw, axis=1)[None],
                  qkv_b=lax.dynamic_slice_in_dim(g_small_sum["qkv_b"], me * n_qb, n_qb, axis=1),
                  o_b=lax.dynamic_slice_in_dim(g_small_sum["o_b"], me * n_ob, n_ob, axis=1))
    tiny = _Pack(D, 8, 8)
    for name in _TINY_SHARDED:
        tiny.add(name, W[name].shape)
    pkt = lambda S: tiny.pack({name: S[name] for name in _TINY_SHARDED}, F32)
    res_tiny = [tiny.unpack(r) for r in _adamw(pkt(g_tiny)[None], pkt(W), pkt(M), pkt(V), "adamw_tiny")]

    dmod_all = parts_small[:, small.entries[0][1]:small.entries[0][1] + small.entries[0][2]].reshape(N_DEV, DEPTH, 6 * D)
    dmod_cols = jnp.transpose(lax.dynamic_slice_in_dim(dmod_all, me * ncol, ncol, axis=2), (1, 0, 2))
    g_ada_w = _ada_w_grad(cond, dmod_cols, "ada_w_grad")
    flat = lambda a: a.reshape(DEPTH * D, ncol)
    res_ada = [r.reshape(DEPTH, D, ncol) for r in _adamw(flat(g_ada_w)[None], flat(W["ada_w"]), flat(M["ada_w"]), flat(V["ada_w"]), "adamw_ada_w")]

    def result(kind_idx, name):
        if name == "ada_w":
            return res_ada[kind_idx]
        if name in _REPLICATED:
            return res_small[kind_idx][name]
        if name in _TINY_SHARDED:
            return res_tiny[kind_idx][name]
        pieces = [res_big[kind_idx][key] for key, nm, layer, kind in _BIG if nm == name]
        return jnp.stack(pieces)

    outs = [loss, grad_x[None]]
    for kind_idx in range(4):
        outs += [result(kind_idx, name) for name in _WEIGHTS]
    return tuple(outs)


def kernel(x, c, ada_w, ada_b, norm_mix_w, norm_ffn_w, in_w_even, conv_w, conv_b, dt_bias, a_log, d_skip, ssm_norm_w, gmlp_ln_w, gmlp_ln_b, gmlp_ws, gmlp_bs, out_w_even, qkv_w, qkv_b, o_w, o_b, sinks, rel_table, ffn_gate_w, ffn_up_w, ffn_down_w, final_norm_w, loss_target, m_ada_w, m_ada_b, m_norm_mix_w, m_norm_ffn_w, m_in_w_even, m_conv_w, m_conv_b, m_dt_bias, m_a_log, m_d_skip, m_ssm_norm_w, m_gmlp_ln_w, m_gmlp_ln_b, m_gmlp_ws, m_gmlp_bs, m_out_w_even, m_qkv_w, m_qkv_b, m_o_w, m_o_b, m_sinks, m_rel_table, m_ffn_gate_w, m_ffn_up_w, m_ffn_down_w, m_final_norm_w, v_ada_w, v_ada_b, v_norm_mix_w, v_norm_ffn_w, v_in_w_even, v_conv_w, v_conv_b, v_dt_bias, v_a_log, v_d_skip, v_ssm_norm_w, v_gmlp_ln_w, v_gmlp_ln_b, v_gmlp_ws, v_gmlp_bs, v_out_w_even, v_qkv_w, v_qkv_b, v_o_w, v_o_b, v_sinks, v_rel_table, v_ffn_gate_w, v_ffn_up_w, v_ffn_down_w, v_final_norm_w):
    args = locals()
    W = {n: args[n] for n in _WEIGHTS}
    M = {n: args["m_" + n] for n in _WEIGHTS}
    V = {n: args["v_" + n] for n in _WEIGHTS}
    return _step(x, c, loss_target, W, M, V)
```

```python
import functools
import math

import numpy as np
import jax
import jax.numpy as jnp
from jax import lax
from jax.experimental import pallas as pl
from jax.experimental.pallas import tpu as pltpu

F32 = jnp.float32
BF16 = jnp.bfloat16
HIGHEST = lax.Precision.HIGHEST
MESH = pl.DeviceIdType.MESH

N_DEV = 8
D = 1024
DEPTH = 2
SSM_HEADS = 16
SSM_HEAD_DIM = 64
SSM_INNER = 1024
SSM_GROUPS = 2
SSM_STATE = 128
SSM_CONV = 4
CHUNK = 128
CONV_DIM = SSM_INNER + 2 * SSM_GROUPS * SSM_STATE
GMLP_GROUPS = 8
GMLP_INNER = 1024
IN_EVEN = 4624
ATTN_HEADS = 16
ATTN_KV = 2
ATTN_DH = 64
QKV_DIM = 1280
REL_BUCKETS = 32
REL_MAX_DIST = 128
FFN = 2816
EPS = 1e-6
NEG_INF = -1e30
LANES = 128

ADAM_LR = 0.001
ADAM_B1 = 0.9
ADAM_B2 = 0.999
ADAM_EPS = 1e-08
ADAM_WD = 0.01
ADAM_STEP = 10

VMEM_LIMIT_BYTES = 56 * 1024 * 1024
ROW_TILE = 512


def _pcall(body, *, name, out_shape, grid=(), in_specs=None, out_specs=None, scratch=(), sem=None):
    params = dict(vmem_limit_bytes=VMEM_LIMIT_BYTES)
    if sem is not None:
        params["dimension_semantics"] = sem
    specs = {} if in_specs is None else dict(in_specs=in_specs, out_specs=out_specs)
    return pl.pallas_call(
        body, name=name, out_shape=out_shape, grid=grid, **specs,
        scratch_shapes=list(scratch), compiler_params=pltpu.CompilerParams(**params))


def _call(body, args, side=None, *, name, out_shape, grid, in_specs, out_specs, scratch=(), sem=None):
    if side is None:
        return _pcall(body, name=name, out_shape=out_shape, grid=grid, in_specs=in_specs, out_specs=out_specs,
                      scratch=scratch, sem=sem)(*args)
    ops_cls, x = side
    n_in, n_out, n_scr = len(in_specs), len(out_shape), len(scratch)
    steps = int(np.prod(grid))
    hbm = pl.BlockSpec(memory_space=pl.ANY)

    def wrapped(*refs):
        ins, x_ref = refs[:n_in], refs[n_in]
        outs, r_ref = refs[n_in + 1:n_in + 1 + n_out], refs[n_in + 1 + n_out]
        scr, sems = refs[n_in + 2 + n_out:n_in + 2 + n_out + n_scr], refs[n_in + 2 + n_out + n_scr:]
        ops = ops_cls(x_ref, r_ref, *sems)
        step = pl.program_id(0)
        for axis in range(1, len(grid)):
            step = step * grid[axis] + pl.program_id(axis)
        pl.when(step == 0)(ops.start)
        body(*ins, *outs, *scr)
        pl.when(step == (3 * steps) // 4)(ops.forward)
        pl.when(step == steps - 1)(ops.finish)

    return _pcall(
        wrapped, name=name, out_shape=list(out_shape) + [ops_cls.result(x)], grid=grid,
        in_specs=list(in_specs) + [hbm], out_specs=list(out_specs) + [hbm],
        scratch=list(scratch) + ops_cls.scratch(), sem=("arbitrary",) * len(grid))(*args, x)


def _tile(n, pref):
    if n <= pref:
        return n
    best = None
    for t in range(LANES, pref + 1, LANES):
        if n % t == 0:
            best = t
    assert best is not None, (n, pref)
    return best


def _rows(T):
    return min(ROW_TILE, T)


def _sds(shape, dtype=F32):
    return jax.ShapeDtypeStruct(shape, dtype)


def _row_spec(tm, c, col=0):
    return pl.BlockSpec((tm, c), lambda i, col=col: (i, col))


def _vec_spec(c, r=1):
    return pl.BlockSpec((r, c), lambda i: (0, 0))


def _sigmoid(x):
    return jax.nn.sigmoid(x)


def _silu(x):
    return x * _sigmoid(x)


def _dsilu(x):
    s = _sigmoid(x)
    return s * (1.0 + x * (1.0 - s))


def _gelu(x):
    return 0.5 * x * (1.0 + lax.erf(x * 0.7071067811865476))


def _dgelu(x):
    return 0.5 * (1.0 + lax.erf(x * 0.7071067811865476)) + x * jnp.exp(-0.5 * x * x) * 0.3989422804014327


def _dot(a, b, dims, precision=None):
    return lax.dot_general(a, b, (dims, ((), ())), precision=precision, preferred_element_type=F32)


def _nn(a, b, precision=None):
    return _dot(a, b, ((1,), (0,)), precision)


def _nt(a, b, precision=None):
    return _dot(a, b, ((1,), (1,)), precision)


def _tn(a, b, precision=None):
    return _dot(a, b, ((0,), (0,)), precision)


def _bf(x):
    return x.astype(BF16)


def _colsum(x):
    return jnp.sum(x, axis=0, keepdims=True)


def _rowsum(x):
    return jnp.sum(x, axis=1, keepdims=True)


def _allsum(x):
    return _colsum(_rowsum(x))


def _comm_call(ops_cls, x, name):
    def body(x_ref, out_ref, *sems):
        ops = ops_cls(x_ref, out_ref, *sems)
        ops.start()
        ops.forward()
        ops.finish()

    return pl.pallas_call(
        body, name=name, out_shape=ops_cls.result(x),
        in_specs=[pl.BlockSpec(memory_space=pl.ANY)], out_specs=pl.BlockSpec(memory_space=pl.ANY),
        scratch_shapes=ops_cls.scratch(),
    )(x)


def _all_gather(x, name):
    return _comm_call(_GatherOps, x, name)


class _GatherOps:
    def __init__(self, x_ref, out_ref, send_sems, recv_sems, local_sem):
        self.x_ref, self.out_ref = x_ref, out_ref
        self.send_sems, self.recv_sems, self.local_sem = send_sems, recv_sems, local_sem
        mx, my, mc = lax.axis_index("x"), lax.axis_index("y"), lax.axis_index("c")
        self.mc = mc
        self.me, self.sibling = (mx, my, mc), (mx, my, 1 - mc)
        self.chips = [(1 - mx, my), (mx, 1 - my), (1 - mx, 1 - my)]

    @staticmethod
    def result(x):
        return _sds((N_DEV,) + x.shape, x.dtype)

    @staticmethod
    def scratch():
        return [pltpu.SemaphoreType.DMA((7,)), pltpu.SemaphoreType.DMA((7,)), pltpu.SemaphoreType.DMA(())]

    def _slot(self, px, py, pc):
        return self.out_ref.at[4 * px + 2 * py + pc]

    def _copy(self, k, block, to, own=False):
        return pltpu.make_async_remote_copy(
            src_ref=self.x_ref if own else self._slot(*block), dst_ref=self._slot(*block),
            send_sem=self.send_sems.at[k], recv_sem=self.recv_sems.at[k], device_id=to, device_id_type=MESH)

    def _mine(self):
        return pltpu.make_async_copy(self.x_ref, self._slot(*self.me), self.local_sem)

    def _first(self):
        return [self._copy(0, self.me, self.sibling, own=True)] + [
            self._copy(1 + j, self.me, (*chip, self.mc), own=True) for j, chip in enumerate(self.chips)]

    def _passed(self):
        return [self._copy(4 + j, (*chip, self.mc), self.sibling) for j, chip in enumerate(self.chips)]

    def start(self):
        self._mine().start()
        for cp in self._first():
            cp.start()

    def forward(self):
        passed = self._passed()
        for j, chip in enumerate(self.chips):
            self._copy(1 + j, (*chip, self.mc), self.me).wait_recv()
            passed[j].start()

    def finish(self):
        self._copy(0, self.sibling, self.me).wait_recv()
        for j, chip in enumerate(self.chips):
            self._copy(4 + j, (*chip, 1 - self.mc), self.me).wait_recv()
        for cp in self._first() + self._passed():
            cp.wait_send()
        self._mine().wait()


N_CHIP = 4


class _SiblingOps:
    def __init__(self, p_ref, theirs_ref, send_sems, recv_sems):
        self.p_ref, self.theirs_ref, self.send_sems, self.recv_sems = p_ref, theirs_ref, send_sems, recv_sems

    @staticmethod
    def result(p):
        return _sds((N_CHIP,) + p.shape[1:], p.dtype)

    @staticmethod
    def scratch():
        return [pltpu.SemaphoreType.DMA((N_CHIP,))] * 2

    def _copies(self):
        mx, my, mc = lax.axis_index("x"), lax.axis_index("y"), lax.axis_index("c")
        return [pltpu.make_async_remote_copy(
            src_ref=self.p_ref.at[2 * chip + 1 - mc], dst_ref=self.theirs_ref.at[chip],
            send_sem=self.send_sems.at[chip], recv_sem=self.recv_sems.at[chip],
            device_id=(mx, my, 1 - mc), device_id_type=MESH) for chip in range(N_CHIP)]

    def start(self):
        for cp in self._copies():
            cp.start()

    def forward(self):
        pass

    def finish(self):
        for cp in self._copies():
            cp.wait()


class _ChipsOps:
    def __init__(self, q_ref, out_ref, send_sems, recv_sems, local_sem):
        self.q_ref, self.out_ref = q_ref, out_ref
        self.send_sems, self.recv_sems, self.local_sem = send_sems, recv_sems, local_sem

    @staticmethod
    def result(q):
        return _sds(q.shape, q.dtype)

    @staticmethod
    def scratch():
        return [pltpu.SemaphoreType.DMA((N_CHIP - 1,)), pltpu.SemaphoreType.DMA((N_CHIP - 1,)), pltpu.SemaphoreType.DMA(())]

    def _copies(self):
        mx, my, mc = lax.axis_index("x"), lax.axis_index("y"), lax.axis_index("c")
        me = 2 * mx + my
        local = pltpu.make_async_copy(self.q_ref.at[me], self.out_ref.at[me], self.local_sem)
        remote = []
        for r in range(1, N_CHIP):
            px = 1 - mx if r & 2 else mx
            py = 1 - my if r & 1 else my
            remote.append(pltpu.make_async_remote_copy(
                src_ref=self.q_ref.at[2 * px + py], dst_ref=self.out_ref.at[me],
                send_sem=self.send_sems.at[r - 1], recv_sem=self.recv_sems.at[r - 1],
                device_id=(px, py, mc), device_id_type=MESH))
        return local, remote

    def start(self):
        local, remote = self._copies()
        local.start()
        for cp in remote:
            cp.start()

    def forward(self):
        pass

    def finish(self):
        local, remote = self._copies()
        for cp in remote:
            cp.wait()
        local.wait()


def _pair_sum(p, theirs, name):
    n, R, C = theirs.shape
    tr = _tile_rows(R, 256)

    def body(p_ref, t_ref, o_ref):
        mc = lax.axis_index("c")
        o_ref[0] = (p_ref[0, mc] + t_ref[0]).astype(BF16)

    blk = pl.BlockSpec((1, tr, C), lambda s, i: (s, i, 0))
    return _pcall(body, name=name, out_shape=_sds((n, R, C), BF16), grid=(n, R // tr),
                  in_specs=[pl.BlockSpec((1, 2, tr, C), lambda s, i: (s, 0, i, 0)), blk],
                  out_specs=blk, sem=("parallel", "parallel"))(p.reshape(n, 2, R, C), theirs)


def _sum_parts(parts, name):
    P, R, C = parts.shape
    tr = _tile_rows(R, 256)

    def body(p_ref, o_ref):
        g = p_ref[0].astype(F32)
        for k in range(1, P):
            g = g + p_ref[k].astype(F32)
        o_ref[...] = g

    return _pcall(body, name=name, out_shape=_sds((R, C)), grid=(R // tr,),
                  in_specs=[pl.BlockSpec((P, tr, C), lambda i: (0, i, 0))],
                  out_specs=pl.BlockSpec((tr, C), lambda i: (i, 0)), sem=("parallel",))(parts)


def _adamw(parts, w, m, v, name):
    P, R, C = parts.shape
    tr = R if R <= 256 else _tile_rows(R, 256)

    def body(p_ref, w_ref, m_ref, v_ref, g_ref, d_ref, nm_ref, nv_ref):
        g = p_ref[0]
        for k in range(1, P):
            g = g + p_ref[k]
        nm = ADAM_B1 * m_ref[...] + (1.0 - ADAM_B1) * g
        nv = ADAM_B2 * v_ref[...] + (1.0 - ADAM_B2) * (g * g)
        m_hat = nm / (1.0 - ADAM_B1 ** ADAM_STEP)
        v_hat = nv / (1.0 - ADAM_B2 ** ADAM_STEP)
        g_ref[...] = g
        d_ref[...] = -ADAM_LR * (m_hat / (jnp.sqrt(v_hat) + ADAM_EPS) + ADAM_WD * w_ref[...])
        nm_ref[...] = nm
        nv_ref[...] = nv

    blk = pl.BlockSpec((tr, C), lambda i: (i, 0))
    return _pcall(
        body, name=name, out_shape=[_sds((R, C))] * 4, grid=(R // tr,),
        in_specs=[pl.BlockSpec((P, tr, C), lambda i: (0, i, 0)), blk, blk, blk],
        out_specs=[blk] * 4, sem=("parallel",))(parts, w, m, v)


def _tile_rows(n, pref):
    best = None
    for t in range(8, pref + 1, 8):
        if n % t == 0:
            best = t
    assert best is not None, (n, pref)
    return best


def _mm(pairs, mode, *, name, out_dtype=F32, bias=None, resid=None, gvec=None, keep=False, tn_pref=1024, side=None):
    M = pairs[0][0].shape[0]
    N = pairs[0][1].shape[1] if mode == "nn" else pairs[0][1].shape[0]
    tm, tn = _rows(M), _tile(N, tn_pref)
    n_pairs = len(pairs)
    has_bias, has_res = bias is not None, resid is not None

    def body(*refs):
        ab = refs[:2 * n_pairs]
        pos = 2 * n_pairs
        b_ref = refs[pos] if has_bias else None
        pos += has_bias
        r_ref, g_ref = (refs[pos], refs[pos + 1]) if has_res else (None, None)
        pos += 2 * has_res
        outs = refs[pos:]
        acc = None
        for p in range(n_pairs):
            a, b = _bf(ab[2 * p][...]), _bf(ab[2 * p + 1][...])
            d = _nn(a, b) if mode == "nn" else _nt(a, b)
            acc = d if acc is None else acc + d
        if has_bias:
            acc = acc + b_ref[...]
        if has_res:
            outs[0][...] = (r_ref[...] + g_ref[...] * acc).astype(outs[0].dtype)
            if keep:
                outs[1][...] = acc
        else:
            outs[0][...] = acc.astype(outs[0].dtype)

    in_specs, args = [], []
    for a, b in pairs:
        K = a.shape[1]
        in_specs.append(pl.BlockSpec((tm, K), lambda j, i: (i, 0)))
        if mode == "nn":
            in_specs.append(pl.BlockSpec((K, tn), lambda j, i: (0, j)))
        else:
            in_specs.append(pl.BlockSpec((tn, K), lambda j, i: (j, 0)))
        args += [a, b]
    if has_bias:
        in_specs.append(pl.BlockSpec((1, tn), lambda j, i: (0, j)))
        args.append(bias)
    if has_res:
        in_specs.append(pl.BlockSpec((tm, tn), lambda j, i: (i, j)))
        in_specs.append(pl.BlockSpec((1, tn), lambda j, i: (0, j)))
        args += [resid, gvec]
    o_spec = pl.BlockSpec((tm, tn), lambda j, i: (i, j))
    n_out = 2 if (has_res and keep) else 1
    out_shape = [_sds((M, N), out_dtype)] + ([_sds((M, N), F32)] if n_out == 2 else [])
    res = _call(body, args, side, name=name, out_shape=out_shape, grid=(N // tn, M // tm), in_specs=in_specs,
                out_specs=[o_spec] * n_out, sem=("parallel", "parallel"))
    out = tuple(res[:n_out]) if n_out == 2 else res[0]
    return out if side is None else (out, res[n_out])


def _mm_tn(a, b, *, name, tm_pref=1408, tn_pref=1536):
    K, M = a.shape
    N = b.shape[1]
    tm, tn = _tile(M, tm_pref), _tile(N, tn_pref)
    tk = K if K <= ROW_TILE else ROW_TILE

    def body(a_ref, b_ref, o_ref):
        @pl.when(pl.program_id(2) == 0)
        def _():
            o_ref[...] = jnp.zeros_like(o_ref)
        o_ref[...] += _tn(_bf(a_ref[...]), _bf(b_ref[...]))

    return _pcall(
        body, name=name, out_shape=_sds((M, N)), grid=(M // tm, N // tn, K // tk),
        in_specs=[pl.BlockSpec((tk, tm), lambda i, j, k: (k, i)), pl.BlockSpec((tk, tn), lambda i, j, k: (k, j))],
        out_specs=pl.BlockSpec((tm, tn), lambda i, j, k: (i, j)),
        sem=("parallel", "parallel", "arbitrary"))(a, b)


def _mm_swiglu(h, wg_t, wu_t, name, side=None):
    M, K = h.shape
    N = wg_t.shape[0]
    tm, tn = _rows(M), _tile(N, 1408)

    def body(h_ref, wg_ref, wu_ref, gate_ref, up_ref, act_ref):
        hv = _bf(h_ref[...])
        gate = _nt(hv, wg_ref[...])
        up = _nt(hv, wu_ref[...])
        gate_ref[...] = gate
        up_ref[...] = up
        act_ref[...] = (_silu(gate) * up).astype(BF16)

    w_spec = pl.BlockSpec((tn, K), lambda j, i: (j, 0))
    o_spec = pl.BlockSpec((tm, tn), lambda j, i: (i, j))
    return _call(body, (h, wg_t, wu_t), side, name=name,
                 out_shape=[_sds((M, N)), _sds((M, N)), _sds((M, N), BF16)], grid=(N // tn, M // tm),
                 in_specs=[pl.BlockSpec((tm, K), lambda j, i: (i, 0)), w_spec, w_spec], out_specs=[o_spec] * 3,
                 sem=("parallel", "parallel"))


def _mm_swiglu_bwd(dout, wd, gate, up, name, side=None):
    M, K = dout.shape
    N = wd.shape[0]
    tm, tn = _rows(M), _tile(N, 1408)

    def body(d_ref, wd_ref, gate_ref, up_ref, dg_ref, du_ref):
        dact = _nt(_bf(d_ref[...]), wd_ref[...])
        g = gate_ref[...]
        dg_ref[...] = (dact * up_ref[...] * _dsilu(g)).astype(BF16)
        du_ref[...] = (dact * _silu(g)).astype(BF16)

    t_spec = pl.BlockSpec((tm, tn), lambda j, i: (i, j))
    return _call(
        body, (dout, wd, gate, up), side, name=name, out_shape=[_sds((M, N), BF16)] * 2, grid=(N // tn, M // tm),
        in_specs=[pl.BlockSpec((tm, K), lambda j, i: (i, 0)), pl.BlockSpec((tn, K), lambda j, i: (j, 0)), t_spec, t_spec],
        out_specs=[t_spec] * 2, sem=("parallel", "parallel"))


def _norm_mod(x, w, sc, sh, name):
    T = x.shape[0]
    tm = _rows(T)

    def body(x_ref, w_ref, sc_ref, sh_ref, o_ref):
        xv = x_ref[...]
        r = lax.rsqrt(jnp.mean(xv * xv, axis=-1, keepdims=True) + EPS)
        o_ref[...] = ((xv * r * w_ref[...]) * (1.0 + sc_ref[...]) + sh_ref[...]).astype(BF16)

    return _pcall(body, name=name, out_shape=_sds((T, D), BF16), grid=(T // tm,),
                  in_specs=[_row_spec(tm, D), _vec_spec(D), _vec_spec(D), _vec_spec(D)],
                  out_specs=_row_spec(tm, D), sem=("parallel",))(x, w, sc, sh)


def _norm_mod_bwd(x, dh, dres, w, sc, name):
    T = x.shape[0]
    tm = _rows(T)

    def body(x_ref, dh_ref, dres_ref, w_ref, sc_ref, dx_ref, acc_ref):
        @pl.when(pl.program_id(0) == 0)
        def _():
            acc_ref[...] = jnp.zeros_like(acc_ref)
        xv, dh_v, wv = x_ref[...], dh_ref[...], w_ref[...]
        r = lax.rsqrt(jnp.mean(xv * xv, axis=-1, keepdims=True) + EPS)
        n = xv * r
        dnw = dh_v * (1.0 + sc_ref[...])
        dn = dnw * wv
        dx_ref[...] = dres_ref[...] + r * (dn - n * jnp.mean(dn * n, axis=-1, keepdims=True))
        acc_ref[0:1, :] += _colsum(dh_v * (n * wv))
        acc_ref[1:2, :] += _colsum(dh_v)
        acc_ref[2:3, :] += _colsum(dnw * n)

    return _pcall(body, name=name, out_shape=[_sds((T, D)), _sds((8, D))], grid=(T // tm,),
                  in_specs=[_row_spec(tm, D), _row_spec(tm, D), _row_spec(tm, D), _vec_spec(D), _vec_spec(D)],
                  out_specs=[_row_spec(tm, D), _vec_spec(D, 8)], sem=("arbitrary",))(x, dh, dres, w, sc)


def _gate_bwd(dx, branch, g, name):
    T = dx.shape[0]
    tm = _rows(T)

    def body(dx_ref, br_ref, g_ref, db_ref, acc_ref):
        @pl.when(pl.program_id(0) == 0)
        def _():
            acc_ref[...] = jnp.zeros_like(acc_ref)
        dxv = dx_ref[...]
        db = g_ref[...] * dxv
        db_ref[...] = db.astype(BF16)
        acc_ref[0:1, :] += _colsum(dxv * br_ref[...])
        acc_ref[1:2, :] += _colsum(db)

    return _pcall(body, name=name, out_shape=[_sds((T, D), BF16), _sds((8, D))], grid=(T // tm,),
                  in_specs=[_row_spec(tm, D), _row_spec(tm, D), _vec_spec(D)],
                  out_specs=[_row_spec(tm, D), _vec_spec(D, 8)], sem=("arbitrary",))(dx, branch, g)


def _final_loss(x, wf, target, name):
    T = x.shape[0]
    tm = _rows(T)

    def body(x_ref, w_ref, t_ref, dx_ref, acc_ref):
        @pl.when(pl.program_id(0) == 0)
        def _():
            acc_ref[...] = jnp.zeros_like(acc_ref)
        xv, wv = x_ref[...], w_ref[...]
        r = lax.rsqrt(jnp.mean(xv * xv, axis=-1, keepdims=True) + EPS)
        n = xv * r
        err = n * wv - t_ref[...]
        dy = err * (1.0 / D)
        dn = dy * wv
        dx_ref[...] = r * (dn - n * jnp.mean(dn * n, axis=-1, keepdims=True))
        acc_ref[0:1, :] += _colsum(dy * n)
        acc_ref[1:2, :] += jnp.broadcast_to(_allsum(err * err) * (0.5 / D), (1, D))

    return _pcall(body, name=name, out_shape=[_sds((T, D)), _sds((8, D))], grid=(T // tm,),
                  in_specs=[_row_spec(tm, D), _vec_spec(D), _row_spec(tm, D)],
                  out_specs=[_row_spec(tm, D), _vec_spec(D, 8)], sem=("arbitrary",))(x, wf, target)


def _colsum_call(x, name):
    T, C = x.shape
    tm = _rows(T)

    def body(x_ref, o_ref):
        @pl.when(pl.program_id(0) == 0)
        def _():
            o_ref[...] = jnp.zeros_like(o_ref)
        o_ref[...] += _colsum(x_ref[...].astype(F32))

    return _pcall(body, name=name, out_shape=_sds((1, C)), grid=(T // tm,), in_specs=[_row_spec(tm, C)],
                  out_specs=_vec_spec(C), sem=("arbitrary",))(x)


def _mod_matmul(c_all, ada_w, name):
    n = ada_w.shape[2]

    def body(c_ref, w_ref, cond_ref, o_ref):
        cond = _silu(c_ref[...])
        cond_ref[...] = cond
        o_ref[0] = _nn(cond, w_ref[0])

    return _pcall(body, name=name, out_shape=[_sds((N_DEV, D)), _sds((DEPTH, N_DEV, n))], grid=(DEPTH,),
                  in_specs=[pl.BlockSpec((N_DEV, D), lambda l: (0, 0)), pl.BlockSpec((1, D, n), lambda l: (l, 0, 0))],
                  out_specs=[pl.BlockSpec((N_DEV, D), lambda l: (0, 0)), pl.BlockSpec((1, N_DEV, n), lambda l: (l, 0, 0))],
                  sem=("arbitrary",))(c_all, ada_w)


def _add_rows(a, b, name):
    def body(a_ref, b_ref, o_ref):
        o_ref[...] = a_ref[...] + b_ref[...]

    return _pcall(body, name=name, out_shape=_sds(a.shape))(a, b)


def _ada_w_grad(cond, dmod_cols, name):
    n = dmod_cols.shape[2]

    def body(c_ref, d_ref, o_ref):
        o_ref[0] = _tn(c_ref[...], d_ref[0])

    return _pcall(body, name=name, out_shape=_sds((DEPTH, D, n)), grid=(DEPTH,),
                  in_specs=[pl.BlockSpec((N_DEV, D), lambda l: (0, 0)), pl.BlockSpec((1, N_DEV, n), lambda l: (l, 0, 0))],
                  out_specs=pl.BlockSpec((1, D, n), lambda l: (l, 0, 0)), sem=("parallel",))(cond, dmod_cols)


def _conv_fwd(pm, conv_w, conv_b, name):
    T = pm.shape[0]
    tm = _rows(T)
    C = CONV_DIM

    def body(x_ref, prev_ref, w_ref, b_ref, o_ref):
        cur = x_ref[...]
        prev = jnp.where(pl.program_id(0) > 0, prev_ref[...], 0.0)
        cur8 = cur[0:8]
        row8 = lax.broadcasted_iota(jnp.int32, (8, C), 0)
        full = w_ref[3:4, :] * cur
        head = w_ref[3:4, :] * cur8
        for k in range(1, SSM_CONV):
            wk = w_ref[3 - k:4 - k, :]
            full = full + wk * pltpu.roll(cur, k, 0)
            head = head + wk * jnp.where(row8 < k, pltpu.roll(prev, k, 0), pltpu.roll(cur8, k, 0))
        o_ref[...] = full + b_ref[...]
        o_ref[0:8, :] = head + b_ref[...]

    return _pcall(
        body, name=name, out_shape=_sds((T, C)), grid=(T // tm,),
        in_specs=[pl.BlockSpec((tm, C), lambda i: (i, 2)),
                  pl.BlockSpec((8, C), lambda i: (jnp.maximum(i * (tm // 8) - 1, 0), 2)),
                  _vec_spec(C, SSM_CONV), _vec_spec(C)],
        out_specs=_row_spec(tm, C), sem=("parallel",))(pm, pm, conv_w, conv_b)


def _conv_bwd(dc, pm, conv_w, name):
    T = dc.shape[0]
    tm = _rows(T)
    C = CONV_DIM
    nt = T // tm

    def body(dc_ref, nxt_ref, x_ref, prev_ref, w_ref, dx_ref, acc_ref):
        i = pl.program_id(0)

        @pl.when(i == 0)
        def _():
            acc_ref[...] = jnp.zeros_like(acc_ref)
        dcv = dc_ref[...]
        nxt = jnp.where(i < nt - 1, nxt_ref[...], 0.0)
        xc = x_ref[...]
        prev = jnp.where(i > 0, prev_ref[...], 0.0)
        dc8h, dc8t, x8 = dcv[0:8], dcv[tm - 8:tm], xc[0:8]
        row8 = lax.broadcasted_iota(jnp.int32, (8, C), 0)
        full = w_ref[3:4, :] * dcv
        tail = w_ref[3:4, :] * dc8t
        acc_ref[3:4, :] += _colsum(dcv * xc)
        for k in range(1, SSM_CONV):
            wk = w_ref[3 - k:4 - k, :]
            full = full + wk * pltpu.roll(dcv, tm - k, 0)
            tail = tail + wk * jnp.where(row8 + k >= 8, pltpu.roll(nxt, 8 - k, 0), pltpu.roll(dc8t, 8 - k, 0))
            xs_head = jnp.where(row8 < k, pltpu.roll(prev, k, 0), pltpu.roll(x8, k, 0))
            prod = dcv * pltpu.roll(xc, k, 0)
            acc_ref[3 - k:4 - k, :] += _colsum(prod) - _colsum(prod[0:8]) + _colsum(dc8h * xs_head)
        acc_ref[4:5, :] += _colsum(dcv)
        dx_ref[...] = full
        dx_ref[tm - 8:tm, :] = tail

    return _pcall(
        body, name=name, out_shape=[_sds((T, C)), _sds((8, C))], grid=(nt,),
        in_specs=[_row_spec(tm, C),
                  pl.BlockSpec((8, C), lambda i: (jnp.minimum((i + 1) * (tm // 8), T // 8 - 1), 0)),
                  pl.BlockSpec((tm, C), lambda i: (i, 2)),
                  pl.BlockSpec((8, C), lambda i: (jnp.maximum(i * (tm // 8) - 1, 0), 2)),
                  _vec_spec(C, SSM_CONV)],
        out_specs=[_row_spec(tm, C), _vec_spec(C, 8)], sem=("arbitrary",))(dc, dc, pm, pm, conv_w)


def _ssd_prologue(cpre, dtr, dtb, alog):
    L = CHUNK
    xc = _silu(cpre)
    pre = dtr + dtb
    dt = jnp.maximum(pre, 0.0) + jnp.log1p(jnp.exp(-jnp.abs(pre)))
    a = -jnp.exp(alog)
    la = dt * a
    row = lax.broadcasted_iota(jnp.int32, (L, L), 0)
    col = lax.broadcasted_iota(jnp.int32, (L, L), 1)
    causal = row >= col
    tri = causal.astype(F32)
    lc = _nn(tri, la, HIGHEST)
    return xc, pre, dt, a, causal, tri, lc, row, col


def _head_indicator():
    m = np.zeros((LANES, SSM_INNER), np.float32)
    for h in range(SSM_HEADS):
        m[h, h * SSM_HEAD_DIM:(h + 1) * SSM_HEAD_DIM] = 1.0
    return jnp.asarray(m, dtype=BF16)


def _split_dot(x, ind, dims):
    hi = x.astype(BF16)
    lo = (x - hi.astype(F32)).astype(BF16)
    return _dot(hi, ind, dims) + _dot(lo, ind, dims)


def _expand(x16, ind):
    return _split_dot(x16, ind, ((1,), (0,)))


def _headsum(x, ind):
    return _split_dot(x, ind, ((1,), (1,)))


def _ssd_fwd(cpre, dtr, pm, dtb, alog, dskip, normw, ind, name, side=None):
    T = cpre.shape[0]
    nc = T // CHUNK
    L, P, H, HPG, N = CHUNK, SSM_HEAD_DIM, SSM_HEADS, SSM_HEADS // SSM_GROUPS, SSM_STATE
    half = SSM_INNER // SSM_GROUPS

    def body(cp_ref, dtr_ref, z_ref, dtb_ref, alog_ref, dskip_ref, nw_ref, ind_ref, ya_ref, y_ref, sp_ref, st_ref):
        @pl.when(pl.program_id(0) == 0)
        def _():
            st_ref[...] = jnp.zeros_like(st_ref)
        xc, _, dt, _, causal, _, lc, _, _ = _ssd_prologue(cp_ref[...], dtr_ref[...], dtb_ref[...], alog_ref[...])
        lct = lc.T
        ind = ind_ref[...]
        llast = lc[L - 1:L, :]
        xs = xc[:, :SSM_INNER]
        xd = xs * _expand(dt, ind)
        ex = _expand(jnp.exp(lc), ind)
        xd_end = _bf(xd * _expand(jnp.exp(llast - lc), ind))
        cdx = _expand(jnp.broadcast_to(jnp.exp(llast), (8, LANES)), ind)[0:1]
        xdb = _bf(xd)
        sp_ref[0] = st_ref[...]
        for g in range(SSM_GROUPS):
            sl = slice(g * half, (g + 1) * half)
            bm = _bf(xc[:, SSM_INNER + g * N:SSM_INNER + (g + 1) * N])
            cm = _bf(xc[:, SSM_INNER + (SSM_GROUPS + g) * N:SSM_INNER + (SSM_GROUPS + g + 1) * N])
            cb = _nt(cm, bm)
            st = st_ref[g]
            y_ref[:, sl] = ex[:, sl] * _nn(cm, _bf(st)) + dskip_ref[:, sl] * xs[:, sl]
            st_ref[g] = st * cdx[:, sl] + _tn(bm, xd_end[:, sl])
            for j in range(HPG):
                h = g * HPG + j
                decay = jnp.where(causal, jnp.exp(jnp.where(causal, lc[:, h:h + 1] - lct[h:h + 1, :], 0.0)), 0.0)
                y_ref[:, h * P:(h + 1) * P] += _nn(_bf(cb * decay), xdb[:, h * P:(h + 1) * P])
        y2 = y_ref[...] * _silu(z_ref[...])
        for g in range(SSM_GROUPS):
            yg = y2[:, g * half:(g + 1) * half]
            r = lax.rsqrt(jnp.mean(yg * yg, axis=-1, keepdims=True) + EPS)
            ya_ref[:, g * half:(g + 1) * half] = (yg * r * nw_ref[:, g * half:(g + 1) * half]).astype(BF16)

    return _call(
        body, (cpre, dtr, pm, dtb, alog, dskip, normw, ind), side, name=name,
        out_shape=[_sds((T, SSM_INNER), BF16), _sds((T, SSM_INNER)), _sds((nc, SSM_GROUPS, N, half))], grid=(nc,),
        in_specs=[_row_spec(L, CONV_DIM), _row_spec(L, LANES), _row_spec(L, SSM_INNER, 0),
                  _vec_spec(LANES), _vec_spec(LANES), _vec_spec(SSM_INNER), _vec_spec(SSM_INNER), _vec_spec(SSM_INNER, LANES)],
        out_specs=[_row_spec(L, SSM_INNER), _row_spec(L, SSM_INNER),
                   pl.BlockSpec((1, SSM_GROUPS, N, half), lambda i: (i, 0, 0, 0))],
        scratch=[pltpu.VMEM((SSM_GROUPS, N, half), F32)], sem=("arbitrary",))


def _ssd_bwd(cpre, dtr, pm, ypre, sprev, dya, dtb, alog, dskip, normw, ind, name, side=None):
    T = cpre.shape[0]
    nc = T // CHUNK
    L, P, H, HPG, N = CHUNK, SSM_HEAD_DIM, SSM_HEADS, SSM_HEADS // SSM_GROUPS, SSM_STATE
    half = SSM_INNER // SSM_GROUPS

    def body(cp_ref, dtr_ref, z_ref, y_ref, sp_ref, dya_ref, dtb_ref, alog_ref, dskip_ref, nw_ref, ind_ref,
             dz_ref, dcp_ref, ddtr_ref, acc_ref, dnw_ref, ds_ref, dy_ref, dxd_ref, rr_ref, yoff_ref, dcd_ref):
        @pl.when(pl.program_id(0) == 0)
        def _():
            ds_ref[...] = jnp.zeros_like(ds_ref)
            acc_ref[...] = jnp.zeros_like(acc_ref)
            dnw_ref[...] = jnp.zeros_like(dnw_ref)
        cpre_v = cp_ref[...]
        xc, pre, dt, a, causal, tri, lc, row, col = _ssd_prologue(cpre_v, dtr_ref[...], dtb_ref[...], alog_ref[...])
        lct = lc.T
        zv, yv = z_ref[...], y_ref[...]
        sz = _silu(zv)
        y2 = yv * sz
        dya_v = dya_ref[...]
        nwv = nw_ref[...]
        for g in range(SSM_GROUPS):
            sl = slice(g * half, (g + 1) * half)
            yg = y2[:, sl]
            r = lax.rsqrt(jnp.mean(yg * yg, axis=-1, keepdims=True) + EPS)
            nrm = yg * r
            dnw_ref[:, sl] += _colsum(dya_v[:, sl] * nrm)
            dn = dya_v[:, sl] * nwv[:, sl]
            dy2 = r * (dn - nrm * jnp.mean(dn * nrm, axis=-1, keepdims=True))
            dy_ref[:, sl] = dy2 * sz[:, sl]
            dz_ref[:, sl] = dy2 * yv[:, sl] * _dsilu(zv[:, sl])
        ind = ind_ref[...]
        llast = lc[L - 1:L, :]
        dte16 = jnp.exp(llast - lc)
        cd16 = jnp.exp(llast)
        xs = xc[:, :SSM_INNER]
        dtx = _expand(dt, ind)
        ex = _expand(jnp.exp(lc), ind)
        dtex = _expand(dte16, ind)
        cdx = _expand(jnp.broadcast_to(cd16, (8, LANES)), ind)[0:1]
        xd = xs * dtx
        xdb = _bf(xd)
        xd_end = _bf(xd * dtex)
        dyv = dy_ref[...]
        dy_off = _bf(ex * dyv)
        dyb = _bf(dyv)
        dskx = dskip_ref[...]
        lane_c = lax.broadcasted_iota(jnp.int32, (L, LANES), 1)
        lane1 = lax.broadcasted_iota(jnp.int32, (1, LANES), 1)
        sub16 = lax.broadcasted_iota(jnp.int32, (H, L), 0)
        dlc_c = jnp.zeros((L, LANES), F32)
        dlc_r = jnp.zeros((H, L), F32)
        for g in range(SSM_GROUPS):
            sl = slice(g * half, (g + 1) * half)
            b_lo = SSM_INNER + g * N
            c_lo = SSM_INNER + (SSM_GROUPS + g) * N
            bm, cm = _bf(xc[:, b_lo:b_lo + N]), _bf(xc[:, c_lo:c_lo + N])
            cb = _nt(cm, bm)
            st, dst = sp_ref[0, g], ds_ref[g]
            stb, dstb = _bf(st), _bf(dst)
            dcm = _nt(dy_off[:, sl], stb)
            ds_ref[g] = _tn(cm, dy_off[:, sl]) + dst * cdx[:, sl]
            rr_ref[:, sl] = _nn(bm, dstb)
            yoff_ref[:, sl] = ex[:, sl] * _nn(cm, stb)
            db = _nt(xd_end[:, sl], dstb)
            dcd_ref[:, sl] = _colsum(dst * st)
            dcb = jnp.zeros((L, L), F32)
            for j in range(HPG):
                h = g * HPG + j
                hs = slice(h * P, (h + 1) * P)
                decay = jnp.where(causal, jnp.exp(jnp.where(causal, lc[:, h:h + 1] - lct[h:h + 1, :], 0.0)), 0.0)
                m = cb * decay
                dxd_ref[:, hs] = _tn(_bf(m), dyb[:, hs])
                dm = _nt(dyb[:, hs], xdb[:, hs])
                dcb = dcb + dm * decay
                gm = dm * m
                dlc_c = dlc_c + jnp.where(lane_c == h, _rowsum(gm), 0.0)
                dlc_r = dlc_r + jnp.where(sub16 == h, _colsum(gm), 0.0)
            dcbb = _bf(dcb)
            dcp_ref[:, c_lo:c_lo + N] = dcm + _nn(dcbb, bm)
            dcp_ref[:, b_lo:b_lo + N] = db + _tn(dcbb, cm)
        dxd_diag, rr = dxd_ref[...], rr_ref[...]
        tt = _headsum(rr * xd, ind) * dte16
        dlc_rt = jnp.concatenate([dlc_r, jnp.zeros((LANES - H, L), F32)], axis=0).T
        dlc = dlc_c - dlc_rt + _headsum(dyv * yoff_ref[...], ind) - tt
        dcd = _headsum(jnp.broadcast_to(dcd_ref[...], (8, SSM_INNER)), ind)[0:1]
        dlc = dlc + jnp.where(row == L - 1, _colsum(tt) + dcd * cd16, 0.0)
        dla = _tn(tri, dlc, HIGHEST)
        dxd = dxd_diag + dtex * rr
        ddt = _headsum(dxd * xs, ind) + dla * a
        ddtr = jnp.where(lane_c < H, ddt * _sigmoid(pre), 0.0)
        ddtr_ref[...] = ddtr
        acc_ref[0:1, :] += _colsum(ddtr)
        acc_ref[1:2, :] += jnp.where(lane1 < H, _colsum(dla * dt) * a, 0.0)
        acc_ref[2:3, :] += _headsum(jnp.broadcast_to(_colsum(dyv * xs), (8, SSM_INNER)), ind)[0:1]
        dcp_ref[:, 0:SSM_INNER] = dxd * dtx + dskx * dyv
        dcp_ref[...] = dcp_ref[...] * _dsilu(cpre_v)

    rev = lambda i: (nc - 1 - i, 0)
    rspec = lambda c: pl.BlockSpec((L, c), rev)
    return _call(
        body, (cpre, dtr, pm, ypre, sprev, dya, dtb, alog, dskip, normw, ind), side, name=name,
        out_shape=[_sds((T, SSM_INNER)), _sds((T, CONV_DIM)), _sds((T, LANES)), _sds((8, LANES)), _sds((1, SSM_INNER))],
        grid=(nc,),
        in_specs=[rspec(CONV_DIM), rspec(LANES), rspec(SSM_INNER), rspec(SSM_INNER),
                  pl.BlockSpec((1, SSM_GROUPS, N, half), lambda i: (nc - 1 - i, 0, 0, 0)), rspec(SSM_INNER),
                  _vec_spec(LANES), _vec_spec(LANES), _vec_spec(SSM_INNER), _vec_spec(SSM_INNER), _vec_spec(SSM_INNER, LANES)],
        out_specs=[rspec(SSM_INNER), rspec(CONV_DIM), rspec(LANES), _vec_spec(LANES, 8), _vec_spec(SSM_INNER)],
        scratch=[pltpu.VMEM((SSM_GROUPS, N, half), F32), pltpu.VMEM((L, SSM_INNER), F32), pltpu.VMEM((L, SSM_INNER), F32),
                 pltpu.VMEM((L, SSM_INNER), F32), pltpu.VMEM((L, SSM_INNER), F32), pltpu.VMEM((1, SSM_INNER), F32)],
        sem=("arbitrary",))


def _gmlp_common(u, v, lnw, lnb):
    ug = _gelu(u)
    vg = _gelu(v)
    mu = jnp.mean(vg, axis=-1, keepdims=True)
    cen = vg - mu
    rstd = lax.rsqrt(jnp.mean(cen * cen, axis=-1, keepdims=True) + EPS)
    vhat = cen * rstd
    return ug, rstd, vhat, vhat * lnw + lnb


def _causal_mask():
    row = lax.broadcasted_iota(jnp.int32, (CHUNK, CHUNK), 0)
    col = lax.broadcasted_iota(jnp.int32, (CHUNK, CHUNK), 1)
    return row >= col


def _gmlp_fwd(pm, lnw, lnb, ws, bs_exp, name):
    T = pm.shape[0]
    nc = T // CHUNK
    L, G = CHUNK, GMLP_GROUPS

    def body(u_ref, v_ref, lnw_ref, lnb_ref, ws_ref, bs_ref, o_ref):
        ug, _, _, vn = _gmlp_common(u_ref[...], v_ref[...], lnw_ref[...], lnb_ref[...])
        causal = _causal_mask()
        for g in range(G):
            sl = slice(g * L, (g + 1) * L)
            wm = _bf(jnp.where(causal, ws_ref[g], 0.0))
            sv = _nn(wm, _bf(vn[:, sl])) + bs_ref[:, sl]
            o_ref[:, sl] = (ug[:, sl] * sv).astype(BF16)

    return _pcall(
        body, name=name, out_shape=_sds((T, GMLP_INNER), BF16), grid=(nc,),
        in_specs=[_row_spec(L, GMLP_INNER, 1), _row_spec(L, GMLP_INNER, 2), _vec_spec(GMLP_INNER), _vec_spec(GMLP_INNER),
                  pl.BlockSpec((G, L, L), lambda i: (0, 0, 0)), _vec_spec(GMLP_INNER, L)],
        out_specs=_row_spec(L, GMLP_INNER), sem=("parallel",))(pm, pm, lnw, lnb, ws, bs_exp)


def _gmlp_bwd(pm, dyb, lnw, lnb, ws, bs_exp, name, side=None):
    T = pm.shape[0]
    nc = T // CHUNK
    L, G = CHUNK, GMLP_GROUPS

    def body(u_ref, v_ref, dy_ref, lnw_ref, lnb_ref, ws_ref, bs_ref, du_ref, dv_ref, dws_ref, dbs_ref, acc_ref, dvn_ref):
        @pl.when(pl.program_id(0) == 0)
        def _():
            dws_ref[...] = jnp.zeros_like(dws_ref)
            dbs_ref[...] = jnp.zeros_like(dbs_ref)
            acc_ref[...] = jnp.zeros_like(acc_ref)
        uv, vv, dyv, lnwv = u_ref[...], v_ref[...], dy_ref[...], lnw_ref[...]
        ug, rstd, vhat, vn = _gmlp_common(uv, vv, lnwv, lnb_ref[...])
        causal = _causal_mask()
        lane = lax.broadcasted_iota(jnp.int32, (L, LANES), 1)
        dbs = jnp.zeros((L, LANES), F32)
        for g in range(G):
            sl = slice(g * L, (g + 1) * L)
            wm = _bf(jnp.where(causal, ws_ref[g], 0.0))
            vng = _bf(vn[:, sl])
            sv = _nn(wm, vng) + bs_ref[:, sl]
            du_ref[:, sl] = dyv[:, sl] * sv * _dgelu(uv[:, sl])
            dsv = dyv[:, sl] * ug[:, sl]
            dsvb = _bf(dsv)
            dws_ref[g] += jnp.where(causal, _nt(dsvb, vng), 0.0)
            dbs = dbs + jnp.where(lane == g, _rowsum(dsv), 0.0)
            dvn_ref[:, sl] = _tn(wm, dsvb)
        dbs_ref[...] += dbs
        dvn = dvn_ref[...]
        acc_ref[0:1, :] += _colsum(dvn * vhat)
        acc_ref[1:2, :] += _colsum(dvn)
        dvh = dvn * lnwv
        dvg = rstd * (dvh - jnp.mean(dvh, axis=-1, keepdims=True) - vhat * jnp.mean(dvh * vhat, axis=-1, keepdims=True))
        dv_ref[...] = dvg * _dgelu(vv)

    return _call(
        body, (pm, pm, dyb, lnw, lnb, ws, bs_exp), side, name=name,
        out_shape=[_sds((T, GMLP_INNER)), _sds((T, GMLP_INNER)), _sds((G, L, L)), _sds((L, LANES)), _sds((8, GMLP_INNER))],
        grid=(nc,),
        in_specs=[_row_spec(L, GMLP_INNER, 1), _row_spec(L, GMLP_INNER, 2), _row_spec(L, GMLP_INNER),
                  _vec_spec(GMLP_INNER), _vec_spec(GMLP_INNER), pl.BlockSpec((G, L, L), lambda i: (0, 0, 0)),
                  _vec_spec(GMLP_INNER, L)],
        out_specs=[_row_spec(L, GMLP_INNER), _row_spec(L, GMLP_INNER), pl.BlockSpec((G, L, L), lambda i: (0, 0, 0)),
                   _vec_spec(LANES, L), _vec_spec(GMLP_INNER, 8)],
        scratch=[pltpu.VMEM((L, GMLP_INNER), F32)], sem=("arbitrary",))


def _rel_buckets():
    qi = np.arange(CHUNK)[:, None]
    sj = np.arange(2 * CHUNK)[None, :]
    dist = np.maximum(qi + CHUNK - sj, 0)
    max_exact = REL_BUCKETS // 2
    log_ratio = (np.log(np.maximum(dist, 1).astype(np.float32) / np.float32(max_exact))
                 / np.float32(math.log(REL_MAX_DIST / max_exact))).astype(np.float32)
    large = max_exact + (log_ratio * np.float32(REL_BUCKETS - max_exact)).astype(np.int32)
    return np.where(dist < max_exact, dist, np.minimum(large, REL_BUCKETS - 1))


def _bucket_onehot_t():
    bucket = _rel_buckets().reshape(-1)
    return jnp.asarray((np.arange(REL_BUCKETS)[:, None] == bucket[None, :]).astype(np.float32))


def _bias_from_table(table_t, onehot_t, name):
    def body(t_ref, o_ref, out_ref):
        out_ref[...] = _nn(t_ref[...], o_ref[...], HIGHEST)

    return _pcall(body, name=name, out_shape=_sds((ATTN_HEADS, onehot_t.shape[1])))(table_t, onehot_t)


def _table_from_dbias(dbias, onehot_t, name):
    def body(d_ref, o_ref, out_ref):
        out_ref[...] = _nt(d_ref[...], o_ref[...], HIGHEST)

    return _pcall(body, name=name, out_shape=_sds((ATTN_HEADS, REL_BUCKETS)))(dbias, onehot_t)


def _softmax_sink(logits, sink, mask):
    logits = jnp.where(mask, logits, NEG_INF)
    mx = jnp.maximum(jnp.max(logits, axis=-1, keepdims=True), sink)
    e = jnp.exp(logits - mx)
    es = jnp.exp(sink - mx)
    inv = 1.0 / (_rowsum(e) + es)
    return e * inv, es * inv


def _attn_mask(n, heads):
    qi = lax.broadcasted_iota(jnp.int32, (heads * CHUNK, 2 * CHUNK), 0) & (CHUNK - 1)
    sj = lax.broadcasted_iota(jnp.int32, (heads * CHUNK, 2 * CHUNK), 1)
    rel = qi + CHUNK - sj
    return (rel >= 0) & (rel < CHUNK) & ((sj >= CHUNK) | (n > 0))


def _stack_heads(ref, first, count, width):
    return jnp.concatenate([_bf(ref[:, (first + j) * width:(first + j + 1) * width]) for j in range(count)], axis=0)


def _attn_fwd(qkv, bias, sinks, name):
    T = qkv.shape[0]
    nb = T // CHUNK
    L, DH, HPK = CHUNK, ATTN_DH, ATTN_HEADS // ATTN_KV
    scale = DH ** -0.5
    kcol, vcol = ATTN_HEADS * DH // LANES, ATTN_HEADS * DH // LANES + 1

    def body(q_ref, k_ref, v_ref, kp_ref, vp_ref, bias_ref, sink_ref, o_ref, lg_ref, p_ref):
        n = pl.program_id(0)
        mask = _attn_mask(n, 1)
        kband = _bf(jnp.concatenate([kp_ref[...], k_ref[...]], axis=0))
        vband = _bf(jnp.concatenate([vp_ref[...], v_ref[...]], axis=0))
        for kv in range(ATTN_KV):
            lg_ref[...] = _nt(_stack_heads(q_ref, kv * HPK, HPK, DH), kband[:, kv * DH:(kv + 1) * DH])
            for j in range(HPK):
                h = kv * HPK + j
                p, _ = _softmax_sink(lg_ref[j * L:(j + 1) * L, :] * scale + bias_ref[h], sink_ref[h], mask)
                p_ref[j * L:(j + 1) * L, :] = _bf(p)
            og = _nn(p_ref[...], vband[:, kv * DH:(kv + 1) * DH])
            for j in range(HPK):
                h = kv * HPK + j
                o_ref[:, h * DH:(h + 1) * DH] = og[j * L:(j + 1) * L].astype(BF16)

    prev = lambda i: jnp.maximum(i - 1, 0)
    return _pcall(
        body, name=name, out_shape=_sds((T, ATTN_HEADS * DH), BF16), grid=(nb,),
        in_specs=[_row_spec(L, ATTN_HEADS * DH, 0), _row_spec(L, LANES, kcol), _row_spec(L, LANES, vcol),
                  pl.BlockSpec((L, LANES), lambda i: (prev(i), kcol)), pl.BlockSpec((L, LANES), lambda i: (prev(i), vcol)),
                  pl.BlockSpec((ATTN_HEADS, L, 2 * L), lambda i: (0, 0, 0)),
                  pl.BlockSpec(memory_space=pltpu.SMEM)],
        out_specs=_row_spec(L, ATTN_HEADS * DH),
        scratch=[pltpu.VMEM((HPK * L, 2 * L), F32), pltpu.VMEM((HPK * L, 2 * L), BF16)],
        sem=("parallel",))(qkv, qkv, qkv, qkv, qkv, bias, sinks)


def _attn_bwd(qkv, datt, bias, sinks, name):
    T = qkv.shape[0]
    nb = T // CHUNK
    L, DH, HPK = CHUNK, ATTN_DH, ATTN_HEADS // ATTN_KV
    scale = DH ** -0.5
    kcol, vcol = ATTN_HEADS * DH // LANES, ATTN_HEADS * DH // LANES + 1

    def body(q_ref, k_ref, v_ref, kp_ref, vp_ref, do_ref, bias_ref, sink_ref,
             dq_ref, dk_ref, dv_ref, dbias_ref, dsink_ref, pend_k, pend_v, band_k, band_v, lg_ref, dp_ref, p_ref, dl_ref):
        n = pl.program_id(0)

        @pl.when(n == 0)
        def _():
            dbias_ref[...] = jnp.zeros_like(dbias_ref)
            dsink_ref[...] = jnp.zeros_like(dsink_ref)

        @pl.when(n < nb)
        def _():
            mask = _attn_mask(n, 1)
            kband = _bf(jnp.concatenate([kp_ref[...], k_ref[...]], axis=0))
            vband = _bf(jnp.concatenate([vp_ref[...], v_ref[...]], axis=0))
            lane1 = lax.broadcasted_iota(jnp.int32, (1, LANES), 1)
            dsink = jnp.zeros((1, LANES), F32)
            for kv in range(ATTN_KV):
                kb, vb = kband[:, kv * DH:(kv + 1) * DH], vband[:, kv * DH:(kv + 1) * DH]
                qg = _stack_heads(q_ref, kv * HPK, HPK, DH)
                dog = _stack_heads(do_ref, kv * HPK, HPK, DH)
                lg_ref[...] = _nt(qg, kb)
                dp_ref[...] = _nt(dog, vb)
                for j in range(HPK):
                    h = kv * HPK + j
                    rows = slice(j * L, (j + 1) * L)
                    p, ps = _softmax_sink(lg_ref[rows, :] * scale + bias_ref[h], sink_ref[h], mask)
                    dp = dp_ref[rows, :]
                    delta = _rowsum(p * dp)
                    dl = p * (dp - delta)
                    dbias_ref[h] += dl
                    p_ref[rows, :] = _bf(p)
                    dl_ref[rows, :] = _bf(dl)
                    dsink = dsink + jnp.where(lane1 == h, -_colsum(ps * delta), 0.0)
                band_v[:, kv * DH:(kv + 1) * DH] = _tn(p_ref[...], dog)
                dqg = _nn(dl_ref[...], kb) * scale
                band_k[:, kv * DH:(kv + 1) * DH] = _tn(dl_ref[...], qg) * scale
                for j in range(HPK):
                    h = kv * HPK + j
                    dq_ref[:, h * DH:(h + 1) * DH] = dqg[j * L:(j + 1) * L]
            dsink_ref[...] += dsink

            @pl.when(n > 0)
            def _():
                dk_ref[...] = pend_k[...] + band_k[0:L, :]
                dv_ref[...] = pend_v[...] + band_v[0:L, :]
            pend_k[...] = band_k[L:2 * L, :]
            pend_v[...] = band_v[L:2 * L, :]

        @pl.when(n == nb)
        def _():
            dk_ref[...] = pend_k[...]
            dv_ref[...] = pend_v[...]

    cur = lambda i: jnp.minimum(i, nb - 1)
    prev = lambda i: jnp.maximum(jnp.minimum(i, nb - 1) - 1, 0)
    lag = lambda i: jnp.maximum(i - 1, 0)
    return _pcall(
        body, name=name,
        out_shape=[_sds((T, ATTN_HEADS * DH)), _sds((T, LANES)), _sds((T, LANES)), _sds((ATTN_HEADS, L, 2 * L)), _sds((1, LANES))],
        grid=(nb + 1,),
        in_specs=[pl.BlockSpec((L, ATTN_HEADS * DH), lambda i: (cur(i), 0)),
                  pl.BlockSpec((L, LANES), lambda i: (cur(i), kcol)), pl.BlockSpec((L, LANES), lambda i: (cur(i), vcol)),
                  pl.BlockSpec((L, LANES), lambda i: (prev(i), kcol)), pl.BlockSpec((L, LANES), lambda i: (prev(i), vcol)),
                  pl.BlockSpec((L, ATTN_HEADS * DH), lambda i: (cur(i), 0)),
                  pl.BlockSpec((ATTN_HEADS, L, 2 * L), lambda i: (0, 0, 0)),
                  pl.BlockSpec(memory_space=pltpu.SMEM)],
        out_specs=[pl.BlockSpec((L, ATTN_HEADS * DH), lambda i: (cur(i), 0)),
                   pl.BlockSpec((L, LANES), lambda i: (lag(i), 0)), pl.BlockSpec((L, LANES), lambda i: (lag(i), 0)),
                   pl.BlockSpec((ATTN_HEADS, L, 2 * L), lambda i: (0, 0, 0)), _vec_spec(LANES)],
        scratch=[pltpu.VMEM((L, LANES), F32), pltpu.VMEM((L, LANES), F32),
                 pltpu.VMEM((2 * L, LANES), F32), pltpu.VMEM((2 * L, LANES), F32),
                 pltpu.VMEM((HPK * L, 2 * L), F32), pltpu.VMEM((HPK * L, 2 * L), F32),
                 pltpu.VMEM((HPK * L, 2 * L), BF16), pltpu.VMEM((HPK * L, 2 * L), BF16)],
        sem=("arbitrary",))(qkv, qkv, qkv, qkv, qkv, datt, bias, sinks)


def _pad_rows(a, mult):
    pad = (-a.shape[-2]) % mult
    if pad == 0:
        return a
    cfg = [(0, 0)] * (a.ndim - 2) + [(0, pad), (0, 0)]
    return jnp.pad(a, cfg)


class _Pack:
    def __init__(self, width, mult, total_mult):
        self.width, self.mult, self.total_mult = width, mult, total_mult
        self.entries = []
        self.rows = 0

    def add(self, key, shape):
        n = int(np.prod(shape))
        rows = -(-n // self.width)
        self.entries.append((key, self.rows, rows, tuple(shape)))
        self.rows += -(-rows // self.mult) * self.mult

    @property
    def total(self):
        return -(-self.rows // self.total_mult) * self.total_mult

    def pack(self, pieces, dtype, lead=()):
        parts = []
        for key, _, rows, shape in self.entries:
            a = pieces[key].astype(dtype).reshape(lead + (-1,))
            n = int(np.prod(shape))
            a = jnp.pad(a, [(0, 0)] * len(lead) + [(0, rows * self.width - n)])
            a = a.reshape(lead + (rows, self.width))
            parts.append(_pad_rows(a, self.mult))
        out = jnp.concatenate(parts, axis=len(lead))
        return _pad_rows(out, self.total_mult)

    def unpack(self, packed, lead=()):
        out = {}
        for key, off, rows, shape in self.entries:
            a = lax.slice_in_dim(packed, off, off + rows, axis=len(lead))
            a = a.reshape(lead + (-1,))
            n = int(np.prod(shape))
            out[key] = lax.slice_in_dim(a, 0, n, axis=len(lead)).reshape(lead + shape)
        return out


def _ffn_fwd(x, mod, norm_w, wg_t, wu_t, wd, tag, gather=None):
    h = _norm_mod(x, norm_w, mod[4:5], mod[3:4], f"ffn_norm_{tag}")
    side = None if gather is None else (_GatherOps, gather)
    gate, up, act, *gathered = _mm_swiglu(h, wg_t, wu_t, f"ffn_gateup_{tag}", side=side)
    x_out, ffn_out = _mm([(act, wd)], "nn", name=f"ffn_down_{tag}", resid=x, gvec=mod[5:6], keep=True)
    return (x_out, dict(h=h, gate=gate, up=up, act=act, out=ffn_out), *gathered)


def _ffn_bwd(dx_out, x_in, saved, mod, norm_w, wg_t, wu_t, wd, tag, exchange=None):
    dffn, acc_g = _gate_bwd(dx_out, saved["out"], mod[5:6], f"ffn_gate_bwd_{tag}")
    side = None if exchange is None else (_SiblingOps, exchange)
    dgate, dup, *theirs = _mm_swiglu_bwd(dffn, wd, saved["gate"], saved["up"], f"ffn_act_bwd_{tag}", side=side)
    d_wd = _mm_tn(saved["act"], dffn, name=f"ffn_dwd_{tag}")
    d_wg_t = _mm_tn(dgate, saved["h"], name=f"ffn_dwg_{tag}")
    d_wu_t = _mm_tn(dup, saved["h"], name=f"ffn_dwu_{tag}")
    side = None if exchange is None else (_ChipsOps, _pair_sum(exchange, theirs[0], f"grads_pair_sum_{tag}"))
    dh = _mm([(dgate, wg_t), (dup, wu_t)], "nn", name=f"ffn_dh_{tag}", side=side)
    dh, *from_chips = dh if exchange is not None else (dh,)
    dx, acc_n = _norm_mod_bwd(x_in, dh, dx_out, norm_w, mod[4:5], f"ffn_norm_bwd_{tag}")
    grads = dict(d_wg=d_wg_t, d_wu=d_wu_t, d_wd=d_wd, d_g=acc_g[0], d_sc=acc_n[0], d_sh=acc_n[1], d_nw=acc_n[2])
    return (dx, grads, *from_chips)


_BIG = [
    ("out_w", "out_w_even", 0, "row"), ("qkv_w", "qkv_w", 0, "col"), ("o_w", "o_w", 0, "row"),
    ("gate0", "ffn_gate_w", 0, "col"), ("up0", "ffn_up_w", 0, "col"), ("down0", "ffn_down_w", 0, "row"),
    ("gate1", "ffn_gate_w", 1, "col"), ("up1", "ffn_up_w", 1, "col"), ("down1", "ffn_down_w", 1, "row"),
    ("in_w", "in_w_even", 0, "col"),
]


def _to_wire(a, kind):
    return a.T if kind == "col" else a


_GATHER_GROUPS = [["out_w", "in_w"], ["gate0", "up0", "down0"], ["qkv_w", "o_w", "gate1", "up1", "down1"]]

_REPLICATED = ["ada_b", "norm_mix_w", "norm_ffn_w", "conv_b", "dt_bias", "a_log", "d_skip", "ssm_norm_w", "gmlp_ln_w",
               "gmlp_ln_b", "gmlp_ws", "gmlp_bs", "sinks", "rel_table", "final_norm_w"]
_TINY_SHARDED = ["conv_w", "qkv_b", "o_b"]

_WEIGHTS = ['ada_w', 'ada_b', 'norm_mix_w', 'norm_ffn_w', 'in_w_even', 'conv_w', 'conv_b', 'dt_bias', 'a_log', 'd_skip',
            'ssm_norm_w', 'gmlp_ln_w', 'gmlp_ln_b', 'gmlp_ws', 'gmlp_bs', 'out_w_even', 'qkv_w', 'qkv_b', 'o_w', 'o_b',
            'sinks', 'rel_table', 'ffn_gate_w', 'ffn_up_w', 'ffn_down_w', 'final_norm_w']


def _step(x, c, loss_target, W, M, V):
    T = x.shape[1]
    x0 = x[0]
    target = loss_target[0]
    me = 4 * lax.axis_index("x") + 2 * lax.axis_index("y") + lax.axis_index("c")

    big = _Pack(D, 16, 128)
    for key, name, layer, kind in _BIG:
        big.add(key, W[name][layer].shape)
    w_wire_local = {key: _to_wire(W[name][layer].astype(BF16), kind) for key, name, layer, kind in _BIG}
    gather_packs = []
    for keys in _GATHER_GROUPS:
        gp = _Pack(D, 1, 128)
        for key in keys:
            gp.add(key, w_wire_local[key].shape)
        gather_packs.append((gp, gp.pack(w_wire_local, BF16)))
    full = {}

    def gathered_weights(group, gathered):
        shards = gather_packs[group][0].unpack(gathered, lead=(N_DEV,))
        full.update({key: a.reshape(-1, D) for key, a in shards.items()})

    gathered_weights(0, _all_gather(gather_packs[0][1], "gather_weights"))

    small_in = _Pack(D, 8, 8)
    small_in.add("c", (1, D))
    small_in.add("conv_w", W["conv_w"][0].shape)
    small_in.add("qkv_b", W["qkv_b"][0].shape)
    small_in.add("o_b", W["o_b"][0].shape)
    sm = small_in.unpack(_all_gather(small_in.pack(
        dict(c=c, conv_w=W["conv_w"][0], qkv_b=W["qkv_b"][0], o_b=W["o_b"][0]), F32), "gather_small"), lead=(N_DEV,))
    c_all = sm["c"].reshape(N_DEV, D)
    conv_w_full = jnp.transpose(sm["conv_w"], (1, 0, 2)).reshape(SSM_CONV, CONV_DIM)
    qkv_b_full = sm["qkv_b"].reshape(1, QKV_DIM)
    o_b_full = sm["o_b"].reshape(1, D)

    ncol = W["ada_w"].shape[2]
    cond, mod_cols = _mod_matmul(c_all, W["ada_w"], "mod_matmul")
    mod_g = _all_gather(mod_cols.reshape(DEPTH * N_DEV, ncol), "gather_mod").reshape(N_DEV, DEPTH, N_DEV, ncol)
    mod_me = lax.dynamic_index_in_dim(mod_g, me, axis=2, keepdims=False)
    mod_me = jnp.transpose(mod_me, (1, 0, 2)).reshape(DEPTH, 6, D)
    mod_me = jnp.pad(mod_me, ((0, 0), (0, 2), (0, 0))).reshape(DEPTH * 8, D)
    ada_b_rows = jnp.pad(W["ada_b"].reshape(DEPTH, 6, D), ((0, 0), (0, 2), (0, 0))).reshape(DEPTH * 8, D)
    mod_all = _add_rows(mod_me, ada_b_rows, "mod_bias").reshape(DEPTH, 8, D)
    mod0, mod1 = mod_all[0], mod_all[1]

    in_t = full["in_w"]
    o1, o2, o3, o4 = SSM_INNER, SSM_INNER + CONV_DIM, SSM_INNER + CONV_DIM + SSM_HEADS, SSM_INNER + CONV_DIM + SSM_HEADS + GMLP_INNER
    w_z, w_xbc, w_dt, w_u, w_v = in_t[:o1], in_t[o1:o2], in_t[o2:o3], in_t[o3:o4], in_t[o4:]
    w_main = jnp.concatenate([w_z, w_u, w_v, w_xbc], axis=0)
    w_dtp = jnp.pad(w_dt, ((0, LANES - SSM_HEADS), (0, 0)))
    out_w = full["out_w"]

    pad16 = lambda a: jnp.pad(a.reshape(1, SSM_HEADS), ((0, 0), (0, LANES - SSM_HEADS)))
    dtb, alog = pad16(W["dt_bias"][0]), pad16(W["a_log"][0])
    dskip = jnp.repeat(W["d_skip"][0], SSM_HEAD_DIM).reshape(1, SSM_INNER)
    ssm_nw = W["ssm_norm_w"]
    lnw, lnb = W["gmlp_ln_w"], W["gmlp_ln_b"]
    ws = W["gmlp_ws"][0]
    bs_exp = jnp.repeat(W["gmlp_bs"][0].T, CHUNK, axis=1)
    conv_b = W["conv_b"]
    nmw, nfw = W["norm_mix_w"], W["norm_ffn_w"]
    onehot_t = _bucket_onehot_t()
    head_ind = _head_indicator()
    bias = _bias_from_table(W["rel_table"].T, onehot_t, "rel_bias").reshape(ATTN_HEADS, CHUNK, 2 * CHUNK)
    sinks = W["sinks"][0]

    h0 = _norm_mod(x0, nmw[0:1], mod0[1:2], mod0[0:1], "mix_norm_0")
    pm = _mm([(h0, w_main)], "nt", name="in_proj", tn_pref=1536)
    dtr = _mm([(h0, w_dtp)], "nt", name="in_proj_dt")
    cpre = _conv_fwd(pm, conv_w_full, conv_b, "conv_fwd")
    ya, ypre, sprev, gathered_b = _ssd_fwd(cpre, dtr, pm, dtb, alog, dskip, ssm_nw, head_ind, "ssd_fwd",
                                           side=(_GatherOps, gather_packs[1][1]))
    gathered_weights(1, gathered_b)
    yb = _gmlp_fwd(pm, lnw, lnb, ws, bs_exp, "gmlp_fwd")
    x1, mix0 = _mm([(ya, out_w[:SSM_INNER]), (yb, out_w[SSM_INNER:])], "nn", name="out_proj", resid=x0, gvec=mod0[2:3], keep=True)
    x2, ffn0, gathered_c = _ffn_fwd(x1, mod0, nfw[0:1], full["gate0"], full["up0"], full["down0"], "0", gather=gather_packs[2][1])
    gathered_weights(2, gathered_c)
    qkv_t, o_w = full["qkv_w"], full["o_w"]
    w_q, w_k, w_v_att = qkv_t[:D], qkv_t[D:D + LANES], qkv_t[D + LANES:]

    h1 = _norm_mod(x2, nmw[1:2], mod1[1:2], mod1[0:1], "mix_norm_1")
    qkv = _mm([(h1, qkv_t)], "nt", name="qkv_proj", bias=qkv_b_full, tn_pref=1280)
    att = _attn_fwd(qkv, bias, sinks, "attn_fwd")
    x3, mix1 = _mm([(att, o_w)], "nn", name="o_proj", bias=o_b_full, resid=x2, gvec=mod1[2:3], keep=True)
    x4, ffn1 = _ffn_fwd(x3, mod1, nfw[1:2], full["gate1"], full["up1"], full["down1"], "1")

    dx4, acc_f = _final_loss(x4, W["final_norm_w"].reshape(1, D), target, "final_loss")
    loss = lax.psum(acc_f[1, 0], ("x", "y", "c"))
    dx3, gf1 = _ffn_bwd(dx4, x3, ffn1, mod1, nfw[1:2], full["gate1"], full["up1"], full["down1"], "1")

    dmix1, acc_m1 = _gate_bwd(dx3, mix1, mod1[2:3], "mix_gate_bwd_1")
    datt = _mm([(dmix1, o_w)], "nt", name="o_proj_dx")
    d_o_w = _mm_tn(att, dmix1, name="o_proj_dw")
    dq, dk, dv, dbias, dsinks = _attn_bwd(qkv, datt, bias, sinks, "attn_bwd")
    d_table = _table_from_dbias(dbias.reshape(ATTN_HEADS, -1), onehot_t, "rel_table_grad").T
    d_qkv_t = jnp.concatenate([_mm_tn(dq, h1, name="qkv_dw_q"), _mm_tn(dk, h1, name="qkv_dw_k"), _mm_tn(dv, h1, name="qkv_dw_v")], axis=0)
    d_qkv_b = jnp.concatenate([_colsum_call(dq, "qkv_db_q"), _colsum_call(dk, "qkv_db_k"), _colsum_call(dv, "qkv_db_v")], axis=1)
    dh1 = _mm([(dq, w_q), (dk, w_k), (dv, w_v_att)], "nn", name="qkv_proj_dx")
    dx2, acc_n1 = _norm_mod_bwd(x2, dh1, dx3, nmw[1:2], mod1[1:2], "mix_norm_bwd_1")

    g_wire = dict(qkv_w=d_qkv_t, o_w=d_o_w, gate1=gf1["d_wg"], up1=gf1["d_wu"], down1=gf1["d_wd"])

    def packed_partials(group):
        return gather_packs[group][0].pack({key: g_wire[key].reshape(N_DEV, -1, D) for key in _GATHER_GROUPS[group]},
                                           F32, lead=(N_DEV,))

    from_chips = {}
    dx1, gf0, from_chips[2] = _ffn_bwd(dx2, x1, ffn0, mod0, nfw[0:1], full["gate0"], full["up0"], full["down0"], "0",
                                       exchange=packed_partials(2))
    g_wire.update(gate0=gf0["d_wg"], up0=gf0["d_wu"], down0=gf0["d_wd"])
    partials_ffn0 = packed_partials(1)

    dmix0, acc_m0 = _gate_bwd(dx1, mix0, mod0[2:3], "mix_gate_bwd_0")
    dya = _mm([(dmix0, out_w[:SSM_INNER])], "nt", name="out_proj_dx_a")
    dyb = _mm([(dmix0, out_w[SSM_INNER:])], "nt", name="out_proj_dx_b")
    d_out_w = jnp.concatenate([_mm_tn(ya, dmix0, name="out_proj_dw_a"), _mm_tn(yb, dmix0, name="out_proj_dw_b")], axis=0)
    du, dvg, d_ws, d_bs, acc_ln, theirs_ffn0 = _gmlp_bwd(pm, dyb, lnw, lnb, ws, bs_exp, "gmlp_bwd",
                                                         side=(_SiblingOps, partials_ffn0))
    pair_ffn0 = _pair_sum(partials_ffn0, theirs_ffn0, "grads_pair_sum_mix")
    dz, dcpre, ddtr, acc_ssd, d_ssm_nw, from_chips[1] = _ssd_bwd(
        cpre, dtr, pm, ypre, sprev, dya, dtb, alog, dskip, ssm_nw, head_ind, "ssd_bwd", side=(_ChipsOps, pair_ffn0))
    dxbc, acc_conv = _conv_bwd(dcpre, pm, conv_w_full, "conv_bwd")
    d_in_t = jnp.concatenate([
        _mm_tn(dz, h0, name="in_dw_z"), _mm_tn(dxbc, h0, name="in_dw_xbc"),
        _mm_tn(ddtr, h0, name="in_dw_dt")[:SSM_HEADS], _mm_tn(du, h0, name="in_dw_u"), _mm_tn(dvg, h0, name="in_dw_v")], axis=0)
    dh0 = _mm([(dz, w_z), (dxbc, w_xbc), (ddtr, w_dtp), (du, w_u), (dvg, w_v)], "nn", name="in_proj_dx")
    grad_x, acc_n0 = _norm_mod_bwd(x0, dh0, dx1, nmw[0:1], mod0[1:2], "mix_norm_bwd_0")

    g_wire.update(in_w=d_in_t, out_w=d_out_w)
    partials_mix = packed_partials(0)
    theirs_mix = _comm_call(_SiblingOps, partials_mix, "exchange_grads_sibling")
    from_chips[0] = _comm_call(_ChipsOps, _pair_sum(partials_mix, theirs_mix, "grads_pair_sum"), "exchange_grads_chips")
    g_mine = {}
    for group in range(len(_GATHER_GROUPS)):
        g_mine.update(gather_packs[group][0].unpack(_sum_parts(from_chips[group], f"grads_chip_sum_{group}")))
    g_nat = {key: _to_wire(g_mine[key], kind) for key, name, layer, kind in _BIG}
    pk = lambda S: big.pack({key: S[name][layer] for key, name, layer, kind in _BIG}, F32)
    res_big = [big.unpack(r) for r in _adamw(big.pack(g_nat, F32)[None], pk(W), pk(M), pk(V), "adamw_big")]

    d_mod = jnp.stack([
        jnp.stack([acc_n0[1], acc_n0[0], acc_m0[0], gf0["d_sh"], gf0["d_sc"], gf0["d_g"]]),
        jnp.stack([acc_n1[1], acc_n1[0], acc_m1[0], gf1["d_sh"], gf1["d_sc"], gf1["d_g"]])])
    g_small = dict(
        ada_b=d_mod.reshape(DEPTH, 6 * D),
        norm_mix_w=jnp.stack([acc_n0[2], acc_n1[2]]), norm_ffn_w=jnp.stack([gf0["d_nw"], gf1["d_nw"]]),
        conv_b=acc_conv[4:5], dt_bias=acc_ssd[0:1, :SSM_HEADS], a_log=acc_ssd[1:2, :SSM_HEADS], d_skip=acc_ssd[2:3, :SSM_HEADS],
        ssm_norm_w=d_ssm_nw, gmlp_ln_w=acc_ln[0:1], gmlp_ln_b=acc_ln[1:2], gmlp_ws=d_ws[None],
        gmlp_bs=d_bs[:, :GMLP_GROUPS].T[None], sinks=dsinks[:, :ATTN_HEADS], rel_table=d_table, final_norm_w=acc_f[0],
        conv_w=acc_conv[0:SSM_CONV], qkv_b=d_qkv_b, o_b=acc_m1[1:2])
    small = _Pack(D, 8, 8)
    for name in _REPLICATED:
        small.add(name, W[name].shape)
    small.add("conv_w", (SSM_CONV, CONV_DIM))
    small.add("qkv_b", (1, QKV_DIM))
    small.add("o_b", (1, D))
    parts_small = _all_gather(small.pack(g_small, F32), "gather_small_grads")
    zeros_tiny = dict(conv_w=jnp.zeros((SSM_CONV, CONV_DIM), F32), qkv_b=jnp.zeros((1, QKV_DIM), F32), o_b=jnp.zeros((1, D), F32))
    pks = lambda S: small.pack({**{name: S[name] for name in _REPLICATED}, **zeros_tiny}, F32)
    res_small = [small.unpack(r) for r in _adamw(parts_small, pks(W), pks(M), pks(V), "adamw_small")]
    g_small_sum = res_small[0]

    n_cw, n_qb, n_ob = W["conv_w"].shape[2], W["qkv_b"].shape[1], W["o_b"].shape[1]
    g_tiny = dict(conv_w=lax.dynamic_slice_in_dim(g_small_sum["conv_w"], me * n_cw, n_cw, axis=1)[None],
                  qkv_b=lax.dynamic_slice_in_dim(g_small_sum["qkv_b"], me * n_qb, n_qb, axis=1),
                  o_b=lax.dynamic_slice_in_dim(g_small_sum["o_b"], me * n_ob, n_ob, axis=1))
    tiny = _Pack(D, 8, 8)
    for name in _TINY_SHARDED:
        tiny.add(name, W[name].shape)
    pkt = lambda S: tiny.pack({name: S[name] for name in _TINY_SHARDED}, F32)
    res_tiny = [tiny.unpack(r) for r in _adamw(pkt(g_tiny)[None], pkt(W), pkt(M), pkt(V), "adamw_tiny")]

    dmod_all = parts_small[:, small.entries[0][1]:small.entries[0][1] + small.entries[0][2]].reshape(N_DEV, DEPTH, 6 * D)
    dmod_cols = jnp.transpose(lax.dynamic_slice_in_dim(dmod_all, me * ncol, ncol, axis=2), (1, 0, 2))
    g_ada_w = _ada_w_grad(cond, dmod_cols, "ada_w_grad")
    flat = lambda a: a.reshape(DEPTH * D, ncol)
    res_ada = [r.reshape(DEPTH, D, ncol) for r in _adamw(flat(g_ada_w)[None], flat(W["ada_w"]), flat(M["ada_w"]), flat(V["ada_w"]), "adamw_ada_w")]

    def result(kind_idx, name):
        if name == "ada_w":
            return res_ada[kind_idx]
        if name in _REPLICATED:
            return res_small[kind_idx][name]
        if name in _TINY_SHARDED:
            return res_tiny[kind_idx][name]
        pieces = [res_big[kind_idx][key] for key, nm, layer, kind in _BIG if nm == name]
        return jnp.stack(pieces)

    outs = [loss, grad_x[None]]
    for kind_idx in range(4):
        outs += [result(kind_idx, name) for name in _WEIGHTS]
    return tuple(outs)


def kernel(x, c, ada_w, ada_b, norm_mix_w, norm_ffn_w, in_w_even, conv_w, conv_b, dt_bias, a_log, d_skip, ssm_norm_w, gmlp_ln_w, gmlp_ln_b, gmlp_ws, gmlp_bs, out_w_even, qkv_w, qkv_b, o_w, o_b, sinks, rel_table, ffn_gate_w, ffn_up_w, ffn_down_w, final_norm_w, loss_target, m_ada_w, m_ada_b, m_norm_mix_w, m_norm_ffn_w, m_in_w_even, m_conv_w, m_conv_b, m_dt_bias, m_a_log, m_d_skip, m_ssm_norm_w, m_gmlp_ln_w, m_gmlp_ln_b, m_gmlp_ws, m_gmlp_bs, m_out_w_even, m_qkv_w, m_qkv_b, m_o_w, m_o_b, m_sinks, m_rel_table, m_ffn_gate_w, m_ffn_up_w, m_ffn_down_w, m_final_norm_w, v_ada_w, v_ada_b, v_norm_mix_w, v_norm_ffn_w, v_in_w_even, v_conv_w, v_conv_b, v_dt_bias, v_a_log, v_d_skip, v_ssm_norm_w, v_gmlp_ln_w, v_gmlp_ln_b, v_gmlp_ws, v_gmlp_bs, v_out_w_even, v_qkv_w, v_qkv_b, v_o_w, v_o_b, v_sinks, v_rel_table, v_ffn_gate_w, v_ffn_up_w, v_ffn_down_w, v_final_norm_w):
    args = locals()
    W = {n: args[n] for n in _WEIGHTS}
    M = {n: args["m_" + n] for n in _WEIGHTS}
    V = {n: args["v_" + n] for n in _WEIGHTS}
    return _step(x, c, loss_target, W, M, V)
```

```python
import functools
import math

import numpy as np
import jax
import jax.numpy as jnp
from jax import lax
from jax.experimental import pallas as pl
from jax.experimental.pallas import tpu as pltpu

F32 = jnp.float32
BF16 = jnp.bfloat16
HIGHEST = lax.Precision.HIGHEST
MESH = pl.DeviceIdType.MESH

N_DEV = 8
D = 1024
DEPTH = 2
SSM_HEADS = 16
SSM_HEAD_DIM = 64
SSM_INNER = 1024
SSM_GROUPS = 2
SSM_STATE = 128
SSM_CONV = 4
CHUNK = 128
CONV_DIM = SSM_INNER + 2 * SSM_GROUPS * SSM_STATE
GMLP_GROUPS = 8
GMLP_INNER = 1024
IN_EVEN = 4624
ATTN_HEADS = 16
ATTN_KV = 2
ATTN_DH = 64
QKV_DIM = 1280
REL_BUCKETS = 32
REL_MAX_DIST = 128
FFN = 2816
EPS = 1e-6
NEG_INF = -1e30
LANES = 128

ADAM_LR = 0.001
ADAM_B1 = 0.9
ADAM_B2 = 0.999
ADAM_EPS = 1e-08
ADAM_WD = 0.01
ADAM_STEP = 10

VMEM_LIMIT_BYTES = 56 * 1024 * 1024
ROW_TILE = 512


def _pcall(body, *, name, out_shape, grid=(), in_specs=None, out_specs=None, scratch=(), sem=None):
    params = dict(vmem_limit_bytes=VMEM_LIMIT_BYTES)
    if sem is not None:
        params["dimension_semantics"] = sem
    specs = {} if in_specs is None else dict(in_specs=in_specs, out_specs=out_specs)
    return pl.pallas_call(
        body, name=name, out_shape=out_shape, grid=grid, **specs,
        scratch_shapes=list(scratch), compiler_params=pltpu.CompilerParams(**params))


def _call(body, args, side=None, *, name, out_shape, grid, in_specs, out_specs, scratch=(), sem=None):
    if side is None:
        return _pcall(body, name=name, out_shape=out_shape, grid=grid, in_specs=in_specs, out_specs=out_specs,
                      scratch=scratch, sem=sem)(*args)
    ops_cls, x = side
    n_in, n_out, n_scr = len(in_specs), len(out_shape), len(scratch)
    steps = int(np.prod(grid))
    hbm = pl.BlockSpec(memory_space=pl.ANY)

    def wrapped(*refs):
        ins, x_ref = refs[:n_in], refs[n_in]
        outs, r_ref = refs[n_in + 1:n_in + 1 + n_out], refs[n_in + 1 + n_out]
        scr, sems = refs[n_in + 2 + n_out:n_in + 2 + n_out + n_scr], refs[n_in + 2 + n_out + n_scr:]
        ops = ops_cls(x_ref, r_ref, *sems)
        step = pl.program_id(0)
        for axis in range(1, len(grid)):
            step = step * grid[axis] + pl.program_id(axis)
        pl.when(step == 0)(ops.start)
        body(*ins, *outs, *scr)
        pl.when(step == (3 * steps) // 4)(ops.forward)
        pl.when(step == steps - 1)(ops.finish)

    return _pcall(
        wrapped, name=name, out_shape=list(out_shape) + [ops_cls.result(x)], grid=grid,
        in_specs=list(in_specs) + [hbm], out_specs=list(out_specs) + [hbm],
        scratch=list(scratch) + ops_cls.scratch(), sem=("arbitrary",) * len(grid))(*args, x)


def _tile(n, pref):
    if n <= pref:
        return n
    best = None
    for t in range(LANES, pref + 1, LANES):
        if n % t == 0:
            best = t
    assert best is not None, (n, pref)
    return best


def _rows(T):
    return min(ROW_TILE, T)


def _sds(shape, dtype=F32):
    return jax.ShapeDtypeStruct(shape, dtype)


def _row_spec(tm, c, col=0):
    return pl.BlockSpec((tm, c), lambda i, col=col: (i, col))


def _vec_spec(c, r=1):
    return pl.BlockSpec((r, c), lambda i: (0, 0))


def _sigmoid(x):
    return jax.nn.sigmoid(x)


def _silu(x):
    return x * _sigmoid(x)


def _dsilu(x):
    s = _sigmoid(x)
    return s * (1.0 + x * (1.0 - s))


def _gelu(x):
    return 0.5 * x * (1.0 + lax.erf(x * 0.7071067811865476))


def _dgelu(x):
    return 0.5 * (1.0 + lax.erf(x * 0.7071067811865476)) + x * jnp.exp(-0.5 * x * x) * 0.3989422804014327


def _dot(a, b, dims, precision=None):
    return lax.dot_general(a, b, (dims, ((), ())), precision=precision, preferred_element_type=F32)


def _nn(a, b, precision=None):
    return _dot(a, b, ((1,), (0,)), precision)


def _nt(a, b, precision=None):
    return _dot(a, b, ((1,), (1,)), precision)


def _tn(a, b, precision=None):
    return _dot(a, b, ((0,), (0,)), precision)


def _bf(x):
    return x.astype(BF16)


def _colsum(x):
    return jnp.sum(x, axis=0, keepdims=True)


def _rowsum(x):
    return jnp.sum(x, axis=1, keepdims=True)


def _allsum(x):
    return _colsum(_rowsum(x))


def _comm_call(ops_cls, x, name, from_vmem=False):
    def body(x_ref, out_ref, *sems):
        ops = ops_cls(x_ref, out_ref, *sems)
        ops.start()
        ops.forward()
        ops.finish()

    return pl.pallas_call(
        body, name=name, out_shape=ops_cls.result(x),
        in_specs=[pl.BlockSpec(memory_space=pltpu.VMEM if from_vmem else pl.ANY)],
        out_specs=pl.BlockSpec(memory_space=pl.ANY), scratch_shapes=ops_cls.scratch(),
    )(x)


def _all_gather(x, name):
    return _comm_call(_GatherOps, x, name, from_vmem=True)


class _GatherOps:
    def __init__(self, x_ref, out_ref, send_sems, recv_sems, local_sem):
        self.x_ref, self.out_ref = x_ref, out_ref
        self.send_sems, self.recv_sems, self.local_sem = send_sems, recv_sems, local_sem
        mx, my, mc = lax.axis_index("x"), lax.axis_index("y"), lax.axis_index("c")
        self.mc = mc
        self.me, self.sibling = (mx, my, mc), (mx, my, 1 - mc)
        self.chips = [(1 - mx, my), (mx, 1 - my), (1 - mx, 1 - my)]

    @staticmethod
    def result(x):
        return _sds((N_DEV,) + x.shape, x.dtype)

    @staticmethod
    def scratch():
        return [pltpu.SemaphoreType.DMA((7,)), pltpu.SemaphoreType.DMA((7,)), pltpu.SemaphoreType.DMA(())]

    def _slot(self, px, py, pc):
        return self.out_ref.at[4 * px + 2 * py + pc]

    def _copy(self, k, block, to, own=False):
        return pltpu.make_async_remote_copy(
            src_ref=self.x_ref if own else self._slot(*block), dst_ref=self._slot(*block),
            send_sem=self.send_sems.at[k], recv_sem=self.recv_sems.at[k], device_id=to, device_id_type=MESH)

    def _mine(self):
        return pltpu.make_async_copy(self.x_ref, self._slot(*self.me), self.local_sem)

    def _first(self):
        return [self._copy(0, self.me, self.sibling, own=True)] + [
            self._copy(1 + j, self.me, (*chip, self.mc), own=True) for j, chip in enumerate(self.chips)]

    def _passed(self):
        return [self._copy(4 + j, (*chip, self.mc), self.sibling) for j, chip in enumerate(self.chips)]

    def start(self):
        self._mine().start()
        for cp in self._first():
            cp.start()

    def forward(self):
        passed = self._passed()
        for j, chip in enumerate(self.chips):
            self._copy(1 + j, (*chip, self.mc), self.me).wait_recv()
            passed[j].start()

    def finish(self):
        self._copy(0, self.sibling, self.me).wait_recv()
        for j, chip in enumerate(self.chips):
            self._copy(4 + j, (*chip, 1 - self.mc), self.me).wait_recv()
        for cp in self._first() + self._passed():
            cp.wait_send()
        self._mine().wait()


N_CHIP = 4


class _SiblingOps:
    def __init__(self, p_ref, theirs_ref, send_sems, recv_sems):
        self.p_ref, self.theirs_ref, self.send_sems, self.recv_sems = p_ref, theirs_ref, send_sems, recv_sems

    @staticmethod
    def result(p):
        return _sds((N_CHIP,) + p.shape[1:], p.dtype)

    @staticmethod
    def scratch():
        return [pltpu.SemaphoreType.DMA((N_CHIP,))] * 2

    def _copies(self):
        mx, my, mc = lax.axis_index("x"), lax.axis_index("y"), lax.axis_index("c")
        return [pltpu.make_async_remote_copy(
            src_ref=self.p_ref.at[2 * chip + 1 - mc], dst_ref=self.theirs_ref.at[chip],
            send_sem=self.send_sems.at[chip], recv_sem=self.recv_sems.at[chip],
            device_id=(mx, my, 1 - mc), device_id_type=MESH) for chip in range(N_CHIP)]

    def start(self):
        for cp in self._copies():
            cp.start()

    def forward(self):
        pass

    def finish(self):
        for cp in self._copies():
            cp.wait()


class _ChipsOps:
    def __init__(self, q_ref, out_ref, send_sems, recv_sems, local_sem):
        self.q_ref, self.out_ref = q_ref, out_ref
        self.send_sems, self.recv_sems, self.local_sem = send_sems, recv_sems, local_sem

    @staticmethod
    def result(q):
        return _sds(q.shape, q.dtype)

    @staticmethod
    def scratch():
        return [pltpu.SemaphoreType.DMA((N_CHIP - 1,)), pltpu.SemaphoreType.DMA((N_CHIP - 1,)), pltpu.SemaphoreType.DMA(())]

    def _copies(self):
        mx, my, mc = lax.axis_index("x"), lax.axis_index("y"), lax.axis_index("c")
        me = 2 * mx + my
        local = pltpu.make_async_copy(self.q_ref.at[me], self.out_ref.at[me], self.local_sem)
        remote = []
        for r in range(1, N_CHIP):
            px = 1 - mx if r & 2 else mx
            py = 1 - my if r & 1 else my
            remote.append(pltpu.make_async_remote_copy(
                src_ref=self.q_ref.at[2 * px + py], dst_ref=self.out_ref.at[me],
                send_sem=self.send_sems.at[r - 1], recv_sem=self.recv_sems.at[r - 1],
                device_id=(px, py, mc), device_id_type=MESH))
        return local, remote

    def start(self):
        local, remote = self._copies()
        local.start()
        for cp in remote:
            cp.start()

    def forward(self):
        pass

    def finish(self):
        local, remote = self._copies()
        for cp in remote:
            cp.wait()
        local.wait()


def _pair_sum(p, theirs, name):
    n, R, C = theirs.shape
    tr = _tile_rows(R, 256)

    def body(p_ref, t_ref, o_ref):
        mc = lax.axis_index("c")
        o_ref[0] = (p_ref[0, mc] + t_ref[0]).astype(BF16)

    blk = pl.BlockSpec((1, tr, C), lambda s, i: (s, i, 0))
    return _pcall(body, name=name, out_shape=_sds((n, R, C), BF16), grid=(n, R // tr),
                  in_specs=[pl.BlockSpec((1, 2, tr, C), lambda s, i: (s, 0, i, 0)), blk],
                  out_specs=blk, sem=("parallel", "parallel"))(p.reshape(n, 2, R, C), theirs)


def _sum_parts(parts, name):
    P, R, C = parts.shape
    tr = _tile_rows(R, 256)

    def body(p_ref, o_ref):
        g = p_ref[0].astype(F32)
        for k in range(1, P):
            g = g + p_ref[k].astype(F32)
        o_ref[...] = g

    return _pcall(body, name=name, out_shape=_sds((R, C)), grid=(R // tr,),
                  in_specs=[pl.BlockSpec((P, tr, C), lambda i: (0, i, 0))],
                  out_specs=pl.BlockSpec((tr, C), lambda i: (i, 0)), sem=("parallel",))(parts)


def _adamw(parts, w, m, v, name):
    P, R, C = parts.shape
    tr = R if R <= 256 else _tile_rows(R, 256)

    def body(p_ref, w_ref, m_ref, v_ref, g_ref, d_ref, nm_ref, nv_ref):
        g = p_ref[0]
        for k in range(1, P):
            g = g + p_ref[k]
        nm = ADAM_B1 * m_ref[...] + (1.0 - ADAM_B1) * g
        nv = ADAM_B2 * v_ref[...] + (1.0 - ADAM_B2) * (g * g)
        m_hat = nm / (1.0 - ADAM_B1 ** ADAM_STEP)
        v_hat = nv / (1.0 - ADAM_B2 ** ADAM_STEP)
        g_ref[...] = g
        d_ref[...] = -ADAM_LR * (m_hat / (jnp.sqrt(v_hat) + ADAM_EPS) + ADAM_WD * w_ref[...])
        nm_ref[...] = nm
        nv_ref[...] = nv

    blk = pl.BlockSpec((tr, C), lambda i: (i, 0))
    return _pcall(
        body, name=name, out_shape=[_sds((R, C))] * 4, grid=(R // tr,),
        in_specs=[pl.BlockSpec((P, tr, C), lambda i: (0, i, 0)), blk, blk, blk],
        out_specs=[blk] * 4, sem=("parallel",))(parts, w, m, v)


def _tile_rows(n, pref):
    best = None
    for t in range(8, pref + 1, 8):
        if n % t == 0:
            best = t
    assert best is not None, (n, pref)
    return best


def _mm(pairs, mode, *, name, out_dtype=F32, bias=None, resid=None, gvec=None, keep=False, tn_pref=1024, side=None):
    M = pairs[0][0].shape[0]
    N = pairs[0][1].shape[1] if mode == "nn" else pairs[0][1].shape[0]
    tm, tn = _rows(M), _tile(N, tn_pref)
    n_pairs = len(pairs)
    has_bias, has_res = bias is not None, resid is not None

    def body(*refs):
        ab = refs[:2 * n_pairs]
        pos = 2 * n_pairs
        b_ref = refs[pos] if has_bias else None
        pos += has_bias
        r_ref, g_ref = (refs[pos], refs[pos + 1]) if has_res else (None, None)
        pos += 2 * has_res
        outs = refs[pos:]
        acc = None
        for p in range(n_pairs):
            a, b = _bf(ab[2 * p][...]), _bf(ab[2 * p + 1][...])
            d = _nn(a, b) if mode == "nn" else _nt(a, b)
            acc = d if acc is None else acc + d
        if has_bias:
            acc = acc + b_ref[...]
        if has_res:
            outs[0][...] = (r_ref[...] + g_ref[...] * acc).astype(outs[0].dtype)
            if keep:
                outs[1][...] = acc
        else:
            outs[0][...] = acc.astype(outs[0].dtype)

    in_specs, args = [], []
    for a, b in pairs:
        K = a.shape[1]
        in_specs.append(pl.BlockSpec((tm, K), lambda j, i: (i, 0)))
        if mode == "nn":
            in_specs.append(pl.BlockSpec((K, tn), lambda j, i: (0, j)))
        else:
            in_specs.append(pl.BlockSpec((tn, K), lambda j, i: (j, 0)))
        args += [a, b]
    if has_bias:
        in_specs.append(pl.BlockSpec((1, tn), lambda j, i: (0, j)))
        args.append(bias)
    if has_res:
        in_specs.append(pl.BlockSpec((tm, tn), lambda j, i: (i, j)))
        in_specs.append(pl.BlockSpec((1, tn), lambda j, i: (0, j)))
        args += [resid, gvec]
    o_spec = pl.BlockSpec((tm, tn), lambda j, i: (i, j))
    n_out = 2 if (has_res and keep) else 1
    out_shape = [_sds((M, N), out_dtype)] + ([_sds((M, N), F32)] if n_out == 2 else [])
    res = _call(body, args, side, name=name, out_shape=out_shape, grid=(N // tn, M // tm), in_specs=in_specs,
                out_specs=[o_spec] * n_out, sem=("parallel", "parallel"))
    out = tuple(res[:n_out]) if n_out == 2 else res[0]
    return out if side is None else (out, res[n_out])


def _mm_tn(a, b, *, name, tm_pref=1408, tn_pref=1536):
    K, M = a.shape
    N = b.shape[1]
    tm, tn = _tile(M, tm_pref), _tile(N, tn_pref)
    tk = K if K <= 2 * ROW_TILE else 2 * ROW_TILE

    def body(a_ref, b_ref, o_ref):
        @pl.when(pl.program_id(2) == 0)
        def _():
            o_ref[...] = jnp.zeros_like(o_ref)
        o_ref[...] += _tn(_bf(a_ref[...]), _bf(b_ref[...]))

    return _pcall(
        body, name=name, out_shape=_sds((M, N)), grid=(M // tm, N // tn, K // tk),
        in_specs=[pl.BlockSpec((tk, tm), lambda i, j, k: (k, i)), pl.BlockSpec((tk, tn), lambda i, j, k: (k, j))],
        out_specs=pl.BlockSpec((tm, tn), lambda i, j, k: (i, j)),
        sem=("parallel", "parallel", "arbitrary"))(a, b)


def _mm_swiglu(h, wg_t, wu_t, name, side=None):
    M, K = h.shape
    N = wg_t.shape[0]
    tm, tn = _rows(M), _tile(N, 1408)

    def body(h_ref, wg_ref, wu_ref, gate_ref, up_ref, act_ref):
        hv = _bf(h_ref[...])
        gate = _nt(hv, wg_ref[...])
        up = _nt(hv, wu_ref[...])
        gate_ref[...] = gate.astype(BF16)
        up_ref[...] = up.astype(BF16)
        act_ref[...] = (_silu(gate) * up).astype(BF16)

    w_spec = pl.BlockSpec((tn, K), lambda j, i: (j, 0))
    o_spec = pl.BlockSpec((tm, tn), lambda j, i: (i, j))
    return _call(body, (h, wg_t, wu_t), side, name=name,
                 out_shape=[_sds((M, N), BF16)] * 3, grid=(N // tn, M // tm),
                 in_specs=[pl.BlockSpec((tm, K), lambda j, i: (i, 0)), w_spec, w_spec], out_specs=[o_spec] * 3,
                 sem=("parallel", "parallel"))


def _mm_swiglu_bwd(dout, wd, gate, up, name, side=None):
    M, K = dout.shape
    N = wd.shape[0]
    tm, tn = _rows(M), _tile(N, 1408)

    def body(d_ref, wd_ref, gate_ref, up_ref, dg_ref, du_ref):
        dact = _nt(_bf(d_ref[...]), wd_ref[...])
        g = gate_ref[...].astype(F32)
        dg_ref[...] = (dact * up_ref[...].astype(F32) * _dsilu(g)).astype(BF16)
        du_ref[...] = (dact * _silu(g)).astype(BF16)

    t_spec = pl.BlockSpec((tm, tn), lambda j, i: (i, j))
    return _call(
        body, (dout, wd, gate, up), side, name=name, out_shape=[_sds((M, N), BF16)] * 2, grid=(N // tn, M // tm),
        in_specs=[pl.BlockSpec((tm, K), lambda j, i: (i, 0)), pl.BlockSpec((tn, K), lambda j, i: (j, 0)), t_spec, t_spec],
        out_specs=[t_spec] * 2, sem=("parallel", "parallel"))


def _norm_mod(x, w, sc, sh, name):
    T = x.shape[0]
    tm = _rows(T)

    def body(x_ref, w_ref, sc_ref, sh_ref, o_ref):
        xv = x_ref[...]
        r = lax.rsqrt(jnp.mean(xv * xv, axis=-1, keepdims=True) + EPS)
        o_ref[...] = ((xv * r * w_ref[...]) * (1.0 + sc_ref[...]) + sh_ref[...]).astype(BF16)

    return _pcall(body, name=name, out_shape=_sds((T, D), BF16), grid=(T // tm,),
                  in_specs=[_row_spec(tm, D), _vec_spec(D), _vec_spec(D), _vec_spec(D)],
                  out_specs=_row_spec(tm, D), sem=("parallel",))(x, w, sc, sh)


def _norm_mod_bwd(x, dh, dres, w, sc, name):
    T = x.shape[0]
    tm = _rows(T)

    def body(x_ref, dh_ref, dres_ref, w_ref, sc_ref, dx_ref, acc_ref):
        @pl.when(pl.program_id(0) == 0)
        def _():
            acc_ref[...] = jnp.zeros_like(acc_ref)
        xv, dh_v, wv = x_ref[...], dh_ref[...], w_ref[...]
        r = lax.rsqrt(jnp.mean(xv * xv, axis=-1, keepdims=True) + EPS)
        n = xv * r
        dnw = dh_v * (1.0 + sc_ref[...])
        dn = dnw * wv
        dx_ref[...] = dres_ref[...] + r * (dn - n * jnp.mean(dn * n, axis=-1, keepdims=True))
        acc_ref[0:1, :] += _colsum(dh_v * (n * wv))
        acc_ref[1:2, :] += _colsum(dh_v)
        acc_ref[2:3, :] += _colsum(dnw * n)

    return _pcall(body, name=name, out_shape=[_sds((T, D)), _sds((8, D))], grid=(T // tm,),
                  in_specs=[_row_spec(tm, D), _row_spec(tm, D), _row_spec(tm, D), _vec_spec(D), _vec_spec(D)],
                  out_specs=[_row_spec(tm, D), _vec_spec(D, 8)], sem=("arbitrary",))(x, dh, dres, w, sc)


def _gate_bwd(dx, branch, g, name):
    T = dx.shape[0]
    tm = _rows(T)

    def body(dx_ref, br_ref, g_ref, db_ref, acc_ref):
        @pl.when(pl.program_id(0) == 0)
        def _():
            acc_ref[...] = jnp.zeros_like(acc_ref)
        dxv = dx_ref[...]
        db = g_ref[...] * dxv
        db_ref[...] = db.astype(BF16)
        acc_ref[0:1, :] += _colsum(dxv * br_ref[...])
        acc_ref[1:2, :] += _colsum(db)

    return _pcall(body, name=name, out_shape=[_sds((T, D), BF16), _sds((8, D))], grid=(T // tm,),
                  in_specs=[_row_spec(tm, D), _row_spec(tm, D), _vec_spec(D)],
                  out_specs=[_row_spec(tm, D), _vec_spec(D, 8)], sem=("arbitrary",))(dx, branch, g)


def _final_loss(x, wf, target, name):
    T = x.shape[0]
    tm = _rows(T)

    def body(x_ref, w_ref, t_ref, dx_ref, acc_ref):
        @pl.when(pl.program_id(0) == 0)
        def _():
            acc_ref[...] = jnp.zeros_like(acc_ref)
        xv, wv = x_ref[...], w_ref[...]
        r = lax.rsqrt(jnp.mean(xv * xv, axis=-1, keepdims=True) + EPS)
        n = xv * r
        err = n * wv - t_ref[...]
        dy = err * (1.0 / D)
        dn = dy * wv
        dx_ref[...] = r * (dn - n * jnp.mean(dn * n, axis=-1, keepdims=True))
        acc_ref[0:1, :] += _colsum(dy * n)
        acc_ref[1:2, :] += jnp.broadcast_to(_allsum(err * err) * (0.5 / D), (1, D))

    return _pcall(body, name=name, out_shape=[_sds((T, D)), _sds((8, D))], grid=(T // tm,),
                  in_specs=[_row_spec(tm, D), _vec_spec(D), _row_spec(tm, D)],
                  out_specs=[_row_spec(tm, D), _vec_spec(D, 8)], sem=("arbitrary",))(x, wf, target)


def _colsum_call(x, name):
    T, C = x.shape
    tm = _rows(T)

    def body(x_ref, o_ref):
        @pl.when(pl.program_id(0) == 0)
        def _():
            o_ref[...] = jnp.zeros_like(o_ref)
        o_ref[...] += _colsum(x_ref[...].astype(F32))

    return _pcall(body, name=name, out_shape=_sds((1, C)), grid=(T // tm,), in_specs=[_row_spec(tm, C)],
                  out_specs=_vec_spec(C), sem=("arbitrary",))(x)


def _mod_matmul(c_all, ada_w, name):
    n = ada_w.shape[2]

    def body(c_ref, w_ref, cond_ref, o_ref):
        cond = _silu(c_ref[...])
        cond_ref[...] = cond
        o_ref[0] = _nn(cond, w_ref[0])

    return _pcall(body, name=name, out_shape=[_sds((N_DEV, D)), _sds((DEPTH, N_DEV, n))], grid=(DEPTH,),
                  in_specs=[pl.BlockSpec((N_DEV, D), lambda l: (0, 0)), pl.BlockSpec((1, D, n), lambda l: (l, 0, 0))],
                  out_specs=[pl.BlockSpec((N_DEV, D), lambda l: (0, 0)), pl.BlockSpec((1, N_DEV, n), lambda l: (l, 0, 0))],
                  sem=("arbitrary",))(c_all, ada_w)


def _add_rows(a, b, name):
    def body(a_ref, b_ref, o_ref):
        o_ref[...] = a_ref[...] + b_ref[...]

    return _pcall(body, name=name, out_shape=_sds(a.shape))(a, b)


def _ada_w_grad(cond, dmod_cols, name):
    n = dmod_cols.shape[2]

    def body(c_ref, d_ref, o_ref):
        o_ref[0] = _tn(c_ref[...], d_ref[0])

    return _pcall(body, name=name, out_shape=_sds((DEPTH, D, n)), grid=(DEPTH,),
                  in_specs=[pl.BlockSpec((N_DEV, D), lambda l: (0, 0)), pl.BlockSpec((1, N_DEV, n), lambda l: (l, 0, 0))],
                  out_specs=pl.BlockSpec((1, D, n), lambda l: (l, 0, 0)), sem=("parallel",))(cond, dmod_cols)


def _conv_fwd(pm, conv_w, conv_b, name):
    T = pm.shape[0]
    tm = _rows(T)
    C = CONV_DIM

    def body(x_ref, prev_ref, w_ref, b_ref, o_ref):
        cur = x_ref[...]
        prev = jnp.where(pl.program_id(0) > 0, prev_ref[...], 0.0)
        cur8 = cur[0:8]
        row8 = lax.broadcasted_iota(jnp.int32, (8, C), 0)
        full = w_ref[3:4, :] * cur
        head = w_ref[3:4, :] * cur8
        for k in range(1, SSM_CONV):
            wk = w_ref[3 - k:4 - k, :]
            full = full + wk * pltpu.roll(cur, k, 0)
            head = head + wk * jnp.where(row8 < k, pltpu.roll(prev, k, 0), pltpu.roll(cur8, k, 0))
        o_ref[...] = full + b_ref[...]
        o_ref[0:8, :] = head + b_ref[...]

    return _pcall(
        body, name=name, out_shape=_sds((T, C)), grid=(T // tm,),
        in_specs=[pl.BlockSpec((tm, C), lambda i: (i, 2)),
                  pl.BlockSpec((8, C), lambda i: (jnp.maximum(i * (tm // 8) - 1, 0), 2)),
                  _vec_spec(C, SSM_CONV), _vec_spec(C)],
        out_specs=_row_spec(tm, C), sem=("parallel",))(pm, pm, conv_w, conv_b)


def _conv_bwd(dc, pm, conv_w, name):
    T = dc.shape[0]
    tm = _rows(T)
    C = CONV_DIM
    nt = T // tm

    def body(dc_ref, nxt_ref, x_ref, prev_ref, w_ref, dx_ref, acc_ref):
        i = pl.program_id(0)

        @pl.when(i == 0)
        def _():
            acc_ref[...] = jnp.zeros_like(acc_ref)
        dcv = dc_ref[...]
        nxt = jnp.where(i < nt - 1, nxt_ref[...], 0.0)
        xc = x_ref[...]
        prev = jnp.where(i > 0, prev_ref[...], 0.0)
        dc8h, dc8t, x8 = dcv[0:8], dcv[tm - 8:tm], xc[0:8]
        row8 = lax.broadcasted_iota(jnp.int32, (8, C), 0)
        full = w_ref[3:4, :] * dcv
        tail = w_ref[3:4, :] * dc8t
        acc_ref[3:4, :] += _colsum(dcv * xc)
        for k in range(1, SSM_CONV):
            wk = w_ref[3 - k:4 - k, :]
            full = full + wk * pltpu.roll(dcv, tm - k, 0)
            tail = tail + wk * jnp.where(row8 + k >= 8, pltpu.roll(nxt, 8 - k, 0), pltpu.roll(dc8t, 8 - k, 0))
            xs_head = jnp.where(row8 < k, pltpu.roll(prev, k, 0), pltpu.roll(x8, k, 0))
            prod = dcv * pltpu.roll(xc, k, 0)
            acc_ref[3 - k:4 - k, :] += _colsum(prod) - _colsum(prod[0:8]) + _colsum(dc8h * xs_head)
        acc_ref[4:5, :] += _colsum(dcv)
        dx_ref[...] = full
        dx_ref[tm - 8:tm, :] = tail

    return _pcall(
        body, name=name, out_shape=[_sds((T, C)), _sds((8, C))], grid=(nt,),
        in_specs=[_row_spec(tm, C),
                  pl.BlockSpec((8, C), lambda i: (jnp.minimum((i + 1) * (tm // 8), T // 8 - 1), 0)),
                  pl.BlockSpec((tm, C), lambda i: (i, 2)),
                  pl.BlockSpec((8, C), lambda i: (jnp.maximum(i * (tm // 8) - 1, 0), 2)),
                  _vec_spec(C, SSM_CONV)],
        out_specs=[_row_spec(tm, C), _vec_spec(C, 8)], sem=("arbitrary",))(dc, dc, pm, pm, conv_w)


def _ssd_prologue(cpre, dtr, dtb, alog):
    L = CHUNK
    xc = _silu(cpre)
    pre = dtr + dtb
    dt = jnp.maximum(pre, 0.0) + jnp.log1p(jnp.exp(-jnp.abs(pre)))
    a = -jnp.exp(alog)
    la = dt * a
    row = lax.broadcasted_iota(jnp.int32, (L, L), 0)
    col = lax.broadcasted_iota(jnp.int32, (L, L), 1)
    causal = row >= col
    tri = causal.astype(F32)
    lc = _nn(tri, la, HIGHEST)
    return xc, pre, dt, a, causal, tri, lc, row, col


def _head_indicator():
    m = np.zeros((LANES, SSM_INNER), np.float32)
    for h in range(SSM_HEADS):
        m[h, h * SSM_HEAD_DIM:(h + 1) * SSM_HEAD_DIM] = 1.0
    return jnp.asarray(m, dtype=BF16)


def _split_dot(x, ind, dims):
    hi = x.astype(BF16)
    lo = (x - hi.astype(F32)).astype(BF16)
    return _dot(hi, ind, dims) + _dot(lo, ind, dims)


def _expand(x16, ind):
    return _split_dot(x16, ind, ((1,), (0,)))


def _headsum(x, ind):
    return _split_dot(x, ind, ((1,), (1,)))


def _ssd_fwd(cpre, dtr, pm, dtb, alog, dskip, normw, ind, name, side=None):
    T = cpre.shape[0]
    nc = T // CHUNK
    L, P, H, HPG, N = CHUNK, SSM_HEAD_DIM, SSM_HEADS, SSM_HEADS // SSM_GROUPS, SSM_STATE
    half = SSM_INNER // SSM_GROUPS

    def body(cp_ref, dtr_ref, z_ref, dtb_ref, alog_ref, dskip_ref, nw_ref, ind_ref, ya_ref, y_ref, sp_ref, st_ref):
        @pl.when(pl.program_id(0) == 0)
        def _():
            st_ref[...] = jnp.zeros_like(st_ref)
        xc, _, dt, _, causal, _, lc, _, _ = _ssd_prologue(cp_ref[...], dtr_ref[...], dtb_ref[...], alog_ref[...])
        lct = lc.T
        ind = ind_ref[...]
        llast = lc[L - 1:L, :]
        xs = xc[:, :SSM_INNER]
        xd = xs * _expand(dt, ind)
        ex = _expand(jnp.exp(lc), ind)
        xd_end = _bf(xd * _expand(jnp.exp(llast - lc), ind))
        cdx = _expand(jnp.broadcast_to(jnp.exp(llast), (8, LANES)), ind)[0:1]
        xdb = _bf(xd)
        sp_ref[0] = st_ref[...]
        for g in range(SSM_GROUPS):
            sl = slice(g * half, (g + 1) * half)
            bm = _bf(xc[:, SSM_INNER + g * N:SSM_INNER + (g + 1) * N])
            cm = _bf(xc[:, SSM_INNER + (SSM_GROUPS + g) * N:SSM_INNER + (SSM_GROUPS + g + 1) * N])
            cb = _nt(cm, bm)
            st = st_ref[g]
            y_ref[:, sl] = ex[:, sl] * _nn(cm, _bf(st)) + dskip_ref[:, sl] * xs[:, sl]
            st_ref[g] = st * cdx[:, sl] + _tn(bm, xd_end[:, sl])
            for j in range(HPG):
                h = g * HPG + j
                decay = jnp.where(causal, jnp.exp(jnp.where(causal, lc[:, h:h + 1] - lct[h:h + 1, :], 0.0)), 0.0)
                y_ref[:, h * P:(h + 1) * P] += _nn(_bf(cb * decay), xdb[:, h * P:(h + 1) * P])
        y2 = y_ref[...] * _silu(z_ref[...])
        for g in range(SSM_GROUPS):
            yg = y2[:, g * half:(g + 1) * half]
            r = lax.rsqrt(jnp.mean(yg * yg, axis=-1, keepdims=True) + EPS)
            ya_ref[:, g * half:(g + 1) * half] = (yg * r * nw_ref[:, g * half:(g + 1) * half]).astype(BF16)

    return _call(
        body, (cpre, dtr, pm, dtb, alog, dskip, normw, ind), side, name=name,
        out_shape=[_sds((T, SSM_INNER), BF16), _sds((T, SSM_INNER)), _sds((nc, SSM_GROUPS, N, half))], grid=(nc,),
        in_specs=[_row_spec(L, CONV_DIM), _row_spec(L, LANES), _row_spec(L, SSM_INNER, 0),
                  _vec_spec(LANES), _vec_spec(LANES), _vec_spec(SSM_INNER), _vec_spec(SSM_INNER), _vec_spec(SSM_INNER, LANES)],
        out_specs=[_row_spec(L, SSM_INNER), _row_spec(L, SSM_INNER),
                   pl.BlockSpec((1, SSM_GROUPS, N, half), lambda i: (i, 0, 0, 0))],
        scratch=[pltpu.VMEM((SSM_GROUPS, N, half), F32)], sem=("arbitrary",))


def _ssd_bwd(cpre, dtr, pm, ypre, sprev, dya, dtb, alog, dskip, normw, ind, name, side=None):
    T = cpre.shape[0]
    nc = T // CHUNK
    L, P, H, HPG, N = CHUNK, SSM_HEAD_DIM, SSM_HEADS, SSM_HEADS // SSM_GROUPS, SSM_STATE
    half = SSM_INNER // SSM_GROUPS

    def body(cp_ref, dtr_ref, z_ref, y_ref, sp_ref, dya_ref, dtb_ref, alog_ref, dskip_ref, nw_ref, ind_ref,
             dz_ref, dcp_ref, ddtr_ref, acc_ref, dnw_ref, ds_ref, dy_ref, dxd_ref, rr_ref, yoff_ref, dcd_ref):
        @pl.when(pl.program_id(0) == 0)
        def _():
            ds_ref[...] = jnp.zeros_like(ds_ref)
            acc_ref[...] = jnp.zeros_like(acc_ref)
            dnw_ref[...] = jnp.zeros_like(dnw_ref)
        cpre_v = cp_ref[...]
        xc, pre, dt, a, causal, tri, lc, row, col = _ssd_prologue(cpre_v, dtr_ref[...], dtb_ref[...], alog_ref[...])
        lct = lc.T
        zv, yv = z_ref[...], y_ref[...]
        sz = _silu(zv)
        y2 = yv * sz
        dya_v = dya_ref[...]
        nwv = nw_ref[...]
        for g in range(SSM_GROUPS):
            sl = slice(g * half, (g + 1) * half)
            yg = y2[:, sl]
            r = lax.rsqrt(jnp.mean(yg * yg, axis=-1, keepdims=True) + EPS)
            nrm = yg * r
            dnw_ref[:, sl] += _colsum(dya_v[:, sl] * nrm)
            dn = dya_v[:, sl] * nwv[:, sl]
            dy2 = r * (dn - nrm * jnp.mean(dn * nrm, axis=-1, keepdims=True))
            dy_ref[:, sl] = dy2 * sz[:, sl]
            dz_ref[:, sl] = dy2 * yv[:, sl] * _dsilu(zv[:, sl])
        ind = ind_ref[...]
        llast = lc[L - 1:L, :]
        dte16 = jnp.exp(llast - lc)
        cd16 = jnp.exp(llast)
        xs = xc[:, :SSM_INNER]
        dtx = _expand(dt, ind)
        ex = _expand(jnp.exp(lc), ind)
        dtex = _expand(dte16, ind)
        cdx = _expand(jnp.broadcast_to(cd16, (8, LANES)), ind)[0:1]
        xd = xs * dtx
        xdb = _bf(xd)
        xd_end = _bf(xd * dtex)
        dyv = dy_ref[...]
        dy_off = _bf(ex * dyv)
        dyb = _bf(dyv)
        dskx = dskip_ref[...]
        lane_c = lax.broadcasted_iota(jnp.int32, (L, LANES), 1)
        lane1 = lax.broadcasted_iota(jnp.int32, (1, LANES), 1)
        sub16 = lax.broadcasted_iota(jnp.int32, (H, L), 0)
        dlc_c = jnp.zeros((L, LANES), F32)
        dlc_r = jnp.zeros((H, L), F32)
        for g in range(SSM_GROUPS):
            sl = slice(g * half, (g + 1) * half)
            b_lo = SSM_INNER + g * N
            c_lo = SSM_INNER + (SSM_GROUPS + g) * N
            bm, cm = _bf(xc[:, b_lo:b_lo + N]), _bf(xc[:, c_lo:c_lo + N])
            cb = _nt(cm, bm)
            st, dst = sp_ref[0, g], ds_ref[g]
            stb, dstb = _bf(st), _bf(dst)
            dcm = _nt(dy_off[:, sl], stb)
            ds_ref[g] = _tn(cm, dy_off[:, sl]) + dst * cdx[:, sl]
            rr_ref[:, sl] = _nn(bm, dstb)
            yoff_ref[:, sl] = ex[:, sl] * _nn(cm, stb)
            db = _nt(xd_end[:, sl], dstb)
            dcd_ref[:, sl] = _colsum(dst * st)
            dcb = jnp.zeros((L, L), F32)
            for j in range(HPG):
                h = g * HPG + j
                hs = slice(h * P, (h + 1) * P)
                decay = jnp.where(causal, jnp.exp(jnp.where(causal, lc[:, h:h + 1] - lct[h:h + 1, :], 0.0)), 0.0)
                m = cb * decay
                dxd_ref[:, hs] = _tn(_bf(m), dyb[:, hs])
                dm = _nt(dyb[:, hs], xdb[:, hs])
                dcb = dcb + dm * decay
                gm = dm * m
                dlc_c = dlc_c + jnp.where(lane_c == h, _rowsum(gm), 0.0)
                dlc_r = dlc_r + jnp.where(sub16 == h, _colsum(gm), 0.0)
            dcbb = _bf(dcb)
            dcp_ref[:, c_lo:c_lo + N] = dcm + _nn(dcbb, bm)
            dcp_ref[:, b_lo:b_lo + N] = db + _tn(dcbb, cm)
        dxd_diag, rr = dxd_ref[...], rr_ref[...]
        tt = _headsum(rr * xd, ind) * dte16
        dlc_rt = jnp.concatenate([dlc_r, jnp.zeros((LANES - H, L), F32)], axis=0).T
        dlc = dlc_c - dlc_rt + _headsum(dyv * yoff_ref[...], ind) - tt
        dcd = _headsum(jnp.broadcast_to(dcd_ref[...], (8, SSM_INNER)), ind)[0:1]
        dlc = dlc + jnp.where(row == L - 1, _colsum(tt) + dcd * cd16, 0.0)
        dla = _tn(tri, dlc, HIGHEST)
        dxd = dxd_diag + dtex * rr
        ddt = _headsum(dxd * xs, ind) + dla * a
        ddtr = jnp.where(lane_c < H, ddt * _sigmoid(pre), 0.0)
        ddtr_ref[...] = ddtr
        acc_ref[0:1, :] += _colsum(ddtr)
        acc_ref[1:2, :] += jnp.where(lane1 < H, _colsum(dla * dt) * a, 0.0)
        acc_ref[2:3, :] += _headsum(jnp.broadcast_to(_colsum(dyv * xs), (8, SSM_INNER)), ind)[0:1]
        dcp_ref[:, 0:SSM_INNER] = dxd * dtx + dskx * dyv
        dcp_ref[...] = dcp_ref[...] * _dsilu(cpre_v)

    rev = lambda i: (nc - 1 - i, 0)
    rspec = lambda c: pl.BlockSpec((L, c), rev)
    return _call(
        body, (cpre, dtr, pm, ypre, sprev, dya, dtb, alog, dskip, normw, ind), side, name=name,
        out_shape=[_sds((T, SSM_INNER)), _sds((T, CONV_DIM)), _sds((T, LANES)), _sds((8, LANES)), _sds((1, SSM_INNER))],
        grid=(nc,),
        in_specs=[rspec(CONV_DIM), rspec(LANES), rspec(SSM_INNER), rspec(SSM_INNER),
                  pl.BlockSpec((1, SSM_GROUPS, N, half), lambda i: (nc - 1 - i, 0, 0, 0)), rspec(SSM_INNER),
                  _vec_spec(LANES), _vec_spec(LANES), _vec_spec(SSM_INNER), _vec_spec(SSM_INNER), _vec_spec(SSM_INNER, LANES)],
        out_specs=[rspec(SSM_INNER), rspec(CONV_DIM), rspec(LANES), _vec_spec(LANES, 8), _vec_spec(SSM_INNER)],
        scratch=[pltpu.VMEM((SSM_GROUPS, N, half), F32), pltpu.VMEM((L, SSM_INNER), F32), pltpu.VMEM((L, SSM_INNER), F32),
                 pltpu.VMEM((L, SSM_INNER), F32), pltpu.VMEM((L, SSM_INNER), F32), pltpu.VMEM((1, SSM_INNER), F32)],
        sem=("arbitrary",))


def _gmlp_common(u, v, lnw, lnb):
    ug = _gelu(u)
    vg = _gelu(v)
    mu = jnp.mean(vg, axis=-1, keepdims=True)
    cen = vg - mu
    rstd = lax.rsqrt(jnp.mean(cen * cen, axis=-1, keepdims=True) + EPS)
    vhat = cen * rstd
    return ug, rstd, vhat, vhat * lnw + lnb


def _causal_mask():
    row = lax.broadcasted_iota(jnp.int32, (CHUNK, CHUNK), 0)
    col = lax.broadcasted_iota(jnp.int32, (CHUNK, CHUNK), 1)
    return row >= col


def _gmlp_fwd(pm, lnw, lnb, ws, bs_exp, name):
    T = pm.shape[0]
    nc = T // CHUNK
    L, G = CHUNK, GMLP_GROUPS

    def body(u_ref, v_ref, lnw_ref, lnb_ref, ws_ref, bs_ref, o_ref):
        ug, _, _, vn = _gmlp_common(u_ref[...], v_ref[...], lnw_ref[...], lnb_ref[...])
        causal = _causal_mask()
        for g in range(G):
            sl = slice(g * L, (g + 1) * L)
            wm = _bf(jnp.where(causal, ws_ref[g], 0.0))
            sv = _nn(wm, _bf(vn[:, sl])) + bs_ref[:, sl]
            o_ref[:, sl] = (ug[:, sl] * sv).astype(BF16)

    return _pcall(
        body, name=name, out_shape=_sds((T, GMLP_INNER), BF16), grid=(nc,),
        in_specs=[_row_spec(L, GMLP_INNER, 1), _row_spec(L, GMLP_INNER, 2), _vec_spec(GMLP_INNER), _vec_spec(GMLP_INNER),
                  pl.BlockSpec((G, L, L), lambda i: (0, 0, 0)), _vec_spec(GMLP_INNER, L)],
        out_specs=_row_spec(L, GMLP_INNER), sem=("parallel",))(pm, pm, lnw, lnb, ws, bs_exp)


def _gmlp_bwd(pm, dyb, lnw, lnb, ws, bs_exp, name, side=None):
    T = pm.shape[0]
    nc = T // CHUNK
    L, G = CHUNK, GMLP_GROUPS

    def body(u_ref, v_ref, dy_ref, lnw_ref, lnb_ref, ws_ref, bs_ref, du_ref, dv_ref, dws_ref, dbs_ref, acc_ref, dvn_ref):
        @pl.when(pl.program_id(0) == 0)
        def _():
            dws_ref[...] = jnp.zeros_like(dws_ref)
            dbs_ref[...] = jnp.zeros_like(dbs_ref)
            acc_ref[...] = jnp.zeros_like(acc_ref)
        uv, vv, dyv, lnwv = u_ref[...], v_ref[...], dy_ref[...], lnw_ref[...]
        ug, rstd, vhat, vn = _gmlp_common(uv, vv, lnwv, lnb_ref[...])
        causal = _causal_mask()
        lane = lax.broadcasted_iota(jnp.int32, (L, LANES), 1)
        dbs = jnp.zeros((L, LANES), F32)
        for g in range(G):
            sl = slice(g * L, (g + 1) * L)
            wm = _bf(jnp.where(causal, ws_ref[g], 0.0))
            vng = _bf(vn[:, sl])
            sv = _nn(wm, vng) + bs_ref[:, sl]
            du_ref[:, sl] = dyv[:, sl] * sv * _dgelu(uv[:, sl])
            dsv = dyv[:, sl] * ug[:, sl]
            dsvb = _bf(dsv)
            dws_ref[g] += jnp.where(causal, _nt(dsvb, vng), 0.0)
            dbs = dbs + jnp.where(lane == g, _rowsum(dsv), 0.0)
            dvn_ref[:, sl] = _tn(wm, dsvb)
        dbs_ref[...] += dbs
        dvn = dvn_ref[...]
        acc_ref[0:1, :] += _colsum(dvn * vhat)
        acc_ref[1:2, :] += _colsum(dvn)
        dvh = dvn * lnwv
        dvg = rstd * (dvh - jnp.mean(dvh, axis=-1, keepdims=True) - vhat * jnp.mean(dvh * vhat, axis=-1, keepdims=True))
        dv_ref[...] = dvg * _dgelu(vv)

    return _call(
        body, (pm, pm, dyb, lnw, lnb, ws, bs_exp), side, name=name,
        out_shape=[_sds((T, GMLP_INNER)), _sds((T, GMLP_INNER)), _sds((G, L, L)), _sds((L, LANES)), _sds((8, GMLP_INNER))],
        grid=(nc,),
        in_specs=[_row_spec(L, GMLP_INNER, 1), _row_spec(L, GMLP_INNER, 2), _row_spec(L, GMLP_INNER),
                  _vec_spec(GMLP_INNER), _vec_spec(GMLP_INNER), pl.BlockSpec((G, L, L), lambda i: (0, 0, 0)),
                  _vec_spec(GMLP_INNER, L)],
        out_specs=[_row_spec(L, GMLP_INNER), _row_spec(L, GMLP_INNER), pl.BlockSpec((G, L, L), lambda i: (0, 0, 0)),
                   _vec_spec(LANES, L), _vec_spec(GMLP_INNER, 8)],
        scratch=[pltpu.VMEM((L, GMLP_INNER), F32)], sem=("arbitrary",))


def _rel_buckets():
    qi = np.arange(CHUNK)[:, None]
    sj = np.arange(2 * CHUNK)[None, :]
    dist = np.maximum(qi + CHUNK - sj, 0)
    max_exact = REL_BUCKETS // 2
    log_ratio = (np.log(np.maximum(dist, 1).astype(np.float32) / np.float32(max_exact))
                 / np.float32(math.log(REL_MAX_DIST / max_exact))).astype(np.float32)
    large = max_exact + (log_ratio * np.float32(REL_BUCKETS - max_exact)).astype(np.int32)
    return np.where(dist < max_exact, dist, np.minimum(large, REL_BUCKETS - 1))


def _bucket_onehot_t():
    bucket = _rel_buckets().reshape(-1)
    return jnp.asarray((np.arange(REL_BUCKETS)[:, None] == bucket[None, :]).astype(np.float32))


def _bias_from_table(table_t, onehot_t, name):
    def body(t_ref, o_ref, out_ref):
        out_ref[...] = _nn(t_ref[...], o_ref[...], HIGHEST)

    return _pcall(body, name=name, out_shape=_sds((ATTN_HEADS, onehot_t.shape[1])))(table_t, onehot_t)


def _table_from_dbias(dbias, onehot_t, name):
    def body(d_ref, o_ref, out_ref):
        out_ref[...] = _nt(d_ref[...], o_ref[...], HIGHEST)

    return _pcall(body, name=name, out_shape=_sds((ATTN_HEADS, REL_BUCKETS)))(dbias, onehot_t)


def _softmax_sink(logits, sink, mask):
    logits = jnp.where(mask, logits, NEG_INF)
    mx = jnp.maximum(jnp.max(logits, axis=-1, keepdims=True), sink)
    e = jnp.exp(logits - mx)
    es = jnp.exp(sink - mx)
    inv = 1.0 / (_rowsum(e) + es)
    return e * inv, es * inv


def _attn_mask(n, heads):
    qi = lax.broadcasted_iota(jnp.int32, (heads * CHUNK, 2 * CHUNK), 0) & (CHUNK - 1)
    sj = lax.broadcasted_iota(jnp.int32, (heads * CHUNK, 2 * CHUNK), 1)
    rel = qi + CHUNK - sj
    return (rel >= 0) & (rel < CHUNK) & ((sj >= CHUNK) | (n > 0))


def _stack_heads(ref, first, count, width):
    return jnp.concatenate([_bf(ref[:, (first + j) * width:(first + j + 1) * width]) for j in range(count)], axis=0)


def _attn_fwd(qkv, bias, sinks, name):
    T = qkv.shape[0]
    nb = T // CHUNK
    L, DH, HPK = CHUNK, ATTN_DH, ATTN_HEADS // ATTN_KV
    scale = DH ** -0.5
    kcol, vcol = ATTN_HEADS * DH // LANES, ATTN_HEADS * DH // LANES + 1

    def body(q_ref, k_ref, v_ref, kp_ref, vp_ref, bias_ref, sink_ref, o_ref, lg_ref, p_ref):
        n = pl.program_id(0)
        mask = _attn_mask(n, 1)
        kband = _bf(jnp.concatenate([kp_ref[...], k_ref[...]], axis=0))
        vband = _bf(jnp.concatenate([vp_ref[...], v_ref[...]], axis=0))
        for kv in range(ATTN_KV):
            lg_ref[...] = _nt(_stack_heads(q_ref, kv * HPK, HPK, DH), kband[:, kv * DH:(kv + 1) * DH])
            for j in range(HPK):
                h = kv * HPK + j
                p, _ = _softmax_sink(lg_ref[j * L:(j + 1) * L, :] * scale + bias_ref[h], sink_ref[h], mask)
                p_ref[j * L:(j + 1) * L, :] = _bf(p)
            og = _nn(p_ref[...], vband[:, kv * DH:(kv + 1) * DH])
            for j in range(HPK):
                h = kv * HPK + j
                o_ref[:, h * DH:(h + 1) * DH] = og[j * L:(j + 1) * L].astype(BF16)

    prev = lambda i: jnp.maximum(i - 1, 0)
    return _pcall(
        body, name=name, out_shape=_sds((T, ATTN_HEADS * DH), BF16), grid=(nb,),
        in_specs=[_row_spec(L, ATTN_HEADS * DH, 0), _row_spec(L, LANES, kcol), _row_spec(L, LANES, vcol),
                  pl.BlockSpec((L, LANES), lambda i: (prev(i), kcol)), pl.BlockSpec((L, LANES), lambda i: (prev(i), vcol)),
                  pl.BlockSpec((ATTN_HEADS, L, 2 * L), lambda i: (0, 0, 0)),
                  pl.BlockSpec(memory_space=pltpu.SMEM)],
        out_specs=_row_spec(L, ATTN_HEADS * DH),
        scratch=[pltpu.VMEM((HPK * L, 2 * L), F32), pltpu.VMEM((HPK * L, 2 * L), BF16)],
        sem=("parallel",))(qkv, qkv, qkv, qkv, qkv, bias, sinks)


def _attn_bwd(qkv, datt, bias, sinks, name):
    T = qkv.shape[0]
    nb = T // CHUNK
    L, DH, HPK = CHUNK, ATTN_DH, ATTN_HEADS // ATTN_KV
    scale = DH ** -0.5
    kcol, vcol = ATTN_HEADS * DH // LANES, ATTN_HEADS * DH // LANES + 1

    def body(q_ref, k_ref, v_ref, kp_ref, vp_ref, do_ref, bias_ref, sink_ref,
             dq_ref, dk_ref, dv_ref, dbias_ref, dsink_ref, pend_k, pend_v, band_k, band_v, lg_ref, dp_ref, p_ref, dl_ref):
        n = pl.program_id(0)

        @pl.when(n == 0)
        def _():
            dbias_ref[...] = jnp.zeros_like(dbias_ref)
            dsink_ref[...] = jnp.zeros_like(dsink_ref)

        @pl.when(n < nb)
        def _():
            mask = _attn_mask(n, 1)
            kband = _bf(jnp.concatenate([kp_ref[...], k_ref[...]], axis=0))
            vband = _bf(jnp.concatenate([vp_ref[...], v_ref[...]], axis=0))
            lane1 = lax.broadcasted_iota(jnp.int32, (1, LANES), 1)
            dsink = jnp.zeros((1, LANES), F32)
            for kv in range(ATTN_KV):
                kb, vb = kband[:, kv * DH:(kv + 1) * DH], vband[:, kv * DH:(kv + 1) * DH]
                qg = _stack_heads(q_ref, kv * HPK, HPK, DH)
                dog = _stack_heads(do_ref, kv * HPK, HPK, DH)
                lg_ref[...] = _nt(qg, kb)
                dp_ref[...] = _nt(dog, vb)
                for j in range(HPK):
                    h = kv * HPK + j
                    rows = slice(j * L, (j + 1) * L)
                    p, ps = _softmax_sink(lg_ref[rows, :] * scale + bias_ref[h], sink_ref[h], mask)
                    dp = dp_ref[rows, :]
                    delta = _rowsum(p * dp)
                    dl = p * (dp - delta)
                    dbias_ref[h] += dl
                    p_ref[rows, :] = _bf(p)
                    dl_ref[rows, :] = _bf(dl)
                    dsink = dsink + jnp.where(lane1 == h, -_colsum(ps * delta), 0.0)
                band_v[:, kv * DH:(kv + 1) * DH] = _tn(p_ref[...], dog)
                dqg = _nn(dl_ref[...], kb) * scale
                band_k[:, kv * DH:(kv + 1) * DH] = _tn(dl_ref[...], qg) * scale
                for j in range(HPK):
                    h = kv * HPK + j
                    dq_ref[:, h * DH:(h + 1) * DH] = dqg[j * L:(j + 1) * L]
            dsink_ref[...] += dsink

            @pl.when(n > 0)
            def _():
                dk_ref[...] = pend_k[...] + band_k[0:L, :]
                dv_ref[...] = pend_v[...] + band_v[0:L, :]
            pend_k[...] = band_k[L:2 * L, :]
            pend_v[...] = band_v[L:2 * L, :]

        @pl.when(n == nb)
        def _():
            dk_ref[...] = pend_k[...]
            dv_ref[...] = pend_v[...]

    cur = lambda i: jnp.minimum(i, nb - 1)
    prev = lambda i: jnp.maximum(jnp.minimum(i, nb - 1) - 1, 0)
    lag = lambda i: jnp.maximum(i - 1, 0)
    return _pcall(
        body, name=name,
        out_shape=[_sds((T, ATTN_HEADS * DH)), _sds((T, LANES)), _sds((T, LANES)), _sds((ATTN_HEADS, L, 2 * L)), _sds((1, LANES))],
        grid=(nb + 1,),
        in_specs=[pl.BlockSpec((L, ATTN_HEADS * DH), lambda i: (cur(i), 0)),
                  pl.BlockSpec((L, LANES), lambda i: (cur(i), kcol)), pl.BlockSpec((L, LANES), lambda i: (cur(i), vcol)),
                  pl.BlockSpec((L, LANES), lambda i: (prev(i), kcol)), pl.BlockSpec((L, LANES), lambda i: (prev(i), vcol)),
                  pl.BlockSpec((L, ATTN_HEADS * DH), lambda i: (cur(i), 0)),
                  pl.BlockSpec((ATTN_HEADS, L, 2 * L), lambda i: (0, 0, 0)),
                  pl.BlockSpec(memory_space=pltpu.SMEM)],
        out_specs=[pl.BlockSpec((L, ATTN_HEADS * DH), lambda i: (cur(i), 0)),
                   pl.BlockSpec((L, LANES), lambda i: (lag(i), 0)), pl.BlockSpec((L, LANES), lambda i: (lag(i), 0)),
                   pl.BlockSpec((ATTN_HEADS, L, 2 * L), lambda i: (0, 0, 0)), _vec_spec(LANES)],
        scratch=[pltpu.VMEM((L, LANES), F32), pltpu.VMEM((L, LANES), F32),
                 pltpu.VMEM((2 * L, LANES), F32), pltpu.VMEM((2 * L, LANES), F32),
                 pltpu.VMEM((HPK * L, 2 * L), F32), pltpu.VMEM((HPK * L, 2 * L), F32),
                 pltpu.VMEM((HPK * L, 2 * L), BF16), pltpu.VMEM((HPK * L, 2 * L), BF16)],
        sem=("arbitrary",))(qkv, qkv, qkv, qkv, qkv, datt, bias, sinks)


def _pad_rows(a, mult):
    pad = (-a.shape[-2]) % mult
    if pad == 0:
        return a
    cfg = [(0, 0)] * (a.ndim - 2) + [(0, pad), (0, 0)]
    return jnp.pad(a, cfg)


class _Pack:
    def __init__(self, width, mult, total_mult):
        self.width, self.mult, self.total_mult = width, mult, total_mult
        self.entries = []
        self.rows = 0

    def add(self, key, shape):
        n = int(np.prod(shape))
        rows = -(-n // self.width)
        self.entries.append((key, self.rows, rows, tuple(shape)))
        self.rows += -(-rows // self.mult) * self.mult

    @property
    def total(self):
        return -(-self.rows // self.total_mult) * self.total_mult

    def pack(self, pieces, dtype, lead=()):
        parts = []
        for key, _, rows, shape in self.entries:
            a = pieces[key].astype(dtype).reshape(lead + (-1,))
            n = int(np.prod(shape))
            a = jnp.pad(a, [(0, 0)] * len(lead) + [(0, rows * self.width - n)])
            a = a.reshape(lead + (rows, self.width))
            parts.append(_pad_rows(a, self.mult))
        out = jnp.concatenate(parts, axis=len(lead))
        return _pad_rows(out, self.total_mult)

    def unpack(self, packed, lead=()):
        out = {}
        for key, off, rows, shape in self.entries:
            a = lax.slice_in_dim(packed, off, off + rows, axis=len(lead))
            a = a.reshape(lead + (-1,))
            n = int(np.prod(shape))
            out[key] = lax.slice_in_dim(a, 0, n, axis=len(lead)).reshape(lead + shape)
        return out


def _ffn_fwd(x, mod, norm_w, wg_t, wu_t, wd, tag, gather=None):
    h = _norm_mod(x, norm_w, mod[4:5], mod[3:4], f"ffn_norm_{tag}")
    side = None if gather is None else (_GatherOps, gather)
    gate, up, act, *gathered = _mm_swiglu(h, wg_t, wu_t, f"ffn_gateup_{tag}", side=side)
    x_out, ffn_out = _mm([(act, wd)], "nn", name=f"ffn_down_{tag}", resid=x, gvec=mod[5:6], keep=True)
    return (x_out, dict(h=h, gate=gate, up=up, act=act, out=ffn_out), *gathered)


def _ffn_bwd(dx_out, x_in, saved, mod, norm_w, wg_t, wu_t, wd, tag, exchange=None):
    dffn, acc_g = _gate_bwd(dx_out, saved["out"], mod[5:6], f"ffn_gate_bwd_{tag}")
    side = None if exchange is None else (_SiblingOps, exchange)
    dgate, dup, *theirs = _mm_swiglu_bwd(dffn, wd, saved["gate"], saved["up"], f"ffn_act_bwd_{tag}", side=side)
    d_wd = _mm_tn(saved["act"], dffn, name=f"ffn_dwd_{tag}")
    d_wg_t = _mm_tn(dgate, saved["h"], name=f"ffn_dwg_{tag}")
    d_wu_t = _mm_tn(dup, saved["h"], name=f"ffn_dwu_{tag}")
    side = None if exchange is None else (_ChipsOps, _pair_sum(exchange, theirs[0], f"grads_pair_sum_{tag}"))
    dh = _mm([(dgate, wg_t), (dup, wu_t)], "nn", name=f"ffn_dh_{tag}", side=side)
    dh, *from_chips = dh if exchange is not None else (dh,)
    dx, acc_n = _norm_mod_bwd(x_in, dh, dx_out, norm_w, mod[4:5], f"ffn_norm_bwd_{tag}")
    grads = dict(d_wg=d_wg_t, d_wu=d_wu_t, d_wd=d_wd, d_g=acc_g[0], d_sc=acc_n[0], d_sh=acc_n[1], d_nw=acc_n[2])
    return (dx, grads, *from_chips)


_BIG = [
    ("out_w", "out_w_even", 0, "row"), ("qkv_w", "qkv_w", 0, "col"), ("o_w", "o_w", 0, "row"),
    ("gate0", "ffn_gate_w", 0, "col"), ("up0", "ffn_up_w", 0, "col"), ("down0", "ffn_down_w", 0, "row"),
    ("gate1", "ffn_gate_w", 1, "col"), ("up1", "ffn_up_w", 1, "col"), ("down1", "ffn_down_w", 1, "row"),
    ("in_w", "in_w_even", 0, "col"),
]


def _to_wire(a, kind):
    return a.T if kind == "col" else a


_GATHER_GROUPS = [["out_w", "in_w"], ["gate0", "up0", "down0"], ["qkv_w", "o_w", "gate1", "up1", "down1"]]

_REPLICATED = ["ada_b", "norm_mix_w", "norm_ffn_w", "conv_b", "dt_bias", "a_log", "d_skip", "ssm_norm_w", "gmlp_ln_w",
               "gmlp_ln_b", "gmlp_ws", "gmlp_bs", "sinks", "rel_table", "final_norm_w"]
_TINY_SHARDED = ["conv_w", "qkv_b", "o_b"]

_WEIGHTS = ['ada_w', 'ada_b', 'norm_mix_w', 'norm_ffn_w', 'in_w_even', 'conv_w', 'conv_b', 'dt_bias', 'a_log', 'd_skip',
            'ssm_norm_w', 'gmlp_ln_w', 'gmlp_ln_b', 'gmlp_ws', 'gmlp_bs', 'out_w_even', 'qkv_w', 'qkv_b', 'o_w', 'o_b',
            'sinks', 'rel_table', 'ffn_gate_w', 'ffn_up_w', 'ffn_down_w', 'final_norm_w']


def _step(x, c, loss_target, W, M, V):
    T = x.shape[1]
    x0 = x[0]
    target = loss_target[0]
    me = 4 * lax.axis_index("x") + 2 * lax.axis_index("y") + lax.axis_index("c")

    big = _Pack(D, 16, 128)
    for key, name, layer, kind in _BIG:
        big.add(key, W[name][layer].shape)
    w_wire_local = {key: _to_wire(W[name][layer].astype(BF16), kind) for key, name, layer, kind in _BIG}
    gather_packs = []
    for keys in _GATHER_GROUPS:
        gp = _Pack(D, 1, 128)
        for key in keys:
            gp.add(key, w_wire_local[key].shape)
        gather_packs.append((gp, gp.pack(w_wire_local, BF16)))
    full = {}

    def gathered_weights(group, gathered):
        shards = gather_packs[group][0].unpack(gathered, lead=(N_DEV,))
        full.update({key: a.reshape(-1, D) for key, a in shards.items()})

    gathered_weights(0, _all_gather(gather_packs[0][1], "gather_weights"))

    small_in = _Pack(D, 8, 8)
    small_in.add("c", (1, D))
    small_in.add("conv_w", W["conv_w"][0].shape)
    small_in.add("qkv_b", W["qkv_b"][0].shape)
    small_in.add("o_b", W["o_b"][0].shape)
    sm = small_in.unpack(_all_gather(small_in.pack(
        dict(c=c, conv_w=W["conv_w"][0], qkv_b=W["qkv_b"][0], o_b=W["o_b"][0]), F32), "gather_small"), lead=(N_DEV,))
    c_all = sm["c"].reshape(N_DEV, D)
    conv_w_full = jnp.transpose(sm["conv_w"], (1, 0, 2)).reshape(SSM_CONV, CONV_DIM)
    qkv_b_full = sm["qkv_b"].reshape(1, QKV_DIM)
    o_b_full = sm["o_b"].reshape(1, D)

    ncol = W["ada_w"].shape[2]
    cond, mod_cols = _mod_matmul(c_all, W["ada_w"], "mod_matmul")
    mod_g = _all_gather(mod_cols.reshape(DEPTH * N_DEV, ncol), "gather_mod").reshape(N_DEV, DEPTH, N_DEV, ncol)
    mod_me = lax.dynamic_index_in_dim(mod_g, me, axis=2, keepdims=False)
    mod_me = jnp.transpose(mod_me, (1, 0, 2)).reshape(DEPTH, 6, D)
    mod_me = jnp.pad(mod_me, ((0, 0), (0, 2), (0, 0))).reshape(DEPTH * 8, D)
    ada_b_rows = jnp.pad(W["ada_b"].reshape(DEPTH, 6, D), ((0, 0), (0, 2), (0, 0))).reshape(DEPTH * 8, D)
    mod_all = _add_rows(mod_me, ada_b_rows, "mod_bias").reshape(DEPTH, 8, D)
    mod0, mod1 = mod_all[0], mod_all[1]

    in_t = full["in_w"]
    o1, o2, o3, o4 = SSM_INNER, SSM_INNER + CONV_DIM, SSM_INNER + CONV_DIM + SSM_HEADS, SSM_INNER + CONV_DIM + SSM_HEADS + GMLP_INNER
    w_z, w_xbc, w_dt, w_u, w_v = in_t[:o1], in_t[o1:o2], in_t[o2:o3], in_t[o3:o4], in_t[o4:]
    w_main = jnp.concatenate([w_z, w_u, w_v, w_xbc], axis=0)
    w_dtp = jnp.pad(w_dt, ((0, LANES - SSM_HEADS), (0, 0)))
    out_w = full["out_w"]

    pad16 = lambda a: jnp.pad(a.reshape(1, SSM_HEADS), ((0, 0), (0, LANES - SSM_HEADS)))
    dtb, alog = pad16(W["dt_bias"][0]), pad16(W["a_log"][0])
    dskip = jnp.repeat(W["d_skip"][0], SSM_HEAD_DIM).reshape(1, SSM_INNER)
    ssm_nw = W["ssm_norm_w"]
    lnw, lnb = W["gmlp_ln_w"], W["gmlp_ln_b"]
    ws = W["gmlp_ws"][0]
    bs_exp = jnp.repeat(W["gmlp_bs"][0].T, CHUNK, axis=1)
    conv_b = W["conv_b"]
    nmw, nfw = W["norm_mix_w"], W["norm_ffn_w"]
    onehot_t = _bucket_onehot_t()
    head_ind = _head_indicator()
    bias = _bias_from_table(W["rel_table"].T, onehot_t, "rel_bias").reshape(ATTN_HEADS, CHUNK, 2 * CHUNK)
    sinks = W["sinks"][0]

    h0 = _norm_mod(x0, nmw[0:1], mod0[1:2], mod0[0:1], "mix_norm_0")
    pm = _mm([(h0, w_main)], "nt", name="in_proj", tn_pref=1536)
    dtr = _mm([(h0, w_dtp)], "nt", name="in_proj_dt")
    cpre = _conv_fwd(pm, conv_w_full, conv_b, "conv_fwd")
    ya, ypre, sprev, gathered_b = _ssd_fwd(cpre, dtr, pm, dtb, alog, dskip, ssm_nw, head_ind, "ssd_fwd",
                                           side=(_GatherOps, gather_packs[1][1]))
    gathered_weights(1, gathered_b)
    yb = _gmlp_fwd(pm, lnw, lnb, ws, bs_exp, "gmlp_fwd")
    x1, mix0 = _mm([(ya, out_w[:SSM_INNER]), (yb, out_w[SSM_INNER:])], "nn", name="out_proj", resid=x0, gvec=mod0[2:3], keep=True)
    x2, ffn0, gathered_c = _ffn_fwd(x1, mod0, nfw[0:1], full["gate0"], full["up0"], full["down0"], "0", gather=gather_packs[2][1])
    gathered_weights(2, gathered_c)
    qkv_t, o_w = full["qkv_w"], full["o_w"]
    w_q, w_k, w_v_att = qkv_t[:D], qkv_t[D:D + LANES], qkv_t[D + LANES:]

    h1 = _norm_mod(x2, nmw[1:2], mod1[1:2], mod1[0:1], "mix_norm_1")
    qkv = _mm([(h1, qkv_t)], "nt", name="qkv_proj", bias=qkv_b_full, tn_pref=1280)
    att = _attn_fwd(qkv, bias, sinks, "attn_fwd")
    x3, mix1 = _mm([(att, o_w)], "nn", name="o_proj", bias=o_b_full, resid=x2, gvec=mod1[2:3], keep=True)
    x4, ffn1 = _ffn_fwd(x3, mod1, nfw[1:2], full["gate1"], full["up1"], full["down1"], "1")

    dx4, acc_f = _final_loss(x4, W["final_norm_w"].reshape(1, D), target, "final_loss")
    loss = lax.psum(acc_f[1, 0], ("x", "y", "c"))
    dx3, gf1 = _ffn_bwd(dx4, x3, ffn1, mod1, nfw[1:2], full["gate1"], full["up1"], full["down1"], "1")

    dmix1, acc_m1 = _gate_bwd(dx3, mix1, mod1[2:3], "mix_gate_bwd_1")
    datt = _mm([(dmix1, o_w)], "nt", name="o_proj_dx")
    d_o_w = _mm_tn(att, dmix1, name="o_proj_dw")
    dq, dk, dv, dbias, dsinks = _attn_bwd(qkv, datt, bias, sinks, "attn_bwd")
    d_table = _table_from_dbias(dbias.reshape(ATTN_HEADS, -1), onehot_t, "rel_table_grad").T
    d_qkv_t = jnp.concatenate([_mm_tn(dq, h1, name="qkv_dw_q"), _mm_tn(dk, h1, name="qkv_dw_k"), _mm_tn(dv, h1, name="qkv_dw_v")], axis=0)
    d_qkv_b = jnp.concatenate([_colsum_call(dq, "qkv_db_q"), _colsum_call(dk, "qkv_db_k"), _colsum_call(dv, "qkv_db_v")], axis=1)
    dh1 = _mm([(dq, w_q), (dk, w_k), (dv, w_v_att)], "nn", name="qkv_proj_dx")
    dx2, acc_n1 = _norm_mod_bwd(x2, dh1, dx3, nmw[1:2], mod1[1:2], "mix_norm_bwd_1")

    g_wire = dict(qkv_w=d_qkv_t, o_w=d_o_w, gate1=gf1["d_wg"], up1=gf1["d_wu"], down1=gf1["d_wd"])

    def packed_partials(group):
        return gather_packs[group][0].pack({key: g_wire[key].reshape(N_DEV, -1, D) for key in _GATHER_GROUPS[group]},
                                           F32, lead=(N_DEV,))

    from_chips = {}
    dx1, gf0, from_chips[2] = _ffn_bwd(dx2, x1, ffn0, mod0, nfw[0:1], full["gate0"], full["up0"], full["down0"], "0",
                                       exchange=packed_partials(2))
    g_wire.update(gate0=gf0["d_wg"], up0=gf0["d_wu"], down0=gf0["d_wd"])
    partials_ffn0 = packed_partials(1)

    dmix0, acc_m0 = _gate_bwd(dx1, mix0, mod0[2:3], "mix_gate_bwd_0")
    dya = _mm([(dmix0, out_w[:SSM_INNER])], "nt", name="out_proj_dx_a")
    dyb = _mm([(dmix0, out_w[SSM_INNER:])], "nt", name="out_proj_dx_b")
    d_out_w = jnp.concatenate([_mm_tn(ya, dmix0, name="out_proj_dw_a"), _mm_tn(yb, dmix0, name="out_proj_dw_b")], axis=0)
    du, dvg, d_ws, d_bs, acc_ln, theirs_ffn0 = _gmlp_bwd(pm, dyb, lnw, lnb, ws, bs_exp, "gmlp_bwd",
                                                         side=(_SiblingOps, partials_ffn0))
    pair_ffn0 = _pair_sum(partials_ffn0, theirs_ffn0, "grads_pair_sum_mix")
    dz, dcpre, ddtr, acc_ssd, d_ssm_nw, from_chips[1] = _ssd_bwd(
        cpre, dtr, pm, ypre, sprev, dya, dtb, alog, dskip, ssm_nw, head_ind, "ssd_bwd", side=(_ChipsOps, pair_ffn0))
    dxbc, acc_conv = _conv_bwd(dcpre, pm, conv_w_full, "conv_bwd")
    d_in_t = jnp.concatenate([
        _mm_tn(dz, h0, name="in_dw_z"), _mm_tn(dxbc, h0, name="in_dw_xbc"),
        _mm_tn(ddtr, h0, name="in_dw_dt")[:SSM_HEADS], _mm_tn(du, h0, name="in_dw_u"), _mm_tn(dvg, h0, name="in_dw_v")], axis=0)
    dh0 = _mm([(dz, w_z), (dxbc, w_xbc), (ddtr, w_dtp), (du, w_u), (dvg, w_v)], "nn", name="in_proj_dx")
    grad_x, acc_n0 = _norm_mod_bwd(x0, dh0, dx1, nmw[0:1], mod0[1:2], "mix_norm_bwd_0")

    g_wire.update(in_w=d_in_t, out_w=d_out_w)
    partials_mix = packed_partials(0)
    theirs_mix = _comm_call(_SiblingOps, partials_mix, "exchange_grads_sibling")
    from_chips[0] = _comm_call(_ChipsOps, _pair_sum(partials_mix, theirs_mix, "grads_pair_sum"), "exchange_grads_chips",
                               from_vmem=True)
    g_mine = {}
    for group in range(len(_GATHER_GROUPS)):
        g_mine.update(gather_packs[group][0].unpack(_sum_parts(from_chips[group], f"grads_chip_sum_{group}")))
    g_nat = {key: _to_wire(g_mine[key], kind) for key, name, layer, kind in _BIG}
    pk = lambda S: big.pack({key: S[name][layer] for key, name, layer, kind in _BIG}, F32)
    res_big = [big.unpack(r) for r in _adamw(big.pack(g_nat, F32)[None], pk(W), pk(M), pk(V), "adamw_big")]

    d_mod = jnp.stack([
        jnp.stack([acc_n0[1], acc_n0[0], acc_m0[0], gf0["d_sh"], gf0["d_sc"], gf0["d_g"]]),
        jnp.stack([acc_n1[1], acc_n1[0], acc_m1[0], gf1["d_sh"], gf1["d_sc"], gf1["d_g"]])])
    g_small = dict(
        ada_b=d_mod.reshape(DEPTH, 6 * D),
        norm_mix_w=jnp.stack([acc_n0[2], acc_n1[2]]), norm_ffn_w=jnp.stack([gf0["d_nw"], gf1["d_nw"]]),
        conv_b=acc_conv[4:5], dt_bias=acc_ssd[0:1, :SSM_HEADS], a_log=acc_ssd[1:2, :SSM_HEADS], d_skip=acc_ssd[2:3, :SSM_HEADS],
        ssm_norm_w=d_ssm_nw, gmlp_ln_w=acc_ln[0:1], gmlp_ln_b=acc_ln[1:2], gmlp_ws=d_ws[None],
        gmlp_bs=d_bs[:, :GMLP_GROUPS].T[None], sinks=dsinks[:, :ATTN_HEADS], rel_table=d_table, final_norm_w=acc_f[0],
        conv_w=acc_conv[0:SSM_CONV], qkv_b=d_qkv_b, o_b=acc_m1[1:2])
    small = _Pack(D, 8, 8)
    for name in _REPLICATED:
        small.add(name, W[name].shape)
    small.add("conv_w", (SSM_CONV, CONV_DIM))
    small.add("qkv_b", (1, QKV_DIM))
    small.add("o_b", (1, D))
    parts_small = _all_gather(small.pack(g_small, F32), "gather_small_grads")
    zeros_tiny = dict(conv_w=jnp.zeros((SSM_CONV, CONV_DIM), F32), qkv_b=jnp.zeros((1, QKV_DIM), F32), o_b=jnp.zeros((1, D), F32))
    pks = lambda S: small.pack({**{name: S[name] for name in _REPLICATED}, **zeros_tiny}, F32)
    res_small = [small.unpack(r) for r in _adamw(parts_small, pks(W), pks(M), pks(V), "adamw_small")]
    g_small_sum = res_small[0]

    n_cw, n_qb, n_ob = W["conv_w"].shape[2], W["qkv_b"].shape[1], W["o_b"].shape[1]
    g_tiny = dict(conv_w=lax.dynamic_slice_in_dim(g_small_sum["conv_w"], me * n_cw, n_cw, axis=1)[None],
                  qkv_b=lax.dynamic_slice_in_dim(g_small_sum["qkv_b"], me * n_qb, n_qb, axis=1),
                  o_b=lax.dynamic_slice_in_dim(g_small_sum["o_b"], me * n_ob, n_ob, axis=1))
    tiny = _Pack(D, 8, 8)
    for name in _TINY_SHARDED:
        tiny.add(name, W[name].shape)
    pkt = lambda S: tiny.pack({name: S[name] for name in _TINY_SHARDED}, F32)
    res_tiny = [tiny.unpack(r) for r in _adamw(pkt(g_tiny)[None], pkt(W), pkt(M), pkt(V), "adamw_tiny")]

    dmod_all = parts_small[:, small.entries[0][1]:small.entries[0][1] + small.entries[0][2]].reshape(N_DEV, DEPTH, 6 * D)
    dmod_cols = jnp.transpose(lax.dynamic_slice_in_dim(dmod_all, me * ncol, ncol, axis=2), (1, 0, 2))
    g_ada_w = _ada_w_grad(cond, dmod_cols, "ada_w_grad")
    flat = lambda a: a.reshape(DEPTH * D, ncol)
    res_ada = [r.reshape(DEPTH, D, ncol) for r in _adamw(flat(g_ada_w)[None], flat(W["ada_w"]), flat(M["ada_w"]), flat(V["ada_w"]), "adamw_ada_w")]

    def result(kind_idx, name):
        if name == "ada_w":
            return res_ada[kind_idx]
        if name in _REPLICATED:
            return res_small[kind_idx][name]
        if name in _TINY_SHARDED:
            return res_tiny[kind_idx][name]
        pieces = [res_big[kind_idx][key] for key, nm, layer, kind in _BIG if nm == name]
        return jnp.stack(pieces)

    outs = [loss, grad_x[None]]
    for kind_idx in range(4):
        outs += [result(kind_idx, name) for name in _WEIGHTS]
    return tuple(outs)


def kernel(x, c, ada_w, ada_b, norm_mix_w, norm_ffn_w, in_w_even, conv_w, conv_b, dt_bias, a_log, d_skip, ssm_norm_w, gmlp_ln_w, gmlp_ln_b, gmlp_ws, gmlp_bs, out_w_even, qkv_w, qkv_b, o_w, o_b, sinks, rel_table, ffn_gate_w, ffn_up_w, ffn_down_w, final_norm_w, loss_target, m_ada_w, m_ada_b, m_norm_mix_w, m_norm_ffn_w, m_in_w_even, m_conv_w, m_conv_b, m_dt_bias, m_a_log, m_d_skip, m_ssm_norm_w, m_gmlp_ln_w, m_gmlp_ln_b, m_gmlp_ws, m_gmlp_bs, m_out_w_even, m_qkv_w, m_qkv_b, m_o_w, m_o_b, m_sinks, m_rel_table, m_ffn_gate_w, m_ffn_up_w, m_ffn_down_w, m_final_norm_w, v_ada_w, v_ada_b, v_norm_mix_w, v_norm_ffn_w, v_in_w_even, v_conv_w, v_conv_b, v_dt_bias, v_a_log, v_d_skip, v_ssm_norm_w, v_gmlp_ln_w, v_gmlp_ln_b, v_gmlp_ws, v_gmlp_bs, v_out_w_even, v_qkv_w, v_qkv_b, v_o_w, v_o_b, v_sinks, v_rel_table, v_ffn_gate_w, v_ffn_up_w, v_ffn_down_w, v_final_norm_w):
    args = locals()
    W = {n: args[n] for n in _WEIGHTS}
    M = {n: args["m_" + n] for n in _WEIGHTS}
    V = {n: args["v_" + n] for n in _WEIGHTS}
    return _step(x, c, loss_target, W, M, V)
```

```python
import functools
import math

import numpy as np
import jax
import jax.numpy as jnp
from jax import lax
from jax.experimental import pallas as pl
from jax.experimental.pallas import tpu as pltpu

F32 = jnp.float32
BF16 = jnp.bfloat16
HIGHEST = lax.Precision.HIGHEST
MESH = pl.DeviceIdType.MESH

N_DEV = 8
D = 1024
DEPTH = 2
SSM_HEADS = 16
SSM_HEAD_DIM = 64
SSM_INNER = 1024
SSM_GROUPS = 2
SSM_STATE = 128
SSM_CONV = 4
CHUNK = 128
CONV_DIM = SSM_INNER + 2 * SSM_GROUPS * SSM_STATE
GMLP_GROUPS = 8
GMLP_INNER = 1024
IN_EVEN = 4624
ATTN_HEADS = 16
ATTN_KV = 2
ATTN_DH = 64
QKV_DIM = 1280
REL_BUCKETS = 32
REL_MAX_DIST = 128
FFN = 2816
EPS = 1e-6
NEG_INF = -1e30
LANES = 128

ADAM_LR = 0.001
ADAM_B1 = 0.9
ADAM_B2 = 0.999
ADAM_EPS = 1e-08
ADAM_WD = 0.01
ADAM_STEP = 10

VMEM_LIMIT_BYTES = 56 * 1024 * 1024
ROW_TILE = 512


def _pcall(body, *, name, out_shape, grid=(), in_specs=None, out_specs=None, scratch=(), sem=None):
    params = dict(vmem_limit_bytes=VMEM_LIMIT_BYTES)
    if sem is not None:
        params["dimension_semantics"] = sem
    specs = {} if in_specs is None else dict(in_specs=in_specs, out_specs=out_specs)
    return pl.pallas_call(
        body, name=name, out_shape=out_shape, grid=grid, **specs,
        scratch_shapes=list(scratch), compiler_params=pltpu.CompilerParams(**params))


def _call(body, args, side=None, *, name, out_shape, grid, in_specs, out_specs, scratch=(), sem=None):
    if side is None:
        return _pcall(body, name=name, out_shape=out_shape, grid=grid, in_specs=in_specs, out_specs=out_specs,
                      scratch=scratch, sem=sem)(*args)
    ops_cls, x = side
    n_in, n_out, n_scr = len(in_specs), len(out_shape), len(scratch)
    steps = int(np.prod(grid))
    hbm = pl.BlockSpec(memory_space=pl.ANY)

    def wrapped(*refs):
        ins, x_ref = refs[:n_in], refs[n_in]
        outs, r_ref = refs[n_in + 1:n_in + 1 + n_out], refs[n_in + 1 + n_out]
        scr, sems = refs[n_in + 2 + n_out:n_in + 2 + n_out + n_scr], refs[n_in + 2 + n_out + n_scr:]
        ops = ops_cls(x_ref, r_ref, *sems)
        step = pl.program_id(0)
        for axis in range(1, len(grid)):
            step = step * grid[axis] + pl.program_id(axis)
        pl.when(step == 0)(ops.start)
        body(*ins, *outs, *scr)
        pl.when(step == (3 * steps) // 4)(ops.forward)
        pl.when(step == steps - 1)(ops.finish)

    return _pcall(
        wrapped, name=name, out_shape=list(out_shape) + [ops_cls.result(x)], grid=grid,
        in_specs=list(in_specs) + [hbm], out_specs=list(out_specs) + [hbm],
        scratch=list(scratch) + ops_cls.scratch(), sem=("arbitrary",) * len(grid))(*args, x)


def _tile(n, pref):
    if n <= pref:
        return n
    best = None
    for t in range(LANES, pref + 1, LANES):
        if n % t == 0:
            best = t
    assert best is not None, (n, pref)
    return best


def _rows(T):
    return min(ROW_TILE, T)


def _sds(shape, dtype=F32):
    return jax.ShapeDtypeStruct(shape, dtype)


def _row_spec(tm, c, col=0):
    return pl.BlockSpec((tm, c), lambda i, col=col: (i, col))


def _vec_spec(c, r=1):
    return pl.BlockSpec((r, c), lambda i: (0, 0))


def _sigmoid(x):
    return jax.nn.sigmoid(x)


def _silu(x):
    return x * _sigmoid(x)


def _dsilu(x):
    s = _sigmoid(x)
    return s * (1.0 + x * (1.0 - s))


def _gelu(x):
    return 0.5 * x * (1.0 + lax.erf(x * 0.7071067811865476))


def _dgelu(x):
    return 0.5 * (1.0 + lax.erf(x * 0.7071067811865476)) + x * jnp.exp(-0.5 * x * x) * 0.3989422804014327


def _dot(a, b, dims, precision=None):
    return lax.dot_general(a, b, (dims, ((), ())), precision=precision, preferred_element_type=F32)


def _nn(a, b, precision=None):
    return _dot(a, b, ((1,), (0,)), precision)


def _nt(a, b, precision=None):
    return _dot(a, b, ((1,), (1,)), precision)


def _tn(a, b, precision=None):
    return _dot(a, b, ((0,), (0,)), precision)


def _bf(x):
    return x.astype(BF16)


def _colsum(x):
    return jnp.sum(x, axis=0, keepdims=True)


def _rowsum(x):
    return jnp.sum(x, axis=1, keepdims=True)


def _allsum(x):
    return _colsum(_rowsum(x))


def _comm_call(ops_cls, x, name, from_vmem=False):
    def body(x_ref, out_ref, *sems):
        ops = ops_cls(x_ref, out_ref, *sems)
        ops.start()
        ops.forward()
        ops.finish()

    return pl.pallas_call(
        body, name=name, out_shape=ops_cls.result(x),
        in_specs=[pl.BlockSpec(memory_space=pltpu.VMEM if from_vmem else pl.ANY)],
        out_specs=pl.BlockSpec(memory_space=pl.ANY), scratch_shapes=ops_cls.scratch(),
    )(x)


def _all_gather(x, name):
    return _comm_call(_GatherOps, x, name, from_vmem=True)


class _GatherOps:
    def __init__(self, x_ref, out_ref, send_sems, recv_sems, local_sem):
        self.x_ref, self.out_ref = x_ref, out_ref
        self.send_sems, self.recv_sems, self.local_sem = send_sems, recv_sems, local_sem
        mx, my, mc = lax.axis_index("x"), lax.axis_index("y"), lax.axis_index("c")
        self.mc = mc
        self.me, self.sibling = (mx, my, mc), (mx, my, 1 - mc)
        self.chips = [(1 - mx, my), (mx, 1 - my), (1 - mx, 1 - my)]

    @staticmethod
    def result(x):
        return _sds((N_DEV,) + x.shape, x.dtype)

    @staticmethod
    def scratch():
        return [pltpu.SemaphoreType.DMA((7,)), pltpu.SemaphoreType.DMA((7,)), pltpu.SemaphoreType.DMA(())]

    def _slot(self, px, py, pc):
        return self.out_ref.at[4 * px + 2 * py + pc]

    def _copy(self, k, block, to, own=False):
        return pltpu.make_async_remote_copy(
            src_ref=self.x_ref if own else self._slot(*block), dst_ref=self._slot(*block),
            send_sem=self.send_sems.at[k], recv_sem=self.recv_sems.at[k], device_id=to, device_id_type=MESH)

    def _mine(self):
        return pltpu.make_async_copy(self.x_ref, self._slot(*self.me), self.local_sem)

    def _first(self):
        return [self._copy(0, self.me, self.sibling, own=True)] + [
            self._copy(1 + j, self.me, (*chip, self.mc), own=True) for j, chip in enumerate(self.chips)]

    def _passed(self):
        return [self._copy(4 + j, (*chip, self.mc), self.sibling) for j, chip in enumerate(self.chips)]

    def start(self):
        self._mine().start()
        for cp in self._first():
            cp.start()

    def forward(self):
        passed = self._passed()
        for j, chip in enumerate(self.chips):
            self._copy(1 + j, (*chip, self.mc), self.me).wait_recv()
            passed[j].start()

    def finish(self):
        self._copy(0, self.sibling, self.me).wait_recv()
        for j, chip in enumerate(self.chips):
            self._copy(4 + j, (*chip, 1 - self.mc), self.me).wait_recv()
        for cp in self._first() + self._passed():
            cp.wait_send()
        self._mine().wait()


N_CHIP = 4


class _SiblingOps:
    def __init__(self, p_ref, theirs_ref, send_sems, recv_sems):
        self.p_ref, self.theirs_ref, self.send_sems, self.recv_sems = p_ref, theirs_ref, send_sems, recv_sems

    @staticmethod
    def result(p):
        return _sds((N_CHIP,) + p.shape[1:], p.dtype)

    @staticmethod
    def scratch():
        return [pltpu.SemaphoreType.DMA((N_CHIP,))] * 2

    def _copies(self):
        mx, my, mc = lax.axis_index("x"), lax.axis_index("y"), lax.axis_index("c")
        return [pltpu.make_async_remote_copy(
            src_ref=self.p_ref.at[2 * chip + 1 - mc], dst_ref=self.theirs_ref.at[chip],
            send_sem=self.send_sems.at[chip], recv_sem=self.recv_sems.at[chip],
            device_id=(mx, my, 1 - mc), device_id_type=MESH) for chip in range(N_CHIP)]

    def start(self):
        for cp in self._copies():
            cp.start()

    def forward(self):
        pass

    def finish(self):
        for cp in self._copies():
            cp.wait()


class _ChipsOps:
    def __init__(self, q_ref, out_ref, send_sems, recv_sems, local_sem):
        self.q_ref, self.out_ref = q_ref, out_ref
        self.send_sems, self.recv_sems, self.local_sem = send_sems, recv_sems, local_sem

    @staticmethod
    def result(q):
        return _sds(q.shape, q.dtype)

    @staticmethod
    def scratch():
        return [pltpu.SemaphoreType.DMA((N_CHIP - 1,)), pltpu.SemaphoreType.DMA((N_CHIP - 1,)), pltpu.SemaphoreType.DMA(())]

    def _copies(self):
        mx, my, mc = lax.axis_index("x"), lax.axis_index("y"), lax.axis_index("c")
        me = 2 * mx + my
        local = pltpu.make_async_copy(self.q_ref.at[me], self.out_ref.at[me], self.local_sem)
        remote = []
        for r in range(1, N_CHIP):
            px = 1 - mx if r & 2 else mx
            py = 1 - my if r & 1 else my
            remote.append(pltpu.make_async_remote_copy(
                src_ref=self.q_ref.at[2 * px + py], dst_ref=self.out_ref.at[me],
                send_sem=self.send_sems.at[r - 1], recv_sem=self.recv_sems.at[r - 1],
                device_id=(px, py, mc), device_id_type=MESH))
        return local, remote

    def start(self):
        local, remote = self._copies()
        local.start()
        for cp in remote:
            cp.start()

    def forward(self):
        pass

    def finish(self):
        local, remote = self._copies()
        for cp in remote:
            cp.wait()
        local.wait()


def _pair_sum(p, theirs, name):
    n, R, C = theirs.shape
    tr = _tile_rows(R, 256)

    def body(p_ref, t_ref, o_ref):
        mc = lax.axis_index("c")
        o_ref[0] = (p_ref[0, mc] + t_ref[0]).astype(BF16)

    blk = pl.BlockSpec((1, tr, C), lambda s, i: (s, i, 0))
    return _pcall(body, name=name, out_shape=_sds((n, R, C), BF16), grid=(n, R // tr),
                  in_specs=[pl.BlockSpec((1, 2, tr, C), lambda s, i: (s, 0, i, 0)), blk],
                  out_specs=blk, sem=("parallel", "parallel"))(p.reshape(n, 2, R, C), theirs)


def _sum_parts(parts, name):
    P, R, C = parts.shape
    tr = _tile_rows(R, 256)

    def body(p_ref, o_ref):
        g = p_ref[0].astype(F32)
        for k in range(1, P):
            g = g + p_ref[k].astype(F32)
        o_ref[...] = g

    return _pcall(body, name=name, out_shape=_sds((R, C)), grid=(R // tr,),
                  in_specs=[pl.BlockSpec((P, tr, C), lambda i: (0, i, 0))],
                  out_specs=pl.BlockSpec((tr, C), lambda i: (i, 0)), sem=("parallel",))(parts)


def _adamw(parts, w, m, v, name):
    P, R, C = parts.shape
    tr = R if R <= 256 else _tile_rows(R, 256)

    def body(p_ref, w_ref, m_ref, v_ref, g_ref, d_ref, nm_ref, nv_ref):
        g = p_ref[0]
        for k in range(1, P):
            g = g + p_ref[k]
        nm = ADAM_B1 * m_ref[...] + (1.0 - ADAM_B1) * g
        nv = ADAM_B2 * v_ref[...] + (1.0 - ADAM_B2) * (g * g)
        m_hat = nm / (1.0 - ADAM_B1 ** ADAM_STEP)
        v_hat = nv / (1.0 - ADAM_B2 ** ADAM_STEP)
        g_ref[...] = g
        d_ref[...] = -ADAM_LR * (m_hat / (jnp.sqrt(v_hat) + ADAM_EPS) + ADAM_WD * w_ref[...])
        nm_ref[...] = nm
        nv_ref[...] = nv

    blk = pl.BlockSpec((tr, C), lambda i: (i, 0))
    return _pcall(
        body, name=name, out_shape=[_sds((R, C))] * 4, grid=(R // tr,),
        in_specs=[pl.BlockSpec((P, tr, C), lambda i: (0, i, 0)), blk, blk, blk],
        out_specs=[blk] * 4, sem=("parallel",))(parts, w, m, v)


def _tile_rows(n, pref):
    best = None
    for t in range(8, pref + 1, 8):
        if n % t == 0:
            best = t
    assert best is not None, (n, pref)
    return best


def _mm(pairs, mode, *, name, out_dtype=F32, bias=None, resid=None, gvec=None, keep=False, tn_pref=1024, side=None):
    M = pairs[0][0].shape[0]
    N = pairs[0][1].shape[1] if mode == "nn" else pairs[0][1].shape[0]
    tm, tn = _rows(M), _tile(N, tn_pref)
    n_pairs = len(pairs)
    has_bias, has_res = bias is not None, resid is not None

    def body(*refs):
        ab = refs[:2 * n_pairs]
        pos = 2 * n_pairs
        b_ref = refs[pos] if has_bias else None
        pos += has_bias
        r_ref, g_ref = (refs[pos], refs[pos + 1]) if has_res else (None, None)
        pos += 2 * has_res
        outs = refs[pos:]
        acc = None
        for p in range(n_pairs):
            a, b = _bf(ab[2 * p][...]), _bf(ab[2 * p + 1][...])
            d = _nn(a, b) if mode == "nn" else _nt(a, b)
            acc = d if acc is None else acc + d
        if has_bias:
            acc = acc + b_ref[...]
        if has_res:
            outs[0][...] = (r_ref[...] + g_ref[...] * acc).astype(outs[0].dtype)
            if keep:
                outs[1][...] = acc
        else:
            outs[0][...] = acc.astype(outs[0].dtype)

    in_specs, args = [], []
    for a, b in pairs:
        K = a.shape[1]
        in_specs.append(pl.BlockSpec((tm, K), lambda j, i: (i, 0)))
        if mode == "nn":
            in_specs.append(pl.BlockSpec((K, tn), lambda j, i: (0, j)))
        else:
            in_specs.append(pl.BlockSpec((tn, K), lambda j, i: (j, 0)))
        args += [a, b]
    if has_bias:
        in_specs.append(pl.BlockSpec((1, tn), lambda j, i: (0, j)))
        args.append(bias)
    if has_res:
        in_specs.append(pl.BlockSpec((tm, tn), lambda j, i: (i, j)))
        in_specs.append(pl.BlockSpec((1, tn), lambda j, i: (0, j)))
        args += [resid, gvec]
    o_spec = pl.BlockSpec((tm, tn), lambda j, i: (i, j))
    n_out = 2 if (has_res and keep) else 1
    out_shape = [_sds((M, N), out_dtype)] + ([_sds((M, N), F32)] if n_out == 2 else [])
    res = _call(body, args, side, name=name, out_shape=out_shape, grid=(N // tn, M // tm), in_specs=in_specs,
                out_specs=[o_spec] * n_out, sem=("parallel", "parallel"))
    out = tuple(res[:n_out]) if n_out == 2 else res[0]
    return out if side is None else (out, res[n_out])


def _mm_tn(a, b, *, name, tm_pref=1408, tn_pref=1536):
    K, M = a.shape
    N = b.shape[1]
    tm, tn = _tile(M, tm_pref), _tile(N, tn_pref)
    tk = K if K <= 2 * ROW_TILE else 2 * ROW_TILE

    def body(a_ref, b_ref, o_ref):
        @pl.when(pl.program_id(2) == 0)
        def _():
            o_ref[...] = jnp.zeros_like(o_ref)
        o_ref[...] += _tn(_bf(a_ref[...]), _bf(b_ref[...]))

    return _pcall(
        body, name=name, out_shape=_sds((M, N)), grid=(M // tm, N // tn, K // tk),
        in_specs=[pl.BlockSpec((tk, tm), lambda i, j, k: (k, i)), pl.BlockSpec((tk, tn), lambda i, j, k: (k, j))],
        out_specs=pl.BlockSpec((tm, tn), lambda i, j, k: (i, j)),
        sem=("parallel", "parallel", "arbitrary"))(a, b)


def _mm_swiglu(h, wg_t, wu_t, name, side=None):
    M, K = h.shape
    N = wg_t.shape[0]
    tm, tn = _rows(M), _tile(N, 1408)

    def body(h_ref, wg_ref, wu_ref, gate_ref, up_ref, act_ref):
        hv = _bf(h_ref[...])
        gate = _nt(hv, wg_ref[...])
        up = _nt(hv, wu_ref[...])
        gate_ref[...] = gate.astype(BF16)
        up_ref[...] = up.astype(BF16)
        act_ref[...] = (_silu(gate) * up).astype(BF16)

    w_spec = pl.BlockSpec((tn, K), lambda j, i: (j, 0))
    o_spec = pl.BlockSpec((tm, tn), lambda j, i: (i, j))
    return _call(body, (h, wg_t, wu_t), side, name=name,
                 out_shape=[_sds((M, N), BF16)] * 3, grid=(N // tn, M // tm),
                 in_specs=[pl.BlockSpec((tm, K), lambda j, i: (i, 0)), w_spec, w_spec], out_specs=[o_spec] * 3,
                 sem=("parallel", "parallel"))


def _mm_swiglu_bwd(dout, wd, gate, up, name, side=None):
    M, K = dout.shape
    N = wd.shape[0]
    tm, tn = _rows(M), _tile(N, 1408)

    def body(d_ref, wd_ref, gate_ref, up_ref, dg_ref, du_ref):
        dact = _nt(_bf(d_ref[...]), wd_ref[...])
        g = gate_ref[...].astype(F32)
        dg_ref[...] = (dact * up_ref[...].astype(F32) * _dsilu(g)).astype(BF16)
        du_ref[...] = (dact * _silu(g)).astype(BF16)

    t_spec = pl.BlockSpec((tm, tn), lambda j, i: (i, j))
    return _call(
        body, (dout, wd, gate, up), side, name=name, out_shape=[_sds((M, N), BF16)] * 2, grid=(N // tn, M // tm),
        in_specs=[pl.BlockSpec((tm, K), lambda j, i: (i, 0)), pl.BlockSpec((tn, K), lambda j, i: (j, 0)), t_spec, t_spec],
        out_specs=[t_spec] * 2, sem=("parallel", "parallel"))


def _norm_mod(x, w, sc, sh, name):
    T = x.shape[0]
    tm = _rows(T)

    def body(x_ref, w_ref, sc_ref, sh_ref, o_ref):
        xv = x_ref[...]
        r = lax.rsqrt(jnp.mean(xv * xv, axis=-1, keepdims=True) + EPS)
        o_ref[...] = ((xv * r * w_ref[...]) * (1.0 + sc_ref[...]) + sh_ref[...]).astype(BF16)

    return _pcall(body, name=name, out_shape=_sds((T, D), BF16), grid=(T // tm,),
                  in_specs=[_row_spec(tm, D), _vec_spec(D), _vec_spec(D), _vec_spec(D)],
                  out_specs=_row_spec(tm, D), sem=("parallel",))(x, w, sc, sh)


def _norm_mod_bwd(x, dh, dres, w, sc, name, side=None):
    T = x.shape[0]
    tm = _rows(T)

    def body(x_ref, dh_ref, dres_ref, w_ref, sc_ref, dx_ref, acc_ref):
        @pl.when(pl.program_id(0) == 0)
        def _():
            acc_ref[...] = jnp.zeros_like(acc_ref)
        xv, dh_v, wv = x_ref[...], dh_ref[...], w_ref[...]
        r = lax.rsqrt(jnp.mean(xv * xv, axis=-1, keepdims=True) + EPS)
        n = xv * r
        dnw = dh_v * (1.0 + sc_ref[...])
        dn = dnw * wv
        dx_ref[...] = dres_ref[...] + r * (dn - n * jnp.mean(dn * n, axis=-1, keepdims=True))
        acc_ref[0:1, :] += _colsum(dh_v * (n * wv))
        acc_ref[1:2, :] += _colsum(dh_v)
        acc_ref[2:3, :] += _colsum(dnw * n)

    return _call(body, (x, dh, dres, w, sc), side, name=name, out_shape=[_sds((T, D)), _sds((8, D))], grid=(T // tm,),
                 in_specs=[_row_spec(tm, D), _row_spec(tm, D), _row_spec(tm, D), _vec_spec(D), _vec_spec(D)],
                 out_specs=[_row_spec(tm, D), _vec_spec(D, 8)], sem=("arbitrary",))


def _gate_bwd(dx, branch, g, name):
    T = dx.shape[0]
    tm = _rows(T)

    def body(dx_ref, br_ref, g_ref, db_ref, acc_ref):
        @pl.when(pl.program_id(0) == 0)
        def _():
            acc_ref[...] = jnp.zeros_like(acc_ref)
        dxv = dx_ref[...]
        db = g_ref[...] * dxv
        db_ref[...] = db.astype(BF16)
        acc_ref[0:1, :] += _colsum(dxv * br_ref[...])
        acc_ref[1:2, :] += _colsum(db)

    return _pcall(body, name=name, out_shape=[_sds((T, D), BF16), _sds((8, D))], grid=(T // tm,),
                  in_specs=[_row_spec(tm, D), _row_spec(tm, D), _vec_spec(D)],
                  out_specs=[_row_spec(tm, D), _vec_spec(D, 8)], sem=("arbitrary",))(dx, branch, g)


def _final_loss(x, wf, target, name):
    T = x.shape[0]
    tm = _rows(T)

    def body(x_ref, w_ref, t_ref, dx_ref, acc_ref):
        @pl.when(pl.program_id(0) == 0)
        def _():
            acc_ref[...] = jnp.zeros_like(acc_ref)
        xv, wv = x_ref[...], w_ref[...]
        r = lax.rsqrt(jnp.mean(xv * xv, axis=-1, keepdims=True) + EPS)
        n = xv * r
        err = n * wv - t_ref[...]
        dy = err * (1.0 / D)
        dn = dy * wv
        dx_ref[...] = r * (dn - n * jnp.mean(dn * n, axis=-1, keepdims=True))
        acc_ref[0:1, :] += _colsum(dy * n)
        acc_ref[1:2, :] += jnp.broadcast_to(_allsum(err * err) * (0.5 / D), (1, D))

    return _pcall(body, name=name, out_shape=[_sds((T, D)), _sds((8, D))], grid=(T // tm,),
                  in_specs=[_row_spec(tm, D), _vec_spec(D), _row_spec(tm, D)],
                  out_specs=[_row_spec(tm, D), _vec_spec(D, 8)], sem=("arbitrary",))(x, wf, target)


def _colsum_call(x, name):
    T, C = x.shape
    tm = _rows(T)

    def body(x_ref, o_ref):
        @pl.when(pl.program_id(0) == 0)
        def _():
            o_ref[...] = jnp.zeros_like(o_ref)
        o_ref[...] += _colsum(x_ref[...].astype(F32))

    return _pcall(body, name=name, out_shape=_sds((1, C)), grid=(T // tm,), in_specs=[_row_spec(tm, C)],
                  out_specs=_vec_spec(C), sem=("arbitrary",))(x)


def _mod_matmul(c_all, ada_w, name):
    n = ada_w.shape[2]

    def body(c_ref, w_ref, cond_ref, o_ref):
        cond = _silu(c_ref[...])
        cond_ref[...] = cond
        o_ref[0] = _nn(cond, w_ref[0])

    return _pcall(body, name=name, out_shape=[_sds((N_DEV, D)), _sds((DEPTH, N_DEV, n))], grid=(DEPTH,),
                  in_specs=[pl.BlockSpec((N_DEV, D), lambda l: (0, 0)), pl.BlockSpec((1, D, n), lambda l: (l, 0, 0))],
                  out_specs=[pl.BlockSpec((N_DEV, D), lambda l: (0, 0)), pl.BlockSpec((1, N_DEV, n), lambda l: (l, 0, 0))],
                  sem=("arbitrary",))(c_all, ada_w)


def _add_rows(a, b, name):
    def body(a_ref, b_ref, o_ref):
        o_ref[...] = a_ref[...] + b_ref[...]

    return _pcall(body, name=name, out_shape=_sds(a.shape))(a, b)


def _ada_w_grad(cond, dmod_cols, name):
    n = dmod_cols.shape[2]

    def body(c_ref, d_ref, o_ref):
        o_ref[0] = _tn(c_ref[...], d_ref[0])

    return _pcall(body, name=name, out_shape=_sds((DEPTH, D, n)), grid=(DEPTH,),
                  in_specs=[pl.BlockSpec((N_DEV, D), lambda l: (0, 0)), pl.BlockSpec((1, N_DEV, n), lambda l: (l, 0, 0))],
                  out_specs=pl.BlockSpec((1, D, n), lambda l: (l, 0, 0)), sem=("parallel",))(cond, dmod_cols)


def _conv_fwd(pm, conv_w, conv_b, name):
    T = pm.shape[0]
    tm = _rows(T)
    C = CONV_DIM

    def body(x_ref, prev_ref, w_ref, b_ref, o_ref):
        cur = x_ref[...]
        prev = jnp.where(pl.program_id(0) > 0, prev_ref[...], 0.0)
        cur8 = cur[0:8]
        row8 = lax.broadcasted_iota(jnp.int32, (8, C), 0)
        full = w_ref[3:4, :] * cur
        head = w_ref[3:4, :] * cur8
        for k in range(1, SSM_CONV):
            wk = w_ref[3 - k:4 - k, :]
            full = full + wk * pltpu.roll(cur, k, 0)
            head = head + wk * jnp.where(row8 < k, pltpu.roll(prev, k, 0), pltpu.roll(cur8, k, 0))
        o_ref[...] = full + b_ref[...]
        o_ref[0:8, :] = head + b_ref[...]

    return _pcall(
        body, name=name, out_shape=_sds((T, C)), grid=(T // tm,),
        in_specs=[pl.BlockSpec((tm, C), lambda i: (i, 2)),
                  pl.BlockSpec((8, C), lambda i: (jnp.maximum(i * (tm // 8) - 1, 0), 2)),
                  _vec_spec(C, SSM_CONV), _vec_spec(C)],
        out_specs=_row_spec(tm, C), sem=("parallel",))(pm, pm, conv_w, conv_b)


def _conv_bwd(dc, pm, conv_w, name):
    T = dc.shape[0]
    tm = _rows(T)
    C = CONV_DIM
    nt = T // tm

    def body(dc_ref, nxt_ref, x_ref, prev_ref, w_ref, dx_ref, acc_ref):
        i = pl.program_id(0)

        @pl.when(i == 0)
        def _():
            acc_ref[...] = jnp.zeros_like(acc_ref)
        dcv = dc_ref[...]
        nxt = jnp.where(i < nt - 1, nxt_ref[...], 0.0)
        xc = x_ref[...]
        prev = jnp.where(i > 0, prev_ref[...], 0.0)
        dc8h, dc8t, x8 = dcv[0:8], dcv[tm - 8:tm], xc[0:8]
        row8 = lax.broadcasted_iota(jnp.int32, (8, C), 0)
        full = w_ref[3:4, :] * dcv
        tail = w_ref[3:4, :] * dc8t
        acc_ref[3:4, :] += _colsum(dcv * xc)
        for k in range(1, SSM_CONV):
            wk = w_ref[3 - k:4 - k, :]
            full = full + wk * pltpu.roll(dcv, tm - k, 0)
            tail = tail + wk * jnp.where(row8 + k >= 8, pltpu.roll(nxt, 8 - k, 0), pltpu.roll(dc8t, 8 - k, 0))
            xs_head = jnp.where(row8 < k, pltpu.roll(prev, k, 0), pltpu.roll(x8, k, 0))
            prod = dcv * pltpu.roll(xc, k, 0)
            acc_ref[3 - k:4 - k, :] += _colsum(prod) - _colsum(prod[0:8]) + _colsum(dc8h * xs_head)
        acc_ref[4:5, :] += _colsum(dcv)
        dx_ref[...] = full
        dx_ref[tm - 8:tm, :] = tail

    return _pcall(
        body, name=name, out_shape=[_sds((T, C)), _sds((8, C))], grid=(nt,),
        in_specs=[_row_spec(tm, C),
                  pl.BlockSpec((8, C), lambda i: (jnp.minimum((i + 1) * (tm // 8), T // 8 - 1), 0)),
                  pl.BlockSpec((tm, C), lambda i: (i, 2)),
                  pl.BlockSpec((8, C), lambda i: (jnp.maximum(i * (tm // 8) - 1, 0), 2)),
                  _vec_spec(C, SSM_CONV)],
        out_specs=[_row_spec(tm, C), _vec_spec(C, 8)], sem=("arbitrary",))(dc, dc, pm, pm, conv_w)


def _ssd_prologue(cpre, dtr, dtb, alog):
    L = CHUNK
    xc = _silu(cpre)
    pre = dtr + dtb
    dt = jnp.maximum(pre, 0.0) + jnp.log1p(jnp.exp(-jnp.abs(pre)))
    a = -jnp.exp(alog)
    la = dt * a
    row = lax.broadcasted_iota(jnp.int32, (L, L), 0)
    col = lax.broadcasted_iota(jnp.int32, (L, L), 1)
    causal = row >= col
    tri = causal.astype(F32)
    lc = _nn(tri, la, HIGHEST)
    return xc, pre, dt, a, causal, tri, lc, row, col


def _head_indicator():
    m = np.zeros((LANES, SSM_INNER), np.float32)
    for h in range(SSM_HEADS):
        m[h, h * SSM_HEAD_DIM:(h + 1) * SSM_HEAD_DIM] = 1.0
    return jnp.asarray(m, dtype=BF16)


def _split_dot(x, ind, dims):
    hi = x.astype(BF16)
    lo = (x - hi.astype(F32)).astype(BF16)
    return _dot(hi, ind, dims) + _dot(lo, ind, dims)


def _expand(x16, ind):
    return _split_dot(x16, ind, ((1,), (0,)))


def _headsum(x, ind):
    return _split_dot(x, ind, ((1,), (1,)))


def _ssd_fwd(cpre, dtr, pm, dtb, alog, dskip, normw, ind, name, side=None):
    T = cpre.shape[0]
    nc = T // CHUNK
    L, P, H, HPG, N = CHUNK, SSM_HEAD_DIM, SSM_HEADS, SSM_HEADS // SSM_GROUPS, SSM_STATE
    half = SSM_INNER // SSM_GROUPS

    def body(cp_ref, dtr_ref, z_ref, dtb_ref, alog_ref, dskip_ref, nw_ref, ind_ref, ya_ref, y_ref, sp_ref, st_ref):
        @pl.when(pl.program_id(0) == 0)
        def _():
            st_ref[...] = jnp.zeros_like(st_ref)
        xc, _, dt, _, causal, _, lc, _, _ = _ssd_prologue(cp_ref[...], dtr_ref[...], dtb_ref[...], alog_ref[...])
        lct = lc.T
        ind = ind_ref[...]
        llast = lc[L - 1:L, :]
        xs = xc[:, :SSM_INNER]
        xd = xs * _expand(dt, ind)
        ex = _expand(jnp.exp(lc), ind)
        xd_end = _bf(xd * _expand(jnp.exp(llast - lc), ind))
        cdx = _expand(jnp.broadcast_to(jnp.exp(llast), (8, LANES)), ind)[0:1]
        xdb = _bf(xd)
        sp_ref[0] = st_ref[...]
        for g in range(SSM_GROUPS):
            sl = slice(g * half, (g + 1) * half)
            bm = _bf(xc[:, SSM_INNER + g * N:SSM_INNER + (g + 1) * N])
            cm = _bf(xc[:, SSM_INNER + (SSM_GROUPS + g) * N:SSM_INNER + (SSM_GROUPS + g + 1) * N])
            cb = _nt(cm, bm)
            st = st_ref[g]
            y_ref[:, sl] = ex[:, sl] * _nn(cm, _bf(st)) + dskip_ref[:, sl] * xs[:, sl]
            st_ref[g] = st * cdx[:, sl] + _tn(bm, xd_end[:, sl])
            for j in range(HPG):
                h = g * HPG + j
                decay = jnp.where(causal, jnp.exp(jnp.where(causal, lc[:, h:h + 1] - lct[h:h + 1, :], 0.0)), 0.0)
                y_ref[:, h * P:(h + 1) * P] += _nn(_bf(cb * decay), xdb[:, h * P:(h + 1) * P])
        y2 = y_ref[...] * _silu(z_ref[...])
        for g in range(SSM_GROUPS):
            yg = y2[:, g * half:(g + 1) * half]
            r = lax.rsqrt(jnp.mean(yg * yg, axis=-1, keepdims=True) + EPS)
            ya_ref[:, g * half:(g + 1) * half] = (yg * r * nw_ref[:, g * half:(g + 1) * half]).astype(BF16)

    return _call(
        body, (cpre, dtr, pm, dtb, alog, dskip, normw, ind), side, name=name,
        out_shape=[_sds((T, SSM_INNER), BF16), _sds((T, SSM_INNER)), _sds((nc, SSM_GROUPS, N, half))], grid=(nc,),
        in_specs=[_row_spec(L, CONV_DIM), _row_spec(L, LANES), _row_spec(L, SSM_INNER, 0),
                  _vec_spec(LANES), _vec_spec(LANES), _vec_spec(SSM_INNER), _vec_spec(SSM_INNER), _vec_spec(SSM_INNER, LANES)],
        out_specs=[_row_spec(L, SSM_INNER), _row_spec(L, SSM_INNER),
                   pl.BlockSpec((1, SSM_GROUPS, N, half), lambda i: (i, 0, 0, 0))],
        scratch=[pltpu.VMEM((SSM_GROUPS, N, half), F32)], sem=("arbitrary",))


def _ssd_bwd(cpre, dtr, pm, ypre, sprev, dya, dtb, alog, dskip, normw, ind, name, side=None):
    T = cpre.shape[0]
    nc = T // CHUNK
    L, P, H, HPG, N = CHUNK, SSM_HEAD_DIM, SSM_HEADS, SSM_HEADS // SSM_GROUPS, SSM_STATE
    half = SSM_INNER // SSM_GROUPS

    def body(cp_ref, dtr_ref, z_ref, y_ref, sp_ref, dya_ref, dtb_ref, alog_ref, dskip_ref, nw_ref, ind_ref,
             dz_ref, dcp_ref, ddtr_ref, acc_ref, dnw_ref, ds_ref, dy_ref, dxd_ref, rr_ref, yoff_ref, dcd_ref):
        @pl.when(pl.program_id(0) == 0)
        def _():
            ds_ref[...] = jnp.zeros_like(ds_ref)
            acc_ref[...] = jnp.zeros_like(acc_ref)
            dnw_ref[...] = jnp.zeros_like(dnw_ref)
        cpre_v = cp_ref[...]
        xc, pre, dt, a, causal, tri, lc, row, col = _ssd_prologue(cpre_v, dtr_ref[...], dtb_ref[...], alog_ref[...])
        lct = lc.T
        zv, yv = z_ref[...], y_ref[...]
        sz = _silu(zv)
        y2 = yv * sz
        dya_v = dya_ref[...]
        nwv = nw_ref[...]
        for g in range(SSM_GROUPS):
            sl = slice(g * half, (g + 1) * half)
            yg = y2[:, sl]
            r = lax.rsqrt(jnp.mean(yg * yg, axis=-1, keepdims=True) + EPS)
            nrm = yg * r
            dnw_ref[:, sl] += _colsum(dya_v[:, sl] * nrm)
            dn = dya_v[:, sl] * nwv[:, sl]
            dy2 = r * (dn - nrm * jnp.mean(dn * nrm, axis=-1, keepdims=True))
            dy_ref[:, sl] = dy2 * sz[:, sl]
            dz_ref[:, sl] = dy2 * yv[:, sl] * _dsilu(zv[:, sl])
        ind = ind_ref[...]
        llast = lc[L - 1:L, :]
        dte16 = jnp.exp(llast - lc)
        cd16 = jnp.exp(llast)
        xs = xc[:, :SSM_INNER]
        dtx = _expand(dt, ind)
        ex = _expand(jnp.exp(lc), ind)
        dtex = _expand(dte16, ind)
        cdx = _expand(jnp.broadcast_to(cd16, (8, LANES)), ind)[0:1]
        xd = xs * dtx
        xdb = _bf(xd)
        xd_end = _bf(xd * dtex)
        dyv = dy_ref[...]
        dy_off = _bf(ex * dyv)
        dyb = _bf(dyv)
        dskx = dskip_ref[...]
        lane_c = lax.broadcasted_iota(jnp.int32, (L, LANES), 1)
        lane1 = lax.broadcasted_iota(jnp.int32, (1, LANES), 1)
        sub16 = lax.broadcasted_iota(jnp.int32, (H, L), 0)
        dlc_c = jnp.zeros((L, LANES), F32)
        dlc_r = jnp.zeros((H, L), F32)
        for g in range(SSM_GROUPS):
            sl = slice(g * half, (g + 1) * half)
            b_lo = SSM_INNER + g * N
            c_lo = SSM_INNER + (SSM_GROUPS + g) * N
            bm, cm = _bf(xc[:, b_lo:b_lo + N]), _bf(xc[:, c_lo:c_lo + N])
            cb = _nt(cm, bm)
            st, dst = sp_ref[0, g], ds_ref[g]
            stb, dstb = _bf(st), _bf(dst)
            dcm = _nt(dy_off[:, sl], stb)
            ds_ref[g] = _tn(cm, dy_off[:, sl]) + dst * cdx[:, sl]
            rr_ref[:, sl] = _nn(bm, dstb)
            yoff_ref[:, sl] = ex[:, sl] * _nn(cm, stb)
            db = _nt(xd_end[:, sl], dstb)
            dcd_ref[:, sl] = _colsum(dst * st)
            dcb = jnp.zeros((L, L), F32)
            for j in range(HPG):
                h = g * HPG + j
                hs = slice(h * P, (h + 1) * P)
                decay = jnp.where(causal, jnp.exp(jnp.where(causal, lc[:, h:h + 1] - lct[h:h + 1, :], 0.0)), 0.0)
                m = cb * decay
                dxd_ref[:, hs] = _tn(_bf(m), dyb[:, hs])
                dm = _nt(dyb[:, hs], xdb[:, hs])
                dcb = dcb + dm * decay
                gm = dm * m
                dlc_c = dlc_c + jnp.where(lane_c == h, _rowsum(gm), 0.0)
                dlc_r = dlc_r + jnp.where(sub16 == h, _colsum(gm), 0.0)
            dcbb = _bf(dcb)
            dcp_ref[:, c_lo:c_lo + N] = dcm + _nn(dcbb, bm)
            dcp_ref[:, b_lo:b_lo + N] = db + _tn(dcbb, cm)
        dxd_diag, rr = dxd_ref[...], rr_ref[...]
        tt = _headsum(rr * xd, ind) * dte16
        dlc_rt = jnp.concatenate([dlc_r, jnp.zeros((LANES - H, L), F32)], axis=0).T
        dlc = dlc_c - dlc_rt + _headsum(dyv * yoff_ref[...], ind) - tt
        dcd = _headsum(jnp.broadcast_to(dcd_ref[...], (8, SSM_INNER)), ind)[0:1]
        dlc = dlc + jnp.where(row == L - 1, _colsum(tt) + dcd * cd16, 0.0)
        dla = _tn(tri, dlc, HIGHEST)
        dxd = dxd_diag + dtex * rr
        ddt = _headsum(dxd * xs, ind) + dla * a
        ddtr = jnp.where(lane_c < H, ddt * _sigmoid(pre), 0.0)
        ddtr_ref[...] = ddtr
        acc_ref[0:1, :] += _colsum(ddtr)
        acc_ref[1:2, :] += jnp.where(lane1 < H, _colsum(dla * dt) * a, 0.0)
        acc_ref[2:3, :] += _headsum(jnp.broadcast_to(_colsum(dyv * xs), (8, SSM_INNER)), ind)[0:1]
        dcp_ref[:, 0:SSM_INNER] = dxd * dtx + dskx * dyv
        dcp_ref[...] = dcp_ref[...] * _dsilu(cpre_v)

    rev = lambda i: (nc - 1 - i, 0)
    rspec = lambda c: pl.BlockSpec((L, c), rev)
    return _call(
        body, (cpre, dtr, pm, ypre, sprev, dya, dtb, alog, dskip, normw, ind), side, name=name,
        out_shape=[_sds((T, SSM_INNER)), _sds((T, CONV_DIM)), _sds((T, LANES)), _sds((8, LANES)), _sds((1, SSM_INNER))],
        grid=(nc,),
        in_specs=[rspec(CONV_DIM), rspec(LANES), rspec(SSM_INNER), rspec(SSM_INNER),
                  pl.BlockSpec((1, SSM_GROUPS, N, half), lambda i: (nc - 1 - i, 0, 0, 0)), rspec(SSM_INNER),
                  _vec_spec(LANES), _vec_spec(LANES), _vec_spec(SSM_INNER), _vec_spec(SSM_INNER), _vec_spec(SSM_INNER, LANES)],
        out_specs=[rspec(SSM_INNER), rspec(CONV_DIM), rspec(LANES), _vec_spec(LANES, 8), _vec_spec(SSM_INNER)],
        scratch=[pltpu.VMEM((SSM_GROUPS, N, half), F32), pltpu.VMEM((L, SSM_INNER), F32), pltpu.VMEM((L, SSM_INNER), F32),
                 pltpu.VMEM((L, SSM_INNER), F32), pltpu.VMEM((L, SSM_INNER), F32), pltpu.VMEM((1, SSM_INNER), F32)],
        sem=("arbitrary",))


def _gmlp_common(u, v, lnw, lnb):
    ug = _gelu(u)
    vg = _gelu(v)
    mu = jnp.mean(vg, axis=-1, keepdims=True)
    cen = vg - mu
    rstd = lax.rsqrt(jnp.mean(cen * cen, axis=-1, keepdims=True) + EPS)
    vhat = cen * rstd
    return ug, rstd, vhat, vhat * lnw + lnb


def _causal_mask():
    row = lax.broadcasted_iota(jnp.int32, (CHUNK, CHUNK), 0)
    col = lax.broadcasted_iota(jnp.int32, (CHUNK, CHUNK), 1)
    return row >= col


def _gmlp_fwd(pm, lnw, lnb, ws, bs_exp, name):
    T = pm.shape[0]
    nc = T // CHUNK
    L, G = CHUNK, GMLP_GROUPS

    def body(u_ref, v_ref, lnw_ref, lnb_ref, ws_ref, bs_ref, o_ref):
        ug, _, _, vn = _gmlp_common(u_ref[...], v_ref[...], lnw_ref[...], lnb_ref[...])
        causal = _causal_mask()
        for g in range(G):
            sl = slice(g * L, (g + 1) * L)
            wm = _bf(jnp.where(causal, ws_ref[g], 0.0))
            sv = _nn(wm, _bf(vn[:, sl])) + bs_ref[:, sl]
            o_ref[:, sl] = (ug[:, sl] * sv).astype(BF16)

    return _pcall(
        body, name=name, out_shape=_sds((T, GMLP_INNER), BF16), grid=(nc,),
        in_specs=[_row_spec(L, GMLP_INNER, 1), _row_spec(L, GMLP_INNER, 2), _vec_spec(GMLP_INNER), _vec_spec(GMLP_INNER),
                  pl.BlockSpec((G, L, L), lambda i: (0, 0, 0)), _vec_spec(GMLP_INNER, L)],
        out_specs=_row_spec(L, GMLP_INNER), sem=("parallel",))(pm, pm, lnw, lnb, ws, bs_exp)


def _gmlp_bwd(pm, dyb, lnw, lnb, ws, bs_exp, name, side=None):
    T = pm.shape[0]
    nc = T // CHUNK
    L, G = CHUNK, GMLP_GROUPS

    def body(u_ref, v_ref, dy_ref, lnw_ref, lnb_ref, ws_ref, bs_ref, du_ref, dv_ref, dws_ref, dbs_ref, acc_ref, dvn_ref):
        @pl.when(pl.program_id(0) == 0)
        def _():
            dws_ref[...] = jnp.zeros_like(dws_ref)
            dbs_ref[...] = jnp.zeros_like(dbs_ref)
            acc_ref[...] = jnp.zeros_like(acc_ref)
        uv, vv, dyv, lnwv = u_ref[...], v_ref[...], dy_ref[...], lnw_ref[...]
        ug, rstd, vhat, vn = _gmlp_common(uv, vv, lnwv, lnb_ref[...])
        causal = _causal_mask()
        lane = lax.broadcasted_iota(jnp.int32, (L, LANES), 1)
        dbs = jnp.zeros((L, LANES), F32)
        for g in range(G):
            sl = slice(g * L, (g + 1) * L)
            wm = _bf(jnp.where(causal, ws_ref[g], 0.0))
            vng = _bf(vn[:, sl])
            sv = _nn(wm, vng) + bs_ref[:, sl]
            du_ref[:, sl] = dyv[:, sl] * sv * _dgelu(uv[:, sl])
            dsv = dyv[:, sl] * ug[:, sl]
            dsvb = _bf(dsv)
            dws_ref[g] += jnp.where(causal, _nt(dsvb, vng), 0.0)
            dbs = dbs + jnp.where(lane == g, _rowsum(dsv), 0.0)
            dvn_ref[:, sl] = _tn(wm, dsvb)
        dbs_ref[...] += dbs
        dvn = dvn_ref[...]
        acc_ref[0:1, :] += _colsum(dvn * vhat)
        acc_ref[1:2, :] += _colsum(dvn)
        dvh = dvn * lnwv
        dvg = rstd * (dvh - jnp.mean(dvh, axis=-1, keepdims=True) - vhat * jnp.mean(dvh * vhat, axis=-1, keepdims=True))
        dv_ref[...] = dvg * _dgelu(vv)

    return _call(
        body, (pm, pm, dyb, lnw, lnb, ws, bs_exp), side, name=name,
        out_shape=[_sds((T, GMLP_INNER)), _sds((T, GMLP_INNER)), _sds((G, L, L)), _sds((L, LANES)), _sds((8, GMLP_INNER))],
        grid=(nc,),
        in_specs=[_row_spec(L, GMLP_INNER, 1), _row_spec(L, GMLP_INNER, 2), _row_spec(L, GMLP_INNER),
                  _vec_spec(GMLP_INNER), _vec_spec(GMLP_INNER), pl.BlockSpec((G, L, L), lambda i: (0, 0, 0)),
                  _vec_spec(GMLP_INNER, L)],
        out_specs=[_row_spec(L, GMLP_INNER), _row_spec(L, GMLP_INNER), pl.BlockSpec((G, L, L), lambda i: (0, 0, 0)),
                   _vec_spec(LANES, L), _vec_spec(GMLP_INNER, 8)],
        scratch=[pltpu.VMEM((L, GMLP_INNER), F32)], sem=("arbitrary",))


def _rel_buckets():
    qi = np.arange(CHUNK)[:, None]
    sj = np.arange(2 * CHUNK)[None, :]
    dist = np.maximum(qi + CHUNK - sj, 0)
    max_exact = REL_BUCKETS // 2
    log_ratio = (np.log(np.maximum(dist, 1).astype(np.float32) / np.float32(max_exact))
                 / np.float32(math.log(REL_MAX_DIST / max_exact))).astype(np.float32)
    large = max_exact + (log_ratio * np.float32(REL_BUCKETS - max_exact)).astype(np.int32)
    return np.where(dist < max_exact, dist, np.minimum(large, REL_BUCKETS - 1))


def _bucket_onehot_t():
    bucket = _rel_buckets().reshape(-1)
    return jnp.asarray((np.arange(REL_BUCKETS)[:, None] == bucket[None, :]).astype(np.float32))


def _bias_from_table(table_t, onehot_t, name):
    def body(t_ref, o_ref, out_ref):
        out_ref[...] = _nn(t_ref[...], o_ref[...], HIGHEST)

    return _pcall(body, name=name, out_shape=_sds((ATTN_HEADS, onehot_t.shape[1])))(table_t, onehot_t)


def _table_from_dbias(dbias, onehot_t, name):
    def body(d_ref, o_ref, out_ref):
        out_ref[...] = _nt(d_ref[...], o_ref[...], HIGHEST)

    return _pcall(body, name=name, out_shape=_sds((ATTN_HEADS, REL_BUCKETS)))(dbias, onehot_t)


def _softmax_sink(logits, sink, mask):
    logits = jnp.where(mask, logits, NEG_INF)
    mx = jnp.maximum(jnp.max(logits, axis=-1, keepdims=True), sink)
    e = jnp.exp(logits - mx)
    es = jnp.exp(sink - mx)
    inv = 1.0 / (_rowsum(e) + es)
    return e * inv, es * inv


def _attn_mask(n, heads):
    qi = lax.broadcasted_iota(jnp.int32, (heads * CHUNK, 2 * CHUNK), 0) & (CHUNK - 1)
    sj = lax.broadcasted_iota(jnp.int32, (heads * CHUNK, 2 * CHUNK), 1)
    rel = qi + CHUNK - sj
    return (rel >= 0) & (rel < CHUNK) & ((sj >= CHUNK) | (n > 0))


def _stack_heads(ref, first, count, width):
    return jnp.concatenate([_bf(ref[:, (first + j) * width:(first + j + 1) * width]) for j in range(count)], axis=0)


def _attn_fwd(qkv, bias, sinks, name):
    T = qkv.shape[0]
    nb = T // CHUNK
    L, DH, HPK = CHUNK, ATTN_DH, ATTN_HEADS // ATTN_KV
    scale = DH ** -0.5
    kcol, vcol = ATTN_HEADS * DH // LANES, ATTN_HEADS * DH // LANES + 1

    def body(q_ref, k_ref, v_ref, kp_ref, vp_ref, bias_ref, sink_ref, o_ref, lg_ref, p_ref):
        n = pl.program_id(0)
        mask = _attn_mask(n, 1)
        kband = _bf(jnp.concatenate([kp_ref[...], k_ref[...]], axis=0))
        vband = _bf(jnp.concatenate([vp_ref[...], v_ref[...]], axis=0))
        for kv in range(ATTN_KV):
            lg_ref[...] = _nt(_stack_heads(q_ref, kv * HPK, HPK, DH), kband[:, kv * DH:(kv + 1) * DH])
            for j in range(HPK):
                h = kv * HPK + j
                p, _ = _softmax_sink(lg_ref[j * L:(j + 1) * L, :] * scale + bias_ref[h], sink_ref[h], mask)
                p_ref[j * L:(j + 1) * L, :] = _bf(p)
            og = _nn(p_ref[...], vband[:, kv * DH:(kv + 1) * DH])
            for j in range(HPK):
                h = kv * HPK + j
                o_ref[:, h * DH:(h + 1) * DH] = og[j * L:(j + 1) * L].astype(BF16)

    prev = lambda i: jnp.maximum(i - 1, 0)
    return _pcall(
        body, name=name, out_shape=_sds((T, ATTN_HEADS * DH), BF16), grid=(nb,),
        in_specs=[_row_spec(L, ATTN_HEADS * DH, 0), _row_spec(L, LANES, kcol), _row_spec(L, LANES, vcol),
                  pl.BlockSpec((L, LANES), lambda i: (prev(i), kcol)), pl.BlockSpec((L, LANES), lambda i: (prev(i), vcol)),
                  pl.BlockSpec((ATTN_HEADS, L, 2 * L), lambda i: (0, 0, 0)),
                  pl.BlockSpec(memory_space=pltpu.SMEM)],
        out_specs=_row_spec(L, ATTN_HEADS * DH),
        scratch=[pltpu.VMEM((HPK * L, 2 * L), F32), pltpu.VMEM((HPK * L, 2 * L), BF16)],
        sem=("parallel",))(qkv, qkv, qkv, qkv, qkv, bias, sinks)


def _attn_bwd(qkv, datt, bias, sinks, name):
    T = qkv.shape[0]
    nb = T // CHUNK
    L, DH, HPK = CHUNK, ATTN_DH, ATTN_HEADS // ATTN_KV
    scale = DH ** -0.5
    kcol, vcol = ATTN_HEADS * DH // LANES, ATTN_HEADS * DH // LANES + 1

    def body(q_ref, k_ref, v_ref, kp_ref, vp_ref, do_ref, bias_ref, sink_ref,
             dq_ref, dk_ref, dv_ref, dbias_ref, dsink_ref, pend_k, pend_v, band_k, band_v, lg_ref, dp_ref, p_ref, dl_ref):
        n = pl.program_id(0)

        @pl.when(n == 0)
        def _():
            dbias_ref[...] = jnp.zeros_like(dbias_ref)
            dsink_ref[...] = jnp.zeros_like(dsink_ref)

        @pl.when(n < nb)
        def _():
            mask = _attn_mask(n, 1)
            kband = _bf(jnp.concatenate([kp_ref[...], k_ref[...]], axis=0))
            vband = _bf(jnp.concatenate([vp_ref[...], v_ref[...]], axis=0))
            lane1 = lax.broadcasted_iota(jnp.int32, (1, LANES), 1)
            dsink = jnp.zeros((1, LANES), F32)
            for kv in range(ATTN_KV):
                kb, vb = kband[:, kv * DH:(kv + 1) * DH], vband[:, kv * DH:(kv + 1) * DH]
                qg = _stack_heads(q_ref, kv * HPK, HPK, DH)
                dog = _stack_heads(do_ref, kv * HPK, HPK, DH)
                lg_ref[...] = _nt(qg, kb)
                dp_ref[...] = _nt(dog, vb)
                for j in range(HPK):
                    h = kv * HPK + j
                    rows = slice(j * L, (j + 1) * L)
                    p, ps = _softmax_sink(lg_ref[rows, :] * scale + bias_ref[h], sink_ref[h], mask)
                    dp = dp_ref[rows, :]
                    delta = _rowsum(p * dp)
                    dl = p * (dp - delta)
                    dbias_ref[h] += dl
                    p_ref[rows, :] = _bf(p)
                    dl_ref[rows, :] = _bf(dl)
                    dsink = dsink + jnp.where(lane1 == h, -_colsum(ps * delta), 0.0)
                band_v[:, kv * DH:(kv + 1) * DH] = _tn(p_ref[...], dog)
                dqg = _nn(dl_ref[...], kb) * scale
                band_k[:, kv * DH:(kv + 1) * DH] = _tn(dl_ref[...], qg) * scale
                for j in range(HPK):
                    h = kv * HPK + j
                    dq_ref[:, h * DH:(h + 1) * DH] = dqg[j * L:(j + 1) * L]
            dsink_ref[...] += dsink

            @pl.when(n > 0)
            def _():
                dk_ref[...] = pend_k[...] + band_k[0:L, :]
                dv_ref[...] = pend_v[...] + band_v[0:L, :]
            pend_k[...] = band_k[L:2 * L, :]
            pend_v[...] = band_v[L:2 * L, :]

        @pl.when(n == nb)
        def _():
            dk_ref[...] = pend_k[...]
            dv_ref[...] = pend_v[...]

    cur = lambda i: jnp.minimum(i, nb - 1)
    prev = lambda i: jnp.maximum(jnp.minimum(i, nb - 1) - 1, 0)
    lag = lambda i: jnp.maximum(i - 1, 0)
    return _pcall(
        body, name=name,
        out_shape=[_sds((T, ATTN_HEADS * DH)), _sds((T, LANES)), _sds((T, LANES)), _sds((ATTN_HEADS, L, 2 * L)), _sds((1, LANES))],
        grid=(nb + 1,),
        in_specs=[pl.BlockSpec((L, ATTN_HEADS * DH), lambda i: (cur(i), 0)),
                  pl.BlockSpec((L, LANES), lambda i: (cur(i), kcol)), pl.BlockSpec((L, LANES), lambda i: (cur(i), vcol)),
                  pl.BlockSpec((L, LANES), lambda i: (prev(i), kcol)), pl.BlockSpec((L, LANES), lambda i: (prev(i), vcol)),
                  pl.BlockSpec((L, ATTN_HEADS * DH), lambda i: (cur(i), 0)),
                  pl.BlockSpec((ATTN_HEADS, L, 2 * L), lambda i: (0, 0, 0)),
                  pl.BlockSpec(memory_space=pltpu.SMEM)],
        out_specs=[pl.BlockSpec((L, ATTN_HEADS * DH), lambda i: (cur(i), 0)),
                   pl.BlockSpec((L, LANES), lambda i: (lag(i), 0)), pl.BlockSpec((L, LANES), lambda i: (lag(i), 0)),
                   pl.BlockSpec((ATTN_HEADS, L, 2 * L), lambda i: (0, 0, 0)), _vec_spec(LANES)],
        scratch=[pltpu.VMEM((L, LANES), F32), pltpu.VMEM((L, LANES), F32),
                 pltpu.VMEM((2 * L, LANES), F32), pltpu.VMEM((2 * L, LANES), F32),
                 pltpu.VMEM((HPK * L, 2 * L), F32), pltpu.VMEM((HPK * L, 2 * L), F32),
                 pltpu.VMEM((HPK * L, 2 * L), BF16), pltpu.VMEM((HPK * L, 2 * L), BF16)],
        sem=("arbitrary",))(qkv, qkv, qkv, qkv, qkv, datt, bias, sinks)


def _pad_rows(a, mult):
    pad = (-a.shape[-2]) % mult
    if pad == 0:
        return a
    cfg = [(0, 0)] * (a.ndim - 2) + [(0, pad), (0, 0)]
    return jnp.pad(a, cfg)


class _Pack:
    def __init__(self, width, mult, total_mult):
        self.width, self.mult, self.total_mult = width, mult, total_mult
        self.entries = []
        self.rows = 0

    def add(self, key, shape):
        n = int(np.prod(shape))
        rows = -(-n // self.width)
        self.entries.append((key, self.rows, rows, tuple(shape)))
        self.rows += -(-rows // self.mult) * self.mult

    @property
    def total(self):
        return -(-self.rows // self.total_mult) * self.total_mult

    def pack(self, pieces, dtype, lead=()):
        parts = []
        for key, _, rows, shape in self.entries:
            a = pieces[key].astype(dtype).reshape(lead + (-1,))
            n = int(np.prod(shape))
            a = jnp.pad(a, [(0, 0)] * len(lead) + [(0, rows * self.width - n)])
            a = a.reshape(lead + (rows, self.width))
            parts.append(_pad_rows(a, self.mult))
        out = jnp.concatenate(parts, axis=len(lead))
        return _pad_rows(out, self.total_mult)

    def unpack(self, packed, lead=()):
        out = {}
        for key, off, rows, shape in self.entries:
            a = lax.slice_in_dim(packed, off, off + rows, axis=len(lead))
            a = a.reshape(lead + (-1,))
            n = int(np.prod(shape))
            out[key] = lax.slice_in_dim(a, 0, n, axis=len(lead)).reshape(lead + shape)
        return out


def _ffn_fwd(x, mod, norm_w, wg_t, wu_t, wd, tag, gather=None):
    h = _norm_mod(x, norm_w, mod[4:5], mod[3:4], f"ffn_norm_{tag}")
    side = None if gather is None else (_GatherOps, gather)
    gate, up, act, *gathered = _mm_swiglu(h, wg_t, wu_t, f"ffn_gateup_{tag}", side=side)
    x_out, ffn_out = _mm([(act, wd)], "nn", name=f"ffn_down_{tag}", resid=x, gvec=mod[5:6], keep=True)
    return (x_out, dict(h=h, gate=gate, up=up, act=act, out=ffn_out), *gathered)


def _ffn_bwd(dx_out, x_in, saved, mod, norm_w, wg_t, wu_t, wd, tag, exchange=None):
    dffn, acc_g = _gate_bwd(dx_out, saved["out"], mod[5:6], f"ffn_gate_bwd_{tag}")
    side = None if exchange is None else (_SiblingOps, exchange)
    dgate, dup, *theirs = _mm_swiglu_bwd(dffn, wd, saved["gate"], saved["up"], f"ffn_act_bwd_{tag}", side=side)
    d_wd = _mm_tn(saved["act"], dffn, name=f"ffn_dwd_{tag}")
    d_wg_t = _mm_tn(dgate, saved["h"], name=f"ffn_dwg_{tag}")
    d_wu_t = _mm_tn(dup, saved["h"], name=f"ffn_dwu_{tag}")
    side = None if exchange is None else (_ChipsOps, _pair_sum(exchange, theirs[0], f"grads_pair_sum_{tag}"))
    dh = _mm([(dgate, wg_t), (dup, wu_t)], "nn", name=f"ffn_dh_{tag}", side=side)
    dh, *from_chips = dh if exchange is not None else (dh,)
    dx, acc_n = _norm_mod_bwd(x_in, dh, dx_out, norm_w, mod[4:5], f"ffn_norm_bwd_{tag}")
    grads = dict(d_wg=d_wg_t, d_wu=d_wu_t, d_wd=d_wd, d_g=acc_g[0], d_sc=acc_n[0], d_sh=acc_n[1], d_nw=acc_n[2])
    return (dx, grads, *from_chips)


_BIG = [
    ("out_w", "out_w_even", 0, "row"), ("qkv_w", "qkv_w", 0, "col"), ("o_w", "o_w", 0, "row"),
    ("gate0", "ffn_gate_w", 0, "col"), ("up0", "ffn_up_w", 0, "col"), ("down0", "ffn_down_w", 0, "row"),
    ("gate1", "ffn_gate_w", 1, "col"), ("up1", "ffn_up_w", 1, "col"), ("down1", "ffn_down_w", 1, "row"),
    ("in_w", "in_w_even", 0, "col"),
]


def _to_wire(a, kind):
    return a.T if kind == "col" else a


_GATHER_GROUPS = [["out_w", "in_w"], ["gate0", "up0", "down0"], ["qkv_w", "o_w", "gate1", "up1", "down1"]]

_REPLICATED = ["ada_b", "norm_mix_w", "norm_ffn_w", "conv_b", "dt_bias", "a_log", "d_skip", "ssm_norm_w", "gmlp_ln_w",
               "gmlp_ln_b", "gmlp_ws", "gmlp_bs", "sinks", "rel_table", "final_norm_w"]
_TINY_SHARDED = ["conv_w", "qkv_b", "o_b"]

_WEIGHTS = ['ada_w', 'ada_b', 'norm_mix_w', 'norm_ffn_w', 'in_w_even', 'conv_w', 'conv_b', 'dt_bias', 'a_log', 'd_skip',
            'ssm_norm_w', 'gmlp_ln_w', 'gmlp_ln_b', 'gmlp_ws', 'gmlp_bs', 'out_w_even', 'qkv_w', 'qkv_b', 'o_w', 'o_b',
            'sinks', 'rel_table', 'ffn_gate_w', 'ffn_up_w', 'ffn_down_w', 'final_norm_w']


def _step(x, c, loss_target, W, M, V):
    T = x.shape[1]
    x0 = x[0]
    target = loss_target[0]
    me = 4 * lax.axis_index("x") + 2 * lax.axis_index("y") + lax.axis_index("c")

    w_wire_local = {key: _to_wire(W[name][layer].astype(BF16), kind) for key, name, layer, kind in _BIG}
    gather_packs = []
    for keys in _GATHER_GROUPS:
        gp = _Pack(D, 1, 128)
        for key in keys:
            gp.add(key, w_wire_local[key].shape)
        gather_packs.append((gp, gp.pack(w_wire_local, BF16)))
    full = {}

    def gathered_weights(group, gathered):
        shards = gather_packs[group][0].unpack(gathered, lead=(N_DEV,))
        full.update({key: a.reshape(-1, D) for key, a in shards.items()})

    gathered_weights(0, _all_gather(gather_packs[0][1], "gather_weights"))

    small_in = _Pack(D, 8, 8)
    small_in.add("c", (1, D))
    small_in.add("conv_w", W["conv_w"][0].shape)
    small_in.add("qkv_b", W["qkv_b"][0].shape)
    small_in.add("o_b", W["o_b"][0].shape)
    sm = small_in.unpack(_all_gather(small_in.pack(
        dict(c=c, conv_w=W["conv_w"][0], qkv_b=W["qkv_b"][0], o_b=W["o_b"][0]), F32), "gather_small"), lead=(N_DEV,))
    c_all = sm["c"].reshape(N_DEV, D)
    conv_w_full = jnp.transpose(sm["conv_w"], (1, 0, 2)).reshape(SSM_CONV, CONV_DIM)
    qkv_b_full = sm["qkv_b"].reshape(1, QKV_DIM)
    o_b_full = sm["o_b"].reshape(1, D)

    ncol = W["ada_w"].shape[2]
    cond, mod_cols = _mod_matmul(c_all, W["ada_w"], "mod_matmul")
    mod_g = _all_gather(mod_cols.reshape(DEPTH * N_DEV, ncol), "gather_mod").reshape(N_DEV, DEPTH, N_DEV, ncol)
    mod_me = lax.dynamic_index_in_dim(mod_g, me, axis=2, keepdims=False)
    mod_me = jnp.transpose(mod_me, (1, 0, 2)).reshape(DEPTH, 6, D)
    mod_me = jnp.pad(mod_me, ((0, 0), (0, 2), (0, 0))).reshape(DEPTH * 8, D)
    ada_b_rows = jnp.pad(W["ada_b"].reshape(DEPTH, 6, D), ((0, 0), (0, 2), (0, 0))).reshape(DEPTH * 8, D)
    mod_all = _add_rows(mod_me, ada_b_rows, "mod_bias").reshape(DEPTH, 8, D)
    mod0, mod1 = mod_all[0], mod_all[1]

    in_t = full["in_w"]
    o1, o2, o3, o4 = SSM_INNER, SSM_INNER + CONV_DIM, SSM_INNER + CONV_DIM + SSM_HEADS, SSM_INNER + CONV_DIM + SSM_HEADS + GMLP_INNER
    w_z, w_xbc, w_dt, w_u, w_v = in_t[:o1], in_t[o1:o2], in_t[o2:o3], in_t[o3:o4], in_t[o4:]
    w_main = jnp.concatenate([w_z, w_u, w_v, w_xbc], axis=0)
    w_dtp = jnp.pad(w_dt, ((0, LANES - SSM_HEADS), (0, 0)))
    out_w = full["out_w"]

    pad16 = lambda a: jnp.pad(a.reshape(1, SSM_HEADS), ((0, 0), (0, LANES - SSM_HEADS)))
    dtb, alog = pad16(W["dt_bias"][0]), pad16(W["a_log"][0])
    dskip = jnp.repeat(W["d_skip"][0], SSM_HEAD_DIM).reshape(1, SSM_INNER)
    ssm_nw = W["ssm_norm_w"]
    lnw, lnb = W["gmlp_ln_w"], W["gmlp_ln_b"]
    ws = W["gmlp_ws"][0]
    bs_exp = jnp.repeat(W["gmlp_bs"][0].T, CHUNK, axis=1)
    conv_b = W["conv_b"]
    nmw, nfw = W["norm_mix_w"], W["norm_ffn_w"]
    onehot_t = _bucket_onehot_t()
    head_ind = _head_indicator()
    bias = _bias_from_table(W["rel_table"].T, onehot_t, "rel_bias").reshape(ATTN_HEADS, CHUNK, 2 * CHUNK)
    sinks = W["sinks"][0]

    h0 = _norm_mod(x0, nmw[0:1], mod0[1:2], mod0[0:1], "mix_norm_0")
    pm = _mm([(h0, w_main)], "nt", name="in_proj", tn_pref=1536)
    dtr = _mm([(h0, w_dtp)], "nt", name="in_proj_dt")
    cpre = _conv_fwd(pm, conv_w_full, conv_b, "conv_fwd")
    ya, ypre, sprev, gathered_b = _ssd_fwd(cpre, dtr, pm, dtb, alog, dskip, ssm_nw, head_ind, "ssd_fwd",
                                           side=(_GatherOps, gather_packs[1][1]))
    gathered_weights(1, gathered_b)
    yb = _gmlp_fwd(pm, lnw, lnb, ws, bs_exp, "gmlp_fwd")
    x1, mix0 = _mm([(ya, out_w[:SSM_INNER]), (yb, out_w[SSM_INNER:])], "nn", name="out_proj", resid=x0, gvec=mod0[2:3], keep=True)
    x2, ffn0, gathered_c = _ffn_fwd(x1, mod0, nfw[0:1], full["gate0"], full["up0"], full["down0"], "0", gather=gather_packs[2][1])
    gathered_weights(2, gathered_c)
    qkv_t, o_w = full["qkv_w"], full["o_w"]
    w_q, w_k, w_v_att = qkv_t[:D], qkv_t[D:D + LANES], qkv_t[D + LANES:]

    h1 = _norm_mod(x2, nmw[1:2], mod1[1:2], mod1[0:1], "mix_norm_1")
    qkv = _mm([(h1, qkv_t)], "nt", name="qkv_proj", bias=qkv_b_full, tn_pref=1280)
    att = _attn_fwd(qkv, bias, sinks, "attn_fwd")
    x3, mix1 = _mm([(att, o_w)], "nn", name="o_proj", bias=o_b_full, resid=x2, gvec=mod1[2:3], keep=True)
    x4, ffn1 = _ffn_fwd(x3, mod1, nfw[1:2], full["gate1"], full["up1"], full["down1"], "1")

    dx4, acc_f = _final_loss(x4, W["final_norm_w"].reshape(1, D), target, "final_loss")
    loss = lax.psum(acc_f[1, 0], ("x", "y", "c"))
    dx3, gf1 = _ffn_bwd(dx4, x3, ffn1, mod1, nfw[1:2], full["gate1"], full["up1"], full["down1"], "1")

    dmix1, acc_m1 = _gate_bwd(dx3, mix1, mod1[2:3], "mix_gate_bwd_1")
    datt = _mm([(dmix1, o_w)], "nt", name="o_proj_dx")
    d_o_w = _mm_tn(att, dmix1, name="o_proj_dw")
    dq, dk, dv, dbias, dsinks = _attn_bwd(qkv, datt, bias, sinks, "attn_bwd")
    d_table = _table_from_dbias(dbias.reshape(ATTN_HEADS, -1), onehot_t, "rel_table_grad").T
    d_qkv_t = jnp.concatenate([_mm_tn(dq, h1, name="qkv_dw_q"), _mm_tn(dk, h1, name="qkv_dw_k"), _mm_tn(dv, h1, name="qkv_dw_v")], axis=0)
    d_qkv_b = jnp.concatenate([_colsum_call(dq, "qkv_db_q"), _colsum_call(dk, "qkv_db_k"), _colsum_call(dv, "qkv_db_v")], axis=1)
    dh1 = _mm([(dq, w_q), (dk, w_k), (dv, w_v_att)], "nn", name="qkv_proj_dx")
    dx2, acc_n1 = _norm_mod_bwd(x2, dh1, dx3, nmw[1:2], mod1[1:2], "mix_norm_bwd_1")

    g_wire = dict(qkv_w=d_qkv_t, o_w=d_o_w, gate1=gf1["d_wg"], up1=gf1["d_wu"], down1=gf1["d_wd"])

    def packed_partials(group):
        return gather_packs[group][0].pack({key: g_wire[key].reshape(N_DEV, -1, D) for key in _GATHER_GROUPS[group]},
                                           F32, lead=(N_DEV,))

    from_chips = {}
    dx1, gf0, from_chips[2] = _ffn_bwd(dx2, x1, ffn0, mod0, nfw[0:1], full["gate0"], full["up0"], full["down0"], "0",
                                       exchange=packed_partials(2))
    g_wire.update(gate0=gf0["d_wg"], up0=gf0["d_wu"], down0=gf0["d_wd"])
    partials_ffn0 = packed_partials(1)

    dmix0, acc_m0 = _gate_bwd(dx1, mix0, mod0[2:3], "mix_gate_bwd_0")
    dya = _mm([(dmix0, out_w[:SSM_INNER])], "nt", name="out_proj_dx_a")
    dyb = _mm([(dmix0, out_w[SSM_INNER:])], "nt", name="out_proj_dx_b")
    d_out_w = jnp.concatenate([_mm_tn(ya, dmix0, name="out_proj_dw_a"), _mm_tn(yb, dmix0, name="out_proj_dw_b")], axis=0)
    du, dvg, d_ws, d_bs, acc_ln, theirs_ffn0 = _gmlp_bwd(pm, dyb, lnw, lnb, ws, bs_exp, "gmlp_bwd",
                                                         side=(_SiblingOps, partials_ffn0))
    pair_ffn0 = _pair_sum(partials_ffn0, theirs_ffn0, "grads_pair_sum_mix")
    dz, dcpre, ddtr, acc_ssd, d_ssm_nw, from_chips[1] = _ssd_bwd(
        cpre, dtr, pm, ypre, sprev, dya, dtb, alog, dskip, ssm_nw, head_ind, "ssd_bwd", side=(_ChipsOps, pair_ffn0))
    dxbc, acc_conv = _conv_bwd(dcpre, pm, conv_w_full, "conv_bwd")
    d_in_t = jnp.concatenate([
        _mm_tn(dz, h0, name="in_dw_z"), _mm_tn(dxbc, h0, name="in_dw_xbc"),
        _mm_tn(ddtr, h0, name="in_dw_dt")[:SSM_HEADS], _mm_tn(du, h0, name="in_dw_u"), _mm_tn(dvg, h0, name="in_dw_v")], axis=0)
    g_wire.update(in_w=d_in_t, out_w=d_out_w)
    partials_mix = packed_partials(0)
    dh0, theirs_mix = _mm([(dz, w_z), (dxbc, w_xbc), (ddtr, w_dtp), (du, w_u), (dvg, w_v)], "nn", name="in_proj_dx",
                          side=(_SiblingOps, partials_mix))
    pair_mix = _pair_sum(partials_mix, theirs_mix, "grads_pair_sum")
    grad_x, acc_n0, from_chips[0] = _norm_mod_bwd(x0, dh0, dx1, nmw[0:1], mod0[1:2], "mix_norm_bwd_0",
                                                  side=(_ChipsOps, pair_mix))

    g_mine = {}
    for group in range(len(_GATHER_GROUPS)):
        g_mine.update(gather_packs[group][0].unpack(_sum_parts(from_chips[group], f"grads_chip_sum_{group}")))
    res_big = [{}, {}, {}, {}]
    for key, name, layer, kind in _BIG:
        g_nat = _to_wire(g_mine[key], kind)
        outs = _adamw(g_nat[None], W[name][layer], M[name][layer], V[name][layer], f"adamw_{key}")
        for res, out in zip(res_big, outs):
            res[key] = out

    d_mod = jnp.stack([
        jnp.stack([acc_n0[1], acc_n0[0], acc_m0[0], gf0["d_sh"], gf0["d_sc"], gf0["d_g"]]),
        jnp.stack([acc_n1[1], acc_n1[0], acc_m1[0], gf1["d_sh"], gf1["d_sc"], gf1["d_g"]])])
    g_small = dict(
        ada_b=d_mod.reshape(DEPTH, 6 * D),
        norm_mix_w=jnp.stack([acc_n0[2], acc_n1[2]]), norm_ffn_w=jnp.stack([gf0["d_nw"], gf1["d_nw"]]),
        conv_b=acc_conv[4:5], dt_bias=acc_ssd[0:1, :SSM_HEADS], a_log=acc_ssd[1:2, :SSM_HEADS], d_skip=acc_ssd[2:3, :SSM_HEADS],
        ssm_norm_w=d_ssm_nw, gmlp_ln_w=acc_ln[0:1], gmlp_ln_b=acc_ln[1:2], gmlp_ws=d_ws[None],
        gmlp_bs=d_bs[:, :GMLP_GROUPS].T[None], sinks=dsinks[:, :ATTN_HEADS], rel_table=d_table, final_norm_w=acc_f[0],
        conv_w=acc_conv[0:SSM_CONV], qkv_b=d_qkv_b, o_b=acc_m1[1:2])
    small = _Pack(D, 1, 8)
    for name in _REPLICATED:
        small.add(name, W[name].shape)
    small.add("conv_w", (SSM_CONV, CONV_DIM))
    small.add("qkv_b", (1, QKV_DIM))
    small.add("o_b", (1, D))
    parts_small = _all_gather(small.pack(g_small, F32), "gather_small_grads")
    zeros_tiny = dict(conv_w=jnp.zeros((SSM_CONV, CONV_DIM), F32), qkv_b=jnp.zeros((1, QKV_DIM), F32), o_b=jnp.zeros((1, D), F32))
    pks = lambda S: small.pack({**{name: S[name] for name in _REPLICATED}, **zeros_tiny}, F32)
    res_small = [small.unpack(r) for r in _adamw(parts_small, pks(W), pks(M), pks(V), "adamw_small")]
    g_small_sum = res_small[0]

    n_cw, n_qb, n_ob = W["conv_w"].shape[2], W["qkv_b"].shape[1], W["o_b"].shape[1]
    g_tiny = dict(conv_w=lax.dynamic_slice_in_dim(g_small_sum["conv_w"], me * n_cw, n_cw, axis=1)[None],
                  qkv_b=lax.dynamic_slice_in_dim(g_small_sum["qkv_b"], me * n_qb, n_qb, axis=1),
                  o_b=lax.dynamic_slice_in_dim(g_small_sum["o_b"], me * n_ob, n_ob, axis=1))
    tiny = _Pack(D, 8, 8)
    for name in _TINY_SHARDED:
        tiny.add(name, W[name].shape)
    pkt = lambda S: tiny.pack({name: S[name] for name in _TINY_SHARDED}, F32)
    res_tiny = [tiny.unpack(r) for r in _adamw(pkt(g_tiny)[None], pkt(W), pkt(M), pkt(V), "adamw_tiny")]

    dmod_all = parts_small[:, small.entries[0][1]:small.entries[0][1] + small.entries[0][2]].reshape(N_DEV, DEPTH, 6 * D)
    dmod_cols = jnp.transpose(lax.dynamic_slice_in_dim(dmod_all, me * ncol, ncol, axis=2), (1, 0, 2))
    g_ada_w = _ada_w_grad(cond, dmod_cols, "ada_w_grad")
    flat = lambda a: a.reshape(DEPTH * D, ncol)
    res_ada = [r.reshape(DEPTH, D, ncol) for r in _adamw(flat(g_ada_w)[None], flat(W["ada_w"]), flat(M["ada_w"]), flat(V["ada_w"]), "adamw_ada_w")]

    def result(kind_idx, name):
        if name == "ada_w":
            return res_ada[kind_idx]
        if name in _REPLICATED:
            return res_small[kind_idx][name]
        if name in _TINY_SHARDED:
            return res_tiny[kind_idx][name]
        pieces = [res_big[kind_idx][key] for key, nm, layer, kind in _BIG if nm == name]
        return jnp.stack(pieces)

    outs = [loss, grad_x[None]]
    for kind_idx in range(4):
        outs += [result(kind_idx, name) for name in _WEIGHTS]
    return tuple(outs)


def kernel(x, c, ada_w, ada_b, norm_mix_w, norm_ffn_w, in_w_even, conv_w, conv_b, dt_bias, a_log, d_skip, ssm_norm_w, gmlp_ln_w, gmlp_ln_b, gmlp_ws, gmlp_bs, out_w_even, qkv_w, qkv_b, o_w, o_b, sinks, rel_table, ffn_gate_w, ffn_up_w, ffn_down_w, final_norm_w, loss_target, m_ada_w, m_ada_b, m_norm_mix_w, m_norm_ffn_w, m_in_w_even, m_conv_w, m_conv_b, m_dt_bias, m_a_log, m_d_skip, m_ssm_norm_w, m_gmlp_ln_w, m_gmlp_ln_b, m_gmlp_ws, m_gmlp_bs, m_out_w_even, m_qkv_w, m_qkv_b, m_o_w, m_o_b, m_sinks, m_rel_table, m_ffn_gate_w, m_ffn_up_w, m_ffn_down_w, m_final_norm_w, v_ada_w, v_ada_b, v_norm_mix_w, v_norm_ffn_w, v_in_w_even, v_conv_w, v_conv_b, v_dt_bias, v_a_log, v_d_skip, v_ssm_norm_w, v_gmlp_ln_w, v_gmlp_ln_b, v_gmlp_ws, v_gmlp_bs, v_out_w_even, v_qkv_w, v_qkv_b, v_o_w, v_o_b, v_sinks, v_rel_table, v_ffn_gate_w, v_ffn_up_w, v_ffn_down_w, v_final_norm_w):
    args = locals()
    W = {n: args[n] for n in _WEIGHTS}
    M = {n: args["m_" + n] for n in _WEIGHTS}
    V = {n: args["v_" + n] for n in _WEIGHTS}
    return _step(x, c, loss_target, W, M, V)
```

```python
import functools
import math

import numpy as np
import jax
import jax.numpy as jnp
from jax import lax
from jax.experimental import pallas as pl
from jax.experimental.pallas import tpu as pltpu

F32 = jnp.float32
BF16 = jnp.bfloat16
HIGHEST = lax.Precision.HIGHEST
MESH = pl.DeviceIdType.MESH

N_DEV = 8
D = 1024
DEPTH = 2
SSM_HEADS = 16
SSM_HEAD_DIM = 64
SSM_INNER = 1024
SSM_GROUPS = 2
SSM_STATE = 128
SSM_CONV = 4
CHUNK = 128
CONV_DIM = SSM_INNER + 2 * SSM_GROUPS * SSM_STATE
GMLP_GROUPS = 8
GMLP_INNER = 1024
IN_EVEN = 4624
ATTN_HEADS = 16
ATTN_KV = 2
ATTN_DH = 64
QKV_DIM = 1280
REL_BUCKETS = 32
REL_MAX_DIST = 128
FFN = 2816
EPS = 1e-6
NEG_INF = -1e30
LANES = 128

ADAM_LR = 0.001
ADAM_B1 = 0.9
ADAM_B2 = 0.999
ADAM_EPS = 1e-08
ADAM_WD = 0.01
ADAM_STEP = 10

VMEM_LIMIT_BYTES = 56 * 1024 * 1024
ROW_TILE = 512


def _pcall(body, *, name, out_shape, grid=(), in_specs=None, out_specs=None, scratch=(), sem=None):
    params = dict(vmem_limit_bytes=VMEM_LIMIT_BYTES)
    if sem is not None:
        params["dimension_semantics"] = sem
    specs = {} if in_specs is None else dict(in_specs=in_specs, out_specs=out_specs)
    return pl.pallas_call(
        body, name=name, out_shape=out_shape, grid=grid, **specs,
        scratch_shapes=list(scratch), compiler_params=pltpu.CompilerParams(**params))


def _call(body, args, side=None, *, name, out_shape, grid, in_specs, out_specs, scratch=(), sem=None):
    if side is None:
        return _pcall(body, name=name, out_shape=out_shape, grid=grid, in_specs=in_specs, out_specs=out_specs,
                      scratch=scratch, sem=sem)(*args)
    ops_cls, x = side
    n_in, n_out, n_scr = len(in_specs), len(out_shape), len(scratch)
    steps = int(np.prod(grid))
    hbm = pl.BlockSpec(memory_space=pl.ANY)

    def wrapped(*refs):
        ins, x_ref = refs[:n_in], refs[n_in]
        outs, r_ref = refs[n_in + 1:n_in + 1 + n_out], refs[n_in + 1 + n_out]
        scr, sems = refs[n_in + 2 + n_out:n_in + 2 + n_out + n_scr], refs[n_in + 2 + n_out + n_scr:]
        ops = ops_cls(x_ref, r_ref, *sems)
        step = pl.program_id(0)
        for axis in range(1, len(grid)):
            step = step * grid[axis] + pl.program_id(axis)
        pl.when(step == 0)(ops.start)
        body(*ins, *outs, *scr)
        pl.when(step == (3 * steps) // 4)(ops.forward)
        pl.when(step == steps - 1)(ops.finish)

    return _pcall(
        wrapped, name=name, out_shape=list(out_shape) + [ops_cls.result(x)], grid=grid,
        in_specs=list(in_specs) + [hbm], out_specs=list(out_specs) + [hbm],
        scratch=list(scratch) + ops_cls.scratch(), sem=("arbitrary",) * len(grid))(*args, x)


def _tile(n, pref):
    if n <= pref:
        return n
    best = None
    for t in range(LANES, pref + 1, LANES):
        if n % t == 0:
            best = t
    assert best is not None, (n, pref)
    return best


def _rows(T):
    return min(ROW_TILE, T)


def _sds(shape, dtype=F32):
    return jax.ShapeDtypeStruct(shape, dtype)


def _row_spec(tm, c, col=0):
    return pl.BlockSpec((tm, c), lambda i, col=col: (i, col))


def _vec_spec(c, r=1):
    return pl.BlockSpec((r, c), lambda i: (0, 0))


def _sigmoid(x):
    return jax.nn.sigmoid(x)


def _silu(x):
    return x * _sigmoid(x)


def _dsilu(x):
    s = _sigmoid(x)
    return s * (1.0 + x * (1.0 - s))


def _gelu(x):
    return 0.5 * x * (1.0 + lax.erf(x * 0.7071067811865476))


def _dgelu(x):
    return 0.5 * (1.0 + lax.erf(x * 0.7071067811865476)) + x * jnp.exp(-0.5 * x * x) * 0.3989422804014327


def _dot(a, b, dims, precision=None):
    return lax.dot_general(a, b, (dims, ((), ())), precision=precision, preferred_element_type=F32)


def _nn(a, b, precision=None):
    return _dot(a, b, ((1,), (0,)), precision)


def _nt(a, b, precision=None):
    return _dot(a, b, ((1,), (1,)), precision)


def _tn(a, b, precision=None):
    return _dot(a, b, ((0,), (0,)), precision)


def _bf(x):
    return x.astype(BF16)


def _colsum(x):
    return jnp.sum(x, axis=0, keepdims=True)


def _rowsum(x):
    return jnp.sum(x, axis=1, keepdims=True)


def _allsum(x):
    return _colsum(_rowsum(x))


def _comm_call(ops_cls, x, name, from_vmem=False):
    def body(x_ref, out_ref, *sems):
        ops = ops_cls(x_ref, out_ref, *sems)
        ops.start()
        ops.forward()
        ops.finish()

    return pl.pallas_call(
        body, name=name, out_shape=ops_cls.result(x),
        in_specs=[pl.BlockSpec(memory_space=pltpu.VMEM if from_vmem else pl.ANY)],
        out_specs=pl.BlockSpec(memory_space=pl.ANY), scratch_shapes=ops_cls.scratch(),
    )(x)


def _all_gather(x, name):
    return _comm_call(_GatherOps, x, name, from_vmem=True)


class _GatherOps:
    def __init__(self, x_ref, out_ref, send_sems, recv_sems, local_sem):
        self.x_ref, self.out_ref = x_ref, out_ref
        self.send_sems, self.recv_sems, self.local_sem = send_sems, recv_sems, local_sem
        mx, my, mc = lax.axis_index("x"), lax.axis_index("y"), lax.axis_index("c")
        self.mc = mc
        self.me, self.sibling = (mx, my, mc), (mx, my, 1 - mc)
        self.chips = [(1 - mx, my), (mx, 1 - my), (1 - mx, 1 - my)]

    @staticmethod
    def result(x):
        return _sds((N_DEV,) + x.shape, x.dtype)

    @staticmethod
    def scratch():
        return [pltpu.SemaphoreType.DMA((7,)), pltpu.SemaphoreType.DMA((7,)), pltpu.SemaphoreType.DMA(())]

    def _slot(self, px, py, pc):
        return self.out_ref.at[4 * px + 2 * py + pc]

    def _copy(self, k, block, to, own=False):
        return pltpu.make_async_remote_copy(
            src_ref=self.x_ref if own else self._slot(*block), dst_ref=self._slot(*block),
            send_sem=self.send_sems.at[k], recv_sem=self.recv_sems.at[k], device_id=to, device_id_type=MESH)

    def _mine(self):
        return pltpu.make_async_copy(self.x_ref, self._slot(*self.me), self.local_sem)

    def _first(self):
        return [self._copy(0, self.me, self.sibling, own=True)] + [
            self._copy(1 + j, self.me, (*chip, self.mc), own=True) for j, chip in enumerate(self.chips)]

    def _passed(self):
        return [self._copy(4 + j, (*chip, self.mc), self.sibling) for j, chip in enumerate(self.chips)]

    def start(self):
        self._mine().start()
        for cp in self._first():
            cp.start()

    def forward(self):
        passed = self._passed()
        for j, chip in enumerate(self.chips):
            self._copy(1 + j, (*chip, self.mc), self.me).wait_recv()
            passed[j].start()

    def finish(self):
        self._copy(0, self.sibling, self.me).wait_recv()
        for j, chip in enumerate(self.chips):
            self._copy(4 + j, (*chip, 1 - self.mc), self.me).wait_recv()
        for cp in self._first() + self._passed():
            cp.wait_send()
        self._mine().wait()


N_CHIP = 4


class _SiblingOps:
    def __init__(self, p_ref, theirs_ref, send_sems, recv_sems):
        self.p_ref, self.theirs_ref, self.send_sems, self.recv_sems = p_ref, theirs_ref, send_sems, recv_sems

    @staticmethod
    def result(p):
        return _sds((N_CHIP,) + p.shape[1:], p.dtype)

    @staticmethod
    def scratch():
        return [pltpu.SemaphoreType.DMA((N_CHIP,))] * 2

    def _copies(self):
        mx, my, mc = lax.axis_index("x"), lax.axis_index("y"), lax.axis_index("c")
        return [pltpu.make_async_remote_copy(
            src_ref=self.p_ref.at[2 * chip + 1 - mc], dst_ref=self.theirs_ref.at[chip],
            send_sem=self.send_sems.at[chip], recv_sem=self.recv_sems.at[chip],
            device_id=(mx, my, 1 - mc), device_id_type=MESH) for chip in range(N_CHIP)]

    def start(self):
        for cp in self._copies():
            cp.start()

    def forward(self):
        pass

    def finish(self):
        for cp in self._copies():
            cp.wait()


class _ChipsOps:
    def __init__(self, q_ref, out_ref, send_sems, recv_sems, local_sem):
        self.q_ref, self.out_ref = q_ref, out_ref
        self.send_sems, self.recv_sems, self.local_sem = send_sems, recv_sems, local_sem

    @staticmethod
    def result(q):
        return _sds(q.shape, q.dtype)

    @staticmethod
    def scratch():
        return [pltpu.SemaphoreType.DMA((N_CHIP - 1,)), pltpu.SemaphoreType.DMA((N_CHIP - 1,)), pltpu.SemaphoreType.DMA(())]

    def _copies(self):
        mx, my, mc = lax.axis_index("x"), lax.axis_index("y"), lax.axis_index("c")
        me = 2 * mx + my
        local = pltpu.make_async_copy(self.q_ref.at[me], self.out_ref.at[me], self.local_sem)
        remote = []
        for r in range(1, N_CHIP):
            px = 1 - mx if r & 2 else mx
            py = 1 - my if r & 1 else my
            remote.append(pltpu.make_async_remote_copy(
                src_ref=self.q_ref.at[2 * px + py], dst_ref=self.out_ref.at[me],
                send_sem=self.send_sems.at[r - 1], recv_sem=self.recv_sems.at[r - 1],
                device_id=(px, py, mc), device_id_type=MESH))
        return local, remote

    def start(self):
        local, remote = self._copies()
        local.start()
        for cp in remote:
            cp.start()

    def forward(self):
        pass

    def finish(self):
        local, remote = self._copies()
        for cp in remote:
            cp.wait()
        local.wait()


def _pair_sum(p, theirs, name):
    n, R, C = theirs.shape
    tr = _tile_rows(R, 256)

    def body(p_ref, t_ref, o_ref):
        mc = lax.axis_index("c")
        o_ref[0] = (p_ref[0, mc] + t_ref[0]).astype(BF16)

    blk = pl.BlockSpec((1, tr, C), lambda s, i: (s, i, 0))
    return _pcall(body, name=name, out_shape=_sds((n, R, C), BF16), grid=(n, R // tr),
                  in_specs=[pl.BlockSpec((1, 2, tr, C), lambda s, i: (s, 0, i, 0)), blk],
                  out_specs=blk, sem=("parallel", "parallel"))(p.reshape(n, 2, R, C), theirs)


def _sum_parts(parts, name):
    P, R, C = parts.shape
    tr = _tile_rows(R, 256)

    def body(p_ref, o_ref):
        g = p_ref[0].astype(F32)
        for k in range(1, P):
            g = g + p_ref[k].astype(F32)
        o_ref[...] = g

    return _pcall(body, name=name, out_shape=_sds((R, C)), grid=(R // tr,),
                  in_specs=[pl.BlockSpec((P, tr, C), lambda i: (0, i, 0))],
                  out_specs=pl.BlockSpec((tr, C), lambda i: (i, 0)), sem=("parallel",))(parts)


def _adamw(parts, w, m, v, name):
    P, R, C = parts.shape
    tr = R if R <= 256 else _tile_rows(R, 256)

    def body(p_ref, w_ref, m_ref, v_ref, g_ref, d_ref, nm_ref, nv_ref):
        g = p_ref[0]
        for k in range(1, P):
            g = g + p_ref[k]
        nm = ADAM_B1 * m_ref[...] + (1.0 - ADAM_B1) * g
        nv = ADAM_B2 * v_ref[...] + (1.0 - ADAM_B2) * (g * g)
        m_hat = nm / (1.0 - ADAM_B1 ** ADAM_STEP)
        v_hat = nv / (1.0 - ADAM_B2 ** ADAM_STEP)
        g_ref[...] = g
        d_ref[...] = -ADAM_LR * (m_hat / (jnp.sqrt(v_hat) + ADAM_EPS) + ADAM_WD * w_ref[...])
        nm_ref[...] = nm
        nv_ref[...] = nv

    blk = pl.BlockSpec((tr, C), lambda i: (i, 0))
    return _pcall(
        body, name=name, out_shape=[_sds((R, C))] * 4, grid=(R // tr,),
        in_specs=[pl.BlockSpec((P, tr, C), lambda i: (0, i, 0)), blk, blk, blk],
        out_specs=[blk] * 4, sem=("parallel",))(parts, w, m, v)


def _tile_rows(n, pref):
    best = None
    for t in range(8, pref + 1, 8):
        if n % t == 0:
            best = t
    assert best is not None, (n, pref)
    return best


def _mm(pairs, mode, *, name, out_dtype=F32, bias=None, tn_pref=1024, side=None):
    M = pairs[0][0].shape[0]
    N = pairs[0][1].shape[1] if mode == "nn" else pairs[0][1].shape[0]
    tm, tn = _rows(M), _tile(N, tn_pref)
    n_pairs = len(pairs)
    has_bias = bias is not None

    def body(*refs):
        acc = _pairs_dot(refs[:2 * n_pairs], mode)
        if has_bias:
            acc = acc + refs[2 * n_pairs][...]
        refs[-1][...] = acc.astype(refs[-1].dtype)

    in_specs, args = _pair_specs(pairs, mode, tm, tn)
    if has_bias:
        in_specs.append(pl.BlockSpec((1, tn), lambda j, i: (0, j)))
        args.append(bias)
    res = _call(body, args, side, name=name, out_shape=[_sds((M, N), out_dtype)], grid=(N // tn, M // tm),
                in_specs=in_specs, out_specs=[pl.BlockSpec((tm, tn), lambda j, i: (i, j))], sem=("parallel", "parallel"))
    return res[0] if side is None else (res[0], res[1])


def _pairs_dot(ab, mode):
    acc = None
    for p in range(len(ab) // 2):
        a, b = _bf(ab[2 * p][...]), _bf(ab[2 * p + 1][...])
        d = _nn(a, b) if mode == "nn" else _nt(a, b)
        acc = d if acc is None else acc + d
    return acc


def _pair_specs(pairs, mode, tm, tn):
    in_specs, args = [], []
    for a, b in pairs:
        K = a.shape[1]
        in_specs.append(pl.BlockSpec((tm, K), lambda j, i: (i, 0)))
        if mode == "nn":
            in_specs.append(pl.BlockSpec((K, tn), lambda j, i: (0, j)))
        else:
            in_specs.append(pl.BlockSpec((tn, K), lambda j, i: (j, 0)))
        args += [a, b]
    return in_specs, args


def _mm_resid(pairs, resid, gvec, *, name, bias=None, norm=None):
    M, N = resid.shape
    tm = _rows(M)
    n_pairs = len(pairs)
    has_bias, has_norm = bias is not None, norm is not None

    def body(*refs):
        acc = _pairs_dot(refs[:2 * n_pairs], "nn")
        pos = 2 * n_pairs
        if has_bias:
            acc = acc + refs[pos][...]
            pos += 1
        xv = refs[pos][...] + refs[pos + 1][...] * acc
        outs = refs[pos + 2 + 3 * has_norm:]
        outs[0][...] = xv
        outs[1][...] = acc.astype(BF16)
        if has_norm:
            w_ref, sc_ref, sh_ref = refs[pos + 2:pos + 5]
            r = lax.rsqrt(jnp.mean(xv * xv, axis=-1, keepdims=True) + EPS)
            outs[2][...] = ((xv * r * w_ref[...]) * (1.0 + sc_ref[...]) + sh_ref[...]).astype(BF16)

    in_specs, args = _pair_specs(pairs, "nn", tm, N)
    vec = pl.BlockSpec((1, N), lambda j, i: (0, 0))
    row = pl.BlockSpec((tm, N), lambda j, i: (i, 0))
    if has_bias:
        in_specs.append(vec)
        args.append(bias)
    in_specs += [row, vec] + [vec] * (3 * has_norm)
    args += [resid, gvec] + (list(norm) if has_norm else [])
    return _pcall(body, name=name, out_shape=[_sds((M, N)), _sds((M, N), BF16)] + [_sds((M, N), BF16)] * has_norm,
                  grid=(1, M // tm), in_specs=in_specs, out_specs=[row] * (2 + has_norm),
                  sem=("parallel", "parallel"))(*args)


def _mm_tn(a, b, *, name, tm_pref=1408, tn_pref=1536):
    K, M = a.shape
    N = b.shape[1]
    tm, tn = _tile(M, tm_pref), _tile(N, tn_pref)
    tk = K if K <= 2 * ROW_TILE else 2 * ROW_TILE

    def body(a_ref, b_ref, o_ref):
        @pl.when(pl.program_id(2) == 0)
        def _():
            o_ref[...] = jnp.zeros_like(o_ref)
        o_ref[...] += _tn(_bf(a_ref[...]), _bf(b_ref[...]))

    return _pcall(
        body, name=name, out_shape=_sds((M, N)), grid=(M // tm, N // tn, K // tk),
        in_specs=[pl.BlockSpec((tk, tm), lambda i, j, k: (k, i)), pl.BlockSpec((tk, tn), lambda i, j, k: (k, j))],
        out_specs=pl.BlockSpec((tm, tn), lambda i, j, k: (i, j)),
        sem=("parallel", "parallel", "arbitrary"))(a, b)


def _mm_swiglu(h, wg_t, wu_t, name, side=None):
    M, K = h.shape
    N = wg_t.shape[0]
    tm, tn = _rows(M), _tile(N, 1408)

    def body(h_ref, wg_ref, wu_ref, gate_ref, up_ref, act_ref):
        hv = _bf(h_ref[...])
        gate = _nt(hv, wg_ref[...])
        up = _nt(hv, wu_ref[...])
        gate_ref[...] = gate.astype(BF16)
        up_ref[...] = up.astype(BF16)
        act_ref[...] = (_silu(gate) * up).astype(BF16)

    w_spec = pl.BlockSpec((tn, K), lambda j, i: (j, 0))
    o_spec = pl.BlockSpec((tm, tn), lambda j, i: (i, j))
    return _call(body, (h, wg_t, wu_t), side, name=name,
                 out_shape=[_sds((M, N), BF16)] * 3, grid=(N // tn, M // tm),
                 in_specs=[pl.BlockSpec((tm, K), lambda j, i: (i, 0)), w_spec, w_spec], out_specs=[o_spec] * 3,
                 sem=("parallel", "parallel"))


def _mm_swiglu_bwd(dout, wd, gate, up, name, side=None):
    M, K = dout.shape
    N = wd.shape[0]
    tm, tn = _rows(M), _tile(N, 1408)

    def body(d_ref, wd_ref, gate_ref, up_ref, dg_ref, du_ref):
        dact = _nt(_bf(d_ref[...]), wd_ref[...])
        g = gate_ref[...].astype(F32)
        dg_ref[...] = (dact * up_ref[...].astype(F32) * _dsilu(g)).astype(BF16)
        du_ref[...] = (dact * _silu(g)).astype(BF16)

    t_spec = pl.BlockSpec((tm, tn), lambda j, i: (i, j))
    return _call(
        body, (dout, wd, gate, up), side, name=name, out_shape=[_sds((M, N), BF16)] * 2, grid=(N // tn, M // tm),
        in_specs=[pl.BlockSpec((tm, K), lambda j, i: (i, 0)), pl.BlockSpec((tn, K), lambda j, i: (j, 0)), t_spec, t_spec],
        out_specs=[t_spec] * 2, sem=("parallel", "parallel"))


def _norm_mod(x, w, sc, sh, name):
    T = x.shape[0]
    tm = _rows(T)

    def body(x_ref, w_ref, sc_ref, sh_ref, o_ref):
        xv = x_ref[...]
        r = lax.rsqrt(jnp.mean(xv * xv, axis=-1, keepdims=True) + EPS)
        o_ref[...] = ((xv * r * w_ref[...]) * (1.0 + sc_ref[...]) + sh_ref[...]).astype(BF16)

    return _pcall(body, name=name, out_shape=_sds((T, D), BF16), grid=(T // tm,),
                  in_specs=[_row_spec(tm, D), _vec_spec(D), _vec_spec(D), _vec_spec(D)],
                  out_specs=_row_spec(tm, D), sem=("parallel",))(x, w, sc, sh)


def _gate_rows(dxv, br_ref, g_ref, db_ref, acc_ref):
    db = g_ref[...] * dxv
    db_ref[...] = db.astype(BF16)
    acc_ref[3:4, :] += _colsum(dxv * br_ref[...].astype(F32))
    acc_ref[4:5, :] += _colsum(db)


def _norm_mod_bwd(x, dh, dres, w, sc, branch, g, name, side=None):
    T = x.shape[0]
    tm = _rows(T)
    gated = branch is not None

    def body(x_ref, dh_ref, dres_ref, w_ref, sc_ref, *rest):
        (br_ref, g_ref, dx_ref, db_ref, acc_ref) = rest if gated else (None, None, rest[0], None, rest[1])

        @pl.when(pl.program_id(0) == 0)
        def _():
            acc_ref[...] = jnp.zeros_like(acc_ref)
        xv, dh_v, wv = x_ref[...], dh_ref[...], w_ref[...]
        r = lax.rsqrt(jnp.mean(xv * xv, axis=-1, keepdims=True) + EPS)
        n = xv * r
        dnw = dh_v * (1.0 + sc_ref[...])
        dn = dnw * wv
        dxv = dres_ref[...] + r * (dn - n * jnp.mean(dn * n, axis=-1, keepdims=True))
        dx_ref[...] = dxv
        acc_ref[0:1, :] += _colsum(dh_v * (n * wv))
        acc_ref[1:2, :] += _colsum(dh_v)
        acc_ref[2:3, :] += _colsum(dnw * n)
        if gated:
            _gate_rows(dxv, br_ref, g_ref, db_ref, acc_ref)

    row = _row_spec(tm, D)
    args = (x, dh, dres, w, sc) + ((branch, g) if gated else ())
    return _call(body, args, side, name=name,
                 out_shape=[_sds((T, D))] + ([_sds((T, D), BF16)] if gated else []) + [_sds((8, D))], grid=(T // tm,),
                 in_specs=[row, row, row, _vec_spec(D), _vec_spec(D)] + ([row, _vec_spec(D)] if gated else []),
                 out_specs=[row] + ([row] if gated else []) + [_vec_spec(D, 8)], sem=("arbitrary",))


def _final_loss(x, wf, target, branch, g, name):
    T = x.shape[0]
    tm = _rows(T)

    def body(x_ref, w_ref, t_ref, br_ref, g_ref, dx_ref, db_ref, acc_ref):
        @pl.when(pl.program_id(0) == 0)
        def _():
            acc_ref[...] = jnp.zeros_like(acc_ref)
        xv, wv = x_ref[...], w_ref[...]
        r = lax.rsqrt(jnp.mean(xv * xv, axis=-1, keepdims=True) + EPS)
        n = xv * r
        err = n * wv - t_ref[...]
        dy = err * (1.0 / D)
        dn = dy * wv
        dxv = r * (dn - n * jnp.mean(dn * n, axis=-1, keepdims=True))
        dx_ref[...] = dxv
        acc_ref[0:1, :] += _colsum(dy * n)
        acc_ref[1:2, :] += jnp.broadcast_to(_allsum(err * err) * (0.5 / D), (1, D))
        _gate_rows(dxv, br_ref, g_ref, db_ref, acc_ref)

    row = _row_spec(tm, D)
    return _pcall(body, name=name, out_shape=[_sds((T, D)), _sds((T, D), BF16), _sds((8, D))], grid=(T // tm,),
                  in_specs=[row, _vec_spec(D), row, row, _vec_spec(D)],
                  out_specs=[row, row, _vec_spec(D, 8)], sem=("arbitrary",))(x, wf, target, branch, g)


def _colsum_call(x, name):
    T, C = x.shape
    tm = _rows(T)

    def body(x_ref, o_ref):
        @pl.when(pl.program_id(0) == 0)
        def _():
            o_ref[...] = jnp.zeros_like(o_ref)
        o_ref[...] += _colsum(x_ref[...].astype(F32))

    return _pcall(body, name=name, out_shape=_sds((1, C)), grid=(T // tm,), in_specs=[_row_spec(tm, C)],
                  out_specs=_vec_spec(C), sem=("arbitrary",))(x)


def _mod_matmul(c_all, ada_w, name):
    n = ada_w.shape[2]

    def body(c_ref, w_ref, cond_ref, o_ref):
        cond = _silu(c_ref[...])
        cond_ref[...] = cond
        o_ref[0] = _nn(cond, w_ref[0])

    return _pcall(body, name=name, out_shape=[_sds((N_DEV, D)), _sds((DEPTH, N_DEV, n))], grid=(DEPTH,),
                  in_specs=[pl.BlockSpec((N_DEV, D), lambda l: (0, 0)), pl.BlockSpec((1, D, n), lambda l: (l, 0, 0))],
                  out_specs=[pl.BlockSpec((N_DEV, D), lambda l: (0, 0)), pl.BlockSpec((1, N_DEV, n), lambda l: (l, 0, 0))],
                  sem=("arbitrary",))(c_all, ada_w)


def _add_rows(a, b, name):
    def body(a_ref, b_ref, o_ref):
        o_ref[...] = a_ref[...] + b_ref[...]

    return _pcall(body, name=name, out_shape=_sds(a.shape))(a, b)


def _ada_w_grad(cond, dmod_cols, name):
    n = dmod_cols.shape[2]

    def body(c_ref, d_ref, o_ref):
        o_ref[0] = _tn(c_ref[...], d_ref[0])

    return _pcall(body, name=name, out_shape=_sds((DEPTH, D, n)), grid=(DEPTH,),
                  in_specs=[pl.BlockSpec((N_DEV, D), lambda l: (0, 0)), pl.BlockSpec((1, N_DEV, n), lambda l: (l, 0, 0))],
                  out_specs=pl.BlockSpec((1, D, n), lambda l: (l, 0, 0)), sem=("parallel",))(cond, dmod_cols)


def _conv_fwd(pm, conv_w, conv_b, name):
    T = pm.shape[0]
    tm = _rows(T)
    C = CONV_DIM

    def body(x_ref, prev_ref, w_ref, b_ref, o_ref):
        cur = x_ref[...]
        prev = jnp.where(pl.program_id(0) > 0, prev_ref[...], 0.0)
        cur8 = cur[0:8]
        row8 = lax.broadcasted_iota(jnp.int32, (8, C), 0)
        full = w_ref[3:4, :] * cur
        head = w_ref[3:4, :] * cur8
        for k in range(1, SSM_CONV):
            wk = w_ref[3 - k:4 - k, :]
            full = full + wk * pltpu.roll(cur, k, 0)
            head = head + wk * jnp.where(row8 < k, pltpu.roll(prev, k, 0), pltpu.roll(cur8, k, 0))
        o_ref[...] = full + b_ref[...]
        o_ref[0:8, :] = head + b_ref[...]

    return _pcall(
        body, name=name, out_shape=_sds((T, C)), grid=(T // tm,),
        in_specs=[pl.BlockSpec((tm, C), lambda i: (i, 2)),
                  pl.BlockSpec((8, C), lambda i: (jnp.maximum(i * (tm // 8) - 1, 0), 2)),
                  _vec_spec(C, SSM_CONV), _vec_spec(C)],
        out_specs=_row_spec(tm, C), sem=("parallel",))(pm, pm, conv_w, conv_b)


def _conv_bwd(dc, pm, conv_w, name):
    T = dc.shape[0]
    tm = _rows(T)
    C = CONV_DIM
    nt = T // tm

    def body(dc_ref, nxt_ref, x_ref, prev_ref, w_ref, dx_ref, acc_ref):
        i = pl.program_id(0)

        @pl.when(i == 0)
        def _():
            acc_ref[...] = jnp.zeros_like(acc_ref)
        dcv = dc_ref[...]
        nxt = jnp.where(i < nt - 1, nxt_ref[...], 0.0)
        xc = x_ref[...]
        prev = jnp.where(i > 0, prev_ref[...], 0.0)
        dc8h, dc8t, x8 = dcv[0:8], dcv[tm - 8:tm], xc[0:8]
        row8 = lax.broadcasted_iota(jnp.int32, (8, C), 0)
        full = w_ref[3:4, :] * dcv
        tail = w_ref[3:4, :] * dc8t
        acc_ref[3:4, :] += _colsum(dcv * xc)
        for k in range(1, SSM_CONV):
            wk = w_ref[3 - k:4 - k, :]
            full = full + wk * pltpu.roll(dcv, tm - k, 0)
            tail = tail + wk * jnp.where(row8 + k >= 8, pltpu.roll(nxt, 8 - k, 0), pltpu.roll(dc8t, 8 - k, 0))
            xs_head = jnp.where(row8 < k, pltpu.roll(prev, k, 0), pltpu.roll(x8, k, 0))
            prod = dcv * pltpu.roll(xc, k, 0)
            acc_ref[3 - k:4 - k, :] += _colsum(prod) - _colsum(prod[0:8]) + _colsum(dc8h * xs_head)
        acc_ref[4:5, :] += _colsum(dcv)
        dx_ref[...] = full
        dx_ref[tm - 8:tm, :] = tail

    return _pcall(
        body, name=name, out_shape=[_sds((T, C)), _sds((8, C))], grid=(nt,),
        in_specs=[_row_spec(tm, C),
                  pl.BlockSpec((8, C), lambda i: (jnp.minimum((i + 1) * (tm // 8), T // 8 - 1), 0)),
                  pl.BlockSpec((tm, C), lambda i: (i, 2)),
                  pl.BlockSpec((8, C), lambda i: (jnp.maximum(i * (tm // 8) - 1, 0), 2)),
                  _vec_spec(C, SSM_CONV)],
        out_specs=[_row_spec(tm, C), _vec_spec(C, 8)], sem=("arbitrary",))(dc, dc, pm, pm, conv_w)


def _ssd_prologue(cpre, dtr, dtb, alog):
    L = CHUNK
    xc = _silu(cpre)
    pre = dtr + dtb
    dt = jnp.maximum(pre, 0.0) + jnp.log1p(jnp.exp(-jnp.abs(pre)))
    a = -jnp.exp(alog)
    la = dt * a
    row = lax.broadcasted_iota(jnp.int32, (L, L), 0)
    col = lax.broadcasted_iota(jnp.int32, (L, L), 1)
    causal = row >= col
    tri = causal.astype(F32)
    lc = _nn(tri, la, HIGHEST)
    return xc, pre, dt, a, causal, tri, lc, row, col


def _head_indicator():
    m = np.zeros((LANES, SSM_INNER), np.float32)
    for h in range(SSM_HEADS):
        m[h, h * SSM_HEAD_DIM:(h + 1) * SSM_HEAD_DIM] = 1.0
    return jnp.asarray(m, dtype=BF16)


def _split_dot(x, ind, dims):
    hi = x.astype(BF16)
    lo = (x - hi.astype(F32)).astype(BF16)
    return _dot(hi, ind, dims) + _dot(lo, ind, dims)


def _expand(x16, ind):
    return _split_dot(x16, ind, ((1,), (0,)))


def _headsum(x, ind):
    return _split_dot(x, ind, ((1,), (1,)))


def _ssd_fwd(cpre, dtr, pm, dtb, alog, dskip, normw, ind, name, side=None):
    T = cpre.shape[0]
    nc = T // CHUNK
    L, P, H, HPG, N = CHUNK, SSM_HEAD_DIM, SSM_HEADS, SSM_HEADS // SSM_GROUPS, SSM_STATE
    half = SSM_INNER // SSM_GROUPS

    def body(cp_ref, dtr_ref, z_ref, dtb_ref, alog_ref, dskip_ref, nw_ref, ind_ref, ya_ref, y_ref, sp_ref, st_ref):
        @pl.when(pl.program_id(0) == 0)
        def _():
            st_ref[...] = jnp.zeros_like(st_ref)
        xc, _, dt, _, causal, _, lc, _, _ = _ssd_prologue(cp_ref[...], dtr_ref[...], dtb_ref[...], alog_ref[...])
        lct = lc.T
        ind = ind_ref[...]
        llast = lc[L - 1:L, :]
        xs = xc[:, :SSM_INNER]
        xd = xs * _expand(dt, ind)
        ex = _expand(jnp.exp(lc), ind)
        xd_end = _bf(xd * _expand(jnp.exp(llast - lc), ind))
        cdx = _expand(jnp.broadcast_to(jnp.exp(llast), (8, LANES)), ind)[0:1]
        xdb = _bf(xd)
        sp_ref[0] = st_ref[...]
        for g in range(SSM_GROUPS):
            sl = slice(g * half, (g + 1) * half)
            bm = _bf(xc[:, SSM_INNER + g * N:SSM_INNER + (g + 1) * N])
            cm = _bf(xc[:, SSM_INNER + (SSM_GROUPS + g) * N:SSM_INNER + (SSM_GROUPS + g + 1) * N])
            cb = _nt(cm, bm)
            st = st_ref[g]
            y_ref[:, sl] = ex[:, sl] * _nn(cm, _bf(st)) + dskip_ref[:, sl] * xs[:, sl]
            st_ref[g] = st * cdx[:, sl] + _tn(bm, xd_end[:, sl])
            for j in range(HPG):
                h = g * HPG + j
                decay = jnp.where(causal, jnp.exp(jnp.where(causal, lc[:, h:h + 1] - lct[h:h + 1, :], 0.0)), 0.0)
                y_ref[:, h * P:(h + 1) * P] += _nn(_bf(cb * decay), xdb[:, h * P:(h + 1) * P])
        y2 = y_ref[...] * _silu(z_ref[...])
        for g in range(SSM_GROUPS):
            yg = y2[:, g * half:(g + 1) * half]
            r = lax.rsqrt(jnp.mean(yg * yg, axis=-1, keepdims=True) + EPS)
            ya_ref[:, g * half:(g + 1) * half] = (yg * r * nw_ref[:, g * half:(g + 1) * half]).astype(BF16)

    return _call(
        body, (cpre, dtr, pm, dtb, alog, dskip, normw, ind), side, name=name,
        out_shape=[_sds((T, SSM_INNER), BF16), _sds((T, SSM_INNER)), _sds((nc, SSM_GROUPS, N, half))], grid=(nc,),
        in_specs=[_row_spec(L, CONV_DIM), _row_spec(L, LANES), _row_spec(L, SSM_INNER, 0),
                  _vec_spec(LANES), _vec_spec(LANES), _vec_spec(SSM_INNER), _vec_spec(SSM_INNER), _vec_spec(SSM_INNER, LANES)],
        out_specs=[_row_spec(L, SSM_INNER), _row_spec(L, SSM_INNER),
                   pl.BlockSpec((1, SSM_GROUPS, N, half), lambda i: (i, 0, 0, 0))],
        scratch=[pltpu.VMEM((SSM_GROUPS, N, half), F32)], sem=("arbitrary",))


def _ssd_bwd(cpre, dtr, pm, ypre, sprev, dya, dtb, alog, dskip, normw, ind, name, side=None):
    T = cpre.shape[0]
    nc = T // CHUNK
    L, P, H, HPG, N = CHUNK, SSM_HEAD_DIM, SSM_HEADS, SSM_HEADS // SSM_GROUPS, SSM_STATE
    half = SSM_INNER // SSM_GROUPS

    def body(cp_ref, dtr_ref, z_ref, y_ref, sp_ref, dya_ref, dtb_ref, alog_ref, dskip_ref, nw_ref, ind_ref,
             dz_ref, dcp_ref, ddtr_ref, acc_ref, dnw_ref, ds_ref, dy_ref, dxd_ref, rr_ref, yoff_ref, dcd_ref):
        @pl.when(pl.program_id(0) == 0)
        def _():
            ds_ref[...] = jnp.zeros_like(ds_ref)
            acc_ref[...] = jnp.zeros_like(acc_ref)
            dnw_ref[...] = jnp.zeros_like(dnw_ref)
        cpre_v = cp_ref[...]
        xc, pre, dt, a, causal, tri, lc, row, col = _ssd_prologue(cpre_v, dtr_ref[...], dtb_ref[...], alog_ref[...])
        lct = lc.T
        zv, yv = z_ref[...], y_ref[...]
        sz = _silu(zv)
        y2 = yv * sz
        dya_v = dya_ref[...]
        nwv = nw_ref[...]
        for g in range(SSM_GROUPS):
            sl = slice(g * half, (g + 1) * half)
            yg = y2[:, sl]
            r = lax.rsqrt(jnp.mean(yg * yg, axis=-1, keepdims=True) + EPS)
            nrm = yg * r
            dnw_ref[:, sl] += _colsum(dya_v[:, sl] * nrm)
            dn = dya_v[:, sl] * nwv[:, sl]
            dy2 = r * (dn - nrm * jnp.mean(dn * nrm, axis=-1, keepdims=True))
            dy_ref[:, sl] = dy2 * sz[:, sl]
            dz_ref[:, sl] = dy2 * yv[:, sl] * _dsilu(zv[:, sl])
        ind = ind_ref[...]
        llast = lc[L - 1:L, :]
        dte16 = jnp.exp(llast - lc)
        cd16 = jnp.exp(llast)
        xs = xc[:, :SSM_INNER]
        dtx = _expand(dt, ind)
        ex = _expand(jnp.exp(lc), ind)
        dtex = _expand(dte16, ind)
        cdx = _expand(jnp.broadcast_to(cd16, (8, LANES)), ind)[0:1]
        xd = xs * dtx
        xdb = _bf(xd)
        xd_end = _bf(xd * dtex)
        dyv = dy_ref[...]
        dy_off = _bf(ex * dyv)
        dyb = _bf(dyv)
        dskx = dskip_ref[...]
        lane_c = lax.broadcasted_iota(jnp.int32, (L, LANES), 1)
        lane1 = lax.broadcasted_iota(jnp.int32, (1, LANES), 1)
        sub16 = lax.broadcasted_iota(jnp.int32, (H, L), 0)
        dlc_c = jnp.zeros((L, LANES), F32)
        dlc_r = jnp.zeros((H, L), F32)
        for g in range(SSM_GROUPS):
            sl = slice(g * half, (g + 1) * half)
            b_lo = SSM_INNER + g * N
            c_lo = SSM_INNER + (SSM_GROUPS + g) * N
            bm, cm = _bf(xc[:, b_lo:b_lo + N]), _bf(xc[:, c_lo:c_lo + N])
            cb = _nt(cm, bm)
            st, dst = sp_ref[0, g], ds_ref[g]
            stb, dstb = _bf(st), _bf(dst)
            dcm = _nt(dy_off[:, sl], stb)
            ds_ref[g] = _tn(cm, dy_off[:, sl]) + dst * cdx[:, sl]
            rr_ref[:, sl] = _nn(bm, dstb)
            yoff_ref[:, sl] = ex[:, sl] * _nn(cm, stb)
            db = _nt(xd_end[:, sl], dstb)
            dcd_ref[:, sl] = _colsum(dst * st)
            dcb = jnp.zeros((L, L), F32)
            for j in range(HPG):
                h = g * HPG + j
                hs = slice(h * P, (h + 1) * P)
                decay = jnp.where(causal, jnp.exp(jnp.where(causal, lc[:, h:h + 1] - lct[h:h + 1, :], 0.0)), 0.0)
                m = cb * decay
                dxd_ref[:, hs] = _tn(_bf(m), dyb[:, hs])
                dm = _nt(dyb[:, hs], xdb[:, hs])
                dcb = dcb + dm * decay
                gm = dm * m
                dlc_c = dlc_c + jnp.where(lane_c == h, _rowsum(gm), 0.0)
                dlc_r = dlc_r + jnp.where(sub16 == h, _colsum(gm), 0.0)
            dcbb = _bf(dcb)
            dcp_ref[:, c_lo:c_lo + N] = dcm + _nn(dcbb, bm)
            dcp_ref[:, b_lo:b_lo + N] = db + _tn(dcbb, cm)
        dxd_diag, rr = dxd_ref[...], rr_ref[...]
        tt = _headsum(rr * xd, ind) * dte16
        dlc_rt = jnp.concatenate([dlc_r, jnp.zeros((LANES - H, L), F32)], axis=0).T
        dlc = dlc_c - dlc_rt + _headsum(dyv * yoff_ref[...], ind) - tt
        dcd = _headsum(jnp.broadcast_to(dcd_ref[...], (8, SSM_INNER)), ind)[0:1]
        dlc = dlc + jnp.where(row == L - 1, _colsum(tt) + dcd * cd16, 0.0)
        dla = _tn(tri, dlc, HIGHEST)
        dxd = dxd_diag + dtex * rr
        ddt = _headsum(dxd * xs, ind) + dla * a
        ddtr = jnp.where(lane_c < H, ddt * _sigmoid(pre), 0.0)
        ddtr_ref[...] = ddtr
        acc_ref[0:1, :] += _colsum(ddtr)
        acc_ref[1:2, :] += jnp.where(lane1 < H, _colsum(dla * dt) * a, 0.0)
        acc_ref[2:3, :] += _headsum(jnp.broadcast_to(_colsum(dyv * xs), (8, SSM_INNER)), ind)[0:1]
        dcp_ref[:, 0:SSM_INNER] = dxd * dtx + dskx * dyv
        dcp_ref[...] = dcp_ref[...] * _dsilu(cpre_v)

    rev = lambda i: (nc - 1 - i, 0)
    rspec = lambda c: pl.BlockSpec((L, c), rev)
    return _call(
        body, (cpre, dtr, pm, ypre, sprev, dya, dtb, alog, dskip, normw, ind), side, name=name,
        out_shape=[_sds((T, SSM_INNER)), _sds((T, CONV_DIM)), _sds((T, LANES)), _sds((8, LANES)), _sds((1, SSM_INNER))],
        grid=(nc,),
        in_specs=[rspec(CONV_DIM), rspec(LANES), rspec(SSM_INNER), rspec(SSM_INNER),
                  pl.BlockSpec((1, SSM_GROUPS, N, half), lambda i: (nc - 1 - i, 0, 0, 0)), rspec(SSM_INNER),
                  _vec_spec(LANES), _vec_spec(LANES), _vec_spec(SSM_INNER), _vec_spec(SSM_INNER), _vec_spec(SSM_INNER, LANES)],
        out_specs=[rspec(SSM_INNER), rspec(CONV_DIM), rspec(LANES), _vec_spec(LANES, 8), _vec_spec(SSM_INNER)],
        scratch=[pltpu.VMEM((SSM_GROUPS, N, half), F32), pltpu.VMEM((L, SSM_INNER), F32), pltpu.VMEM((L, SSM_INNER), F32),
                 pltpu.VMEM((L, SSM_INNER), F32), pltpu.VMEM((L, SSM_INNER), F32), pltpu.VMEM((1, SSM_INNER), F32)],
        sem=("arbitrary",))


def _gmlp_common(u, v, lnw, lnb):
    ug = _gelu(u)
    vg = _gelu(v)
    mu = jnp.mean(vg, axis=-1, keepdims=True)
    cen = vg - mu
    rstd = lax.rsqrt(jnp.mean(cen * cen, axis=-1, keepdims=True) + EPS)
    vhat = cen * rstd
    return ug, rstd, vhat, vhat * lnw + lnb


def _causal_mask():
    row = lax.broadcasted_iota(jnp.int32, (CHUNK, CHUNK), 0)
    col = lax.broadcasted_iota(jnp.int32, (CHUNK, CHUNK), 1)
    return row >= col


def _gmlp_fwd(pm, lnw, lnb, ws, bs_exp, name):
    T = pm.shape[0]
    nc = T // CHUNK
    L, G = CHUNK, GMLP_GROUPS

    def body(u_ref, v_ref, lnw_ref, lnb_ref, ws_ref, bs_ref, o_ref):
        ug, _, _, vn = _gmlp_common(u_ref[...], v_ref[...], lnw_ref[...], lnb_ref[...])
        causal = _causal_mask()
        for g in range(G):
            sl = slice(g * L, (g + 1) * L)
            wm = _bf(jnp.where(causal, ws_ref[g], 0.0))
            sv = _nn(wm, _bf(vn[:, sl])) + bs_ref[:, sl]
            o_ref[:, sl] = (ug[:, sl] * sv).astype(BF16)

    return _pcall(
        body, name=name, out_shape=_sds((T, GMLP_INNER), BF16), grid=(nc,),
        in_specs=[_row_spec(L, GMLP_INNER, 1), _row_spec(L, GMLP_INNER, 2), _vec_spec(GMLP_INNER), _vec_spec(GMLP_INNER),
                  pl.BlockSpec((G, L, L), lambda i: (0, 0, 0)), _vec_spec(GMLP_INNER, L)],
        out_specs=_row_spec(L, GMLP_INNER), sem=("parallel",))(pm, pm, lnw, lnb, ws, bs_exp)


def _gmlp_bwd(pm, dyb, lnw, lnb, ws, bs_exp, name, side=None):
    T = pm.shape[0]
    nc = T // CHUNK
    L, G = CHUNK, GMLP_GROUPS

    def body(u_ref, v_ref, dy_ref, lnw_ref, lnb_ref, ws_ref, bs_ref, du_ref, dv_ref, dws_ref, dbs_ref, acc_ref, dvn_ref):
        @pl.when(pl.program_id(0) == 0)
        def _():
            dws_ref[...] = jnp.zeros_like(dws_ref)
            dbs_ref[...] = jnp.zeros_like(dbs_ref)
            acc_ref[...] = jnp.zeros_like(acc_ref)
        uv, vv, dyv, lnwv = u_ref[...], v_ref[...], dy_ref[...], lnw_ref[...]
        ug, rstd, vhat, vn = _gmlp_common(uv, vv, lnwv, lnb_ref[...])
        causal = _causal_mask()
        lane = lax.broadcasted_iota(jnp.int32, (L, LANES), 1)
        dbs = jnp.zeros((L, LANES), F32)
        for g in range(G):
            sl = slice(g * L, (g + 1) * L)
            wm = _bf(jnp.where(causal, ws_ref[g], 0.0))
            vng = _bf(vn[:, sl])
            sv = _nn(wm, vng) + bs_ref[:, sl]
            du_ref[:, sl] = dyv[:, sl] * sv * _dgelu(uv[:, sl])
            dsv = dyv[:, sl] * ug[:, sl]
            dsvb = _bf(dsv)
            dws_ref[g] += jnp.where(causal, _nt(dsvb, vng), 0.0)
            dbs = dbs + jnp.where(lane == g, _rowsum(dsv), 0.0)
            dvn_ref[:, sl] = _tn(wm, dsvb)
        dbs_ref[...] += dbs
        dvn = dvn_ref[...]
        acc_ref[0:1, :] += _colsum(dvn * vhat)
        acc_ref[1:2, :] += _colsum(dvn)
        dvh = dvn * lnwv
        dvg = rstd * (dvh - jnp.mean(dvh, axis=-1, keepdims=True) - vhat * jnp.mean(dvh * vhat, axis=-1, keepdims=True))
        dv_ref[...] = dvg * _dgelu(vv)

    return _call(
        body, (pm, pm, dyb, lnw, lnb, ws, bs_exp), side, name=name,
        out_shape=[_sds((T, GMLP_INNER)), _sds((T, GMLP_INNER)), _sds((G, L, L)), _sds((L, LANES)), _sds((8, GMLP_INNER))],
        grid=(nc,),
        in_specs=[_row_spec(L, GMLP_INNER, 1), _row_spec(L, GMLP_INNER, 2), _row_spec(L, GMLP_INNER),
                  _vec_spec(GMLP_INNER), _vec_spec(GMLP_INNER), pl.BlockSpec((G, L, L), lambda i: (0, 0, 0)),
                  _vec_spec(GMLP_INNER, L)],
        out_specs=[_row_spec(L, GMLP_INNER), _row_spec(L, GMLP_INNER), pl.BlockSpec((G, L, L), lambda i: (0, 0, 0)),
                   _vec_spec(LANES, L), _vec_spec(GMLP_INNER, 8)],
        scratch=[pltpu.VMEM((L, GMLP_INNER), F32)], sem=("arbitrary",))


def _rel_buckets():
    qi = np.arange(CHUNK)[:, None]
    sj = np.arange(2 * CHUNK)[None, :]
    dist = np.maximum(qi + CHUNK - sj, 0)
    max_exact = REL_BUCKETS // 2
    log_ratio = (np.log(np.maximum(dist, 1).astype(np.float32) / np.float32(max_exact))
                 / np.float32(math.log(REL_MAX_DIST / max_exact))).astype(np.float32)
    large = max_exact + (log_ratio * np.float32(REL_BUCKETS - max_exact)).astype(np.int32)
    return np.where(dist < max_exact, dist, np.minimum(large, REL_BUCKETS - 1))


def _bucket_onehot_t():
    bucket = _rel_buckets().reshape(-1)
    return jnp.asarray((np.arange(REL_BUCKETS)[:, None] == bucket[None, :]).astype(np.float32))


def _bias_from_table(table_t, onehot_t, name):
    def body(t_ref, o_ref, out_ref):
        out_ref[...] = _nn(t_ref[...], o_ref[...], HIGHEST)

    return _pcall(body, name=name, out_shape=_sds((ATTN_HEADS, onehot_t.shape[1])))(table_t, onehot_t)


def _table_from_dbias(dbias, onehot_t, name):
    def body(d_ref, o_ref, out_ref):
        out_ref[...] = _nt(d_ref[...], o_ref[...], HIGHEST)

    return _pcall(body, name=name, out_shape=_sds((ATTN_HEADS, REL_BUCKETS)))(dbias, onehot_t)


def _softmax_sink(logits, sink, mask):
    logits = jnp.where(mask, logits, NEG_INF)
    mx = jnp.maximum(jnp.max(logits, axis=-1, keepdims=True), sink)
    e = jnp.exp(logits - mx)
    es = jnp.exp(sink - mx)
    inv = 1.0 / (_rowsum(e) + es)
    return e * inv, es * inv


def _attn_mask(n, heads):
    qi = lax.broadcasted_iota(jnp.int32, (heads * CHUNK, 2 * CHUNK), 0) & (CHUNK - 1)
    sj = lax.broadcasted_iota(jnp.int32, (heads * CHUNK, 2 * CHUNK), 1)
    rel = qi + CHUNK - sj
    return (rel >= 0) & (rel < CHUNK) & ((sj >= CHUNK) | (n > 0))


def _stack_heads(ref, first, count, width):
    return jnp.concatenate([_bf(ref[:, (first + j) * width:(first + j + 1) * width]) for j in range(count)], axis=0)


def _attn_fwd(qkv, bias, sinks, name):
    T = qkv.shape[0]
    nb = T // CHUNK
    L, DH, HPK = CHUNK, ATTN_DH, ATTN_HEADS // ATTN_KV
    scale = DH ** -0.5
    kcol, vcol = ATTN_HEADS * DH // LANES, ATTN_HEADS * DH // LANES + 1

    def body(q_ref, k_ref, v_ref, kp_ref, vp_ref, bias_ref, sink_ref, o_ref, lg_ref, p_ref):
        n = pl.program_id(0)
        mask = _attn_mask(n, 1)
        kband = _bf(jnp.concatenate([kp_ref[...], k_ref[...]], axis=0))
        vband = _bf(jnp.concatenate([vp_ref[...], v_ref[...]], axis=0))
        for kv in range(ATTN_KV):
            lg_ref[...] = _nt(_stack_heads(q_ref, kv * HPK, HPK, DH), kband[:, kv * DH:(kv + 1) * DH])
            for j in range(HPK):
                h = kv * HPK + j
                p, _ = _softmax_sink(lg_ref[j * L:(j + 1) * L, :] * scale + bias_ref[h], sink_ref[h], mask)
                p_ref[j * L:(j + 1) * L, :] = _bf(p)
            og = _nn(p_ref[...], vband[:, kv * DH:(kv + 1) * DH])
            for j in range(HPK):
                h = kv * HPK + j
                o_ref[:, h * DH:(h + 1) * DH] = og[j * L:(j + 1) * L].astype(BF16)

    prev = lambda i: jnp.maximum(i - 1, 0)
    return _pcall(
        body, name=name, out_shape=_sds((T, ATTN_HEADS * DH), BF16), grid=(nb,),
        in_specs=[_row_spec(L, ATTN_HEADS * DH, 0), _row_spec(L, LANES, kcol), _row_spec(L, LANES, vcol),
                  pl.BlockSpec((L, LANES), lambda i: (prev(i), kcol)), pl.BlockSpec((L, LANES), lambda i: (prev(i), vcol)),
                  pl.BlockSpec((ATTN_HEADS, L, 2 * L), lambda i: (0, 0, 0)),
                  pl.BlockSpec(memory_space=pltpu.SMEM)],
        out_specs=_row_spec(L, ATTN_HEADS * DH),
        scratch=[pltpu.VMEM((HPK * L, 2 * L), F32), pltpu.VMEM((HPK * L, 2 * L), BF16)],
        sem=("parallel",))(qkv, qkv, qkv, qkv, qkv, bias, sinks)


def _attn_bwd(qkv, datt, bias, sinks, name):
    T = qkv.shape[0]
    nb = T // CHUNK
    L, DH, HPK = CHUNK, ATTN_DH, ATTN_HEADS // ATTN_KV
    scale = DH ** -0.5
    kcol, vcol = ATTN_HEADS * DH // LANES, ATTN_HEADS * DH // LANES + 1

    def body(q_ref, k_ref, v_ref, kp_ref, vp_ref, do_ref, bias_ref, sink_ref,
             dq_ref, dk_ref, dv_ref, dbias_ref, dsink_ref, pend_k, pend_v, band_k, band_v, lg_ref, dp_ref, p_ref, dl_ref):
        n = pl.program_id(0)

        @pl.when(n == 0)
        def _():
            dbias_ref[...] = jnp.zeros_like(dbias_ref)
            dsink_ref[...] = jnp.zeros_like(dsink_ref)

        @pl.when(n < nb)
        def _():
            mask = _attn_mask(n, 1)
            kband = _bf(jnp.concatenate([kp_ref[...], k_ref[...]], axis=0))
            vband = _bf(jnp.concatenate([vp_ref[...], v_ref[...]], axis=0))
            lane1 = lax.broadcasted_iota(jnp.int32, (1, LANES), 1)
            dsink = jnp.zeros((1, LANES), F32)
            for kv in range(ATTN_KV):
                kb, vb = kband[:, kv * DH:(kv + 1) * DH], vband[:, kv * DH:(kv + 1) * DH]
                qg = _stack_heads(q_ref, kv * HPK, HPK, DH)
                dog = _stack_heads(do_ref, kv * HPK, HPK, DH)
                lg_ref[...] = _nt(qg, kb)
                dp_ref[...] = _nt(dog, vb)
                for j in range(HPK):
                    h = kv * HPK + j
                    rows = slice(j * L, (j + 1) * L)
                    p, ps = _softmax_sink(lg_ref[rows, :] * scale + bias_ref[h], sink_ref[h], mask)
                    dp = dp_ref[rows, :]
                    delta = _rowsum(p * dp)
                    dl = p * (dp - delta)
                    dbias_ref[h] += dl
                    p_ref[rows, :] = _bf(p)
                    dl_ref[rows, :] = _bf(dl)
                    dsink = dsink + jnp.where(lane1 == h, -_colsum(ps * delta), 0.0)
                band_v[:, kv * DH:(kv + 1) * DH] = _tn(p_ref[...], dog)
                dqg = _nn(dl_ref[...], kb) * scale
                band_k[:, kv * DH:(kv + 1) * DH] = _tn(dl_ref[...], qg) * scale
                for j in range(HPK):
                    h = kv * HPK + j
                    dq_ref[:, h * DH:(h + 1) * DH] = dqg[j * L:(j + 1) * L]
            dsink_ref[...] += dsink

            @pl.when(n > 0)
            def _():
                dk_ref[...] = pend_k[...] + band_k[0:L, :]
                dv_ref[...] = pend_v[...] + band_v[0:L, :]
            pend_k[...] = band_k[L:2 * L, :]
            pend_v[...] = band_v[L:2 * L, :]

        @pl.when(n == nb)
        def _():
            dk_ref[...] = pend_k[...]
            dv_ref[...] = pend_v[...]

    cur = lambda i: jnp.minimum(i, nb - 1)
    prev = lambda i: jnp.maximum(jnp.minimum(i, nb - 1) - 1, 0)
    lag = lambda i: jnp.maximum(i - 1, 0)
    return _pcall(
        body, name=name,
        out_shape=[_sds((T, ATTN_HEADS * DH)), _sds((T, LANES)), _sds((T, LANES)), _sds((ATTN_HEADS, L, 2 * L)), _sds((1, LANES))],
        grid=(nb + 1,),
        in_specs=[pl.BlockSpec((L, ATTN_HEADS * DH), lambda i: (cur(i), 0)),
                  pl.BlockSpec((L, LANES), lambda i: (cur(i), kcol)), pl.BlockSpec((L, LANES), lambda i: (cur(i), vcol)),
                  pl.BlockSpec((L, LANES), lambda i: (prev(i), kcol)), pl.BlockSpec((L, LANES), lambda i: (prev(i), vcol)),
                  pl.BlockSpec((L, ATTN_HEADS * DH), lambda i: (cur(i), 0)),
                  pl.BlockSpec((ATTN_HEADS, L, 2 * L), lambda i: (0, 0, 0)),
                  pl.BlockSpec(memory_space=pltpu.SMEM)],
        out_specs=[pl.BlockSpec((L, ATTN_HEADS * DH), lambda i: (cur(i), 0)),
                   pl.BlockSpec((L, LANES), lambda i: (lag(i), 0)), pl.BlockSpec((L, LANES), lambda i: (lag(i), 0)),
                   pl.BlockSpec((ATTN_HEADS, L, 2 * L), lambda i: (0, 0, 0)), _vec_spec(LANES)],
        scratch=[pltpu.VMEM((L, LANES), F32), pltpu.VMEM((L, LANES), F32),
                 pltpu.VMEM((2 * L, LANES), F32), pltpu.VMEM((2 * L, LANES), F32),
                 pltpu.VMEM((HPK * L, 2 * L), F32), pltpu.VMEM((HPK * L, 2 * L), F32),
                 pltpu.VMEM((HPK * L, 2 * L), BF16), pltpu.VMEM((HPK * L, 2 * L), BF16)],
        sem=("arbitrary",))(qkv, qkv, qkv, qkv, qkv, datt, bias, sinks)


def _pad_rows(a, mult):
    pad = (-a.shape[-2]) % mult
    if pad == 0:
        return a
    cfg = [(0, 0)] * (a.ndim - 2) + [(0, pad), (0, 0)]
    return jnp.pad(a, cfg)


class _Pack:
    def __init__(self, width, mult, total_mult):
        self.width, self.mult, self.total_mult = width, mult, total_mult
        self.entries = []
        self.rows = 0

    def add(self, key, shape):
        n = int(np.prod(shape))
        rows = -(-n // self.width)
        self.entries.append((key, self.rows, rows, tuple(shape)))
        self.rows += -(-rows // self.mult) * self.mult

    @property
    def total(self):
        return -(-self.rows // self.total_mult) * self.total_mult

    def pack(self, pieces, dtype, lead=()):
        parts = []
        for key, _, rows, shape in self.entries:
            a = pieces[key].astype(dtype).reshape(lead + (-1,))
            n = int(np.prod(shape))
            a = jnp.pad(a, [(0, 0)] * len(lead) + [(0, rows * self.width - n)])
            a = a.reshape(lead + (rows, self.width))
            parts.append(_pad_rows(a, self.mult))
        out = jnp.concatenate(parts, axis=len(lead))
        return _pad_rows(out, self.total_mult)

    def unpack(self, packed, lead=()):
        out = {}
        for key, off, rows, shape in self.entries:
            a = lax.slice_in_dim(packed, off, off + rows, axis=len(lead))
            a = a.reshape(lead + (-1,))
            n = int(np.prod(shape))
            out[key] = lax.slice_in_dim(a, 0, n, axis=len(lead)).reshape(lead + shape)
        return out


def _ffn_fwd(x, h, mod, wg_t, wu_t, wd, tag, next_norm=None, gather=None):
    side = None if gather is None else (_GatherOps, gather)
    gate, up, act, *gathered = _mm_swiglu(h, wg_t, wu_t, f"ffn_gateup_{tag}", side=side)
    x_out, ffn_out, *h_next = _mm_resid([(act, wd)], x, mod[5:6], name=f"ffn_down_{tag}", norm=next_norm)
    return (x_out, dict(h=h, gate=gate, up=up, act=act, out=ffn_out), *h_next, *gathered)


def _ffn_bwd(dx_out, dffn, x_in, saved, mod, norm_w, wg_t, wu_t, wd, below, tag, exchange=None):
    side = None if exchange is None else (_SiblingOps, exchange)
    dgate, dup, *theirs = _mm_swiglu_bwd(dffn, wd, saved["gate"], saved["up"], f"ffn_act_bwd_{tag}", side=side)
    d_wd = _mm_tn(saved["act"], dffn, name=f"ffn_dwd_{tag}")
    d_wg_t = _mm_tn(dgate, saved["h"], name=f"ffn_dwg_{tag}")
    d_wu_t = _mm_tn(dup, saved["h"], name=f"ffn_dwu_{tag}")
    side = None if exchange is None else (_ChipsOps, _pair_sum(exchange, theirs[0], f"grads_pair_sum_{tag}"))
    dh = _mm([(dgate, wg_t), (dup, wu_t)], "nn", name=f"ffn_dh_{tag}", side=side)
    dh, *from_chips = dh if exchange is not None else (dh,)
    dx, d_below, acc = _norm_mod_bwd(x_in, dh, dx_out, norm_w, mod[4:5], below[0], below[1], f"ffn_norm_bwd_{tag}")
    return (dx, d_below, dict(d_wg=d_wg_t, d_wu=d_wu_t, d_wd=d_wd, acc=acc), *from_chips)


_BIG = [
    ("out_w", "out_w_even", 0, "row"), ("qkv_w", "qkv_w", 0, "col"), ("o_w", "o_w", 0, "row"),
    ("gate0", "ffn_gate_w", 0, "col"), ("up0", "ffn_up_w", 0, "col"), ("down0", "ffn_down_w", 0, "row"),
    ("gate1", "ffn_gate_w", 1, "col"), ("up1", "ffn_up_w", 1, "col"), ("down1", "ffn_down_w", 1, "row"),
    ("in_w", "in_w_even", 0, "col"),
]


def _to_wire(a, kind):
    return a.T if kind == "col" else a


_GATHER_GROUPS = [["out_w", "in_w"], ["gate0", "up0", "down0"], ["qkv_w", "o_w", "gate1", "up1", "down1"]]

_REPLICATED = ["ada_b", "norm_mix_w", "norm_ffn_w", "conv_b", "dt_bias", "a_log", "d_skip", "ssm_norm_w", "gmlp_ln_w",
               "gmlp_ln_b", "gmlp_ws", "gmlp_bs", "sinks", "rel_table", "final_norm_w"]
_TINY_SHARDED = ["conv_w", "qkv_b", "o_b"]

_WEIGHTS = ['ada_w', 'ada_b', 'norm_mix_w', 'norm_ffn_w', 'in_w_even', 'conv_w', 'conv_b', 'dt_bias', 'a_log', 'd_skip',
            'ssm_norm_w', 'gmlp_ln_w', 'gmlp_ln_b', 'gmlp_ws', 'gmlp_bs', 'out_w_even', 'qkv_w', 'qkv_b', 'o_w', 'o_b',
            'sinks', 'rel_table', 'ffn_gate_w', 'ffn_up_w', 'ffn_down_w', 'final_norm_w']


def _step(x, c, loss_target, W, M, V):
    T = x.shape[1]
    x0 = x[0]
    target = loss_target[0]
    me = 4 * lax.axis_index("x") + 2 * lax.axis_index("y") + lax.axis_index("c")

    w_wire_local = {key: _to_wire(W[name][layer].astype(BF16), kind) for key, name, layer, kind in _BIG}
    gather_packs = []
    for keys in _GATHER_GROUPS:
        gp = _Pack(D, 1, 128)
        for key in keys:
            gp.add(key, w_wire_local[key].shape)
        gather_packs.append((gp, gp.pack(w_wire_local, BF16)))
    full = {}

    def gathered_weights(group, gathered):
        shards = gather_packs[group][0].unpack(gathered, lead=(N_DEV,))
        full.update({key: a.reshape(-1, D) for key, a in shards.items()})

    gathered_weights(0, _all_gather(gather_packs[0][1], "gather_weights"))

    small_in = _Pack(D, 8, 8)
    small_in.add("c", (1, D))
    small_in.add("conv_w", W["conv_w"][0].shape)
    small_in.add("qkv_b", W["qkv_b"][0].shape)
    small_in.add("o_b", W["o_b"][0].shape)
    sm = small_in.unpack(_all_gather(small_in.pack(
        dict(c=c, conv_w=W["conv_w"][0], qkv_b=W["qkv_b"][0], o_b=W["o_b"][0]), F32), "gather_small"), lead=(N_DEV,))
    c_all = sm["c"].reshape(N_DEV, D)
    conv_w_full = jnp.transpose(sm["conv_w"], (1, 0, 2)).reshape(SSM_CONV, CONV_DIM)
    qkv_b_full = sm["qkv_b"].reshape(1, QKV_DIM)
    o_b_full = sm["o_b"].reshape(1, D)

    ncol = W["ada_w"].shape[2]
    cond, mod_cols = _mod_matmul(c_all, W["ada_w"], "mod_matmul")
    mod_g = _all_gather(mod_cols.reshape(DEPTH * N_DEV, ncol), "gather_mod").reshape(N_DEV, DEPTH, N_DEV, ncol)
    mod_me = lax.dynamic_index_in_dim(mod_g, me, axis=2, keepdims=False)
    mod_me = jnp.transpose(mod_me, (1, 0, 2)).reshape(DEPTH, 6, D)
    mod_me = jnp.pad(mod_me, ((0, 0), (0, 2), (0, 0))).reshape(DEPTH * 8, D)
    ada_b_rows = jnp.pad(W["ada_b"].reshape(DEPTH, 6, D), ((0, 0), (0, 2), (0, 0))).reshape(DEPTH * 8, D)
    mod_all = _add_rows(mod_me, ada_b_rows, "mod_bias").reshape(DEPTH, 8, D)
    mod0, mod1 = mod_all[0], mod_all[1]

    in_t = full["in_w"]
    o1, o2, o3, o4 = SSM_INNER, SSM_INNER + CONV_DIM, SSM_INNER + CONV_DIM + SSM_HEADS, SSM_INNER + CONV_DIM + SSM_HEADS + GMLP_INNER
    w_z, w_xbc, w_dt, w_u, w_v = in_t[:o1], in_t[o1:o2], in_t[o2:o3], in_t[o3:o4], in_t[o4:]
    w_main = jnp.concatenate([w_z, w_u, w_v, w_xbc], axis=0)
    w_dtp = jnp.pad(w_dt, ((0, LANES - SSM_HEADS), (0, 0)))
    out_w = full["out_w"]

    pad16 = lambda a: jnp.pad(a.reshape(1, SSM_HEADS), ((0, 0), (0, LANES - SSM_HEADS)))
    dtb, alog = pad16(W["dt_bias"][0]), pad16(W["a_log"][0])
    dskip = jnp.repeat(W["d_skip"][0], SSM_HEAD_DIM).reshape(1, SSM_INNER)
    ssm_nw = W["ssm_norm_w"]
    lnw, lnb = W["gmlp_ln_w"], W["gmlp_ln_b"]
    ws = W["gmlp_ws"][0]
    bs_exp = jnp.repeat(W["gmlp_bs"][0].T, CHUNK, axis=1)
    conv_b = W["conv_b"]
    nmw, nfw = W["norm_mix_w"], W["norm_ffn_w"]
    onehot_t = _bucket_onehot_t()
    head_ind = _head_indicator()
    bias = _bias_from_table(W["rel_table"].T, onehot_t, "rel_bias").reshape(ATTN_HEADS, CHUNK, 2 * CHUNK)
    sinks = W["sinks"][0]

    h0 = _norm_mod(x0, nmw[0:1], mod0[1:2], mod0[0:1], "mix_norm_0")
    pm = _mm([(h0, w_main)], "nt", name="in_proj", tn_pref=1536)
    dtr = _mm([(h0, w_dtp)], "nt", name="in_proj_dt")
    cpre = _conv_fwd(pm, conv_w_full, conv_b, "conv_fwd")
    ya, ypre, sprev, gathered_b = _ssd_fwd(cpre, dtr, pm, dtb, alog, dskip, ssm_nw, head_ind, "ssd_fwd",
                                           side=(_GatherOps, gather_packs[1][1]))
    gathered_weights(1, gathered_b)
    yb = _gmlp_fwd(pm, lnw, lnb, ws, bs_exp, "gmlp_fwd")
    x1, mix0, hf0 = _mm_resid([(ya, out_w[:SSM_INNER]), (yb, out_w[SSM_INNER:])], x0, mod0[2:3], name="out_proj",
                              norm=(nfw[0:1], mod0[4:5], mod0[3:4]))
    x2, ffn0, h1, gathered_c = _ffn_fwd(x1, hf0, mod0, full["gate0"], full["up0"], full["down0"], "0",
                                        next_norm=(nmw[1:2], mod1[1:2], mod1[0:1]), gather=gather_packs[2][1])
    gathered_weights(2, gathered_c)
    qkv_t, o_w = full["qkv_w"], full["o_w"]
    w_q, w_k, w_v_att = qkv_t[:D], qkv_t[D:D + LANES], qkv_t[D + LANES:]

    qkv = _mm([(h1, qkv_t)], "nt", name="qkv_proj", bias=qkv_b_full, tn_pref=1280)
    att = _attn_fwd(qkv, bias, sinks, "attn_fwd")
    x3, mix1, hf1 = _mm_resid([(att, o_w)], x2, mod1[2:3], name="o_proj", bias=o_b_full,
                              norm=(nfw[1:2], mod1[4:5], mod1[3:4]))
    x4, ffn1 = _ffn_fwd(x3, hf1, mod1, full["gate1"], full["up1"], full["down1"], "1")

    dx4, dffn1, acc_f = _final_loss(x4, W["final_norm_w"].reshape(1, D), target, ffn1["out"], mod1[5:6], "final_loss")
    loss = lax.psum(acc_f[1, 0], ("x", "y", "c"))
    dx3, dmix1, gf1 = _ffn_bwd(dx4, dffn1, x3, ffn1, mod1, nfw[1:2], full["gate1"], full["up1"], full["down1"],
                               (mix1, mod1[2:3]), "1")
    datt = _mm([(dmix1, o_w)], "nt", name="o_proj_dx")
    d_o_w = _mm_tn(att, dmix1, name="o_proj_dw")
    dq, dk, dv, dbias, dsinks = _attn_bwd(qkv, datt, bias, sinks, "attn_bwd")
    d_table = _table_from_dbias(dbias.reshape(ATTN_HEADS, -1), onehot_t, "rel_table_grad").T
    d_qkv_t = jnp.concatenate([_mm_tn(dq, h1, name="qkv_dw_q"), _mm_tn(dk, h1, name="qkv_dw_k"), _mm_tn(dv, h1, name="qkv_dw_v")], axis=0)
    d_qkv_b = jnp.concatenate([_colsum_call(dq, "qkv_db_q"), _colsum_call(dk, "qkv_db_k"), _colsum_call(dv, "qkv_db_v")], axis=1)
    dh1 = _mm([(dq, w_q), (dk, w_k), (dv, w_v_att)], "nn", name="qkv_proj_dx")
    dx2, dffn0, acc_n1 = _norm_mod_bwd(x2, dh1, dx3, nmw[1:2], mod1[1:2], ffn0["out"], mod0[5:6], "mix_norm_bwd_1")

    g_wire = dict(qkv_w=d_qkv_t, o_w=d_o_w, gate1=gf1["d_wg"], up1=gf1["d_wu"], down1=gf1["d_wd"])

    def packed_partials(group):
        return gather_packs[group][0].pack({key: g_wire[key].reshape(N_DEV, -1, D) for key in _GATHER_GROUPS[group]},
                                           F32, lead=(N_DEV,))

    from_chips = {}
    dx1, dmix0, gf0, from_chips[2] = _ffn_bwd(dx2, dffn0, x1, ffn0, mod0, nfw[0:1], full["gate0"], full["up0"], full["down0"],
                                              (mix0, mod0[2:3]), "0", exchange=packed_partials(2))
    g_wire.update(gate0=gf0["d_wg"], up0=gf0["d_wu"], down0=gf0["d_wd"])
    partials_ffn0 = packed_partials(1)

    dya = _mm([(dmix0, out_w[:SSM_INNER])], "nt", name="out_proj_dx_a")
    dyb = _mm([(dmix0, out_w[SSM_INNER:])], "nt", name="out_proj_dx_b")
    d_out_w = jnp.concatenate([_mm_tn(ya, dmix0, name="out_proj_dw_a"), _mm_tn(yb, dmix0, name="out_proj_dw_b")], axis=0)
    du, dvg, d_ws, d_bs, acc_ln, theirs_ffn0 = _gmlp_bwd(pm, dyb, lnw, lnb, ws, bs_exp, "gmlp_bwd",
                                                         side=(_SiblingOps, partials_ffn0))
    pair_ffn0 = _pair_sum(partials_ffn0, theirs_ffn0, "grads_pair_sum_mix")
    dz, dcpre, ddtr, acc_ssd, d_ssm_nw, from_chips[1] = _ssd_bwd(
        cpre, dtr, pm, ypre, sprev, dya, dtb, alog, dskip, ssm_nw, head_ind, "ssd_bwd", side=(_ChipsOps, pair_ffn0))
    dxbc, acc_conv = _conv_bwd(dcpre, pm, conv_w_full, "conv_bwd")
    d_in_t = jnp.concatenate([
        _mm_tn(dz, h0, name="in_dw_z"), _mm_tn(dxbc, h0, name="in_dw_xbc"),
        _mm_tn(ddtr, h0, name="in_dw_dt")[:SSM_HEADS], _mm_tn(du, h0, name="in_dw_u"), _mm_tn(dvg, h0, name="in_dw_v")], axis=0)
    g_wire.update(in_w=d_in_t, out_w=d_out_w)
    partials_mix = packed_partials(0)
    dh0, theirs_mix = _mm([(dz, w_z), (dxbc, w_xbc), (ddtr, w_dtp), (du, w_u), (dvg, w_v)], "nn", name="in_proj_dx",
                          side=(_SiblingOps, partials_mix))
    pair_mix = _pair_sum(partials_mix, theirs_mix, "grads_pair_sum")
    grad_x, acc_n0, from_chips[0] = _norm_mod_bwd(x0, dh0, dx1, nmw[0:1], mod0[1:2], None, None, "mix_norm_bwd_0",
                                                  side=(_ChipsOps, pair_mix))

    g_mine = {}
    for group in range(len(_GATHER_GROUPS)):
        g_mine.update(gather_packs[group][0].unpack(_sum_parts(from_chips[group], f"grads_chip_sum_{group}")))
    res_big = [{}, {}, {}, {}]
    for key, name, layer, kind in _BIG:
        g_nat = _to_wire(g_mine[key], kind)
        outs = _adamw(g_nat[None], W[name][layer], M[name][layer], V[name][layer], f"adamw_{key}")
        for res, out in zip(res_big, outs):
            res[key] = out

    acc_f0, acc_f1 = gf0["acc"], gf1["acc"]
    d_mod = jnp.stack([
        jnp.stack([acc_n0[1], acc_n0[0], acc_f0[3], acc_f0[1], acc_f0[0], acc_n1[3]]),
        jnp.stack([acc_n1[1], acc_n1[0], acc_f1[3], acc_f1[1], acc_f1[0], acc_f[3]])])
    g_small = dict(
        ada_b=d_mod.reshape(DEPTH, 6 * D),
        norm_mix_w=jnp.stack([acc_n0[2], acc_n1[2]]), norm_ffn_w=jnp.stack([acc_f0[2], acc_f1[2]]),
        conv_b=acc_conv[4:5], dt_bias=acc_ssd[0:1, :SSM_HEADS], a_log=acc_ssd[1:2, :SSM_HEADS], d_skip=acc_ssd[2:3, :SSM_HEADS],
        ssm_norm_w=d_ssm_nw, gmlp_ln_w=acc_ln[0:1], gmlp_ln_b=acc_ln[1:2], gmlp_ws=d_ws[None],
        gmlp_bs=d_bs[:, :GMLP_GROUPS].T[None], sinks=dsinks[:, :ATTN_HEADS], rel_table=d_table, final_norm_w=acc_f[0],
        conv_w=acc_conv[0:SSM_CONV], qkv_b=d_qkv_b, o_b=acc_f1[4:5])
    small = _Pack(D, 1, 8)
    for name in _REPLICATED:
        small.add(name, W[name].shape)
    small.add("conv_w", (SSM_CONV, CONV_DIM))
    small.add("qkv_b", (1, QKV_DIM))
    small.add("o_b", (1, D))
    parts_small = _all_gather(small.pack(g_small, F32), "gather_small_grads")
    zeros_tiny = dict(conv_w=jnp.zeros((SSM_CONV, CONV_DIM), F32), qkv_b=jnp.zeros((1, QKV_DIM), F32), o_b=jnp.zeros((1, D), F32))
    pks = lambda S: small.pack({**{name: S[name] for name in _REPLICATED}, **zeros_tiny}, F32)
    res_small = [small.unpack(r) for r in _adamw(parts_small, pks(W), pks(M), pks(V), "adamw_small")]
    g_small_sum = res_small[0]

    n_cw, n_qb, n_ob = W["conv_w"].shape[2], W["qkv_b"].shape[1], W["o_b"].shape[1]
    g_tiny = dict(conv_w=lax.dynamic_slice_in_dim(g_small_sum["conv_w"], me * n_cw, n_cw, axis=1)[None],
                  qkv_b=lax.dynamic_slice_in_dim(g_small_sum["qkv_b"], me * n_qb, n_qb, axis=1),
                  o_b=lax.dynamic_slice_in_dim(g_small_sum["o_b"], me * n_ob, n_ob, axis=1))
    tiny = _Pack(D, 8, 8)
    for name in _TINY_SHARDED:
        tiny.add(name, W[name].shape)
    pkt = lambda S: tiny.pack({name: S[name] for name in _TINY_SHARDED}, F32)
    res_tiny = [tiny.unpack(r) for r in _adamw(pkt(g_tiny)[None], pkt(W), pkt(M), pkt(V), "adamw_tiny")]

    dmod_all = parts_small[:, small.entries[0][1]:small.entries[0][1] + small.entries[0][2]].reshape(N_DEV, DEPTH, 6 * D)
    dmod_cols = jnp.transpose(lax.dynamic_slice_in_dim(dmod_all, me * ncol, ncol, axis=2), (1, 0, 2))
    g_ada_w = _ada_w_grad(cond, dmod_cols, "ada_w_grad")
    flat = lambda a: a.reshape(DEPTH * D, ncol)
    res_ada = [r.reshape(DEPTH, D, ncol) for r in _adamw(flat(g_ada_w)[None], flat(W["ada_w"]), flat(M["ada_w"]), flat(V["ada_w"]), "adamw_ada_w")]

    def result(kind_idx, name):
        if name == "ada_w":
            return res_ada[kind_idx]
        if name in _REPLICATED:
            return res_small[kind_idx][name]
        if name in _TINY_SHARDED:
            return res_tiny[kind_idx][name]
        pieces = [res_big[kind_idx][key] for key, nm, layer, kind in _BIG if nm == name]
        return jnp.stack(pieces)

    outs = [loss, grad_x[None]]
    for kind_idx in range(4):
        outs += [result(kind_idx, name) for name in _WEIGHTS]
    return tuple(outs)


def kernel(x, c, ada_w, ada_b, norm_mix_w, norm_ffn_w, in_w_even, conv_w, conv_b, dt_bias, a_log, d_skip, ssm_norm_w, gmlp_ln_w, gmlp_ln_b, gmlp_ws, gmlp_bs, out_w_even, qkv_w, qkv_b, o_w, o_b, sinks, rel_table, ffn_gate_w, ffn_up_w, ffn_down_w, final_norm_w, loss_target, m_ada_w, m_ada_b, m_norm_mix_w, m_norm_ffn_w, m_in_w_even, m_conv_w, m_conv_b, m_dt_bias, m_a_log, m_d_skip, m_ssm_norm_w, m_gmlp_ln_w, m_gmlp_ln_b, m_gmlp_ws, m_gmlp_bs, m_out_w_even, m_qkv_w, m_qkv_b, m_o_w, m_o_b, m_sinks, m_rel_table, m_ffn_gate_w, m_ffn_up_w, m_ffn_down_w, m_final_norm_w, v_ada_w, v_ada_b, v_norm_mix_w, v_norm_ffn_w, v_in_w_even, v_conv_w, v_conv_b, v_dt_bias, v_a_log, v_d_skip, v_ssm_norm_w, v_gmlp_ln_w, v_gmlp_ln_b, v_gmlp_ws, v_gmlp_bs, v_out_w_even, v_qkv_w, v_qkv_b, v_o_w, v_o_b, v_sinks, v_rel_table, v_ffn_gate_w, v_ffn_up_w, v_ffn_down_w, v_final_norm_w):
    args = locals()
    W = {n: args[n] for n in _WEIGHTS}
    M = {n: args["m_" + n] for n in _WEIGHTS}
    V = {n: args["v_" + n] for n in _WEIGHTS}
    return _step(x, c, loss_target, W, M, V)
```

```python
import functools
import math

import numpy as np
import jax
import jax.numpy as jnp
from jax import lax
from jax.experimental import pallas as pl
from jax.experimental.pallas import tpu as pltpu

F32 = jnp.float32
BF16 = jnp.bfloat16
HIGHEST = lax.Precision.HIGHEST
MESH = pl.DeviceIdType.MESH

N_DEV = 8
D = 1024
DEPTH = 2
SSM_HEADS = 16
SSM_HEAD_DIM = 64
SSM_INNER = 1024
SSM_GROUPS = 2
SSM_STATE = 128
SSM_CONV = 4
CHUNK = 128
CONV_DIM = SSM_INNER + 2 * SSM_GROUPS * SSM_STATE
GMLP_GROUPS = 8
GMLP_INNER = 1024
IN_EVEN = 4624
ATTN_HEADS = 16
ATTN_KV = 2
ATTN_DH = 64
QKV_DIM = 1280
REL_BUCKETS = 32
REL_MAX_DIST = 128
FFN = 2816
EPS = 1e-6
NEG_INF = -1e30
LANES = 128

ADAM_LR = 0.001
ADAM_B1 = 0.9
ADAM_B2 = 0.999
ADAM_EPS = 1e-08
ADAM_WD = 0.01
ADAM_STEP = 10

VMEM_LIMIT_BYTES = 56 * 1024 * 1024
ROW_TILE = 512


def _pcall(body, *, name, out_shape, grid=(), in_specs=None, out_specs=None, scratch=(), sem=None):
    params = dict(vmem_limit_bytes=VMEM_LIMIT_BYTES)
    if sem is not None:
        params["dimension_semantics"] = sem
    specs = {} if in_specs is None else dict(in_specs=in_specs, out_specs=out_specs)
    return pl.pallas_call(
        body, name=name, out_shape=out_shape, grid=grid, **specs,
        scratch_shapes=list(scratch), compiler_params=pltpu.CompilerParams(**params))


def _call(body, args, side=None, *, name, out_shape, grid, in_specs, out_specs, scratch=(), sem=None):
    if side is None:
        return _pcall(body, name=name, out_shape=out_shape, grid=grid, in_specs=in_specs, out_specs=out_specs,
                      scratch=scratch, sem=sem)(*args)
    ops_cls, x = side
    n_in, n_out, n_scr = len(in_specs), len(out_shape), len(scratch)
    steps = int(np.prod(grid))
    hbm = pl.BlockSpec(memory_space=pl.ANY)

    def wrapped(*refs):
        ins, x_ref = refs[:n_in], refs[n_in]
        outs, r_ref = refs[n_in + 1:n_in + 1 + n_out], refs[n_in + 1 + n_out]
        scr, sems = refs[n_in + 2 + n_out:n_in + 2 + n_out + n_scr], refs[n_in + 2 + n_out + n_scr:]
        ops = ops_cls(x_ref, r_ref, *sems)
        step = pl.program_id(0)
        for axis in range(1, len(grid)):
            step = step * grid[axis] + pl.program_id(axis)
        pl.when(step == 0)(ops.start)
        body(*ins, *outs, *scr)
        pl.when(step == (3 * steps) // 4)(ops.forward)
        pl.when(step == steps - 1)(ops.finish)

    return _pcall(
        wrapped, name=name, out_shape=list(out_shape) + [ops_cls.result(x)], grid=grid,
        in_specs=list(in_specs) + [hbm], out_specs=list(out_specs) + [hbm],
        scratch=list(scratch) + ops_cls.scratch(), sem=("arbitrary",) * len(grid))(*args, x)


def _tile(n, pref):
    if n <= pref:
        return n
    best = None
    for t in range(LANES, pref + 1, LANES):
        if n % t == 0:
            best = t
    assert best is not None, (n, pref)
    return best


def _rows(T):
    return min(ROW_TILE, T)


def _sds(shape, dtype=F32):
    return jax.ShapeDtypeStruct(shape, dtype)


def _row_spec(tm, c, col=0):
    return pl.BlockSpec((tm, c), lambda i, col=col: (i, col))


def _vec_spec(c, r=1):
    return pl.BlockSpec((r, c), lambda i: (0, 0))


def _sigmoid(x):
    return jax.nn.sigmoid(x)


def _silu(x):
    return x * _sigmoid(x)


def _dsilu(x):
    s = _sigmoid(x)
    return s * (1.0 + x * (1.0 - s))


def _gelu(x):
    return 0.5 * x * (1.0 + lax.erf(x * 0.7071067811865476))


def _dgelu(x):
    return 0.5 * (1.0 + lax.erf(x * 0.7071067811865476)) + x * jnp.exp(-0.5 * x * x) * 0.3989422804014327


def _dot(a, b, dims, precision=None):
    return lax.dot_general(a, b, (dims, ((), ())), precision=precision, preferred_element_type=F32)


def _nn(a, b, precision=None):
    return _dot(a, b, ((1,), (0,)), precision)


def _nt(a, b, precision=None):
    return _dot(a, b, ((1,), (1,)), precision)


def _tn(a, b, precision=None):
    return _dot(a, b, ((0,), (0,)), precision)


def _bf(x):
    return x.astype(BF16)


def _colsum(x):
    return jnp.sum(x, axis=0, keepdims=True)


def _rowsum(x):
    return jnp.sum(x, axis=1, keepdims=True)


def _allsum(x):
    return _colsum(_rowsum(x))


def _comm_call(ops_cls, x, name, from_vmem=False):
    def body(x_ref, out_ref, *sems):
        ops = ops_cls(x_ref, out_ref, *sems)
        ops.start()
        ops.forward()
        ops.finish()

    return pl.pallas_call(
        body, name=name, out_shape=ops_cls.result(x),
        in_specs=[pl.BlockSpec(memory_space=pltpu.VMEM if from_vmem else pl.ANY)],
        out_specs=pl.BlockSpec(memory_space=pl.ANY), scratch_shapes=ops_cls.scratch(),
    )(x)


def _all_gather(x, name):
    return _comm_call(_GatherOps, x, name, from_vmem=True)


class _GatherOps:
    def __init__(self, x_ref, out_ref, send_sems, recv_sems, local_sem):
        self.x_ref, self.out_ref = x_ref, out_ref
        self.send_sems, self.recv_sems, self.local_sem = send_sems, recv_sems, local_sem
        mx, my, mc = lax.axis_index("x"), lax.axis_index("y"), lax.axis_index("c")
        self.mc = mc
        self.me, self.sibling = (mx, my, mc), (mx, my, 1 - mc)
        self.chips = [(1 - mx, my), (mx, 1 - my), (1 - mx, 1 - my)]

    @staticmethod
    def result(x):
        return _sds((N_DEV,) + x.shape, x.dtype)

    @staticmethod
    def scratch():
        return [pltpu.SemaphoreType.DMA((7,)), pltpu.SemaphoreType.DMA((7,)), pltpu.SemaphoreType.DMA(())]

    def _slot(self, px, py, pc):
        return self.out_ref.at[4 * px + 2 * py + pc]

    def _copy(self, k, block, to, own=False):
        return pltpu.make_async_remote_copy(
            src_ref=self.x_ref if own else self._slot(*block), dst_ref=self._slot(*block),
            send_sem=self.send_sems.at[k], recv_sem=self.recv_sems.at[k], device_id=to, device_id_type=MESH)

    def _mine(self):
        return pltpu.make_async_copy(self.x_ref, self._slot(*self.me), self.local_sem)

    def _first(self):
        return [self._copy(0, self.me, self.sibling, own=True)] + [
            self._copy(1 + j, self.me, (*chip, self.mc), own=True) for j, chip in enumerate(self.chips)]

    def _passed(self):
        return [self._copy(4 + j, (*chip, self.mc), self.sibling) for j, chip in enumerate(self.chips)]

    def start(self):
        self._mine().start()
        for cp in self._first():
            cp.start()

    def forward(self):
        passed = self._passed()
        for j, chip in enumerate(self.chips):
            self._copy(1 + j, (*chip, self.mc), self.me).wait_recv()
            passed[j].start()

    def finish(self):
        self._copy(0, self.sibling, self.me).wait_recv()
        for j, chip in enumerate(self.chips):
            self._copy(4 + j, (*chip, 1 - self.mc), self.me).wait_recv()
        for cp in self._first() + self._passed():
            cp.wait_send()
        self._mine().wait()


N_CHIP = 4


class _SiblingOps:
    def __init__(self, p_ref, theirs_ref, send_sems, recv_sems):
        self.p_ref, self.theirs_ref, self.send_sems, self.recv_sems = p_ref, theirs_ref, send_sems, recv_sems

    @staticmethod
    def result(p):
        return _sds((N_CHIP,) + p.shape[1:], p.dtype)

    @staticmethod
    def scratch():
        return [pltpu.SemaphoreType.DMA((N_CHIP,))] * 2

    def _copies(self):
        mx, my, mc = lax.axis_index("x"), lax.axis_index("y"), lax.axis_index("c")
        return [pltpu.make_async_remote_copy(
            src_ref=self.p_ref.at[2 * chip + 1 - mc], dst_ref=self.theirs_ref.at[chip],
            send_sem=self.send_sems.at[chip], recv_sem=self.recv_sems.at[chip],
            device_id=(mx, my, 1 - mc), device_id_type=MESH) for chip in range(N_CHIP)]

    def start(self):
        for cp in self._copies():
            cp.start()

    def forward(self):
        pass

    def finish(self):
        for cp in self._copies():
            cp.wait()


class _ChipsOps:
    def __init__(self, q_ref, out_ref, send_sems, recv_sems, local_sem):
        self.q_ref, self.out_ref = q_ref, out_ref
        self.send_sems, self.recv_sems, self.local_sem = send_sems, recv_sems, local_sem

    @staticmethod
    def result(q):
        return _sds(q.shape, q.dtype)

    @staticmethod
    def scratch():
        return [pltpu.SemaphoreType.DMA((N_CHIP - 1,)), pltpu.SemaphoreType.DMA((N_CHIP - 1,)), pltpu.SemaphoreType.DMA(())]

    def _copies(self):
        mx, my, mc = lax.axis_index("x"), lax.axis_index("y"), lax.axis_index("c")
        me = 2 * mx + my
        local = pltpu.make_async_copy(self.q_ref.at[me], self.out_ref.at[me], self.local_sem)
        remote = []
        for r in range(1, N_CHIP):
            px = 1 - mx if r & 2 else mx
            py = 1 - my if r & 1 else my
            remote.append(pltpu.make_async_remote_copy(
                src_ref=self.q_ref.at[2 * px + py], dst_ref=self.out_ref.at[me],
                send_sem=self.send_sems.at[r - 1], recv_sem=self.recv_sems.at[r - 1],
                device_id=(px, py, mc), device_id_type=MESH))
        return local, remote

    def start(self):
        local, remote = self._copies()
        local.start()
        for cp in remote:
            cp.start()

    def forward(self):
        pass

    def finish(self):
        local, remote = self._copies()
        for cp in remote:
            cp.wait()
        local.wait()


def _pair_sum(p, theirs, name):
    n, R, C = theirs.shape
    tr = _tile_rows(R, 256)

    def body(p_ref, t_ref, o_ref):
        mc = lax.axis_index("c")
        o_ref[0] = (p_ref[0, mc] + t_ref[0]).astype(BF16)

    blk = pl.BlockSpec((1, tr, C), lambda s, i: (s, i, 0))
    return _pcall(body, name=name, out_shape=_sds((n, R, C), BF16), grid=(n, R // tr),
                  in_specs=[pl.BlockSpec((1, 2, tr, C), lambda s, i: (s, 0, i, 0)), blk],
                  out_specs=blk, sem=("parallel", "parallel"))(p.reshape(n, 2, R, C), theirs)


def _sum_parts(parts, name):
    P, R, C = parts.shape
    tr = _tile_rows(R, 256)

    def body(p_ref, o_ref):
        g = p_ref[0].astype(F32)
        for k in range(1, P):
            g = g + p_ref[k].astype(F32)
        o_ref[...] = g

    return _pcall(body, name=name, out_shape=_sds((R, C)), grid=(R // tr,),
                  in_specs=[pl.BlockSpec((P, tr, C), lambda i: (0, i, 0))],
                  out_specs=pl.BlockSpec((tr, C), lambda i: (i, 0)), sem=("parallel",))(parts)


def _adamw(parts, w, m, v, name):
    P, R, C = parts.shape
    tr = R if R <= 256 else _tile_rows(R, 256)

    def body(p_ref, w_ref, m_ref, v_ref, g_ref, d_ref, nm_ref, nv_ref):
        g = p_ref[0]
        for k in range(1, P):
            g = g + p_ref[k]
        nm = ADAM_B1 * m_ref[...] + (1.0 - ADAM_B1) * g
        nv = ADAM_B2 * v_ref[...] + (1.0 - ADAM_B2) * (g * g)
        m_hat = nm / (1.0 - ADAM_B1 ** ADAM_STEP)
        v_hat = nv / (1.0 - ADAM_B2 ** ADAM_STEP)
        g_ref[...] = g
        d_ref[...] = -ADAM_LR * (m_hat / (jnp.sqrt(v_hat) + ADAM_EPS) + ADAM_WD * w_ref[...])
        nm_ref[...] = nm
        nv_ref[...] = nv

    blk = pl.BlockSpec((tr, C), lambda i: (i, 0))
    return _pcall(
        body, name=name, out_shape=[_sds((R, C))] * 4, grid=(R // tr,),
        in_specs=[pl.BlockSpec((P, tr, C), lambda i: (0, i, 0)), blk, blk, blk],
        out_specs=[blk] * 4, sem=("parallel",))(parts, w, m, v)


def _tile_rows(n, pref):
    best = None
    for t in range(8, pref + 1, 8):
        if n % t == 0:
            best = t
    assert best is not None, (n, pref)
    return best


def _mm(pairs, mode, *, name, out_dtype=F32, bias=None, tn_pref=1024, side=None):
    M = pairs[0][0].shape[0]
    N = pairs[0][1].shape[1] if mode == "nn" else pairs[0][1].shape[0]
    tm, tn = _rows(M), _tile(N, tn_pref)
    n_pairs = len(pairs)
    has_bias = bias is not None

    def body(*refs):
        acc = _pairs_dot(refs[:2 * n_pairs], mode)
        if has_bias:
            acc = acc + refs[2 * n_pairs][...]
        refs[-1][...] = acc.astype(refs[-1].dtype)

    in_specs, args = _pair_specs(pairs, mode, tm, tn)
    if has_bias:
        in_specs.append(pl.BlockSpec((1, tn), lambda j, i: (0, j)))
        args.append(bias)
    res = _call(body, args, side, name=name, out_shape=[_sds((M, N), out_dtype)], grid=(N // tn, M // tm),
                in_specs=in_specs, out_specs=[pl.BlockSpec((tm, tn), lambda j, i: (i, j))], sem=("parallel", "parallel"))
    return res[0] if side is None else (res[0], res[1])


def _pairs_dot(ab, mode):
    acc = None
    for p in range(len(ab) // 2):
        a, b = _bf(ab[2 * p][...]), _bf(ab[2 * p + 1][...])
        d = _nn(a, b) if mode == "nn" else _nt(a, b)
        acc = d if acc is None else acc + d
    return acc


def _pair_specs(pairs, mode, tm, tn):
    in_specs, args = [], []
    for a, b in pairs:
        K = a.shape[1]
        in_specs.append(pl.BlockSpec((tm, K), lambda j, i: (i, 0)))
        if mode == "nn":
            in_specs.append(pl.BlockSpec((K, tn), lambda j, i: (0, j)))
        else:
            in_specs.append(pl.BlockSpec((tn, K), lambda j, i: (j, 0)))
        args += [a, b]
    return in_specs, args


def _mm_resid(pairs, resid, gvec, *, name, bias=None, norm=None):
    M, N = resid.shape
    tm = _rows(M)
    n_pairs = len(pairs)
    has_bias, has_norm = bias is not None, norm is not None

    def body(*refs):
        acc = _pairs_dot(refs[:2 * n_pairs], "nn")
        pos = 2 * n_pairs
        if has_bias:
            acc = acc + refs[pos][...]
            pos += 1
        xv = refs[pos][...] + refs[pos + 1][...] * acc
        outs = refs[pos + 2 + 3 * has_norm:]
        outs[0][...] = xv
        outs[1][...] = acc.astype(BF16)
        if has_norm:
            w_ref, sc_ref, sh_ref = refs[pos + 2:pos + 5]
            r = lax.rsqrt(jnp.mean(xv * xv, axis=-1, keepdims=True) + EPS)
            outs[2][...] = ((xv * r * w_ref[...]) * (1.0 + sc_ref[...]) + sh_ref[...]).astype(BF16)

    in_specs, args = _pair_specs(pairs, "nn", tm, N)
    vec = pl.BlockSpec((1, N), lambda j, i: (0, 0))
    row = pl.BlockSpec((tm, N), lambda j, i: (i, 0))
    if has_bias:
        in_specs.append(vec)
        args.append(bias)
    in_specs += [row, vec] + [vec] * (3 * has_norm)
    args += [resid, gvec] + (list(norm) if has_norm else [])
    return _pcall(body, name=name, out_shape=[_sds((M, N)), _sds((M, N), BF16)] + [_sds((M, N), BF16)] * has_norm,
                  grid=(1, M // tm), in_specs=in_specs, out_specs=[row] * (2 + has_norm),
                  sem=("parallel", "parallel"))(*args)


def _mm_tn(a, b, *, name, tm_pref=1408, tn_pref=1536):
    K, M = a.shape
    N = b.shape[1]
    tm, tn = _tile(M, tm_pref), _tile(N, tn_pref)
    tk = K if K <= 2 * ROW_TILE else 2 * ROW_TILE

    def body(a_ref, b_ref, o_ref):
        @pl.when(pl.program_id(2) == 0)
        def _():
            o_ref[...] = jnp.zeros_like(o_ref)
        o_ref[...] += _tn(_bf(a_ref[...]), _bf(b_ref[...]))

    return _pcall(
        body, name=name, out_shape=_sds((M, N)), grid=(M // tm, N // tn, K // tk),
        in_specs=[pl.BlockSpec((tk, tm), lambda i, j, k: (k, i)), pl.BlockSpec((tk, tn), lambda i, j, k: (k, j))],
        out_specs=pl.BlockSpec((tm, tn), lambda i, j, k: (i, j)),
        sem=("parallel", "parallel", "arbitrary"))(a, b)


def _mm_swiglu(h, wg_t, wu_t, name, side=None):
    M, K = h.shape
    N = wg_t.shape[0]
    tm, tn = _rows(M), _tile(N, 1408)

    def body(h_ref, wg_ref, wu_ref, gate_ref, up_ref, act_ref):
        hv = _bf(h_ref[...])
        gate = _nt(hv, wg_ref[...])
        up = _nt(hv, wu_ref[...])
        gate_ref[...] = gate.astype(BF16)
        up_ref[...] = up.astype(BF16)
        act_ref[...] = (_silu(gate) * up).astype(BF16)

    w_spec = pl.BlockSpec((tn, K), lambda j, i: (j, 0))
    o_spec = pl.BlockSpec((tm, tn), lambda j, i: (i, j))
    return _call(body, (h, wg_t, wu_t), side, name=name,
                 out_shape=[_sds((M, N), BF16)] * 3, grid=(N // tn, M // tm),
                 in_specs=[pl.BlockSpec((tm, K), lambda j, i: (i, 0)), w_spec, w_spec], out_specs=[o_spec] * 3,
                 sem=("parallel", "parallel"))


def _mm_swiglu_bwd(dout, wd, gate, up, name, side=None):
    M, K = dout.shape
    N = wd.shape[0]
    tm, tn = _rows(M), _tile(N, 1408)

    def body(d_ref, wd_ref, gate_ref, up_ref, dg_ref, du_ref):
        dact = _nt(_bf(d_ref[...]), wd_ref[...])
        g = gate_ref[...].astype(F32)
        dg_ref[...] = (dact * up_ref[...].astype(F32) * _dsilu(g)).astype(BF16)
        du_ref[...] = (dact * _silu(g)).astype(BF16)

    t_spec = pl.BlockSpec((tm, tn), lambda j, i: (i, j))
    return _call(
        body, (dout, wd, gate, up), side, name=name, out_shape=[_sds((M, N), BF16)] * 2, grid=(N // tn, M // tm),
        in_specs=[pl.BlockSpec((tm, K), lambda j, i: (i, 0)), pl.BlockSpec((tn, K), lambda j, i: (j, 0)), t_spec, t_spec],
        out_specs=[t_spec] * 2, sem=("parallel", "parallel"))


def _norm_mod(x, w, sc, sh, name):
    T = x.shape[0]
    tm = _rows(T)

    def body(x_ref, w_ref, sc_ref, sh_ref, o_ref):
        xv = x_ref[...]
        r = lax.rsqrt(jnp.mean(xv * xv, axis=-1, keepdims=True) + EPS)
        o_ref[...] = ((xv * r * w_ref[...]) * (1.0 + sc_ref[...]) + sh_ref[...]).astype(BF16)

    return _pcall(body, name=name, out_shape=_sds((T, D), BF16), grid=(T // tm,),
                  in_specs=[_row_spec(tm, D), _vec_spec(D), _vec_spec(D), _vec_spec(D)],
                  out_specs=_row_spec(tm, D), sem=("parallel",))(x, w, sc, sh)


def _gate_rows(dxv, br_ref, g_ref, db_ref, acc_ref):
    db = g_ref[...] * dxv
    db_ref[...] = db.astype(BF16)
    acc_ref[3:4, :] += _colsum(dxv * br_ref[...].astype(F32))
    acc_ref[4:5, :] += _colsum(db)


def _norm_mod_bwd(x, dh, dres, w, sc, branch, g, name, side=None):
    T = x.shape[0]
    tm = _rows(T)
    gated = branch is not None

    def body(x_ref, dh_ref, dres_ref, w_ref, sc_ref, *rest):
        (br_ref, g_ref, dx_ref, db_ref, acc_ref) = rest if gated else (None, None, rest[0], None, rest[1])

        @pl.when(pl.program_id(0) == 0)
        def _():
            acc_ref[...] = jnp.zeros_like(acc_ref)
        xv, dh_v, wv = x_ref[...], dh_ref[...], w_ref[...]
        r = lax.rsqrt(jnp.mean(xv * xv, axis=-1, keepdims=True) + EPS)
        n = xv * r
        dnw = dh_v * (1.0 + sc_ref[...])
        dn = dnw * wv
        dxv = dres_ref[...] + r * (dn - n * jnp.mean(dn * n, axis=-1, keepdims=True))
        dx_ref[...] = dxv
        acc_ref[0:1, :] += _colsum(dh_v * (n * wv))
        acc_ref[1:2, :] += _colsum(dh_v)
        acc_ref[2:3, :] += _colsum(dnw * n)
        if gated:
            _gate_rows(dxv, br_ref, g_ref, db_ref, acc_ref)

    row = _row_spec(tm, D)
    args = (x, dh, dres, w, sc) + ((branch, g) if gated else ())
    return _call(body, args, side, name=name,
                 out_shape=[_sds((T, D))] + ([_sds((T, D), BF16)] if gated else []) + [_sds((8, D))], grid=(T // tm,),
                 in_specs=[row, row, row, _vec_spec(D), _vec_spec(D)] + ([row, _vec_spec(D)] if gated else []),
                 out_specs=[row] + ([row] if gated else []) + [_vec_spec(D, 8)], sem=("arbitrary",))


def _final_loss(x, wf, target, branch, g, name):
    T = x.shape[0]
    tm = _rows(T)

    def body(x_ref, w_ref, t_ref, br_ref, g_ref, dx_ref, db_ref, acc_ref):
        @pl.when(pl.program_id(0) == 0)
        def _():
            acc_ref[...] = jnp.zeros_like(acc_ref)
        xv, wv = x_ref[...], w_ref[...]
        r = lax.rsqrt(jnp.mean(xv * xv, axis=-1, keepdims=True) + EPS)
        n = xv * r
        err = n * wv - t_ref[...]
        dy = err * (1.0 / D)
        dn = dy * wv
        dxv = r * (dn - n * jnp.mean(dn * n, axis=-1, keepdims=True))
        dx_ref[...] = dxv
        acc_ref[0:1, :] += _colsum(dy * n)
        acc_ref[1:2, :] += jnp.broadcast_to(_allsum(err * err) * (0.5 / D), (1, D))
        _gate_rows(dxv, br_ref, g_ref, db_ref, acc_ref)

    row = _row_spec(tm, D)
    return _pcall(body, name=name, out_shape=[_sds((T, D)), _sds((T, D), BF16), _sds((8, D))], grid=(T // tm,),
                  in_specs=[row, _vec_spec(D), row, row, _vec_spec(D)],
                  out_specs=[row, row, _vec_spec(D, 8)], sem=("arbitrary",))(x, wf, target, branch, g)


def _colsum_call(x, name):
    T, C = x.shape
    tm = _rows(T)

    def body(x_ref, o_ref):
        @pl.when(pl.program_id(0) == 0)
        def _():
            o_ref[...] = jnp.zeros_like(o_ref)
        o_ref[...] += _colsum(x_ref[...].astype(F32))

    return _pcall(body, name=name, out_shape=_sds((1, C)), grid=(T // tm,), in_specs=[_row_spec(tm, C)],
                  out_specs=_vec_spec(C), sem=("arbitrary",))(x)


def _mod_matmul(c_all, ada_w, name):
    n = ada_w.shape[2]

    def body(c_ref, w_ref, cond_ref, o_ref):
        cond = _silu(c_ref[...])
        cond_ref[...] = cond
        o_ref[0] = _nn(cond, w_ref[0])

    return _pcall(body, name=name, out_shape=[_sds((N_DEV, D)), _sds((DEPTH, N_DEV, n))], grid=(DEPTH,),
                  in_specs=[pl.BlockSpec((N_DEV, D), lambda l: (0, 0)), pl.BlockSpec((1, D, n), lambda l: (l, 0, 0))],
                  out_specs=[pl.BlockSpec((N_DEV, D), lambda l: (0, 0)), pl.BlockSpec((1, N_DEV, n), lambda l: (l, 0, 0))],
                  sem=("arbitrary",))(c_all, ada_w)


def _add_rows(a, b, name):
    def body(a_ref, b_ref, o_ref):
        o_ref[...] = a_ref[...] + b_ref[...]

    return _pcall(body, name=name, out_shape=_sds(a.shape))(a, b)


def _ada_w_grad(cond, dmod_cols, name):
    n = dmod_cols.shape[2]

    def body(c_ref, d_ref, o_ref):
        o_ref[0] = _tn(c_ref[...], d_ref[0])

    return _pcall(body, name=name, out_shape=_sds((DEPTH, D, n)), grid=(DEPTH,),
                  in_specs=[pl.BlockSpec((N_DEV, D), lambda l: (0, 0)), pl.BlockSpec((1, N_DEV, n), lambda l: (l, 0, 0))],
                  out_specs=pl.BlockSpec((1, D, n), lambda l: (l, 0, 0)), sem=("parallel",))(cond, dmod_cols)


def _conv_fwd(pm, conv_w, conv_b, name):
    T = pm.shape[0]
    tm = _rows(T)
    C = CONV_DIM

    def body(x_ref, prev_ref, w_ref, b_ref, o_ref):
        cur = x_ref[...]
        prev = jnp.where(pl.program_id(0) > 0, prev_ref[...], 0.0)
        cur8 = cur[0:8]
        row8 = lax.broadcasted_iota(jnp.int32, (8, C), 0)
        full = w_ref[3:4, :] * cur
        head = w_ref[3:4, :] * cur8
        for k in range(1, SSM_CONV):
            wk = w_ref[3 - k:4 - k, :]
            full = full + wk * pltpu.roll(cur, k, 0)
            head = head + wk * jnp.where(row8 < k, pltpu.roll(prev, k, 0), pltpu.roll(cur8, k, 0))
        o_ref[...] = full + b_ref[...]
        o_ref[0:8, :] = head + b_ref[...]

    return _pcall(
        body, name=name, out_shape=_sds((T, C)), grid=(T // tm,),
        in_specs=[pl.BlockSpec((tm, C), lambda i: (i, 2)),
                  pl.BlockSpec((8, C), lambda i: (jnp.maximum(i * (tm // 8) - 1, 0), 2)),
                  _vec_spec(C, SSM_CONV), _vec_spec(C)],
        out_specs=_row_spec(tm, C), sem=("parallel",))(pm, pm, conv_w, conv_b)


def _conv_bwd(dc, pm, conv_w, name):
    T = dc.shape[0]
    tm = _rows(T)
    C = CONV_DIM
    nt = T // tm

    def body(dc_ref, nxt_ref, x_ref, prev_ref, w_ref, dx_ref, acc_ref):
        i = pl.program_id(0)

        @pl.when(i == 0)
        def _():
            acc_ref[...] = jnp.zeros_like(acc_ref)
        dcv = dc_ref[...]
        nxt = jnp.where(i < nt - 1, nxt_ref[...], 0.0)
        xc = x_ref[...]
        prev = jnp.where(i > 0, prev_ref[...], 0.0)
        dc8h, dc8t, x8 = dcv[0:8], dcv[tm - 8:tm], xc[0:8]
        row8 = lax.broadcasted_iota(jnp.int32, (8, C), 0)
        full = w_ref[3:4, :] * dcv
        tail = w_ref[3:4, :] * dc8t
        acc_ref[3:4, :] += _colsum(dcv * xc)
        for k in range(1, SSM_CONV):
            wk = w_ref[3 - k:4 - k, :]
            full = full + wk * pltpu.roll(dcv, tm - k, 0)
            tail = tail + wk * jnp.where(row8 + k >= 8, pltpu.roll(nxt, 8 - k, 0), pltpu.roll(dc8t, 8 - k, 0))
            xs_head = jnp.where(row8 < k, pltpu.roll(prev, k, 0), pltpu.roll(x8, k, 0))
            prod = dcv * pltpu.roll(xc, k, 0)
            acc_ref[3 - k:4 - k, :] += _colsum(prod) - _colsum(prod[0:8]) + _colsum(dc8h * xs_head)
        acc_ref[4:5, :] += _colsum(dcv)
        dx_ref[...] = full
        dx_ref[tm - 8:tm, :] = tail

    return _pcall(
        body, name=name, out_shape=[_sds((T, C)), _sds((8, C))], grid=(nt,),
        in_specs=[_row_spec(tm, C),
                  pl.BlockSpec((8, C), lambda i: (jnp.minimum((i + 1) * (tm // 8), T // 8 - 1), 0)),
                  pl.BlockSpec((tm, C), lambda i: (i, 2)),
                  pl.BlockSpec((8, C), lambda i: (jnp.maximum(i * (tm // 8) - 1, 0), 2)),
                  _vec_spec(C, SSM_CONV)],
        out_specs=[_row_spec(tm, C), _vec_spec(C, 8)], sem=("arbitrary",))(dc, dc, pm, pm, conv_w)


def _ssd_prologue(cpre, dtr, dtb, alog):
    L = CHUNK
    xc = _silu(cpre)
    pre = dtr + dtb
    dt = jnp.maximum(pre, 0.0) + jnp.log1p(jnp.exp(-jnp.abs(pre)))
    a = -jnp.exp(alog)
    la = dt * a
    row = lax.broadcasted_iota(jnp.int32, (L, L), 0)
    col = lax.broadcasted_iota(jnp.int32, (L, L), 1)
    causal = row >= col
    tri = causal.astype(F32)
    lc = _nn(tri, la, HIGHEST)
    return xc, pre, dt, a, causal, tri, lc, row, col


def _head_indicator():
    m = np.zeros((LANES, SSM_INNER), np.float32)
    for h in range(SSM_HEADS):
        m[h, h * SSM_HEAD_DIM:(h + 1) * SSM_HEAD_DIM] = 1.0
    return jnp.asarray(m, dtype=BF16)


def _split_dot(x, ind, dims):
    hi = x.astype(BF16)
    lo = (x - hi.astype(F32)).astype(BF16)
    return _dot(hi, ind, dims) + _dot(lo, ind, dims)


def _expand(x16, ind):
    return _split_dot(x16, ind, ((1,), (0,)))


def _headsum(x, ind):
    return _split_dot(x, ind, ((1,), (1,)))


def _ssd_fwd(cpre, dtr, pm, dtb, alog, dskip, normw, ind, name, side=None):
    T = cpre.shape[0]
    nc = T // CHUNK
    L, P, H, HPG, N = CHUNK, SSM_HEAD_DIM, SSM_HEADS, SSM_HEADS // SSM_GROUPS, SSM_STATE
    half = SSM_INNER // SSM_GROUPS

    def body(cp_ref, dtr_ref, z_ref, dtb_ref, alog_ref, dskip_ref, nw_ref, ind_ref, ya_ref, y_ref, sp_ref, st_ref):
        @pl.when(pl.program_id(0) == 0)
        def _():
            st_ref[...] = jnp.zeros_like(st_ref)
        xc, _, dt, _, causal, _, lc, _, _ = _ssd_prologue(cp_ref[...], dtr_ref[...], dtb_ref[...], alog_ref[...])
        lct = lc.T
        ind = ind_ref[...]
        llast = lc[L - 1:L, :]
        xs = xc[:, :SSM_INNER]
        xd = xs * _expand(dt, ind)
        ex = _expand(jnp.exp(lc), ind)
        xd_end = _bf(xd * _expand(jnp.exp(llast - lc), ind))
        cdx = _expand(jnp.broadcast_to(jnp.exp(llast), (8, LANES)), ind)[0:1]
        xdb = _bf(xd)
        sp_ref[0] = st_ref[...]
        for g in range(SSM_GROUPS):
            sl = slice(g * half, (g + 1) * half)
            bm = _bf(xc[:, SSM_INNER + g * N:SSM_INNER + (g + 1) * N])
            cm = _bf(xc[:, SSM_INNER + (SSM_GROUPS + g) * N:SSM_INNER + (SSM_GROUPS + g + 1) * N])
            cb = _nt(cm, bm)
            st = st_ref[g]
            y_ref[:, sl] = ex[:, sl] * _nn(cm, _bf(st)) + dskip_ref[:, sl] * xs[:, sl]
            st_ref[g] = st * cdx[:, sl] + _tn(bm, xd_end[:, sl])
            for j in range(HPG):
                h = g * HPG + j
                decay = jnp.where(causal, jnp.exp(jnp.where(causal, lc[:, h:h + 1] - lct[h:h + 1, :], 0.0)), 0.0)
                y_ref[:, h * P:(h + 1) * P] += _nn(_bf(cb * decay), xdb[:, h * P:(h + 1) * P])
        y2 = y_ref[...] * _silu(z_ref[...])
        for g in range(SSM_GROUPS):
            yg = y2[:, g * half:(g + 1) * half]
            r = lax.rsqrt(jnp.mean(yg * yg, axis=-1, keepdims=True) + EPS)
            ya_ref[:, g * half:(g + 1) * half] = (yg * r * nw_ref[:, g * half:(g + 1) * half]).astype(BF16)

    return _call(
        body, (cpre, dtr, pm, dtb, alog, dskip, normw, ind), side, name=name,
        out_shape=[_sds((T, SSM_INNER), BF16), _sds((T, SSM_INNER)), _sds((nc, SSM_GROUPS, N, half))], grid=(nc,),
        in_specs=[_row_spec(L, CONV_DIM), _row_spec(L, LANES), _row_spec(L, SSM_INNER, 0),
                  _vec_spec(LANES), _vec_spec(LANES), _vec_spec(SSM_INNER), _vec_spec(SSM_INNER), _vec_spec(SSM_INNER, LANES)],
        out_specs=[_row_spec(L, SSM_INNER), _row_spec(L, SSM_INNER),
                   pl.BlockSpec((1, SSM_GROUPS, N, half), lambda i: (i, 0, 0, 0))],
        scratch=[pltpu.VMEM((SSM_GROUPS, N, half), F32)], sem=("arbitrary",))


def _ssd_bwd(cpre, dtr, pm, ypre, sprev, dya, dtb, alog, dskip, normw, ind, name, side=None):
    T = cpre.shape[0]
    nc = T // CHUNK
    L, P, H, HPG, N = CHUNK, SSM_HEAD_DIM, SSM_HEADS, SSM_HEADS // SSM_GROUPS, SSM_STATE
    half = SSM_INNER // SSM_GROUPS

    def body(cp_ref, dtr_ref, z_ref, y_ref, sp_ref, dya_ref, dtb_ref, alog_ref, dskip_ref, nw_ref, ind_ref,
             dz_ref, dcp_ref, ddtr_ref, acc_ref, dnw_ref, ds_ref, dy_ref, dxd_ref, rr_ref, yoff_ref, dcd_ref):
        @pl.when(pl.program_id(0) == 0)
        def _():
            ds_ref[...] = jnp.zeros_like(ds_ref)
            acc_ref[...] = jnp.zeros_like(acc_ref)
            dnw_ref[...] = jnp.zeros_like(dnw_ref)
        cpre_v = cp_ref[...]
        xc, pre, dt, a, causal, tri, lc, row, col = _ssd_prologue(cpre_v, dtr_ref[...], dtb_ref[...], alog_ref[...])
        lct = lc.T
        zv, yv = z_ref[...], y_ref[...]
        sz = _silu(zv)
        y2 = yv * sz
        dya_v = dya_ref[...]
        nwv = nw_ref[...]
        for g in range(SSM_GROUPS):
            sl = slice(g * half, (g + 1) * half)
            yg = y2[:, sl]
            r = lax.rsqrt(jnp.mean(yg * yg, axis=-1, keepdims=True) + EPS)
            nrm = yg * r
            dnw_ref[:, sl] += _colsum(dya_v[:, sl] * nrm)
            dn = dya_v[:, sl] * nwv[:, sl]
            dy2 = r * (dn - nrm * jnp.mean(dn * nrm, axis=-1, keepdims=True))
            dy_ref[:, sl] = dy2 * sz[:, sl]
            dz_ref[:, sl] = dy2 * yv[:, sl] * _dsilu(zv[:, sl])
        ind = ind_ref[...]
        llast = lc[L - 1:L, :]
        dte16 = jnp.exp(llast - lc)
        cd16 = jnp.exp(llast)
        xs = xc[:, :SSM_INNER]
        dtx = _expand(dt, ind)
        ex = _expand(jnp.exp(lc), ind)
        dtex = _expand(dte16, ind)
        cdx = _expand(jnp.broadcast_to(cd16, (8, LANES)), ind)[0:1]
        xd = xs * dtx
        xdb = _bf(xd)
        xd_end = _bf(xd * dtex)
        dyv = dy_ref[...]
        dy_off = _bf(ex * dyv)
        dyb = _bf(dyv)
        dskx = dskip_ref[...]
        lane_c = lax.broadcasted_iota(jnp.int32, (L, LANES), 1)
        lane1 = lax.broadcasted_iota(jnp.int32, (1, LANES), 1)
        sub16 = lax.broadcasted_iota(jnp.int32, (H, L), 0)
        dlc_c = jnp.zeros((L, LANES), F32)
        dlc_r = jnp.zeros((H, L), F32)
        for g in range(SSM_GROUPS):
            sl = slice(g * half, (g + 1) * half)
            b_lo = SSM_INNER + g * N
            c_lo = SSM_INNER + (SSM_GROUPS + g) * N
            bm, cm = _bf(xc[:, b_lo:b_lo + N]), _bf(xc[:, c_lo:c_lo + N])
            cb = _nt(cm, bm)
            st, dst = sp_ref[0, g], ds_ref[g]
            stb, dstb = _bf(st), _bf(dst)
            dcm = _nt(dy_off[:, sl], stb)
            ds_ref[g] = _tn(cm, dy_off[:, sl]) + dst * cdx[:, sl]
            rr_ref[:, sl] = _nn(bm, dstb)
            yoff_ref[:, sl] = ex[:, sl] * _nn(cm, stb)
            db = _nt(xd_end[:, sl], dstb)
            dcd_ref[:, sl] = _colsum(dst * st)
            dcb = jnp.zeros((L, L), F32)
            for j in range(HPG):
                h = g * HPG + j
                hs = slice(h * P, (h + 1) * P)
                decay = jnp.where(causal, jnp.exp(jnp.where(causal, lc[:, h:h + 1] - lct[h:h + 1, :], 0.0)), 0.0)
                m = cb * decay
                dxd_ref[:, hs] = _tn(_bf(m), dyb[:, hs])
                dm = _nt(dyb[:, hs], xdb[:, hs])
                dcb = dcb + dm * decay
                gm = dm * m
                dlc_c = dlc_c + jnp.where(lane_c == h, _rowsum(gm), 0.0)
                dlc_r = dlc_r + jnp.where(sub16 == h, _colsum(gm), 0.0)
            dcbb = _bf(dcb)
            dcp_ref[:, c_lo:c_lo + N] = dcm + _nn(dcbb, bm)
            dcp_ref[:, b_lo:b_lo + N] = db + _tn(dcbb, cm)
        dxd_diag, rr = dxd_ref[...], rr_ref[...]
        tt = _headsum(rr * xd, ind) * dte16
        dlc_rt = jnp.concatenate([dlc_r, jnp.zeros((LANES - H, L), F32)], axis=0).T
        dlc = dlc_c - dlc_rt + _headsum(dyv * yoff_ref[...], ind) - tt
        dcd = _headsum(jnp.broadcast_to(dcd_ref[...], (8, SSM_INNER)), ind)[0:1]
        dlc = dlc + jnp.where(row == L - 1, _colsum(tt) + dcd * cd16, 0.0)
        dla = _tn(tri, dlc, HIGHEST)
        dxd = dxd_diag + dtex * rr
        ddt = _headsum(dxd * xs, ind) + dla * a
        ddtr = jnp.where(lane_c < H, ddt * _sigmoid(pre), 0.0)
        ddtr_ref[...] = ddtr
        acc_ref[0:1, :] += _colsum(ddtr)
        acc_ref[1:2, :] += jnp.where(lane1 < H, _colsum(dla * dt) * a, 0.0)
        acc_ref[2:3, :] += _headsum(jnp.broadcast_to(_colsum(dyv * xs), (8, SSM_INNER)), ind)[0:1]
        dcp_ref[:, 0:SSM_INNER] = dxd * dtx + dskx * dyv
        dcp_ref[...] = dcp_ref[...] * _dsilu(cpre_v)

    rev = lambda i: (nc - 1 - i, 0)
    rspec = lambda c: pl.BlockSpec((L, c), rev)
    return _call(
        body, (cpre, dtr, pm, ypre, sprev, dya, dtb, alog, dskip, normw, ind), side, name=name,
        out_shape=[_sds((T, SSM_INNER)), _sds((T, CONV_DIM)), _sds((T, LANES)), _sds((8, LANES)), _sds((1, SSM_INNER))],
        grid=(nc,),
        in_specs=[rspec(CONV_DIM), rspec(LANES), rspec(SSM_INNER), rspec(SSM_INNER),
                  pl.BlockSpec((1, SSM_GROUPS, N, half), lambda i: (nc - 1 - i, 0, 0, 0)), rspec(SSM_INNER),
                  _vec_spec(LANES), _vec_spec(LANES), _vec_spec(SSM_INNER), _vec_spec(SSM_INNER), _vec_spec(SSM_INNER, LANES)],
        out_specs=[rspec(SSM_INNER), rspec(CONV_DIM), rspec(LANES), _vec_spec(LANES, 8), _vec_spec(SSM_INNER)],
        scratch=[pltpu.VMEM((SSM_GROUPS, N, half), F32), pltpu.VMEM((L, SSM_INNER), F32), pltpu.VMEM((L, SSM_INNER), F32),
                 pltpu.VMEM((L, SSM_INNER), F32), pltpu.VMEM((L, SSM_INNER), F32), pltpu.VMEM((1, SSM_INNER), F32)],
        sem=("arbitrary",))


def _gmlp_common(u, v, lnw, lnb):
    ug = _gelu(u)
    vg = _gelu(v)
    mu = jnp.mean(vg, axis=-1, keepdims=True)
    cen = vg - mu
    rstd = lax.rsqrt(jnp.mean(cen * cen, axis=-1, keepdims=True) + EPS)
    vhat = cen * rstd
    return ug, rstd, vhat, vhat * lnw + lnb


def _causal_mask():
    row = lax.broadcasted_iota(jnp.int32, (CHUNK, CHUNK), 0)
    col = lax.broadcasted_iota(jnp.int32, (CHUNK, CHUNK), 1)
    return row >= col


def _gmlp_fwd(pm, lnw, lnb, ws, bs_exp, name, side=None):
    T = pm.shape[0]
    nc = T // CHUNK
    L, G = CHUNK, GMLP_GROUPS

    def body(u_ref, v_ref, lnw_ref, lnb_ref, ws_ref, bs_ref, o_ref):
        ug, _, _, vn = _gmlp_common(u_ref[...], v_ref[...], lnw_ref[...], lnb_ref[...])
        causal = _causal_mask()
        for g in range(G):
            sl = slice(g * L, (g + 1) * L)
            wm = _bf(jnp.where(causal, ws_ref[g], 0.0))
            sv = _nn(wm, _bf(vn[:, sl])) + bs_ref[:, sl]
            o_ref[:, sl] = (ug[:, sl] * sv).astype(BF16)

    return _call(
        body, (pm, pm, lnw, lnb, ws, bs_exp), side, name=name, out_shape=[_sds((T, GMLP_INNER), BF16)], grid=(nc,),
        in_specs=[_row_spec(L, GMLP_INNER, 1), _row_spec(L, GMLP_INNER, 2), _vec_spec(GMLP_INNER), _vec_spec(GMLP_INNER),
                  pl.BlockSpec((G, L, L), lambda i: (0, 0, 0)), _vec_spec(GMLP_INNER, L)],
        out_specs=[_row_spec(L, GMLP_INNER)], sem=("parallel",))


def _gmlp_bwd(pm, dyb, lnw, lnb, ws, bs_exp, name, side=None):
    T = pm.shape[0]
    nc = T // CHUNK
    L, G = CHUNK, GMLP_GROUPS

    def body(u_ref, v_ref, dy_ref, lnw_ref, lnb_ref, ws_ref, bs_ref, du_ref, dv_ref, dws_ref, dbs_ref, acc_ref, dvn_ref):
        @pl.when(pl.program_id(0) == 0)
        def _():
            dws_ref[...] = jnp.zeros_like(dws_ref)
            dbs_ref[...] = jnp.zeros_like(dbs_ref)
            acc_ref[...] = jnp.zeros_like(acc_ref)
        uv, vv, dyv, lnwv = u_ref[...], v_ref[...], dy_ref[...], lnw_ref[...]
        ug, rstd, vhat, vn = _gmlp_common(uv, vv, lnwv, lnb_ref[...])
        causal = _causal_mask()
        lane = lax.broadcasted_iota(jnp.int32, (L, LANES), 1)
        dbs = jnp.zeros((L, LANES), F32)
        for g in range(G):
            sl = slice(g * L, (g + 1) * L)
            wm = _bf(jnp.where(causal, ws_ref[g], 0.0))
            vng = _bf(vn[:, sl])
            sv = _nn(wm, vng) + bs_ref[:, sl]
            du_ref[:, sl] = dyv[:, sl] * sv * _dgelu(uv[:, sl])
            dsv = dyv[:, sl] * ug[:, sl]
            dsvb = _bf(dsv)
            dws_ref[g] += jnp.where(causal, _nt(dsvb, vng), 0.0)
            dbs = dbs + jnp.where(lane == g, _rowsum(dsv), 0.0)
            dvn_ref[:, sl] = _tn(wm, dsvb)
        dbs_ref[...] += dbs
        dvn = dvn_ref[...]
        acc_ref[0:1, :] += _colsum(dvn * vhat)
        acc_ref[1:2, :] += _colsum(dvn)
        dvh = dvn * lnwv
        dvg = rstd * (dvh - jnp.mean(dvh, axis=-1, keepdims=True) - vhat * jnp.mean(dvh * vhat, axis=-1, keepdims=True))
        dv_ref[...] = dvg * _dgelu(vv)

    return _call(
        body, (pm, pm, dyb, lnw, lnb, ws, bs_exp), side, name=name,
        out_shape=[_sds((T, GMLP_INNER)), _sds((T, GMLP_INNER)), _sds((G, L, L)), _sds((L, LANES)), _sds((8, GMLP_INNER))],
        grid=(nc,),
        in_specs=[_row_spec(L, GMLP_INNER, 1), _row_spec(L, GMLP_INNER, 2), _row_spec(L, GMLP_INNER),
                  _vec_spec(GMLP_INNER), _vec_spec(GMLP_INNER), pl.BlockSpec((G, L, L), lambda i: (0, 0, 0)),
                  _vec_spec(GMLP_INNER, L)],
        out_specs=[_row_spec(L, GMLP_INNER), _row_spec(L, GMLP_INNER), pl.BlockSpec((G, L, L), lambda i: (0, 0, 0)),
                   _vec_spec(LANES, L), _vec_spec(GMLP_INNER, 8)],
        scratch=[pltpu.VMEM((L, GMLP_INNER), F32)], sem=("arbitrary",))


def _rel_buckets():
    qi = np.arange(CHUNK)[:, None]
    sj = np.arange(2 * CHUNK)[None, :]
    dist = np.maximum(qi + CHUNK - sj, 0)
    max_exact = REL_BUCKETS // 2
    log_ratio = (np.log(np.maximum(dist, 1).astype(np.float32) / np.float32(max_exact))
                 / np.float32(math.log(REL_MAX_DIST / max_exact))).astype(np.float32)
    large = max_exact + (log_ratio * np.float32(REL_BUCKETS - max_exact)).astype(np.int32)
    return np.where(dist < max_exact, dist, np.minimum(large, REL_BUCKETS - 1))


def _bucket_onehot_t():
    bucket = _rel_buckets().reshape(-1)
    return jnp.asarray((np.arange(REL_BUCKETS)[:, None] == bucket[None, :]).astype(np.float32))


def _bias_from_table(table_t, onehot_t, name):
    def body(t_ref, o_ref, out_ref):
        out_ref[...] = _nn(t_ref[...], o_ref[...], HIGHEST)

    return _pcall(body, name=name, out_shape=_sds((ATTN_HEADS, onehot_t.shape[1])))(table_t, onehot_t)


def _table_from_dbias(dbias, onehot_t, name):
    def body(d_ref, o_ref, out_ref):
        out_ref[...] = _nt(d_ref[...], o_ref[...], HIGHEST)

    return _pcall(body, name=name, out_shape=_sds((ATTN_HEADS, REL_BUCKETS)))(dbias, onehot_t)


def _softmax_sink(logits, sink, mask):
    logits = jnp.where(mask, logits, NEG_INF)
    mx = jnp.maximum(jnp.max(logits, axis=-1, keepdims=True), sink)
    e = jnp.exp(logits - mx)
    es = jnp.exp(sink - mx)
    inv = 1.0 / (_rowsum(e) + es)
    return e * inv, es * inv


def _attn_mask(n, heads):
    qi = lax.broadcasted_iota(jnp.int32, (heads * CHUNK, 2 * CHUNK), 0) & (CHUNK - 1)
    sj = lax.broadcasted_iota(jnp.int32, (heads * CHUNK, 2 * CHUNK), 1)
    rel = qi + CHUNK - sj
    return (rel >= 0) & (rel < CHUNK) & ((sj >= CHUNK) | (n > 0))


def _stack_heads(ref, first, count, width):
    return jnp.concatenate([_bf(ref[:, (first + j) * width:(first + j + 1) * width]) for j in range(count)], axis=0)


def _attn_fwd(qkv, bias, sinks, name):
    T = qkv.shape[0]
    nb = T // CHUNK
    L, DH, HPK = CHUNK, ATTN_DH, ATTN_HEADS // ATTN_KV
    scale = DH ** -0.5
    kcol, vcol = ATTN_HEADS * DH // LANES, ATTN_HEADS * DH // LANES + 1

    def body(q_ref, k_ref, v_ref, kp_ref, vp_ref, bias_ref, sink_ref, o_ref, lg_ref, p_ref):
        n = pl.program_id(0)
        mask = _attn_mask(n, 1)
        kband = _bf(jnp.concatenate([kp_ref[...], k_ref[...]], axis=0))
        vband = _bf(jnp.concatenate([vp_ref[...], v_ref[...]], axis=0))
        for kv in range(ATTN_KV):
            lg_ref[...] = _nt(_stack_heads(q_ref, kv * HPK, HPK, DH), kband[:, kv * DH:(kv + 1) * DH])
            for j in range(HPK):
                h = kv * HPK + j
                p, _ = _softmax_sink(lg_ref[j * L:(j + 1) * L, :] * scale + bias_ref[h], sink_ref[h], mask)
                p_ref[j * L:(j + 1) * L, :] = _bf(p)
            og = _nn(p_ref[...], vband[:, kv * DH:(kv + 1) * DH])
            for j in range(HPK):
                h = kv * HPK + j
                o_ref[:, h * DH:(h + 1) * DH] = og[j * L:(j + 1) * L].astype(BF16)

    prev = lambda i: jnp.maximum(i - 1, 0)
    return _pcall(
        body, name=name, out_shape=_sds((T, ATTN_HEADS * DH), BF16), grid=(nb,),
        in_specs=[_row_spec(L, ATTN_HEADS * DH, 0), _row_spec(L, LANES, kcol), _row_spec(L, LANES, vcol),
                  pl.BlockSpec((L, LANES), lambda i: (prev(i), kcol)), pl.BlockSpec((L, LANES), lambda i: (prev(i), vcol)),
                  pl.BlockSpec((ATTN_HEADS, L, 2 * L), lambda i: (0, 0, 0)),
                  pl.BlockSpec(memory_space=pltpu.SMEM)],
        out_specs=_row_spec(L, ATTN_HEADS * DH),
        scratch=[pltpu.VMEM((HPK * L, 2 * L), F32), pltpu.VMEM((HPK * L, 2 * L), BF16)],
        sem=("parallel",))(qkv, qkv, qkv, qkv, qkv, bias, sinks)


def _attn_bwd(qkv, datt, bias, sinks, name):
    T = qkv.shape[0]
    nb = T // CHUNK
    L, DH, HPK = CHUNK, ATTN_DH, ATTN_HEADS // ATTN_KV
    scale = DH ** -0.5
    kcol, vcol = ATTN_HEADS * DH // LANES, ATTN_HEADS * DH // LANES + 1

    def body(q_ref, k_ref, v_ref, kp_ref, vp_ref, do_ref, bias_ref, sink_ref,
             dq_ref, dk_ref, dv_ref, dbias_ref, dsink_ref, pend_k, pend_v, band_k, band_v, lg_ref, dp_ref, p_ref, dl_ref):
        n = pl.program_id(0)

        @pl.when(n == 0)
        def _():
            dbias_ref[...] = jnp.zeros_like(dbias_ref)
            dsink_ref[...] = jnp.zeros_like(dsink_ref)

        @pl.when(n < nb)
        def _():
            mask = _attn_mask(n, 1)
            kband = _bf(jnp.concatenate([kp_ref[...], k_ref[...]], axis=0))
            vband = _bf(jnp.concatenate([vp_ref[...], v_ref[...]], axis=0))
            lane1 = lax.broadcasted_iota(jnp.int32, (1, LANES), 1)
            dsink = jnp.zeros((1, LANES), F32)
            for kv in range(ATTN_KV):
                kb, vb = kband[:, kv * DH:(kv + 1) * DH], vband[:, kv * DH:(kv + 1) * DH]
                qg = _stack_heads(q_ref, kv * HPK, HPK, DH)
                dog = _stack_heads(do_ref, kv * HPK, HPK, DH)
                lg_ref[...] = _nt(qg, kb)
                dp_ref[...] = _nt(dog, vb)
                for j in range(HPK):
                    h = kv * HPK + j
                    rows = slice(j * L, (j + 1) * L)
                    p, ps = _softmax_sink(lg_ref[rows, :] * scale + bias_ref[h], sink_ref[h], mask)
                    dp = dp_ref[rows, :]
                    delta = _rowsum(p * dp)
                    dl = p * (dp - delta)
                    dbias_ref[h] += dl
                    p_ref[rows, :] = _bf(p)
                    dl_ref[rows, :] = _bf(dl)
                    dsink = dsink + jnp.where(lane1 == h, -_colsum(ps * delta), 0.0)
                band_v[:, kv * DH:(kv + 1) * DH] = _tn(p_ref[...], dog)
                dqg = _nn(dl_ref[...], kb) * scale
                band_k[:, kv * DH:(kv + 1) * DH] = _tn(dl_ref[...], qg) * scale
                for j in range(HPK):
                    h = kv * HPK + j
                    dq_ref[:, h * DH:(h + 1) * DH] = dqg[j * L:(j + 1) * L]
            dsink_ref[...] += dsink

            @pl.when(n > 0)
            def _():
                dk_ref[...] = pend_k[...] + band_k[0:L, :]
                dv_ref[...] = pend_v[...] + band_v[0:L, :]
            pend_k[...] = band_k[L:2 * L, :]
            pend_v[...] = band_v[L:2 * L, :]

        @pl.when(n == nb)
        def _():
            dk_ref[...] = pend_k[...]
            dv_ref[...] = pend_v[...]

    cur = lambda i: jnp.minimum(i, nb - 1)
    prev = lambda i: jnp.maximum(jnp.minimum(i, nb - 1) - 1, 0)
    lag = lambda i: jnp.maximum(i - 1, 0)
    return _pcall(
        body, name=name,
        out_shape=[_sds((T, ATTN_HEADS * DH)), _sds((T, LANES)), _sds((T, LANES)), _sds((ATTN_HEADS, L, 2 * L)), _sds((1, LANES))],
        grid=(nb + 1,),
        in_specs=[pl.BlockSpec((L, ATTN_HEADS * DH), lambda i: (cur(i), 0)),
                  pl.BlockSpec((L, LANES), lambda i: (cur(i), kcol)), pl.BlockSpec((L, LANES), lambda i: (cur(i), vcol)),
                  pl.BlockSpec((L, LANES), lambda i: (prev(i), kcol)), pl.BlockSpec((L, LANES), lambda i: (prev(i), vcol)),
                  pl.BlockSpec((L, ATTN_HEADS * DH), lambda i: (cur(i), 0)),
                  pl.BlockSpec((ATTN_HEADS, L, 2 * L), lambda i: (0, 0, 0)),
                  pl.BlockSpec(memory_space=pltpu.SMEM)],
        out_specs=[pl.BlockSpec((L, ATTN_HEADS * DH), lambda i: (cur(i), 0)),
                   pl.BlockSpec((L, LANES), lambda i: (lag(i), 0)), pl.BlockSpec((L, LANES), lambda i: (lag(i), 0)),
                   pl.BlockSpec((ATTN_HEADS, L, 2 * L), lambda i: (0, 0, 0)), _vec_spec(LANES)],
        scratch=[pltpu.VMEM((L, LANES), F32), pltpu.VMEM((L, LANES), F32),
                 pltpu.VMEM((2 * L, LANES), F32), pltpu.VMEM((2 * L, LANES), F32),
                 pltpu.VMEM((HPK * L, 2 * L), F32), pltpu.VMEM((HPK * L, 2 * L), F32),
                 pltpu.VMEM((HPK * L, 2 * L), BF16), pltpu.VMEM((HPK * L, 2 * L), BF16)],
        sem=("arbitrary",))(qkv, qkv, qkv, qkv, qkv, datt, bias, sinks)


def _pad_rows(a, mult):
    pad = (-a.shape[-2]) % mult
    if pad == 0:
        return a
    cfg = [(0, 0)] * (a.ndim - 2) + [(0, pad), (0, 0)]
    return jnp.pad(a, cfg)


class _Pack:
    def __init__(self, width, mult, total_mult):
        self.width, self.mult, self.total_mult = width, mult, total_mult
        self.entries = []
        self.rows = 0

    def add(self, key, shape):
        n = int(np.prod(shape))
        rows = -(-n // self.width)
        self.entries.append((key, self.rows, rows, tuple(shape)))
        self.rows += -(-rows // self.mult) * self.mult

    @property
    def total(self):
        return -(-self.rows // self.total_mult) * self.total_mult

    def pack(self, pieces, dtype, lead=()):
        parts = []
        for key, _, rows, shape in self.entries:
            a = pieces[key].astype(dtype).reshape(lead + (-1,))
            n = int(np.prod(shape))
            a = jnp.pad(a, [(0, 0)] * len(lead) + [(0, rows * self.width - n)])
            a = a.reshape(lead + (rows, self.width))
            parts.append(_pad_rows(a, self.mult))
        out = jnp.concatenate(parts, axis=len(lead))
        return _pad_rows(out, self.total_mult)

    def unpack(self, packed, lead=()):
        out = {}
        for key, off, rows, shape in self.entries:
            a = lax.slice_in_dim(packed, off, off + rows, axis=len(lead))
            a = a.reshape(lead + (-1,))
            n = int(np.prod(shape))
            out[key] = lax.slice_in_dim(a, 0, n, axis=len(lead)).reshape(lead + shape)
        return out


def _ffn_fwd(x, h, mod, wg_t, wu_t, wd, tag, next_norm=None, gather=None):
    side = None if gather is None else (_GatherOps, gather)
    gate, up, act, *gathered = _mm_swiglu(h, wg_t, wu_t, f"ffn_gateup_{tag}", side=side)
    x_out, ffn_out, *h_next = _mm_resid([(act, wd)], x, mod[5:6], name=f"ffn_down_{tag}", norm=next_norm)
    return (x_out, dict(h=h, gate=gate, up=up, act=act, out=ffn_out), *h_next, *gathered)


def _ffn_bwd(dx_out, dffn, x_in, saved, mod, norm_w, wg_t, wu_t, wd, below, tag, exchange=None):
    side = None if exchange is None else (_SiblingOps, exchange)
    dgate, dup, *theirs = _mm_swiglu_bwd(dffn, wd, saved["gate"], saved["up"], f"ffn_act_bwd_{tag}", side=side)
    d_wd = _mm_tn(saved["act"], dffn, name=f"ffn_dwd_{tag}")
    d_wg_t = _mm_tn(dgate, saved["h"], name=f"ffn_dwg_{tag}")
    d_wu_t = _mm_tn(dup, saved["h"], name=f"ffn_dwu_{tag}")
    side = None if exchange is None else (_ChipsOps, _pair_sum(exchange, theirs[0], f"grads_pair_sum_{tag}"))
    dh = _mm([(dgate, wg_t), (dup, wu_t)], "nn", name=f"ffn_dh_{tag}", side=side)
    dh, *from_chips = dh if exchange is not None else (dh,)
    dx, d_below, acc = _norm_mod_bwd(x_in, dh, dx_out, norm_w, mod[4:5], below[0], below[1], f"ffn_norm_bwd_{tag}")
    return (dx, d_below, dict(d_wg=d_wg_t, d_wu=d_wu_t, d_wd=d_wd, acc=acc), *from_chips)


_BIG = [
    ("out_w", "out_w_even", 0, "row"), ("qkv_w", "qkv_w", 0, "col"), ("o_w", "o_w", 0, "row"),
    ("gate0", "ffn_gate_w", 0, "col"), ("up0", "ffn_up_w", 0, "col"), ("down0", "ffn_down_w", 0, "row"),
    ("gate1", "ffn_gate_w", 1, "col"), ("up1", "ffn_up_w", 1, "col"), ("down1", "ffn_down_w", 1, "row"),
    ("in_w", "in_w_even", 0, "col"),
]


def _to_wire(a, kind):
    return a.T if kind == "col" else a


_GATHER_GROUPS = [["out_w", "in_w"], ["gate0", "up0", "down0"], ["qkv_w", "o_w"], ["gate1", "up1", "down1"]]
_GRAD_GROUPS = [["qkv_w", "o_w", "gate1", "up1", "down1"], ["out_w", "gate0", "up0", "down0"], ["in_w"]]

_REPLICATED = ["ada_b", "norm_mix_w", "norm_ffn_w", "conv_b", "dt_bias", "a_log", "d_skip", "ssm_norm_w", "gmlp_ln_w",
               "gmlp_ln_b", "gmlp_ws", "gmlp_bs", "sinks", "rel_table", "final_norm_w"]
_TINY_SHARDED = ["conv_w", "qkv_b", "o_b"]

_WEIGHTS = ['ada_w', 'ada_b', 'norm_mix_w', 'norm_ffn_w', 'in_w_even', 'conv_w', 'conv_b', 'dt_bias', 'a_log', 'd_skip',
            'ssm_norm_w', 'gmlp_ln_w', 'gmlp_ln_b', 'gmlp_ws', 'gmlp_bs', 'out_w_even', 'qkv_w', 'qkv_b', 'o_w', 'o_b',
            'sinks', 'rel_table', 'ffn_gate_w', 'ffn_up_w', 'ffn_down_w', 'final_norm_w']


def _step(x, c, loss_target, W, M, V):
    T = x.shape[1]
    x0 = x[0]
    target = loss_target[0]
    me = 4 * lax.axis_index("x") + 2 * lax.axis_index("y") + lax.axis_index("c")

    w_wire_local = {key: _to_wire(W[name][layer].astype(BF16), kind) for key, name, layer, kind in _BIG}

    def wire_pack(keys, mult):
        gp = _Pack(D, 1, mult)
        for key in keys:
            gp.add(key, w_wire_local[key].shape)
        return gp

    gather_packs = [(gp, gp.pack(w_wire_local, BF16)) for gp in (wire_pack(keys, 16) for keys in _GATHER_GROUPS)]
    grad_packs = [wire_pack(keys, 128) for keys in _GRAD_GROUPS]
    full = {}

    def gathered_weights(group, gathered):
        shards = gather_packs[group][0].unpack(gathered, lead=(N_DEV,))
        full.update({key: a.reshape(-1, D) for key, a in shards.items()})

    gathered_weights(0, _all_gather(gather_packs[0][1], "gather_weights"))

    small_in = _Pack(D, 8, 8)
    small_in.add("c", (1, D))
    small_in.add("conv_w", W["conv_w"][0].shape)
    small_in.add("qkv_b", W["qkv_b"][0].shape)
    small_in.add("o_b", W["o_b"][0].shape)
    sm = small_in.unpack(_all_gather(small_in.pack(
        dict(c=c, conv_w=W["conv_w"][0], qkv_b=W["qkv_b"][0], o_b=W["o_b"][0]), F32), "gather_small"), lead=(N_DEV,))
    c_all = sm["c"].reshape(N_DEV, D)
    conv_w_full = jnp.transpose(sm["conv_w"], (1, 0, 2)).reshape(SSM_CONV, CONV_DIM)
    qkv_b_full = sm["qkv_b"].reshape(1, QKV_DIM)
    o_b_full = sm["o_b"].reshape(1, D)

    ncol = W["ada_w"].shape[2]
    cond, mod_cols = _mod_matmul(c_all, W["ada_w"], "mod_matmul")
    mod_g = _all_gather(mod_cols.reshape(DEPTH * N_DEV, ncol), "gather_mod").reshape(N_DEV, DEPTH, N_DEV, ncol)
    mod_me = lax.dynamic_index_in_dim(mod_g, me, axis=2, keepdims=False)
    mod_me = jnp.transpose(mod_me, (1, 0, 2)).reshape(DEPTH, 6, D)
    mod_me = jnp.pad(mod_me, ((0, 0), (0, 2), (0, 0))).reshape(DEPTH * 8, D)
    ada_b_rows = jnp.pad(W["ada_b"].reshape(DEPTH, 6, D), ((0, 0), (0, 2), (0, 0))).reshape(DEPTH * 8, D)
    mod_all = _add_rows(mod_me, ada_b_rows, "mod_bias").reshape(DEPTH, 8, D)
    mod0, mod1 = mod_all[0], mod_all[1]

    in_t = full["in_w"]
    o1, o2, o3, o4 = SSM_INNER, SSM_INNER + CONV_DIM, SSM_INNER + CONV_DIM + SSM_HEADS, SSM_INNER + CONV_DIM + SSM_HEADS + GMLP_INNER
    w_z, w_xbc, w_dt, w_u, w_v = in_t[:o1], in_t[o1:o2], in_t[o2:o3], in_t[o3:o4], in_t[o4:]
    w_main = jnp.concatenate([w_z, w_u, w_v, w_xbc], axis=0)
    w_dtp = jnp.pad(w_dt, ((0, LANES - SSM_HEADS), (0, 0)))
    out_w = full["out_w"]

    pad16 = lambda a: jnp.pad(a.reshape(1, SSM_HEADS), ((0, 0), (0, LANES - SSM_HEADS)))
    dtb, alog = pad16(W["dt_bias"][0]), pad16(W["a_log"][0])
    dskip = jnp.repeat(W["d_skip"][0], SSM_HEAD_DIM).reshape(1, SSM_INNER)
    ssm_nw = W["ssm_norm_w"]
    lnw, lnb = W["gmlp_ln_w"], W["gmlp_ln_b"]
    ws = W["gmlp_ws"][0]
    bs_exp = jnp.repeat(W["gmlp_bs"][0].T, CHUNK, axis=1)
    conv_b = W["conv_b"]
    nmw, nfw = W["norm_mix_w"], W["norm_ffn_w"]
    onehot_t = _bucket_onehot_t()
    head_ind = _head_indicator()
    bias = _bias_from_table(W["rel_table"].T, onehot_t, "rel_bias").reshape(ATTN_HEADS, CHUNK, 2 * CHUNK)
    sinks = W["sinks"][0]

    h0 = _norm_mod(x0, nmw[0:1], mod0[1:2], mod0[0:1], "mix_norm_0")
    pm = _mm([(h0, w_main)], "nt", name="in_proj", tn_pref=1536)
    dtr = _mm([(h0, w_dtp)], "nt", name="in_proj_dt")
    cpre = _conv_fwd(pm, conv_w_full, conv_b, "conv_fwd")
    ya, ypre, sprev, gathered_b = _ssd_fwd(cpre, dtr, pm, dtb, alog, dskip, ssm_nw, head_ind, "ssd_fwd",
                                           side=(_GatherOps, gather_packs[1][1]))
    gathered_weights(1, gathered_b)
    yb, gathered_c = _gmlp_fwd(pm, lnw, lnb, ws, bs_exp, "gmlp_fwd", side=(_GatherOps, gather_packs[2][1]))
    gathered_weights(2, gathered_c)
    x1, mix0, hf0 = _mm_resid([(ya, out_w[:SSM_INNER]), (yb, out_w[SSM_INNER:])], x0, mod0[2:3], name="out_proj",
                              norm=(nfw[0:1], mod0[4:5], mod0[3:4]))
    x2, ffn0, h1, gathered_d = _ffn_fwd(x1, hf0, mod0, full["gate0"], full["up0"], full["down0"], "0",
                                        next_norm=(nmw[1:2], mod1[1:2], mod1[0:1]), gather=gather_packs[3][1])
    gathered_weights(3, gathered_d)
    qkv_t, o_w = full["qkv_w"], full["o_w"]
    w_q, w_k, w_v_att = qkv_t[:D], qkv_t[D:D + LANES], qkv_t[D + LANES:]

    qkv = _mm([(h1, qkv_t)], "nt", name="qkv_proj", bias=qkv_b_full, tn_pref=1280)
    att = _attn_fwd(qkv, bias, sinks, "attn_fwd")
    x3, mix1, hf1 = _mm_resid([(att, o_w)], x2, mod1[2:3], name="o_proj", bias=o_b_full,
                              norm=(nfw[1:2], mod1[4:5], mod1[3:4]))
    x4, ffn1 = _ffn_fwd(x3, hf1, mod1, full["gate1"], full["up1"], full["down1"], "1")

    dx4, dffn1, acc_f = _final_loss(x4, W["final_norm_w"].reshape(1, D), target, ffn1["out"], mod1[5:6], "final_loss")
    loss = lax.psum(acc_f[1, 0], ("x", "y", "c"))
    dx3, dmix1, gf1 = _ffn_bwd(dx4, dffn1, x3, ffn1, mod1, nfw[1:2], full["gate1"], full["up1"], full["down1"],
                               (mix1, mod1[2:3]), "1")
    datt = _mm([(dmix1, o_w)], "nt", name="o_proj_dx")
    d_o_w = _mm_tn(att, dmix1, name="o_proj_dw")
    dq, dk, dv, dbias, dsinks = _attn_bwd(qkv, datt, bias, sinks, "attn_bwd")
    d_table = _table_from_dbias(dbias.reshape(ATTN_HEADS, -1), onehot_t, "rel_table_grad").T
    d_qkv_t = jnp.concatenate([_mm_tn(dq, h1, name="qkv_dw_q"), _mm_tn(dk, h1, name="qkv_dw_k"), _mm_tn(dv, h1, name="qkv_dw_v")], axis=0)
    d_qkv_b = jnp.concatenate([_colsum_call(dq, "qkv_db_q"), _colsum_call(dk, "qkv_db_k"), _colsum_call(dv, "qkv_db_v")], axis=1)
    dh1 = _mm([(dq, w_q), (dk, w_k), (dv, w_v_att)], "nn", name="qkv_proj_dx")
    dx2, dffn0, acc_n1 = _norm_mod_bwd(x2, dh1, dx3, nmw[1:2], mod1[1:2], ffn0["out"], mod0[5:6], "mix_norm_bwd_1")

    g_wire = dict(qkv_w=d_qkv_t, o_w=d_o_w, gate1=gf1["d_wg"], up1=gf1["d_wu"], down1=gf1["d_wd"])

    def packed_partials(group):
        return grad_packs[group].pack({key: g_wire[key].reshape(N_DEV, -1, D) for key in _GRAD_GROUPS[group]},
                                      F32, lead=(N_DEV,))

    from_chips = {}
    dx1, dmix0, gf0, from_chips[0] = _ffn_bwd(dx2, dffn0, x1, ffn0, mod0, nfw[0:1], full["gate0"], full["up0"], full["down0"],
                                              (mix0, mod0[2:3]), "0", exchange=packed_partials(0))

    dya = _mm([(dmix0, out_w[:SSM_INNER])], "nt", name="out_proj_dx_a")
    dyb = _mm([(dmix0, out_w[SSM_INNER:])], "nt", name="out_proj_dx_b")
    d_out_w = jnp.concatenate([_mm_tn(ya, dmix0, name="out_proj_dw_a"), _mm_tn(yb, dmix0, name="out_proj_dw_b")], axis=0)
    g_wire.update(gate0=gf0["d_wg"], up0=gf0["d_wu"], down0=gf0["d_wd"], out_w=d_out_w)
    partials_ffn0 = packed_partials(1)
    du, dvg, d_ws, d_bs, acc_ln, theirs_ffn0 = _gmlp_bwd(pm, dyb, lnw, lnb, ws, bs_exp, "gmlp_bwd",
                                                         side=(_SiblingOps, partials_ffn0))
    pair_ffn0 = _pair_sum(partials_ffn0, theirs_ffn0, "grads_pair_sum_mix")
    dz, dcpre, ddtr, acc_ssd, d_ssm_nw, from_chips[1] = _ssd_bwd(
        cpre, dtr, pm, ypre, sprev, dya, dtb, alog, dskip, ssm_nw, head_ind, "ssd_bwd", side=(_ChipsOps, pair_ffn0))
    dxbc, acc_conv = _conv_bwd(dcpre, pm, conv_w_full, "conv_bwd")
    d_in_t = jnp.concatenate([
        _mm_tn(dz, h0, name="in_dw_z"), _mm_tn(dxbc, h0, name="in_dw_xbc"),
        _mm_tn(ddtr, h0, name="in_dw_dt")[:SSM_HEADS], _mm_tn(du, h0, name="in_dw_u"), _mm_tn(dvg, h0, name="in_dw_v")], axis=0)
    g_wire.update(in_w=d_in_t)
    partials_mix = packed_partials(2)
    dh0, theirs_mix = _mm([(dz, w_z), (dxbc, w_xbc), (ddtr, w_dtp), (du, w_u), (dvg, w_v)], "nn", name="in_proj_dx",
                          side=(_SiblingOps, partials_mix))
    pair_mix = _pair_sum(partials_mix, theirs_mix, "grads_pair_sum")
    grad_x, acc_n0, from_chips[2] = _norm_mod_bwd(x0, dh0, dx1, nmw[0:1], mod0[1:2], None, None, "mix_norm_bwd_0",
                                                  side=(_ChipsOps, pair_mix))

    g_mine = {}
    for group in range(len(_GRAD_GROUPS)):
        g_mine.update(grad_packs[group].unpack(_sum_parts(from_chips[group], f"grads_chip_sum_{group}")))
    res_big = [{}, {}, {}, {}]
    for key, name, layer, kind in _BIG:
        g_nat = _to_wire(g_mine[key], kind)
        outs = _adamw(g_nat[None], W[name][layer], M[name][layer], V[name][layer], f"adamw_{key}")
        for res, out in zip(res_big, outs):
            res[key] = out

    acc_f0, acc_f1 = gf0["acc"], gf1["acc"]
    d_mod = jnp.stack([
        jnp.stack([acc_n0[1], acc_n0[0], acc_f0[3], acc_f0[1], acc_f0[0], acc_n1[3]]),
        jnp.stack([acc_n1[1], acc_n1[0], acc_f1[3], acc_f1[1], acc_f1[0], acc_f[3]])])
    g_small = dict(
        ada_b=d_mod.reshape(DEPTH, 6 * D),
        norm_mix_w=jnp.stack([acc_n0[2], acc_n1[2]]), norm_ffn_w=jnp.stack([acc_f0[2], acc_f1[2]]),
        conv_b=acc_conv[4:5], dt_bias=acc_ssd[0:1, :SSM_HEADS], a_log=acc_ssd[1:2, :SSM_HEADS], d_skip=acc_ssd[2:3, :SSM_HEADS],
        ssm_norm_w=d_ssm_nw, gmlp_ln_w=acc_ln[0:1], gmlp_ln_b=acc_ln[1:2], gmlp_ws=d_ws[None],
        gmlp_bs=d_bs[:, :GMLP_GROUPS].T[None], sinks=dsinks[:, :ATTN_HEADS], rel_table=d_table, final_norm_w=acc_f[0],
        conv_w=acc_conv[0:SSM_CONV], qkv_b=d_qkv_b, o_b=acc_f1[4:5])
    small = _Pack(D, 1, 8)
    for name in _REPLICATED:
        small.add(name, W[name].shape)
    small.add("conv_w", (SSM_CONV, CONV_DIM))
    small.add("qkv_b", (1, QKV_DIM))
    small.add("o_b", (1, D))
    parts_small = _all_gather(small.pack(g_small, F32), "gather_small_grads")
    zeros_tiny = dict(conv_w=jnp.zeros((SSM_CONV, CONV_DIM), F32), qkv_b=jnp.zeros((1, QKV_DIM), F32), o_b=jnp.zeros((1, D), F32))
    pks = lambda S: small.pack({**{name: S[name] for name in _REPLICATED}, **zeros_tiny}, F32)
    res_small = [small.unpack(r) for r in _adamw(parts_small, pks(W), pks(M), pks(V), "adamw_small")]
    g_small_sum = res_small[0]

    n_cw, n_qb, n_ob = W["conv_w"].shape[2], W["qkv_b"].shape[1], W["o_b"].shape[1]
    g_tiny = dict(conv_w=lax.dynamic_slice_in_dim(g_small_sum["conv_w"], me * n_cw, n_cw, axis=1)[None],
                  qkv_b=lax.dynamic_slice_in_dim(g_small_sum["qkv_b"], me * n_qb, n_qb, axis=1),
                  o_b=lax.dynamic_slice_in_dim(g_small_sum["o_b"], me * n_ob, n_ob, axis=1))
    tiny = _Pack(D, 8, 8)
    for name in _TINY_SHARDED:
        tiny.add(name, W[name].shape)
    pkt = lambda S: tiny.pack({name: S[name] for name in _TINY_SHARDED}, F32)
    res_tiny = [tiny.unpack(r) for r in _adamw(pkt(g_tiny)[None], pkt(W), pkt(M), pkt(V), "adamw_tiny")]

    dmod_all = parts_small[:, small.entries[0][1]:small.entries[0][1] + small.entries[0][2]].reshape(N_DEV, DEPTH, 6 * D)
    dmod_cols = jnp.transpose(lax.dynamic_slice_in_dim(dmod_all, me * ncol, ncol, axis=2), (1, 0, 2))
    g_ada_w = _ada_w_grad(cond, dmod_cols, "ada_w_grad")
    flat = lambda a: a.reshape(DEPTH * D, ncol)
    res_ada = [r.reshape(DEPTH, D, ncol) for r in _adamw(flat(g_ada_w)[None], flat(W["ada_w"]), flat(M["ada_w"]), flat(V["ada_w"]), "adamw_ada_w")]

    def result(kind_idx, name):
        if name == "ada_w":
            return res_ada[kind_idx]
        if name in _REPLICATED:
            return res_small[kind_idx][name]
        if name in _TINY_SHARDED:
            return res_tiny[kind_idx][name]
        pieces = [res_big[kind_idx][key] for key, nm, layer, kind in _BIG if nm == name]
        return jnp.stack(pieces)

    outs = [loss, grad_x[None]]
    for kind_idx in range(4):
        outs += [result(kind_idx, name) for name in _WEIGHTS]
    return tuple(outs)


def kernel(x, c, ada_w, ada_b, norm_mix_w, norm_ffn_w, in_w_even, conv_w, conv_b, dt_bias, a_log, d_skip, ssm_norm_w, gmlp_ln_w, gmlp_ln_b, gmlp_ws, gmlp_bs, out_w_even, qkv_w, qkv_b, o_w, o_b, sinks, rel_table, ffn_gate_w, ffn_up_w, ffn_down_w, final_norm_w, loss_target, m_ada_w, m_ada_b, m_norm_mix_w, m_norm_ffn_w, m_in_w_even, m_conv_w, m_conv_b, m_dt_bias, m_a_log, m_d_skip, m_ssm_norm_w, m_gmlp_ln_w, m_gmlp_ln_b, m_gmlp_ws, m_gmlp_bs, m_out_w_even, m_qkv_w, m_qkv_b, m_o_w, m_o_b, m_sinks, m_rel_table, m_ffn_gate_w, m_ffn_up_w, m_ffn_down_w, m_final_norm_w, v_ada_w, v_ada_b, v_norm_mix_w, v_norm_ffn_w, v_in_w_even, v_conv_w, v_conv_b, v_dt_bias, v_a_log, v_d_skip, v_ssm_norm_w, v_gmlp_ln_w, v_gmlp_ln_b, v_gmlp_ws, v_gmlp_bs, v_out_w_even, v_qkv_w, v_qkv_b, v_o_w, v_o_b, v_sinks, v_rel_table, v_ffn_gate_w, v_ffn_up_w, v_ffn_down_w, v_final_norm_w):
    args = locals()
    W = {n: args[n] for n in _WEIGHTS}
    M = {n: args["m_" + n] for n in _WEIGHTS}
    V = {n: args["v_" + n] for n in _WEIGHTS}
    return _step(x, c, loss_target, W, M, V)
```

```python
import functools
import math

import numpy as np
import jax
import jax.numpy as jnp
from jax import lax
from jax.experimental import pallas as pl
from jax.experimental.pallas import tpu as pltpu

F32 = jnp.float32
BF16 = jnp.bfloat16
HIGHEST = lax.Precision.HIGHEST
MESH = pl.DeviceIdType.MESH

N_DEV = 8
D = 1024
DEPTH = 2
SSM_HEADS = 16
SSM_HEAD_DIM = 64
SSM_INNER = 1024
SSM_GROUPS = 2
SSM_STATE = 128
SSM_CONV = 4
CHUNK = 128
CONV_DIM = SSM_INNER + 2 * SSM_GROUPS * SSM_STATE
GMLP_GROUPS = 8
GMLP_INNER = 1024
IN_EVEN = 4624
ATTN_HEADS = 16
ATTN_KV = 2
ATTN_DH = 64
QKV_DIM = 1280
REL_BUCKETS = 32
REL_MAX_DIST = 128
FFN = 2816
EPS = 1e-6
NEG_INF = -1e30
LANES = 128

ADAM_LR = 0.001
ADAM_B1 = 0.9
ADAM_B2 = 0.999
ADAM_EPS = 1e-08
ADAM_WD = 0.01
ADAM_STEP = 10

VMEM_LIMIT_BYTES = 56 * 1024 * 1024
ROW_TILE = 512


def _pcall(body, *, name, out_shape, grid=(), in_specs=None, out_specs=None, scratch=(), sem=None):
    params = dict(vmem_limit_bytes=VMEM_LIMIT_BYTES)
    if sem is not None:
        params["dimension_semantics"] = sem
    specs = {} if in_specs is None else dict(in_specs=in_specs, out_specs=out_specs)
    return pl.pallas_call(
        body, name=name, out_shape=out_shape, grid=grid, **specs,
        scratch_shapes=list(scratch), compiler_params=pltpu.CompilerParams(**params))


def _call(body, args, side=None, *, name, out_shape, grid, in_specs, out_specs, scratch=(), sem=None):
    if side is None:
        return _pcall(body, name=name, out_shape=out_shape, grid=grid, in_specs=in_specs, out_specs=out_specs,
                      scratch=scratch, sem=sem)(*args)
    ops_cls, x = side
    n_in, n_out, n_scr = len(in_specs), len(out_shape), len(scratch)
    steps = int(np.prod(grid))
    hbm = pl.BlockSpec(memory_space=pl.ANY)

    def wrapped(*refs):
        ins, x_ref = refs[:n_in], refs[n_in]
        outs, r_ref = refs[n_in + 1:n_in + 1 + n_out], refs[n_in + 1 + n_out]
        scr, sems = refs[n_in + 2 + n_out:n_in + 2 + n_out + n_scr], refs[n_in + 2 + n_out + n_scr:]
        ops = ops_cls(x_ref, r_ref, *sems)
        step = pl.program_id(0)
        for axis in range(1, len(grid)):
            step = step * grid[axis] + pl.program_id(axis)
        pl.when(step == 0)(ops.start)
        body(*ins, *outs, *scr)
        pl.when(step == (3 * steps) // 4)(ops.forward)
        pl.when(step == steps - 1)(ops.finish)

    return _pcall(
        wrapped, name=name, out_shape=list(out_shape) + [ops_cls.result(x)], grid=grid,
        in_specs=list(in_specs) + [hbm], out_specs=list(out_specs) + [hbm],
        scratch=list(scratch) + ops_cls.scratch(), sem=("arbitrary",) * len(grid))(*args, x)


def _tile(n, pref):
    if n <= pref:
        return n
    best = None
    for t in range(LANES, pref + 1, LANES):
        if n % t == 0:
            best = t
    assert best is not None, (n, pref)
    return best


def _rows(T):
    return min(ROW_TILE, T)


def _sds(shape, dtype=F32):
    return jax.ShapeDtypeStruct(shape, dtype)


def _row_spec(tm, c, col=0):
    return pl.BlockSpec((tm, c), lambda i, col=col: (i, col))


def _vec_spec(c, r=1):
    return pl.BlockSpec((r, c), lambda i: (0, 0))


def _sigmoid(x):
    return jax.nn.sigmoid(x)


def _silu(x):
    return x * _sigmoid(x)


def _dsilu(x):
    s = _sigmoid(x)
    return s * (1.0 + x * (1.0 - s))


def _gelu(x):
    return 0.5 * x * (1.0 + lax.erf(x * 0.7071067811865476))


def _dgelu(x):
    return 0.5 * (1.0 + lax.erf(x * 0.7071067811865476)) + x * jnp.exp(-0.5 * x * x) * 0.3989422804014327


def _dot(a, b, dims, precision=None):
    return lax.dot_general(a, b, (dims, ((), ())), precision=precision, preferred_element_type=F32)


def _nn(a, b, precision=None):
    return _dot(a, b, ((1,), (0,)), precision)


def _nt(a, b, precision=None):
    return _dot(a, b, ((1,), (1,)), precision)


def _tn(a, b, precision=None):
    return _dot(a, b, ((0,), (0,)), precision)


def _bf(x):
    return x.astype(BF16)


def _colsum(x):
    return jnp.sum(x, axis=0, keepdims=True)


def _rowsum(x):
    return jnp.sum(x, axis=1, keepdims=True)


def _allsum(x):
    return _colsum(_rowsum(x))


def _comm_call(ops_cls, x, name, from_vmem=False):
    def body(x_ref, out_ref, *sems):
        ops = ops_cls(x_ref, out_ref, *sems)
        ops.start()
        ops.forward()
        ops.finish()

    return pl.pallas_call(
        body, name=name, out_shape=ops_cls.result(x),
        in_specs=[pl.BlockSpec(memory_space=pltpu.VMEM if from_vmem else pl.ANY)],
        out_specs=pl.BlockSpec(memory_space=pl.ANY), scratch_shapes=ops_cls.scratch(),
    )(x)


def _all_gather(x, name):
    return _comm_call(_GatherOps, x, name, from_vmem=True)


class _GatherOps:
    def __init__(self, x_ref, out_ref, send_sems, recv_sems, local_sem):
        self.x_ref, self.out_ref = x_ref, out_ref
        self.send_sems, self.recv_sems, self.local_sem = send_sems, recv_sems, local_sem
        mx, my, mc = lax.axis_index("x"), lax.axis_index("y"), lax.axis_index("c")
        self.mc = mc
        self.me, self.sibling = (mx, my, mc), (mx, my, 1 - mc)
        self.chips = [(1 - mx, my), (mx, 1 - my), (1 - mx, 1 - my)]

    @staticmethod
    def result(x):
        return _sds((N_DEV,) + x.shape, x.dtype)

    @staticmethod
    def scratch():
        return [pltpu.SemaphoreType.DMA((7,)), pltpu.SemaphoreType.DMA((7,)), pltpu.SemaphoreType.DMA(())]

    def _slot(self, px, py, pc):
        return self.out_ref.at[4 * px + 2 * py + pc]

    def _copy(self, k, block, to, own=False):
        return pltpu.make_async_remote_copy(
            src_ref=self.x_ref if own else self._slot(*block), dst_ref=self._slot(*block),
            send_sem=self.send_sems.at[k], recv_sem=self.recv_sems.at[k], device_id=to, device_id_type=MESH)

    def _mine(self):
        return pltpu.make_async_copy(self.x_ref, self._slot(*self.me), self.local_sem)

    def _first(self):
        return [self._copy(0, self.me, self.sibling, own=True)] + [
            self._copy(1 + j, self.me, (*chip, self.mc), own=True) for j, chip in enumerate(self.chips)]

    def _passed(self):
        return [self._copy(4 + j, (*chip, self.mc), self.sibling) for j, chip in enumerate(self.chips)]

    def start(self):
        self._mine().start()
        for cp in self._first():
            cp.start()

    def forward(self):
        passed = self._passed()
        for j, chip in enumerate(self.chips):
            self._copy(1 + j, (*chip, self.mc), self.me).wait_recv()
            passed[j].start()

    def finish(self):
        self._copy(0, self.sibling, self.me).wait_recv()
        for j, chip in enumerate(self.chips):
            self._copy(4 + j, (*chip, 1 - self.mc), self.me).wait_recv()
        for cp in self._first() + self._passed():
            cp.wait_send()
        self._mine().wait()


N_CHIP = 4


class _SiblingOps:
    def __init__(self, p_ref, theirs_ref, send_sems, recv_sems):
        self.p_ref, self.theirs_ref, self.send_sems, self.recv_sems = p_ref, theirs_ref, send_sems, recv_sems

    @staticmethod
    def result(p):
        return _sds((N_CHIP,) + p.shape[1:], p.dtype)

    @staticmethod
    def scratch():
        return [pltpu.SemaphoreType.DMA((N_CHIP,))] * 2

    def _copies(self):
        mx, my, mc = lax.axis_index("x"), lax.axis_index("y"), lax.axis_index("c")
        return [pltpu.make_async_remote_copy(
            src_ref=self.p_ref.at[2 * chip + 1 - mc], dst_ref=self.theirs_ref.at[chip],
            send_sem=self.send_sems.at[chip], recv_sem=self.recv_sems.at[chip],
            device_id=(mx, my, 1 - mc), device_id_type=MESH) for chip in range(N_CHIP)]

    def start(self):
        for cp in self._copies():
            cp.start()

    def forward(self):
        pass

    def finish(self):
        for cp in self._copies():
            cp.wait()


class _ChipsOps:
    def __init__(self, q_ref, out_ref, send_sems, recv_sems, local_sem):
        self.q_ref, self.out_ref = q_ref, out_ref
        self.send_sems, self.recv_sems, self.local_sem = send_sems, recv_sems, local_sem

    @staticmethod
    def result(q):
        return _sds(q.shape, q.dtype)

    @staticmethod
    def scratch():
        return [pltpu.SemaphoreType.DMA((N_CHIP - 1,)), pltpu.SemaphoreType.DMA((N_CHIP - 1,)), pltpu.SemaphoreType.DMA(())]

    def _copies(self):
        mx, my, mc = lax.axis_index("x"), lax.axis_index("y"), lax.axis_index("c")
        me = 2 * mx + my
        local = pltpu.make_async_copy(self.q_ref.at[me], self.out_ref.at[me], self.local_sem)
        remote = []
        for r in range(1, N_CHIP):
            px = 1 - mx if r & 2 else mx
            py = 1 - my if r & 1 else my
            remote.append(pltpu.make_async_remote_copy(
                src_ref=self.q_ref.at[2 * px + py], dst_ref=self.out_ref.at[me],
                send_sem=self.send_sems.at[r - 1], recv_sem=self.recv_sems.at[r - 1],
                device_id=(px, py, mc), device_id_type=MESH))
        return local, remote

    def start(self):
        local, remote = self._copies()
        local.start()
        for cp in remote:
            cp.start()

    def forward(self):
        pass

    def finish(self):
        local, remote = self._copies()
        for cp in remote:
            cp.wait()
        local.wait()


def _pair_sum(p, theirs, name):
    n, R, C = theirs.shape
    tr = _tile_rows(R, 256)

    def body(p_ref, t_ref, o_ref):
        mc = lax.axis_index("c")
        o_ref[0] = (p_ref[0, mc] + t_ref[0]).astype(BF16)

    blk = pl.BlockSpec((1, tr, C), lambda s, i: (s, i, 0))
    return _pcall(body, name=name, out_shape=_sds((n, R, C), BF16), grid=(n, R // tr),
                  in_specs=[pl.BlockSpec((1, 2, tr, C), lambda s, i: (s, 0, i, 0)), blk],
                  out_specs=blk, sem=("parallel", "parallel"))(p.reshape(n, 2, R, C), theirs)


def _sum_parts(parts, name):
    P, R, C = parts.shape
    tr = _tile_rows(R, 256)

    def body(p_ref, o_ref):
        g = p_ref[0].astype(F32)
        for k in range(1, P):
            g = g + p_ref[k].astype(F32)
        o_ref[...] = g

    return _pcall(body, name=name, out_shape=_sds((R, C)), grid=(R // tr,),
                  in_specs=[pl.BlockSpec((P, tr, C), lambda i: (0, i, 0))],
                  out_specs=pl.BlockSpec((tr, C), lambda i: (i, 0)), sem=("parallel",))(parts)


def _adamw(parts, w, m, v, name):
    P, R, C = parts.shape
    tr = R if R <= 256 else _tile_rows(R, 256)

    def body(p_ref, w_ref, m_ref, v_ref, g_ref, d_ref, nm_ref, nv_ref):
        g = p_ref[0]
        for k in range(1, P):
            g = g + p_ref[k]
        nm = ADAM_B1 * m_ref[...] + (1.0 - ADAM_B1) * g
        nv = ADAM_B2 * v_ref[...] + (1.0 - ADAM_B2) * (g * g)
        m_hat = nm / (1.0 - ADAM_B1 ** ADAM_STEP)
        v_hat = nv / (1.0 - ADAM_B2 ** ADAM_STEP)
        g_ref[...] = g
        d_ref[...] = -ADAM_LR * (m_hat / (jnp.sqrt(v_hat) + ADAM_EPS) + ADAM_WD * w_ref[...])
        nm_ref[...] = nm
        nv_ref[...] = nv

    blk = pl.BlockSpec((tr, C), lambda i: (i, 0))
    return _pcall(
        body, name=name, out_shape=[_sds((R, C))] * 4, grid=(R // tr,),
        in_specs=[pl.BlockSpec((P, tr, C), lambda i: (0, i, 0)), blk, blk, blk],
        out_specs=[blk] * 4, sem=("parallel",))(parts, w, m, v)


def _tile_rows(n, pref):
    best = None
    for t in range(8, pref + 1, 8):
        if n % t == 0:
            best = t
    assert best is not None, (n, pref)
    return best


def _mm(pairs, mode, *, name, out_dtype=F32, bias=None, tn_pref=1024, side=None):
    M = pairs[0][0].shape[0]
    N = pairs[0][1].shape[1] if mode == "nn" else pairs[0][1].shape[0]
    tm, tn = _rows(M), _tile(N, tn_pref)
    n_pairs = len(pairs)
    has_bias = bias is not None

    def body(*refs):
        acc = _pairs_dot(refs[:2 * n_pairs], mode)
        if has_bias:
            acc = acc + refs[2 * n_pairs][...]
        refs[-1][...] = acc.astype(refs[-1].dtype)

    in_specs, args = _pair_specs(pairs, mode, tm, tn)
    if has_bias:
        in_specs.append(pl.BlockSpec((1, tn), lambda j, i: (0, j)))
        args.append(bias)
    res = _call(body, args, side, name=name, out_shape=[_sds((M, N), out_dtype)], grid=(N // tn, M // tm),
                in_specs=in_specs, out_specs=[pl.BlockSpec((tm, tn), lambda j, i: (i, j))], sem=("parallel", "parallel"))
    return res[0] if side is None else (res[0], res[1])


def _pairs_dot(ab, mode):
    acc = None
    for p in range(len(ab) // 2):
        a, b = _bf(ab[2 * p][...]), _bf(ab[2 * p + 1][...])
        d = _nn(a, b) if mode == "nn" else _nt(a, b)
        acc = d if acc is None else acc + d
    return acc


def _pair_specs(pairs, mode, tm, tn):
    in_specs, args = [], []
    for a, b in pairs:
        K = a.shape[1]
        in_specs.append(pl.BlockSpec((tm, K), lambda j, i: (i, 0)))
        if mode == "nn":
            in_specs.append(pl.BlockSpec((K, tn), lambda j, i: (0, j)))
        else:
            in_specs.append(pl.BlockSpec((tn, K), lambda j, i: (j, 0)))
        args += [a, b]
    return in_specs, args


def _mm_resid(pairs, resid, gvec, *, name, bias=None, norm=None):
    M, N = resid.shape
    tm = _rows(M)
    n_pairs = len(pairs)
    has_bias, has_norm = bias is not None, norm is not None

    def body(*refs):
        acc = _pairs_dot(refs[:2 * n_pairs], "nn")
        pos = 2 * n_pairs
        if has_bias:
            acc = acc + refs[pos][...]
            pos += 1
        xv = refs[pos][...] + refs[pos + 1][...] * acc
        outs = refs[pos + 2 + 3 * has_norm:]
        outs[0][...] = xv
        outs[1][...] = acc.astype(BF16)
        if has_norm:
            w_ref, sc_ref, sh_ref = refs[pos + 2:pos + 5]
            r = lax.rsqrt(jnp.mean(xv * xv, axis=-1, keepdims=True) + EPS)
            outs[2][...] = ((xv * r * w_ref[...]) * (1.0 + sc_ref[...]) + sh_ref[...]).astype(BF16)

    in_specs, args = _pair_specs(pairs, "nn", tm, N)
    vec = pl.BlockSpec((1, N), lambda j, i: (0, 0))
    row = pl.BlockSpec((tm, N), lambda j, i: (i, 0))
    if has_bias:
        in_specs.append(vec)
        args.append(bias)
    in_specs += [row, vec] + [vec] * (3 * has_norm)
    args += [resid, gvec] + (list(norm) if has_norm else [])
    return _pcall(body, name=name, out_shape=[_sds((M, N)), _sds((M, N), BF16)] + [_sds((M, N), BF16)] * has_norm,
                  grid=(1, M // tm), in_specs=in_specs, out_specs=[row] * (2 + has_norm),
                  sem=("parallel", "parallel"))(*args)


def _mm_tn(a, b, *, name, tm_pref=1408, tn_pref=1536):
    K, M = a.shape
    N = b.shape[1]
    tm, tn = _tile(M, tm_pref), _tile(N, tn_pref)
    tk = K if K <= 2 * ROW_TILE else 2 * ROW_TILE

    def body(a_ref, b_ref, o_ref):
        @pl.when(pl.program_id(2) == 0)
        def _():
            o_ref[...] = jnp.zeros_like(o_ref)
        o_ref[...] += _tn(_bf(a_ref[...]), _bf(b_ref[...]))

    return _pcall(
        body, name=name, out_shape=_sds((M, N)), grid=(M // tm, N // tn, K // tk),
        in_specs=[pl.BlockSpec((tk, tm), lambda i, j, k: (k, i)), pl.BlockSpec((tk, tn), lambda i, j, k: (k, j))],
        out_specs=pl.BlockSpec((tm, tn), lambda i, j, k: (i, j)),
        sem=("parallel", "parallel", "arbitrary"))(a, b)


def _mm_swiglu(h, wg_t, wu_t, name, side=None):
    M, K = h.shape
    N = wg_t.shape[0]
    tm, tn = _rows(M), _tile(N, 1408)

    def body(h_ref, wg_ref, wu_ref, gate_ref, up_ref, act_ref):
        hv = _bf(h_ref[...])
        gate = _nt(hv, wg_ref[...])
        up = _nt(hv, wu_ref[...])
        gate_ref[...] = gate.astype(BF16)
        up_ref[...] = up.astype(BF16)
        act_ref[...] = (_silu(gate) * up).astype(BF16)

    w_spec = pl.BlockSpec((tn, K), lambda j, i: (j, 0))
    o_spec = pl.BlockSpec((tm, tn), lambda j, i: (i, j))
    return _call(body, (h, wg_t, wu_t), side, name=name,
                 out_shape=[_sds((M, N), BF16)] * 3, grid=(N // tn, M // tm),
                 in_specs=[pl.BlockSpec((tm, K), lambda j, i: (i, 0)), w_spec, w_spec], out_specs=[o_spec] * 3,
                 sem=("parallel", "parallel"))


def _mm_swiglu_bwd(dout, wd, gate, up, name, side=None):
    M, K = dout.shape
    N = wd.shape[0]
    tm, tn = _rows(M), _tile(N, 1408)

    def body(d_ref, wd_ref, gate_ref, up_ref, dg_ref, du_ref):
        dact = _nt(_bf(d_ref[...]), wd_ref[...])
        g = gate_ref[...].astype(F32)
        dg_ref[...] = (dact * up_ref[...].astype(F32) * _dsilu(g)).astype(BF16)
        du_ref[...] = (dact * _silu(g)).astype(BF16)

    t_spec = pl.BlockSpec((tm, tn), lambda j, i: (i, j))
    return _call(
        body, (dout, wd, gate, up), side, name=name, out_shape=[_sds((M, N), BF16)] * 2, grid=(N // tn, M // tm),
        in_specs=[pl.BlockSpec((tm, K), lambda j, i: (i, 0)), pl.BlockSpec((tn, K), lambda j, i: (j, 0)), t_spec, t_spec],
        out_specs=[t_spec] * 2, sem=("parallel", "parallel"))


def _norm_mod(x, w, sc, sh, name):
    T = x.shape[0]
    tm = _rows(T)

    def body(x_ref, w_ref, sc_ref, sh_ref, o_ref):
        xv = x_ref[...]
        r = lax.rsqrt(jnp.mean(xv * xv, axis=-1, keepdims=True) + EPS)
        o_ref[...] = ((xv * r * w_ref[...]) * (1.0 + sc_ref[...]) + sh_ref[...]).astype(BF16)

    return _pcall(body, name=name, out_shape=_sds((T, D), BF16), grid=(T // tm,),
                  in_specs=[_row_spec(tm, D), _vec_spec(D), _vec_spec(D), _vec_spec(D)],
                  out_specs=_row_spec(tm, D), sem=("parallel",))(x, w, sc, sh)


def _gate_rows(dxv, br_ref, g_ref, db_ref, acc_ref):
    db = g_ref[...] * dxv
    db_ref[...] = db.astype(BF16)
    acc_ref[3:4, :] += _colsum(dxv * br_ref[...].astype(F32))
    acc_ref[4:5, :] += _colsum(db)


def _norm_mod_bwd(x, dh, dres, w, sc, branch, g, name, side=None):
    T = x.shape[0]
    tm = _rows(T)
    gated = branch is not None

    def body(x_ref, dh_ref, dres_ref, w_ref, sc_ref, *rest):
        (br_ref, g_ref, dx_ref, db_ref, acc_ref) = rest if gated else (None, None, rest[0], None, rest[1])

        @pl.when(pl.program_id(0) == 0)
        def _():
            acc_ref[...] = jnp.zeros_like(acc_ref)
        xv, dh_v, wv = x_ref[...], dh_ref[...], w_ref[...]
        r = lax.rsqrt(jnp.mean(xv * xv, axis=-1, keepdims=True) + EPS)
        n = xv * r
        dnw = dh_v * (1.0 + sc_ref[...])
        dn = dnw * wv
        dxv = dres_ref[...] + r * (dn - n * jnp.mean(dn * n, axis=-1, keepdims=True))
        dx_ref[...] = dxv
        acc_ref[0:1, :] += _colsum(dh_v * (n * wv))
        acc_ref[1:2, :] += _colsum(dh_v)
        acc_ref[2:3, :] += _colsum(dnw * n)
        if gated:
            _gate_rows(dxv, br_ref, g_ref, db_ref, acc_ref)

    row = _row_spec(tm, D)
    args = (x, dh, dres, w, sc) + ((branch, g) if gated else ())
    return _call(body, args, side, name=name,
                 out_shape=[_sds((T, D))] + ([_sds((T, D), BF16)] if gated else []) + [_sds((8, D))], grid=(T // tm,),
                 in_specs=[row, row, row, _vec_spec(D), _vec_spec(D)] + ([row, _vec_spec(D)] if gated else []),
                 out_specs=[row] + ([row] if gated else []) + [_vec_spec(D, 8)], sem=("arbitrary",))


def _final_loss(x, wf, target, branch, g, name):
    T = x.shape[0]
    tm = _rows(T)

    def body(x_ref, w_ref, t_ref, br_ref, g_ref, dx_ref, db_ref, acc_ref):
        @pl.when(pl.program_id(0) == 0)
        def _():
            acc_ref[...] = jnp.zeros_like(acc_ref)
        xv, wv = x_ref[...], w_ref[...]
        r = lax.rsqrt(jnp.mean(xv * xv, axis=-1, keepdims=True) + EPS)
        n = xv * r
        err = n * wv - t_ref[...]
        dy = err * (1.0 / D)
        dn = dy * wv
        dxv = r * (dn - n * jnp.mean(dn * n, axis=-1, keepdims=True))
        dx_ref[...] = dxv
        acc_ref[0:1, :] += _colsum(dy * n)
        acc_ref[1:2, :] += jnp.broadcast_to(_allsum(err * err) * (0.5 / D), (1, D))
        _gate_rows(dxv, br_ref, g_ref, db_ref, acc_ref)

    row = _row_spec(tm, D)
    return _pcall(body, name=name, out_shape=[_sds((T, D)), _sds((T, D), BF16), _sds((8, D))], grid=(T // tm,),
                  in_specs=[row, _vec_spec(D), row, row, _vec_spec(D)],
                  out_specs=[row, row, _vec_spec(D, 8)], sem=("arbitrary",))(x, wf, target, branch, g)


def _mod_matmul(c_all, ada_w, name):
    n = ada_w.shape[2]

    def body(c_ref, w_ref, cond_ref, o_ref):
        cond = _silu(c_ref[...])
        cond_ref[...] = cond
        o_ref[0] = _nn(cond, w_ref[0])

    return _pcall(body, name=name, out_shape=[_sds((N_DEV, D)), _sds((DEPTH, N_DEV, n))], grid=(DEPTH,),
                  in_specs=[pl.BlockSpec((N_DEV, D), lambda l: (0, 0)), pl.BlockSpec((1, D, n), lambda l: (l, 0, 0))],
                  out_specs=[pl.BlockSpec((N_DEV, D), lambda l: (0, 0)), pl.BlockSpec((1, N_DEV, n), lambda l: (l, 0, 0))],
                  sem=("arbitrary",))(c_all, ada_w)


def _add_rows(a, b, name):
    def body(a_ref, b_ref, o_ref):
        o_ref[...] = a_ref[...] + b_ref[...]

    return _pcall(body, name=name, out_shape=_sds(a.shape))(a, b)


def _ada_w_grad(cond, dmod_cols, name):
    n = dmod_cols.shape[2]

    def body(c_ref, d_ref, o_ref):
        o_ref[0] = _tn(c_ref[...], d_ref[0])

    return _pcall(body, name=name, out_shape=_sds((DEPTH, D, n)), grid=(DEPTH,),
                  in_specs=[pl.BlockSpec((N_DEV, D), lambda l: (0, 0)), pl.BlockSpec((1, N_DEV, n), lambda l: (l, 0, 0))],
                  out_specs=pl.BlockSpec((1, D, n), lambda l: (l, 0, 0)), sem=("parallel",))(cond, dmod_cols)


def _conv_fwd(pm, conv_w, conv_b, name):
    T = pm.shape[0]
    tm = _rows(T)
    C = CONV_DIM

    def body(x_ref, prev_ref, w_ref, b_ref, o_ref):
        cur = x_ref[...]
        prev = jnp.where(pl.program_id(0) > 0, prev_ref[...], 0.0)
        cur8 = cur[0:8]
        row8 = lax.broadcasted_iota(jnp.int32, (8, C), 0)
        full = w_ref[3:4, :] * cur
        head = w_ref[3:4, :] * cur8
        for k in range(1, SSM_CONV):
            wk = w_ref[3 - k:4 - k, :]
            full = full + wk * pltpu.roll(cur, k, 0)
            head = head + wk * jnp.where(row8 < k, pltpu.roll(prev, k, 0), pltpu.roll(cur8, k, 0))
        o_ref[...] = full + b_ref[...]
        o_ref[0:8, :] = head + b_ref[...]

    return _pcall(
        body, name=name, out_shape=_sds((T, C)), grid=(T // tm,),
        in_specs=[pl.BlockSpec((tm, C), lambda i: (i, 2)),
                  pl.BlockSpec((8, C), lambda i: (jnp.maximum(i * (tm // 8) - 1, 0), 2)),
                  _vec_spec(C, SSM_CONV), _vec_spec(C)],
        out_specs=_row_spec(tm, C), sem=("parallel",))(pm, pm, conv_w, conv_b)


def _conv_bwd(dc, pm, conv_w, name):
    T = dc.shape[0]
    tm = _rows(T)
    C = CONV_DIM
    nt = T // tm

    def body(dc_ref, nxt_ref, x_ref, prev_ref, w_ref, dx_ref, acc_ref):
        i = pl.program_id(0)

        @pl.when(i == 0)
        def _():
            acc_ref[...] = jnp.zeros_like(acc_ref)
        dcv = dc_ref[...]
        nxt = jnp.where(i < nt - 1, nxt_ref[...], 0.0)
        xc = x_ref[...]
        prev = jnp.where(i > 0, prev_ref[...], 0.0)
        dc8h, dc8t, x8 = dcv[0:8], dcv[tm - 8:tm], xc[0:8]
        row8 = lax.broadcasted_iota(jnp.int32, (8, C), 0)
        full = w_ref[3:4, :] * dcv
        tail = w_ref[3:4, :] * dc8t
        acc_ref[3:4, :] += _colsum(dcv * xc)
        for k in range(1, SSM_CONV):
            wk = w_ref[3 - k:4 - k, :]
            full = full + wk * pltpu.roll(dcv, tm - k, 0)
            tail = tail + wk * jnp.where(row8 + k >= 8, pltpu.roll(nxt, 8 - k, 0), pltpu.roll(dc8t, 8 - k, 0))
            xs_head = jnp.where(row8 < k, pltpu.roll(prev, k, 0), pltpu.roll(x8, k, 0))
            prod = dcv * pltpu.roll(xc, k, 0)
            acc_ref[3 - k:4 - k, :] += _colsum(prod) - _colsum(prod[0:8]) + _colsum(dc8h * xs_head)
        acc_ref[4:5, :] += _colsum(dcv)
        dx_ref[...] = jnp.concatenate([full[0:tm - 8], tail], axis=0).astype(BF16)

    return _pcall(
        body, name=name, out_shape=[_sds((T, C), BF16), _sds((8, C))], grid=(nt,),
        in_specs=[_row_spec(tm, C),
                  pl.BlockSpec((8, C), lambda i: (jnp.minimum((i + 1) * (tm // 8), T // 8 - 1), 0)),
                  pl.BlockSpec((tm, C), lambda i: (i, 2)),
                  pl.BlockSpec((8, C), lambda i: (jnp.maximum(i * (tm // 8) - 1, 0), 2)),
                  _vec_spec(C, SSM_CONV)],
        out_specs=[_row_spec(tm, C), _vec_spec(C, 8)], sem=("arbitrary",))(dc, dc, pm, pm, conv_w)


def _ssd_prologue(cpre, dtr, dtb, alog):
    L = CHUNK
    xc = _silu(cpre)
    pre = dtr + dtb
    dt = jnp.maximum(pre, 0.0) + jnp.log1p(jnp.exp(-jnp.abs(pre)))
    a = -jnp.exp(alog)
    la = dt * a
    row = lax.broadcasted_iota(jnp.int32, (L, L), 0)
    col = lax.broadcasted_iota(jnp.int32, (L, L), 1)
    causal = row >= col
    tri = causal.astype(F32)
    lc = _nn(tri, la, HIGHEST)
    return xc, pre, dt, a, causal, tri, lc, row, col


def _head_indicator():
    m = np.zeros((LANES, SSM_INNER), np.float32)
    for h in range(SSM_HEADS):
        m[h, h * SSM_HEAD_DIM:(h + 1) * SSM_HEAD_DIM] = 1.0
    return jnp.asarray(m, dtype=BF16)


def _split_dot(x, ind, dims):
    hi = x.astype(BF16)
    lo = (x - hi.astype(F32)).astype(BF16)
    return _dot(hi, ind, dims) + _dot(lo, ind, dims)


def _expand(x16, ind):
    return _split_dot(x16, ind, ((1,), (0,)))


def _headsum(x, ind):
    return _split_dot(x, ind, ((1,), (1,)))


def _ssd_fwd(cpre, dtr, pm, dtb, alog, dskip, normw, ind, name, side=None):
    T = cpre.shape[0]
    nc = T // CHUNK
    L, P, H, HPG, N = CHUNK, SSM_HEAD_DIM, SSM_HEADS, SSM_HEADS // SSM_GROUPS, SSM_STATE
    half = SSM_INNER // SSM_GROUPS

    def body(cp_ref, dtr_ref, z_ref, dtb_ref, alog_ref, dskip_ref, nw_ref, ind_ref, ya_ref, y_ref, sp_ref, st_ref):
        @pl.when(pl.program_id(0) == 0)
        def _():
            st_ref[...] = jnp.zeros_like(st_ref)
        xc, _, dt, _, causal, _, lc, _, _ = _ssd_prologue(cp_ref[...], dtr_ref[...], dtb_ref[...], alog_ref[...])
        lct = lc.T
        ind = ind_ref[...]
        llast = lc[L - 1:L, :]
        xs = xc[:, :SSM_INNER]
        xd = xs * _expand(dt, ind)
        ex = _expand(jnp.exp(lc), ind)
        xd_end = _bf(xd * _expand(jnp.exp(llast - lc), ind))
        cdx = _expand(jnp.broadcast_to(jnp.exp(llast), (8, LANES)), ind)[0:1]
        xdb = _bf(xd)
        sp_ref[0] = st_ref[...]
        for g in range(SSM_GROUPS):
            sl = slice(g * half, (g + 1) * half)
            bm = _bf(xc[:, SSM_INNER + g * N:SSM_INNER + (g + 1) * N])
            cm = _bf(xc[:, SSM_INNER + (SSM_GROUPS + g) * N:SSM_INNER + (SSM_GROUPS + g + 1) * N])
            cb = _nt(cm, bm)
            st = st_ref[g]
            y_ref[:, sl] = ex[:, sl] * _nn(cm, _bf(st)) + dskip_ref[:, sl] * xs[:, sl]
            st_ref[g] = st * cdx[:, sl] + _tn(bm, xd_end[:, sl])
            for j in range(HPG):
                h = g * HPG + j
                decay = jnp.where(causal, jnp.exp(jnp.where(causal, lc[:, h:h + 1] - lct[h:h + 1, :], 0.0)), 0.0)
                y_ref[:, h * P:(h + 1) * P] += _nn(_bf(cb * decay), xdb[:, h * P:(h + 1) * P])
        y2 = y_ref[...] * _silu(z_ref[...])
        for g in range(SSM_GROUPS):
            yg = y2[:, g * half:(g + 1) * half]
            r = lax.rsqrt(jnp.mean(yg * yg, axis=-1, keepdims=True) + EPS)
            ya_ref[:, g * half:(g + 1) * half] = (yg * r * nw_ref[:, g * half:(g + 1) * half]).astype(BF16)

    return _call(
        body, (cpre, dtr, pm, dtb, alog, dskip, normw, ind), side, name=name,
        out_shape=[_sds((T, SSM_INNER), BF16), _sds((T, SSM_INNER)), _sds((nc, SSM_GROUPS, N, half))], grid=(nc,),
        in_specs=[_row_spec(L, CONV_DIM), _row_spec(L, LANES), _row_spec(L, SSM_INNER, 0),
                  _vec_spec(LANES), _vec_spec(LANES), _vec_spec(SSM_INNER), _vec_spec(SSM_INNER), _vec_spec(SSM_INNER, LANES)],
        out_specs=[_row_spec(L, SSM_INNER), _row_spec(L, SSM_INNER),
                   pl.BlockSpec((1, SSM_GROUPS, N, half), lambda i: (i, 0, 0, 0))],
        scratch=[pltpu.VMEM((SSM_GROUPS, N, half), F32)], sem=("arbitrary",))


def _ssd_bwd(cpre, dtr, pm, ypre, sprev, dya, dtb, alog, dskip, normw, ind, name, side=None):
    T = cpre.shape[0]
    nc = T // CHUNK
    L, P, H, HPG, N = CHUNK, SSM_HEAD_DIM, SSM_HEADS, SSM_HEADS // SSM_GROUPS, SSM_STATE
    half = SSM_INNER // SSM_GROUPS

    def body(cp_ref, dtr_ref, z_ref, y_ref, sp_ref, dya_ref, dtb_ref, alog_ref, dskip_ref, nw_ref, ind_ref,
             dz_ref, dcp_ref, ddtr_ref, acc_ref, dnw_ref, ds_ref, dy_ref, dxd_ref, rr_ref, yoff_ref, dcd_ref):
        @pl.when(pl.program_id(0) == 0)
        def _():
            ds_ref[...] = jnp.zeros_like(ds_ref)
            acc_ref[...] = jnp.zeros_like(acc_ref)
            dnw_ref[...] = jnp.zeros_like(dnw_ref)
        cpre_v = cp_ref[...]
        xc, pre, dt, a, causal, tri, lc, row, col = _ssd_prologue(cpre_v, dtr_ref[...], dtb_ref[...], alog_ref[...])
        lct = lc.T
        zv, yv = z_ref[...], y_ref[...]
        sz = _silu(zv)
        y2 = yv * sz
        dya_v = dya_ref[...]
        nwv = nw_ref[...]
        for g in range(SSM_GROUPS):
            sl = slice(g * half, (g + 1) * half)
            yg = y2[:, sl]
            r = lax.rsqrt(jnp.mean(yg * yg, axis=-1, keepdims=True) + EPS)
            nrm = yg * r
            dnw_ref[:, sl] += _colsum(dya_v[:, sl] * nrm)
            dn = dya_v[:, sl] * nwv[:, sl]
            dy2 = r * (dn - nrm * jnp.mean(dn * nrm, axis=-1, keepdims=True))
            dy_ref[:, sl] = dy2 * sz[:, sl]
            dz_ref[:, sl] = (dy2 * yv[:, sl] * _dsilu(zv[:, sl])).astype(BF16)
        ind = ind_ref[...]
        llast = lc[L - 1:L, :]
        dte16 = jnp.exp(llast - lc)
        cd16 = jnp.exp(llast)
        xs = xc[:, :SSM_INNER]
        dtx = _expand(dt, ind)
        ex = _expand(jnp.exp(lc), ind)
        dtex = _expand(dte16, ind)
        cdx = _expand(jnp.broadcast_to(cd16, (8, LANES)), ind)[0:1]
        xd = xs * dtx
        xdb = _bf(xd)
        xd_end = _bf(xd * dtex)
        dyv = dy_ref[...]
        dy_off = _bf(ex * dyv)
        dyb = _bf(dyv)
        dskx = dskip_ref[...]
        lane_c = lax.broadcasted_iota(jnp.int32, (L, LANES), 1)
        lane1 = lax.broadcasted_iota(jnp.int32, (1, LANES), 1)
        sub16 = lax.broadcasted_iota(jnp.int32, (H, L), 0)
        dlc_c = jnp.zeros((L, LANES), F32)
        dlc_r = jnp.zeros((H, L), F32)
        for g in range(SSM_GROUPS):
            sl = slice(g * half, (g + 1) * half)
            b_lo = SSM_INNER + g * N
            c_lo = SSM_INNER + (SSM_GROUPS + g) * N
            bm, cm = _bf(xc[:, b_lo:b_lo + N]), _bf(xc[:, c_lo:c_lo + N])
            cb = _nt(cm, bm)
            st, dst = sp_ref[0, g], ds_ref[g]
            stb, dstb = _bf(st), _bf(dst)
            dcm = _nt(dy_off[:, sl], stb)
            ds_ref[g] = _tn(cm, dy_off[:, sl]) + dst * cdx[:, sl]
            rr_ref[:, sl] = _nn(bm, dstb)
            yoff_ref[:, sl] = ex[:, sl] * _nn(cm, stb)
            db = _nt(xd_end[:, sl], dstb)
            dcd_ref[:, sl] = _colsum(dst * st)
            dcb = jnp.zeros((L, L), F32)
            for j in range(HPG):
                h = g * HPG + j
                hs = slice(h * P, (h + 1) * P)
                decay = jnp.where(causal, jnp.exp(jnp.where(causal, lc[:, h:h + 1] - lct[h:h + 1, :], 0.0)), 0.0)
                m = cb * decay
                dxd_ref[:, hs] = _tn(_bf(m), dyb[:, hs])
                dm = _nt(dyb[:, hs], xdb[:, hs])
                dcb = dcb + dm * decay
                gm = dm * m
                dlc_c = dlc_c + jnp.where(lane_c == h, _rowsum(gm), 0.0)
                dlc_r = dlc_r + jnp.where(sub16 == h, _colsum(gm), 0.0)
            dcbb = _bf(dcb)
            dcp_ref[:, c_lo:c_lo + N] = dcm + _nn(dcbb, bm)
            dcp_ref[:, b_lo:b_lo + N] = db + _tn(dcbb, cm)
        dxd_diag, rr = dxd_ref[...], rr_ref[...]
        tt = _headsum(rr * xd, ind) * dte16
        dlc_rt = jnp.concatenate([dlc_r, jnp.zeros((LANES - H, L), F32)], axis=0).T
        dlc = dlc_c - dlc_rt + _headsum(dyv * yoff_ref[...], ind) - tt
        dcd = _headsum(jnp.broadcast_to(dcd_ref[...], (8, SSM_INNER)), ind)[0:1]
        dlc = dlc + jnp.where(row == L - 1, _colsum(tt) + dcd * cd16, 0.0)
        dla = _tn(tri, dlc, HIGHEST)
        dxd = dxd_diag + dtex * rr
        ddt = _headsum(dxd * xs, ind) + dla * a
        ddtr = jnp.where(lane_c < H, ddt * _sigmoid(pre), 0.0)
        ddtr_ref[...] = ddtr
        acc_ref[0:1, :] += _colsum(ddtr)
        acc_ref[1:2, :] += jnp.where(lane1 < H, _colsum(dla * dt) * a, 0.0)
        acc_ref[2:3, :] += _headsum(jnp.broadcast_to(_colsum(dyv * xs), (8, SSM_INNER)), ind)[0:1]
        dcp_ref[:, 0:SSM_INNER] = dxd * dtx + dskx * dyv
        dcp_ref[...] = dcp_ref[...] * _dsilu(cpre_v)

    rev = lambda i: (nc - 1 - i, 0)
    rspec = lambda c: pl.BlockSpec((L, c), rev)
    return _call(
        body, (cpre, dtr, pm, ypre, sprev, dya, dtb, alog, dskip, normw, ind), side, name=name,
        out_shape=[_sds((T, SSM_INNER), BF16), _sds((T, CONV_DIM)), _sds((T, LANES)), _sds((8, LANES)), _sds((1, SSM_INNER))],
        grid=(nc,),
        in_specs=[rspec(CONV_DIM), rspec(LANES), rspec(SSM_INNER), rspec(SSM_INNER),
                  pl.BlockSpec((1, SSM_GROUPS, N, half), lambda i: (nc - 1 - i, 0, 0, 0)), rspec(SSM_INNER),
                  _vec_spec(LANES), _vec_spec(LANES), _vec_spec(SSM_INNER), _vec_spec(SSM_INNER), _vec_spec(SSM_INNER, LANES)],
        out_specs=[rspec(SSM_INNER), rspec(CONV_DIM), rspec(LANES), _vec_spec(LANES, 8), _vec_spec(SSM_INNER)],
        scratch=[pltpu.VMEM((SSM_GROUPS, N, half), F32), pltpu.VMEM((L, SSM_INNER), F32), pltpu.VMEM((L, SSM_INNER), F32),
                 pltpu.VMEM((L, SSM_INNER), F32), pltpu.VMEM((L, SSM_INNER), F32), pltpu.VMEM((1, SSM_INNER), F32)],
        sem=("arbitrary",))


def _gmlp_common(u, v, lnw, lnb):
    ug = _gelu(u)
    vg = _gelu(v)
    mu = jnp.mean(vg, axis=-1, keepdims=True)
    cen = vg - mu
    rstd = lax.rsqrt(jnp.mean(cen * cen, axis=-1, keepdims=True) + EPS)
    vhat = cen * rstd
    return ug, rstd, vhat, vhat * lnw + lnb


def _causal_mask():
    row = lax.broadcasted_iota(jnp.int32, (CHUNK, CHUNK), 0)
    col = lax.broadcasted_iota(jnp.int32, (CHUNK, CHUNK), 1)
    return row >= col


def _gmlp_fwd(pm, lnw, lnb, ws, bs_exp, name, side=None):
    T = pm.shape[0]
    nc = T // CHUNK
    L, G = CHUNK, GMLP_GROUPS

    def body(u_ref, v_ref, lnw_ref, lnb_ref, ws_ref, bs_ref, o_ref):
        ug, _, _, vn = _gmlp_common(u_ref[...], v_ref[...], lnw_ref[...], lnb_ref[...])
        causal = _causal_mask()
        for g in range(G):
            sl = slice(g * L, (g + 1) * L)
            wm = _bf(jnp.where(causal, ws_ref[g], 0.0))
            sv = _nn(wm, _bf(vn[:, sl])) + bs_ref[:, sl]
            o_ref[:, sl] = (ug[:, sl] * sv).astype(BF16)

    return _call(
        body, (pm, pm, lnw, lnb, ws, bs_exp), side, name=name, out_shape=[_sds((T, GMLP_INNER), BF16)], grid=(nc,),
        in_specs=[_row_spec(L, GMLP_INNER, 1), _row_spec(L, GMLP_INNER, 2), _vec_spec(GMLP_INNER), _vec_spec(GMLP_INNER),
                  pl.BlockSpec((G, L, L), lambda i: (0, 0, 0)), _vec_spec(GMLP_INNER, L)],
        out_specs=[_row_spec(L, GMLP_INNER)], sem=("parallel",))


def _gmlp_bwd(pm, dyb, lnw, lnb, ws, bs_exp, name, side=None):
    T = pm.shape[0]
    nc = T // CHUNK
    L, G = CHUNK, GMLP_GROUPS

    def body(u_ref, v_ref, dy_ref, lnw_ref, lnb_ref, ws_ref, bs_ref, du_ref, dv_ref, dws_ref, dbs_ref, acc_ref, dvn_ref):
        @pl.when(pl.program_id(0) == 0)
        def _():
            dws_ref[...] = jnp.zeros_like(dws_ref)
            dbs_ref[...] = jnp.zeros_like(dbs_ref)
            acc_ref[...] = jnp.zeros_like(acc_ref)
        uv, vv, dyv, lnwv = u_ref[...], v_ref[...], dy_ref[...], lnw_ref[...]
        ug, rstd, vhat, vn = _gmlp_common(uv, vv, lnwv, lnb_ref[...])
        causal = _causal_mask()
        lane = lax.broadcasted_iota(jnp.int32, (L, LANES), 1)
        dbs = jnp.zeros((L, LANES), F32)
        for g in range(G):
            sl = slice(g * L, (g + 1) * L)
            wm = _bf(jnp.where(causal, ws_ref[g], 0.0))
            vng = _bf(vn[:, sl])
            sv = _nn(wm, vng) + bs_ref[:, sl]
            du_ref[:, sl] = (dyv[:, sl] * sv * _dgelu(uv[:, sl])).astype(BF16)
            dsv = dyv[:, sl] * ug[:, sl]
            dsvb = _bf(dsv)
            dws_ref[g] += jnp.where(causal, _nt(dsvb, vng), 0.0)
            dbs = dbs + jnp.where(lane == g, _rowsum(dsv), 0.0)
            dvn_ref[:, sl] = _tn(wm, dsvb)
        dbs_ref[...] += dbs
        dvn = dvn_ref[...]
        acc_ref[0:1, :] += _colsum(dvn * vhat)
        acc_ref[1:2, :] += _colsum(dvn)
        dvh = dvn * lnwv
        dvg = rstd * (dvh - jnp.mean(dvh, axis=-1, keepdims=True) - vhat * jnp.mean(dvh * vhat, axis=-1, keepdims=True))
        dv_ref[...] = (dvg * _dgelu(vv)).astype(BF16)

    return _call(
        body, (pm, pm, dyb, lnw, lnb, ws, bs_exp), side, name=name,
        out_shape=[_sds((T, GMLP_INNER), BF16), _sds((T, GMLP_INNER), BF16), _sds((G, L, L)), _sds((L, LANES)), _sds((8, GMLP_INNER))],
        grid=(nc,),
        in_specs=[_row_spec(L, GMLP_INNER, 1), _row_spec(L, GMLP_INNER, 2), _row_spec(L, GMLP_INNER),
                  _vec_spec(GMLP_INNER), _vec_spec(GMLP_INNER), pl.BlockSpec((G, L, L), lambda i: (0, 0, 0)),
                  _vec_spec(GMLP_INNER, L)],
        out_specs=[_row_spec(L, GMLP_INNER), _row_spec(L, GMLP_INNER), pl.BlockSpec((G, L, L), lambda i: (0, 0, 0)),
                   _vec_spec(LANES, L), _vec_spec(GMLP_INNER, 8)],
        scratch=[pltpu.VMEM((L, GMLP_INNER), F32)], sem=("arbitrary",))


def _rel_buckets():
    qi = np.arange(CHUNK)[:, None]
    sj = np.arange(2 * CHUNK)[None, :]
    dist = np.maximum(qi + CHUNK - sj, 0)
    max_exact = REL_BUCKETS // 2
    log_ratio = (np.log(np.maximum(dist, 1).astype(np.float32) / np.float32(max_exact))
                 / np.float32(math.log(REL_MAX_DIST / max_exact))).astype(np.float32)
    large = max_exact + (log_ratio * np.float32(REL_BUCKETS - max_exact)).astype(np.int32)
    return np.where(dist < max_exact, dist, np.minimum(large, REL_BUCKETS - 1))


def _bucket_onehot_t():
    bucket = _rel_buckets().reshape(-1)
    return jnp.asarray((np.arange(REL_BUCKETS)[:, None] == bucket[None, :]).astype(np.float32))


def _bias_from_table(table_t, onehot_t, name):
    def body(t_ref, o_ref, out_ref):
        out_ref[...] = _nn(t_ref[...], o_ref[...], HIGHEST)

    return _pcall(body, name=name, out_shape=_sds((ATTN_HEADS, onehot_t.shape[1])))(table_t, onehot_t)


def _table_from_dbias(dbias, onehot_t, name):
    def body(d_ref, o_ref, out_ref):
        out_ref[...] = _nt(d_ref[...], o_ref[...], HIGHEST)

    return _pcall(body, name=name, out_shape=_sds((ATTN_HEADS, REL_BUCKETS)))(dbias, onehot_t)


def _softmax_sink(logits, sink, mask):
    logits = jnp.where(mask, logits, NEG_INF)
    mx = jnp.maximum(jnp.max(logits, axis=-1, keepdims=True), sink)
    e = jnp.exp(logits - mx)
    es = jnp.exp(sink - mx)
    inv = 1.0 / (_rowsum(e) + es)
    return e * inv, es * inv


def _attn_mask(n, heads):
    qi = lax.broadcasted_iota(jnp.int32, (heads * CHUNK, 2 * CHUNK), 0) & (CHUNK - 1)
    sj = lax.broadcasted_iota(jnp.int32, (heads * CHUNK, 2 * CHUNK), 1)
    rel = qi + CHUNK - sj
    return (rel >= 0) & (rel < CHUNK) & ((sj >= CHUNK) | (n > 0))


def _stack_heads(ref, first, count, width):
    return jnp.concatenate([_bf(ref[:, (first + j) * width:(first + j + 1) * width]) for j in range(count)], axis=0)


def _attn_fwd(qkv, bias, sinks, name):
    T = qkv.shape[0]
    nb = T // CHUNK
    L, DH, HPK = CHUNK, ATTN_DH, ATTN_HEADS // ATTN_KV
    scale = DH ** -0.5
    kcol, vcol = ATTN_HEADS * DH // LANES, ATTN_HEADS * DH // LANES + 1

    def body(q_ref, k_ref, v_ref, kp_ref, vp_ref, bias_ref, sink_ref, o_ref, lg_ref, p_ref):
        n = pl.program_id(0)
        mask = _attn_mask(n, 1)
        kband = _bf(jnp.concatenate([kp_ref[...], k_ref[...]], axis=0))
        vband = _bf(jnp.concatenate([vp_ref[...], v_ref[...]], axis=0))
        for kv in range(ATTN_KV):
            lg_ref[...] = _nt(_stack_heads(q_ref, kv * HPK, HPK, DH), kband[:, kv * DH:(kv + 1) * DH])
            for j in range(HPK):
                h = kv * HPK + j
                p, _ = _softmax_sink(lg_ref[j * L:(j + 1) * L, :] * scale + bias_ref[h], sink_ref[h], mask)
                p_ref[j * L:(j + 1) * L, :] = _bf(p)
            og = _nn(p_ref[...], vband[:, kv * DH:(kv + 1) * DH])
            for j in range(HPK):
                h = kv * HPK + j
                o_ref[:, h * DH:(h + 1) * DH] = og[j * L:(j + 1) * L].astype(BF16)

    prev = lambda i: jnp.maximum(i - 1, 0)
    return _pcall(
        body, name=name, out_shape=_sds((T, ATTN_HEADS * DH), BF16), grid=(nb,),
        in_specs=[_row_spec(L, ATTN_HEADS * DH, 0), _row_spec(L, LANES, kcol), _row_spec(L, LANES, vcol),
                  pl.BlockSpec((L, LANES), lambda i: (prev(i), kcol)), pl.BlockSpec((L, LANES), lambda i: (prev(i), vcol)),
                  pl.BlockSpec((ATTN_HEADS, L, 2 * L), lambda i: (0, 0, 0)),
                  pl.BlockSpec(memory_space=pltpu.SMEM)],
        out_specs=_row_spec(L, ATTN_HEADS * DH),
        scratch=[pltpu.VMEM((HPK * L, 2 * L), F32), pltpu.VMEM((HPK * L, 2 * L), BF16)],
        sem=("parallel",))(qkv, qkv, qkv, qkv, qkv, bias, sinks)


def _attn_bwd(qkv, datt, bias, sinks, name):
    T = qkv.shape[0]
    nb = T // CHUNK
    L, DH, HPK = CHUNK, ATTN_DH, ATTN_HEADS // ATTN_KV
    scale = DH ** -0.5
    kcol, vcol = ATTN_HEADS * DH // LANES, ATTN_HEADS * DH // LANES + 1

    def body(q_ref, k_ref, v_ref, kp_ref, vp_ref, do_ref, bias_ref, sink_ref,
             dq_ref, dk_ref, dv_ref, bsum_ref, dbias_ref, dsink_ref, pend_k, pend_v, band_k, band_v, lg_ref, dp_ref, p_ref, dl_ref):
        n = pl.program_id(0)

        @pl.when(n == 0)
        def _():
            dbias_ref[...] = jnp.zeros_like(dbias_ref)
            dsink_ref[...] = jnp.zeros_like(dsink_ref)
            bsum_ref[...] = jnp.zeros_like(bsum_ref)

        def emit_kv(dk, dv):
            dk_ref[...] = dk.astype(BF16)
            dv_ref[...] = dv.astype(BF16)
            bsum_ref[:, ATTN_HEADS * DH:ATTN_HEADS * DH + LANES] += _colsum(dk)
            bsum_ref[:, ATTN_HEADS * DH + LANES:] += _colsum(dv)

        @pl.when(n < nb)
        def _():
            mask = _attn_mask(n, 1)
            kband = _bf(jnp.concatenate([kp_ref[...], k_ref[...]], axis=0))
            vband = _bf(jnp.concatenate([vp_ref[...], v_ref[...]], axis=0))
            lane1 = lax.broadcasted_iota(jnp.int32, (1, LANES), 1)
            dsink = jnp.zeros((1, LANES), F32)
            for kv in range(ATTN_KV):
                kb, vb = kband[:, kv * DH:(kv + 1) * DH], vband[:, kv * DH:(kv + 1) * DH]
                qg = _stack_heads(q_ref, kv * HPK, HPK, DH)
                dog = _stack_heads(do_ref, kv * HPK, HPK, DH)
                lg_ref[...] = _nt(qg, kb)
                dp_ref[...] = _nt(dog, vb)
                for j in range(HPK):
                    h = kv * HPK + j
                    rows = slice(j * L, (j + 1) * L)
                    p, ps = _softmax_sink(lg_ref[rows, :] * scale + bias_ref[h], sink_ref[h], mask)
                    dp = dp_ref[rows, :]
                    delta = _rowsum(p * dp)
                    dl = p * (dp - delta)
                    dbias_ref[h] += dl
                    p_ref[rows, :] = _bf(p)
                    dl_ref[rows, :] = _bf(dl)
                    dsink = dsink + jnp.where(lane1 == h, -_colsum(ps * delta), 0.0)
                band_v[:, kv * DH:(kv + 1) * DH] = _tn(p_ref[...], dog)
                dqg = _nn(dl_ref[...], kb) * scale
                band_k[:, kv * DH:(kv + 1) * DH] = _tn(dl_ref[...], qg) * scale
                for j in range(HPK):
                    h = kv * HPK + j
                    dq_ref[:, h * DH:(h + 1) * DH] = dqg[j * L:(j + 1) * L].astype(BF16)
                    bsum_ref[:, h * DH:(h + 1) * DH] += _colsum(dqg[j * L:(j + 1) * L])
            dsink_ref[...] += dsink

            @pl.when(n > 0)
            def _():
                emit_kv(pend_k[...] + band_k[0:L, :], pend_v[...] + band_v[0:L, :])
            pend_k[...] = band_k[L:2 * L, :]
            pend_v[...] = band_v[L:2 * L, :]

        @pl.when(n == nb)
        def _():
            emit_kv(pend_k[...], pend_v[...])

    cur = lambda i: jnp.minimum(i, nb - 1)
    prev = lambda i: jnp.maximum(jnp.minimum(i, nb - 1) - 1, 0)
    lag = lambda i: jnp.maximum(i - 1, 0)
    return _pcall(
        body, name=name,
        out_shape=[_sds((T, ATTN_HEADS * DH), BF16), _sds((T, LANES), BF16), _sds((T, LANES), BF16), _sds((1, QKV_DIM)),
                   _sds((ATTN_HEADS, L, 2 * L)), _sds((1, LANES))],
        grid=(nb + 1,),
        in_specs=[pl.BlockSpec((L, ATTN_HEADS * DH), lambda i: (cur(i), 0)),
                  pl.BlockSpec((L, LANES), lambda i: (cur(i), kcol)), pl.BlockSpec((L, LANES), lambda i: (cur(i), vcol)),
                  pl.BlockSpec((L, LANES), lambda i: (prev(i), kcol)), pl.BlockSpec((L, LANES), lambda i: (prev(i), vcol)),
                  pl.BlockSpec((L, ATTN_HEADS * DH), lambda i: (cur(i), 0)),
                  pl.BlockSpec((ATTN_HEADS, L, 2 * L), lambda i: (0, 0, 0)),
                  pl.BlockSpec(memory_space=pltpu.SMEM)],
        out_specs=[pl.BlockSpec((L, ATTN_HEADS * DH), lambda i: (cur(i), 0)),
                   pl.BlockSpec((L, LANES), lambda i: (lag(i), 0)), pl.BlockSpec((L, LANES), lambda i: (lag(i), 0)),
                   _vec_spec(QKV_DIM), pl.BlockSpec((ATTN_HEADS, L, 2 * L), lambda i: (0, 0, 0)), _vec_spec(LANES)],
        scratch=[pltpu.VMEM((L, LANES), F32), pltpu.VMEM((L, LANES), F32),
                 pltpu.VMEM((2 * L, LANES), F32), pltpu.VMEM((2 * L, LANES), F32),
                 pltpu.VMEM((HPK * L, 2 * L), F32), pltpu.VMEM((HPK * L, 2 * L), F32),
                 pltpu.VMEM((HPK * L, 2 * L), BF16), pltpu.VMEM((HPK * L, 2 * L), BF16)],
        sem=("arbitrary",))(qkv, qkv, qkv, qkv, qkv, datt, bias, sinks)


def _pad_rows(a, mult):
    pad = (-a.shape[-2]) % mult
    if pad == 0:
        return a
    cfg = [(0, 0)] * (a.ndim - 2) + [(0, pad), (0, 0)]
    return jnp.pad(a, cfg)


class _Pack:
    def __init__(self, width, mult, total_mult):
        self.width, self.mult, self.total_mult = width, mult, total_mult
        self.entries = []
        self.rows = 0

    def add(self, key, shape):
        n = int(np.prod(shape))
        rows = -(-n // self.width)
        self.entries.append((key, self.rows, rows, tuple(shape)))
        self.rows += -(-rows // self.mult) * self.mult

    @property
    def total(self):
        return -(-self.rows // self.total_mult) * self.total_mult

    def pack(self, pieces, dtype, lead=()):
        parts = []
        for key, _, rows, shape in self.entries:
            a = pieces[key].astype(dtype).reshape(lead + (-1,))
            n = int(np.prod(shape))
            a = jnp.pad(a, [(0, 0)] * len(lead) + [(0, rows * self.width - n)])
            a = a.reshape(lead + (rows, self.width))
            parts.append(_pad_rows(a, self.mult))
        out = jnp.concatenate(parts, axis=len(lead))
        return _pad_rows(out, self.total_mult)

    def unpack(self, packed, lead=()):
        out = {}
        for key, off, rows, shape in self.entries:
            a = lax.slice_in_dim(packed, off, off + rows, axis=len(lead))
            a = a.reshape(lead + (-1,))
            n = int(np.prod(shape))
            out[key] = lax.slice_in_dim(a, 0, n, axis=len(lead)).reshape(lead + shape)
        return out


def _ffn_fwd(x, h, mod, wg_t, wu_t, wd, tag, next_norm=None, gather=None):
    side = None if gather is None else (_GatherOps, gather)
    gate, up, act, *gathered = _mm_swiglu(h, wg_t, wu_t, f"ffn_gateup_{tag}", side=side)
    x_out, ffn_out, *h_next = _mm_resid([(act, wd)], x, mod[5:6], name=f"ffn_down_{tag}", norm=next_norm)
    return (x_out, dict(h=h, gate=gate, up=up, act=act, out=ffn_out), *h_next, *gathered)


def _ffn_bwd(dx_out, dffn, x_in, saved, mod, norm_w, wg_t, wu_t, wd, below, tag, exchange=None):
    side = None if exchange is None else (_SiblingOps, exchange)
    dgate, dup, *theirs = _mm_swiglu_bwd(dffn, wd, saved["gate"], saved["up"], f"ffn_act_bwd_{tag}", side=side)
    d_wd = _mm_tn(saved["act"], dffn, name=f"ffn_dwd_{tag}")
    d_wg_t = _mm_tn(dgate, saved["h"], name=f"ffn_dwg_{tag}")
    d_wu_t = _mm_tn(dup, saved["h"], name=f"ffn_dwu_{tag}")
    side = None if exchange is None else (_ChipsOps, _pair_sum(exchange, theirs[0], f"grads_pair_sum_{tag}"))
    dh = _mm([(dgate, wg_t), (dup, wu_t)], "nn", name=f"ffn_dh_{tag}", side=side)
    dh, *from_chips = dh if exchange is not None else (dh,)
    dx, d_below, acc = _norm_mod_bwd(x_in, dh, dx_out, norm_w, mod[4:5], below[0], below[1], f"ffn_norm_bwd_{tag}")
    return (dx, d_below, dict(d_wg=d_wg_t, d_wu=d_wu_t, d_wd=d_wd, acc=acc), *from_chips)


_BIG = [
    ("out_w", "out_w_even", 0, "row"), ("qkv_w", "qkv_w", 0, "col"), ("o_w", "o_w", 0, "row"),
    ("gate0", "ffn_gate_w", 0, "col"), ("up0", "ffn_up_w", 0, "col"), ("down0", "ffn_down_w", 0, "row"),
    ("gate1", "ffn_gate_w", 1, "col"), ("up1", "ffn_up_w", 1, "col"), ("down1", "ffn_down_w", 1, "row"),
    ("in_w", "in_w_even", 0, "col"),
]


def _to_wire(a, kind):
    return a.T if kind == "col" else a


_GATHER_GROUPS = [["out_w", "in_w"], ["gate0", "up0", "down0"], ["qkv_w", "o_w"], ["gate1", "up1", "down1"]]
_GRAD_GROUPS = [["qkv_w", "o_w", "gate1", "up1", "down1"], ["out_w", "gate0", "up0", "down0"], ["in_w"]]

_REPLICATED = ["ada_b", "norm_mix_w", "norm_ffn_w", "conv_b", "dt_bias", "a_log", "d_skip", "ssm_norm_w", "gmlp_ln_w",
               "gmlp_ln_b", "gmlp_ws", "gmlp_bs", "sinks", "rel_table", "final_norm_w"]
_TINY_SHARDED = ["conv_w", "qkv_b", "o_b"]

_WEIGHTS = ['ada_w', 'ada_b', 'norm_mix_w', 'norm_ffn_w', 'in_w_even', 'conv_w', 'conv_b', 'dt_bias', 'a_log', 'd_skip',
            'ssm_norm_w', 'gmlp_ln_w', 'gmlp_ln_b', 'gmlp_ws', 'gmlp_bs', 'out_w_even', 'qkv_w', 'qkv_b', 'o_w', 'o_b',
            'sinks', 'rel_table', 'ffn_gate_w', 'ffn_up_w', 'ffn_down_w', 'final_norm_w']


def _step(x, c, loss_target, W, M, V):
    T = x.shape[1]
    x0 = x[0]
    target = loss_target[0]
    me = 4 * lax.axis_index("x") + 2 * lax.axis_index("y") + lax.axis_index("c")

    w_wire_local = {key: _to_wire(W[name][layer].astype(BF16), kind) for key, name, layer, kind in _BIG}

    def wire_pack(keys, mult):
        gp = _Pack(D, 1, mult)
        for key in keys:
            gp.add(key, w_wire_local[key].shape)
        return gp

    gather_packs = [(gp, gp.pack(w_wire_local, BF16)) for gp in (wire_pack(keys, 16) for keys in _GATHER_GROUPS)]
    grad_packs = [wire_pack(keys, 128) for keys in _GRAD_GROUPS]
    full = {}

    def gathered_weights(group, gathered):
        shards = gather_packs[group][0].unpack(gathered, lead=(N_DEV,))
        full.update({key: a.reshape(-1, D) for key, a in shards.items()})

    gathered_weights(0, _all_gather(gather_packs[0][1], "gather_weights"))

    small_in = _Pack(D, 8, 8)
    small_in.add("c", (1, D))
    small_in.add("conv_w", W["conv_w"][0].shape)
    small_in.add("qkv_b", W["qkv_b"][0].shape)
    small_in.add("o_b", W["o_b"][0].shape)
    sm = small_in.unpack(_all_gather(small_in.pack(
        dict(c=c, conv_w=W["conv_w"][0], qkv_b=W["qkv_b"][0], o_b=W["o_b"][0]), F32), "gather_small"), lead=(N_DEV,))
    c_all = sm["c"].reshape(N_DEV, D)
    conv_w_full = jnp.transpose(sm["conv_w"], (1, 0, 2)).reshape(SSM_CONV, CONV_DIM)
    qkv_b_full = sm["qkv_b"].reshape(1, QKV_DIM)
    o_b_full = sm["o_b"].reshape(1, D)

    ncol = W["ada_w"].shape[2]
    cond, mod_cols = _mod_matmul(c_all, W["ada_w"], "mod_matmul")
    mod_g = _all_gather(mod_cols.reshape(DEPTH * N_DEV, ncol), "gather_mod").reshape(N_DEV, DEPTH, N_DEV, ncol)
    mod_me = lax.dynamic_index_in_dim(mod_g, me, axis=2, keepdims=False)
    mod_me = jnp.transpose(mod_me, (1, 0, 2)).reshape(DEPTH, 6, D)
    mod_me = jnp.pad(mod_me, ((0, 0), (0, 2), (0, 0))).reshape(DEPTH * 8, D)
    ada_b_rows = jnp.pad(W["ada_b"].reshape(DEPTH, 6, D), ((0, 0), (0, 2), (0, 0))).reshape(DEPTH * 8, D)
    mod_all = _add_rows(mod_me, ada_b_rows, "mod_bias").reshape(DEPTH, 8, D)
    mod0, mod1 = mod_all[0], mod_all[1]

    in_t = full["in_w"]
    o1, o2, o3, o4 = SSM_INNER, SSM_INNER + CONV_DIM, SSM_INNER + CONV_DIM + SSM_HEADS, SSM_INNER + CONV_DIM + SSM_HEADS + GMLP_INNER
    w_z, w_xbc, w_dt, w_u, w_v = in_t[:o1], in_t[o1:o2], in_t[o2:o3], in_t[o3:o4], in_t[o4:]
    w_main = jnp.concatenate([w_z, w_u, w_v, w_xbc], axis=0)
    w_dtp = jnp.pad(w_dt, ((0, LANES - SSM_HEADS), (0, 0)))
    out_w = full["out_w"]

    pad16 = lambda a: jnp.pad(a.reshape(1, SSM_HEADS), ((0, 0), (0, LANES - SSM_HEADS)))
    dtb, alog = pad16(W["dt_bias"][0]), pad16(W["a_log"][0])
    dskip = jnp.repeat(W["d_skip"][0], SSM_HEAD_DIM).reshape(1, SSM_INNER)
    ssm_nw = W["ssm_norm_w"]
    lnw, lnb = W["gmlp_ln_w"], W["gmlp_ln_b"]
    ws = W["gmlp_ws"][0]
    bs_exp = jnp.repeat(W["gmlp_bs"][0].T, CHUNK, axis=1)
    conv_b = W["conv_b"]
    nmw, nfw = W["norm_mix_w"], W["norm_ffn_w"]
    onehot_t = _bucket_onehot_t()
    head_ind = _head_indicator()
    bias = _bias_from_table(W["rel_table"].T, onehot_t, "rel_bias").reshape(ATTN_HEADS, CHUNK, 2 * CHUNK)
    sinks = W["sinks"][0]

    h0 = _norm_mod(x0, nmw[0:1], mod0[1:2], mod0[0:1], "mix_norm_0")
    pm = _mm([(h0, w_main)], "nt", name="in_proj", tn_pref=1536)
    dtr = _mm([(h0, w_dtp)], "nt", name="in_proj_dt")
    cpre = _conv_fwd(pm, conv_w_full, conv_b, "conv_fwd")
    ya, ypre, sprev, gathered_b = _ssd_fwd(cpre, dtr, pm, dtb, alog, dskip, ssm_nw, head_ind, "ssd_fwd",
                                           side=(_GatherOps, gather_packs[1][1]))
    gathered_weights(1, gathered_b)
    yb, gathered_c = _gmlp_fwd(pm, lnw, lnb, ws, bs_exp, "gmlp_fwd", side=(_GatherOps, gather_packs[2][1]))
    gathered_weights(2, gathered_c)
    x1, mix0, hf0 = _mm_resid([(ya, out_w[:SSM_INNER]), (yb, out_w[SSM_INNER:])], x0, mod0[2:3], name="out_proj",
                              norm=(nfw[0:1], mod0[4:5], mod0[3:4]))
    x2, ffn0, h1, gathered_d = _ffn_fwd(x1, hf0, mod0, full["gate0"], full["up0"], full["down0"], "0",
                                        next_norm=(nmw[1:2], mod1[1:2], mod1[0:1]), gather=gather_packs[3][1])
    gathered_weights(3, gathered_d)
    qkv_t, o_w = full["qkv_w"], full["o_w"]
    w_q, w_k, w_v_att = qkv_t[:D], qkv_t[D:D + LANES], qkv_t[D + LANES:]

    qkv = _mm([(h1, qkv_t)], "nt", name="qkv_proj", bias=qkv_b_full, tn_pref=1280, out_dtype=BF16)
    att = _attn_fwd(qkv, bias, sinks, "attn_fwd")
    x3, mix1, hf1 = _mm_resid([(att, o_w)], x2, mod1[2:3], name="o_proj", bias=o_b_full,
                              norm=(nfw[1:2], mod1[4:5], mod1[3:4]))
    x4, ffn1 = _ffn_fwd(x3, hf1, mod1, full["gate1"], full["up1"], full["down1"], "1")

    dx4, dffn1, acc_f = _final_loss(x4, W["final_norm_w"].reshape(1, D), target, ffn1["out"], mod1[5:6], "final_loss")
    loss = lax.psum(acc_f[1, 0], ("x", "y", "c"))
    dx3, dmix1, gf1 = _ffn_bwd(dx4, dffn1, x3, ffn1, mod1, nfw[1:2], full["gate1"], full["up1"], full["down1"],
                               (mix1, mod1[2:3]), "1")
    datt = _mm([(dmix1, o_w)], "nt", name="o_proj_dx", out_dtype=BF16)
    d_o_w = _mm_tn(att, dmix1, name="o_proj_dw")
    dq, dk, dv, d_qkv_b, dbias, dsinks = _attn_bwd(qkv, datt, bias, sinks, "attn_bwd")
    d_table = _table_from_dbias(dbias.reshape(ATTN_HEADS, -1), onehot_t, "rel_table_grad").T
    d_qkv_t = jnp.concatenate([_mm_tn(dq, h1, name="qkv_dw_q"), _mm_tn(dk, h1, name="qkv_dw_k"), _mm_tn(dv, h1, name="qkv_dw_v")], axis=0)
    dh1 = _mm([(dq, w_q), (dk, w_k), (dv, w_v_att)], "nn", name="qkv_proj_dx")
    dx2, dffn0, acc_n1 = _norm_mod_bwd(x2, dh1, dx3, nmw[1:2], mod1[1:2], ffn0["out"], mod0[5:6], "mix_norm_bwd_1")

    g_wire = dict(qkv_w=d_qkv_t, o_w=d_o_w, gate1=gf1["d_wg"], up1=gf1["d_wu"], down1=gf1["d_wd"])

    def packed_partials(group):
        return grad_packs[group].pack({key: g_wire[key].reshape(N_DEV, -1, D) for key in _GRAD_GROUPS[group]},
                                      F32, lead=(N_DEV,))

    from_chips = {}
    dx1, dmix0, gf0, from_chips[0] = _ffn_bwd(dx2, dffn0, x1, ffn0, mod0, nfw[0:1], full["gate0"], full["up0"], full["down0"],
                                              (mix0, mod0[2:3]), "0", exchange=packed_partials(0))

    dya = _mm([(dmix0, out_w[:SSM_INNER])], "nt", name="out_proj_dx_a")
    dyb = _mm([(dmix0, out_w[SSM_INNER:])], "nt", name="out_proj_dx_b")
    d_out_w = jnp.concatenate([_mm_tn(ya, dmix0, name="out_proj_dw_a"), _mm_tn(yb, dmix0, name="out_proj_dw_b")], axis=0)
    g_wire.update(gate0=gf0["d_wg"], up0=gf0["d_wu"], down0=gf0["d_wd"], out_w=d_out_w)
    partials_ffn0 = packed_partials(1)
    du, dvg, d_ws, d_bs, acc_ln, theirs_ffn0 = _gmlp_bwd(pm, dyb, lnw, lnb, ws, bs_exp, "gmlp_bwd",
                                                         side=(_SiblingOps, partials_ffn0))
    pair_ffn0 = _pair_sum(partials_ffn0, theirs_ffn0, "grads_pair_sum_mix")
    dz, dcpre, ddtr, acc_ssd, d_ssm_nw, from_chips[1] = _ssd_bwd(
        cpre, dtr, pm, ypre, sprev, dya, dtb, alog, dskip, ssm_nw, head_ind, "ssd_bwd", side=(_ChipsOps, pair_ffn0))
    dxbc, acc_conv = _conv_bwd(dcpre, pm, conv_w_full, "conv_bwd")
    d_in_t = jnp.concatenate([
        _mm_tn(dz, h0, name="in_dw_z"), _mm_tn(dxbc, h0, name="in_dw_xbc"),
        _mm_tn(ddtr, h0, name="in_dw_dt")[:SSM_HEADS], _mm_tn(du, h0, name="in_dw_u"), _mm_tn(dvg, h0, name="in_dw_v")], axis=0)
    g_wire.update(in_w=d_in_t)
    partials_mix = packed_partials(2)
    dh0, theirs_mix = _mm([(dz, w_z), (dxbc, w_xbc), (ddtr, w_dtp), (du, w_u), (dvg, w_v)], "nn", name="in_proj_dx",
                          side=(_SiblingOps, partials_mix))
    pair_mix = _pair_sum(partials_mix, theirs_mix, "grads_pair_sum")
    grad_x, acc_n0, from_chips[2] = _norm_mod_bwd(x0, dh0, dx1, nmw[0:1], mod0[1:2], None, None, "mix_norm_bwd_0",
                                                  side=(_ChipsOps, pair_mix))

    g_mine = {}
    for group in range(len(_GRAD_GROUPS)):
        g_mine.update(grad_packs[group].unpack(_sum_parts(from_chips[group], f"grads_chip_sum_{group}")))
    res_big = [{}, {}, {}, {}]
    for key, name, layer, kind in _BIG:
        g_nat = _to_wire(g_mine[key], kind)
        outs = _adamw(g_nat[None], W[name][layer], M[name][layer], V[name][layer], f"adamw_{key}")
        for res, out in zip(res_big, outs):
            res[key] = out

    acc_f0, acc_f1 = gf0["acc"], gf1["acc"]
    d_mod = jnp.stack([
        jnp.stack([acc_n0[1], acc_n0[0], acc_f0[3], acc_f0[1], acc_f0[0], acc_n1[3]]),
        jnp.stack([acc_n1[1], acc_n1[0], acc_f1[3], acc_f1[1], acc_f1[0], acc_f[3]])])
    g_small = dict(
        ada_b=d_mod.reshape(DEPTH, 6 * D),
        norm_mix_w=jnp.stack([acc_n0[2], acc_n1[2]]), norm_ffn_w=jnp.stack([acc_f0[2], acc_f1[2]]),
        conv_b=acc_conv[4:5], dt_bias=acc_ssd[0:1, :SSM_HEADS], a_log=acc_ssd[1:2, :SSM_HEADS], d_skip=acc_ssd[2:3, :SSM_HEADS],
        ssm_norm_w=d_ssm_nw, gmlp_ln_w=acc_ln[0:1], gmlp_ln_b=acc_ln[1:2], gmlp_ws=d_ws[None],
        gmlp_bs=d_bs[:, :GMLP_GROUPS].T[None], sinks=dsinks[:, :ATTN_HEADS], rel_table=d_table, final_norm_w=acc_f[0],
        conv_w=acc_conv[0:SSM_CONV], qkv_b=d_qkv_b, o_b=acc_f1[4:5])
    small = _Pack(D, 1, 8)
    for name in _REPLICATED:
        small.add(name, W[name].shape)
    small.add("conv_w", (SSM_CONV, CONV_DIM))
    small.add("qkv_b", (1, QKV_DIM))
    small.add("o_b", (1, D))
    parts_small = _all_gather(small.pack(g_small, F32), "gather_small_grads")
    zeros_tiny = dict(conv_w=jnp.zeros((SSM_CONV, CONV_DIM), F32), qkv_b=jnp.zeros((1, QKV_DIM), F32), o_b=jnp.zeros((1, D), F32))
    pks = lambda S: small.pack({**{name: S[name] for name in _REPLICATED}, **zeros_tiny}, F32)
    res_small = [small.unpack(r) for r in _adamw(parts_small, pks(W), pks(M), pks(V), "adamw_small")]
    g_small_sum = res_small[0]

    n_cw, n_qb, n_ob = W["conv_w"].shape[2], W["qkv_b"].shape[1], W["o_b"].shape[1]
    g_tiny = dict(conv_w=lax.dynamic_slice_in_dim(g_small_sum["conv_w"], me * n_cw, n_cw, axis=1)[None],
                  qkv_b=lax.dynamic_slice_in_dim(g_small_sum["qkv_b"], me * n_qb, n_qb, axis=1),
                  o_b=lax.dynamic_slice_in_dim(g_small_sum["o_b"], me * n_ob, n_ob, axis=1))
    tiny = _Pack(D, 8, 8)
    for name in _TINY_SHARDED:
        tiny.add(name, W[name].shape)
    pkt = lambda S: tiny.pack({name: S[name] for name in _TINY_SHARDED}, F32)
    res_tiny = [tiny.unpack(r) for r in _adamw(pkt(g_tiny)[None], pkt(W), pkt(M), pkt(V), "adamw_tiny")]

    dmod_all = parts_small[:, small.entries[0][1]:small.entries[0][1] + small.entries[0][2]].reshape(N_DEV, DEPTH, 6 * D)
    dmod_cols = jnp.transpose(lax.dynamic_slice_in_dim(dmod_all, me * ncol, ncol, axis=2), (1, 0, 2))
    g_ada_w = _ada_w_grad(cond, dmod_cols, "ada_w_grad")
    flat = lambda a: a.reshape(DEPTH * D, ncol)
    res_ada = [r.reshape(DEPTH, D, ncol) for r in _adamw(flat(g_ada_w)[None], flat(W["ada_w"]), flat(M["ada_w"]), flat(V["ada_w"]), "adamw_ada_w")]

    def result(kind_idx, name):
        if name == "ada_w":
            return res_ada[kind_idx]
        if name in _REPLICATED:
            return res_small[kind_idx][name]
        if name in _TINY_SHARDED:
            return res_tiny[kind_idx][name]
        pieces = [res_big[kind_idx][key] for key, nm, layer, kind in _BIG if nm == name]
        return jnp.stack(pieces)

    outs = [loss, grad_x[None]]
    for kind_idx in range(4):
        outs += [result(kind_idx, name) for name in _WEIGHTS]
    return tuple(outs)


def kernel(x, c, ada_w, ada_b, norm_mix_w, norm_ffn_w, in_w_even, conv_w, conv_b, dt_bias, a_log, d_skip, ssm_norm_w, gmlp_ln_w, gmlp_ln_b, gmlp_ws, gmlp_bs, out_w_even, qkv_w, qkv_b, o_w, o_b, sinks, rel_table, ffn_gate_w, ffn_up_w, ffn_down_w, final_norm_w, loss_target, m_ada_w, m_ada_b, m_norm_mix_w, m_norm_ffn_w, m_in_w_even, m_conv_w, m_conv_b, m_dt_bias, m_a_log, m_d_skip, m_ssm_norm_w, m_gmlp_ln_w, m_gmlp_ln_b, m_gmlp_ws, m_gmlp_bs, m_out_w_even, m_qkv_w, m_qkv_b, m_o_w, m_o_b, m_sinks, m_rel_table, m_ffn_gate_w, m_ffn_up_w, m_ffn_down_w, m_final_norm_w, v_ada_w, v_ada_b, v_norm_mix_w, v_norm_ffn_w, v_in_w_even, v_conv_w, v_conv_b, v_dt_bias, v_a_log, v_d_skip, v_ssm_norm_w, v_gmlp_ln_w, v_gmlp_ln_b, v_gmlp_ws, v_gmlp_bs, v_out_w_even, v_qkv_w, v_qkv_b, v_o_w, v_o_b, v_sinks, v_rel_table, v_ffn_gate_w, v_ffn_up_w, v_ffn_down_w, v_final_norm_w):
    args = locals()
    W = {n: args[n] for n in _WEIGHTS}
    M = {n: args["m_" + n] for n in _WEIGHTS}
    V = {n: args["v_" + n] for n in _WEIGHTS}
    return _step(x, c, loss_target, W, M, V)
```

```python
import functools
import math

import numpy as np
import jax
import jax.numpy as jnp
from jax import lax
from jax.experimental import pallas as pl
from jax.experimental.pallas import tpu as pltpu

F32 = jnp.float32
BF16 = jnp.bfloat16
HIGHEST = lax.Precision.HIGHEST
MESH = pl.DeviceIdType.MESH

N_DEV = 8
D = 1024
DEPTH = 2
SSM_HEADS = 16
SSM_HEAD_DIM = 64
SSM_INNER = 1024
SSM_GROUPS = 2
SSM_STATE = 128
SSM_CONV = 4
CHUNK = 128
CONV_DIM = SSM_INNER + 2 * SSM_GROUPS * SSM_STATE
GMLP_GROUPS = 8
GMLP_INNER = 1024
IN_EVEN = 4624
ATTN_HEADS = 16
ATTN_KV = 2
ATTN_DH = 64
QKV_DIM = 1280
REL_BUCKETS = 32
REL_MAX_DIST = 128
FFN = 2816
EPS = 1e-6
NEG_INF = -1e30
LANES = 128

ADAM_LR = 0.001
ADAM_B1 = 0.9
ADAM_B2 = 0.999
ADAM_EPS = 1e-08
ADAM_WD = 0.01
ADAM_STEP = 10

VMEM_LIMIT_BYTES = 56 * 1024 * 1024
ROW_TILE = 512


def _pcall(body, *, name, out_shape, grid=(), in_specs=None, out_specs=None, scratch=(), sem=None):
    params = dict(vmem_limit_bytes=VMEM_LIMIT_BYTES)
    if sem is not None:
        params["dimension_semantics"] = sem
    specs = {} if in_specs is None else dict(in_specs=in_specs, out_specs=out_specs)
    return pl.pallas_call(
        body, name=name, out_shape=out_shape, grid=grid, **specs,
        scratch_shapes=list(scratch), compiler_params=pltpu.CompilerParams(**params))


def _call(body, args, side=None, *, name, out_shape, grid, in_specs, out_specs, scratch=(), sem=None):
    if side is None:
        return _pcall(body, name=name, out_shape=out_shape, grid=grid, in_specs=in_specs, out_specs=out_specs,
                      scratch=scratch, sem=sem)(*args)
    ops_cls, x = side
    n_in, n_out, n_scr = len(in_specs), len(out_shape), len(scratch)
    steps = int(np.prod(grid))
    hbm = pl.BlockSpec(memory_space=pl.ANY)

    def wrapped(*refs):
        ins, x_ref = refs[:n_in], refs[n_in]
        outs, r_ref = refs[n_in + 1:n_in + 1 + n_out], refs[n_in + 1 + n_out]
        scr, sems = refs[n_in + 2 + n_out:n_in + 2 + n_out + n_scr], refs[n_in + 2 + n_out + n_scr:]
        ops = ops_cls(x_ref, r_ref, *sems)
        step = pl.program_id(0)
        for axis in range(1, len(grid)):
            step = step * grid[axis] + pl.program_id(axis)
        pl.when(step == 0)(ops.start)
        body(*ins, *outs, *scr)
        pl.when(step == (3 * steps) // 4)(ops.forward)
        pl.when(step == steps - 1)(ops.finish)

    return _pcall(
        wrapped, name=name, out_shape=list(out_shape) + [ops_cls.result(x)], grid=grid,
        in_specs=list(in_specs) + [hbm], out_specs=list(out_specs) + [hbm],
        scratch=list(scratch) + ops_cls.scratch(), sem=("arbitrary",) * len(grid))(*args, x)


def _tile(n, pref):
    if n <= pref:
        return n
    best = None
    for t in range(LANES, pref + 1, LANES):
        if n % t == 0:
            best = t
    assert best is not None, (n, pref)
    return best


def _rows(T):
    return min(ROW_TILE, T)


def _sds(shape, dtype=F32):
    return jax.ShapeDtypeStruct(shape, dtype)


def _row_spec(tm, c, col=0):
    return pl.BlockSpec((tm, c), lambda i, col=col: (i, col))


def _vec_spec(c, r=1):
    return pl.BlockSpec((r, c), lambda i: (0, 0))


def _sigmoid(x):
    return jax.nn.sigmoid(x)


def _silu(x):
    return x * _sigmoid(x)


def _dsilu(x):
    s = _sigmoid(x)
    return s * (1.0 + x * (1.0 - s))


def _gelu(x):
    return 0.5 * x * (1.0 + lax.erf(x * 0.7071067811865476))


def _dgelu(x):
    return 0.5 * (1.0 + lax.erf(x * 0.7071067811865476)) + x * jnp.exp(-0.5 * x * x) * 0.3989422804014327


def _dot(a, b, dims, precision=None):
    return lax.dot_general(a, b, (dims, ((), ())), precision=precision, preferred_element_type=F32)


def _nn(a, b, precision=None):
    return _dot(a, b, ((1,), (0,)), precision)


def _nt(a, b, precision=None):
    return _dot(a, b, ((1,), (1,)), precision)


def _tn(a, b, precision=None):
    return _dot(a, b, ((0,), (0,)), precision)


def _bf(x):
    return x.astype(BF16)


def _colsum(x):
    return jnp.sum(x, axis=0, keepdims=True)


def _rowsum(x):
    return jnp.sum(x, axis=1, keepdims=True)


def _allsum(x):
    return _colsum(_rowsum(x))


def _comm_call(ops_cls, x, name, from_vmem=False):
    def body(x_ref, out_ref, *sems):
        ops = ops_cls(x_ref, out_ref, *sems)
        ops.start()
        ops.forward()
        ops.finish()

    return pl.pallas_call(
        body, name=name, out_shape=ops_cls.result(x),
        in_specs=[pl.BlockSpec(memory_space=pltpu.VMEM if from_vmem else pl.ANY)],
        out_specs=pl.BlockSpec(memory_space=pl.ANY), scratch_shapes=ops_cls.scratch(),
    )(x)


def _all_gather(x, name):
    return _comm_call(_GatherOps, x, name, from_vmem=True)


class _GatherOps:
    def __init__(self, x_ref, out_ref, send_sems, recv_sems, local_sem):
        self.x_ref, self.out_ref = x_ref, out_ref
        self.send_sems, self.recv_sems, self.local_sem = send_sems, recv_sems, local_sem
        mx, my, mc = lax.axis_index("x"), lax.axis_index("y"), lax.axis_index("c")
        self.mc = mc
        self.me, self.sibling = (mx, my, mc), (mx, my, 1 - mc)
        self.chips = [(1 - mx, my), (mx, 1 - my), (1 - mx, 1 - my)]

    @staticmethod
    def result(x):
        return _sds((N_DEV,) + x.shape, x.dtype)

    @staticmethod
    def scratch():
        return [pltpu.SemaphoreType.DMA((7,)), pltpu.SemaphoreType.DMA((7,)), pltpu.SemaphoreType.DMA(())]

    def _slot(self, px, py, pc):
        return self.out_ref.at[4 * px + 2 * py + pc]

    def _copy(self, k, block, to, own=False):
        return pltpu.make_async_remote_copy(
            src_ref=self.x_ref if own else self._slot(*block), dst_ref=self._slot(*block),
            send_sem=self.send_sems.at[k], recv_sem=self.recv_sems.at[k], device_id=to, device_id_type=MESH)

    def _mine(self):
        return pltpu.make_async_copy(self.x_ref, self._slot(*self.me), self.local_sem)

    def _first(self):
        return [self._copy(0, self.me, self.sibling, own=True)] + [
            self._copy(1 + j, self.me, (*chip, self.mc), own=True) for j, chip in enumerate(self.chips)]

    def _passed(self):
        return [self._copy(4 + j, (*chip, self.mc), self.sibling) for j, chip in enumerate(self.chips)]

    def start(self):
        self._mine().start()
        for cp in self._first():
            cp.start()

    def forward(self):
        passed = self._passed()
        for j, chip in enumerate(self.chips):
            self._copy(1 + j, (*chip, self.mc), self.me).wait_recv()
            passed[j].start()

    def finish(self):
        self._copy(0, self.sibling, self.me).wait_recv()
        for j, chip in enumerate(self.chips):
            self._copy(4 + j, (*chip, 1 - self.mc), self.me).wait_recv()
        for cp in self._first() + self._passed():
            cp.wait_send()
        self._mine().wait()


N_CHIP = 4


class _SiblingOps:
    def __init__(self, p_ref, theirs_ref, send_sems, recv_sems):
        self.p_ref, self.theirs_ref, self.send_sems, self.recv_sems = p_ref, theirs_ref, send_sems, recv_sems

    @staticmethod
    def result(p):
        return _sds((N_CHIP,) + p.shape[1:], p.dtype)

    @staticmethod
    def scratch():
        return [pltpu.SemaphoreType.DMA((N_CHIP,))] * 2

    def _copies(self):
        mx, my, mc = lax.axis_index("x"), lax.axis_index("y"), lax.axis_index("c")
        return [pltpu.make_async_remote_copy(
            src_ref=self.p_ref.at[2 * chip + 1 - mc], dst_ref=self.theirs_ref.at[chip],
            send_sem=self.send_sems.at[chip], recv_sem=self.recv_sems.at[chip],
            device_id=(mx, my, 1 - mc), device_id_type=MESH) for chip in range(N_CHIP)]

    def start(self):
        for cp in self._copies():
            cp.start()

    def forward(self):
        pass

    def finish(self):
        for cp in self._copies():
            cp.wait()


class _ChipsOps:
    def __init__(self, q_ref, out_ref, send_sems, recv_sems, local_sem):
        self.q_ref, self.out_ref = q_ref, out_ref
        self.send_sems, self.recv_sems, self.local_sem = send_sems, recv_sems, local_sem

    @staticmethod
    def result(q):
        return _sds(q.shape, q.dtype)

    @staticmethod
    def scratch():
        return [pltpu.SemaphoreType.DMA((N_CHIP - 1,)), pltpu.SemaphoreType.DMA((N_CHIP - 1,)), pltpu.SemaphoreType.DMA(())]

    def _copies(self):
        mx, my, mc = lax.axis_index("x"), lax.axis_index("y"), lax.axis_index("c")
        me = 2 * mx + my
        local = pltpu.make_async_copy(self.q_ref.at[me], self.out_ref.at[me], self.local_sem)
        remote = []
        for r in range(1, N_CHIP):
            px = 1 - mx if r & 2 else mx
            py = 1 - my if r & 1 else my
            remote.append(pltpu.make_async_remote_copy(
                src_ref=self.q_ref.at[2 * px + py], dst_ref=self.out_ref.at[me],
                send_sem=self.send_sems.at[r - 1], recv_sem=self.recv_sems.at[r - 1],
                device_id=(px, py, mc), device_id_type=MESH))
        return local, remote

    def start(self):
        local, remote = self._copies()
        local.start()
        for cp in remote:
            cp.start()

    def forward(self):
        pass

    def finish(self):
        local, remote = self._copies()
        for cp in remote:
            cp.wait()
        local.wait()


def _pair_sum(p, theirs, name):
    n, R, C = theirs.shape
    tr = _tile_rows(R, 256)

    def body(p_ref, t_ref, o_ref):
        mc = lax.axis_index("c")
        o_ref[0] = (p_ref[0, mc] + t_ref[0]).astype(BF16)

    blk = pl.BlockSpec((1, tr, C), lambda s, i: (s, i, 0))
    return _pcall(body, name=name, out_shape=_sds((n, R, C), BF16), grid=(n, R // tr),
                  in_specs=[pl.BlockSpec((1, 2, tr, C), lambda s, i: (s, 0, i, 0)), blk],
                  out_specs=blk, sem=("parallel", "parallel"))(p.reshape(n, 2, R, C), theirs)


def _sum_parts(parts, name):
    P, R, C = parts.shape
    tr = _tile_rows(R, 256)

    def body(p_ref, o_ref):
        g = p_ref[0].astype(F32)
        for k in range(1, P):
            g = g + p_ref[k].astype(F32)
        o_ref[...] = g

    return _pcall(body, name=name, out_shape=_sds((R, C)), grid=(R // tr,),
                  in_specs=[pl.BlockSpec((P, tr, C), lambda i: (0, i, 0))],
                  out_specs=pl.BlockSpec((tr, C), lambda i: (i, 0)), sem=("parallel",))(parts)


def _adamw(parts, w, m, v, name):
    P, R, C = parts.shape
    tr = R if R <= 256 else _tile_rows(R, 256)

    def body(p_ref, w_ref, m_ref, v_ref, g_ref, d_ref, nm_ref, nv_ref):
        g = p_ref[0]
        for k in range(1, P):
            g = g + p_ref[k]
        nm = ADAM_B1 * m_ref[...] + (1.0 - ADAM_B1) * g
        nv = ADAM_B2 * v_ref[...] + (1.0 - ADAM_B2) * (g * g)
        m_hat = nm / (1.0 - ADAM_B1 ** ADAM_STEP)
        v_hat = nv / (1.0 - ADAM_B2 ** ADAM_STEP)
        g_ref[...] = g
        d_ref[...] = -ADAM_LR * (m_hat / (jnp.sqrt(v_hat) + ADAM_EPS) + ADAM_WD * w_ref[...])
        nm_ref[...] = nm
        nv_ref[...] = nv

    blk = pl.BlockSpec((tr, C), lambda i: (i, 0))
    return _pcall(
        body, name=name, out_shape=[_sds((R, C))] * 4, grid=(R // tr,),
        in_specs=[pl.BlockSpec((P, tr, C), lambda i: (0, i, 0)), blk, blk, blk],
        out_specs=[blk] * 4, sem=("parallel",))(parts, w, m, v)


def _tile_rows(n, pref):
    best = None
    for t in range(8, pref + 1, 8):
        if n % t == 0:
            best = t
    assert best is not None, (n, pref)
    return best


def _mm(pairs, mode, *, name, out_dtype=F32, bias=None, tn_pref=1024, side=None):
    M = pairs[0][0].shape[0]
    N = pairs[0][1].shape[1] if mode == "nn" else pairs[0][1].shape[0]
    tm, tn = _rows(M), _tile(N, tn_pref)
    n_pairs = len(pairs)
    has_bias = bias is not None

    def body(*refs):
        acc = _pairs_dot(refs[:2 * n_pairs], mode)
        if has_bias:
            acc = acc + refs[2 * n_pairs][...]
        refs[-1][...] = acc.astype(refs[-1].dtype)

    in_specs, args = _pair_specs(pairs, mode, tm, tn)
    if has_bias:
        in_specs.append(pl.BlockSpec((1, tn), lambda j, i: (0, j)))
        args.append(bias)
    res = _call(body, args, side, name=name, out_shape=[_sds((M, N), out_dtype)], grid=(N // tn, M // tm),
                in_specs=in_specs, out_specs=[pl.BlockSpec((tm, tn), lambda j, i: (i, j))], sem=("parallel", "parallel"))
    return res[0] if side is None else (res[0], res[1])


def _pairs_dot(ab, mode):
    acc = None
    for p in range(len(ab) // 2):
        a, b = _bf(ab[2 * p][...]), _bf(ab[2 * p + 1][...])
        d = _nn(a, b) if mode == "nn" else _nt(a, b)
        acc = d if acc is None else acc + d
    return acc


def _pair_specs(pairs, mode, tm, tn):
    in_specs, args = [], []
    for a, b in pairs:
        K = a.shape[1]
        in_specs.append(pl.BlockSpec((tm, K), lambda j, i: (i, 0)))
        if mode == "nn":
            in_specs.append(pl.BlockSpec((K, tn), lambda j, i: (0, j)))
        else:
            in_specs.append(pl.BlockSpec((tn, K), lambda j, i: (j, 0)))
        args += [a, b]
    return in_specs, args


def _mm_resid(pairs, resid, gvec, *, name, bias=None, norm=None):
    M, N = resid.shape
    tm = _rows(M)
    n_pairs = len(pairs)
    has_bias, has_norm = bias is not None, norm is not None

    def body(*refs):
        acc = _pairs_dot(refs[:2 * n_pairs], "nn")
        pos = 2 * n_pairs
        if has_bias:
            acc = acc + refs[pos][...]
            pos += 1
        xv = refs[pos][...] + refs[pos + 1][...] * acc
        outs = refs[pos + 2 + 3 * has_norm:]
        outs[0][...] = xv
        outs[1][...] = acc.astype(BF16)
        if has_norm:
            w_ref, sc_ref, sh_ref = refs[pos + 2:pos + 5]
            r = lax.rsqrt(jnp.mean(xv * xv, axis=-1, keepdims=True) + EPS)
            outs[2][...] = ((xv * r * w_ref[...]) * (1.0 + sc_ref[...]) + sh_ref[...]).astype(BF16)

    in_specs, args = _pair_specs(pairs, "nn", tm, N)
    vec = pl.BlockSpec((1, N), lambda j, i: (0, 0))
    row = pl.BlockSpec((tm, N), lambda j, i: (i, 0))
    if has_bias:
        in_specs.append(vec)
        args.append(bias)
    in_specs += [row, vec] + [vec] * (3 * has_norm)
    args += [resid, gvec] + (list(norm) if has_norm else [])
    return _pcall(body, name=name, out_shape=[_sds((M, N)), _sds((M, N), BF16)] + [_sds((M, N), BF16)] * has_norm,
                  grid=(1, M // tm), in_specs=in_specs, out_specs=[row] * (2 + has_norm),
                  sem=("parallel", "parallel"))(*args)


def _mm_tn(a, b, *, name, tm_pref=1408, tn_pref=1536):
    K, M = a.shape
    N = b.shape[1]
    tm, tn = _tile(M, tm_pref), _tile(N, tn_pref)
    tk = K if K <= 2 * ROW_TILE else 2 * ROW_TILE

    def body(a_ref, b_ref, o_ref):
        @pl.when(pl.program_id(2) == 0)
        def _():
            o_ref[...] = jnp.zeros_like(o_ref)
        o_ref[...] += _tn(_bf(a_ref[...]), _bf(b_ref[...]))

    return _pcall(
        body, name=name, out_shape=_sds((M, N)), grid=(M // tm, N // tn, K // tk),
        in_specs=[pl.BlockSpec((tk, tm), lambda i, j, k: (k, i)), pl.BlockSpec((tk, tn), lambda i, j, k: (k, j))],
        out_specs=pl.BlockSpec((tm, tn), lambda i, j, k: (i, j)),
        sem=("parallel", "parallel", "arbitrary"))(a, b)


def _mm_swiglu(h, wg_t, wu_t, name, side=None):
    M, K = h.shape
    N = wg_t.shape[0]
    tm, tn = _rows(M), _tile(N, 1408)

    def body(h_ref, wg_ref, wu_ref, gate_ref, up_ref, act_ref):
        hv = _bf(h_ref[...])
        gate = _nt(hv, wg_ref[...])
        up = _nt(hv, wu_ref[...])
        gate_ref[...] = gate.astype(BF16)
        up_ref[...] = up.astype(BF16)
        act_ref[...] = (_silu(gate) * up).astype(BF16)

    w_spec = pl.BlockSpec((tn, K), lambda j, i: (j, 0))
    o_spec = pl.BlockSpec((tm, tn), lambda j, i: (i, j))
    return _call(body, (h, wg_t, wu_t), side, name=name,
                 out_shape=[_sds((M, N), BF16)] * 3, grid=(N // tn, M // tm),
                 in_specs=[pl.BlockSpec((tm, K), lambda j, i: (i, 0)), w_spec, w_spec], out_specs=[o_spec] * 3,
                 sem=("parallel", "parallel"))


def _mm_swiglu_bwd(dout, wd, gate, up, name, side=None):
    M, K = dout.shape
    N = wd.shape[0]
    tm, tn = _rows(M), _tile(N, 1408)

    def body(d_ref, wd_ref, gate_ref, up_ref, dg_ref, du_ref):
        dact = _nt(_bf(d_ref[...]), wd_ref[...])
        g = gate_ref[...].astype(F32)
        dg_ref[...] = (dact * up_ref[...].astype(F32) * _dsilu(g)).astype(BF16)
        du_ref[...] = (dact * _silu(g)).astype(BF16)

    t_spec = pl.BlockSpec((tm, tn), lambda j, i: (i, j))
    return _call(
        body, (dout, wd, gate, up), side, name=name, out_shape=[_sds((M, N), BF16)] * 2, grid=(N // tn, M // tm),
        in_specs=[pl.BlockSpec((tm, K), lambda j, i: (i, 0)), pl.BlockSpec((tn, K), lambda j, i: (j, 0)), t_spec, t_spec],
        out_specs=[t_spec] * 2, sem=("parallel", "parallel"))


def _norm_mod(x, w, sc, sh, name):
    T = x.shape[0]
    tm = _rows(T)

    def body(x_ref, w_ref, sc_ref, sh_ref, o_ref):
        xv = x_ref[...]
        r = lax.rsqrt(jnp.mean(xv * xv, axis=-1, keepdims=True) + EPS)
        o_ref[...] = ((xv * r * w_ref[...]) * (1.0 + sc_ref[...]) + sh_ref[...]).astype(BF16)

    return _pcall(body, name=name, out_shape=_sds((T, D), BF16), grid=(T // tm,),
                  in_specs=[_row_spec(tm, D), _vec_spec(D), _vec_spec(D), _vec_spec(D)],
                  out_specs=_row_spec(tm, D), sem=("parallel",))(x, w, sc, sh)


def _gate_rows(dxv, br_ref, g_ref, db_ref, acc_ref):
    db = g_ref[...] * dxv
    db_ref[...] = db.astype(BF16)
    acc_ref[3:4, :] += _colsum(dxv * br_ref[...].astype(F32))
    acc_ref[4:5, :] += _colsum(db)


def _norm_mod_bwd(x, dh, dres, w, sc, branch, g, name, side=None):
    T = x.shape[0]
    tm = _rows(T)
    gated = branch is not None

    def body(x_ref, dh_ref, dres_ref, w_ref, sc_ref, *rest):
        (br_ref, g_ref, dx_ref, db_ref, acc_ref) = rest if gated else (None, None, rest[0], None, rest[1])

        @pl.when(pl.program_id(0) == 0)
        def _():
            acc_ref[...] = jnp.zeros_like(acc_ref)
        xv, dh_v, wv = x_ref[...], dh_ref[...], w_ref[...]
        r = lax.rsqrt(jnp.mean(xv * xv, axis=-1, keepdims=True) + EPS)
        n = xv * r
        dnw = dh_v * (1.0 + sc_ref[...])
        dn = dnw * wv
        dxv = dres_ref[...] + r * (dn - n * jnp.mean(dn * n, axis=-1, keepdims=True))
        dx_ref[...] = dxv
        acc_ref[0:1, :] += _colsum(dh_v * (n * wv))
        acc_ref[1:2, :] += _colsum(dh_v)
        acc_ref[2:3, :] += _colsum(dnw * n)
        if gated:
            _gate_rows(dxv, br_ref, g_ref, db_ref, acc_ref)

    row = _row_spec(tm, D)
    args = (x, dh, dres, w, sc) + ((branch, g) if gated else ())
    return _call(body, args, side, name=name,
                 out_shape=[_sds((T, D))] + ([_sds((T, D), BF16)] if gated else []) + [_sds((8, D))], grid=(T // tm,),
                 in_specs=[row, row, row, _vec_spec(D), _vec_spec(D)] + ([row, _vec_spec(D)] if gated else []),
                 out_specs=[row] + ([row] if gated else []) + [_vec_spec(D, 8)], sem=("arbitrary",))


def _final_loss(x, wf, target, branch, g, name):
    T = x.shape[0]
    tm = _rows(T)

    def body(x_ref, w_ref, t_ref, br_ref, g_ref, dx_ref, db_ref, acc_ref):
        @pl.when(pl.program_id(0) == 0)
        def _():
            acc_ref[...] = jnp.zeros_like(acc_ref)
        xv, wv = x_ref[...], w_ref[...]
        r = lax.rsqrt(jnp.mean(xv * xv, axis=-1, keepdims=True) + EPS)
        n = xv * r
        err = n * wv - t_ref[...]
        dy = err * (1.0 / D)
        dn = dy * wv
        dxv = r * (dn - n * jnp.mean(dn * n, axis=-1, keepdims=True))
        dx_ref[...] = dxv
        acc_ref[0:1, :] += _colsum(dy * n)
        acc_ref[1:2, :] += jnp.broadcast_to(_allsum(err * err) * (0.5 / D), (1, D))
        _gate_rows(dxv, br_ref, g_ref, db_ref, acc_ref)

    row = _row_spec(tm, D)
    return _pcall(body, name=name, out_shape=[_sds((T, D)), _sds((T, D), BF16), _sds((8, D))], grid=(T // tm,),
                  in_specs=[row, _vec_spec(D), row, row, _vec_spec(D)],
                  out_specs=[row, row, _vec_spec(D, 8)], sem=("arbitrary",))(x, wf, target, branch, g)


def _mod_matmul(c_all, ada_w, name):
    n = ada_w.shape[2]

    def body(c_ref, w_ref, cond_ref, o_ref):
        cond = _silu(c_ref[...])
        cond_ref[...] = cond
        o_ref[0] = _nn(cond, w_ref[0])

    return _pcall(body, name=name, out_shape=[_sds((N_DEV, D)), _sds((DEPTH, N_DEV, n))], grid=(DEPTH,),
                  in_specs=[pl.BlockSpec((N_DEV, D), lambda l: (0, 0)), pl.BlockSpec((1, D, n), lambda l: (l, 0, 0))],
                  out_specs=[pl.BlockSpec((N_DEV, D), lambda l: (0, 0)), pl.BlockSpec((1, N_DEV, n), lambda l: (l, 0, 0))],
                  sem=("arbitrary",))(c_all, ada_w)


def _add_rows(a, b, name):
    def body(a_ref, b_ref, o_ref):
        o_ref[...] = a_ref[...] + b_ref[...]

    return _pcall(body, name=name, out_shape=_sds(a.shape))(a, b)


def _ada_w_grad(cond, dmod_cols, name):
    n = dmod_cols.shape[2]

    def body(c_ref, d_ref, o_ref):
        o_ref[0] = _tn(c_ref[...], d_ref[0])

    return _pcall(body, name=name, out_shape=_sds((DEPTH, D, n)), grid=(DEPTH,),
                  in_specs=[pl.BlockSpec((N_DEV, D), lambda l: (0, 0)), pl.BlockSpec((1, N_DEV, n), lambda l: (l, 0, 0))],
                  out_specs=pl.BlockSpec((1, D, n), lambda l: (l, 0, 0)), sem=("parallel",))(cond, dmod_cols)


def _conv_fwd(pm, conv_w, conv_b, name):
    T = pm.shape[0]
    tm = _rows(T)
    C = CONV_DIM

    def body(x_ref, prev_ref, w_ref, b_ref, o_ref):
        cur = x_ref[...].astype(F32)
        prev = jnp.where(pl.program_id(0) > 0, prev_ref[...].astype(F32)[8:16], 0.0)
        cur8 = cur[0:8]
        row8 = lax.broadcasted_iota(jnp.int32, (8, C), 0)
        full = w_ref[3:4, :] * cur
        head = w_ref[3:4, :] * cur8
        for k in range(1, SSM_CONV):
            wk = w_ref[3 - k:4 - k, :]
            full = full + wk * pltpu.roll(cur, k, 0)
            head = head + wk * jnp.where(row8 < k, pltpu.roll(prev, k, 0), pltpu.roll(cur8, k, 0))
        o_ref[...] = full + b_ref[...]
        o_ref[0:8, :] = head + b_ref[...]

    return _pcall(
        body, name=name, out_shape=_sds((T, C)), grid=(T // tm,),
        in_specs=[pl.BlockSpec((tm, C), lambda i: (i, 2)),
                  pl.BlockSpec((16, C), lambda i: (jnp.maximum(i * (tm // 16) - 1, 0), 2)),
                  _vec_spec(C, SSM_CONV), _vec_spec(C)],
        out_specs=_row_spec(tm, C), sem=("parallel",))(pm, pm, conv_w, conv_b)


def _conv_bwd(dc, pm, conv_w, name):
    T = dc.shape[0]
    tm = _rows(T)
    C = CONV_DIM
    nt = T // tm

    def body(dc_ref, nxt_ref, x_ref, prev_ref, w_ref, dx_ref, acc_ref):
        i = pl.program_id(0)

        @pl.when(i == 0)
        def _():
            acc_ref[...] = jnp.zeros_like(acc_ref)
        dcv = dc_ref[...]
        nxt = jnp.where(i < nt - 1, nxt_ref[...], 0.0)
        xc = x_ref[...].astype(F32)
        prev = jnp.where(i > 0, prev_ref[...].astype(F32)[8:16], 0.0)
        dc8h, dc8t, x8 = dcv[0:8], dcv[tm - 8:tm], xc[0:8]
        row8 = lax.broadcasted_iota(jnp.int32, (8, C), 0)
        full = w_ref[3:4, :] * dcv
        tail = w_ref[3:4, :] * dc8t
        acc_ref[3:4, :] += _colsum(dcv * xc)
        for k in range(1, SSM_CONV):
            wk = w_ref[3 - k:4 - k, :]
            full = full + wk * pltpu.roll(dcv, tm - k, 0)
            tail = tail + wk * jnp.where(row8 + k >= 8, pltpu.roll(nxt, 8 - k, 0), pltpu.roll(dc8t, 8 - k, 0))
            xs_head = jnp.where(row8 < k, pltpu.roll(prev, k, 0), pltpu.roll(x8, k, 0))
            prod = dcv * pltpu.roll(xc, k, 0)
            acc_ref[3 - k:4 - k, :] += _colsum(prod) - _colsum(prod[0:8]) + _colsum(dc8h * xs_head)
        acc_ref[4:5, :] += _colsum(dcv)
        dx_ref[...] = jnp.concatenate([full[0:tm - 8], tail], axis=0).astype(BF16)

    return _pcall(
        body, name=name, out_shape=[_sds((T, C), BF16), _sds((8, C))], grid=(nt,),
        in_specs=[_row_spec(tm, C),
                  pl.BlockSpec((8, C), lambda i: (jnp.minimum((i + 1) * (tm // 8), T // 8 - 1), 0)),
                  pl.BlockSpec((tm, C), lambda i: (i, 2)),
                  pl.BlockSpec((16, C), lambda i: (jnp.maximum(i * (tm // 16) - 1, 0), 2)),
                  _vec_spec(C, SSM_CONV)],
        out_specs=[_row_spec(tm, C), _vec_spec(C, 8)], sem=("arbitrary",))(dc, dc, pm, pm, conv_w)


def _ssd_prologue(cpre, dtr, dtb, alog):
    L = CHUNK
    xc = _silu(cpre)
    pre = dtr + dtb
    dt = jnp.maximum(pre, 0.0) + jnp.log1p(jnp.exp(-jnp.abs(pre)))
    a = -jnp.exp(alog)
    la = dt * a
    row = lax.broadcasted_iota(jnp.int32, (L, L), 0)
    col = lax.broadcasted_iota(jnp.int32, (L, L), 1)
    causal = row >= col
    tri = causal.astype(F32)
    lc = _nn(tri, la, HIGHEST)
    return xc, pre, dt, a, causal, tri, lc, row, col


def _head_indicator():
    m = np.zeros((LANES, SSM_INNER), np.float32)
    for h in range(SSM_HEADS):
        m[h, h * SSM_HEAD_DIM:(h + 1) * SSM_HEAD_DIM] = 1.0
    return jnp.asarray(m, dtype=BF16)


def _split_dot(x, ind, dims):
    hi = x.astype(BF16)
    lo = (x - hi.astype(F32)).astype(BF16)
    return _dot(hi, ind, dims) + _dot(lo, ind, dims)


def _expand(x16, ind):
    return _split_dot(x16, ind, ((1,), (0,)))


def _headsum(x, ind, single_pass=False):
    if single_pass:
        return _dot(x.astype(BF16), ind, ((1,), (1,)))
    return _split_dot(x, ind, ((1,), (1,)))


def _ssd_fwd(cpre, dtr, pm, dtb, alog, dskip, normw, ind, name, side=None):
    T = cpre.shape[0]
    nc = T // CHUNK
    L, P, H, HPG, N = CHUNK, SSM_HEAD_DIM, SSM_HEADS, SSM_HEADS // SSM_GROUPS, SSM_STATE
    half = SSM_INNER // SSM_GROUPS

    def body(cp_ref, dtr_ref, z_ref, dtb_ref, alog_ref, dskip_ref, nw_ref, ind_ref, ya_ref, y_ref, sp_ref, st_ref):
        @pl.when(pl.program_id(0) == 0)
        def _():
            st_ref[...] = jnp.zeros_like(st_ref)
        xc, _, dt, _, causal, _, lc, _, _ = _ssd_prologue(cp_ref[...], dtr_ref[...], dtb_ref[...], alog_ref[...])
        lct = lc.T
        ind = ind_ref[...]
        llast = lc[L - 1:L, :]
        xs = xc[:, :SSM_INNER]
        xd = xs * _expand(dt, ind)
        ex = _expand(jnp.exp(lc), ind)
        xd_end = _bf(xd * _expand(jnp.exp(llast - lc), ind))
        cdx = _expand(jnp.broadcast_to(jnp.exp(llast), (8, LANES)), ind)[0:1]
        xdb = _bf(xd)
        sp_ref[0] = st_ref[...]
        for g in range(SSM_GROUPS):
            sl = slice(g * half, (g + 1) * half)
            bm = _bf(xc[:, SSM_INNER + g * N:SSM_INNER + (g + 1) * N])
            cm = _bf(xc[:, SSM_INNER + (SSM_GROUPS + g) * N:SSM_INNER + (SSM_GROUPS + g + 1) * N])
            cb = _nt(cm, bm)
            st = st_ref[g]
            y_ref[:, sl] = ex[:, sl] * _nn(cm, _bf(st)) + dskip_ref[:, sl] * xs[:, sl]
            st_ref[g] = st * cdx[:, sl] + _tn(bm, xd_end[:, sl])
            for j in range(HPG):
                h = g * HPG + j
                decay = jnp.where(causal, jnp.exp(jnp.where(causal, lc[:, h:h + 1] - lct[h:h + 1, :], 0.0)), 0.0)
                y_ref[:, h * P:(h + 1) * P] += _nn(_bf(cb * decay), xdb[:, h * P:(h + 1) * P])
        y2 = y_ref[...] * _silu(z_ref[...].astype(F32))
        for g in range(SSM_GROUPS):
            yg = y2[:, g * half:(g + 1) * half]
            r = lax.rsqrt(jnp.mean(yg * yg, axis=-1, keepdims=True) + EPS)
            ya_ref[:, g * half:(g + 1) * half] = (yg * r * nw_ref[:, g * half:(g + 1) * half]).astype(BF16)

    return _call(
        body, (cpre, dtr, pm, dtb, alog, dskip, normw, ind), side, name=name,
        out_shape=[_sds((T, SSM_INNER), BF16), _sds((T, SSM_INNER)), _sds((nc, SSM_GROUPS, N, half))], grid=(nc,),
        in_specs=[_row_spec(L, CONV_DIM), _row_spec(L, LANES), _row_spec(L, SSM_INNER, 0),
                  _vec_spec(LANES), _vec_spec(LANES), _vec_spec(SSM_INNER), _vec_spec(SSM_INNER), _vec_spec(SSM_INNER, LANES)],
        out_specs=[_row_spec(L, SSM_INNER), _row_spec(L, SSM_INNER),
                   pl.BlockSpec((1, SSM_GROUPS, N, half), lambda i: (i, 0, 0, 0))],
        scratch=[pltpu.VMEM((SSM_GROUPS, N, half), F32)], sem=("arbitrary",))


def _ssd_bwd(cpre, dtr, pm, ypre, sprev, dya, dtb, alog, dskip, normw, ind, name, side=None):
    T = cpre.shape[0]
    nc = T // CHUNK
    L, P, H, HPG, N = CHUNK, SSM_HEAD_DIM, SSM_HEADS, SSM_HEADS // SSM_GROUPS, SSM_STATE
    half = SSM_INNER // SSM_GROUPS

    def body(cp_ref, dtr_ref, z_ref, y_ref, sp_ref, dya_ref, dtb_ref, alog_ref, dskip_ref, nw_ref, ind_ref,
             dz_ref, dcp_ref, ddtr_ref, acc_ref, dnw_ref, ds_ref, dy_ref, dxd_ref, rr_ref, yoff_ref, dcd_ref):
        @pl.when(pl.program_id(0) == 0)
        def _():
            ds_ref[...] = jnp.zeros_like(ds_ref)
            acc_ref[...] = jnp.zeros_like(acc_ref)
            dnw_ref[...] = jnp.zeros_like(dnw_ref)
        cpre_v = cp_ref[...]
        xc, pre, dt, a, causal, tri, lc, row, col = _ssd_prologue(cpre_v, dtr_ref[...], dtb_ref[...], alog_ref[...])
        lct = lc.T
        zv, yv = z_ref[...].astype(F32), y_ref[...]
        sz = _silu(zv)
        y2 = yv * sz
        dya_v = dya_ref[...]
        nwv = nw_ref[...]
        for g in range(SSM_GROUPS):
            sl = slice(g * half, (g + 1) * half)
            yg = y2[:, sl]
            r = lax.rsqrt(jnp.mean(yg * yg, axis=-1, keepdims=True) + EPS)
            nrm = yg * r
            dnw_ref[:, sl] += _colsum(dya_v[:, sl] * nrm)
            dn = dya_v[:, sl] * nwv[:, sl]
            dy2 = r * (dn - nrm * jnp.mean(dn * nrm, axis=-1, keepdims=True))
            dy_ref[:, sl] = dy2 * sz[:, sl]
            dz_ref[:, sl] = (dy2 * yv[:, sl] * _dsilu(zv[:, sl])).astype(BF16)
        ind = ind_ref[...]
        llast = lc[L - 1:L, :]
        dte16 = jnp.exp(llast - lc)
        cd16 = jnp.exp(llast)
        xs = xc[:, :SSM_INNER]
        dtx = _expand(dt, ind)
        ex = _expand(jnp.exp(lc), ind)
        dtex = _expand(dte16, ind)
        cdx = _expand(jnp.broadcast_to(cd16, (8, LANES)), ind)[0:1]
        xd = xs * dtx
        xdb = _bf(xd)
        xd_end = _bf(xd * dtex)
        dyv = dy_ref[...]
        dy_off = _bf(ex * dyv)
        dyb = _bf(dyv)
        dskx = dskip_ref[...]
        lane_c = lax.broadcasted_iota(jnp.int32, (L, LANES), 1)
        lane1 = lax.broadcasted_iota(jnp.int32, (1, LANES), 1)
        sub16 = lax.broadcasted_iota(jnp.int32, (H, L), 0)
        dlc_c = jnp.zeros((L, LANES), F32)
        dlc_r = jnp.zeros((H, L), F32)
        for g in range(SSM_GROUPS):
            sl = slice(g * half, (g + 1) * half)
            b_lo = SSM_INNER + g * N
            c_lo = SSM_INNER + (SSM_GROUPS + g) * N
            bm, cm = _bf(xc[:, b_lo:b_lo + N]), _bf(xc[:, c_lo:c_lo + N])
            cb = _nt(cm, bm)
            st, dst = sp_ref[0, g], ds_ref[g]
            stb, dstb = _bf(st), _bf(dst)
            dcm = _nt(dy_off[:, sl], stb)
            ds_ref[g] = _tn(cm, dy_off[:, sl]) + dst * cdx[:, sl]
            rr_ref[:, sl] = _nn(bm, dstb)
            yoff_ref[:, sl] = ex[:, sl] * _nn(cm, stb)
            db = _nt(xd_end[:, sl], dstb)
            dcd_ref[:, sl] = _colsum(dst * st)
            dcb = jnp.zeros((L, L), F32)
            for j in range(HPG):
                h = g * HPG + j
                hs = slice(h * P, (h + 1) * P)
                decay = jnp.where(causal, jnp.exp(jnp.where(causal, lc[:, h:h + 1] - lct[h:h + 1, :], 0.0)), 0.0)
                m = cb * decay
                dxd_ref[:, hs] = _tn(_bf(m), dyb[:, hs])
                dm = _nt(dyb[:, hs], xdb[:, hs])
                dcb = dcb + dm * decay
                gm = dm * m
                dlc_c = dlc_c + jnp.where(lane_c == h, _rowsum(gm), 0.0)
                dlc_r = dlc_r + jnp.where(sub16 == h, _colsum(gm), 0.0)
            dcbb = _bf(dcb)
            dcp_ref[:, c_lo:c_lo + N] = dcm + _nn(dcbb, bm)
            dcp_ref[:, b_lo:b_lo + N] = db + _tn(dcbb, cm)
        dxd_diag, rr = dxd_ref[...], rr_ref[...]
        tt = _headsum(rr * xd, ind, single_pass=True) * dte16
        dlc_rt = jnp.concatenate([dlc_r, jnp.zeros((LANES - H, L), F32)], axis=0).T
        dlc = dlc_c - dlc_rt + _headsum(dyv * yoff_ref[...], ind, single_pass=True) - tt
        dcd = _headsum(jnp.broadcast_to(dcd_ref[...], (8, SSM_INNER)), ind)[0:1]
        dlc = dlc + jnp.where(row == L - 1, _colsum(tt) + dcd * cd16, 0.0)
        dla = _tn(tri, dlc, HIGHEST)
        dxd = dxd_diag + dtex * rr
        ddt = _headsum(dxd * xs, ind, single_pass=True) + dla * a
        ddtr = jnp.where(lane_c < H, ddt * _sigmoid(pre), 0.0)
        ddtr_ref[...] = ddtr
        acc_ref[0:1, :] += _colsum(ddtr)
        acc_ref[1:2, :] += jnp.where(lane1 < H, _colsum(dla * dt) * a, 0.0)
        acc_ref[2:3, :] += _headsum(jnp.broadcast_to(_colsum(dyv * xs), (8, SSM_INNER)), ind)[0:1]
        dcp_ref[:, 0:SSM_INNER] = dxd * dtx + dskx * dyv
        dcp_ref[...] = dcp_ref[...] * _dsilu(cpre_v)

    rev = lambda i: (nc - 1 - i, 0)
    rspec = lambda c: pl.BlockSpec((L, c), rev)
    return _call(
        body, (cpre, dtr, pm, ypre, sprev, dya, dtb, alog, dskip, normw, ind), side, name=name,
        out_shape=[_sds((T, SSM_INNER), BF16), _sds((T, CONV_DIM)), _sds((T, LANES)), _sds((8, LANES)), _sds((1, SSM_INNER))],
        grid=(nc,),
        in_specs=[rspec(CONV_DIM), rspec(LANES), rspec(SSM_INNER), rspec(SSM_INNER),
                  pl.BlockSpec((1, SSM_GROUPS, N, half), lambda i: (nc - 1 - i, 0, 0, 0)), rspec(SSM_INNER),
                  _vec_spec(LANES), _vec_spec(LANES), _vec_spec(SSM_INNER), _vec_spec(SSM_INNER), _vec_spec(SSM_INNER, LANES)],
        out_specs=[rspec(SSM_INNER), rspec(CONV_DIM), rspec(LANES), _vec_spec(LANES, 8), _vec_spec(SSM_INNER)],
        scratch=[pltpu.VMEM((SSM_GROUPS, N, half), F32), pltpu.VMEM((L, SSM_INNER), F32), pltpu.VMEM((L, SSM_INNER), F32),
                 pltpu.VMEM((L, SSM_INNER), F32), pltpu.VMEM((L, SSM_INNER), F32), pltpu.VMEM((1, SSM_INNER), F32)],
        sem=("arbitrary",))


def _gmlp_common(u, v, lnw, lnb):
    ug = _gelu(u)
    vg = _gelu(v)
    mu = jnp.mean(vg, axis=-1, keepdims=True)
    cen = vg - mu
    rstd = lax.rsqrt(jnp.mean(cen * cen, axis=-1, keepdims=True) + EPS)
    vhat = cen * rstd
    return ug, rstd, vhat, vhat * lnw + lnb


def _causal_mask():
    row = lax.broadcasted_iota(jnp.int32, (CHUNK, CHUNK), 0)
    col = lax.broadcasted_iota(jnp.int32, (CHUNK, CHUNK), 1)
    return row >= col


def _gmlp_fwd(pm, lnw, lnb, ws, bs_exp, name, side=None):
    T = pm.shape[0]
    nc = T // CHUNK
    L, G = CHUNK, GMLP_GROUPS

    def body(u_ref, v_ref, lnw_ref, lnb_ref, ws_ref, bs_ref, o_ref):
        ug, _, _, vn = _gmlp_common(u_ref[...].astype(F32), v_ref[...].astype(F32), lnw_ref[...], lnb_ref[...])
        causal = _causal_mask()
        for g in range(G):
            sl = slice(g * L, (g + 1) * L)
            wm = _bf(jnp.where(causal, ws_ref[g], 0.0))
            sv = _nn(wm, _bf(vn[:, sl])) + bs_ref[:, sl]
            o_ref[:, sl] = (ug[:, sl] * sv).astype(BF16)

    return _call(
        body, (pm, pm, lnw, lnb, ws, bs_exp), side, name=name, out_shape=[_sds((T, GMLP_INNER), BF16)], grid=(nc,),
        in_specs=[_row_spec(L, GMLP_INNER, 1), _row_spec(L, GMLP_INNER, 2), _vec_spec(GMLP_INNER), _vec_spec(GMLP_INNER),
                  pl.BlockSpec((G, L, L), lambda i: (0, 0, 0)), _vec_spec(GMLP_INNER, L)],
        out_specs=[_row_spec(L, GMLP_INNER)], sem=("parallel",))


def _gmlp_bwd(pm, dyb, lnw, lnb, ws, bs_exp, name, side=None):
    T = pm.shape[0]
    nc = T // CHUNK
    L, G = CHUNK, GMLP_GROUPS

    def body(u_ref, v_ref, dy_ref, lnw_ref, lnb_ref, ws_ref, bs_ref, du_ref, dv_ref, dws_ref, dbs_ref, acc_ref, dvn_ref):
        @pl.when(pl.program_id(0) == 0)
        def _():
            dws_ref[...] = jnp.zeros_like(dws_ref)
            dbs_ref[...] = jnp.zeros_like(dbs_ref)
            acc_ref[...] = jnp.zeros_like(acc_ref)
        uv, vv, dyv, lnwv = u_ref[...].astype(F32), v_ref[...].astype(F32), dy_ref[...], lnw_ref[...]
        ug, rstd, vhat, vn = _gmlp_common(uv, vv, lnwv, lnb_ref[...])
        causal = _causal_mask()
        lane = lax.broadcasted_iota(jnp.int32, (L, LANES), 1)
        dbs = jnp.zeros((L, LANES), F32)
        for g in range(G):
            sl = slice(g * L, (g + 1) * L)
            wm = _bf(jnp.where(causal, ws_ref[g], 0.0))
            vng = _bf(vn[:, sl])
            sv = _nn(wm, vng) + bs_ref[:, sl]
            du_ref[:, sl] = (dyv[:, sl] * sv * _dgelu(uv[:, sl])).astype(BF16)
            dsv = dyv[:, sl] * ug[:, sl]
            dsvb = _bf(dsv)
            dws_ref[g] += jnp.where(causal, _nt(dsvb, vng), 0.0)
            dbs = dbs + jnp.where(lane == g, _rowsum(dsv), 0.0)
            dvn_ref[:, sl] = _tn(wm, dsvb)
        dbs_ref[...] += dbs
        dvn = dvn_ref[...]
        acc_ref[0:1, :] += _colsum(dvn * vhat)
        acc_ref[1:2, :] += _colsum(dvn)
        dvh = dvn * lnwv
        dvg = rstd * (dvh - jnp.mean(dvh, axis=-1, keepdims=True) - vhat * jnp.mean(dvh * vhat, axis=-1, keepdims=True))
        dv_ref[...] = (dvg * _dgelu(vv)).astype(BF16)

    return _call(
        body, (pm, pm, dyb, lnw, lnb, ws, bs_exp), side, name=name,
        out_shape=[_sds((T, GMLP_INNER), BF16), _sds((T, GMLP_INNER), BF16), _sds((G, L, L)), _sds((L, LANES)), _sds((8, GMLP_INNER))],
        grid=(nc,),
        in_specs=[_row_spec(L, GMLP_INNER, 1), _row_spec(L, GMLP_INNER, 2), _row_spec(L, GMLP_INNER),
                  _vec_spec(GMLP_INNER), _vec_spec(GMLP_INNER), pl.BlockSpec((G, L, L), lambda i: (0, 0, 0)),
                  _vec_spec(GMLP_INNER, L)],
        out_specs=[_row_spec(L, GMLP_INNER), _row_spec(L, GMLP_INNER), pl.BlockSpec((G, L, L), lambda i: (0, 0, 0)),
                   _vec_spec(LANES, L), _vec_spec(GMLP_INNER, 8)],
        scratch=[pltpu.VMEM((L, GMLP_INNER), F32)], sem=("arbitrary",))


def _rel_buckets():
    qi = np.arange(CHUNK)[:, None]
    sj = np.arange(2 * CHUNK)[None, :]
    dist = np.maximum(qi + CHUNK - sj, 0)
    max_exact = REL_BUCKETS // 2
    log_ratio = (np.log(np.maximum(dist, 1).astype(np.float32) / np.float32(max_exact))
                 / np.float32(math.log(REL_MAX_DIST / max_exact))).astype(np.float32)
    large = max_exact + (log_ratio * np.float32(REL_BUCKETS - max_exact)).astype(np.int32)
    return np.where(dist < max_exact, dist, np.minimum(large, REL_BUCKETS - 1))


def _bucket_onehot_t():
    bucket = _rel_buckets().reshape(-1)
    return jnp.asarray((np.arange(REL_BUCKETS)[:, None] == bucket[None, :]).astype(np.float32))


def _bias_from_table(table_t, onehot_t, name):
    def body(t_ref, o_ref, out_ref):
        out_ref[...] = _nn(t_ref[...], o_ref[...], HIGHEST)

    return _pcall(body, name=name, out_shape=_sds((ATTN_HEADS, onehot_t.shape[1])))(table_t, onehot_t)


def _table_from_dbias(dbias, onehot_t, name):
    def body(d_ref, o_ref, out_ref):
        out_ref[...] = _nt(d_ref[...], o_ref[...], HIGHEST)

    return _pcall(body, name=name, out_shape=_sds((ATTN_HEADS, REL_BUCKETS)))(dbias, onehot_t)


def _softmax_sink(logits, sink, mask):
    logits = jnp.where(mask, logits, NEG_INF)
    mx = jnp.maximum(jnp.max(logits, axis=-1, keepdims=True), sink)
    e = jnp.exp(logits - mx)
    es = jnp.exp(sink - mx)
    inv = 1.0 / (_rowsum(e) + es)
    return e * inv, es * inv


def _attn_mask(n, heads):
    qi = lax.broadcasted_iota(jnp.int32, (heads * CHUNK, 2 * CHUNK), 0) & (CHUNK - 1)
    sj = lax.broadcasted_iota(jnp.int32, (heads * CHUNK, 2 * CHUNK), 1)
    rel = qi + CHUNK - sj
    return (rel >= 0) & (rel < CHUNK) & ((sj >= CHUNK) | (n > 0))


def _stack_heads(ref, first, count, width):
    return jnp.concatenate([_bf(ref[:, (first + j) * width:(first + j + 1) * width]) for j in range(count)], axis=0)


def _attn_fwd(qkv, bias, sinks, name):
    T = qkv.shape[0]
    nb = T // CHUNK
    L, DH, HPK = CHUNK, ATTN_DH, ATTN_HEADS // ATTN_KV
    scale = DH ** -0.5
    kcol, vcol = ATTN_HEADS * DH // LANES, ATTN_HEADS * DH // LANES + 1

    def body(q_ref, k_ref, v_ref, kp_ref, vp_ref, bias_ref, sink_ref, o_ref, lg_ref, p_ref):
        n = pl.program_id(0)
        mask = _attn_mask(n, 1)
        kband = _bf(jnp.concatenate([kp_ref[...], k_ref[...]], axis=0))
        vband = _bf(jnp.concatenate([vp_ref[...], v_ref[...]], axis=0))
        for kv in range(ATTN_KV):
            lg_ref[...] = _nt(_stack_heads(q_ref, kv * HPK, HPK, DH), kband[:, kv * DH:(kv + 1) * DH])
            for j in range(HPK):
                h = kv * HPK + j
                p, _ = _softmax_sink(lg_ref[j * L:(j + 1) * L, :] * scale + bias_ref[h], sink_ref[h], mask)
                p_ref[j * L:(j + 1) * L, :] = _bf(p)
            og = _nn(p_ref[...], vband[:, kv * DH:(kv + 1) * DH])
            for j in range(HPK):
                h = kv * HPK + j
                o_ref[:, h * DH:(h + 1) * DH] = og[j * L:(j + 1) * L].astype(BF16)

    prev = lambda i: jnp.maximum(i - 1, 0)
    return _pcall(
        body, name=name, out_shape=_sds((T, ATTN_HEADS * DH), BF16), grid=(nb,),
        in_specs=[_row_spec(L, ATTN_HEADS * DH, 0), _row_spec(L, LANES, kcol), _row_spec(L, LANES, vcol),
                  pl.BlockSpec((L, LANES), lambda i: (prev(i), kcol)), pl.BlockSpec((L, LANES), lambda i: (prev(i), vcol)),
                  pl.BlockSpec((ATTN_HEADS, L, 2 * L), lambda i: (0, 0, 0)),
                  pl.BlockSpec(memory_space=pltpu.SMEM)],
        out_specs=_row_spec(L, ATTN_HEADS * DH),
        scratch=[pltpu.VMEM((HPK * L, 2 * L), F32), pltpu.VMEM((HPK * L, 2 * L), BF16)],
        sem=("parallel",))(qkv, qkv, qkv, qkv, qkv, bias, sinks)


def _attn_bwd(qkv, datt, bias, sinks, name):
    T = qkv.shape[0]
    nb = T // CHUNK
    L, DH, HPK = CHUNK, ATTN_DH, ATTN_HEADS // ATTN_KV
    scale = DH ** -0.5
    kcol, vcol = ATTN_HEADS * DH // LANES, ATTN_HEADS * DH // LANES + 1

    def body(q_ref, k_ref, v_ref, kp_ref, vp_ref, do_ref, bias_ref, sink_ref,
             dq_ref, dk_ref, dv_ref, bsum_ref, dbias_ref, dsink_ref, pend_k, pend_v, band_k, band_v, lg_ref, dp_ref, p_ref, dl_ref):
        n = pl.program_id(0)

        @pl.when(n == 0)
        def _():
            dbias_ref[...] = jnp.zeros_like(dbias_ref)
            dsink_ref[...] = jnp.zeros_like(dsink_ref)
            bsum_ref[...] = jnp.zeros_like(bsum_ref)

        def emit_kv(dk, dv):
            dk_ref[...] = dk.astype(BF16)
            dv_ref[...] = dv.astype(BF16)
            bsum_ref[:, ATTN_HEADS * DH:ATTN_HEADS * DH + LANES] += _colsum(dk)
            bsum_ref[:, ATTN_HEADS * DH + LANES:] += _colsum(dv)

        @pl.when(n < nb)
        def _():
            mask = _attn_mask(n, 1)
            kband = _bf(jnp.concatenate([kp_ref[...], k_ref[...]], axis=0))
            vband = _bf(jnp.concatenate([vp_ref[...], v_ref[...]], axis=0))
            lane1 = lax.broadcasted_iota(jnp.int32, (1, LANES), 1)
            dsink = jnp.zeros((1, LANES), F32)
            for kv in range(ATTN_KV):
                kb, vb = kband[:, kv * DH:(kv + 1) * DH], vband[:, kv * DH:(kv + 1) * DH]
                qg = _stack_heads(q_ref, kv * HPK, HPK, DH)
                dog = _stack_heads(do_ref, kv * HPK, HPK, DH)
                lg_ref[...] = _nt(qg, kb)
                dp_ref[...] = _nt(dog, vb)
                for j in range(HPK):
                    h = kv * HPK + j
                    rows = slice(j * L, (j + 1) * L)
                    p, ps = _softmax_sink(lg_ref[rows, :] * scale + bias_ref[h], sink_ref[h], mask)
                    dp = dp_ref[rows, :]
                    delta = _rowsum(p * dp)
                    dl = p * (dp - delta)
                    dbias_ref[h] += dl
                    p_ref[rows, :] = _bf(p)
                    dl_ref[rows, :] = _bf(dl)
                    dsink = dsink + jnp.where(lane1 == h, -_colsum(ps * delta), 0.0)
                band_v[:, kv * DH:(kv + 1) * DH] = _tn(p_ref[...], dog)
                dqg = _nn(dl_ref[...], kb) * scale
                band_k[:, kv * DH:(kv + 1) * DH] = _tn(dl_ref[...], qg) * scale
                for j in range(HPK):
                    h = kv * HPK + j
                    dq_ref[:, h * DH:(h + 1) * DH] = dqg[j * L:(j + 1) * L].astype(BF16)
                    bsum_ref[:, h * DH:(h + 1) * DH] += _colsum(dqg[j * L:(j + 1) * L])
            dsink_ref[...] += dsink

            @pl.when(n > 0)
            def _():
                emit_kv(pend_k[...] + band_k[0:L, :], pend_v[...] + band_v[0:L, :])
            pend_k[...] = band_k[L:2 * L, :]
            pend_v[...] = band_v[L:2 * L, :]

        @pl.when(n == nb)
        def _():
            emit_kv(pend_k[...], pend_v[...])

    cur = lambda i: jnp.minimum(i, nb - 1)
    prev = lambda i: jnp.maximum(jnp.minimum(i, nb - 1) - 1, 0)
    lag = lambda i: jnp.maximum(i - 1, 0)
    return _pcall(
        body, name=name,
        out_shape=[_sds((T, ATTN_HEADS * DH), BF16), _sds((T, LANES), BF16), _sds((T, LANES), BF16), _sds((1, QKV_DIM)),
                   _sds((ATTN_HEADS, L, 2 * L)), _sds((1, LANES))],
        grid=(nb + 1,),
        in_specs=[pl.BlockSpec((L, ATTN_HEADS * DH), lambda i: (cur(i), 0)),
                  pl.BlockSpec((L, LANES), lambda i: (cur(i), kcol)), pl.BlockSpec((L, LANES), lambda i: (cur(i), vcol)),
                  pl.BlockSpec((L, LANES), lambda i: (prev(i), kcol)), pl.BlockSpec((L, LANES), lambda i: (prev(i), vcol)),
                  pl.BlockSpec((L, ATTN_HEADS * DH), lambda i: (cur(i), 0)),
                  pl.BlockSpec((ATTN_HEADS, L, 2 * L), lambda i: (0, 0, 0)),
                  pl.BlockSpec(memory_space=pltpu.SMEM)],
        out_specs=[pl.BlockSpec((L, ATTN_HEADS * DH), lambda i: (cur(i), 0)),
                   pl.BlockSpec((L, LANES), lambda i: (lag(i), 0)), pl.BlockSpec((L, LANES), lambda i: (lag(i), 0)),
                   _vec_spec(QKV_DIM), pl.BlockSpec((ATTN_HEADS, L, 2 * L), lambda i: (0, 0, 0)), _vec_spec(LANES)],
        scratch=[pltpu.VMEM((L, LANES), F32), pltpu.VMEM((L, LANES), F32),
                 pltpu.VMEM((2 * L, LANES), F32), pltpu.VMEM((2 * L, LANES), F32),
                 pltpu.VMEM((HPK * L, 2 * L), F32), pltpu.VMEM((HPK * L, 2 * L), F32),
                 pltpu.VMEM((HPK * L, 2 * L), BF16), pltpu.VMEM((HPK * L, 2 * L), BF16)],
        sem=("arbitrary",))(qkv, qkv, qkv, qkv, qkv, datt, bias, sinks)


def _pad_rows(a, mult):
    pad = (-a.shape[-2]) % mult
    if pad == 0:
        return a
    cfg = [(0, 0)] * (a.ndim - 2) + [(0, pad), (0, 0)]
    return jnp.pad(a, cfg)


class _Pack:
    def __init__(self, width, mult, total_mult):
        self.width, self.mult, self.total_mult = width, mult, total_mult
        self.entries = []
        self.rows = 0

    def add(self, key, shape):
        n = int(np.prod(shape))
        rows = -(-n // self.width)
        self.entries.append((key, self.rows, rows, tuple(shape)))
        self.rows += -(-rows // self.mult) * self.mult

    @property
    def total(self):
        return -(-self.rows // self.total_mult) * self.total_mult

    def pack(self, pieces, dtype, lead=()):
        parts = []
        for key, _, rows, shape in self.entries:
            a = pieces[key].astype(dtype).reshape(lead + (-1,))
            n = int(np.prod(shape))
            a = jnp.pad(a, [(0, 0)] * len(lead) + [(0, rows * self.width - n)])
            a = a.reshape(lead + (rows, self.width))
            parts.append(_pad_rows(a, self.mult))
        out = jnp.concatenate(parts, axis=len(lead))
        return _pad_rows(out, self.total_mult)

    def unpack(self, packed, lead=()):
        out = {}
        for key, off, rows, shape in self.entries:
            a = lax.slice_in_dim(packed, off, off + rows, axis=len(lead))
            a = a.reshape(lead + (-1,))
            n = int(np.prod(shape))
            out[key] = lax.slice_in_dim(a, 0, n, axis=len(lead)).reshape(lead + shape)
        return out


def _ffn_fwd(x, h, mod, wg_t, wu_t, wd, tag, next_norm=None, gather=None):
    side = None if gather is None else (_GatherOps, gather)
    gate, up, act, *gathered = _mm_swiglu(h, wg_t, wu_t, f"ffn_gateup_{tag}", side=side)
    x_out, ffn_out, *h_next = _mm_resid([(act, wd)], x, mod[5:6], name=f"ffn_down_{tag}", norm=next_norm)
    return (x_out, dict(h=h, gate=gate, up=up, act=act, out=ffn_out), *h_next, *gathered)


def _ffn_bwd(dx_out, dffn, x_in, saved, mod, norm_w, wg_t, wu_t, wd, below, tag, exchange=None):
    side = None if exchange is None else (_SiblingOps, exchange)
    dgate, dup, *theirs = _mm_swiglu_bwd(dffn, wd, saved["gate"], saved["up"], f"ffn_act_bwd_{tag}", side=side)
    d_wd = _mm_tn(saved["act"], dffn, name=f"ffn_dwd_{tag}")
    d_wg_t = _mm_tn(dgate, saved["h"], name=f"ffn_dwg_{tag}")
    d_wu_t = _mm_tn(dup, saved["h"], name=f"ffn_dwu_{tag}")
    side = None if exchange is None else (_ChipsOps, _pair_sum(exchange, theirs[0], f"grads_pair_sum_{tag}"))
    dh = _mm([(dgate, wg_t), (dup, wu_t)], "nn", name=f"ffn_dh_{tag}", side=side)
    dh, *from_chips = dh if exchange is not None else (dh,)
    dx, d_below, acc = _norm_mod_bwd(x_in, dh, dx_out, norm_w, mod[4:5], below[0], below[1], f"ffn_norm_bwd_{tag}")
    return (dx, d_below, dict(d_wg=d_wg_t, d_wu=d_wu_t, d_wd=d_wd, acc=acc), *from_chips)


_BIG = [
    ("out_w", "out_w_even", 0, "row"), ("qkv_w", "qkv_w", 0, "col"), ("o_w", "o_w", 0, "row"),
    ("gate0", "ffn_gate_w", 0, "col"), ("up0", "ffn_up_w", 0, "col"), ("down0", "ffn_down_w", 0, "row"),
    ("gate1", "ffn_gate_w", 1, "col"), ("up1", "ffn_up_w", 1, "col"), ("down1", "ffn_down_w", 1, "row"),
    ("in_w", "in_w_even", 0, "col"),
]


def _to_wire(a, kind):
    return a.T if kind == "col" else a


_GATHER_GROUPS = [["out_w", "in_w"], ["gate0", "up0", "down0"], ["qkv_w", "o_w"], ["gate1", "up1", "down1"]]
_GRAD_GROUPS = [["qkv_w", "o_w", "gate1", "up1", "down1"], ["out_w", "gate0", "up0", "down0"], ["in_w"]]

_REPLICATED = ["ada_b", "norm_mix_w", "norm_ffn_w", "conv_b", "dt_bias", "a_log", "d_skip", "ssm_norm_w", "gmlp_ln_w",
               "gmlp_ln_b", "gmlp_ws", "gmlp_bs", "sinks", "rel_table", "final_norm_w"]
_TINY_SHARDED = ["conv_w", "qkv_b", "o_b"]

_WEIGHTS = ['ada_w', 'ada_b', 'norm_mix_w', 'norm_ffn_w', 'in_w_even', 'conv_w', 'conv_b', 'dt_bias', 'a_log', 'd_skip',
            'ssm_norm_w', 'gmlp_ln_w', 'gmlp_ln_b', 'gmlp_ws', 'gmlp_bs', 'out_w_even', 'qkv_w', 'qkv_b', 'o_w', 'o_b',
            'sinks', 'rel_table', 'ffn_gate_w', 'ffn_up_w', 'ffn_down_w', 'final_norm_w']


def _step(x, c, loss_target, W, M, V):
    T = x.shape[1]
    x0 = x[0]
    target = loss_target[0]
    me = 4 * lax.axis_index("x") + 2 * lax.axis_index("y") + lax.axis_index("c")

    w_wire_local = {key: _to_wire(W[name][layer].astype(BF16), kind) for key, name, layer, kind in _BIG}

    def wire_pack(keys, mult):
        gp = _Pack(D, 1, mult)
        for key in keys:
            gp.add(key, w_wire_local[key].shape)
        return gp

    gather_packs = [(gp, gp.pack(w_wire_local, BF16)) for gp in (wire_pack(keys, 16) for keys in _GATHER_GROUPS)]
    grad_packs = [wire_pack(keys, 128) for keys in _GRAD_GROUPS]
    full = {}

    def gathered_weights(group, gathered):
        shards = gather_packs[group][0].unpack(gathered, lead=(N_DEV,))
        full.update({key: a.reshape(-1, D) for key, a in shards.items()})

    gathered_weights(0, _all_gather(gather_packs[0][1], "gather_weights"))

    small_in = _Pack(D, 8, 8)
    small_in.add("c", (1, D))
    small_in.add("conv_w", W["conv_w"][0].shape)
    small_in.add("qkv_b", W["qkv_b"][0].shape)
    small_in.add("o_b", W["o_b"][0].shape)
    sm = small_in.unpack(_all_gather(small_in.pack(
        dict(c=c, conv_w=W["conv_w"][0], qkv_b=W["qkv_b"][0], o_b=W["o_b"][0]), F32), "gather_small"), lead=(N_DEV,))
    c_all = sm["c"].reshape(N_DEV, D)
    conv_w_full = jnp.transpose(sm["conv_w"], (1, 0, 2)).reshape(SSM_CONV, CONV_DIM)
    qkv_b_full = sm["qkv_b"].reshape(1, QKV_DIM)
    o_b_full = sm["o_b"].reshape(1, D)

    ncol = W["ada_w"].shape[2]
    cond, mod_cols = _mod_matmul(c_all, W["ada_w"], "mod_matmul")
    mod_g = _all_gather(mod_cols.reshape(DEPTH * N_DEV, ncol), "gather_mod").reshape(N_DEV, DEPTH, N_DEV, ncol)
    mod_me = lax.dynamic_index_in_dim(mod_g, me, axis=2, keepdims=False)
    mod_me = jnp.transpose(mod_me, (1, 0, 2)).reshape(DEPTH, 6, D)
    mod_me = jnp.pad(mod_me, ((0, 0), (0, 2), (0, 0))).reshape(DEPTH * 8, D)
    ada_b_rows = jnp.pad(W["ada_b"].reshape(DEPTH, 6, D), ((0, 0), (0, 2), (0, 0))).reshape(DEPTH * 8, D)
    mod_all = _add_rows(mod_me, ada_b_rows, "mod_bias").reshape(DEPTH, 8, D)
    mod0, mod1 = mod_all[0], mod_all[1]

    in_t = full["in_w"]
    o1, o2, o3, o4 = SSM_INNER, SSM_INNER + CONV_DIM, SSM_INNER + CONV_DIM + SSM_HEADS, SSM_INNER + CONV_DIM + SSM_HEADS + GMLP_INNER
    w_z, w_xbc, w_dt, w_u, w_v = in_t[:o1], in_t[o1:o2], in_t[o2:o3], in_t[o3:o4], in_t[o4:]
    w_main = jnp.concatenate([w_z, w_u, w_v, w_xbc], axis=0)
    w_dtp = jnp.pad(w_dt, ((0, LANES - SSM_HEADS), (0, 0)))
    out_w = full["out_w"]

    pad16 = lambda a: jnp.pad(a.reshape(1, SSM_HEADS), ((0, 0), (0, LANES - SSM_HEADS)))
    dtb, alog = pad16(W["dt_bias"][0]), pad16(W["a_log"][0])
    dskip = jnp.repeat(W["d_skip"][0], SSM_HEAD_DIM).reshape(1, SSM_INNER)
    ssm_nw = W["ssm_norm_w"]
    lnw, lnb = W["gmlp_ln_w"], W["gmlp_ln_b"]
    ws = W["gmlp_ws"][0]
    bs_exp = jnp.repeat(W["gmlp_bs"][0].T, CHUNK, axis=1)
    conv_b = W["conv_b"]
    nmw, nfw = W["norm_mix_w"], W["norm_ffn_w"]
    onehot_t = _bucket_onehot_t()
    head_ind = _head_indicator()
    bias = _bias_from_table(W["rel_table"].T, onehot_t, "rel_bias").reshape(ATTN_HEADS, CHUNK, 2 * CHUNK)
    sinks = W["sinks"][0]

    h0 = _norm_mod(x0, nmw[0:1], mod0[1:2], mod0[0:1], "mix_norm_0")
    pm = _mm([(h0, w_main)], "nt", name="in_proj", tn_pref=1536, out_dtype=BF16)
    dtr = _mm([(h0, w_dtp)], "nt", name="in_proj_dt")
    cpre = _conv_fwd(pm, conv_w_full, conv_b, "conv_fwd")
    ya, ypre, sprev, gathered_b = _ssd_fwd(cpre, dtr, pm, dtb, alog, dskip, ssm_nw, head_ind, "ssd_fwd",
                                           side=(_GatherOps, gather_packs[1][1]))
    gathered_weights(1, gathered_b)
    yb, gathered_c = _gmlp_fwd(pm, lnw, lnb, ws, bs_exp, "gmlp_fwd", side=(_GatherOps, gather_packs[2][1]))
    gathered_weights(2, gathered_c)
    x1, mix0, hf0 = _mm_resid([(ya, out_w[:SSM_INNER]), (yb, out_w[SSM_INNER:])], x0, mod0[2:3], name="out_proj",
                              norm=(nfw[0:1], mod0[4:5], mod0[3:4]))
    x2, ffn0, h1, gathered_d = _ffn_fwd(x1, hf0, mod0, full["gate0"], full["up0"], full["down0"], "0",
                                        next_norm=(nmw[1:2], mod1[1:2], mod1[0:1]), gather=gather_packs[3][1])
    gathered_weights(3, gathered_d)
    qkv_t, o_w = full["qkv_w"], full["o_w"]
    w_q, w_k, w_v_att = qkv_t[:D], qkv_t[D:D + LANES], qkv_t[D + LANES:]

    qkv = _mm([(h1, qkv_t)], "nt", name="qkv_proj", bias=qkv_b_full, tn_pref=1280, out_dtype=BF16)
    att = _attn_fwd(qkv, bias, sinks, "attn_fwd")
    x3, mix1, hf1 = _mm_resid([(att, o_w)], x2, mod1[2:3], name="o_proj", bias=o_b_full,
                              norm=(nfw[1:2], mod1[4:5], mod1[3:4]))
    x4, ffn1 = _ffn_fwd(x3, hf1, mod1, full["gate1"], full["up1"], full["down1"], "1")

    dx4, dffn1, acc_f = _final_loss(x4, W["final_norm_w"].reshape(1, D), target, ffn1["out"], mod1[5:6], "final_loss")
    loss = lax.psum(acc_f[1, 0], ("x", "y", "c"))
    dx3, dmix1, gf1 = _ffn_bwd(dx4, dffn1, x3, ffn1, mod1, nfw[1:2], full["gate1"], full["up1"], full["down1"],
                               (mix1, mod1[2:3]), "1")
    datt = _mm([(dmix1, o_w)], "nt", name="o_proj_dx", out_dtype=BF16)
    d_o_w = _mm_tn(att, dmix1, name="o_proj_dw")
    dq, dk, dv, d_qkv_b, dbias, dsinks = _attn_bwd(qkv, datt, bias, sinks, "attn_bwd")
    d_table = _table_from_dbias(dbias.reshape(ATTN_HEADS, -1), onehot_t, "rel_table_grad").T
    d_qkv_t = jnp.concatenate([_mm_tn(dq, h1, name="qkv_dw_q"), _mm_tn(dk, h1, name="qkv_dw_k"), _mm_tn(dv, h1, name="qkv_dw_v")], axis=0)
    dh1 = _mm([(dq, w_q), (dk, w_k), (dv, w_v_att)], "nn", name="qkv_proj_dx")
    dx2, dffn0, acc_n1 = _norm_mod_bwd(x2, dh1, dx3, nmw[1:2], mod1[1:2], ffn0["out"], mod0[5:6], "mix_norm_bwd_1")

    g_wire = dict(qkv_w=d_qkv_t, o_w=d_o_w, gate1=gf1["d_wg"], up1=gf1["d_wu"], down1=gf1["d_wd"])

    def packed_partials(group):
        return grad_packs[group].pack({key: g_wire[key].reshape(N_DEV, -1, D) for key in _GRAD_GROUPS[group]},
                                      F32, lead=(N_DEV,))

    from_chips = {}
    dx1, dmix0, gf0, from_chips[0] = _ffn_bwd(dx2, dffn0, x1, ffn0, mod0, nfw[0:1], full["gate0"], full["up0"], full["down0"],
                                              (mix0, mod0[2:3]), "0", exchange=packed_partials(0))

    dya = _mm([(dmix0, out_w[:SSM_INNER])], "nt", name="out_proj_dx_a")
    dyb = _mm([(dmix0, out_w[SSM_INNER:])], "nt", name="out_proj_dx_b")
    d_out_w = jnp.concatenate([_mm_tn(ya, dmix0, name="out_proj_dw_a"), _mm_tn(yb, dmix0, name="out_proj_dw_b")], axis=0)
    g_wire.update(gate0=gf0["d_wg"], up0=gf0["d_wu"], down0=gf0["d_wd"], out_w=d_out_w)
    partials_ffn0 = packed_partials(1)
    du, dvg, d_ws, d_bs, acc_ln, theirs_ffn0 = _gmlp_bwd(pm, dyb, lnw, lnb, ws, bs_exp, "gmlp_bwd",
                                                         side=(_SiblingOps, partials_ffn0))
    pair_ffn0 = _pair_sum(partials_ffn0, theirs_ffn0, "grads_pair_sum_mix")
    dz, dcpre, ddtr, acc_ssd, d_ssm_nw, from_chips[1] = _ssd_bwd(
        cpre, dtr, pm, ypre, sprev, dya, dtb, alog, dskip, ssm_nw, head_ind, "ssd_bwd", side=(_ChipsOps, pair_ffn0))
    dxbc, acc_conv = _conv_bwd(dcpre, pm, conv_w_full, "conv_bwd")
    d_in_t = jnp.concatenate([
        _mm_tn(dz, h0, name="in_dw_z"), _mm_tn(dxbc, h0, name="in_dw_xbc"),
        _mm_tn(ddtr, h0, name="in_dw_dt")[:SSM_HEADS], _mm_tn(du, h0, name="in_dw_u"), _mm_tn(dvg, h0, name="in_dw_v")], axis=0)
    g_wire.update(in_w=d_in_t)
    partials_mix = packed_partials(2)
    dh0, theirs_mix = _mm([(dz, w_z), (dxbc, w_xbc), (ddtr, w_dtp), (du, w_u), (dvg, w_v)], "nn", name="in_proj_dx",
                          side=(_SiblingOps, partials_mix))
    pair_mix = _pair_sum(partials_mix, theirs_mix, "grads_pair_sum")
    grad_x, acc_n0, from_chips[2] = _norm_mod_bwd(x0, dh0, dx1, nmw[0:1], mod0[1:2], None, None, "mix_norm_bwd_0",
                                                  side=(_ChipsOps, pair_mix))

    g_mine = {}
    for group in range(len(_GRAD_GROUPS)):
        g_mine.update(grad_packs[group].unpack(_sum_parts(from_chips[group], f"grads_chip_sum_{group}")))
    res_big = [{}, {}, {}, {}]
    for key, name, layer, kind in _BIG:
        g_nat = _to_wire(g_mine[key], kind)
        outs = _adamw(g_nat[None], W[name][layer], M[name][layer], V[name][layer], f"adamw_{key}")
        for res, out in zip(res_big, outs):
            res[key] = out

    acc_f0, acc_f1 = gf0["acc"], gf1["acc"]
    d_mod = jnp.stack([
        jnp.stack([acc_n0[1], acc_n0[0], acc_f0[3], acc_f0[1], acc_f0[0], acc_n1[3]]),
        jnp.stack([acc_n1[1], acc_n1[0], acc_f1[3], acc_f1[1], acc_f1[0], acc_f[3]])])
    g_small = dict(
        ada_b=d_mod.reshape(DEPTH, 6 * D),
        norm_mix_w=jnp.stack([acc_n0[2], acc_n1[2]]), norm_ffn_w=jnp.stack([acc_f0[2], acc_f1[2]]),
        conv_b=acc_conv[4:5], dt_bias=acc_ssd[0:1, :SSM_HEADS], a_log=acc_ssd[1:2, :SSM_HEADS], d_skip=acc_ssd[2:3, :SSM_HEADS],
        ssm_norm_w=d_ssm_nw, gmlp_ln_w=acc_ln[0:1], gmlp_ln_b=acc_ln[1:2], gmlp_ws=d_ws[None],
        gmlp_bs=d_bs[:, :GMLP_GROUPS].T[None], sinks=dsinks[:, :ATTN_HEADS], rel_table=d_table, final_norm_w=acc_f[0],
        conv_w=acc_conv[0:SSM_CONV], qkv_b=d_qkv_b, o_b=acc_f1[4:5])
    small = _Pack(D, 1, 8)
    for name in _REPLICATED:
        small.add(name, W[name].shape)
    small.add("conv_w", (SSM_CONV, CONV_DIM))
    small.add("qkv_b", (1, QKV_DIM))
    small.add("o_b", (1, D))
    parts_small = _all_gather(small.pack(g_small, F32), "gather_small_grads")
    zeros_tiny = dict(conv_w=jnp.zeros((SSM_CONV, CONV_DIM), F32), qkv_b=jnp.zeros((1, QKV_DIM), F32), o_b=jnp.zeros((1, D), F32))
    pks = lambda S: small.pack({**{name: S[name] for name in _REPLICATED}, **zeros_tiny}, F32)
    res_small = [small.unpack(r) for r in _adamw(parts_small, pks(W), pks(M), pks(V), "adamw_small")]
    g_small_sum = res_small[0]

    n_cw, n_qb, n_ob = W["conv_w"].shape[2], W["qkv_b"].shape[1], W["o_b"].shape[1]
    g_tiny = dict(conv_w=lax.dynamic_slice_in_dim(g_small_sum["conv_w"], me * n_cw, n_cw, axis=1)[None],
                  qkv_b=lax.dynamic_slice_in_dim(g_small_sum["qkv_b"], me * n_qb, n_qb, axis=1),
                  o_b=lax.dynamic_slice_in_dim(g_small_sum["o_b"], me * n_ob, n_ob, axis=1))
    tiny = _Pack(D, 8, 8)
    for name in _TINY_SHARDED:
        tiny.add(name, W[name].shape)
    pkt = lambda S: tiny.pack({name: S[name] for name in _TINY_SHARDED}, F32)
    res_tiny = [tiny.unpack(r) for r in _adamw(pkt(g_tiny)[None], pkt(W), pkt(M), pkt(V), "adamw_tiny")]

    dmod_all = parts_small[:, small.entries[0][1]:small.entries[0][1] + small.entries[0][2]].reshape(N_DEV, DEPTH, 6 * D)
    dmod_cols = jnp.transpose(lax.dynamic_slice_in_dim(dmod_all, me * ncol, ncol, axis=2), (1, 0, 2))
    g_ada_w = _ada_w_grad(cond, dmod_cols, "ada_w_grad")
    flat = lambda a: a.reshape(DEPTH * D, ncol)
    res_ada = [r.reshape(DEPTH, D, ncol) for r in _adamw(flat(g_ada_w)[None], flat(W["ada_w"]), flat(M["ada_w"]), flat(V["ada_w"]), "adamw_ada_w")]

    def result(kind_idx, name):
        if name == "ada_w":
            return res_ada[kind_idx]
        if name in _REPLICATED:
            return res_small[kind_idx][name]
        if name in _TINY_SHARDED:
            return res_tiny[kind_idx][name]
        pieces = [res_big[kind_idx][key] for key, nm, layer, kind in _BIG if nm == name]
        return jnp.stack(pieces)

    outs = [loss, grad_x[None]]
    for kind_idx in range(4):
        outs += [result(kind_idx, name) for name in _WEIGHTS]
    return tuple(outs)


def kernel(x, c, ada_w, ada_b, norm_mix_w, norm_ffn_w, in_w_even, conv_w, conv_b, dt_bias, a_log, d_skip, ssm_norm_w, gmlp_ln_w, gmlp_ln_b, gmlp_ws, gmlp_bs, out_w_even, qkv_w, qkv_b, o_w, o_b, sinks, rel_table, ffn_gate_w, ffn_up_w, ffn_down_w, final_norm_w, loss_target, m_ada_w, m_ada_b, m_norm_mix_w, m_norm_ffn_w, m_in_w_even, m_conv_w, m_conv_b, m_dt_bias, m_a_log, m_d_skip, m_ssm_norm_w, m_gmlp_ln_w, m_gmlp_ln_b, m_gmlp_ws, m_gmlp_bs, m_out_w_even, m_qkv_w, m_qkv_b, m_o_w, m_o_b, m_sinks, m_rel_table, m_ffn_gate_w, m_ffn_up_w, m_ffn_down_w, m_final_norm_w, v_ada_w, v_ada_b, v_norm_mix_w, v_norm_ffn_w, v_in_w_even, v_conv_w, v_conv_b, v_dt_bias, v_a_log, v_d_skip, v_ssm_norm_w, v_gmlp_ln_w, v_gmlp_ln_b, v_gmlp_ws, v_gmlp_bs, v_out_w_even, v_qkv_w, v_qkv_b, v_o_w, v_o_b, v_sinks, v_rel_table, v_ffn_gate_w, v_ffn_up_w, v_ffn_down_w, v_final_norm_w):
    args = locals()
    W = {n: args[n] for n in _WEIGHTS}
    M = {n: args["m_" + n] for n in _WEIGHTS}
    V = {n: args["v_" + n] for n in _WEIGHTS}
    return _step(x, c, loss_target, W, M, V)
```

```python
import functools
import math

import numpy as np
import jax
import jax.numpy as jnp
from jax import lax
from jax.experimental import pallas as pl
from jax.experimental.pallas import tpu as pltpu

F32 = jnp.float32
BF16 = jnp.bfloat16
HIGHEST = lax.Precision.HIGHEST
MESH = pl.DeviceIdType.MESH

N_DEV = 8
D = 1024
DEPTH = 2
SSM_HEADS = 16
SSM_HEAD_DIM = 64
SSM_INNER = 1024
SSM_GROUPS = 2
SSM_STATE = 128
SSM_CONV = 4
CHUNK = 128
CONV_DIM = SSM_INNER + 2 * SSM_GROUPS * SSM_STATE
GMLP_GROUPS = 8
GMLP_INNER = 1024
IN_EVEN = 4624
ATTN_HEADS = 16
ATTN_KV = 2
ATTN_DH = 64
QKV_DIM = 1280
REL_BUCKETS = 32
REL_MAX_DIST = 128
FFN = 2816
EPS = 1e-6
NEG_INF = -1e30
LANES = 128

ADAM_LR = 0.001
ADAM_B1 = 0.9
ADAM_B2 = 0.999
ADAM_EPS = 1e-08
ADAM_WD = 0.01
ADAM_STEP = 10

VMEM_LIMIT_BYTES = 56 * 1024 * 1024
ROW_TILE = 512


def _pcall(body, *, name, out_shape, grid=(), in_specs=None, out_specs=None, scratch=(), sem=None):
    params = dict(vmem_limit_bytes=VMEM_LIMIT_BYTES)
    if sem is not None:
        params["dimension_semantics"] = sem
    specs = {} if in_specs is None else dict(in_specs=in_specs, out_specs=out_specs)
    return pl.pallas_call(
        body, name=name, out_shape=out_shape, grid=grid, **specs,
        scratch_shapes=list(scratch), compiler_params=pltpu.CompilerParams(**params))


def _call(body, args, side=None, *, name, out_shape, grid, in_specs, out_specs, scratch=(), sem=None):
    if side is None:
        return _pcall(body, name=name, out_shape=out_shape, grid=grid, in_specs=in_specs, out_specs=out_specs,
                      scratch=scratch, sem=sem)(*args)
    ops_cls, x = side
    n_in, n_out, n_scr = len(in_specs), len(out_shape), len(scratch)
    steps = int(np.prod(grid))
    hbm = pl.BlockSpec(memory_space=pl.ANY)

    def wrapped(*refs):
        ins, x_ref = refs[:n_in], refs[n_in]
        outs, r_ref = refs[n_in + 1:n_in + 1 + n_out], refs[n_in + 1 + n_out]
        scr, sems = refs[n_in + 2 + n_out:n_in + 2 + n_out + n_scr], refs[n_in + 2 + n_out + n_scr:]
        ops = ops_cls(x_ref, r_ref, *sems)
        step = pl.program_id(0)
        for axis in range(1, len(grid)):
            step = step * grid[axis] + pl.program_id(axis)
        pl.when(step == 0)(ops.start)
        body(*ins, *outs, *scr)
        pl.when(step == (3 * steps) // 4)(ops.forward)
        pl.when(step == steps - 1)(ops.finish)

    return _pcall(
        wrapped, name=name, out_shape=list(out_shape) + [ops_cls.result(x)], grid=grid,
        in_specs=list(in_specs) + [hbm], out_specs=list(out_specs) + [hbm],
        scratch=list(scratch) + ops_cls.scratch(), sem=("arbitrary",) * len(grid))(*args, x)


def _tile(n, pref):
    if n <= pref:
        return n
    best = None
    for t in range(LANES, pref + 1, LANES):
        if n % t == 0:
            best = t
    assert best is not None, (n, pref)
    return best


def _rows(T):
    return min(ROW_TILE, T)


def _sds(shape, dtype=F32):
    return jax.ShapeDtypeStruct(shape, dtype)


def _row_spec(tm, c, col=0):
    return pl.BlockSpec((tm, c), lambda i, col=col: (i, col))


def _vec_spec(c, r=1):
    return pl.BlockSpec((r, c), lambda i: (0, 0))


def _sigmoid(x):
    return jax.nn.sigmoid(x)


def _silu(x):
    return x * _sigmoid(x)


def _dsilu(x):
    s = _sigmoid(x)
    return s * (1.0 + x * (1.0 - s))


def _gelu(x):
    return 0.5 * x * (1.0 + lax.erf(x * 0.7071067811865476))


def _dgelu(x):
    return 0.5 * (1.0 + lax.erf(x * 0.7071067811865476)) + x * jnp.exp(-0.5 * x * x) * 0.3989422804014327


def _dot(a, b, dims, precision=None):
    return lax.dot_general(a, b, (dims, ((), ())), precision=precision, preferred_element_type=F32)


def _nn(a, b, precision=None):
    return _dot(a, b, ((1,), (0,)), precision)


def _nt(a, b, precision=None):
    return _dot(a, b, ((1,), (1,)), precision)


def _tn(a, b, precision=None):
    return _dot(a, b, ((0,), (0,)), precision)


def _bf(x):
    return x.astype(BF16)


def _colsum(x):
    return jnp.sum(x, axis=0, keepdims=True)


def _rowsum(x):
    return jnp.sum(x, axis=1, keepdims=True)


def _allsum(x):
    return _colsum(_rowsum(x))


def _comm_call(ops_cls, x, name, from_vmem=False):
    def body(x_ref, out_ref, *sems):
        ops = ops_cls(x_ref, out_ref, *sems)
        ops.start()
        ops.forward()
        ops.finish()

    return pl.pallas_call(
        body, name=name, out_shape=ops_cls.result(x),
        in_specs=[pl.BlockSpec(memory_space=pltpu.VMEM if from_vmem else pl.ANY)],
        out_specs=pl.BlockSpec(memory_space=pl.ANY), scratch_shapes=ops_cls.scratch(),
    )(x)


def _all_gather(x, name):
    return _comm_call(_GatherOps, x, name, from_vmem=True)


class _GatherOps:
    def __init__(self, x_ref, out_ref, send_sems, recv_sems, local_sem):
        self.x_ref, self.out_ref = x_ref, out_ref
        self.send_sems, self.recv_sems, self.local_sem = send_sems, recv_sems, local_sem
        mx, my, mc = lax.axis_index("x"), lax.axis_index("y"), lax.axis_index("c")
        self.mc = mc
        self.me, self.sibling = (mx, my, mc), (mx, my, 1 - mc)
        self.chips = [(1 - mx, my), (mx, 1 - my), (1 - mx, 1 - my)]

    @staticmethod
    def result(x):
        return _sds((N_DEV,) + x.shape, x.dtype)

    @staticmethod
    def scratch():
        return [pltpu.SemaphoreType.DMA((7,)), pltpu.SemaphoreType.DMA((7,)), pltpu.SemaphoreType.DMA(())]

    def _slot(self, px, py, pc):
        return self.out_ref.at[4 * px + 2 * py + pc]

    def _copy(self, k, block, to, own=False):
        return pltpu.make_async_remote_copy(
            src_ref=self.x_ref if own else self._slot(*block), dst_ref=self._slot(*block),
            send_sem=self.send_sems.at[k], recv_sem=self.recv_sems.at[k], device_id=to, device_id_type=MESH)

    def _mine(self):
        return pltpu.make_async_copy(self.x_ref, self._slot(*self.me), self.local_sem)

    def _first(self):
        return [self._copy(0, self.me, self.sibling, own=True)] + [
            self._copy(1 + j, self.me, (*chip, self.mc), own=True) for j, chip in enumerate(self.chips)]

    def _passed(self):
        return [self._copy(4 + j, (*chip, self.mc), self.sibling) for j, chip in enumerate(self.chips)]

    def start(self):
        self._mine().start()
        for cp in self._first():
            cp.start()

    def forward(self):
        passed = self._passed()
        for j, chip in enumerate(self.chips):
            self._copy(1 + j, (*chip, self.mc), self.me).wait_recv()
            passed[j].start()

    def finish(self):
        self._copy(0, self.sibling, self.me).wait_recv()
        for j, chip in enumerate(self.chips):
            self._copy(4 + j, (*chip, 1 - self.mc), self.me).wait_recv()
        for cp in self._first() + self._passed():
            cp.wait_send()
        self._mine().wait()


N_CHIP = 4


class _SiblingOps:
    def __init__(self, p_ref, theirs_ref, send_sems, recv_sems):
        self.p_ref, self.theirs_ref, self.send_sems, self.recv_sems = p_ref, theirs_ref, send_sems, recv_sems

    @staticmethod
    def result(p):
        return _sds((N_CHIP,) + p.shape[1:], p.dtype)

    @staticmethod
    def scratch():
        return [pltpu.SemaphoreType.DMA((N_CHIP,))] * 2

    def _copies(self):
        mx, my, mc = lax.axis_index("x"), lax.axis_index("y"), lax.axis_index("c")
        return [pltpu.make_async_remote_copy(
            src_ref=self.p_ref.at[2 * chip + 1 - mc], dst_ref=self.theirs_ref.at[chip],
            send_sem=self.send_sems.at[chip], recv_sem=self.recv_sems.at[chip],
            device_id=(mx, my, 1 - mc), device_id_type=MESH) for chip in range(N_CHIP)]

    def start(self):
        for cp in self._copies():
            cp.start()

    def forward(self):
        pass

    def finish(self):
        for cp in self._copies():
            cp.wait()


class _ChipsOps:
    def __init__(self, q_ref, out_ref, send_sems, recv_sems, local_sem):
        self.q_ref, self.out_ref = q_ref, out_ref
        self.send_sems, self.recv_sems, self.local_sem = send_sems, recv_sems, local_sem

    @staticmethod
    def result(q):
        return _sds(q.shape, q.dtype)

    @staticmethod
    def scratch():
        return [pltpu.SemaphoreType.DMA((N_CHIP - 1,)), pltpu.SemaphoreType.DMA((N_CHIP - 1,)), pltpu.SemaphoreType.DMA(())]

    def _copies(self):
        mx, my, mc = lax.axis_index("x"), lax.axis_index("y"), lax.axis_index("c")
        me = 2 * mx + my
        local = pltpu.make_async_copy(self.q_ref.at[me], self.out_ref.at[me], self.local_sem)
        remote = []
        for r in range(1, N_CHIP):
            px = 1 - mx if r & 2 else mx
            py = 1 - my if r & 1 else my
            remote.append(pltpu.make_async_remote_copy(
                src_ref=self.q_ref.at[2 * px + py], dst_ref=self.out_ref.at[me],
                send_sem=self.send_sems.at[r - 1], recv_sem=self.recv_sems.at[r - 1],
                device_id=(px, py, mc), device_id_type=MESH))
        return local, remote

    def start(self):
        local, remote = self._copies()
        local.start()
        for cp in remote:
            cp.start()

    def forward(self):
        pass

    def finish(self):
        local, remote = self._copies()
        for cp in remote:
            cp.wait()
        local.wait()


def _pair_sum(p, theirs, name):
    n, R, C = theirs.shape
    tr = _tile_rows(R, 256)

    def body(p_ref, t_ref, o_ref):
        mc = lax.axis_index("c")
        o_ref[0] = (p_ref[0, mc] + t_ref[0]).astype(BF16)

    blk = pl.BlockSpec((1, tr, C), lambda s, i: (s, i, 0))
    return _pcall(body, name=name, out_shape=_sds((n, R, C), BF16), grid=(n, R // tr),
                  in_specs=[pl.BlockSpec((1, 2, tr, C), lambda s, i: (s, 0, i, 0)), blk],
                  out_specs=blk, sem=("parallel", "parallel"))(p.reshape(n, 2, R, C), theirs)


def _sum_parts(parts, name):
    P, R, C = parts.shape
    tr = _tile_rows(R, 256)

    def body(p_ref, o_ref):
        g = p_ref[0].astype(F32)
        for k in range(1, P):
            g = g + p_ref[k].astype(F32)
        o_ref[...] = g

    return _pcall(body, name=name, out_shape=_sds((R, C)), grid=(R // tr,),
                  in_specs=[pl.BlockSpec((P, tr, C), lambda i: (0, i, 0))],
                  out_specs=pl.BlockSpec((tr, C), lambda i: (i, 0)), sem=("parallel",))(parts)


def _adamw(parts, w, m, v, name):
    P, R, C = parts.shape
    tr = R if R <= 256 else _tile_rows(R, 256)

    def body(p_ref, w_ref, m_ref, v_ref, g_ref, d_ref, nm_ref, nv_ref):
        g = p_ref[0]
        for k in range(1, P):
            g = g + p_ref[k]
        nm = ADAM_B1 * m_ref[...] + (1.0 - ADAM_B1) * g
        nv = ADAM_B2 * v_ref[...] + (1.0 - ADAM_B2) * (g * g)
        m_hat = nm / (1.0 - ADAM_B1 ** ADAM_STEP)
        v_hat = nv / (1.0 - ADAM_B2 ** ADAM_STEP)
        g_ref[...] = g
        d_ref[...] = -ADAM_LR * (m_hat / (jnp.sqrt(v_hat) + ADAM_EPS) + ADAM_WD * w_ref[...])
        nm_ref[...] = nm
        nv_ref[...] = nv

    blk = pl.BlockSpec((tr, C), lambda i: (i, 0))
    return _pcall(
        body, name=name, out_shape=[_sds((R, C))] * 4, grid=(R // tr,),
        in_specs=[pl.BlockSpec((P, tr, C), lambda i: (0, i, 0)), blk, blk, blk],
        out_specs=[blk] * 4, sem=("parallel",))(parts, w, m, v)


def _adam_update(g, w, m, v):
    nm = ADAM_B1 * m + (1.0 - ADAM_B1) * g
    nv = ADAM_B2 * v + (1.0 - ADAM_B2) * (g * g)
    m_hat = nm / (1.0 - ADAM_B1 ** ADAM_STEP)
    v_hat = nv / (1.0 - ADAM_B2 ** ADAM_STEP)
    return -ADAM_LR * (m_hat / (jnp.sqrt(v_hat) + ADAM_EPS) + ADAM_WD * w), nm, nv


class _Canvas:
    def __init__(self):
        self.offset, self.views, self.rows = {}, {}, 0

    def add(self, key, r, c, blocks=1):
        need = r // blocks if blocks > 1 else r * (-(-c // D))
        if need >= 8:
            self.rows = -(-self.rows // 8) * 8
        self.offset[key], self.views[key] = self.rows, (r, c, blocks)
        self.rows += need

    @property
    def total(self):
        return -(-self.rows // 8) * 8

    def cells(self, key):
        (r, c, blocks), off = self.views[key], self.offset[key]
        if blocks > 1:
            n = r // blocks
            return [(off, n, b * c, c, slice(b * n, (b + 1) * n), slice(0, c)) for b in range(blocks)]
        if c <= D:
            return [(off, r, 0, c, slice(0, r), slice(0, c))]
        chunks = -(-c // D)
        return [(off + i * chunks + j, 1, 0, min(D, c - j * D), slice(i, i + 1), slice(j * D, min(c, (j + 1) * D)))
                for i in range(r) for j in range(chunks)]


def _canvas_fill(canvas, sources, name):
    arrays = [a for a, _ in sources]

    def body(*refs):
        out_ref = refs[-1]
        out_ref[...] = jnp.zeros_like(out_ref)
        for ref, (_, items) in zip(refs[:-1], sources):
            for key, src_rows, view_row in items:
                n = src_rows.stop - src_rows.start
                for row, count, lane, width, vrows, vcols in canvas.cells(key):
                    lo, hi = max(vrows.start, view_row), min(vrows.stop, view_row + n)
                    if lo < hi:
                        src = slice(src_rows.start + lo - view_row, src_rows.start + hi - view_row)
                        out_ref[row + lo - vrows.start:row + hi - vrows.start, lane:lane + width] = ref[src, vcols]

    return _pcall(body, name=name, out_shape=_sds((canvas.total, D)))(*arrays)


def _adamw_canvas(canvas, parts, params, sum_only, name):
    n = len(params)

    def body(p_ref, *refs):
        sum_ref = refs[-1]
        g_all = p_ref[0]
        for k in range(1, N_DEV):
            g_all = g_all + p_ref[k]
        sum_ref[...] = g_all
        for i, (key, _, _, _) in enumerate(params):
            w_ref, m_ref, v_ref = refs[3 * i:3 * i + 3]
            outs = refs[3 * n + 4 * i:3 * n + 4 * i + 4]
            for row, count, lane, width, vrows, vcols in canvas.cells(key):
                g = sum_ref[row:row + count, lane:lane + width]
                delta, nm, nv = _adam_update(g, w_ref[vrows, vcols], m_ref[vrows, vcols], v_ref[vrows, vcols])
                for ref, val in zip(outs, (g, delta, nm, nv)):
                    ref[vrows, vcols] = val
        for i, key in enumerate(sum_only):
            for row, count, lane, width, vrows, vcols in canvas.cells(key):
                refs[7 * n + i][vrows, vcols] = sum_ref[row:row + count, lane:lane + width]

    args = [a for _, w, m, v in params for a in (w, m, v)]
    out_shape = [_sds(w.shape) for _, w, _, _ in params for _ in range(4)] + [_sds(canvas.views[k][:2]) for k in sum_only]
    res = _pcall(body, name=name, out_shape=out_shape, scratch=[pltpu.VMEM(parts.shape[1:], F32)])(parts, *args)
    out = {key: res[4 * i:4 * i + 4] for i, (key, _, _, _) in enumerate(params)}
    out.update({key: [res[4 * n + i]] for i, key in enumerate(sum_only)})
    return out


def _tile_rows(n, pref):
    best = None
    for t in range(8, pref + 1, 8):
        if n % t == 0:
            best = t
    assert best is not None, (n, pref)
    return best


def _mm(pairs, mode, *, name, out_dtype=F32, bias=None, tn_pref=1024, side=None):
    M = pairs[0][0].shape[0]
    N = pairs[0][1].shape[1] if mode == "nn" else pairs[0][1].shape[0]
    tm, tn = _rows(M), _tile(N, tn_pref)
    n_pairs = len(pairs)
    has_bias = bias is not None

    def body(*refs):
        acc = _pairs_dot(refs[:2 * n_pairs], mode)
        if has_bias:
            acc = acc + refs[2 * n_pairs][...]
        refs[-1][...] = acc.astype(refs[-1].dtype)

    in_specs, args = _pair_specs(pairs, mode, tm, tn)
    if has_bias:
        in_specs.append(pl.BlockSpec((1, tn), lambda j, i: (0, j)))
        args.append(bias)
    res = _call(body, args, side, name=name, out_shape=[_sds((M, N), out_dtype)], grid=(N // tn, M // tm),
                in_specs=in_specs, out_specs=[pl.BlockSpec((tm, tn), lambda j, i: (i, j))], sem=("parallel", "parallel"))
    return res[0] if side is None else (res[0], res[1])


def _pairs_dot(ab, mode):
    acc = None
    for p in range(len(ab) // 2):
        a, b = _bf(ab[2 * p][...]), _bf(ab[2 * p + 1][...])
        d = _nn(a, b) if mode == "nn" else _nt(a, b)
        acc = d if acc is None else acc + d
    return acc


def _pair_specs(pairs, mode, tm, tn):
    in_specs, args = [], []
    for a, b in pairs:
        K = a.shape[1]
        in_specs.append(pl.BlockSpec((tm, K), lambda j, i: (i, 0)))
        if mode == "nn":
            in_specs.append(pl.BlockSpec((K, tn), lambda j, i: (0, j)))
        else:
            in_specs.append(pl.BlockSpec((tn, K), lambda j, i: (j, 0)))
        args += [a, b]
    return in_specs, args


def _mm_resid(pairs, resid, gvec, *, name, bias=None, norm=None):
    M, N = resid.shape
    tm = _rows(M)
    n_pairs = len(pairs)
    has_bias, has_norm = bias is not None, norm is not None

    def body(*refs):
        acc = _pairs_dot(refs[:2 * n_pairs], "nn")
        pos = 2 * n_pairs
        if has_bias:
            acc = acc + refs[pos][...]
            pos += 1
        xv = refs[pos][...] + refs[pos + 1][...] * acc
        outs = refs[pos + 2 + 3 * has_norm:]
        outs[0][...] = xv
        outs[1][...] = acc.astype(BF16)
        if has_norm:
            w_ref, sc_ref, sh_ref = refs[pos + 2:pos + 5]
            r = lax.rsqrt(jnp.mean(xv * xv, axis=-1, keepdims=True) + EPS)
            outs[2][...] = ((xv * r * w_ref[...]) * (1.0 + sc_ref[...]) + sh_ref[...]).astype(BF16)

    in_specs, args = _pair_specs(pairs, "nn", tm, N)
    vec = pl.BlockSpec((1, N), lambda j, i: (0, 0))
    row = pl.BlockSpec((tm, N), lambda j, i: (i, 0))
    if has_bias:
        in_specs.append(vec)
        args.append(bias)
    in_specs += [row, vec] + [vec] * (3 * has_norm)
    args += [resid, gvec] + (list(norm) if has_norm else [])
    return _pcall(body, name=name, out_shape=[_sds((M, N)), _sds((M, N), BF16)] + [_sds((M, N), BF16)] * has_norm,
                  grid=(1, M // tm), in_specs=in_specs, out_specs=[row] * (2 + has_norm),
                  sem=("parallel", "parallel"))(*args)


def _mm_tn(a, b, *, name, tm_pref=1408, tn_pref=1536):
    K, M = a.shape
    N = b.shape[1]
    tm, tn = _tile(M, tm_pref), _tile(N, tn_pref)
    tk = K if K <= 2 * ROW_TILE else 2 * ROW_TILE

    def body(a_ref, b_ref, o_ref):
        @pl.when(pl.program_id(2) == 0)
        def _():
            o_ref[...] = jnp.zeros_like(o_ref)
        o_ref[...] += _tn(_bf(a_ref[...]), _bf(b_ref[...]))

    return _pcall(
        body, name=name, out_shape=_sds((M, N)), grid=(M // tm, N // tn, K // tk),
        in_specs=[pl.BlockSpec((tk, tm), lambda i, j, k: (k, i)), pl.BlockSpec((tk, tn), lambda i, j, k: (k, j))],
        out_specs=pl.BlockSpec((tm, tn), lambda i, j, k: (i, j)),
        sem=("parallel", "parallel", "arbitrary"))(a, b)


def _mm_swiglu(h, wg_t, wu_t, name, side=None):
    M, K = h.shape
    N = wg_t.shape[0]
    tm, tn = _rows(M), _tile(N, 1408)

    def body(h_ref, wg_ref, wu_ref, gate_ref, up_ref, act_ref):
        hv = _bf(h_ref[...])
        gate = _nt(hv, wg_ref[...])
        up = _nt(hv, wu_ref[...])
        gate_ref[...] = gate.astype(BF16)
        up_ref[...] = up.astype(BF16)
        act_ref[...] = (_silu(gate) * up).astype(BF16)

    w_spec = pl.BlockSpec((tn, K), lambda j, i: (j, 0))
    o_spec = pl.BlockSpec((tm, tn), lambda j, i: (i, j))
    return _call(body, (h, wg_t, wu_t), side, name=name,
                 out_shape=[_sds((M, N), BF16)] * 3, grid=(N // tn, M // tm),
                 in_specs=[pl.BlockSpec((tm, K), lambda j, i: (i, 0)), w_spec, w_spec], out_specs=[o_spec] * 3,
                 sem=("parallel", "parallel"))


def _mm_swiglu_bwd(dout, wd, gate, up, name, side=None):
    M, K = dout.shape
    N = wd.shape[0]
    tm, tn = _rows(M), _tile(N, 1408)

    def body(d_ref, wd_ref, gate_ref, up_ref, dg_ref, du_ref):
        dact = _nt(_bf(d_ref[...]), wd_ref[...])
        g = gate_ref[...].astype(F32)
        dg_ref[...] = (dact * up_ref[...].astype(F32) * _dsilu(g)).astype(BF16)
        du_ref[...] = (dact * _silu(g)).astype(BF16)

    t_spec = pl.BlockSpec((tm, tn), lambda j, i: (i, j))
    return _call(
        body, (dout, wd, gate, up), side, name=name, out_shape=[_sds((M, N), BF16)] * 2, grid=(N // tn, M // tm),
        in_specs=[pl.BlockSpec((tm, K), lambda j, i: (i, 0)), pl.BlockSpec((tn, K), lambda j, i: (j, 0)), t_spec, t_spec],
        out_specs=[t_spec] * 2, sem=("parallel", "parallel"))


def _norm_mod(x, w, sc, sh, name):
    T = x.shape[0]
    tm = _rows(T)

    def body(x_ref, w_ref, sc_ref, sh_ref, o_ref):
        xv = x_ref[...]
        r = lax.rsqrt(jnp.mean(xv * xv, axis=-1, keepdims=True) + EPS)
        o_ref[...] = ((xv * r * w_ref[...]) * (1.0 + sc_ref[...]) + sh_ref[...]).astype(BF16)

    return _pcall(body, name=name, out_shape=_sds((T, D), BF16), grid=(T // tm,),
                  in_specs=[_row_spec(tm, D), _vec_spec(D), _vec_spec(D), _vec_spec(D)],
                  out_specs=_row_spec(tm, D), sem=("parallel",))(x, w, sc, sh)


def _gate_rows(dxv, br_ref, g_ref, db_ref, acc_ref):
    db = g_ref[...] * dxv
    db_ref[...] = db.astype(BF16)
    acc_ref[3:4, :] += _colsum(dxv * br_ref[...].astype(F32))
    acc_ref[4:5, :] += _colsum(db)


def _norm_mod_bwd(x, dh, dres, w, sc, branch, g, name, side=None):
    T = x.shape[0]
    tm = _rows(T)
    gated = branch is not None

    def body(x_ref, dh_ref, dres_ref, w_ref, sc_ref, *rest):
        (br_ref, g_ref, dx_ref, db_ref, acc_ref) = rest if gated else (None, None, rest[0], None, rest[1])

        @pl.when(pl.program_id(0) == 0)
        def _():
            acc_ref[...] = jnp.zeros_like(acc_ref)
        xv, dh_v, wv = x_ref[...], dh_ref[...], w_ref[...]
        r = lax.rsqrt(jnp.mean(xv * xv, axis=-1, keepdims=True) + EPS)
        n = xv * r
        dnw = dh_v * (1.0 + sc_ref[...])
        dn = dnw * wv
        dxv = dres_ref[...] + r * (dn - n * jnp.mean(dn * n, axis=-1, keepdims=True))
        dx_ref[...] = dxv
        acc_ref[0:1, :] += _colsum(dh_v * (n * wv))
        acc_ref[1:2, :] += _colsum(dh_v)
        acc_ref[2:3, :] += _colsum(dnw * n)
        if gated:
            _gate_rows(dxv, br_ref, g_ref, db_ref, acc_ref)

    row = _row_spec(tm, D)
    args = (x, dh, dres, w, sc) + ((branch, g) if gated else ())
    return _call(body, args, side, name=name,
                 out_shape=[_sds((T, D))] + ([_sds((T, D), BF16)] if gated else []) + [_sds((8, D))], grid=(T // tm,),
                 in_specs=[row, row, row, _vec_spec(D), _vec_spec(D)] + ([row, _vec_spec(D)] if gated else []),
                 out_specs=[row] + ([row] if gated else []) + [_vec_spec(D, 8)], sem=("arbitrary",))


def _final_loss(x, wf, target, branch, g, name):
    T = x.shape[0]
    tm = _rows(T)

    def body(x_ref, w_ref, t_ref, br_ref, g_ref, dx_ref, db_ref, acc_ref):
        @pl.when(pl.program_id(0) == 0)
        def _():
            acc_ref[...] = jnp.zeros_like(acc_ref)
        xv, wv = x_ref[...], w_ref[...]
        r = lax.rsqrt(jnp.mean(xv * xv, axis=-1, keepdims=True) + EPS)
        n = xv * r
        err = n * wv - t_ref[...]
        dy = err * (1.0 / D)
        dn = dy * wv
        dxv = r * (dn - n * jnp.mean(dn * n, axis=-1, keepdims=True))
        dx_ref[...] = dxv
        acc_ref[0:1, :] += _colsum(dy * n)
        acc_ref[1:2, :] += jnp.broadcast_to(_allsum(err * err) * (0.5 / D), (1, D))
        _gate_rows(dxv, br_ref, g_ref, db_ref, acc_ref)

    row = _row_spec(tm, D)
    return _pcall(body, name=name, out_shape=[_sds((T, D)), _sds((T, D), BF16), _sds((8, D))], grid=(T // tm,),
                  in_specs=[row, _vec_spec(D), row, row, _vec_spec(D)],
                  out_specs=[row, row, _vec_spec(D, 8)], sem=("arbitrary",))(x, wf, target, branch, g)


def _mod_matmul(c_all, ada_w, name):
    n = ada_w.shape[2]

    def body(c_ref, w_ref, cond_ref, o_ref):
        cond = _silu(c_ref[...])
        cond_ref[...] = cond
        o_ref[0] = _nn(cond, w_ref[0])

    return _pcall(body, name=name, out_shape=[_sds((N_DEV, D)), _sds((DEPTH, N_DEV, n))], grid=(DEPTH,),
                  in_specs=[pl.BlockSpec((N_DEV, D), lambda l: (0, 0)), pl.BlockSpec((1, D, n), lambda l: (l, 0, 0))],
                  out_specs=[pl.BlockSpec((N_DEV, D), lambda l: (0, 0)), pl.BlockSpec((1, N_DEV, n), lambda l: (l, 0, 0))],
                  sem=("arbitrary",))(c_all, ada_w)


def _add_rows(a, b, name):
    def body(a_ref, b_ref, o_ref):
        o_ref[...] = a_ref[...] + b_ref[...]

    return _pcall(body, name=name, out_shape=_sds(a.shape))(a, b)


def _ada_w_grad(cond, dmod_cols, name):
    n = dmod_cols.shape[2]

    def body(c_ref, d_ref, o_ref):
        o_ref[0] = _tn(c_ref[...], d_ref[0])

    return _pcall(body, name=name, out_shape=_sds((DEPTH, D, n)), grid=(DEPTH,),
                  in_specs=[pl.BlockSpec((N_DEV, D), lambda l: (0, 0)), pl.BlockSpec((1, N_DEV, n), lambda l: (l, 0, 0))],
                  out_specs=pl.BlockSpec((1, D, n), lambda l: (l, 0, 0)), sem=("parallel",))(cond, dmod_cols)


def _conv_fwd(pm, conv_w, conv_b, name):
    T = pm.shape[0]
    tm = _rows(T)
    C = CONV_DIM

    def body(x_ref, prev_ref, w_ref, b_ref, o_ref):
        cur = x_ref[...].astype(F32)
        prev = jnp.where(pl.program_id(0) > 0, prev_ref[...].astype(F32)[8:16], 0.0)
        cur8 = cur[0:8]
        row8 = lax.broadcasted_iota(jnp.int32, (8, C), 0)
        full = w_ref[3:4, :] * cur
        head = w_ref[3:4, :] * cur8
        for k in range(1, SSM_CONV):
            wk = w_ref[3 - k:4 - k, :]
            full = full + wk * pltpu.roll(cur, k, 0)
            head = head + wk * jnp.where(row8 < k, pltpu.roll(prev, k, 0), pltpu.roll(cur8, k, 0))
        o_ref[...] = full + b_ref[...]
        o_ref[0:8, :] = head + b_ref[...]

    return _pcall(
        body, name=name, out_shape=_sds((T, C)), grid=(T // tm,),
        in_specs=[pl.BlockSpec((tm, C), lambda i: (i, 2)),
                  pl.BlockSpec((16, C), lambda i: (jnp.maximum(i * (tm // 16) - 1, 0), 2)),
                  _vec_spec(C, SSM_CONV), _vec_spec(C)],
        out_specs=_row_spec(tm, C), sem=("parallel",))(pm, pm, conv_w, conv_b)


def _conv_bwd(dc, pm, conv_w, name):
    T = dc.shape[0]
    tm = _rows(T)
    C = CONV_DIM
    nt = T // tm

    def body(dc_ref, nxt_ref, x_ref, prev_ref, w_ref, dx_ref, acc_ref):
        i = pl.program_id(0)

        @pl.when(i == 0)
        def _():
            acc_ref[...] = jnp.zeros_like(acc_ref)
        dcv = dc_ref[...]
        nxt = jnp.where(i < nt - 1, nxt_ref[...], 0.0)
        xc = x_ref[...].astype(F32)
        prev = jnp.where(i > 0, prev_ref[...].astype(F32)[8:16], 0.0)
        dc8h, dc8t, x8 = dcv[0:8], dcv[tm - 8:tm], xc[0:8]
        row8 = lax.broadcasted_iota(jnp.int32, (8, C), 0)
        full = w_ref[3:4, :] * dcv
        tail = w_ref[3:4, :] * dc8t
        acc_ref[3:4, :] += _colsum(dcv * xc)
        for k in range(1, SSM_CONV):
            wk = w_ref[3 - k:4 - k, :]
            full = full + wk * pltpu.roll(dcv, tm - k, 0)
            tail = tail + wk * jnp.where(row8 + k >= 8, pltpu.roll(nxt, 8 - k, 0), pltpu.roll(dc8t, 8 - k, 0))
            xs_head = jnp.where(row8 < k, pltpu.roll(prev, k, 0), pltpu.roll(x8, k, 0))
            prod = dcv * pltpu.roll(xc, k, 0)
            acc_ref[3 - k:4 - k, :] += _colsum(prod) - _colsum(prod[0:8]) + _colsum(dc8h * xs_head)
        acc_ref[4:5, :] += _colsum(dcv)
        dx_ref[...] = jnp.concatenate([full[0:tm - 8], tail], axis=0).astype(BF16)

    return _pcall(
        body, name=name, out_shape=[_sds((T, C), BF16), _sds((8, C))], grid=(nt,),
        in_specs=[_row_spec(tm, C),
                  pl.BlockSpec((8, C), lambda i: (jnp.minimum((i + 1) * (tm // 8), T // 8 - 1), 0)),
                  pl.BlockSpec((tm, C), lambda i: (i, 2)),
                  pl.BlockSpec((16, C), lambda i: (jnp.maximum(i * (tm // 16) - 1, 0), 2)),
                  _vec_spec(C, SSM_CONV)],
        out_specs=[_row_spec(tm, C), _vec_spec(C, 8)], sem=("arbitrary",))(dc, dc, pm, pm, conv_w)


def _ssd_prologue(cpre, dtr, dtb, alog):
    L = CHUNK
    xc = _silu(cpre)
    pre = dtr + dtb
    dt = jnp.maximum(pre, 0.0) + jnp.log1p(jnp.exp(-jnp.abs(pre)))
    a = -jnp.exp(alog)
    la = dt * a
    row = lax.broadcasted_iota(jnp.int32, (L, L), 0)
    col = lax.broadcasted_iota(jnp.int32, (L, L), 1)
    causal = row >= col
    tri = causal.astype(F32)
    lc = _nn(tri, la, HIGHEST)
    return xc, pre, dt, a, causal, tri, lc, row, col


def _head_indicator():
    m = np.zeros((LANES, SSM_INNER), np.float32)
    for h in range(SSM_HEADS):
        m[h, h * SSM_HEAD_DIM:(h + 1) * SSM_HEAD_DIM] = 1.0
    return jnp.asarray(m, dtype=BF16)


def _split_dot(x, ind, dims):
    hi = x.astype(BF16)
    lo = (x - hi.astype(F32)).astype(BF16)
    return _dot(hi, ind, dims) + _dot(lo, ind, dims)


def _expand(x16, ind):
    return _split_dot(x16, ind, ((1,), (0,)))


def _headsum(x, ind, single_pass=False):
    if single_pass:
        return _dot(x.astype(BF16), ind, ((1,), (1,)))
    return _split_dot(x, ind, ((1,), (1,)))


def _ssd_fwd(cpre, dtr, pm, dtb, alog, dskip, normw, ind, name, side=None):
    T = cpre.shape[0]
    nc = T // CHUNK
    L, P, H, HPG, N = CHUNK, SSM_HEAD_DIM, SSM_HEADS, SSM_HEADS // SSM_GROUPS, SSM_STATE
    half = SSM_INNER // SSM_GROUPS

    def body(cp_ref, dtr_ref, z_ref, dtb_ref, alog_ref, dskip_ref, nw_ref, ind_ref, ya_ref, y_ref, sp_ref, st_ref):
        @pl.when(pl.program_id(0) == 0)
        def _():
            st_ref[...] = jnp.zeros_like(st_ref)
        xc, _, dt, _, causal, _, lc, _, _ = _ssd_prologue(cp_ref[...], dtr_ref[...], dtb_ref[...], alog_ref[...])
        lct = lc.T
        ind = ind_ref[...]
        llast = lc[L - 1:L, :]
        xs = xc[:, :SSM_INNER]
        xd = xs * _expand(dt, ind)
        ex = _expand(jnp.exp(lc), ind)
        xd_end = _bf(xd * _expand(jnp.exp(llast - lc), ind))
        cdx = _expand(jnp.broadcast_to(jnp.exp(llast), (8, LANES)), ind)[0:1]
        xdb = _bf(xd)
        sp_ref[0] = st_ref[...]
        for g in range(SSM_GROUPS):
            sl = slice(g * half, (g + 1) * half)
            bm = _bf(xc[:, SSM_INNER + g * N:SSM_INNER + (g + 1) * N])
            cm = _bf(xc[:, SSM_INNER + (SSM_GROUPS + g) * N:SSM_INNER + (SSM_GROUPS + g + 1) * N])
            cb = _nt(cm, bm)
            st = st_ref[g]
            y_ref[:, sl] = ex[:, sl] * _nn(cm, _bf(st)) + dskip_ref[:, sl] * xs[:, sl]
            st_ref[g] = st * cdx[:, sl] + _tn(bm, xd_end[:, sl])
            for j in range(HPG):
                h = g * HPG + j
                decay = jnp.where(causal, jnp.exp(jnp.where(causal, lc[:, h:h + 1] - lct[h:h + 1, :], 0.0)), 0.0)
                y_ref[:, h * P:(h + 1) * P] += _nn(_bf(cb * decay), xdb[:, h * P:(h + 1) * P])
        y2 = y_ref[...] * _silu(z_ref[...].astype(F32))
        for g in range(SSM_GROUPS):
            yg = y2[:, g * half:(g + 1) * half]
            r = lax.rsqrt(jnp.mean(yg * yg, axis=-1, keepdims=True) + EPS)
            ya_ref[:, g * half:(g + 1) * half] = (yg * r * nw_ref[:, g * half:(g + 1) * half]).astype(BF16)

    return _call(
        body, (cpre, dtr, pm, dtb, alog, dskip, normw, ind), side, name=name,
        out_shape=[_sds((T, SSM_INNER), BF16), _sds((T, SSM_INNER)), _sds((nc, SSM_GROUPS, N, half))], grid=(nc,),
        in_specs=[_row_spec(L, CONV_DIM), _row_spec(L, LANES), _row_spec(L, SSM_INNER, 0),
                  _vec_spec(LANES), _vec_spec(LANES), _vec_spec(SSM_INNER), _vec_spec(SSM_INNER), _vec_spec(SSM_INNER, LANES)],
        out_specs=[_row_spec(L, SSM_INNER), _row_spec(L, SSM_INNER),
                   pl.BlockSpec((1, SSM_GROUPS, N, half), lambda i: (i, 0, 0, 0))],
        scratch=[pltpu.VMEM((SSM_GROUPS, N, half), F32)], sem=("arbitrary",))


def _ssd_bwd(cpre, dtr, pm, ypre, sprev, dya, dtb, alog, dskip, normw, ind, name, side=None):
    T = cpre.shape[0]
    nc = T // CHUNK
    L, P, H, HPG, N = CHUNK, SSM_HEAD_DIM, SSM_HEADS, SSM_HEADS // SSM_GROUPS, SSM_STATE
    half = SSM_INNER // SSM_GROUPS

    def body(cp_ref, dtr_ref, z_ref, y_ref, sp_ref, dya_ref, dtb_ref, alog_ref, dskip_ref, nw_ref, ind_ref,
             dz_ref, dcp_ref, ddtr_ref, acc_ref, dnw_ref, ds_ref, dy_ref, dxd_ref, rr_ref, yoff_ref, dcd_ref):
        @pl.when(pl.program_id(0) == 0)
        def _():
            ds_ref[...] = jnp.zeros_like(ds_ref)
            acc_ref[...] = jnp.zeros_like(acc_ref)
            dnw_ref[...] = jnp.zeros_like(dnw_ref)
        cpre_v = cp_ref[...]
        xc, pre, dt, a, causal, tri, lc, row, col = _ssd_prologue(cpre_v, dtr_ref[...], dtb_ref[...], alog_ref[...])
        lct = lc.T
        zv, yv = z_ref[...].astype(F32), y_ref[...]
        sz = _silu(zv)
        y2 = yv * sz
        dya_v = dya_ref[...]
        nwv = nw_ref[...]
        for g in range(SSM_GROUPS):
            sl = slice(g * half, (g + 1) * half)
            yg = y2[:, sl]
            r = lax.rsqrt(jnp.mean(yg * yg, axis=-1, keepdims=True) + EPS)
            nrm = yg * r
            dnw_ref[:, sl] += _colsum(dya_v[:, sl] * nrm)
            dn = dya_v[:, sl] * nwv[:, sl]
            dy2 = r * (dn - nrm * jnp.mean(dn * nrm, axis=-1, keepdims=True))
            dy_ref[:, sl] = dy2 * sz[:, sl]
            dz_ref[:, sl] = (dy2 * yv[:, sl] * _dsilu(zv[:, sl])).astype(BF16)
        ind = ind_ref[...]
        llast = lc[L - 1:L, :]
        dte16 = jnp.exp(llast - lc)
        cd16 = jnp.exp(llast)
        xs = xc[:, :SSM_INNER]
        dtx = _expand(dt, ind)
        ex = _expand(jnp.exp(lc), ind)
        dtex = _expand(dte16, ind)
        cdx = _expand(jnp.broadcast_to(cd16, (8, LANES)), ind)[0:1]
        xd = xs * dtx
        xdb = _bf(xd)
        xd_end = _bf(xd * dtex)
        dyv = dy_ref[...]
        dy_off = _bf(ex * dyv)
        dyb = _bf(dyv)
        dskx = dskip_ref[...]
        lane_c = lax.broadcasted_iota(jnp.int32, (L, LANES), 1)
        lane1 = lax.broadcasted_iota(jnp.int32, (1, LANES), 1)
        sub16 = lax.broadcasted_iota(jnp.int32, (H, L), 0)
        dlc_c = jnp.zeros((L, LANES), F32)
        dlc_r = jnp.zeros((H, L), F32)
        for g in range(SSM_GROUPS):
            sl = slice(g * half, (g + 1) * half)
            b_lo = SSM_INNER + g * N
            c_lo = SSM_INNER + (SSM_GROUPS + g) * N
            bm, cm = _bf(xc[:, b_lo:b_lo + N]), _bf(xc[:, c_lo:c_lo + N])
            cb = _nt(cm, bm)
            st, dst = sp_ref[0, g], ds_ref[g]
            stb, dstb = _bf(st), _bf(dst)
            dcm = _nt(dy_off[:, sl], stb)
            ds_ref[g] = _tn(cm, dy_off[:, sl]) + dst * cdx[:, sl]
            rr_ref[:, sl] = _nn(bm, dstb)
            yoff_ref[:, sl] = ex[:, sl] * _nn(cm, stb)
            db = _nt(xd_end[:, sl], dstb)
            dcd_ref[:, sl] = _colsum(dst * st)
            dcb = jnp.zeros((L, L), F32)
            for j in range(HPG):
                h = g * HPG + j
                hs = slice(h * P, (h + 1) * P)
                decay = jnp.where(causal, jnp.exp(jnp.where(causal, lc[:, h:h + 1] - lct[h:h + 1, :], 0.0)), 0.0)
                m = cb * decay
                dxd_ref[:, hs] = _tn(_bf(m), dyb[:, hs])
                dm = _nt(dyb[:, hs], xdb[:, hs])
                dcb = dcb + dm * decay
                gm = dm * m
                dlc_c = dlc_c + jnp.where(lane_c == h, _rowsum(gm), 0.0)
                dlc_r = dlc_r + jnp.where(sub16 == h, _colsum(gm), 0.0)
            dcbb = _bf(dcb)
            dcp_ref[:, c_lo:c_lo + N] = dcm + _nn(dcbb, bm)
            dcp_ref[:, b_lo:b_lo + N] = db + _tn(dcbb, cm)
        dxd_diag, rr = dxd_ref[...], rr_ref[...]
        tt = _headsum(rr * xd, ind, single_pass=True) * dte16
        dlc_rt = jnp.concatenate([dlc_r, jnp.zeros((LANES - H, L), F32)], axis=0).T
        dlc = dlc_c - dlc_rt + _headsum(dyv * yoff_ref[...], ind, single_pass=True) - tt
        dcd = _headsum(jnp.broadcast_to(dcd_ref[...], (8, SSM_INNER)), ind)[0:1]
        dlc = dlc + jnp.where(row == L - 1, _colsum(tt) + dcd * cd16, 0.0)
        dla = _tn(tri, dlc, HIGHEST)
        dxd = dxd_diag + dtex * rr
        ddt = _headsum(dxd * xs, ind, single_pass=True) + dla * a
        ddtr = jnp.where(lane_c < H, ddt * _sigmoid(pre), 0.0)
        ddtr_ref[...] = ddtr
        acc_ref[0:1, :] += _colsum(ddtr)
        acc_ref[1:2, :] += jnp.where(lane1 < H, _colsum(dla * dt) * a, 0.0)
        acc_ref[2:3, :] += _headsum(jnp.broadcast_to(_colsum(dyv * xs), (8, SSM_INNER)), ind)[0:1]
        dcp_ref[:, 0:SSM_INNER] = dxd * dtx + dskx * dyv
        dcp_ref[...] = dcp_ref[...] * _dsilu(cpre_v)

    rev = lambda i: (nc - 1 - i, 0)
    rspec = lambda c: pl.BlockSpec((L, c), rev)
    return _call(
        body, (cpre, dtr, pm, ypre, sprev, dya, dtb, alog, dskip, normw, ind), side, name=name,
        out_shape=[_sds((T, SSM_INNER), BF16), _sds((T, CONV_DIM)), _sds((T, LANES)), _sds((8, LANES)), _sds((1, SSM_INNER))],
        grid=(nc,),
        in_specs=[rspec(CONV_DIM), rspec(LANES), rspec(SSM_INNER), rspec(SSM_INNER),
                  pl.BlockSpec((1, SSM_GROUPS, N, half), lambda i: (nc - 1 - i, 0, 0, 0)), rspec(SSM_INNER),
                  _vec_spec(LANES), _vec_spec(LANES), _vec_spec(SSM_INNER), _vec_spec(SSM_INNER), _vec_spec(SSM_INNER, LANES)],
        out_specs=[rspec(SSM_INNER), rspec(CONV_DIM), rspec(LANES), _vec_spec(LANES, 8), _vec_spec(SSM_INNER)],
        scratch=[pltpu.VMEM((SSM_GROUPS, N, half), F32), pltpu.VMEM((L, SSM_INNER), F32), pltpu.VMEM((L, SSM_INNER), F32),
                 pltpu.VMEM((L, SSM_INNER), F32), pltpu.VMEM((L, SSM_INNER), F32), pltpu.VMEM((1, SSM_INNER), F32)],
        sem=("arbitrary",))


def _gmlp_common(u, v, lnw, lnb):
    ug = _gelu(u)
    vg = _gelu(v)
    mu = jnp.mean(vg, axis=-1, keepdims=True)
    cen = vg - mu
    rstd = lax.rsqrt(jnp.mean(cen * cen, axis=-1, keepdims=True) + EPS)
    vhat = cen * rstd
    return ug, rstd, vhat, vhat * lnw + lnb


def _causal_mask():
    row = lax.broadcasted_iota(jnp.int32, (CHUNK, CHUNK), 0)
    col = lax.broadcasted_iota(jnp.int32, (CHUNK, CHUNK), 1)
    return row >= col


def _gmlp_fwd(pm, lnw, lnb, ws, bs_exp, name, side=None):
    T = pm.shape[0]
    nc = T // CHUNK
    L, G = CHUNK, GMLP_GROUPS

    def body(u_ref, v_ref, lnw_ref, lnb_ref, ws_ref, bs_ref, o_ref):
        ug, _, _, vn = _gmlp_common(u_ref[...].astype(F32), v_ref[...].astype(F32), lnw_ref[...], lnb_ref[...])
        causal = _causal_mask()
        for g in range(G):
            sl = slice(g * L, (g + 1) * L)
            wm = _bf(jnp.where(causal, ws_ref[g], 0.0))
            sv = _nn(wm, _bf(vn[:, sl])) + bs_ref[:, sl]
            o_ref[:, sl] = (ug[:, sl] * sv).astype(BF16)

    return _call(
        body, (pm, pm, lnw, lnb, ws, bs_exp), side, name=name, out_shape=[_sds((T, GMLP_INNER), BF16)], grid=(nc,),
        in_specs=[_row_spec(L, GMLP_INNER, 1), _row_spec(L, GMLP_INNER, 2), _vec_spec(GMLP_INNER), _vec_spec(GMLP_INNER),
                  pl.BlockSpec((G, L, L), lambda i: (0, 0, 0)), _vec_spec(GMLP_INNER, L)],
        out_specs=[_row_spec(L, GMLP_INNER)], sem=("parallel",))


def _gmlp_bwd(pm, dyb, lnw, lnb, ws, bs_exp, name, side=None):
    T = pm.shape[0]
    nc = T // CHUNK
    L, G = CHUNK, GMLP_GROUPS

    def body(u_ref, v_ref, dy_ref, lnw_ref, lnb_ref, ws_ref, bs_ref, du_ref, dv_ref, dws_ref, dbs_ref, acc_ref, dvn_ref):
        @pl.when(pl.program_id(0) == 0)
        def _():
            dws_ref[...] = jnp.zeros_like(dws_ref)
            dbs_ref[...] = jnp.zeros_like(dbs_ref)
            acc_ref[...] = jnp.zeros_like(acc_ref)
        uv, vv, dyv, lnwv = u_ref[...].astype(F32), v_ref[...].astype(F32), dy_ref[...], lnw_ref[...]
        ug, rstd, vhat, vn = _gmlp_common(uv, vv, lnwv, lnb_ref[...])
        causal = _causal_mask()
        lane = lax.broadcasted_iota(jnp.int32, (L, LANES), 1)
        dbs = jnp.zeros((L, LANES), F32)
        for g in range(G):
            sl = slice(g * L, (g + 1) * L)
            wm = _bf(jnp.where(causal, ws_ref[g], 0.0))
            vng = _bf(vn[:, sl])
            sv = _nn(wm, vng) + bs_ref[:, sl]
            du_ref[:, sl] = (dyv[:, sl] * sv * _dgelu(uv[:, sl])).astype(BF16)
            dsv = dyv[:, sl] * ug[:, sl]
            dsvb = _bf(dsv)
            dws_ref[g] += jnp.where(causal, _nt(dsvb, vng), 0.0)
            dbs = dbs + jnp.where(lane == g, _rowsum(dsv), 0.0)
            dvn_ref[:, sl] = _tn(wm, dsvb)
        dbs_ref[...] += dbs
        dvn = dvn_ref[...]
        acc_ref[0:1, :] += _colsum(dvn * vhat)
        acc_ref[1:2, :] += _colsum(dvn)
        dvh = dvn * lnwv
        dvg = rstd * (dvh - jnp.mean(dvh, axis=-1, keepdims=True) - vhat * jnp.mean(dvh * vhat, axis=-1, keepdims=True))
        dv_ref[...] = (dvg * _dgelu(vv)).astype(BF16)

    return _call(
        body, (pm, pm, dyb, lnw, lnb, ws, bs_exp), side, name=name,
        out_shape=[_sds((T, GMLP_INNER), BF16), _sds((T, GMLP_INNER), BF16), _sds((G, L, L)), _sds((L, LANES)), _sds((8, GMLP_INNER))],
        grid=(nc,),
        in_specs=[_row_spec(L, GMLP_INNER, 1), _row_spec(L, GMLP_INNER, 2), _row_spec(L, GMLP_INNER),
                  _vec_spec(GMLP_INNER), _vec_spec(GMLP_INNER), pl.BlockSpec((G, L, L), lambda i: (0, 0, 0)),
                  _vec_spec(GMLP_INNER, L)],
        out_specs=[_row_spec(L, GMLP_INNER), _row_spec(L, GMLP_INNER), pl.BlockSpec((G, L, L), lambda i: (0, 0, 0)),
                   _vec_spec(LANES, L), _vec_spec(GMLP_INNER, 8)],
        scratch=[pltpu.VMEM((L, GMLP_INNER), F32)], sem=("arbitrary",))


def _rel_buckets():
    qi = np.arange(CHUNK)[:, None]
    sj = np.arange(2 * CHUNK)[None, :]
    dist = np.maximum(qi + CHUNK - sj, 0)
    max_exact = REL_BUCKETS // 2
    log_ratio = (np.log(np.maximum(dist, 1).astype(np.float32) / np.float32(max_exact))
                 / np.float32(math.log(REL_MAX_DIST / max_exact))).astype(np.float32)
    large = max_exact + (log_ratio * np.float32(REL_BUCKETS - max_exact)).astype(np.int32)
    return np.where(dist < max_exact, dist, np.minimum(large, REL_BUCKETS - 1))


def _bucket_onehot_t():
    bucket = _rel_buckets().reshape(-1)
    return jnp.asarray((np.arange(REL_BUCKETS)[:, None] == bucket[None, :]).astype(np.float32))


def _bias_from_table(table_t, onehot_t, name):
    def body(t_ref, o_ref, out_ref):
        out_ref[...] = _nn(t_ref[...], o_ref[...], HIGHEST)

    return _pcall(body, name=name, out_shape=_sds((ATTN_HEADS, onehot_t.shape[1])))(table_t, onehot_t)


def _table_from_dbias(dbias, onehot_t, name):
    def body(d_ref, o_ref, out_ref):
        out_ref[...] = _nt(o_ref[...], d_ref[...], HIGHEST)

    return _pcall(body, name=name, out_shape=_sds((REL_BUCKETS, ATTN_HEADS)))(dbias, onehot_t)


def _softmax_sink(logits, sink, mask):
    logits = jnp.where(mask, logits, NEG_INF)
    mx = jnp.maximum(jnp.max(logits, axis=-1, keepdims=True), sink)
    e = jnp.exp(logits - mx)
    es = jnp.exp(sink - mx)
    inv = 1.0 / (_rowsum(e) + es)
    return e * inv, es * inv


def _attn_mask(n, heads):
    qi = lax.broadcasted_iota(jnp.int32, (heads * CHUNK, 2 * CHUNK), 0) & (CHUNK - 1)
    sj = lax.broadcasted_iota(jnp.int32, (heads * CHUNK, 2 * CHUNK), 1)
    rel = qi + CHUNK - sj
    return (rel >= 0) & (rel < CHUNK) & ((sj >= CHUNK) | (n > 0))


def _stack_heads(ref, first, count, width):
    return jnp.concatenate([_bf(ref[:, (first + j) * width:(first + j + 1) * width]) for j in range(count)], axis=0)


def _attn_fwd(qkv, bias, sinks, name):
    T = qkv.shape[0]
    nb = T // CHUNK
    L, DH, HPK = CHUNK, ATTN_DH, ATTN_HEADS // ATTN_KV
    scale = DH ** -0.5
    kcol, vcol = ATTN_HEADS * DH // LANES, ATTN_HEADS * DH // LANES + 1

    def body(q_ref, k_ref, v_ref, kp_ref, vp_ref, bias_ref, sink_ref, o_ref, lg_ref, p_ref):
        n = pl.program_id(0)
        mask = _attn_mask(n, 1)
        kband = _bf(jnp.concatenate([kp_ref[...], k_ref[...]], axis=0))
        vband = _bf(jnp.concatenate([vp_ref[...], v_ref[...]], axis=0))
        for kv in range(ATTN_KV):
            lg_ref[...] = _nt(_stack_heads(q_ref, kv * HPK, HPK, DH), kband[:, kv * DH:(kv + 1) * DH])
            for j in range(HPK):
                h = kv * HPK + j
                p, _ = _softmax_sink(lg_ref[j * L:(j + 1) * L, :] * scale + bias_ref[h], sink_ref[h], mask)
                p_ref[j * L:(j + 1) * L, :] = _bf(p)
            og = _nn(p_ref[...], vband[:, kv * DH:(kv + 1) * DH])
            for j in range(HPK):
                h = kv * HPK + j
                o_ref[:, h * DH:(h + 1) * DH] = og[j * L:(j + 1) * L].astype(BF16)

    prev = lambda i: jnp.maximum(i - 1, 0)
    return _pcall(
        body, name=name, out_shape=_sds((T, ATTN_HEADS * DH), BF16), grid=(nb,),
        in_specs=[_row_spec(L, ATTN_HEADS * DH, 0), _row_spec(L, LANES, kcol), _row_spec(L, LANES, vcol),
                  pl.BlockSpec((L, LANES), lambda i: (prev(i), kcol)), pl.BlockSpec((L, LANES), lambda i: (prev(i), vcol)),
                  pl.BlockSpec((ATTN_HEADS, L, 2 * L), lambda i: (0, 0, 0)),
                  pl.BlockSpec(memory_space=pltpu.SMEM)],
        out_specs=_row_spec(L, ATTN_HEADS * DH),
        scratch=[pltpu.VMEM((HPK * L, 2 * L), F32), pltpu.VMEM((HPK * L, 2 * L), BF16)],
        sem=("parallel",))(qkv, qkv, qkv, qkv, qkv, bias, sinks)


def _attn_bwd(qkv, datt, bias, sinks, name):
    T = qkv.shape[0]
    nb = T // CHUNK
    L, DH, HPK = CHUNK, ATTN_DH, ATTN_HEADS // ATTN_KV
    scale = DH ** -0.5
    kcol, vcol = ATTN_HEADS * DH // LANES, ATTN_HEADS * DH // LANES + 1

    def body(q_ref, k_ref, v_ref, kp_ref, vp_ref, do_ref, bias_ref, sink_ref,
             dq_ref, dk_ref, dv_ref, bsum_ref, dbias_ref, dsink_ref, pend_k, pend_v, band_k, band_v, lg_ref, dp_ref, p_ref, dl_ref):
        n = pl.program_id(0)

        @pl.when(n == 0)
        def _():
            dbias_ref[...] = jnp.zeros_like(dbias_ref)
            dsink_ref[...] = jnp.zeros_like(dsink_ref)
            bsum_ref[...] = jnp.zeros_like(bsum_ref)

        def emit_kv(dk, dv):
            dk_ref[...] = dk.astype(BF16)
            dv_ref[...] = dv.astype(BF16)
            bsum_ref[:, ATTN_HEADS * DH:ATTN_HEADS * DH + LANES] += _colsum(dk)
            bsum_ref[:, ATTN_HEADS * DH + LANES:] += _colsum(dv)

        @pl.when(n < nb)
        def _():
            mask = _attn_mask(n, 1)
            kband = _bf(jnp.concatenate([kp_ref[...], k_ref[...]], axis=0))
            vband = _bf(jnp.concatenate([vp_ref[...], v_ref[...]], axis=0))
            lane1 = lax.broadcasted_iota(jnp.int32, (1, LANES), 1)
            dsink = jnp.zeros((1, LANES), F32)
            for kv in range(ATTN_KV):
                kb, vb = kband[:, kv * DH:(kv + 1) * DH], vband[:, kv * DH:(kv + 1) * DH]
                qg = _stack_heads(q_ref, kv * HPK, HPK, DH)
                dog = _stack_heads(do_ref, kv * HPK, HPK, DH)
                lg_ref[...] = _nt(qg, kb)
                dp_ref[...] = _nt(dog, vb)
                for j in range(HPK):
                    h = kv * HPK + j
                    rows = slice(j * L, (j + 1) * L)
                    p, ps = _softmax_sink(lg_ref[rows, :] * scale + bias_ref[h], sink_ref[h], mask)
                    dp = dp_ref[rows, :]
                    delta = _rowsum(p * dp)
                    dl = p * (dp - delta)
                    dbias_ref[h] += dl
                    p_ref[rows, :] = _bf(p)
                    dl_ref[rows, :] = _bf(dl)
                    dsink = dsink + jnp.where(lane1 == h, -_colsum(ps * delta), 0.0)
                band_v[:, kv * DH:(kv + 1) * DH] = _tn(p_ref[...], dog)
                dqg = _nn(dl_ref[...], kb) * scale
                band_k[:, kv * DH:(kv + 1) * DH] = _tn(dl_ref[...], qg) * scale
                for j in range(HPK):
                    h = kv * HPK + j
                    dq_ref[:, h * DH:(h + 1) * DH] = dqg[j * L:(j + 1) * L].astype(BF16)
                    bsum_ref[:, h * DH:(h + 1) * DH] += _colsum(dqg[j * L:(j + 1) * L])
            dsink_ref[...] += dsink

            @pl.when(n > 0)
            def _():
                emit_kv(pend_k[...] + band_k[0:L, :], pend_v[...] + band_v[0:L, :])
            pend_k[...] = band_k[L:2 * L, :]
            pend_v[...] = band_v[L:2 * L, :]

        @pl.when(n == nb)
        def _():
            emit_kv(pend_k[...], pend_v[...])

    cur = lambda i: jnp.minimum(i, nb - 1)
    prev = lambda i: jnp.maximum(jnp.minimum(i, nb - 1) - 1, 0)
    lag = lambda i: jnp.maximum(i - 1, 0)
    return _pcall(
        body, name=name,
        out_shape=[_sds((T, ATTN_HEADS * DH), BF16), _sds((T, LANES), BF16), _sds((T, LANES), BF16), _sds((1, QKV_DIM)),
                   _sds((ATTN_HEADS, L, 2 * L)), _sds((1, LANES))],
        grid=(nb + 1,),
        in_specs=[pl.BlockSpec((L, ATTN_HEADS * DH), lambda i: (cur(i), 0)),
                  pl.BlockSpec((L, LANES), lambda i: (cur(i), kcol)), pl.BlockSpec((L, LANES), lambda i: (cur(i), vcol)),
                  pl.BlockSpec((L, LANES), lambda i: (prev(i), kcol)), pl.BlockSpec((L, LANES), lambda i: (prev(i), vcol)),
                  pl.BlockSpec((L, ATTN_HEADS * DH), lambda i: (cur(i), 0)),
                  pl.BlockSpec((ATTN_HEADS, L, 2 * L), lambda i: (0, 0, 0)),
                  pl.BlockSpec(memory_space=pltpu.SMEM)],
        out_specs=[pl.BlockSpec((L, ATTN_HEADS * DH), lambda i: (cur(i), 0)),
                   pl.BlockSpec((L, LANES), lambda i: (lag(i), 0)), pl.BlockSpec((L, LANES), lambda i: (lag(i), 0)),
                   _vec_spec(QKV_DIM), pl.BlockSpec((ATTN_HEADS, L, 2 * L), lambda i: (0, 0, 0)), _vec_spec(LANES)],
        scratch=[pltpu.VMEM((L, LANES), F32), pltpu.VMEM((L, LANES), F32),
                 pltpu.VMEM((2 * L, LANES), F32), pltpu.VMEM((2 * L, LANES), F32),
                 pltpu.VMEM((HPK * L, 2 * L), F32), pltpu.VMEM((HPK * L, 2 * L), F32),
                 pltpu.VMEM((HPK * L, 2 * L), BF16), pltpu.VMEM((HPK * L, 2 * L), BF16)],
        sem=("arbitrary",))(qkv, qkv, qkv, qkv, qkv, datt, bias, sinks)


def _pad_rows(a, mult):
    pad = (-a.shape[-2]) % mult
    if pad == 0:
        return a
    cfg = [(0, 0)] * (a.ndim - 2) + [(0, pad), (0, 0)]
    return jnp.pad(a, cfg)


class _Pack:
    def __init__(self, width, mult, total_mult):
        self.width, self.mult, self.total_mult = width, mult, total_mult
        self.entries = []
        self.rows = 0

    def add(self, key, shape):
        n = int(np.prod(shape))
        rows = -(-n // self.width)
        self.entries.append((key, self.rows, rows, tuple(shape)))
        self.rows += -(-rows // self.mult) * self.mult

    @property
    def total(self):
        return -(-self.rows // self.total_mult) * self.total_mult

    def pack(self, pieces, dtype, lead=()):
        parts = []
        for key, _, rows, shape in self.entries:
            a = pieces[key].astype(dtype).reshape(lead + (-1,))
            n = int(np.prod(shape))
            a = jnp.pad(a, [(0, 0)] * len(lead) + [(0, rows * self.width - n)])
            a = a.reshape(lead + (rows, self.width))
            parts.append(_pad_rows(a, self.mult))
        out = jnp.concatenate(parts, axis=len(lead))
        return _pad_rows(out, self.total_mult)

    def unpack(self, packed, lead=()):
        out = {}
        for key, off, rows, shape in self.entries:
            a = lax.slice_in_dim(packed, off, off + rows, axis=len(lead))
            a = a.reshape(lead + (-1,))
            n = int(np.prod(shape))
            out[key] = lax.slice_in_dim(a, 0, n, axis=len(lead)).reshape(lead + shape)
        return out


def _ffn_fwd(x, h, mod, wg_t, wu_t, wd, tag, next_norm=None, gather=None):
    side = None if gather is None else (_GatherOps, gather)
    gate, up, act, *gathered = _mm_swiglu(h, wg_t, wu_t, f"ffn_gateup_{tag}", side=side)
    x_out, ffn_out, *h_next = _mm_resid([(act, wd)], x, mod[5:6], name=f"ffn_down_{tag}", norm=next_norm)
    return (x_out, dict(h=h, gate=gate, up=up, act=act, out=ffn_out), *h_next, *gathered)


def _ffn_bwd(dx_out, dffn, x_in, saved, mod, norm_w, wg_t, wu_t, wd, below, tag, exchange=None):
    side = None if exchange is None else (_SiblingOps, exchange)
    dgate, dup, *theirs = _mm_swiglu_bwd(dffn, wd, saved["gate"], saved["up"], f"ffn_act_bwd_{tag}", side=side)
    d_wd = _mm_tn(saved["act"], dffn, name=f"ffn_dwd_{tag}")
    d_wg_t = _mm_tn(dgate, saved["h"], name=f"ffn_dwg_{tag}")
    d_wu_t = _mm_tn(dup, saved["h"], name=f"ffn_dwu_{tag}")
    side = None if exchange is None else (_ChipsOps, _pair_sum(exchange, theirs[0], f"grads_pair_sum_{tag}"))
    dh = _mm([(dgate, wg_t), (dup, wu_t)], "nn", name=f"ffn_dh_{tag}", side=side)
    dh, *from_chips = dh if exchange is not None else (dh,)
    dx, d_below, acc = _norm_mod_bwd(x_in, dh, dx_out, norm_w, mod[4:5], below[0], below[1], f"ffn_norm_bwd_{tag}")
    return (dx, d_below, dict(d_wg=d_wg_t, d_wu=d_wu_t, d_wd=d_wd, acc=acc), *from_chips)


_BIG = [
    ("out_w", "out_w_even", 0, "row"), ("qkv_w", "qkv_w", 0, "col"), ("o_w", "o_w", 0, "row"),
    ("gate0", "ffn_gate_w", 0, "col"), ("up0", "ffn_up_w", 0, "col"), ("down0", "ffn_down_w", 0, "row"),
    ("gate1", "ffn_gate_w", 1, "col"), ("up1", "ffn_up_w", 1, "col"), ("down1", "ffn_down_w", 1, "row"),
    ("in_w", "in_w_even", 0, "col"),
]


def _to_wire(a, kind):
    return a.T if kind == "col" else a


_GATHER_GROUPS = [["out_w", "in_w"], ["gate0", "up0", "down0"], ["qkv_w", "o_w"], ["gate1", "up1", "down1"]]
_GRAD_GROUPS = [["qkv_w", "o_w", "gate1", "up1", "down1"], ["out_w", "gate0", "up0", "down0"], ["in_w"]]

_REPLICATED = ["ada_b", "norm_mix_w", "norm_ffn_w", "conv_b", "dt_bias", "a_log", "d_skip", "ssm_norm_w", "gmlp_ln_w",
               "gmlp_ln_b", "gmlp_ws", "gmlp_bs", "sinks", "rel_table", "final_norm_w"]
_TINY_SHARDED = ["conv_w", "qkv_b", "o_b"]

_WEIGHTS = ['ada_w', 'ada_b', 'norm_mix_w', 'norm_ffn_w', 'in_w_even', 'conv_w', 'conv_b', 'dt_bias', 'a_log', 'd_skip',
            'ssm_norm_w', 'gmlp_ln_w', 'gmlp_ln_b', 'gmlp_ws', 'gmlp_bs', 'out_w_even', 'qkv_w', 'qkv_b', 'o_w', 'o_b',
            'sinks', 'rel_table', 'ffn_gate_w', 'ffn_up_w', 'ffn_down_w', 'final_norm_w']


def _step(x, c, loss_target, W, M, V):
    T = x.shape[1]
    x0 = x[0]
    target = loss_target[0]
    me = 4 * lax.axis_index("x") + 2 * lax.axis_index("y") + lax.axis_index("c")

    w_wire_local = {key: _to_wire(W[name][layer].astype(BF16), kind) for key, name, layer, kind in _BIG}

    def wire_pack(keys, mult):
        gp = _Pack(D, 1, mult)
        for key in keys:
            gp.add(key, w_wire_local[key].shape)
        return gp

    gather_packs = [(gp, gp.pack(w_wire_local, BF16)) for gp in (wire_pack(keys, 16) for keys in _GATHER_GROUPS)]
    grad_packs = [wire_pack(keys, 128) for keys in _GRAD_GROUPS]
    full = {}

    def gathered_weights(group, gathered):
        shards = gather_packs[group][0].unpack(gathered, lead=(N_DEV,))
        full.update({key: a.reshape(-1, D) for key, a in shards.items()})

    gathered_weights(0, _all_gather(gather_packs[0][1], "gather_weights"))

    small_in = _Pack(D, 8, 8)
    small_in.add("c", (1, D))
    small_in.add("conv_w", W["conv_w"][0].shape)
    small_in.add("qkv_b", W["qkv_b"][0].shape)
    small_in.add("o_b", W["o_b"][0].shape)
    sm = small_in.unpack(_all_gather(small_in.pack(
        dict(c=c, conv_w=W["conv_w"][0], qkv_b=W["qkv_b"][0], o_b=W["o_b"][0]), F32), "gather_small"), lead=(N_DEV,))
    c_all = sm["c"].reshape(N_DEV, D)
    conv_w_full = jnp.transpose(sm["conv_w"], (1, 0, 2)).reshape(SSM_CONV, CONV_DIM)
    qkv_b_full = sm["qkv_b"].reshape(1, QKV_DIM)
    o_b_full = sm["o_b"].reshape(1, D)

    ncol = W["ada_w"].shape[2]
    cond, mod_cols = _mod_matmul(c_all, W["ada_w"], "mod_matmul")
    mod_g = _all_gather(mod_cols.reshape(DEPTH * N_DEV, ncol), "gather_mod").reshape(N_DEV, DEPTH, N_DEV, ncol)
    mod_me = lax.dynamic_index_in_dim(mod_g, me, axis=2, keepdims=False)
    mod_me = jnp.transpose(mod_me, (1, 0, 2)).reshape(DEPTH, 6, D)
    mod_me = jnp.pad(mod_me, ((0, 0), (0, 2), (0, 0))).reshape(DEPTH * 8, D)
    ada_b_rows = jnp.pad(W["ada_b"].reshape(DEPTH, 6, D), ((0, 0), (0, 2), (0, 0))).reshape(DEPTH * 8, D)
    mod_all = _add_rows(mod_me, ada_b_rows, "mod_bias").reshape(DEPTH, 8, D)
    mod0, mod1 = mod_all[0], mod_all[1]

    in_t = full["in_w"]
    o1, o2, o3, o4 = SSM_INNER, SSM_INNER + CONV_DIM, SSM_INNER + CONV_DIM + SSM_HEADS, SSM_INNER + CONV_DIM + SSM_HEADS + GMLP_INNER
    w_z, w_xbc, w_dt, w_u, w_v = in_t[:o1], in_t[o1:o2], in_t[o2:o3], in_t[o3:o4], in_t[o4:]
    w_main = jnp.concatenate([w_z, w_u, w_v, w_xbc], axis=0)
    w_dtp = jnp.pad(w_dt, ((0, LANES - SSM_HEADS), (0, 0)))
    out_w = full["out_w"]

    pad16 = lambda a: jnp.pad(a.reshape(1, SSM_HEADS), ((0, 0), (0, LANES - SSM_HEADS)))
    dtb, alog = pad16(W["dt_bias"][0]), pad16(W["a_log"][0])
    dskip = jnp.repeat(W["d_skip"][0], SSM_HEAD_DIM).reshape(1, SSM_INNER)
    ssm_nw = W["ssm_norm_w"]
    lnw, lnb = W["gmlp_ln_w"], W["gmlp_ln_b"]
    ws = W["gmlp_ws"][0]
    bs_exp = jnp.repeat(W["gmlp_bs"][0].T, CHUNK, axis=1)
    conv_b = W["conv_b"]
    nmw, nfw = W["norm_mix_w"], W["norm_ffn_w"]
    onehot_t = _bucket_onehot_t()
    head_ind = _head_indicator()
    bias = _bias_from_table(W["rel_table"].T, onehot_t, "rel_bias").reshape(ATTN_HEADS, CHUNK, 2 * CHUNK)
    sinks = W["sinks"][0]

    h0 = _norm_mod(x0, nmw[0:1], mod0[1:2], mod0[0:1], "mix_norm_0")
    pm = _mm([(h0, w_main)], "nt", name="in_proj", tn_pref=1536, out_dtype=BF16)
    dtr = _mm([(h0, w_dtp)], "nt", name="in_proj_dt")
    cpre = _conv_fwd(pm, conv_w_full, conv_b, "conv_fwd")
    ya, ypre, sprev, gathered_b = _ssd_fwd(cpre, dtr, pm, dtb, alog, dskip, ssm_nw, head_ind, "ssd_fwd",
                                           side=(_GatherOps, gather_packs[1][1]))
    gathered_weights(1, gathered_b)
    yb, gathered_c = _gmlp_fwd(pm, lnw, lnb, ws, bs_exp, "gmlp_fwd", side=(_GatherOps, gather_packs[2][1]))
    gathered_weights(2, gathered_c)
    x1, mix0, hf0 = _mm_resid([(ya, out_w[:SSM_INNER]), (yb, out_w[SSM_INNER:])], x0, mod0[2:3], name="out_proj",
                              norm=(nfw[0:1], mod0[4:5], mod0[3:4]))
    x2, ffn0, h1, gathered_d = _ffn_fwd(x1, hf0, mod0, full["gate0"], full["up0"], full["down0"], "0",
                                        next_norm=(nmw[1:2], mod1[1:2], mod1[0:1]), gather=gather_packs[3][1])
    gathered_weights(3, gathered_d)
    qkv_t, o_w = full["qkv_w"], full["o_w"]
    w_q, w_k, w_v_att = qkv_t[:D], qkv_t[D:D + LANES], qkv_t[D + LANES:]

    qkv = _mm([(h1, qkv_t)], "nt", name="qkv_proj", bias=qkv_b_full, tn_pref=1280, out_dtype=BF16)
    att = _attn_fwd(qkv, bias, sinks, "attn_fwd")
    x3, mix1, hf1 = _mm_resid([(att, o_w)], x2, mod1[2:3], name="o_proj", bias=o_b_full,
                              norm=(nfw[1:2], mod1[4:5], mod1[3:4]))
    x4, ffn1 = _ffn_fwd(x3, hf1, mod1, full["gate1"], full["up1"], full["down1"], "1")

    dx4, dffn1, acc_f = _final_loss(x4, W["final_norm_w"].reshape(1, D), target, ffn1["out"], mod1[5:6], "final_loss")
    loss = lax.psum(acc_f[1, 0], ("x", "y", "c"))
    dx3, dmix1, gf1 = _ffn_bwd(dx4, dffn1, x3, ffn1, mod1, nfw[1:2], full["gate1"], full["up1"], full["down1"],
                               (mix1, mod1[2:3]), "1")
    datt = _mm([(dmix1, o_w)], "nt", name="o_proj_dx", out_dtype=BF16)
    d_o_w = _mm_tn(att, dmix1, name="o_proj_dw")
    dq, dk, dv, d_qkv_b, dbias, dsinks = _attn_bwd(qkv, datt, bias, sinks, "attn_bwd")
    d_table = _table_from_dbias(dbias.reshape(ATTN_HEADS, -1), onehot_t, "rel_table_grad")
    d_qkv_t = jnp.concatenate([_mm_tn(dq, h1, name="qkv_dw_q"), _mm_tn(dk, h1, name="qkv_dw_k"), _mm_tn(dv, h1, name="qkv_dw_v")], axis=0)
    dh1 = _mm([(dq, w_q), (dk, w_k), (dv, w_v_att)], "nn", name="qkv_proj_dx")
    dx2, dffn0, acc_n1 = _norm_mod_bwd(x2, dh1, dx3, nmw[1:2], mod1[1:2], ffn0["out"], mod0[5:6], "mix_norm_bwd_1")

    g_wire = dict(qkv_w=d_qkv_t, o_w=d_o_w, gate1=gf1["d_wg"], up1=gf1["d_wu"], down1=gf1["d_wd"])

    def packed_partials(group):
        return grad_packs[group].pack({key: g_wire[key].reshape(N_DEV, -1, D) for key in _GRAD_GROUPS[group]},
                                      F32, lead=(N_DEV,))

    from_chips = {}
    dx1, dmix0, gf0, from_chips[0] = _ffn_bwd(dx2, dffn0, x1, ffn0, mod0, nfw[0:1], full["gate0"], full["up0"], full["down0"],
                                              (mix0, mod0[2:3]), "0", exchange=packed_partials(0))

    dya = _mm([(dmix0, out_w[:SSM_INNER])], "nt", name="out_proj_dx_a")
    dyb = _mm([(dmix0, out_w[SSM_INNER:])], "nt", name="out_proj_dx_b")
    d_out_w = jnp.concatenate([_mm_tn(ya, dmix0, name="out_proj_dw_a"), _mm_tn(yb, dmix0, name="out_proj_dw_b")], axis=0)
    g_wire.update(gate0=gf0["d_wg"], up0=gf0["d_wu"], down0=gf0["d_wd"], out_w=d_out_w)
    partials_ffn0 = packed_partials(1)
    du, dvg, d_ws, d_bs, acc_ln, theirs_ffn0 = _gmlp_bwd(pm, dyb, lnw, lnb, ws, bs_exp, "gmlp_bwd",
                                                         side=(_SiblingOps, partials_ffn0))
    pair_ffn0 = _pair_sum(partials_ffn0, theirs_ffn0, "grads_pair_sum_mix")
    dz, dcpre, ddtr, acc_ssd, d_ssm_nw, from_chips[1] = _ssd_bwd(
        cpre, dtr, pm, ypre, sprev, dya, dtb, alog, dskip, ssm_nw, head_ind, "ssd_bwd", side=(_ChipsOps, pair_ffn0))
    dxbc, acc_conv = _conv_bwd(dcpre, pm, conv_w_full, "conv_bwd")
    d_in_t = jnp.concatenate([
        _mm_tn(dz, h0, name="in_dw_z"), _mm_tn(dxbc, h0, name="in_dw_xbc"),
        _mm_tn(ddtr, h0, name="in_dw_dt")[:SSM_HEADS], _mm_tn(du, h0, name="in_dw_u"), _mm_tn(dvg, h0, name="in_dw_v")], axis=0)
    g_wire.update(in_w=d_in_t)
    partials_mix = packed_partials(2)
    dh0, theirs_mix = _mm([(dz, w_z), (dxbc, w_xbc), (ddtr, w_dtp), (du, w_u), (dvg, w_v)], "nn", name="in_proj_dx",
                          side=(_SiblingOps, partials_mix))
    pair_mix = _pair_sum(partials_mix, theirs_mix, "grads_pair_sum")
    grad_x, acc_n0, from_chips[2] = _norm_mod_bwd(x0, dh0, dx1, nmw[0:1], mod0[1:2], None, None, "mix_norm_bwd_0",
                                                  side=(_ChipsOps, pair_mix))

    g_mine = {}
    for group in range(len(_GRAD_GROUPS)):
        g_mine.update(grad_packs[group].unpack(_sum_parts(from_chips[group], f"grads_chip_sum_{group}")))
    res_big = [{}, {}, {}, {}]
    for key, name, layer, kind in _BIG:
        g_nat = _to_wire(g_mine[key], kind)
        outs = _adamw(g_nat[None], W[name][layer], M[name][layer], V[name][layer], f"adamw_{key}")
        for res, out in zip(res_big, outs):
            res[key] = out

    acc_f0, acc_f1 = gf0["acc"], gf1["acc"]
    views = dict(ada_b=(DEPTH * 6, D), norm_mix_w=(DEPTH, D), norm_ffn_w=(DEPTH, D), conv_b=(1, CONV_DIM),
                 dt_bias=(1, SSM_HEADS), a_log=(1, SSM_HEADS), d_skip=(1, SSM_HEADS), ssm_norm_w=(1, D), gmlp_ln_w=(1, D),
                 gmlp_ln_b=(1, D), gmlp_ws=(GMLP_GROUPS * CHUNK, CHUNK), gmlp_bs=(GMLP_GROUPS, CHUNK),
                 sinks=(1, ATTN_HEADS), rel_table=(REL_BUCKETS, ATTN_HEADS), final_norm_w=(1, D),
                 conv_w=(SSM_CONV, CONV_DIM), qkv_b=(1, QKV_DIM), o_b=(1, D))
    canvas = _Canvas()
    for key, (r, cdim) in views.items():
        canvas.add(key, r, cdim, blocks=GMLP_GROUPS if key == "gmlp_ws" else 1)
    row = lambda i: slice(i, i + 1)
    sources = [
        (acc_n0, [("ada_b", row(1), 0), ("ada_b", row(0), 1), ("norm_mix_w", row(2), 0)]),
        (acc_f0, [("ada_b", row(3), 2), ("ada_b", row(1), 3), ("ada_b", row(0), 4), ("norm_ffn_w", row(2), 0)]),
        (acc_n1, [("ada_b", row(3), 5), ("ada_b", row(1), 6), ("ada_b", row(0), 7), ("norm_mix_w", row(2), 1)]),
        (acc_f1, [("ada_b", row(3), 8), ("ada_b", row(1), 9), ("ada_b", row(0), 10), ("norm_ffn_w", row(2), 1),
                  ("o_b", row(4), 0)]),
        (acc_f, [("ada_b", row(3), 11), ("final_norm_w", row(0), 0)]),
        (acc_conv, [("conv_w", slice(0, SSM_CONV), 0), ("conv_b", row(4), 0)]),
        (acc_ssd, [("dt_bias", row(0), 0), ("a_log", row(1), 0), ("d_skip", row(2), 0)]),
        (d_ssm_nw, [("ssm_norm_w", row(0), 0)]),
        (acc_ln, [("gmlp_ln_w", row(0), 0), ("gmlp_ln_b", row(1), 0)]),
        (d_ws.reshape(GMLP_GROUPS * CHUNK, CHUNK), [("gmlp_ws", slice(0, GMLP_GROUPS * CHUNK), 0)]),
        (d_bs.T, [("gmlp_bs", slice(0, GMLP_GROUPS), 0)]),
        (dsinks, [("sinks", row(0), 0)]),
        (d_table, [("rel_table", slice(0, REL_BUCKETS), 0)]),
        (d_qkv_b, [("qkv_b", row(0), 0)]),
    ]
    parts_small = _all_gather(_canvas_fill(canvas, sources, "small_grads_canvas"), "gather_small_grads")
    view = lambda a, key: a.reshape(views[key])
    params = [(name, view(W[name], name), view(M[name], name), view(V[name], name)) for name in _REPLICATED]
    small_out = _adamw_canvas(canvas, parts_small, params, _TINY_SHARDED, "adamw_small")
    res_small = [{name: small_out[name][k].reshape(W[name].shape) for name in _REPLICATED} for k in range(4)]

    n_cw, n_qb, n_ob = W["conv_w"].shape[2], W["qkv_b"].shape[1], W["o_b"].shape[1]
    g_tiny = dict(conv_w=lax.dynamic_slice_in_dim(small_out["conv_w"][0], me * n_cw, n_cw, axis=1)[None],
                  qkv_b=lax.dynamic_slice_in_dim(small_out["qkv_b"][0], me * n_qb, n_qb, axis=1),
                  o_b=lax.dynamic_slice_in_dim(small_out["o_b"][0], me * n_ob, n_ob, axis=1))
    tiny = _Pack(D, 8, 8)
    for name in _TINY_SHARDED:
        tiny.add(name, W[name].shape)
    pkt = lambda S: tiny.pack({name: S[name] for name in _TINY_SHARDED}, F32)
    res_tiny = [tiny.unpack(r) for r in _adamw(pkt(g_tiny)[None], pkt(W), pkt(M), pkt(V), "adamw_tiny")]

    dmod_all = parts_small[:, canvas.offset["ada_b"]:canvas.offset["ada_b"] + DEPTH * 6].reshape(N_DEV, DEPTH, 6 * D)
    dmod_cols = jnp.transpose(lax.dynamic_slice_in_dim(dmod_all, me * ncol, ncol, axis=2), (1, 0, 2))
    g_ada_w = _ada_w_grad(cond, dmod_cols, "ada_w_grad")
    flat = lambda a: a.reshape(DEPTH * D, ncol)
    res_ada = [r.reshape(DEPTH, D, ncol) for r in _adamw(flat(g_ada_w)[None], flat(W["ada_w"]), flat(M["ada_w"]), flat(V["ada_w"]), "adamw_ada_w")]

    def result(kind_idx, name):
        if name == "ada_w":
            return res_ada[kind_idx]
        if name in _REPLICATED:
            return res_small[kind_idx][name]
        if name in _TINY_SHARDED:
            return res_tiny[kind_idx][name]
        pieces = [res_big[kind_idx][key] for key, nm, layer, kind in _BIG if nm == name]
        return jnp.stack(pieces)

    outs = [loss, grad_x[None]]
    for kind_idx in range(4):
        outs += [result(kind_idx, name) for name in _WEIGHTS]
    return tuple(outs)


def kernel(x, c, ada_w, ada_b, norm_mix_w, norm_ffn_w, in_w_even, conv_w, conv_b, dt_bias, a_log, d_skip, ssm_norm_w, gmlp_ln_w, gmlp_ln_b, gmlp_ws, gmlp_bs, out_w_even, qkv_w, qkv_b, o_w, o_b, sinks, rel_table, ffn_gate_w, ffn_up_w, ffn_down_w, final_norm_w, loss_target, m_ada_w, m_ada_b, m_norm_mix_w, m_norm_ffn_w, m_in_w_even, m_conv_w, m_conv_b, m_dt_bias, m_a_log, m_d_skip, m_ssm_norm_w, m_gmlp_ln_w, m_gmlp_ln_b, m_gmlp_ws, m_gmlp_bs, m_out_w_even, m_qkv_w, m_qkv_b, m_o_w, m_o_b, m_sinks, m_rel_table, m_ffn_gate_w, m_ffn_up_w, m_ffn_down_w, m_final_norm_w, v_ada_w, v_ada_b, v_norm_mix_w, v_norm_ffn_w, v_in_w_even, v_conv_w, v_conv_b, v_dt_bias, v_a_log, v_d_skip, v_ssm_norm_w, v_gmlp_ln_w, v_gmlp_ln_b, v_gmlp_ws, v_gmlp_bs, v_out_w_even, v_qkv_w, v_qkv_b, v_o_w, v_o_b, v_sinks, v_rel_table, v_ffn_gate_w, v_ffn_up_w, v_ffn_down_w, v_final_norm_w):
    args = locals()
    W = {n: args[n] for n in _WEIGHTS}
    M = {n: args["m_" + n] for n in _WEIGHTS}
    V = {n: args["v_" + n] for n in _WEIGHTS}
    return _step(x, c, loss_target, W, M, V)
```

```python
import functools
import math

import numpy as np
import jax
import jax.numpy as jnp
from jax import lax
from jax.experimental import pallas as pl
from jax.experimental.pallas import tpu as pltpu

F32 = jnp.float32
BF16 = jnp.bfloat16
HIGHEST = lax.Precision.HIGHEST
MESH = pl.DeviceIdType.MESH

N_DEV = 8
D = 1024
DEPTH = 2
SSM_HEADS = 16
SSM_HEAD_DIM = 64
SSM_INNER = 1024
SSM_GROUPS = 2
SSM_STATE = 128
SSM_CONV = 4
CHUNK = 128
CONV_DIM = SSM_INNER + 2 * SSM_GROUPS * SSM_STATE
GMLP_GROUPS = 8
GMLP_INNER = 1024
IN_EVEN = 4624
ATTN_HEADS = 16
ATTN_KV = 2
ATTN_DH = 64
QKV_DIM = 1280
REL_BUCKETS = 32
REL_MAX_DIST = 128
FFN = 2816
EPS = 1e-6
NEG_INF = -1e30
LANES = 128

ADAM_LR = 0.001
ADAM_B1 = 0.9
ADAM_B2 = 0.999
ADAM_EPS = 1e-08
ADAM_WD = 0.01
ADAM_STEP = 10

VMEM_LIMIT_BYTES = 56 * 1024 * 1024
ROW_TILE = 512


def _pcall(body, *, name, out_shape, grid=(), in_specs=None, out_specs=None, scratch=(), sem=None):
    params = dict(vmem_limit_bytes=VMEM_LIMIT_BYTES)
    if sem is not None:
        params["dimension_semantics"] = sem
    specs = {} if in_specs is None else dict(in_specs=in_specs, out_specs=out_specs)
    return pl.pallas_call(
        body, name=name, out_shape=out_shape, grid=grid, **specs,
        scratch_shapes=list(scratch), compiler_params=pltpu.CompilerParams(**params))


def _call(body, args, side=None, *, name, out_shape, grid, in_specs, out_specs, scratch=(), sem=None):
    if side is None:
        return _pcall(body, name=name, out_shape=out_shape, grid=grid, in_specs=in_specs, out_specs=out_specs,
                      scratch=scratch, sem=sem)(*args)
    ops_cls, x = side
    n_in, n_out, n_scr = len(in_specs), len(out_shape), len(scratch)
    steps = int(np.prod(grid))
    hbm = pl.BlockSpec(memory_space=pl.ANY)

    def wrapped(*refs):
        ins, x_ref = refs[:n_in], refs[n_in]
        outs, r_ref = refs[n_in + 1:n_in + 1 + n_out], refs[n_in + 1 + n_out]
        scr, sems = refs[n_in + 2 + n_out:n_in + 2 + n_out + n_scr], refs[n_in + 2 + n_out + n_scr:]
        ops = ops_cls(x_ref, r_ref, *sems)
        step = pl.program_id(0)
        for axis in range(1, len(grid)):
            step = step * grid[axis] + pl.program_id(axis)
        pl.when(step == 0)(ops.start)
        body(*ins, *outs, *scr)
        pl.when(step == (3 * steps) // 4)(ops.forward)
        pl.when(step == steps - 1)(ops.finish)

    return _pcall(
        wrapped, name=name, out_shape=list(out_shape) + [ops_cls.result(x)], grid=grid,
        in_specs=list(in_specs) + [hbm], out_specs=list(out_specs) + [hbm],
        scratch=list(scratch) + ops_cls.scratch(), sem=("arbitrary",) * len(grid))(*args, x)


def _tile(n, pref):
    if n <= pref:
        return n
    best = None
    for t in range(LANES, pref + 1, LANES):
        if n % t == 0:
            best = t
    assert best is not None, (n, pref)
    return best


def _rows(T):
    return min(ROW_TILE, T)


def _sds(shape, dtype=F32):
    return jax.ShapeDtypeStruct(shape, dtype)


def _row_spec(tm, c, col=0):
    return pl.BlockSpec((tm, c), lambda i, col=col: (i, col))


def _vec_spec(c, r=1):
    return pl.BlockSpec((r, c), lambda i: (0, 0))


def _sigmoid(x):
    return jax.nn.sigmoid(x)


def _silu(x):
    return x * _sigmoid(x)


def _dsilu(x):
    s = _sigmoid(x)
    return s * (1.0 + x * (1.0 - s))


def _gelu(x):
    return 0.5 * x * (1.0 + lax.erf(x * 0.7071067811865476))


def _dgelu(x):
    return 0.5 * (1.0 + lax.erf(x * 0.7071067811865476)) + x * jnp.exp(-0.5 * x * x) * 0.3989422804014327


def _dot(a, b, dims, precision=None):
    return lax.dot_general(a, b, (dims, ((), ())), precision=precision, preferred_element_type=F32)


def _nn(a, b, precision=None):
    return _dot(a, b, ((1,), (0,)), precision)


def _nt(a, b, precision=None):
    return _dot(a, b, ((1,), (1,)), precision)


def _tn(a, b, precision=None):
    return _dot(a, b, ((0,), (0,)), precision)


def _bf(x):
    return x.astype(BF16)


def _colsum(x):
    return jnp.sum(x, axis=0, keepdims=True)


def _rowsum(x):
    return jnp.sum(x, axis=1, keepdims=True)


def _allsum(x):
    return _colsum(_rowsum(x))


def _comm_call(ops_cls, x, name, from_vmem=False):
    def body(x_ref, out_ref, *sems):
        ops = ops_cls(x_ref, out_ref, *sems)
        ops.start()
        ops.forward()
        ops.finish()

    return pl.pallas_call(
        body, name=name, out_shape=ops_cls.result(x),
        in_specs=[pl.BlockSpec(memory_space=pltpu.VMEM if from_vmem else pl.ANY)],
        out_specs=pl.BlockSpec(memory_space=pl.ANY), scratch_shapes=ops_cls.scratch(),
    )(x)


def _all_gather(x, name):
    return _comm_call(_GatherOps, x, name, from_vmem=True)


class _GatherOps:
    def __init__(self, x_ref, out_ref, send_sems, recv_sems, local_sem):
        self.x_ref, self.out_ref = x_ref, out_ref
        self.send_sems, self.recv_sems, self.local_sem = send_sems, recv_sems, local_sem
        mx, my, mc = lax.axis_index("x"), lax.axis_index("y"), lax.axis_index("c")
        self.mc = mc
        self.me, self.sibling = (mx, my, mc), (mx, my, 1 - mc)
        self.chips = [(1 - mx, my), (mx, 1 - my), (1 - mx, 1 - my)]

    @staticmethod
    def result(x):
        return _sds((N_DEV,) + x.shape, x.dtype)

    @staticmethod
    def scratch():
        return [pltpu.SemaphoreType.DMA((7,)), pltpu.SemaphoreType.DMA((7,)), pltpu.SemaphoreType.DMA(())]

    def _slot(self, px, py, pc):
        return self.out_ref.at[4 * px + 2 * py + pc]

    def _copy(self, k, block, to, own=False):
        return pltpu.make_async_remote_copy(
            src_ref=self.x_ref if own else self._slot(*block), dst_ref=self._slot(*block),
            send_sem=self.send_sems.at[k], recv_sem=self.recv_sems.at[k], device_id=to, device_id_type=MESH)

    def _mine(self):
        return pltpu.make_async_copy(self.x_ref, self._slot(*self.me), self.local_sem)

    def _first(self):
        return [self._copy(0, self.me, self.sibling, own=True)] + [
            self._copy(1 + j, self.me, (*chip, self.mc), own=True) for j, chip in enumerate(self.chips)]

    def _passed(self):
        return [self._copy(4 + j, (*chip, self.mc), self.sibling) for j, chip in enumerate(self.chips)]

    def start(self):
        self._mine().start()
        for cp in self._first():
            cp.start()

    def forward(self):
        passed = self._passed()
        for j, chip in enumerate(self.chips):
            self._copy(1 + j, (*chip, self.mc), self.me).wait_recv()
            passed[j].start()

    def finish(self):
        self._copy(0, self.sibling, self.me).wait_recv()
        for j, chip in enumerate(self.chips):
            self._copy(4 + j, (*chip, 1 - self.mc), self.me).wait_recv()
        for cp in self._first() + self._passed():
            cp.wait_send()
        self._mine().wait()


N_CHIP = 4


class _SiblingOps:
    def __init__(self, p_ref, theirs_ref, send_sems, recv_sems):
        self.p_ref, self.theirs_ref, self.send_sems, self.recv_sems = p_ref, theirs_ref, send_sems, recv_sems

    @staticmethod
    def result(p):
        return _sds((N_CHIP,) + p.shape[1:], p.dtype)

    @staticmethod
    def scratch():
        return [pltpu.SemaphoreType.DMA((N_CHIP,))] * 2

    def _copies(self):
        mx, my, mc = lax.axis_index("x"), lax.axis_index("y"), lax.axis_index("c")
        return [pltpu.make_async_remote_copy(
            src_ref=self.p_ref.at[2 * chip + 1 - mc], dst_ref=self.theirs_ref.at[chip],
            send_sem=self.send_sems.at[chip], recv_sem=self.recv_sems.at[chip],
            device_id=(mx, my, 1 - mc), device_id_type=MESH) for chip in range(N_CHIP)]

    def start(self):
        for cp in self._copies():
            cp.start()

    def forward(self):
        pass

    def finish(self):
        for cp in self._copies():
            cp.wait()


class _ChipsOps:
    def __init__(self, q_ref, out_ref, send_sems, recv_sems, local_sem):
        self.q_ref, self.out_ref = q_ref, out_ref
        self.send_sems, self.recv_sems, self.local_sem = send_sems, recv_sems, local_sem

    @staticmethod
    def result(q):
        return _sds(q.shape, q.dtype)

    @staticmethod
    def scratch():
        return [pltpu.SemaphoreType.DMA((N_CHIP - 1,)), pltpu.SemaphoreType.DMA((N_CHIP - 1,)), pltpu.SemaphoreType.DMA(())]

    def _copies(self):
        mx, my, mc = lax.axis_index("x"), lax.axis_index("y"), lax.axis_index("c")
        me = 2 * mx + my
        local = pltpu.make_async_copy(self.q_ref.at[me], self.out_ref.at[me], self.local_sem)
        remote = []
        for r in range(1, N_CHIP):
            px = 1 - mx if r & 2 else mx
            py = 1 - my if r & 1 else my
            remote.append(pltpu.make_async_remote_copy(
                src_ref=self.q_ref.at[2 * px + py], dst_ref=self.out_ref.at[me],
                send_sem=self.send_sems.at[r - 1], recv_sem=self.recv_sems.at[r - 1],
                device_id=(px, py, mc), device_id_type=MESH))
        return local, remote

    def start(self):
        local, remote = self._copies()
        local.start()
        for cp in remote:
            cp.start()

    def forward(self):
        pass

    def finish(self):
        local, remote = self._copies()
        for cp in remote:
            cp.wait()
        local.wait()


def _pair_sum(p, theirs, name):
    n, R, C = theirs.shape
    tr = _tile_rows(R, 256)

    def body(p_ref, t_ref, o_ref):
        mc = lax.axis_index("c")
        o_ref[0] = (p_ref[0, mc] + t_ref[0]).astype(BF16)

    blk = pl.BlockSpec((1, tr, C), lambda s, i: (s, i, 0))
    return _pcall(body, name=name, out_shape=_sds((n, R, C), BF16), grid=(n, R // tr),
                  in_specs=[pl.BlockSpec((1, 2, tr, C), lambda s, i: (s, 0, i, 0)), blk],
                  out_specs=blk, sem=("parallel", "parallel"))(p.reshape(n, 2, R, C), theirs)


def _sum_parts(parts, name):
    P, R, C = parts.shape
    tr = _tile_rows(R, 256)

    def body(p_ref, o_ref):
        g = p_ref[0].astype(F32)
        for k in range(1, P):
            g = g + p_ref[k].astype(F32)
        o_ref[...] = g

    return _pcall(body, name=name, out_shape=_sds((R, C)), grid=(R // tr,),
                  in_specs=[pl.BlockSpec((P, tr, C), lambda i: (0, i, 0))],
                  out_specs=pl.BlockSpec((tr, C), lambda i: (i, 0)), sem=("parallel",))(parts)


def _adamw(parts, w, m, v, name):
    P, R, C = parts.shape
    tr = R if R <= 256 else _tile_rows(R, 256)

    def body(p_ref, w_ref, m_ref, v_ref, g_ref, d_ref, nm_ref, nv_ref):
        g = p_ref[0]
        for k in range(1, P):
            g = g + p_ref[k]
        nm = ADAM_B1 * m_ref[...] + (1.0 - ADAM_B1) * g
        nv = ADAM_B2 * v_ref[...] + (1.0 - ADAM_B2) * (g * g)
        m_hat = nm / (1.0 - ADAM_B1 ** ADAM_STEP)
        v_hat = nv / (1.0 - ADAM_B2 ** ADAM_STEP)
        g_ref[...] = g
        d_ref[...] = -ADAM_LR * (m_hat / (jnp.sqrt(v_hat) + ADAM_EPS) + ADAM_WD * w_ref[...])
        nm_ref[...] = nm
        nv_ref[...] = nv

    blk = pl.BlockSpec((tr, C), lambda i: (i, 0))
    return _pcall(
        body, name=name, out_shape=[_sds((R, C))] * 4, grid=(R // tr,),
        in_specs=[pl.BlockSpec((P, tr, C), lambda i: (0, i, 0)), blk, blk, blk],
        out_specs=[blk] * 4, sem=("parallel",))(parts, w, m, v)


def _adam_update(g, w, m, v):
    nm = ADAM_B1 * m + (1.0 - ADAM_B1) * g
    nv = ADAM_B2 * v + (1.0 - ADAM_B2) * (g * g)
    m_hat = nm / (1.0 - ADAM_B1 ** ADAM_STEP)
    v_hat = nv / (1.0 - ADAM_B2 ** ADAM_STEP)
    return -ADAM_LR * (m_hat / (jnp.sqrt(v_hat) + ADAM_EPS) + ADAM_WD * w), nm, nv


class _Canvas:
    def __init__(self):
        self.offset, self.views, self.rows = {}, {}, 0

    def add(self, key, r, c, blocks=1):
        need = r // blocks if blocks > 1 else r * (-(-c // D))
        if need >= 8:
            self.rows = -(-self.rows // 8) * 8
        self.offset[key], self.views[key] = self.rows, (r, c, blocks)
        self.rows += need

    @property
    def total(self):
        return -(-self.rows // 8) * 8

    def cells(self, key):
        (r, c, blocks), off = self.views[key], self.offset[key]
        if blocks > 1:
            n = r // blocks
            return [(off, n, b * c, c, slice(b * n, (b + 1) * n), slice(0, c)) for b in range(blocks)]
        if c <= D:
            return [(off, r, 0, c, slice(0, r), slice(0, c))]
        chunks = -(-c // D)
        return [(off + i * chunks + j, 1, 0, min(D, c - j * D), slice(i, i + 1), slice(j * D, min(c, (j + 1) * D)))
                for i in range(r) for j in range(chunks)]


def _canvas_fill(canvas, sources, name):
    arrays = [a for a, _ in sources]

    def body(*refs):
        out_ref = refs[-1]
        out_ref[...] = jnp.zeros_like(out_ref)
        for ref, (_, items) in zip(refs[:-1], sources):
            for key, src_rows, view_row in items:
                n = src_rows.stop - src_rows.start
                for row, count, lane, width, vrows, vcols in canvas.cells(key):
                    lo, hi = max(vrows.start, view_row), min(vrows.stop, view_row + n)
                    if lo < hi:
                        src = slice(src_rows.start + lo - view_row, src_rows.start + hi - view_row)
                        out_ref[row + lo - vrows.start:row + hi - vrows.start, lane:lane + width] = ref[src, vcols]

    return _pcall(body, name=name, out_shape=_sds((canvas.total, D)))(*arrays)


def _adamw_canvas(canvas, parts, params, sum_only, name):
    n = len(params)

    def body(p_ref, *refs):
        sum_ref = refs[-1]
        g_all = p_ref[0]
        for k in range(1, N_DEV):
            g_all = g_all + p_ref[k]
        sum_ref[...] = g_all
        for i, (key, _, _, _) in enumerate(params):
            w_ref, m_ref, v_ref = refs[3 * i:3 * i + 3]
            outs = refs[3 * n + 4 * i:3 * n + 4 * i + 4]
            for row, count, lane, width, vrows, vcols in canvas.cells(key):
                g = sum_ref[row:row + count, lane:lane + width]
                delta, nm, nv = _adam_update(g, w_ref[vrows, vcols], m_ref[vrows, vcols], v_ref[vrows, vcols])
                for ref, val in zip(outs, (g, delta, nm, nv)):
                    ref[vrows, vcols] = val
        for i, key in enumerate(sum_only):
            for row, count, lane, width, vrows, vcols in canvas.cells(key):
                refs[7 * n + i][vrows, vcols] = sum_ref[row:row + count, lane:lane + width]

    args = [a for _, w, m, v in params for a in (w, m, v)]
    out_shape = [_sds(w.shape) for _, w, _, _ in params for _ in range(4)] + [_sds(canvas.views[k][:2]) for k in sum_only]
    res = _pcall(body, name=name, out_shape=out_shape, scratch=[pltpu.VMEM(parts.shape[1:], F32)])(parts, *args)
    out = {key: res[4 * i:4 * i + 4] for i, (key, _, _, _) in enumerate(params)}
    out.update({key: [res[4 * n + i]] for i, key in enumerate(sum_only)})
    return out


def _tile_rows(n, pref):
    best = None
    for t in range(8, pref + 1, 8):
        if n % t == 0:
            best = t
    assert best is not None, (n, pref)
    return best


def _mm(pairs, mode, *, name, out_dtype=F32, bias=None, tn_pref=1024, side=None):
    M = pairs[0][0].shape[0]
    N = pairs[0][1].shape[1] if mode == "nn" else pairs[0][1].shape[0]
    tm, tn = _rows(M), _tile(N, tn_pref)
    n_pairs = len(pairs)
    has_bias = bias is not None

    def body(*refs):
        acc = _pairs_dot(refs[:2 * n_pairs], mode)
        if has_bias:
            acc = acc + refs[2 * n_pairs][...]
        refs[-1][...] = acc.astype(refs[-1].dtype)

    in_specs, args = _pair_specs(pairs, mode, tm, tn)
    if has_bias:
        in_specs.append(pl.BlockSpec((1, tn), lambda j, i: (0, j)))
        args.append(bias)
    res = _call(body, args, side, name=name, out_shape=[_sds((M, N), out_dtype)], grid=(N // tn, M // tm),
                in_specs=in_specs, out_specs=[pl.BlockSpec((tm, tn), lambda j, i: (i, j))], sem=("parallel", "parallel"))
    return res[0] if side is None else (res[0], res[1])


def _pairs_dot(ab, mode):
    acc = None
    for p in range(len(ab) // 2):
        a, b = _bf(ab[2 * p][...]), _bf(ab[2 * p + 1][...])
        d = _nn(a, b) if mode == "nn" else _nt(a, b)
        acc = d if acc is None else acc + d
    return acc


def _pair_specs(pairs, mode, tm, tn):
    in_specs, args = [], []
    for a, b in pairs:
        K = a.shape[1]
        in_specs.append(pl.BlockSpec((tm, K), lambda j, i: (i, 0)))
        if mode == "nn":
            in_specs.append(pl.BlockSpec((K, tn), lambda j, i: (0, j)))
        else:
            in_specs.append(pl.BlockSpec((tn, K), lambda j, i: (j, 0)))
        args += [a, b]
    return in_specs, args


def _mm_resid(pairs, resid, gvec, *, name, bias=None, norm=None):
    M, N = resid.shape
    tm = _rows(M)
    n_pairs = len(pairs)
    has_bias, has_norm = bias is not None, norm is not None

    def body(*refs):
        acc = _pairs_dot(refs[:2 * n_pairs], "nn")
        pos = 2 * n_pairs
        if has_bias:
            acc = acc + refs[pos][...]
            pos += 1
        xv = refs[pos][...] + refs[pos + 1][...] * acc
        outs = refs[pos + 2 + 3 * has_norm:]
        outs[0][...] = xv
        outs[1][...] = acc.astype(BF16)
        if has_norm:
            w_ref, sc_ref, sh_ref = refs[pos + 2:pos + 5]
            r = lax.rsqrt(jnp.mean(xv * xv, axis=-1, keepdims=True) + EPS)
            outs[2][...] = ((xv * r * w_ref[...]) * (1.0 + sc_ref[...]) + sh_ref[...]).astype(BF16)

    in_specs, args = _pair_specs(pairs, "nn", tm, N)
    vec = pl.BlockSpec((1, N), lambda j, i: (0, 0))
    row = pl.BlockSpec((tm, N), lambda j, i: (i, 0))
    if has_bias:
        in_specs.append(vec)
        args.append(bias)
    in_specs += [row, vec] + [vec] * (3 * has_norm)
    args += [resid, gvec] + (list(norm) if has_norm else [])
    return _pcall(body, name=name, out_shape=[_sds((M, N)), _sds((M, N), BF16)] + [_sds((M, N), BF16)] * has_norm,
                  grid=(1, M // tm), in_specs=in_specs, out_specs=[row] * (2 + has_norm),
                  sem=("parallel", "parallel"))(*args)


def _mm_tn(a, b, *, name, tm_pref=1408, tn_pref=1536):
    K, M = a.shape
    N = b.shape[1]
    tm, tn = _tile(M, tm_pref), _tile(N, tn_pref)
    tk = K if K <= 2 * ROW_TILE else 2 * ROW_TILE

    def body(a_ref, b_ref, o_ref):
        @pl.when(pl.program_id(2) == 0)
        def _():
            o_ref[...] = jnp.zeros_like(o_ref)
        o_ref[...] += _tn(_bf(a_ref[...]), _bf(b_ref[...]))

    return _pcall(
        body, name=name, out_shape=_sds((M, N)), grid=(M // tm, N // tn, K // tk),
        in_specs=[pl.BlockSpec((tk, tm), lambda i, j, k: (k, i)), pl.BlockSpec((tk, tn), lambda i, j, k: (k, j))],
        out_specs=pl.BlockSpec((tm, tn), lambda i, j, k: (i, j)),
        sem=("parallel", "parallel", "arbitrary"))(a, b)


def _mm_swiglu(h, wg_t, wu_t, name, side=None):
    M, K = h.shape
    N = wg_t.shape[0]
    tm, tn = _rows(M), _tile(N, 1408)

    def body(h_ref, wg_ref, wu_ref, gate_ref, up_ref, act_ref):
        hv = _bf(h_ref[...])
        gate = _nt(hv, wg_ref[...])
        up = _nt(hv, wu_ref[...])
        gate_ref[...] = gate.astype(BF16)
        up_ref[...] = up.astype(BF16)
        act_ref[...] = (_silu(gate) * up).astype(BF16)

    w_spec = pl.BlockSpec((tn, K), lambda j, i: (j, 0))
    o_spec = pl.BlockSpec((tm, tn), lambda j, i: (i, j))
    return _call(body, (h, wg_t, wu_t), side, name=name,
                 out_shape=[_sds((M, N), BF16)] * 3, grid=(N // tn, M // tm),
                 in_specs=[pl.BlockSpec((tm, K), lambda j, i: (i, 0)), w_spec, w_spec], out_specs=[o_spec] * 3,
                 sem=("parallel", "parallel"))


def _mm_swiglu_bwd(dout, wd, gate, up, name, side=None):
    M, K = dout.shape
    N = wd.shape[0]
    tm, tn = _rows(M), _tile(N, 1408)

    def body(d_ref, wd_ref, gate_ref, up_ref, dg_ref, du_ref):
        dact = _nt(_bf(d_ref[...]), wd_ref[...])
        g = gate_ref[...].astype(F32)
        dg_ref[...] = (dact * up_ref[...].astype(F32) * _dsilu(g)).astype(BF16)
        du_ref[...] = (dact * _silu(g)).astype(BF16)

    t_spec = pl.BlockSpec((tm, tn), lambda j, i: (i, j))
    return _call(
        body, (dout, wd, gate, up), side, name=name, out_shape=[_sds((M, N), BF16)] * 2, grid=(N // tn, M // tm),
        in_specs=[pl.BlockSpec((tm, K), lambda j, i: (i, 0)), pl.BlockSpec((tn, K), lambda j, i: (j, 0)), t_spec, t_spec],
        out_specs=[t_spec] * 2, sem=("parallel", "parallel"))


def _norm_mod(x, w, sc, sh, name):
    T = x.shape[0]
    tm = _rows(T)

    def body(x_ref, w_ref, sc_ref, sh_ref, o_ref):
        xv = x_ref[...]
        r = lax.rsqrt(jnp.mean(xv * xv, axis=-1, keepdims=True) + EPS)
        o_ref[...] = ((xv * r * w_ref[...]) * (1.0 + sc_ref[...]) + sh_ref[...]).astype(BF16)

    return _pcall(body, name=name, out_shape=_sds((T, D), BF16), grid=(T // tm,),
                  in_specs=[_row_spec(tm, D), _vec_spec(D), _vec_spec(D), _vec_spec(D)],
                  out_specs=_row_spec(tm, D), sem=("parallel",))(x, w, sc, sh)


def _gate_rows(dxv, br_ref, g_ref, db_ref, acc_ref):
    db = g_ref[...] * dxv
    db_ref[...] = db.astype(BF16)
    acc_ref[3:4, :] += _colsum(dxv * br_ref[...].astype(F32))
    acc_ref[4:5, :] += _colsum(db)


def _norm_mod_bwd(x, dh, dres, w, sc, branch, g, name, side=None):
    T = x.shape[0]
    tm = _rows(T)
    gated = branch is not None

    def body(x_ref, dh_ref, dres_ref, w_ref, sc_ref, *rest):
        (br_ref, g_ref, dx_ref, db_ref, acc_ref) = rest if gated else (None, None, rest[0], None, rest[1])

        @pl.when(pl.program_id(0) == 0)
        def _():
            acc_ref[...] = jnp.zeros_like(acc_ref)
        xv, dh_v, wv = x_ref[...], dh_ref[...], w_ref[...]
        r = lax.rsqrt(jnp.mean(xv * xv, axis=-1, keepdims=True) + EPS)
        n = xv * r
        dnw = dh_v * (1.0 + sc_ref[...])
        dn = dnw * wv
        dxv = dres_ref[...] + r * (dn - n * jnp.mean(dn * n, axis=-1, keepdims=True))
        dx_ref[...] = dxv
        acc_ref[0:1, :] += _colsum(dh_v * (n * wv))
        acc_ref[1:2, :] += _colsum(dh_v)
        acc_ref[2:3, :] += _colsum(dnw * n)
        if gated:
            _gate_rows(dxv, br_ref, g_ref, db_ref, acc_ref)

    row = _row_spec(tm, D)
    args = (x, dh, dres, w, sc) + ((branch, g) if gated else ())
    return _call(body, args, side, name=name,
                 out_shape=[_sds((T, D))] + ([_sds((T, D), BF16)] if gated else []) + [_sds((8, D))], grid=(T // tm,),
                 in_specs=[row, row, row, _vec_spec(D), _vec_spec(D)] + ([row, _vec_spec(D)] if gated else []),
                 out_specs=[row] + ([row] if gated else []) + [_vec_spec(D, 8)], sem=("arbitrary",))


def _final_loss(x, wf, target, branch, g, name):
    T = x.shape[0]
    tm = _rows(T)

    def body(x_ref, w_ref, t_ref, br_ref, g_ref, dx_ref, db_ref, acc_ref):
        @pl.when(pl.program_id(0) == 0)
        def _():
            acc_ref[...] = jnp.zeros_like(acc_ref)
        xv, wv = x_ref[...], w_ref[...]
        r = lax.rsqrt(jnp.mean(xv * xv, axis=-1, keepdims=True) + EPS)
        n = xv * r
        err = n * wv - t_ref[...]
        dy = err * (1.0 / D)
        dn = dy * wv
        dxv = r * (dn - n * jnp.mean(dn * n, axis=-1, keepdims=True))
        dx_ref[...] = dxv
        acc_ref[0:1, :] += _colsum(dy * n)
        acc_ref[1:2, :] += jnp.broadcast_to(_allsum(err * err) * (0.5 / D), (1, D))
        _gate_rows(dxv, br_ref, g_ref, db_ref, acc_ref)

    row = _row_spec(tm, D)
    return _pcall(body, name=name, out_shape=[_sds((T, D)), _sds((T, D), BF16), _sds((8, D))], grid=(T // tm,),
                  in_specs=[row, _vec_spec(D), row, row, _vec_spec(D)],
                  out_specs=[row, row, _vec_spec(D, 8)], sem=("arbitrary",))(x, wf, target, branch, g)


def _mod_matmul(c_all, ada_w, name):
    n = ada_w.shape[2]

    def body(c_ref, w_ref, cond_ref, o_ref):
        cond = _silu(c_ref[...])
        cond_ref[...] = cond
        o_ref[0] = _nn(cond, w_ref[0])

    return _pcall(body, name=name, out_shape=[_sds((N_DEV, D)), _sds((DEPTH, N_DEV, n))], grid=(DEPTH,),
                  in_specs=[pl.BlockSpec((N_DEV, D), lambda l: (0, 0)), pl.BlockSpec((1, D, n), lambda l: (l, 0, 0))],
                  out_specs=[pl.BlockSpec((N_DEV, D), lambda l: (0, 0)), pl.BlockSpec((1, N_DEV, n), lambda l: (l, 0, 0))],
                  sem=("arbitrary",))(c_all, ada_w)


def _add_rows(a, b, name):
    def body(a_ref, b_ref, o_ref):
        o_ref[...] = a_ref[...] + b_ref[...]

    return _pcall(body, name=name, out_shape=_sds(a.shape))(a, b)


def _ada_w_grad(cond, dmod_cols, name):
    n = dmod_cols.shape[2]

    def body(c_ref, d_ref, o_ref):
        o_ref[0] = _tn(c_ref[...], d_ref[0])

    return _pcall(body, name=name, out_shape=_sds((DEPTH, D, n)), grid=(DEPTH,),
                  in_specs=[pl.BlockSpec((N_DEV, D), lambda l: (0, 0)), pl.BlockSpec((1, N_DEV, n), lambda l: (l, 0, 0))],
                  out_specs=pl.BlockSpec((1, D, n), lambda l: (l, 0, 0)), sem=("parallel",))(cond, dmod_cols)


def _conv_fwd(pm, conv_w, conv_b, name):
    T = pm.shape[0]
    tm = _rows(T)
    C = CONV_DIM

    def body(x_ref, prev_ref, w_ref, b_ref, o_ref):
        cur = x_ref[...].astype(F32)
        prev = jnp.where(pl.program_id(0) > 0, prev_ref[...].astype(F32)[8:16], 0.0)
        cur8 = cur[0:8]
        row8 = lax.broadcasted_iota(jnp.int32, (8, C), 0)
        full = w_ref[3:4, :] * cur
        head = w_ref[3:4, :] * cur8
        for k in range(1, SSM_CONV):
            wk = w_ref[3 - k:4 - k, :]
            full = full + wk * pltpu.roll(cur, k, 0)
            head = head + wk * jnp.where(row8 < k, pltpu.roll(prev, k, 0), pltpu.roll(cur8, k, 0))
        o_ref[...] = full + b_ref[...]
        o_ref[0:8, :] = head + b_ref[...]

    return _pcall(
        body, name=name, out_shape=_sds((T, C)), grid=(T // tm,),
        in_specs=[pl.BlockSpec((tm, C), lambda i: (i, 2)),
                  pl.BlockSpec((16, C), lambda i: (jnp.maximum(i * (tm // 16) - 1, 0), 2)),
                  _vec_spec(C, SSM_CONV), _vec_spec(C)],
        out_specs=_row_spec(tm, C), sem=("parallel",))(pm, pm, conv_w, conv_b)


def _conv_bwd(dc, pm, conv_w, name):
    T = dc.shape[0]
    tm = _rows(T)
    C = CONV_DIM
    nt = T // tm

    def body(dc_ref, nxt_ref, x_ref, prev_ref, w_ref, dx_ref, acc_ref):
        i = pl.program_id(0)

        @pl.when(i == 0)
        def _():
            acc_ref[...] = jnp.zeros_like(acc_ref)
        dcv = dc_ref[...]
        nxt = jnp.where(i < nt - 1, nxt_ref[...], 0.0)
        xc = x_ref[...].astype(F32)
        prev = jnp.where(i > 0, prev_ref[...].astype(F32)[8:16], 0.0)
        dc8h, dc8t, x8 = dcv[0:8], dcv[tm - 8:tm], xc[0:8]
        row8 = lax.broadcasted_iota(jnp.int32, (8, C), 0)
        full = w_ref[3:4, :] * dcv
        tail = w_ref[3:4, :] * dc8t
        acc_ref[3:4, :] += _colsum(dcv * xc)
        for k in range(1, SSM_CONV):
            wk = w_ref[3 - k:4 - k, :]
            full = full + wk * pltpu.roll(dcv, tm - k, 0)
            tail = tail + wk * jnp.where(row8 + k >= 8, pltpu.roll(nxt, 8 - k, 0), pltpu.roll(dc8t, 8 - k, 0))
            xs_head = jnp.where(row8 < k, pltpu.roll(prev, k, 0), pltpu.roll(x8, k, 0))
            prod = dcv * pltpu.roll(xc, k, 0)
            acc_ref[3 - k:4 - k, :] += _colsum(prod) - _colsum(prod[0:8]) + _colsum(dc8h * xs_head)
        acc_ref[4:5, :] += _colsum(dcv)
        dx_ref[...] = jnp.concatenate([full[0:tm - 8], tail], axis=0).astype(BF16)

    return _pcall(
        body, name=name, out_shape=[_sds((T, C), BF16), _sds((8, C))], grid=(nt,),
        in_specs=[_row_spec(tm, C),
                  pl.BlockSpec((8, C), lambda i: (jnp.minimum((i + 1) * (tm // 8), T // 8 - 1), 0)),
                  pl.BlockSpec((tm, C), lambda i: (i, 2)),
                  pl.BlockSpec((16, C), lambda i: (jnp.maximum(i * (tm // 16) - 1, 0), 2)),
                  _vec_spec(C, SSM_CONV)],
        out_specs=[_row_spec(tm, C), _vec_spec(C, 8)], sem=("arbitrary",))(dc, dc, pm, pm, conv_w)


def _ssd_prologue(cpre, dtr, dtb, alog):
    L = CHUNK
    xc = _silu(cpre)
    pre = dtr + dtb
    dt = jnp.maximum(pre, 0.0) + jnp.log1p(jnp.exp(-jnp.abs(pre)))
    a = -jnp.exp(alog)
    la = dt * a
    row = lax.broadcasted_iota(jnp.int32, (L, L), 0)
    col = lax.broadcasted_iota(jnp.int32, (L, L), 1)
    causal = row >= col
    tri = causal.astype(F32)
    lc = _nn(tri, la, HIGHEST)
    return xc, pre, dt, a, causal, tri, lc, row, col


def _head_indicator():
    m = np.zeros((LANES, SSM_INNER), np.float32)
    for h in range(SSM_HEADS):
        m[h, h * SSM_HEAD_DIM:(h + 1) * SSM_HEAD_DIM] = 1.0
    return jnp.asarray(m, dtype=BF16)


def _split_dot(x, ind, dims):
    hi = x.astype(BF16)
    lo = (x - hi.astype(F32)).astype(BF16)
    return _dot(hi, ind, dims) + _dot(lo, ind, dims)


def _expand(x16, ind):
    return _split_dot(x16, ind, ((1,), (0,)))


def _headsum(x, ind, single_pass=False):
    if single_pass:
        return _dot(x.astype(BF16), ind, ((1,), (1,)))
    return _split_dot(x, ind, ((1,), (1,)))


def _ssd_fwd(cpre, dtr, pm, dtb, alog, dskip, normw, ind, name, side=None):
    T = cpre.shape[0]
    nc = T // CHUNK
    L, P, H, HPG, N = CHUNK, SSM_HEAD_DIM, SSM_HEADS, SSM_HEADS // SSM_GROUPS, SSM_STATE
    half = SSM_INNER // SSM_GROUPS

    def body(cp_ref, dtr_ref, z_ref, dtb_ref, alog_ref, dskip_ref, nw_ref, ind_ref, ya_ref, y_ref, sp_ref, st_ref):
        @pl.when(pl.program_id(0) == 0)
        def _():
            st_ref[...] = jnp.zeros_like(st_ref)
        xc, _, dt, _, causal, _, lc, _, _ = _ssd_prologue(cp_ref[...], dtr_ref[...], dtb_ref[...], alog_ref[...])
        lct = lc.T
        ind = ind_ref[...]
        llast = lc[L - 1:L, :]
        xs = xc[:, :SSM_INNER]
        xd = xs * _expand(dt, ind)
        ex = _expand(jnp.exp(lc), ind)
        xd_end = _bf(xd * _expand(jnp.exp(llast - lc), ind))
        cdx = _expand(jnp.broadcast_to(jnp.exp(llast), (8, LANES)), ind)[0:1]
        xdb = _bf(xd)
        sp_ref[0] = st_ref[...]
        for g in range(SSM_GROUPS):
            sl = slice(g * half, (g + 1) * half)
            bm = _bf(xc[:, SSM_INNER + g * N:SSM_INNER + (g + 1) * N])
            cm = _bf(xc[:, SSM_INNER + (SSM_GROUPS + g) * N:SSM_INNER + (SSM_GROUPS + g + 1) * N])
            cb = _nt(cm, bm)
            st = st_ref[g]
            y_ref[:, sl] = ex[:, sl] * _nn(cm, _bf(st)) + dskip_ref[:, sl] * xs[:, sl]
            st_ref[g] = st * cdx[:, sl] + _tn(bm, xd_end[:, sl])
            for j in range(HPG):
                h = g * HPG + j
                decay = jnp.where(causal, jnp.exp(jnp.where(causal, lc[:, h:h + 1] - lct[h:h + 1, :], 0.0)), 0.0)
                y_ref[:, h * P:(h + 1) * P] += _nn(_bf(cb * decay), xdb[:, h * P:(h + 1) * P])
        y2 = y_ref[...] * _silu(z_ref[...].astype(F32))
        for g in range(SSM_GROUPS):
            yg = y2[:, g * half:(g + 1) * half]
            r = lax.rsqrt(jnp.mean(yg * yg, axis=-1, keepdims=True) + EPS)
            ya_ref[:, g * half:(g + 1) * half] = (yg * r * nw_ref[:, g * half:(g + 1) * half]).astype(BF16)

    return _call(
        body, (cpre, dtr, pm, dtb, alog, dskip, normw, ind), side, name=name,
        out_shape=[_sds((T, SSM_INNER), BF16), _sds((T, SSM_INNER)), _sds((nc, SSM_GROUPS, N, half))], grid=(nc,),
        in_specs=[_row_spec(L, CONV_DIM), _row_spec(L, LANES), _row_spec(L, SSM_INNER, 0),
                  _vec_spec(LANES), _vec_spec(LANES), _vec_spec(SSM_INNER), _vec_spec(SSM_INNER), _vec_spec(SSM_INNER, LANES)],
        out_specs=[_row_spec(L, SSM_INNER), _row_spec(L, SSM_INNER),
                   pl.BlockSpec((1, SSM_GROUPS, N, half), lambda i: (i, 0, 0, 0))],
        scratch=[pltpu.VMEM((SSM_GROUPS, N, half), F32)], sem=("arbitrary",))


def _ssd_bwd(cpre, dtr, pm, ypre, sprev, dya, dtb, alog, dskip, normw, ind, name, side=None):
    T = cpre.shape[0]
    nc = T // CHUNK
    L, P, H, HPG, N = CHUNK, SSM_HEAD_DIM, SSM_HEADS, SSM_HEADS // SSM_GROUPS, SSM_STATE
    half = SSM_INNER // SSM_GROUPS

    def body(cp_ref, dtr_ref, z_ref, y_ref, sp_ref, dya_ref, dtb_ref, alog_ref, dskip_ref, nw_ref, ind_ref,
             dz_ref, dcp_ref, ddtr_ref, acc_ref, dnw_ref, ds_ref, dy_ref, dxd_ref, rr_ref, yoff_ref, dcd_ref):
        @pl.when(pl.program_id(0) == 0)
        def _():
            ds_ref[...] = jnp.zeros_like(ds_ref)
            acc_ref[...] = jnp.zeros_like(acc_ref)
            dnw_ref[...] = jnp.zeros_like(dnw_ref)
        cpre_v = cp_ref[...]
        xc, pre, dt, a, causal, tri, lc, row, col = _ssd_prologue(cpre_v, dtr_ref[...], dtb_ref[...], alog_ref[...])
        lct = lc.T
        zv, yv = z_ref[...].astype(F32), y_ref[...]
        sz = _silu(zv)
        y2 = yv * sz
        dya_v = dya_ref[...]
        nwv = nw_ref[...]
        for g in range(SSM_GROUPS):
            sl = slice(g * half, (g + 1) * half)
            yg = y2[:, sl]
            r = lax.rsqrt(jnp.mean(yg * yg, axis=-1, keepdims=True) + EPS)
            nrm = yg * r
            dnw_ref[:, sl] += _colsum(dya_v[:, sl] * nrm)
            dn = dya_v[:, sl] * nwv[:, sl]
            dy2 = r * (dn - nrm * jnp.mean(dn * nrm, axis=-1, keepdims=True))
            dy_ref[:, sl] = dy2 * sz[:, sl]
            dz_ref[:, sl] = (dy2 * yv[:, sl] * _dsilu(zv[:, sl])).astype(BF16)
        ind = ind_ref[...]
        llast = lc[L - 1:L, :]
        dte16 = jnp.exp(llast - lc)
        cd16 = jnp.exp(llast)
        xs = xc[:, :SSM_INNER]
        dtx = _expand(dt, ind)
        ex = _expand(jnp.exp(lc), ind)
        dtex = _expand(dte16, ind)
        cdx = _expand(jnp.broadcast_to(cd16, (8, LANES)), ind)[0:1]
        xd = xs * dtx
        xdb = _bf(xd)
        xd_end = _bf(xd * dtex)
        dyv = dy_ref[...]
        dy_off = _bf(ex * dyv)
        dyb = _bf(dyv)
        dskx = dskip_ref[...]
        lane_c = lax.broadcasted_iota(jnp.int32, (L, LANES), 1)
        lane1 = lax.broadcasted_iota(jnp.int32, (1, LANES), 1)
        sub16 = lax.broadcasted_iota(jnp.int32, (H, L), 0)
        dlc_c = jnp.zeros((L, LANES), F32)
        dlc_r = jnp.zeros((H, L), F32)
        for g in range(SSM_GROUPS):
            sl = slice(g * half, (g + 1) * half)
            b_lo = SSM_INNER + g * N
            c_lo = SSM_INNER + (SSM_GROUPS + g) * N
            bm, cm = _bf(xc[:, b_lo:b_lo + N]), _bf(xc[:, c_lo:c_lo + N])
            cb = _nt(cm, bm)
            st, dst = sp_ref[0, g], ds_ref[g]
            stb, dstb = _bf(st), _bf(dst)
            dcm = _nt(dy_off[:, sl], stb)
            ds_ref[g] = _tn(cm, dy_off[:, sl]) + dst * cdx[:, sl]
            rr_ref[:, sl] = _nn(bm, dstb)
            yoff_ref[:, sl] = ex[:, sl] * _nn(cm, stb)
            db = _nt(xd_end[:, sl], dstb)
            dcd_ref[:, sl] = _colsum(dst * st)
            dcb = jnp.zeros((L, L), F32)
            for j in range(HPG):
                h = g * HPG + j
                hs = slice(h * P, (h + 1) * P)
                decay = jnp.where(causal, jnp.exp(jnp.where(causal, lc[:, h:h + 1] - lct[h:h + 1, :], 0.0)), 0.0)
                m = cb * decay
                dxd_ref[:, hs] = _tn(_bf(m), dyb[:, hs])
                dm = _nt(dyb[:, hs], xdb[:, hs])
                dcb = dcb + dm * decay
                gm = dm * m
                dlc_c = dlc_c + jnp.where(lane_c == h, _rowsum(gm), 0.0)
                dlc_r = dlc_r + jnp.where(sub16 == h, _colsum(gm), 0.0)
            dcbb = _bf(dcb)
            dcp_ref[:, c_lo:c_lo + N] = dcm + _nn(dcbb, bm)
            dcp_ref[:, b_lo:b_lo + N] = db + _tn(dcbb, cm)
        dxd_diag, rr = dxd_ref[...], rr_ref[...]
        tt = _headsum(rr * xd, ind, single_pass=True) * dte16
        dlc_rt = jnp.concatenate([dlc_r, jnp.zeros((LANES - H, L), F32)], axis=0).T
        dlc = dlc_c - dlc_rt + _headsum(dyv * yoff_ref[...], ind, single_pass=True) - tt
        dcd = _headsum(jnp.broadcast_to(dcd_ref[...], (8, SSM_INNER)), ind)[0:1]
        dlc = dlc + jnp.where(row == L - 1, _colsum(tt) + dcd * cd16, 0.0)
        dla = _tn(tri, dlc, HIGHEST)
        dxd = dxd_diag + dtex * rr
        ddt = _headsum(dxd * xs, ind, single_pass=True) + dla * a
        ddtr = jnp.where(lane_c < H, ddt * _sigmoid(pre), 0.0)
        ddtr_ref[...] = ddtr
        acc_ref[0:1, :] += _colsum(ddtr)
        acc_ref[1:2, :] += jnp.where(lane1 < H, _colsum(dla * dt) * a, 0.0)
        acc_ref[2:3, :] += _headsum(jnp.broadcast_to(_colsum(dyv * xs), (8, SSM_INNER)), ind)[0:1]
        dcp_ref[:, 0:SSM_INNER] = dxd * dtx + dskx * dyv
        dcp_ref[...] = dcp_ref[...] * _dsilu(cpre_v)

    rev = lambda i: (nc - 1 - i, 0)
    rspec = lambda c: pl.BlockSpec((L, c), rev)
    return _call(
        body, (cpre, dtr, pm, ypre, sprev, dya, dtb, alog, dskip, normw, ind), side, name=name,
        out_shape=[_sds((T, SSM_INNER), BF16), _sds((T, CONV_DIM)), _sds((T, LANES)), _sds((8, LANES)), _sds((1, SSM_INNER))],
        grid=(nc,),
        in_specs=[rspec(CONV_DIM), rspec(LANES), rspec(SSM_INNER), rspec(SSM_INNER),
                  pl.BlockSpec((1, SSM_GROUPS, N, half), lambda i: (nc - 1 - i, 0, 0, 0)), rspec(SSM_INNER),
                  _vec_spec(LANES), _vec_spec(LANES), _vec_spec(SSM_INNER), _vec_spec(SSM_INNER), _vec_spec(SSM_INNER, LANES)],
        out_specs=[rspec(SSM_INNER), rspec(CONV_DIM), rspec(LANES), _vec_spec(LANES, 8), _vec_spec(SSM_INNER)],
        scratch=[pltpu.VMEM((SSM_GROUPS, N, half), F32), pltpu.VMEM((L, SSM_INNER), F32), pltpu.VMEM((L, SSM_INNER), F32),
                 pltpu.VMEM((L, SSM_INNER), F32), pltpu.VMEM((L, SSM_INNER), F32), pltpu.VMEM((1, SSM_INNER), F32)],
        sem=("arbitrary",))


def _gmlp_common(u, v, lnw, lnb):
    ug = _gelu(u)
    vg = _gelu(v)
    mu = jnp.mean(vg, axis=-1, keepdims=True)
    cen = vg - mu
    rstd = lax.rsqrt(jnp.mean(cen * cen, axis=-1, keepdims=True) + EPS)
    vhat = cen * rstd
    return ug, rstd, vhat, vhat * lnw + lnb


def _causal_mask():
    row = lax.broadcasted_iota(jnp.int32, (CHUNK, CHUNK), 0)
    col = lax.broadcasted_iota(jnp.int32, (CHUNK, CHUNK), 1)
    return row >= col


def _gmlp_fwd(pm, lnw, lnb, ws, bs_exp, name, side=None):
    T = pm.shape[0]
    nc = T // CHUNK
    L, G = CHUNK, GMLP_GROUPS

    def body(u_ref, v_ref, lnw_ref, lnb_ref, ws_ref, bs_ref, o_ref):
        ug, _, _, vn = _gmlp_common(u_ref[...].astype(F32), v_ref[...].astype(F32), lnw_ref[...], lnb_ref[...])
        causal = _causal_mask()
        for g in range(G):
            sl = slice(g * L, (g + 1) * L)
            wm = _bf(jnp.where(causal, ws_ref[g], 0.0))
            sv = _nn(wm, _bf(vn[:, sl])) + bs_ref[:, sl]
            o_ref[:, sl] = (ug[:, sl] * sv).astype(BF16)

    return _call(
        body, (pm, pm, lnw, lnb, ws, bs_exp), side, name=name, out_shape=[_sds((T, GMLP_INNER), BF16)], grid=(nc,),
        in_specs=[_row_spec(L, GMLP_INNER, 1), _row_spec(L, GMLP_INNER, 2), _vec_spec(GMLP_INNER), _vec_spec(GMLP_INNER),
                  pl.BlockSpec((G, L, L), lambda i: (0, 0, 0)), _vec_spec(GMLP_INNER, L)],
        out_specs=[_row_spec(L, GMLP_INNER)], sem=("parallel",))


def _gmlp_bwd(pm, dyb, lnw, lnb, ws, bs_exp, name, side=None):
    T = pm.shape[0]
    nc = T // CHUNK
    L, G = CHUNK, GMLP_GROUPS

    def body(u_ref, v_ref, dy_ref, lnw_ref, lnb_ref, ws_ref, bs_ref, du_ref, dv_ref, dws_ref, dbs_ref, acc_ref, dvn_ref):
        @pl.when(pl.program_id(0) == 0)
        def _():
            dws_ref[...] = jnp.zeros_like(dws_ref)
            dbs_ref[...] = jnp.zeros_like(dbs_ref)
            acc_ref[...] = jnp.zeros_like(acc_ref)
        uv, vv, dyv, lnwv = u_ref[...].astype(F32), v_ref[...].astype(F32), dy_ref[...], lnw_ref[...]
        ug, rstd, vhat, vn = _gmlp_common(uv, vv, lnwv, lnb_ref[...])
        causal = _causal_mask()
        lane = lax.broadcasted_iota(jnp.int32, (L, LANES), 1)
        dbs = jnp.zeros((L, LANES), F32)
        for g in range(G):
            sl = slice(g * L, (g + 1) * L)
            wm = _bf(jnp.where(causal, ws_ref[g], 0.0))
            vng = _bf(vn[:, sl])
            sv = _nn(wm, vng) + bs_ref[:, sl]
            du_ref[:, sl] = (dyv[:, sl] * sv * _dgelu(uv[:, sl])).astype(BF16)
            dsv = dyv[:, sl] * ug[:, sl]
            dsvb = _bf(dsv)
            dws_ref[g] += jnp.where(causal, _nt(dsvb, vng), 0.0)
            dbs = dbs + jnp.where(lane == g, _rowsum(dsv), 0.0)
            dvn_ref[:, sl] = _tn(wm, dsvb)
        dbs_ref[...] += dbs
        dvn = dvn_ref[...]
        acc_ref[0:1, :] += _colsum(dvn * vhat)
        acc_ref[1:2, :] += _colsum(dvn)
        dvh = dvn * lnwv
        dvg = rstd * (dvh - jnp.mean(dvh, axis=-1, keepdims=True) - vhat * jnp.mean(dvh * vhat, axis=-1, keepdims=True))
        dv_ref[...] = (dvg * _dgelu(vv)).astype(BF16)

    return _call(
        body, (pm, pm, dyb, lnw, lnb, ws, bs_exp), side, name=name,
        out_shape=[_sds((T, GMLP_INNER), BF16), _sds((T, GMLP_INNER), BF16), _sds((G, L, L)), _sds((L, LANES)), _sds((8, GMLP_INNER))],
        grid=(nc,),
        in_specs=[_row_spec(L, GMLP_INNER, 1), _row_spec(L, GMLP_INNER, 2), _row_spec(L, GMLP_INNER),
                  _vec_spec(GMLP_INNER), _vec_spec(GMLP_INNER), pl.BlockSpec((G, L, L), lambda i: (0, 0, 0)),
                  _vec_spec(GMLP_INNER, L)],
        out_specs=[_row_spec(L, GMLP_INNER), _row_spec(L, GMLP_INNER), pl.BlockSpec((G, L, L), lambda i: (0, 0, 0)),
                   _vec_spec(LANES, L), _vec_spec(GMLP_INNER, 8)],
        scratch=[pltpu.VMEM((L, GMLP_INNER), F32)], sem=("arbitrary",))


def _rel_buckets():
    qi = np.arange(CHUNK)[:, None]
    sj = np.arange(2 * CHUNK)[None, :]
    dist = np.maximum(qi + CHUNK - sj, 0)
    max_exact = REL_BUCKETS // 2
    log_ratio = (np.log(np.maximum(dist, 1).astype(np.float32) / np.float32(max_exact))
                 / np.float32(math.log(REL_MAX_DIST / max_exact))).astype(np.float32)
    large = max_exact + (log_ratio * np.float32(REL_BUCKETS - max_exact)).astype(np.int32)
    return np.where(dist < max_exact, dist, np.minimum(large, REL_BUCKETS - 1))


def _bucket_onehot_t():
    bucket = _rel_buckets().reshape(-1)
    return jnp.asarray((np.arange(REL_BUCKETS)[:, None] == bucket[None, :]).astype(np.float32))


def _bias_from_table(table_t, onehot_t, name):
    def body(t_ref, o_ref, out_ref):
        out_ref[...] = _nn(t_ref[...], o_ref[...], HIGHEST)

    return _pcall(body, name=name, out_shape=_sds((ATTN_HEADS, onehot_t.shape[1])))(table_t, onehot_t)


def _table_from_dbias(dbias, onehot_t, name):
    def body(d_ref, o_ref, out_ref):
        out_ref[...] = _nt(o_ref[...], d_ref[...], HIGHEST)

    return _pcall(body, name=name, out_shape=_sds((REL_BUCKETS, ATTN_HEADS)))(dbias, onehot_t)


def _softmax_sink(logits, sink, mask):
    logits = jnp.where(mask, logits, NEG_INF)
    mx = jnp.maximum(jnp.max(logits, axis=-1, keepdims=True), sink)
    e = jnp.exp(logits - mx)
    es = jnp.exp(sink - mx)
    inv = 1.0 / (_rowsum(e) + es)
    return e * inv, es * inv


def _attn_mask(n, heads):
    qi = lax.broadcasted_iota(jnp.int32, (heads * CHUNK, 2 * CHUNK), 0) & (CHUNK - 1)
    sj = lax.broadcasted_iota(jnp.int32, (heads * CHUNK, 2 * CHUNK), 1)
    rel = qi + CHUNK - sj
    return (rel >= 0) & (rel < CHUNK) & ((sj >= CHUNK) | (n > 0))


def _stack_heads(ref, first, count, width):
    return jnp.concatenate([_bf(ref[:, (first + j) * width:(first + j + 1) * width]) for j in range(count)], axis=0)


def _attn_fwd(qkv, bias, sinks, name):
    T = qkv.shape[0]
    nb = T // CHUNK
    L, DH, HPK = CHUNK, ATTN_DH, ATTN_HEADS // ATTN_KV
    scale = DH ** -0.5
    kcol, vcol = ATTN_HEADS * DH // LANES, ATTN_HEADS * DH // LANES + 1

    def body(q_ref, k_ref, v_ref, kp_ref, vp_ref, bias_ref, sink_ref, o_ref, lg_ref, p_ref):
        n = pl.program_id(0)
        mask = _attn_mask(n, 1)
        kband = _bf(jnp.concatenate([kp_ref[...], k_ref[...]], axis=0))
        vband = _bf(jnp.concatenate([vp_ref[...], v_ref[...]], axis=0))
        for kv in range(ATTN_KV):
            lg_ref[...] = _nt(_stack_heads(q_ref, kv * HPK, HPK, DH), kband[:, kv * DH:(kv + 1) * DH])
            for j in range(HPK):
                h = kv * HPK + j
                p, _ = _softmax_sink(lg_ref[j * L:(j + 1) * L, :] * scale + bias_ref[h], sink_ref[h], mask)
                p_ref[j * L:(j + 1) * L, :] = _bf(p)
            og = _nn(p_ref[...], vband[:, kv * DH:(kv + 1) * DH])
            for j in range(HPK):
                h = kv * HPK + j
                o_ref[:, h * DH:(h + 1) * DH] = og[j * L:(j + 1) * L].astype(BF16)

    prev = lambda i: jnp.maximum(i - 1, 0)
    return _pcall(
        body, name=name, out_shape=_sds((T, ATTN_HEADS * DH), BF16), grid=(nb,),
        in_specs=[_row_spec(L, ATTN_HEADS * DH, 0), _row_spec(L, LANES, kcol), _row_spec(L, LANES, vcol),
                  pl.BlockSpec((L, LANES), lambda i: (prev(i), kcol)), pl.BlockSpec((L, LANES), lambda i: (prev(i), vcol)),
                  pl.BlockSpec((ATTN_HEADS, L, 2 * L), lambda i: (0, 0, 0)),
                  pl.BlockSpec(memory_space=pltpu.SMEM)],
        out_specs=_row_spec(L, ATTN_HEADS * DH),
        scratch=[pltpu.VMEM((HPK * L, 2 * L), F32), pltpu.VMEM((HPK * L, 2 * L), BF16)],
        sem=("parallel",))(qkv, qkv, qkv, qkv, qkv, bias, sinks)


def _attn_bwd(qkv, datt, bias, sinks, name):
    T = qkv.shape[0]
    nb = T // CHUNK
    L, DH, HPK = CHUNK, ATTN_DH, ATTN_HEADS // ATTN_KV
    scale = DH ** -0.5
    kcol, vcol = ATTN_HEADS * DH // LANES, ATTN_HEADS * DH // LANES + 1

    def body(q_ref, k_ref, v_ref, kp_ref, vp_ref, do_ref, bias_ref, sink_ref,
             dq_ref, dk_ref, dv_ref, bsum_ref, dbias_ref, dsink_ref, pend_k, pend_v, band_k, band_v, lg_ref, dp_ref, p_ref, dl_ref):
        n = pl.program_id(0)

        @pl.when(n == 0)
        def _():
            dbias_ref[...] = jnp.zeros_like(dbias_ref)
            dsink_ref[...] = jnp.zeros_like(dsink_ref)
            bsum_ref[...] = jnp.zeros_like(bsum_ref)

        def emit_kv(dk, dv):
            dk_ref[...] = dk.astype(BF16)
            dv_ref[...] = dv.astype(BF16)
            bsum_ref[:, ATTN_HEADS * DH:ATTN_HEADS * DH + LANES] += _colsum(dk)
            bsum_ref[:, ATTN_HEADS * DH + LANES:] += _colsum(dv)

        @pl.when(n < nb)
        def _():
            mask = _attn_mask(n, 1)
            kband = _bf(jnp.concatenate([kp_ref[...], k_ref[...]], axis=0))
            vband = _bf(jnp.concatenate([vp_ref[...], v_ref[...]], axis=0))
            lane1 = lax.broadcasted_iota(jnp.int32, (1, LANES), 1)
            dsink = jnp.zeros((1, LANES), F32)
            for kv in range(ATTN_KV):
                kb, vb = kband[:, kv * DH:(kv + 1) * DH], vband[:, kv * DH:(kv + 1) * DH]
                qg = _stack_heads(q_ref, kv * HPK, HPK, DH)
                dog = _stack_heads(do_ref, kv * HPK, HPK, DH)
                lg_ref[...] = _nt(qg, kb)
                dp_ref[...] = _nt(dog, vb)
                for j in range(HPK):
                    h = kv * HPK + j
                    rows = slice(j * L, (j + 1) * L)
                    p, ps = _softmax_sink(lg_ref[rows, :] * scale + bias_ref[h], sink_ref[h], mask)
                    dp = dp_ref[rows, :]
                    delta = _rowsum(p * dp)
                    dl = p * (dp - delta)
                    dbias_ref[h] += dl
                    p_ref[rows, :] = _bf(p)
                    dl_ref[rows, :] = _bf(dl)
                    dsink = dsink + jnp.where(lane1 == h, -_colsum(ps * delta), 0.0)
                band_v[:, kv * DH:(kv + 1) * DH] = _tn(p_ref[...], dog)
                dqg = _nn(dl_ref[...], kb) * scale
                band_k[:, kv * DH:(kv + 1) * DH] = _tn(dl_ref[...], qg) * scale
                for j in range(HPK):
                    h = kv * HPK + j
                    dq_ref[:, h * DH:(h + 1) * DH] = dqg[j * L:(j + 1) * L].astype(BF16)
                    bsum_ref[:, h * DH:(h + 1) * DH] += _colsum(dqg[j * L:(j + 1) * L])
            dsink_ref[...] += dsink

            @pl.when(n > 0)
            def _():
                emit_kv(pend_k[...] + band_k[0:L, :], pend_v[...] + band_v[0:L, :])
            pend_k[...] = band_k[L:2 * L, :]
            pend_v[...] = band_v[L:2 * L, :]

        @pl.when(n == nb)
        def _():
            emit_kv(pend_k[...], pend_v[...])

    cur = lambda i: jnp.minimum(i, nb - 1)
    prev = lambda i: jnp.maximum(jnp.minimum(i, nb - 1) - 1, 0)
    lag = lambda i: jnp.maximum(i - 1, 0)
    return _pcall(
        body, name=name,
        out_shape=[_sds((T, ATTN_HEADS * DH), BF16), _sds((T, LANES), BF16), _sds((T, LANES), BF16), _sds((1, QKV_DIM)),
                   _sds((ATTN_HEADS, L, 2 * L)), _sds((1, LANES))],
        grid=(nb + 1,),
        in_specs=[pl.BlockSpec((L, ATTN_HEADS * DH), lambda i: (cur(i), 0)),
                  pl.BlockSpec((L, LANES), lambda i: (cur(i), kcol)), pl.BlockSpec((L, LANES), lambda i: (cur(i), vcol)),
                  pl.BlockSpec((L, LANES), lambda i: (prev(i), kcol)), pl.BlockSpec((L, LANES), lambda i: (prev(i), vcol)),
                  pl.BlockSpec((L, ATTN_HEADS * DH), lambda i: (cur(i), 0)),
                  pl.BlockSpec((ATTN_HEADS, L, 2 * L), lambda i: (0, 0, 0)),
                  pl.BlockSpec(memory_space=pltpu.SMEM)],
        out_specs=[pl.BlockSpec((L, ATTN_HEADS * DH), lambda i: (cur(i), 0)),
                   pl.BlockSpec((L, LANES), lambda i: (lag(i), 0)), pl.BlockSpec((L, LANES), lambda i: (lag(i), 0)),
                   _vec_spec(QKV_DIM), pl.BlockSpec((ATTN_HEADS, L, 2 * L), lambda i: (0, 0, 0)), _vec_spec(LANES)],
        scratch=[pltpu.VMEM((L, LANES), F32), pltpu.VMEM((L, LANES), F32),
                 pltpu.VMEM((2 * L, LANES), F32), pltpu.VMEM((2 * L, LANES), F32),
                 pltpu.VMEM((HPK * L, 2 * L), F32), pltpu.VMEM((HPK * L, 2 * L), F32),
                 pltpu.VMEM((HPK * L, 2 * L), BF16), pltpu.VMEM((HPK * L, 2 * L), BF16)],
        sem=("arbitrary",))(qkv, qkv, qkv, qkv, qkv, datt, bias, sinks)


def _pad_rows(a, mult):
    pad = (-a.shape[-2]) % mult
    if pad == 0:
        return a
    cfg = [(0, 0)] * (a.ndim - 2) + [(0, pad), (0, 0)]
    return jnp.pad(a, cfg)


class _Pack:
    def __init__(self, width, mult, total_mult):
        self.width, self.mult, self.total_mult = width, mult, total_mult
        self.entries = []
        self.rows = 0

    def add(self, key, shape):
        n = int(np.prod(shape))
        rows = -(-n // self.width)
        self.entries.append((key, self.rows, rows, tuple(shape)))
        self.rows += -(-rows // self.mult) * self.mult

    @property
    def total(self):
        return -(-self.rows // self.total_mult) * self.total_mult

    def pack(self, pieces, dtype, lead=()):
        parts = []
        for key, _, rows, shape in self.entries:
            a = pieces[key].astype(dtype).reshape(lead + (-1,))
            n = int(np.prod(shape))
            a = jnp.pad(a, [(0, 0)] * len(lead) + [(0, rows * self.width - n)])
            a = a.reshape(lead + (rows, self.width))
            parts.append(_pad_rows(a, self.mult))
        if self.total > self.rows:
            parts.append(jnp.zeros(lead + (self.total - self.rows, self.width), dtype))
        return jnp.concatenate(parts, axis=len(lead))

    def unpack(self, packed, lead=()):
        out = {}
        for key, off, rows, shape in self.entries:
            a = lax.slice_in_dim(packed, off, off + rows, axis=len(lead))
            a = a.reshape(lead + (-1,))
            n = int(np.prod(shape))
            out[key] = lax.slice_in_dim(a, 0, n, axis=len(lead)).reshape(lead + shape)
        return out


def _ffn_fwd(x, h, mod, wg_t, wu_t, wd, tag, next_norm=None, gather=None):
    side = None if gather is None else (_GatherOps, gather)
    gate, up, act, *gathered = _mm_swiglu(h, wg_t, wu_t, f"ffn_gateup_{tag}", side=side)
    x_out, ffn_out, *h_next = _mm_resid([(act, wd)], x, mod[5:6], name=f"ffn_down_{tag}", norm=next_norm)
    return (x_out, dict(h=h, gate=gate, up=up, act=act, out=ffn_out), *h_next, *gathered)


def _ffn_bwd(dx_out, dffn, x_in, saved, mod, norm_w, wg_t, wu_t, wd, below, tag, exchange=None):
    side = None if exchange is None else (_SiblingOps, exchange)
    dgate, dup, *theirs = _mm_swiglu_bwd(dffn, wd, saved["gate"], saved["up"], f"ffn_act_bwd_{tag}", side=side)
    d_wd = _mm_tn(saved["act"], dffn, name=f"ffn_dwd_{tag}")
    d_wg_t = _mm_tn(dgate, saved["h"], name=f"ffn_dwg_{tag}")
    d_wu_t = _mm_tn(dup, saved["h"], name=f"ffn_dwu_{tag}")
    side = None if exchange is None else (_ChipsOps, _pair_sum(exchange, theirs[0], f"grads_pair_sum_{tag}"))
    dh = _mm([(dgate, wg_t), (dup, wu_t)], "nn", name=f"ffn_dh_{tag}", side=side)
    dh, *from_chips = dh if exchange is not None else (dh,)
    dx, d_below, acc = _norm_mod_bwd(x_in, dh, dx_out, norm_w, mod[4:5], below[0], below[1], f"ffn_norm_bwd_{tag}")
    return (dx, d_below, dict(d_wg=d_wg_t, d_wu=d_wu_t, d_wd=d_wd, acc=acc), *from_chips)


_BIG = [
    ("out_w", "out_w_even", 0, "row"), ("qkv_w", "qkv_w", 0, "col"), ("o_w", "o_w", 0, "row"),
    ("gate0", "ffn_gate_w", 0, "col"), ("up0", "ffn_up_w", 0, "col"), ("down0", "ffn_down_w", 0, "row"),
    ("gate1", "ffn_gate_w", 1, "col"), ("up1", "ffn_up_w", 1, "col"), ("down1", "ffn_down_w", 1, "row"),
    ("in_w", "in_w_even", 0, "col"),
]


def _to_wire(a, kind):
    return a.T if kind == "col" else a


_GATHER_GROUPS = [["out_w", "in_w"], ["gate0", "up0", "down0"], ["qkv_w", "o_w"], ["gate1", "up1", "down1"]]
_GRAD_GROUPS = [["qkv_w", "o_w", "gate1", "up1", "down1"], ["out_w", "gate0", "up0", "down0"], ["in_w"]]

_REPLICATED = ["ada_b", "norm_mix_w", "norm_ffn_w", "conv_b", "dt_bias", "a_log", "d_skip", "ssm_norm_w", "gmlp_ln_w",
               "gmlp_ln_b", "gmlp_ws", "gmlp_bs", "sinks", "rel_table", "final_norm_w"]
_TINY_SHARDED = ["conv_w", "qkv_b", "o_b"]

_WEIGHTS = ['ada_w', 'ada_b', 'norm_mix_w', 'norm_ffn_w', 'in_w_even', 'conv_w', 'conv_b', 'dt_bias', 'a_log', 'd_skip',
            'ssm_norm_w', 'gmlp_ln_w', 'gmlp_ln_b', 'gmlp_ws', 'gmlp_bs', 'out_w_even', 'qkv_w', 'qkv_b', 'o_w', 'o_b',
            'sinks', 'rel_table', 'ffn_gate_w', 'ffn_up_w', 'ffn_down_w', 'final_norm_w']


def _step(x, c, loss_target, W, M, V):
    T = x.shape[1]
    x0 = x[0]
    target = loss_target[0]
    me = 4 * lax.axis_index("x") + 2 * lax.axis_index("y") + lax.axis_index("c")

    w_wire_local = {key: _to_wire(W[name][layer].astype(BF16), kind) for key, name, layer, kind in _BIG}

    def wire_pack(keys, mult):
        gp = _Pack(D, 1, mult)
        for key in keys:
            gp.add(key, w_wire_local[key].shape)
        return gp

    gather_packs = [(gp, gp.pack(w_wire_local, BF16)) for gp in (wire_pack(keys, 16) for keys in _GATHER_GROUPS)]
    grad_packs = [wire_pack(keys, 64) for keys in _GRAD_GROUPS]
    full = {}

    def gathered_weights(group, gathered):
        shards = gather_packs[group][0].unpack(gathered, lead=(N_DEV,))
        full.update({key: a.reshape(-1, D) for key, a in shards.items()})

    gathered_weights(0, _all_gather(gather_packs[0][1], "gather_weights"))

    small_in = _Pack(D, 8, 8)
    small_in.add("c", (1, D))
    small_in.add("conv_w", W["conv_w"][0].shape)
    small_in.add("qkv_b", W["qkv_b"][0].shape)
    small_in.add("o_b", W["o_b"][0].shape)
    sm = small_in.unpack(_all_gather(small_in.pack(
        dict(c=c, conv_w=W["conv_w"][0], qkv_b=W["qkv_b"][0], o_b=W["o_b"][0]), F32), "gather_small"), lead=(N_DEV,))
    c_all = sm["c"].reshape(N_DEV, D)
    conv_w_full = jnp.transpose(sm["conv_w"], (1, 0, 2)).reshape(SSM_CONV, CONV_DIM)
    qkv_b_full = sm["qkv_b"].reshape(1, QKV_DIM)
    o_b_full = sm["o_b"].reshape(1, D)

    ncol = W["ada_w"].shape[2]
    cond, mod_cols = _mod_matmul(c_all, W["ada_w"], "mod_matmul")
    mod_g = _all_gather(mod_cols.reshape(DEPTH * N_DEV, ncol), "gather_mod").reshape(N_DEV, DEPTH, N_DEV, ncol)
    mod_me = lax.dynamic_index_in_dim(mod_g, me, axis=2, keepdims=False)
    mod_me = jnp.transpose(mod_me, (1, 0, 2)).reshape(DEPTH, 6, D)
    mod_me = jnp.pad(mod_me, ((0, 0), (0, 2), (0, 0))).reshape(DEPTH * 8, D)
    ada_b_rows = jnp.pad(W["ada_b"].reshape(DEPTH, 6, D), ((0, 0), (0, 2), (0, 0))).reshape(DEPTH * 8, D)
    mod_all = _add_rows(mod_me, ada_b_rows, "mod_bias").reshape(DEPTH, 8, D)
    mod0, mod1 = mod_all[0], mod_all[1]

    in_t = full["in_w"]
    o1, o2, o3, o4 = SSM_INNER, SSM_INNER + CONV_DIM, SSM_INNER + CONV_DIM + SSM_HEADS, SSM_INNER + CONV_DIM + SSM_HEADS + GMLP_INNER
    w_z, w_xbc, w_dt, w_u, w_v = in_t[:o1], in_t[o1:o2], in_t[o2:o3], in_t[o3:o4], in_t[o4:]
    w_main = jnp.concatenate([w_z, w_u, w_v, w_xbc], axis=0)
    w_dtp = jnp.pad(w_dt, ((0, LANES - SSM_HEADS), (0, 0)))
    out_w = full["out_w"]

    pad16 = lambda a: jnp.pad(a.reshape(1, SSM_HEADS), ((0, 0), (0, LANES - SSM_HEADS)))
    dtb, alog = pad16(W["dt_bias"][0]), pad16(W["a_log"][0])
    dskip = jnp.repeat(W["d_skip"][0], SSM_HEAD_DIM).reshape(1, SSM_INNER)
    ssm_nw = W["ssm_norm_w"]
    lnw, lnb = W["gmlp_ln_w"], W["gmlp_ln_b"]
    ws = W["gmlp_ws"][0]
    bs_exp = jnp.repeat(W["gmlp_bs"][0].T, CHUNK, axis=1)
    conv_b = W["conv_b"]
    nmw, nfw = W["norm_mix_w"], W["norm_ffn_w"]
    onehot_t = _bucket_onehot_t()
    head_ind = _head_indicator()
    bias = _bias_from_table(W["rel_table"].T, onehot_t, "rel_bias").reshape(ATTN_HEADS, CHUNK, 2 * CHUNK)
    sinks = W["sinks"][0]

    h0 = _norm_mod(x0, nmw[0:1], mod0[1:2], mod0[0:1], "mix_norm_0")
    pm = _mm([(h0, w_main)], "nt", name="in_proj", tn_pref=1536, out_dtype=BF16)
    dtr = _mm([(h0, w_dtp)], "nt", name="in_proj_dt")
    cpre = _conv_fwd(pm, conv_w_full, conv_b, "conv_fwd")
    ya, ypre, sprev, gathered_b = _ssd_fwd(cpre, dtr, pm, dtb, alog, dskip, ssm_nw, head_ind, "ssd_fwd",
                                           side=(_GatherOps, gather_packs[1][1]))
    gathered_weights(1, gathered_b)
    yb, gathered_c = _gmlp_fwd(pm, lnw, lnb, ws, bs_exp, "gmlp_fwd", side=(_GatherOps, gather_packs[2][1]))
    gathered_weights(2, gathered_c)
    x1, mix0, hf0 = _mm_resid([(ya, out_w[:SSM_INNER]), (yb, out_w[SSM_INNER:])], x0, mod0[2:3], name="out_proj",
                              norm=(nfw[0:1], mod0[4:5], mod0[3:4]))
    x2, ffn0, h1, gathered_d = _ffn_fwd(x1, hf0, mod0, full["gate0"], full["up0"], full["down0"], "0",
                                        next_norm=(nmw[1:2], mod1[1:2], mod1[0:1]), gather=gather_packs[3][1])
    gathered_weights(3, gathered_d)
    qkv_t, o_w = full["qkv_w"], full["o_w"]
    w_q, w_k, w_v_att = qkv_t[:D], qkv_t[D:D + LANES], qkv_t[D + LANES:]

    qkv = _mm([(h1, qkv_t)], "nt", name="qkv_proj", bias=qkv_b_full, tn_pref=1280, out_dtype=BF16)
    att = _attn_fwd(qkv, bias, sinks, "attn_fwd")
    x3, mix1, hf1 = _mm_resid([(att, o_w)], x2, mod1[2:3], name="o_proj", bias=o_b_full,
                              norm=(nfw[1:2], mod1[4:5], mod1[3:4]))
    x4, ffn1 = _ffn_fwd(x3, hf1, mod1, full["gate1"], full["up1"], full["down1"], "1")

    dx4, dffn1, acc_f = _final_loss(x4, W["final_norm_w"].reshape(1, D), target, ffn1["out"], mod1[5:6], "final_loss")
    dx3, dmix1, gf1 = _ffn_bwd(dx4, dffn1, x3, ffn1, mod1, nfw[1:2], full["gate1"], full["up1"], full["down1"],
                               (mix1, mod1[2:3]), "1")
    datt = _mm([(dmix1, o_w)], "nt", name="o_proj_dx", out_dtype=BF16)
    d_o_w = _mm_tn(att, dmix1, name="o_proj_dw")
    dq, dk, dv, d_qkv_b, dbias, dsinks = _attn_bwd(qkv, datt, bias, sinks, "attn_bwd")
    d_table = _table_from_dbias(dbias.reshape(ATTN_HEADS, -1), onehot_t, "rel_table_grad")
    d_qkv_t = jnp.concatenate([_mm_tn(dq, h1, name="qkv_dw_q"), _mm_tn(dk, h1, name="qkv_dw_k"), _mm_tn(dv, h1, name="qkv_dw_v")], axis=0)
    dh1 = _mm([(dq, w_q), (dk, w_k), (dv, w_v_att)], "nn", name="qkv_proj_dx")
    dx2, dffn0, acc_n1 = _norm_mod_bwd(x2, dh1, dx3, nmw[1:2], mod1[1:2], ffn0["out"], mod0[5:6], "mix_norm_bwd_1")

    g_wire = dict(qkv_w=d_qkv_t, o_w=d_o_w, gate1=gf1["d_wg"], up1=gf1["d_wu"], down1=gf1["d_wd"])

    def packed_partials(group):
        return grad_packs[group].pack({key: g_wire[key].reshape(N_DEV, -1, D) for key in _GRAD_GROUPS[group]},
                                      F32, lead=(N_DEV,))

    from_chips = {}
    dx1, dmix0, gf0, from_chips[0] = _ffn_bwd(dx2, dffn0, x1, ffn0, mod0, nfw[0:1], full["gate0"], full["up0"], full["down0"],
                                              (mix0, mod0[2:3]), "0", exchange=packed_partials(0))

    dya = _mm([(dmix0, out_w[:SSM_INNER])], "nt", name="out_proj_dx_a")
    dyb = _mm([(dmix0, out_w[SSM_INNER:])], "nt", name="out_proj_dx_b")
    d_out_w = jnp.concatenate([_mm_tn(ya, dmix0, name="out_proj_dw_a"), _mm_tn(yb, dmix0, name="out_proj_dw_b")], axis=0)
    g_wire.update(gate0=gf0["d_wg"], up0=gf0["d_wu"], down0=gf0["d_wd"], out_w=d_out_w)
    partials_ffn0 = packed_partials(1)
    du, dvg, d_ws, d_bs, acc_ln, theirs_ffn0 = _gmlp_bwd(pm, dyb, lnw, lnb, ws, bs_exp, "gmlp_bwd",
                                                         side=(_SiblingOps, partials_ffn0))
    pair_ffn0 = _pair_sum(partials_ffn0, theirs_ffn0, "grads_pair_sum_mix")
    dz, dcpre, ddtr, acc_ssd, d_ssm_nw, from_chips[1] = _ssd_bwd(
        cpre, dtr, pm, ypre, sprev, dya, dtb, alog, dskip, ssm_nw, head_ind, "ssd_bwd", side=(_ChipsOps, pair_ffn0))
    dxbc, acc_conv = _conv_bwd(dcpre, pm, conv_w_full, "conv_bwd")
    d_in_t = jnp.concatenate([
        _mm_tn(dz, h0, name="in_dw_z"), _mm_tn(dxbc, h0, name="in_dw_xbc"),
        _mm_tn(ddtr, h0, name="in_dw_dt")[:SSM_HEADS], _mm_tn(du, h0, name="in_dw_u"), _mm_tn(dvg, h0, name="in_dw_v")], axis=0)
    g_wire.update(in_w=d_in_t)
    partials_mix = packed_partials(2)
    dh0, theirs_mix = _mm([(dz, w_z), (dxbc, w_xbc), (ddtr, w_dtp), (du, w_u), (dvg, w_v)], "nn", name="in_proj_dx",
                          side=(_SiblingOps, partials_mix))
    pair_mix = _pair_sum(partials_mix, theirs_mix, "grads_pair_sum")
    grad_x, acc_n0, from_chips[2] = _norm_mod_bwd(x0, dh0, dx1, nmw[0:1], mod0[1:2], None, None, "mix_norm_bwd_0",
                                                  side=(_ChipsOps, pair_mix))

    g_mine = {}
    for group in range(len(_GRAD_GROUPS)):
        g_mine.update(grad_packs[group].unpack(_sum_parts(from_chips[group], f"grads_chip_sum_{group}")))
    res_big = [{}, {}, {}, {}]
    for key, name, layer, kind in _BIG:
        g_nat = _to_wire(g_mine[key], kind)
        outs = _adamw(g_nat[None], W[name][layer], M[name][layer], V[name][layer], f"adamw_{key}")
        for res, out in zip(res_big, outs):
            res[key] = out

    acc_f0, acc_f1 = gf0["acc"], gf1["acc"]
    views = dict(ada_b=(DEPTH * 6, D), norm_mix_w=(DEPTH, D), norm_ffn_w=(DEPTH, D), conv_b=(1, CONV_DIM),
                 dt_bias=(1, SSM_HEADS), a_log=(1, SSM_HEADS), d_skip=(1, SSM_HEADS), ssm_norm_w=(1, D), gmlp_ln_w=(1, D),
                 gmlp_ln_b=(1, D), gmlp_ws=(GMLP_GROUPS * CHUNK, CHUNK), gmlp_bs=(GMLP_GROUPS, CHUNK),
                 sinks=(1, ATTN_HEADS), rel_table=(REL_BUCKETS, ATTN_HEADS), final_norm_w=(1, D),
                 conv_w=(SSM_CONV, CONV_DIM), qkv_b=(1, QKV_DIM), o_b=(1, D), loss=(1, D))
    canvas = _Canvas()
    for key, (r, cdim) in views.items():
        canvas.add(key, r, cdim, blocks=GMLP_GROUPS if key == "gmlp_ws" else 1)
    row = lambda i: slice(i, i + 1)
    sources = [
        (acc_n0, [("ada_b", row(1), 0), ("ada_b", row(0), 1), ("norm_mix_w", row(2), 0)]),
        (acc_f0, [("ada_b", row(3), 2), ("ada_b", row(1), 3), ("ada_b", row(0), 4), ("norm_ffn_w", row(2), 0)]),
        (acc_n1, [("ada_b", row(3), 5), ("ada_b", row(1), 6), ("ada_b", row(0), 7), ("norm_mix_w", row(2), 1)]),
        (acc_f1, [("ada_b", row(3), 8), ("ada_b", row(1), 9), ("ada_b", row(0), 10), ("norm_ffn_w", row(2), 1),
                  ("o_b", row(4), 0)]),
        (acc_f, [("ada_b", row(3), 11), ("final_norm_w", row(0), 0), ("loss", row(1), 0)]),
        (acc_conv, [("conv_w", slice(0, SSM_CONV), 0), ("conv_b", row(4), 0)]),
        (acc_ssd, [("dt_bias", row(0), 0), ("a_log", row(1), 0), ("d_skip", row(2), 0)]),
        (d_ssm_nw, [("ssm_norm_w", row(0), 0)]),
        (acc_ln, [("gmlp_ln_w", row(0), 0), ("gmlp_ln_b", row(1), 0)]),
        (d_ws.reshape(GMLP_GROUPS * CHUNK, CHUNK), [("gmlp_ws", slice(0, GMLP_GROUPS * CHUNK), 0)]),
        (d_bs.T, [("gmlp_bs", slice(0, GMLP_GROUPS), 0)]),
        (dsinks, [("sinks", row(0), 0)]),
        (d_table, [("rel_table", slice(0, REL_BUCKETS), 0)]),
        (d_qkv_b, [("qkv_b", row(0), 0)]),
    ]
    parts_small = _all_gather(_canvas_fill(canvas, sources, "small_grads_canvas"), "gather_small_grads")
    view = lambda a, key: a.reshape(views[key])
    params = [(name, view(W[name], name), view(M[name], name), view(V[name], name)) for name in _REPLICATED]
    small_out = _adamw_canvas(canvas, parts_small, params, _TINY_SHARDED + ["loss"], "adamw_small")
    loss = small_out["loss"][0][0, 0]
    res_small = [{name: small_out[name][k].reshape(W[name].shape) for name in _REPLICATED} for k in range(4)]

    n_cw, n_qb, n_ob = W["conv_w"].shape[2], W["qkv_b"].shape[1], W["o_b"].shape[1]
    g_tiny = dict(conv_w=lax.dynamic_slice_in_dim(small_out["conv_w"][0], me * n_cw, n_cw, axis=1)[None],
                  qkv_b=lax.dynamic_slice_in_dim(small_out["qkv_b"][0], me * n_qb, n_qb, axis=1),
                  o_b=lax.dynamic_slice_in_dim(small_out["o_b"][0], me * n_ob, n_ob, axis=1))
    tiny = _Pack(D, 8, 8)
    for name in _TINY_SHARDED:
        tiny.add(name, W[name].shape)
    pkt = lambda S: tiny.pack({name: S[name] for name in _TINY_SHARDED}, F32)
    res_tiny = [tiny.unpack(r) for r in _adamw(pkt(g_tiny)[None], pkt(W), pkt(M), pkt(V), "adamw_tiny")]

    dmod_all = parts_small[:, canvas.offset["ada_b"]:canvas.offset["ada_b"] + DEPTH * 6].reshape(N_DEV, DEPTH, 6 * D)
    dmod_cols = jnp.transpose(lax.dynamic_slice_in_dim(dmod_all, me * ncol, ncol, axis=2), (1, 0, 2))
    g_ada_w = _ada_w_grad(cond, dmod_cols, "ada_w_grad")
    flat = lambda a: a.reshape(DEPTH * D, ncol)
    res_ada = [r.reshape(DEPTH, D, ncol) for r in _adamw(flat(g_ada_w)[None], flat(W["ada_w"]), flat(M["ada_w"]), flat(V["ada_w"]), "adamw_ada_w")]

    def result(kind_idx, name):
        if name == "ada_w":
            return res_ada[kind_idx]
        if name in _REPLICATED:
            return res_small[kind_idx][name]
        if name in _TINY_SHARDED:
            return res_tiny[kind_idx][name]
        pieces = [res_big[kind_idx][key] for key, nm, layer, kind in _BIG if nm == name]
        return jnp.stack(pieces)

    outs = [loss, grad_x[None]]
    for kind_idx in range(4):
        outs += [result(kind_idx, name) for name in _WEIGHTS]
    return tuple(outs)


def kernel(x, c, ada_w, ada_b, norm_mix_w, norm_ffn_w, in_w_even, conv_w, conv_b, dt_bias, a_log, d_skip, ssm_norm_w, gmlp_ln_w, gmlp_ln_b, gmlp_ws, gmlp_bs, out_w_even, qkv_w, qkv_b, o_w, o_b, sinks, rel_table, ffn_gate_w, ffn_up_w, ffn_down_w, final_norm_w, loss_target, m_ada_w, m_ada_b, m_norm_mix_w, m_norm_ffn_w, m_in_w_even, m_conv_w, m_conv_b, m_dt_bias, m_a_log, m_d_skip, m_ssm_norm_w, m_gmlp_ln_w, m_gmlp_ln_b, m_gmlp_ws, m_gmlp_bs, m_out_w_even, m_qkv_w, m_qkv_b, m_o_w, m_o_b, m_sinks, m_rel_table, m_ffn_gate_w, m_ffn_up_w, m_ffn_down_w, m_final_norm_w, v_ada_w, v_ada_b, v_norm_mix_w, v_norm_ffn_w, v_in_w_even, v_conv_w, v_conv_b, v_dt_bias, v_a_log, v_d_skip, v_ssm_norm_w, v_gmlp_ln_w, v_gmlp_ln_b, v_gmlp_ws, v_gmlp_bs, v_out_w_even, v_qkv_w, v_qkv_b, v_o_w, v_o_b, v_sinks, v_rel_table, v_ffn_gate_w, v_ffn_up_w, v_ffn_down_w, v_final_norm_w):
    args = locals()
    W = {n: args[n] for n in _WEIGHTS}
    M = {n: args["m_" + n] for n in _WEIGHTS}
    V = {n: args["v_" + n] for n in _WEIGHTS}
    return _step(x, c, loss_target, W, M, V)
```

```python
import functools
import math

import numpy as np
import jax
import jax.numpy as jnp
from jax import lax
from jax.experimental import pallas as pl
from jax.experimental.pallas import tpu as pltpu

F32 = jnp.float32
BF16 = jnp.bfloat16
HIGHEST = lax.Precision.HIGHEST
MESH = pl.DeviceIdType.MESH

N_DEV = 8
D = 1024
DEPTH = 2
SSM_HEADS = 16
SSM_HEAD_DIM = 64
SSM_INNER = 1024
SSM_GROUPS = 2
SSM_STATE = 128
SSM_CONV = 4
CHUNK = 128
CONV_DIM = SSM_INNER + 2 * SSM_GROUPS * SSM_STATE
GMLP_GROUPS = 8
GMLP_INNER = 1024
IN_EVEN = 4624
ATTN_HEADS = 16
ATTN_KV = 2
ATTN_DH = 64
QKV_DIM = 1280
REL_BUCKETS = 32
REL_MAX_DIST = 128
FFN = 2816
EPS = 1e-6
NEG_INF = -1e30
LANES = 128

ADAM_LR = 0.001
ADAM_B1 = 0.9
ADAM_B2 = 0.999
ADAM_EPS = 1e-08
ADAM_WD = 0.01
ADAM_STEP = 10

VMEM_LIMIT_BYTES = 56 * 1024 * 1024
ROW_TILE = 512


def _pcall(body, *, name, out_shape, grid=(), in_specs=None, out_specs=None, scratch=(), sem=None):
    params = dict(vmem_limit_bytes=VMEM_LIMIT_BYTES)
    if sem is not None:
        params["dimension_semantics"] = sem
    specs = {} if in_specs is None else dict(in_specs=in_specs, out_specs=out_specs)
    return pl.pallas_call(
        body, name=name, out_shape=out_shape, grid=grid, **specs,
        scratch_shapes=list(scratch), compiler_params=pltpu.CompilerParams(**params))


def _call(body, args, side=None, *, name, out_shape, grid, in_specs, out_specs, scratch=(), sem=None):
    if side is None:
        return _pcall(body, name=name, out_shape=out_shape, grid=grid, in_specs=in_specs, out_specs=out_specs,
                      scratch=scratch, sem=sem)(*args)
    ops_cls, x = side
    n_in, n_out, n_scr = len(in_specs), len(out_shape), len(scratch)
    steps = int(np.prod(grid))
    hbm = pl.BlockSpec(memory_space=pl.ANY)

    def wrapped(*refs):
        ins, x_ref = refs[:n_in], refs[n_in]
        outs, r_ref = refs[n_in + 1:n_in + 1 + n_out], refs[n_in + 1 + n_out]
        scr, sems = refs[n_in + 2 + n_out:n_in + 2 + n_out + n_scr], refs[n_in + 2 + n_out + n_scr:]
        ops = ops_cls(x_ref, r_ref, *sems)
        step = pl.program_id(0)
        for axis in range(1, len(grid)):
            step = step * grid[axis] + pl.program_id(axis)
        pl.when(step == 0)(ops.start)
        body(*ins, *outs, *scr)
        pl.when(step == (3 * steps) // 4)(ops.forward)
        pl.when(step == steps - 1)(ops.finish)

    return _pcall(
        wrapped, name=name, out_shape=list(out_shape) + [ops_cls.result(x)], grid=grid,
        in_specs=list(in_specs) + [hbm], out_specs=list(out_specs) + [hbm],
        scratch=list(scratch) + ops_cls.scratch(), sem=("arbitrary",) * len(grid))(*args, x)


def _tile(n, pref):
    if n <= pref:
        return n
    best = None
    for t in range(LANES, pref + 1, LANES):
        if n % t == 0:
            best = t
    assert best is not None, (n, pref)
    return best


def _rows(T):
    return min(ROW_TILE, T)


def _sds(shape, dtype=F32):
    return jax.ShapeDtypeStruct(shape, dtype)


def _row_spec(tm, c, col=0):
    return pl.BlockSpec((tm, c), lambda i, col=col: (i, col))


def _vec_spec(c, r=1):
    return pl.BlockSpec((r, c), lambda i: (0, 0))


def _sigmoid(x):
    return jax.nn.sigmoid(x)


def _silu(x):
    return x * _sigmoid(x)


def _dsilu(x):
    s = _sigmoid(x)
    return s * (1.0 + x * (1.0 - s))


def _gelu(x):
    return 0.5 * x * (1.0 + lax.erf(x * 0.7071067811865476))


def _dgelu(x):
    return 0.5 * (1.0 + lax.erf(x * 0.7071067811865476)) + x * jnp.exp(-0.5 * x * x) * 0.3989422804014327


def _dot(a, b, dims, precision=None):
    return lax.dot_general(a, b, (dims, ((), ())), precision=precision, preferred_element_type=F32)


def _nn(a, b, precision=None):
    return _dot(a, b, ((1,), (0,)), precision)


def _nt(a, b, precision=None):
    return _dot(a, b, ((1,), (1,)), precision)


def _tn(a, b, precision=None):
    return _dot(a, b, ((0,), (0,)), precision)


def _bf(x):
    return x.astype(BF16)


def _colsum(x):
    return jnp.sum(x, axis=0, keepdims=True)


def _rowsum(x):
    return jnp.sum(x, axis=1, keepdims=True)


def _allsum(x):
    return _colsum(_rowsum(x))


def _comm_call(ops_cls, x, name, from_vmem=False):
    def body(x_ref, out_ref, *sems):
        ops = ops_cls(x_ref, out_ref, *sems)
        ops.start()
        ops.forward()
        ops.finish()

    return pl.pallas_call(
        body, name=name, out_shape=ops_cls.result(x),
        in_specs=[pl.BlockSpec(memory_space=pltpu.VMEM if from_vmem else pl.ANY)],
        out_specs=pl.BlockSpec(memory_space=pl.ANY), scratch_shapes=ops_cls.scratch(),
    )(x)


def _all_gather(x, name):
    return _comm_call(_GatherOps, x, name, from_vmem=True)


class _GatherOps:
    def __init__(self, x_ref, out_ref, send_sems, recv_sems, local_sem):
        self.x_ref, self.out_ref = x_ref, out_ref
        self.send_sems, self.recv_sems, self.local_sem = send_sems, recv_sems, local_sem
        mx, my, mc = lax.axis_index("x"), lax.axis_index("y"), lax.axis_index("c")
        self.mc = mc
        self.me, self.sibling = (mx, my, mc), (mx, my, 1 - mc)
        self.chips = [(1 - mx, my), (mx, 1 - my), (1 - mx, 1 - my)]

    @staticmethod
    def result(x):
        return _sds((N_DEV,) + x.shape, x.dtype)

    @staticmethod
    def scratch():
        return [pltpu.SemaphoreType.DMA((7,)), pltpu.SemaphoreType.DMA((7,)), pltpu.SemaphoreType.DMA(())]

    def _slot(self, px, py, pc):
        return self.out_ref.at[4 * px + 2 * py + pc]

    def _copy(self, k, block, to, own=False):
        return pltpu.make_async_remote_copy(
            src_ref=self.x_ref if own else self._slot(*block), dst_ref=self._slot(*block),
            send_sem=self.send_sems.at[k], recv_sem=self.recv_sems.at[k], device_id=to, device_id_type=MESH)

    def _mine(self):
        return pltpu.make_async_copy(self.x_ref, self._slot(*self.me), self.local_sem)

    def _first(self):
        return [self._copy(0, self.me, self.sibling, own=True)] + [
            self._copy(1 + j, self.me, (*chip, self.mc), own=True) for j, chip in enumerate(self.chips)]

    def _passed(self):
        return [self._copy(4 + j, (*chip, self.mc), self.sibling) for j, chip in enumerate(self.chips)]

    def start(self):
        self._mine().start()
        for cp in self._first():
            cp.start()

    def forward(self):
        passed = self._passed()
        for j, chip in enumerate(self.chips):
            self._copy(1 + j, (*chip, self.mc), self.me).wait_recv()
            passed[j].start()

    def finish(self):
        self._copy(0, self.sibling, self.me).wait_recv()
        for j, chip in enumerate(self.chips):
            self._copy(4 + j, (*chip, 1 - self.mc), self.me).wait_recv()
        for cp in self._first() + self._passed():
            cp.wait_send()
        self._mine().wait()


N_CHIP = 4


class _SiblingOps:
    def __init__(self, p_ref, theirs_ref, send_sems, recv_sems):
        self.p_ref, self.theirs_ref, self.send_sems, self.recv_sems = p_ref, theirs_ref, send_sems, recv_sems

    @staticmethod
    def result(p):
        return _sds((N_CHIP,) + p.shape[1:], p.dtype)

    @staticmethod
    def scratch():
        return [pltpu.SemaphoreType.DMA((N_CHIP,))] * 2

    def _copies(self):
        mx, my, mc = lax.axis_index("x"), lax.axis_index("y"), lax.axis_index("c")
        return [pltpu.make_async_remote_copy(
            src_ref=self.p_ref.at[2 * chip + 1 - mc], dst_ref=self.theirs_ref.at[chip],
            send_sem=self.send_sems.at[chip], recv_sem=self.recv_sems.at[chip],
            device_id=(mx, my, 1 - mc), device_id_type=MESH) for chip in range(N_CHIP)]

    def start(self):
        for cp in self._copies():
            cp.start()

    def forward(self):
        pass

    def finish(self):
        for cp in self._copies():
            cp.wait()


class _ChipsOps:
    def __init__(self, q_ref, out_ref, send_sems, recv_sems, local_sem):
        self.q_ref, self.out_ref = q_ref, out_ref
        self.send_sems, self.recv_sems, self.local_sem = send_sems, recv_sems, local_sem

    @staticmethod
    def result(q):
        return _sds(q.shape, q.dtype)

    @staticmethod
    def scratch():
        return [pltpu.SemaphoreType.DMA((N_CHIP - 1,)), pltpu.SemaphoreType.DMA((N_CHIP - 1,)), pltpu.SemaphoreType.DMA(())]

    def _copies(self):
        mx, my, mc = lax.axis_index("x"), lax.axis_index("y"), lax.axis_index("c")
        me = 2 * mx + my
        local = pltpu.make_async_copy(self.q_ref.at[me], self.out_ref.at[me], self.local_sem)
        remote = []
        for r in range(1, N_CHIP):
            px = 1 - mx if r & 2 else mx
            py = 1 - my if r & 1 else my
            remote.append(pltpu.make_async_remote_copy(
                src_ref=self.q_ref.at[2 * px + py], dst_ref=self.out_ref.at[me],
                send_sem=self.send_sems.at[r - 1], recv_sem=self.recv_sems.at[r - 1],
                device_id=(px, py, mc), device_id_type=MESH))
        return local, remote

    def start(self):
        local, remote = self._copies()
        local.start()
        for cp in remote:
            cp.start()

    def forward(self):
        pass

    def finish(self):
        local, remote = self._copies()
        for cp in remote:
            cp.wait()
        local.wait()


def _pair_sum(p, theirs, name):
    n, R, C = theirs.shape
    tr = _tile_rows(R, 256)

    def body(p_ref, t_ref, o_ref):
        mc = lax.axis_index("c")
        o_ref[0] = (p_ref[0, mc] + t_ref[0]).astype(BF16)

    blk = pl.BlockSpec((1, tr, C), lambda s, i: (s, i, 0))
    return _pcall(body, name=name, out_shape=_sds((n, R, C), BF16), grid=(n, R // tr),
                  in_specs=[pl.BlockSpec((1, 2, tr, C), lambda s, i: (s, 0, i, 0)), blk],
                  out_specs=blk, sem=("parallel", "parallel"))(p.reshape(n, 2, R, C), theirs)


def _sum_parts(parts, name):
    P, R, C = parts.shape
    tr = _tile_rows(R, 256)

    def body(p_ref, o_ref):
        g = p_ref[0].astype(F32)
        for k in range(1, P):
            g = g + p_ref[k].astype(F32)
        o_ref[...] = g

    return _pcall(body, name=name, out_shape=_sds((R, C)), grid=(R // tr,),
                  in_specs=[pl.BlockSpec((P, tr, C), lambda i: (0, i, 0))],
                  out_specs=pl.BlockSpec((tr, C), lambda i: (i, 0)), sem=("parallel",))(parts)


def _adamw(parts, w, m, v, name):
    P, R, C = parts.shape
    tr = R if R <= 256 else _tile_rows(R, 256)

    def body(p_ref, w_ref, m_ref, v_ref, g_ref, d_ref, nm_ref, nv_ref):
        g = p_ref[0]
        for k in range(1, P):
            g = g + p_ref[k]
        nm = ADAM_B1 * m_ref[...] + (1.0 - ADAM_B1) * g
        nv = ADAM_B2 * v_ref[...] + (1.0 - ADAM_B2) * (g * g)
        m_hat = nm / (1.0 - ADAM_B1 ** ADAM_STEP)
        v_hat = nv / (1.0 - ADAM_B2 ** ADAM_STEP)
        g_ref[...] = g
        d_ref[...] = -ADAM_LR * (m_hat / (jnp.sqrt(v_hat) + ADAM_EPS) + ADAM_WD * w_ref[...])
        nm_ref[...] = nm
        nv_ref[...] = nv

    blk = pl.BlockSpec((tr, C), lambda i: (i, 0))
    return _pcall(
        body, name=name, out_shape=[_sds((R, C))] * 4, grid=(R // tr,),
        in_specs=[pl.BlockSpec((P, tr, C), lambda i: (0, i, 0)), blk, blk, blk],
        out_specs=[blk] * 4, sem=("parallel",))(parts, w, m, v)


def _adam_update(g, w, m, v):
    nm = ADAM_B1 * m + (1.0 - ADAM_B1) * g
    nv = ADAM_B2 * v + (1.0 - ADAM_B2) * (g * g)
    m_hat = nm / (1.0 - ADAM_B1 ** ADAM_STEP)
    v_hat = nv / (1.0 - ADAM_B2 ** ADAM_STEP)
    return -ADAM_LR * (m_hat / (jnp.sqrt(v_hat) + ADAM_EPS) + ADAM_WD * w), nm, nv


class _Canvas:
    def __init__(self):
        self.offset, self.views, self.rows = {}, {}, 0

    def add(self, key, r, c, blocks=1):
        need = r // blocks if blocks > 1 else r * (-(-c // D))
        if need >= 8:
            self.rows = -(-self.rows // 8) * 8
        self.offset[key], self.views[key] = self.rows, (r, c, blocks)
        self.rows += need

    @property
    def total(self):
        return -(-self.rows // 8) * 8

    def cells(self, key):
        (r, c, blocks), off = self.views[key], self.offset[key]
        if blocks > 1:
            n = r // blocks
            return [(off, n, b * c, c, slice(b * n, (b + 1) * n), slice(0, c)) for b in range(blocks)]
        if c <= D:
            return [(off, r, 0, c, slice(0, r), slice(0, c))]
        chunks = -(-c // D)
        return [(off + i * chunks + j, 1, 0, min(D, c - j * D), slice(i, i + 1), slice(j * D, min(c, (j + 1) * D)))
                for i in range(r) for j in range(chunks)]


def _canvas_fill(canvas, sources, name):
    arrays = [a for a, _ in sources]

    def body(*refs):
        out_ref = refs[-1]
        out_ref[...] = jnp.zeros_like(out_ref)
        for ref, (_, items) in zip(refs[:-1], sources):
            for key, src_rows, view_row in items:
                n = src_rows.stop - src_rows.start
                for row, count, lane, width, vrows, vcols in canvas.cells(key):
                    lo, hi = max(vrows.start, view_row), min(vrows.stop, view_row + n)
                    if lo < hi:
                        src = slice(src_rows.start + lo - view_row, src_rows.start + hi - view_row)
                        out_ref[row + lo - vrows.start:row + hi - vrows.start, lane:lane + width] = ref[src, vcols]

    return _pcall(body, name=name, out_shape=_sds((canvas.total, D)))(*arrays)


def _adamw_canvas(canvas, parts, params, sum_only, name):
    n = len(params)

    def body(p_ref, *refs):
        sum_ref = refs[-1]
        g_all = p_ref[0]
        for k in range(1, N_DEV):
            g_all = g_all + p_ref[k]
        sum_ref[...] = g_all
        for i, (key, _, _, _) in enumerate(params):
            w_ref, m_ref, v_ref = refs[3 * i:3 * i + 3]
            outs = refs[3 * n + 4 * i:3 * n + 4 * i + 4]
            for row, count, lane, width, vrows, vcols in canvas.cells(key):
                g = sum_ref[row:row + count, lane:lane + width]
                delta, nm, nv = _adam_update(g, w_ref[vrows, vcols], m_ref[vrows, vcols], v_ref[vrows, vcols])
                for ref, val in zip(outs, (g, delta, nm, nv)):
                    ref[vrows, vcols] = val
        for i, key in enumerate(sum_only):
            for row, count, lane, width, vrows, vcols in canvas.cells(key):
                refs[7 * n + i][vrows, vcols] = sum_ref[row:row + count, lane:lane + width]

    args = [a for _, w, m, v in params for a in (w, m, v)]
    out_shape = [_sds(w.shape) for _, w, _, _ in params for _ in range(4)] + [_sds(canvas.views[k][:2]) for k in sum_only]
    res = _pcall(body, name=name, out_shape=out_shape, scratch=[pltpu.VMEM(parts.shape[1:], F32)])(parts, *args)
    out = {key: res[4 * i:4 * i + 4] for i, (key, _, _, _) in enumerate(params)}
    out.update({key: [res[4 * n + i]] for i, key in enumerate(sum_only)})
    return out


def _tile_rows(n, pref):
    best = None
    for t in range(8, pref + 1, 8):
        if n % t == 0:
            best = t
    assert best is not None, (n, pref)
    return best


def _mm(pairs, mode, *, name, out_dtype=F32, bias=None, tn_pref=1024, side=None):
    M = pairs[0][0].shape[0]
    N = pairs[0][1].shape[1] if mode == "nn" else pairs[0][1].shape[0]
    tm, tn = _rows(M), _tile(N, tn_pref)
    n_pairs = len(pairs)
    has_bias = bias is not None

    def body(*refs):
        acc = _pairs_dot(refs[:2 * n_pairs], mode)
        if has_bias:
            acc = acc + refs[2 * n_pairs][...]
        refs[-1][...] = acc.astype(refs[-1].dtype)

    in_specs, args = _pair_specs(pairs, mode, tm, tn)
    if has_bias:
        in_specs.append(pl.BlockSpec((1, tn), lambda j, i: (0, j)))
        args.append(bias)
    res = _call(body, args, side, name=name, out_shape=[_sds((M, N), out_dtype)], grid=(N // tn, M // tm),
                in_specs=in_specs, out_specs=[pl.BlockSpec((tm, tn), lambda j, i: (i, j))], sem=("parallel", "parallel"))
    return res[0] if side is None else (res[0], res[1])


def _pairs_dot(ab, mode):
    acc = None
    for p in range(len(ab) // 2):
        a, b = _bf(ab[2 * p][...]), _bf(ab[2 * p + 1][...])
        d = _nn(a, b) if mode == "nn" else _nt(a, b)
        acc = d if acc is None else acc + d
    return acc


def _pair_specs(pairs, mode, tm, tn):
    in_specs, args = [], []
    for a, b in pairs:
        K = a.shape[1]
        in_specs.append(pl.BlockSpec((tm, K), lambda j, i: (i, 0)))
        if mode == "nn":
            in_specs.append(pl.BlockSpec((K, tn), lambda j, i: (0, j)))
        else:
            in_specs.append(pl.BlockSpec((tn, K), lambda j, i: (j, 0)))
        args += [a, b]
    return in_specs, args


def _mm_resid(pairs, resid, gvec, *, name, bias=None, norm=None):
    M, N = resid.shape
    tm = _rows(M)
    n_pairs = len(pairs)
    has_bias, has_norm = bias is not None, norm is not None

    def body(*refs):
        acc = _pairs_dot(refs[:2 * n_pairs], "nn")
        pos = 2 * n_pairs
        if has_bias:
            acc = acc + refs[pos][...]
            pos += 1
        xv = refs[pos][...] + refs[pos + 1][...] * acc
        outs = refs[pos + 2 + 3 * has_norm:]
        outs[0][...] = xv
        outs[1][...] = acc.astype(BF16)
        if has_norm:
            w_ref, sc_ref, sh_ref = refs[pos + 2:pos + 5]
            r = lax.rsqrt(jnp.mean(xv * xv, axis=-1, keepdims=True) + EPS)
            outs[2][...] = ((xv * r * w_ref[...]) * (1.0 + sc_ref[...]) + sh_ref[...]).astype(BF16)

    in_specs, args = _pair_specs(pairs, "nn", tm, N)
    vec = pl.BlockSpec((1, N), lambda j, i: (0, 0))
    row = pl.BlockSpec((tm, N), lambda j, i: (i, 0))
    if has_bias:
        in_specs.append(vec)
        args.append(bias)
    in_specs += [row, vec] + [vec] * (3 * has_norm)
    args += [resid, gvec] + (list(norm) if has_norm else [])
    return _pcall(body, name=name, out_shape=[_sds((M, N)), _sds((M, N), BF16)] + [_sds((M, N), BF16)] * has_norm,
                  grid=(1, M // tm), in_specs=in_specs, out_specs=[row] * (2 + has_norm),
                  sem=("parallel", "parallel"))(*args)


def _mm_tn(a, b, *, name, tm_pref=1408, tn_pref=1536):
    K, M = a.shape
    N = b.shape[1]
    tm, tn = _tile(M, tm_pref), _tile(N, tn_pref)
    tk = K if K <= 2 * ROW_TILE else 2 * ROW_TILE

    def body(a_ref, b_ref, o_ref):
        @pl.when(pl.program_id(2) == 0)
        def _():
            o_ref[...] = jnp.zeros_like(o_ref)
        o_ref[...] += _tn(_bf(a_ref[...]), _bf(b_ref[...]))

    return _pcall(
        body, name=name, out_shape=_sds((M, N)), grid=(M // tm, N // tn, K // tk),
        in_specs=[pl.BlockSpec((tk, tm), lambda i, j, k: (k, i)), pl.BlockSpec((tk, tn), lambda i, j, k: (k, j))],
        out_specs=pl.BlockSpec((tm, tn), lambda i, j, k: (i, j)),
        sem=("parallel", "parallel", "arbitrary"))(a, b)


def _mm_swiglu(h, wg_t, wu_t, name, side=None):
    M, K = h.shape
    N = wg_t.shape[0]
    tm, tn = _rows(M), _tile(N, 1408)

    def body(h_ref, wg_ref, wu_ref, gate_ref, up_ref, act_ref):
        hv = _bf(h_ref[...])
        gate = _nt(hv, wg_ref[...])
        up = _nt(hv, wu_ref[...])
        gate_ref[...] = gate.astype(BF16)
        up_ref[...] = up.astype(BF16)
        act_ref[...] = (_silu(gate) * up).astype(BF16)

    w_spec = pl.BlockSpec((tn, K), lambda j, i: (j, 0))
    o_spec = pl.BlockSpec((tm, tn), lambda j, i: (i, j))
    return _call(body, (h, wg_t, wu_t), side, name=name,
                 out_shape=[_sds((M, N), BF16)] * 3, grid=(N // tn, M // tm),
                 in_specs=[pl.BlockSpec((tm, K), lambda j, i: (i, 0)), w_spec, w_spec], out_specs=[o_spec] * 3,
                 sem=("parallel", "parallel"))


def _mm_swiglu_bwd(dout, wd, gate, up, name, side=None):
    M, K = dout.shape
    N = wd.shape[0]
    tm, tn = _rows(M), _tile(N, 1408)

    def body(d_ref, wd_ref, gate_ref, up_ref, dg_ref, du_ref):
        dact = _nt(_bf(d_ref[...]), wd_ref[...])
        g = gate_ref[...].astype(F32)
        dg_ref[...] = (dact * up_ref[...].astype(F32) * _dsilu(g)).astype(BF16)
        du_ref[...] = (dact * _silu(g)).astype(BF16)

    t_spec = pl.BlockSpec((tm, tn), lambda j, i: (i, j))
    return _call(
        body, (dout, wd, gate, up), side, name=name, out_shape=[_sds((M, N), BF16)] * 2, grid=(N // tn, M // tm),
        in_specs=[pl.BlockSpec((tm, K), lambda j, i: (i, 0)), pl.BlockSpec((tn, K), lambda j, i: (j, 0)), t_spec, t_spec],
        out_specs=[t_spec] * 2, sem=("parallel", "parallel"))


def _norm_mod(x, w, sc, sh, name):
    T = x.shape[0]
    tm = _rows(T)

    def body(x_ref, w_ref, sc_ref, sh_ref, o_ref):
        xv = x_ref[...]
        r = lax.rsqrt(jnp.mean(xv * xv, axis=-1, keepdims=True) + EPS)
        o_ref[...] = ((xv * r * w_ref[...]) * (1.0 + sc_ref[...]) + sh_ref[...]).astype(BF16)

    return _pcall(body, name=name, out_shape=_sds((T, D), BF16), grid=(T // tm,),
                  in_specs=[_row_spec(tm, D), _vec_spec(D), _vec_spec(D), _vec_spec(D)],
                  out_specs=_row_spec(tm, D), sem=("parallel",))(x, w, sc, sh)


def _gate_rows(dxv, br_ref, g_ref, db_ref, acc_ref):
    db = g_ref[...] * dxv
    db_ref[...] = db.astype(BF16)
    acc_ref[3:4, :] += _colsum(dxv * br_ref[...].astype(F32))
    acc_ref[4:5, :] += _colsum(db)


def _norm_mod_bwd(x, dh, dres, w, sc, branch, g, name, side=None):
    T = x.shape[0]
    tm = _rows(T)
    gated = branch is not None

    def body(x_ref, dh_ref, dres_ref, w_ref, sc_ref, *rest):
        (br_ref, g_ref, dx_ref, db_ref, acc_ref) = rest if gated else (None, None, rest[0], None, rest[1])

        @pl.when(pl.program_id(0) == 0)
        def _():
            acc_ref[...] = jnp.zeros_like(acc_ref)
        xv, dh_v, wv = x_ref[...], dh_ref[...], w_ref[...]
        r = lax.rsqrt(jnp.mean(xv * xv, axis=-1, keepdims=True) + EPS)
        n = xv * r
        dnw = dh_v * (1.0 + sc_ref[...])
        dn = dnw * wv
        dxv = dres_ref[...] + r * (dn - n * jnp.mean(dn * n, axis=-1, keepdims=True))
        dx_ref[...] = dxv
        acc_ref[0:1, :] += _colsum(dh_v * (n * wv))
        acc_ref[1:2, :] += _colsum(dh_v)
        acc_ref[2:3, :] += _colsum(dnw * n)
        if gated:
            _gate_rows(dxv, br_ref, g_ref, db_ref, acc_ref)

    row = _row_spec(tm, D)
    args = (x, dh, dres, w, sc) + ((branch, g) if gated else ())
    return _call(body, args, side, name=name,
                 out_shape=[_sds((T, D))] + ([_sds((T, D), BF16)] if gated else []) + [_sds((8, D))], grid=(T // tm,),
                 in_specs=[row, row, row, _vec_spec(D), _vec_spec(D)] + ([row, _vec_spec(D)] if gated else []),
                 out_specs=[row] + ([row] if gated else []) + [_vec_spec(D, 8)], sem=("arbitrary",))


def _final_loss(x, wf, target, branch, g, name):
    T = x.shape[0]
    tm = _rows(T)

    def body(x_ref, w_ref, t_ref, br_ref, g_ref, dx_ref, db_ref, acc_ref):
        @pl.when(pl.program_id(0) == 0)
        def _():
            acc_ref[...] = jnp.zeros_like(acc_ref)
        xv, wv = x_ref[...], w_ref[...]
        r = lax.rsqrt(jnp.mean(xv * xv, axis=-1, keepdims=True) + EPS)
        n = xv * r
        err = n * wv - t_ref[...]
        dy = err * (1.0 / D)
        dn = dy * wv
        dxv = r * (dn - n * jnp.mean(dn * n, axis=-1, keepdims=True))
        dx_ref[...] = dxv
        acc_ref[0:1, :] += _colsum(dy * n)
        acc_ref[1:2, :] += jnp.broadcast_to(_allsum(err * err) * (0.5 / D), (1, D))
        _gate_rows(dxv, br_ref, g_ref, db_ref, acc_ref)

    row = _row_spec(tm, D)
    return _pcall(body, name=name, out_shape=[_sds((T, D)), _sds((T, D), BF16), _sds((8, D))], grid=(T // tm,),
                  in_specs=[row, _vec_spec(D), row, row, _vec_spec(D)],
                  out_specs=[row, row, _vec_spec(D, 8)], sem=("arbitrary",))(x, wf, target, branch, g)


def _mod_matmul(c_all, ada_w, name):
    n = ada_w.shape[2]

    def body(c_ref, w_ref, cond_ref, o_ref):
        cond = _silu(c_ref[...])
        cond_ref[...] = cond
        o_ref[0] = _nn(cond, w_ref[0])

    return _pcall(body, name=name, out_shape=[_sds((N_DEV, D)), _sds((DEPTH, N_DEV, n))], grid=(DEPTH,),
                  in_specs=[pl.BlockSpec((N_DEV, D), lambda l: (0, 0)), pl.BlockSpec((1, D, n), lambda l: (l, 0, 0))],
                  out_specs=[pl.BlockSpec((N_DEV, D), lambda l: (0, 0)), pl.BlockSpec((1, N_DEV, n), lambda l: (l, 0, 0))],
                  sem=("arbitrary",))(c_all, ada_w)


def _add_rows(a, b, name):
    def body(a_ref, b_ref, o_ref):
        o_ref[...] = a_ref[...] + b_ref[...]

    return _pcall(body, name=name, out_shape=_sds(a.shape))(a, b)


def _ada_w_grad(cond, dmod_cols, name):
    n = dmod_cols.shape[2]

    def body(c_ref, d_ref, o_ref):
        o_ref[0] = _tn(c_ref[...], d_ref[0])

    return _pcall(body, name=name, out_shape=_sds((DEPTH, D, n)), grid=(DEPTH,),
                  in_specs=[pl.BlockSpec((N_DEV, D), lambda l: (0, 0)), pl.BlockSpec((1, N_DEV, n), lambda l: (l, 0, 0))],
                  out_specs=pl.BlockSpec((1, D, n), lambda l: (l, 0, 0)), sem=("parallel",))(cond, dmod_cols)


def _conv_fwd(pm, conv_w, conv_b, name):
    T = pm.shape[0]
    tm = _rows(T)
    C = CONV_DIM

    def body(x_ref, prev_ref, w_ref, b_ref, o_ref):
        cur = x_ref[...].astype(F32)
        prev = jnp.where(pl.program_id(0) > 0, prev_ref[...].astype(F32)[8:16], 0.0)
        cur8 = cur[0:8]
        row8 = lax.broadcasted_iota(jnp.int32, (8, C), 0)
        full = w_ref[3:4, :] * cur
        head = w_ref[3:4, :] * cur8
        for k in range(1, SSM_CONV):
            wk = w_ref[3 - k:4 - k, :]
            full = full + wk * pltpu.roll(cur, k, 0)
            head = head + wk * jnp.where(row8 < k, pltpu.roll(prev, k, 0), pltpu.roll(cur8, k, 0))
        o_ref[...] = full + b_ref[...]
        o_ref[0:8, :] = head + b_ref[...]

    return _pcall(
        body, name=name, out_shape=_sds((T, C)), grid=(T // tm,),
        in_specs=[pl.BlockSpec((tm, C), lambda i: (i, 2)),
                  pl.BlockSpec((16, C), lambda i: (jnp.maximum(i * (tm // 16) - 1, 0), 2)),
                  _vec_spec(C, SSM_CONV), _vec_spec(C)],
        out_specs=_row_spec(tm, C), sem=("parallel",))(pm, pm, conv_w, conv_b)


def _conv_bwd(dc, pm, conv_w, name):
    T = dc.shape[0]
    tm = _rows(T)
    C = CONV_DIM
    nt = T // tm

    def body(dc_ref, nxt_ref, x_ref, prev_ref, w_ref, dx_ref, acc_ref):
        i = pl.program_id(0)

        @pl.when(i == 0)
        def _():
            acc_ref[...] = jnp.zeros_like(acc_ref)
        dcv = dc_ref[...]
        nxt = jnp.where(i < nt - 1, nxt_ref[...], 0.0)
        xc = x_ref[...].astype(F32)
        prev = jnp.where(i > 0, prev_ref[...].astype(F32)[8:16], 0.0)
        dc8h, dc8t, x8 = dcv[0:8], dcv[tm - 8:tm], xc[0:8]
        row8 = lax.broadcasted_iota(jnp.int32, (8, C), 0)
        full = w_ref[3:4, :] * dcv
        tail = w_ref[3:4, :] * dc8t
        acc_ref[3:4, :] += _colsum(dcv * xc)
        for k in range(1, SSM_CONV):
            wk = w_ref[3 - k:4 - k, :]
            full = full + wk * pltpu.roll(dcv, tm - k, 0)
            tail = tail + wk * jnp.where(row8 + k >= 8, pltpu.roll(nxt, 8 - k, 0), pltpu.roll(dc8t, 8 - k, 0))
            xs_head = jnp.where(row8 < k, pltpu.roll(prev, k, 0), pltpu.roll(x8, k, 0))
            prod = dcv * pltpu.roll(xc, k, 0)
            acc_ref[3 - k:4 - k, :] += _colsum(prod) - _colsum(prod[0:8]) + _colsum(dc8h * xs_head)
        acc_ref[4:5, :] += _colsum(dcv)
        dx_ref[...] = jnp.concatenate([full[0:tm - 8], tail], axis=0).astype(BF16)

    return _pcall(
        body, name=name, out_shape=[_sds((T, C), BF16), _sds((8, C))], grid=(nt,),
        in_specs=[_row_spec(tm, C),
                  pl.BlockSpec((8, C), lambda i: (jnp.minimum((i + 1) * (tm // 8), T // 8 - 1), 0)),
                  pl.BlockSpec((tm, C), lambda i: (i, 2)),
                  pl.BlockSpec((16, C), lambda i: (jnp.maximum(i * (tm // 16) - 1, 0), 2)),
                  _vec_spec(C, SSM_CONV)],
        out_specs=[_row_spec(tm, C), _vec_spec(C, 8)], sem=("arbitrary",))(dc, dc, pm, pm, conv_w)


def _ssd_prologue(cpre, dtr, dtb, alog):
    L = CHUNK
    xc = _silu(cpre)
    pre = dtr + dtb
    dt = jnp.maximum(pre, 0.0) + jnp.log1p(jnp.exp(-jnp.abs(pre)))
    a = -jnp.exp(alog)
    la = dt * a
    row = lax.broadcasted_iota(jnp.int32, (L, L), 0)
    col = lax.broadcasted_iota(jnp.int32, (L, L), 1)
    causal = row >= col
    tri = causal.astype(F32)
    lc = _nn(tri, la, HIGHEST)
    return xc, pre, dt, a, causal, tri, lc, row, col


def _head_indicator():
    m = np.zeros((LANES, SSM_INNER), np.float32)
    for h in range(SSM_HEADS):
        m[h, h * SSM_HEAD_DIM:(h + 1) * SSM_HEAD_DIM] = 1.0
    return jnp.asarray(m, dtype=BF16)


def _split_dot(x, ind, dims):
    hi = x.astype(BF16)
    lo = (x - hi.astype(F32)).astype(BF16)
    return _dot(hi, ind, dims) + _dot(lo, ind, dims)


def _expand(x16, ind):
    return _split_dot(x16, ind, ((1,), (0,)))


def _headsum(x, ind, single_pass=False):
    if single_pass:
        return _dot(x.astype(BF16), ind, ((1,), (1,)))
    return _split_dot(x, ind, ((1,), (1,)))


def _ssd_fwd(cpre, dtr, pm, dtb, alog, dskip, normw, ind, name, side=None):
    T = cpre.shape[0]
    nc = T // CHUNK
    L, P, H, HPG, N = CHUNK, SSM_HEAD_DIM, SSM_HEADS, SSM_HEADS // SSM_GROUPS, SSM_STATE
    half = SSM_INNER // SSM_GROUPS

    def body(cp_ref, dtr_ref, z_ref, dtb_ref, alog_ref, dskip_ref, nw_ref, ind_ref, ya_ref, y_ref, sp_ref, st_ref):
        @pl.when(pl.program_id(0) == 0)
        def _():
            st_ref[...] = jnp.zeros_like(st_ref)
        xc, _, dt, _, causal, _, lc, _, _ = _ssd_prologue(cp_ref[...], dtr_ref[...], dtb_ref[...], alog_ref[...])
        lct = lc.T
        ind = ind_ref[...]
        llast = lc[L - 1:L, :]
        xs = xc[:, :SSM_INNER]
        xd = xs * _expand(dt, ind)
        ex = _expand(jnp.exp(lc), ind)
        xd_end = _bf(xd * _expand(jnp.exp(llast - lc), ind))
        cdx = _expand(jnp.broadcast_to(jnp.exp(llast), (8, LANES)), ind)[0:1]
        xdb = _bf(xd)
        sp_ref[0] = st_ref[...]
        for g in range(SSM_GROUPS):
            sl = slice(g * half, (g + 1) * half)
            bm = _bf(xc[:, SSM_INNER + g * N:SSM_INNER + (g + 1) * N])
            cm = _bf(xc[:, SSM_INNER + (SSM_GROUPS + g) * N:SSM_INNER + (SSM_GROUPS + g + 1) * N])
            cb = _nt(cm, bm)
            st = st_ref[g]
            y_ref[:, sl] = ex[:, sl] * _nn(cm, _bf(st)) + dskip_ref[:, sl] * xs[:, sl]
            st_ref[g] = st * cdx[:, sl] + _tn(bm, xd_end[:, sl])
            for j in range(HPG):
                h = g * HPG + j
                decay = jnp.where(causal, jnp.exp(jnp.where(causal, lc[:, h:h + 1] - lct[h:h + 1, :], 0.0)), 0.0)
                y_ref[:, h * P:(h + 1) * P] += _nn(_bf(cb * decay), xdb[:, h * P:(h + 1) * P])
        y2 = y_ref[...] * _silu(z_ref[...].astype(F32))
        for g in range(SSM_GROUPS):
            yg = y2[:, g * half:(g + 1) * half]
            r = lax.rsqrt(jnp.mean(yg * yg, axis=-1, keepdims=True) + EPS)
            ya_ref[:, g * half:(g + 1) * half] = (yg * r * nw_ref[:, g * half:(g + 1) * half]).astype(BF16)

    return _call(
        body, (cpre, dtr, pm, dtb, alog, dskip, normw, ind), side, name=name,
        out_shape=[_sds((T, SSM_INNER), BF16), _sds((T, SSM_INNER)), _sds((nc, SSM_GROUPS, N, half))], grid=(nc,),
        in_specs=[_row_spec(L, CONV_DIM), _row_spec(L, LANES), _row_spec(L, SSM_INNER, 0),
                  _vec_spec(LANES), _vec_spec(LANES), _vec_spec(SSM_INNER), _vec_spec(SSM_INNER), _vec_spec(SSM_INNER, LANES)],
        out_specs=[_row_spec(L, SSM_INNER), _row_spec(L, SSM_INNER),
                   pl.BlockSpec((1, SSM_GROUPS, N, half), lambda i: (i, 0, 0, 0))],
        scratch=[pltpu.VMEM((SSM_GROUPS, N, half), F32)], sem=("arbitrary",))


def _ssd_bwd(cpre, dtr, pm, ypre, sprev, dya, dtb, alog, dskip, normw, ind, name, side=None):
    T = cpre.shape[0]
    nc = T // CHUNK
    L, P, H, HPG, N = CHUNK, SSM_HEAD_DIM, SSM_HEADS, SSM_HEADS // SSM_GROUPS, SSM_STATE
    half = SSM_INNER // SSM_GROUPS

    def body(cp_ref, dtr_ref, z_ref, y_ref, sp_ref, dya_ref, dtb_ref, alog_ref, dskip_ref, nw_ref, ind_ref,
             dz_ref, dcp_ref, ddtr_ref, acc_ref, dnw_ref, ds_ref, dy_ref, dxd_ref, rr_ref, yoff_ref, dcd_ref):
        @pl.when(pl.program_id(0) == 0)
        def _():
            ds_ref[...] = jnp.zeros_like(ds_ref)
            acc_ref[...] = jnp.zeros_like(acc_ref)
            dnw_ref[...] = jnp.zeros_like(dnw_ref)
        cpre_v = cp_ref[...]
        xc, pre, dt, a, causal, tri, lc, row, col = _ssd_prologue(cpre_v, dtr_ref[...], dtb_ref[...], alog_ref[...])
        lct = lc.T
        zv, yv = z_ref[...].astype(F32), y_ref[...]
        sz = _silu(zv)
        y2 = yv * sz
        dya_v = dya_ref[...]
        nwv = nw_ref[...]
        for g in range(SSM_GROUPS):
            sl = slice(g * half, (g + 1) * half)
            yg = y2[:, sl]
            r = lax.rsqrt(jnp.mean(yg * yg, axis=-1, keepdims=True) + EPS)
            nrm = yg * r
            dnw_ref[:, sl] += _colsum(dya_v[:, sl] * nrm)
            dn = dya_v[:, sl] * nwv[:, sl]
            dy2 = r * (dn - nrm * jnp.mean(dn * nrm, axis=-1, keepdims=True))
            dy_ref[:, sl] = dy2 * sz[:, sl]
            dz_ref[:, sl] = (dy2 * yv[:, sl] * _dsilu(zv[:, sl])).astype(BF16)
        ind = ind_ref[...]
        llast = lc[L - 1:L, :]
        dte16 = jnp.exp(llast - lc)
        cd16 = jnp.exp(llast)
        xs = xc[:, :SSM_INNER]
        dtx = _expand(dt, ind)
        ex = _expand(jnp.exp(lc), ind)
        dtex = _expand(dte16, ind)
        cdx = _expand(jnp.broadcast_to(cd16, (8, LANES)), ind)[0:1]
        xd = xs * dtx
        xdb = _bf(xd)
        xd_end = _bf(xd * dtex)
        dyv = dy_ref[...]
        dy_off = _bf(ex * dyv)
        dyb = _bf(dyv)
        dskx = dskip_ref[...]
        lane_c = lax.broadcasted_iota(jnp.int32, (L, LANES), 1)
        lane1 = lax.broadcasted_iota(jnp.int32, (1, LANES), 1)
        sub16 = lax.broadcasted_iota(jnp.int32, (H, L), 0)
        dlc_c = jnp.zeros((L, LANES), F32)
        dlc_r = jnp.zeros((H, L), F32)
        for g in range(SSM_GROUPS):
            sl = slice(g * half, (g + 1) * half)
            b_lo = SSM_INNER + g * N
            c_lo = SSM_INNER + (SSM_GROUPS + g) * N
            bm, cm = _bf(xc[:, b_lo:b_lo + N]), _bf(xc[:, c_lo:c_lo + N])
            cb = _nt(cm, bm)
            st, dst = sp_ref[0, g], ds_ref[g]
            stb, dstb = _bf(st), _bf(dst)
            dcm = _nt(dy_off[:, sl], stb)
            ds_ref[g] = _tn(cm, dy_off[:, sl]) + dst * cdx[:, sl]
            rr_ref[:, sl] = _nn(bm, dstb)
            yoff_ref[:, sl] = ex[:, sl] * _nn(cm, stb)
            db = _nt(xd_end[:, sl], dstb)
            dcd_ref[:, sl] = _colsum(dst * st)
            dcb = jnp.zeros((L, L), F32)
            for j in range(HPG):
                h = g * HPG + j
                hs = slice(h * P, (h + 1) * P)
                decay = jnp.where(causal, jnp.exp(jnp.where(causal, lc[:, h:h + 1] - lct[h:h + 1, :], 0.0)), 0.0)
                m = cb * decay
                dxd_ref[:, hs] = _tn(_bf(m), dyb[:, hs])
                dm = _nt(dyb[:, hs], xdb[:, hs])
                dcb = dcb + dm * decay
                gm = dm * m
                dlc_c = dlc_c + jnp.where(lane_c == h, _rowsum(gm), 0.0)
                dlc_r = dlc_r + jnp.where(sub16 == h, _colsum(gm), 0.0)
            dcbb = _bf(dcb)
            dcp_ref[:, c_lo:c_lo + N] = dcm + _nn(dcbb, bm)
            dcp_ref[:, b_lo:b_lo + N] = db + _tn(dcbb, cm)
        dxd_diag, rr = dxd_ref[...], rr_ref[...]
        tt = _headsum(rr * xd, ind, single_pass=True) * dte16
        dlc_rt = jnp.concatenate([dlc_r, jnp.zeros((LANES - H, L), F32)], axis=0).T
        dlc = dlc_c - dlc_rt + _headsum(dyv * yoff_ref[...], ind, single_pass=True) - tt
        dcd = _headsum(jnp.broadcast_to(dcd_ref[...], (8, SSM_INNER)), ind)[0:1]
        dlc = dlc + jnp.where(row == L - 1, _colsum(tt) + dcd * cd16, 0.0)
        dla = _tn(tri, dlc, HIGHEST)
        dxd = dxd_diag + dtex * rr
        ddt = _headsum(dxd * xs, ind, single_pass=True) + dla * a
        ddtr = jnp.where(lane_c < H, ddt * _sigmoid(pre), 0.0)
        ddtr_ref[...] = ddtr
        acc_ref[0:1, :] += _colsum(ddtr)
        acc_ref[1:2, :] += jnp.where(lane1 < H, _colsum(dla * dt) * a, 0.0)
        acc_ref[2:3, :] += _headsum(jnp.broadcast_to(_colsum(dyv * xs), (8, SSM_INNER)), ind)[0:1]
        dcp_ref[:, 0:SSM_INNER] = dxd * dtx + dskx * dyv
        dcp_ref[...] = dcp_ref[...] * _dsilu(cpre_v)

    rev = lambda i: (nc - 1 - i, 0)
    rspec = lambda c: pl.BlockSpec((L, c), rev)
    return _call(
        body, (cpre, dtr, pm, ypre, sprev, dya, dtb, alog, dskip, normw, ind), side, name=name,
        out_shape=[_sds((T, SSM_INNER), BF16), _sds((T, CONV_DIM)), _sds((T, LANES)), _sds((8, LANES)), _sds((1, SSM_INNER))],
        grid=(nc,),
        in_specs=[rspec(CONV_DIM), rspec(LANES), rspec(SSM_INNER), rspec(SSM_INNER),
                  pl.BlockSpec((1, SSM_GROUPS, N, half), lambda i: (nc - 1 - i, 0, 0, 0)), rspec(SSM_INNER),
                  _vec_spec(LANES), _vec_spec(LANES), _vec_spec(SSM_INNER), _vec_spec(SSM_INNER), _vec_spec(SSM_INNER, LANES)],
        out_specs=[rspec(SSM_INNER), rspec(CONV_DIM), rspec(LANES), _vec_spec(LANES, 8), _vec_spec(SSM_INNER)],
        scratch=[pltpu.VMEM((SSM_GROUPS, N, half), F32), pltpu.VMEM((L, SSM_INNER), F32), pltpu.VMEM((L, SSM_INNER), F32),
                 pltpu.VMEM((L, SSM_INNER), F32), pltpu.VMEM((L, SSM_INNER), F32), pltpu.VMEM((1, SSM_INNER), F32)],
        sem=("arbitrary",))


def _gmlp_common(u, v, lnw, lnb):
    ug = _gelu(u)
    vg = _gelu(v)
    mu = jnp.mean(vg, axis=-1, keepdims=True)
    cen = vg - mu
    rstd = lax.rsqrt(jnp.mean(cen * cen, axis=-1, keepdims=True) + EPS)
    vhat = cen * rstd
    return ug, rstd, vhat, vhat * lnw + lnb


def _causal_mask():
    row = lax.broadcasted_iota(jnp.int32, (CHUNK, CHUNK), 0)
    col = lax.broadcasted_iota(jnp.int32, (CHUNK, CHUNK), 1)
    return row >= col


def _gmlp_fwd(pm, lnw, lnb, ws, bs_exp, name, side=None):
    T = pm.shape[0]
    nc = T // CHUNK
    L, G = CHUNK, GMLP_GROUPS

    def body(u_ref, v_ref, lnw_ref, lnb_ref, ws_ref, bs_ref, o_ref):
        ug, _, _, vn = _gmlp_common(u_ref[...].astype(F32), v_ref[...].astype(F32), lnw_ref[...], lnb_ref[...])
        causal = _causal_mask()
        for g in range(G):
            sl = slice(g * L, (g + 1) * L)
            wm = _bf(jnp.where(causal, ws_ref[g], 0.0))
            sv = _nn(wm, _bf(vn[:, sl])) + bs_ref[:, sl]
            o_ref[:, sl] = (ug[:, sl] * sv).astype(BF16)

    return _call(
        body, (pm, pm, lnw, lnb, ws, bs_exp), side, name=name, out_shape=[_sds((T, GMLP_INNER), BF16)], grid=(nc,),
        in_specs=[_row_spec(L, GMLP_INNER, 1), _row_spec(L, GMLP_INNER, 2), _vec_spec(GMLP_INNER), _vec_spec(GMLP_INNER),
                  pl.BlockSpec((G, L, L), lambda i: (0, 0, 0)), _vec_spec(GMLP_INNER, L)],
        out_specs=[_row_spec(L, GMLP_INNER)], sem=("parallel",))


def _gmlp_bwd(pm, dyb, lnw, lnb, ws, bs_exp, name, side=None):
    T = pm.shape[0]
    nc = T // CHUNK
    L, G = CHUNK, GMLP_GROUPS

    def body(u_ref, v_ref, dy_ref, lnw_ref, lnb_ref, ws_ref, bs_ref, du_ref, dv_ref, dws_ref, dbs_ref, acc_ref, dvn_ref):
        @pl.when(pl.program_id(0) == 0)
        def _():
            dws_ref[...] = jnp.zeros_like(dws_ref)
            dbs_ref[...] = jnp.zeros_like(dbs_ref)
            acc_ref[...] = jnp.zeros_like(acc_ref)
        uv, vv, dyv, lnwv = u_ref[...].astype(F32), v_ref[...].astype(F32), dy_ref[...], lnw_ref[...]
        ug, rstd, vhat, vn = _gmlp_common(uv, vv, lnwv, lnb_ref[...])
        causal = _causal_mask()
        lane = lax.broadcasted_iota(jnp.int32, (L, LANES), 1)
        dbs = jnp.zeros((L, LANES), F32)
        for g in range(G):
            sl = slice(g * L, (g + 1) * L)
            wm = _bf(jnp.where(causal, ws_ref[g], 0.0))
            vng = _bf(vn[:, sl])
            sv = _nn(wm, vng) + bs_ref[:, sl]
            du_ref[:, sl] = (dyv[:, sl] * sv * _dgelu(uv[:, sl])).astype(BF16)
            dsv = dyv[:, sl] * ug[:, sl]
            dsvb = _bf(dsv)
            dws_ref[g] += jnp.where(causal, _nt(dsvb, vng), 0.0)
            dbs = dbs + jnp.where(lane == g, _rowsum(dsv), 0.0)
            dvn_ref[:, sl] = _tn(wm, dsvb)
        dbs_ref[...] += dbs
        dvn = dvn_ref[...]
        acc_ref[0:1, :] += _colsum(dvn * vhat)
        acc_ref[1:2, :] += _colsum(dvn)
        dvh = dvn * lnwv
        dvg = rstd * (dvh - jnp.mean(dvh, axis=-1, keepdims=True) - vhat * jnp.mean(dvh * vhat, axis=-1, keepdims=True))
        dv_ref[...] = (dvg * _dgelu(vv)).astype(BF16)

    return _call(
        body, (pm, pm, dyb, lnw, lnb, ws, bs_exp), side, name=name,
        out_shape=[_sds((T, GMLP_INNER), BF16), _sds((T, GMLP_INNER), BF16), _sds((G, L, L)), _sds((L, LANES)), _sds((8, GMLP_INNER))],
        grid=(nc,),
        in_specs=[_row_spec(L, GMLP_INNER, 1), _row_spec(L, GMLP_INNER, 2), _row_spec(L, GMLP_INNER),
                  _vec_spec(GMLP_INNER), _vec_spec(GMLP_INNER), pl.BlockSpec((G, L, L), lambda i: (0, 0, 0)),
                  _vec_spec(GMLP_INNER, L)],
        out_specs=[_row_spec(L, GMLP_INNER), _row_spec(L, GMLP_INNER), pl.BlockSpec((G, L, L), lambda i: (0, 0, 0)),
                   _vec_spec(LANES, L), _vec_spec(GMLP_INNER, 8)],
        scratch=[pltpu.VMEM((L, GMLP_INNER), F32)], sem=("arbitrary",))


def _rel_buckets():
    qi = np.arange(CHUNK)[:, None]
    sj = np.arange(2 * CHUNK)[None, :]
    dist = np.maximum(qi + CHUNK - sj, 0)
    max_exact = REL_BUCKETS // 2
    log_ratio = (np.log(np.maximum(dist, 1).astype(np.float32) / np.float32(max_exact))
                 / np.float32(math.log(REL_MAX_DIST / max_exact))).astype(np.float32)
    large = max_exact + (log_ratio * np.float32(REL_BUCKETS - max_exact)).astype(np.int32)
    return np.where(dist < max_exact, dist, np.minimum(large, REL_BUCKETS - 1))


def _bucket_onehot_t():
    bucket = _rel_buckets().reshape(-1)
    return jnp.asarray((np.arange(REL_BUCKETS)[:, None] == bucket[None, :]).astype(np.float32))


def _bias_from_table(table_t, onehot_t, name):
    def body(t_ref, o_ref, out_ref):
        out_ref[...] = _nn(t_ref[...], o_ref[...], HIGHEST)

    return _pcall(body, name=name, out_shape=_sds((ATTN_HEADS, onehot_t.shape[1])))(table_t, onehot_t)


def _table_from_dbias(dbias, onehot_t, name):
    def body(d_ref, o_ref, out_ref):
        out_ref[...] = _nt(o_ref[...], d_ref[...], HIGHEST)

    return _pcall(body, name=name, out_shape=_sds((REL_BUCKETS, ATTN_HEADS)))(dbias, onehot_t)


def _softmax_sink(logits, sink, mask):
    logits = jnp.where(mask, logits, NEG_INF)
    mx = jnp.maximum(jnp.max(logits, axis=-1, keepdims=True), sink)
    e = jnp.exp(logits - mx)
    es = jnp.exp(sink - mx)
    inv = 1.0 / (_rowsum(e) + es)
    return e * inv, es * inv


def _attn_mask(n, heads):
    qi = lax.broadcasted_iota(jnp.int32, (heads * CHUNK, 2 * CHUNK), 0) & (CHUNK - 1)
    sj = lax.broadcasted_iota(jnp.int32, (heads * CHUNK, 2 * CHUNK), 1)
    rel = qi + CHUNK - sj
    return (rel >= 0) & (rel < CHUNK) & ((sj >= CHUNK) | (n > 0))


def _stack_heads(ref, first, count, width):
    return jnp.concatenate([_bf(ref[:, (first + j) * width:(first + j + 1) * width]) for j in range(count)], axis=0)


def _attn_fwd(qkv, bias, sinks, name):
    T = qkv.shape[0]
    nb = T // CHUNK
    L, DH, HPK = CHUNK, ATTN_DH, ATTN_HEADS // ATTN_KV
    scale = DH ** -0.5
    kcol, vcol = ATTN_HEADS * DH // LANES, ATTN_HEADS * DH // LANES + 1

    def body(q_ref, k_ref, v_ref, kp_ref, vp_ref, bias_ref, sink_ref, o_ref, lg_ref, p_ref):
        n = pl.program_id(0)
        mask = _attn_mask(n, 1)
        kband = _bf(jnp.concatenate([kp_ref[...], k_ref[...]], axis=0))
        vband = _bf(jnp.concatenate([vp_ref[...], v_ref[...]], axis=0))
        for kv in range(ATTN_KV):
            lg_ref[...] = _nt(_stack_heads(q_ref, kv * HPK, HPK, DH), kband[:, kv * DH:(kv + 1) * DH])
            for j in range(HPK):
                h = kv * HPK + j
                p, _ = _softmax_sink(lg_ref[j * L:(j + 1) * L, :] * scale + bias_ref[h], sink_ref[h], mask)
                p_ref[j * L:(j + 1) * L, :] = _bf(p)
            og = _nn(p_ref[...], vband[:, kv * DH:(kv + 1) * DH])
            for j in range(HPK):
                h = kv * HPK + j
                o_ref[:, h * DH:(h + 1) * DH] = og[j * L:(j + 1) * L].astype(BF16)

    prev = lambda i: jnp.maximum(i - 1, 0)
    return _pcall(
        body, name=name, out_shape=_sds((T, ATTN_HEADS * DH), BF16), grid=(nb,),
        in_specs=[_row_spec(L, ATTN_HEADS * DH, 0), _row_spec(L, LANES, kcol), _row_spec(L, LANES, vcol),
                  pl.BlockSpec((L, LANES), lambda i: (prev(i), kcol)), pl.BlockSpec((L, LANES), lambda i: (prev(i), vcol)),
                  pl.BlockSpec((ATTN_HEADS, L, 2 * L), lambda i: (0, 0, 0)),
                  pl.BlockSpec(memory_space=pltpu.SMEM)],
        out_specs=_row_spec(L, ATTN_HEADS * DH),
        scratch=[pltpu.VMEM((HPK * L, 2 * L), F32), pltpu.VMEM((HPK * L, 2 * L), BF16)],
        sem=("parallel",))(qkv, qkv, qkv, qkv, qkv, bias, sinks)


def _attn_bwd(qkv, datt, bias, sinks, name):
    T = qkv.shape[0]
    nb = T // CHUNK
    L, DH, HPK = CHUNK, ATTN_DH, ATTN_HEADS // ATTN_KV
    scale = DH ** -0.5
    kcol, vcol = ATTN_HEADS * DH // LANES, ATTN_HEADS * DH // LANES + 1

    def body(q_ref, k_ref, v_ref, kp_ref, vp_ref, do_ref, bias_ref, sink_ref,
             dq_ref, dk_ref, dv_ref, bsum_ref, dbias_ref, dsink_ref, pend_k, pend_v, band_k, band_v, lg_ref, dp_ref, p_ref, dl_ref):
        n = pl.program_id(0)

        @pl.when(n == 0)
        def _():
            dbias_ref[...] = jnp.zeros_like(dbias_ref)
            dsink_ref[...] = jnp.zeros_like(dsink_ref)
            bsum_ref[...] = jnp.zeros_like(bsum_ref)

        def emit_kv(dk, dv):
            dk_ref[...] = dk.astype(BF16)
            dv_ref[...] = dv.astype(BF16)
            bsum_ref[:, ATTN_HEADS * DH:ATTN_HEADS * DH + LANES] += _colsum(dk)
            bsum_ref[:, ATTN_HEADS * DH + LANES:] += _colsum(dv)

        @pl.when(n < nb)
        def _():
            mask = _attn_mask(n, 1)
            kband = _bf(jnp.concatenate([kp_ref[...], k_ref[...]], axis=0))
            vband = _bf(jnp.concatenate([vp_ref[...], v_ref[...]], axis=0))
            lane1 = lax.broadcasted_iota(jnp.int32, (1, LANES), 1)
            dsink = jnp.zeros((1, LANES), F32)
            for kv in range(ATTN_KV):
                kb, vb = kband[:, kv * DH:(kv + 1) * DH], vband[:, kv * DH:(kv + 1) * DH]
                qg = _stack_heads(q_ref, kv * HPK, HPK, DH)
                dog = _stack_heads(do_ref, kv * HPK, HPK, DH)
                lg_ref[...] = _nt(qg, kb)
                dp_ref[...] = _nt(dog, vb)
                for j in range(HPK):
                    h = kv * HPK + j
                    rows = slice(j * L, (j + 1) * L)
                    p, ps = _softmax_sink(lg_ref[rows, :] * scale + bias_ref[h], sink_ref[h], mask)
                    dp = dp_ref[rows, :]
                    delta = _rowsum(p * dp)
                    dl = p * (dp - delta)
                    dbias_ref[h] += dl
                    p_ref[rows, :] = _bf(p)
                    dl_ref[rows, :] = _bf(dl)
                    dsink = dsink + jnp.where(lane1 == h, -_colsum(ps * delta), 0.0)
                band_v[:, kv * DH:(kv + 1) * DH] = _tn(p_ref[...], dog)
                dqg = _nn(dl_ref[...], kb) * scale
                band_k[:, kv * DH:(kv + 1) * DH] = _tn(dl_ref[...], qg) * scale
                for j in range(HPK):
                    h = kv * HPK + j
                    dq_ref[:, h * DH:(h + 1) * DH] = dqg[j * L:(j + 1) * L].astype(BF16)
                    bsum_ref[:, h * DH:(h + 1) * DH] += _colsum(dqg[j * L:(j + 1) * L])
            dsink_ref[...] += dsink

            @pl.when(n > 0)
            def _():
                emit_kv(pend_k[...] + band_k[0:L, :], pend_v[...] + band_v[0:L, :])
            pend_k[...] = band_k[L:2 * L, :]
            pend_v[...] = band_v[L:2 * L, :]

        @pl.when(n == nb)
        def _():
            emit_kv(pend_k[...], pend_v[...])

    cur = lambda i: jnp.minimum(i, nb - 1)
    prev = lambda i: jnp.maximum(jnp.minimum(i, nb - 1) - 1, 0)
    lag = lambda i: jnp.maximum(i - 1, 0)
    return _pcall(
        body, name=name,
        out_shape=[_sds((T, ATTN_HEADS * DH), BF16), _sds((T, LANES), BF16), _sds((T, LANES), BF16), _sds((1, QKV_DIM)),
                   _sds((ATTN_HEADS, L, 2 * L)), _sds((1, LANES))],
        grid=(nb + 1,),
        in_specs=[pl.BlockSpec((L, ATTN_HEADS * DH), lambda i: (cur(i), 0)),
                  pl.BlockSpec((L, LANES), lambda i: (cur(i), kcol)), pl.BlockSpec((L, LANES), lambda i: (cur(i), vcol)),
                  pl.BlockSpec((L, LANES), lambda i: (prev(i), kcol)), pl.BlockSpec((L, LANES), lambda i: (prev(i), vcol)),
                  pl.BlockSpec((L, ATTN_HEADS * DH), lambda i: (cur(i), 0)),
                  pl.BlockSpec((ATTN_HEADS, L, 2 * L), lambda i: (0, 0, 0)),
                  pl.BlockSpec(memory_space=pltpu.SMEM)],
        out_specs=[pl.BlockSpec((L, ATTN_HEADS * DH), lambda i: (cur(i), 0)),
                   pl.BlockSpec((L, LANES), lambda i: (lag(i), 0)), pl.BlockSpec((L, LANES), lambda i: (lag(i), 0)),
                   _vec_spec(QKV_DIM), pl.BlockSpec((ATTN_HEADS, L, 2 * L), lambda i: (0, 0, 0)), _vec_spec(LANES)],
        scratch=[pltpu.VMEM((L, LANES), F32), pltpu.VMEM((L, LANES), F32),
                 pltpu.VMEM((2 * L, LANES), F32), pltpu.VMEM((2 * L, LANES), F32),
                 pltpu.VMEM((HPK * L, 2 * L), F32), pltpu.VMEM((HPK * L, 2 * L), F32),
                 pltpu.VMEM((HPK * L, 2 * L), BF16), pltpu.VMEM((HPK * L, 2 * L), BF16)],
        sem=("arbitrary",))(qkv, qkv, qkv, qkv, qkv, datt, bias, sinks)


def _pad_rows(a, mult):
    pad = (-a.shape[-2]) % mult
    if pad == 0:
        return a
    cfg = [(0, 0)] * (a.ndim - 2) + [(0, pad), (0, 0)]
    return jnp.pad(a, cfg)


class _Pack:
    def __init__(self, width, mult, total_mult):
        self.width, self.mult, self.total_mult = width, mult, total_mult
        self.entries = []
        self.rows = 0

    def add(self, key, shape):
        n = int(np.prod(shape))
        rows = -(-n // self.width)
        self.entries.append((key, self.rows, rows, tuple(shape)))
        self.rows += -(-rows // self.mult) * self.mult

    @property
    def total(self):
        return -(-self.rows // self.total_mult) * self.total_mult

    def pack(self, pieces, dtype, lead=()):
        parts = []
        for key, _, rows, shape in self.entries:
            a = pieces[key].astype(dtype).reshape(lead + (-1,))
            n = int(np.prod(shape))
            a = jnp.pad(a, [(0, 0)] * len(lead) + [(0, rows * self.width - n)])
            a = a.reshape(lead + (rows, self.width))
            parts.append(_pad_rows(a, self.mult))
        out = jnp.concatenate(parts, axis=len(lead))
        return _pad_rows(out, self.total_mult)

    def unpack(self, packed, lead=()):
        out = {}
        for key, off, rows, shape in self.entries:
            a = lax.slice_in_dim(packed, off, off + rows, axis=len(lead))
            a = a.reshape(lead + (-1,))
            n = int(np.prod(shape))
            out[key] = lax.slice_in_dim(a, 0, n, axis=len(lead)).reshape(lead + shape)
        return out


def _ffn_fwd(x, h, mod, wg_t, wu_t, wd, tag, next_norm=None, gather=None):
    side = None if gather is None else (_GatherOps, gather)
    gate, up, act, *gathered = _mm_swiglu(h, wg_t, wu_t, f"ffn_gateup_{tag}", side=side)
    x_out, ffn_out, *h_next = _mm_resid([(act, wd)], x, mod[5:6], name=f"ffn_down_{tag}", norm=next_norm)
    return (x_out, dict(h=h, gate=gate, up=up, act=act, out=ffn_out), *h_next, *gathered)


def _ffn_bwd(dx_out, dffn, x_in, saved, mod, norm_w, wg_t, wu_t, wd, below, tag, exchange=None):
    side = None if exchange is None else (_ChipsOps, exchange)
    dgate, dup, *from_chips = _mm_swiglu_bwd(dffn, wd, saved["gate"], saved["up"], f"ffn_act_bwd_{tag}", side=side)
    d_wd = _mm_tn(saved["act"], dffn, name=f"ffn_dwd_{tag}")
    d_wg_t = _mm_tn(dgate, saved["h"], name=f"ffn_dwg_{tag}")
    d_wu_t = _mm_tn(dup, saved["h"], name=f"ffn_dwu_{tag}")
    dh = _mm([(dgate, wg_t), (dup, wu_t)], "nn", name=f"ffn_dh_{tag}")
    dx, d_below, acc = _norm_mod_bwd(x_in, dh, dx_out, norm_w, mod[4:5], below[0], below[1], f"ffn_norm_bwd_{tag}")
    return (dx, d_below, dict(d_wg=d_wg_t, d_wu=d_wu_t, d_wd=d_wd, acc=acc), *from_chips)


_BIG = [
    ("out_w", "out_w_even", 0, "row"), ("qkv_w", "qkv_w", 0, "col"), ("o_w", "o_w", 0, "row"),
    ("gate0", "ffn_gate_w", 0, "col"), ("up0", "ffn_up_w", 0, "col"), ("down0", "ffn_down_w", 0, "row"),
    ("gate1", "ffn_gate_w", 1, "col"), ("up1", "ffn_up_w", 1, "col"), ("down1", "ffn_down_w", 1, "row"),
    ("in_w", "in_w_even", 0, "col"),
]


def _to_wire(a, kind):
    return a.T if kind == "col" else a


_GATHER_GROUPS = [["in_w"], ["out_w"], ["gate0", "up0", "down0"], ["qkv_w", "o_w"], ["gate1", "up1", "down1"]]
_GRAD_GROUPS = [["qkv_w", "o_w", "gate1", "up1", "down1"], ["out_w", "gate0", "up0", "down0"], ["in_w"]]

_REPLICATED = ["ada_b", "norm_mix_w", "norm_ffn_w", "conv_b", "dt_bias", "a_log", "d_skip", "ssm_norm_w", "gmlp_ln_w",
               "gmlp_ln_b", "gmlp_ws", "gmlp_bs", "sinks", "rel_table", "final_norm_w"]
_TINY_SHARDED = ["conv_w", "qkv_b", "o_b"]

_WEIGHTS = ['ada_w', 'ada_b', 'norm_mix_w', 'norm_ffn_w', 'in_w_even', 'conv_w', 'conv_b', 'dt_bias', 'a_log', 'd_skip',
            'ssm_norm_w', 'gmlp_ln_w', 'gmlp_ln_b', 'gmlp_ws', 'gmlp_bs', 'out_w_even', 'qkv_w', 'qkv_b', 'o_w', 'o_b',
            'sinks', 'rel_table', 'ffn_gate_w', 'ffn_up_w', 'ffn_down_w', 'final_norm_w']


def _step(x, c, loss_target, W, M, V):
    T = x.shape[1]
    x0 = x[0]
    target = loss_target[0]
    me = 4 * lax.axis_index("x") + 2 * lax.axis_index("y") + lax.axis_index("c")

    w_wire_local = {key: _to_wire(W[name][layer].astype(BF16), kind) for key, name, layer, kind in _BIG}

    def wire_pack(keys, mult):
        gp = _Pack(D, 1, mult)
        for key in keys:
            gp.add(key, w_wire_local[key].shape)
        return gp

    gather_packs = [(gp, gp.pack(w_wire_local, BF16)) for gp in (wire_pack(keys, 16) for keys in _GATHER_GROUPS)]
    grad_packs = [wire_pack(keys, 64) for keys in _GRAD_GROUPS]
    full = {}

    def gathered_weights(group, gathered):
        shards = gather_packs[group][0].unpack(gathered, lead=(N_DEV,))
        full.update({key: a.reshape(-1, D) for key, a in shards.items()})

    gathered_weights(0, _all_gather(gather_packs[0][1], "gather_weights"))

    small_in = _Pack(D, 8, 8)
    small_in.add("c", (1, D))
    small_in.add("conv_w", W["conv_w"][0].shape)
    small_in.add("qkv_b", W["qkv_b"][0].shape)
    small_in.add("o_b", W["o_b"][0].shape)
    sm = small_in.unpack(_all_gather(small_in.pack(
        dict(c=c, conv_w=W["conv_w"][0], qkv_b=W["qkv_b"][0], o_b=W["o_b"][0]), F32), "gather_small"), lead=(N_DEV,))
    c_all = sm["c"].reshape(N_DEV, D)
    conv_w_full = jnp.transpose(sm["conv_w"], (1, 0, 2)).reshape(SSM_CONV, CONV_DIM)
    qkv_b_full = sm["qkv_b"].reshape(1, QKV_DIM)
    o_b_full = sm["o_b"].reshape(1, D)

    ncol = W["ada_w"].shape[2]
    cond, mod_cols = _mod_matmul(c_all, W["ada_w"], "mod_matmul")
    mod_g = _all_gather(mod_cols.reshape(DEPTH * N_DEV, ncol), "gather_mod").reshape(N_DEV, DEPTH, N_DEV, ncol)
    mod_me = lax.dynamic_index_in_dim(mod_g, me, axis=2, keepdims=False)
    mod_me = jnp.transpose(mod_me, (1, 0, 2)).reshape(DEPTH, 6, D)
    mod_me = jnp.pad(mod_me, ((0, 0), (0, 2), (0, 0))).reshape(DEPTH * 8, D)
    ada_b_rows = jnp.pad(W["ada_b"].reshape(DEPTH, 6, D), ((0, 0), (0, 2), (0, 0))).reshape(DEPTH * 8, D)
    mod_all = _add_rows(mod_me, ada_b_rows, "mod_bias").reshape(DEPTH, 8, D)
    mod0, mod1 = mod_all[0], mod_all[1]

    in_t = full["in_w"]
    o1, o2, o3, o4 = SSM_INNER, SSM_INNER + CONV_DIM, SSM_INNER + CONV_DIM + SSM_HEADS, SSM_INNER + CONV_DIM + SSM_HEADS + GMLP_INNER
    w_z, w_xbc, w_dt, w_u, w_v = in_t[:o1], in_t[o1:o2], in_t[o2:o3], in_t[o3:o4], in_t[o4:]
    w_main = jnp.concatenate([w_z, w_u, w_v, w_xbc], axis=0)
    w_dtp = jnp.pad(w_dt, ((0, LANES - SSM_HEADS), (0, 0)))

    pad16 = lambda a: jnp.pad(a.reshape(1, SSM_HEADS), ((0, 0), (0, LANES - SSM_HEADS)))
    dtb, alog = pad16(W["dt_bias"][0]), pad16(W["a_log"][0])
    dskip = jnp.repeat(W["d_skip"][0], SSM_HEAD_DIM).reshape(1, SSM_INNER)
    ssm_nw = W["ssm_norm_w"]
    lnw, lnb = W["gmlp_ln_w"], W["gmlp_ln_b"]
    ws = W["gmlp_ws"][0]
    bs_exp = jnp.repeat(W["gmlp_bs"][0].T, CHUNK, axis=1)
    conv_b = W["conv_b"]
    nmw, nfw = W["norm_mix_w"], W["norm_ffn_w"]
    onehot_t = _bucket_onehot_t()
    head_ind = _head_indicator()
    bias = _bias_from_table(W["rel_table"].T, onehot_t, "rel_bias").reshape(ATTN_HEADS, CHUNK, 2 * CHUNK)
    sinks = W["sinks"][0]

    h0 = _norm_mod(x0, nmw[0:1], mod0[1:2], mod0[0:1], "mix_norm_0")
    pm, gathered_a = _mm([(h0, w_main)], "nt", name="in_proj", tn_pref=1536, out_dtype=BF16,
                         side=(_GatherOps, gather_packs[1][1]))
    gathered_weights(1, gathered_a)
    out_w = full["out_w"]
    dtr = _mm([(h0, w_dtp)], "nt", name="in_proj_dt")
    cpre = _conv_fwd(pm, conv_w_full, conv_b, "conv_fwd")
    ya, ypre, sprev, gathered_b = _ssd_fwd(cpre, dtr, pm, dtb, alog, dskip, ssm_nw, head_ind, "ssd_fwd",
                                           side=(_GatherOps, gather_packs[2][1]))
    gathered_weights(2, gathered_b)
    yb, gathered_c = _gmlp_fwd(pm, lnw, lnb, ws, bs_exp, "gmlp_fwd", side=(_GatherOps, gather_packs[3][1]))
    gathered_weights(3, gathered_c)
    x1, mix0, hf0 = _mm_resid([(ya, out_w[:SSM_INNER]), (yb, out_w[SSM_INNER:])], x0, mod0[2:3], name="out_proj",
                              norm=(nfw[0:1], mod0[4:5], mod0[3:4]))
    x2, ffn0, h1, gathered_d = _ffn_fwd(x1, hf0, mod0, full["gate0"], full["up0"], full["down0"], "0",
                                        next_norm=(nmw[1:2], mod1[1:2], mod1[0:1]), gather=gather_packs[4][1])
    gathered_weights(4, gathered_d)
    qkv_t, o_w = full["qkv_w"], full["o_w"]
    w_q, w_k, w_v_att = qkv_t[:D], qkv_t[D:D + LANES], qkv_t[D + LANES:]

    qkv = _mm([(h1, qkv_t)], "nt", name="qkv_proj", bias=qkv_b_full, tn_pref=1280, out_dtype=BF16)
    att = _attn_fwd(qkv, bias, sinks, "attn_fwd")
    x3, mix1, hf1 = _mm_resid([(att, o_w)], x2, mod1[2:3], name="o_proj", bias=o_b_full,
                              norm=(nfw[1:2], mod1[4:5], mod1[3:4]))
    x4, ffn1 = _ffn_fwd(x3, hf1, mod1, full["gate1"], full["up1"], full["down1"], "1")

    dx4, dffn1, acc_f = _final_loss(x4, W["final_norm_w"].reshape(1, D), target, ffn1["out"], mod1[5:6], "final_loss")
    dx3, dmix1, gf1 = _ffn_bwd(dx4, dffn1, x3, ffn1, mod1, nfw[1:2], full["gate1"], full["up1"], full["down1"],
                               (mix1, mod1[2:3]), "1")
    datt = _mm([(dmix1, o_w)], "nt", name="o_proj_dx", out_dtype=BF16)
    d_o_w = _mm_tn(att, dmix1, name="o_proj_dw")
    dq, dk, dv, d_qkv_b, dbias, dsinks = _attn_bwd(qkv, datt, bias, sinks, "attn_bwd")
    d_table = _table_from_dbias(dbias.reshape(ATTN_HEADS, -1), onehot_t, "rel_table_grad")
    d_qkv_t = jnp.concatenate([_mm_tn(dq, h1, name="qkv_dw_q"), _mm_tn(dk, h1, name="qkv_dw_k"), _mm_tn(dv, h1, name="qkv_dw_v")], axis=0)
    g_wire = dict(qkv_w=d_qkv_t, o_w=d_o_w, gate1=gf1["d_wg"], up1=gf1["d_wu"], down1=gf1["d_wd"])

    def packed_partials(group):
        return grad_packs[group].pack({key: g_wire[key].reshape(N_DEV, -1, D) for key in _GRAD_GROUPS[group]},
                                      F32, lead=(N_DEV,))

    partials_l1 = packed_partials(0)
    dh1, theirs_l1 = _mm([(dq, w_q), (dk, w_k), (dv, w_v_att)], "nn", name="qkv_proj_dx", side=(_SiblingOps, partials_l1))
    pair_l1 = _pair_sum(partials_l1, theirs_l1, "grads_pair_sum_l1")
    dx2, dffn0, acc_n1 = _norm_mod_bwd(x2, dh1, dx3, nmw[1:2], mod1[1:2], ffn0["out"], mod0[5:6], "mix_norm_bwd_1")

    from_chips = {}
    dx1, dmix0, gf0, from_chips[0] = _ffn_bwd(dx2, dffn0, x1, ffn0, mod0, nfw[0:1], full["gate0"], full["up0"], full["down0"],
                                              (mix0, mod0[2:3]), "0", exchange=pair_l1)

    dya = _mm([(dmix0, out_w[:SSM_INNER])], "nt", name="out_proj_dx_a")
    dyb = _mm([(dmix0, out_w[SSM_INNER:])], "nt", name="out_proj_dx_b")
    d_out_w = jnp.concatenate([_mm_tn(ya, dmix0, name="out_proj_dw_a"), _mm_tn(yb, dmix0, name="out_proj_dw_b")], axis=0)
    g_wire.update(gate0=gf0["d_wg"], up0=gf0["d_wu"], down0=gf0["d_wd"], out_w=d_out_w)
    partials_ffn0 = packed_partials(1)
    du, dvg, d_ws, d_bs, acc_ln, theirs_ffn0 = _gmlp_bwd(pm, dyb, lnw, lnb, ws, bs_exp, "gmlp_bwd",
                                                         side=(_SiblingOps, partials_ffn0))
    pair_ffn0 = _pair_sum(partials_ffn0, theirs_ffn0, "grads_pair_sum_mix")
    dz, dcpre, ddtr, acc_ssd, d_ssm_nw, from_chips[1] = _ssd_bwd(
        cpre, dtr, pm, ypre, sprev, dya, dtb, alog, dskip, ssm_nw, head_ind, "ssd_bwd", side=(_ChipsOps, pair_ffn0))
    dxbc, acc_conv = _conv_bwd(dcpre, pm, conv_w_full, "conv_bwd")
    d_in_t = jnp.concatenate([
        _mm_tn(dz, h0, name="in_dw_z"), _mm_tn(dxbc, h0, name="in_dw_xbc"),
        _mm_tn(ddtr, h0, name="in_dw_dt")[:SSM_HEADS], _mm_tn(du, h0, name="in_dw_u"), _mm_tn(dvg, h0, name="in_dw_v")], axis=0)
    g_wire.update(in_w=d_in_t)
    partials_mix = packed_partials(2)
    dh0, theirs_mix = _mm([(dz, w_z), (dxbc, w_xbc), (ddtr, w_dtp), (du, w_u), (dvg, w_v)], "nn", name="in_proj_dx",
                          side=(_SiblingOps, partials_mix))
    pair_mix = _pair_sum(partials_mix, theirs_mix, "grads_pair_sum")
    grad_x, acc_n0, from_chips[2] = _norm_mod_bwd(x0, dh0, dx1, nmw[0:1], mod0[1:2], None, None, "mix_norm_bwd_0",
                                                  side=(_ChipsOps, pair_mix))

    g_mine = {}
    for group in range(len(_GRAD_GROUPS)):
        g_mine.update(grad_packs[group].unpack(_sum_parts(from_chips[group], f"grads_chip_sum_{group}")))
    res_big = [{}, {}, {}, {}]
    for key, name, layer, kind in _BIG:
        g_nat = _to_wire(g_mine[key], kind)
        outs = _adamw(g_nat[None], W[name][layer], M[name][layer], V[name][layer], f"adamw_{key}")
        for res, out in zip(res_big, outs):
            res[key] = out

    acc_f0, acc_f1 = gf0["acc"], gf1["acc"]
    views = dict(ada_b=(DEPTH * 6, D), norm_mix_w=(DEPTH, D), norm_ffn_w=(DEPTH, D), conv_b=(1, CONV_DIM),
                 dt_bias=(1, SSM_HEADS), a_log=(1, SSM_HEADS), d_skip=(1, SSM_HEADS), ssm_norm_w=(1, D), gmlp_ln_w=(1, D),
                 gmlp_ln_b=(1, D), gmlp_ws=(GMLP_GROUPS * CHUNK, CHUNK), gmlp_bs=(GMLP_GROUPS, CHUNK),
                 sinks=(1, ATTN_HEADS), rel_table=(REL_BUCKETS, ATTN_HEADS), final_norm_w=(1, D),
                 conv_w=(SSM_CONV, CONV_DIM), qkv_b=(1, QKV_DIM), o_b=(1, D), loss=(1, D))
    canvas = _Canvas()
    for key, (r, cdim) in views.items():
        canvas.add(key, r, cdim, blocks=GMLP_GROUPS if key == "gmlp_ws" else 1)
    row = lambda i: slice(i, i + 1)
    sources = [
        (acc_n0, [("ada_b", row(1), 0), ("ada_b", row(0), 1), ("norm_mix_w", row(2), 0)]),
        (acc_f0, [("ada_b", row(3), 2), ("ada_b", row(1), 3), ("ada_b", row(0), 4), ("norm_ffn_w", row(2), 0)]),
        (acc_n1, [("ada_b", row(3), 5), ("ada_b", row(1), 6), ("ada_b", row(0), 7), ("norm_mix_w", row(2), 1)]),
        (acc_f1, [("ada_b", row(3), 8), ("ada_b", row(1), 9), ("ada_b", row(0), 10), ("norm_ffn_w", row(2), 1),
                  ("o_b", row(4), 0)]),
        (acc_f, [("ada_b", row(3), 11), ("final_norm_w", row(0), 0), ("loss", row(1), 0)]),
        (acc_conv, [("conv_w", slice(0, SSM_CONV), 0), ("conv_b", row(4), 0)]),
        (acc_ssd, [("dt_bias", row(0), 0), ("a_log", row(1), 0), ("d_skip", row(2), 0)]),
        (d_ssm_nw, [("ssm_norm_w", row(0), 0)]),
        (acc_ln, [("gmlp_ln_w", row(0), 0), ("gmlp_ln_b", row(1), 0)]),
        (d_ws.reshape(GMLP_GROUPS * CHUNK, CHUNK), [("gmlp_ws", slice(0, GMLP_GROUPS * CHUNK), 0)]),
        (d_bs.T, [("gmlp_bs", slice(0, GMLP_GROUPS), 0)]),
        (dsinks, [("sinks", row(0), 0)]),
        (d_table, [("rel_table", slice(0, REL_BUCKETS), 0)]),
        (d_qkv_b, [("qkv_b", row(0), 0)]),
    ]
    parts_small = _all_gather(_canvas_fill(canvas, sources, "small_grads_canvas"), "gather_small_grads")
    view = lambda a, key: a.reshape(views[key])
    params = [(name, view(W[name], name), view(M[name], name), view(V[name], name)) for name in _REPLICATED]
    small_out = _adamw_canvas(canvas, parts_small, params, _TINY_SHARDED + ["loss"], "adamw_small")
    loss = small_out["loss"][0][0, 0]
    res_small = [{name: small_out[name][k].reshape(W[name].shape) for name in _REPLICATED} for k in range(4)]

    n_cw, n_qb, n_ob = W["conv_w"].shape[2], W["qkv_b"].shape[1], W["o_b"].shape[1]
    g_tiny = dict(conv_w=lax.dynamic_slice_in_dim(small_out["conv_w"][0], me * n_cw, n_cw, axis=1)[None],
                  qkv_b=lax.dynamic_slice_in_dim(small_out["qkv_b"][0], me * n_qb, n_qb, axis=1),
                  o_b=lax.dynamic_slice_in_dim(small_out["o_b"][0], me * n_ob, n_ob, axis=1))
    tiny = _Pack(D, 8, 8)
    for name in _TINY_SHARDED:
        tiny.add(name, W[name].shape)
    pkt = lambda S: tiny.pack({name: S[name] for name in _TINY_SHARDED}, F32)
    res_tiny = [tiny.unpack(r) for r in _adamw(pkt(g_tiny)[None], pkt(W), pkt(M), pkt(V), "adamw_tiny")]

    dmod_all = parts_small[:, canvas.offset["ada_b"]:canvas.offset["ada_b"] + DEPTH * 6].reshape(N_DEV, DEPTH, 6 * D)
    dmod_cols = jnp.transpose(lax.dynamic_slice_in_dim(dmod_all, me * ncol, ncol, axis=2), (1, 0, 2))
    g_ada_w = _ada_w_grad(cond, dmod_cols, "ada_w_grad")
    flat = lambda a: a.reshape(DEPTH * D, ncol)
    res_ada = [r.reshape(DEPTH, D, ncol) for r in _adamw(flat(g_ada_w)[None], flat(W["ada_w"]), flat(M["ada_w"]), flat(V["ada_w"]), "adamw_ada_w")]

    def result(kind_idx, name):
        if name == "ada_w":
            return res_ada[kind_idx]
        if name in _REPLICATED:
            return res_small[kind_idx][name]
        if name in _TINY_SHARDED:
            return res_tiny[kind_idx][name]
        pieces = [res_big[kind_idx][key] for key, nm, layer, kind in _BIG if nm == name]
        return jnp.stack(pieces)

    outs = [loss, grad_x[None]]
    for kind_idx in range(4):
        outs += [result(kind_idx, name) for name in _WEIGHTS]
    return tuple(outs)


def kernel(x, c, ada_w, ada_b, norm_mix_w, norm_ffn_w, in_w_even, conv_w, conv_b, dt_bias, a_log, d_skip, ssm_norm_w, gmlp_ln_w, gmlp_ln_b, gmlp_ws, gmlp_bs, out_w_even, qkv_w, qkv_b, o_w, o_b, sinks, rel_table, ffn_gate_w, ffn_up_w, ffn_down_w, final_norm_w, loss_target, m_ada_w, m_ada_b, m_norm_mix_w, m_norm_ffn_w, m_in_w_even, m_conv_w, m_conv_b, m_dt_bias, m_a_log, m_d_skip, m_ssm_norm_w, m_gmlp_ln_w, m_gmlp_ln_b, m_gmlp_ws, m_gmlp_bs, m_out_w_even, m_qkv_w, m_qkv_b, m_o_w, m_o_b, m_sinks, m_rel_table, m_ffn_gate_w, m_ffn_up_w, m_ffn_down_w, m_final_norm_w, v_ada_w, v_ada_b, v_norm_mix_w, v_norm_ffn_w, v_in_w_even, v_conv_w, v_conv_b, v_dt_bias, v_a_log, v_d_skip, v_ssm_norm_w, v_gmlp_ln_w, v_gmlp_ln_b, v_gmlp_ws, v_gmlp_bs, v_out_w_even, v_qkv_w, v_qkv_b, v_o_w, v_o_b, v_sinks, v_rel_table, v_ffn_gate_w, v_ffn_up_w, v_ffn_down_w, v_final_norm_w):
    args = locals()
    W = {n: args[n] for n in _WEIGHTS}
    M = {n: args["m_" + n] for n in _WEIGHTS}
    V = {n: args["v_" + n] for n in _WEIGHTS}
    return _step(x, c, loss_target, W, M, V)
```

```python
import functools
import math

import numpy as np
import jax
import jax.numpy as jnp
from jax import lax
from jax.experimental import pallas as pl
from jax.experimental.pallas import tpu as pltpu

F32 = jnp.float32
BF16 = jnp.bfloat16
HIGHEST = lax.Precision.HIGHEST
MESH = pl.DeviceIdType.MESH

N_DEV = 8
D = 1024
DEPTH = 2
SSM_HEADS = 16
SSM_HEAD_DIM = 64
SSM_INNER = 1024
SSM_GROUPS = 2
SSM_STATE = 128
SSM_CONV = 4
CHUNK = 128
CONV_DIM = SSM_INNER + 2 * SSM_GROUPS * SSM_STATE
GMLP_GROUPS = 8
GMLP_INNER = 1024
IN_EVEN = 4624
ATTN_HEADS = 16
ATTN_KV = 2
ATTN_DH = 64
QKV_DIM = 1280
REL_BUCKETS = 32
REL_MAX_DIST = 128
FFN = 2816
EPS = 1e-6
NEG_INF = -1e30
LANES = 128

ADAM_LR = 0.001
ADAM_B1 = 0.9
ADAM_B2 = 0.999
ADAM_EPS = 1e-08
ADAM_WD = 0.01
ADAM_STEP = 10

VMEM_LIMIT_BYTES = 56 * 1024 * 1024
ROW_TILE = 512


def _pcall(body, *, name, out_shape, grid=(), in_specs=None, out_specs=None, scratch=(), sem=None):
    params = dict(vmem_limit_bytes=VMEM_LIMIT_BYTES)
    if sem is not None:
        params["dimension_semantics"] = sem
    specs = {} if in_specs is None else dict(in_specs=in_specs, out_specs=out_specs)
    return pl.pallas_call(
        body, name=name, out_shape=out_shape, grid=grid, **specs,
        scratch_shapes=list(scratch), compiler_params=pltpu.CompilerParams(**params))


def _call(body, args, side=None, *, name, out_shape, grid, in_specs, out_specs, scratch=(), sem=None):
    if side is None:
        return _pcall(body, name=name, out_shape=out_shape, grid=grid, in_specs=in_specs, out_specs=out_specs,
                      scratch=scratch, sem=sem)(*args)
    ops_cls, x = side
    n_in, n_out, n_scr = len(in_specs), len(out_shape), len(scratch)
    steps = int(np.prod(grid))
    hbm = pl.BlockSpec(memory_space=pl.ANY)

    def wrapped(*refs):
        ins, x_ref = refs[:n_in], refs[n_in]
        outs, r_ref = refs[n_in + 1:n_in + 1 + n_out], refs[n_in + 1 + n_out]
        scr, sems = refs[n_in + 2 + n_out:n_in + 2 + n_out + n_scr], refs[n_in + 2 + n_out + n_scr:]
        ops = ops_cls(x_ref, r_ref, *sems)
        step = pl.program_id(0)
        for axis in range(1, len(grid)):
            step = step * grid[axis] + pl.program_id(axis)
        pl.when(step == 0)(ops.start)
        body(*ins, *outs, *scr)
        pl.when(step == (3 * steps) // 4)(ops.forward)
        pl.when(step == steps - 1)(ops.finish)

    return _pcall(
        wrapped, name=name, out_shape=list(out_shape) + [ops_cls.result(x)], grid=grid,
        in_specs=list(in_specs) + [hbm], out_specs=list(out_specs) + [hbm],
        scratch=list(scratch) + ops_cls.scratch(), sem=("arbitrary",) * len(grid))(*args, x)


def _tile(n, pref):
    if n <= pref:
        return n
    best = None
    for t in range(LANES, pref + 1, LANES):
        if n % t == 0:
            best = t
    assert best is not None, (n, pref)
    return best


def _rows(T):
    return min(ROW_TILE, T)


def _sds(shape, dtype=F32):
    return jax.ShapeDtypeStruct(shape, dtype)


def _row_spec(tm, c, col=0):
    return pl.BlockSpec((tm, c), lambda i, col=col: (i, col))


def _vec_spec(c, r=1):
    return pl.BlockSpec((r, c), lambda i: (0, 0))


def _sigmoid(x):
    return jax.nn.sigmoid(x)


def _silu(x):
    return x * _sigmoid(x)


def _dsilu(x):
    s = _sigmoid(x)
    return s * (1.0 + x * (1.0 - s))


def _gelu(x):
    return 0.5 * x * (1.0 + lax.erf(x * 0.7071067811865476))


def _dgelu(x):
    return 0.5 * (1.0 + lax.erf(x * 0.7071067811865476)) + x * jnp.exp(-0.5 * x * x) * 0.3989422804014327


def _dot(a, b, dims, precision=None):
    return lax.dot_general(a, b, (dims, ((), ())), precision=precision, preferred_element_type=F32)


def _nn(a, b, precision=None):
    return _dot(a, b, ((1,), (0,)), precision)


def _nt(a, b, precision=None):
    return _dot(a, b, ((1,), (1,)), precision)


def _tn(a, b, precision=None):
    return _dot(a, b, ((0,), (0,)), precision)


def _bf(x):
    return x.astype(BF16)


def _colsum(x):
    return jnp.sum(x, axis=0, keepdims=True)


def _rowsum(x):
    return jnp.sum(x, axis=1, keepdims=True)


def _allsum(x):
    return _colsum(_rowsum(x))


def _comm_call(ops_cls, x, name, from_vmem=False):
    def body(x_ref, out_ref, *sems):
        ops = ops_cls(x_ref, out_ref, *sems)
        ops.start()
        ops.forward()
        ops.finish()

    return pl.pallas_call(
        body, name=name, out_shape=ops_cls.result(x),
        in_specs=[pl.BlockSpec(memory_space=pltpu.VMEM if from_vmem else pl.ANY)],
        out_specs=pl.BlockSpec(memory_space=pl.ANY), scratch_shapes=ops_cls.scratch(),
    )(x)


def _all_gather(x, name):
    return _comm_call(_GatherOps, x, name, from_vmem=True)


class _GatherOps:
    def __init__(self, x_ref, out_ref, send_sems, recv_sems, local_sem):
        self.x_ref, self.out_ref = x_ref, out_ref
        self.send_sems, self.recv_sems, self.local_sem = send_sems, recv_sems, local_sem
        mx, my, mc = lax.axis_index("x"), lax.axis_index("y"), lax.axis_index("c")
        self.mc = mc
        self.me, self.sibling = (mx, my, mc), (mx, my, 1 - mc)
        self.chips = [(1 - mx, my), (mx, 1 - my), (1 - mx, 1 - my)]

    @staticmethod
    def result(x):
        return _sds((N_DEV,) + x.shape, x.dtype)

    @staticmethod
    def scratch():
        return [pltpu.SemaphoreType.DMA((7,)), pltpu.SemaphoreType.DMA((7,)), pltpu.SemaphoreType.DMA(())]

    def _slot(self, px, py, pc):
        return self.out_ref.at[4 * px + 2 * py + pc]

    def _copy(self, k, block, to, own=False):
        return pltpu.make_async_remote_copy(
            src_ref=self.x_ref if own else self._slot(*block), dst_ref=self._slot(*block),
            send_sem=self.send_sems.at[k], recv_sem=self.recv_sems.at[k], device_id=to, device_id_type=MESH)

    def _mine(self):
        return pltpu.make_async_copy(self.x_ref, self._slot(*self.me), self.local_sem)

    def _first(self):
        return [self._copy(0, self.me, self.sibling, own=True)] + [
            self._copy(1 + j, self.me, (*chip, self.mc), own=True) for j, chip in enumerate(self.chips)]

    def _passed(self):
        return [self._copy(4 + j, (*chip, self.mc), self.sibling) for j, chip in enumerate(self.chips)]

    def start(self):
        self._mine().start()
        for cp in self._first():
            cp.start()

    def forward(self):
        passed = self._passed()
        for j, chip in enumerate(self.chips):
            self._copy(1 + j, (*chip, self.mc), self.me).wait_recv()
            passed[j].start()

    def finish(self):
        self._copy(0, self.sibling, self.me).wait_recv()
        for j, chip in enumerate(self.chips):
            self._copy(4 + j, (*chip, 1 - self.mc), self.me).wait_recv()
        for cp in self._first() + self._passed():
            cp.wait_send()
        self._mine().wait()


N_CHIP = 4


class _SiblingOps:
    def __init__(self, p_ref, theirs_ref, send_sems, recv_sems):
        self.p_ref, self.theirs_ref, self.send_sems, self.recv_sems = p_ref, theirs_ref, send_sems, recv_sems

    @staticmethod
    def result(p):
        return _sds((N_CHIP,) + p.shape[1:], p.dtype)

    @staticmethod
    def scratch():
        return [pltpu.SemaphoreType.DMA((N_CHIP,))] * 2

    def _copies(self):
        mx, my, mc = lax.axis_index("x"), lax.axis_index("y"), lax.axis_index("c")
        return [pltpu.make_async_remote_copy(
            src_ref=self.p_ref.at[2 * chip + 1 - mc], dst_ref=self.theirs_ref.at[chip],
            send_sem=self.send_sems.at[chip], recv_sem=self.recv_sems.at[chip],
            device_id=(mx, my, 1 - mc), device_id_type=MESH) for chip in range(N_CHIP)]

    def start(self):
        for cp in self._copies():
            cp.start()

    def forward(self):
        pass

    def finish(self):
        for cp in self._copies():
            cp.wait()


class _ChipsOps:
    def __init__(self, q_ref, out_ref, send_sems, recv_sems, local_sem):
        self.q_ref, self.out_ref = q_ref, out_ref
        self.send_sems, self.recv_sems, self.local_sem = send_sems, recv_sems, local_sem

    @staticmethod
    def result(q):
        return _sds(q.shape, q.dtype)

    @staticmethod
    def scratch():
        return [pltpu.SemaphoreType.DMA((N_CHIP - 1,)), pltpu.SemaphoreType.DMA((N_CHIP - 1,)), pltpu.SemaphoreType.DMA(())]

    def _copies(self):
        mx, my, mc = lax.axis_index("x"), lax.axis_index("y"), lax.axis_index("c")
        me = 2 * mx + my
        local = pltpu.make_async_copy(self.q_ref.at[me], self.out_ref.at[me], self.local_sem)
        remote = []
        for r in range(1, N_CHIP):
            px = 1 - mx if r & 2 else mx
            py = 1 - my if r & 1 else my
            remote.append(pltpu.make_async_remote_copy(
                src_ref=self.q_ref.at[2 * px + py], dst_ref=self.out_ref.at[me],
                send_sem=self.send_sems.at[r - 1], recv_sem=self.recv_sems.at[r - 1],
                device_id=(px, py, mc), device_id_type=MESH))
        return local, remote

    def start(self):
        local, remote = self._copies()
        local.start()
        for cp in remote:
            cp.start()

    def forward(self):
        pass

    def finish(self):
        local, remote = self._copies()
        for cp in remote:
            cp.wait()
        local.wait()


def _pair_sum(p, theirs, name):
    n, R, C = theirs.shape
    tr = _tile_rows(R, 256)

    def body(p_ref, t_ref, o_ref):
        mc = lax.axis_index("c")
        o_ref[0] = (p_ref[0, mc] + t_ref[0]).astype(BF16)

    blk = pl.BlockSpec((1, tr, C), lambda s, i: (s, i, 0))
    return _pcall(body, name=name, out_shape=_sds((n, R, C), BF16), grid=(n, R // tr),
                  in_specs=[pl.BlockSpec((1, 2, tr, C), lambda s, i: (s, 0, i, 0)), blk],
                  out_specs=blk, sem=("parallel", "parallel"))(p.reshape(n, 2, R, C), theirs)


def _sum_parts(parts, name):
    P, R, C = parts.shape
    tr = _tile_rows(R, 256)

    def body(p_ref, o_ref):
        g = p_ref[0].astype(F32)
        for k in range(1, P):
            g = g + p_ref[k].astype(F32)
        o_ref[...] = g

    return _pcall(body, name=name, out_shape=_sds((R, C)), grid=(R // tr,),
                  in_specs=[pl.BlockSpec((P, tr, C), lambda i: (0, i, 0))],
                  out_specs=pl.BlockSpec((tr, C), lambda i: (i, 0)), sem=("parallel",))(parts)


def _adamw(parts, w, m, v, name):
    P, R, C = parts.shape
    tr = R if R <= 256 else _tile_rows(R, 256)

    def body(p_ref, w_ref, m_ref, v_ref, g_ref, d_ref, nm_ref, nv_ref):
        g = p_ref[0]
        for k in range(1, P):
            g = g + p_ref[k]
        nm = ADAM_B1 * m_ref[...] + (1.0 - ADAM_B1) * g
        nv = ADAM_B2 * v_ref[...] + (1.0 - ADAM_B2) * (g * g)
        m_hat = nm / (1.0 - ADAM_B1 ** ADAM_STEP)
        v_hat = nv / (1.0 - ADAM_B2 ** ADAM_STEP)
        g_ref[...] = g
        d_ref[...] = -ADAM_LR * (m_hat / (jnp.sqrt(v_hat) + ADAM_EPS) + ADAM_WD * w_ref[...])
        nm_ref[...] = nm
        nv_ref[...] = nv

    blk = pl.BlockSpec((tr, C), lambda i: (i, 0))
    return _pcall(
        body, name=name, out_shape=[_sds((R, C))] * 4, grid=(R // tr,),
        in_specs=[pl.BlockSpec((P, tr, C), lambda i: (0, i, 0)), blk, blk, blk],
        out_specs=[blk] * 4, sem=("parallel",))(parts, w, m, v)


def _adam_update(g, w, m, v):
    nm = ADAM_B1 * m + (1.0 - ADAM_B1) * g
    nv = ADAM_B2 * v + (1.0 - ADAM_B2) * (g * g)
    m_hat = nm / (1.0 - ADAM_B1 ** ADAM_STEP)
    v_hat = nv / (1.0 - ADAM_B2 ** ADAM_STEP)
    return -ADAM_LR * (m_hat / (jnp.sqrt(v_hat) + ADAM_EPS) + ADAM_WD * w), nm, nv


class _Canvas:
    def __init__(self):
        self.offset, self.views, self.rows = {}, {}, 0

    def add(self, key, r, c, blocks=1):
        need = r // blocks if blocks > 1 else r * (-(-c // D))
        if need >= 8:
            self.rows = -(-self.rows // 8) * 8
        self.offset[key], self.views[key] = self.rows, (r, c, blocks)
        self.rows += need

    @property
    def total(self):
        return -(-self.rows // 8) * 8

    def cells(self, key):
        (r, c, blocks), off = self.views[key], self.offset[key]
        if blocks > 1:
            n = r // blocks
            return [(off, n, b * c, c, slice(b * n, (b + 1) * n), slice(0, c)) for b in range(blocks)]
        if c <= D:
            return [(off, r, 0, c, slice(0, r), slice(0, c))]
        chunks = -(-c // D)
        return [(off + i * chunks + j, 1, 0, min(D, c - j * D), slice(i, i + 1), slice(j * D, min(c, (j + 1) * D)))
                for i in range(r) for j in range(chunks)]


def _canvas_fill(canvas, sources, name):
    arrays = [a for a, _ in sources]

    def body(*refs):
        out_ref = refs[-1]
        out_ref[...] = jnp.zeros_like(out_ref)
        for ref, (_, items) in zip(refs[:-1], sources):
            for key, src_rows, view_row in items:
                n = src_rows.stop - src_rows.start
                for row, count, lane, width, vrows, vcols in canvas.cells(key):
                    lo, hi = max(vrows.start, view_row), min(vrows.stop, view_row + n)
                    if lo < hi:
                        src = slice(src_rows.start + lo - view_row, src_rows.start + hi - view_row)
                        out_ref[row + lo - vrows.start:row + hi - vrows.start, lane:lane + width] = ref[src, vcols]

    return _pcall(body, name=name, out_shape=_sds((canvas.total, D)))(*arrays)


def _adamw_canvas(canvas, parts, params, sum_only, name):
    n = len(params)

    def body(p_ref, *refs):
        sum_ref = refs[-1]
        g_all = p_ref[0]
        for k in range(1, N_DEV):
            g_all = g_all + p_ref[k]
        sum_ref[...] = g_all
        for i, (key, _, _, _) in enumerate(params):
            w_ref, m_ref, v_ref = refs[3 * i:3 * i + 3]
            outs = refs[3 * n + 4 * i:3 * n + 4 * i + 4]
            for row, count, lane, width, vrows, vcols in canvas.cells(key):
                g = sum_ref[row:row + count, lane:lane + width]
                delta, nm, nv = _adam_update(g, w_ref[vrows, vcols], m_ref[vrows, vcols], v_ref[vrows, vcols])
                for ref, val in zip(outs, (g, delta, nm, nv)):
                    ref[vrows, vcols] = val
        for i, key in enumerate(sum_only):
            for row, count, lane, width, vrows, vcols in canvas.cells(key):
                refs[7 * n + i][vrows, vcols] = sum_ref[row:row + count, lane:lane + width]

    args = [a for _, w, m, v in params for a in (w, m, v)]
    out_shape = [_sds(w.shape) for _, w, _, _ in params for _ in range(4)] + [_sds(canvas.views[k][:2]) for k in sum_only]
    res = _pcall(body, name=name, out_shape=out_shape, scratch=[pltpu.VMEM(parts.shape[1:], F32)])(parts, *args)
    out = {key: res[4 * i:4 * i + 4] for i, (key, _, _, _) in enumerate(params)}
    out.update({key: [res[4 * n + i]] for i, key in enumerate(sum_only)})
    return out


def _tile_rows(n, pref):
    best = None
    for t in range(8, pref + 1, 8):
        if n % t == 0:
            best = t
    assert best is not None, (n, pref)
    return best


def _mm(pairs, mode, *, name, out_dtype=F32, bias=None, tn_pref=1024, side=None):
    M = pairs[0][0].shape[0]
    N = pairs[0][1].shape[1] if mode == "nn" else pairs[0][1].shape[0]
    tm, tn = _rows(M), _tile(N, tn_pref)
    n_pairs = len(pairs)
    has_bias = bias is not None

    def body(*refs):
        acc = _pairs_dot(refs[:2 * n_pairs], mode)
        if has_bias:
            acc = acc + refs[2 * n_pairs][...]
        refs[-1][...] = acc.astype(refs[-1].dtype)

    in_specs, args = _pair_specs(pairs, mode, tm, tn)
    if has_bias:
        in_specs.append(pl.BlockSpec((1, tn), lambda j, i: (0, j)))
        args.append(bias)
    res = _call(body, args, side, name=name, out_shape=[_sds((M, N), out_dtype)], grid=(N // tn, M // tm),
                in_specs=in_specs, out_specs=[pl.BlockSpec((tm, tn), lambda j, i: (i, j))], sem=("parallel", "parallel"))
    return res[0] if side is None else (res[0], res[1])


def _pairs_dot(ab, mode):
    acc = None
    for p in range(len(ab) // 2):
        a, b = _bf(ab[2 * p][...]), _bf(ab[2 * p + 1][...])
        d = _nn(a, b) if mode == "nn" else _nt(a, b)
        acc = d if acc is None else acc + d
    return acc


def _pair_specs(pairs, mode, tm, tn):
    in_specs, args = [], []
    for a, b in pairs:
        K = a.shape[1]
        in_specs.append(pl.BlockSpec((tm, K), lambda j, i: (i, 0)))
        if mode == "nn":
            in_specs.append(pl.BlockSpec((K, tn), lambda j, i: (0, j)))
        else:
            in_specs.append(pl.BlockSpec((tn, K), lambda j, i: (j, 0)))
        args += [a, b]
    return in_specs, args


def _mm_resid(pairs, resid, gvec, *, name, bias=None, norm=None):
    M, N = resid.shape
    tm = _rows(M)
    n_pairs = len(pairs)
    has_bias, has_norm = bias is not None, norm is not None

    def body(*refs):
        acc = _pairs_dot(refs[:2 * n_pairs], "nn")
        pos = 2 * n_pairs
        if has_bias:
            acc = acc + refs[pos][...]
            pos += 1
        xv = refs[pos][...] + refs[pos + 1][...] * acc
        outs = refs[pos + 2 + 3 * has_norm:]
        outs[0][...] = xv
        outs[1][...] = acc.astype(BF16)
        if has_norm:
            w_ref, sc_ref, sh_ref = refs[pos + 2:pos + 5]
            r = lax.rsqrt(jnp.mean(xv * xv, axis=-1, keepdims=True) + EPS)
            outs[2][...] = ((xv * r * w_ref[...]) * (1.0 + sc_ref[...]) + sh_ref[...]).astype(BF16)

    in_specs, args = _pair_specs(pairs, "nn", tm, N)
    vec = pl.BlockSpec((1, N), lambda j, i: (0, 0))
    row = pl.BlockSpec((tm, N), lambda j, i: (i, 0))
    if has_bias:
        in_specs.append(vec)
        args.append(bias)
    in_specs += [row, vec] + [vec] * (3 * has_norm)
    args += [resid, gvec] + (list(norm) if has_norm else [])
    return _pcall(body, name=name, out_shape=[_sds((M, N)), _sds((M, N), BF16)] + [_sds((M, N), BF16)] * has_norm,
                  grid=(1, M // tm), in_specs=in_specs, out_specs=[row] * (2 + has_norm),
                  sem=("parallel", "parallel"))(*args)


def _mm_tn(a, b, *, name, tm_pref=1408, tn_pref=1536):
    K, M = a.shape
    N = b.shape[1]
    tm, tn = _tile(M, tm_pref), _tile(N, tn_pref)
    tk = K if K <= 2 * ROW_TILE else 2 * ROW_TILE

    def body(a_ref, b_ref, o_ref):
        @pl.when(pl.program_id(2) == 0)
        def _():
            o_ref[...] = jnp.zeros_like(o_ref)
        o_ref[...] += _tn(_bf(a_ref[...]), _bf(b_ref[...]))

    return _pcall(
        body, name=name, out_shape=_sds((M, N)), grid=(M // tm, N // tn, K // tk),
        in_specs=[pl.BlockSpec((tk, tm), lambda i, j, k: (k, i)), pl.BlockSpec((tk, tn), lambda i, j, k: (k, j))],
        out_specs=pl.BlockSpec((tm, tn), lambda i, j, k: (i, j)),
        sem=("parallel", "parallel", "arbitrary"))(a, b)


def _mm_swiglu(h, wg_t, wu_t, name, side=None):
    M, K = h.shape
    N = wg_t.shape[0]
    tm, tn = _rows(M), _tile(N, 1408)

    def body(h_ref, wg_ref, wu_ref, gate_ref, up_ref, act_ref):
        hv = _bf(h_ref[...])
        gate = _nt(hv, wg_ref[...])
        up = _nt(hv, wu_ref[...])
        gate_ref[...] = gate.astype(BF16)
        up_ref[...] = up.astype(BF16)
        act_ref[...] = (_silu(gate) * up).astype(BF16)

    w_spec = pl.BlockSpec((tn, K), lambda j, i: (j, 0))
    o_spec = pl.BlockSpec((tm, tn), lambda j, i: (i, j))
    return _call(body, (h, wg_t, wu_t), side, name=name,
                 out_shape=[_sds((M, N), BF16)] * 3, grid=(N // tn, M // tm),
                 in_specs=[pl.BlockSpec((tm, K), lambda j, i: (i, 0)), w_spec, w_spec], out_specs=[o_spec] * 3,
                 sem=("parallel", "parallel"))


def _mm_swiglu_bwd(dout, wd, gate, up, name, side=None):
    M, K = dout.shape
    N = wd.shape[0]
    tm, tn = _rows(M), _tile(N, 1408)

    def body(d_ref, wd_ref, gate_ref, up_ref, dg_ref, du_ref):
        dact = _nt(_bf(d_ref[...]), wd_ref[...])
        g = gate_ref[...].astype(F32)
        dg_ref[...] = (dact * up_ref[...].astype(F32) * _dsilu(g)).astype(BF16)
        du_ref[...] = (dact * _silu(g)).astype(BF16)

    t_spec = pl.BlockSpec((tm, tn), lambda j, i: (i, j))
    return _call(
        body, (dout, wd, gate, up), side, name=name, out_shape=[_sds((M, N), BF16)] * 2, grid=(N // tn, M // tm),
        in_specs=[pl.BlockSpec((tm, K), lambda j, i: (i, 0)), pl.BlockSpec((tn, K), lambda j, i: (j, 0)), t_spec, t_spec],
        out_specs=[t_spec] * 2, sem=("parallel", "parallel"))


def _norm_mod(x, w, sc, sh, name, side=None):
    T = x.shape[0]
    tm = _rows(T)

    def body(x_ref, w_ref, sc_ref, sh_ref, o_ref):
        xv = x_ref[...]
        r = lax.rsqrt(jnp.mean(xv * xv, axis=-1, keepdims=True) + EPS)
        o_ref[...] = ((xv * r * w_ref[...]) * (1.0 + sc_ref[...]) + sh_ref[...]).astype(BF16)

    return _call(body, (x, w, sc, sh), side, name=name, out_shape=[_sds((T, D), BF16)], grid=(T // tm,),
                 in_specs=[_row_spec(tm, D), _vec_spec(D), _vec_spec(D), _vec_spec(D)],
                 out_specs=[_row_spec(tm, D)], sem=("parallel",))


def _gate_rows(dxv, br_ref, g_ref, db_ref, acc_ref):
    db = g_ref[...] * dxv
    db_ref[...] = db.astype(BF16)
    acc_ref[3:4, :] += _colsum(dxv * br_ref[...].astype(F32))
    acc_ref[4:5, :] += _colsum(db)


def _norm_mod_bwd(x, dh, dres, w, sc, branch, g, name, side=None):
    T = x.shape[0]
    tm = _rows(T)
    gated = branch is not None

    def body(x_ref, dh_ref, dres_ref, w_ref, sc_ref, *rest):
        (br_ref, g_ref, dx_ref, db_ref, acc_ref) = rest if gated else (None, None, rest[0], None, rest[1])

        @pl.when(pl.program_id(0) == 0)
        def _():
            acc_ref[...] = jnp.zeros_like(acc_ref)
        xv, dh_v, wv = x_ref[...], dh_ref[...], w_ref[...]
        r = lax.rsqrt(jnp.mean(xv * xv, axis=-1, keepdims=True) + EPS)
        n = xv * r
        dnw = dh_v * (1.0 + sc_ref[...])
        dn = dnw * wv
        dxv = dres_ref[...] + r * (dn - n * jnp.mean(dn * n, axis=-1, keepdims=True))
        dx_ref[...] = dxv
        acc_ref[0:1, :] += _colsum(dh_v * (n * wv))
        acc_ref[1:2, :] += _colsum(dh_v)
        acc_ref[2:3, :] += _colsum(dnw * n)
        if gated:
            _gate_rows(dxv, br_ref, g_ref, db_ref, acc_ref)

    row = _row_spec(tm, D)
    args = (x, dh, dres, w, sc) + ((branch, g) if gated else ())
    return _call(body, args, side, name=name,
                 out_shape=[_sds((T, D))] + ([_sds((T, D), BF16)] if gated else []) + [_sds((8, D))], grid=(T // tm,),
                 in_specs=[row, row, row, _vec_spec(D), _vec_spec(D)] + ([row, _vec_spec(D)] if gated else []),
                 out_specs=[row] + ([row] if gated else []) + [_vec_spec(D, 8)], sem=("arbitrary",))


def _final_loss(x, wf, target, branch, g, name):
    T = x.shape[0]
    tm = _rows(T)

    def body(x_ref, w_ref, t_ref, br_ref, g_ref, dx_ref, db_ref, acc_ref):
        @pl.when(pl.program_id(0) == 0)
        def _():
            acc_ref[...] = jnp.zeros_like(acc_ref)
        xv, wv = x_ref[...], w_ref[...]
        r = lax.rsqrt(jnp.mean(xv * xv, axis=-1, keepdims=True) + EPS)
        n = xv * r
        err = n * wv - t_ref[...]
        dy = err * (1.0 / D)
        dn = dy * wv
        dxv = r * (dn - n * jnp.mean(dn * n, axis=-1, keepdims=True))
        dx_ref[...] = dxv
        acc_ref[0:1, :] += _colsum(dy * n)
        acc_ref[1:2, :] += jnp.broadcast_to(_allsum(err * err) * (0.5 / D), (1, D))
        _gate_rows(dxv, br_ref, g_ref, db_ref, acc_ref)

    row = _row_spec(tm, D)
    return _pcall(body, name=name, out_shape=[_sds((T, D)), _sds((T, D), BF16), _sds((8, D))], grid=(T // tm,),
                  in_specs=[row, _vec_spec(D), row, row, _vec_spec(D)],
                  out_specs=[row, row, _vec_spec(D, 8)], sem=("arbitrary",))(x, wf, target, branch, g)


def _mod_matmul(c_all, ada_w, name):
    n = ada_w.shape[2]

    def body(c_ref, w_ref, cond_ref, o_ref):
        cond = _silu(c_ref[...])
        cond_ref[...] = cond
        o_ref[0] = _nn(cond, w_ref[0])

    return _pcall(body, name=name, out_shape=[_sds((N_DEV, D)), _sds((DEPTH, N_DEV, n))], grid=(DEPTH,),
                  in_specs=[pl.BlockSpec((N_DEV, D), lambda l: (0, 0)), pl.BlockSpec((1, D, n), lambda l: (l, 0, 0))],
                  out_specs=[pl.BlockSpec((N_DEV, D), lambda l: (0, 0)), pl.BlockSpec((1, N_DEV, n), lambda l: (l, 0, 0))],
                  sem=("arbitrary",))(c_all, ada_w)


def _add_rows(a, b, name):
    def body(a_ref, b_ref, o_ref):
        o_ref[...] = a_ref[...] + b_ref[...]

    return _pcall(body, name=name, out_shape=_sds(a.shape))(a, b)


def _ada_w_grad(cond, dmod_cols, name):
    n = dmod_cols.shape[2]

    def body(c_ref, d_ref, o_ref):
        o_ref[0] = _tn(c_ref[...], d_ref[0])

    return _pcall(body, name=name, out_shape=_sds((DEPTH, D, n)), grid=(DEPTH,),
                  in_specs=[pl.BlockSpec((N_DEV, D), lambda l: (0, 0)), pl.BlockSpec((1, N_DEV, n), lambda l: (l, 0, 0))],
                  out_specs=pl.BlockSpec((1, D, n), lambda l: (l, 0, 0)), sem=("parallel",))(cond, dmod_cols)


def _conv_fwd(pm, conv_w, conv_b, name):
    T = pm.shape[0]
    tm = _rows(T)
    C = CONV_DIM

    def body(x_ref, prev_ref, w_ref, b_ref, o_ref):
        cur = x_ref[...].astype(F32)
        prev = jnp.where(pl.program_id(0) > 0, prev_ref[...].astype(F32)[8:16], 0.0)
        cur8 = cur[0:8]
        row8 = lax.broadcasted_iota(jnp.int32, (8, C), 0)
        full = w_ref[3:4, :] * cur
        head = w_ref[3:4, :] * cur8
        for k in range(1, SSM_CONV):
            wk = w_ref[3 - k:4 - k, :]
            full = full + wk * pltpu.roll(cur, k, 0)
            head = head + wk * jnp.where(row8 < k, pltpu.roll(prev, k, 0), pltpu.roll(cur8, k, 0))
        o_ref[...] = full + b_ref[...]
        o_ref[0:8, :] = head + b_ref[...]

    return _pcall(
        body, name=name, out_shape=_sds((T, C)), grid=(T // tm,),
        in_specs=[pl.BlockSpec((tm, C), lambda i: (i, 2)),
                  pl.BlockSpec((16, C), lambda i: (jnp.maximum(i * (tm // 16) - 1, 0), 2)),
                  _vec_spec(C, SSM_CONV), _vec_spec(C)],
        out_specs=_row_spec(tm, C), sem=("parallel",))(pm, pm, conv_w, conv_b)


def _conv_bwd(dc, pm, conv_w, name):
    T = dc.shape[0]
    tm = _rows(T)
    C = CONV_DIM
    nt = T // tm

    def body(dc_ref, nxt_ref, x_ref, prev_ref, w_ref, dx_ref, acc_ref):
        i = pl.program_id(0)

        @pl.when(i == 0)
        def _():
            acc_ref[...] = jnp.zeros_like(acc_ref)
        dcv = dc_ref[...]
        nxt = jnp.where(i < nt - 1, nxt_ref[...], 0.0)
        xc = x_ref[...].astype(F32)
        prev = jnp.where(i > 0, prev_ref[...].astype(F32)[8:16], 0.0)
        dc8h, dc8t, x8 = dcv[0:8], dcv[tm - 8:tm], xc[0:8]
        row8 = lax.broadcasted_iota(jnp.int32, (8, C), 0)
        full = w_ref[3:4, :] * dcv
        tail = w_ref[3:4, :] * dc8t
        acc_ref[3:4, :] += _colsum(dcv * xc)
        for k in range(1, SSM_CONV):
            wk = w_ref[3 - k:4 - k, :]
            full = full + wk * pltpu.roll(dcv, tm - k, 0)
            tail = tail + wk * jnp.where(row8 + k >= 8, pltpu.roll(nxt, 8 - k, 0), pltpu.roll(dc8t, 8 - k, 0))
            xs_head = jnp.where(row8 < k, pltpu.roll(prev, k, 0), pltpu.roll(x8, k, 0))
            prod = dcv * pltpu.roll(xc, k, 0)
            acc_ref[3 - k:4 - k, :] += _colsum(prod) - _colsum(prod[0:8]) + _colsum(dc8h * xs_head)
        acc_ref[4:5, :] += _colsum(dcv)
        dx_ref[...] = jnp.concatenate([full[0:tm - 8], tail], axis=0).astype(BF16)

    return _pcall(
        body, name=name, out_shape=[_sds((T, C), BF16), _sds((8, C))], grid=(nt,),
        in_specs=[_row_spec(tm, C),
                  pl.BlockSpec((8, C), lambda i: (jnp.minimum((i + 1) * (tm // 8), T // 8 - 1), 0)),
                  pl.BlockSpec((tm, C), lambda i: (i, 2)),
                  pl.BlockSpec((16, C), lambda i: (jnp.maximum(i * (tm // 16) - 1, 0), 2)),
                  _vec_spec(C, SSM_CONV)],
        out_specs=[_row_spec(tm, C), _vec_spec(C, 8)], sem=("arbitrary",))(dc, dc, pm, pm, conv_w)


def _ssd_prologue(cpre, dtr, dtb, alog):
    L = CHUNK
    xc = _silu(cpre)
    pre = dtr + dtb
    dt = jnp.maximum(pre, 0.0) + jnp.log1p(jnp.exp(-jnp.abs(pre)))
    a = -jnp.exp(alog)
    la = dt * a
    row = lax.broadcasted_iota(jnp.int32, (L, L), 0)
    col = lax.broadcasted_iota(jnp.int32, (L, L), 1)
    causal = row >= col
    tri = causal.astype(F32)
    lc = _nn(tri, la, HIGHEST)
    return xc, pre, dt, a, causal, tri, lc, row, col


def _head_indicator():
    m = np.zeros((LANES, SSM_INNER), np.float32)
    for h in range(SSM_HEADS):
        m[h, h * SSM_HEAD_DIM:(h + 1) * SSM_HEAD_DIM] = 1.0
    return jnp.asarray(m, dtype=BF16)


def _split_dot(x, ind, dims):
    hi = x.astype(BF16)
    lo = (x - hi.astype(F32)).astype(BF16)
    return _dot(hi, ind, dims) + _dot(lo, ind, dims)


def _expand(x16, ind):
    return _split_dot(x16, ind, ((1,), (0,)))


def _headsum(x, ind, single_pass=False):
    if single_pass:
        return _dot(x.astype(BF16), ind, ((1,), (1,)))
    return _split_dot(x, ind, ((1,), (1,)))


def _ssd_fwd(cpre, dtr, pm, dtb, alog, dskip, normw, ind, name, side=None):
    T = cpre.shape[0]
    nc = T // CHUNK
    L, P, H, HPG, N = CHUNK, SSM_HEAD_DIM, SSM_HEADS, SSM_HEADS // SSM_GROUPS, SSM_STATE
    half = SSM_INNER // SSM_GROUPS

    def body(cp_ref, dtr_ref, z_ref, dtb_ref, alog_ref, dskip_ref, nw_ref, ind_ref, ya_ref, y_ref, sp_ref, st_ref):
        @pl.when(pl.program_id(0) == 0)
        def _():
            st_ref[...] = jnp.zeros_like(st_ref)
        xc, _, dt, _, causal, _, lc, _, _ = _ssd_prologue(cp_ref[...], dtr_ref[...], dtb_ref[...], alog_ref[...])
        lct = lc.T
        ind = ind_ref[...]
        llast = lc[L - 1:L, :]
        xs = xc[:, :SSM_INNER]
        xd = xs * _expand(dt, ind)
        ex = _expand(jnp.exp(lc), ind)
        xd_end = _bf(xd * _expand(jnp.exp(llast - lc), ind))
        cdx = _expand(jnp.broadcast_to(jnp.exp(llast), (8, LANES)), ind)[0:1]
        xdb = _bf(xd)
        sp_ref[0] = st_ref[...]
        for g in range(SSM_GROUPS):
            sl = slice(g * half, (g + 1) * half)
            bm = _bf(xc[:, SSM_INNER + g * N:SSM_INNER + (g + 1) * N])
            cm = _bf(xc[:, SSM_INNER + (SSM_GROUPS + g) * N:SSM_INNER + (SSM_GROUPS + g + 1) * N])
            cb = _nt(cm, bm)
            st = st_ref[g]
            y_ref[:, sl] = ex[:, sl] * _nn(cm, _bf(st)) + dskip_ref[:, sl] * xs[:, sl]
            st_ref[g] = st * cdx[:, sl] + _tn(bm, xd_end[:, sl])
            for j in range(HPG):
                h = g * HPG + j
                decay = jnp.where(causal, jnp.exp(jnp.where(causal, lc[:, h:h + 1] - lct[h:h + 1, :], 0.0)), 0.0)
                y_ref[:, h * P:(h + 1) * P] += _nn(_bf(cb * decay), xdb[:, h * P:(h + 1) * P])
        y2 = y_ref[...] * _silu(z_ref[...].astype(F32))
        for g in range(SSM_GROUPS):
            yg = y2[:, g * half:(g + 1) * half]
            r = lax.rsqrt(jnp.mean(yg * yg, axis=-1, keepdims=True) + EPS)
            ya_ref[:, g * half:(g + 1) * half] = (yg * r * nw_ref[:, g * half:(g + 1) * half]).astype(BF16)

    return _call(
        body, (cpre, dtr, pm, dtb, alog, dskip, normw, ind), side, name=name,
        out_shape=[_sds((T, SSM_INNER), BF16), _sds((T, SSM_INNER)), _sds((nc, SSM_GROUPS, N, half))], grid=(nc,),
        in_specs=[_row_spec(L, CONV_DIM), _row_spec(L, LANES), _row_spec(L, SSM_INNER, 0),
                  _vec_spec(LANES), _vec_spec(LANES), _vec_spec(SSM_INNER), _vec_spec(SSM_INNER), _vec_spec(SSM_INNER, LANES)],
        out_specs=[_row_spec(L, SSM_INNER), _row_spec(L, SSM_INNER),
                   pl.BlockSpec((1, SSM_GROUPS, N, half), lambda i: (i, 0, 0, 0))],
        scratch=[pltpu.VMEM((SSM_GROUPS, N, half), F32)], sem=("arbitrary",))


def _ssd_bwd(cpre, dtr, pm, ypre, sprev, dya, dtb, alog, dskip, normw, ind, name, side=None):
    T = cpre.shape[0]
    nc = T // CHUNK
    L, P, H, HPG, N = CHUNK, SSM_HEAD_DIM, SSM_HEADS, SSM_HEADS // SSM_GROUPS, SSM_STATE
    half = SSM_INNER // SSM_GROUPS

    def body(cp_ref, dtr_ref, z_ref, y_ref, sp_ref, dya_ref, dtb_ref, alog_ref, dskip_ref, nw_ref, ind_ref,
             dz_ref, dcp_ref, ddtr_ref, acc_ref, dnw_ref, ds_ref, dy_ref, dxd_ref, rr_ref, yoff_ref, dcd_ref):
        @pl.when(pl.program_id(0) == 0)
        def _():
            ds_ref[...] = jnp.zeros_like(ds_ref)
            acc_ref[...] = jnp.zeros_like(acc_ref)
            dnw_ref[...] = jnp.zeros_like(dnw_ref)
        cpre_v = cp_ref[...]
        xc, pre, dt, a, causal, tri, lc, row, col = _ssd_prologue(cpre_v, dtr_ref[...], dtb_ref[...], alog_ref[...])
        lct = lc.T
        zv, yv = z_ref[...].astype(F32), y_ref[...]
        sz = _silu(zv)
        y2 = yv * sz
        dya_v = dya_ref[...]
        nwv = nw_ref[...]
        for g in range(SSM_GROUPS):
            sl = slice(g * half, (g + 1) * half)
            yg = y2[:, sl]
            r = lax.rsqrt(jnp.mean(yg * yg, axis=-1, keepdims=True) + EPS)
            nrm = yg * r
            dnw_ref[:, sl] += _colsum(dya_v[:, sl] * nrm)
            dn = dya_v[:, sl] * nwv[:, sl]
            dy2 = r * (dn - nrm * jnp.mean(dn * nrm, axis=-1, keepdims=True))
            dy_ref[:, sl] = dy2 * sz[:, sl]
            dz_ref[:, sl] = (dy2 * yv[:, sl] * _dsilu(zv[:, sl])).astype(BF16)
        ind = ind_ref[...]
        llast = lc[L - 1:L, :]
        dte16 = jnp.exp(llast - lc)
        cd16 = jnp.exp(llast)
        xs = xc[:, :SSM_INNER]
        dtx = _expand(dt, ind)
        ex = _expand(jnp.exp(lc), ind)
        dtex = _expand(dte16, ind)
        cdx = _expand(jnp.broadcast_to(cd16, (8, LANES)), ind)[0:1]
        xd = xs * dtx
        xdb = _bf(xd)
        xd_end = _bf(xd * dtex)
        dyv = dy_ref[...]
        dy_off = _bf(ex * dyv)
        dyb = _bf(dyv)
        dskx = dskip_ref[...]
        lane_c = lax.broadcasted_iota(jnp.int32, (L, LANES), 1)
        lane1 = lax.broadcasted_iota(jnp.int32, (1, LANES), 1)
        sub16 = lax.broadcasted_iota(jnp.int32, (H, L), 0)
        dlc_c = jnp.zeros((L, LANES), F32)
        dlc_r = jnp.zeros((H, L), F32)
        for g in range(SSM_GROUPS):
            sl = slice(g * half, (g + 1) * half)
            b_lo = SSM_INNER + g * N
            c_lo = SSM_INNER + (SSM_GROUPS + g) * N
            bm, cm = _bf(xc[:, b_lo:b_lo + N]), _bf(xc[:, c_lo:c_lo + N])
            cb = _nt(cm, bm)
            st, dst = sp_ref[0, g], ds_ref[g]
            stb, dstb = _bf(st), _bf(dst)
            dcm = _nt(dy_off[:, sl], stb)
            ds_ref[g] = _tn(cm, dy_off[:, sl]) + dst * cdx[:, sl]
            rr_ref[:, sl] = _nn(bm, dstb)
            yoff_ref[:, sl] = ex[:, sl] * _nn(cm, stb)
            db = _nt(xd_end[:, sl], dstb)
            dcd_ref[:, sl] = _colsum(dst * st)
            dcb = jnp.zeros((L, L), F32)
            for j in range(HPG):
                h = g * HPG + j
                hs = slice(h * P, (h + 1) * P)
                decay = jnp.where(causal, jnp.exp(jnp.where(causal, lc[:, h:h + 1] - lct[h:h + 1, :], 0.0)), 0.0)
                m = cb * decay
                dxd_ref[:, hs] = _tn(_bf(m), dyb[:, hs])
                dm = _nt(dyb[:, hs], xdb[:, hs])
                dcb = dcb + dm * decay
                gm = dm * m
                dlc_c = dlc_c + jnp.where(lane_c == h, _rowsum(gm), 0.0)
                dlc_r = dlc_r + jnp.where(sub16 == h, _colsum(gm), 0.0)
            dcbb = _bf(dcb)
            dcp_ref[:, c_lo:c_lo + N] = dcm + _nn(dcbb, bm)
            dcp_ref[:, b_lo:b_lo + N] = db + _tn(dcbb, cm)
        dxd_diag, rr = dxd_ref[...], rr_ref[...]
        tt = _headsum(rr * xd, ind, single_pass=True) * dte16
        dlc_rt = jnp.concatenate([dlc_r, jnp.zeros((LANES - H, L), F32)], axis=0).T
        dlc = dlc_c - dlc_rt + _headsum(dyv * yoff_ref[...], ind, single_pass=True) - tt
        dcd = _headsum(jnp.broadcast_to(dcd_ref[...], (8, SSM_INNER)), ind)[0:1]
        dlc = dlc + jnp.where(row == L - 1, _colsum(tt) + dcd * cd16, 0.0)
        dla = _tn(tri, dlc, HIGHEST)
        dxd = dxd_diag + dtex * rr
        ddt = _headsum(dxd * xs, ind, single_pass=True) + dla * a
        ddtr = jnp.where(lane_c < H, ddt * _sigmoid(pre), 0.0)
        ddtr_ref[...] = ddtr
        acc_ref[0:1, :] += _colsum(ddtr)
        acc_ref[1:2, :] += jnp.where(lane1 < H, _colsum(dla * dt) * a, 0.0)
        acc_ref[2:3, :] += _headsum(jnp.broadcast_to(_colsum(dyv * xs), (8, SSM_INNER)), ind)[0:1]
        dcp_ref[:, 0:SSM_INNER] = dxd * dtx + dskx * dyv
        dcp_ref[...] = dcp_ref[...] * _dsilu(cpre_v)

    rev = lambda i: (nc - 1 - i, 0)
    rspec = lambda c: pl.BlockSpec((L, c), rev)
    return _call(
        body, (cpre, dtr, pm, ypre, sprev, dya, dtb, alog, dskip, normw, ind), side, name=name,
        out_shape=[_sds((T, SSM_INNER), BF16), _sds((T, CONV_DIM)), _sds((T, LANES)), _sds((8, LANES)), _sds((1, SSM_INNER))],
        grid=(nc,),
        in_specs=[rspec(CONV_DIM), rspec(LANES), rspec(SSM_INNER), rspec(SSM_INNER),
                  pl.BlockSpec((1, SSM_GROUPS, N, half), lambda i: (nc - 1 - i, 0, 0, 0)), rspec(SSM_INNER),
                  _vec_spec(LANES), _vec_spec(LANES), _vec_spec(SSM_INNER), _vec_spec(SSM_INNER), _vec_spec(SSM_INNER, LANES)],
        out_specs=[rspec(SSM_INNER), rspec(CONV_DIM), rspec(LANES), _vec_spec(LANES, 8), _vec_spec(SSM_INNER)],
        scratch=[pltpu.VMEM((SSM_GROUPS, N, half), F32), pltpu.VMEM((L, SSM_INNER), F32), pltpu.VMEM((L, SSM_INNER), F32),
                 pltpu.VMEM((L, SSM_INNER), F32), pltpu.VMEM((L, SSM_INNER), F32), pltpu.VMEM((1, SSM_INNER), F32)],
        sem=("arbitrary",))


def _gmlp_common(u, v, lnw, lnb):
    ug = _gelu(u)
    vg = _gelu(v)
    mu = jnp.mean(vg, axis=-1, keepdims=True)
    cen = vg - mu
    rstd = lax.rsqrt(jnp.mean(cen * cen, axis=-1, keepdims=True) + EPS)
    vhat = cen * rstd
    return ug, rstd, vhat, vhat * lnw + lnb


def _causal_mask():
    row = lax.broadcasted_iota(jnp.int32, (CHUNK, CHUNK), 0)
    col = lax.broadcasted_iota(jnp.int32, (CHUNK, CHUNK), 1)
    return row >= col


def _gmlp_fwd(pm, lnw, lnb, ws, bs_exp, name, side=None):
    T = pm.shape[0]
    nc = T // CHUNK
    L, G = CHUNK, GMLP_GROUPS

    def body(u_ref, v_ref, lnw_ref, lnb_ref, ws_ref, bs_ref, o_ref):
        ug, _, _, vn = _gmlp_common(u_ref[...].astype(F32), v_ref[...].astype(F32), lnw_ref[...], lnb_ref[...])
        causal = _causal_mask()
        for g in range(G):
            sl = slice(g * L, (g + 1) * L)
            wm = _bf(jnp.where(causal, ws_ref[g], 0.0))
            sv = _nn(wm, _bf(vn[:, sl])) + bs_ref[:, sl]
            o_ref[:, sl] = (ug[:, sl] * sv).astype(BF16)

    return _call(
        body, (pm, pm, lnw, lnb, ws, bs_exp), side, name=name, out_shape=[_sds((T, GMLP_INNER), BF16)], grid=(nc,),
        in_specs=[_row_spec(L, GMLP_INNER, 1), _row_spec(L, GMLP_INNER, 2), _vec_spec(GMLP_INNER), _vec_spec(GMLP_INNER),
                  pl.BlockSpec((G, L, L), lambda i: (0, 0, 0)), _vec_spec(GMLP_INNER, L)],
        out_specs=[_row_spec(L, GMLP_INNER)], sem=("parallel",))


def _gmlp_bwd(pm, dyb, lnw, lnb, ws, bs_exp, name, side=None):
    T = pm.shape[0]
    nc = T // CHUNK
    L, G = CHUNK, GMLP_GROUPS

    def body(u_ref, v_ref, dy_ref, lnw_ref, lnb_ref, ws_ref, bs_ref, du_ref, dv_ref, dws_ref, dbs_ref, acc_ref, dvn_ref):
        @pl.when(pl.program_id(0) == 0)
        def _():
            dws_ref[...] = jnp.zeros_like(dws_ref)
            dbs_ref[...] = jnp.zeros_like(dbs_ref)
            acc_ref[...] = jnp.zeros_like(acc_ref)
        uv, vv, dyv, lnwv = u_ref[...].astype(F32), v_ref[...].astype(F32), dy_ref[...], lnw_ref[...]
        ug, rstd, vhat, vn = _gmlp_common(uv, vv, lnwv, lnb_ref[...])
        causal = _causal_mask()
        lane = lax.broadcasted_iota(jnp.int32, (L, LANES), 1)
        dbs = jnp.zeros((L, LANES), F32)
        for g in range(G):
            sl = slice(g * L, (g + 1) * L)
            wm = _bf(jnp.where(causal, ws_ref[g], 0.0))
            vng = _bf(vn[:, sl])
            sv = _nn(wm, vng) + bs_ref[:, sl]
            du_ref[:, sl] = (dyv[:, sl] * sv * _dgelu(uv[:, sl])).astype(BF16)
            dsv = dyv[:, sl] * ug[:, sl]
            dsvb = _bf(dsv)
            dws_ref[g] += jnp.where(causal, _nt(dsvb, vng), 0.0)
            dbs = dbs + jnp.where(lane == g, _rowsum(dsv), 0.0)
            dvn_ref[:, sl] = _tn(wm, dsvb)
        dbs_ref[...] += dbs
        dvn = dvn_ref[...]
        acc_ref[0:1, :] += _colsum(dvn * vhat)
        acc_ref[1:2, :] += _colsum(dvn)
        dvh = dvn * lnwv
        dvg = rstd * (dvh - jnp.mean(dvh, axis=-1, keepdims=True) - vhat * jnp.mean(dvh * vhat, axis=-1, keepdims=True))
        dv_ref[...] = (dvg * _dgelu(vv)).astype(BF16)

    return _call(
        body, (pm, pm, dyb, lnw, lnb, ws, bs_exp), side, name=name,
        out_shape=[_sds((T, GMLP_INNER), BF16), _sds((T, GMLP_INNER), BF16), _sds((G, L, L)), _sds((L, LANES)), _sds((8, GMLP_INNER))],
        grid=(nc,),
        in_specs=[_row_spec(L, GMLP_INNER, 1), _row_spec(L, GMLP_INNER, 2), _row_spec(L, GMLP_INNER),
                  _vec_spec(GMLP_INNER), _vec_spec(GMLP_INNER), pl.BlockSpec((G, L, L), lambda i: (0, 0, 0)),
                  _vec_spec(GMLP_INNER, L)],
        out_specs=[_row_spec(L, GMLP_INNER), _row_spec(L, GMLP_INNER), pl.BlockSpec((G, L, L), lambda i: (0, 0, 0)),
                   _vec_spec(LANES, L), _vec_spec(GMLP_INNER, 8)],
        scratch=[pltpu.VMEM((L, GMLP_INNER), F32)], sem=("arbitrary",))


def _rel_buckets():
    qi = np.arange(CHUNK)[:, None]
    sj = np.arange(2 * CHUNK)[None, :]
    dist = np.maximum(qi + CHUNK - sj, 0)
    max_exact = REL_BUCKETS // 2
    log_ratio = (np.log(np.maximum(dist, 1).astype(np.float32) / np.float32(max_exact))
                 / np.float32(math.log(REL_MAX_DIST / max_exact))).astype(np.float32)
    large = max_exact + (log_ratio * np.float32(REL_BUCKETS - max_exact)).astype(np.int32)
    return np.where(dist < max_exact, dist, np.minimum(large, REL_BUCKETS - 1))


def _bucket_onehot_t():
    bucket = _rel_buckets().reshape(-1)
    return jnp.asarray((np.arange(REL_BUCKETS)[:, None] == bucket[None, :]).astype(np.float32))


def _bias_from_table(table_t, onehot_t, window, name):
    def body(t_ref, o_ref, w_ref, out_ref):
        out_ref[...] = jnp.where(w_ref[...] > 0.5, _nn(t_ref[...], o_ref[...], HIGHEST), NEG_INF)

    return _pcall(body, name=name, out_shape=_sds((ATTN_HEADS, onehot_t.shape[1])))(table_t, onehot_t, window)


def _table_from_dbias(dbias, onehot_t, name):
    def body(d_ref, o_ref, out_ref):
        out_ref[...] = _nt(o_ref[...], d_ref[...], HIGHEST)

    return _pcall(body, name=name, out_shape=_sds((REL_BUCKETS, ATTN_HEADS)))(dbias, onehot_t)


def _softmax_sink(logits, sink):
    mx = jnp.maximum(jnp.max(logits, axis=-1, keepdims=True), sink)
    e = jnp.exp(logits - mx)
    es = jnp.exp(sink - mx)
    inv = 1.0 / (_rowsum(e) + es)
    return e * inv, es * inv


def _first_block_penalty(n):
    sj = lax.broadcasted_iota(jnp.int32, (1, 2 * CHUNK), 1)
    return jnp.where((sj < CHUNK) & (n == 0), NEG_INF, 0.0)


def _window_mask_flat():
    qi = np.arange(CHUNK)[:, None]
    sj = np.arange(2 * CHUNK)[None, :]
    rel = qi + CHUNK - sj
    return jnp.asarray(((rel >= 0) & (rel < CHUNK)).astype(np.float32).reshape(1, -1))


def _stack_heads(ref, first, count, width, scale=None):
    x = jnp.concatenate([_bf(ref[:, (first + j) * width:(first + j + 1) * width]) for j in range(count)], axis=0)
    return x if scale is None else x * jnp.asarray(scale, x.dtype)


def _attn_fwd(qkv, bias, sinks, name):
    T = qkv.shape[0]
    nb = T // CHUNK
    L, DH, HPK = CHUNK, ATTN_DH, ATTN_HEADS // ATTN_KV
    scale = DH ** -0.5
    kcol, vcol = ATTN_HEADS * DH // LANES, ATTN_HEADS * DH // LANES + 1

    def body(q_ref, k_ref, v_ref, kp_ref, vp_ref, bias_ref, sink_ref, o_ref, lg_ref, p_ref):
        n = pl.program_id(0)
        pen = _first_block_penalty(n)
        kband = _bf(jnp.concatenate([kp_ref[...], k_ref[...]], axis=0))
        vband = _bf(jnp.concatenate([vp_ref[...], v_ref[...]], axis=0))
        for kv in range(ATTN_KV):
            lg_ref[...] = _nt(_stack_heads(q_ref, kv * HPK, HPK, DH, scale), kband[:, kv * DH:(kv + 1) * DH])
            for j in range(HPK):
                h = kv * HPK + j
                p, _ = _softmax_sink(lg_ref[j * L:(j + 1) * L, :] + bias_ref[h] + pen, sink_ref[h])
                p_ref[j * L:(j + 1) * L, :] = _bf(p)
            og = _nn(p_ref[...], vband[:, kv * DH:(kv + 1) * DH])
            for j in range(HPK):
                h = kv * HPK + j
                o_ref[:, h * DH:(h + 1) * DH] = og[j * L:(j + 1) * L].astype(BF16)

    prev = lambda i: jnp.maximum(i - 1, 0)
    return _pcall(
        body, name=name, out_shape=_sds((T, ATTN_HEADS * DH), BF16), grid=(nb,),
        in_specs=[_row_spec(L, ATTN_HEADS * DH, 0), _row_spec(L, LANES, kcol), _row_spec(L, LANES, vcol),
                  pl.BlockSpec((L, LANES), lambda i: (prev(i), kcol)), pl.BlockSpec((L, LANES), lambda i: (prev(i), vcol)),
                  pl.BlockSpec((ATTN_HEADS, L, 2 * L), lambda i: (0, 0, 0)),
                  pl.BlockSpec(memory_space=pltpu.SMEM)],
        out_specs=_row_spec(L, ATTN_HEADS * DH),
        scratch=[pltpu.VMEM((HPK * L, 2 * L), F32), pltpu.VMEM((HPK * L, 2 * L), BF16)],
        sem=("parallel",))(qkv, qkv, qkv, qkv, qkv, bias, sinks)


def _attn_bwd(qkv, datt, bias, sinks, name):
    T = qkv.shape[0]
    nb = T // CHUNK
    L, DH, HPK = CHUNK, ATTN_DH, ATTN_HEADS // ATTN_KV
    scale = DH ** -0.5
    kcol, vcol = ATTN_HEADS * DH // LANES, ATTN_HEADS * DH // LANES + 1

    def body(q_ref, k_ref, v_ref, kp_ref, vp_ref, do_ref, bias_ref, sink_ref,
             dq_ref, dk_ref, dv_ref, bsum_ref, dbias_ref, dsink_ref, pend_k, pend_v, band_k, band_v, lg_ref, dp_ref, p_ref, dl_ref):
        n = pl.program_id(0)

        @pl.when(n == 0)
        def _():
            dbias_ref[...] = jnp.zeros_like(dbias_ref)
            dsink_ref[...] = jnp.zeros_like(dsink_ref)
            bsum_ref[...] = jnp.zeros_like(bsum_ref)

        def emit_kv(dk, dv):
            dk_ref[...] = dk.astype(BF16)
            dv_ref[...] = dv.astype(BF16)
            bsum_ref[:, ATTN_HEADS * DH:ATTN_HEADS * DH + LANES] += _colsum(dk)
            bsum_ref[:, ATTN_HEADS * DH + LANES:] += _colsum(dv)

        @pl.when(n < nb)
        def _():
            pen = _first_block_penalty(n)
            kband = _bf(jnp.concatenate([kp_ref[...], k_ref[...]], axis=0))
            vband = _bf(jnp.concatenate([vp_ref[...], v_ref[...]], axis=0))
            lane1 = lax.broadcasted_iota(jnp.int32, (1, LANES), 1)
            dsink = jnp.zeros((1, LANES), F32)
            for kv in range(ATTN_KV):
                kb, vb = kband[:, kv * DH:(kv + 1) * DH], vband[:, kv * DH:(kv + 1) * DH]
                qg = _stack_heads(q_ref, kv * HPK, HPK, DH, scale)
                dog = _stack_heads(do_ref, kv * HPK, HPK, DH)
                lg_ref[...] = _nt(qg, kb)
                dp_ref[...] = _nt(dog, vb)
                for j in range(HPK):
                    h = kv * HPK + j
                    rows = slice(j * L, (j + 1) * L)
                    p, ps = _softmax_sink(lg_ref[rows, :] + bias_ref[h] + pen, sink_ref[h])
                    dp = dp_ref[rows, :]
                    delta = _rowsum(p * dp)
                    dl = p * (dp - delta)
                    dbias_ref[h] += dl
                    p_ref[rows, :] = _bf(p)
                    dl_ref[rows, :] = _bf(dl)
                    dsink = dsink + jnp.where(lane1 == h, -_colsum(ps * delta), 0.0)
                band_v[:, kv * DH:(kv + 1) * DH] = _tn(p_ref[...], dog)
                dqg = _nn(dl_ref[...], kb) * scale
                band_k[:, kv * DH:(kv + 1) * DH] = _tn(dl_ref[...], qg)
                for j in range(HPK):
                    h = kv * HPK + j
                    dq_ref[:, h * DH:(h + 1) * DH] = dqg[j * L:(j + 1) * L].astype(BF16)
                    bsum_ref[:, h * DH:(h + 1) * DH] += _colsum(dqg[j * L:(j + 1) * L])
            dsink_ref[...] += dsink

            @pl.when(n > 0)
            def _():
                emit_kv(pend_k[...] + band_k[0:L, :], pend_v[...] + band_v[0:L, :])
            pend_k[...] = band_k[L:2 * L, :]
            pend_v[...] = band_v[L:2 * L, :]

        @pl.when(n == nb)
        def _():
            emit_kv(pend_k[...], pend_v[...])

    cur = lambda i: jnp.minimum(i, nb - 1)
    prev = lambda i: jnp.maximum(jnp.minimum(i, nb - 1) - 1, 0)
    lag = lambda i: jnp.maximum(i - 1, 0)
    return _pcall(
        body, name=name,
        out_shape=[_sds((T, ATTN_HEADS * DH), BF16), _sds((T, LANES), BF16), _sds((T, LANES), BF16), _sds((1, QKV_DIM)),
                   _sds((ATTN_HEADS, L, 2 * L)), _sds((1, LANES))],
        grid=(nb + 1,),
        in_specs=[pl.BlockSpec((L, ATTN_HEADS * DH), lambda i: (cur(i), 0)),
                  pl.BlockSpec((L, LANES), lambda i: (cur(i), kcol)), pl.BlockSpec((L, LANES), lambda i: (cur(i), vcol)),
                  pl.BlockSpec((L, LANES), lambda i: (prev(i), kcol)), pl.BlockSpec((L, LANES), lambda i: (prev(i), vcol)),
                  pl.BlockSpec((L, ATTN_HEADS * DH), lambda i: (cur(i), 0)),
                  pl.BlockSpec((ATTN_HEADS, L, 2 * L), lambda i: (0, 0, 0)),
                  pl.BlockSpec(memory_space=pltpu.SMEM)],
        out_specs=[pl.BlockSpec((L, ATTN_HEADS * DH), lambda i: (cur(i), 0)),
                   pl.BlockSpec((L, LANES), lambda i: (lag(i), 0)), pl.BlockSpec((L, LANES), lambda i: (lag(i), 0)),
                   _vec_spec(QKV_DIM), pl.BlockSpec((ATTN_HEADS, L, 2 * L), lambda i: (0, 0, 0)), _vec_spec(LANES)],
        scratch=[pltpu.VMEM((L, LANES), F32), pltpu.VMEM((L, LANES), F32),
                 pltpu.VMEM((2 * L, LANES), F32), pltpu.VMEM((2 * L, LANES), F32),
                 pltpu.VMEM((HPK * L, 2 * L), F32), pltpu.VMEM((HPK * L, 2 * L), F32),
                 pltpu.VMEM((HPK * L, 2 * L), BF16), pltpu.VMEM((HPK * L, 2 * L), BF16)],
        sem=("arbitrary",))(qkv, qkv, qkv, qkv, qkv, datt, bias, sinks)


def _pad_rows(a, mult):
    pad = (-a.shape[-2]) % mult
    if pad == 0:
        return a
    cfg = [(0, 0)] * (a.ndim - 2) + [(0, pad), (0, 0)]
    return jnp.pad(a, cfg)


class _Pack:
    def __init__(self, width, mult, total_mult):
        self.width, self.mult, self.total_mult = width, mult, total_mult
        self.entries = []
        self.rows = 0

    def add(self, key, shape):
        n = int(np.prod(shape))
        rows = -(-n // self.width)
        self.entries.append((key, self.rows, rows, tuple(shape)))
        self.rows += -(-rows // self.mult) * self.mult

    @property
    def total(self):
        return -(-self.rows // self.total_mult) * self.total_mult

    def pack(self, pieces, dtype, lead=()):
        parts = []
        for key, _, rows, shape in self.entries:
            a = pieces[key].astype(dtype).reshape(lead + (-1,))
            n = int(np.prod(shape))
            a = jnp.pad(a, [(0, 0)] * len(lead) + [(0, rows * self.width - n)])
            a = a.reshape(lead + (rows, self.width))
            parts.append(_pad_rows(a, self.mult))
        out = jnp.concatenate(parts, axis=len(lead))
        return _pad_rows(out, self.total_mult)

    def unpack(self, packed, lead=()):
        out = {}
        for key, off, rows, shape in self.entries:
            a = lax.slice_in_dim(packed, off, off + rows, axis=len(lead))
            a = a.reshape(lead + (-1,))
            n = int(np.prod(shape))
            out[key] = lax.slice_in_dim(a, 0, n, axis=len(lead)).reshape(lead + shape)
        return out


def _ffn_fwd(x, h, mod, wg_t, wu_t, wd, tag, next_norm=None, gather=None):
    side = None if gather is None else (_GatherOps, gather)
    gate, up, act, *gathered = _mm_swiglu(h, wg_t, wu_t, f"ffn_gateup_{tag}", side=side)
    x_out, ffn_out, *h_next = _mm_resid([(act, wd)], x, mod[5:6], name=f"ffn_down_{tag}", norm=next_norm)
    return (x_out, dict(h=h, gate=gate, up=up, act=act, out=ffn_out), *h_next, *gathered)


def _ffn_bwd(dx_out, dffn, x_in, saved, mod, norm_w, wg_t, wu_t, wd, below, tag, exchange=None):
    side = None if exchange is None else (_ChipsOps, exchange)
    dgate, dup, *from_chips = _mm_swiglu_bwd(dffn, wd, saved["gate"], saved["up"], f"ffn_act_bwd_{tag}", side=side)
    d_wd = _mm_tn(saved["act"], dffn, name=f"ffn_dwd_{tag}")
    d_wg_t = _mm_tn(dgate, saved["h"], name=f"ffn_dwg_{tag}")
    d_wu_t = _mm_tn(dup, saved["h"], name=f"ffn_dwu_{tag}")
    dh = _mm([(dgate, wg_t), (dup, wu_t)], "nn", name=f"ffn_dh_{tag}")
    dx, d_below, acc = _norm_mod_bwd(x_in, dh, dx_out, norm_w, mod[4:5], below[0], below[1], f"ffn_norm_bwd_{tag}")
    return (dx, d_below, dict(d_wg=d_wg_t, d_wu=d_wu_t, d_wd=d_wd, acc=acc), *from_chips)


_BIG = [
    ("out_w", "out_w_even", 0, "row"), ("qkv_w", "qkv_w", 0, "col"), ("o_w", "o_w", 0, "row"),
    ("gate0", "ffn_gate_w", 0, "col"), ("up0", "ffn_up_w", 0, "col"), ("down0", "ffn_down_w", 0, "row"),
    ("gate1", "ffn_gate_w", 1, "col"), ("up1", "ffn_up_w", 1, "col"), ("down1", "ffn_down_w", 1, "row"),
    ("in_w", "in_w_even", 0, "col"),
]


def _to_wire(a, kind):
    return a.T if kind == "col" else a


_GATHER_GROUPS = [["in_w"], ["out_w"], ["gate0", "up0", "down0"], ["qkv_w", "o_w"], ["gate1", "up1", "down1"]]
_GRAD_GROUPS = [["qkv_w", "o_w", "gate1", "up1", "down1"], ["out_w", "gate0", "up0", "down0"], ["in_w"]]

_REPLICATED = ["ada_b", "norm_mix_w", "norm_ffn_w", "conv_b", "dt_bias", "a_log", "d_skip", "ssm_norm_w", "gmlp_ln_w",
               "gmlp_ln_b", "gmlp_ws", "gmlp_bs", "sinks", "rel_table", "final_norm_w"]
_TINY_SHARDED = ["conv_w", "qkv_b", "o_b"]

_WEIGHTS = ['ada_w', 'ada_b', 'norm_mix_w', 'norm_ffn_w', 'in_w_even', 'conv_w', 'conv_b', 'dt_bias', 'a_log', 'd_skip',
            'ssm_norm_w', 'gmlp_ln_w', 'gmlp_ln_b', 'gmlp_ws', 'gmlp_bs', 'out_w_even', 'qkv_w', 'qkv_b', 'o_w', 'o_b',
            'sinks', 'rel_table', 'ffn_gate_w', 'ffn_up_w', 'ffn_down_w', 'final_norm_w']


def _step(x, c, loss_target, W, M, V):
    T = x.shape[1]
    x0 = x[0]
    target = loss_target[0]
    me = 4 * lax.axis_index("x") + 2 * lax.axis_index("y") + lax.axis_index("c")

    w_wire_local = {key: _to_wire(W[name][layer].astype(BF16), kind) for key, name, layer, kind in _BIG}

    def wire_pack(keys, mult):
        gp = _Pack(D, 1, mult)
        for key in keys:
            gp.add(key, w_wire_local[key].shape)
        return gp

    gather_packs = [(gp, gp.pack(w_wire_local, BF16)) for gp in (wire_pack(keys, 16) for keys in _GATHER_GROUPS)]
    grad_packs = [wire_pack(keys, 64) for keys in _GRAD_GROUPS]
    full = {}

    def gathered_weights(group, gathered):
        shards = gather_packs[group][0].unpack(gathered, lead=(N_DEV,))
        full.update({key: a.reshape(-1, D) for key, a in shards.items()})


    small_in = _Pack(D, 8, 8)
    small_in.add("c", (1, D))
    small_in.add("conv_w", W["conv_w"][0].shape)
    small_in.add("qkv_b", W["qkv_b"][0].shape)
    small_in.add("o_b", W["o_b"][0].shape)
    sm = small_in.unpack(_all_gather(small_in.pack(
        dict(c=c, conv_w=W["conv_w"][0], qkv_b=W["qkv_b"][0], o_b=W["o_b"][0]), F32), "gather_small"), lead=(N_DEV,))
    c_all = sm["c"].reshape(N_DEV, D)
    conv_w_full = jnp.transpose(sm["conv_w"], (1, 0, 2)).reshape(SSM_CONV, CONV_DIM)
    qkv_b_full = sm["qkv_b"].reshape(1, QKV_DIM)
    o_b_full = sm["o_b"].reshape(1, D)

    ncol = W["ada_w"].shape[2]
    cond, mod_cols = _mod_matmul(c_all, W["ada_w"], "mod_matmul")
    mod_g = _all_gather(mod_cols.reshape(DEPTH * N_DEV, ncol), "gather_mod").reshape(N_DEV, DEPTH, N_DEV, ncol)
    mod_me = lax.dynamic_index_in_dim(mod_g, me, axis=2, keepdims=False)
    mod_me = jnp.transpose(mod_me, (1, 0, 2)).reshape(DEPTH, 6, D)
    mod_me = jnp.pad(mod_me, ((0, 0), (0, 2), (0, 0))).reshape(DEPTH * 8, D)
    ada_b_rows = jnp.pad(W["ada_b"].reshape(DEPTH, 6, D), ((0, 0), (0, 2), (0, 0))).reshape(DEPTH * 8, D)
    mod_all = _add_rows(mod_me, ada_b_rows, "mod_bias").reshape(DEPTH, 8, D)
    mod0, mod1 = mod_all[0], mod_all[1]
    h0, gathered_in = _norm_mod(x0, W["norm_mix_w"][0:1], mod0[1:2], mod0[0:1], "mix_norm_0",
                                side=(_GatherOps, gather_packs[0][1]))
    gathered_weights(0, gathered_in)

    in_t = full["in_w"]
    o1, o2, o3, o4 = SSM_INNER, SSM_INNER + CONV_DIM, SSM_INNER + CONV_DIM + SSM_HEADS, SSM_INNER + CONV_DIM + SSM_HEADS + GMLP_INNER
    w_z, w_xbc, w_dt, w_u, w_v = in_t[:o1], in_t[o1:o2], in_t[o2:o3], in_t[o3:o4], in_t[o4:]
    w_main = jnp.concatenate([w_z, w_u, w_v, w_xbc], axis=0)
    w_dtp = jnp.pad(w_dt, ((0, LANES - SSM_HEADS), (0, 0)))

    pad16 = lambda a: jnp.pad(a.reshape(1, SSM_HEADS), ((0, 0), (0, LANES - SSM_HEADS)))
    dtb, alog = pad16(W["dt_bias"][0]), pad16(W["a_log"][0])
    dskip = jnp.repeat(W["d_skip"][0], SSM_HEAD_DIM).reshape(1, SSM_INNER)
    ssm_nw = W["ssm_norm_w"]
    lnw, lnb = W["gmlp_ln_w"], W["gmlp_ln_b"]
    ws = W["gmlp_ws"][0]
    bs_exp = jnp.repeat(W["gmlp_bs"][0].T, CHUNK, axis=1)
    conv_b = W["conv_b"]
    nmw, nfw = W["norm_mix_w"], W["norm_ffn_w"]
    onehot_t = _bucket_onehot_t()
    head_ind = _head_indicator()
    bias = _bias_from_table(W["rel_table"].T, onehot_t, _window_mask_flat(), "rel_bias").reshape(ATTN_HEADS, CHUNK, 2 * CHUNK)
    sinks = W["sinks"][0]

    pm, gathered_a = _mm([(h0, w_main)], "nt", name="in_proj", tn_pref=1536, out_dtype=BF16,
                         side=(_GatherOps, gather_packs[1][1]))
    gathered_weights(1, gathered_a)
    out_w = full["out_w"]
    dtr = _mm([(h0, w_dtp)], "nt", name="in_proj_dt")
    cpre = _conv_fwd(pm, conv_w_full, conv_b, "conv_fwd")
    ya, ypre, sprev, gathered_b = _ssd_fwd(cpre, dtr, pm, dtb, alog, dskip, ssm_nw, head_ind, "ssd_fwd",
                                           side=(_GatherOps, gather_packs[2][1]))
    gathered_weights(2, gathered_b)
    yb, gathered_c = _gmlp_fwd(pm, lnw, lnb, ws, bs_exp, "gmlp_fwd", side=(_GatherOps, gather_packs[3][1]))
    gathered_weights(3, gathered_c)
    x1, mix0, hf0 = _mm_resid([(ya, out_w[:SSM_INNER]), (yb, out_w[SSM_INNER:])], x0, mod0[2:3], name="out_proj",
                              norm=(nfw[0:1], mod0[4:5], mod0[3:4]))
    x2, ffn0, h1, gathered_d = _ffn_fwd(x1, hf0, mod0, full["gate0"], full["up0"], full["down0"], "0",
                                        next_norm=(nmw[1:2], mod1[1:2], mod1[0:1]), gather=gather_packs[4][1])
    gathered_weights(4, gathered_d)
    qkv_t, o_w = full["qkv_w"], full["o_w"]
    w_q, w_k, w_v_att = qkv_t[:D], qkv_t[D:D + LANES], qkv_t[D + LANES:]

    qkv = _mm([(h1, qkv_t)], "nt", name="qkv_proj", bias=qkv_b_full, tn_pref=1280, out_dtype=BF16)
    att = _attn_fwd(qkv, bias, sinks, "attn_fwd")
    x3, mix1, hf1 = _mm_resid([(att, o_w)], x2, mod1[2:3], name="o_proj", bias=o_b_full,
                              norm=(nfw[1:2], mod1[4:5], mod1[3:4]))
    x4, ffn1 = _ffn_fwd(x3, hf1, mod1, full["gate1"], full["up1"], full["down1"], "1")

    dx4, dffn1, acc_f = _final_loss(x4, W["final_norm_w"].reshape(1, D), target, ffn1["out"], mod1[5:6], "final_loss")
    dx3, dmix1, gf1 = _ffn_bwd(dx4, dffn1, x3, ffn1, mod1, nfw[1:2], full["gate1"], full["up1"], full["down1"],
                               (mix1, mod1[2:3]), "1")
    datt = _mm([(dmix1, o_w)], "nt", name="o_proj_dx", out_dtype=BF16)
    d_o_w = _mm_tn(att, dmix1, name="o_proj_dw")
    dq, dk, dv, d_qkv_b, dbias, dsinks = _attn_bwd(qkv, datt, bias, sinks, "attn_bwd")
    d_table = _table_from_dbias(dbias.reshape(ATTN_HEADS, -1), onehot_t, "rel_table_grad")
    d_qkv_t = jnp.concatenate([_mm_tn(dq, h1, name="qkv_dw_q"), _mm_tn(dk, h1, name="qkv_dw_k"), _mm_tn(dv, h1, name="qkv_dw_v")], axis=0)
    g_wire = dict(qkv_w=d_qkv_t, o_w=d_o_w, gate1=gf1["d_wg"], up1=gf1["d_wu"], down1=gf1["d_wd"])

    def packed_partials(group):
        return grad_packs[group].pack({key: g_wire[key].reshape(N_DEV, -1, D) for key in _GRAD_GROUPS[group]},
                                      F32, lead=(N_DEV,))

    partials_l1 = packed_partials(0)
    dh1, theirs_l1 = _mm([(dq, w_q), (dk, w_k), (dv, w_v_att)], "nn", name="qkv_proj_dx", side=(_SiblingOps, partials_l1))
    pair_l1 = _pair_sum(partials_l1, theirs_l1, "grads_pair_sum_l1")
    dx2, dffn0, acc_n1 = _norm_mod_bwd(x2, dh1, dx3, nmw[1:2], mod1[1:2], ffn0["out"], mod0[5:6], "mix_norm_bwd_1")

    from_chips = {}
    dx1, dmix0, gf0, from_chips[0] = _ffn_bwd(dx2, dffn0, x1, ffn0, mod0, nfw[0:1], full["gate0"], full["up0"], full["down0"],
                                              (mix0, mod0[2:3]), "0", exchange=pair_l1)

    dya = _mm([(dmix0, out_w[:SSM_INNER])], "nt", name="out_proj_dx_a")
    dyb = _mm([(dmix0, out_w[SSM_INNER:])], "nt", name="out_proj_dx_b")
    d_out_w = jnp.concatenate([_mm_tn(ya, dmix0, name="out_proj_dw_a"), _mm_tn(yb, dmix0, name="out_proj_dw_b")], axis=0)
    g_wire.update(gate0=gf0["d_wg"], up0=gf0["d_wu"], down0=gf0["d_wd"], out_w=d_out_w)
    partials_ffn0 = packed_partials(1)
    du, dvg, d_ws, d_bs, acc_ln, theirs_ffn0 = _gmlp_bwd(pm, dyb, lnw, lnb, ws, bs_exp, "gmlp_bwd",
                                                         side=(_SiblingOps, partials_ffn0))
    pair_ffn0 = _pair_sum(partials_ffn0, theirs_ffn0, "grads_pair_sum_mix")
    dz, dcpre, ddtr, acc_ssd, d_ssm_nw, from_chips[1] = _ssd_bwd(
        cpre, dtr, pm, ypre, sprev, dya, dtb, alog, dskip, ssm_nw, head_ind, "ssd_bwd", side=(_ChipsOps, pair_ffn0))
    dxbc, acc_conv = _conv_bwd(dcpre, pm, conv_w_full, "conv_bwd")
    d_in_t = jnp.concatenate([
        _mm_tn(dz, h0, name="in_dw_z"), _mm_tn(dxbc, h0, name="in_dw_xbc"),
        _mm_tn(ddtr, h0, name="in_dw_dt")[:SSM_HEADS], _mm_tn(du, h0, name="in_dw_u"), _mm_tn(dvg, h0, name="in_dw_v")], axis=0)
    g_wire.update(in_w=d_in_t)
    partials_mix = packed_partials(2)
    theirs_mix = _comm_call(_SiblingOps, partials_mix, "exchange_grads_sibling")
    pair_mix = _pair_sum(partials_mix, theirs_mix, "grads_pair_sum")
    dh0, from_chips[2] = _mm([(dz, w_z), (dxbc, w_xbc), (ddtr, w_dtp), (du, w_u), (dvg, w_v)], "nn", name="in_proj_dx",
                             side=(_ChipsOps, pair_mix))
    grad_x, acc_n0 = _norm_mod_bwd(x0, dh0, dx1, nmw[0:1], mod0[1:2], None, None, "mix_norm_bwd_0")

    g_mine = {}
    for group in range(len(_GRAD_GROUPS)):
        g_mine.update(grad_packs[group].unpack(_sum_parts(from_chips[group], f"grads_chip_sum_{group}")))
    res_big = [{}, {}, {}, {}]
    for key, name, layer, kind in _BIG:
        g_nat = _to_wire(g_mine[key], kind)
        outs = _adamw(g_nat[None], W[name][layer], M[name][layer], V[name][layer], f"adamw_{key}")
        for res, out in zip(res_big, outs):
            res[key] = out

    acc_f0, acc_f1 = gf0["acc"], gf1["acc"]
    views = dict(ada_b=(DEPTH * 6, D), norm_mix_w=(DEPTH, D), norm_ffn_w=(DEPTH, D), conv_b=(1, CONV_DIM),
                 dt_bias=(1, SSM_HEADS), a_log=(1, SSM_HEADS), d_skip=(1, SSM_HEADS), ssm_norm_w=(1, D), gmlp_ln_w=(1, D),
                 gmlp_ln_b=(1, D), gmlp_ws=(GMLP_GROUPS * CHUNK, CHUNK), gmlp_bs=(GMLP_GROUPS, CHUNK),
                 sinks=(1, ATTN_HEADS), rel_table=(REL_BUCKETS, ATTN_HEADS), final_norm_w=(1, D),
                 conv_w=(SSM_CONV, CONV_DIM), qkv_b=(1, QKV_DIM), o_b=(1, D), loss=(1, D))
    canvas = _Canvas()
    for key, (r, cdim) in views.items():
        canvas.add(key, r, cdim, blocks=GMLP_GROUPS if key == "gmlp_ws" else 1)
    row = lambda i: slice(i, i + 1)
    sources = [
        (acc_n0, [("ada_b", row(1), 0), ("ada_b", row(0), 1), ("norm_mix_w", row(2), 0)]),
        (acc_f0, [("ada_b", row(3), 2), ("ada_b", row(1), 3), ("ada_b", row(0), 4), ("norm_ffn_w", row(2), 0)]),
        (acc_n1, [("ada_b", row(3), 5), ("ada_b", row(1), 6), ("ada_b", row(0), 7), ("norm_mix_w", row(2), 1)]),
        (acc_f1, [("ada_b", row(3), 8), ("ada_b", row(1), 9), ("ada_b", row(0), 10), ("norm_ffn_w", row(2), 1),
                  ("o_b", row(4), 0)]),
        (acc_f, [("ada_b", row(3), 11), ("final_norm_w", row(0), 0), ("loss", row(1), 0)]),
        (acc_conv, [("conv_w", slice(0, SSM_CONV), 0), ("conv_b", row(4), 0)]),
        (acc_ssd, [("dt_bias", row(0), 0), ("a_log", row(1), 0), ("d_skip", row(2), 0)]),
        (d_ssm_nw, [("ssm_norm_w", row(0), 0)]),
        (acc_ln, [("gmlp_ln_w", row(0), 0), ("gmlp_ln_b", row(1), 0)]),
        (d_ws.reshape(GMLP_GROUPS * CHUNK, CHUNK), [("gmlp_ws", slice(0, GMLP_GROUPS * CHUNK), 0)]),
        (d_bs.T, [("gmlp_bs", slice(0, GMLP_GROUPS), 0)]),
        (dsinks, [("sinks", row(0), 0)]),
        (d_table, [("rel_table", slice(0, REL_BUCKETS), 0)]),
        (d_qkv_b, [("qkv_b", row(0), 0)]),
    ]
    parts_small = _all_gather(_canvas_fill(canvas, sources, "small_grads_canvas"), "gather_small_grads")
    view = lambda a, key: a.reshape(views[key])
    params = [(name, view(W[name], name), view(M[name], name), view(V[name], name)) for name in _REPLICATED]
    small_out = _adamw_canvas(canvas, parts_small, params, _TINY_SHARDED + ["loss"], "adamw_small")
    loss = small_out["loss"][0][0, 0]
    res_small = [{name: small_out[name][k].reshape(W[name].shape) for name in _REPLICATED} for k in range(4)]

    n_cw, n_qb, n_ob = W["conv_w"].shape[2], W["qkv_b"].shape[1], W["o_b"].shape[1]
    g_tiny = dict(conv_w=lax.dynamic_slice_in_dim(small_out["conv_w"][0], me * n_cw, n_cw, axis=1)[None],
                  qkv_b=lax.dynamic_slice_in_dim(small_out["qkv_b"][0], me * n_qb, n_qb, axis=1),
                  o_b=lax.dynamic_slice_in_dim(small_out["o_b"][0], me * n_ob, n_ob, axis=1))
    tiny = _Pack(D, 8, 8)
    for name in _TINY_SHARDED:
        tiny.add(name, W[name].shape)
    pkt = lambda S: tiny.pack({name: S[name] for name in _TINY_SHARDED}, F32)
    res_tiny = [tiny.unpack(r) for r in _adamw(pkt(g_tiny)[None], pkt(W), pkt(M), pkt(V), "adamw_tiny")]

    dmod_all = parts_small[:, canvas.offset["ada_b"]:canvas.offset["ada_b"] + DEPTH * 6].reshape(N_DEV, DEPTH, 6 * D)
    dmod_cols = jnp.transpose(lax.dynamic_slice_in_dim(dmod_all, me * ncol, ncol, axis=2), (1, 0, 2))
    g_ada_w = _ada_w_grad(cond, dmod_cols, "ada_w_grad")
    flat = lambda a: a.reshape(DEPTH * D, ncol)
    res_ada = [r.reshape(DEPTH, D, ncol) for r in _adamw(flat(g_ada_w)[None], flat(W["ada_w"]), flat(M["ada_w"]), flat(V["ada_w"]), "adamw_ada_w")]

    def result(kind_idx, name):
        if name == "ada_w":
            return res_ada[kind_idx]
        if name in _REPLICATED:
            return res_small[kind_idx][name]
        if name in _TINY_SHARDED:
            return res_tiny[kind_idx][name]
        pieces = [res_big[kind_idx][key] for key, nm, layer, kind in _BIG if nm == name]
        return jnp.stack(pieces)

    outs = [loss, grad_x[None]]
    for kind_idx in range(4):
        outs += [result(kind_idx, name) for name in _WEIGHTS]
    return tuple(outs)


def kernel(x, c, ada_w, ada_b, norm_mix_w, norm_ffn_w, in_w_even, conv_w, conv_b, dt_bias, a_log, d_skip, ssm_norm_w, gmlp_ln_w, gmlp_ln_b, gmlp_ws, gmlp_bs, out_w_even, qkv_w, qkv_b, o_w, o_b, sinks, rel_table, ffn_gate_w, ffn_up_w, ffn_down_w, final_norm_w, loss_target, m_ada_w, m_ada_b, m_norm_mix_w, m_norm_ffn_w, m_in_w_even, m_conv_w, m_conv_b, m_dt_bias, m_a_log, m_d_skip, m_ssm_norm_w, m_gmlp_ln_w, m_gmlp_ln_b, m_gmlp_ws, m_gmlp_bs, m_out_w_even, m_qkv_w, m_qkv_b, m_o_w, m_o_b, m_sinks, m_rel_table, m_ffn_gate_w, m_ffn_up_w, m_ffn_down_w, m_final_norm_w, v_ada_w, v_ada_b, v_norm_mix_w, v_norm_ffn_w, v_in_w_even, v_conv_w, v_conv_b, v_dt_bias, v_a_log, v_d_skip, v_ssm_norm_w, v_gmlp_ln_w, v_gmlp_ln_b, v_gmlp_ws, v_gmlp_bs, v_out_w_even, v_qkv_w, v_qkv_b, v_o_w, v_o_b, v_sinks, v_rel_table, v_ffn_gate_w, v_ffn_up_w, v_ffn_down_w, v_final_norm_w):
    args = locals()
    W = {n: args[n] for n in _WEIGHTS}
    M = {n: args["m_" + n] for n in _WEIGHTS}
    V = {n: args["v_" + n] for n in _WEIGHTS}
    return _step(x, c, loss_target, W, M, V)
```

```python
import functools
import math

import numpy as np
import jax
import jax.numpy as jnp
from jax import lax
from jax.experimental import pallas as pl
from jax.experimental.pallas import tpu as pltpu

F32 = jnp.float32
BF16 = jnp.bfloat16
HIGHEST = lax.Precision.HIGHEST
MESH = pl.DeviceIdType.MESH

N_DEV = 8
D = 1024
DEPTH = 2
SSM_HEADS = 16
SSM_HEAD_DIM = 64
SSM_INNER = 1024
SSM_GROUPS = 2
SSM_STATE = 128
SSM_CONV = 4
CHUNK = 128
CONV_DIM = SSM_INNER + 2 * SSM_GROUPS * SSM_STATE
GMLP_GROUPS = 8
GMLP_INNER = 1024
IN_EVEN = 4624
ATTN_HEADS = 16
ATTN_KV = 2
ATTN_DH = 64
QKV_DIM = 1280
REL_BUCKETS = 32
REL_MAX_DIST = 128
FFN = 2816
EPS = 1e-6
NEG_INF = -1e30
LANES = 128

ADAM_LR = 0.001
ADAM_B1 = 0.9
ADAM_B2 = 0.999
ADAM_EPS = 1e-08
ADAM_WD = 0.01
ADAM_STEP = 10

VMEM_LIMIT_BYTES = 56 * 1024 * 1024
ROW_TILE = 512


def _pcall(body, *, name, out_shape, grid=(), in_specs=None, out_specs=None, scratch=(), sem=None):
    params = dict(vmem_limit_bytes=VMEM_LIMIT_BYTES)
    if sem is not None:
        params["dimension_semantics"] = sem
    specs = {} if in_specs is None else dict(in_specs=in_specs, out_specs=out_specs)
    return pl.pallas_call(
        body, name=name, out_shape=out_shape, grid=grid, **specs,
        scratch_shapes=list(scratch), compiler_params=pltpu.CompilerParams(**params))


def _call(body, args, side=None, *, name, out_shape, grid, in_specs, out_specs, scratch=(), sem=None):
    if side is None:
        return _pcall(body, name=name, out_shape=out_shape, grid=grid, in_specs=in_specs, out_specs=out_specs,
                      scratch=scratch, sem=sem)(*args)
    ops_cls, x = side
    n_in, n_out, n_scr = len(in_specs), len(out_shape), len(scratch)
    steps = int(np.prod(grid))
    hbm = pl.BlockSpec(memory_space=pl.ANY)

    def wrapped(*refs):
        ins, x_ref = refs[:n_in], refs[n_in]
        outs, r_ref = refs[n_in + 1:n_in + 1 + n_out], refs[n_in + 1 + n_out]
        scr, sems = refs[n_in + 2 + n_out:n_in + 2 + n_out + n_scr], refs[n_in + 2 + n_out + n_scr:]
        ops = ops_cls(x_ref, r_ref, *sems)
        step = pl.program_id(0)
        for axis in range(1, len(grid)):
            step = step * grid[axis] + pl.program_id(axis)
        pl.when(step == 0)(ops.start)
        body(*ins, *outs, *scr)
        pl.when(step == (3 * steps) // 4)(ops.forward)
        pl.when(step == steps - 1)(ops.finish)

    return _pcall(
        wrapped, name=name, out_shape=list(out_shape) + [ops_cls.result(x)], grid=grid,
        in_specs=list(in_specs) + [hbm], out_specs=list(out_specs) + [hbm],
        scratch=list(scratch) + ops_cls.scratch(), sem=("arbitrary",) * len(grid))(*args, x)


def _tile(n, pref):
    if n <= pref:
        return n
    best = None
    for t in range(LANES, pref + 1, LANES):
        if n % t == 0:
            best = t
    assert best is not None, (n, pref)
    return best


def _rows(T):
    return min(ROW_TILE, T)


def _sds(shape, dtype=F32):
    return jax.ShapeDtypeStruct(shape, dtype)


def _row_spec(tm, c, col=0):
    return pl.BlockSpec((tm, c), lambda i, col=col: (i, col))


def _vec_spec(c, r=1):
    return pl.BlockSpec((r, c), lambda i: (0, 0))


def _sigmoid(x):
    return jax.nn.sigmoid(x)


def _silu(x):
    return x * _sigmoid(x)


def _dsilu(x):
    s = _sigmoid(x)
    return s * (1.0 + x * (1.0 - s))


def _gelu(x):
    return 0.5 * x * (1.0 + lax.erf(x * 0.7071067811865476))


def _dgelu(x):
    return 0.5 * (1.0 + lax.erf(x * 0.7071067811865476)) + x * jnp.exp(-0.5 * x * x) * 0.3989422804014327


def _dot(a, b, dims, precision=None):
    return lax.dot_general(a, b, (dims, ((), ())), precision=precision, preferred_element_type=F32)


def _nn(a, b, precision=None):
    return _dot(a, b, ((1,), (0,)), precision)


def _nt(a, b, precision=None):
    return _dot(a, b, ((1,), (1,)), precision)


def _tn(a, b, precision=None):
    return _dot(a, b, ((0,), (0,)), precision)


def _bf(x):
    return x.astype(BF16)


def _colsum(x):
    return jnp.sum(x, axis=0, keepdims=True)


def _rowsum(x):
    return jnp.sum(x, axis=1, keepdims=True)


def _allsum(x):
    return _colsum(_rowsum(x))


def _comm_call(ops_cls, x, name, from_vmem=False):
    def body(x_ref, out_ref, *sems):
        ops = ops_cls(x_ref, out_ref, *sems)
        ops.start()
        ops.forward()
        ops.finish()

    return pl.pallas_call(
        body, name=name, out_shape=ops_cls.result(x),
        in_specs=[pl.BlockSpec(memory_space=pltpu.VMEM if from_vmem else pl.ANY)],
        out_specs=pl.BlockSpec(memory_space=pl.ANY), scratch_shapes=ops_cls.scratch(),
    )(x)


def _all_gather(x, name):
    return _comm_call(_GatherOps, x, name, from_vmem=True)


class _GatherOps:
    def __init__(self, x_ref, out_ref, send_sems, recv_sems, local_sem):
        self.x_ref, self.out_ref = x_ref, out_ref
        self.send_sems, self.recv_sems, self.local_sem = send_sems, recv_sems, local_sem
        mx, my, mc = lax.axis_index("x"), lax.axis_index("y"), lax.axis_index("c")
        self.mc = mc
        self.me, self.sibling = (mx, my, mc), (mx, my, 1 - mc)
        self.chips = [(1 - mx, my), (mx, 1 - my), (1 - mx, 1 - my)]

    @staticmethod
    def result(x):
        return _sds((N_DEV,) + x.shape, x.dtype)

    @staticmethod
    def scratch():
        return [pltpu.SemaphoreType.DMA((7,)), pltpu.SemaphoreType.DMA((7,)), pltpu.SemaphoreType.DMA(())]

    def _slot(self, px, py, pc):
        return self.out_ref.at[4 * px + 2 * py + pc]

    def _copy(self, k, block, to, own=False):
        return pltpu.make_async_remote_copy(
            src_ref=self.x_ref if own else self._slot(*block), dst_ref=self._slot(*block),
            send_sem=self.send_sems.at[k], recv_sem=self.recv_sems.at[k], device_id=to, device_id_type=MESH)

    def _mine(self):
        return pltpu.make_async_copy(self.x_ref, self._slot(*self.me), self.local_sem)

    def _first(self):
        return [self._copy(0, self.me, self.sibling, own=True)] + [
            self._copy(1 + j, self.me, (*chip, self.mc), own=True) for j, chip in enumerate(self.chips)]

    def _passed(self):
        return [self._copy(4 + j, (*chip, self.mc), self.sibling) for j, chip in enumerate(self.chips)]

    def start(self):
        self._mine().start()
        for cp in self._first():
            cp.start()

    def forward(self):
        passed = self._passed()
        for j, chip in enumerate(self.chips):
            self._copy(1 + j, (*chip, self.mc), self.me).wait_recv()
            passed[j].start()

    def finish(self):
        self._copy(0, self.sibling, self.me).wait_recv()
        for j, chip in enumerate(self.chips):
            self._copy(4 + j, (*chip, 1 - self.mc), self.me).wait_recv()
        for cp in self._first() + self._passed():
            cp.wait_send()
        self._mine().wait()


N_CHIP = 4


class _SiblingOps:
    def __init__(self, p_ref, theirs_ref, send_sems, recv_sems):
        self.p_ref, self.theirs_ref, self.send_sems, self.recv_sems = p_ref, theirs_ref, send_sems, recv_sems

    @staticmethod
    def result(p):
        return _sds((N_CHIP,) + p.shape[1:], p.dtype)

    @staticmethod
    def scratch():
        return [pltpu.SemaphoreType.DMA((N_CHIP,))] * 2

    def _copies(self):
        mx, my, mc = lax.axis_index("x"), lax.axis_index("y"), lax.axis_index("c")
        return [pltpu.make_async_remote_copy(
            src_ref=self.p_ref.at[2 * chip + 1 - mc], dst_ref=self.theirs_ref.at[chip],
            send_sem=self.send_sems.at[chip], recv_sem=self.recv_sems.at[chip],
            device_id=(mx, my, 1 - mc), device_id_type=MESH) for chip in range(N_CHIP)]

    def start(self):
        for cp in self._copies():
            cp.start()

    def forward(self):
        pass

    def finish(self):
        for cp in self._copies():
            cp.wait()


class _ChipsOps:
    def __init__(self, q_ref, out_ref, send_sems, recv_sems, local_sem):
        self.q_ref, self.out_ref = q_ref, out_ref
        self.send_sems, self.recv_sems, self.local_sem = send_sems, recv_sems, local_sem

    @staticmethod
    def result(q):
        return _sds(q.shape, q.dtype)

    @staticmethod
    def scratch():
        return [pltpu.SemaphoreType.DMA((N_CHIP - 1,)), pltpu.SemaphoreType.DMA((N_CHIP - 1,)), pltpu.SemaphoreType.DMA(())]

    def _copies(self):
        mx, my, mc = lax.axis_index("x"), lax.axis_index("y"), lax.axis_index("c")
        me = 2 * mx + my
        local = pltpu.make_async_copy(self.q_ref.at[me], self.out_ref.at[me], self.local_sem)
        remote = []
        for r in range(1, N_CHIP):
            px = 1 - mx if r & 2 else mx
            py = 1 - my if r & 1 else my
            remote.append(pltpu.make_async_remote_copy(
                src_ref=self.q_ref.at[2 * px + py], dst_ref=self.out_ref.at[me],
                send_sem=self.send_sems.at[r - 1], recv_sem=self.recv_sems.at[r - 1],
                device_id=(px, py, mc), device_id_type=MESH))
        return local, remote

    def start(self):
        local, remote = self._copies()
        local.start()
        for cp in remote:
            cp.start()

    def forward(self):
        pass

    def finish(self):
        local, remote = self._copies()
        for cp in remote:
            cp.wait()
        local.wait()


def _pair_sum(p, theirs, name):
    n, R, C = theirs.shape
    tr = _tile_rows(R, 256)

    def body(p_ref, t_ref, o_ref):
        mc = lax.axis_index("c")
        o_ref[0] = (p_ref[0, mc] + t_ref[0]).astype(BF16)

    blk = pl.BlockSpec((1, tr, C), lambda s, i: (s, i, 0))
    return _pcall(body, name=name, out_shape=_sds((n, R, C), BF16), grid=(n, R // tr),
                  in_specs=[pl.BlockSpec((1, 2, tr, C), lambda s, i: (s, 0, i, 0)), blk],
                  out_specs=blk, sem=("parallel", "parallel"))(p.reshape(n, 2, R, C), theirs)


def _sum_parts(parts, name):
    P, R, C = parts.shape
    tr = _tile_rows(R, 256)

    def body(p_ref, o_ref):
        g = p_ref[0].astype(F32)
        for k in range(1, P):
            g = g + p_ref[k].astype(F32)
        o_ref[...] = g

    return _pcall(body, name=name, out_shape=_sds((R, C)), grid=(R // tr,),
                  in_specs=[pl.BlockSpec((P, tr, C), lambda i: (0, i, 0))],
                  out_specs=pl.BlockSpec((tr, C), lambda i: (i, 0)), sem=("parallel",))(parts)


def _adamw(parts, w, m, v, name):
    P, R, C = parts.shape
    tr = R if R <= 256 else _tile_rows(R, 256)

    def body(p_ref, w_ref, m_ref, v_ref, g_ref, d_ref, nm_ref, nv_ref):
        g = p_ref[0]
        for k in range(1, P):
            g = g + p_ref[k]
        nm = ADAM_B1 * m_ref[...] + (1.0 - ADAM_B1) * g
        nv = ADAM_B2 * v_ref[...] + (1.0 - ADAM_B2) * (g * g)
        m_hat = nm / (1.0 - ADAM_B1 ** ADAM_STEP)
        v_hat = nv / (1.0 - ADAM_B2 ** ADAM_STEP)
        g_ref[...] = g
        d_ref[...] = -ADAM_LR * (m_hat / (jnp.sqrt(v_hat) + ADAM_EPS) + ADAM_WD * w_ref[...])
        nm_ref[...] = nm
        nv_ref[...] = nv

    blk = pl.BlockSpec((tr, C), lambda i: (i, 0))
    return _pcall(
        body, name=name, out_shape=[_sds((R, C))] * 4, grid=(R // tr,),
        in_specs=[pl.BlockSpec((P, tr, C), lambda i: (0, i, 0)), blk, blk, blk],
        out_specs=[blk] * 4, sem=("parallel",))(parts, w, m, v)


def _adam_update(g, w, m, v):
    nm = ADAM_B1 * m + (1.0 - ADAM_B1) * g
    nv = ADAM_B2 * v + (1.0 - ADAM_B2) * (g * g)
    m_hat = nm / (1.0 - ADAM_B1 ** ADAM_STEP)
    v_hat = nv / (1.0 - ADAM_B2 ** ADAM_STEP)
    return -ADAM_LR * (m_hat / (jnp.sqrt(v_hat) + ADAM_EPS) + ADAM_WD * w), nm, nv


class _Canvas:
    def __init__(self):
        self.offset, self.views, self.rows = {}, {}, 0

    def add(self, key, r, c, blocks=1):
        need = r // blocks if blocks > 1 else r * (-(-c // D))
        if need >= 8:
            self.rows = -(-self.rows // 8) * 8
        self.offset[key], self.views[key] = self.rows, (r, c, blocks)
        self.rows += need

    @property
    def total(self):
        return -(-self.rows // 8) * 8

    def cells(self, key):
        (r, c, blocks), off = self.views[key], self.offset[key]
        if blocks > 1:
            n = r // blocks
            return [(off, n, b * c, c, slice(b * n, (b + 1) * n), slice(0, c)) for b in range(blocks)]
        if c <= D:
            return [(off, r, 0, c, slice(0, r), slice(0, c))]
        chunks = -(-c // D)
        return [(off + i * chunks + j, 1, 0, min(D, c - j * D), slice(i, i + 1), slice(j * D, min(c, (j + 1) * D)))
                for i in range(r) for j in range(chunks)]


def _canvas_fill(canvas, sources, name):
    arrays = [a for a, _ in sources]

    def body(*refs):
        out_ref = refs[-1]
        out_ref[...] = jnp.zeros_like(out_ref)
        for ref, (_, items) in zip(refs[:-1], sources):
            for key, src_rows, view_row in items:
                n = src_rows.stop - src_rows.start
                for row, count, lane, width, vrows, vcols in canvas.cells(key):
                    lo, hi = max(vrows.start, view_row), min(vrows.stop, view_row + n)
                    if lo < hi:
                        src = slice(src_rows.start + lo - view_row, src_rows.start + hi - view_row)
                        out_ref[row + lo - vrows.start:row + hi - vrows.start, lane:lane + width] = ref[src, vcols]

    return _pcall(body, name=name, out_shape=_sds((canvas.total, D)))(*arrays)


def _adamw_canvas(canvas, parts, params, sum_only, name):
    n = len(params)

    def body(p_ref, *refs):
        sum_ref = refs[-1]
        g_all = p_ref[0]
        for k in range(1, N_DEV):
            g_all = g_all + p_ref[k]
        sum_ref[...] = g_all
        for i, (key, _, _, _) in enumerate(params):
            w_ref, m_ref, v_ref = refs[3 * i:3 * i + 3]
            outs = refs[3 * n + 4 * i:3 * n + 4 * i + 4]
            for row, count, lane, width, vrows, vcols in canvas.cells(key):
                g = sum_ref[row:row + count, lane:lane + width]
                delta, nm, nv = _adam_update(g, w_ref[vrows, vcols], m_ref[vrows, vcols], v_ref[vrows, vcols])
                for ref, val in zip(outs, (g, delta, nm, nv)):
                    ref[vrows, vcols] = val
        for i, key in enumerate(sum_only):
            for row, count, lane, width, vrows, vcols in canvas.cells(key):
                refs[7 * n + i][vrows, vcols] = sum_ref[row:row + count, lane:lane + width]

    args = [a for _, w, m, v in params for a in (w, m, v)]
    out_shape = [_sds(w.shape) for _, w, _, _ in params for _ in range(4)] + [_sds(canvas.views[k][:2]) for k in sum_only]
    res = _pcall(body, name=name, out_shape=out_shape, scratch=[pltpu.VMEM(parts.shape[1:], F32)])(parts, *args)
    out = {key: res[4 * i:4 * i + 4] for i, (key, _, _, _) in enumerate(params)}
    out.update({key: [res[4 * n + i]] for i, key in enumerate(sum_only)})
    return out


def _tile_rows(n, pref):
    best = None
    for t in range(8, pref + 1, 8):
        if n % t == 0:
            best = t
    assert best is not None, (n, pref)
    return best


def _mm(pairs, mode, *, name, out_dtype=F32, bias=None, tn_pref=1024, side=None):
    M = pairs[0][0].shape[0]
    N = pairs[0][1].shape[1] if mode == "nn" else pairs[0][1].shape[0]
    tm, tn = _rows(M), _tile(N, tn_pref)
    n_pairs = len(pairs)
    has_bias = bias is not None

    def body(*refs):
        acc = _pairs_dot(refs[:2 * n_pairs], mode)
        if has_bias:
            acc = acc + refs[2 * n_pairs][...]
        refs[-1][...] = acc.astype(refs[-1].dtype)

    in_specs, args = _pair_specs(pairs, mode, tm, tn)
    if has_bias:
        in_specs.append(pl.BlockSpec((1, tn), lambda j, i: (0, j)))
        args.append(bias)
    res = _call(body, args, side, name=name, out_shape=[_sds((M, N), out_dtype)], grid=(N // tn, M // tm),
                in_specs=in_specs, out_specs=[pl.BlockSpec((tm, tn), lambda j, i: (i, j))], sem=("parallel", "parallel"))
    return res[0] if side is None else (res[0], res[1])


def _pairs_dot(ab, mode):
    acc = None
    for p in range(len(ab) // 2):
        a, b = _bf(ab[2 * p][...]), _bf(ab[2 * p + 1][...])
        d = _nn(a, b) if mode == "nn" else _nt(a, b)
        acc = d if acc is None else acc + d
    return acc


def _pair_specs(pairs, mode, tm, tn):
    in_specs, args = [], []
    for a, b in pairs:
        K = a.shape[1]
        in_specs.append(pl.BlockSpec((tm, K), lambda j, i: (i, 0)))
        if mode == "nn":
            in_specs.append(pl.BlockSpec((K, tn), lambda j, i: (0, j)))
        else:
            in_specs.append(pl.BlockSpec((tn, K), lambda j, i: (j, 0)))
        args += [a, b]
    return in_specs, args


def _mm_resid(pairs, resid, gvec, *, name, bias=None, norm=None):
    M, N = resid.shape
    tm = _rows(M)
    n_pairs = len(pairs)
    has_bias, has_norm = bias is not None, norm is not None

    def body(*refs):
        acc = _pairs_dot(refs[:2 * n_pairs], "nn")
        pos = 2 * n_pairs
        if has_bias:
            acc = acc + refs[pos][...]
            pos += 1
        xv = refs[pos][...] + refs[pos + 1][...] * acc
        outs = refs[pos + 2 + 3 * has_norm:]
        outs[0][...] = xv
        outs[1][...] = acc.astype(BF16)
        if has_norm:
            w_ref, sc_ref, sh_ref = refs[pos + 2:pos + 5]
            r = lax.rsqrt(jnp.mean(xv * xv, axis=-1, keepdims=True) + EPS)
            outs[2][...] = ((xv * r * w_ref[...]) * (1.0 + sc_ref[...]) + sh_ref[...]).astype(BF16)

    in_specs, args = _pair_specs(pairs, "nn", tm, N)
    vec = pl.BlockSpec((1, N), lambda j, i: (0, 0))
    row = pl.BlockSpec((tm, N), lambda j, i: (i, 0))
    if has_bias:
        in_specs.append(vec)
        args.append(bias)
    in_specs += [row, vec] + [vec] * (3 * has_norm)
    args += [resid, gvec] + (list(norm) if has_norm else [])
    return _pcall(body, name=name, out_shape=[_sds((M, N)), _sds((M, N), BF16)] + [_sds((M, N), BF16)] * has_norm,
                  grid=(1, M // tm), in_specs=in_specs, out_specs=[row] * (2 + has_norm),
                  sem=("parallel", "parallel"))(*args)


def _mm_tn(a, b, *, name, tm_pref=1408, tn_pref=1536):
    K, M = a.shape
    N = b.shape[1]
    tm, tn = _tile(M, tm_pref), _tile(N, tn_pref)
    tk = K if K <= 2 * ROW_TILE else 2 * ROW_TILE

    def body(a_ref, b_ref, o_ref):
        @pl.when(pl.program_id(2) == 0)
        def _():
            o_ref[...] = jnp.zeros_like(o_ref)
        o_ref[...] += _tn(_bf(a_ref[...]), _bf(b_ref[...]))

    return _pcall(
        body, name=name, out_shape=_sds((M, N)), grid=(M // tm, N // tn, K // tk),
        in_specs=[pl.BlockSpec((tk, tm), lambda i, j, k: (k, i)), pl.BlockSpec((tk, tn), lambda i, j, k: (k, j))],
        out_specs=pl.BlockSpec((tm, tn), lambda i, j, k: (i, j)),
        sem=("parallel", "parallel", "arbitrary"))(a, b)


def _mm_swiglu(h, wg_t, wu_t, name, side=None):
    M, K = h.shape
    N = wg_t.shape[0]
    tm, tn = _rows(M), _tile(N, 1408)

    def body(h_ref, wg_ref, wu_ref, gate_ref, up_ref, act_ref):
        hv = _bf(h_ref[...])
        gate = _nt(hv, wg_ref[...])
        up = _nt(hv, wu_ref[...])
        gate_ref[...] = gate.astype(BF16)
        up_ref[...] = up.astype(BF16)
        act_ref[...] = (_silu(gate) * up).astype(BF16)

    w_spec = pl.BlockSpec((tn, K), lambda j, i: (j, 0))
    o_spec = pl.BlockSpec((tm, tn), lambda j, i: (i, j))
    return _call(body, (h, wg_t, wu_t), side, name=name,
                 out_shape=[_sds((M, N), BF16)] * 3, grid=(N // tn, M // tm),
                 in_specs=[pl.BlockSpec((tm, K), lambda j, i: (i, 0)), w_spec, w_spec], out_specs=[o_spec] * 3,
                 sem=("parallel", "parallel"))


def _mm_swiglu_bwd(dout, wd, gate, up, name, side=None):
    M, K = dout.shape
    N = wd.shape[0]
    tm, tn = _rows(M), _tile(N, 1408)

    def body(d_ref, wd_ref, gate_ref, up_ref, dg_ref, du_ref):
        dact = _nt(_bf(d_ref[...]), wd_ref[...])
        g = gate_ref[...].astype(F32)
        dg_ref[...] = (dact * up_ref[...].astype(F32) * _dsilu(g)).astype(BF16)
        du_ref[...] = (dact * _silu(g)).astype(BF16)

    t_spec = pl.BlockSpec((tm, tn), lambda j, i: (i, j))
    return _call(
        body, (dout, wd, gate, up), side, name=name, out_shape=[_sds((M, N), BF16)] * 2, grid=(N // tn, M // tm),
        in_specs=[pl.BlockSpec((tm, K), lambda j, i: (i, 0)), pl.BlockSpec((tn, K), lambda j, i: (j, 0)), t_spec, t_spec],
        out_specs=[t_spec] * 2, sem=("parallel", "parallel"))


def _norm_mod(x, w, sc, sh, name, side=None):
    T = x.shape[0]
    tm = _rows(T)

    def body(x_ref, w_ref, sc_ref, sh_ref, o_ref):
        xv = x_ref[...]
        r = lax.rsqrt(jnp.mean(xv * xv, axis=-1, keepdims=True) + EPS)
        o_ref[...] = ((xv * r * w_ref[...]) * (1.0 + sc_ref[...]) + sh_ref[...]).astype(BF16)

    return _call(body, (x, w, sc, sh), side, name=name, out_shape=[_sds((T, D), BF16)], grid=(T // tm,),
                 in_specs=[_row_spec(tm, D), _vec_spec(D), _vec_spec(D), _vec_spec(D)],
                 out_specs=[_row_spec(tm, D)], sem=("parallel",))


def _gate_rows(dxv, br_ref, g_ref, db_ref, acc_ref):
    db = g_ref[...] * dxv
    db_ref[...] = db.astype(BF16)
    acc_ref[3:4, :] += _colsum(dxv * br_ref[...].astype(F32))
    acc_ref[4:5, :] += _colsum(db)


def _norm_mod_bwd(x, dh, dres, w, sc, branch, g, name, side=None):
    T = x.shape[0]
    tm = _rows(T)
    gated = branch is not None

    def body(x_ref, dh_ref, dres_ref, w_ref, sc_ref, *rest):
        (br_ref, g_ref, dx_ref, db_ref, acc_ref) = rest if gated else (None, None, rest[0], None, rest[1])

        @pl.when(pl.program_id(0) == 0)
        def _():
            acc_ref[...] = jnp.zeros_like(acc_ref)
        xv, dh_v, wv = x_ref[...], dh_ref[...], w_ref[...]
        r = lax.rsqrt(jnp.mean(xv * xv, axis=-1, keepdims=True) + EPS)
        n = xv * r
        dnw = dh_v * (1.0 + sc_ref[...])
        dn = dnw * wv
        dxv = dres_ref[...] + r * (dn - n * jnp.mean(dn * n, axis=-1, keepdims=True))
        dx_ref[...] = dxv
        acc_ref[0:1, :] += _colsum(dh_v * (n * wv))
        acc_ref[1:2, :] += _colsum(dh_v)
        acc_ref[2:3, :] += _colsum(dnw * n)
        if gated:
            _gate_rows(dxv, br_ref, g_ref, db_ref, acc_ref)

    row = _row_spec(tm, D)
    args = (x, dh, dres, w, sc) + ((branch, g) if gated else ())
    return _call(body, args, side, name=name,
                 out_shape=[_sds((T, D))] + ([_sds((T, D), BF16)] if gated else []) + [_sds((8, D))], grid=(T // tm,),
                 in_specs=[row, row, row, _vec_spec(D), _vec_spec(D)] + ([row, _vec_spec(D)] if gated else []),
                 out_specs=[row] + ([row] if gated else []) + [_vec_spec(D, 8)], sem=("arbitrary",))


def _final_loss(x, wf, target, branch, g, name):
    T = x.shape[0]
    tm = _rows(T)

    def body(x_ref, w_ref, t_ref, br_ref, g_ref, dx_ref, db_ref, acc_ref):
        @pl.when(pl.program_id(0) == 0)
        def _():
            acc_ref[...] = jnp.zeros_like(acc_ref)
        xv, wv = x_ref[...], w_ref[...]
        r = lax.rsqrt(jnp.mean(xv * xv, axis=-1, keepdims=True) + EPS)
        n = xv * r
        err = n * wv - t_ref[...]
        dy = err * (1.0 / D)
        dn = dy * wv
        dxv = r * (dn - n * jnp.mean(dn * n, axis=-1, keepdims=True))
        dx_ref[...] = dxv
        acc_ref[0:1, :] += _colsum(dy * n)
        acc_ref[1:2, :] += jnp.broadcast_to(_allsum(err * err) * (0.5 / D), (1, D))
        _gate_rows(dxv, br_ref, g_ref, db_ref, acc_ref)

    row = _row_spec(tm, D)
    return _pcall(body, name=name, out_shape=[_sds((T, D)), _sds((T, D), BF16), _sds((8, D))], grid=(T // tm,),
                  in_specs=[row, _vec_spec(D), row, row, _vec_spec(D)],
                  out_specs=[row, row, _vec_spec(D, 8)], sem=("arbitrary",))(x, wf, target, branch, g)


def _mod_matmul(c_all, ada_w, name):
    n = ada_w.shape[2]

    def body(c_ref, w_ref, cond_ref, o_ref):
        cond = _silu(c_ref[...])
        cond_ref[...] = cond
        o_ref[0] = _nn(cond, w_ref[0])

    return _pcall(body, name=name, out_shape=[_sds((N_DEV, D)), _sds((DEPTH, N_DEV, n))], grid=(DEPTH,),
                  in_specs=[pl.BlockSpec((N_DEV, D), lambda l: (0, 0)), pl.BlockSpec((1, D, n), lambda l: (l, 0, 0))],
                  out_specs=[pl.BlockSpec((N_DEV, D), lambda l: (0, 0)), pl.BlockSpec((1, N_DEV, n), lambda l: (l, 0, 0))],
                  sem=("arbitrary",))(c_all, ada_w)


def _add_rows(a, b, name):
    def body(a_ref, b_ref, o_ref):
        o_ref[...] = a_ref[...] + b_ref[...]

    return _pcall(body, name=name, out_shape=_sds(a.shape))(a, b)


def _ada_w_grad(cond, dmod_cols, name):
    n = dmod_cols.shape[2]

    def body(c_ref, d_ref, o_ref):
        o_ref[0] = _tn(c_ref[...], d_ref[0])

    return _pcall(body, name=name, out_shape=_sds((DEPTH, D, n)), grid=(DEPTH,),
                  in_specs=[pl.BlockSpec((N_DEV, D), lambda l: (0, 0)), pl.BlockSpec((1, N_DEV, n), lambda l: (l, 0, 0))],
                  out_specs=pl.BlockSpec((1, D, n), lambda l: (l, 0, 0)), sem=("parallel",))(cond, dmod_cols)


def _conv_fwd(pm, conv_w, conv_b, name):
    T = pm.shape[0]
    tm = _rows(T)
    C = CONV_DIM

    def body(x_ref, prev_ref, w_ref, b_ref, o_ref):
        cur = x_ref[...].astype(F32)
        prev = jnp.where(pl.program_id(0) > 0, prev_ref[...].astype(F32)[8:16], 0.0)
        cur8 = cur[0:8]
        row8 = lax.broadcasted_iota(jnp.int32, (8, C), 0)
        full = w_ref[3:4, :] * cur
        head = w_ref[3:4, :] * cur8
        for k in range(1, SSM_CONV):
            wk = w_ref[3 - k:4 - k, :]
            full = full + wk * pltpu.roll(cur, k, 0)
            head = head + wk * jnp.where(row8 < k, pltpu.roll(prev, k, 0), pltpu.roll(cur8, k, 0))
        o_ref[...] = full + b_ref[...]
        o_ref[0:8, :] = head + b_ref[...]

    return _pcall(
        body, name=name, out_shape=_sds((T, C)), grid=(T // tm,),
        in_specs=[pl.BlockSpec((tm, C), lambda i: (i, 2)),
                  pl.BlockSpec((16, C), lambda i: (jnp.maximum(i * (tm // 16) - 1, 0), 2)),
                  _vec_spec(C, SSM_CONV), _vec_spec(C)],
        out_specs=_row_spec(tm, C), sem=("parallel",))(pm, pm, conv_w, conv_b)


def _conv_bwd(dc, pm, conv_w, name):
    T = dc.shape[0]
    tm = _rows(T)
    C = CONV_DIM
    nt = T // tm

    def body(dc_ref, nxt_ref, x_ref, w_ref, dx_ref, acc_ref):
        i = pl.program_id(0)

        @pl.when(i == 0)
        def _():
            acc_ref[...] = jnp.zeros_like(acc_ref)
        dcv = dc_ref[...]
        nxt = jnp.where(i < nt - 1, nxt_ref[...], 0.0)
        xc = x_ref[...].astype(F32)
        dc8t, x8t = dcv[tm - 8:tm], xc[tm - 8:tm]
        row8 = lax.broadcasted_iota(jnp.int32, (8, C), 0)
        full = w_ref[3:4, :] * dcv
        tail = w_ref[3:4, :] * dc8t
        acc_ref[3:4, :] += _colsum(dcv * xc)
        for k in range(1, SSM_CONV):
            wk = w_ref[3 - k:4 - k, :]
            up = pltpu.roll(dcv, tm - k, 0)
            up_tail = jnp.where(row8 + k >= 8, pltpu.roll(nxt, 8 - k, 0), pltpu.roll(dc8t, 8 - k, 0))
            full = full + wk * up
            tail = tail + wk * up_tail
            prod = up * xc
            acc_ref[3 - k:4 - k, :] += _colsum(prod) - _colsum(prod[tm - 8:tm]) + _colsum(up_tail * x8t)
        acc_ref[4:5, :] += _colsum(dcv)
        dx_ref[...] = jnp.concatenate([full[0:tm - 8], tail], axis=0).astype(BF16)

    return _pcall(
        body, name=name, out_shape=[_sds((T, C), BF16), _sds((8, C))], grid=(nt,),
        in_specs=[_row_spec(tm, C),
                  pl.BlockSpec((8, C), lambda i: (jnp.minimum((i + 1) * (tm // 8), T // 8 - 1), 0)),
                  pl.BlockSpec((tm, C), lambda i: (i, 2)),
                  _vec_spec(C, SSM_CONV)],
        out_specs=[_row_spec(tm, C), _vec_spec(C, 8)], sem=("arbitrary",))(dc, dc, pm, conv_w)


def _ssd_prologue(cpre, dtr, dtb, alog):
    L = CHUNK
    xc = _silu(cpre)
    pre = dtr + dtb
    dt = jnp.maximum(pre, 0.0) + jnp.log1p(jnp.exp(-jnp.abs(pre)))
    a = -jnp.exp(alog)
    la = dt * a
    row = lax.broadcasted_iota(jnp.int32, (L, L), 0)
    col = lax.broadcasted_iota(jnp.int32, (L, L), 1)
    causal = row >= col
    tri = causal.astype(F32)
    lc = _nn(tri, la, HIGHEST)
    return xc, pre, dt, a, causal, tri, lc, row, col


def _head_indicator():
    m = np.zeros((LANES, SSM_INNER), np.float32)
    for h in range(SSM_HEADS):
        m[h, h * SSM_HEAD_DIM:(h + 1) * SSM_HEAD_DIM] = 1.0
    return jnp.asarray(m, dtype=BF16)


def _split_dot(x, ind, dims):
    hi = x.astype(BF16)
    lo = (x - hi.astype(F32)).astype(BF16)
    return _dot(hi, ind, dims) + _dot(lo, ind, dims)


def _expand(x16, ind):
    return _split_dot(x16, ind, ((1,), (0,)))


def _headsum(x, ind, single_pass=False):
    if single_pass:
        return _dot(x.astype(BF16), ind, ((1,), (1,)))
    return _split_dot(x, ind, ((1,), (1,)))


def _ssd_fwd(cpre, dtr, pm, dtb, alog, dskip, normw, ind, name, side=None):
    T = cpre.shape[0]
    nc = T // CHUNK
    L, P, H, HPG, N = CHUNK, SSM_HEAD_DIM, SSM_HEADS, SSM_HEADS // SSM_GROUPS, SSM_STATE
    half = SSM_INNER // SSM_GROUPS

    def body(cp_ref, dtr_ref, z_ref, dtb_ref, alog_ref, dskip_ref, nw_ref, ind_ref, ya_ref, y_ref, sp_ref, st_ref):
        @pl.when(pl.program_id(0) == 0)
        def _():
            st_ref[...] = jnp.zeros_like(st_ref)
        xc, _, dt, _, causal, _, lc, _, _ = _ssd_prologue(cp_ref[...], dtr_ref[...], dtb_ref[...], alog_ref[...])
        lct = lc.T
        ind = ind_ref[...]
        llast = lc[L - 1:L, :]
        xs = xc[:, :SSM_INNER]
        xd = xs * _expand(dt, ind)
        ex = _expand(jnp.exp(lc), ind)
        xd_end = _bf(xd * _expand(jnp.exp(llast - lc), ind))
        cdx = _expand(jnp.broadcast_to(jnp.exp(llast), (8, LANES)), ind)[0:1]
        xdb = _bf(xd)
        sp_ref[0] = st_ref[...]
        for g in range(SSM_GROUPS):
            sl = slice(g * half, (g + 1) * half)
            bm = _bf(xc[:, SSM_INNER + g * N:SSM_INNER + (g + 1) * N])
            cm = _bf(xc[:, SSM_INNER + (SSM_GROUPS + g) * N:SSM_INNER + (SSM_GROUPS + g + 1) * N])
            cb = _nt(cm, bm)
            st = st_ref[g]
            y_ref[:, sl] = ex[:, sl] * _nn(cm, _bf(st)) + dskip_ref[:, sl] * xs[:, sl]
            st_ref[g] = st * cdx[:, sl] + _tn(bm, xd_end[:, sl])
            for j in range(HPG):
                h = g * HPG + j
                decay = jnp.where(causal, jnp.exp(jnp.where(causal, lc[:, h:h + 1] - lct[h:h + 1, :], 0.0)), 0.0)
                y_ref[:, h * P:(h + 1) * P] += _nn(_bf(cb * decay), xdb[:, h * P:(h + 1) * P])
        y2 = y_ref[...] * _silu(z_ref[...].astype(F32))
        for g in range(SSM_GROUPS):
            yg = y2[:, g * half:(g + 1) * half]
            r = lax.rsqrt(jnp.mean(yg * yg, axis=-1, keepdims=True) + EPS)
            ya_ref[:, g * half:(g + 1) * half] = (yg * r * nw_ref[:, g * half:(g + 1) * half]).astype(BF16)

    return _call(
        body, (cpre, dtr, pm, dtb, alog, dskip, normw, ind), side, name=name,
        out_shape=[_sds((T, SSM_INNER), BF16), _sds((T, SSM_INNER)), _sds((nc, SSM_GROUPS, N, half))], grid=(nc,),
        in_specs=[_row_spec(L, CONV_DIM), _row_spec(L, LANES), _row_spec(L, SSM_INNER, 0),
                  _vec_spec(LANES), _vec_spec(LANES), _vec_spec(SSM_INNER), _vec_spec(SSM_INNER), _vec_spec(SSM_INNER, LANES)],
        out_specs=[_row_spec(L, SSM_INNER), _row_spec(L, SSM_INNER),
                   pl.BlockSpec((1, SSM_GROUPS, N, half), lambda i: (i, 0, 0, 0))],
        scratch=[pltpu.VMEM((SSM_GROUPS, N, half), F32)], sem=("arbitrary",))


def _ssd_bwd(cpre, dtr, pm, ypre, sprev, dya, dtb, alog, dskip, normw, ind, name, side=None):
    T = cpre.shape[0]
    nc = T // CHUNK
    L, P, H, HPG, N = CHUNK, SSM_HEAD_DIM, SSM_HEADS, SSM_HEADS // SSM_GROUPS, SSM_STATE
    half = SSM_INNER // SSM_GROUPS

    def body(cp_ref, dtr_ref, z_ref, y_ref, sp_ref, dya_ref, dtb_ref, alog_ref, dskip_ref, nw_ref, ind_ref,
             dz_ref, dcp_ref, ddtr_ref, acc_ref, dnw_ref, ds_ref, dy_ref, dxd_ref, rr_ref, yoff_ref, dcd_ref):
        @pl.when(pl.program_id(0) == 0)
        def _():
            ds_ref[...] = jnp.zeros_like(ds_ref)
            acc_ref[...] = jnp.zeros_like(acc_ref)
            dnw_ref[...] = jnp.zeros_like(dnw_ref)
        cpre_v = cp_ref[...]
        xc, pre, dt, a, causal, tri, lc, row, col = _ssd_prologue(cpre_v, dtr_ref[...], dtb_ref[...], alog_ref[...])
        lct = lc.T
        zv, yv = z_ref[...].astype(F32), y_ref[...]
        sz = _silu(zv)
        y2 = yv * sz
        dya_v = dya_ref[...]
        nwv = nw_ref[...]
        for g in range(SSM_GROUPS):
            sl = slice(g * half, (g + 1) * half)
            yg = y2[:, sl]
            r = lax.rsqrt(jnp.mean(yg * yg, axis=-1, keepdims=True) + EPS)
            nrm = yg * r
            dnw_ref[:, sl] += _colsum(dya_v[:, sl] * nrm)
            dn = dya_v[:, sl] * nwv[:, sl]
            dy2 = r * (dn - nrm * jnp.mean(dn * nrm, axis=-1, keepdims=True))
            dy_ref[:, sl] = dy2 * sz[:, sl]
            dz_ref[:, sl] = (dy2 * yv[:, sl] * _dsilu(zv[:, sl])).astype(BF16)
        ind = ind_ref[...]
        llast = lc[L - 1:L, :]
        dte16 = jnp.exp(llast - lc)
        cd16 = jnp.exp(llast)
        xs = xc[:, :SSM_INNER]
        dtx = _expand(dt, ind)
        ex = _expand(jnp.exp(lc), ind)
        dtex = _expand(dte16, ind)
        cdx = _expand(jnp.broadcast_to(cd16, (8, LANES)), ind)[0:1]
        xd = xs * dtx
        xdb = _bf(xd)
        xd_end = _bf(xd * dtex)
        dyv = dy_ref[...]
        dy_off = _bf(ex * dyv)
        dyb = _bf(dyv)
        dskx = dskip_ref[...]
        lane_c = lax.broadcasted_iota(jnp.int32, (L, LANES), 1)
        lane1 = lax.broadcasted_iota(jnp.int32, (1, LANES), 1)
        sub16 = lax.broadcasted_iota(jnp.int32, (H, L), 0)
        dlc_c = jnp.zeros((L, LANES), F32)
        dlc_r = jnp.zeros((H, L), F32)
        for g in range(SSM_GROUPS):
            sl = slice(g * half, (g + 1) * half)
            b_lo = SSM_INNER + g * N
            c_lo = SSM_INNER + (SSM_GROUPS + g) * N
            bm, cm = _bf(xc[:, b_lo:b_lo + N]), _bf(xc[:, c_lo:c_lo + N])
            cb = _nt(cm, bm)
            st, dst = sp_ref[0, g], ds_ref[g]
            stb, dstb = _bf(st), _bf(dst)
            dcm = _nt(dy_off[:, sl], stb)
            ds_ref[g] = _tn(cm, dy_off[:, sl]) + dst * cdx[:, sl]
            rr_ref[:, sl] = _nn(bm, dstb)
            yoff_ref[:, sl] = ex[:, sl] * _nn(cm, stb)
            db = _nt(xd_end[:, sl], dstb)
            dcd_ref[:, sl] = _colsum(dst * st)
            dcb = jnp.zeros((L, L), F32)
            for j in range(HPG):
                h = g * HPG + j
                hs = slice(h * P, (h + 1) * P)
                decay = jnp.where(causal, jnp.exp(jnp.where(causal, lc[:, h:h + 1] - lct[h:h + 1, :], 0.0)), 0.0)
                m = cb * decay
                dxd_ref[:, hs] = _tn(_bf(m), dyb[:, hs])
                dm = _nt(dyb[:, hs], xdb[:, hs])
                dcb = dcb + dm * decay
                gm = dm * m
                dlc_c = dlc_c + jnp.where(lane_c == h, _rowsum(gm), 0.0)
                dlc_r = dlc_r + jnp.where(sub16 == h, _colsum(gm), 0.0)
            dcbb = _bf(dcb)
            dcp_ref[:, c_lo:c_lo + N] = dcm + _nn(dcbb, bm)
            dcp_ref[:, b_lo:b_lo + N] = db + _tn(dcbb, cm)
        dxd_diag, rr = dxd_ref[...], rr_ref[...]
        tt = _headsum(rr * xd, ind, single_pass=True) * dte16
        dlc_rt = jnp.concatenate([dlc_r, jnp.zeros((LANES - H, L), F32)], axis=0).T
        dlc = dlc_c - dlc_rt + _headsum(dyv * yoff_ref[...], ind, single_pass=True) - tt
        dcd = _headsum(jnp.broadcast_to(dcd_ref[...], (8, SSM_INNER)), ind)[0:1]
        dlc = dlc + jnp.where(row == L - 1, _colsum(tt) + dcd * cd16, 0.0)
        dla = _tn(tri, dlc, HIGHEST)
        dxd = dxd_diag + dtex * rr
        ddt = _headsum(dxd * xs, ind, single_pass=True) + dla * a
        ddtr = jnp.where(lane_c < H, ddt * _sigmoid(pre), 0.0)
        ddtr_ref[...] = ddtr
        acc_ref[0:1, :] += _colsum(ddtr)
        acc_ref[1:2, :] += jnp.where(lane1 < H, _colsum(dla * dt) * a, 0.0)
        acc_ref[2:3, :] += _headsum(jnp.broadcast_to(_colsum(dyv * xs), (8, SSM_INNER)), ind)[0:1]
        dcp_ref[:, 0:SSM_INNER] = dxd * dtx + dskx * dyv
        dcp_ref[...] = dcp_ref[...] * _dsilu(cpre_v)

    rev = lambda i: (nc - 1 - i, 0)
    rspec = lambda c: pl.BlockSpec((L, c), rev)
    return _call(
        body, (cpre, dtr, pm, ypre, sprev, dya, dtb, alog, dskip, normw, ind), side, name=name,
        out_shape=[_sds((T, SSM_INNER), BF16), _sds((T, CONV_DIM)), _sds((T, LANES)), _sds((8, LANES)), _sds((1, SSM_INNER))],
        grid=(nc,),
        in_specs=[rspec(CONV_DIM), rspec(LANES), rspec(SSM_INNER), rspec(SSM_INNER),
                  pl.BlockSpec((1, SSM_GROUPS, N, half), lambda i: (nc - 1 - i, 0, 0, 0)), rspec(SSM_INNER),
                  _vec_spec(LANES), _vec_spec(LANES), _vec_spec(SSM_INNER), _vec_spec(SSM_INNER), _vec_spec(SSM_INNER, LANES)],
        out_specs=[rspec(SSM_INNER), rspec(CONV_DIM), rspec(LANES), _vec_spec(LANES, 8), _vec_spec(SSM_INNER)],
        scratch=[pltpu.VMEM((SSM_GROUPS, N, half), F32), pltpu.VMEM((L, SSM_INNER), F32), pltpu.VMEM((L, SSM_INNER), F32),
                 pltpu.VMEM((L, SSM_INNER), F32), pltpu.VMEM((L, SSM_INNER), F32), pltpu.VMEM((1, SSM_INNER), F32)],
        sem=("arbitrary",))


def _gmlp_common(u, v, lnw, lnb):
    ug = _gelu(u)
    vg = _gelu(v)
    mu = jnp.mean(vg, axis=-1, keepdims=True)
    cen = vg - mu
    rstd = lax.rsqrt(jnp.mean(cen * cen, axis=-1, keepdims=True) + EPS)
    vhat = cen * rstd
    return ug, rstd, vhat, vhat * lnw + lnb


def _causal_mask():
    row = lax.broadcasted_iota(jnp.int32, (CHUNK, CHUNK), 0)
    col = lax.broadcasted_iota(jnp.int32, (CHUNK, CHUNK), 1)
    return row >= col


def _gmlp_fwd(pm, lnw, lnb, ws, bs_exp, name, side=None):
    T = pm.shape[0]
    nc = T // CHUNK
    L, G = CHUNK, GMLP_GROUPS

    def body(u_ref, v_ref, lnw_ref, lnb_ref, ws_ref, bs_ref, o_ref):
        ug, _, _, vn = _gmlp_common(u_ref[...].astype(F32), v_ref[...].astype(F32), lnw_ref[...], lnb_ref[...])
        causal = _causal_mask()
        for g in range(G):
            sl = slice(g * L, (g + 1) * L)
            wm = _bf(jnp.where(causal, ws_ref[g], 0.0))
            sv = _nn(wm, _bf(vn[:, sl])) + bs_ref[:, sl]
            o_ref[:, sl] = (ug[:, sl] * sv).astype(BF16)

    return _call(
        body, (pm, pm, lnw, lnb, ws, bs_exp), side, name=name, out_shape=[_sds((T, GMLP_INNER), BF16)], grid=(nc,),
        in_specs=[_row_spec(L, GMLP_INNER, 1), _row_spec(L, GMLP_INNER, 2), _vec_spec(GMLP_INNER), _vec_spec(GMLP_INNER),
                  pl.BlockSpec((G, L, L), lambda i: (0, 0, 0)), _vec_spec(GMLP_INNER, L)],
        out_specs=[_row_spec(L, GMLP_INNER)], sem=("parallel",))


def _gmlp_bwd(pm, dyb, lnw, lnb, ws, bs_exp, name, side=None):
    T = pm.shape[0]
    nc = T // CHUNK
    L, G = CHUNK, GMLP_GROUPS

    def body(u_ref, v_ref, dy_ref, lnw_ref, lnb_ref, ws_ref, bs_ref, du_ref, dv_ref, dws_ref, dbs_ref, acc_ref, dvn_ref):
        @pl.when(pl.program_id(0) == 0)
        def _():
            dws_ref[...] = jnp.zeros_like(dws_ref)
            dbs_ref[...] = jnp.zeros_like(dbs_ref)
            acc_ref[...] = jnp.zeros_like(acc_ref)
        uv, vv, dyv, lnwv = u_ref[...].astype(F32), v_ref[...].astype(F32), dy_ref[...], lnw_ref[...]
        ug, rstd, vhat, vn = _gmlp_common(uv, vv, lnwv, lnb_ref[...])
        causal = _causal_mask()
        lane = lax.broadcasted_iota(jnp.int32, (L, LANES), 1)
        dbs = jnp.zeros((L, LANES), F32)
        for g in range(G):
            sl = slice(g * L, (g + 1) * L)
            wm = _bf(jnp.where(causal, ws_ref[g], 0.0))
            vng = _bf(vn[:, sl])
            sv = _nn(wm, vng) + bs_ref[:, sl]
            du_ref[:, sl] = (dyv[:, sl] * sv * _dgelu(uv[:, sl])).astype(BF16)
            dsv = dyv[:, sl] * ug[:, sl]
            dsvb = _bf(dsv)
            dws_ref[g] += jnp.where(causal, _nt(dsvb, vng), 0.0)
            dbs = dbs + jnp.where(lane == g, _rowsum(dsv), 0.0)
            dvn_ref[:, sl] = _tn(wm, dsvb)
        dbs_ref[...] += dbs
        dvn = dvn_ref[...]
        acc_ref[0:1, :] += _colsum(dvn * vhat)
        acc_ref[1:2, :] += _colsum(dvn)
        dvh = dvn * lnwv
        dvg = rstd * (dvh - jnp.mean(dvh, axis=-1, keepdims=True) - vhat * jnp.mean(dvh * vhat, axis=-1, keepdims=True))
        dv_ref[...] = (dvg * _dgelu(vv)).astype(BF16)

    return _call(
        body, (pm, pm, dyb, lnw, lnb, ws, bs_exp), side, name=name,
        out_shape=[_sds((T, GMLP_INNER), BF16), _sds((T, GMLP_INNER), BF16), _sds((G, L, L)), _sds((L, LANES)), _sds((8, GMLP_INNER))],
        grid=(nc,),
        in_specs=[_row_spec(L, GMLP_INNER, 1), _row_spec(L, GMLP_INNER, 2), _row_spec(L, GMLP_INNER),
                  _vec_spec(GMLP_INNER), _vec_spec(GMLP_INNER), pl.BlockSpec((G, L, L), lambda i: (0, 0, 0)),
                  _vec_spec(GMLP_INNER, L)],
        out_specs=[_row_spec(L, GMLP_INNER), _row_spec(L, GMLP_INNER), pl.BlockSpec((G, L, L), lambda i: (0, 0, 0)),
                   _vec_spec(LANES, L), _vec_spec(GMLP_INNER, 8)],
        scratch=[pltpu.VMEM((L, GMLP_INNER), F32)], sem=("arbitrary",))


def _rel_buckets():
    qi = np.arange(CHUNK)[:, None]
    sj = np.arange(2 * CHUNK)[None, :]
    dist = np.maximum(qi + CHUNK - sj, 0)
    max_exact = REL_BUCKETS // 2
    log_ratio = (np.log(np.maximum(dist, 1).astype(np.float32) / np.float32(max_exact))
                 / np.float32(math.log(REL_MAX_DIST / max_exact))).astype(np.float32)
    large = max_exact + (log_ratio * np.float32(REL_BUCKETS - max_exact)).astype(np.int32)
    return np.where(dist < max_exact, dist, np.minimum(large, REL_BUCKETS - 1))


def _bucket_onehot_t():
    bucket = _rel_buckets().reshape(-1)
    return jnp.asarray((np.arange(REL_BUCKETS)[:, None] == bucket[None, :]).astype(np.float32))


def _bias_from_table(table_t, onehot_t, window, name):
    def body(t_ref, o_ref, w_ref, out_ref):
        out_ref[...] = jnp.where(w_ref[...] > 0.5, _nn(t_ref[...], o_ref[...], HIGHEST), NEG_INF)

    return _pcall(body, name=name, out_shape=_sds((ATTN_HEADS, onehot_t.shape[1])))(table_t, onehot_t, window)


def _table_from_dbias(dbias, onehot_t, name):
    def body(d_ref, o_ref, out_ref):
        out_ref[...] = _nt(o_ref[...], d_ref[...], HIGHEST)

    return _pcall(body, name=name, out_shape=_sds((REL_BUCKETS, ATTN_HEADS)))(dbias, onehot_t)


def _softmax_sink(logits, sink):
    mx = jnp.maximum(jnp.max(logits, axis=-1, keepdims=True), sink)
    e = jnp.exp(logits - mx)
    es = jnp.exp(sink - mx)
    inv = 1.0 / (_rowsum(e) + es)
    return e * inv, es * inv


def _first_block_penalty(n):
    sj = lax.broadcasted_iota(jnp.int32, (1, 2 * CHUNK), 1)
    return jnp.where((sj < CHUNK) & (n == 0), NEG_INF, 0.0)


def _window_mask_flat():
    qi = np.arange(CHUNK)[:, None]
    sj = np.arange(2 * CHUNK)[None, :]
    rel = qi + CHUNK - sj
    return jnp.asarray(((rel >= 0) & (rel < CHUNK)).astype(np.float32).reshape(1, -1))


def _stack_heads(ref, first, count, width, scale=None):
    x = jnp.concatenate([_bf(ref[:, (first + j) * width:(first + j + 1) * width]) for j in range(count)], axis=0)
    return x if scale is None else x * jnp.asarray(scale, x.dtype)


def _attn_fwd(qkv, bias, sinks, name):
    T = qkv.shape[0]
    nb = T // CHUNK
    L, DH, HPK = CHUNK, ATTN_DH, ATTN_HEADS // ATTN_KV
    scale = DH ** -0.5
    kcol, vcol = ATTN_HEADS * DH // LANES, ATTN_HEADS * DH // LANES + 1

    def body(q_ref, k_ref, v_ref, kp_ref, vp_ref, bias_ref, sink_ref, o_ref, lg_ref, p_ref):
        n = pl.program_id(0)
        pen = _first_block_penalty(n)
        kband = _bf(jnp.concatenate([kp_ref[...], k_ref[...]], axis=0))
        vband = _bf(jnp.concatenate([vp_ref[...], v_ref[...]], axis=0))
        for kv in range(ATTN_KV):
            lg_ref[...] = _nt(_stack_heads(q_ref, kv * HPK, HPK, DH, scale), kband[:, kv * DH:(kv + 1) * DH])
            for j in range(HPK):
                h = kv * HPK + j
                p, _ = _softmax_sink(lg_ref[j * L:(j + 1) * L, :] + bias_ref[h] + pen, sink_ref[h])
                p_ref[j * L:(j + 1) * L, :] = _bf(p)
            og = _nn(p_ref[...], vband[:, kv * DH:(kv + 1) * DH])
            for j in range(HPK):
                h = kv * HPK + j
                o_ref[:, h * DH:(h + 1) * DH] = og[j * L:(j + 1) * L].astype(BF16)

    prev = lambda i: jnp.maximum(i - 1, 0)
    return _pcall(
        body, name=name, out_shape=_sds((T, ATTN_HEADS * DH), BF16), grid=(nb,),
        in_specs=[_row_spec(L, ATTN_HEADS * DH, 0), _row_spec(L, LANES, kcol), _row_spec(L, LANES, vcol),
                  pl.BlockSpec((L, LANES), lambda i: (prev(i), kcol)), pl.BlockSpec((L, LANES), lambda i: (prev(i), vcol)),
                  pl.BlockSpec((ATTN_HEADS, L, 2 * L), lambda i: (0, 0, 0)),
                  pl.BlockSpec(memory_space=pltpu.SMEM)],
        out_specs=_row_spec(L, ATTN_HEADS * DH),
        scratch=[pltpu.VMEM((HPK * L, 2 * L), F32), pltpu.VMEM((HPK * L, 2 * L), BF16)],
        sem=("parallel",))(qkv, qkv, qkv, qkv, qkv, bias, sinks)


def _attn_bwd(qkv, datt, bias, sinks, name):
    T = qkv.shape[0]
    nb = T // CHUNK
    L, DH, HPK = CHUNK, ATTN_DH, ATTN_HEADS // ATTN_KV
    scale = DH ** -0.5
    kcol, vcol = ATTN_HEADS * DH // LANES, ATTN_HEADS * DH // LANES + 1

    def body(q_ref, k_ref, v_ref, kp_ref, vp_ref, do_ref, bias_ref, sink_ref,
             dq_ref, dk_ref, dv_ref, bsum_ref, dbias_ref, dsink_ref, pend_k, pend_v, band_k, band_v, lg_ref, dp_ref, p_ref, dl_ref):
        n = pl.program_id(0)

        @pl.when(n == 0)
        def _():
            dbias_ref[...] = jnp.zeros_like(dbias_ref)
            dsink_ref[...] = jnp.zeros_like(dsink_ref)
            bsum_ref[...] = jnp.zeros_like(bsum_ref)

        def emit_kv(dk, dv):
            dk_ref[...] = dk.astype(BF16)
            dv_ref[...] = dv.astype(BF16)
            bsum_ref[:, ATTN_HEADS * DH:ATTN_HEADS * DH + LANES] += _colsum(dk)
            bsum_ref[:, ATTN_HEADS * DH + LANES:] += _colsum(dv)

        @pl.when(n < nb)
        def _():
            pen = _first_block_penalty(n)
            kband = _bf(jnp.concatenate([kp_ref[...], k_ref[...]], axis=0))
            vband = _bf(jnp.concatenate([vp_ref[...], v_ref[...]], axis=0))
            lane1 = lax.broadcasted_iota(jnp.int32, (1, LANES), 1)
            dsink = jnp.zeros((1, LANES), F32)
            for kv in range(ATTN_KV):
                kb, vb = kband[:, kv * DH:(kv + 1) * DH], vband[:, kv * DH:(kv + 1) * DH]
                qg = _stack_heads(q_ref, kv * HPK, HPK, DH, scale)
                dog = _stack_heads(do_ref, kv * HPK, HPK, DH)
                lg_ref[...] = _nt(qg, kb)
                dp_ref[...] = _nt(dog, vb)
                for j in range(HPK):
                    h = kv * HPK + j
                    rows = slice(j * L, (j + 1) * L)
                    p, ps = _softmax_sink(lg_ref[rows, :] + bias_ref[h] + pen, sink_ref[h])
                    dp = dp_ref[rows, :]
                    delta = _rowsum(p * dp)
                    dl = p * (dp - delta)
                    dbias_ref[h] += dl
                    p_ref[rows, :] = _bf(p)
                    dl_ref[rows, :] = _bf(dl)
                    dsink = dsink + jnp.where(lane1 == h, -_colsum(ps * delta), 0.0)
                band_v[:, kv * DH:(kv + 1) * DH] = _tn(p_ref[...], dog)
                dqg = _nn(dl_ref[...], kb) * scale
                band_k[:, kv * DH:(kv + 1) * DH] = _tn(dl_ref[...], qg)
                for j in range(HPK):
                    h = kv * HPK + j
                    dq_ref[:, h * DH:(h + 1) * DH] = dqg[j * L:(j + 1) * L].astype(BF16)
                    bsum_ref[:, h * DH:(h + 1) * DH] += _colsum(dqg[j * L:(j + 1) * L])
            dsink_ref[...] += dsink

            @pl.when(n > 0)
            def _():
                emit_kv(pend_k[...] + band_k[0:L, :], pend_v[...] + band_v[0:L, :])
            pend_k[...] = band_k[L:2 * L, :]
            pend_v[...] = band_v[L:2 * L, :]

        @pl.when(n == nb)
        def _():
            emit_kv(pend_k[...], pend_v[...])

    cur = lambda i: jnp.minimum(i, nb - 1)
    prev = lambda i: jnp.maximum(jnp.minimum(i, nb - 1) - 1, 0)
    lag = lambda i: jnp.maximum(i - 1, 0)
    return _pcall(
        body, name=name,
        out_shape=[_sds((T, ATTN_HEADS * DH), BF16), _sds((T, LANES), BF16), _sds((T, LANES), BF16), _sds((1, QKV_DIM)),
                   _sds((ATTN_HEADS, L, 2 * L)), _sds((1, LANES))],
        grid=(nb + 1,),
        in_specs=[pl.BlockSpec((L, ATTN_HEADS * DH), lambda i: (cur(i), 0)),
                  pl.BlockSpec((L, LANES), lambda i: (cur(i), kcol)), pl.BlockSpec((L, LANES), lambda i: (cur(i), vcol)),
                  pl.BlockSpec((L, LANES), lambda i: (prev(i), kcol)), pl.BlockSpec((L, LANES), lambda i: (prev(i), vcol)),
                  pl.BlockSpec((L, ATTN_HEADS * DH), lambda i: (cur(i), 0)),
                  pl.BlockSpec((ATTN_HEADS, L, 2 * L), lambda i: (0, 0, 0)),
                  pl.BlockSpec(memory_space=pltpu.SMEM)],
        out_specs=[pl.BlockSpec((L, ATTN_HEADS * DH), lambda i: (cur(i), 0)),
                   pl.BlockSpec((L, LANES), lambda i: (lag(i), 0)), pl.BlockSpec((L, LANES), lambda i: (lag(i), 0)),
                   _vec_spec(QKV_DIM), pl.BlockSpec((ATTN_HEADS, L, 2 * L), lambda i: (0, 0, 0)), _vec_spec(LANES)],
        scratch=[pltpu.VMEM((L, LANES), F32), pltpu.VMEM((L, LANES), F32),
                 pltpu.VMEM((2 * L, LANES), F32), pltpu.VMEM((2 * L, LANES), F32),
                 pltpu.VMEM((HPK * L, 2 * L), F32), pltpu.VMEM((HPK * L, 2 * L), F32),
                 pltpu.VMEM((HPK * L, 2 * L), BF16), pltpu.VMEM((HPK * L, 2 * L), BF16)],
        sem=("arbitrary",))(qkv, qkv, qkv, qkv, qkv, datt, bias, sinks)


def _pad_rows(a, mult):
    pad = (-a.shape[-2]) % mult
    if pad == 0:
        return a
    cfg = [(0, 0)] * (a.ndim - 2) + [(0, pad), (0, 0)]
    return jnp.pad(a, cfg)


class _Pack:
    def __init__(self, width, mult, total_mult):
        self.width, self.mult, self.total_mult = width, mult, total_mult
        self.entries = []
        self.rows = 0

    def add(self, key, shape):
        n = int(np.prod(shape))
        rows = -(-n // self.width)
        self.entries.append((key, self.rows, rows, tuple(shape)))
        self.rows += -(-rows // self.mult) * self.mult

    @property
    def total(self):
        return -(-self.rows // self.total_mult) * self.total_mult

    def pack(self, pieces, dtype, lead=()):
        parts = []
        for key, _, rows, shape in self.entries:
            a = pieces[key].astype(dtype).reshape(lead + (-1,))
            n = int(np.prod(shape))
            a = jnp.pad(a, [(0, 0)] * len(lead) + [(0, rows * self.width - n)])
            a = a.reshape(lead + (rows, self.width))
            parts.append(_pad_rows(a, self.mult))
        out = jnp.concatenate(parts, axis=len(lead))
        return _pad_rows(out, self.total_mult)

    def unpack(self, packed, lead=()):
        out = {}
        for key, off, rows, shape in self.entries:
            a = lax.slice_in_dim(packed, off, off + rows, axis=len(lead))
            a = a.reshape(lead + (-1,))
            n = int(np.prod(shape))
            out[key] = lax.slice_in_dim(a, 0, n, axis=len(lead)).reshape(lead + shape)
        return out


def _ffn_fwd(x, h, mod, wg_t, wu_t, wd, tag, next_norm=None, gather=None):
    side = None if gather is None else (_GatherOps, gather)
    gate, up, act, *gathered = _mm_swiglu(h, wg_t, wu_t, f"ffn_gateup_{tag}", side=side)
    x_out, ffn_out, *h_next = _mm_resid([(act, wd)], x, mod[5:6], name=f"ffn_down_{tag}", norm=next_norm)
    return (x_out, dict(h=h, gate=gate, up=up, act=act, out=ffn_out), *h_next, *gathered)


def _ffn_bwd(dx_out, dffn, x_in, saved, mod, norm_w, wg_t, wu_t, wd, below, tag, exchange=None):
    side = None if exchange is None else (_ChipsOps, exchange)
    dgate, dup, *from_chips = _mm_swiglu_bwd(dffn, wd, saved["gate"], saved["up"], f"ffn_act_bwd_{tag}", side=side)
    d_wd = _mm_tn(saved["act"], dffn, name=f"ffn_dwd_{tag}")
    d_wg_t = _mm_tn(dgate, saved["h"], name=f"ffn_dwg_{tag}")
    d_wu_t = _mm_tn(dup, saved["h"], name=f"ffn_dwu_{tag}")
    dh = _mm([(dgate, wg_t), (dup, wu_t)], "nn", name=f"ffn_dh_{tag}")
    dx, d_below, acc = _norm_mod_bwd(x_in, dh, dx_out, norm_w, mod[4:5], below[0], below[1], f"ffn_norm_bwd_{tag}")
    return (dx, d_below, dict(d_wg=d_wg_t, d_wu=d_wu_t, d_wd=d_wd, acc=acc), *from_chips)


_BIG = [
    ("out_w", "out_w_even", 0, "row"), ("qkv_w", "qkv_w", 0, "col"), ("o_w", "o_w", 0, "row"),
    ("gate0", "ffn_gate_w", 0, "col"), ("up0", "ffn_up_w", 0, "col"), ("down0", "ffn_down_w", 0, "row"),
    ("gate1", "ffn_gate_w", 1, "col"), ("up1", "ffn_up_w", 1, "col"), ("down1", "ffn_down_w", 1, "row"),
    ("in_w", "in_w_even", 0, "col"),
]


def _to_wire(a, kind):
    return a.T if kind == "col" else a


_GATHER_GROUPS = [["in_w"], ["out_w"], ["gate0", "up0", "down0"], ["qkv_w", "o_w"], ["gate1", "up1", "down1"]]
_GRAD_GROUPS = [["qkv_w", "o_w", "gate1", "up1", "down1"], ["out_w", "gate0", "up0", "down0"], ["in_w"]]

_REPLICATED = ["ada_b", "norm_mix_w", "norm_ffn_w", "conv_b", "dt_bias", "a_log", "d_skip", "ssm_norm_w", "gmlp_ln_w",
               "gmlp_ln_b", "gmlp_ws", "gmlp_bs", "sinks", "rel_table", "final_norm_w"]
_TINY_SHARDED = ["conv_w", "qkv_b", "o_b"]

_WEIGHTS = ['ada_w', 'ada_b', 'norm_mix_w', 'norm_ffn_w', 'in_w_even', 'conv_w', 'conv_b', 'dt_bias', 'a_log', 'd_skip',
            'ssm_norm_w', 'gmlp_ln_w', 'gmlp_ln_b', 'gmlp_ws', 'gmlp_bs', 'out_w_even', 'qkv_w', 'qkv_b', 'o_w', 'o_b',
            'sinks', 'rel_table', 'ffn_gate_w', 'ffn_up_w', 'ffn_down_w', 'final_norm_w']


def _step(x, c, loss_target, W, M, V):
    T = x.shape[1]
    x0 = x[0]
    target = loss_target[0]
    me = 4 * lax.axis_index("x") + 2 * lax.axis_index("y") + lax.axis_index("c")

    w_wire_local = {key: _to_wire(W[name][layer].astype(BF16), kind) for key, name, layer, kind in _BIG}

    def wire_pack(keys, mult):
        gp = _Pack(D, 1, mult)
        for key in keys:
            gp.add(key, w_wire_local[key].shape)
        return gp

    gather_packs = [(gp, gp.pack(w_wire_local, BF16)) for gp in (wire_pack(keys, 16) for keys in _GATHER_GROUPS)]
    grad_packs = [wire_pack(keys, 64) for keys in _GRAD_GROUPS]
    full = {}

    def gathered_weights(group, gathered):
        shards = gather_packs[group][0].unpack(gathered, lead=(N_DEV,))
        full.update({key: a.reshape(-1, D) for key, a in shards.items()})


    small_in = _Pack(D, 8, 8)
    small_in.add("c", (1, D))
    small_in.add("conv_w", W["conv_w"][0].shape)
    small_in.add("qkv_b", W["qkv_b"][0].shape)
    small_in.add("o_b", W["o_b"][0].shape)
    sm = small_in.unpack(_all_gather(small_in.pack(
        dict(c=c, conv_w=W["conv_w"][0], qkv_b=W["qkv_b"][0], o_b=W["o_b"][0]), F32), "gather_small"), lead=(N_DEV,))
    c_all = sm["c"].reshape(N_DEV, D)
    conv_w_full = jnp.transpose(sm["conv_w"], (1, 0, 2)).reshape(SSM_CONV, CONV_DIM)
    qkv_b_full = sm["qkv_b"].reshape(1, QKV_DIM)
    o_b_full = sm["o_b"].reshape(1, D)

    ncol = W["ada_w"].shape[2]
    cond, mod_cols = _mod_matmul(c_all, W["ada_w"], "mod_matmul")
    mod_g = _all_gather(mod_cols.reshape(DEPTH * N_DEV, ncol), "gather_mod").reshape(N_DEV, DEPTH, N_DEV, ncol)
    mod_me = lax.dynamic_index_in_dim(mod_g, me, axis=2, keepdims=False)
    mod_me = jnp.transpose(mod_me, (1, 0, 2)).reshape(DEPTH, 6, D)
    mod_me = jnp.pad(mod_me, ((0, 0), (0, 2), (0, 0))).reshape(DEPTH * 8, D)
    ada_b_rows = jnp.pad(W["ada_b"].reshape(DEPTH, 6, D), ((0, 0), (0, 2), (0, 0))).reshape(DEPTH * 8, D)
    mod_all = _add_rows(mod_me, ada_b_rows, "mod_bias").reshape(DEPTH, 8, D)
    mod0, mod1 = mod_all[0], mod_all[1]
    h0, gathered_in = _norm_mod(x0, W["norm_mix_w"][0:1], mod0[1:2], mod0[0:1], "mix_norm_0",
                                side=(_GatherOps, gather_packs[0][1]))
    gathered_weights(0, gathered_in)

    in_t = full["in_w"]
    o1, o2, o3, o4 = SSM_INNER, SSM_INNER + CONV_DIM, SSM_INNER + CONV_DIM + SSM_HEADS, SSM_INNER + CONV_DIM + SSM_HEADS + GMLP_INNER
    w_z, w_xbc, w_dt, w_u, w_v = in_t[:o1], in_t[o1:o2], in_t[o2:o3], in_t[o3:o4], in_t[o4:]
    w_main = jnp.concatenate([w_z, w_u, w_v, w_xbc], axis=0)
    w_dtp = jnp.pad(w_dt, ((0, LANES - SSM_HEADS), (0, 0)))

    pad16 = lambda a: jnp.pad(a.reshape(1, SSM_HEADS), ((0, 0), (0, LANES - SSM_HEADS)))
    dtb, alog = pad16(W["dt_bias"][0]), pad16(W["a_log"][0])
    dskip = jnp.repeat(W["d_skip"][0], SSM_HEAD_DIM).reshape(1, SSM_INNER)
    ssm_nw = W["ssm_norm_w"]
    lnw, lnb = W["gmlp_ln_w"], W["gmlp_ln_b"]
    ws = W["gmlp_ws"][0]
    bs_exp = jnp.repeat(W["gmlp_bs"][0].T, CHUNK, axis=1)
    conv_b = W["conv_b"]
    nmw, nfw = W["norm_mix_w"], W["norm_ffn_w"]
    onehot_t = _bucket_onehot_t()
    head_ind = _head_indicator()
    bias = _bias_from_table(W["rel_table"].T, onehot_t, _window_mask_flat(), "rel_bias").reshape(ATTN_HEADS, CHUNK, 2 * CHUNK)
    sinks = W["sinks"][0]

    pm, gathered_a = _mm([(h0, w_main)], "nt", name="in_proj", tn_pref=1536, out_dtype=BF16,
                         side=(_GatherOps, gather_packs[1][1]))
    gathered_weights(1, gathered_a)
    out_w = full["out_w"]
    dtr = _mm([(h0, w_dtp)], "nt", name="in_proj_dt")
    cpre = _conv_fwd(pm, conv_w_full, conv_b, "conv_fwd")
    ya, ypre, sprev, gathered_b = _ssd_fwd(cpre, dtr, pm, dtb, alog, dskip, ssm_nw, head_ind, "ssd_fwd",
                                           side=(_GatherOps, gather_packs[2][1]))
    gathered_weights(2, gathered_b)
    yb, gathered_c = _gmlp_fwd(pm, lnw, lnb, ws, bs_exp, "gmlp_fwd", side=(_GatherOps, gather_packs[3][1]))
    gathered_weights(3, gathered_c)
    x1, mix0, hf0 = _mm_resid([(ya, out_w[:SSM_INNER]), (yb, out_w[SSM_INNER:])], x0, mod0[2:3], name="out_proj",
                              norm=(nfw[0:1], mod0[4:5], mod0[3:4]))
    x2, ffn0, h1, gathered_d = _ffn_fwd(x1, hf0, mod0, full["gate0"], full["up0"], full["down0"], "0",
                                        next_norm=(nmw[1:2], mod1[1:2], mod1[0:1]), gather=gather_packs[4][1])
    gathered_weights(4, gathered_d)
    qkv_t, o_w = full["qkv_w"], full["o_w"]
    w_q, w_k, w_v_att = qkv_t[:D], qkv_t[D:D + LANES], qkv_t[D + LANES:]

    qkv = _mm([(h1, qkv_t)], "nt", name="qkv_proj", bias=qkv_b_full, tn_pref=1280, out_dtype=BF16)
    att = _attn_fwd(qkv, bias, sinks, "attn_fwd")
    x3, mix1, hf1 = _mm_resid([(att, o_w)], x2, mod1[2:3], name="o_proj", bias=o_b_full,
                              norm=(nfw[1:2], mod1[4:5], mod1[3:4]))
    x4, ffn1 = _ffn_fwd(x3, hf1, mod1, full["gate1"], full["up1"], full["down1"], "1")

    dx4, dffn1, acc_f = _final_loss(x4, W["final_norm_w"].reshape(1, D), target, ffn1["out"], mod1[5:6], "final_loss")
    dx3, dmix1, gf1 = _ffn_bwd(dx4, dffn1, x3, ffn1, mod1, nfw[1:2], full["gate1"], full["up1"], full["down1"],
                               (mix1, mod1[2:3]), "1")
    datt = _mm([(dmix1, o_w)], "nt", name="o_proj_dx", out_dtype=BF16)
    d_o_w = _mm_tn(att, dmix1, name="o_proj_dw")
    dq, dk, dv, d_qkv_b, dbias, dsinks = _attn_bwd(qkv, datt, bias, sinks, "attn_bwd")
    d_table = _table_from_dbias(dbias.reshape(ATTN_HEADS, -1), onehot_t, "rel_table_grad")
    d_qkv_t = jnp.concatenate([_mm_tn(dq, h1, name="qkv_dw_q"), _mm_tn(dk, h1, name="qkv_dw_k"), _mm_tn(dv, h1, name="qkv_dw_v")], axis=0)
    g_wire = dict(qkv_w=d_qkv_t, o_w=d_o_w, gate1=gf1["d_wg"], up1=gf1["d_wu"], down1=gf1["d_wd"])

    def packed_partials(group):
        return grad_packs[group].pack({key: g_wire[key].reshape(N_DEV, -1, D) for key in _GRAD_GROUPS[group]},
                                      F32, lead=(N_DEV,))

    partials_l1 = packed_partials(0)
    dh1, theirs_l1 = _mm([(dq, w_q), (dk, w_k), (dv, w_v_att)], "nn", name="qkv_proj_dx", side=(_SiblingOps, partials_l1))
    pair_l1 = _pair_sum(partials_l1, theirs_l1, "grads_pair_sum_l1")
    dx2, dffn0, acc_n1 = _norm_mod_bwd(x2, dh1, dx3, nmw[1:2], mod1[1:2], ffn0["out"], mod0[5:6], "mix_norm_bwd_1")

    from_chips = {}
    dx1, dmix0, gf0, from_chips[0] = _ffn_bwd(dx2, dffn0, x1, ffn0, mod0, nfw[0:1], full["gate0"], full["up0"], full["down0"],
                                              (mix0, mod0[2:3]), "0", exchange=pair_l1)

    dya = _mm([(dmix0, out_w[:SSM_INNER])], "nt", name="out_proj_dx_a")
    dyb = _mm([(dmix0, out_w[SSM_INNER:])], "nt", name="out_proj_dx_b")
    d_out_w = jnp.concatenate([_mm_tn(ya, dmix0, name="out_proj_dw_a"), _mm_tn(yb, dmix0, name="out_proj_dw_b")], axis=0)
    g_wire.update(gate0=gf0["d_wg"], up0=gf0["d_wu"], down0=gf0["d_wd"], out_w=d_out_w)
    partials_ffn0 = packed_partials(1)
    du, dvg, d_ws, d_bs, acc_ln, theirs_ffn0 = _gmlp_bwd(pm, dyb, lnw, lnb, ws, bs_exp, "gmlp_bwd",
                                                         side=(_SiblingOps, partials_ffn0))
    pair_ffn0 = _pair_sum(partials_ffn0, theirs_ffn0, "grads_pair_sum_mix")
    dz, dcpre, ddtr, acc_ssd, d_ssm_nw, from_chips[1] = _ssd_bwd(
        cpre, dtr, pm, ypre, sprev, dya, dtb, alog, dskip, ssm_nw, head_ind, "ssd_bwd", side=(_ChipsOps, pair_ffn0))
    dxbc, acc_conv = _conv_bwd(dcpre, pm, conv_w_full, "conv_bwd")
    d_in_t = jnp.concatenate([
        _mm_tn(dz, h0, name="in_dw_z"), _mm_tn(dxbc, h0, name="in_dw_xbc"),
        _mm_tn(ddtr, h0, name="in_dw_dt")[:SSM_HEADS], _mm_tn(du, h0, name="in_dw_u"), _mm_tn(dvg, h0, name="in_dw_v")], axis=0)
    g_wire.update(in_w=d_in_t)
    partials_mix = packed_partials(2)
    theirs_mix = _comm_call(_SiblingOps, partials_mix, "exchange_grads_sibling")
    pair_mix = _pair_sum(partials_mix, theirs_mix, "grads_pair_sum")
    dh0, from_chips[2] = _mm([(dz, w_z), (dxbc, w_xbc), (ddtr, w_dtp), (du, w_u), (dvg, w_v)], "nn", name="in_proj_dx",
                             side=(_ChipsOps, pair_mix))
    grad_x, acc_n0 = _norm_mod_bwd(x0, dh0, dx1, nmw[0:1], mod0[1:2], None, None, "mix_norm_bwd_0")

    g_mine = {}
    for group in range(len(_GRAD_GROUPS)):
        g_mine.update(grad_packs[group].unpack(_sum_parts(from_chips[group], f"grads_chip_sum_{group}")))
    res_big = [{}, {}, {}, {}]
    for key, name, layer, kind in _BIG:
        g_nat = _to_wire(g_mine[key], kind)
        outs = _adamw(g_nat[None], W[name][layer], M[name][layer], V[name][layer], f"adamw_{key}")
        for res, out in zip(res_big, outs):
            res[key] = out

    acc_f0, acc_f1 = gf0["acc"], gf1["acc"]
    views = dict(ada_b=(DEPTH * 6, D), norm_mix_w=(DEPTH, D), norm_ffn_w=(DEPTH, D), conv_b=(1, CONV_DIM),
                 dt_bias=(1, SSM_HEADS), a_log=(1, SSM_HEADS), d_skip=(1, SSM_HEADS), ssm_norm_w=(1, D), gmlp_ln_w=(1, D),
                 gmlp_ln_b=(1, D), gmlp_ws=(GMLP_GROUPS * CHUNK, CHUNK), gmlp_bs=(GMLP_GROUPS, CHUNK),
                 sinks=(1, ATTN_HEADS), rel_table=(REL_BUCKETS, ATTN_HEADS), final_norm_w=(1, D),
                 conv_w=(SSM_CONV, CONV_DIM), qkv_b=(1, QKV_DIM), o_b=(1, D), loss=(1, D))
    canvas = _Canvas()
    for key, (r, cdim) in views.items():
        canvas.add(key, r, cdim, blocks=GMLP_GROUPS if key == "gmlp_ws" else 1)
    row = lambda i: slice(i, i + 1)
    sources = [
        (acc_n0, [("ada_b", row(1), 0), ("ada_b", row(0), 1), ("norm_mix_w", row(2), 0)]),
        (acc_f0, [("ada_b", row(3), 2), ("ada_b", row(1), 3), ("ada_b", row(0), 4), ("norm_ffn_w", row(2), 0)]),
        (acc_n1, [("ada_b", row(3), 5), ("ada_b", row(1), 6), ("ada_b", row(0), 7), ("norm_mix_w", row(2), 1)]),
        (acc_f1, [("ada_b", row(3), 8), ("ada_b", row(1), 9), ("ada_b", row(0), 10), ("norm_ffn_w", row(2), 1),
                  ("o_b", row(4), 0)]),
        (acc_f, [("ada_b", row(3), 11), ("final_norm_w", row(0), 0), ("loss", row(1), 0)]),
        (acc_conv, [("conv_w", slice(0, SSM_CONV), 0), ("conv_b", row(4), 0)]),
        (acc_ssd, [("dt_bias", row(0), 0), ("a_log", row(1), 0), ("d_skip", row(2), 0)]),
        (d_ssm_nw, [("ssm_norm_w", row(0), 0)]),
        (acc_ln, [("gmlp_ln_w", row(0), 0), ("gmlp_ln_b", row(1), 0)]),
        (d_ws.reshape(GMLP_GROUPS * CHUNK, CHUNK), [("gmlp_ws", slice(0, GMLP_GROUPS * CHUNK), 0)]),
        (d_bs.T, [("gmlp_bs", slice(0, GMLP_GROUPS), 0)]),
        (dsinks, [("sinks", row(0), 0)]),
        (d_table, [("rel_table", slice(0, REL_BUCKETS), 0)]),
        (d_qkv_b, [("qkv_b", row(0), 0)]),
    ]
    parts_small = _all_gather(_canvas_fill(canvas, sources, "small_grads_canvas"), "gather_small_grads")
    view = lambda a, key: a.reshape(views[key])
    params = [(name, view(W[name], name), view(M[name], name), view(V[name], name)) for name in _REPLICATED]
    small_out = _adamw_canvas(canvas, parts_small, params, _TINY_SHARDED + ["loss"], "adamw_small")
    loss = small_out["loss"][0][0, 0]
    res_small = [{name: small_out[name][k].reshape(W[name].shape) for name in _REPLICATED} for k in range(4)]

    n_cw, n_qb, n_ob = W["conv_w"].shape[2], W["qkv_b"].shape[1], W["o_b"].shape[1]
    g_tiny = dict(conv_w=lax.dynamic_slice_in_dim(small_out["conv_w"][0], me * n_cw, n_cw, axis=1)[None],
                  qkv_b=lax.dynamic_slice_in_dim(small_out["qkv_b"][0], me * n_qb, n_qb, axis=1),
                  o_b=lax.dynamic_slice_in_dim(small_out["o_b"][0], me * n_ob, n_ob, axis=1))
    tiny = _Pack(D, 8, 8)
    for name in _TINY_SHARDED:
        tiny.add(name, W[name].shape)
    pkt = lambda S: tiny.pack({name: S[name] for name in _TINY_SHARDED}, F32)
    res_tiny = [tiny.unpack(r) for r in _adamw(pkt(g_tiny)[None], pkt(W), pkt(M), pkt(V), "adamw_tiny")]

    dmod_all = parts_small[:, canvas.offset["ada_b"]:canvas.offset["ada_b"] + DEPTH * 6].reshape(N_DEV, DEPTH, 6 * D)
    dmod_cols = jnp.transpose(lax.dynamic_slice_in_dim(dmod_all, me * ncol, ncol, axis=2), (1, 0, 2))
    g_ada_w = _ada_w_grad(cond, dmod_cols, "ada_w_grad")
    flat = lambda a: a.reshape(DEPTH * D, ncol)
    res_ada = [r.reshape(DEPTH, D, ncol) for r in _adamw(flat(g_ada_w)[None], flat(W["ada_w"]), flat(M["ada_w"]), flat(V["ada_w"]), "adamw_ada_w")]

    def result(kind_idx, name):
        if name == "ada_w":
            return res_ada[kind_idx]
        if name in _REPLICATED:
            return res_small[kind_idx][name]
        if name in _TINY_SHARDED:
            return res_tiny[kind_idx][name]
        pieces = [res_big[kind_idx][key] for key, nm, layer, kind in _BIG if nm == name]
        return jnp.stack(pieces)

    outs = [loss, grad_x[None]]
    for kind_idx in range(4):
        outs += [result(kind_idx, name) for name in _WEIGHTS]
    return tuple(outs)


def kernel(x, c, ada_w, ada_b, norm_mix_w, norm_ffn_w, in_w_even, conv_w, conv_b, dt_bias, a_log, d_skip, ssm_norm_w, gmlp_ln_w, gmlp_ln_b, gmlp_ws, gmlp_bs, out_w_even, qkv_w, qkv_b, o_w, o_b, sinks, rel_table, ffn_gate_w, ffn_up_w, ffn_down_w, final_norm_w, loss_target, m_ada_w, m_ada_b, m_norm_mix_w, m_norm_ffn_w, m_in_w_even, m_conv_w, m_conv_b, m_dt_bias, m_a_log, m_d_skip, m_ssm_norm_w, m_gmlp_ln_w, m_gmlp_ln_b, m_gmlp_ws, m_gmlp_bs, m_out_w_even, m_qkv_w, m_qkv_b, m_o_w, m_o_b, m_sinks, m_rel_table, m_ffn_gate_w, m_ffn_up_w, m_ffn_down_w, m_final_norm_w, v_ada_w, v_ada_b, v_norm_mix_w, v_norm_ffn_w, v_in_w_even, v_conv_w, v_conv_b, v_dt_bias, v_a_log, v_d_skip, v_ssm_norm_w, v_gmlp_ln_w, v_gmlp_ln_b, v_gmlp_ws, v_gmlp_bs, v_out_w_even, v_qkv_w, v_qkv_b, v_o_w, v_o_b, v_sinks, v_rel_table, v_ffn_gate_w, v_ffn_up_w, v_ffn_down_w, v_final_norm_w):
    args = locals()
    W = {n: args[n] for n in _WEIGHTS}
    M = {n: args["m_" + n] for n in _WEIGHTS}
    V = {n: args["v_" + n] for n in _WEIGHTS}
    return _step(x, c, loss_target, W, M, V)
```

```python
import functools
import math

import numpy as np
import jax
import jax.numpy as jnp
from jax import lax
from jax.experimental import pallas as pl
from jax.experimental.pallas import tpu as pltpu

F32 = jnp.float32
BF16 = jnp.bfloat16
HIGHEST = lax.Precision.HIGHEST
MESH = pl.DeviceIdType.MESH

N_DEV = 8
D = 1024
DEPTH = 2
SSM_HEADS = 16
SSM_HEAD_DIM = 64
SSM_INNER = 1024
SSM_GROUPS = 2
SSM_STATE = 128
SSM_CONV = 4
CHUNK = 128
CONV_DIM = SSM_INNER + 2 * SSM_GROUPS * SSM_STATE
GMLP_GROUPS = 8
GMLP_INNER = 1024
IN_EVEN = 4624
ATTN_HEADS = 16
ATTN_KV = 2
ATTN_DH = 64
QKV_DIM = 1280
REL_BUCKETS = 32
REL_MAX_DIST = 128
FFN = 2816
EPS = 1e-6
NEG_INF = -1e30
LANES = 128

ADAM_LR = 0.001
ADAM_B1 = 0.9
ADAM_B2 = 0.999
ADAM_EPS = 1e-08
ADAM_WD = 0.01
ADAM_STEP = 10

VMEM_LIMIT_BYTES = 56 * 1024 * 1024
ROW_TILE = 512


def _pcall(body, *, name, out_shape, grid=(), in_specs=None, out_specs=None, scratch=(), sem=None):
    params = dict(vmem_limit_bytes=VMEM_LIMIT_BYTES)
    if sem is not None:
        params["dimension_semantics"] = sem
    specs = {} if in_specs is None else dict(in_specs=in_specs, out_specs=out_specs)
    return pl.pallas_call(
        body, name=name, out_shape=out_shape, grid=grid, **specs,
        scratch_shapes=list(scratch), compiler_params=pltpu.CompilerParams(**params))


def _call(body, args, side=None, *, name, out_shape, grid, in_specs, out_specs, scratch=(), sem=None):
    if side is None:
        return _pcall(body, name=name, out_shape=out_shape, grid=grid, in_specs=in_specs, out_specs=out_specs,
                      scratch=scratch, sem=sem)(*args)
    ops_cls, x = side
    n_in, n_out, n_scr = len(in_specs), len(out_shape), len(scratch)
    steps = int(np.prod(grid))
    hbm = pl.BlockSpec(memory_space=pl.ANY)

    def wrapped(*refs):
        ins, x_ref = refs[:n_in], refs[n_in]
        outs, r_ref = refs[n_in + 1:n_in + 1 + n_out], refs[n_in + 1 + n_out]
        scr, sems = refs[n_in + 2 + n_out:n_in + 2 + n_out + n_scr], refs[n_in + 2 + n_out + n_scr:]
        ops = ops_cls(x_ref, r_ref, *sems)
        step = pl.program_id(0)
        for axis in range(1, len(grid)):
            step = step * grid[axis] + pl.program_id(axis)
        pl.when(step == 0)(ops.start)
        body(*ins, *outs, *scr)
        pl.when(step == (3 * steps) // 4)(ops.forward)
        pl.when(step == steps - 1)(ops.finish)

    return _pcall(
        wrapped, name=name, out_shape=list(out_shape) + [ops_cls.result(x)], grid=grid,
        in_specs=list(in_specs) + [hbm], out_specs=list(out_specs) + [hbm],
        scratch=list(scratch) + ops_cls.scratch(), sem=("arbitrary",) * len(grid))(*args, x)


def _tile(n, pref):
    if n <= pref:
        return n
    best = None
    for t in range(LANES, pref + 1, LANES):
        if n % t == 0:
            best = t
    assert best is not None, (n, pref)
    return best


def _rows(T):
    return min(ROW_TILE, T)


def _sds(shape, dtype=F32):
    return jax.ShapeDtypeStruct(shape, dtype)


def _row_spec(tm, c, col=0):
    return pl.BlockSpec((tm, c), lambda i, col=col: (i, col))


def _vec_spec(c, r=1):
    return pl.BlockSpec((r, c), lambda i: (0, 0))


def _sigmoid(x):
    return jax.nn.sigmoid(x)


def _silu(x):
    return x * _sigmoid(x)


def _dsilu(x):
    s = _sigmoid(x)
    return s * (1.0 + x * (1.0 - s))


def _silu_and_grad(x):
    s = _sigmoid(x)
    return x * s, s * (1.0 + x * (1.0 - s))


def _gelu(x):
    return 0.5 * x * (1.0 + lax.erf(x * 0.7071067811865476))


def _gelu_and_grad(x):
    phi = 0.5 * (1.0 + lax.erf(x * 0.7071067811865476))
    return x * phi, phi + x * jnp.exp(-0.5 * x * x) * 0.3989422804014327


def _dot(a, b, dims, precision=None):
    return lax.dot_general(a, b, (dims, ((), ())), precision=precision, preferred_element_type=F32)


def _nn(a, b, precision=None):
    return _dot(a, b, ((1,), (0,)), precision)


def _nt(a, b, precision=None):
    return _dot(a, b, ((1,), (1,)), precision)


def _tn(a, b, precision=None):
    return _dot(a, b, ((0,), (0,)), precision)


def _bf(x):
    return x.astype(BF16)


def _colsum(x):
    return jnp.sum(x, axis=0, keepdims=True)


def _rowsum(x):
    return jnp.sum(x, axis=1, keepdims=True)


def _allsum(x):
    return _colsum(_rowsum(x))


def _comm_call(ops_cls, x, name, from_vmem=False):
    def body(x_ref, out_ref, *sems):
        ops = ops_cls(x_ref, out_ref, *sems)
        ops.start()
        ops.forward()
        ops.finish()

    return pl.pallas_call(
        body, name=name, out_shape=ops_cls.result(x),
        in_specs=[pl.BlockSpec(memory_space=pltpu.VMEM if from_vmem else pl.ANY)],
        out_specs=pl.BlockSpec(memory_space=pl.ANY), scratch_shapes=ops_cls.scratch(),
    )(x)


def _all_gather(x, name):
    return _comm_call(_GatherOps, x, name, from_vmem=True)


class _GatherOps:
    def __init__(self, x_ref, out_ref, send_sems, recv_sems, local_sem):
        self.x_ref, self.out_ref = x_ref, out_ref
        self.send_sems, self.recv_sems, self.local_sem = send_sems, recv_sems, local_sem
        mx, my, mc = lax.axis_index("x"), lax.axis_index("y"), lax.axis_index("c")
        self.mc = mc
        self.me, self.sibling = (mx, my, mc), (mx, my, 1 - mc)
        self.chips = [(1 - mx, my), (mx, 1 - my), (1 - mx, 1 - my)]

    @staticmethod
    def result(x):
        return _sds((N_DEV,) + x.shape, x.dtype)

    @staticmethod
    def scratch():
        return [pltpu.SemaphoreType.DMA((7,)), pltpu.SemaphoreType.DMA((7,)), pltpu.SemaphoreType.DMA(())]

    def _slot(self, px, py, pc):
        return self.out_ref.at[4 * px + 2 * py + pc]

    def _copy(self, k, block, to, own=False):
        return pltpu.make_async_remote_copy(
            src_ref=self.x_ref if own else self._slot(*block), dst_ref=self._slot(*block),
            send_sem=self.send_sems.at[k], recv_sem=self.recv_sems.at[k], device_id=to, device_id_type=MESH)

    def _mine(self):
        return pltpu.make_async_copy(self.x_ref, self._slot(*self.me), self.local_sem)

    def _first(self):
        return [self._copy(0, self.me, self.sibling, own=True)] + [
            self._copy(1 + j, self.me, (*chip, self.mc), own=True) for j, chip in enumerate(self.chips)]

    def _passed(self):
        return [self._copy(4 + j, (*chip, self.mc), self.sibling) for j, chip in enumerate(self.chips)]

    def start(self):
        self._mine().start()
        for cp in self._first():
            cp.start()

    def forward(self):
        passed = self._passed()
        for j, chip in enumerate(self.chips):
            self._copy(1 + j, (*chip, self.mc), self.me).wait_recv()
            passed[j].start()

    def finish(self):
        self._copy(0, self.sibling, self.me).wait_recv()
        for j, chip in enumerate(self.chips):
            self._copy(4 + j, (*chip, 1 - self.mc), self.me).wait_recv()
        for cp in self._first() + self._passed():
            cp.wait_send()
        self._mine().wait()


N_CHIP = 4


class _SiblingOps:
    def __init__(self, p_ref, theirs_ref, send_sems, recv_sems):
        self.p_ref, self.theirs_ref, self.send_sems, self.recv_sems = p_ref, theirs_ref, send_sems, recv_sems

    @staticmethod
    def result(p):
        return _sds((N_CHIP,) + p.shape[1:], p.dtype)

    @staticmethod
    def scratch():
        return [pltpu.SemaphoreType.DMA((N_CHIP,))] * 2

    def _copies(self):
        mx, my, mc = lax.axis_index("x"), lax.axis_index("y"), lax.axis_index("c")
        return [pltpu.make_async_remote_copy(
            src_ref=self.p_ref.at[2 * chip + 1 - mc], dst_ref=self.theirs_ref.at[chip],
            send_sem=self.send_sems.at[chip], recv_sem=self.recv_sems.at[chip],
            device_id=(mx, my, 1 - mc), device_id_type=MESH) for chip in range(N_CHIP)]

    def start(self):
        for cp in self._copies():
            cp.start()

    def forward(self):
        pass

    def finish(self):
        for cp in self._copies():
            cp.wait()


class _ChipsOps:
    def __init__(self, q_ref, out_ref, send_sems, recv_sems, local_sem):
        self.q_ref, self.out_ref = q_ref, out_ref
        self.send_sems, self.recv_sems, self.local_sem = send_sems, recv_sems, local_sem

    @staticmethod
    def result(q):
        return _sds(q.shape, q.dtype)

    @staticmethod
    def scratch():
        return [pltpu.SemaphoreType.DMA((N_CHIP - 1,)), pltpu.SemaphoreType.DMA((N_CHIP - 1,)), pltpu.SemaphoreType.DMA(())]

    def _copies(self):
        mx, my, mc = lax.axis_index("x"), lax.axis_index("y"), lax.axis_index("c")
        me = 2 * mx + my
        local = pltpu.make_async_copy(self.q_ref.at[me], self.out_ref.at[me], self.local_sem)
        remote = []
        for r in range(1, N_CHIP):
            px = 1 - mx if r & 2 else mx
            py = 1 - my if r & 1 else my
            remote.append(pltpu.make_async_remote_copy(
                src_ref=self.q_ref.at[2 * px + py], dst_ref=self.out_ref.at[me],
                send_sem=self.send_sems.at[r - 1], recv_sem=self.recv_sems.at[r - 1],
                device_id=(px, py, mc), device_id_type=MESH))
        return local, remote

    def start(self):
        local, remote = self._copies()
        local.start()
        for cp in remote:
            cp.start()

    def forward(self):
        pass

    def finish(self):
        local, remote = self._copies()
        for cp in remote:
            cp.wait()
        local.wait()


def _pair_sum(p, theirs, name):
    n, R, C = theirs.shape
    tr = _tile_rows(R, 256)

    def body(p_ref, t_ref, o_ref):
        mc = lax.axis_index("c")
        o_ref[0] = (p_ref[0, mc] + t_ref[0]).astype(BF16)

    blk = pl.BlockSpec((1, tr, C), lambda s, i: (s, i, 0))
    return _pcall(body, name=name, out_shape=_sds((n, R, C), BF16), grid=(n, R // tr),
                  in_specs=[pl.BlockSpec((1, 2, tr, C), lambda s, i: (s, 0, i, 0)), blk],
                  out_specs=blk, sem=("parallel", "parallel"))(p.reshape(n, 2, R, C), theirs)


def _sum_parts(parts, name):
    P, R, C = parts.shape
    tr = _tile_rows(R, 256)

    def body(p_ref, o_ref):
        g = p_ref[0].astype(F32)
        for k in range(1, P):
            g = g + p_ref[k].astype(F32)
        o_ref[...] = g

    return _pcall(body, name=name, out_shape=_sds((R, C)), grid=(R // tr,),
                  in_specs=[pl.BlockSpec((P, tr, C), lambda i: (0, i, 0))],
                  out_specs=pl.BlockSpec((tr, C), lambda i: (i, 0)), sem=("parallel",))(parts)


def _adamw(parts, w, m, v, name):
    P, R, C = parts.shape
    tr = R if R <= 256 else _tile_rows(R, 256)

    def body(p_ref, w_ref, m_ref, v_ref, g_ref, d_ref, nm_ref, nv_ref):
        g = p_ref[0]
        for k in range(1, P):
            g = g + p_ref[k]
        nm = ADAM_B1 * m_ref[...] + (1.0 - ADAM_B1) * g
        nv = ADAM_B2 * v_ref[...] + (1.0 - ADAM_B2) * (g * g)
        m_hat = nm / (1.0 - ADAM_B1 ** ADAM_STEP)
        v_hat = nv / (1.0 - ADAM_B2 ** ADAM_STEP)
        g_ref[...] = g
        d_ref[...] = -ADAM_LR * (m_hat / (jnp.sqrt(v_hat) + ADAM_EPS) + ADAM_WD * w_ref[...])
        nm_ref[...] = nm
        nv_ref[...] = nv

    blk = pl.BlockSpec((tr, C), lambda i: (i, 0))
    return _pcall(
        body, name=name, out_shape=[_sds((R, C))] * 4, grid=(R // tr,),
        in_specs=[pl.BlockSpec((P, tr, C), lambda i: (0, i, 0)), blk, blk, blk],
        out_specs=[blk] * 4, sem=("parallel",))(parts, w, m, v)


def _adam_update(g, w, m, v):
    nm = ADAM_B1 * m + (1.0 - ADAM_B1) * g
    nv = ADAM_B2 * v + (1.0 - ADAM_B2) * (g * g)
    m_hat = nm / (1.0 - ADAM_B1 ** ADAM_STEP)
    v_hat = nv / (1.0 - ADAM_B2 ** ADAM_STEP)
    return -ADAM_LR * (m_hat / (jnp.sqrt(v_hat) + ADAM_EPS) + ADAM_WD * w), nm, nv


class _Canvas:
    def __init__(self):
        self.offset, self.views, self.rows = {}, {}, 0

    def add(self, key, r, c, blocks=1):
        need = r // blocks if blocks > 1 else r * (-(-c // D))
        if need >= 8:
            self.rows = -(-self.rows // 8) * 8
        self.offset[key], self.views[key] = self.rows, (r, c, blocks)
        self.rows += need

    @property
    def total(self):
        return -(-self.rows // 8) * 8

    def cells(self, key):
        (r, c, blocks), off = self.views[key], self.offset[key]
        if blocks > 1:
            n = r // blocks
            return [(off, n, b * c, c, slice(b * n, (b + 1) * n), slice(0, c)) for b in range(blocks)]
        if c <= D:
            return [(off, r, 0, c, slice(0, r), slice(0, c))]
        chunks = -(-c // D)
        return [(off + i * chunks + j, 1, 0, min(D, c - j * D), slice(i, i + 1), slice(j * D, min(c, (j + 1) * D)))
                for i in range(r) for j in range(chunks)]


def _canvas_fill(canvas, sources, name):
    arrays = [a for a, _ in sources]

    def body(*refs):
        out_ref = refs[-1]
        out_ref[...] = jnp.zeros_like(out_ref)
        for ref, (_, items) in zip(refs[:-1], sources):
            for key, src_rows, view_row in items:
                n = src_rows.stop - src_rows.start
                for row, count, lane, width, vrows, vcols in canvas.cells(key):
                    lo, hi = max(vrows.start, view_row), min(vrows.stop, view_row + n)
                    if lo < hi:
                        src = slice(src_rows.start + lo - view_row, src_rows.start + hi - view_row)
                        out_ref[row + lo - vrows.start:row + hi - vrows.start, lane:lane + width] = ref[src, vcols]

    return _pcall(body, name=name, out_shape=_sds((canvas.total, D)))(*arrays)


def _adamw_canvas(canvas, parts, params, sum_only, name):
    n = len(params)

    def body(p_ref, *refs):
        sum_ref = refs[-1]
        g_all = p_ref[0]
        for k in range(1, N_DEV):
            g_all = g_all + p_ref[k]
        sum_ref[...] = g_all
        for i, (key, _, _, _) in enumerate(params):
            w_ref, m_ref, v_ref = refs[3 * i:3 * i + 3]
            outs = refs[3 * n + 4 * i:3 * n + 4 * i + 4]
            for row, count, lane, width, vrows, vcols in canvas.cells(key):
                g = sum_ref[row:row + count, lane:lane + width]
                delta, nm, nv = _adam_update(g, w_ref[vrows, vcols], m_ref[vrows, vcols], v_ref[vrows, vcols])
                for ref, val in zip(outs, (g, delta, nm, nv)):
                    ref[vrows, vcols] = val
        for i, key in enumerate(sum_only):
            for row, count, lane, width, vrows, vcols in canvas.cells(key):
                refs[7 * n + i][vrows, vcols] = sum_ref[row:row + count, lane:lane + width]

    args = [a for _, w, m, v in params for a in (w, m, v)]
    out_shape = [_sds(w.shape) for _, w, _, _ in params for _ in range(4)] + [_sds(canvas.views[k][:2]) for k in sum_only]
    res = _pcall(body, name=name, out_shape=out_shape, scratch=[pltpu.VMEM(parts.shape[1:], F32)])(parts, *args)
    out = {key: res[4 * i:4 * i + 4] for i, (key, _, _, _) in enumerate(params)}
    out.update({key: [res[4 * n + i]] for i, key in enumerate(sum_only)})
    return out


def _tile_rows(n, pref):
    best = None
    for t in range(8, pref + 1, 8):
        if n % t == 0:
            best = t
    assert best is not None, (n, pref)
    return best


def _mm(pairs, mode, *, name, out_dtype=F32, bias=None, tn_pref=1024, side=None):
    M = pairs[0][0].shape[0]
    N = pairs[0][1].shape[1] if mode == "nn" else pairs[0][1].shape[0]
    tm, tn = _rows(M), _tile(N, tn_pref)
    n_pairs = len(pairs)
    has_bias = bias is not None

    def body(*refs):
        acc = _pairs_dot(refs[:2 * n_pairs], mode)
        if has_bias:
            acc = acc + refs[2 * n_pairs][...]
        refs[-1][...] = acc.astype(refs[-1].dtype)

    in_specs, args = _pair_specs(pairs, mode, tm, tn)
    if has_bias:
        in_specs.append(pl.BlockSpec((1, tn), lambda j, i: (0, j)))
        args.append(bias)
    res = _call(body, args, side, name=name, out_shape=[_sds((M, N), out_dtype)], grid=(N // tn, M // tm),
                in_specs=in_specs, out_specs=[pl.BlockSpec((tm, tn), lambda j, i: (i, j))], sem=("parallel", "parallel"))
    return res[0] if side is None else (res[0], res[1])


def _pairs_dot(ab, mode):
    acc = None
    for p in range(len(ab) // 2):
        a, b = _bf(ab[2 * p][...]), _bf(ab[2 * p + 1][...])
        d = _nn(a, b) if mode == "nn" else _nt(a, b)
        acc = d if acc is None else acc + d
    return acc


def _pair_specs(pairs, mode, tm, tn):
    in_specs, args = [], []
    for a, b in pairs:
        K = a.shape[1]
        in_specs.append(pl.BlockSpec((tm, K), lambda j, i: (i, 0)))
        if mode == "nn":
            in_specs.append(pl.BlockSpec((K, tn), lambda j, i: (0, j)))
        else:
            in_specs.append(pl.BlockSpec((tn, K), lambda j, i: (j, 0)))
        args += [a, b]
    return in_specs, args


def _mm_resid(pairs, resid, gvec, *, name, bias=None, norm=None):
    M, N = resid.shape
    tm = _rows(M)
    n_pairs = len(pairs)
    has_bias, has_norm = bias is not None, norm is not None

    def body(*refs):
        acc = _pairs_dot(refs[:2 * n_pairs], "nn")
        pos = 2 * n_pairs
        if has_bias:
            acc = acc + refs[pos][...]
            pos += 1
        xv = refs[pos][...] + refs[pos + 1][...] * acc
        outs = refs[pos + 2 + 3 * has_norm:]
        outs[0][...] = xv
        outs[1][...] = acc.astype(BF16)
        if has_norm:
            w_ref, sc_ref, sh_ref = refs[pos + 2:pos + 5]
            r = lax.rsqrt(jnp.mean(xv * xv, axis=-1, keepdims=True) + EPS)
            outs[2][...] = ((xv * r * w_ref[...]) * (1.0 + sc_ref[...]) + sh_ref[...]).astype(BF16)

    in_specs, args = _pair_specs(pairs, "nn", tm, N)
    vec = pl.BlockSpec((1, N), lambda j, i: (0, 0))
    row = pl.BlockSpec((tm, N), lambda j, i: (i, 0))
    if has_bias:
        in_specs.append(vec)
        args.append(bias)
    in_specs += [row, vec] + [vec] * (3 * has_norm)
    args += [resid, gvec] + (list(norm) if has_norm else [])
    return _pcall(body, name=name, out_shape=[_sds((M, N)), _sds((M, N), BF16)] + [_sds((M, N), BF16)] * has_norm,
                  grid=(1, M // tm), in_specs=in_specs, out_specs=[row] * (2 + has_norm),
                  sem=("parallel", "parallel"))(*args)


def _mm_tn(a, b, *, name, tm_pref=1408, tn_pref=1536):
    K, M = a.shape
    N = b.shape[1]
    tm, tn = _tile(M, tm_pref), _tile(N, tn_pref)
    tk = K if K <= 2 * ROW_TILE else 2 * ROW_TILE

    def body(a_ref, b_ref, o_ref):
        @pl.when(pl.program_id(2) == 0)
        def _():
            o_ref[...] = jnp.zeros_like(o_ref)
        o_ref[...] += _tn(_bf(a_ref[...]), _bf(b_ref[...]))

    return _pcall(
        body, name=name, out_shape=_sds((M, N)), grid=(M // tm, N // tn, K // tk),
        in_specs=[pl.BlockSpec((tk, tm), lambda i, j, k: (k, i)), pl.BlockSpec((tk, tn), lambda i, j, k: (k, j))],
        out_specs=pl.BlockSpec((tm, tn), lambda i, j, k: (i, j)),
        sem=("parallel", "parallel", "arbitrary"))(a, b)


def _mm_swiglu(h, wg_t, wu_t, name, side=None):
    M, K = h.shape
    N = wg_t.shape[0]
    tm, tn = _rows(M), _tile(N, 1408)

    def body(h_ref, wg_ref, wu_ref, gate_ref, up_ref, act_ref):
        hv = _bf(h_ref[...])
        gate = _nt(hv, wg_ref[...])
        up = _nt(hv, wu_ref[...])
        gate_ref[...] = gate.astype(BF16)
        up_ref[...] = up.astype(BF16)
        act_ref[...] = (_silu(gate) * up).astype(BF16)

    w_spec = pl.BlockSpec((tn, K), lambda j, i: (j, 0))
    o_spec = pl.BlockSpec((tm, tn), lambda j, i: (i, j))
    return _call(body, (h, wg_t, wu_t), side, name=name,
                 out_shape=[_sds((M, N), BF16)] * 3, grid=(N // tn, M // tm),
                 in_specs=[pl.BlockSpec((tm, K), lambda j, i: (i, 0)), w_spec, w_spec], out_specs=[o_spec] * 3,
                 sem=("parallel", "parallel"))


def _mm_swiglu_bwd(dout, wd, gate, up, name, side=None):
    M, K = dout.shape
    N = wd.shape[0]
    tm, tn = _rows(M), _tile(N, 1408)

    def body(d_ref, wd_ref, gate_ref, up_ref, dg_ref, du_ref):
        dact = _nt(_bf(d_ref[...]), wd_ref[...])
        act, dact_dg = _silu_and_grad(gate_ref[...].astype(F32))
        dg_ref[...] = (dact * up_ref[...].astype(F32) * dact_dg).astype(BF16)
        du_ref[...] = (dact * act).astype(BF16)

    t_spec = pl.BlockSpec((tm, tn), lambda j, i: (i, j))
    return _call(
        body, (dout, wd, gate, up), side, name=name, out_shape=[_sds((M, N), BF16)] * 2, grid=(N // tn, M // tm),
        in_specs=[pl.BlockSpec((tm, K), lambda j, i: (i, 0)), pl.BlockSpec((tn, K), lambda j, i: (j, 0)), t_spec, t_spec],
        out_specs=[t_spec] * 2, sem=("parallel", "parallel"))


def _norm_mod(x, w, sc, sh, name, side=None):
    T = x.shape[0]
    tm = _rows(T)

    def body(x_ref, w_ref, sc_ref, sh_ref, o_ref):
        xv = x_ref[...]
        r = lax.rsqrt(jnp.mean(xv * xv, axis=-1, keepdims=True) + EPS)
        o_ref[...] = ((xv * r * w_ref[...]) * (1.0 + sc_ref[...]) + sh_ref[...]).astype(BF16)

    return _call(body, (x, w, sc, sh), side, name=name, out_shape=[_sds((T, D), BF16)], grid=(T // tm,),
                 in_specs=[_row_spec(tm, D), _vec_spec(D), _vec_spec(D), _vec_spec(D)],
                 out_specs=[_row_spec(tm, D)], sem=("parallel",))


def _gate_rows(dxv, br_ref, g_ref, db_ref, acc_ref):
    db = g_ref[...] * dxv
    db_ref[...] = db.astype(BF16)
    acc_ref[3:4, :] += _colsum(dxv * br_ref[...].astype(F32))
    acc_ref[4:5, :] += _colsum(db)


def _norm_mod_bwd(x, dh, dres, w, sc, branch, g, name, side=None):
    T = x.shape[0]
    tm = _rows(T)
    gated = branch is not None

    def body(x_ref, dh_ref, dres_ref, w_ref, sc_ref, *rest):
        (br_ref, g_ref, dx_ref, db_ref, acc_ref) = rest if gated else (None, None, rest[0], None, rest[1])

        @pl.when(pl.program_id(0) == 0)
        def _():
            acc_ref[...] = jnp.zeros_like(acc_ref)
        xv, dh_v, wv = x_ref[...], dh_ref[...], w_ref[...]
        r = lax.rsqrt(jnp.mean(xv * xv, axis=-1, keepdims=True) + EPS)
        n = xv * r
        dnw = dh_v * (1.0 + sc_ref[...])
        dn = dnw * wv
        dxv = dres_ref[...] + r * (dn - n * jnp.mean(dn * n, axis=-1, keepdims=True))
        dx_ref[...] = dxv
        acc_ref[0:1, :] += _colsum(dh_v * (n * wv))
        acc_ref[1:2, :] += _colsum(dh_v)
        acc_ref[2:3, :] += _colsum(dnw * n)
        if gated:
            _gate_rows(dxv, br_ref, g_ref, db_ref, acc_ref)

    row = _row_spec(tm, D)
    args = (x, dh, dres, w, sc) + ((branch, g) if gated else ())
    return _call(body, args, side, name=name,
                 out_shape=[_sds((T, D))] + ([_sds((T, D), BF16)] if gated else []) + [_sds((8, D))], grid=(T // tm,),
                 in_specs=[row, row, row, _vec_spec(D), _vec_spec(D)] + ([row, _vec_spec(D)] if gated else []),
                 out_specs=[row] + ([row] if gated else []) + [_vec_spec(D, 8)], sem=("arbitrary",))


def _final_loss(x, wf, target, branch, g, name):
    T = x.shape[0]
    tm = _rows(T)

    def body(x_ref, w_ref, t_ref, br_ref, g_ref, dx_ref, db_ref, acc_ref):
        @pl.when(pl.program_id(0) == 0)
        def _():
            acc_ref[...] = jnp.zeros_like(acc_ref)
        xv, wv = x_ref[...], w_ref[...]
        r = lax.rsqrt(jnp.mean(xv * xv, axis=-1, keepdims=True) + EPS)
        n = xv * r
        err = n * wv - t_ref[...]
        dy = err * (1.0 / D)
        dn = dy * wv
        dxv = r * (dn - n * jnp.mean(dn * n, axis=-1, keepdims=True))
        dx_ref[...] = dxv
        acc_ref[0:1, :] += _colsum(dy * n)
        acc_ref[1:2, :] += jnp.broadcast_to(_allsum(err * err) * (0.5 / D), (1, D))
        _gate_rows(dxv, br_ref, g_ref, db_ref, acc_ref)

    row = _row_spec(tm, D)
    return _pcall(body, name=name, out_shape=[_sds((T, D)), _sds((T, D), BF16), _sds((8, D))], grid=(T // tm,),
                  in_specs=[row, _vec_spec(D), row, row, _vec_spec(D)],
                  out_specs=[row, row, _vec_spec(D, 8)], sem=("arbitrary",))(x, wf, target, branch, g)


def _mod_matmul(c_all, ada_w, name):
    n = ada_w.shape[2]

    def body(c_ref, w_ref, cond_ref, o_ref):
        cond = _silu(c_ref[...])
        cond_ref[...] = cond
        o_ref[0] = _nn(cond, w_ref[0])

    return _pcall(body, name=name, out_shape=[_sds((N_DEV, D)), _sds((DEPTH, N_DEV, n))], grid=(DEPTH,),
                  in_specs=[pl.BlockSpec((N_DEV, D), lambda l: (0, 0)), pl.BlockSpec((1, D, n), lambda l: (l, 0, 0))],
                  out_specs=[pl.BlockSpec((N_DEV, D), lambda l: (0, 0)), pl.BlockSpec((1, N_DEV, n), lambda l: (l, 0, 0))],
                  sem=("arbitrary",))(c_all, ada_w)


def _add_rows(a, b, name):
    def body(a_ref, b_ref, o_ref):
        o_ref[...] = a_ref[...] + b_ref[...]

    return _pcall(body, name=name, out_shape=_sds(a.shape))(a, b)


def _ada_w_grad(cond, dmod_cols, name):
    n = dmod_cols.shape[2]

    def body(c_ref, d_ref, o_ref):
        o_ref[0] = _tn(c_ref[...], d_ref[0])

    return _pcall(body, name=name, out_shape=_sds((DEPTH, D, n)), grid=(DEPTH,),
                  in_specs=[pl.BlockSpec((N_DEV, D), lambda l: (0, 0)), pl.BlockSpec((1, N_DEV, n), lambda l: (l, 0, 0))],
                  out_specs=pl.BlockSpec((1, D, n), lambda l: (l, 0, 0)), sem=("parallel",))(cond, dmod_cols)


def _conv_fwd(pm, conv_w, conv_b, name):
    T = pm.shape[0]
    tm = _rows(T)
    C = CONV_DIM

    def body(x_ref, prev_ref, w_ref, b_ref, o_ref):
        cur = x_ref[...].astype(F32)
        prev = jnp.where(pl.program_id(0) > 0, prev_ref[...].astype(F32)[8:16], 0.0)
        cur8 = cur[0:8]
        row8 = lax.broadcasted_iota(jnp.int32, (8, C), 0)
        full = w_ref[3:4, :] * cur
        head = w_ref[3:4, :] * cur8
        for k in range(1, SSM_CONV):
            wk = w_ref[3 - k:4 - k, :]
            full = full + wk * pltpu.roll(cur, k, 0)
            head = head + wk * jnp.where(row8 < k, pltpu.roll(prev, k, 0), pltpu.roll(cur8, k, 0))
        o_ref[...] = full + b_ref[...]
        o_ref[0:8, :] = head + b_ref[...]

    return _pcall(
        body, name=name, out_shape=_sds((T, C)), grid=(T // tm,),
        in_specs=[pl.BlockSpec((tm, C), lambda i: (i, 2)),
                  pl.BlockSpec((16, C), lambda i: (jnp.maximum(i * (tm // 16) - 1, 0), 2)),
                  _vec_spec(C, SSM_CONV), _vec_spec(C)],
        out_specs=_row_spec(tm, C), sem=("parallel",))(pm, pm, conv_w, conv_b)


def _conv_bwd(dc, pm, conv_w, name):
    T = dc.shape[0]
    tm = _rows(T)
    C = CONV_DIM
    nt = T // tm

    def body(dc_ref, nxt_ref, x_ref, w_ref, dx_ref, acc_ref):
        i = pl.program_id(0)

        @pl.when(i == 0)
        def _():
            acc_ref[...] = jnp.zeros_like(acc_ref)
        dcv = dc_ref[...]
        nxt = jnp.where(i < nt - 1, nxt_ref[...], 0.0)
        xc = x_ref[...].astype(F32)
        dc8t, x8t = dcv[tm - 8:tm], xc[tm - 8:tm]
        row8 = lax.broadcasted_iota(jnp.int32, (8, C), 0)
        full = w_ref[3:4, :] * dcv
        tail = w_ref[3:4, :] * dc8t
        acc_ref[3:4, :] += _colsum(dcv * xc)
        for k in range(1, SSM_CONV):
            wk = w_ref[3 - k:4 - k, :]
            up = pltpu.roll(dcv, tm - k, 0)
            up_tail = jnp.where(row8 + k >= 8, pltpu.roll(nxt, 8 - k, 0), pltpu.roll(dc8t, 8 - k, 0))
            full = full + wk * up
            tail = tail + wk * up_tail
            prod = up * xc
            acc_ref[3 - k:4 - k, :] += _colsum(prod) - _colsum(prod[tm - 8:tm]) + _colsum(up_tail * x8t)
        acc_ref[4:5, :] += _colsum(dcv)
        dx_ref[...] = jnp.concatenate([full[0:tm - 8], tail], axis=0).astype(BF16)

    return _pcall(
        body, name=name, out_shape=[_sds((T, C), BF16), _sds((8, C))], grid=(nt,),
        in_specs=[_row_spec(tm, C),
                  pl.BlockSpec((8, C), lambda i: (jnp.minimum((i + 1) * (tm // 8), T // 8 - 1), 0)),
                  pl.BlockSpec((tm, C), lambda i: (i, 2)),
                  _vec_spec(C, SSM_CONV)],
        out_specs=[_row_spec(tm, C), _vec_spec(C, 8)], sem=("arbitrary",))(dc, dc, pm, conv_w)


def _ssd_prologue(cpre, dtr, dtb, alog):
    L = CHUNK
    xc = _silu(cpre)
    pre = dtr + dtb
    dt = jnp.maximum(pre, 0.0) + jnp.log1p(jnp.exp(-jnp.abs(pre)))
    a = -jnp.exp(alog)
    la = dt * a
    row = lax.broadcasted_iota(jnp.int32, (L, L), 0)
    col = lax.broadcasted_iota(jnp.int32, (L, L), 1)
    causal = row >= col
    tri = causal.astype(F32)
    lc = _nn(tri, la, HIGHEST)
    return xc, pre, dt, a, causal, tri, lc, row, col


def _head_indicator():
    m = np.zeros((LANES, SSM_INNER), np.float32)
    for h in range(SSM_HEADS):
        m[h, h * SSM_HEAD_DIM:(h + 1) * SSM_HEAD_DIM] = 1.0
    return jnp.asarray(m, dtype=BF16)


def _split_dot(x, ind, dims):
    hi = x.astype(BF16)
    lo = (x - hi.astype(F32)).astype(BF16)
    return _dot(hi, ind, dims) + _dot(lo, ind, dims)


def _expand(x16, ind):
    return _split_dot(x16, ind, ((1,), (0,)))


def _headsum(x, ind, single_pass=False):
    if single_pass:
        return _dot(x.astype(BF16), ind, ((1,), (1,)))
    return _split_dot(x, ind, ((1,), (1,)))


def _ssd_fwd(cpre, dtr, pm, dtb, alog, dskip, normw, ind, name, side=None):
    T = cpre.shape[0]
    nc = T // CHUNK
    L, P, H, HPG, N = CHUNK, SSM_HEAD_DIM, SSM_HEADS, SSM_HEADS // SSM_GROUPS, SSM_STATE
    half = SSM_INNER // SSM_GROUPS

    def body(cp_ref, dtr_ref, z_ref, dtb_ref, alog_ref, dskip_ref, nw_ref, ind_ref, ya_ref, y_ref, sp_ref, st_ref):
        @pl.when(pl.program_id(0) == 0)
        def _():
            st_ref[...] = jnp.zeros_like(st_ref)
        xc, _, dt, _, causal, _, lc, _, _ = _ssd_prologue(cp_ref[...], dtr_ref[...], dtb_ref[...], alog_ref[...])
        lct = lc.T
        ind = ind_ref[...]
        llast = lc[L - 1:L, :]
        xs = xc[:, :SSM_INNER]
        xd = xs * _expand(dt, ind)
        ex = _expand(jnp.exp(lc), ind)
        xd_end = _bf(xd * _expand(jnp.exp(llast - lc), ind))
        cdx = _expand(jnp.broadcast_to(jnp.exp(llast), (8, LANES)), ind)[0:1]
        xdb = _bf(xd)
        sp_ref[0] = st_ref[...]
        for g in range(SSM_GROUPS):
            sl = slice(g * half, (g + 1) * half)
            bm = _bf(xc[:, SSM_INNER + g * N:SSM_INNER + (g + 1) * N])
            cm = _bf(xc[:, SSM_INNER + (SSM_GROUPS + g) * N:SSM_INNER + (SSM_GROUPS + g + 1) * N])
            cb = _nt(cm, bm)
            st = st_ref[g]
            y_ref[:, sl] = ex[:, sl] * _nn(cm, _bf(st)) + dskip_ref[:, sl] * xs[:, sl]
            st_ref[g] = st * cdx[:, sl] + _tn(bm, xd_end[:, sl])
            for j in range(HPG):
                h = g * HPG + j
                decay = jnp.where(causal, jnp.exp(jnp.where(causal, lc[:, h:h + 1] - lct[h:h + 1, :], 0.0)), 0.0)
                y_ref[:, h * P:(h + 1) * P] += _nn(_bf(cb * decay), xdb[:, h * P:(h + 1) * P])
        y2 = y_ref[...] * _silu(z_ref[...].astype(F32))
        for g in range(SSM_GROUPS):
            yg = y2[:, g * half:(g + 1) * half]
            r = lax.rsqrt(jnp.mean(yg * yg, axis=-1, keepdims=True) + EPS)
            ya_ref[:, g * half:(g + 1) * half] = (yg * r * nw_ref[:, g * half:(g + 1) * half]).astype(BF16)

    return _call(
        body, (cpre, dtr, pm, dtb, alog, dskip, normw, ind), side, name=name,
        out_shape=[_sds((T, SSM_INNER), BF16), _sds((T, SSM_INNER)), _sds((nc, SSM_GROUPS, N, half))], grid=(nc,),
        in_specs=[_row_spec(L, CONV_DIM), _row_spec(L, LANES), _row_spec(L, SSM_INNER, 0),
                  _vec_spec(LANES), _vec_spec(LANES), _vec_spec(SSM_INNER), _vec_spec(SSM_INNER), _vec_spec(SSM_INNER, LANES)],
        out_specs=[_row_spec(L, SSM_INNER), _row_spec(L, SSM_INNER),
                   pl.BlockSpec((1, SSM_GROUPS, N, half), lambda i: (i, 0, 0, 0))],
        scratch=[pltpu.VMEM((SSM_GROUPS, N, half), F32)], sem=("arbitrary",))


def _ssd_bwd(cpre, dtr, pm, ypre, sprev, dya, dtb, alog, dskip, normw, ind, name, side=None):
    T = cpre.shape[0]
    nc = T // CHUNK
    L, P, H, HPG, N = CHUNK, SSM_HEAD_DIM, SSM_HEADS, SSM_HEADS // SSM_GROUPS, SSM_STATE
    half = SSM_INNER // SSM_GROUPS

    def body(cp_ref, dtr_ref, z_ref, y_ref, sp_ref, dya_ref, dtb_ref, alog_ref, dskip_ref, nw_ref, ind_ref,
             dz_ref, dcp_ref, ddtr_ref, acc_ref, dnw_ref, ds_ref, dy_ref, dxd_ref, rr_ref, yoff_ref, dcd_ref):
        @pl.when(pl.program_id(0) == 0)
        def _():
            ds_ref[...] = jnp.zeros_like(ds_ref)
            acc_ref[...] = jnp.zeros_like(acc_ref)
            dnw_ref[...] = jnp.zeros_like(dnw_ref)
        cpre_v = cp_ref[...]
        xc, pre, dt, a, causal, tri, lc, row, col = _ssd_prologue(cpre_v, dtr_ref[...], dtb_ref[...], alog_ref[...])
        lct = lc.T
        zv, yv = z_ref[...].astype(F32), y_ref[...]
        sz, dsz = _silu_and_grad(zv)
        y2 = yv * sz
        dya_v = dya_ref[...]
        nwv = nw_ref[...]
        for g in range(SSM_GROUPS):
            sl = slice(g * half, (g + 1) * half)
            yg = y2[:, sl]
            r = lax.rsqrt(jnp.mean(yg * yg, axis=-1, keepdims=True) + EPS)
            nrm = yg * r
            dnw_ref[:, sl] += _colsum(dya_v[:, sl] * nrm)
            dn = dya_v[:, sl] * nwv[:, sl]
            dy2 = r * (dn - nrm * jnp.mean(dn * nrm, axis=-1, keepdims=True))
            dy_ref[:, sl] = dy2 * sz[:, sl]
            dz_ref[:, sl] = (dy2 * yv[:, sl] * dsz[:, sl]).astype(BF16)
        ind = ind_ref[...]
        llast = lc[L - 1:L, :]
        dte16 = jnp.exp(llast - lc)
        cd16 = jnp.exp(llast)
        xs = xc[:, :SSM_INNER]
        dtx = _expand(dt, ind)
        ex = _expand(jnp.exp(lc), ind)
        dtex = _expand(dte16, ind)
        cdx = _expand(jnp.broadcast_to(cd16, (8, LANES)), ind)[0:1]
        xd = xs * dtx
        xdb = _bf(xd)
        xd_end = _bf(xd * dtex)
        dyv = dy_ref[...]
        dy_off = _bf(ex * dyv)
        dyb = _bf(dyv)
        dskx = dskip_ref[...]
        lane_c = lax.broadcasted_iota(jnp.int32, (L, LANES), 1)
        lane1 = lax.broadcasted_iota(jnp.int32, (1, LANES), 1)
        sub16 = lax.broadcasted_iota(jnp.int32, (H, L), 0)
        dlc_c = jnp.zeros((L, LANES), F32)
        dlc_r = jnp.zeros((H, L), F32)
        for g in range(SSM_GROUPS):
            sl = slice(g * half, (g + 1) * half)
            b_lo = SSM_INNER + g * N
            c_lo = SSM_INNER + (SSM_GROUPS + g) * N
            bm, cm = _bf(xc[:, b_lo:b_lo + N]), _bf(xc[:, c_lo:c_lo + N])
            cb = _nt(cm, bm)
            st, dst = sp_ref[0, g], ds_ref[g]
            stb, dstb = _bf(st), _bf(dst)
            dcm = _nt(dy_off[:, sl], stb)
            ds_ref[g] = _tn(cm, dy_off[:, sl]) + dst * cdx[:, sl]
            rr_ref[:, sl] = _nn(bm, dstb)
            yoff_ref[:, sl] = ex[:, sl] * _nn(cm, stb)
            db = _nt(xd_end[:, sl], dstb)
            dcd_ref[:, sl] = _colsum(dst * st)
            dcb = jnp.zeros((L, L), F32)
            for j in range(HPG):
                h = g * HPG + j
                hs = slice(h * P, (h + 1) * P)
                decay = jnp.where(causal, jnp.exp(jnp.where(causal, lc[:, h:h + 1] - lct[h:h + 1, :], 0.0)), 0.0)
                m = cb * decay
                dxd_ref[:, hs] = _tn(_bf(m), dyb[:, hs])
                dm = _nt(dyb[:, hs], xdb[:, hs])
                dcb = dcb + dm * decay
                gm = dm * m
                dlc_c = dlc_c + jnp.where(lane_c == h, _rowsum(gm), 0.0)
                dlc_r = dlc_r + jnp.where(sub16 == h, _colsum(gm), 0.0)
            dcbb = _bf(dcb)
            dcp_ref[:, c_lo:c_lo + N] = dcm + _nn(dcbb, bm)
            dcp_ref[:, b_lo:b_lo + N] = db + _tn(dcbb, cm)
        dxd_diag, rr = dxd_ref[...], rr_ref[...]
        tt = _headsum(rr * xd, ind, single_pass=True) * dte16
        dlc_rt = jnp.concatenate([dlc_r, jnp.zeros((LANES - H, L), F32)], axis=0).T
        dlc = dlc_c - dlc_rt + _headsum(dyv * yoff_ref[...], ind, single_pass=True) - tt
        dcd = _headsum(jnp.broadcast_to(dcd_ref[...], (8, SSM_INNER)), ind)[0:1]
        dlc = dlc + jnp.where(row == L - 1, _colsum(tt) + dcd * cd16, 0.0)
        dla = _tn(tri, dlc, HIGHEST)
        dxd = dxd_diag + dtex * rr
        ddt = _headsum(dxd * xs, ind, single_pass=True) + dla * a
        ddtr = jnp.where(lane_c < H, ddt * _sigmoid(pre), 0.0)
        ddtr_ref[...] = ddtr
        acc_ref[0:1, :] += _colsum(ddtr)
        acc_ref[1:2, :] += jnp.where(lane1 < H, _colsum(dla * dt) * a, 0.0)
        acc_ref[2:3, :] += _headsum(jnp.broadcast_to(_colsum(dyv * xs), (8, SSM_INNER)), ind)[0:1]
        dcp_ref[:, 0:SSM_INNER] = dxd * dtx + dskx * dyv
        dcp_ref[...] = dcp_ref[...] * _dsilu(cpre_v)

    rev = lambda i: (nc - 1 - i, 0)
    rspec = lambda c: pl.BlockSpec((L, c), rev)
    return _call(
        body, (cpre, dtr, pm, ypre, sprev, dya, dtb, alog, dskip, normw, ind), side, name=name,
        out_shape=[_sds((T, SSM_INNER), BF16), _sds((T, CONV_DIM)), _sds((T, LANES)), _sds((8, LANES)), _sds((1, SSM_INNER))],
        grid=(nc,),
        in_specs=[rspec(CONV_DIM), rspec(LANES), rspec(SSM_INNER), rspec(SSM_INNER),
                  pl.BlockSpec((1, SSM_GROUPS, N, half), lambda i: (nc - 1 - i, 0, 0, 0)), rspec(SSM_INNER),
                  _vec_spec(LANES), _vec_spec(LANES), _vec_spec(SSM_INNER), _vec_spec(SSM_INNER), _vec_spec(SSM_INNER, LANES)],
        out_specs=[rspec(SSM_INNER), rspec(CONV_DIM), rspec(LANES), _vec_spec(LANES, 8), _vec_spec(SSM_INNER)],
        scratch=[pltpu.VMEM((SSM_GROUPS, N, half), F32), pltpu.VMEM((L, SSM_INNER), F32), pltpu.VMEM((L, SSM_INNER), F32),
                 pltpu.VMEM((L, SSM_INNER), F32), pltpu.VMEM((L, SSM_INNER), F32), pltpu.VMEM((1, SSM_INNER), F32)],
        sem=("arbitrary",))


def _gmlp_common(u, v, lnw, lnb, with_grads=False):
    (ug, dug), (vg, dvg) = (_gelu_and_grad(u), _gelu_and_grad(v)) if with_grads else ((_gelu(u), None), (_gelu(v), None))
    mu = jnp.mean(vg, axis=-1, keepdims=True)
    cen = vg - mu
    rstd = lax.rsqrt(jnp.mean(cen * cen, axis=-1, keepdims=True) + EPS)
    vhat = cen * rstd
    out = (ug, rstd, vhat, vhat * lnw + lnb)
    return out + (dug, dvg) if with_grads else out


def _causal_mask():
    row = lax.broadcasted_iota(jnp.int32, (CHUNK, CHUNK), 0)
    col = lax.broadcasted_iota(jnp.int32, (CHUNK, CHUNK), 1)
    return row >= col


def _gmlp_fwd(pm, lnw, lnb, ws, bs_exp, name, side=None):
    T = pm.shape[0]
    nc = T // CHUNK
    L, G = CHUNK, GMLP_GROUPS

    def body(u_ref, v_ref, lnw_ref, lnb_ref, ws_ref, bs_ref, o_ref):
        ug, _, _, vn = _gmlp_common(u_ref[...].astype(F32), v_ref[...].astype(F32), lnw_ref[...], lnb_ref[...])
        causal = _causal_mask()
        for g in range(G):
            sl = slice(g * L, (g + 1) * L)
            wm = _bf(jnp.where(causal, ws_ref[g], 0.0))
            sv = _nn(wm, _bf(vn[:, sl])) + bs_ref[:, sl]
            o_ref[:, sl] = (ug[:, sl] * sv).astype(BF16)

    return _call(
        body, (pm, pm, lnw, lnb, ws, bs_exp), side, name=name, out_shape=[_sds((T, GMLP_INNER), BF16)], grid=(nc,),
        in_specs=[_row_spec(L, GMLP_INNER, 1), _row_spec(L, GMLP_INNER, 2), _vec_spec(GMLP_INNER), _vec_spec(GMLP_INNER),
                  pl.BlockSpec((G, L, L), lambda i: (0, 0, 0)), _vec_spec(GMLP_INNER, L)],
        out_specs=[_row_spec(L, GMLP_INNER)], sem=("parallel",))


def _gmlp_bwd(pm, dyb, lnw, lnb, ws, bs_exp, name, side=None):
    T = pm.shape[0]
    nc = T // CHUNK
    L, G = CHUNK, GMLP_GROUPS

    def body(u_ref, v_ref, dy_ref, lnw_ref, lnb_ref, ws_ref, bs_ref, du_ref, dv_ref, dws_ref, dbs_ref, acc_ref, dvn_ref):
        @pl.when(pl.program_id(0) == 0)
        def _():
            dws_ref[...] = jnp.zeros_like(dws_ref)
            dbs_ref[...] = jnp.zeros_like(dbs_ref)
            acc_ref[...] = jnp.zeros_like(acc_ref)
        uv, vv, dyv, lnwv = u_ref[...].astype(F32), v_ref[...].astype(F32), dy_ref[...], lnw_ref[...]
        ug, rstd, vhat, vn, dug, dvg_act = _gmlp_common(uv, vv, lnwv, lnb_ref[...], with_grads=True)
        causal = _causal_mask()
        lane = lax.broadcasted_iota(jnp.int32, (L, LANES), 1)
        dbs = jnp.zeros((L, LANES), F32)
        for g in range(G):
            sl = slice(g * L, (g + 1) * L)
            wm = _bf(jnp.where(causal, ws_ref[g], 0.0))
            vng = _bf(vn[:, sl])
            sv = _nn(wm, vng) + bs_ref[:, sl]
            du_ref[:, sl] = (dyv[:, sl] * sv * dug[:, sl]).astype(BF16)
            dsv = dyv[:, sl] * ug[:, sl]
            dsvb = _bf(dsv)
            dws_ref[g] += jnp.where(causal, _nt(dsvb, vng), 0.0)
            dbs = dbs + jnp.where(lane == g, _rowsum(dsv), 0.0)
            dvn_ref[:, sl] = _tn(wm, dsvb)
        dbs_ref[...] += dbs
        dvn = dvn_ref[...]
        acc_ref[0:1, :] += _colsum(dvn * vhat)
        acc_ref[1:2, :] += _colsum(dvn)
        dvh = dvn * lnwv
        dvg = rstd * (dvh - jnp.mean(dvh, axis=-1, keepdims=True) - vhat * jnp.mean(dvh * vhat, axis=-1, keepdims=True))
        dv_ref[...] = (dvg * dvg_act).astype(BF16)

    return _call(
        body, (pm, pm, dyb, lnw, lnb, ws, bs_exp), side, name=name,
        out_shape=[_sds((T, GMLP_INNER), BF16), _sds((T, GMLP_INNER), BF16), _sds((G, L, L)), _sds((L, LANES)), _sds((8, GMLP_INNER))],
        grid=(nc,),
        in_specs=[_row_spec(L, GMLP_INNER, 1), _row_spec(L, GMLP_INNER, 2), _row_spec(L, GMLP_INNER),
                  _vec_spec(GMLP_INNER), _vec_spec(GMLP_INNER), pl.BlockSpec((G, L, L), lambda i: (0, 0, 0)),
                  _vec_spec(GMLP_INNER, L)],
        out_specs=[_row_spec(L, GMLP_INNER), _row_spec(L, GMLP_INNER), pl.BlockSpec((G, L, L), lambda i: (0, 0, 0)),
                   _vec_spec(LANES, L), _vec_spec(GMLP_INNER, 8)],
        scratch=[pltpu.VMEM((L, GMLP_INNER), F32)], sem=("arbitrary",))


def _rel_buckets():
    qi = np.arange(CHUNK)[:, None]
    sj = np.arange(2 * CHUNK)[None, :]
    dist = np.maximum(qi + CHUNK - sj, 0)
    max_exact = REL_BUCKETS // 2
    log_ratio = (np.log(np.maximum(dist, 1).astype(np.float32) / np.float32(max_exact))
                 / np.float32(math.log(REL_MAX_DIST / max_exact))).astype(np.float32)
    large = max_exact + (log_ratio * np.float32(REL_BUCKETS - max_exact)).astype(np.int32)
    return np.where(dist < max_exact, dist, np.minimum(large, REL_BUCKETS - 1))


def _bucket_onehot_t():
    bucket = _rel_buckets().reshape(-1)
    return jnp.asarray((np.arange(REL_BUCKETS)[:, None] == bucket[None, :]).astype(np.float32))


def _bias_from_table(table_t, onehot_t, window, name):
    def body(t_ref, o_ref, w_ref, out_ref):
        out_ref[...] = jnp.where(w_ref[...] > 0.5, _nn(t_ref[...], o_ref[...], HIGHEST), NEG_INF)

    return _pcall(body, name=name, out_shape=_sds((ATTN_HEADS, onehot_t.shape[1])))(table_t, onehot_t, window)


def _table_from_dbias(dbias, onehot_t, name):
    def body(d_ref, o_ref, out_ref):
        out_ref[...] = _nt(o_ref[...], d_ref[...], HIGHEST)

    return _pcall(body, name=name, out_shape=_sds((REL_BUCKETS, ATTN_HEADS)))(dbias, onehot_t)


def _softmax_sink(logits, sink):
    mx = jnp.maximum(jnp.max(logits, axis=-1, keepdims=True), sink)
    e = jnp.exp(logits - mx)
    es = jnp.exp(sink - mx)
    inv = 1.0 / (_rowsum(e) + es)
    return e * inv, es * inv


def _first_block_penalty(n):
    sj = lax.broadcasted_iota(jnp.int32, (1, 2 * CHUNK), 1)
    return jnp.where((sj < CHUNK) & (n == 0), NEG_INF, 0.0)


def _window_mask_flat():
    qi = np.arange(CHUNK)[:, None]
    sj = np.arange(2 * CHUNK)[None, :]
    rel = qi + CHUNK - sj
    return jnp.asarray(((rel >= 0) & (rel < CHUNK)).astype(np.float32).reshape(1, -1))


def _stack_heads(ref, first, count, width, scale=None):
    x = jnp.concatenate([_bf(ref[:, (first + j) * width:(first + j + 1) * width]) for j in range(count)], axis=0)
    return x if scale is None else x * jnp.asarray(scale, x.dtype)


def _attn_fwd(qkv, bias, sinks, name):
    T = qkv.shape[0]
    nb = T // CHUNK
    L, DH, HPK = CHUNK, ATTN_DH, ATTN_HEADS // ATTN_KV
    scale = DH ** -0.5
    kcol, vcol = ATTN_HEADS * DH // LANES, ATTN_HEADS * DH // LANES + 1

    def body(q_ref, k_ref, v_ref, kp_ref, vp_ref, bias_ref, sink_ref, o_ref, lg_ref, p_ref):
        n = pl.program_id(0)
        pen = _first_block_penalty(n)
        kband = _bf(jnp.concatenate([kp_ref[...], k_ref[...]], axis=0))
        vband = _bf(jnp.concatenate([vp_ref[...], v_ref[...]], axis=0))
        for kv in range(ATTN_KV):
            lg_ref[...] = _nt(_stack_heads(q_ref, kv * HPK, HPK, DH, scale), kband[:, kv * DH:(kv + 1) * DH])
            for j in range(HPK):
                h = kv * HPK + j
                p, _ = _softmax_sink(lg_ref[j * L:(j + 1) * L, :] + bias_ref[h] + pen, sink_ref[h])
                p_ref[j * L:(j + 1) * L, :] = _bf(p)
            og = _nn(p_ref[...], vband[:, kv * DH:(kv + 1) * DH])
            for j in range(HPK):
                h = kv * HPK + j
                o_ref[:, h * DH:(h + 1) * DH] = og[j * L:(j + 1) * L].astype(BF16)

    prev = lambda i: jnp.maximum(i - 1, 0)
    return _pcall(
        body, name=name, out_shape=_sds((T, ATTN_HEADS * DH), BF16), grid=(nb,),
        in_specs=[_row_spec(L, ATTN_HEADS * DH, 0), _row_spec(L, LANES, kcol), _row_spec(L, LANES, vcol),
                  pl.BlockSpec((L, LANES), lambda i: (prev(i), kcol)), pl.BlockSpec((L, LANES), lambda i: (prev(i), vcol)),
                  pl.BlockSpec((ATTN_HEADS, L, 2 * L), lambda i: (0, 0, 0)),
                  pl.BlockSpec(memory_space=pltpu.SMEM)],
        out_specs=_row_spec(L, ATTN_HEADS * DH),
        scratch=[pltpu.VMEM((HPK * L, 2 * L), F32), pltpu.VMEM((HPK * L, 2 * L), BF16)],
        sem=("parallel",))(qkv, qkv, qkv, qkv, qkv, bias, sinks)


def _attn_bwd(qkv, datt, bias, sinks, name):
    T = qkv.shape[0]
    nb = T // CHUNK
    L, DH, HPK = CHUNK, ATTN_DH, ATTN_HEADS // ATTN_KV
    scale = DH ** -0.5
    kcol, vcol = ATTN_HEADS * DH // LANES, ATTN_HEADS * DH // LANES + 1

    def body(q_ref, k_ref, v_ref, kp_ref, vp_ref, do_ref, bias_ref, sink_ref,
             dq_ref, dk_ref, dv_ref, bsum_ref, dbias_ref, dsink_ref, pend_k, pend_v, band_k, band_v, lg_ref, dp_ref, p_ref, dl_ref):
        n = pl.program_id(0)

        @pl.when(n == 0)
        def _():
            dbias_ref[...] = jnp.zeros_like(dbias_ref)
            dsink_ref[...] = jnp.zeros_like(dsink_ref)
            bsum_ref[...] = jnp.zeros_like(bsum_ref)

        def emit_kv(dk, dv):
            dk_ref[...] = dk.astype(BF16)
            dv_ref[...] = dv.astype(BF16)
            bsum_ref[:, ATTN_HEADS * DH:ATTN_HEADS * DH + LANES] += _colsum(dk)
            bsum_ref[:, ATTN_HEADS * DH + LANES:] += _colsum(dv)

        @pl.when(n < nb)
        def _():
            pen = _first_block_penalty(n)
            kband = _bf(jnp.concatenate([kp_ref[...], k_ref[...]], axis=0))
            vband = _bf(jnp.concatenate([vp_ref[...], v_ref[...]], axis=0))
            lane1 = lax.broadcasted_iota(jnp.int32, (1, LANES), 1)
            dsink = jnp.zeros((1, LANES), F32)
            for kv in range(ATTN_KV):
                kb, vb = kband[:, kv * DH:(kv + 1) * DH], vband[:, kv * DH:(kv + 1) * DH]
                qg = _stack_heads(q_ref, kv * HPK, HPK, DH, scale)
                dog = _stack_heads(do_ref, kv * HPK, HPK, DH)
                lg_ref[...] = _nt(qg, kb)
                dp_ref[...] = _nt(dog, vb)
                for j in range(HPK):
                    h = kv * HPK + j
                    rows = slice(j * L, (j + 1) * L)
                    p, ps = _softmax_sink(lg_ref[rows, :] + bias_ref[h] + pen, sink_ref[h])
                    dp = dp_ref[rows, :]
                    delta = _rowsum(p * dp)
                    dl = p * (dp - delta)
                    dbias_ref[h] += dl
                    p_ref[rows, :] = _bf(p)
                    dl_ref[rows, :] = _bf(dl)
                    dsink = dsink + jnp.where(lane1 == h, -_colsum(ps * delta), 0.0)
                band_v[:, kv * DH:(kv + 1) * DH] = _tn(p_ref[...], dog)
                dqg = _nn(dl_ref[...], kb) * scale
                band_k[:, kv * DH:(kv + 1) * DH] = _tn(dl_ref[...], qg)
                for j in range(HPK):
                    h = kv * HPK + j
                    dq_ref[:, h * DH:(h + 1) * DH] = dqg[j * L:(j + 1) * L].astype(BF16)
                    bsum_ref[:, h * DH:(h + 1) * DH] += _colsum(dqg[j * L:(j + 1) * L])
            dsink_ref[...] += dsink

            @pl.when(n > 0)
            def _():
                emit_kv(pend_k[...] + band_k[0:L, :], pend_v[...] + band_v[0:L, :])
            pend_k[...] = band_k[L:2 * L, :]
            pend_v[...] = band_v[L:2 * L, :]

        @pl.when(n == nb)
        def _():
            emit_kv(pend_k[...], pend_v[...])

    cur = lambda i: jnp.minimum(i, nb - 1)
    prev = lambda i: jnp.maximum(jnp.minimum(i, nb - 1) - 1, 0)
    lag = lambda i: jnp.maximum(i - 1, 0)
    return _pcall(
        body, name=name,
        out_shape=[_sds((T, ATTN_HEADS * DH), BF16), _sds((T, LANES), BF16), _sds((T, LANES), BF16), _sds((1, QKV_DIM)),
                   _sds((ATTN_HEADS, L, 2 * L)), _sds((1, LANES))],
        grid=(nb + 1,),
        in_specs=[pl.BlockSpec((L, ATTN_HEADS * DH), lambda i: (cur(i), 0)),
                  pl.BlockSpec((L, LANES), lambda i: (cur(i), kcol)), pl.BlockSpec((L, LANES), lambda i: (cur(i), vcol)),
                  pl.BlockSpec((L, LANES), lambda i: (prev(i), kcol)), pl.BlockSpec((L, LANES), lambda i: (prev(i), vcol)),
                  pl.BlockSpec((L, ATTN_HEADS * DH), lambda i: (cur(i), 0)),
                  pl.BlockSpec((ATTN_HEADS, L, 2 * L), lambda i: (0, 0, 0)),
                  pl.BlockSpec(memory_space=pltpu.SMEM)],
        out_specs=[pl.BlockSpec((L, ATTN_HEADS * DH), lambda i: (cur(i), 0)),
                   pl.BlockSpec((L, LANES), lambda i: (lag(i), 0)), pl.BlockSpec((L, LANES), lambda i: (lag(i), 0)),
                   _vec_spec(QKV_DIM), pl.BlockSpec((ATTN_HEADS, L, 2 * L), lambda i: (0, 0, 0)), _vec_spec(LANES)],
        scratch=[pltpu.VMEM((L, LANES), F32), pltpu.VMEM((L, LANES), F32),
                 pltpu.VMEM((2 * L, LANES), F32), pltpu.VMEM((2 * L, LANES), F32),
                 pltpu.VMEM((HPK * L, 2 * L), F32), pltpu.VMEM((HPK * L, 2 * L), F32),
                 pltpu.VMEM((HPK * L, 2 * L), BF16), pltpu.VMEM((HPK * L, 2 * L), BF16)],
        sem=("arbitrary",))(qkv, qkv, qkv, qkv, qkv, datt, bias, sinks)


def _pad_rows(a, mult):
    pad = (-a.shape[-2]) % mult
    if pad == 0:
        return a
    cfg = [(0, 0)] * (a.ndim - 2) + [(0, pad), (0, 0)]
    return jnp.pad(a, cfg)


class _Pack:
    def __init__(self, width, mult, total_mult):
        self.width, self.mult, self.total_mult = width, mult, total_mult
        self.entries = []
        self.rows = 0

    def add(self, key, shape):
        n = int(np.prod(shape))
        rows = -(-n // self.width)
        self.entries.append((key, self.rows, rows, tuple(shape)))
        self.rows += -(-rows // self.mult) * self.mult

    @property
    def total(self):
        return -(-self.rows // self.total_mult) * self.total_mult

    def pack(self, pieces, dtype, lead=()):
        parts = []
        for key, _, rows, shape in self.entries:
            a = pieces[key].astype(dtype).reshape(lead + (-1,))
            n = int(np.prod(shape))
            a = jnp.pad(a, [(0, 0)] * len(lead) + [(0, rows * self.width - n)])
            a = a.reshape(lead + (rows, self.width))
            parts.append(_pad_rows(a, self.mult))
        out = jnp.concatenate(parts, axis=len(lead))
        return _pad_rows(out, self.total_mult)

    def unpack(self, packed, lead=()):
        out = {}
        for key, off, rows, shape in self.entries:
            a = lax.slice_in_dim(packed, off, off + rows, axis=len(lead))
            a = a.reshape(lead + (-1,))
            n = int(np.prod(shape))
            out[key] = lax.slice_in_dim(a, 0, n, axis=len(lead)).reshape(lead + shape)
        return out


def _ffn_fwd(x, h, mod, wg_t, wu_t, wd, tag, next_norm=None, gather=None):
    side = None if gather is None else (_GatherOps, gather)
    gate, up, act, *gathered = _mm_swiglu(h, wg_t, wu_t, f"ffn_gateup_{tag}", side=side)
    x_out, ffn_out, *h_next = _mm_resid([(act, wd)], x, mod[5:6], name=f"ffn_down_{tag}", norm=next_norm)
    return (x_out, dict(h=h, gate=gate, up=up, act=act, out=ffn_out), *h_next, *gathered)


def _ffn_bwd(dx_out, dffn, x_in, saved, mod, norm_w, wg_t, wu_t, wd, below, tag, exchange=None):
    side = None if exchange is None else (_ChipsOps, exchange)
    dgate, dup, *from_chips = _mm_swiglu_bwd(dffn, wd, saved["gate"], saved["up"], f"ffn_act_bwd_{tag}", side=side)
    d_wd = _mm_tn(saved["act"], dffn, name=f"ffn_dwd_{tag}")
    d_wg_t = _mm_tn(dgate, saved["h"], name=f"ffn_dwg_{tag}")
    d_wu_t = _mm_tn(dup, saved["h"], name=f"ffn_dwu_{tag}")
    dh = _mm([(dgate, wg_t), (dup, wu_t)], "nn", name=f"ffn_dh_{tag}")
    dx, d_below, acc = _norm_mod_bwd(x_in, dh, dx_out, norm_w, mod[4:5], below[0], below[1], f"ffn_norm_bwd_{tag}")
    return (dx, d_below, dict(d_wg=d_wg_t, d_wu=d_wu_t, d_wd=d_wd, acc=acc), *from_chips)


_BIG = [
    ("out_w", "out_w_even", 0, "row"), ("qkv_w", "qkv_w", 0, "col"), ("o_w", "o_w", 0, "row"),
    ("gate0", "ffn_gate_w", 0, "col"), ("up0", "ffn_up_w", 0, "col"), ("down0", "ffn_down_w", 0, "row"),
    ("gate1", "ffn_gate_w", 1, "col"), ("up1", "ffn_up_w", 1, "col"), ("down1", "ffn_down_w", 1, "row"),
    ("in_w", "in_w_even", 0, "col"),
]


def _to_wire(a, kind):
    return a.T if kind == "col" else a


_GATHER_GROUPS = [["in_w"], ["out_w"], ["gate0", "up0", "down0"], ["qkv_w", "o_w"], ["gate1", "up1", "down1"]]
_GRAD_GROUPS = [["qkv_w", "o_w", "gate1", "up1", "down1"], ["out_w", "gate0", "up0", "down0"], ["in_w"]]

_REPLICATED = ["ada_b", "norm_mix_w", "norm_ffn_w", "conv_b", "dt_bias", "a_log", "d_skip", "ssm_norm_w", "gmlp_ln_w",
               "gmlp_ln_b", "gmlp_ws", "gmlp_bs", "sinks", "rel_table", "final_norm_w"]
_TINY_SHARDED = ["conv_w", "qkv_b", "o_b"]

_WEIGHTS = ['ada_w', 'ada_b', 'norm_mix_w', 'norm_ffn_w', 'in_w_even', 'conv_w', 'conv_b', 'dt_bias', 'a_log', 'd_skip',
            'ssm_norm_w', 'gmlp_ln_w', 'gmlp_ln_b', 'gmlp_ws', 'gmlp_bs', 'out_w_even', 'qkv_w', 'qkv_b', 'o_w', 'o_b',
            'sinks', 'rel_table', 'ffn_gate_w', 'ffn_up_w', 'ffn_down_w', 'final_norm_w']


def _step(x, c, loss_target, W, M, V):
    T = x.shape[1]
    x0 = x[0]
    target = loss_target[0]
    me = 4 * lax.axis_index("x") + 2 * lax.axis_index("y") + lax.axis_index("c")

    w_wire_local = {key: _to_wire(W[name][layer].astype(BF16), kind) for key, name, layer, kind in _BIG}

    def wire_pack(keys, mult):
        gp = _Pack(D, 1, mult)
        for key in keys:
            gp.add(key, w_wire_local[key].shape)
        return gp

    gather_packs = [(gp, gp.pack(w_wire_local, BF16)) for gp in (wire_pack(keys, 16) for keys in _GATHER_GROUPS)]
    grad_packs = [wire_pack(keys, 64) for keys in _GRAD_GROUPS]
    full = {}

    def gathered_weights(group, gathered):
        shards = gather_packs[group][0].unpack(gathered, lead=(N_DEV,))
        full.update({key: a.reshape(-1, D) for key, a in shards.items()})


    small_in = _Pack(D, 8, 8)
    small_in.add("c", (1, D))
    small_in.add("conv_w", W["conv_w"][0].shape)
    small_in.add("qkv_b", W["qkv_b"][0].shape)
    small_in.add("o_b", W["o_b"][0].shape)
    sm = small_in.unpack(_all_gather(small_in.pack(
        dict(c=c, conv_w=W["conv_w"][0], qkv_b=W["qkv_b"][0], o_b=W["o_b"][0]), F32), "gather_small"), lead=(N_DEV,))
    c_all = sm["c"].reshape(N_DEV, D)
    conv_w_full = jnp.transpose(sm["conv_w"], (1, 0, 2)).reshape(SSM_CONV, CONV_DIM)
    qkv_b_full = sm["qkv_b"].reshape(1, QKV_DIM)
    o_b_full = sm["o_b"].reshape(1, D)

    ncol = W["ada_w"].shape[2]
    cond, mod_cols = _mod_matmul(c_all, W["ada_w"], "mod_matmul")
    mod_g = _all_gather(mod_cols.reshape(DEPTH * N_DEV, ncol), "gather_mod").reshape(N_DEV, DEPTH, N_DEV, ncol)
    mod_me = lax.dynamic_index_in_dim(mod_g, me, axis=2, keepdims=False)
    mod_me = jnp.transpose(mod_me, (1, 0, 2)).reshape(DEPTH, 6, D)
    mod_me = jnp.pad(mod_me, ((0, 0), (0, 2), (0, 0))).reshape(DEPTH * 8, D)
    ada_b_rows = jnp.pad(W["ada_b"].reshape(DEPTH, 6, D), ((0, 0), (0, 2), (0, 0))).reshape(DEPTH * 8, D)
    mod_all = _add_rows(mod_me, ada_b_rows, "mod_bias").reshape(DEPTH, 8, D)
    mod0, mod1 = mod_all[0], mod_all[1]
    h0, gathered_in = _norm_mod(x0, W["norm_mix_w"][0:1], mod0[1:2], mod0[0:1], "mix_norm_0",
                                side=(_GatherOps, gather_packs[0][1]))
    gathered_weights(0, gathered_in)

    in_t = full["in_w"]
    o1, o2, o3, o4 = SSM_INNER, SSM_INNER + CONV_DIM, SSM_INNER + CONV_DIM + SSM_HEADS, SSM_INNER + CONV_DIM + SSM_HEADS + GMLP_INNER
    w_z, w_xbc, w_dt, w_u, w_v = in_t[:o1], in_t[o1:o2], in_t[o2:o3], in_t[o3:o4], in_t[o4:]
    w_main = jnp.concatenate([w_z, w_u, w_v, w_xbc], axis=0)
    w_dtp = jnp.pad(w_dt, ((0, LANES - SSM_HEADS), (0, 0)))

    pad16 = lambda a: jnp.pad(a.reshape(1, SSM_HEADS), ((0, 0), (0, LANES - SSM_HEADS)))
    dtb, alog = pad16(W["dt_bias"][0]), pad16(W["a_log"][0])
    dskip = jnp.repeat(W["d_skip"][0], SSM_HEAD_DIM).reshape(1, SSM_INNER)
    ssm_nw = W["ssm_norm_w"]
    lnw, lnb = W["gmlp_ln_w"], W["gmlp_ln_b"]
    ws = W["gmlp_ws"][0]
    bs_exp = jnp.repeat(W["gmlp_bs"][0].T, CHUNK, axis=1)
    conv_b = W["conv_b"]
    nmw, nfw = W["norm_mix_w"], W["norm_ffn_w"]
    onehot_t = _bucket_onehot_t()
    head_ind = _head_indicator()
    bias = _bias_from_table(W["rel_table"].T, onehot_t, _window_mask_flat(), "rel_bias").reshape(ATTN_HEADS, CHUNK, 2 * CHUNK)
    sinks = W["sinks"][0]

    pm, gathered_a = _mm([(h0, w_main)], "nt", name="in_proj", tn_pref=1536, out_dtype=BF16,
                         side=(_GatherOps, gather_packs[1][1]))
    gathered_weights(1, gathered_a)
    out_w = full["out_w"]
    dtr = _mm([(h0, w_dtp)], "nt", name="in_proj_dt")
    cpre = _conv_fwd(pm, conv_w_full, conv_b, "conv_fwd")
    ya, ypre, sprev, gathered_b = _ssd_fwd(cpre, dtr, pm, dtb, alog, dskip, ssm_nw, head_ind, "ssd_fwd",
                                           side=(_GatherOps, gather_packs[2][1]))
    gathered_weights(2, gathered_b)
    yb, gathered_c = _gmlp_fwd(pm, lnw, lnb, ws, bs_exp, "gmlp_fwd", side=(_GatherOps, gather_packs[3][1]))
    gathered_weights(3, gathered_c)
    x1, mix0, hf0 = _mm_resid([(ya, out_w[:SSM_INNER]), (yb, out_w[SSM_INNER:])], x0, mod0[2:3], name="out_proj",
                              norm=(nfw[0:1], mod0[4:5], mod0[3:4]))
    x2, ffn0, h1, gathered_d = _ffn_fwd(x1, hf0, mod0, full["gate0"], full["up0"], full["down0"], "0",
                                        next_norm=(nmw[1:2], mod1[1:2], mod1[0:1]), gather=gather_packs[4][1])
    gathered_weights(4, gathered_d)
    qkv_t, o_w = full["qkv_w"], full["o_w"]
    w_q, w_k, w_v_att = qkv_t[:D], qkv_t[D:D + LANES], qkv_t[D + LANES:]

    qkv = _mm([(h1, qkv_t)], "nt", name="qkv_proj", bias=qkv_b_full, tn_pref=1280, out_dtype=BF16)
    att = _attn_fwd(qkv, bias, sinks, "attn_fwd")
    x3, mix1, hf1 = _mm_resid([(att, o_w)], x2, mod1[2:3], name="o_proj", bias=o_b_full,
                              norm=(nfw[1:2], mod1[4:5], mod1[3:4]))
    x4, ffn1 = _ffn_fwd(x3, hf1, mod1, full["gate1"], full["up1"], full["down1"], "1")

    dx4, dffn1, acc_f = _final_loss(x4, W["final_norm_w"].reshape(1, D), target, ffn1["out"], mod1[5:6], "final_loss")
    dx3, dmix1, gf1 = _ffn_bwd(dx4, dffn1, x3, ffn1, mod1, nfw[1:2], full["gate1"], full["up1"], full["down1"],
                               (mix1, mod1[2:3]), "1")
    datt = _mm([(dmix1, o_w)], "nt", name="o_proj_dx", out_dtype=BF16)
    d_o_w = _mm_tn(att, dmix1, name="o_proj_dw")
    dq, dk, dv, d_qkv_b, dbias, dsinks = _attn_bwd(qkv, datt, bias, sinks, "attn_bwd")
    d_table = _table_from_dbias(dbias.reshape(ATTN_HEADS, -1), onehot_t, "rel_table_grad")
    d_qkv_t = jnp.concatenate([_mm_tn(dq, h1, name="qkv_dw_q"), _mm_tn(dk, h1, name="qkv_dw_k"), _mm_tn(dv, h1, name="qkv_dw_v")], axis=0)
    g_wire = dict(qkv_w=d_qkv_t, o_w=d_o_w, gate1=gf1["d_wg"], up1=gf1["d_wu"], down1=gf1["d_wd"])

    def packed_partials(group):
        return grad_packs[group].pack({key: g_wire[key].reshape(N_DEV, -1, D) for key in _GRAD_GROUPS[group]},
                                      F32, lead=(N_DEV,))

    partials_l1 = packed_partials(0)
    dh1, theirs_l1 = _mm([(dq, w_q), (dk, w_k), (dv, w_v_att)], "nn", name="qkv_proj_dx", side=(_SiblingOps, partials_l1))
    pair_l1 = _pair_sum(partials_l1, theirs_l1, "grads_pair_sum_l1")
    dx2, dffn0, acc_n1 = _norm_mod_bwd(x2, dh1, dx3, nmw[1:2], mod1[1:2], ffn0["out"], mod0[5:6], "mix_norm_bwd_1")

    from_chips = {}
    dx1, dmix0, gf0, from_chips[0] = _ffn_bwd(dx2, dffn0, x1, ffn0, mod0, nfw[0:1], full["gate0"], full["up0"], full["down0"],
                                              (mix0, mod0[2:3]), "0", exchange=pair_l1)

    dya = _mm([(dmix0, out_w[:SSM_INNER])], "nt", name="out_proj_dx_a")
    dyb = _mm([(dmix0, out_w[SSM_INNER:])], "nt", name="out_proj_dx_b")
    d_out_w = jnp.concatenate([_mm_tn(ya, dmix0, name="out_proj_dw_a"), _mm_tn(yb, dmix0, name="out_proj_dw_b")], axis=0)
    g_wire.update(gate0=gf0["d_wg"], up0=gf0["d_wu"], down0=gf0["d_wd"], out_w=d_out_w)
    partials_ffn0 = packed_partials(1)
    du, dvg, d_ws, d_bs, acc_ln, theirs_ffn0 = _gmlp_bwd(pm, dyb, lnw, lnb, ws, bs_exp, "gmlp_bwd",
                                                         side=(_SiblingOps, partials_ffn0))
    pair_ffn0 = _pair_sum(partials_ffn0, theirs_ffn0, "grads_pair_sum_mix")
    dz, dcpre, ddtr, acc_ssd, d_ssm_nw, from_chips[1] = _ssd_bwd(
        cpre, dtr, pm, ypre, sprev, dya, dtb, alog, dskip, ssm_nw, head_ind, "ssd_bwd", side=(_ChipsOps, pair_ffn0))
    dxbc, acc_conv = _conv_bwd(dcpre, pm, conv_w_full, "conv_bwd")
    d_in_t = jnp.concatenate([
        _mm_tn(dz, h0, name="in_dw_z"), _mm_tn(dxbc, h0, name="in_dw_xbc"),
        _mm_tn(ddtr, h0, name="in_dw_dt")[:SSM_HEADS], _mm_tn(du, h0, name="in_dw_u"), _mm_tn(dvg, h0, name="in_dw_v")], axis=0)
    g_wire.update(in_w=d_in_t)
    partials_mix = packed_partials(2)
    theirs_mix = _comm_call(_SiblingOps, partials_mix, "exchange_grads_sibling")
    pair_mix = _pair_sum(partials_mix, theirs_mix, "grads_pair_sum")
    dh0, from_chips[2] = _mm([(dz, w_z), (dxbc, w_xbc), (ddtr, w_dtp), (du, w_u), (dvg, w_v)], "nn", name="in_proj_dx",
                             side=(_ChipsOps, pair_mix))
    grad_x, acc_n0 = _norm_mod_bwd(x0, dh0, dx1, nmw[0:1], mod0[1:2], None, None, "mix_norm_bwd_0")

    g_mine = {}
    for group in range(len(_GRAD_GROUPS)):
        g_mine.update(grad_packs[group].unpack(_sum_parts(from_chips[group], f"grads_chip_sum_{group}")))
    res_big = [{}, {}, {}, {}]
    for key, name, layer, kind in _BIG:
        g_nat = _to_wire(g_mine[key], kind)
        outs = _adamw(g_nat[None], W[name][layer], M[name][layer], V[name][layer], f"adamw_{key}")
        for res, out in zip(res_big, outs):
            res[key] = out

    acc_f0, acc_f1 = gf0["acc"], gf1["acc"]
    views = dict(ada_b=(DEPTH * 6, D), norm_mix_w=(DEPTH, D), norm_ffn_w=(DEPTH, D), conv_b=(1, CONV_DIM),
                 dt_bias=(1, SSM_HEADS), a_log=(1, SSM_HEADS), d_skip=(1, SSM_HEADS), ssm_norm_w=(1, D), gmlp_ln_w=(1, D),
                 gmlp_ln_b=(1, D), gmlp_ws=(GMLP_GROUPS * CHUNK, CHUNK), gmlp_bs=(GMLP_GROUPS, CHUNK),
                 sinks=(1, ATTN_HEADS), rel_table=(REL_BUCKETS, ATTN_HEADS), final_norm_w=(1, D),
                 conv_w=(SSM_CONV, CONV_DIM), qkv_b=(1, QKV_DIM), o_b=(1, D), loss=(1, D))
    canvas = _Canvas()
    for key, (r, cdim) in views.items():
        canvas.add(key, r, cdim, blocks=GMLP_GROUPS if key == "gmlp_ws" else 1)
    row = lambda i: slice(i, i + 1)
    sources = [
        (acc_n0, [("ada_b", row(1), 0), ("ada_b", row(0), 1), ("norm_mix_w", row(2), 0)]),
        (acc_f0, [("ada_b", row(3), 2), ("ada_b", row(1), 3), ("ada_b", row(0), 4), ("norm_ffn_w", row(2), 0)]),
        (acc_n1, [("ada_b", row(3), 5), ("ada_b", row(1), 6), ("ada_b", row(0), 7), ("norm_mix_w", row(2), 1)]),
        (acc_f1, [("ada_b", row(3), 8), ("ada_b", row(1), 9), ("ada_b", row(0), 10), ("norm_ffn_w", row(2), 1),
                  ("o_b", row(4), 0)]),
        (acc_f, [("ada_b", row(3), 11), ("final_norm_w", row(0), 0), ("loss", row(1), 0)]),
        (acc_conv, [("conv_w", slice(0, SSM_CONV), 0), ("conv_b", row(4), 0)]),
        (acc_ssd, [("dt_bias", row(0), 0), ("a_log", row(1), 0), ("d_skip", row(2), 0)]),
        (d_ssm_nw, [("ssm_norm_w", row(0), 0)]),
        (acc_ln, [("gmlp_ln_w", row(0), 0), ("gmlp_ln_b", row(1), 0)]),
        (d_ws.reshape(GMLP_GROUPS * CHUNK, CHUNK), [("gmlp_ws", slice(0, GMLP_GROUPS * CHUNK), 0)]),
        (d_bs.T, [("gmlp_bs", slice(0, GMLP_GROUPS), 0)]),
        (dsinks, [("sinks", row(0), 0)]),
        (d_table, [("rel_table", slice(0, REL_BUCKETS), 0)]),
        (d_qkv_b, [("qkv_b", row(0), 0)]),
    ]
    parts_small = _all_gather(_canvas_fill(canvas, sources, "small_grads_canvas"), "gather_small_grads")
    view = lambda a, key: a.reshape(views[key])
    params = [(name, view(W[name], name), view(M[name], name), view(V[name], name)) for name in _REPLICATED]
    small_out = _adamw_canvas(canvas, parts_small, params, _TINY_SHARDED + ["loss"], "adamw_small")
    loss = small_out["loss"][0][0, 0]
    res_small = [{name: small_out[name][k].reshape(W[name].shape) for name in _REPLICATED} for k in range(4)]

    n_cw, n_qb, n_ob = W["conv_w"].shape[2], W["qkv_b"].shape[1], W["o_b"].shape[1]
    g_tiny = dict(conv_w=lax.dynamic_slice_in_dim(small_out["conv_w"][0], me * n_cw, n_cw, axis=1)[None],
                  qkv_b=lax.dynamic_slice_in_dim(small_out["qkv_b"][0], me * n_qb, n_qb, axis=1),
                  o_b=lax.dynamic_slice_in_dim(small_out["o_b"][0], me * n_ob, n_ob, axis=1))
    tiny = _Pack(D, 8, 8)
    for name in _TINY_SHARDED:
        tiny.add(name, W[name].shape)
    pkt = lambda S: tiny.pack({name: S[name] for name in _TINY_SHARDED}, F32)
    res_tiny = [tiny.unpack(r) for r in _adamw(pkt(g_tiny)[None], pkt(W), pkt(M), pkt(V), "adamw_tiny")]

    dmod_all = parts_small[:, canvas.offset["ada_b"]:canvas.offset["ada_b"] + DEPTH * 6].reshape(N_DEV, DEPTH, 6 * D)
    dmod_cols = jnp.transpose(lax.dynamic_slice_in_dim(dmod_all, me * ncol, ncol, axis=2), (1, 0, 2))
    g_ada_w = _ada_w_grad(cond, dmod_cols, "ada_w_grad")
    flat = lambda a: a.reshape(DEPTH * D, ncol)
    res_ada = [r.reshape(DEPTH, D, ncol) for r in _adamw(flat(g_ada_w)[None], flat(W["ada_w"]), flat(M["ada_w"]), flat(V["ada_w"]), "adamw_ada_w")]

    def result(kind_idx, name):
        if name == "ada_w":
            return res_ada[kind_idx]
        if name in _REPLICATED:
            return res_small[kind_idx][name]
        if name in _TINY_SHARDED:
            return res_tiny[kind_idx][name]
        pieces = [res_big[kind_idx][key] for key, nm, layer, kind in _BIG if nm == name]
        return jnp.stack(pieces)

    outs = [loss, grad_x[None]]
    for kind_idx in range(4):
        outs += [result(kind_idx, name) for name in _WEIGHTS]
    return tuple(outs)


def kernel(x, c, ada_w, ada_b, norm_mix_w, norm_ffn_w, in_w_even, conv_w, conv_b, dt_bias, a_log, d_skip, ssm_norm_w, gmlp_ln_w, gmlp_ln_b, gmlp_ws, gmlp_bs, out_w_even, qkv_w, qkv_b, o_w, o_b, sinks, rel_table, ffn_gate_w, ffn_up_w, ffn_down_w, final_norm_w, loss_target, m_ada_w, m_ada_b, m_norm_mix_w, m_norm_ffn_w, m_in_w_even, m_conv_w, m_conv_b, m_dt_bias, m_a_log, m_d_skip, m_ssm_norm_w, m_gmlp_ln_w, m_gmlp_ln_b, m_gmlp_ws, m_gmlp_bs, m_out_w_even, m_qkv_w, m_qkv_b, m_o_w, m_o_b, m_sinks, m_rel_table, m_ffn_gate_w, m_ffn_up_w, m_ffn_down_w, m_final_norm_w, v_ada_w, v_ada_b, v_norm_mix_w, v_norm_ffn_w, v_in_w_even, v_conv_w, v_conv_b, v_dt_bias, v_a_log, v_d_skip, v_ssm_norm_w, v_gmlp_ln_w, v_gmlp_ln_b, v_gmlp_ws, v_gmlp_bs, v_out_w_even, v_qkv_w, v_qkv_b, v_o_w, v_o_b, v_sinks, v_rel_table, v_ffn_gate_w, v_ffn_up_w, v_ffn_down_w, v_final_norm_w):
    args = locals()
    W = {n: args[n] for n in _WEIGHTS}
    M = {n: args["m_" + n] for n in _WEIGHTS}
    V = {n: args["v_" + n] for n in _WEIGHTS}
    return _step(x, c, loss_target, W, M, V)
```

```python
import functools
import math

import numpy as np
import jax
import jax.numpy as jnp
from jax import lax
from jax.experimental import pallas as pl
from jax.experimental.pallas import tpu as pltpu

F32 = jnp.float32
BF16 = jnp.bfloat16
HIGHEST = lax.Precision.HIGHEST
MESH = pl.DeviceIdType.MESH

N_DEV = 8
D = 1024
DEPTH = 2
SSM_HEADS = 16
SSM_HEAD_DIM = 64
SSM_INNER = 1024
SSM_GROUPS = 2
SSM_STATE = 128
SSM_CONV = 4
CHUNK = 128
CONV_DIM = SSM_INNER + 2 * SSM_GROUPS * SSM_STATE
GMLP_GROUPS = 8
GMLP_INNER = 1024
IN_EVEN = 4624
ATTN_HEADS = 16
ATTN_KV = 2
ATTN_DH = 64
QKV_DIM = 1280
REL_BUCKETS = 32
REL_MAX_DIST = 128
FFN = 2816
EPS = 1e-6
NEG_INF = -1e30
LANES = 128

ADAM_LR = 0.001
ADAM_B1 = 0.9
ADAM_B2 = 0.999
ADAM_EPS = 1e-08
ADAM_WD = 0.01
ADAM_STEP = 10

VMEM_LIMIT_BYTES = 56 * 1024 * 1024
ROW_TILE = 512


def _pcall(body, *, name, out_shape, grid=(), in_specs=None, out_specs=None, scratch=(), sem=None):
    params = dict(vmem_limit_bytes=VMEM_LIMIT_BYTES)
    if sem is not None:
        params["dimension_semantics"] = sem
    specs = {} if in_specs is None else dict(in_specs=in_specs, out_specs=out_specs)
    return pl.pallas_call(
        body, name=name, out_shape=out_shape, grid=grid, **specs,
        scratch_shapes=list(scratch), compiler_params=pltpu.CompilerParams(**params))


def _call(body, args, side=None, *, name, out_shape, grid, in_specs, out_specs, scratch=(), sem=None):
    if side is None:
        return _pcall(body, name=name, out_shape=out_shape, grid=grid, in_specs=in_specs, out_specs=out_specs,
                      scratch=scratch, sem=sem)(*args)
    ops_cls, x = side
    n_in, n_out, n_scr = len(in_specs), len(out_shape), len(scratch)
    steps = int(np.prod(grid))
    hbm = pl.BlockSpec(memory_space=pl.ANY)

    def wrapped(*refs):
        ins, x_ref = refs[:n_in], refs[n_in]
        outs, r_ref = refs[n_in + 1:n_in + 1 + n_out], refs[n_in + 1 + n_out]
        scr, sems = refs[n_in + 2 + n_out:n_in + 2 + n_out + n_scr], refs[n_in + 2 + n_out + n_scr:]
        ops = ops_cls(x_ref, r_ref, *sems)
        step = pl.program_id(0)
        for axis in range(1, len(grid)):
            step = step * grid[axis] + pl.program_id(axis)
        pl.when(step == 0)(ops.start)
        body(*ins, *outs, *scr)
        pl.when(step == (3 * steps) // 4)(ops.forward)
        pl.when(step == steps - 1)(ops.finish)

    return _pcall(
        wrapped, name=name, out_shape=list(out_shape) + [ops_cls.result(x)], grid=grid,
        in_specs=list(in_specs) + [hbm], out_specs=list(out_specs) + [hbm],
        scratch=list(scratch) + ops_cls.scratch(), sem=("arbitrary",) * len(grid))(*args, x)


def _tile(n, pref):
    if n <= pref:
        return n
    best = None
    for t in range(LANES, pref + 1, LANES):
        if n % t == 0:
            best = t
    assert best is not None, (n, pref)
    return best


def _rows(T):
    return min(ROW_TILE, T)


def _sds(shape, dtype=F32):
    return jax.ShapeDtypeStruct(shape, dtype)


def _row_spec(tm, c, col=0):
    return pl.BlockSpec((tm, c), lambda i, col=col: (i, col))


def _vec_spec(c, r=1):
    return pl.BlockSpec((r, c), lambda i: (0, 0))


def _sigmoid(x):
    return jax.nn.sigmoid(x)


def _silu(x):
    return x * _sigmoid(x)


def _dsilu(x):
    s = _sigmoid(x)
    return s * (1.0 + x * (1.0 - s))


def _silu_and_grad(x):
    s = _sigmoid(x)
    return x * s, s * (1.0 + x * (1.0 - s))


def _gelu(x):
    return 0.5 * x * (1.0 + lax.erf(x * 0.7071067811865476))


def _gelu_and_grad(x):
    phi = 0.5 * (1.0 + lax.erf(x * 0.7071067811865476))
    return x * phi, phi + x * jnp.exp(-0.5 * x * x) * 0.3989422804014327


def _dot(a, b, dims, precision=None):
    return lax.dot_general(a, b, (dims, ((), ())), precision=precision, preferred_element_type=F32)


def _nn(a, b, precision=None):
    return _dot(a, b, ((1,), (0,)), precision)


def _nt(a, b, precision=None):
    return _dot(a, b, ((1,), (1,)), precision)


def _tn(a, b, precision=None):
    return _dot(a, b, ((0,), (0,)), precision)


def _bf(x):
    return x.astype(BF16)


def _colsum(x):
    return jnp.sum(x, axis=0, keepdims=True)


def _rowsum(x):
    return jnp.sum(x, axis=1, keepdims=True)


def _allsum(x):
    return _colsum(_rowsum(x))


def _comm_call(ops_cls, x, name, from_vmem=False):
    def body(x_ref, out_ref, *sems):
        ops = ops_cls(x_ref, out_ref, *sems)
        ops.start()
        ops.forward()
        ops.finish()

    return pl.pallas_call(
        body, name=name, out_shape=ops_cls.result(x),
        in_specs=[pl.BlockSpec(memory_space=pltpu.VMEM if from_vmem else pl.ANY)],
        out_specs=pl.BlockSpec(memory_space=pl.ANY), scratch_shapes=ops_cls.scratch(),
    )(x)


def _all_gather(x, name):
    return _comm_call(_GatherOps, x, name, from_vmem=True)


class _GatherOps:
    def __init__(self, x_ref, out_ref, send_sems, recv_sems, local_sem):
        self.x_ref, self.out_ref = x_ref, out_ref
        self.send_sems, self.recv_sems, self.local_sem = send_sems, recv_sems, local_sem
        mx, my, mc = lax.axis_index("x"), lax.axis_index("y"), lax.axis_index("c")
        self.mc = mc
        self.me, self.sibling = (mx, my, mc), (mx, my, 1 - mc)
        self.chips = [(1 - mx, my), (mx, 1 - my), (1 - mx, 1 - my)]

    @staticmethod
    def result(x):
        return _sds((N_DEV,) + x.shape, x.dtype)

    @staticmethod
    def scratch():
        return [pltpu.SemaphoreType.DMA((7,)), pltpu.SemaphoreType.DMA((7,)), pltpu.SemaphoreType.DMA(())]

    def _slot(self, px, py, pc):
        return self.out_ref.at[4 * px + 2 * py + pc]

    def _copy(self, k, block, to, own=False):
        return pltpu.make_async_remote_copy(
            src_ref=self.x_ref if own else self._slot(*block), dst_ref=self._slot(*block),
            send_sem=self.send_sems.at[k], recv_sem=self.recv_sems.at[k], device_id=to, device_id_type=MESH)

    def _mine(self):
        return pltpu.make_async_copy(self.x_ref, self._slot(*self.me), self.local_sem)

    def _first(self):
        return [self._copy(0, self.me, self.sibling, own=True)] + [
            self._copy(1 + j, self.me, (*chip, self.mc), own=True) for j, chip in enumerate(self.chips)]

    def _passed(self):
        return [self._copy(4 + j, (*chip, self.mc), self.sibling) for j, chip in enumerate(self.chips)]

    def start(self):
        self._mine().start()
        for cp in self._first():
            cp.start()

    def forward(self):
        passed = self._passed()
        for j, chip in enumerate(self.chips):
            self._copy(1 + j, (*chip, self.mc), self.me).wait_recv()
            passed[j].start()

    def finish(self):
        self._copy(0, self.sibling, self.me).wait_recv()
        for j, chip in enumerate(self.chips):
            self._copy(4 + j, (*chip, 1 - self.mc), self.me).wait_recv()
        for cp in self._first() + self._passed():
            cp.wait_send()
        self._mine().wait()


N_CHIP = 4


class _SiblingOps:
    def __init__(self, p_ref, theirs_ref, send_sems, recv_sems):
        self.p_ref, self.theirs_ref, self.send_sems, self.recv_sems = p_ref, theirs_ref, send_sems, recv_sems

    @staticmethod
    def result(p):
        return _sds((N_CHIP,) + p.shape[1:], p.dtype)

    @staticmethod
    def scratch():
        return [pltpu.SemaphoreType.DMA((N_CHIP,))] * 2

    def _copies(self):
        mx, my, mc = lax.axis_index("x"), lax.axis_index("y"), lax.axis_index("c")
        return [pltpu.make_async_remote_copy(
            src_ref=self.p_ref.at[2 * chip + 1 - mc], dst_ref=self.theirs_ref.at[chip],
            send_sem=self.send_sems.at[chip], recv_sem=self.recv_sems.at[chip],
            device_id=(mx, my, 1 - mc), device_id_type=MESH) for chip in range(N_CHIP)]

    def start(self):
        for cp in self._copies():
            cp.start()

    def forward(self):
        pass

    def finish(self):
        for cp in self._copies():
            cp.wait()


class _ChipsOps:
    def __init__(self, q_ref, out_ref, send_sems, recv_sems, local_sem):
        self.q_ref, self.out_ref = q_ref, out_ref
        self.send_sems, self.recv_sems, self.local_sem = send_sems, recv_sems, local_sem

    @staticmethod
    def result(q):
        return _sds(q.shape, q.dtype)

    @staticmethod
    def scratch():
        return [pltpu.SemaphoreType.DMA((N_CHIP - 1,)), pltpu.SemaphoreType.DMA((N_CHIP - 1,)), pltpu.SemaphoreType.DMA(())]

    def _copies(self):
        mx, my, mc = lax.axis_index("x"), lax.axis_index("y"), lax.axis_index("c")
        me = 2 * mx + my
        local = pltpu.make_async_copy(self.q_ref.at[me], self.out_ref.at[me], self.local_sem)
        remote = []
        for r in range(1, N_CHIP):
            px = 1 - mx if r & 2 else mx
            py = 1 - my if r & 1 else my
            remote.append(pltpu.make_async_remote_copy(
                src_ref=self.q_ref.at[2 * px + py], dst_ref=self.out_ref.at[me],
                send_sem=self.send_sems.at[r - 1], recv_sem=self.recv_sems.at[r - 1],
                device_id=(px, py, mc), device_id_type=MESH))
        return local, remote

    def start(self):
        local, remote = self._copies()
        local.start()
        for cp in remote:
            cp.start()

    def forward(self):
        pass

    def finish(self):
        local, remote = self._copies()
        for cp in remote:
            cp.wait()
        local.wait()


def _pair_sum(p, theirs, name):
    n, R, C = theirs.shape
    tr = _tile_rows(R, 256)

    def body(p_ref, t_ref, o_ref):
        mc = lax.axis_index("c")
        o_ref[0] = (p_ref[0, mc] + t_ref[0]).astype(BF16)

    blk = pl.BlockSpec((1, tr, C), lambda s, i: (s, i, 0))
    return _pcall(body, name=name, out_shape=_sds((n, R, C), BF16), grid=(n, R // tr),
                  in_specs=[pl.BlockSpec((1, 2, tr, C), lambda s, i: (s, 0, i, 0)), blk],
                  out_specs=blk, sem=("parallel", "parallel"))(p.reshape(n, 2, R, C), theirs)


def _sum_parts(parts, name):
    P, R, C = parts.shape
    tr = _tile_rows(R, 256)

    def body(p_ref, o_ref):
        g = p_ref[0].astype(F32)
        for k in range(1, P):
            g = g + p_ref[k].astype(F32)
        o_ref[...] = g

    return _pcall(body, name=name, out_shape=_sds((R, C)), grid=(R // tr,),
                  in_specs=[pl.BlockSpec((P, tr, C), lambda i: (0, i, 0))],
                  out_specs=pl.BlockSpec((tr, C), lambda i: (i, 0)), sem=("parallel",))(parts)


def _adamw(parts, w, m, v, name):
    P, R, C = parts.shape
    tr = R if R <= 256 else _tile_rows(R, 256)

    def body(p_ref, w_ref, m_ref, v_ref, g_ref, d_ref, nm_ref, nv_ref):
        g = p_ref[0]
        for k in range(1, P):
            g = g + p_ref[k]
        nm = ADAM_B1 * m_ref[...] + (1.0 - ADAM_B1) * g
        nv = ADAM_B2 * v_ref[...] + (1.0 - ADAM_B2) * (g * g)
        m_hat = nm / (1.0 - ADAM_B1 ** ADAM_STEP)
        v_hat = nv / (1.0 - ADAM_B2 ** ADAM_STEP)
        g_ref[...] = g
        d_ref[...] = -ADAM_LR * (m_hat / (jnp.sqrt(v_hat) + ADAM_EPS) + ADAM_WD * w_ref[...])
        nm_ref[...] = nm
        nv_ref[...] = nv

    blk = pl.BlockSpec((tr, C), lambda i: (i, 0))
    return _pcall(
        body, name=name, out_shape=[_sds((R, C))] * 4, grid=(R // tr,),
        in_specs=[pl.BlockSpec((P, tr, C), lambda i: (0, i, 0)), blk, blk, blk],
        out_specs=[blk] * 4, sem=("parallel",))(parts, w, m, v)


def _adam_update(g, w, m, v):
    nm = ADAM_B1 * m + (1.0 - ADAM_B1) * g
    nv = ADAM_B2 * v + (1.0 - ADAM_B2) * (g * g)
    m_hat = nm / (1.0 - ADAM_B1 ** ADAM_STEP)
    v_hat = nv / (1.0 - ADAM_B2 ** ADAM_STEP)
    return -ADAM_LR * (m_hat / (jnp.sqrt(v_hat) + ADAM_EPS) + ADAM_WD * w), nm, nv


class _Canvas:
    def __init__(self):
        self.offset, self.views, self.rows = {}, {}, 0

    def add(self, key, r, c, blocks=1):
        need = r // blocks if blocks > 1 else r * (-(-c // D))
        if need >= 8:
            self.rows = -(-self.rows // 8) * 8
        self.offset[key], self.views[key] = self.rows, (r, c, blocks)
        self.rows += need

    @property
    def total(self):
        return -(-self.rows // 8) * 8

    def cells(self, key):
        (r, c, blocks), off = self.views[key], self.offset[key]
        if blocks > 1:
            n = r // blocks
            return [(off, n, b * c, c, slice(b * n, (b + 1) * n), slice(0, c)) for b in range(blocks)]
        if c <= D:
            return [(off, r, 0, c, slice(0, r), slice(0, c))]
        chunks = -(-c // D)
        return [(off + i * chunks + j, 1, 0, min(D, c - j * D), slice(i, i + 1), slice(j * D, min(c, (j + 1) * D)))
                for i in range(r) for j in range(chunks)]


def _canvas_fill(canvas, sources, name):
    arrays = [a for a, _ in sources]

    def body(*refs):
        out_ref = refs[-1]
        out_ref[...] = jnp.zeros_like(out_ref)
        for ref, (_, items) in zip(refs[:-1], sources):
            for key, src_rows, view_row in items:
                n = src_rows.stop - src_rows.start
                for row, count, lane, width, vrows, vcols in canvas.cells(key):
                    lo, hi = max(vrows.start, view_row), min(vrows.stop, view_row + n)
                    if lo < hi:
                        src = slice(src_rows.start + lo - view_row, src_rows.start + hi - view_row)
                        out_ref[row + lo - vrows.start:row + hi - vrows.start, lane:lane + width] = ref[src, vcols]

    return _pcall(body, name=name, out_shape=_sds((canvas.total, D)))(*arrays)


def _adamw_canvas(canvas, parts, params, sum_only, name):
    n = len(params)

    def body(p_ref, *refs):
        sum_ref = refs[-1]
        g_all = p_ref[0]
        for k in range(1, N_DEV):
            g_all = g_all + p_ref[k]
        sum_ref[...] = g_all
        for i, (key, _, _, _) in enumerate(params):
            w_ref, m_ref, v_ref = refs[3 * i:3 * i + 3]
            outs = refs[3 * n + 4 * i:3 * n + 4 * i + 4]
            for row, count, lane, width, vrows, vcols in canvas.cells(key):
                g = sum_ref[row:row + count, lane:lane + width]
                delta, nm, nv = _adam_update(g, w_ref[vrows, vcols], m_ref[vrows, vcols], v_ref[vrows, vcols])
                for ref, val in zip(outs, (g, delta, nm, nv)):
                    ref[vrows, vcols] = val
        for i, key in enumerate(sum_only):
            for row, count, lane, width, vrows, vcols in canvas.cells(key):
                refs[7 * n + i][vrows, vcols] = sum_ref[row:row + count, lane:lane + width]

    args = [a for _, w, m, v in params for a in (w, m, v)]
    out_shape = [_sds(w.shape) for _, w, _, _ in params for _ in range(4)] + [_sds(canvas.views[k][:2]) for k in sum_only]
    res = _pcall(body, name=name, out_shape=out_shape, scratch=[pltpu.VMEM(parts.shape[1:], F32)])(parts, *args)
    out = {key: res[4 * i:4 * i + 4] for i, (key, _, _, _) in enumerate(params)}
    out.update({key: [res[4 * n + i]] for i, key in enumerate(sum_only)})
    return out


def _tile_rows(n, pref):
    best = None
    for t in range(8, pref + 1, 8):
        if n % t == 0:
            best = t
    assert best is not None, (n, pref)
    return best


def _mm(pairs, mode, *, name, out_dtype=F32, bias=None, tn_pref=1024, side=None):
    M = pairs[0][0].shape[0]
    N = pairs[0][1].shape[1] if mode == "nn" else pairs[0][1].shape[0]
    tm, tn = _rows(M), _tile(N, tn_pref)
    n_pairs = len(pairs)
    has_bias = bias is not None

    def body(*refs):
        acc = _pairs_dot(refs[:2 * n_pairs], mode)
        if has_bias:
            acc = acc + refs[2 * n_pairs][...]
        refs[-1][...] = acc.astype(refs[-1].dtype)

    in_specs, args = _pair_specs(pairs, mode, tm, tn)
    if has_bias:
        in_specs.append(pl.BlockSpec((1, tn), lambda j, i: (0, j)))
        args.append(bias)
    res = _call(body, args, side, name=name, out_shape=[_sds((M, N), out_dtype)], grid=(N // tn, M // tm),
                in_specs=in_specs, out_specs=[pl.BlockSpec((tm, tn), lambda j, i: (i, j))], sem=("parallel", "parallel"))
    return res[0] if side is None else (res[0], res[1])


def _pairs_dot(ab, mode):
    acc = None
    for p in range(len(ab) // 2):
        a, b = _bf(ab[2 * p][...]), _bf(ab[2 * p + 1][...])
        d = _nn(a, b) if mode == "nn" else _nt(a, b)
        acc = d if acc is None else acc + d
    return acc


def _pair_specs(pairs, mode, tm, tn):
    in_specs, args = [], []
    for a, b in pairs:
        K = a.shape[1]
        in_specs.append(pl.BlockSpec((tm, K), lambda j, i: (i, 0)))
        if mode == "nn":
            in_specs.append(pl.BlockSpec((K, tn), lambda j, i: (0, j)))
        else:
            in_specs.append(pl.BlockSpec((tn, K), lambda j, i: (j, 0)))
        args += [a, b]
    return in_specs, args


def _mm_resid(pairs, resid, gvec, *, name, bias=None, norm=None):
    M, N = resid.shape
    tm = _rows(M)
    n_pairs = len(pairs)
    has_bias, has_norm = bias is not None, norm is not None

    def body(*refs):
        acc = _pairs_dot(refs[:2 * n_pairs], "nn")
        pos = 2 * n_pairs
        if has_bias:
            acc = acc + refs[pos][...]
            pos += 1
        xv = refs[pos][...] + refs[pos + 1][...] * acc
        outs = refs[pos + 2 + 3 * has_norm:]
        outs[0][...] = xv
        outs[1][...] = acc.astype(BF16)
        if has_norm:
            w_ref, sc_ref, sh_ref = refs[pos + 2:pos + 5]
            r = lax.rsqrt(jnp.mean(xv * xv, axis=-1, keepdims=True) + EPS)
            outs[2][...] = ((xv * r * w_ref[...]) * (1.0 + sc_ref[...]) + sh_ref[...]).astype(BF16)

    in_specs, args = _pair_specs(pairs, "nn", tm, N)
    vec = pl.BlockSpec((1, N), lambda j, i: (0, 0))
    row = pl.BlockSpec((tm, N), lambda j, i: (i, 0))
    if has_bias:
        in_specs.append(vec)
        args.append(bias)
    in_specs += [row, vec] + [vec] * (3 * has_norm)
    args += [resid, gvec] + (list(norm) if has_norm else [])
    return _pcall(body, name=name, out_shape=[_sds((M, N)), _sds((M, N), BF16)] + [_sds((M, N), BF16)] * has_norm,
                  grid=(1, M // tm), in_specs=in_specs, out_specs=[row] * (2 + has_norm),
                  sem=("parallel", "parallel"))(*args)


def _mm_tn(a, b, *, name, tm_pref=1408, tn_pref=1536):
    K, M = a.shape
    N = b.shape[1]
    tm, tn = _tile(M, tm_pref), _tile(N, tn_pref)
    tk = K if K <= 2 * ROW_TILE else 2 * ROW_TILE

    def body(a_ref, b_ref, o_ref):
        @pl.when(pl.program_id(2) == 0)
        def _():
            o_ref[...] = jnp.zeros_like(o_ref)
        o_ref[...] += _tn(_bf(a_ref[...]), _bf(b_ref[...]))

    return _pcall(
        body, name=name, out_shape=_sds((M, N)), grid=(M // tm, N // tn, K // tk),
        in_specs=[pl.BlockSpec((tk, tm), lambda i, j, k: (k, i)), pl.BlockSpec((tk, tn), lambda i, j, k: (k, j))],
        out_specs=pl.BlockSpec((tm, tn), lambda i, j, k: (i, j)),
        sem=("parallel", "parallel", "arbitrary"))(a, b)


def _mm_swiglu(h, wg_t, wu_t, name, side=None):
    M, K = h.shape
    N = wg_t.shape[0]
    tm, tn = _rows(M), _tile(N, 1408)

    def body(h_ref, wg_ref, wu_ref, gate_ref, up_ref, act_ref):
        hv = _bf(h_ref[...])
        gate = _nt(hv, wg_ref[...])
        up = _nt(hv, wu_ref[...])
        gate_ref[...] = gate.astype(BF16)
        up_ref[...] = up.astype(BF16)
        act_ref[...] = (_silu(gate) * up).astype(BF16)

    w_spec = pl.BlockSpec((tn, K), lambda j, i: (j, 0))
    o_spec = pl.BlockSpec((tm, tn), lambda j, i: (i, j))
    return _call(body, (h, wg_t, wu_t), side, name=name,
                 out_shape=[_sds((M, N), BF16)] * 3, grid=(N // tn, M // tm),
                 in_specs=[pl.BlockSpec((tm, K), lambda j, i: (i, 0)), w_spec, w_spec], out_specs=[o_spec] * 3,
                 sem=("parallel", "parallel"))


def _mm_swiglu_bwd(dout, wd, gate, up, name, side=None):
    M, K = dout.shape
    N = wd.shape[0]
    tm, tn = _rows(M), _tile(N, 1408)

    def body(d_ref, wd_ref, gate_ref, up_ref, dg_ref, du_ref):
        dact = _nt(_bf(d_ref[...]), wd_ref[...])
        act, dact_dg = _silu_and_grad(gate_ref[...].astype(F32))
        dg_ref[...] = (dact * up_ref[...].astype(F32) * dact_dg).astype(BF16)
        du_ref[...] = (dact * act).astype(BF16)

    t_spec = pl.BlockSpec((tm, tn), lambda j, i: (i, j))
    return _call(
        body, (dout, wd, gate, up), side, name=name, out_shape=[_sds((M, N), BF16)] * 2, grid=(N // tn, M // tm),
        in_specs=[pl.BlockSpec((tm, K), lambda j, i: (i, 0)), pl.BlockSpec((tn, K), lambda j, i: (j, 0)), t_spec, t_spec],
        out_specs=[t_spec] * 2, sem=("parallel", "parallel"))


def _norm_mod(x, w, sc, sh, name, side=None):
    T = x.shape[0]
    tm = _rows(T)

    def body(x_ref, w_ref, sc_ref, sh_ref, o_ref):
        xv = x_ref[...]
        r = lax.rsqrt(jnp.mean(xv * xv, axis=-1, keepdims=True) + EPS)
        o_ref[...] = ((xv * r * w_ref[...]) * (1.0 + sc_ref[...]) + sh_ref[...]).astype(BF16)

    return _call(body, (x, w, sc, sh), side, name=name, out_shape=[_sds((T, D), BF16)], grid=(T // tm,),
                 in_specs=[_row_spec(tm, D), _vec_spec(D), _vec_spec(D), _vec_spec(D)],
                 out_specs=[_row_spec(tm, D)], sem=("parallel",))


def _gate_rows(dxv, br_ref, g_ref, db_ref, acc_ref):
    db = g_ref[...] * dxv
    db_ref[...] = db.astype(BF16)
    acc_ref[3:4, :] += _colsum(dxv * br_ref[...].astype(F32))
    acc_ref[4:5, :] += _colsum(db)


def _norm_mod_bwd(x, dh, dres, w, sc, branch, g, name, side=None):
    T = x.shape[0]
    tm = _rows(T)
    gated = branch is not None

    def body(x_ref, dh_ref, dres_ref, w_ref, sc_ref, *rest):
        (br_ref, g_ref, dx_ref, db_ref, acc_ref) = rest if gated else (None, None, rest[0], None, rest[1])

        @pl.when(pl.program_id(0) == 0)
        def _():
            acc_ref[...] = jnp.zeros_like(acc_ref)
        xv, dh_v, wv = x_ref[...], dh_ref[...], w_ref[...]
        r = lax.rsqrt(jnp.mean(xv * xv, axis=-1, keepdims=True) + EPS)
        n = xv * r
        dnw = dh_v * (1.0 + sc_ref[...])
        dn = dnw * wv
        dxv = dres_ref[...] + r * (dn - n * jnp.mean(dn * n, axis=-1, keepdims=True))
        dx_ref[...] = dxv
        acc_ref[0:1, :] += _colsum(dh_v * (n * wv))
        acc_ref[1:2, :] += _colsum(dh_v)
        acc_ref[2:3, :] += _colsum(dnw * n)
        if gated:
            _gate_rows(dxv, br_ref, g_ref, db_ref, acc_ref)

    row = _row_spec(tm, D)
    args = (x, dh, dres, w, sc) + ((branch, g) if gated else ())
    return _call(body, args, side, name=name,
                 out_shape=[_sds((T, D))] + ([_sds((T, D), BF16)] if gated else []) + [_sds((8, D))], grid=(T // tm,),
                 in_specs=[row, row, row, _vec_spec(D), _vec_spec(D)] + ([row, _vec_spec(D)] if gated else []),
                 out_specs=[row] + ([row] if gated else []) + [_vec_spec(D, 8)], sem=("arbitrary",))


def _final_loss(x, wf, target, branch, g, name):
    T = x.shape[0]
    tm = _rows(T)

    def body(x_ref, w_ref, t_ref, br_ref, g_ref, dx_ref, db_ref, acc_ref):
        @pl.when(pl.program_id(0) == 0)
        def _():
            acc_ref[...] = jnp.zeros_like(acc_ref)
        xv, wv = x_ref[...], w_ref[...]
        r = lax.rsqrt(jnp.mean(xv * xv, axis=-1, keepdims=True) + EPS)
        n = xv * r
        err = n * wv - t_ref[...]
        dy = err * (1.0 / D)
        dn = dy * wv
        dxv = r * (dn - n * jnp.mean(dn * n, axis=-1, keepdims=True))
        dx_ref[...] = dxv
        acc_ref[0:1, :] += _colsum(dy * n)
        acc_ref[1:2, :] += jnp.broadcast_to(_allsum(err * err) * (0.5 / D), (1, D))
        _gate_rows(dxv, br_ref, g_ref, db_ref, acc_ref)

    row = _row_spec(tm, D)
    return _pcall(body, name=name, out_shape=[_sds((T, D)), _sds((T, D), BF16), _sds((8, D))], grid=(T // tm,),
                  in_specs=[row, _vec_spec(D), row, row, _vec_spec(D)],
                  out_specs=[row, row, _vec_spec(D, 8)], sem=("arbitrary",))(x, wf, target, branch, g)


def _mod_matmul(c_all, ada_w, name):
    n = ada_w.shape[2]

    def body(c_ref, w_ref, cond_ref, o_ref):
        cond = _silu(c_ref[...])
        cond_ref[...] = cond
        o_ref[0] = _nn(cond, w_ref[0])

    return _pcall(body, name=name, out_shape=[_sds((N_DEV, D)), _sds((DEPTH, N_DEV, n))], grid=(DEPTH,),
                  in_specs=[pl.BlockSpec((N_DEV, D), lambda l: (0, 0)), pl.BlockSpec((1, D, n), lambda l: (l, 0, 0))],
                  out_specs=[pl.BlockSpec((N_DEV, D), lambda l: (0, 0)), pl.BlockSpec((1, N_DEV, n), lambda l: (l, 0, 0))],
                  sem=("arbitrary",))(c_all, ada_w)


def _add_rows(a, b, name):
    def body(a_ref, b_ref, o_ref):
        o_ref[...] = a_ref[...] + b_ref[...]

    return _pcall(body, name=name, out_shape=_sds(a.shape))(a, b)


def _ada_w_grad(cond, dmod_cols, name):
    n = dmod_cols.shape[2]

    def body(c_ref, d_ref, o_ref):
        o_ref[0] = _tn(c_ref[...], d_ref[0])

    return _pcall(body, name=name, out_shape=_sds((DEPTH, D, n)), grid=(DEPTH,),
                  in_specs=[pl.BlockSpec((N_DEV, D), lambda l: (0, 0)), pl.BlockSpec((1, N_DEV, n), lambda l: (l, 0, 0))],
                  out_specs=pl.BlockSpec((1, D, n), lambda l: (l, 0, 0)), sem=("parallel",))(cond, dmod_cols)


def _conv_fwd(pm, conv_w, conv_b, name):
    T = pm.shape[0]
    tm = _rows(T)
    C = CONV_DIM

    def body(x_ref, prev_ref, w_ref, b_ref, o_ref):
        cur = x_ref[...].astype(F32)
        prev = jnp.where(pl.program_id(0) > 0, prev_ref[...].astype(F32)[8:16], 0.0)
        cur8 = cur[0:8]
        row8 = lax.broadcasted_iota(jnp.int32, (8, C), 0)
        full = w_ref[3:4, :] * cur
        head = w_ref[3:4, :] * cur8
        for k in range(1, SSM_CONV):
            wk = w_ref[3 - k:4 - k, :]
            full = full + wk * pltpu.roll(cur, k, 0)
            head = head + wk * jnp.where(row8 < k, pltpu.roll(prev, k, 0), pltpu.roll(cur8, k, 0))
        o_ref[...] = full + b_ref[...]
        o_ref[0:8, :] = head + b_ref[...]

    return _pcall(
        body, name=name, out_shape=_sds((T, C)), grid=(T // tm,),
        in_specs=[pl.BlockSpec((tm, C), lambda i: (i, 2)),
                  pl.BlockSpec((16, C), lambda i: (jnp.maximum(i * (tm // 16) - 1, 0), 2)),
                  _vec_spec(C, SSM_CONV), _vec_spec(C)],
        out_specs=_row_spec(tm, C), sem=("parallel",))(pm, pm, conv_w, conv_b)


def _conv_bwd(dc, pm, conv_w, name):
    T = dc.shape[0]
    tm = _rows(T)
    C = CONV_DIM
    nt = T // tm

    def body(dc_ref, nxt_ref, x_ref, w_ref, dx_ref, acc_ref):
        i = pl.program_id(0)

        @pl.when(i == 0)
        def _():
            acc_ref[...] = jnp.zeros_like(acc_ref)
        dcv = dc_ref[...]
        nxt = jnp.where(i < nt - 1, nxt_ref[...], 0.0)
        xc = x_ref[...].astype(F32)
        dc8t, x8t = dcv[tm - 8:tm], xc[tm - 8:tm]
        row8 = lax.broadcasted_iota(jnp.int32, (8, C), 0)
        full = w_ref[3:4, :] * dcv
        tail = w_ref[3:4, :] * dc8t
        acc_ref[3:4, :] += _colsum(dcv * xc)
        for k in range(1, SSM_CONV):
            wk = w_ref[3 - k:4 - k, :]
            up = pltpu.roll(dcv, tm - k, 0)
            up_tail = jnp.where(row8 + k >= 8, pltpu.roll(nxt, 8 - k, 0), pltpu.roll(dc8t, 8 - k, 0))
            full = full + wk * up
            tail = tail + wk * up_tail
            prod = up * xc
            acc_ref[3 - k:4 - k, :] += _colsum(prod) - _colsum(prod[tm - 8:tm]) + _colsum(up_tail * x8t)
        acc_ref[4:5, :] += _colsum(dcv)
        dx_ref[...] = jnp.concatenate([full[0:tm - 8], tail], axis=0).astype(BF16)

    return _pcall(
        body, name=name, out_shape=[_sds((T, C), BF16), _sds((8, C))], grid=(nt,),
        in_specs=[_row_spec(tm, C),
                  pl.BlockSpec((8, C), lambda i: (jnp.minimum((i + 1) * (tm // 8), T // 8 - 1), 0)),
                  pl.BlockSpec((tm, C), lambda i: (i, 2)),
                  _vec_spec(C, SSM_CONV)],
        out_specs=[_row_spec(tm, C), _vec_spec(C, 8)], sem=("arbitrary",))(dc, dc, pm, conv_w)


def _ssd_prologue(cpre, dtr, dtb, alog):
    L = CHUNK
    xc = _silu(cpre)
    pre = dtr + dtb
    dt = jnp.maximum(pre, 0.0) + jnp.log1p(jnp.exp(-jnp.abs(pre)))
    a = -jnp.exp(alog)
    la = dt * a
    row = lax.broadcasted_iota(jnp.int32, (L, L), 0)
    col = lax.broadcasted_iota(jnp.int32, (L, L), 1)
    causal = row >= col
    tri = causal.astype(F32)
    lc = _nn(tri, la, HIGHEST)
    return xc, pre, dt, a, causal, tri, lc, row, col


def _head_indicator():
    m = np.zeros((LANES, SSM_INNER), np.float32)
    for h in range(SSM_HEADS):
        m[h, h * SSM_HEAD_DIM:(h + 1) * SSM_HEAD_DIM] = 1.0
    return jnp.asarray(m, dtype=BF16)


def _split_dot(x, ind, dims):
    hi = x.astype(BF16)
    lo = (x - hi.astype(F32)).astype(BF16)
    return _dot(hi, ind, dims) + _dot(lo, ind, dims)


def _expand(x16, ind):
    return _split_dot(x16, ind, ((1,), (0,)))


def _headsum(x, ind, single_pass=False):
    if single_pass:
        return _dot(x.astype(BF16), ind, ((1,), (1,)))
    return _split_dot(x, ind, ((1,), (1,)))


def _ssd_fwd(cpre, dtr, pm, dtb, alog, dskip, normw, ind, name, side=None):
    T = cpre.shape[0]
    nc = T // CHUNK
    L, P, H, HPG, N = CHUNK, SSM_HEAD_DIM, SSM_HEADS, SSM_HEADS // SSM_GROUPS, SSM_STATE
    half = SSM_INNER // SSM_GROUPS

    def body(cp_ref, dtr_ref, z_ref, dtb_ref, alog_ref, dskip_ref, nw_ref, ind_ref, ya_ref, y_ref, sp_ref, st_ref):
        @pl.when(pl.program_id(0) == 0)
        def _():
            st_ref[...] = jnp.zeros_like(st_ref)
        xc, _, dt, _, causal, _, lc, _, _ = _ssd_prologue(cp_ref[...], dtr_ref[...], dtb_ref[...], alog_ref[...])
        lct = lc.T
        ind = ind_ref[...]
        llast = lc[L - 1:L, :]
        xs = xc[:, :SSM_INNER]
        xd = xs * _expand(dt, ind)
        ex = _expand(jnp.exp(lc), ind)
        xd_end = _bf(xd * _expand(jnp.exp(llast - lc), ind))
        cdx = _expand(jnp.broadcast_to(jnp.exp(llast), (8, LANES)), ind)[0:1]
        xdb = _bf(xd)
        sp_ref[0] = st_ref[...]
        for g in range(SSM_GROUPS):
            sl = slice(g * half, (g + 1) * half)
            bm = _bf(xc[:, SSM_INNER + g * N:SSM_INNER + (g + 1) * N])
            cm = _bf(xc[:, SSM_INNER + (SSM_GROUPS + g) * N:SSM_INNER + (SSM_GROUPS + g + 1) * N])
            cb = _nt(cm, bm)
            st = st_ref[g]
            y_ref[:, sl] = ex[:, sl] * _nn(cm, _bf(st)) + dskip_ref[:, sl] * xs[:, sl]
            st_ref[g] = st * cdx[:, sl] + _tn(bm, xd_end[:, sl])
            for j in range(HPG):
                h = g * HPG + j
                decay = jnp.where(causal, jnp.exp(jnp.where(causal, lc[:, h:h + 1] - lct[h:h + 1, :], 0.0)), 0.0)
                y_ref[:, h * P:(h + 1) * P] += _nn(_bf(cb * decay), xdb[:, h * P:(h + 1) * P])
        y2 = y_ref[...] * _silu(z_ref[...].astype(F32))
        for g in range(SSM_GROUPS):
            yg = y2[:, g * half:(g + 1) * half]
            r = lax.rsqrt(jnp.mean(yg * yg, axis=-1, keepdims=True) + EPS)
            ya_ref[:, g * half:(g + 1) * half] = (yg * r * nw_ref[:, g * half:(g + 1) * half]).astype(BF16)

    return _call(
        body, (cpre, dtr, pm, dtb, alog, dskip, normw, ind), side, name=name,
        out_shape=[_sds((T, SSM_INNER), BF16), _sds((T, SSM_INNER)), _sds((nc, SSM_GROUPS, N, half))], grid=(nc,),
        in_specs=[_row_spec(L, CONV_DIM), _row_spec(L, LANES), _row_spec(L, SSM_INNER, 0),
                  _vec_spec(LANES), _vec_spec(LANES), _vec_spec(SSM_INNER), _vec_spec(SSM_INNER), _vec_spec(SSM_INNER, LANES)],
        out_specs=[_row_spec(L, SSM_INNER), _row_spec(L, SSM_INNER),
                   pl.BlockSpec((1, SSM_GROUPS, N, half), lambda i: (i, 0, 0, 0))],
        scratch=[pltpu.VMEM((SSM_GROUPS, N, half), F32)], sem=("arbitrary",))


def _ssd_bwd(cpre, dtr, pm, ypre, sprev, dya, dtb, alog, dskip, normw, ind, name, side=None):
    T = cpre.shape[0]
    nc = T // CHUNK
    L, P, H, HPG, N = CHUNK, SSM_HEAD_DIM, SSM_HEADS, SSM_HEADS // SSM_GROUPS, SSM_STATE
    half = SSM_INNER // SSM_GROUPS

    def body(cp_ref, dtr_ref, z_ref, y_ref, sp_ref, dya_ref, dtb_ref, alog_ref, dskip_ref, nw_ref, ind_ref,
             dz_ref, dcp_ref, ddtr_ref, acc_ref, dnw_ref, ds_ref, dy_ref, dxd_ref, rr_ref, yoff_ref, dcd_ref):
        @pl.when(pl.program_id(0) == 0)
        def _():
            ds_ref[...] = jnp.zeros_like(ds_ref)
            acc_ref[...] = jnp.zeros_like(acc_ref)
            dnw_ref[...] = jnp.zeros_like(dnw_ref)
        cpre_v = cp_ref[...]
        xc, pre, dt, a, causal, tri, lc, row, col = _ssd_prologue(cpre_v, dtr_ref[...], dtb_ref[...], alog_ref[...])
        lct = lc.T
        zv, yv = z_ref[...].astype(F32), y_ref[...]
        sz, dsz = _silu_and_grad(zv)
        y2 = yv * sz
        dya_v = dya_ref[...]
        nwv = nw_ref[...]
        for g in range(SSM_GROUPS):
            sl = slice(g * half, (g + 1) * half)
            yg = y2[:, sl]
            r = lax.rsqrt(jnp.mean(yg * yg, axis=-1, keepdims=True) + EPS)
            nrm = yg * r
            dnw_ref[:, sl] += _colsum(dya_v[:, sl] * nrm)
            dn = dya_v[:, sl] * nwv[:, sl]
            dy2 = r * (dn - nrm * jnp.mean(dn * nrm, axis=-1, keepdims=True))
            dy_ref[:, sl] = dy2 * sz[:, sl]
            dz_ref[:, sl] = (dy2 * yv[:, sl] * dsz[:, sl]).astype(BF16)
        ind = ind_ref[...]
        llast = lc[L - 1:L, :]
        dte16 = jnp.exp(llast - lc)
        cd16 = jnp.exp(llast)
        xs = xc[:, :SSM_INNER]
        dtx = _expand(dt, ind)
        ex = _expand(jnp.exp(lc), ind)
        dtex = _expand(dte16, ind)
        cdx = _expand(jnp.broadcast_to(cd16, (8, LANES)), ind)[0:1]
        xd = xs * dtx
        xdb = _bf(xd)
        xd_end = _bf(xd * dtex)
        dyv = dy_ref[...]
        dy_off = _bf(ex * dyv)
        dyb = _bf(dyv)
        dskx = dskip_ref[...]
        lane_c = lax.broadcasted_iota(jnp.int32, (L, LANES), 1)
        lane1 = lax.broadcasted_iota(jnp.int32, (1, LANES), 1)
        sub16 = lax.broadcasted_iota(jnp.int32, (H, L), 0)
        dlc_c = jnp.zeros((L, LANES), F32)
        dlc_r = jnp.zeros((H, L), F32)
        for g in range(SSM_GROUPS):
            sl = slice(g * half, (g + 1) * half)
            b_lo = SSM_INNER + g * N
            c_lo = SSM_INNER + (SSM_GROUPS + g) * N
            bm, cm = _bf(xc[:, b_lo:b_lo + N]), _bf(xc[:, c_lo:c_lo + N])
            cb = _nt(cm, bm)
            st, dst = sp_ref[0, g], ds_ref[g]
            stb, dstb = _bf(st), _bf(dst)
            dcm = _nt(dy_off[:, sl], stb)
            ds_ref[g] = _tn(cm, dy_off[:, sl]) + dst * cdx[:, sl]
            rr_ref[:, sl] = _nn(bm, dstb)
            yoff_ref[:, sl] = ex[:, sl] * _nn(cm, stb)
            db = _nt(xd_end[:, sl], dstb)
            dcd_ref[:, sl] = _colsum(dst * st)
            dcb = jnp.zeros((L, L), F32)
            for j in range(HPG):
                h = g * HPG + j
                hs = slice(h * P, (h + 1) * P)
                decay = jnp.where(causal, jnp.exp(jnp.where(causal, lc[:, h:h + 1] - lct[h:h + 1, :], 0.0)), 0.0)
                m = cb * decay
                dxd_ref[:, hs] = _tn(_bf(m), dyb[:, hs])
                dm = _nt(dyb[:, hs], xdb[:, hs])
                dcb = dcb + dm * decay
                gm = dm * m
                dlc_c = dlc_c + jnp.where(lane_c == h, _rowsum(gm), 0.0)
                dlc_r = dlc_r + jnp.where(sub16 == h, _colsum(gm), 0.0)
            dcbb = _bf(dcb)
            dcp_ref[:, c_lo:c_lo + N] = dcm + _nn(dcbb, bm)
            dcp_ref[:, b_lo:b_lo + N] = db + _tn(dcbb, cm)
        dxd_diag, rr = dxd_ref[...], rr_ref[...]
        tt = _headsum(rr * xd, ind, single_pass=True) * dte16
        dlc_rt = jnp.concatenate([dlc_r, jnp.zeros((LANES - H, L), F32)], axis=0).T
        dlc = dlc_c - dlc_rt + _headsum(dyv * yoff_ref[...], ind, single_pass=True) - tt
        dcd = _headsum(jnp.broadcast_to(dcd_ref[...], (8, SSM_INNER)), ind)[0:1]
        dlc = dlc + jnp.where(row == L - 1, _colsum(tt) + dcd * cd16, 0.0)
        dla = _tn(tri, dlc, HIGHEST)
        dxd = dxd_diag + dtex * rr
        ddt = _headsum(dxd * xs, ind, single_pass=True) + dla * a
        ddtr = jnp.where(lane_c < H, ddt * _sigmoid(pre), 0.0)
        ddtr_ref[...] = ddtr
        acc_ref[0:1, :] += _colsum(ddtr)
        acc_ref[1:2, :] += jnp.where(lane1 < H, _colsum(dla * dt) * a, 0.0)
        acc_ref[2:3, :] += _headsum(jnp.broadcast_to(_colsum(dyv * xs), (8, SSM_INNER)), ind)[0:1]
        dcp_ref[:, 0:SSM_INNER] = dxd * dtx + dskx * dyv
        dcp_ref[...] = dcp_ref[...] * _dsilu(cpre_v)

    rev = lambda i: (nc - 1 - i, 0)
    rspec = lambda c: pl.BlockSpec((L, c), rev)
    return _call(
        body, (cpre, dtr, pm, ypre, sprev, dya, dtb, alog, dskip, normw, ind), side, name=name,
        out_shape=[_sds((T, SSM_INNER), BF16), _sds((T, CONV_DIM)), _sds((T, LANES)), _sds((8, LANES)), _sds((1, SSM_INNER))],
        grid=(nc,),
        in_specs=[rspec(CONV_DIM), rspec(LANES), rspec(SSM_INNER), rspec(SSM_INNER),
                  pl.BlockSpec((1, SSM_GROUPS, N, half), lambda i: (nc - 1 - i, 0, 0, 0)), rspec(SSM_INNER),
                  _vec_spec(LANES), _vec_spec(LANES), _vec_spec(SSM_INNER), _vec_spec(SSM_INNER), _vec_spec(SSM_INNER, LANES)],
        out_specs=[rspec(SSM_INNER), rspec(CONV_DIM), rspec(LANES), _vec_spec(LANES, 8), _vec_spec(SSM_INNER)],
        scratch=[pltpu.VMEM((SSM_GROUPS, N, half), F32), pltpu.VMEM((L, SSM_INNER), F32), pltpu.VMEM((L, SSM_INNER), F32),
                 pltpu.VMEM((L, SSM_INNER), F32), pltpu.VMEM((L, SSM_INNER), F32), pltpu.VMEM((1, SSM_INNER), F32)],
        sem=("arbitrary",))


def _gmlp_common(u, v, lnw, lnb, with_grads=False):
    (ug, dug), (vg, dvg) = (_gelu_and_grad(u), _gelu_and_grad(v)) if with_grads else ((_gelu(u), None), (_gelu(v), None))
    mu = jnp.mean(vg, axis=-1, keepdims=True)
    cen = vg - mu
    rstd = lax.rsqrt(jnp.mean(cen * cen, axis=-1, keepdims=True) + EPS)
    vhat = cen * rstd
    out = (ug, rstd, vhat, vhat * lnw + lnb)
    return out + (dug, dvg) if with_grads else out


def _causal_mask():
    row = lax.broadcasted_iota(jnp.int32, (CHUNK, CHUNK), 0)
    col = lax.broadcasted_iota(jnp.int32, (CHUNK, CHUNK), 1)
    return row >= col


def _gmlp_fwd(pm, lnw, lnb, ws, bs_exp, name, side=None):
    T = pm.shape[0]
    nc = T // CHUNK
    L, G = CHUNK, GMLP_GROUPS

    def body(u_ref, v_ref, lnw_ref, lnb_ref, ws_ref, bs_ref, o_ref):
        ug, _, _, vn = _gmlp_common(u_ref[...].astype(F32), v_ref[...].astype(F32), lnw_ref[...], lnb_ref[...])
        causal = _causal_mask()
        for g in range(G):
            sl = slice(g * L, (g + 1) * L)
            wm = _bf(jnp.where(causal, ws_ref[g], 0.0))
            sv = _nn(wm, _bf(vn[:, sl])) + bs_ref[:, sl]
            o_ref[:, sl] = (ug[:, sl] * sv).astype(BF16)

    return _call(
        body, (pm, pm, lnw, lnb, ws, bs_exp), side, name=name, out_shape=[_sds((T, GMLP_INNER), BF16)], grid=(nc,),
        in_specs=[_row_spec(L, GMLP_INNER, 1), _row_spec(L, GMLP_INNER, 2), _vec_spec(GMLP_INNER), _vec_spec(GMLP_INNER),
                  pl.BlockSpec((G, L, L), lambda i: (0, 0, 0)), _vec_spec(GMLP_INNER, L)],
        out_specs=[_row_spec(L, GMLP_INNER)], sem=("parallel",))


def _gmlp_bwd(pm, dyb, lnw, lnb, ws, bs_exp, name, side=None):
    T = pm.shape[0]
    nc = T // CHUNK
    L, G = CHUNK, GMLP_GROUPS

    def body(u_ref, v_ref, dy_ref, lnw_ref, lnb_ref, ws_ref, bs_ref, du_ref, dv_ref, dws_ref, dbs_ref, acc_ref, dvn_ref):
        @pl.when(pl.program_id(0) == 0)
        def _():
            dws_ref[...] = jnp.zeros_like(dws_ref)
            dbs_ref[...] = jnp.zeros_like(dbs_ref)
            acc_ref[...] = jnp.zeros_like(acc_ref)
        uv, vv, dyv, lnwv = u_ref[...].astype(F32), v_ref[...].astype(F32), dy_ref[...], lnw_ref[...]
        ug, rstd, vhat, vn, dug, dvg_act = _gmlp_common(uv, vv, lnwv, lnb_ref[...], with_grads=True)
        causal = _causal_mask()
        lane = lax.broadcasted_iota(jnp.int32, (L, LANES), 1)
        dbs = jnp.zeros((L, LANES), F32)
        for g in range(G):
            sl = slice(g * L, (g + 1) * L)
            wm = _bf(jnp.where(causal, ws_ref[g], 0.0))
            vng = _bf(vn[:, sl])
            sv = _nn(wm, vng) + bs_ref[:, sl]
            du_ref[:, sl] = (dyv[:, sl] * sv * dug[:, sl]).astype(BF16)
            dsv = dyv[:, sl] * ug[:, sl]
            dsvb = _bf(dsv)
            dws_ref[g] += jnp.where(causal, _nt(dsvb, vng), 0.0)
            dbs = dbs + jnp.where(lane == g, _rowsum(dsv), 0.0)
            dvn_ref[:, sl] = _tn(wm, dsvb)
        dbs_ref[...] += dbs
        dvn = dvn_ref[...]
        acc_ref[0:1, :] += _colsum(dvn * vhat)
        acc_ref[1:2, :] += _colsum(dvn)
        dvh = dvn * lnwv
        dvg = rstd * (dvh - jnp.mean(dvh, axis=-1, keepdims=True) - vhat * jnp.mean(dvh * vhat, axis=-1, keepdims=True))
        dv_ref[...] = (dvg * dvg_act).astype(BF16)

    return _call(
        body, (pm, pm, dyb, lnw, lnb, ws, bs_exp), side, name=name,
        out_shape=[_sds((T, GMLP_INNER), BF16), _sds((T, GMLP_INNER), BF16), _sds((G, L, L)), _sds((L, LANES)), _sds((8, GMLP_INNER))],
        grid=(nc,),
        in_specs=[_row_spec(L, GMLP_INNER, 1), _row_spec(L, GMLP_INNER, 2), _row_spec(L, GMLP_INNER),
                  _vec_spec(GMLP_INNER), _vec_spec(GMLP_INNER), pl.BlockSpec((G, L, L), lambda i: (0, 0, 0)),
                  _vec_spec(GMLP_INNER, L)],
        out_specs=[_row_spec(L, GMLP_INNER), _row_spec(L, GMLP_INNER), pl.BlockSpec((G, L, L), lambda i: (0, 0, 0)),
                   _vec_spec(LANES, L), _vec_spec(GMLP_INNER, 8)],
        scratch=[pltpu.VMEM((L, GMLP_INNER), F32)], sem=("arbitrary",))


def _rel_buckets():
    qi = np.arange(CHUNK)[:, None]
    sj = np.arange(2 * CHUNK)[None, :]
    dist = np.maximum(qi + CHUNK - sj, 0)
    max_exact = REL_BUCKETS // 2
    log_ratio = (np.log(np.maximum(dist, 1).astype(np.float32) / np.float32(max_exact))
                 / np.float32(math.log(REL_MAX_DIST / max_exact))).astype(np.float32)
    large = max_exact + (log_ratio * np.float32(REL_BUCKETS - max_exact)).astype(np.int32)
    return np.where(dist < max_exact, dist, np.minimum(large, REL_BUCKETS - 1))


def _bucket_onehot_t():
    bucket = _rel_buckets().reshape(-1)
    return jnp.asarray((np.arange(REL_BUCKETS)[:, None] == bucket[None, :]).astype(np.float32))


def _bias_from_table(table_t, onehot_t, window, name):
    def body(t_ref, o_ref, w_ref, out_ref):
        out_ref[...] = jnp.where(w_ref[...] > 0.5, _nn(t_ref[...], o_ref[...], HIGHEST), NEG_INF)

    return _pcall(body, name=name, out_shape=_sds((ATTN_HEADS, onehot_t.shape[1])))(table_t, onehot_t, window)


def _table_from_dbias(dbias, onehot_t, name):
    def body(d_ref, o_ref, out_ref):
        out_ref[...] = _nt(o_ref[...], d_ref[...], HIGHEST)

    return _pcall(body, name=name, out_shape=_sds((REL_BUCKETS, ATTN_HEADS)))(dbias, onehot_t)


def _softmax_sink(logits, sink):
    mx = jnp.maximum(jnp.max(logits, axis=-1, keepdims=True), sink)
    e = jnp.exp(logits - mx)
    es = jnp.exp(sink - mx)
    inv = 1.0 / (_rowsum(e) + es)
    return e * inv, es * inv


def _first_block_penalty(n):
    sj = lax.broadcasted_iota(jnp.int32, (1, 2 * CHUNK), 1)
    return jnp.where((sj < CHUNK) & (n == 0), NEG_INF, 0.0)


def _window_mask_flat():
    qi = np.arange(CHUNK)[:, None]
    sj = np.arange(2 * CHUNK)[None, :]
    rel = qi + CHUNK - sj
    return jnp.asarray(((rel >= 0) & (rel < CHUNK)).astype(np.float32).reshape(1, -1))


def _stack_heads(ref, first, count, width, scale=None):
    x = jnp.concatenate([_bf(ref[:, (first + j) * width:(first + j + 1) * width]) for j in range(count)], axis=0)
    return x if scale is None else x * jnp.asarray(scale, x.dtype)


def _attn_fwd(qkv, bias, sinks, name):
    T = qkv.shape[0]
    nb = T // CHUNK
    L, DH, HPK = CHUNK, ATTN_DH, ATTN_HEADS // ATTN_KV
    scale = DH ** -0.5
    kcol, vcol = ATTN_HEADS * DH // LANES, ATTN_HEADS * DH // LANES + 1

    def body(q_ref, k_ref, v_ref, kp_ref, vp_ref, bias_ref, sink_ref, o_ref, lg_ref, p_ref):
        n = pl.program_id(0)
        pen = _first_block_penalty(n)
        kband = _bf(jnp.concatenate([kp_ref[...], k_ref[...]], axis=0))
        vband = _bf(jnp.concatenate([vp_ref[...], v_ref[...]], axis=0))
        for kv in range(ATTN_KV):
            lg_ref[...] = _nt(_stack_heads(q_ref, kv * HPK, HPK, DH, scale), kband[:, kv * DH:(kv + 1) * DH])
            for j in range(HPK):
                h = kv * HPK + j
                p, _ = _softmax_sink(lg_ref[j * L:(j + 1) * L, :] + bias_ref[h] + pen, sink_ref[h])
                p_ref[j * L:(j + 1) * L, :] = _bf(p)
            og = _nn(p_ref[...], vband[:, kv * DH:(kv + 1) * DH])
            for j in range(HPK):
                h = kv * HPK + j
                o_ref[:, h * DH:(h + 1) * DH] = og[j * L:(j + 1) * L].astype(BF16)

    prev = lambda i: jnp.maximum(i - 1, 0)
    return _pcall(
        body, name=name, out_shape=_sds((T, ATTN_HEADS * DH), BF16), grid=(nb,),
        in_specs=[_row_spec(L, ATTN_HEADS * DH, 0), _row_spec(L, LANES, kcol), _row_spec(L, LANES, vcol),
                  pl.BlockSpec((L, LANES), lambda i: (prev(i), kcol)), pl.BlockSpec((L, LANES), lambda i: (prev(i), vcol)),
                  pl.BlockSpec((ATTN_HEADS, L, 2 * L), lambda i: (0, 0, 0)),
                  pl.BlockSpec(memory_space=pltpu.SMEM)],
        out_specs=_row_spec(L, ATTN_HEADS * DH),
        scratch=[pltpu.VMEM((HPK * L, 2 * L), F32), pltpu.VMEM((HPK * L, 2 * L), BF16)],
        sem=("parallel",))(qkv, qkv, qkv, qkv, qkv, bias, sinks)


def _attn_bwd(qkv, datt, bias, sinks, name, side=None):
    T = qkv.shape[0]
    nb = T // CHUNK
    L, DH, HPK = CHUNK, ATTN_DH, ATTN_HEADS // ATTN_KV
    scale = DH ** -0.5
    kcol, vcol = ATTN_HEADS * DH // LANES, ATTN_HEADS * DH // LANES + 1

    def body(q_ref, k_ref, v_ref, kp_ref, vp_ref, do_ref, bias_ref, sink_ref,
             dq_ref, dk_ref, dv_ref, bsum_ref, dbias_ref, dsink_ref, pend_k, pend_v, band_k, band_v, lg_ref, dp_ref, p_ref, dl_ref):
        n = pl.program_id(0)

        @pl.when(n == 0)
        def _():
            dbias_ref[...] = jnp.zeros_like(dbias_ref)
            dsink_ref[...] = jnp.zeros_like(dsink_ref)
            bsum_ref[...] = jnp.zeros_like(bsum_ref)

        def emit_kv(dk, dv):
            dk_ref[...] = dk.astype(BF16)
            dv_ref[...] = dv.astype(BF16)
            bsum_ref[:, ATTN_HEADS * DH:ATTN_HEADS * DH + LANES] += _colsum(dk)
            bsum_ref[:, ATTN_HEADS * DH + LANES:] += _colsum(dv)

        @pl.when(n < nb)
        def _():
            pen = _first_block_penalty(n)
            kband = _bf(jnp.concatenate([kp_ref[...], k_ref[...]], axis=0))
            vband = _bf(jnp.concatenate([vp_ref[...], v_ref[...]], axis=0))
            lane1 = lax.broadcasted_iota(jnp.int32, (1, LANES), 1)
            dsink = jnp.zeros((1, LANES), F32)
            for kv in range(ATTN_KV):
                kb, vb = kband[:, kv * DH:(kv + 1) * DH], vband[:, kv * DH:(kv + 1) * DH]
                qg = _stack_heads(q_ref, kv * HPK, HPK, DH, scale)
                dog = _stack_heads(do_ref, kv * HPK, HPK, DH)
                lg_ref[...] = _nt(qg, kb)
                dp_ref[...] = _nt(dog, vb)
                for j in range(HPK):
                    h = kv * HPK + j
                    rows = slice(j * L, (j + 1) * L)
                    p, ps = _softmax_sink(lg_ref[rows, :] + bias_ref[h] + pen, sink_ref[h])
                    dp = dp_ref[rows, :]
                    delta = _rowsum(p * dp)
                    dl = p * (dp - delta)
                    dbias_ref[h] += dl
                    p_ref[rows, :] = _bf(p)
                    dl_ref[rows, :] = _bf(dl)
                    dsink = dsink + jnp.where(lane1 == h, -_colsum(ps * delta), 0.0)
                band_v[:, kv * DH:(kv + 1) * DH] = _tn(p_ref[...], dog)
                dqg = _nn(dl_ref[...], kb) * scale
                band_k[:, kv * DH:(kv + 1) * DH] = _tn(dl_ref[...], qg)
                for j in range(HPK):
                    h = kv * HPK + j
                    dq_ref[:, h * DH:(h + 1) * DH] = dqg[j * L:(j + 1) * L].astype(BF16)
                    bsum_ref[:, h * DH:(h + 1) * DH] += _colsum(dqg[j * L:(j + 1) * L])
            dsink_ref[...] += dsink

            @pl.when(n > 0)
            def _():
                emit_kv(pend_k[...] + band_k[0:L, :], pend_v[...] + band_v[0:L, :])
            pend_k[...] = band_k[L:2 * L, :]
            pend_v[...] = band_v[L:2 * L, :]

        @pl.when(n == nb)
        def _():
            emit_kv(pend_k[...], pend_v[...])

    cur = lambda i: jnp.minimum(i, nb - 1)
    prev = lambda i: jnp.maximum(jnp.minimum(i, nb - 1) - 1, 0)
    lag = lambda i: jnp.maximum(i - 1, 0)
    return _call(
        body, (qkv, qkv, qkv, qkv, qkv, datt, bias, sinks), side, name=name,
        out_shape=[_sds((T, ATTN_HEADS * DH), BF16), _sds((T, LANES), BF16), _sds((T, LANES), BF16), _sds((1, QKV_DIM)),
                   _sds((ATTN_HEADS, L, 2 * L)), _sds((1, LANES))],
        grid=(nb + 1,),
        in_specs=[pl.BlockSpec((L, ATTN_HEADS * DH), lambda i: (cur(i), 0)),
                  pl.BlockSpec((L, LANES), lambda i: (cur(i), kcol)), pl.BlockSpec((L, LANES), lambda i: (cur(i), vcol)),
                  pl.BlockSpec((L, LANES), lambda i: (prev(i), kcol)), pl.BlockSpec((L, LANES), lambda i: (prev(i), vcol)),
                  pl.BlockSpec((L, ATTN_HEADS * DH), lambda i: (cur(i), 0)),
                  pl.BlockSpec((ATTN_HEADS, L, 2 * L), lambda i: (0, 0, 0)),
                  pl.BlockSpec(memory_space=pltpu.SMEM)],
        out_specs=[pl.BlockSpec((L, ATTN_HEADS * DH), lambda i: (cur(i), 0)),
                   pl.BlockSpec((L, LANES), lambda i: (lag(i), 0)), pl.BlockSpec((L, LANES), lambda i: (lag(i), 0)),
                   _vec_spec(QKV_DIM), pl.BlockSpec((ATTN_HEADS, L, 2 * L), lambda i: (0, 0, 0)), _vec_spec(LANES)],
        scratch=[pltpu.VMEM((L, LANES), F32), pltpu.VMEM((L, LANES), F32),
                 pltpu.VMEM((2 * L, LANES), F32), pltpu.VMEM((2 * L, LANES), F32),
                 pltpu.VMEM((HPK * L, 2 * L), F32), pltpu.VMEM((HPK * L, 2 * L), F32),
                 pltpu.VMEM((HPK * L, 2 * L), BF16), pltpu.VMEM((HPK * L, 2 * L), BF16)],
        sem=("arbitrary",))


def _pad_rows(a, mult):
    pad = (-a.shape[-2]) % mult
    if pad == 0:
        return a
    cfg = [(0, 0)] * (a.ndim - 2) + [(0, pad), (0, 0)]
    return jnp.pad(a, cfg)


class _Pack:
    def __init__(self, width, mult, total_mult):
        self.width, self.mult, self.total_mult = width, mult, total_mult
        self.entries = []
        self.rows = 0

    def add(self, key, shape):
        n = int(np.prod(shape))
        rows = -(-n // self.width)
        self.entries.append((key, self.rows, rows, tuple(shape)))
        self.rows += -(-rows // self.mult) * self.mult

    @property
    def total(self):
        return -(-self.rows // self.total_mult) * self.total_mult

    def pack(self, pieces, dtype, lead=()):
        parts = []
        for key, _, rows, shape in self.entries:
            a = pieces[key].astype(dtype).reshape(lead + (-1,))
            n = int(np.prod(shape))
            a = jnp.pad(a, [(0, 0)] * len(lead) + [(0, rows * self.width - n)])
            a = a.reshape(lead + (rows, self.width))
            parts.append(_pad_rows(a, self.mult))
        out = jnp.concatenate(parts, axis=len(lead))
        return _pad_rows(out, self.total_mult)

    def unpack(self, packed, lead=()):
        out = {}
        for key, off, rows, shape in self.entries:
            a = lax.slice_in_dim(packed, off, off + rows, axis=len(lead))
            a = a.reshape(lead + (-1,))
            n = int(np.prod(shape))
            out[key] = lax.slice_in_dim(a, 0, n, axis=len(lead)).reshape(lead + shape)
        return out


def _ffn_fwd(x, h, mod, wg_t, wu_t, wd, tag, next_norm=None, gather=None):
    side = None if gather is None else (_GatherOps, gather)
    gate, up, act, *gathered = _mm_swiglu(h, wg_t, wu_t, f"ffn_gateup_{tag}", side=side)
    x_out, ffn_out, *h_next = _mm_resid([(act, wd)], x, mod[5:6], name=f"ffn_down_{tag}", norm=next_norm)
    return (x_out, dict(h=h, gate=gate, up=up, act=act, out=ffn_out), *h_next, *gathered)


def _ffn_bwd(dx_out, dffn, x_in, saved, mod, norm_w, wg_t, wu_t, wd, below, tag, exchange=None):
    side = None if exchange is None else (_ChipsOps, exchange)
    dgate, dup, *from_chips = _mm_swiglu_bwd(dffn, wd, saved["gate"], saved["up"], f"ffn_act_bwd_{tag}", side=side)
    d_wd = _mm_tn(saved["act"], dffn, name=f"ffn_dwd_{tag}")
    d_wg_t = _mm_tn(dgate, saved["h"], name=f"ffn_dwg_{tag}")
    d_wu_t = _mm_tn(dup, saved["h"], name=f"ffn_dwu_{tag}")
    dh = _mm([(dgate, wg_t), (dup, wu_t)], "nn", name=f"ffn_dh_{tag}")
    dx, d_below, acc = _norm_mod_bwd(x_in, dh, dx_out, norm_w, mod[4:5], below[0], below[1], f"ffn_norm_bwd_{tag}")
    return (dx, d_below, dict(d_wg=d_wg_t, d_wu=d_wu_t, d_wd=d_wd, acc=acc), *from_chips)


_BIG = [
    ("out_w", "out_w_even", 0, "row"), ("qkv_w", "qkv_w", 0, "col"), ("o_w", "o_w", 0, "row"),
    ("gate0", "ffn_gate_w", 0, "col"), ("up0", "ffn_up_w", 0, "col"), ("down0", "ffn_down_w", 0, "row"),
    ("gate1", "ffn_gate_w", 1, "col"), ("up1", "ffn_up_w", 1, "col"), ("down1", "ffn_down_w", 1, "row"),
    ("in_w", "in_w_even", 0, "col"),
]


def _to_wire(a, kind):
    return a.T if kind == "col" else a


_GATHER_GROUPS = [["in_w"], ["out_w"], ["gate0", "up0", "down0"], ["qkv_w", "o_w"], ["gate1", "up1", "down1"]]
_GRAD_GROUPS = [["gate1", "up1", "down1"], ["qkv_w", "o_w"], ["out_w", "gate0", "up0", "down0"], ["in_w"]]

_REPLICATED = ["ada_b", "norm_mix_w", "norm_ffn_w", "conv_b", "dt_bias", "a_log", "d_skip", "ssm_norm_w", "gmlp_ln_w",
               "gmlp_ln_b", "gmlp_ws", "gmlp_bs", "sinks", "rel_table", "final_norm_w"]
_TINY_SHARDED = ["conv_w", "qkv_b", "o_b"]

_WEIGHTS = ['ada_w', 'ada_b', 'norm_mix_w', 'norm_ffn_w', 'in_w_even', 'conv_w', 'conv_b', 'dt_bias', 'a_log', 'd_skip',
            'ssm_norm_w', 'gmlp_ln_w', 'gmlp_ln_b', 'gmlp_ws', 'gmlp_bs', 'out_w_even', 'qkv_w', 'qkv_b', 'o_w', 'o_b',
            'sinks', 'rel_table', 'ffn_gate_w', 'ffn_up_w', 'ffn_down_w', 'final_norm_w']


def _step(x, c, loss_target, W, M, V):
    T = x.shape[1]
    x0 = x[0]
    target = loss_target[0]
    me = 4 * lax.axis_index("x") + 2 * lax.axis_index("y") + lax.axis_index("c")

    w_wire_local = {key: _to_wire(W[name][layer].astype(BF16), kind) for key, name, layer, kind in _BIG}

    def wire_pack(keys, mult):
        gp = _Pack(D, 1, mult)
        for key in keys:
            gp.add(key, w_wire_local[key].shape)
        return gp

    gather_packs = [(gp, gp.pack(w_wire_local, BF16)) for gp in (wire_pack(keys, 16) for keys in _GATHER_GROUPS)]
    grad_packs = [wire_pack(keys, 64) for keys in _GRAD_GROUPS]
    full = {}

    def gathered_weights(group, gathered):
        shards = gather_packs[group][0].unpack(gathered, lead=(N_DEV,))
        full.update({key: a.reshape(-1, D) for key, a in shards.items()})


    small_in = _Pack(D, 8, 8)
    small_in.add("c", (1, D))
    small_in.add("conv_w", W["conv_w"][0].shape)
    small_in.add("qkv_b", W["qkv_b"][0].shape)
    small_in.add("o_b", W["o_b"][0].shape)
    sm = small_in.unpack(_all_gather(small_in.pack(
        dict(c=c, conv_w=W["conv_w"][0], qkv_b=W["qkv_b"][0], o_b=W["o_b"][0]), F32), "gather_small"), lead=(N_DEV,))
    c_all = sm["c"].reshape(N_DEV, D)
    conv_w_full = jnp.transpose(sm["conv_w"], (1, 0, 2)).reshape(SSM_CONV, CONV_DIM)
    qkv_b_full = sm["qkv_b"].reshape(1, QKV_DIM)
    o_b_full = sm["o_b"].reshape(1, D)

    ncol = W["ada_w"].shape[2]
    cond, mod_cols = _mod_matmul(c_all, W["ada_w"], "mod_matmul")
    mod_g = _all_gather(mod_cols.reshape(DEPTH * N_DEV, ncol), "gather_mod").reshape(N_DEV, DEPTH, N_DEV, ncol)
    mod_me = lax.dynamic_index_in_dim(mod_g, me, axis=2, keepdims=False)
    mod_me = jnp.transpose(mod_me, (1, 0, 2)).reshape(DEPTH, 6, D)
    mod_me = jnp.pad(mod_me, ((0, 0), (0, 2), (0, 0))).reshape(DEPTH * 8, D)
    ada_b_rows = jnp.pad(W["ada_b"].reshape(DEPTH, 6, D), ((0, 0), (0, 2), (0, 0))).reshape(DEPTH * 8, D)
    mod_all = _add_rows(mod_me, ada_b_rows, "mod_bias").reshape(DEPTH, 8, D)
    mod0, mod1 = mod_all[0], mod_all[1]
    h0, gathered_in = _norm_mod(x0, W["norm_mix_w"][0:1], mod0[1:2], mod0[0:1], "mix_norm_0",
                                side=(_GatherOps, gather_packs[0][1]))
    gathered_weights(0, gathered_in)

    in_t = full["in_w"]
    o1, o2, o3, o4 = SSM_INNER, SSM_INNER + CONV_DIM, SSM_INNER + CONV_DIM + SSM_HEADS, SSM_INNER + CONV_DIM + SSM_HEADS + GMLP_INNER
    w_z, w_xbc, w_dt, w_u, w_v = in_t[:o1], in_t[o1:o2], in_t[o2:o3], in_t[o3:o4], in_t[o4:]
    w_main = jnp.concatenate([w_z, w_u, w_v, w_xbc], axis=0)
    w_dtp = jnp.pad(w_dt, ((0, LANES - SSM_HEADS), (0, 0)))

    pad16 = lambda a: jnp.pad(a.reshape(1, SSM_HEADS), ((0, 0), (0, LANES - SSM_HEADS)))
    dtb, alog = pad16(W["dt_bias"][0]), pad16(W["a_log"][0])
    dskip = jnp.repeat(W["d_skip"][0], SSM_HEAD_DIM).reshape(1, SSM_INNER)
    ssm_nw = W["ssm_norm_w"]
    lnw, lnb = W["gmlp_ln_w"], W["gmlp_ln_b"]
    ws = W["gmlp_ws"][0]
    bs_exp = jnp.repeat(W["gmlp_bs"][0].T, CHUNK, axis=1)
    conv_b = W["conv_b"]
    nmw, nfw = W["norm_mix_w"], W["norm_ffn_w"]
    onehot_t = _bucket_onehot_t()
    head_ind = _head_indicator()
    bias = _bias_from_table(W["rel_table"].T, onehot_t, _window_mask_flat(), "rel_bias").reshape(ATTN_HEADS, CHUNK, 2 * CHUNK)
    sinks = W["sinks"][0]

    pm, gathered_a = _mm([(h0, w_main)], "nt", name="in_proj", tn_pref=1536, out_dtype=BF16,
                         side=(_GatherOps, gather_packs[1][1]))
    gathered_weights(1, gathered_a)
    out_w = full["out_w"]
    dtr = _mm([(h0, w_dtp)], "nt", name="in_proj_dt")
    cpre = _conv_fwd(pm, conv_w_full, conv_b, "conv_fwd")
    ya, ypre, sprev, gathered_b = _ssd_fwd(cpre, dtr, pm, dtb, alog, dskip, ssm_nw, head_ind, "ssd_fwd",
                                           side=(_GatherOps, gather_packs[2][1]))
    gathered_weights(2, gathered_b)
    yb, gathered_c = _gmlp_fwd(pm, lnw, lnb, ws, bs_exp, "gmlp_fwd", side=(_GatherOps, gather_packs[3][1]))
    gathered_weights(3, gathered_c)
    x1, mix0, hf0 = _mm_resid([(ya, out_w[:SSM_INNER]), (yb, out_w[SSM_INNER:])], x0, mod0[2:3], name="out_proj",
                              norm=(nfw[0:1], mod0[4:5], mod0[3:4]))
    x2, ffn0, h1, gathered_d = _ffn_fwd(x1, hf0, mod0, full["gate0"], full["up0"], full["down0"], "0",
                                        next_norm=(nmw[1:2], mod1[1:2], mod1[0:1]), gather=gather_packs[4][1])
    gathered_weights(4, gathered_d)
    qkv_t, o_w = full["qkv_w"], full["o_w"]
    w_q, w_k, w_v_att = qkv_t[:D], qkv_t[D:D + LANES], qkv_t[D + LANES:]

    qkv = _mm([(h1, qkv_t)], "nt", name="qkv_proj", bias=qkv_b_full, tn_pref=1280, out_dtype=BF16)
    att = _attn_fwd(qkv, bias, sinks, "attn_fwd")
    x3, mix1, hf1 = _mm_resid([(att, o_w)], x2, mod1[2:3], name="o_proj", bias=o_b_full,
                              norm=(nfw[1:2], mod1[4:5], mod1[3:4]))
    x4, ffn1 = _ffn_fwd(x3, hf1, mod1, full["gate1"], full["up1"], full["down1"], "1")

    dx4, dffn1, acc_f = _final_loss(x4, W["final_norm_w"].reshape(1, D), target, ffn1["out"], mod1[5:6], "final_loss")
    dx3, dmix1, gf1 = _ffn_bwd(dx4, dffn1, x3, ffn1, mod1, nfw[1:2], full["gate1"], full["up1"], full["down1"],
                               (mix1, mod1[2:3]), "1")
    g_wire = dict(gate1=gf1["d_wg"], up1=gf1["d_wu"], down1=gf1["d_wd"])

    def packed_partials(group):
        return grad_packs[group].pack({key: g_wire[key].reshape(N_DEV, -1, D) for key in _GRAD_GROUPS[group]},
                                      F32, lead=(N_DEV,))

    from_chips = {}
    partials_ffn1 = packed_partials(0)
    datt, theirs_ffn1 = _mm([(dmix1, o_w)], "nt", name="o_proj_dx", out_dtype=BF16, side=(_SiblingOps, partials_ffn1))
    pair_ffn1 = _pair_sum(partials_ffn1, theirs_ffn1, "grads_pair_sum_ffn1")
    d_o_w = _mm_tn(att, dmix1, name="o_proj_dw")
    dq, dk, dv, d_qkv_b, dbias, dsinks, from_chips[0] = _attn_bwd(qkv, datt, bias, sinks, "attn_bwd",
                                                                 side=(_ChipsOps, pair_ffn1))
    d_table = _table_from_dbias(dbias.reshape(ATTN_HEADS, -1), onehot_t, "rel_table_grad")
    d_qkv_t = jnp.concatenate([_mm_tn(dq, h1, name="qkv_dw_q"), _mm_tn(dk, h1, name="qkv_dw_k"), _mm_tn(dv, h1, name="qkv_dw_v")], axis=0)
    g_wire.update(qkv_w=d_qkv_t, o_w=d_o_w)
    partials_l1 = packed_partials(1)
    dh1, theirs_l1 = _mm([(dq, w_q), (dk, w_k), (dv, w_v_att)], "nn", name="qkv_proj_dx", side=(_SiblingOps, partials_l1))
    pair_l1 = _pair_sum(partials_l1, theirs_l1, "grads_pair_sum_l1")
    dx2, dffn0, acc_n1 = _norm_mod_bwd(x2, dh1, dx3, nmw[1:2], mod1[1:2], ffn0["out"], mod0[5:6], "mix_norm_bwd_1")

    dx1, dmix0, gf0, from_chips[1] = _ffn_bwd(dx2, dffn0, x1, ffn0, mod0, nfw[0:1], full["gate0"], full["up0"], full["down0"],
                                              (mix0, mod0[2:3]), "0", exchange=pair_l1)

    dya = _mm([(dmix0, out_w[:SSM_INNER])], "nt", name="out_proj_dx_a")
    dyb = _mm([(dmix0, out_w[SSM_INNER:])], "nt", name="out_proj_dx_b")
    d_out_w = jnp.concatenate([_mm_tn(ya, dmix0, name="out_proj_dw_a"), _mm_tn(yb, dmix0, name="out_proj_dw_b")], axis=0)
    g_wire.update(gate0=gf0["d_wg"], up0=gf0["d_wu"], down0=gf0["d_wd"], out_w=d_out_w)
    partials_ffn0 = packed_partials(2)
    du, dvg, d_ws, d_bs, acc_ln, theirs_ffn0 = _gmlp_bwd(pm, dyb, lnw, lnb, ws, bs_exp, "gmlp_bwd",
                                                         side=(_SiblingOps, partials_ffn0))
    pair_ffn0 = _pair_sum(partials_ffn0, theirs_ffn0, "grads_pair_sum_mix")
    dz, dcpre, ddtr, acc_ssd, d_ssm_nw, from_chips[2] = _ssd_bwd(
        cpre, dtr, pm, ypre, sprev, dya, dtb, alog, dskip, ssm_nw, head_ind, "ssd_bwd", side=(_ChipsOps, pair_ffn0))
    dxbc, acc_conv = _conv_bwd(dcpre, pm, conv_w_full, "conv_bwd")
    d_in_t = jnp.concatenate([
        _mm_tn(dz, h0, name="in_dw_z"), _mm_tn(dxbc, h0, name="in_dw_xbc"),
        _mm_tn(ddtr, h0, name="in_dw_dt")[:SSM_HEADS], _mm_tn(du, h0, name="in_dw_u"), _mm_tn(dvg, h0, name="in_dw_v")], axis=0)
    g_wire.update(in_w=d_in_t)
    partials_mix = packed_partials(3)
    theirs_mix = _comm_call(_SiblingOps, partials_mix, "exchange_grads_sibling")
    pair_mix = _pair_sum(partials_mix, theirs_mix, "grads_pair_sum")
    dh0, from_chips[3] = _mm([(dz, w_z), (dxbc, w_xbc), (ddtr, w_dtp), (du, w_u), (dvg, w_v)], "nn", name="in_proj_dx",
                             side=(_ChipsOps, pair_mix))
    grad_x, acc_n0 = _norm_mod_bwd(x0, dh0, dx1, nmw[0:1], mod0[1:2], None, None, "mix_norm_bwd_0")

    g_mine = {}
    for group in range(len(_GRAD_GROUPS)):
        g_mine.update(grad_packs[group].unpack(_sum_parts(from_chips[group], f"grads_chip_sum_{group}")))
    res_big = [{}, {}, {}, {}]
    for key, name, layer, kind in _BIG:
        g_nat = _to_wire(g_mine[key], kind)
        outs = _adamw(g_nat[None], W[name][layer], M[name][layer], V[name][layer], f"adamw_{key}")
        for res, out in zip(res_big, outs):
            res[key] = out

    acc_f0, acc_f1 = gf0["acc"], gf1["acc"]
    views = dict(ada_b=(DEPTH * 6, D), norm_mix_w=(DEPTH, D), norm_ffn_w=(DEPTH, D), conv_b=(1, CONV_DIM),
                 dt_bias=(1, SSM_HEADS), a_log=(1, SSM_HEADS), d_skip=(1, SSM_HEADS), ssm_norm_w=(1, D), gmlp_ln_w=(1, D),
                 gmlp_ln_b=(1, D), gmlp_ws=(GMLP_GROUPS * CHUNK, CHUNK), gmlp_bs=(GMLP_GROUPS, CHUNK),
                 sinks=(1, ATTN_HEADS), rel_table=(REL_BUCKETS, ATTN_HEADS), final_norm_w=(1, D),
                 conv_w=(SSM_CONV, CONV_DIM), qkv_b=(1, QKV_DIM), o_b=(1, D), loss=(1, D))
    canvas = _Canvas()
    for key, (r, cdim) in views.items():
        canvas.add(key, r, cdim, blocks=GMLP_GROUPS if key == "gmlp_ws" else 1)
    row = lambda i: slice(i, i + 1)
    sources = [
        (acc_n0, [("ada_b", row(1), 0), ("ada_b", row(0), 1), ("norm_mix_w", row(2), 0)]),
        (acc_f0, [("ada_b", row(3), 2), ("ada_b", row(1), 3), ("ada_b", row(0), 4), ("norm_ffn_w", row(2), 0)]),
        (acc_n1, [("ada_b", row(3), 5), ("ada_b", row(1), 6), ("ada_b", row(0), 7), ("norm_mix_w", row(2), 1)]),
        (acc_f1, [("ada_b", row(3), 8), ("ada_b", row(1), 9), ("ada_b", row(0), 10), ("norm_ffn_w", row(2), 1),
                  ("o_b", row(4), 0)]),
        (acc_f, [("ada_b", row(3), 11), ("final_norm_w", row(0), 0), ("loss", row(1), 0)]),
        (acc_conv, [("conv_w", slice(0, SSM_CONV), 0), ("conv_b", row(4), 0)]),
        (acc_ssd, [("dt_bias", row(0), 0), ("a_log", row(1), 0), ("d_skip", row(2), 0)]),
        (d_ssm_nw, [("ssm_norm_w", row(0), 0)]),
        (acc_ln, [("gmlp_ln_w", row(0), 0), ("gmlp_ln_b", row(1), 0)]),
        (d_ws.reshape(GMLP_GROUPS * CHUNK, CHUNK), [("gmlp_ws", slice(0, GMLP_GROUPS * CHUNK), 0)]),
        (d_bs.T, [("gmlp_bs", slice(0, GMLP_GROUPS), 0)]),
        (dsinks, [("sinks", row(0), 0)]),
        (d_table, [("rel_table", slice(0, REL_BUCKETS), 0)]),
        (d_qkv_b, [("qkv_b", row(0), 0)]),
    ]
    parts_small = _all_gather(_canvas_fill(canvas, sources, "small_grads_canvas"), "gather_small_grads")
    view = lambda a, key: a.reshape(views[key])
    params = [(name, view(W[name], name), view(M[name], name), view(V[name], name)) for name in _REPLICATED]
    small_out = _adamw_canvas(canvas, parts_small, params, _TINY_SHARDED + ["loss"], "adamw_small")
    loss = small_out["loss"][0][0, 0]
    res_small = [{name: small_out[name][k].reshape(W[name].shape) for name in _REPLICATED} for k in range(4)]

    n_cw, n_qb, n_ob = W["conv_w"].shape[2], W["qkv_b"].shape[1], W["o_b"].shape[1]
    g_tiny = dict(conv_w=lax.dynamic_slice_in_dim(small_out["conv_w"][0], me * n_cw, n_cw, axis=1)[None],
                  qkv_b=lax.dynamic_slice_in_dim(small_out["qkv_b"][0], me * n_qb, n_qb, axis=1),
                  o_b=lax.dynamic_slice_in_dim(small_out["o_b"][0], me * n_ob, n_ob, axis=1))
    tiny = _Pack(D, 8, 8)
    for name in _TINY_SHARDED:
        tiny.add(name, W[name].shape)
    pkt = lambda S: tiny.pack({name: S[name] for name in _TINY_SHARDED}, F32)
    res_tiny = [tiny.unpack(r) for r in _adamw(pkt(g_tiny)[None], pkt(W), pkt(M), pkt(V), "adamw_tiny")]

    dmod_all = parts_small[:, canvas.offset["ada_b"]:canvas.offset["ada_b"] + DEPTH * 6].reshape(N_DEV, DEPTH, 6 * D)
    dmod_cols = jnp.transpose(lax.dynamic_slice_in_dim(dmod_all, me * ncol, ncol, axis=2), (1, 0, 2))
    g_ada_w = _ada_w_grad(cond, dmod_cols, "ada_w_grad")
    flat = lambda a: a.reshape(DEPTH * D, ncol)
    res_ada = [r.reshape(DEPTH, D, ncol) for r in _adamw(flat(g_ada_w)[None], flat(W["ada_w"]), flat(M["ada_w"]), flat(V["ada_w"]), "adamw_ada_w")]

    def result(kind_idx, name):
        if name == "ada_w":
            return res_ada[kind_idx]
        if name in _REPLICATED:
            return res_small[kind_idx][name]
        if name in _TINY_SHARDED:
            return res_tiny[kind_idx][name]
        pieces = [res_big[kind_idx][key] for key, nm, layer, kind in _BIG if nm == name]
        return jnp.stack(pieces)

    outs = [loss, grad_x[None]]
    for kind_idx in range(4):
        outs += [result(kind_idx, name) for name in _WEIGHTS]
    return tuple(outs)


def kernel(x, c, ada_w, ada_b, norm_mix_w, norm_ffn_w, in_w_even, conv_w, conv_b, dt_bias, a_log, d_skip, ssm_norm_w, gmlp_ln_w, gmlp_ln_b, gmlp_ws, gmlp_bs, out_w_even, qkv_w, qkv_b, o_w, o_b, sinks, rel_table, ffn_gate_w, ffn_up_w, ffn_down_w, final_norm_w, loss_target, m_ada_w, m_ada_b, m_norm_mix_w, m_norm_ffn_w, m_in_w_even, m_conv_w, m_conv_b, m_dt_bias, m_a_log, m_d_skip, m_ssm_norm_w, m_gmlp_ln_w, m_gmlp_ln_b, m_gmlp_ws, m_gmlp_bs, m_out_w_even, m_qkv_w, m_qkv_b, m_o_w, m_o_b, m_sinks, m_rel_table, m_ffn_gate_w, m_ffn_up_w, m_ffn_down_w, m_final_norm_w, v_ada_w, v_ada_b, v_norm_mix_w, v_norm_ffn_w, v_in_w_even, v_conv_w, v_conv_b, v_dt_bias, v_a_log, v_d_skip, v_ssm_norm_w, v_gmlp_ln_w, v_gmlp_ln_b, v_gmlp_ws, v_gmlp_bs, v_out_w_even, v_qkv_w, v_qkv_b, v_o_w, v_o_b, v_sinks, v_rel_table, v_ffn_gate_w, v_ffn_up_w, v_ffn_down_w, v_final_norm_w):
    args = locals()
    W = {n: args[n] for n in _WEIGHTS}
    M = {n: args["m_" + n] for n in _WEIGHTS}
    V = {n: args["v_" + n] for n in _WEIGHTS}
    return _step(x, c, loss_target, W, M, V)
```

```python
import functools
import math

import numpy as np
import jax
import jax.numpy as jnp
from jax import lax
from jax.experimental import pallas as pl
from jax.experimental.pallas import tpu as pltpu

F32 = jnp.float32
BF16 = jnp.bfloat16
HIGHEST = lax.Precision.HIGHEST
MESH = pl.DeviceIdType.MESH

N_DEV = 8
D = 1024
DEPTH = 2
SSM_HEADS = 16
SSM_HEAD_DIM = 64
SSM_INNER = 1024
SSM_GROUPS = 2
SSM_STATE = 128
SSM_CONV = 4
CHUNK = 128
CONV_DIM = SSM_INNER + 2 * SSM_GROUPS * SSM_STATE
GMLP_GROUPS = 8
GMLP_INNER = 1024
IN_EVEN = 4624
ATTN_HEADS = 16
ATTN_KV = 2
ATTN_DH = 64
QKV_DIM = 1280
REL_BUCKETS = 32
REL_MAX_DIST = 128
FFN = 2816
EPS = 1e-6
NEG_INF = -1e30
LANES = 128

ADAM_LR = 0.001
ADAM_B1 = 0.9
ADAM_B2 = 0.999
ADAM_EPS = 1e-08
ADAM_WD = 0.01
ADAM_STEP = 10

VMEM_LIMIT_BYTES = 56 * 1024 * 1024
ROW_TILE = 512


def _pcall(body, *, name, out_shape, grid=(), in_specs=None, out_specs=None, scratch=(), sem=None):
    params = dict(vmem_limit_bytes=VMEM_LIMIT_BYTES)
    if sem is not None:
        params["dimension_semantics"] = sem
    specs = {} if in_specs is None else dict(in_specs=in_specs, out_specs=out_specs)
    return pl.pallas_call(
        body, name=name, out_shape=out_shape, grid=grid, **specs,
        scratch_shapes=list(scratch), compiler_params=pltpu.CompilerParams(**params))


def _call(body, args, side=None, *, name, out_shape, grid, in_specs, out_specs, scratch=(), sem=None):
    if side is None:
        return _pcall(body, name=name, out_shape=out_shape, grid=grid, in_specs=in_specs, out_specs=out_specs,
                      scratch=scratch, sem=sem)(*args)
    ops_cls, x = side
    n_in, n_out, n_scr = len(in_specs), len(out_shape), len(scratch)
    steps = int(np.prod(grid))
    hbm = pl.BlockSpec(memory_space=pl.ANY)

    def wrapped(*refs):
        ins, x_ref = refs[:n_in], refs[n_in]
        outs, r_ref = refs[n_in + 1:n_in + 1 + n_out], refs[n_in + 1 + n_out]
        scr, sems = refs[n_in + 2 + n_out:n_in + 2 + n_out + n_scr], refs[n_in + 2 + n_out + n_scr:]
        ops = ops_cls(x_ref, r_ref, *sems)
        step = pl.program_id(0)
        for axis in range(1, len(grid)):
            step = step * grid[axis] + pl.program_id(axis)
        pl.when(step == 0)(ops.start)
        body(*ins, *outs, *scr)
        pl.when(step == (3 * steps) // 4)(ops.forward)
        pl.when(step == steps - 1)(ops.finish)

    return _pcall(
        wrapped, name=name, out_shape=list(out_shape) + [ops_cls.result(x)], grid=grid,
        in_specs=list(in_specs) + [hbm], out_specs=list(out_specs) + [hbm],
        scratch=list(scratch) + ops_cls.scratch(), sem=("arbitrary",) * len(grid))(*args, x)


def _tile(n, pref):
    if n <= pref:
        return n
    best = None
    for t in range(LANES, pref + 1, LANES):
        if n % t == 0:
            best = t
    assert best is not None, (n, pref)
    return best


def _rows(T):
    return min(ROW_TILE, T)


def _sds(shape, dtype=F32):
    return jax.ShapeDtypeStruct(shape, dtype)


def _row_spec(tm, c, col=0):
    return pl.BlockSpec((tm, c), lambda i, col=col: (i, col))


def _vec_spec(c, r=1):
    return pl.BlockSpec((r, c), lambda i: (0, 0))


def _sigmoid(x):
    return jax.nn.sigmoid(x)


def _silu(x):
    return x * _sigmoid(x)


def _dsilu(x):
    s = _sigmoid(x)
    return s * (1.0 + x * (1.0 - s))


def _silu_and_grad(x):
    s = _sigmoid(x)
    return x * s, s * (1.0 + x * (1.0 - s))


def _gelu(x):
    return 0.5 * x * (1.0 + lax.erf(x * 0.7071067811865476))


def _gelu_and_grad(x):
    phi = 0.5 * (1.0 + lax.erf(x * 0.7071067811865476))
    return x * phi, phi + x * jnp.exp(-0.5 * x * x) * 0.3989422804014327


def _dot(a, b, dims, precision=None):
    return lax.dot_general(a, b, (dims, ((), ())), precision=precision, preferred_element_type=F32)


def _nn(a, b, precision=None):
    return _dot(a, b, ((1,), (0,)), precision)


def _nt(a, b, precision=None):
    return _dot(a, b, ((1,), (1,)), precision)


def _tn(a, b, precision=None):
    return _dot(a, b, ((0,), (0,)), precision)


def _bf(x):
    return x.astype(BF16)


def _colsum(x):
    return jnp.sum(x, axis=0, keepdims=True)


def _rowsum(x):
    return jnp.sum(x, axis=1, keepdims=True)


def _allsum(x):
    return _colsum(_rowsum(x))


def _comm_call(ops_cls, x, name, from_vmem=False):
    def body(x_ref, out_ref, *sems):
        ops = ops_cls(x_ref, out_ref, *sems)
        ops.start()
        ops.forward()
        ops.finish()

    return pl.pallas_call(
        body, name=name, out_shape=ops_cls.result(x),
        in_specs=[pl.BlockSpec(memory_space=pltpu.VMEM if from_vmem else pl.ANY)],
        out_specs=pl.BlockSpec(memory_space=pl.ANY), scratch_shapes=ops_cls.scratch(),
    )(x)


def _all_gather(x, name):
    return _comm_call(_GatherOps, x, name, from_vmem=True)


class _GatherOps:
    def __init__(self, x_ref, out_ref, send_sems, recv_sems, local_sem):
        self.x_ref, self.out_ref = x_ref, out_ref
        self.send_sems, self.recv_sems, self.local_sem = send_sems, recv_sems, local_sem
        mx, my, mc = lax.axis_index("x"), lax.axis_index("y"), lax.axis_index("c")
        self.mc = mc
        self.me, self.sibling = (mx, my, mc), (mx, my, 1 - mc)
        self.chips = [(1 - mx, my), (mx, 1 - my), (1 - mx, 1 - my)]

    @staticmethod
    def result(x):
        return _sds((N_DEV,) + x.shape, x.dtype)

    @staticmethod
    def scratch():
        return [pltpu.SemaphoreType.DMA((7,)), pltpu.SemaphoreType.DMA((7,)), pltpu.SemaphoreType.DMA(())]

    def _slot(self, px, py, pc):
        return self.out_ref.at[4 * px + 2 * py + pc]

    def _copy(self, k, block, to, own=False):
        return pltpu.make_async_remote_copy(
            src_ref=self.x_ref if own else self._slot(*block), dst_ref=self._slot(*block),
            send_sem=self.send_sems.at[k], recv_sem=self.recv_sems.at[k], device_id=to, device_id_type=MESH)

    def _mine(self):
        return pltpu.make_async_copy(self.x_ref, self._slot(*self.me), self.local_sem)

    def _first(self):
        return [self._copy(0, self.me, self.sibling, own=True)] + [
            self._copy(1 + j, self.me, (*chip, self.mc), own=True) for j, chip in enumerate(self.chips)]

    def _passed(self):
        return [self._copy(4 + j, (*chip, self.mc), self.sibling) for j, chip in enumerate(self.chips)]

    def start(self):
        self._mine().start()
        for cp in self._first():
            cp.start()

    def forward(self):
        passed = self._passed()
        for j, chip in enumerate(self.chips):
            self._copy(1 + j, (*chip, self.mc), self.me).wait_recv()
            passed[j].start()

    def finish(self):
        self._copy(0, self.sibling, self.me).wait_recv()
        for j, chip in enumerate(self.chips):
            self._copy(4 + j, (*chip, 1 - self.mc), self.me).wait_recv()
        for cp in self._first() + self._passed():
            cp.wait_send()
        self._mine().wait()


N_CHIP = 4


class _SiblingOps:
    def __init__(self, p_ref, theirs_ref, send_sems, recv_sems):
        self.p_ref, self.theirs_ref, self.send_sems, self.recv_sems = p_ref, theirs_ref, send_sems, recv_sems

    @staticmethod
    def result(p):
        return _sds((N_CHIP,) + p.shape[1:], p.dtype)

    @staticmethod
    def scratch():
        return [pltpu.SemaphoreType.DMA((N_CHIP,))] * 2

    def _copies(self):
        mx, my, mc = lax.axis_index("x"), lax.axis_index("y"), lax.axis_index("c")
        return [pltpu.make_async_remote_copy(
            src_ref=self.p_ref.at[2 * chip + 1 - mc], dst_ref=self.theirs_ref.at[chip],
            send_sem=self.send_sems.at[chip], recv_sem=self.recv_sems.at[chip],
            device_id=(mx, my, 1 - mc), device_id_type=MESH) for chip in range(N_CHIP)]

    def start(self):
        for cp in self._copies():
            cp.start()

    def forward(self):
        pass

    def finish(self):
        for cp in self._copies():
            cp.wait()


class _ChipsOps:
    def __init__(self, q_ref, out_ref, send_sems, recv_sems, local_sem):
        self.q_ref, self.out_ref = q_ref, out_ref
        self.send_sems, self.recv_sems, self.local_sem = send_sems, recv_sems, local_sem

    @staticmethod
    def result(q):
        return _sds(q.shape, q.dtype)

    @staticmethod
    def scratch():
        return [pltpu.SemaphoreType.DMA((N_CHIP - 1,)), pltpu.SemaphoreType.DMA((N_CHIP - 1,)), pltpu.SemaphoreType.DMA(())]

    def _copies(self):
        mx, my, mc = lax.axis_index("x"), lax.axis_index("y"), lax.axis_index("c")
        me = 2 * mx + my
        local = pltpu.make_async_copy(self.q_ref.at[me], self.out_ref.at[me], self.local_sem)
        remote = []
        for r in range(1, N_CHIP):
            px = 1 - mx if r & 2 else mx
            py = 1 - my if r & 1 else my
            remote.append(pltpu.make_async_remote_copy(
                src_ref=self.q_ref.at[2 * px + py], dst_ref=self.out_ref.at[me],
                send_sem=self.send_sems.at[r - 1], recv_sem=self.recv_sems.at[r - 1],
                device_id=(px, py, mc), device_id_type=MESH))
        return local, remote

    def start(self):
        local, remote = self._copies()
        local.start()
        for cp in remote:
            cp.start()

    def forward(self):
        pass

    def finish(self):
        local, remote = self._copies()
        for cp in remote:
            cp.wait()
        local.wait()


def _pair_sum(p, theirs, name):
    n, R, C = theirs.shape
    tr = _tile_rows(R, 256)

    def body(p_ref, t_ref, o_ref):
        mc = lax.axis_index("c")
        o_ref[0] = (p_ref[0, mc] + t_ref[0]).astype(BF16)

    blk = pl.BlockSpec((1, tr, C), lambda s, i: (s, i, 0))
    return _pcall(body, name=name, out_shape=_sds((n, R, C), BF16), grid=(n, R // tr),
                  in_specs=[pl.BlockSpec((1, 2, tr, C), lambda s, i: (s, 0, i, 0)), blk],
                  out_specs=blk, sem=("parallel", "parallel"))(p.reshape(n, 2, R, C), theirs)


def _sum_parts(parts, name):
    P, R, C = parts.shape
    tr = _tile_rows(R, 256)

    def body(p_ref, o_ref):
        g = p_ref[0].astype(F32)
        for k in range(1, P):
            g = g + p_ref[k].astype(F32)
        o_ref[...] = g

    return _pcall(body, name=name, out_shape=_sds((R, C)), grid=(R // tr,),
                  in_specs=[pl.BlockSpec((P, tr, C), lambda i: (0, i, 0))],
                  out_specs=pl.BlockSpec((tr, C), lambda i: (i, 0)), sem=("parallel",))(parts)


def _adamw(parts, w, m, v, name):
    P, R, C = parts.shape
    tr = R if R <= 256 else _tile_rows(R, 256)

    def body(p_ref, w_ref, m_ref, v_ref, g_ref, d_ref, nm_ref, nv_ref):
        g = p_ref[0]
        for k in range(1, P):
            g = g + p_ref[k]
        nm = ADAM_B1 * m_ref[...] + (1.0 - ADAM_B1) * g
        nv = ADAM_B2 * v_ref[...] + (1.0 - ADAM_B2) * (g * g)
        m_hat = nm / (1.0 - ADAM_B1 ** ADAM_STEP)
        v_hat = nv / (1.0 - ADAM_B2 ** ADAM_STEP)
        g_ref[...] = g
        d_ref[...] = -ADAM_LR * (m_hat / (jnp.sqrt(v_hat) + ADAM_EPS) + ADAM_WD * w_ref[...])
        nm_ref[...] = nm
        nv_ref[...] = nv

    blk = pl.BlockSpec((tr, C), lambda i: (i, 0))
    return _pcall(
        body, name=name, out_shape=[_sds((R, C))] * 4, grid=(R // tr,),
        in_specs=[pl.BlockSpec((P, tr, C), lambda i: (0, i, 0)), blk, blk, blk],
        out_specs=[blk] * 4, sem=("parallel",))(parts, w, m, v)


def _adam_update(g, w, m, v):
    nm = ADAM_B1 * m + (1.0 - ADAM_B1) * g
    nv = ADAM_B2 * v + (1.0 - ADAM_B2) * (g * g)
    m_hat = nm / (1.0 - ADAM_B1 ** ADAM_STEP)
    v_hat = nv / (1.0 - ADAM_B2 ** ADAM_STEP)
    return -ADAM_LR * (m_hat / (jnp.sqrt(v_hat) + ADAM_EPS) + ADAM_WD * w), nm, nv


class _Canvas:
    def __init__(self):
        self.offset, self.views, self.rows = {}, {}, 0

    def add(self, key, r, c, blocks=1):
        need = r // blocks if blocks > 1 else r * (-(-c // D))
        if need >= 8:
            self.rows = -(-self.rows // 8) * 8
        self.offset[key], self.views[key] = self.rows, (r, c, blocks)
        self.rows += need

    @property
    def total(self):
        return -(-self.rows // 8) * 8

    def cells(self, key):
        (r, c, blocks), off = self.views[key], self.offset[key]
        if blocks > 1:
            n = r // blocks
            return [(off, n, b * c, c, slice(b * n, (b + 1) * n), slice(0, c)) for b in range(blocks)]
        if c <= D:
            return [(off, r, 0, c, slice(0, r), slice(0, c))]
        chunks = -(-c // D)
        return [(off + i * chunks + j, 1, 0, min(D, c - j * D), slice(i, i + 1), slice(j * D, min(c, (j + 1) * D)))
                for i in range(r) for j in range(chunks)]


def _canvas_fill(canvas, sources, name):
    arrays = [a for a, _ in sources]

    def body(*refs):
        out_ref = refs[-1]
        out_ref[...] = jnp.zeros_like(out_ref)
        for ref, (_, items) in zip(refs[:-1], sources):
            for key, src_rows, view_row in items:
                n = src_rows.stop - src_rows.start
                for row, count, lane, width, vrows, vcols in canvas.cells(key):
                    lo, hi = max(vrows.start, view_row), min(vrows.stop, view_row + n)
                    if lo < hi:
                        src = slice(src_rows.start + lo - view_row, src_rows.start + hi - view_row)
                        out_ref[row + lo - vrows.start:row + hi - vrows.start, lane:lane + width] = ref[src, vcols]

    return _pcall(body, name=name, out_shape=_sds((canvas.total, D)))(*arrays)


def _adamw_canvas(canvas, parts, params, sum_only, name):
    n = len(params)

    def body(p_ref, *refs):
        sum_ref = refs[-1]
        g_all = p_ref[0]
        for k in range(1, N_DEV):
            g_all = g_all + p_ref[k]
        sum_ref[...] = g_all
        for i, (key, _, _, _) in enumerate(params):
            w_ref, m_ref, v_ref = refs[3 * i:3 * i + 3]
            outs = refs[3 * n + 4 * i:3 * n + 4 * i + 4]
            for row, count, lane, width, vrows, vcols in canvas.cells(key):
                g = sum_ref[row:row + count, lane:lane + width]
                delta, nm, nv = _adam_update(g, w_ref[vrows, vcols], m_ref[vrows, vcols], v_ref[vrows, vcols])
                for ref, val in zip(outs, (g, delta, nm, nv)):
                    ref[vrows, vcols] = val
        for i, key in enumerate(sum_only):
            for row, count, lane, width, vrows, vcols in canvas.cells(key):
                refs[7 * n + i][vrows, vcols] = sum_ref[row:row + count, lane:lane + width]

    args = [a for _, w, m, v in params for a in (w, m, v)]
    out_shape = [_sds(w.shape) for _, w, _, _ in params for _ in range(4)] + [_sds(canvas.views[k][:2]) for k in sum_only]
    res = _pcall(body, name=name, out_shape=out_shape, scratch=[pltpu.VMEM(parts.shape[1:], F32)])(parts, *args)
    out = {key: res[4 * i:4 * i + 4] for i, (key, _, _, _) in enumerate(params)}
    out.update({key: [res[4 * n + i]] for i, key in enumerate(sum_only)})
    return out


def _tile_rows(n, pref):
    best = None
    for t in range(8, pref + 1, 8):
        if n % t == 0:
            best = t
    assert best is not None, (n, pref)
    return best


def _mm(pairs, mode, *, name, out_dtype=F32, bias=None, tn_pref=1024, side=None):
    M = pairs[0][0].shape[0]
    N = pairs[0][1].shape[1] if mode == "nn" else pairs[0][1].shape[0]
    tm, tn = _rows(M), _tile(N, tn_pref)
    n_pairs = len(pairs)
    has_bias = bias is not None

    def body(*refs):
        acc = _pairs_dot(refs[:2 * n_pairs], mode)
        if has_bias:
            acc = acc + refs[2 * n_pairs][...]
        refs[-1][...] = acc.astype(refs[-1].dtype)

    in_specs, args = _pair_specs(pairs, mode, tm, tn)
    if has_bias:
        in_specs.append(pl.BlockSpec((1, tn), lambda j, i: (0, j)))
        args.append(bias)
    res = _call(body, args, side, name=name, out_shape=[_sds((M, N), out_dtype)], grid=(N // tn, M // tm),
                in_specs=in_specs, out_specs=[pl.BlockSpec((tm, tn), lambda j, i: (i, j))], sem=("parallel", "parallel"))
    return res[0] if side is None else (res[0], res[1])


def _pairs_dot(ab, mode):
    acc = None
    for p in range(len(ab) // 2):
        a, b = _bf(ab[2 * p][...]), _bf(ab[2 * p + 1][...])
        d = _nn(a, b) if mode == "nn" else _nt(a, b)
        acc = d if acc is None else acc + d
    return acc


def _pair_specs(pairs, mode, tm, tn):
    in_specs, args = [], []
    for a, b in pairs:
        K = a.shape[1]
        in_specs.append(pl.BlockSpec((tm, K), lambda j, i: (i, 0)))
        if mode == "nn":
            in_specs.append(pl.BlockSpec((K, tn), lambda j, i: (0, j)))
        else:
            in_specs.append(pl.BlockSpec((tn, K), lambda j, i: (j, 0)))
        args += [a, b]
    return in_specs, args


def _mm_resid(pairs, resid, gvec, *, name, bias=None, norm=None):
    M, N = resid.shape
    tm = _rows(M)
    n_pairs = len(pairs)
    has_bias, has_norm = bias is not None, norm is not None

    def body(*refs):
        acc = _pairs_dot(refs[:2 * n_pairs], "nn")
        pos = 2 * n_pairs
        if has_bias:
            acc = acc + refs[pos][...]
            pos += 1
        xv = refs[pos][...] + refs[pos + 1][...] * acc
        outs = refs[pos + 2 + 3 * has_norm:]
        outs[0][...] = xv
        outs[1][...] = acc.astype(BF16)
        if has_norm:
            w_ref, sc_ref, sh_ref = refs[pos + 2:pos + 5]
            r = lax.rsqrt(jnp.mean(xv * xv, axis=-1, keepdims=True) + EPS)
            outs[2][...] = ((xv * r * w_ref[...]) * (1.0 + sc_ref[...]) + sh_ref[...]).astype(BF16)

    in_specs, args = _pair_specs(pairs, "nn", tm, N)
    vec = pl.BlockSpec((1, N), lambda j, i: (0, 0))
    row = pl.BlockSpec((tm, N), lambda j, i: (i, 0))
    if has_bias:
        in_specs.append(vec)
        args.append(bias)
    in_specs += [row, vec] + [vec] * (3 * has_norm)
    args += [resid, gvec] + (list(norm) if has_norm else [])
    return _pcall(body, name=name, out_shape=[_sds((M, N)), _sds((M, N), BF16)] + [_sds((M, N), BF16)] * has_norm,
                  grid=(1, M // tm), in_specs=in_specs, out_specs=[row] * (2 + has_norm),
                  sem=("parallel", "parallel"))(*args)


def _mm_tn(a, b, *, name, tm_pref=1408, tn_pref=1536):
    K, M = a.shape
    N = b.shape[1]
    tm, tn = _tile(M, tm_pref), _tile(N, tn_pref)
    tk = K if K <= 2 * ROW_TILE else 2 * ROW_TILE

    def body(a_ref, b_ref, o_ref):
        @pl.when(pl.program_id(2) == 0)
        def _():
            o_ref[...] = jnp.zeros_like(o_ref)
        o_ref[...] += _tn(_bf(a_ref[...]), _bf(b_ref[...]))

    return _pcall(
        body, name=name, out_shape=_sds((M, N)), grid=(M // tm, N // tn, K // tk),
        in_specs=[pl.BlockSpec((tk, tm), lambda i, j, k: (k, i)), pl.BlockSpec((tk, tn), lambda i, j, k: (k, j))],
        out_specs=pl.BlockSpec((tm, tn), lambda i, j, k: (i, j)),
        sem=("parallel", "parallel", "arbitrary"))(a, b)


def _mm_swiglu(h, wg_t, wu_t, name, side=None):
    M, K = h.shape
    N = wg_t.shape[0]
    tm, tn = _rows(M), _tile(N, 1408)

    def body(h_ref, wg_ref, wu_ref, gate_ref, up_ref, act_ref):
        hv = _bf(h_ref[...])
        gate = _nt(hv, wg_ref[...])
        up = _nt(hv, wu_ref[...])
        gate_ref[...] = gate.astype(BF16)
        up_ref[...] = up.astype(BF16)
        act_ref[...] = (_silu(gate) * up).astype(BF16)

    w_spec = pl.BlockSpec((tn, K), lambda j, i: (j, 0))
    o_spec = pl.BlockSpec((tm, tn), lambda j, i: (i, j))
    return _call(body, (h, wg_t, wu_t), side, name=name,
                 out_shape=[_sds((M, N), BF16)] * 3, grid=(N // tn, M // tm),
                 in_specs=[pl.BlockSpec((tm, K), lambda j, i: (i, 0)), w_spec, w_spec], out_specs=[o_spec] * 3,
                 sem=("parallel", "parallel"))


def _mm_swiglu_bwd(dout, wd, gate, up, name, side=None):
    M, K = dout.shape
    N = wd.shape[0]
    tm, tn = _rows(M), _tile(N, 1408)

    def body(d_ref, wd_ref, gate_ref, up_ref, dg_ref, du_ref):
        dact = _nt(_bf(d_ref[...]), wd_ref[...])
        act, dact_dg = _silu_and_grad(gate_ref[...].astype(F32))
        dg_ref[...] = (dact * up_ref[...].astype(F32) * dact_dg).astype(BF16)
        du_ref[...] = (dact * act).astype(BF16)

    t_spec = pl.BlockSpec((tm, tn), lambda j, i: (i, j))
    return _call(
        body, (dout, wd, gate, up), side, name=name, out_shape=[_sds((M, N), BF16)] * 2, grid=(N // tn, M // tm),
        in_specs=[pl.BlockSpec((tm, K), lambda j, i: (i, 0)), pl.BlockSpec((tn, K), lambda j, i: (j, 0)), t_spec, t_spec],
        out_specs=[t_spec] * 2, sem=("parallel", "parallel"))


def _norm_mod(x, w, sc, sh, name, side=None):
    T = x.shape[0]
    tm = _rows(T)

    def body(x_ref, w_ref, sc_ref, sh_ref, o_ref):
        xv = x_ref[...]
        r = lax.rsqrt(jnp.mean(xv * xv, axis=-1, keepdims=True) + EPS)
        o_ref[...] = ((xv * r * w_ref[...]) * (1.0 + sc_ref[...]) + sh_ref[...]).astype(BF16)

    return _call(body, (x, w, sc, sh), side, name=name, out_shape=[_sds((T, D), BF16)], grid=(T // tm,),
                 in_specs=[_row_spec(tm, D), _vec_spec(D), _vec_spec(D), _vec_spec(D)],
                 out_specs=[_row_spec(tm, D)], sem=("parallel",))


def _gate_rows(dxv, br_ref, g_ref, db_ref, acc_ref):
    db = g_ref[...] * dxv
    db_ref[...] = db.astype(BF16)
    acc_ref[3:4, :] += _colsum(dxv * br_ref[...].astype(F32))
    acc_ref[4:5, :] += _colsum(db)


def _norm_mod_bwd(x, dh, dres, w, sc, branch, g, name, side=None):
    T = x.shape[0]
    tm = _rows(T)
    gated = branch is not None

    def body(x_ref, dh_ref, dres_ref, w_ref, sc_ref, *rest):
        (br_ref, g_ref, dx_ref, db_ref, acc_ref) = rest if gated else (None, None, rest[0], None, rest[1])

        @pl.when(pl.program_id(0) == 0)
        def _():
            acc_ref[...] = jnp.zeros_like(acc_ref)
        xv, dh_v, wv = x_ref[...], dh_ref[...], w_ref[...]
        r = lax.rsqrt(jnp.mean(xv * xv, axis=-1, keepdims=True) + EPS)
        n = xv * r
        dnw = dh_v * (1.0 + sc_ref[...])
        dn = dnw * wv
        dxv = dres_ref[...] + r * (dn - n * jnp.mean(dn * n, axis=-1, keepdims=True))
        dx_ref[...] = dxv
        acc_ref[0:1, :] += _colsum(dh_v * (n * wv))
        acc_ref[1:2, :] += _colsum(dh_v)
        acc_ref[2:3, :] += _colsum(dnw * n)
        if gated:
            _gate_rows(dxv, br_ref, g_ref, db_ref, acc_ref)

    row = _row_spec(tm, D)
    args = (x, dh, dres, w, sc) + ((branch, g) if gated else ())
    return _call(body, args, side, name=name,
                 out_shape=[_sds((T, D))] + ([_sds((T, D), BF16)] if gated else []) + [_sds((8, D))], grid=(T // tm,),
                 in_specs=[row, row, row, _vec_spec(D), _vec_spec(D)] + ([row, _vec_spec(D)] if gated else []),
                 out_specs=[row] + ([row] if gated else []) + [_vec_spec(D, 8)], sem=("arbitrary",))


def _final_loss(x, wf, target, branch, g, name):
    T = x.shape[0]
    tm = _rows(T)

    def body(x_ref, w_ref, t_ref, br_ref, g_ref, dx_ref, db_ref, acc_ref):
        @pl.when(pl.program_id(0) == 0)
        def _():
            acc_ref[...] = jnp.zeros_like(acc_ref)
        xv, wv = x_ref[...], w_ref[...]
        r = lax.rsqrt(jnp.mean(xv * xv, axis=-1, keepdims=True) + EPS)
        n = xv * r
        err = n * wv - t_ref[...]
        dy = err * (1.0 / D)
        dn = dy * wv
        dxv = r * (dn - n * jnp.mean(dn * n, axis=-1, keepdims=True))
        dx_ref[...] = dxv
        acc_ref[0:1, :] += _colsum(dy * n)
        acc_ref[1:2, :] += jnp.broadcast_to(_allsum(err * err) * (0.5 / D), (1, D))
        _gate_rows(dxv, br_ref, g_ref, db_ref, acc_ref)

    row = _row_spec(tm, D)
    return _pcall(body, name=name, out_shape=[_sds((T, D)), _sds((T, D), BF16), _sds((8, D))], grid=(T // tm,),
                  in_specs=[row, _vec_spec(D), row, row, _vec_spec(D)],
                  out_specs=[row, row, _vec_spec(D, 8)], sem=("arbitrary",))(x, wf, target, branch, g)


def _mod_matmul(c_all, ada_w, name):
    n = ada_w.shape[2]

    def body(c_ref, w_ref, cond_ref, o_ref):
        cond = _silu(c_ref[...])
        cond_ref[...] = cond
        o_ref[0] = _nn(cond, w_ref[0])

    return _pcall(body, name=name, out_shape=[_sds((N_DEV, D)), _sds((DEPTH, N_DEV, n))], grid=(DEPTH,),
                  in_specs=[pl.BlockSpec((N_DEV, D), lambda l: (0, 0)), pl.BlockSpec((1, D, n), lambda l: (l, 0, 0))],
                  out_specs=[pl.BlockSpec((N_DEV, D), lambda l: (0, 0)), pl.BlockSpec((1, N_DEV, n), lambda l: (l, 0, 0))],
                  sem=("arbitrary",))(c_all, ada_w)


def _add_rows(a, b, name):
    def body(a_ref, b_ref, o_ref):
        o_ref[...] = a_ref[...] + b_ref[...]

    return _pcall(body, name=name, out_shape=_sds(a.shape))(a, b)


def _ada_w_grad(cond, dmod_cols, name):
    n = dmod_cols.shape[2]

    def body(c_ref, d_ref, o_ref):
        o_ref[0] = _tn(c_ref[...], d_ref[0])

    return _pcall(body, name=name, out_shape=_sds((DEPTH, D, n)), grid=(DEPTH,),
                  in_specs=[pl.BlockSpec((N_DEV, D), lambda l: (0, 0)), pl.BlockSpec((1, N_DEV, n), lambda l: (l, 0, 0))],
                  out_specs=pl.BlockSpec((1, D, n), lambda l: (l, 0, 0)), sem=("parallel",))(cond, dmod_cols)


def _conv_fwd(pm, conv_w, conv_b, name):
    T = pm.shape[0]
    tm = _rows(T)
    C = CONV_DIM

    def body(x_ref, prev_ref, w_ref, b_ref, o_ref):
        cur = x_ref[...].astype(F32)
        prev = jnp.where(pl.program_id(0) > 0, prev_ref[...].astype(F32)[8:16], 0.0)
        cur8 = cur[0:8]
        row8 = lax.broadcasted_iota(jnp.int32, (8, C), 0)
        full = w_ref[3:4, :] * cur
        head = w_ref[3:4, :] * cur8
        for k in range(1, SSM_CONV):
            wk = w_ref[3 - k:4 - k, :]
            full = full + wk * pltpu.roll(cur, k, 0)
            head = head + wk * jnp.where(row8 < k, pltpu.roll(prev, k, 0), pltpu.roll(cur8, k, 0))
        o_ref[...] = full + b_ref[...]
        o_ref[0:8, :] = head + b_ref[...]

    return _pcall(
        body, name=name, out_shape=_sds((T, C)), grid=(T // tm,),
        in_specs=[pl.BlockSpec((tm, C), lambda i: (i, 2)),
                  pl.BlockSpec((16, C), lambda i: (jnp.maximum(i * (tm // 16) - 1, 0), 2)),
                  _vec_spec(C, SSM_CONV), _vec_spec(C)],
        out_specs=_row_spec(tm, C), sem=("parallel",))(pm, pm, conv_w, conv_b)


def _conv_bwd(dc, pm, conv_w, name, side=None):
    T = dc.shape[0]
    tm = _rows(T)
    C = CONV_DIM
    nt = T // tm

    def body(dc_ref, nxt_ref, x_ref, w_ref, dx_ref, acc_ref):
        i = pl.program_id(0)

        @pl.when(i == 0)
        def _():
            acc_ref[...] = jnp.zeros_like(acc_ref)
        dcv = dc_ref[...]
        nxt = jnp.where(i < nt - 1, nxt_ref[...], 0.0)
        xc = x_ref[...].astype(F32)
        dc8t, x8t = dcv[tm - 8:tm], xc[tm - 8:tm]
        row8 = lax.broadcasted_iota(jnp.int32, (8, C), 0)
        full = w_ref[3:4, :] * dcv
        tail = w_ref[3:4, :] * dc8t
        acc_ref[3:4, :] += _colsum(dcv * xc)
        for k in range(1, SSM_CONV):
            wk = w_ref[3 - k:4 - k, :]
            up = pltpu.roll(dcv, tm - k, 0)
            up_tail = jnp.where(row8 + k >= 8, pltpu.roll(nxt, 8 - k, 0), pltpu.roll(dc8t, 8 - k, 0))
            full = full + wk * up
            tail = tail + wk * up_tail
            prod = up * xc
            acc_ref[3 - k:4 - k, :] += _colsum(prod) - _colsum(prod[tm - 8:tm]) + _colsum(up_tail * x8t)
        acc_ref[4:5, :] += _colsum(dcv)
        dx_ref[...] = jnp.concatenate([full[0:tm - 8], tail], axis=0).astype(BF16)

    return _call(
        body, (dc, dc, pm, conv_w), side, name=name, out_shape=[_sds((T, C), BF16), _sds((8, C))], grid=(nt,),
        in_specs=[_row_spec(tm, C),
                  pl.BlockSpec((8, C), lambda i: (jnp.minimum((i + 1) * (tm // 8), T // 8 - 1), 0)),
                  pl.BlockSpec((tm, C), lambda i: (i, 2)),
                  _vec_spec(C, SSM_CONV)],
        out_specs=[_row_spec(tm, C), _vec_spec(C, 8)], sem=("arbitrary",))


def _ssd_prologue(cpre, dtr, dtb, alog):
    L = CHUNK
    xc = _silu(cpre)
    pre = dtr + dtb
    dt = jnp.maximum(pre, 0.0) + jnp.log1p(jnp.exp(-jnp.abs(pre)))
    a = -jnp.exp(alog)
    la = dt * a
    row = lax.broadcasted_iota(jnp.int32, (L, L), 0)
    col = lax.broadcasted_iota(jnp.int32, (L, L), 1)
    causal = row >= col
    tri = causal.astype(F32)
    lc = _nn(tri, la, HIGHEST)
    return xc, pre, dt, a, causal, tri, lc, row, col


def _head_indicator():
    m = np.zeros((LANES, SSM_INNER), np.float32)
    for h in range(SSM_HEADS):
        m[h, h * SSM_HEAD_DIM:(h + 1) * SSM_HEAD_DIM] = 1.0
    return jnp.asarray(m, dtype=BF16)


def _split_dot(x, ind, dims):
    hi = x.astype(BF16)
    lo = (x - hi.astype(F32)).astype(BF16)
    return _dot(hi, ind, dims) + _dot(lo, ind, dims)


def _expand(x16, ind):
    return _split_dot(x16, ind, ((1,), (0,)))


def _headsum(x, ind, single_pass=False):
    if single_pass:
        return _dot(x.astype(BF16), ind, ((1,), (1,)))
    return _split_dot(x, ind, ((1,), (1,)))


def _ssd_fwd(cpre, dtr, pm, dtb, alog, dskip, normw, ind, name, side=None):
    T = cpre.shape[0]
    nc = T // CHUNK
    L, P, H, HPG, N = CHUNK, SSM_HEAD_DIM, SSM_HEADS, SSM_HEADS // SSM_GROUPS, SSM_STATE
    half = SSM_INNER // SSM_GROUPS

    def body(cp_ref, dtr_ref, z_ref, dtb_ref, alog_ref, dskip_ref, nw_ref, ind_ref, ya_ref, y_ref, sp_ref, st_ref):
        @pl.when(pl.program_id(0) == 0)
        def _():
            st_ref[...] = jnp.zeros_like(st_ref)
        xc, _, dt, _, causal, _, lc, _, _ = _ssd_prologue(cp_ref[...], dtr_ref[...], dtb_ref[...], alog_ref[...])
        lct = lc.T
        ind = ind_ref[...]
        llast = lc[L - 1:L, :]
        xs = xc[:, :SSM_INNER]
        xd = xs * _expand(dt, ind)
        ex = _expand(jnp.exp(lc), ind)
        xd_end = _bf(xd * _expand(jnp.exp(llast - lc), ind))
        cdx = _expand(jnp.broadcast_to(jnp.exp(llast), (8, LANES)), ind)[0:1]
        xdb = _bf(xd)
        sp_ref[0] = st_ref[...]
        for g in range(SSM_GROUPS):
            sl = slice(g * half, (g + 1) * half)
            bm = _bf(xc[:, SSM_INNER + g * N:SSM_INNER + (g + 1) * N])
            cm = _bf(xc[:, SSM_INNER + (SSM_GROUPS + g) * N:SSM_INNER + (SSM_GROUPS + g + 1) * N])
            cb = _nt(cm, bm)
            st = st_ref[g]
            y_ref[:, sl] = ex[:, sl] * _nn(cm, _bf(st)) + dskip_ref[:, sl] * xs[:, sl]
            st_ref[g] = st * cdx[:, sl] + _tn(bm, xd_end[:, sl])
            for j in range(HPG):
                h = g * HPG + j
                decay = jnp.where(causal, jnp.exp(jnp.where(causal, lc[:, h:h + 1] - lct[h:h + 1, :], 0.0)), 0.0)
                y_ref[:, h * P:(h + 1) * P] += _nn(_bf(cb * decay), xdb[:, h * P:(h + 1) * P])
        y2 = y_ref[...] * _silu(z_ref[...].astype(F32))
        for g in range(SSM_GROUPS):
            yg = y2[:, g * half:(g + 1) * half]
            r = lax.rsqrt(jnp.mean(yg * yg, axis=-1, keepdims=True) + EPS)
            ya_ref[:, g * half:(g + 1) * half] = (yg * r * nw_ref[:, g * half:(g + 1) * half]).astype(BF16)

    return _call(
        body, (cpre, dtr, pm, dtb, alog, dskip, normw, ind), side, name=name,
        out_shape=[_sds((T, SSM_INNER), BF16), _sds((T, SSM_INNER)), _sds((nc, SSM_GROUPS, N, half))], grid=(nc,),
        in_specs=[_row_spec(L, CONV_DIM), _row_spec(L, LANES), _row_spec(L, SSM_INNER, 0),
                  _vec_spec(LANES), _vec_spec(LANES), _vec_spec(SSM_INNER), _vec_spec(SSM_INNER), _vec_spec(SSM_INNER, LANES)],
        out_specs=[_row_spec(L, SSM_INNER), _row_spec(L, SSM_INNER),
                   pl.BlockSpec((1, SSM_GROUPS, N, half), lambda i: (i, 0, 0, 0))],
        scratch=[pltpu.VMEM((SSM_GROUPS, N, half), F32)], sem=("arbitrary",))


def _ssd_bwd(cpre, dtr, pm, ypre, sprev, dya, dtb, alog, dskip, normw, ind, name, side=None):
    T = cpre.shape[0]
    nc = T // CHUNK
    L, P, H, HPG, N = CHUNK, SSM_HEAD_DIM, SSM_HEADS, SSM_HEADS // SSM_GROUPS, SSM_STATE
    half = SSM_INNER // SSM_GROUPS

    def body(cp_ref, dtr_ref, z_ref, y_ref, sp_ref, dya_ref, dtb_ref, alog_ref, dskip_ref, nw_ref, ind_ref,
             dz_ref, dcp_ref, ddtr_ref, acc_ref, dnw_ref, ds_ref, dy_ref, dxd_ref, rr_ref, yoff_ref, dcd_ref):
        @pl.when(pl.program_id(0) == 0)
        def _():
            ds_ref[...] = jnp.zeros_like(ds_ref)
            acc_ref[...] = jnp.zeros_like(acc_ref)
            dnw_ref[...] = jnp.zeros_like(dnw_ref)
        cpre_v = cp_ref[...]
        xc, pre, dt, a, causal, tri, lc, row, col = _ssd_prologue(cpre_v, dtr_ref[...], dtb_ref[...], alog_ref[...])
        lct = lc.T
        zv, yv = z_ref[...].astype(F32), y_ref[...]
        sz, dsz = _silu_and_grad(zv)
        y2 = yv * sz
        dya_v = dya_ref[...]
        nwv = nw_ref[...]
        for g in range(SSM_GROUPS):
            sl = slice(g * half, (g + 1) * half)
            yg = y2[:, sl]
            r = lax.rsqrt(jnp.mean(yg * yg, axis=-1, keepdims=True) + EPS)
            nrm = yg * r
            dnw_ref[:, sl] += _colsum(dya_v[:, sl] * nrm)
            dn = dya_v[:, sl] * nwv[:, sl]
            dy2 = r * (dn - nrm * jnp.mean(dn * nrm, axis=-1, keepdims=True))
            dy_ref[:, sl] = dy2 * sz[:, sl]
            dz_ref[:, sl] = (dy2 * yv[:, sl] * dsz[:, sl]).astype(BF16)
        ind = ind_ref[...]
        llast = lc[L - 1:L, :]
        dte16 = jnp.exp(llast - lc)
        cd16 = jnp.exp(llast)
        xs = xc[:, :SSM_INNER]
        dtx = _expand(dt, ind)
        ex = _expand(jnp.exp(lc), ind)
        dtex = _expand(dte16, ind)
        cdx = _expand(jnp.broadcast_to(cd16, (8, LANES)), ind)[0:1]
        xd = xs * dtx
        xdb = _bf(xd)
        xd_end = _bf(xd * dtex)
        dyv = dy_ref[...]
        dy_off = _bf(ex * dyv)
        dyb = _bf(dyv)
        dskx = dskip_ref[...]
        lane_c = lax.broadcasted_iota(jnp.int32, (L, LANES), 1)
        lane1 = lax.broadcasted_iota(jnp.int32, (1, LANES), 1)
        sub16 = lax.broadcasted_iota(jnp.int32, (H, L), 0)
        dlc_c = jnp.zeros((L, LANES), F32)
        dlc_r = jnp.zeros((H, L), F32)
        for g in range(SSM_GROUPS):
            sl = slice(g * half, (g + 1) * half)
            b_lo = SSM_INNER + g * N
            c_lo = SSM_INNER + (SSM_GROUPS + g) * N
            bm, cm = _bf(xc[:, b_lo:b_lo + N]), _bf(xc[:, c_lo:c_lo + N])
            cb = _nt(cm, bm)
            st, dst = sp_ref[0, g], ds_ref[g]
            stb, dstb = _bf(st), _bf(dst)
            dcm = _nt(dy_off[:, sl], stb)
            ds_ref[g] = _tn(cm, dy_off[:, sl]) + dst * cdx[:, sl]
            rr_ref[:, sl] = _nn(bm, dstb)
            yoff_ref[:, sl] = ex[:, sl] * _nn(cm, stb)
            db = _nt(xd_end[:, sl], dstb)
            dcd_ref[:, sl] = _colsum(dst * st)
            dcb = jnp.zeros((L, L), F32)
            for j in range(HPG):
                h = g * HPG + j
                hs = slice(h * P, (h + 1) * P)
                decay = jnp.where(causal, jnp.exp(jnp.where(causal, lc[:, h:h + 1] - lct[h:h + 1, :], 0.0)), 0.0)
                m = cb * decay
                dxd_ref[:, hs] = _tn(_bf(m), dyb[:, hs])
                dm = _nt(dyb[:, hs], xdb[:, hs])
                dcb = dcb + dm * decay
                gm = dm * m
                dlc_c = dlc_c + jnp.where(lane_c == h, _rowsum(gm), 0.0)
                dlc_r = dlc_r + jnp.where(sub16 == h, _colsum(gm), 0.0)
            dcbb = _bf(dcb)
            dcp_ref[:, c_lo:c_lo + N] = dcm + _nn(dcbb, bm)
            dcp_ref[:, b_lo:b_lo + N] = db + _tn(dcbb, cm)
        dxd_diag, rr = dxd_ref[...], rr_ref[...]
        tt = _headsum(rr * xd, ind, single_pass=True) * dte16
        dlc_rt = jnp.concatenate([dlc_r, jnp.zeros((LANES - H, L), F32)], axis=0).T
        dlc = dlc_c - dlc_rt + _headsum(dyv * yoff_ref[...], ind, single_pass=True) - tt
        dcd = _headsum(jnp.broadcast_to(dcd_ref[...], (8, SSM_INNER)), ind)[0:1]
        dlc = dlc + jnp.where(row == L - 1, _colsum(tt) + dcd * cd16, 0.0)
        dla = _tn(tri, dlc, HIGHEST)
        dxd = dxd_diag + dtex * rr
        ddt = _headsum(dxd * xs, ind, single_pass=True) + dla * a
        ddtr = jnp.where(lane_c < H, ddt * _sigmoid(pre), 0.0)
        ddtr_ref[...] = ddtr
        acc_ref[0:1, :] += _colsum(ddtr)
        acc_ref[1:2, :] += jnp.where(lane1 < H, _colsum(dla * dt) * a, 0.0)
        acc_ref[2:3, :] += _headsum(jnp.broadcast_to(_colsum(dyv * xs), (8, SSM_INNER)), ind)[0:1]
        dcp_ref[:, 0:SSM_INNER] = dxd * dtx + dskx * dyv
        dcp_ref[...] = dcp_ref[...] * _dsilu(cpre_v)

    rev = lambda i: (nc - 1 - i, 0)
    rspec = lambda c: pl.BlockSpec((L, c), rev)
    return _call(
        body, (cpre, dtr, pm, ypre, sprev, dya, dtb, alog, dskip, normw, ind), side, name=name,
        out_shape=[_sds((T, SSM_INNER), BF16), _sds((T, CONV_DIM)), _sds((T, LANES)), _sds((8, LANES)), _sds((1, SSM_INNER))],
        grid=(nc,),
        in_specs=[rspec(CONV_DIM), rspec(LANES), rspec(SSM_INNER), rspec(SSM_INNER),
                  pl.BlockSpec((1, SSM_GROUPS, N, half), lambda i: (nc - 1 - i, 0, 0, 0)), rspec(SSM_INNER),
                  _vec_spec(LANES), _vec_spec(LANES), _vec_spec(SSM_INNER), _vec_spec(SSM_INNER), _vec_spec(SSM_INNER, LANES)],
        out_specs=[rspec(SSM_INNER), rspec(CONV_DIM), rspec(LANES), _vec_spec(LANES, 8), _vec_spec(SSM_INNER)],
        scratch=[pltpu.VMEM((SSM_GROUPS, N, half), F32), pltpu.VMEM((L, SSM_INNER), F32), pltpu.VMEM((L, SSM_INNER), F32),
                 pltpu.VMEM((L, SSM_INNER), F32), pltpu.VMEM((L, SSM_INNER), F32), pltpu.VMEM((1, SSM_INNER), F32)],
        sem=("arbitrary",))


def _gmlp_common(u, v, lnw, lnb, with_grads=False):
    (ug, dug), (vg, dvg) = (_gelu_and_grad(u), _gelu_and_grad(v)) if with_grads else ((_gelu(u), None), (_gelu(v), None))
    mu = jnp.mean(vg, axis=-1, keepdims=True)
    cen = vg - mu
    rstd = lax.rsqrt(jnp.mean(cen * cen, axis=-1, keepdims=True) + EPS)
    vhat = cen * rstd
    out = (ug, rstd, vhat, vhat * lnw + lnb)
    return out + (dug, dvg) if with_grads else out


def _causal_mask():
    row = lax.broadcasted_iota(jnp.int32, (CHUNK, CHUNK), 0)
    col = lax.broadcasted_iota(jnp.int32, (CHUNK, CHUNK), 1)
    return row >= col


def _gmlp_fwd(pm, lnw, lnb, ws, bs_exp, name, side=None):
    T = pm.shape[0]
    nc = T // CHUNK
    L, G = CHUNK, GMLP_GROUPS

    def body(u_ref, v_ref, lnw_ref, lnb_ref, ws_ref, bs_ref, o_ref):
        ug, _, _, vn = _gmlp_common(u_ref[...].astype(F32), v_ref[...].astype(F32), lnw_ref[...], lnb_ref[...])
        causal = _causal_mask()
        for g in range(G):
            sl = slice(g * L, (g + 1) * L)
            wm = _bf(jnp.where(causal, ws_ref[g], 0.0))
            sv = _nn(wm, _bf(vn[:, sl])) + bs_ref[:, sl]
            o_ref[:, sl] = (ug[:, sl] * sv).astype(BF16)

    return _call(
        body, (pm, pm, lnw, lnb, ws, bs_exp), side, name=name, out_shape=[_sds((T, GMLP_INNER), BF16)], grid=(nc,),
        in_specs=[_row_spec(L, GMLP_INNER, 1), _row_spec(L, GMLP_INNER, 2), _vec_spec(GMLP_INNER), _vec_spec(GMLP_INNER),
                  pl.BlockSpec((G, L, L), lambda i: (0, 0, 0)), _vec_spec(GMLP_INNER, L)],
        out_specs=[_row_spec(L, GMLP_INNER)], sem=("parallel",))


def _gmlp_bwd(pm, dyb, lnw, lnb, ws, bs_exp, name, side=None):
    T = pm.shape[0]
    nc = T // CHUNK
    L, G = CHUNK, GMLP_GROUPS

    def body(u_ref, v_ref, dy_ref, lnw_ref, lnb_ref, ws_ref, bs_ref, du_ref, dv_ref, dws_ref, dbs_ref, acc_ref, dvn_ref):
        @pl.when(pl.program_id(0) == 0)
        def _():
            dws_ref[...] = jnp.zeros_like(dws_ref)
            dbs_ref[...] = jnp.zeros_like(dbs_ref)
            acc_ref[...] = jnp.zeros_like(acc_ref)
        uv, vv, dyv, lnwv = u_ref[...].astype(F32), v_ref[...].astype(F32), dy_ref[...], lnw_ref[...]
        ug, rstd, vhat, vn, dug, dvg_act = _gmlp_common(uv, vv, lnwv, lnb_ref[...], with_grads=True)
        causal = _causal_mask()
        lane = lax.broadcasted_iota(jnp.int32, (L, LANES), 1)
        dbs = jnp.zeros((L, LANES), F32)
        for g in range(G):
            sl = slice(g * L, (g + 1) * L)
            wm = _bf(jnp.where(causal, ws_ref[g], 0.0))
            vng = _bf(vn[:, sl])
            sv = _nn(wm, vng) + bs_ref[:, sl]
            du_ref[:, sl] = (dyv[:, sl] * sv * dug[:, sl]).astype(BF16)
            dsv = dyv[:, sl] * ug[:, sl]
            dsvb = _bf(dsv)
            dws_ref[g] += jnp.where(causal, _nt(dsvb, vng), 0.0)
            dbs = dbs + jnp.where(lane == g, _rowsum(dsv), 0.0)
            dvn_ref[:, sl] = _tn(wm, dsvb)
        dbs_ref[...] += dbs
        dvn = dvn_ref[...]
        acc_ref[0:1, :] += _colsum(dvn * vhat)
        acc_ref[1:2, :] += _colsum(dvn)
        dvh = dvn * lnwv
        dvg = rstd * (dvh - jnp.mean(dvh, axis=-1, keepdims=True) - vhat * jnp.mean(dvh * vhat, axis=-1, keepdims=True))
        dv_ref[...] = (dvg * dvg_act).astype(BF16)

    return _call(
        body, (pm, pm, dyb, lnw, lnb, ws, bs_exp), side, name=name,
        out_shape=[_sds((T, GMLP_INNER), BF16), _sds((T, GMLP_INNER), BF16), _sds((G, L, L)), _sds((L, LANES)), _sds((8, GMLP_INNER))],
        grid=(nc,),
        in_specs=[_row_spec(L, GMLP_INNER, 1), _row_spec(L, GMLP_INNER, 2), _row_spec(L, GMLP_INNER),
                  _vec_spec(GMLP_INNER), _vec_spec(GMLP_INNER), pl.BlockSpec((G, L, L), lambda i: (0, 0, 0)),
                  _vec_spec(GMLP_INNER, L)],
        out_specs=[_row_spec(L, GMLP_INNER), _row_spec(L, GMLP_INNER), pl.BlockSpec((G, L, L), lambda i: (0, 0, 0)),
                   _vec_spec(LANES, L), _vec_spec(GMLP_INNER, 8)],
        scratch=[pltpu.VMEM((L, GMLP_INNER), F32)], sem=("arbitrary",))


def _rel_buckets():
    qi = np.arange(CHUNK)[:, None]
    sj = np.arange(2 * CHUNK)[None, :]
    dist = np.maximum(qi + CHUNK - sj, 0)
    max_exact = REL_BUCKETS // 2
    log_ratio = (np.log(np.maximum(dist, 1).astype(np.float32) / np.float32(max_exact))
                 / np.float32(math.log(REL_MAX_DIST / max_exact))).astype(np.float32)
    large = max_exact + (log_ratio * np.float32(REL_BUCKETS - max_exact)).astype(np.int32)
    return np.where(dist < max_exact, dist, np.minimum(large, REL_BUCKETS - 1))


def _bucket_onehot_t():
    bucket = _rel_buckets().reshape(-1)
    return jnp.asarray((np.arange(REL_BUCKETS)[:, None] == bucket[None, :]).astype(np.float32))


def _bias_from_table(table_t, onehot_t, window, name):
    def body(t_ref, o_ref, w_ref, out_ref):
        out_ref[...] = jnp.where(w_ref[...] > 0.5, _nn(t_ref[...], o_ref[...], HIGHEST), NEG_INF)

    return _pcall(body, name=name, out_shape=_sds((ATTN_HEADS, onehot_t.shape[1])))(table_t, onehot_t, window)


def _table_from_dbias(dbias, onehot_t, name):
    def body(d_ref, o_ref, out_ref):
        out_ref[...] = _nt(o_ref[...], d_ref[...], HIGHEST)

    return _pcall(body, name=name, out_shape=_sds((REL_BUCKETS, ATTN_HEADS)))(dbias, onehot_t)


def _softmax_sink(logits, sink):
    mx = jnp.maximum(jnp.max(logits, axis=-1, keepdims=True), sink)
    e = jnp.exp(logits - mx)
    es = jnp.exp(sink - mx)
    inv = 1.0 / (_rowsum(e) + es)
    return e * inv, es * inv


def _first_block_penalty(n):
    sj = lax.broadcasted_iota(jnp.int32, (1, 2 * CHUNK), 1)
    return jnp.where((sj < CHUNK) & (n == 0), NEG_INF, 0.0)


def _window_mask_flat():
    qi = np.arange(CHUNK)[:, None]
    sj = np.arange(2 * CHUNK)[None, :]
    rel = qi + CHUNK - sj
    return jnp.asarray(((rel >= 0) & (rel < CHUNK)).astype(np.float32).reshape(1, -1))


def _stack_heads(ref, first, count, width, scale=None):
    x = jnp.concatenate([_bf(ref[:, (first + j) * width:(first + j + 1) * width]) for j in range(count)], axis=0)
    return x if scale is None else x * jnp.asarray(scale, x.dtype)


def _attn_fwd(qkv, bias, sinks, name):
    T = qkv.shape[0]
    nb = T // CHUNK
    L, DH, HPK = CHUNK, ATTN_DH, ATTN_HEADS // ATTN_KV
    scale = DH ** -0.5
    kcol, vcol = ATTN_HEADS * DH // LANES, ATTN_HEADS * DH // LANES + 1

    def body(q_ref, k_ref, v_ref, kp_ref, vp_ref, bias_ref, sink_ref, o_ref, lg_ref, p_ref):
        n = pl.program_id(0)
        pen = _first_block_penalty(n)
        kband = _bf(jnp.concatenate([kp_ref[...], k_ref[...]], axis=0))
        vband = _bf(jnp.concatenate([vp_ref[...], v_ref[...]], axis=0))
        for kv in range(ATTN_KV):
            lg_ref[...] = _nt(_stack_heads(q_ref, kv * HPK, HPK, DH, scale), kband[:, kv * DH:(kv + 1) * DH])
            for j in range(HPK):
                h = kv * HPK + j
                p, _ = _softmax_sink(lg_ref[j * L:(j + 1) * L, :] + bias_ref[h] + pen, sink_ref[h])
                p_ref[j * L:(j + 1) * L, :] = _bf(p)
            og = _nn(p_ref[...], vband[:, kv * DH:(kv + 1) * DH])
            for j in range(HPK):
                h = kv * HPK + j
                o_ref[:, h * DH:(h + 1) * DH] = og[j * L:(j + 1) * L].astype(BF16)

    prev = lambda i: jnp.maximum(i - 1, 0)
    return _pcall(
        body, name=name, out_shape=_sds((T, ATTN_HEADS * DH), BF16), grid=(nb,),
        in_specs=[_row_spec(L, ATTN_HEADS * DH, 0), _row_spec(L, LANES, kcol), _row_spec(L, LANES, vcol),
                  pl.BlockSpec((L, LANES), lambda i: (prev(i), kcol)), pl.BlockSpec((L, LANES), lambda i: (prev(i), vcol)),
                  pl.BlockSpec((ATTN_HEADS, L, 2 * L), lambda i: (0, 0, 0)),
                  pl.BlockSpec(memory_space=pltpu.SMEM)],
        out_specs=_row_spec(L, ATTN_HEADS * DH),
        scratch=[pltpu.VMEM((HPK * L, 2 * L), F32), pltpu.VMEM((HPK * L, 2 * L), BF16)],
        sem=("parallel",))(qkv, qkv, qkv, qkv, qkv, bias, sinks)


def _attn_bwd(qkv, datt, bias, sinks, name, side=None):
    T = qkv.shape[0]
    nb = T // CHUNK
    L, DH, HPK = CHUNK, ATTN_DH, ATTN_HEADS // ATTN_KV
    scale = DH ** -0.5
    kcol, vcol = ATTN_HEADS * DH // LANES, ATTN_HEADS * DH // LANES + 1

    def body(q_ref, k_ref, v_ref, kp_ref, vp_ref, do_ref, bias_ref, sink_ref,
             dq_ref, dk_ref, dv_ref, bsum_ref, dbias_ref, dsink_ref, pend_k, pend_v, band_k, band_v, lg_ref, dp_ref, p_ref, dl_ref):
        n = pl.program_id(0)

        @pl.when(n == 0)
        def _():
            dbias_ref[...] = jnp.zeros_like(dbias_ref)
            dsink_ref[...] = jnp.zeros_like(dsink_ref)
            bsum_ref[...] = jnp.zeros_like(bsum_ref)

        def emit_kv(dk, dv):
            dk_ref[...] = dk.astype(BF16)
            dv_ref[...] = dv.astype(BF16)
            bsum_ref[:, ATTN_HEADS * DH:ATTN_HEADS * DH + LANES] += _colsum(dk)
            bsum_ref[:, ATTN_HEADS * DH + LANES:] += _colsum(dv)

        @pl.when(n < nb)
        def _():
            pen = _first_block_penalty(n)
            kband = _bf(jnp.concatenate([kp_ref[...], k_ref[...]], axis=0))
            vband = _bf(jnp.concatenate([vp_ref[...], v_ref[...]], axis=0))
            lane1 = lax.broadcasted_iota(jnp.int32, (1, LANES), 1)
            dsink = jnp.zeros((1, LANES), F32)
            for kv in range(ATTN_KV):
                kb, vb = kband[:, kv * DH:(kv + 1) * DH], vband[:, kv * DH:(kv + 1) * DH]
                qg = _stack_heads(q_ref, kv * HPK, HPK, DH, scale)
                dog = _stack_heads(do_ref, kv * HPK, HPK, DH)
                lg_ref[...] = _nt(qg, kb)
                dp_ref[...] = _nt(dog, vb)
                for j in range(HPK):
                    h = kv * HPK + j
                    rows = slice(j * L, (j + 1) * L)
                    p, ps = _softmax_sink(lg_ref[rows, :] + bias_ref[h] + pen, sink_ref[h])
                    dp = dp_ref[rows, :]
                    delta = _rowsum(p * dp)
                    dl = p * (dp - delta)
                    dbias_ref[h] += dl
                    p_ref[rows, :] = _bf(p)
                    dl_ref[rows, :] = _bf(dl)
                    dsink = dsink + jnp.where(lane1 == h, -_colsum(ps * delta), 0.0)
                band_v[:, kv * DH:(kv + 1) * DH] = _tn(p_ref[...], dog)
                dqg = _nn(dl_ref[...], kb) * scale
                band_k[:, kv * DH:(kv + 1) * DH] = _tn(dl_ref[...], qg)
                for j in range(HPK):
                    h = kv * HPK + j
                    dq_ref[:, h * DH:(h + 1) * DH] = dqg[j * L:(j + 1) * L].astype(BF16)
                    bsum_ref[:, h * DH:(h + 1) * DH] += _colsum(dqg[j * L:(j + 1) * L])
            dsink_ref[...] += dsink

            @pl.when(n > 0)
            def _():
                emit_kv(pend_k[...] + band_k[0:L, :], pend_v[...] + band_v[0:L, :])
            pend_k[...] = band_k[L:2 * L, :]
            pend_v[...] = band_v[L:2 * L, :]

        @pl.when(n == nb)
        def _():
            emit_kv(pend_k[...], pend_v[...])

    cur = lambda i: jnp.minimum(i, nb - 1)
    prev = lambda i: jnp.maximum(jnp.minimum(i, nb - 1) - 1, 0)
    lag = lambda i: jnp.maximum(i - 1, 0)
    return _call(
        body, (qkv, qkv, qkv, qkv, qkv, datt, bias, sinks), side, name=name,
        out_shape=[_sds((T, ATTN_HEADS * DH), BF16), _sds((T, LANES), BF16), _sds((T, LANES), BF16), _sds((1, QKV_DIM)),
                   _sds((ATTN_HEADS, L, 2 * L)), _sds((1, LANES))],
        grid=(nb + 1,),
        in_specs=[pl.BlockSpec((L, ATTN_HEADS * DH), lambda i: (cur(i), 0)),
                  pl.BlockSpec((L, LANES), lambda i: (cur(i), kcol)), pl.BlockSpec((L, LANES), lambda i: (cur(i), vcol)),
                  pl.BlockSpec((L, LANES), lambda i: (prev(i), kcol)), pl.BlockSpec((L, LANES), lambda i: (prev(i), vcol)),
                  pl.BlockSpec((L, ATTN_HEADS * DH), lambda i: (cur(i), 0)),
                  pl.BlockSpec((ATTN_HEADS, L, 2 * L), lambda i: (0, 0, 0)),
                  pl.BlockSpec(memory_space=pltpu.SMEM)],
        out_specs=[pl.BlockSpec((L, ATTN_HEADS * DH), lambda i: (cur(i), 0)),
                   pl.BlockSpec((L, LANES), lambda i: (lag(i), 0)), pl.BlockSpec((L, LANES), lambda i: (lag(i), 0)),
                   _vec_spec(QKV_DIM), pl.BlockSpec((ATTN_HEADS, L, 2 * L), lambda i: (0, 0, 0)), _vec_spec(LANES)],
        scratch=[pltpu.VMEM((L, LANES), F32), pltpu.VMEM((L, LANES), F32),
                 pltpu.VMEM((2 * L, LANES), F32), pltpu.VMEM((2 * L, LANES), F32),
                 pltpu.VMEM((HPK * L, 2 * L), F32), pltpu.VMEM((HPK * L, 2 * L), F32),
                 pltpu.VMEM((HPK * L, 2 * L), BF16), pltpu.VMEM((HPK * L, 2 * L), BF16)],
        sem=("arbitrary",))


def _pad_rows(a, mult):
    pad = (-a.shape[-2]) % mult
    if pad == 0:
        return a
    cfg = [(0, 0)] * (a.ndim - 2) + [(0, pad), (0, 0)]
    return jnp.pad(a, cfg)


class _Pack:
    def __init__(self, width, mult, total_mult):
        self.width, self.mult, self.total_mult = width, mult, total_mult
        self.entries = []
        self.rows = 0

    def add(self, key, shape):
        n = int(np.prod(shape))
        rows = -(-n // self.width)
        self.entries.append((key, self.rows, rows, tuple(shape)))
        self.rows += -(-rows // self.mult) * self.mult

    @property
    def total(self):
        return -(-self.rows // self.total_mult) * self.total_mult

    def pack(self, pieces, dtype, lead=()):
        parts = []
        for key, _, rows, shape in self.entries:
            a = pieces[key].astype(dtype).reshape(lead + (-1,))
            n = int(np.prod(shape))
            a = jnp.pad(a, [(0, 0)] * len(lead) + [(0, rows * self.width - n)])
            a = a.reshape(lead + (rows, self.width))
            parts.append(_pad_rows(a, self.mult))
        out = jnp.concatenate(parts, axis=len(lead))
        return _pad_rows(out, self.total_mult)

    def unpack(self, packed, lead=()):
        out = {}
        for key, off, rows, shape in self.entries:
            a = lax.slice_in_dim(packed, off, off + rows, axis=len(lead))
            a = a.reshape(lead + (-1,))
            n = int(np.prod(shape))
            out[key] = lax.slice_in_dim(a, 0, n, axis=len(lead)).reshape(lead + shape)
        return out


def _ffn_fwd(x, h, mod, wg_t, wu_t, wd, tag, next_norm=None, gather=None):
    side = None if gather is None else (_GatherOps, gather)
    gate, up, act, *gathered = _mm_swiglu(h, wg_t, wu_t, f"ffn_gateup_{tag}", side=side)
    x_out, ffn_out, *h_next = _mm_resid([(act, wd)], x, mod[5:6], name=f"ffn_down_{tag}", norm=next_norm)
    return (x_out, dict(h=h, gate=gate, up=up, act=act, out=ffn_out), *h_next, *gathered)


def _ffn_bwd(dx_out, dffn, x_in, saved, mod, norm_w, wg_t, wu_t, wd, below, tag, exchange=None):
    side = None if exchange is None else (_ChipsOps, exchange)
    dgate, dup, *from_chips = _mm_swiglu_bwd(dffn, wd, saved["gate"], saved["up"], f"ffn_act_bwd_{tag}", side=side)
    d_wd = _mm_tn(saved["act"], dffn, name=f"ffn_dwd_{tag}")
    d_wg_t = _mm_tn(dgate, saved["h"], name=f"ffn_dwg_{tag}")
    d_wu_t = _mm_tn(dup, saved["h"], name=f"ffn_dwu_{tag}")
    dh = _mm([(dgate, wg_t), (dup, wu_t)], "nn", name=f"ffn_dh_{tag}")
    dx, d_below, acc = _norm_mod_bwd(x_in, dh, dx_out, norm_w, mod[4:5], below[0], below[1], f"ffn_norm_bwd_{tag}")
    return (dx, d_below, dict(d_wg=d_wg_t, d_wu=d_wu_t, d_wd=d_wd, acc=acc), *from_chips)


_BIG = [
    ("out_w", "out_w_even", 0, "row"), ("qkv_w", "qkv_w", 0, "col"), ("o_w", "o_w", 0, "row"),
    ("gate0", "ffn_gate_w", 0, "col"), ("up0", "ffn_up_w", 0, "col"), ("down0", "ffn_down_w", 0, "row"),
    ("gate1", "ffn_gate_w", 1, "col"), ("up1", "ffn_up_w", 1, "col"), ("down1", "ffn_down_w", 1, "row"),
    ("in_w", "in_w_even", 0, "col"),
]


def _to_wire(a, kind):
    return a.T if kind == "col" else a


_GATHER_GROUPS = [["in_w"], ["out_w"], ["gate0", "up0", "down0"], ["qkv_w", "o_w"], ["gate1", "up1", "down1"]]
_GRAD_GROUPS = [["gate1", "up1", "down1"], ["qkv_w", "o_w"], ["out_w", "gate0", "up0", "down0"], ["in_w"]]

_REPLICATED = ["ada_b", "norm_mix_w", "norm_ffn_w", "conv_b", "dt_bias", "a_log", "d_skip", "ssm_norm_w", "gmlp_ln_w",
               "gmlp_ln_b", "gmlp_ws", "gmlp_bs", "sinks", "rel_table", "final_norm_w"]
_TINY_SHARDED = ["conv_w", "qkv_b", "o_b"]

_WEIGHTS = ['ada_w', 'ada_b', 'norm_mix_w', 'norm_ffn_w', 'in_w_even', 'conv_w', 'conv_b', 'dt_bias', 'a_log', 'd_skip',
            'ssm_norm_w', 'gmlp_ln_w', 'gmlp_ln_b', 'gmlp_ws', 'gmlp_bs', 'out_w_even', 'qkv_w', 'qkv_b', 'o_w', 'o_b',
            'sinks', 'rel_table', 'ffn_gate_w', 'ffn_up_w', 'ffn_down_w', 'final_norm_w']


def _step(x, c, loss_target, W, M, V):
    T = x.shape[1]
    x0 = x[0]
    target = loss_target[0]
    me = 4 * lax.axis_index("x") + 2 * lax.axis_index("y") + lax.axis_index("c")

    w_wire_local = {key: _to_wire(W[name][layer].astype(BF16), kind) for key, name, layer, kind in _BIG}

    def wire_pack(keys, mult):
        gp = _Pack(D, 1, mult)
        for key in keys:
            gp.add(key, w_wire_local[key].shape)
        return gp

    gather_packs = [(gp, gp.pack(w_wire_local, BF16)) for gp in (wire_pack(keys, 16) for keys in _GATHER_GROUPS)]
    grad_packs = [wire_pack(keys, mult) for keys, mult in zip(_GRAD_GROUPS, (32, 32, 64, 32))]
    full = {}

    def gathered_weights(group, gathered):
        shards = gather_packs[group][0].unpack(gathered, lead=(N_DEV,))
        full.update({key: a.reshape(-1, D) for key, a in shards.items()})


    small_in = _Pack(D, 8, 8)
    small_in.add("c", (1, D))
    small_in.add("conv_w", W["conv_w"][0].shape)
    small_in.add("qkv_b", W["qkv_b"][0].shape)
    small_in.add("o_b", W["o_b"][0].shape)
    sm = small_in.unpack(_all_gather(small_in.pack(
        dict(c=c, conv_w=W["conv_w"][0], qkv_b=W["qkv_b"][0], o_b=W["o_b"][0]), F32), "gather_small"), lead=(N_DEV,))
    c_all = sm["c"].reshape(N_DEV, D)
    conv_w_full = jnp.transpose(sm["conv_w"], (1, 0, 2)).reshape(SSM_CONV, CONV_DIM)
    qkv_b_full = sm["qkv_b"].reshape(1, QKV_DIM)
    o_b_full = sm["o_b"].reshape(1, D)

    ncol = W["ada_w"].shape[2]
    cond, mod_cols = _mod_matmul(c_all, W["ada_w"], "mod_matmul")
    mod_g = _all_gather(mod_cols.reshape(DEPTH * N_DEV, ncol), "gather_mod").reshape(N_DEV, DEPTH, N_DEV, ncol)
    mod_me = lax.dynamic_index_in_dim(mod_g, me, axis=2, keepdims=False)
    mod_me = jnp.transpose(mod_me, (1, 0, 2)).reshape(DEPTH, 6, D)
    mod_me = jnp.pad(mod_me, ((0, 0), (0, 2), (0, 0))).reshape(DEPTH * 8, D)
    ada_b_rows = jnp.pad(W["ada_b"].reshape(DEPTH, 6, D), ((0, 0), (0, 2), (0, 0))).reshape(DEPTH * 8, D)
    mod_all = _add_rows(mod_me, ada_b_rows, "mod_bias").reshape(DEPTH, 8, D)
    mod0, mod1 = mod_all[0], mod_all[1]
    h0, gathered_in = _norm_mod(x0, W["norm_mix_w"][0:1], mod0[1:2], mod0[0:1], "mix_norm_0",
                                side=(_GatherOps, gather_packs[0][1]))
    gathered_weights(0, gathered_in)

    in_t = full["in_w"]
    o1, o2, o3, o4 = SSM_INNER, SSM_INNER + CONV_DIM, SSM_INNER + CONV_DIM + SSM_HEADS, SSM_INNER + CONV_DIM + SSM_HEADS + GMLP_INNER
    w_z, w_xbc, w_dt, w_u, w_v = in_t[:o1], in_t[o1:o2], in_t[o2:o3], in_t[o3:o4], in_t[o4:]
    w_main = jnp.concatenate([w_z, w_u, w_v, w_xbc], axis=0)
    w_dtp = jnp.pad(w_dt, ((0, LANES - SSM_HEADS), (0, 0)))

    pad16 = lambda a: jnp.pad(a.reshape(1, SSM_HEADS), ((0, 0), (0, LANES - SSM_HEADS)))
    dtb, alog = pad16(W["dt_bias"][0]), pad16(W["a_log"][0])
    dskip = jnp.repeat(W["d_skip"][0], SSM_HEAD_DIM).reshape(1, SSM_INNER)
    ssm_nw = W["ssm_norm_w"]
    lnw, lnb = W["gmlp_ln_w"], W["gmlp_ln_b"]
    ws = W["gmlp_ws"][0]
    bs_exp = jnp.repeat(W["gmlp_bs"][0].T, CHUNK, axis=1)
    conv_b = W["conv_b"]
    nmw, nfw = W["norm_mix_w"], W["norm_ffn_w"]
    onehot_t = _bucket_onehot_t()
    head_ind = _head_indicator()
    bias = _bias_from_table(W["rel_table"].T, onehot_t, _window_mask_flat(), "rel_bias").reshape(ATTN_HEADS, CHUNK, 2 * CHUNK)
    sinks = W["sinks"][0]

    pm, gathered_a = _mm([(h0, w_main)], "nt", name="in_proj", tn_pref=1536, out_dtype=BF16,
                         side=(_GatherOps, gather_packs[1][1]))
    gathered_weights(1, gathered_a)
    out_w = full["out_w"]
    dtr = _mm([(h0, w_dtp)], "nt", name="in_proj_dt")
    cpre = _conv_fwd(pm, conv_w_full, conv_b, "conv_fwd")
    ya, ypre, sprev, gathered_b = _ssd_fwd(cpre, dtr, pm, dtb, alog, dskip, ssm_nw, head_ind, "ssd_fwd",
                                           side=(_GatherOps, gather_packs[2][1]))
    gathered_weights(2, gathered_b)
    yb, gathered_c = _gmlp_fwd(pm, lnw, lnb, ws, bs_exp, "gmlp_fwd", side=(_GatherOps, gather_packs[3][1]))
    gathered_weights(3, gathered_c)
    x1, mix0, hf0 = _mm_resid([(ya, out_w[:SSM_INNER]), (yb, out_w[SSM_INNER:])], x0, mod0[2:3], name="out_proj",
                              norm=(nfw[0:1], mod0[4:5], mod0[3:4]))
    x2, ffn0, h1, gathered_d = _ffn_fwd(x1, hf0, mod0, full["gate0"], full["up0"], full["down0"], "0",
                                        next_norm=(nmw[1:2], mod1[1:2], mod1[0:1]), gather=gather_packs[4][1])
    gathered_weights(4, gathered_d)
    qkv_t, o_w = full["qkv_w"], full["o_w"]
    w_q, w_k, w_v_att = qkv_t[:D], qkv_t[D:D + LANES], qkv_t[D + LANES:]

    qkv = _mm([(h1, qkv_t)], "nt", name="qkv_proj", bias=qkv_b_full, tn_pref=1280, out_dtype=BF16)
    att = _attn_fwd(qkv, bias, sinks, "attn_fwd")
    x3, mix1, hf1 = _mm_resid([(att, o_w)], x2, mod1[2:3], name="o_proj", bias=o_b_full,
                              norm=(nfw[1:2], mod1[4:5], mod1[3:4]))
    x4, ffn1 = _ffn_fwd(x3, hf1, mod1, full["gate1"], full["up1"], full["down1"], "1")

    dx4, dffn1, acc_f = _final_loss(x4, W["final_norm_w"].reshape(1, D), target, ffn1["out"], mod1[5:6], "final_loss")
    dx3, dmix1, gf1 = _ffn_bwd(dx4, dffn1, x3, ffn1, mod1, nfw[1:2], full["gate1"], full["up1"], full["down1"],
                               (mix1, mod1[2:3]), "1")
    g_wire = dict(gate1=gf1["d_wg"], up1=gf1["d_wu"], down1=gf1["d_wd"])

    def packed_partials(group):
        return grad_packs[group].pack({key: g_wire[key].reshape(N_DEV, -1, D) for key in _GRAD_GROUPS[group]},
                                      F32, lead=(N_DEV,))

    from_chips = {}
    partials_ffn1 = packed_partials(0)
    datt, theirs_ffn1 = _mm([(dmix1, o_w)], "nt", name="o_proj_dx", out_dtype=BF16, side=(_SiblingOps, partials_ffn1))
    pair_ffn1 = _pair_sum(partials_ffn1, theirs_ffn1, "grads_pair_sum_ffn1")
    d_o_w = _mm_tn(att, dmix1, name="o_proj_dw")
    dq, dk, dv, d_qkv_b, dbias, dsinks, from_chips[0] = _attn_bwd(qkv, datt, bias, sinks, "attn_bwd",
                                                                 side=(_ChipsOps, pair_ffn1))
    d_table = _table_from_dbias(dbias.reshape(ATTN_HEADS, -1), onehot_t, "rel_table_grad")
    d_qkv_t = jnp.concatenate([_mm_tn(dq, h1, name="qkv_dw_q"), _mm_tn(dk, h1, name="qkv_dw_k"), _mm_tn(dv, h1, name="qkv_dw_v")], axis=0)
    g_wire.update(qkv_w=d_qkv_t, o_w=d_o_w)
    partials_l1 = packed_partials(1)
    dh1, theirs_l1 = _mm([(dq, w_q), (dk, w_k), (dv, w_v_att)], "nn", name="qkv_proj_dx", side=(_SiblingOps, partials_l1))
    pair_l1 = _pair_sum(partials_l1, theirs_l1, "grads_pair_sum_l1")
    dx2, dffn0, acc_n1 = _norm_mod_bwd(x2, dh1, dx3, nmw[1:2], mod1[1:2], ffn0["out"], mod0[5:6], "mix_norm_bwd_1")

    dx1, dmix0, gf0, from_chips[1] = _ffn_bwd(dx2, dffn0, x1, ffn0, mod0, nfw[0:1], full["gate0"], full["up0"], full["down0"],
                                              (mix0, mod0[2:3]), "0", exchange=pair_l1)

    dya = _mm([(dmix0, out_w[:SSM_INNER])], "nt", name="out_proj_dx_a")
    dyb = _mm([(dmix0, out_w[SSM_INNER:])], "nt", name="out_proj_dx_b")
    d_out_w = jnp.concatenate([_mm_tn(ya, dmix0, name="out_proj_dw_a"), _mm_tn(yb, dmix0, name="out_proj_dw_b")], axis=0)
    g_wire.update(gate0=gf0["d_wg"], up0=gf0["d_wu"], down0=gf0["d_wd"], out_w=d_out_w)
    partials_ffn0 = packed_partials(2)
    du, dvg, d_ws, d_bs, acc_ln, theirs_ffn0 = _gmlp_bwd(pm, dyb, lnw, lnb, ws, bs_exp, "gmlp_bwd",
                                                         side=(_SiblingOps, partials_ffn0))
    pair_ffn0 = _pair_sum(partials_ffn0, theirs_ffn0, "grads_pair_sum_mix")
    dz, dcpre, ddtr, acc_ssd, d_ssm_nw, from_chips[2] = _ssd_bwd(
        cpre, dtr, pm, ypre, sprev, dya, dtb, alog, dskip, ssm_nw, head_ind, "ssd_bwd", side=(_ChipsOps, pair_ffn0))
    acc_f0, acc_f1 = gf0["acc"], gf1["acc"]
    row = lambda i: slice(i, i + 1)
    views = dict(ada_b=(DEPTH * 6, D), norm_mix_w=(DEPTH, D), norm_ffn_w=(DEPTH, D), conv_b=(1, CONV_DIM),
                 dt_bias=(1, SSM_HEADS), a_log=(1, SSM_HEADS), d_skip=(1, SSM_HEADS), ssm_norm_w=(1, D), gmlp_ln_w=(1, D),
                 gmlp_ln_b=(1, D), gmlp_ws=(GMLP_GROUPS * CHUNK, CHUNK), gmlp_bs=(GMLP_GROUPS, CHUNK),
                 sinks=(1, ATTN_HEADS), rel_table=(REL_BUCKETS, ATTN_HEADS), final_norm_w=(1, D),
                 conv_w=(SSM_CONV, CONV_DIM), qkv_b=(1, QKV_DIM), o_b=(1, D), loss=(1, D))
    late_keys = ["ada_b", "norm_mix_w", "conv_b", "dt_bias", "a_log", "d_skip", "ssm_norm_w", "conv_w"]
    canvas_early, canvas_late = _Canvas(), _Canvas()
    for key, (r, cdim) in views.items():
        (canvas_late if key in late_keys else canvas_early).add(key, r, cdim, blocks=GMLP_GROUPS if key == "gmlp_ws" else 1)
    early_sources = [
        (acc_f0, [("norm_ffn_w", row(2), 0)]),
        (acc_f1, [("norm_ffn_w", row(2), 1), ("o_b", row(4), 0)]),
        (acc_f, [("final_norm_w", row(0), 0), ("loss", row(1), 0)]),
        (acc_ln, [("gmlp_ln_w", row(0), 0), ("gmlp_ln_b", row(1), 0)]),
        (d_ws.reshape(GMLP_GROUPS * CHUNK, CHUNK), [("gmlp_ws", slice(0, GMLP_GROUPS * CHUNK), 0)]),
        (d_bs.T, [("gmlp_bs", slice(0, GMLP_GROUPS), 0)]),
        (dsinks, [("sinks", row(0), 0)]),
        (d_table, [("rel_table", slice(0, REL_BUCKETS), 0)]),
        (d_qkv_b, [("qkv_b", row(0), 0)]),
    ]
    dxbc, acc_conv, parts_early = _conv_bwd(dcpre, pm, conv_w_full, "conv_bwd",
                                            side=(_GatherOps, _canvas_fill(canvas_early, early_sources, "small_grads_early")))
    d_in_t = jnp.concatenate([
        _mm_tn(dz, h0, name="in_dw_z"), _mm_tn(dxbc, h0, name="in_dw_xbc"),
        _mm_tn(ddtr, h0, name="in_dw_dt")[:SSM_HEADS], _mm_tn(du, h0, name="in_dw_u"), _mm_tn(dvg, h0, name="in_dw_v")], axis=0)
    g_wire.update(in_w=d_in_t)
    partials_mix = packed_partials(3)
    theirs_mix = _comm_call(_SiblingOps, partials_mix, "exchange_grads_sibling")
    pair_mix = _pair_sum(partials_mix, theirs_mix, "grads_pair_sum")
    dh0, from_chips[3] = _mm([(dz, w_z), (dxbc, w_xbc), (ddtr, w_dtp), (du, w_u), (dvg, w_v)], "nn", name="in_proj_dx",
                             side=(_ChipsOps, pair_mix))
    grad_x, acc_n0 = _norm_mod_bwd(x0, dh0, dx1, nmw[0:1], mod0[1:2], None, None, "mix_norm_bwd_0")

    g_mine = {}
    for group in range(len(_GRAD_GROUPS)):
        g_mine.update(grad_packs[group].unpack(_sum_parts(from_chips[group], f"grads_chip_sum_{group}")))
    res_big = [{}, {}, {}, {}]
    for key, name, layer, kind in _BIG:
        g_nat = _to_wire(g_mine[key], kind)
        outs = _adamw(g_nat[None], W[name][layer], M[name][layer], V[name][layer], f"adamw_{key}")
        for res, out in zip(res_big, outs):
            res[key] = out

    late_sources = [
        (acc_n0, [("ada_b", row(1), 0), ("ada_b", row(0), 1), ("norm_mix_w", row(2), 0)]),
        (acc_f0, [("ada_b", row(3), 2), ("ada_b", row(1), 3), ("ada_b", row(0), 4)]),
        (acc_n1, [("ada_b", row(3), 5), ("ada_b", row(1), 6), ("ada_b", row(0), 7), ("norm_mix_w", row(2), 1)]),
        (acc_f1, [("ada_b", row(3), 8), ("ada_b", row(1), 9), ("ada_b", row(0), 10)]),
        (acc_f, [("ada_b", row(3), 11)]),
        (acc_conv, [("conv_w", slice(0, SSM_CONV), 0), ("conv_b", row(4), 0)]),
        (acc_ssd, [("dt_bias", row(0), 0), ("a_log", row(1), 0), ("d_skip", row(2), 0)]),
        (d_ssm_nw, [("ssm_norm_w", row(0), 0)]),
    ]
    parts_small = _all_gather(_canvas_fill(canvas_late, late_sources, "small_grads_late"), "gather_small_grads")
    view = lambda a, key: a.reshape(views[key])
    params = lambda keys: [(n, view(W[n], n), view(M[n], n), view(V[n], n)) for n in _REPLICATED if (n in late_keys) == keys]
    small_out = _adamw_canvas(canvas_early, parts_early, params(False), ["qkv_b", "o_b", "loss"], "adamw_small_early")
    small_out.update(_adamw_canvas(canvas_late, parts_small, params(True), ["conv_w"], "adamw_small_late"))
    loss = small_out["loss"][0][0, 0]
    res_small = [{name: small_out[name][k].reshape(W[name].shape) for name in _REPLICATED} for k in range(4)]

    n_cw, n_qb, n_ob = W["conv_w"].shape[2], W["qkv_b"].shape[1], W["o_b"].shape[1]
    g_tiny = dict(conv_w=lax.dynamic_slice_in_dim(small_out["conv_w"][0], me * n_cw, n_cw, axis=1)[None],
                  qkv_b=lax.dynamic_slice_in_dim(small_out["qkv_b"][0], me * n_qb, n_qb, axis=1),
                  o_b=lax.dynamic_slice_in_dim(small_out["o_b"][0], me * n_ob, n_ob, axis=1))
    tiny = _Pack(D, 8, 8)
    for name in _TINY_SHARDED:
        tiny.add(name, W[name].shape)
    pkt = lambda S: tiny.pack({name: S[name] for name in _TINY_SHARDED}, F32)
    res_tiny = [tiny.unpack(r) for r in _adamw(pkt(g_tiny)[None], pkt(W), pkt(M), pkt(V), "adamw_tiny")]

    dmod_all = parts_small[:, canvas_late.offset["ada_b"]:canvas_late.offset["ada_b"] + DEPTH * 6].reshape(N_DEV, DEPTH, 6 * D)
    dmod_cols = jnp.transpose(lax.dynamic_slice_in_dim(dmod_all, me * ncol, ncol, axis=2), (1, 0, 2))
    g_ada_w = _ada_w_grad(cond, dmod_cols, "ada_w_grad")
    flat = lambda a: a.reshape(DEPTH * D, ncol)
    res_ada = [r.reshape(DEPTH, D, ncol) for r in _adamw(flat(g_ada_w)[None], flat(W["ada_w"]), flat(M["ada_w"]), flat(V["ada_w"]), "adamw_ada_w")]

    def result(kind_idx, name):
        if name == "ada_w":
            return res_ada[kind_idx]
        if name in _REPLICATED:
            return res_small[kind_idx][name]
        if name in _TINY_SHARDED:
            return res_tiny[kind_idx][name]
        pieces = [res_big[kind_idx][key] for key, nm, layer, kind in _BIG if nm == name]
        return jnp.stack(pieces)

    outs = [loss, grad_x[None]]
    for kind_idx in range(4):
        outs += [result(kind_idx, name) for name in _WEIGHTS]
    return tuple(outs)


def kernel(x, c, ada_w, ada_b, norm_mix_w, norm_ffn_w, in_w_even, conv_w, conv_b, dt_bias, a_log, d_skip, ssm_norm_w, gmlp_ln_w, gmlp_ln_b, gmlp_ws, gmlp_bs, out_w_even, qkv_w, qkv_b, o_w, o_b, sinks, rel_table, ffn_gate_w, ffn_up_w, ffn_down_w, final_norm_w, loss_target, m_ada_w, m_ada_b, m_norm_mix_w, m_norm_ffn_w, m_in_w_even, m_conv_w, m_conv_b, m_dt_bias, m_a_log, m_d_skip, m_ssm_norm_w, m_gmlp_ln_w, m_gmlp_ln_b, m_gmlp_ws, m_gmlp_bs, m_out_w_even, m_qkv_w, m_qkv_b, m_o_w, m_o_b, m_sinks, m_rel_table, m_ffn_gate_w, m_ffn_up_w, m_ffn_down_w, m_final_norm_w, v_ada_w, v_ada_b, v_norm_mix_w, v_norm_ffn_w, v_in_w_even, v_conv_w, v_conv_b, v_dt_bias, v_a_log, v_d_skip, v_ssm_norm_w, v_gmlp_ln_w, v_gmlp_ln_b, v_gmlp_ws, v_gmlp_bs, v_out_w_even, v_qkv_w, v_qkv_b, v_o_w, v_o_b, v_sinks, v_rel_table, v_ffn_gate_w, v_ffn_up_w, v_ffn_down_w, v_final_norm_w):
    args = locals()
    W = {n: args[n] for n in _WEIGHTS}
    M = {n: args["m_" + n] for n in _WEIGHTS}
    V = {n: args["v_" + n] for n in _WEIGHTS}
    return _step(x, c, loss_target, W, M, V)
```

```python
import functools
import math

import numpy as np
import jax
import jax.numpy as jnp
from jax import lax
from jax.experimental import pallas as pl
from jax.experimental.pallas import tpu as pltpu

F32 = jnp.float32
BF16 = jnp.bfloat16
HIGHEST = lax.Precision.HIGHEST
MESH = pl.DeviceIdType.MESH

N_DEV = 8
D = 1024
DEPTH = 2
SSM_HEADS = 16
SSM_HEAD_DIM = 64
SSM_INNER = 1024
SSM_GROUPS = 2
SSM_STATE = 128
SSM_CONV = 4
CHUNK = 128
CONV_DIM = SSM_INNER + 2 * SSM_GROUPS * SSM_STATE
GMLP_GROUPS = 8
GMLP_INNER = 1024
IN_EVEN = 4624
ATTN_HEADS = 16
ATTN_KV = 2
ATTN_DH = 64
QKV_DIM = 1280
REL_BUCKETS = 32
REL_MAX_DIST = 128
FFN = 2816
EPS = 1e-6
NEG_INF = -1e30
LANES = 128

ADAM_LR = 0.001
ADAM_B1 = 0.9
ADAM_B2 = 0.999
ADAM_EPS = 1e-08
ADAM_WD = 0.01
ADAM_STEP = 10

VMEM_LIMIT_BYTES = 56 * 1024 * 1024
ROW_TILE = 512


def _pcall(body, *, name, out_shape, grid=(), in_specs=None, out_specs=None, scratch=(), sem=None):
    params = dict(vmem_limit_bytes=VMEM_LIMIT_BYTES)
    if sem is not None:
        params["dimension_semantics"] = sem
    specs = {} if in_specs is None else dict(in_specs=in_specs, out_specs=out_specs)
    return pl.pallas_call(
        body, name=name, out_shape=out_shape, grid=grid, **specs,
        scratch_shapes=list(scratch), compiler_params=pltpu.CompilerParams(**params))


def _call(body, args, side=None, *, name, out_shape, grid, in_specs, out_specs, scratch=(), sem=None):
    if side is None:
        return _pcall(body, name=name, out_shape=out_shape, grid=grid, in_specs=in_specs, out_specs=out_specs,
                      scratch=scratch, sem=sem)(*args)
    ops_cls, x = side
    n_in, n_out, n_scr = len(in_specs), len(out_shape), len(scratch)
    steps = int(np.prod(grid))
    hbm = pl.BlockSpec(memory_space=pl.ANY)

    def wrapped(*refs):
        ins, x_ref = refs[:n_in], refs[n_in]
        outs, r_ref = refs[n_in + 1:n_in + 1 + n_out], refs[n_in + 1 + n_out]
        scr, sems = refs[n_in + 2 + n_out:n_in + 2 + n_out + n_scr], refs[n_in + 2 + n_out + n_scr:]
        ops = ops_cls(x_ref, r_ref, *sems)
        step = pl.program_id(0)
        for axis in range(1, len(grid)):
            step = step * grid[axis] + pl.program_id(axis)
        pl.when(step == 0)(ops.start)
        body(*ins, *outs, *scr)
        pl.when(step == (3 * steps) // 4)(ops.forward)
        pl.when(step == steps - 1)(ops.finish)

    return _pcall(
        wrapped, name=name, out_shape=list(out_shape) + [ops_cls.result(x)], grid=grid,
        in_specs=list(in_specs) + [hbm], out_specs=list(out_specs) + [hbm],
        scratch=list(scratch) + ops_cls.scratch(), sem=("arbitrary",) * len(grid))(*args, x)


def _tile(n, pref):
    if n <= pref:
        return n
    best = None
    for t in range(LANES, pref + 1, LANES):
        if n % t == 0:
            best = t
    assert best is not None, (n, pref)
    return best


def _rows(T):
    return min(ROW_TILE, T)


def _sds(shape, dtype=F32):
    return jax.ShapeDtypeStruct(shape, dtype)


def _row_spec(tm, c, col=0):
    return pl.BlockSpec((tm, c), lambda i, col=col: (i, col))


def _vec_spec(c, r=1):
    return pl.BlockSpec((r, c), lambda i: (0, 0))


def _sigmoid(x):
    return jax.nn.sigmoid(x)


def _silu(x):
    return x * _sigmoid(x)


def _dsilu(x):
    s = _sigmoid(x)
    return s * (1.0 + x * (1.0 - s))


def _silu_and_grad(x):
    s = _sigmoid(x)
    return x * s, s * (1.0 + x * (1.0 - s))


def _gelu(x):
    return 0.5 * x * (1.0 + lax.erf(x * 0.7071067811865476))


def _gelu_and_grad(x):
    phi = 0.5 * (1.0 + lax.erf(x * 0.7071067811865476))
    return x * phi, phi + x * jnp.exp(-0.5 * x * x) * 0.3989422804014327


def _dot(a, b, dims, precision=None):
    return lax.dot_general(a, b, (dims, ((), ())), precision=precision, preferred_element_type=F32)


def _nn(a, b, precision=None):
    return _dot(a, b, ((1,), (0,)), precision)


def _nt(a, b, precision=None):
    return _dot(a, b, ((1,), (1,)), precision)


def _tn(a, b, precision=None):
    return _dot(a, b, ((0,), (0,)), precision)


def _bf(x):
    return x.astype(BF16)


def _colsum(x):
    return jnp.sum(x, axis=0, keepdims=True)


def _rowsum(x):
    return jnp.sum(x, axis=1, keepdims=True)


def _allsum(x):
    return _colsum(_rowsum(x))


def _comm_call(ops_cls, x, name, from_vmem=False):
    def body(x_ref, out_ref, *sems):
        ops = ops_cls(x_ref, out_ref, *sems)
        ops.start()
        ops.forward()
        ops.finish()

    return pl.pallas_call(
        body, name=name, out_shape=ops_cls.result(x),
        in_specs=[pl.BlockSpec(memory_space=pltpu.VMEM if from_vmem else pl.ANY)],
        out_specs=pl.BlockSpec(memory_space=pl.ANY), scratch_shapes=ops_cls.scratch(),
    )(x)


def _all_gather(x, name):
    return _comm_call(_GatherOps, x, name, from_vmem=True)


class _GatherOps:
    def __init__(self, x_ref, out_ref, send_sems, recv_sems, local_sem):
        self.x_ref, self.out_ref = x_ref, out_ref
        self.send_sems, self.recv_sems, self.local_sem = send_sems, recv_sems, local_sem
        mx, my, mc = lax.axis_index("x"), lax.axis_index("y"), lax.axis_index("c")
        self.mc = mc
        self.me, self.sibling = (mx, my, mc), (mx, my, 1 - mc)
        self.chips = [(1 - mx, my), (mx, 1 - my), (1 - mx, 1 - my)]

    @staticmethod
    def result(x):
        return _sds((N_DEV,) + x.shape, x.dtype)

    @staticmethod
    def scratch():
        return [pltpu.SemaphoreType.DMA((7,)), pltpu.SemaphoreType.DMA((7,)), pltpu.SemaphoreType.DMA(())]

    def _slot(self, px, py, pc):
        return self.out_ref.at[4 * px + 2 * py + pc]

    def _copy(self, k, block, to, own=False):
        return pltpu.make_async_remote_copy(
            src_ref=self.x_ref if own else self._slot(*block), dst_ref=self._slot(*block),
            send_sem=self.send_sems.at[k], recv_sem=self.recv_sems.at[k], device_id=to, device_id_type=MESH)

    def _mine(self):
        return pltpu.make_async_copy(self.x_ref, self._slot(*self.me), self.local_sem)

    def _first(self):
        return [self._copy(0, self.me, self.sibling, own=True)] + [
            self._copy(1 + j, self.me, (*chip, self.mc), own=True) for j, chip in enumerate(self.chips)]

    def _passed(self):
        return [self._copy(4 + j, (*chip, self.mc), self.sibling) for j, chip in enumerate(self.chips)]

    def start(self):
        self._mine().start()
        for cp in self._first():
            cp.start()

    def forward(self):
        passed = self._passed()
        for j, chip in enumerate(self.chips):
            self._copy(1 + j, (*chip, self.mc), self.me).wait_recv()
            passed[j].start()

    def finish(self):
        self._copy(0, self.sibling, self.me).wait_recv()
        for j, chip in enumerate(self.chips):
            self._copy(4 + j, (*chip, 1 - self.mc), self.me).wait_recv()
        for cp in self._first() + self._passed():
            cp.wait_send()
        self._mine().wait()


N_CHIP = 4


class _SiblingOps:
    def __init__(self, p_ref, theirs_ref, send_sems, recv_sems):
        self.p_ref, self.theirs_ref, self.send_sems, self.recv_sems = p_ref, theirs_ref, send_sems, recv_sems

    @staticmethod
    def result(p):
        return _sds((N_CHIP,) + p.shape[1:], p.dtype)

    @staticmethod
    def scratch():
        return [pltpu.SemaphoreType.DMA((N_CHIP,))] * 2

    def _copies(self):
        mx, my, mc = lax.axis_index("x"), lax.axis_index("y"), lax.axis_index("c")
        return [pltpu.make_async_remote_copy(
            src_ref=self.p_ref.at[2 * chip + 1 - mc], dst_ref=self.theirs_ref.at[chip],
            send_sem=self.send_sems.at[chip], recv_sem=self.recv_sems.at[chip],
            device_id=(mx, my, 1 - mc), device_id_type=MESH) for chip in range(N_CHIP)]

    def start(self):
        for cp in self._copies():
            cp.start()

    def forward(self):
        pass

    def finish(self):
        for cp in self._copies():
            cp.wait()


class _ChipsOps:
    def __init__(self, q_ref, out_ref, send_sems, recv_sems, local_sem):
        self.q_ref, self.out_ref = q_ref, out_ref
        self.send_sems, self.recv_sems, self.local_sem = send_sems, recv_sems, local_sem

    @staticmethod
    def result(q):
        return _sds(q.shape, q.dtype)

    @staticmethod
    def scratch():
        return [pltpu.SemaphoreType.DMA((N_CHIP - 1,)), pltpu.SemaphoreType.DMA((N_CHIP - 1,)), pltpu.SemaphoreType.DMA(())]

    def _copies(self):
        mx, my, mc = lax.axis_index("x"), lax.axis_index("y"), lax.axis_index("c")
        me = 2 * mx + my
        local = pltpu.make_async_copy(self.q_ref.at[me], self.out_ref.at[me], self.local_sem)
        remote = []
        for r in range(1, N_CHIP):
            px = 1 - mx if r & 2 else mx
            py = 1 - my if r & 1 else my
            remote.append(pltpu.make_async_remote_copy(
                src_ref=self.q_ref.at[2 * px + py], dst_ref=self.out_ref.at[me],
                send_sem=self.send_sems.at[r - 1], recv_sem=self.recv_sems.at[r - 1],
                device_id=(px, py, mc), device_id_type=MESH))
        return local, remote

    def start(self):
        local, remote = self._copies()
        local.start()
        for cp in remote:
            cp.start()

    def forward(self):
        pass

    def finish(self):
        local, remote = self._copies()
        for cp in remote:
            cp.wait()
        local.wait()


def _pair_sum(p, theirs, name):
    n, R, C = theirs.shape
    tr = _tile_rows(R, 256)

    def body(p_ref, t_ref, o_ref):
        mc = lax.axis_index("c")
        o_ref[0] = (p_ref[0, mc] + t_ref[0]).astype(BF16)

    blk = pl.BlockSpec((1, tr, C), lambda s, i: (s, i, 0))
    return _pcall(body, name=name, out_shape=_sds((n, R, C), BF16), grid=(n, R // tr),
                  in_specs=[pl.BlockSpec((1, 2, tr, C), lambda s, i: (s, 0, i, 0)), blk],
                  out_specs=blk, sem=("parallel", "parallel"))(p.reshape(n, 2, R, C), theirs)


def _sum_parts(parts, name):
    P, R, C = parts.shape
    tr = _tile_rows(R, 256)

    def body(p_ref, o_ref):
        g = p_ref[0].astype(F32)
        for k in range(1, P):
            g = g + p_ref[k].astype(F32)
        o_ref[...] = g

    return _pcall(body, name=name, out_shape=_sds((R, C)), grid=(R // tr,),
                  in_specs=[pl.BlockSpec((P, tr, C), lambda i: (0, i, 0))],
                  out_specs=pl.BlockSpec((tr, C), lambda i: (i, 0)), sem=("parallel",))(parts)


def _adamw(parts, w, m, v, name):
    P, R, C = parts.shape
    tr = R if R <= 256 else _tile_rows(R, 256)

    def body(p_ref, w_ref, m_ref, v_ref, g_ref, d_ref, nm_ref, nv_ref):
        g = p_ref[0]
        for k in range(1, P):
            g = g + p_ref[k]
        nm = ADAM_B1 * m_ref[...] + (1.0 - ADAM_B1) * g
        nv = ADAM_B2 * v_ref[...] + (1.0 - ADAM_B2) * (g * g)
        m_hat = nm / (1.0 - ADAM_B1 ** ADAM_STEP)
        v_hat = nv / (1.0 - ADAM_B2 ** ADAM_STEP)
        g_ref[...] = g
        d_ref[...] = -ADAM_LR * (m_hat / (jnp.sqrt(v_hat) + ADAM_EPS) + ADAM_WD * w_ref[...])
        nm_ref[...] = nm
        nv_ref[...] = nv

    blk = pl.BlockSpec((tr, C), lambda i: (i, 0))
    return _pcall(
        body, name=name, out_shape=[_sds((R, C))] * 4, grid=(R // tr,),
        in_specs=[pl.BlockSpec((P, tr, C), lambda i: (0, i, 0)), blk, blk, blk],
        out_specs=[blk] * 4, sem=("parallel",))(parts, w, m, v)


def _adam_update(g, w, m, v):
    nm = ADAM_B1 * m + (1.0 - ADAM_B1) * g
    nv = ADAM_B2 * v + (1.0 - ADAM_B2) * (g * g)
    m_hat = nm / (1.0 - ADAM_B1 ** ADAM_STEP)
    v_hat = nv / (1.0 - ADAM_B2 ** ADAM_STEP)
    return -ADAM_LR * (m_hat / (jnp.sqrt(v_hat) + ADAM_EPS) + ADAM_WD * w), nm, nv


class _Canvas:
    def __init__(self):
        self.offset, self.views, self.rows = {}, {}, 0

    def add(self, key, r, c, blocks=1):
        need = r // blocks if blocks > 1 else r * (-(-c // D))
        if need >= 8:
            self.rows = -(-self.rows // 8) * 8
        self.offset[key], self.views[key] = self.rows, (r, c, blocks)
        self.rows += need

    @property
    def total(self):
        return -(-self.rows // 8) * 8

    def cells(self, key):
        (r, c, blocks), off = self.views[key], self.offset[key]
        if blocks > 1:
            n = r // blocks
            return [(off, n, b * c, c, slice(b * n, (b + 1) * n), slice(0, c)) for b in range(blocks)]
        if c <= D:
            return [(off, r, 0, c, slice(0, r), slice(0, c))]
        chunks = -(-c // D)
        return [(off + i * chunks + j, 1, 0, min(D, c - j * D), slice(i, i + 1), slice(j * D, min(c, (j + 1) * D)))
                for i in range(r) for j in range(chunks)]


def _canvas_fill(canvas, sources, name):
    arrays = [a for a, _ in sources]

    def body(*refs):
        out_ref = refs[-1]
        out_ref[...] = jnp.zeros_like(out_ref)
        for ref, (_, items) in zip(refs[:-1], sources):
            for key, src_rows, view_row in items:
                n = src_rows.stop - src_rows.start
                for row, count, lane, width, vrows, vcols in canvas.cells(key):
                    lo, hi = max(vrows.start, view_row), min(vrows.stop, view_row + n)
                    if lo < hi:
                        src = slice(src_rows.start + lo - view_row, src_rows.start + hi - view_row)
                        out_ref[row + lo - vrows.start:row + hi - vrows.start, lane:lane + width] = ref[src, vcols]

    return _pcall(body, name=name, out_shape=_sds((canvas.total, D)))(*arrays)


def _adamw_canvas(canvas, parts, params, sum_only, name):
    n = len(params)

    def body(p_ref, *refs):
        sum_ref = refs[-1]
        g_all = p_ref[0]
        for k in range(1, N_DEV):
            g_all = g_all + p_ref[k]
        sum_ref[...] = g_all
        for i, (key, _, _, _) in enumerate(params):
            w_ref, m_ref, v_ref = refs[3 * i:3 * i + 3]
            outs = refs[3 * n + 4 * i:3 * n + 4 * i + 4]
            for row, count, lane, width, vrows, vcols in canvas.cells(key):
                g = sum_ref[row:row + count, lane:lane + width]
                delta, nm, nv = _adam_update(g, w_ref[vrows, vcols], m_ref[vrows, vcols], v_ref[vrows, vcols])
                for ref, val in zip(outs, (g, delta, nm, nv)):
                    ref[vrows, vcols] = val
        for i, key in enumerate(sum_only):
            for row, count, lane, width, vrows, vcols in canvas.cells(key):
                refs[7 * n + i][vrows, vcols] = sum_ref[row:row + count, lane:lane + width]

    args = [a for _, w, m, v in params for a in (w, m, v)]
    out_shape = [_sds(w.shape) for _, w, _, _ in params for _ in range(4)] + [_sds(canvas.views[k][:2]) for k in sum_only]
    res = _pcall(body, name=name, out_shape=out_shape, scratch=[pltpu.VMEM(parts.shape[1:], F32)])(parts, *args)
    out = {key: res[4 * i:4 * i + 4] for i, (key, _, _, _) in enumerate(params)}
    out.update({key: [res[4 * n + i]] for i, key in enumerate(sum_only)})
    return out


def _tile_rows(n, pref):
    best = None
    for t in range(8, pref + 1, 8):
        if n % t == 0:
            best = t
    assert best is not None, (n, pref)
    return best


def _mm(pairs, mode, *, name, out_dtype=F32, bias=None, tn_pref=1024, side=None):
    M = pairs[0][0].shape[0]
    N = pairs[0][1].shape[1] if mode == "nn" else pairs[0][1].shape[0]
    tm, tn = _rows(M), _tile(N, tn_pref)
    n_pairs = len(pairs)
    has_bias = bias is not None

    def body(*refs):
        acc = _pairs_dot(refs[:2 * n_pairs], mode)
        if has_bias:
            acc = acc + refs[2 * n_pairs][...]
        refs[-1][...] = acc.astype(refs[-1].dtype)

    in_specs, args = _pair_specs(pairs, mode, tm, tn)
    if has_bias:
        in_specs.append(pl.BlockSpec((1, tn), lambda j, i: (0, j)))
        args.append(bias)
    res = _call(body, args, side, name=name, out_shape=[_sds((M, N), out_dtype)], grid=(N // tn, M // tm),
                in_specs=in_specs, out_specs=[pl.BlockSpec((tm, tn), lambda j, i: (i, j))], sem=("parallel", "parallel"))
    return res[0] if side is None else (res[0], res[1])


def _pairs_dot(ab, mode):
    acc = None
    for p in range(len(ab) // 2):
        a, b = _bf(ab[2 * p][...]), _bf(ab[2 * p + 1][...])
        d = _nn(a, b) if mode == "nn" else _nt(a, b)
        acc = d if acc is None else acc + d
    return acc


def _pair_specs(pairs, mode, tm, tn):
    in_specs, args = [], []
    for a, b in pairs:
        K = a.shape[1]
        in_specs.append(pl.BlockSpec((tm, K), lambda j, i: (i, 0)))
        if mode == "nn":
            in_specs.append(pl.BlockSpec((K, tn), lambda j, i: (0, j)))
        else:
            in_specs.append(pl.BlockSpec((tn, K), lambda j, i: (j, 0)))
        args += [a, b]
    return in_specs, args


def _mm_resid(pairs, resid, gvec, *, name, bias=None, norm=None):
    M, N = resid.shape
    tm = _rows(M)
    n_pairs = len(pairs)
    has_bias, has_norm = bias is not None, norm is not None

    def body(*refs):
        acc = _pairs_dot(refs[:2 * n_pairs], "nn")
        pos = 2 * n_pairs
        if has_bias:
            acc = acc + refs[pos][...]
            pos += 1
        xv = refs[pos][...] + refs[pos + 1][...] * acc
        outs = refs[pos + 2 + 3 * has_norm:]
        outs[0][...] = xv
        outs[1][...] = acc.astype(BF16)
        if has_norm:
            w_ref, sc_ref, sh_ref = refs[pos + 2:pos + 5]
            r = lax.rsqrt(jnp.mean(xv * xv, axis=-1, keepdims=True) + EPS)
            outs[2][...] = ((xv * r * w_ref[...]) * (1.0 + sc_ref[...]) + sh_ref[...]).astype(BF16)

    in_specs, args = _pair_specs(pairs, "nn", tm, N)
    vec = pl.BlockSpec((1, N), lambda j, i: (0, 0))
    row = pl.BlockSpec((tm, N), lambda j, i: (i, 0))
    if has_bias:
        in_specs.append(vec)
        args.append(bias)
    in_specs += [row, vec] + [vec] * (3 * has_norm)
    args += [resid, gvec] + (list(norm) if has_norm else [])
    return _pcall(body, name=name, out_shape=[_sds((M, N)), _sds((M, N), BF16)] + [_sds((M, N), BF16)] * has_norm,
                  grid=(1, M // tm), in_specs=in_specs, out_specs=[row] * (2 + has_norm),
                  sem=("parallel", "parallel"))(*args)


def _mm_tn(a, b, *, name, tm_pref=1408, tn_pref=1536):
    K, M = a.shape
    N = b.shape[1]
    tm, tn = _tile(M, tm_pref), _tile(N, tn_pref)
    tk = K if K <= 2 * ROW_TILE else 2 * ROW_TILE

    def body(a_ref, b_ref, o_ref):
        @pl.when(pl.program_id(2) == 0)
        def _():
            o_ref[...] = jnp.zeros_like(o_ref)
        o_ref[...] += _tn(_bf(a_ref[...]), _bf(b_ref[...]))

    return _pcall(
        body, name=name, out_shape=_sds((M, N)), grid=(M // tm, N // tn, K // tk),
        in_specs=[pl.BlockSpec((tk, tm), lambda i, j, k: (k, i)), pl.BlockSpec((tk, tn), lambda i, j, k: (k, j))],
        out_specs=pl.BlockSpec((tm, tn), lambda i, j, k: (i, j)),
        sem=("parallel", "parallel", "arbitrary"))(a, b)


def _mm_swiglu(h, wg_t, wu_t, name, side=None):
    M, K = h.shape
    N = wg_t.shape[0]
    tm, tn = _rows(M), _tile(N, 1408)

    def body(h_ref, wg_ref, wu_ref, gate_ref, up_ref, act_ref):
        hv = _bf(h_ref[...])
        gate = _nt(hv, wg_ref[...])
        up = _nt(hv, wu_ref[...])
        gate_ref[...] = gate.astype(BF16)
        up_ref[...] = up.astype(BF16)
        act_ref[...] = (_silu(gate) * up).astype(BF16)

    w_spec = pl.BlockSpec((tn, K), lambda j, i: (j, 0))
    o_spec = pl.BlockSpec((tm, tn), lambda j, i: (i, j))
    return _call(body, (h, wg_t, wu_t), side, name=name,
                 out_shape=[_sds((M, N), BF16)] * 3, grid=(N // tn, M // tm),
                 in_specs=[pl.BlockSpec((tm, K), lambda j, i: (i, 0)), w_spec, w_spec], out_specs=[o_spec] * 3,
                 sem=("parallel", "parallel"))


def _mm_swiglu_bwd(dout, wd, gate, up, name, side=None):
    M, K = dout.shape
    N = wd.shape[0]
    tm, tn = _rows(M), _tile(N, 1408)

    def body(d_ref, wd_ref, gate_ref, up_ref, dg_ref, du_ref):
        dact = _nt(_bf(d_ref[...]), wd_ref[...])
        act, dact_dg = _silu_and_grad(gate_ref[...].astype(F32))
        dg_ref[...] = (dact * up_ref[...].astype(F32) * dact_dg).astype(BF16)
        du_ref[...] = (dact * act).astype(BF16)

    t_spec = pl.BlockSpec((tm, tn), lambda j, i: (i, j))
    return _call(
        body, (dout, wd, gate, up), side, name=name, out_shape=[_sds((M, N), BF16)] * 2, grid=(N // tn, M // tm),
        in_specs=[pl.BlockSpec((tm, K), lambda j, i: (i, 0)), pl.BlockSpec((tn, K), lambda j, i: (j, 0)), t_spec, t_spec],
        out_specs=[t_spec] * 2, sem=("parallel", "parallel"))


def _norm_mod(x, w, sc, sh, name, side=None):
    T = x.shape[0]
    tm = _rows(T)

    def body(x_ref, w_ref, sc_ref, sh_ref, o_ref):
        xv = x_ref[...]
        r = lax.rsqrt(jnp.mean(xv * xv, axis=-1, keepdims=True) + EPS)
        o_ref[...] = ((xv * r * w_ref[...]) * (1.0 + sc_ref[...]) + sh_ref[...]).astype(BF16)

    return _call(body, (x, w, sc, sh), side, name=name, out_shape=[_sds((T, D), BF16)], grid=(T // tm,),
                 in_specs=[_row_spec(tm, D), _vec_spec(D), _vec_spec(D), _vec_spec(D)],
                 out_specs=[_row_spec(tm, D)], sem=("parallel",))


def _gate_rows(dxv, br_ref, g_ref, db_ref, acc_ref):
    db = g_ref[...] * dxv
    db_ref[...] = db.astype(BF16)
    acc_ref[3:4, :] += _colsum(dxv * br_ref[...].astype(F32))
    acc_ref[4:5, :] += _colsum(db)


def _norm_mod_bwd(x, dh, dres, w, sc, branch, g, name, side=None):
    T = x.shape[0]
    tm = _rows(T)
    gated = branch is not None

    def body(x_ref, dh_ref, dres_ref, w_ref, sc_ref, *rest):
        (br_ref, g_ref, dx_ref, db_ref, acc_ref) = rest if gated else (None, None, rest[0], None, rest[1])

        @pl.when(pl.program_id(0) == 0)
        def _():
            acc_ref[...] = jnp.zeros_like(acc_ref)
        xv, dh_v, wv = x_ref[...], dh_ref[...], w_ref[...]
        r = lax.rsqrt(jnp.mean(xv * xv, axis=-1, keepdims=True) + EPS)
        n = xv * r
        dnw = dh_v * (1.0 + sc_ref[...])
        dn = dnw * wv
        dxv = dres_ref[...] + r * (dn - n * jnp.mean(dn * n, axis=-1, keepdims=True))
        dx_ref[...] = dxv
        acc_ref[0:1, :] += _colsum(dh_v * (n * wv))
        acc_ref[1:2, :] += _colsum(dh_v)
        acc_ref[2:3, :] += _colsum(dnw * n)
        if gated:
            _gate_rows(dxv, br_ref, g_ref, db_ref, acc_ref)

    row = _row_spec(tm, D)
    args = (x, dh, dres, w, sc) + ((branch, g) if gated else ())
    return _call(body, args, side, name=name,
                 out_shape=[_sds((T, D))] + ([_sds((T, D), BF16)] if gated else []) + [_sds((8, D))], grid=(T // tm,),
                 in_specs=[row, row, row, _vec_spec(D), _vec_spec(D)] + ([row, _vec_spec(D)] if gated else []),
                 out_specs=[row] + ([row] if gated else []) + [_vec_spec(D, 8)], sem=("arbitrary",))


def _final_loss(x, wf, target, branch, g, name):
    T = x.shape[0]
    tm = _rows(T)

    def body(x_ref, w_ref, t_ref, br_ref, g_ref, dx_ref, db_ref, acc_ref):
        @pl.when(pl.program_id(0) == 0)
        def _():
            acc_ref[...] = jnp.zeros_like(acc_ref)
        xv, wv = x_ref[...], w_ref[...]
        r = lax.rsqrt(jnp.mean(xv * xv, axis=-1, keepdims=True) + EPS)
        n = xv * r
        err = n * wv - t_ref[...]
        dy = err * (1.0 / D)
        dn = dy * wv
        dxv = r * (dn - n * jnp.mean(dn * n, axis=-1, keepdims=True))
        dx_ref[...] = dxv
        acc_ref[0:1, :] += _colsum(dy * n)
        acc_ref[1:2, :] += jnp.broadcast_to(_allsum(err * err) * (0.5 / D), (1, D))
        _gate_rows(dxv, br_ref, g_ref, db_ref, acc_ref)

    row = _row_spec(tm, D)
    return _pcall(body, name=name, out_shape=[_sds((T, D)), _sds((T, D), BF16), _sds((8, D))], grid=(T // tm,),
                  in_specs=[row, _vec_spec(D), row, row, _vec_spec(D)],
                  out_specs=[row, row, _vec_spec(D, 8)], sem=("arbitrary",))(x, wf, target, branch, g)


def _mod_matmul(c_all, ada_w, name):
    n = ada_w.shape[2]

    def body(c_ref, w_ref, cond_ref, o_ref):
        cond = _silu(c_ref[...])
        cond_ref[...] = cond
        o_ref[0] = _nn(cond, w_ref[0])

    return _pcall(body, name=name, out_shape=[_sds((N_DEV, D)), _sds((DEPTH, N_DEV, n))], grid=(DEPTH,),
                  in_specs=[pl.BlockSpec((N_DEV, D), lambda l: (0, 0)), pl.BlockSpec((1, D, n), lambda l: (l, 0, 0))],
                  out_specs=[pl.BlockSpec((N_DEV, D), lambda l: (0, 0)), pl.BlockSpec((1, N_DEV, n), lambda l: (l, 0, 0))],
                  sem=("arbitrary",))(c_all, ada_w)


def _add_rows(a, b, name):
    def body(a_ref, b_ref, o_ref):
        o_ref[...] = a_ref[...] + b_ref[...]

    return _pcall(body, name=name, out_shape=_sds(a.shape))(a, b)


def _ada_w_grad(cond, dmod_cols, name):
    n = dmod_cols.shape[2]

    def body(c_ref, d_ref, o_ref):
        o_ref[0] = _tn(c_ref[...], d_ref[0])

    return _pcall(body, name=name, out_shape=_sds((DEPTH, D, n)), grid=(DEPTH,),
                  in_specs=[pl.BlockSpec((N_DEV, D), lambda l: (0, 0)), pl.BlockSpec((1, N_DEV, n), lambda l: (l, 0, 0))],
                  out_specs=pl.BlockSpec((1, D, n), lambda l: (l, 0, 0)), sem=("parallel",))(cond, dmod_cols)


def _conv_fwd(pm, conv_w, conv_b, name):
    T = pm.shape[0]
    tm = _rows(T)
    C = CONV_DIM

    def body(x_ref, prev_ref, w_ref, b_ref, o_ref):
        cur = x_ref[...].astype(F32)
        prev = jnp.where(pl.program_id(0) > 0, prev_ref[...].astype(F32)[8:16], 0.0)
        cur8 = cur[0:8]
        row8 = lax.broadcasted_iota(jnp.int32, (8, C), 0)
        full = w_ref[3:4, :] * cur
        head = w_ref[3:4, :] * cur8
        for k in range(1, SSM_CONV):
            wk = w_ref[3 - k:4 - k, :]
            full = full + wk * pltpu.roll(cur, k, 0)
            head = head + wk * jnp.where(row8 < k, pltpu.roll(prev, k, 0), pltpu.roll(cur8, k, 0))
        o_ref[...] = full + b_ref[...]
        o_ref[0:8, :] = head + b_ref[...]

    return _pcall(
        body, name=name, out_shape=_sds((T, C)), grid=(T // tm,),
        in_specs=[pl.BlockSpec((tm, C), lambda i: (i, 2)),
                  pl.BlockSpec((16, C), lambda i: (jnp.maximum(i * (tm // 16) - 1, 0), 2)),
                  _vec_spec(C, SSM_CONV), _vec_spec(C)],
        out_specs=_row_spec(tm, C), sem=("parallel",))(pm, pm, conv_w, conv_b)


def _conv_bwd(dc, pm, conv_w, name, side=None):
    T = dc.shape[0]
    tm = _rows(T)
    C = CONV_DIM
    nt = T // tm

    def body(dc_ref, nxt_ref, x_ref, w_ref, dx_ref, acc_ref):
        i = pl.program_id(0)

        @pl.when(i == 0)
        def _():
            acc_ref[...] = jnp.zeros_like(acc_ref)
        dcv = dc_ref[...]
        nxt = jnp.where(i < nt - 1, nxt_ref[...], 0.0)
        xc = x_ref[...].astype(F32)
        dc8t, x8t = dcv[tm - 8:tm], xc[tm - 8:tm]
        row8 = lax.broadcasted_iota(jnp.int32, (8, C), 0)
        full = w_ref[3:4, :] * dcv
        tail = w_ref[3:4, :] * dc8t
        acc_ref[3:4, :] += _colsum(dcv * xc)
        for k in range(1, SSM_CONV):
            wk = w_ref[3 - k:4 - k, :]
            up = pltpu.roll(dcv, tm - k, 0)
            up_tail = jnp.where(row8 + k >= 8, pltpu.roll(nxt, 8 - k, 0), pltpu.roll(dc8t, 8 - k, 0))
            full = full + wk * up
            tail = tail + wk * up_tail
            prod = up * xc
            acc_ref[3 - k:4 - k, :] += _colsum(prod) - _colsum(prod[tm - 8:tm]) + _colsum(up_tail * x8t)
        acc_ref[4:5, :] += _colsum(dcv)
        dx_ref[...] = jnp.concatenate([full[0:tm - 8], tail], axis=0).astype(BF16)

    return _call(
        body, (dc, dc, pm, conv_w), side, name=name, out_shape=[_sds((T, C), BF16), _sds((8, C))], grid=(nt,),
        in_specs=[_row_spec(tm, C),
                  pl.BlockSpec((8, C), lambda i: (jnp.minimum((i + 1) * (tm // 8), T // 8 - 1), 0)),
                  pl.BlockSpec((tm, C), lambda i: (i, 2)),
                  _vec_spec(C, SSM_CONV)],
        out_specs=[_row_spec(tm, C), _vec_spec(C, 8)], sem=("arbitrary",))


def _ssd_prologue(cpre, dtr, dtb, alog):
    L = CHUNK
    xc = _silu(cpre)
    pre = dtr + dtb
    dt = jnp.maximum(pre, 0.0) + jnp.log1p(jnp.exp(-jnp.abs(pre)))
    a = -jnp.exp(alog)
    la = dt * a
    row = lax.broadcasted_iota(jnp.int32, (L, L), 0)
    col = lax.broadcasted_iota(jnp.int32, (L, L), 1)
    causal = row >= col
    tri = causal.astype(F32)
    lc = _nn(tri, la, HIGHEST)
    return xc, pre, dt, a, causal, tri, lc, row, col


def _head_indicator():
    m = np.zeros((LANES, SSM_INNER), np.float32)
    for h in range(SSM_HEADS):
        m[h, h * SSM_HEAD_DIM:(h + 1) * SSM_HEAD_DIM] = 1.0
    return jnp.asarray(m, dtype=BF16)


def _split_dot(x, ind, dims):
    hi = x.astype(BF16)
    lo = (x - hi.astype(F32)).astype(BF16)
    return _dot(hi, ind, dims) + _dot(lo, ind, dims)


def _expand(x16, ind):
    return _split_dot(x16, ind, ((1,), (0,)))


def _headsum(x, ind, single_pass=False):
    if single_pass:
        return _dot(x.astype(BF16), ind, ((1,), (1,)))
    return _split_dot(x, ind, ((1,), (1,)))


def _ssd_fwd(cpre, dtr, pm, dtb, alog, dskip, normw, ind, name, side=None):
    T = cpre.shape[0]
    nc = T // CHUNK
    L, P, H, HPG, N = CHUNK, SSM_HEAD_DIM, SSM_HEADS, SSM_HEADS // SSM_GROUPS, SSM_STATE
    half = SSM_INNER // SSM_GROUPS

    def body(cp_ref, dtr_ref, z_ref, dtb_ref, alog_ref, dskip_ref, nw_ref, ind_ref, ya_ref, y_ref, sp_ref, st_ref):
        @pl.when(pl.program_id(0) == 0)
        def _():
            st_ref[...] = jnp.zeros_like(st_ref)
        xc, _, dt, _, causal, _, lc, _, _ = _ssd_prologue(cp_ref[...], dtr_ref[...], dtb_ref[...], alog_ref[...])
        lct = lc.T
        ind = ind_ref[...]
        llast = lc[L - 1:L, :]
        xs = xc[:, :SSM_INNER]
        xd = xs * _expand(dt, ind)
        ex = _expand(jnp.exp(lc), ind)
        xd_end = _bf(xd * _expand(jnp.exp(llast - lc), ind))
        cdx = _expand(jnp.broadcast_to(jnp.exp(llast), (8, LANES)), ind)[0:1]
        xdb = _bf(xd)
        sp_ref[0] = st_ref[...]
        for g in range(SSM_GROUPS):
            sl = slice(g * half, (g + 1) * half)
            bm = _bf(xc[:, SSM_INNER + g * N:SSM_INNER + (g + 1) * N])
            cm = _bf(xc[:, SSM_INNER + (SSM_GROUPS + g) * N:SSM_INNER + (SSM_GROUPS + g + 1) * N])
            cb = _nt(cm, bm)
            st = st_ref[g]
            y_ref[:, sl] = ex[:, sl] * _nn(cm, _bf(st)) + dskip_ref[:, sl] * xs[:, sl]
            st_ref[g] = st * cdx[:, sl] + _tn(bm, xd_end[:, sl])
            for j in range(HPG):
                h = g * HPG + j
                decay = jnp.where(causal, jnp.exp(jnp.where(causal, lc[:, h:h + 1] - lct[h:h + 1, :], 0.0)), 0.0)
                y_ref[:, h * P:(h + 1) * P] += _nn(_bf(cb * decay), xdb[:, h * P:(h + 1) * P])
        y2 = y_ref[...] * _silu(z_ref[...].astype(F32))
        for g in range(SSM_GROUPS):
            yg = y2[:, g * half:(g + 1) * half]
            r = lax.rsqrt(jnp.mean(yg * yg, axis=-1, keepdims=True) + EPS)
            ya_ref[:, g * half:(g + 1) * half] = (yg * r * nw_ref[:, g * half:(g + 1) * half]).astype(BF16)

    return _call(
        body, (cpre, dtr, pm, dtb, alog, dskip, normw, ind), side, name=name,
        out_shape=[_sds((T, SSM_INNER), BF16), _sds((T, SSM_INNER)), _sds((nc, SSM_GROUPS, N, half))], grid=(nc,),
        in_specs=[_row_spec(L, CONV_DIM), _row_spec(L, LANES), _row_spec(L, SSM_INNER, 0),
                  _vec_spec(LANES), _vec_spec(LANES), _vec_spec(SSM_INNER), _vec_spec(SSM_INNER), _vec_spec(SSM_INNER, LANES)],
        out_specs=[_row_spec(L, SSM_INNER), _row_spec(L, SSM_INNER),
                   pl.BlockSpec((1, SSM_GROUPS, N, half), lambda i: (i, 0, 0, 0))],
        scratch=[pltpu.VMEM((SSM_GROUPS, N, half), F32)], sem=("arbitrary",))


def _ssd_bwd(cpre, dtr, pm, ypre, sprev, dya, dtb, alog, dskip, normw, ind, name, side=None):
    T = cpre.shape[0]
    nc = T // CHUNK
    L, P, H, HPG, N = CHUNK, SSM_HEAD_DIM, SSM_HEADS, SSM_HEADS // SSM_GROUPS, SSM_STATE
    half = SSM_INNER // SSM_GROUPS

    def body(cp_ref, dtr_ref, z_ref, y_ref, sp_ref, dya_ref, dtb_ref, alog_ref, dskip_ref, nw_ref, ind_ref,
             dz_ref, dcp_ref, ddtr_ref, acc_ref, dnw_ref, ds_ref, dy_ref, dxd_ref, rr_ref, yoff_ref, dcd_ref):
        @pl.when(pl.program_id(0) == 0)
        def _():
            ds_ref[...] = jnp.zeros_like(ds_ref)
            acc_ref[...] = jnp.zeros_like(acc_ref)
            dnw_ref[...] = jnp.zeros_like(dnw_ref)
        cpre_v = cp_ref[...]
        xc, pre, dt, a, causal, tri, lc, row, col = _ssd_prologue(cpre_v, dtr_ref[...], dtb_ref[...], alog_ref[...])
        lct = lc.T
        zv, yv = z_ref[...].astype(F32), y_ref[...]
        sz, dsz = _silu_and_grad(zv)
        y2 = yv * sz
        dya_v = dya_ref[...]
        nwv = nw_ref[...]
        for g in range(SSM_GROUPS):
            sl = slice(g * half, (g + 1) * half)
            yg = y2[:, sl]
            r = lax.rsqrt(jnp.mean(yg * yg, axis=-1, keepdims=True) + EPS)
            nrm = yg * r
            dnw_ref[:, sl] += _colsum(dya_v[:, sl] * nrm)
            dn = dya_v[:, sl] * nwv[:, sl]
            dy2 = r * (dn - nrm * jnp.mean(dn * nrm, axis=-1, keepdims=True))
            dy_ref[:, sl] = dy2 * sz[:, sl]
            dz_ref[:, sl] = (dy2 * yv[:, sl] * dsz[:, sl]).astype(BF16)
        ind = ind_ref[...]
        llast = lc[L - 1:L, :]
        dte16 = jnp.exp(llast - lc)
        cd16 = jnp.exp(llast)
        xs = xc[:, :SSM_INNER]
        dtx = _expand(dt, ind)
        ex = _expand(jnp.exp(lc), ind)
        dtex = _expand(dte16, ind)
        cdx = _expand(jnp.broadcast_to(cd16, (8, LANES)), ind)[0:1]
        xd = xs * dtx
        xdb = _bf(xd)
        xd_end = _bf(xd * dtex)
        dyv = dy_ref[...]
        dy_off = _bf(ex * dyv)
        dyb = _bf(dyv)
        dskx = dskip_ref[...]
        lane_c = lax.broadcasted_iota(jnp.int32, (L, LANES), 1)
        lane1 = lax.broadcasted_iota(jnp.int32, (1, LANES), 1)
        sub16 = lax.broadcasted_iota(jnp.int32, (H, L), 0)
        dlc_c = jnp.zeros((L, LANES), F32)
        dlc_r = jnp.zeros((H, L), F32)
        for g in range(SSM_GROUPS):
            sl = slice(g * half, (g + 1) * half)
            b_lo = SSM_INNER + g * N
            c_lo = SSM_INNER + (SSM_GROUPS + g) * N
            bm, cm = _bf(xc[:, b_lo:b_lo + N]), _bf(xc[:, c_lo:c_lo + N])
            cb = _nt(cm, bm)
            st, dst = sp_ref[0, g], ds_ref[g]
            stb, dstb = _bf(st), _bf(dst)
            dcm = _nt(dy_off[:, sl], stb)
            ds_ref[g] = _tn(cm, dy_off[:, sl]) + dst * cdx[:, sl]
            rr_ref[:, sl] = _nn(bm, dstb)
            yoff_ref[:, sl] = ex[:, sl] * _nn(cm, stb)
            db = _nt(xd_end[:, sl], dstb)
            dcd_ref[:, sl] = _colsum(dst * st)
            dcb = jnp.zeros((L, L), F32)
            for j in range(HPG):
                h = g * HPG + j
                hs = slice(h * P, (h + 1) * P)
                decay = jnp.where(causal, jnp.exp(jnp.where(causal, lc[:, h:h + 1] - lct[h:h + 1, :], 0.0)), 0.0)
                m = cb * decay
                dxd_ref[:, hs] = _tn(_bf(m), dyb[:, hs])
                dm = _nt(dyb[:, hs], xdb[:, hs])
                dcb = dcb + dm * decay
                gm = dm * m
                dlc_c = dlc_c + jnp.where(lane_c == h, _rowsum(gm), 0.0)
                dlc_r = dlc_r + jnp.where(sub16 == h, _colsum(gm), 0.0)
            dcbb = _bf(dcb)
            dcp_ref[:, c_lo:c_lo + N] = dcm + _nn(dcbb, bm)
            dcp_ref[:, b_lo:b_lo + N] = db + _tn(dcbb, cm)
        dxd_diag, rr = dxd_ref[...], rr_ref[...]
        tt = _headsum(rr * xd, ind, single_pass=True) * dte16
        dlc_rt = jnp.concatenate([dlc_r, jnp.zeros((LANES - H, L), F32)], axis=0).T
        dlc = dlc_c - dlc_rt + _headsum(dyv * yoff_ref[...], ind, single_pass=True) - tt
        dcd = _headsum(jnp.broadcast_to(dcd_ref[...], (8, SSM_INNER)), ind)[0:1]
        dlc = dlc + jnp.where(row == L - 1, _colsum(tt) + dcd * cd16, 0.0)
        dla = _tn(tri, dlc, HIGHEST)
        dxd = dxd_diag + dtex * rr
        ddt = _headsum(dxd * xs, ind, single_pass=True) + dla * a
        ddtr = jnp.where(lane_c < H, ddt * _sigmoid(pre), 0.0)
        ddtr_ref[...] = ddtr
        acc_ref[0:1, :] += _colsum(ddtr)
        acc_ref[1:2, :] += jnp.where(lane1 < H, _colsum(dla * dt) * a, 0.0)
        acc_ref[2:3, :] += _headsum(jnp.broadcast_to(_colsum(dyv * xs), (8, SSM_INNER)), ind)[0:1]
        dcp_ref[:, 0:SSM_INNER] = dxd * dtx + dskx * dyv
        dcp_ref[...] = dcp_ref[...] * _dsilu(cpre_v)

    rev = lambda i: (nc - 1 - i, 0)
    rspec = lambda c: pl.BlockSpec((L, c), rev)
    return _call(
        body, (cpre, dtr, pm, ypre, sprev, dya, dtb, alog, dskip, normw, ind), side, name=name,
        out_shape=[_sds((T, SSM_INNER), BF16), _sds((T, CONV_DIM)), _sds((T, LANES)), _sds((8, LANES)), _sds((1, SSM_INNER))],
        grid=(nc,),
        in_specs=[rspec(CONV_DIM), rspec(LANES), rspec(SSM_INNER), rspec(SSM_INNER),
                  pl.BlockSpec((1, SSM_GROUPS, N, half), lambda i: (nc - 1 - i, 0, 0, 0)), rspec(SSM_INNER),
                  _vec_spec(LANES), _vec_spec(LANES), _vec_spec(SSM_INNER), _vec_spec(SSM_INNER), _vec_spec(SSM_INNER, LANES)],
        out_specs=[rspec(SSM_INNER), rspec(CONV_DIM), rspec(LANES), _vec_spec(LANES, 8), _vec_spec(SSM_INNER)],
        scratch=[pltpu.VMEM((SSM_GROUPS, N, half), F32), pltpu.VMEM((L, SSM_INNER), F32), pltpu.VMEM((L, SSM_INNER), F32),
                 pltpu.VMEM((L, SSM_INNER), F32), pltpu.VMEM((L, SSM_INNER), F32), pltpu.VMEM((1, SSM_INNER), F32)],
        sem=("arbitrary",))


def _gmlp_common(u, v, lnw, lnb, with_grads=False):
    (ug, dug), (vg, dvg) = (_gelu_and_grad(u), _gelu_and_grad(v)) if with_grads else ((_gelu(u), None), (_gelu(v), None))
    mu = jnp.mean(vg, axis=-1, keepdims=True)
    cen = vg - mu
    rstd = lax.rsqrt(jnp.mean(cen * cen, axis=-1, keepdims=True) + EPS)
    vhat = cen * rstd
    out = (ug, rstd, vhat, vhat * lnw + lnb)
    return out + (dug, dvg) if with_grads else out


def _causal_mask():
    row = lax.broadcasted_iota(jnp.int32, (CHUNK, CHUNK), 0)
    col = lax.broadcasted_iota(jnp.int32, (CHUNK, CHUNK), 1)
    return row >= col


def _gmlp_fwd(pm, lnw, lnb, ws, bs_exp, name, side=None):
    T = pm.shape[0]
    nc = T // CHUNK
    L, G = CHUNK, GMLP_GROUPS

    def body(u_ref, v_ref, lnw_ref, lnb_ref, ws_ref, bs_ref, o_ref):
        ug, _, _, vn = _gmlp_common(u_ref[...].astype(F32), v_ref[...].astype(F32), lnw_ref[...], lnb_ref[...])
        causal = _causal_mask()
        for g in range(G):
            sl = slice(g * L, (g + 1) * L)
            wm = _bf(jnp.where(causal, ws_ref[g], 0.0))
            sv = _nn(wm, _bf(vn[:, sl])) + bs_ref[:, sl]
            o_ref[:, sl] = (ug[:, sl] * sv).astype(BF16)

    return _call(
        body, (pm, pm, lnw, lnb, ws, bs_exp), side, name=name, out_shape=[_sds((T, GMLP_INNER), BF16)], grid=(nc,),
        in_specs=[_row_spec(L, GMLP_INNER, 1), _row_spec(L, GMLP_INNER, 2), _vec_spec(GMLP_INNER), _vec_spec(GMLP_INNER),
                  pl.BlockSpec((G, L, L), lambda i: (0, 0, 0)), _vec_spec(GMLP_INNER, L)],
        out_specs=[_row_spec(L, GMLP_INNER)], sem=("parallel",))


def _gmlp_bwd(pm, dyb, lnw, lnb, ws, bs_exp, name, side=None):
    T = pm.shape[0]
    nc = T // CHUNK
    L, G = CHUNK, GMLP_GROUPS

    def body(u_ref, v_ref, dy_ref, lnw_ref, lnb_ref, ws_ref, bs_ref, du_ref, dv_ref, dws_ref, dbs_ref, acc_ref, dvn_ref):
        @pl.when(pl.program_id(0) == 0)
        def _():
            dws_ref[...] = jnp.zeros_like(dws_ref)
            dbs_ref[...] = jnp.zeros_like(dbs_ref)
            acc_ref[...] = jnp.zeros_like(acc_ref)
        uv, vv, dyv, lnwv = u_ref[...].astype(F32), v_ref[...].astype(F32), dy_ref[...], lnw_ref[...]
        ug, rstd, vhat, vn, dug, dvg_act = _gmlp_common(uv, vv, lnwv, lnb_ref[...], with_grads=True)
        causal = _causal_mask()
        lane = lax.broadcasted_iota(jnp.int32, (L, LANES), 1)
        dbs = jnp.zeros((L, LANES), F32)
        for g in range(G):
            sl = slice(g * L, (g + 1) * L)
            wm = _bf(jnp.where(causal, ws_ref[g], 0.0))
            vng = _bf(vn[:, sl])
            sv = _nn(wm, vng) + bs_ref[:, sl]
            du_ref[:, sl] = (dyv[:, sl] * sv * dug[:, sl]).astype(BF16)
            dsv = dyv[:, sl] * ug[:, sl]
            dsvb = _bf(dsv)
            dws_ref[g] += jnp.where(causal, _nt(dsvb, vng), 0.0)
            dbs = dbs + jnp.where(lane == g, _rowsum(dsv), 0.0)
            dvn_ref[:, sl] = _tn(wm, dsvb)
        dbs_ref[...] += dbs
        dvn = dvn_ref[...]
        acc_ref[0:1, :] += _colsum(dvn * vhat)
        acc_ref[1:2, :] += _colsum(dvn)
        dvh = dvn * lnwv
        dvg = rstd * (dvh - jnp.mean(dvh, axis=-1, keepdims=True) - vhat * jnp.mean(dvh * vhat, axis=-1, keepdims=True))
        dv_ref[...] = (dvg * dvg_act).astype(BF16)

    return _call(
        body, (pm, pm, dyb, lnw, lnb, ws, bs_exp), side, name=name,
        out_shape=[_sds((T, GMLP_INNER), BF16), _sds((T, GMLP_INNER), BF16), _sds((G, L, L)), _sds((L, LANES)), _sds((8, GMLP_INNER))],
        grid=(nc,),
        in_specs=[_row_spec(L, GMLP_INNER, 1), _row_spec(L, GMLP_INNER, 2), _row_spec(L, GMLP_INNER),
                  _vec_spec(GMLP_INNER), _vec_spec(GMLP_INNER), pl.BlockSpec((G, L, L), lambda i: (0, 0, 0)),
                  _vec_spec(GMLP_INNER, L)],
        out_specs=[_row_spec(L, GMLP_INNER), _row_spec(L, GMLP_INNER), pl.BlockSpec((G, L, L), lambda i: (0, 0, 0)),
                   _vec_spec(LANES, L), _vec_spec(GMLP_INNER, 8)],
        scratch=[pltpu.VMEM((L, GMLP_INNER), F32)], sem=("arbitrary",))


def _rel_buckets():
    qi = np.arange(CHUNK)[:, None]
    sj = np.arange(2 * CHUNK)[None, :]
    dist = np.maximum(qi + CHUNK - sj, 0)
    max_exact = REL_BUCKETS // 2
    log_ratio = (np.log(np.maximum(dist, 1).astype(np.float32) / np.float32(max_exact))
                 / np.float32(math.log(REL_MAX_DIST / max_exact))).astype(np.float32)
    large = max_exact + (log_ratio * np.float32(REL_BUCKETS - max_exact)).astype(np.int32)
    return np.where(dist < max_exact, dist, np.minimum(large, REL_BUCKETS - 1))


def _bucket_onehot_t():
    bucket = _rel_buckets().reshape(-1)
    return jnp.asarray((np.arange(REL_BUCKETS)[:, None] == bucket[None, :]).astype(np.float32))


def _bias_from_table(table_t, onehot_t, window, name):
    def body(t_ref, o_ref, w_ref, out_ref):
        out_ref[...] = jnp.where(w_ref[...] > 0.5, _nn(t_ref[...], o_ref[...], HIGHEST), NEG_INF)

    return _pcall(body, name=name, out_shape=_sds((ATTN_HEADS, onehot_t.shape[1])))(table_t, onehot_t, window)


def _table_from_dbias(dbias, onehot_t, name):
    def body(d_ref, o_ref, out_ref):
        out_ref[...] = _nt(o_ref[...], d_ref[...], HIGHEST)

    return _pcall(body, name=name, out_shape=_sds((REL_BUCKETS, ATTN_HEADS)))(dbias, onehot_t)


def _softmax_sink(logits, sink):
    mx = jnp.maximum(jnp.max(logits, axis=-1, keepdims=True), sink)
    e = jnp.exp(logits - mx)
    es = jnp.exp(sink - mx)
    inv = 1.0 / (_rowsum(e) + es)
    return e * inv, es * inv


def _first_block_penalty(n):
    sj = lax.broadcasted_iota(jnp.int32, (1, 2 * CHUNK), 1)
    return jnp.where((sj < CHUNK) & (n == 0), NEG_INF, 0.0)


def _window_mask_flat():
    qi = np.arange(CHUNK)[:, None]
    sj = np.arange(2 * CHUNK)[None, :]
    rel = qi + CHUNK - sj
    return jnp.asarray(((rel >= 0) & (rel < CHUNK)).astype(np.float32).reshape(1, -1))


def _stack_heads(ref, first, count, width, scale=None):
    x = jnp.concatenate([_bf(ref[:, (first + j) * width:(first + j + 1) * width]) for j in range(count)], axis=0)
    return x if scale is None else x * jnp.asarray(scale, x.dtype)


def _attn_fwd(qkv, bias, sinks, name):
    T = qkv.shape[0]
    nb = T // CHUNK
    L, DH, HPK = CHUNK, ATTN_DH, ATTN_HEADS // ATTN_KV
    scale = DH ** -0.5
    kcol, vcol = ATTN_HEADS * DH // LANES, ATTN_HEADS * DH // LANES + 1

    def body(q_ref, k_ref, v_ref, kp_ref, vp_ref, bias_ref, sink_ref, o_ref, lg_ref, p_ref):
        n = pl.program_id(0)
        pen = _first_block_penalty(n)
        kband = _bf(jnp.concatenate([kp_ref[...], k_ref[...]], axis=0))
        vband = _bf(jnp.concatenate([vp_ref[...], v_ref[...]], axis=0))
        for kv in range(ATTN_KV):
            lg_ref[...] = _nt(_stack_heads(q_ref, kv * HPK, HPK, DH, scale), kband[:, kv * DH:(kv + 1) * DH])
            for j in range(HPK):
                h = kv * HPK + j
                p, _ = _softmax_sink(lg_ref[j * L:(j + 1) * L, :] + bias_ref[h] + pen, sink_ref[h])
                p_ref[j * L:(j + 1) * L, :] = _bf(p)
            og = _nn(p_ref[...], vband[:, kv * DH:(kv + 1) * DH])
            for j in range(HPK):
                h = kv * HPK + j
                o_ref[:, h * DH:(h + 1) * DH] = og[j * L:(j + 1) * L].astype(BF16)

    prev = lambda i: jnp.maximum(i - 1, 0)
    return _pcall(
        body, name=name, out_shape=_sds((T, ATTN_HEADS * DH), BF16), grid=(nb,),
        in_specs=[_row_spec(L, ATTN_HEADS * DH, 0), _row_spec(L, LANES, kcol), _row_spec(L, LANES, vcol),
                  pl.BlockSpec((L, LANES), lambda i: (prev(i), kcol)), pl.BlockSpec((L, LANES), lambda i: (prev(i), vcol)),
                  pl.BlockSpec((ATTN_HEADS, L, 2 * L), lambda i: (0, 0, 0)),
                  pl.BlockSpec(memory_space=pltpu.SMEM)],
        out_specs=_row_spec(L, ATTN_HEADS * DH),
        scratch=[pltpu.VMEM((HPK * L, 2 * L), F32), pltpu.VMEM((HPK * L, 2 * L), BF16)],
        sem=("parallel",))(qkv, qkv, qkv, qkv, qkv, bias, sinks)


def _attn_bwd(qkv, datt, bias, sinks, name, side=None):
    T = qkv.shape[0]
    nb = T // CHUNK
    L, DH, HPK = CHUNK, ATTN_DH, ATTN_HEADS // ATTN_KV
    scale = DH ** -0.5
    kcol, vcol = ATTN_HEADS * DH // LANES, ATTN_HEADS * DH // LANES + 1

    def body(q_ref, k_ref, v_ref, kp_ref, vp_ref, do_ref, bias_ref, sink_ref,
             dq_ref, dk_ref, dv_ref, bsum_ref, dbias_ref, dsink_ref, pend_k, pend_v, band_k, band_v, lg_ref, dp_ref, p_ref, dl_ref):
        n = pl.program_id(0)

        @pl.when(n == 0)
        def _():
            dbias_ref[...] = jnp.zeros_like(dbias_ref)
            dsink_ref[...] = jnp.zeros_like(dsink_ref)
            bsum_ref[...] = jnp.zeros_like(bsum_ref)

        def emit_kv(dk, dv):
            dk_ref[...] = dk.astype(BF16)
            dv_ref[...] = dv.astype(BF16)
            bsum_ref[:, ATTN_HEADS * DH:ATTN_HEADS * DH + LANES] += _colsum(dk)
            bsum_ref[:, ATTN_HEADS * DH + LANES:] += _colsum(dv)

        @pl.when(n < nb)
        def _():
            pen = _first_block_penalty(n)
            kband = _bf(jnp.concatenate([kp_ref[...], k_ref[...]], axis=0))
            vband = _bf(jnp.concatenate([vp_ref[...], v_ref[...]], axis=0))
            lane1 = lax.broadcasted_iota(jnp.int32, (1, LANES), 1)
            dsink = jnp.zeros((1, LANES), F32)
            for kv in range(ATTN_KV):
                kb, vb = kband[:, kv * DH:(kv + 1) * DH], vband[:, kv * DH:(kv + 1) * DH]
                qg = _stack_heads(q_ref, kv * HPK, HPK, DH, scale)
                dog = _stack_heads(do_ref, kv * HPK, HPK, DH)
                lg_ref[...] = _nt(qg, kb)
                dp_ref[...] = _nt(dog, vb)
                for j in range(HPK):
                    h = kv * HPK + j
                    rows = slice(j * L, (j + 1) * L)
                    p, ps = _softmax_sink(lg_ref[rows, :] + bias_ref[h] + pen, sink_ref[h])
                    dp = dp_ref[rows, :]
                    delta = _rowsum(p * dp)
                    dl = p * (dp - delta)
                    dbias_ref[h] += dl
                    p_ref[rows, :] = _bf(p)
                    dl_ref[rows, :] = _bf(dl)
                    dsink = dsink + jnp.where(lane1 == h, -_colsum(ps * delta), 0.0)
                band_v[:, kv * DH:(kv + 1) * DH] = _tn(p_ref[...], dog)
                dqg = _nn(dl_ref[...], kb) * scale
                band_k[:, kv * DH:(kv + 1) * DH] = _tn(dl_ref[...], qg)
                for j in range(HPK):
                    h = kv * HPK + j
                    dq_ref[:, h * DH:(h + 1) * DH] = dqg[j * L:(j + 1) * L].astype(BF16)
                    bsum_ref[:, h * DH:(h + 1) * DH] += _colsum(dqg[j * L:(j + 1) * L])
            dsink_ref[...] += dsink

            @pl.when(n > 0)
            def _():
                emit_kv(pend_k[...] + band_k[0:L, :], pend_v[...] + band_v[0:L, :])
            pend_k[...] = band_k[L:2 * L, :]
            pend_v[...] = band_v[L:2 * L, :]

        @pl.when(n == nb)
        def _():
            emit_kv(pend_k[...], pend_v[...])

    cur = lambda i: jnp.minimum(i, nb - 1)
    prev = lambda i: jnp.maximum(jnp.minimum(i, nb - 1) - 1, 0)
    lag = lambda i: jnp.maximum(i - 1, 0)
    return _call(
        body, (qkv, qkv, qkv, qkv, qkv, datt, bias, sinks), side, name=name,
        out_shape=[_sds((T, ATTN_HEADS * DH), BF16), _sds((T, LANES), BF16), _sds((T, LANES), BF16), _sds((1, QKV_DIM)),
                   _sds((ATTN_HEADS, L, 2 * L)), _sds((1, LANES))],
        grid=(nb + 1,),
        in_specs=[pl.BlockSpec((L, ATTN_HEADS * DH), lambda i: (cur(i), 0)),
                  pl.BlockSpec((L, LANES), lambda i: (cur(i), kcol)), pl.BlockSpec((L, LANES), lambda i: (cur(i), vcol)),
                  pl.BlockSpec((L, LANES), lambda i: (prev(i), kcol)), pl.BlockSpec((L, LANES), lambda i: (prev(i), vcol)),
                  pl.BlockSpec((L, ATTN_HEADS * DH), lambda i: (cur(i), 0)),
                  pl.BlockSpec((ATTN_HEADS, L, 2 * L), lambda i: (0, 0, 0)),
                  pl.BlockSpec(memory_space=pltpu.SMEM)],
        out_specs=[pl.BlockSpec((L, ATTN_HEADS * DH), lambda i: (cur(i), 0)),
                   pl.BlockSpec((L, LANES), lambda i: (lag(i), 0)), pl.BlockSpec((L, LANES), lambda i: (lag(i), 0)),
                   _vec_spec(QKV_DIM), pl.BlockSpec((ATTN_HEADS, L, 2 * L), lambda i: (0, 0, 0)), _vec_spec(LANES)],
        scratch=[pltpu.VMEM((L, LANES), F32), pltpu.VMEM((L, LANES), F32),
                 pltpu.VMEM((2 * L, LANES), F32), pltpu.VMEM((2 * L, LANES), F32),
                 pltpu.VMEM((HPK * L, 2 * L), F32), pltpu.VMEM((HPK * L, 2 * L), F32),
                 pltpu.VMEM((HPK * L, 2 * L), BF16), pltpu.VMEM((HPK * L, 2 * L), BF16)],
        sem=("arbitrary",))


def _pad_rows(a, mult):
    pad = (-a.shape[-2]) % mult
    if pad == 0:
        return a
    cfg = [(0, 0)] * (a.ndim - 2) + [(0, pad), (0, 0)]
    return jnp.pad(a, cfg)


class _Pack:
    def __init__(self, width, mult, total_mult):
        self.width, self.mult, self.total_mult = width, mult, total_mult
        self.entries = []
        self.rows = 0

    def add(self, key, shape):
        n = int(np.prod(shape))
        rows = -(-n // self.width)
        self.entries.append((key, self.rows, rows, tuple(shape)))
        self.rows += -(-rows // self.mult) * self.mult

    @property
    def total(self):
        return -(-self.rows // self.total_mult) * self.total_mult

    def pack(self, pieces, dtype, lead=()):
        parts = []
        for key, _, rows, shape in self.entries:
            a = pieces[key].astype(dtype).reshape(lead + (-1,))
            n = int(np.prod(shape))
            a = jnp.pad(a, [(0, 0)] * len(lead) + [(0, rows * self.width - n)])
            a = a.reshape(lead + (rows, self.width))
            parts.append(_pad_rows(a, self.mult))
        out = jnp.concatenate(parts, axis=len(lead))
        return _pad_rows(out, self.total_mult)

    def unpack(self, packed, lead=()):
        out = {}
        for key, off, rows, shape in self.entries:
            a = lax.slice_in_dim(packed, off, off + rows, axis=len(lead))
            a = a.reshape(lead + (-1,))
            n = int(np.prod(shape))
            out[key] = lax.slice_in_dim(a, 0, n, axis=len(lead)).reshape(lead + shape)
        return out


def _ffn_fwd(x, h, mod, wg_t, wu_t, wd, tag, next_norm=None, gather=None):
    side = None if gather is None else (_GatherOps, gather)
    gate, up, act, *gathered = _mm_swiglu(h, wg_t, wu_t, f"ffn_gateup_{tag}", side=side)
    x_out, ffn_out, *h_next = _mm_resid([(act, wd)], x, mod[5:6], name=f"ffn_down_{tag}", norm=next_norm)
    return (x_out, dict(h=h, gate=gate, up=up, act=act, out=ffn_out), *h_next, *gathered)


def _ffn_bwd(dx_out, dffn, x_in, saved, mod, norm_w, wg_t, wu_t, wd, below, tag, exchange=None):
    side = None if exchange is None else (_ChipsOps, exchange)
    dgate, dup, *from_chips = _mm_swiglu_bwd(dffn, wd, saved["gate"], saved["up"], f"ffn_act_bwd_{tag}", side=side)
    d_wd = _mm_tn(saved["act"], dffn, name=f"ffn_dwd_{tag}")
    d_wg_t = _mm_tn(dgate, saved["h"], name=f"ffn_dwg_{tag}")
    d_wu_t = _mm_tn(dup, saved["h"], name=f"ffn_dwu_{tag}")
    dh = _mm([(dgate, wg_t), (dup, wu_t)], "nn", name=f"ffn_dh_{tag}")
    dx, d_below, acc = _norm_mod_bwd(x_in, dh, dx_out, norm_w, mod[4:5], below[0], below[1], f"ffn_norm_bwd_{tag}")
    return (dx, d_below, dict(d_wg=d_wg_t, d_wu=d_wu_t, d_wd=d_wd, acc=acc), *from_chips)


_BIG = [
    ("out_w", "out_w_even", 0, "row"), ("qkv_w", "qkv_w", 0, "col"), ("o_w", "o_w", 0, "row"),
    ("gate0", "ffn_gate_w", 0, "col"), ("up0", "ffn_up_w", 0, "col"), ("down0", "ffn_down_w", 0, "row"),
    ("gate1", "ffn_gate_w", 1, "col"), ("up1", "ffn_up_w", 1, "col"), ("down1", "ffn_down_w", 1, "row"),
    ("in_w", "in_w_even", 0, "col"),
]


def _to_wire(a, kind):
    return a.T if kind == "col" else a


_GATHER_GROUPS = [["in_w"], ["out_w"], ["gate0", "up0", "down0"], ["qkv_w", "o_w"], ["gate1", "up1", "down1"]]
_GRAD_GROUPS = [["gate1", "up1", "down1"], ["qkv_w", "o_w"], ["out_w", "gate0", "up0", "down0"], ["in_w"]]

_REPLICATED = ["ada_b", "norm_mix_w", "norm_ffn_w", "conv_b", "dt_bias", "a_log", "d_skip", "ssm_norm_w", "gmlp_ln_w",
               "gmlp_ln_b", "gmlp_ws", "gmlp_bs", "sinks", "rel_table", "final_norm_w"]
_TINY_SHARDED = ["conv_w", "qkv_b", "o_b"]

_WEIGHTS = ['ada_w', 'ada_b', 'norm_mix_w', 'norm_ffn_w', 'in_w_even', 'conv_w', 'conv_b', 'dt_bias', 'a_log', 'd_skip',
            'ssm_norm_w', 'gmlp_ln_w', 'gmlp_ln_b', 'gmlp_ws', 'gmlp_bs', 'out_w_even', 'qkv_w', 'qkv_b', 'o_w', 'o_b',
            'sinks', 'rel_table', 'ffn_gate_w', 'ffn_up_w', 'ffn_down_w', 'final_norm_w']


def _step(x, c, loss_target, W, M, V):
    T = x.shape[1]
    x0 = x[0]
    target = loss_target[0]
    me = 4 * lax.axis_index("x") + 2 * lax.axis_index("y") + lax.axis_index("c")

    w_wire_local = {key: _to_wire(W[name][layer].astype(BF16), kind) for key, name, layer, kind in _BIG}

    def wire_pack(keys, mult):
        gp = _Pack(D, 1, mult)
        for key in keys:
            gp.add(key, w_wire_local[key].shape)
        return gp

    gather_packs = [(gp, gp.pack(w_wire_local, BF16)) for gp in (wire_pack(keys, 16) for keys in _GATHER_GROUPS)]
    grad_packs = [wire_pack(keys, 64) for keys in _GRAD_GROUPS]
    full = {}

    def gathered_weights(group, gathered):
        shards = gather_packs[group][0].unpack(gathered, lead=(N_DEV,))
        full.update({key: a.reshape(-1, D) for key, a in shards.items()})


    small_in = _Pack(D, 8, 8)
    small_in.add("c", (1, D))
    small_in.add("conv_w", W["conv_w"][0].shape)
    small_in.add("qkv_b", W["qkv_b"][0].shape)
    small_in.add("o_b", W["o_b"][0].shape)
    sm = small_in.unpack(_all_gather(small_in.pack(
        dict(c=c, conv_w=W["conv_w"][0], qkv_b=W["qkv_b"][0], o_b=W["o_b"][0]), F32), "gather_small"), lead=(N_DEV,))
    c_all = sm["c"].reshape(N_DEV, D)
    conv_w_full = jnp.transpose(sm["conv_w"], (1, 0, 2)).reshape(SSM_CONV, CONV_DIM)
    qkv_b_full = sm["qkv_b"].reshape(1, QKV_DIM)
    o_b_full = sm["o_b"].reshape(1, D)

    ncol = W["ada_w"].shape[2]
    cond, mod_cols = _mod_matmul(c_all, W["ada_w"], "mod_matmul")
    mod_g = _all_gather(mod_cols.reshape(DEPTH * N_DEV, ncol), "gather_mod").reshape(N_DEV, DEPTH, N_DEV, ncol)
    mod_me = lax.dynamic_index_in_dim(mod_g, me, axis=2, keepdims=False)
    mod_me = jnp.transpose(mod_me, (1, 0, 2)).reshape(DEPTH, 6, D)
    mod_me = jnp.pad(mod_me, ((0, 0), (0, 2), (0, 0))).reshape(DEPTH * 8, D)
    ada_b_rows = jnp.pad(W["ada_b"].reshape(DEPTH, 6, D), ((0, 0), (0, 2), (0, 0))).reshape(DEPTH * 8, D)
    mod_all = _add_rows(mod_me, ada_b_rows, "mod_bias").reshape(DEPTH, 8, D)
    mod0, mod1 = mod_all[0], mod_all[1]
    h0, gathered_in = _norm_mod(x0, W["norm_mix_w"][0:1], mod0[1:2], mod0[0:1], "mix_norm_0",
                                side=(_GatherOps, gather_packs[0][1]))
    gathered_weights(0, gathered_in)

    in_t = full["in_w"]
    o1, o2, o3, o4 = SSM_INNER, SSM_INNER + CONV_DIM, SSM_INNER + CONV_DIM + SSM_HEADS, SSM_INNER + CONV_DIM + SSM_HEADS + GMLP_INNER
    w_z, w_xbc, w_dt, w_u, w_v = in_t[:o1], in_t[o1:o2], in_t[o2:o3], in_t[o3:o4], in_t[o4:]
    w_main = jnp.concatenate([w_z, w_u, w_v, w_xbc], axis=0)
    w_dtp = jnp.pad(w_dt, ((0, LANES - SSM_HEADS), (0, 0)))

    pad16 = lambda a: jnp.pad(a.reshape(1, SSM_HEADS), ((0, 0), (0, LANES - SSM_HEADS)))
    dtb, alog = pad16(W["dt_bias"][0]), pad16(W["a_log"][0])
    dskip = jnp.repeat(W["d_skip"][0], SSM_HEAD_DIM).reshape(1, SSM_INNER)
    ssm_nw = W["ssm_norm_w"]
    lnw, lnb = W["gmlp_ln_w"], W["gmlp_ln_b"]
    ws = W["gmlp_ws"][0]
    bs_exp = jnp.repeat(W["gmlp_bs"][0].T, CHUNK, axis=1)
    conv_b = W["conv_b"]
    nmw, nfw = W["norm_mix_w"], W["norm_ffn_w"]
    onehot_t = _bucket_onehot_t()
    head_ind = _head_indicator()
    bias = _bias_from_table(W["rel_table"].T, onehot_t, _window_mask_flat(), "rel_bias").reshape(ATTN_HEADS, CHUNK, 2 * CHUNK)
    sinks = W["sinks"][0]

    pm, gathered_a = _mm([(h0, w_main)], "nt", name="in_proj", tn_pref=1536, out_dtype=BF16,
                         side=(_GatherOps, gather_packs[1][1]))
    gathered_weights(1, gathered_a)
    out_w = full["out_w"]
    dtr = _mm([(h0, w_dtp)], "nt", name="in_proj_dt")
    cpre = _conv_fwd(pm, conv_w_full, conv_b, "conv_fwd")
    ya, ypre, sprev, gathered_b = _ssd_fwd(cpre, dtr, pm, dtb, alog, dskip, ssm_nw, head_ind, "ssd_fwd",
                                           side=(_GatherOps, gather_packs[2][1]))
    gathered_weights(2, gathered_b)
    yb, gathered_c = _gmlp_fwd(pm, lnw, lnb, ws, bs_exp, "gmlp_fwd", side=(_GatherOps, gather_packs[3][1]))
    gathered_weights(3, gathered_c)
    x1, mix0, hf0 = _mm_resid([(ya, out_w[:SSM_INNER]), (yb, out_w[SSM_INNER:])], x0, mod0[2:3], name="out_proj",
                              norm=(nfw[0:1], mod0[4:5], mod0[3:4]))
    x2, ffn0, h1, gathered_d = _ffn_fwd(x1, hf0, mod0, full["gate0"], full["up0"], full["down0"], "0",
                                        next_norm=(nmw[1:2], mod1[1:2], mod1[0:1]), gather=gather_packs[4][1])
    gathered_weights(4, gathered_d)
    qkv_t, o_w = full["qkv_w"], full["o_w"]
    w_q, w_k, w_v_att = qkv_t[:D], qkv_t[D:D + LANES], qkv_t[D + LANES:]

    qkv = _mm([(h1, qkv_t)], "nt", name="qkv_proj", bias=qkv_b_full, tn_pref=1280, out_dtype=BF16)
    att = _attn_fwd(qkv, bias, sinks, "attn_fwd")
    x3, mix1, hf1 = _mm_resid([(att, o_w)], x2, mod1[2:3], name="o_proj", bias=o_b_full,
                              norm=(nfw[1:2], mod1[4:5], mod1[3:4]))
    x4, ffn1 = _ffn_fwd(x3, hf1, mod1, full["gate1"], full["up1"], full["down1"], "1")

    dx4, dffn1, acc_f = _final_loss(x4, W["final_norm_w"].reshape(1, D), target, ffn1["out"], mod1[5:6], "final_loss")
    dx3, dmix1, gf1 = _ffn_bwd(dx4, dffn1, x3, ffn1, mod1, nfw[1:2], full["gate1"], full["up1"], full["down1"],
                               (mix1, mod1[2:3]), "1")
    g_wire = dict(gate1=gf1["d_wg"], up1=gf1["d_wu"], down1=gf1["d_wd"])

    def packed_partials(group):
        return grad_packs[group].pack({key: g_wire[key].reshape(N_DEV, -1, D) for key in _GRAD_GROUPS[group]},
                                      F32, lead=(N_DEV,))

    from_chips = {}
    partials_ffn1 = packed_partials(0)
    datt, theirs_ffn1 = _mm([(dmix1, o_w)], "nt", name="o_proj_dx", out_dtype=BF16, side=(_SiblingOps, partials_ffn1))
    pair_ffn1 = _pair_sum(partials_ffn1, theirs_ffn1, "grads_pair_sum_ffn1")
    d_o_w = _mm_tn(att, dmix1, name="o_proj_dw")
    dq, dk, dv, d_qkv_b, dbias, dsinks, from_chips[0] = _attn_bwd(qkv, datt, bias, sinks, "attn_bwd",
                                                                 side=(_ChipsOps, pair_ffn1))
    d_table = _table_from_dbias(dbias.reshape(ATTN_HEADS, -1), onehot_t, "rel_table_grad")
    d_qkv_t = jnp.concatenate([_mm_tn(dq, h1, name="qkv_dw_q"), _mm_tn(dk, h1, name="qkv_dw_k"), _mm_tn(dv, h1, name="qkv_dw_v")], axis=0)
    g_wire.update(qkv_w=d_qkv_t, o_w=d_o_w)
    partials_l1 = packed_partials(1)
    dh1, theirs_l1 = _mm([(dq, w_q), (dk, w_k), (dv, w_v_att)], "nn", name="qkv_proj_dx", side=(_SiblingOps, partials_l1))
    pair_l1 = _pair_sum(partials_l1, theirs_l1, "grads_pair_sum_l1")
    dx2, dffn0, acc_n1 = _norm_mod_bwd(x2, dh1, dx3, nmw[1:2], mod1[1:2], ffn0["out"], mod0[5:6], "mix_norm_bwd_1")

    dx1, dmix0, gf0, from_chips[1] = _ffn_bwd(dx2, dffn0, x1, ffn0, mod0, nfw[0:1], full["gate0"], full["up0"], full["down0"],
                                              (mix0, mod0[2:3]), "0", exchange=pair_l1)

    dya = _mm([(dmix0, out_w[:SSM_INNER])], "nt", name="out_proj_dx_a")
    dyb = _mm([(dmix0, out_w[SSM_INNER:])], "nt", name="out_proj_dx_b")
    d_out_w = jnp.concatenate([_mm_tn(ya, dmix0, name="out_proj_dw_a"), _mm_tn(yb, dmix0, name="out_proj_dw_b")], axis=0)
    g_wire.update(gate0=gf0["d_wg"], up0=gf0["d_wu"], down0=gf0["d_wd"], out_w=d_out_w)
    partials_ffn0 = packed_partials(2)
    du, dvg, d_ws, d_bs, acc_ln, theirs_ffn0 = _gmlp_bwd(pm, dyb, lnw, lnb, ws, bs_exp, "gmlp_bwd",
                                                         side=(_SiblingOps, partials_ffn0))
    pair_ffn0 = _pair_sum(partials_ffn0, theirs_ffn0, "grads_pair_sum_mix")
    dz, dcpre, ddtr, acc_ssd, d_ssm_nw, from_chips[2] = _ssd_bwd(
        cpre, dtr, pm, ypre, sprev, dya, dtb, alog, dskip, ssm_nw, head_ind, "ssd_bwd", side=(_ChipsOps, pair_ffn0))
    acc_f0, acc_f1 = gf0["acc"], gf1["acc"]
    row = lambda i: slice(i, i + 1)
    views = dict(ada_b=(DEPTH * 6, D), norm_mix_w=(DEPTH, D), norm_ffn_w=(DEPTH, D), conv_b=(1, CONV_DIM),
                 dt_bias=(1, SSM_HEADS), a_log=(1, SSM_HEADS), d_skip=(1, SSM_HEADS), ssm_norm_w=(1, D), gmlp_ln_w=(1, D),
                 gmlp_ln_b=(1, D), gmlp_ws=(GMLP_GROUPS * CHUNK, CHUNK), gmlp_bs=(GMLP_GROUPS, CHUNK),
                 sinks=(1, ATTN_HEADS), rel_table=(REL_BUCKETS, ATTN_HEADS), final_norm_w=(1, D),
                 conv_w=(SSM_CONV, CONV_DIM), qkv_b=(1, QKV_DIM), o_b=(1, D), loss=(1, D))
    late_keys = ["ada_b", "norm_mix_w", "conv_b", "dt_bias", "a_log", "d_skip", "ssm_norm_w", "conv_w"]
    canvas_early, canvas_late = _Canvas(), _Canvas()
    for key, (r, cdim) in views.items():
        (canvas_late if key in late_keys else canvas_early).add(key, r, cdim, blocks=GMLP_GROUPS if key == "gmlp_ws" else 1)
    early_sources = [
        (acc_f0, [("norm_ffn_w", row(2), 0)]),
        (acc_f1, [("norm_ffn_w", row(2), 1), ("o_b", row(4), 0)]),
        (acc_f, [("final_norm_w", row(0), 0), ("loss", row(1), 0)]),
        (acc_ln, [("gmlp_ln_w", row(0), 0), ("gmlp_ln_b", row(1), 0)]),
        (d_ws.reshape(GMLP_GROUPS * CHUNK, CHUNK), [("gmlp_ws", slice(0, GMLP_GROUPS * CHUNK), 0)]),
        (d_bs.T, [("gmlp_bs", slice(0, GMLP_GROUPS), 0)]),
        (dsinks, [("sinks", row(0), 0)]),
        (d_table, [("rel_table", slice(0, REL_BUCKETS), 0)]),
        (d_qkv_b, [("qkv_b", row(0), 0)]),
    ]
    dxbc, acc_conv, parts_early = _conv_bwd(dcpre, pm, conv_w_full, "conv_bwd",
                                            side=(_GatherOps, _canvas_fill(canvas_early, early_sources, "small_grads_early")))
    d_in_t = jnp.concatenate([
        _mm_tn(dz, h0, name="in_dw_z"), _mm_tn(dxbc, h0, name="in_dw_xbc"),
        _mm_tn(ddtr, h0, name="in_dw_dt")[:SSM_HEADS], _mm_tn(du, h0, name="in_dw_u"), _mm_tn(dvg, h0, name="in_dw_v")], axis=0)
    g_wire.update(in_w=d_in_t)
    partials_mix = packed_partials(3)
    theirs_mix = _comm_call(_SiblingOps, partials_mix, "exchange_grads_sibling")
    pair_mix = _pair_sum(partials_mix, theirs_mix, "grads_pair_sum")
    dh0, from_chips[3] = _mm([(dz, w_z), (dxbc, w_xbc), (ddtr, w_dtp), (du, w_u), (dvg, w_v)], "nn", name="in_proj_dx",
                             side=(_ChipsOps, pair_mix))
    grad_x, acc_n0 = _norm_mod_bwd(x0, dh0, dx1, nmw[0:1], mod0[1:2], None, None, "mix_norm_bwd_0")

    g_mine = {}
    for group in range(len(_GRAD_GROUPS)):
        g_mine.update(grad_packs[group].unpack(_sum_parts(from_chips[group], f"grads_chip_sum_{group}")))
    res_big = [{}, {}, {}, {}]
    for key, name, layer, kind in _BIG:
        g_nat = _to_wire(g_mine[key], kind)
        outs = _adamw(g_nat[None], W[name][layer], M[name][layer], V[name][layer], f"adamw_{key}")
        for res, out in zip(res_big, outs):
            res[key] = out

    late_sources = [
        (acc_n0, [("ada_b", row(1), 0), ("ada_b", row(0), 1), ("norm_mix_w", row(2), 0)]),
        (acc_f0, [("ada_b", row(3), 2), ("ada_b", row(1), 3), ("ada_b", row(0), 4)]),
        (acc_n1, [("ada_b", row(3), 5), ("ada_b", row(1), 6), ("ada_b", row(0), 7), ("norm_mix_w", row(2), 1)]),
        (acc_f1, [("ada_b", row(3), 8), ("ada_b", row(1), 9), ("ada_b", row(0), 10)]),
        (acc_f, [("ada_b", row(3), 11)]),
        (acc_conv, [("conv_w", slice(0, SSM_CONV), 0), ("conv_b", row(4), 0)]),
        (acc_ssd, [("dt_bias", row(0), 0), ("a_log", row(1), 0), ("d_skip", row(2), 0)]),
        (d_ssm_nw, [("ssm_norm_w", row(0), 0)]),
    ]
    parts_small = _all_gather(_canvas_fill(canvas_late, late_sources, "small_grads_late"), "gather_small_grads")
    view = lambda a, key: a.reshape(views[key])
    params = lambda keys: [(n, view(W[n], n), view(M[n], n), view(V[n], n)) for n in _REPLICATED if (n in late_keys) == keys]
    small_out = _adamw_canvas(canvas_early, parts_early, params(False), ["qkv_b", "o_b", "loss"], "adamw_small_early")
    small_out.update(_adamw_canvas(canvas_late, parts_small, params(True), ["conv_w"], "adamw_small_late"))
    loss = small_out["loss"][0][0, 0]
    res_small = [{name: small_out[name][k].reshape(W[name].shape) for name in _REPLICATED} for k in range(4)]

    n_cw, n_qb, n_ob = W["conv_w"].shape[2], W["qkv_b"].shape[1], W["o_b"].shape[1]
    g_tiny = dict(conv_w=lax.dynamic_slice_in_dim(small_out["conv_w"][0], me * n_cw, n_cw, axis=1)[None],
                  qkv_b=lax.dynamic_slice_in_dim(small_out["qkv_b"][0], me * n_qb, n_qb, axis=1),
                  o_b=lax.dynamic_slice_in_dim(small_out["o_b"][0], me * n_ob, n_ob, axis=1))
    tiny = _Pack(D, 8, 8)
    for name in _TINY_SHARDED:
        tiny.add(name, W[name].shape)
    pkt = lambda S: tiny.pack({name: S[name] for name in _TINY_SHARDED}, F32)
    res_tiny = [tiny.unpack(r) for r in _adamw(pkt(g_tiny)[None], pkt(W), pkt(M), pkt(V), "adamw_tiny")]

    dmod_all = parts_small[:, canvas_late.offset["ada_b"]:canvas_late.offset["ada_b"] + DEPTH * 6].reshape(N_DEV, DEPTH, 6 * D)
    dmod_cols = jnp.transpose(lax.dynamic_slice_in_dim(dmod_all, me * ncol, ncol, axis=2), (1, 0, 2))
    g_ada_w = _ada_w_grad(cond, dmod_cols, "ada_w_grad")
    flat = lambda a: a.reshape(DEPTH * D, ncol)
    res_ada = [r.reshape(DEPTH, D, ncol) for r in _adamw(flat(g_ada_w)[None], flat(W["ada_w"]), flat(M["ada_w"]), flat(V["ada_w"]), "adamw_ada_w")]

    def result(kind_idx, name):
        if name == "ada_w":
            return res_ada[kind_idx]
        if name in _REPLICATED:
            return res_small[kind_idx][name]
        if name in _TINY_SHARDED:
            return res_tiny[kind_idx][name]
        pieces = [res_big[kind_idx][key] for key, nm, layer, kind in _BIG if nm == name]
        return jnp.stack(pieces)

    outs = [loss, grad_x[None]]
    for kind_idx in range(4):
        outs += [result(kind_idx, name) for name in _WEIGHTS]
    return tuple(outs)


def kernel(x, c, ada_w, ada_b, norm_mix_w, norm_ffn_w, in_w_even, conv_w, conv_b, dt_bias, a_log, d_skip, ssm_norm_w, gmlp_ln_w, gmlp_ln_b, gmlp_ws, gmlp_bs, out_w_even, qkv_w, qkv_b, o_w, o_b, sinks, rel_table, ffn_gate_w, ffn_up_w, ffn_down_w, final_norm_w, loss_target, m_ada_w, m_ada_b, m_norm_mix_w, m_norm_ffn_w, m_in_w_even, m_conv_w, m_conv_b, m_dt_bias, m_a_log, m_d_skip, m_ssm_norm_w, m_gmlp_ln_w, m_gmlp_ln_b, m_gmlp_ws, m_gmlp_bs, m_out_w_even, m_qkv_w, m_qkv_b, m_o_w, m_o_b, m_sinks, m_rel_table, m_ffn_gate_w, m_ffn_up_w, m_ffn_down_w, m_final_norm_w, v_ada_w, v_ada_b, v_norm_mix_w, v_norm_ffn_w, v_in_w_even, v_conv_w, v_conv_b, v_dt_bias, v_a_log, v_d_skip, v_ssm_norm_w, v_gmlp_ln_w, v_gmlp_ln_b, v_gmlp_ws, v_gmlp_bs, v_out_w_even, v_qkv_w, v_qkv_b, v_o_w, v_o_b, v_sinks, v_rel_table, v_ffn_gate_w, v_ffn_up_w, v_ffn_down_w, v_final_norm_w):
    args = locals()
    W = {n: args[n] for n in _WEIGHTS}
    M = {n: args["m_" + n] for n in _WEIGHTS}
    V = {n: args["v_" + n] for n in _WEIGHTS}
    return _step(x, c, loss_target, W, M, V)
```

```python
import functools
import math

import numpy as np
import jax
import jax.numpy as jnp
from jax import lax
from jax.experimental import pallas as pl
from jax.experimental.pallas import tpu as pltpu

F32 = jnp.float32
BF16 = jnp.bfloat16
HIGHEST = lax.Precision.HIGHEST
MESH = pl.DeviceIdType.MESH

N_DEV = 8
D = 1024
DEPTH = 2
SSM_HEADS = 16
SSM_HEAD_DIM = 64
SSM_INNER = 1024
SSM_GROUPS = 2
SSM_STATE = 128
SSM_CONV = 4
CHUNK = 128
CONV_DIM = SSM_INNER + 2 * SSM_GROUPS * SSM_STATE
GMLP_GROUPS = 8
GMLP_INNER = 1024
IN_EVEN = 4624
ATTN_HEADS = 16
ATTN_KV = 2
ATTN_DH = 64
QKV_DIM = 1280
REL_BUCKETS = 32
REL_MAX_DIST = 128
FFN = 2816
EPS = 1e-6
NEG_INF = -1e30
LANES = 128

ADAM_LR = 0.001
ADAM_B1 = 0.9
ADAM_B2 = 0.999
ADAM_EPS = 1e-08
ADAM_WD = 0.01
ADAM_STEP = 10

VMEM_LIMIT_BYTES = 56 * 1024 * 1024
ROW_TILE = 512


def _pcall(body, *, name, out_shape, grid=(), in_specs=None, out_specs=None, scratch=(), sem=None):
    params = dict(vmem_limit_bytes=VMEM_LIMIT_BYTES)
    if sem is not None:
        params["dimension_semantics"] = sem
    specs = {} if in_specs is None else dict(in_specs=in_specs, out_specs=out_specs)
    return pl.pallas_call(
        body, name=name, out_shape=out_shape, grid=grid, **specs,
        scratch_shapes=list(scratch), compiler_params=pltpu.CompilerParams(**params))


def _call(body, args, side=None, *, name, out_shape, grid, in_specs, out_specs, scratch=(), sem=None):
    if side is None:
        return _pcall(body, name=name, out_shape=out_shape, grid=grid, in_specs=in_specs, out_specs=out_specs,
                      scratch=scratch, sem=sem)(*args)
    ops_cls, x = side
    n_in, n_out, n_scr = len(in_specs), len(out_shape), len(scratch)
    steps = int(np.prod(grid))
    hbm = pl.BlockSpec(memory_space=pl.ANY)

    def wrapped(*refs):
        ins, x_ref = refs[:n_in], refs[n_in]
        outs, r_ref = refs[n_in + 1:n_in + 1 + n_out], refs[n_in + 1 + n_out]
        scr, sems = refs[n_in + 2 + n_out:n_in + 2 + n_out + n_scr], refs[n_in + 2 + n_out + n_scr:]
        ops = ops_cls(x_ref, r_ref, *sems)
        step = pl.program_id(0)
        for axis in range(1, len(grid)):
            step = step * grid[axis] + pl.program_id(axis)
        pl.when(step == 0)(ops.start)
        body(*ins, *outs, *scr)
        pl.when(step == (3 * steps) // 4)(ops.forward)
        pl.when(step == steps - 1)(ops.finish)

    return _pcall(
        wrapped, name=name, out_shape=list(out_shape) + [ops_cls.result(x)], grid=grid,
        in_specs=list(in_specs) + [hbm], out_specs=list(out_specs) + [hbm],
        scratch=list(scratch) + ops_cls.scratch(), sem=("arbitrary",) * len(grid))(*args, x)


def _tile(n, pref):
    if n <= pref:
        return n
    best = None
    for t in range(LANES, pref + 1, LANES):
        if n % t == 0:
            best = t
    assert best is not None, (n, pref)
    return best


def _rows(T):
    return min(ROW_TILE, T)


def _sds(shape, dtype=F32):
    return jax.ShapeDtypeStruct(shape, dtype)


def _row_spec(tm, c, col=0):
    return pl.BlockSpec((tm, c), lambda i, col=col: (i, col))


def _vec_spec(c, r=1):
    return pl.BlockSpec((r, c), lambda i: (0, 0))


def _sigmoid(x):
    return jax.nn.sigmoid(x)


def _silu(x):
    return x * _sigmoid(x)


def _dsilu(x):
    s = _sigmoid(x)
    return s * (1.0 + x * (1.0 - s))


def _silu_and_grad(x):
    s = _sigmoid(x)
    return x * s, s * (1.0 + x * (1.0 - s))


def _gelu(x):
    return 0.5 * x * (1.0 + lax.erf(x * 0.7071067811865476))


def _gelu_and_grad(x):
    phi = 0.5 * (1.0 + lax.erf(x * 0.7071067811865476))
    return x * phi, phi + x * jnp.exp(-0.5 * x * x) * 0.3989422804014327


def _dot(a, b, dims, precision=None):
    return lax.dot_general(a, b, (dims, ((), ())), precision=precision, preferred_element_type=F32)


def _nn(a, b, precision=None):
    return _dot(a, b, ((1,), (0,)), precision)


def _nt(a, b, precision=None):
    return _dot(a, b, ((1,), (1,)), precision)


def _tn(a, b, precision=None):
    return _dot(a, b, ((0,), (0,)), precision)


def _bf(x):
    return x.astype(BF16)


def _colsum(x):
    return jnp.sum(x, axis=0, keepdims=True)


def _rowsum(x):
    return jnp.sum(x, axis=1, keepdims=True)


def _allsum(x):
    return _colsum(_rowsum(x))


def _comm_call(ops_cls, x, name, from_vmem=False):
    def body(x_ref, out_ref, *sems):
        ops = ops_cls(x_ref, out_ref, *sems)
        ops.start()
        ops.forward()
        ops.finish()

    return pl.pallas_call(
        body, name=name, out_shape=ops_cls.result(x),
        in_specs=[pl.BlockSpec(memory_space=pltpu.VMEM if from_vmem else pl.ANY)],
        out_specs=pl.BlockSpec(memory_space=pl.ANY), scratch_shapes=ops_cls.scratch(),
    )(x)


def _all_gather(x, name):
    return _comm_call(_GatherOps, x, name, from_vmem=True)


class _GatherOps:
    def __init__(self, x_ref, out_ref, send_sems, recv_sems, local_sem):
        self.x_ref, self.out_ref = x_ref, out_ref
        self.send_sems, self.recv_sems, self.local_sem = send_sems, recv_sems, local_sem
        mx, my, mc = lax.axis_index("x"), lax.axis_index("y"), lax.axis_index("c")
        self.mc = mc
        self.me, self.sibling = (mx, my, mc), (mx, my, 1 - mc)
        self.chips = [(1 - mx, my), (mx, 1 - my), (1 - mx, 1 - my)]

    @staticmethod
    def result(x):
        return _sds((N_DEV,) + x.shape, x.dtype)

    @staticmethod
    def scratch():
        return [pltpu.SemaphoreType.DMA((7,)), pltpu.SemaphoreType.DMA((7,)), pltpu.SemaphoreType.DMA(())]

    def _slot(self, px, py, pc):
        return self.out_ref.at[4 * px + 2 * py + pc]

    def _copy(self, k, block, to, own=False):
        return pltpu.make_async_remote_copy(
            src_ref=self.x_ref if own else self._slot(*block), dst_ref=self._slot(*block),
            send_sem=self.send_sems.at[k], recv_sem=self.recv_sems.at[k], device_id=to, device_id_type=MESH)

    def _mine(self):
        return pltpu.make_async_copy(self.x_ref, self._slot(*self.me), self.local_sem)

    def _first(self):
        return [self._copy(0, self.me, self.sibling, own=True)] + [
            self._copy(1 + j, self.me, (*chip, self.mc), own=True) for j, chip in enumerate(self.chips)]

    def _passed(self):
        return [self._copy(4 + j, (*chip, self.mc), self.sibling) for j, chip in enumerate(self.chips)]

    def start(self):
        self._mine().start()
        for cp in self._first():
            cp.start()

    def forward(self):
        passed = self._passed()
        for j, chip in enumerate(self.chips):
            self._copy(1 + j, (*chip, self.mc), self.me).wait_recv()
            passed[j].start()

    def finish(self):
        self._copy(0, self.sibling, self.me).wait_recv()
        for j, chip in enumerate(self.chips):
            self._copy(4 + j, (*chip, 1 - self.mc), self.me).wait_recv()
        for cp in self._first() + self._passed():
            cp.wait_send()
        self._mine().wait()


N_CHIP = 4


class _SiblingOps:
    def __init__(self, p_ref, theirs_ref, send_sems, recv_sems):
        self.p_ref, self.theirs_ref, self.send_sems, self.recv_sems = p_ref, theirs_ref, send_sems, recv_sems

    @staticmethod
    def result(p):
        return _sds((N_CHIP,) + p.shape[1:], p.dtype)

    @staticmethod
    def scratch():
        return [pltpu.SemaphoreType.DMA((N_CHIP,))] * 2

    def _copies(self):
        mx, my, mc = lax.axis_index("x"), lax.axis_index("y"), lax.axis_index("c")
        return [pltpu.make_async_remote_copy(
            src_ref=self.p_ref.at[2 * chip + 1 - mc], dst_ref=self.theirs_ref.at[chip],
            send_sem=self.send_sems.at[chip], recv_sem=self.recv_sems.at[chip],
            device_id=(mx, my, 1 - mc), device_id_type=MESH) for chip in range(N_CHIP)]

    def start(self):
        for cp in self._copies():
            cp.start()

    def forward(self):
        pass

    def finish(self):
        for cp in self._copies():
            cp.wait()


class _ChipsOps:
    def __init__(self, q_ref, out_ref, send_sems, recv_sems, local_sem):
        self.q_ref, self.out_ref = q_ref, out_ref
        self.send_sems, self.recv_sems, self.local_sem = send_sems, recv_sems, local_sem

    @staticmethod
    def result(q):
        return _sds(q.shape, q.dtype)

    @staticmethod
    def scratch():
        return [pltpu.SemaphoreType.DMA((N_CHIP - 1,)), pltpu.SemaphoreType.DMA((N_CHIP - 1,)), pltpu.SemaphoreType.DMA(())]

    def _copies(self):
        mx, my, mc = lax.axis_index("x"), lax.axis_index("y"), lax.axis_index("c")
        me = 2 * mx + my
        local = pltpu.make_async_copy(self.q_ref.at[me], self.out_ref.at[me], self.local_sem)
        remote = []
        for r in range(1, N_CHIP):
            px = 1 - mx if r & 2 else mx
            py = 1 - my if r & 1 else my
            remote.append(pltpu.make_async_remote_copy(
                src_ref=self.q_ref.at[2 * px + py], dst_ref=self.out_ref.at[me],
                send_sem=self.send_sems.at[r - 1], recv_sem=self.recv_sems.at[r - 1],
                device_id=(px, py, mc), device_id_type=MESH))
        return local, remote

    def start(self):
        local, remote = self._copies()
        local.start()
        for cp in remote:
            cp.start()

    def forward(self):
        pass

    def finish(self):
        local, remote = self._copies()
        for cp in remote:
            cp.wait()
        local.wait()


def _pair_sum(p, theirs, name):
    n, R, C = theirs.shape
    tr = _tile_rows(R, 256)

    def body(p_ref, t_ref, o_ref):
        mc = lax.axis_index("c")
        o_ref[0] = (p_ref[0, mc] + t_ref[0]).astype(BF16)

    blk = pl.BlockSpec((1, tr, C), lambda s, i: (s, i, 0))
    return _pcall(body, name=name, out_shape=_sds((n, R, C), BF16), grid=(n, R // tr),
                  in_specs=[pl.BlockSpec((1, 2, tr, C), lambda s, i: (s, 0, i, 0)), blk],
                  out_specs=blk, sem=("parallel", "parallel"))(p.reshape(n, 2, R, C), theirs)


def _sum_parts(parts, name):
    P, R, C = parts.shape
    tr = _tile_rows(R, 256)

    def body(p_ref, o_ref):
        g = p_ref[0].astype(F32)
        for k in range(1, P):
            g = g + p_ref[k].astype(F32)
        o_ref[...] = g

    return _pcall(body, name=name, out_shape=_sds((R, C)), grid=(R // tr,),
                  in_specs=[pl.BlockSpec((P, tr, C), lambda i: (0, i, 0))],
                  out_specs=pl.BlockSpec((tr, C), lambda i: (i, 0)), sem=("parallel",))(parts)


def _adamw(parts, w, m, v, name):
    P, R, C = parts.shape
    tr = R if R <= 256 else _tile_rows(R, 256)

    def body(p_ref, w_ref, m_ref, v_ref, g_ref, d_ref, nm_ref, nv_ref):
        g = p_ref[0]
        for k in range(1, P):
            g = g + p_ref[k]
        nm = ADAM_B1 * m_ref[...] + (1.0 - ADAM_B1) * g
        nv = ADAM_B2 * v_ref[...] + (1.0 - ADAM_B2) * (g * g)
        m_hat = nm / (1.0 - ADAM_B1 ** ADAM_STEP)
        v_hat = nv / (1.0 - ADAM_B2 ** ADAM_STEP)
        g_ref[...] = g
        d_ref[...] = -ADAM_LR * (m_hat / (jnp.sqrt(v_hat) + ADAM_EPS) + ADAM_WD * w_ref[...])
        nm_ref[...] = nm
        nv_ref[...] = nv

    blk = pl.BlockSpec((tr, C), lambda i: (i, 0))
    return _pcall(
        body, name=name, out_shape=[_sds((R, C))] * 4, grid=(R // tr,),
        in_specs=[pl.BlockSpec((P, tr, C), lambda i: (0, i, 0)), blk, blk, blk],
        out_specs=[blk] * 4, sem=("parallel",))(parts, w, m, v)


def _adam_update(g, w, m, v):
    nm = ADAM_B1 * m + (1.0 - ADAM_B1) * g
    nv = ADAM_B2 * v + (1.0 - ADAM_B2) * (g * g)
    m_hat = nm / (1.0 - ADAM_B1 ** ADAM_STEP)
    v_hat = nv / (1.0 - ADAM_B2 ** ADAM_STEP)
    return -ADAM_LR * (m_hat / (jnp.sqrt(v_hat) + ADAM_EPS) + ADAM_WD * w), nm, nv


class _Canvas:
    def __init__(self):
        self.offset, self.views, self.rows = {}, {}, 0

    def add(self, key, r, c, blocks=1):
        need = r // blocks if blocks > 1 else r * (-(-c // D))
        if need >= 8:
            self.rows = -(-self.rows // 8) * 8
        self.offset[key], self.views[key] = self.rows, (r, c, blocks)
        self.rows += need

    @property
    def total(self):
        return -(-self.rows // 8) * 8

    def cells(self, key):
        (r, c, blocks), off = self.views[key], self.offset[key]
        if blocks > 1:
            n = r // blocks
            return [(off, n, b * c, c, slice(b * n, (b + 1) * n), slice(0, c)) for b in range(blocks)]
        if c <= D:
            return [(off, r, 0, c, slice(0, r), slice(0, c))]
        chunks = -(-c // D)
        return [(off + i * chunks + j, 1, 0, min(D, c - j * D), slice(i, i + 1), slice(j * D, min(c, (j + 1) * D)))
                for i in range(r) for j in range(chunks)]


def _canvas_fill(canvas, sources, name):
    arrays = [a for a, _ in sources]

    def body(*refs):
        out_ref = refs[-1]
        out_ref[...] = jnp.zeros_like(out_ref)
        for ref, (_, items) in zip(refs[:-1], sources):
            for key, src_rows, view_row in items:
                n = src_rows.stop - src_rows.start
                for row, count, lane, width, vrows, vcols in canvas.cells(key):
                    lo, hi = max(vrows.start, view_row), min(vrows.stop, view_row + n)
                    if lo < hi:
                        src = slice(src_rows.start + lo - view_row, src_rows.start + hi - view_row)
                        out_ref[row + lo - vrows.start:row + hi - vrows.start, lane:lane + width] = ref[src, vcols]

    return _pcall(body, name=name, out_shape=_sds((canvas.total, D)))(*arrays)


def _adamw_canvas(canvas, parts, params, sum_only, name):
    n = len(params)

    def body(p_ref, *refs):
        sum_ref = refs[-1]
        g_all = p_ref[0]
        for k in range(1, N_DEV):
            g_all = g_all + p_ref[k]
        sum_ref[...] = g_all
        for i, (key, _, _, _) in enumerate(params):
            w_ref, m_ref, v_ref = refs[3 * i:3 * i + 3]
            outs = refs[3 * n + 4 * i:3 * n + 4 * i + 4]
            for row, count, lane, width, vrows, vcols in canvas.cells(key):
                g = sum_ref[row:row + count, lane:lane + width]
                delta, nm, nv = _adam_update(g, w_ref[vrows, vcols], m_ref[vrows, vcols], v_ref[vrows, vcols])
                for ref, val in zip(outs, (g, delta, nm, nv)):
                    ref[vrows, vcols] = val
        for i, key in enumerate(sum_only):
            for row, count, lane, width, vrows, vcols in canvas.cells(key):
                refs[7 * n + i][vrows, vcols] = sum_ref[row:row + count, lane:lane + width]

    args = [a for _, w, m, v in params for a in (w, m, v)]
    out_shape = [_sds(w.shape) for _, w, _, _ in params for _ in range(4)] + [_sds(canvas.views[k][:2]) for k in sum_only]
    res = _pcall(body, name=name, out_shape=out_shape, scratch=[pltpu.VMEM(parts.shape[1:], F32)])(parts, *args)
    out = {key: res[4 * i:4 * i + 4] for i, (key, _, _, _) in enumerate(params)}
    out.update({key: [res[4 * n + i]] for i, key in enumerate(sum_only)})
    return out


def _tile_rows(n, pref):
    best = None
    for t in range(8, pref + 1, 8):
        if n % t == 0:
            best = t
    assert best is not None, (n, pref)
    return best


def _mm(pairs, mode, *, name, out_dtype=F32, bias=None, tn_pref=1024, side=None):
    M = pairs[0][0].shape[0]
    N = pairs[0][1].shape[1] if mode == "nn" else pairs[0][1].shape[0]
    tm, tn = _rows(M), _tile(N, tn_pref)
    n_pairs = len(pairs)
    has_bias = bias is not None

    def body(*refs):
        acc = _pairs_dot(refs[:2 * n_pairs], mode)
        if has_bias:
            acc = acc + refs[2 * n_pairs][...]
        refs[-1][...] = acc.astype(refs[-1].dtype)

    in_specs, args = _pair_specs(pairs, mode, tm, tn)
    if has_bias:
        in_specs.append(pl.BlockSpec((1, tn), lambda j, i: (0, j)))
        args.append(bias)
    res = _call(body, args, side, name=name, out_shape=[_sds((M, N), out_dtype)], grid=(N // tn, M // tm),
                in_specs=in_specs, out_specs=[pl.BlockSpec((tm, tn), lambda j, i: (i, j))], sem=("parallel", "parallel"))
    return res[0] if side is None else (res[0], res[1])


def _pairs_dot(ab, mode):
    acc = None
    for p in range(len(ab) // 2):
        a, b = _bf(ab[2 * p][...]), _bf(ab[2 * p + 1][...])
        d = _nn(a, b) if mode == "nn" else _nt(a, b)
        acc = d if acc is None else acc + d
    return acc


def _pair_specs(pairs, mode, tm, tn):
    in_specs, args = [], []
    for a, b in pairs:
        K = a.shape[1]
        in_specs.append(pl.BlockSpec((tm, K), lambda j, i: (i, 0)))
        if mode == "nn":
            in_specs.append(pl.BlockSpec((K, tn), lambda j, i: (0, j)))
        else:
            in_specs.append(pl.BlockSpec((tn, K), lambda j, i: (j, 0)))
        args += [a, b]
    return in_specs, args


def _mm_resid(pairs, resid, gvec, *, name, bias=None, norm=None):
    M, N = resid.shape
    tm = _rows(M)
    n_pairs = len(pairs)
    has_bias, has_norm = bias is not None, norm is not None

    def body(*refs):
        acc = _pairs_dot(refs[:2 * n_pairs], "nn")
        pos = 2 * n_pairs
        if has_bias:
            acc = acc + refs[pos][...]
            pos += 1
        xv = refs[pos][...] + refs[pos + 1][...] * acc
        outs = refs[pos + 2 + 3 * has_norm:]
        outs[0][...] = xv
        outs[1][...] = acc.astype(BF16)
        if has_norm:
            w_ref, sc_ref, sh_ref = refs[pos + 2:pos + 5]
            r = lax.rsqrt(jnp.mean(xv * xv, axis=-1, keepdims=True) + EPS)
            outs[2][...] = ((xv * r * w_ref[...]) * (1.0 + sc_ref[...]) + sh_ref[...]).astype(BF16)

    in_specs, args = _pair_specs(pairs, "nn", tm, N)
    vec = pl.BlockSpec((1, N), lambda j, i: (0, 0))
    row = pl.BlockSpec((tm, N), lambda j, i: (i, 0))
    if has_bias:
        in_specs.append(vec)
        args.append(bias)
    in_specs += [row, vec] + [vec] * (3 * has_norm)
    args += [resid, gvec] + (list(norm) if has_norm else [])
    return _pcall(body, name=name, out_shape=[_sds((M, N)), _sds((M, N), BF16)] + [_sds((M, N), BF16)] * has_norm,
                  grid=(1, M // tm), in_specs=in_specs, out_specs=[row] * (2 + has_norm),
                  sem=("parallel", "parallel"))(*args)


def _mm_tn(a, b, *, name, tm_pref=1408, tn_pref=1536):
    K, M = a.shape
    N = b.shape[1]
    tm, tn = _tile(M, tm_pref), _tile(N, tn_pref)
    tk = K if K <= 2 * ROW_TILE else 2 * ROW_TILE

    def body(a_ref, b_ref, o_ref):
        @pl.when(pl.program_id(2) == 0)
        def _():
            o_ref[...] = jnp.zeros_like(o_ref)
        o_ref[...] += _tn(_bf(a_ref[...]), _bf(b_ref[...]))

    return _pcall(
        body, name=name, out_shape=_sds((M, N)), grid=(M // tm, N // tn, K // tk),
        in_specs=[pl.BlockSpec((tk, tm), lambda i, j, k: (k, i)), pl.BlockSpec((tk, tn), lambda i, j, k: (k, j))],
        out_specs=pl.BlockSpec((tm, tn), lambda i, j, k: (i, j)),
        sem=("parallel", "parallel", "arbitrary"))(a, b)


def _mm_swiglu(h, wg_t, wu_t, name, side=None):
    M, K = h.shape
    N = wg_t.shape[0]
    tm, tn = _rows(M), _tile(N, 1408)

    def body(h_ref, wg_ref, wu_ref, gate_ref, up_ref, act_ref):
        hv = _bf(h_ref[...])
        gate = _nt(hv, wg_ref[...])
        up = _nt(hv, wu_ref[...])
        gate_ref[...] = gate.astype(BF16)
        up_ref[...] = up.astype(BF16)
        act_ref[...] = (_silu(gate) * up).astype(BF16)

    w_spec = pl.BlockSpec((tn, K), lambda j, i: (j, 0))
    o_spec = pl.BlockSpec((tm, tn), lambda j, i: (i, j))
    return _call(body, (h, wg_t, wu_t), side, name=name,
                 out_shape=[_sds((M, N), BF16)] * 3, grid=(N // tn, M // tm),
                 in_specs=[pl.BlockSpec((tm, K), lambda j, i: (i, 0)), w_spec, w_spec], out_specs=[o_spec] * 3,
                 sem=("parallel", "parallel"))


def _mm_swiglu_bwd(dout, wd, gate, up, name, side=None):
    M, K = dout.shape
    N = wd.shape[0]
    tm, tn = _rows(M), _tile(N, 1408)

    def body(d_ref, wd_ref, gate_ref, up_ref, dg_ref, du_ref):
        dact = _nt(_bf(d_ref[...]), wd_ref[...])
        act, dact_dg = _silu_and_grad(gate_ref[...].astype(F32))
        dg_ref[...] = (dact * up_ref[...].astype(F32) * dact_dg).astype(BF16)
        du_ref[...] = (dact * act).astype(BF16)

    t_spec = pl.BlockSpec((tm, tn), lambda j, i: (i, j))
    return _call(
        body, (dout, wd, gate, up), side, name=name, out_shape=[_sds((M, N), BF16)] * 2, grid=(N // tn, M // tm),
        in_specs=[pl.BlockSpec((tm, K), lambda j, i: (i, 0)), pl.BlockSpec((tn, K), lambda j, i: (j, 0)), t_spec, t_spec],
        out_specs=[t_spec] * 2, sem=("parallel", "parallel"))


def _norm_mod(x, w, sc, sh, name, side=None):
    T = x.shape[0]
    tm = _rows(T)

    def body(x_ref, w_ref, sc_ref, sh_ref, o_ref):
        xv = x_ref[...]
        r = lax.rsqrt(jnp.mean(xv * xv, axis=-1, keepdims=True) + EPS)
        o_ref[...] = ((xv * r * w_ref[...]) * (1.0 + sc_ref[...]) + sh_ref[...]).astype(BF16)

    return _call(body, (x, w, sc, sh), side, name=name, out_shape=[_sds((T, D), BF16)], grid=(T // tm,),
                 in_specs=[_row_spec(tm, D), _vec_spec(D), _vec_spec(D), _vec_spec(D)],
                 out_specs=[_row_spec(tm, D)], sem=("parallel",))


def _gate_rows(dxv, br_ref, g_ref, db_ref, acc_ref):
    db = g_ref[...] * dxv
    db_ref[...] = db.astype(BF16)
    acc_ref[3:4, :] += _colsum(dxv * br_ref[...].astype(F32))
    acc_ref[4:5, :] += _colsum(db)


def _norm_mod_bwd(x, dh, dres, w, sc, branch, g, name, side=None):
    T = x.shape[0]
    tm = _rows(T)
    gated = branch is not None

    def body(x_ref, dh_ref, dres_ref, w_ref, sc_ref, *rest):
        (br_ref, g_ref, dx_ref, db_ref, acc_ref) = rest if gated else (None, None, rest[0], None, rest[1])

        @pl.when(pl.program_id(0) == 0)
        def _():
            acc_ref[...] = jnp.zeros_like(acc_ref)
        xv, dh_v, wv = x_ref[...], dh_ref[...], w_ref[...]
        r = lax.rsqrt(jnp.mean(xv * xv, axis=-1, keepdims=True) + EPS)
        n = xv * r
        dnw = dh_v * (1.0 + sc_ref[...])
        dn = dnw * wv
        dxv = dres_ref[...] + r * (dn - n * jnp.mean(dn * n, axis=-1, keepdims=True))
        dx_ref[...] = dxv
        acc_ref[0:1, :] += _colsum(dh_v * (n * wv))
        acc_ref[1:2, :] += _colsum(dh_v)
        acc_ref[2:3, :] += _colsum(dnw * n)
        if gated:
            _gate_rows(dxv, br_ref, g_ref, db_ref, acc_ref)

    row = _row_spec(tm, D)
    args = (x, dh, dres, w, sc) + ((branch, g) if gated else ())
    return _call(body, args, side, name=name,
                 out_shape=[_sds((T, D))] + ([_sds((T, D), BF16)] if gated else []) + [_sds((8, D))], grid=(T // tm,),
                 in_specs=[row, row, row, _vec_spec(D), _vec_spec(D)] + ([row, _vec_spec(D)] if gated else []),
                 out_specs=[row] + ([row] if gated else []) + [_vec_spec(D, 8)], sem=("arbitrary",))


def _final_loss(x, wf, target, branch, g, name):
    T = x.shape[0]
    tm = _rows(T)

    def body(x_ref, w_ref, t_ref, br_ref, g_ref, dx_ref, db_ref, acc_ref):
        @pl.when(pl.program_id(0) == 0)
        def _():
            acc_ref[...] = jnp.zeros_like(acc_ref)
        xv, wv = x_ref[...], w_ref[...]
        r = lax.rsqrt(jnp.mean(xv * xv, axis=-1, keepdims=True) + EPS)
        n = xv * r
        err = n * wv - t_ref[...]
        dy = err * (1.0 / D)
        dn = dy * wv
        dxv = r * (dn - n * jnp.mean(dn * n, axis=-1, keepdims=True))
        dx_ref[...] = dxv
        acc_ref[0:1, :] += _colsum(dy * n)
        acc_ref[1:2, :] += jnp.broadcast_to(_allsum(err * err) * (0.5 / D), (1, D))
        _gate_rows(dxv, br_ref, g_ref, db_ref, acc_ref)

    row = _row_spec(tm, D)
    return _pcall(body, name=name, out_shape=[_sds((T, D)), _sds((T, D), BF16), _sds((8, D))], grid=(T // tm,),
                  in_specs=[row, _vec_spec(D), row, row, _vec_spec(D)],
                  out_specs=[row, row, _vec_spec(D, 8)], sem=("arbitrary",))(x, wf, target, branch, g)


def _mod_matmul(c_all, ada_w, name):
    n = ada_w.shape[2]

    def body(c_ref, w_ref, cond_ref, o_ref):
        cond = _silu(c_ref[...])
        cond_ref[...] = cond
        o_ref[0] = _nn(cond, w_ref[0])

    return _pcall(body, name=name, out_shape=[_sds((N_DEV, D)), _sds((DEPTH, N_DEV, n))], grid=(DEPTH,),
                  in_specs=[pl.BlockSpec((N_DEV, D), lambda l: (0, 0)), pl.BlockSpec((1, D, n), lambda l: (l, 0, 0))],
                  out_specs=[pl.BlockSpec((N_DEV, D), lambda l: (0, 0)), pl.BlockSpec((1, N_DEV, n), lambda l: (l, 0, 0))],
                  sem=("arbitrary",))(c_all, ada_w)


def _add_rows(a, b, name):
    def body(a_ref, b_ref, o_ref):
        o_ref[...] = a_ref[...] + b_ref[...]

    return _pcall(body, name=name, out_shape=_sds(a.shape))(a, b)


def _ada_w_grad(cond, dmod_cols, name):
    n = dmod_cols.shape[2]

    def body(c_ref, d_ref, o_ref):
        o_ref[0] = _tn(c_ref[...], d_ref[0])

    return _pcall(body, name=name, out_shape=_sds((DEPTH, D, n)), grid=(DEPTH,),
                  in_specs=[pl.BlockSpec((N_DEV, D), lambda l: (0, 0)), pl.BlockSpec((1, N_DEV, n), lambda l: (l, 0, 0))],
                  out_specs=pl.BlockSpec((1, D, n), lambda l: (l, 0, 0)), sem=("parallel",))(cond, dmod_cols)


def _conv_fwd(pm, conv_w, conv_b, name):
    T = pm.shape[0]
    tm = _rows(T)
    C = CONV_DIM

    def body(x_ref, prev_ref, w_ref, b_ref, o_ref):
        cur = x_ref[...].astype(F32)
        prev = jnp.where(pl.program_id(0) > 0, prev_ref[...].astype(F32)[8:16], 0.0)
        cur8 = cur[0:8]
        row8 = lax.broadcasted_iota(jnp.int32, (8, C), 0)
        full = w_ref[3:4, :] * cur
        head = w_ref[3:4, :] * cur8
        for k in range(1, SSM_CONV):
            wk = w_ref[3 - k:4 - k, :]
            full = full + wk * pltpu.roll(cur, k, 0)
            head = head + wk * jnp.where(row8 < k, pltpu.roll(prev, k, 0), pltpu.roll(cur8, k, 0))
        o_ref[...] = full + b_ref[...]
        o_ref[0:8, :] = head + b_ref[...]

    return _pcall(
        body, name=name, out_shape=_sds((T, C)), grid=(T // tm,),
        in_specs=[pl.BlockSpec((tm, C), lambda i: (i, 2)),
                  pl.BlockSpec((16, C), lambda i: (jnp.maximum(i * (tm // 16) - 1, 0), 2)),
                  _vec_spec(C, SSM_CONV), _vec_spec(C)],
        out_specs=_row_spec(tm, C), sem=("parallel",))(pm, pm, conv_w, conv_b)


def _conv_bwd(dc, pm, conv_w, name, side=None):
    T = dc.shape[0]
    tm = _rows(T)
    C = CONV_DIM
    nt = T // tm

    def body(dc_ref, nxt_ref, x_ref, w_ref, dx_ref, acc_ref):
        i = pl.program_id(0)

        @pl.when(i == 0)
        def _():
            acc_ref[...] = jnp.zeros_like(acc_ref)
        dcv = dc_ref[...]
        nxt = jnp.where(i < nt - 1, nxt_ref[...], 0.0)
        xc = x_ref[...].astype(F32)
        dc8t, x8t = dcv[tm - 8:tm], xc[tm - 8:tm]
        row8 = lax.broadcasted_iota(jnp.int32, (8, C), 0)
        full = w_ref[3:4, :] * dcv
        tail = w_ref[3:4, :] * dc8t
        acc_ref[3:4, :] += _colsum(dcv * xc)
        for k in range(1, SSM_CONV):
            wk = w_ref[3 - k:4 - k, :]
            up = pltpu.roll(dcv, tm - k, 0)
            up_tail = jnp.where(row8 + k >= 8, pltpu.roll(nxt, 8 - k, 0), pltpu.roll(dc8t, 8 - k, 0))
            full = full + wk * up
            tail = tail + wk * up_tail
            prod = up * xc
            acc_ref[3 - k:4 - k, :] += _colsum(prod) - _colsum(prod[tm - 8:tm]) + _colsum(up_tail * x8t)
        acc_ref[4:5, :] += _colsum(dcv)
        dx_ref[...] = jnp.concatenate([full[0:tm - 8], tail], axis=0).astype(BF16)

    return _call(
        body, (dc, dc, pm, conv_w), side, name=name, out_shape=[_sds((T, C), BF16), _sds((8, C))], grid=(nt,),
        in_specs=[_row_spec(tm, C),
                  pl.BlockSpec((8, C), lambda i: (jnp.minimum((i + 1) * (tm // 8), T // 8 - 1), 0)),
                  pl.BlockSpec((tm, C), lambda i: (i, 2)),
                  _vec_spec(C, SSM_CONV)],
        out_specs=[_row_spec(tm, C), _vec_spec(C, 8)], sem=("arbitrary",))


def _ssd_prologue(cpre, dtr, dtb, alog):
    L = CHUNK
    xc = _silu(cpre)
    pre = dtr + dtb
    dt = jnp.maximum(pre, 0.0) + jnp.log1p(jnp.exp(-jnp.abs(pre)))
    a = -jnp.exp(alog)
    la = dt * a
    row = lax.broadcasted_iota(jnp.int32, (L, L), 0)
    col = lax.broadcasted_iota(jnp.int32, (L, L), 1)
    causal = row >= col
    tri = causal.astype(F32)
    lc = _nn(tri, la, HIGHEST)
    return xc, pre, dt, a, causal, tri, lc, row, col


def _head_indicator():
    m = np.zeros((LANES, SSM_INNER), np.float32)
    for h in range(SSM_HEADS):
        m[h, h * SSM_HEAD_DIM:(h + 1) * SSM_HEAD_DIM] = 1.0
    return jnp.asarray(m, dtype=BF16)


def _split_dot(x, ind, dims):
    hi = x.astype(BF16)
    lo = (x - hi.astype(F32)).astype(BF16)
    return _dot(hi, ind, dims) + _dot(lo, ind, dims)


def _expand(x16, ind):
    return _split_dot(x16, ind, ((1,), (0,)))


def _headsum(x, ind, single_pass=False):
    if single_pass:
        return _dot(x.astype(BF16), ind, ((1,), (1,)))
    return _split_dot(x, ind, ((1,), (1,)))


def _ssd_fwd(cpre, dtr, pm, dtb, alog, dskip, normw, ind, name, side=None):
    T = cpre.shape[0]
    nc = T // CHUNK
    L, P, H, HPG, N = CHUNK, SSM_HEAD_DIM, SSM_HEADS, SSM_HEADS // SSM_GROUPS, SSM_STATE
    half = SSM_INNER // SSM_GROUPS

    def body(cp_ref, dtr_ref, z_ref, dtb_ref, alog_ref, dskip_ref, nw_ref, ind_ref, ya_ref, y_ref, sp_ref, st_ref):
        @pl.when(pl.program_id(0) == 0)
        def _():
            st_ref[...] = jnp.zeros_like(st_ref)
        xc, _, dt, _, causal, _, lc, _, _ = _ssd_prologue(cp_ref[...], dtr_ref[...], dtb_ref[...], alog_ref[...])
        lct = lc.T
        ind = ind_ref[...]
        llast = lc[L - 1:L, :]
        xs = xc[:, :SSM_INNER]
        xd = xs * _expand(dt, ind)
        ex = _expand(jnp.exp(lc), ind)
        xd_end = _bf(xd * _expand(jnp.exp(llast - lc), ind))
        cdx = _expand(jnp.broadcast_to(jnp.exp(llast), (8, LANES)), ind)[0:1]
        xdb = _bf(xd)
        sp_ref[0] = st_ref[...]
        for g in range(SSM_GROUPS):
            sl = slice(g * half, (g + 1) * half)
            bm = _bf(xc[:, SSM_INNER + g * N:SSM_INNER + (g + 1) * N])
            cm = _bf(xc[:, SSM_INNER + (SSM_GROUPS + g) * N:SSM_INNER + (SSM_GROUPS + g + 1) * N])
            cb = _nt(cm, bm)
            st = st_ref[g]
            y_ref[:, sl] = ex[:, sl] * _nn(cm, _bf(st)) + dskip_ref[:, sl] * xs[:, sl]
            st_ref[g] = st * cdx[:, sl] + _tn(bm, xd_end[:, sl])
            for j in range(HPG):
                h = g * HPG + j
                decay = jnp.where(causal, jnp.exp(jnp.where(causal, lc[:, h:h + 1] - lct[h:h + 1, :], 0.0)), 0.0)
                y_ref[:, h * P:(h + 1) * P] += _nn(_bf(cb * decay), xdb[:, h * P:(h + 1) * P])
        y2 = y_ref[...] * _silu(z_ref[...].astype(F32))
        for g in range(SSM_GROUPS):
            yg = y2[:, g * half:(g + 1) * half]
            r = lax.rsqrt(jnp.mean(yg * yg, axis=-1, keepdims=True) + EPS)
            ya_ref[:, g * half:(g + 1) * half] = (yg * r * nw_ref[:, g * half:(g + 1) * half]).astype(BF16)

    return _call(
        body, (cpre, dtr, pm, dtb, alog, dskip, normw, ind), side, name=name,
        out_shape=[_sds((T, SSM_INNER), BF16), _sds((T, SSM_INNER)), _sds((nc, SSM_GROUPS, N, half))], grid=(nc,),
        in_specs=[_row_spec(L, CONV_DIM), _row_spec(L, LANES), _row_spec(L, SSM_INNER, 0),
                  _vec_spec(LANES), _vec_spec(LANES), _vec_spec(SSM_INNER), _vec_spec(SSM_INNER), _vec_spec(SSM_INNER, LANES)],
        out_specs=[_row_spec(L, SSM_INNER), _row_spec(L, SSM_INNER),
                   pl.BlockSpec((1, SSM_GROUPS, N, half), lambda i: (i, 0, 0, 0))],
        scratch=[pltpu.VMEM((SSM_GROUPS, N, half), F32)], sem=("arbitrary",))


def _ssd_bwd(cpre, dtr, pm, ypre, sprev, dya, dtb, alog, dskip, normw, ind, name, side=None):
    T = cpre.shape[0]
    nc = T // CHUNK
    L, P, H, HPG, N = CHUNK, SSM_HEAD_DIM, SSM_HEADS, SSM_HEADS // SSM_GROUPS, SSM_STATE
    half = SSM_INNER // SSM_GROUPS

    def body(cp_ref, dtr_ref, z_ref, y_ref, sp_ref, dya_ref, dtb_ref, alog_ref, dskip_ref, nw_ref, ind_ref,
             dz_ref, dcp_ref, ddtr_ref, acc_ref, dnw_ref, ds_ref, dy_ref, dxd_ref, rr_ref, yoff_ref, dcd_ref):
        @pl.when(pl.program_id(0) == 0)
        def _():
            ds_ref[...] = jnp.zeros_like(ds_ref)
            acc_ref[...] = jnp.zeros_like(acc_ref)
            dnw_ref[...] = jnp.zeros_like(dnw_ref)
        cpre_v = cp_ref[...]
        xc, pre, dt, a, causal, tri, lc, row, col = _ssd_prologue(cpre_v, dtr_ref[...], dtb_ref[...], alog_ref[...])
        lct = lc.T
        zv, yv = z_ref[...].astype(F32), y_ref[...]
        sz, dsz = _silu_and_grad(zv)
        y2 = yv * sz
        dya_v = dya_ref[...]
        nwv = nw_ref[...]
        for g in range(SSM_GROUPS):
            sl = slice(g * half, (g + 1) * half)
            yg = y2[:, sl]
            r = lax.rsqrt(jnp.mean(yg * yg, axis=-1, keepdims=True) + EPS)
            nrm = yg * r
            dnw_ref[:, sl] += _colsum(dya_v[:, sl] * nrm)
            dn = dya_v[:, sl] * nwv[:, sl]
            dy2 = r * (dn - nrm * jnp.mean(dn * nrm, axis=-1, keepdims=True))
            dy_ref[:, sl] = dy2 * sz[:, sl]
            dz_ref[:, sl] = (dy2 * yv[:, sl] * dsz[:, sl]).astype(BF16)
        ind = ind_ref[...]
        llast = lc[L - 1:L, :]
        dte16 = jnp.exp(llast - lc)
        cd16 = jnp.exp(llast)
        xs = xc[:, :SSM_INNER]
        dtx = _expand(dt, ind)
        ex = _expand(jnp.exp(lc), ind)
        dtex = _expand(dte16, ind)
        cdx = _expand(jnp.broadcast_to(cd16, (8, LANES)), ind)[0:1]
        xd = xs * dtx
        xdb = _bf(xd)
        xd_end = _bf(xd * dtex)
        dyv = dy_ref[...]
        dy_off = _bf(ex * dyv)
        dyb = _bf(dyv)
        dskx = dskip_ref[...]
        lane_c = lax.broadcasted_iota(jnp.int32, (L, LANES), 1)
        lane1 = lax.broadcasted_iota(jnp.int32, (1, LANES), 1)
        sub16 = lax.broadcasted_iota(jnp.int32, (H, L), 0)
        dlc_c = jnp.zeros((L, LANES), F32)
        dlc_r = jnp.zeros((H, L), F32)
        for g in range(SSM_GROUPS):
            sl = slice(g * half, (g + 1) * half)
            b_lo = SSM_INNER + g * N
            c_lo = SSM_INNER + (SSM_GROUPS + g) * N
            bm, cm = _bf(xc[:, b_lo:b_lo + N]), _bf(xc[:, c_lo:c_lo + N])
            cb = _nt(cm, bm)
            st, dst = sp_ref[0, g], ds_ref[g]
            stb, dstb = _bf(st), _bf(dst)
            dcm = _nt(dy_off[:, sl], stb)
            ds_ref[g] = _tn(cm, dy_off[:, sl]) + dst * cdx[:, sl]
            rr_ref[:, sl] = _nn(bm, dstb)
            yoff_ref[:, sl] = ex[:, sl] * _nn(cm, stb)
            db = _nt(xd_end[:, sl], dstb)
            dcd_ref[:, sl] = _colsum(dst * st)
            dcb = jnp.zeros((L, L), F32)
            for j in range(HPG):
                h = g * HPG + j
                hs = slice(h * P, (h + 1) * P)
                decay = jnp.where(causal, jnp.exp(jnp.where(causal, lc[:, h:h + 1] - lct[h:h + 1, :], 0.0)), 0.0)
                m = cb * decay
                dxd_ref[:, hs] = _tn(_bf(m), dyb[:, hs])
                dm = _nt(dyb[:, hs], xdb[:, hs])
                dcb = dcb + dm * decay
                gm = dm * m
                dlc_c = dlc_c + jnp.where(lane_c == h, _rowsum(gm), 0.0)
                dlc_r = dlc_r + jnp.where(sub16 == h, _colsum(gm), 0.0)
            dcbb = _bf(dcb)
            dcp_ref[:, c_lo:c_lo + N] = dcm + _nn(dcbb, bm)
            dcp_ref[:, b_lo:b_lo + N] = db + _tn(dcbb, cm)
        dxd_diag, rr = dxd_ref[...], rr_ref[...]
        tt = _headsum(rr * xd, ind, single_pass=True) * dte16
        dlc_rt = jnp.concatenate([dlc_r, jnp.zeros((LANES - H, L), F32)], axis=0).T
        dlc = dlc_c - dlc_rt + _headsum(dyv * yoff_ref[...], ind, single_pass=True) - tt
        dcd = _headsum(jnp.broadcast_to(dcd_ref[...], (8, SSM_INNER)), ind)[0:1]
        dlc = dlc + jnp.where(row == L - 1, _colsum(tt) + dcd * cd16, 0.0)
        dla = _tn(tri, dlc, HIGHEST)
        dxd = dxd_diag + dtex * rr
        ddt = _headsum(dxd * xs, ind, single_pass=True) + dla * a
        ddtr = jnp.where(lane_c < H, ddt * _sigmoid(pre), 0.0)
        ddtr_ref[...] = ddtr
        acc_ref[0:1, :] += _colsum(ddtr)
        acc_ref[1:2, :] += jnp.where(lane1 < H, _colsum(dla * dt) * a, 0.0)
        acc_ref[2:3, :] += _headsum(jnp.broadcast_to(_colsum(dyv * xs), (8, SSM_INNER)), ind)[0:1]
        dcp_ref[:, 0:SSM_INNER] = dxd * dtx + dskx * dyv
        dcp_ref[...] = dcp_ref[...] * _dsilu(cpre_v)

    rev = lambda i: (nc - 1 - i, 0)
    rspec = lambda c: pl.BlockSpec((L, c), rev)
    return _call(
        body, (cpre, dtr, pm, ypre, sprev, dya, dtb, alog, dskip, normw, ind), side, name=name,
        out_shape=[_sds((T, SSM_INNER), BF16), _sds((T, CONV_DIM)), _sds((T, LANES)), _sds((8, LANES)), _sds((1, SSM_INNER))],
        grid=(nc,),
        in_specs=[rspec(CONV_DIM), rspec(LANES), rspec(SSM_INNER), rspec(SSM_INNER),
                  pl.BlockSpec((1, SSM_GROUPS, N, half), lambda i: (nc - 1 - i, 0, 0, 0)), rspec(SSM_INNER),
                  _vec_spec(LANES), _vec_spec(LANES), _vec_spec(SSM_INNER), _vec_spec(SSM_INNER), _vec_spec(SSM_INNER, LANES)],
        out_specs=[rspec(SSM_INNER), rspec(CONV_DIM), rspec(LANES), _vec_spec(LANES, 8), _vec_spec(SSM_INNER)],
        scratch=[pltpu.VMEM((SSM_GROUPS, N, half), F32), pltpu.VMEM((L, SSM_INNER), F32), pltpu.VMEM((L, SSM_INNER), F32),
                 pltpu.VMEM((L, SSM_INNER), F32), pltpu.VMEM((L, SSM_INNER), F32), pltpu.VMEM((1, SSM_INNER), F32)],
        sem=("arbitrary",))


def _gmlp_common(u, v, lnw, lnb, with_grads=False):
    (ug, dug), (vg, dvg) = (_gelu_and_grad(u), _gelu_and_grad(v)) if with_grads else ((_gelu(u), None), (_gelu(v), None))
    mu = jnp.mean(vg, axis=-1, keepdims=True)
    cen = vg - mu
    rstd = lax.rsqrt(jnp.mean(cen * cen, axis=-1, keepdims=True) + EPS)
    vhat = cen * rstd
    out = (ug, rstd, vhat, vhat * lnw + lnb)
    return out + (dug, dvg) if with_grads else out


def _causal_mask():
    row = lax.broadcasted_iota(jnp.int32, (CHUNK, CHUNK), 0)
    col = lax.broadcasted_iota(jnp.int32, (CHUNK, CHUNK), 1)
    return row >= col


def _gmlp_rows(T):
    return 2 * CHUNK if T % (2 * CHUNK) == 0 else CHUNK


def _gmlp_fwd(pm, lnw, lnb, ws, bs_exp, name, side=None):
    T = pm.shape[0]
    L, G = CHUNK, GMLP_GROUPS
    R = _gmlp_rows(T)

    def body(u_ref, v_ref, lnw_ref, lnb_ref, ws_ref, bs_ref, o_ref):
        ug, _, _, vn = _gmlp_common(u_ref[...].astype(F32), v_ref[...].astype(F32), lnw_ref[...], lnb_ref[...])
        causal = _causal_mask()
        for g in range(G):
            sl = slice(g * L, (g + 1) * L)
            wm = _bf(jnp.where(causal, ws_ref[g], 0.0))
            for c in range(R // L):
                rows = slice(c * L, (c + 1) * L)
                sv = _nn(wm, _bf(vn[rows, sl])) + bs_ref[:, sl]
                o_ref[rows, sl] = (ug[rows, sl] * sv).astype(BF16)

    return _call(
        body, (pm, pm, lnw, lnb, ws, bs_exp), side, name=name, out_shape=[_sds((T, GMLP_INNER), BF16)], grid=(T // R,),
        in_specs=[_row_spec(R, GMLP_INNER, 1), _row_spec(R, GMLP_INNER, 2), _vec_spec(GMLP_INNER), _vec_spec(GMLP_INNER),
                  pl.BlockSpec((G, L, L), lambda i: (0, 0, 0)), _vec_spec(GMLP_INNER, L)],
        out_specs=[_row_spec(R, GMLP_INNER)], sem=("parallel",))


def _gmlp_bwd(pm, dyb, lnw, lnb, ws, bs_exp, name, side=None):
    T = pm.shape[0]
    L, G = CHUNK, GMLP_GROUPS
    R = _gmlp_rows(T)

    def body(u_ref, v_ref, dy_ref, lnw_ref, lnb_ref, ws_ref, bs_ref, du_ref, dv_ref, dws_ref, dbs_ref, acc_ref, dvn_ref):
        @pl.when(pl.program_id(0) == 0)
        def _():
            dws_ref[...] = jnp.zeros_like(dws_ref)
            dbs_ref[...] = jnp.zeros_like(dbs_ref)
            acc_ref[...] = jnp.zeros_like(acc_ref)
        uv, vv, dyv, lnwv = u_ref[...].astype(F32), v_ref[...].astype(F32), dy_ref[...], lnw_ref[...]
        ug, rstd, vhat, vn, dug, dvg_act = _gmlp_common(uv, vv, lnwv, lnb_ref[...], with_grads=True)
        causal = _causal_mask()
        lane = lax.broadcasted_iota(jnp.int32, (L, LANES), 1)
        dbs = jnp.zeros((L, LANES), F32)
        for g in range(G):
            sl = slice(g * L, (g + 1) * L)
            wm = _bf(jnp.where(causal, ws_ref[g], 0.0))
            for c in range(R // L):
                rows = slice(c * L, (c + 1) * L)
                vng = _bf(vn[rows, sl])
                sv = _nn(wm, vng) + bs_ref[:, sl]
                du_ref[rows, sl] = (dyv[rows, sl] * sv * dug[rows, sl]).astype(BF16)
                dsv = dyv[rows, sl] * ug[rows, sl]
                dsvb = _bf(dsv)
                dws_ref[g] += jnp.where(causal, _nt(dsvb, vng), 0.0)
                dbs = dbs + jnp.where(lane == g, _rowsum(dsv), 0.0)
                dvn_ref[rows, sl] = _tn(wm, dsvb)
        dbs_ref[...] += dbs
        dvn = dvn_ref[...]
        acc_ref[0:1, :] += _colsum(dvn * vhat)
        acc_ref[1:2, :] += _colsum(dvn)
        dvh = dvn * lnwv
        dvg = rstd * (dvh - jnp.mean(dvh, axis=-1, keepdims=True) - vhat * jnp.mean(dvh * vhat, axis=-1, keepdims=True))
        dv_ref[...] = (dvg * dvg_act).astype(BF16)

    return _call(
        body, (pm, pm, dyb, lnw, lnb, ws, bs_exp), side, name=name,
        out_shape=[_sds((T, GMLP_INNER), BF16), _sds((T, GMLP_INNER), BF16), _sds((G, L, L)), _sds((L, LANES)), _sds((8, GMLP_INNER))],
        grid=(T // R,),
        in_specs=[_row_spec(R, GMLP_INNER, 1), _row_spec(R, GMLP_INNER, 2), _row_spec(R, GMLP_INNER),
                  _vec_spec(GMLP_INNER), _vec_spec(GMLP_INNER), pl.BlockSpec((G, L, L), lambda i: (0, 0, 0)),
                  _vec_spec(GMLP_INNER, L)],
        out_specs=[_row_spec(R, GMLP_INNER), _row_spec(R, GMLP_INNER), pl.BlockSpec((G, L, L), lambda i: (0, 0, 0)),
                   _vec_spec(LANES, L), _vec_spec(GMLP_INNER, 8)],
        scratch=[pltpu.VMEM((R, GMLP_INNER), F32)], sem=("arbitrary",))


def _rel_buckets():
    qi = np.arange(CHUNK)[:, None]
    sj = np.arange(2 * CHUNK)[None, :]
    dist = np.maximum(qi + CHUNK - sj, 0)
    max_exact = REL_BUCKETS // 2
    log_ratio = (np.log(np.maximum(dist, 1).astype(np.float32) / np.float32(max_exact))
                 / np.float32(math.log(REL_MAX_DIST / max_exact))).astype(np.float32)
    large = max_exact + (log_ratio * np.float32(REL_BUCKETS - max_exact)).astype(np.int32)
    return np.where(dist < max_exact, dist, np.minimum(large, REL_BUCKETS - 1))


def _bucket_onehot_t():
    bucket = _rel_buckets().reshape(-1)
    return jnp.asarray((np.arange(REL_BUCKETS)[:, None] == bucket[None, :]).astype(np.float32))


def _bias_from_table(table_t, onehot_t, window, name):
    def body(t_ref, o_ref, w_ref, out_ref):
        out_ref[...] = jnp.where(w_ref[...] > 0.5, _nn(t_ref[...], o_ref[...], HIGHEST), NEG_INF)

    return _pcall(body, name=name, out_shape=_sds((ATTN_HEADS, onehot_t.shape[1])))(table_t, onehot_t, window)


def _table_from_dbias(dbias, onehot_t, name):
    def body(d_ref, o_ref, out_ref):
        out_ref[...] = _nt(o_ref[...], d_ref[...], HIGHEST)

    return _pcall(body, name=name, out_shape=_sds((REL_BUCKETS, ATTN_HEADS)))(dbias, onehot_t)


def _softmax_sink(logits, sink):
    mx = jnp.maximum(jnp.max(logits, axis=-1, keepdims=True), sink)
    e = jnp.exp(logits - mx)
    es = jnp.exp(sink - mx)
    inv = 1.0 / (_rowsum(e) + es)
    return e * inv, es * inv


def _first_block_penalty(n):
    sj = lax.broadcasted_iota(jnp.int32, (1, 2 * CHUNK), 1)
    return jnp.where((sj < CHUNK) & (n == 0), NEG_INF, 0.0)


def _window_mask_flat():
    qi = np.arange(CHUNK)[:, None]
    sj = np.arange(2 * CHUNK)[None, :]
    rel = qi + CHUNK - sj
    return jnp.asarray(((rel >= 0) & (rel < CHUNK)).astype(np.float32).reshape(1, -1))


def _stack_heads(ref, first, count, width, scale=None):
    x = jnp.concatenate([_bf(ref[:, (first + j) * width:(first + j + 1) * width]) for j in range(count)], axis=0)
    return x if scale is None else x * jnp.asarray(scale, x.dtype)


def _attn_fwd(qkv, bias, sinks, name):
    T = qkv.shape[0]
    nb = T // CHUNK
    L, DH, HPK = CHUNK, ATTN_DH, ATTN_HEADS // ATTN_KV
    scale = DH ** -0.5
    kcol, vcol = ATTN_HEADS * DH // LANES, ATTN_HEADS * DH // LANES + 1

    def body(q_ref, k_ref, v_ref, kp_ref, vp_ref, bias_ref, sink_ref, o_ref, lg_ref, p_ref):
        n = pl.program_id(0)
        pen = _first_block_penalty(n)
        kband = _bf(jnp.concatenate([kp_ref[...], k_ref[...]], axis=0))
        vband = _bf(jnp.concatenate([vp_ref[...], v_ref[...]], axis=0))
        for kv in range(ATTN_KV):
            lg_ref[...] = _nt(_stack_heads(q_ref, kv * HPK, HPK, DH, scale), kband[:, kv * DH:(kv + 1) * DH])
            for j in range(HPK):
                h = kv * HPK + j
                p, _ = _softmax_sink(lg_ref[j * L:(j + 1) * L, :] + bias_ref[h] + pen, sink_ref[h])
                p_ref[j * L:(j + 1) * L, :] = _bf(p)
            og = _nn(p_ref[...], vband[:, kv * DH:(kv + 1) * DH])
            for j in range(HPK):
                h = kv * HPK + j
                o_ref[:, h * DH:(h + 1) * DH] = og[j * L:(j + 1) * L].astype(BF16)

    prev = lambda i: jnp.maximum(i - 1, 0)
    return _pcall(
        body, name=name, out_shape=_sds((T, ATTN_HEADS * DH), BF16), grid=(nb,),
        in_specs=[_row_spec(L, ATTN_HEADS * DH, 0), _row_spec(L, LANES, kcol), _row_spec(L, LANES, vcol),
                  pl.BlockSpec((L, LANES), lambda i: (prev(i), kcol)), pl.BlockSpec((L, LANES), lambda i: (prev(i), vcol)),
                  pl.BlockSpec((ATTN_HEADS, L, 2 * L), lambda i: (0, 0, 0)),
                  pl.BlockSpec(memory_space=pltpu.SMEM)],
        out_specs=_row_spec(L, ATTN_HEADS * DH),
        scratch=[pltpu.VMEM((HPK * L, 2 * L), F32), pltpu.VMEM((HPK * L, 2 * L), BF16)],
        sem=("parallel",))(qkv, qkv, qkv, qkv, qkv, bias, sinks)


def _attn_bwd(qkv, datt, bias, sinks, name, side=None):
    T = qkv.shape[0]
    nb = T // CHUNK
    L, DH, HPK = CHUNK, ATTN_DH, ATTN_HEADS // ATTN_KV
    scale = DH ** -0.5
    kcol, vcol = ATTN_HEADS * DH // LANES, ATTN_HEADS * DH // LANES + 1

    def body(q_ref, k_ref, v_ref, kp_ref, vp_ref, do_ref, bias_ref, sink_ref,
             dq_ref, dk_ref, dv_ref, bsum_ref, dbias_ref, dsink_ref, pend_k, pend_v, band_k, band_v, lg_ref, dp_ref, p_ref, dl_ref):
        n = pl.program_id(0)

        @pl.when(n == 0)
        def _():
            dbias_ref[...] = jnp.zeros_like(dbias_ref)
            dsink_ref[...] = jnp.zeros_like(dsink_ref)
            bsum_ref[...] = jnp.zeros_like(bsum_ref)

        def emit_kv(dk, dv):
            dk_ref[...] = dk.astype(BF16)
            dv_ref[...] = dv.astype(BF16)
            bsum_ref[:, ATTN_HEADS * DH:ATTN_HEADS * DH + LANES] += _colsum(dk)
            bsum_ref[:, ATTN_HEADS * DH + LANES:] += _colsum(dv)

        @pl.when(n < nb)
        def _():
            pen = _first_block_penalty(n)
            kband = _bf(jnp.concatenate([kp_ref[...], k_ref[...]], axis=0))
            vband = _bf(jnp.concatenate([vp_ref[...], v_ref[...]], axis=0))
            lane1 = lax.broadcasted_iota(jnp.int32, (1, LANES), 1)
            dsink = jnp.zeros((1, LANES), F32)
            for kv in range(ATTN_KV):
                kb, vb = kband[:, kv * DH:(kv + 1) * DH], vband[:, kv * DH:(kv + 1) * DH]
                qg = _stack_heads(q_ref, kv * HPK, HPK, DH, scale)
                dog = _stack_heads(do_ref, kv * HPK, HPK, DH)
                lg_ref[...] = _nt(qg, kb)
                dp_ref[...] = _nt(dog, vb)
                for j in range(HPK):
                    h = kv * HPK + j
                    rows = slice(j * L, (j + 1) * L)
                    p, ps = _softmax_sink(lg_ref[rows, :] + bias_ref[h] + pen, sink_ref[h])
                    dp = dp_ref[rows, :]
                    delta = _rowsum(p * dp)
                    dl = p * (dp - delta)
                    dbias_ref[h] += dl
                    p_ref[rows, :] = _bf(p)
                    dl_ref[rows, :] = _bf(dl)
                    dsink = dsink + jnp.where(lane1 == h, -_colsum(ps * delta), 0.0)
                band_v[:, kv * DH:(kv + 1) * DH] = _tn(p_ref[...], dog)
                dqg = _nn(dl_ref[...], kb) * scale
                band_k[:, kv * DH:(kv + 1) * DH] = _tn(dl_ref[...], qg)
                for j in range(HPK):
                    h = kv * HPK + j
                    dq_ref[:, h * DH:(h + 1) * DH] = dqg[j * L:(j + 1) * L].astype(BF16)
                    bsum_ref[:, h * DH:(h + 1) * DH] += _colsum(dqg[j * L:(j + 1) * L])
            dsink_ref[...] += dsink

            @pl.when(n > 0)
            def _():
                emit_kv(pend_k[...] + band_k[0:L, :], pend_v[...] + band_v[0:L, :])
            pend_k[...] = band_k[L:2 * L, :]
            pend_v[...] = band_v[L:2 * L, :]

        @pl.when(n == nb)
        def _():
            emit_kv(pend_k[...], pend_v[...])

    cur = lambda i: jnp.minimum(i, nb - 1)
    prev = lambda i: jnp.maximum(jnp.minimum(i, nb - 1) - 1, 0)
    lag = lambda i: jnp.maximum(i - 1, 0)
    return _call(
        body, (qkv, qkv, qkv, qkv, qkv, datt, bias, sinks), side, name=name,
        out_shape=[_sds((T, ATTN_HEADS * DH), BF16), _sds((T, LANES), BF16), _sds((T, LANES), BF16), _sds((1, QKV_DIM)),
                   _sds((ATTN_HEADS, L, 2 * L)), _sds((1, LANES))],
        grid=(nb + 1,),
        in_specs=[pl.BlockSpec((L, ATTN_HEADS * DH), lambda i: (cur(i), 0)),
                  pl.BlockSpec((L, LANES), lambda i: (cur(i), kcol)), pl.BlockSpec((L, LANES), lambda i: (cur(i), vcol)),
                  pl.BlockSpec((L, LANES), lambda i: (prev(i), kcol)), pl.BlockSpec((L, LANES), lambda i: (prev(i), vcol)),
                  pl.BlockSpec((L, ATTN_HEADS * DH), lambda i: (cur(i), 0)),
                  pl.BlockSpec((ATTN_HEADS, L, 2 * L), lambda i: (0, 0, 0)),
                  pl.BlockSpec(memory_space=pltpu.SMEM)],
        out_specs=[pl.BlockSpec((L, ATTN_HEADS * DH), lambda i: (cur(i), 0)),
                   pl.BlockSpec((L, LANES), lambda i: (lag(i), 0)), pl.BlockSpec((L, LANES), lambda i: (lag(i), 0)),
                   _vec_spec(QKV_DIM), pl.BlockSpec((ATTN_HEADS, L, 2 * L), lambda i: (0, 0, 0)), _vec_spec(LANES)],
        scratch=[pltpu.VMEM((L, LANES), F32), pltpu.VMEM((L, LANES), F32),
                 pltpu.VMEM((2 * L, LANES), F32), pltpu.VMEM((2 * L, LANES), F32),
                 pltpu.VMEM((HPK * L, 2 * L), F32), pltpu.VMEM((HPK * L, 2 * L), F32),
                 pltpu.VMEM((HPK * L, 2 * L), BF16), pltpu.VMEM((HPK * L, 2 * L), BF16)],
        sem=("arbitrary",))


def _pad_rows(a, mult):
    pad = (-a.shape[-2]) % mult
    if pad == 0:
        return a
    cfg = [(0, 0)] * (a.ndim - 2) + [(0, pad), (0, 0)]
    return jnp.pad(a, cfg)


class _Pack:
    def __init__(self, width, mult, total_mult):
        self.width, self.mult, self.total_mult = width, mult, total_mult
        self.entries = []
        self.rows = 0

    def add(self, key, shape):
        n = int(np.prod(shape))
        rows = -(-n // self.width)
        self.entries.append((key, self.rows, rows, tuple(shape)))
        self.rows += -(-rows // self.mult) * self.mult

    @property
    def total(self):
        return -(-self.rows // self.total_mult) * self.total_mult

    def pack(self, pieces, dtype, lead=()):
        parts = []
        for key, _, rows, shape in self.entries:
            a = pieces[key].astype(dtype).reshape(lead + (-1,))
            n = int(np.prod(shape))
            a = jnp.pad(a, [(0, 0)] * len(lead) + [(0, rows * self.width - n)])
            a = a.reshape(lead + (rows, self.width))
            parts.append(_pad_rows(a, self.mult))
        out = jnp.concatenate(parts, axis=len(lead))
        return _pad_rows(out, self.total_mult)

    def unpack(self, packed, lead=()):
        out = {}
        for key, off, rows, shape in self.entries:
            a = lax.slice_in_dim(packed, off, off + rows, axis=len(lead))
            a = a.reshape(lead + (-1,))
            n = int(np.prod(shape))
            out[key] = lax.slice_in_dim(a, 0, n, axis=len(lead)).reshape(lead + shape)
        return out


def _ffn_fwd(x, h, mod, wg_t, wu_t, wd, tag, next_norm=None, gather=None):
    side = None if gather is None else (_GatherOps, gather)
    gate, up, act, *gathered = _mm_swiglu(h, wg_t, wu_t, f"ffn_gateup_{tag}", side=side)
    x_out, ffn_out, *h_next = _mm_resid([(act, wd)], x, mod[5:6], name=f"ffn_down_{tag}", norm=next_norm)
    return (x_out, dict(h=h, gate=gate, up=up, act=act, out=ffn_out), *h_next, *gathered)


def _ffn_bwd(dx_out, dffn, x_in, saved, mod, norm_w, wg_t, wu_t, wd, below, tag, exchange=None):
    side = None if exchange is None else (_ChipsOps, exchange)
    dgate, dup, *from_chips = _mm_swiglu_bwd(dffn, wd, saved["gate"], saved["up"], f"ffn_act_bwd_{tag}", side=side)
    d_wd = _mm_tn(saved["act"], dffn, name=f"ffn_dwd_{tag}")
    d_wg_t = _mm_tn(dgate, saved["h"], name=f"ffn_dwg_{tag}")
    d_wu_t = _mm_tn(dup, saved["h"], name=f"ffn_dwu_{tag}")
    dh = _mm([(dgate, wg_t), (dup, wu_t)], "nn", name=f"ffn_dh_{tag}")
    dx, d_below, acc = _norm_mod_bwd(x_in, dh, dx_out, norm_w, mod[4:5], below[0], below[1], f"ffn_norm_bwd_{tag}")
    return (dx, d_below, dict(d_wg=d_wg_t, d_wu=d_wu_t, d_wd=d_wd, acc=acc), *from_chips)


_BIG = [
    ("out_w", "out_w_even", 0, "row"), ("qkv_w", "qkv_w", 0, "col"), ("o_w", "o_w", 0, "row"),
    ("gate0", "ffn_gate_w", 0, "col"), ("up0", "ffn_up_w", 0, "col"), ("down0", "ffn_down_w", 0, "row"),
    ("gate1", "ffn_gate_w", 1, "col"), ("up1", "ffn_up_w", 1, "col"), ("down1", "ffn_down_w", 1, "row"),
    ("in_w", "in_w_even", 0, "col"),
]


def _to_wire(a, kind):
    return a.T if kind == "col" else a


_GATHER_GROUPS = [["in_w"], ["out_w"], ["gate0", "up0", "down0"], ["qkv_w", "o_w"], ["gate1", "up1", "down1"]]
_GRAD_GROUPS = [["gate1", "up1", "down1"], ["qkv_w", "o_w"], ["out_w", "gate0", "up0", "down0"], ["in_w"]]

_REPLICATED = ["ada_b", "norm_mix_w", "norm_ffn_w", "conv_b", "dt_bias", "a_log", "d_skip", "ssm_norm_w", "gmlp_ln_w",
               "gmlp_ln_b", "gmlp_ws", "gmlp_bs", "sinks", "rel_table", "final_norm_w"]
_TINY_SHARDED = ["conv_w", "qkv_b", "o_b"]

_WEIGHTS = ['ada_w', 'ada_b', 'norm_mix_w', 'norm_ffn_w', 'in_w_even', 'conv_w', 'conv_b', 'dt_bias', 'a_log', 'd_skip',
            'ssm_norm_w', 'gmlp_ln_w', 'gmlp_ln_b', 'gmlp_ws', 'gmlp_bs', 'out_w_even', 'qkv_w', 'qkv_b', 'o_w', 'o_b',
            'sinks', 'rel_table', 'ffn_gate_w', 'ffn_up_w', 'ffn_down_w', 'final_norm_w']


def _step(x, c, loss_target, W, M, V):
    T = x.shape[1]
    x0 = x[0]
    target = loss_target[0]
    me = 4 * lax.axis_index("x") + 2 * lax.axis_index("y") + lax.axis_index("c")

    w_wire_local = {key: _to_wire(W[name][layer].astype(BF16), kind) for key, name, layer, kind in _BIG}

    def wire_pack(keys, mult):
        gp = _Pack(D, 1, mult)
        for key in keys:
            gp.add(key, w_wire_local[key].shape)
        return gp

    gather_packs = [(gp, gp.pack(w_wire_local, BF16)) for gp in (wire_pack(keys, 16) for keys in _GATHER_GROUPS)]
    grad_packs = [wire_pack(keys, 64) for keys in _GRAD_GROUPS]
    full = {}

    def gathered_weights(group, gathered):
        shards = gather_packs[group][0].unpack(gathered, lead=(N_DEV,))
        full.update({key: a.reshape(-1, D) for key, a in shards.items()})


    small_in = _Pack(D, 8, 8)
    small_in.add("c", (1, D))
    small_in.add("conv_w", W["conv_w"][0].shape)
    small_in.add("qkv_b", W["qkv_b"][0].shape)
    small_in.add("o_b", W["o_b"][0].shape)
    sm = small_in.unpack(_all_gather(small_in.pack(
        dict(c=c, conv_w=W["conv_w"][0], qkv_b=W["qkv_b"][0], o_b=W["o_b"][0]), F32), "gather_small"), lead=(N_DEV,))
    c_all = sm["c"].reshape(N_DEV, D)
    conv_w_full = jnp.transpose(sm["conv_w"], (1, 0, 2)).reshape(SSM_CONV, CONV_DIM)
    qkv_b_full = sm["qkv_b"].reshape(1, QKV_DIM)
    o_b_full = sm["o_b"].reshape(1, D)

    ncol = W["ada_w"].shape[2]
    cond, mod_cols = _mod_matmul(c_all, W["ada_w"], "mod_matmul")
    mod_g = _all_gather(mod_cols.reshape(DEPTH * N_DEV, ncol), "gather_mod").reshape(N_DEV, DEPTH, N_DEV, ncol)
    mod_me = lax.dynamic_index_in_dim(mod_g, me, axis=2, keepdims=False)
    mod_me = jnp.transpose(mod_me, (1, 0, 2)).reshape(DEPTH, 6, D)
    mod_me = jnp.pad(mod_me, ((0, 0), (0, 2), (0, 0))).reshape(DEPTH * 8, D)
    ada_b_rows = jnp.pad(W["ada_b"].reshape(DEPTH, 6, D), ((0, 0), (0, 2), (0, 0))).reshape(DEPTH * 8, D)
    mod_all = _add_rows(mod_me, ada_b_rows, "mod_bias").reshape(DEPTH, 8, D)
    mod0, mod1 = mod_all[0], mod_all[1]
    h0, gathered_in = _norm_mod(x0, W["norm_mix_w"][0:1], mod0[1:2], mod0[0:1], "mix_norm_0",
                                side=(_GatherOps, gather_packs[0][1]))
    gathered_weights(0, gathered_in)

    in_t = full["in_w"]
    o1, o2, o3, o4 = SSM_INNER, SSM_INNER + CONV_DIM, SSM_INNER + CONV_DIM + SSM_HEADS, SSM_INNER + CONV_DIM + SSM_HEADS + GMLP_INNER
    w_z, w_xbc, w_dt, w_u, w_v = in_t[:o1], in_t[o1:o2], in_t[o2:o3], in_t[o3:o4], in_t[o4:]
    w_main = jnp.concatenate([w_z, w_u, w_v, w_xbc], axis=0)
    w_dtp = jnp.pad(w_dt, ((0, LANES - SSM_HEADS), (0, 0)))

    pad16 = lambda a: jnp.pad(a.reshape(1, SSM_HEADS), ((0, 0), (0, LANES - SSM_HEADS)))
    dtb, alog = pad16(W["dt_bias"][0]), pad16(W["a_log"][0])
    dskip = jnp.repeat(W["d_skip"][0], SSM_HEAD_DIM).reshape(1, SSM_INNER)
    ssm_nw = W["ssm_norm_w"]
    lnw, lnb = W["gmlp_ln_w"], W["gmlp_ln_b"]
    ws = W["gmlp_ws"][0]
    bs_exp = jnp.repeat(W["gmlp_bs"][0].T, CHUNK, axis=1)
    conv_b = W["conv_b"]
    nmw, nfw = W["norm_mix_w"], W["norm_ffn_w"]
    onehot_t = _bucket_onehot_t()
    head_ind = _head_indicator()
    bias = _bias_from_table(W["rel_table"].T, onehot_t, _window_mask_flat(), "rel_bias").reshape(ATTN_HEADS, CHUNK, 2 * CHUNK)
    sinks = W["sinks"][0]

    pm, gathered_a = _mm([(h0, w_main)], "nt", name="in_proj", tn_pref=1536, out_dtype=BF16,
                         side=(_GatherOps, gather_packs[1][1]))
    gathered_weights(1, gathered_a)
    out_w = full["out_w"]
    dtr = _mm([(h0, w_dtp)], "nt", name="in_proj_dt")
    cpre = _conv_fwd(pm, conv_w_full, conv_b, "conv_fwd")
    ya, ypre, sprev, gathered_b = _ssd_fwd(cpre, dtr, pm, dtb, alog, dskip, ssm_nw, head_ind, "ssd_fwd",
                                           side=(_GatherOps, gather_packs[2][1]))
    gathered_weights(2, gathered_b)
    yb, gathered_c = _gmlp_fwd(pm, lnw, lnb, ws, bs_exp, "gmlp_fwd", side=(_GatherOps, gather_packs[3][1]))
    gathered_weights(3, gathered_c)
    x1, mix0, hf0 = _mm_resid([(ya, out_w[:SSM_INNER]), (yb, out_w[SSM_INNER:])], x0, mod0[2:3], name="out_proj",
                              norm=(nfw[0:1], mod0[4:5], mod0[3:4]))
    x2, ffn0, h1, gathered_d = _ffn_fwd(x1, hf0, mod0, full["gate0"], full["up0"], full["down0"], "0",
                                        next_norm=(nmw[1:2], mod1[1:2], mod1[0:1]), gather=gather_packs[4][1])
    gathered_weights(4, gathered_d)
    qkv_t, o_w = full["qkv_w"], full["o_w"]
    w_q, w_k, w_v_att = qkv_t[:D], qkv_t[D:D + LANES], qkv_t[D + LANES:]

    qkv = _mm([(h1, qkv_t)], "nt", name="qkv_proj", bias=qkv_b_full, tn_pref=1280, out_dtype=BF16)
    att = _attn_fwd(qkv, bias, sinks, "attn_fwd")
    x3, mix1, hf1 = _mm_resid([(att, o_w)], x2, mod1[2:3], name="o_proj", bias=o_b_full,
                              norm=(nfw[1:2], mod1[4:5], mod1[3:4]))
    x4, ffn1 = _ffn_fwd(x3, hf1, mod1, full["gate1"], full["up1"], full["down1"], "1")

    dx4, dffn1, acc_f = _final_loss(x4, W["final_norm_w"].reshape(1, D), target, ffn1["out"], mod1[5:6], "final_loss")
    dx3, dmix1, gf1 = _ffn_bwd(dx4, dffn1, x3, ffn1, mod1, nfw[1:2], full["gate1"], full["up1"], full["down1"],
                               (mix1, mod1[2:3]), "1")
    g_wire = dict(gate1=gf1["d_wg"], up1=gf1["d_wu"], down1=gf1["d_wd"])

    def packed_partials(group):
        return grad_packs[group].pack({key: g_wire[key].reshape(N_DEV, -1, D) for key in _GRAD_GROUPS[group]},
                                      F32, lead=(N_DEV,))

    from_chips = {}
    partials_ffn1 = packed_partials(0)
    datt, theirs_ffn1 = _mm([(dmix1, o_w)], "nt", name="o_proj_dx", out_dtype=BF16, side=(_SiblingOps, partials_ffn1))
    pair_ffn1 = _pair_sum(partials_ffn1, theirs_ffn1, "grads_pair_sum_ffn1")
    d_o_w = _mm_tn(att, dmix1, name="o_proj_dw")
    dq, dk, dv, d_qkv_b, dbias, dsinks, from_chips[0] = _attn_bwd(qkv, datt, bias, sinks, "attn_bwd",
                                                                 side=(_ChipsOps, pair_ffn1))
    d_table = _table_from_dbias(dbias.reshape(ATTN_HEADS, -1), onehot_t, "rel_table_grad")
    d_qkv_t = jnp.concatenate([_mm_tn(dq, h1, name="qkv_dw_q"), _mm_tn(dk, h1, name="qkv_dw_k"), _mm_tn(dv, h1, name="qkv_dw_v")], axis=0)
    g_wire.update(qkv_w=d_qkv_t, o_w=d_o_w)
    partials_l1 = packed_partials(1)
    dh1, theirs_l1 = _mm([(dq, w_q), (dk, w_k), (dv, w_v_att)], "nn", name="qkv_proj_dx", side=(_SiblingOps, partials_l1))
    pair_l1 = _pair_sum(partials_l1, theirs_l1, "grads_pair_sum_l1")
    dx2, dffn0, acc_n1 = _norm_mod_bwd(x2, dh1, dx3, nmw[1:2], mod1[1:2], ffn0["out"], mod0[5:6], "mix_norm_bwd_1")

    dx1, dmix0, gf0, from_chips[1] = _ffn_bwd(dx2, dffn0, x1, ffn0, mod0, nfw[0:1], full["gate0"], full["up0"], full["down0"],
                                              (mix0, mod0[2:3]), "0", exchange=pair_l1)

    dya = _mm([(dmix0, out_w[:SSM_INNER])], "nt", name="out_proj_dx_a")
    dyb = _mm([(dmix0, out_w[SSM_INNER:])], "nt", name="out_proj_dx_b")
    d_out_w = jnp.concatenate([_mm_tn(ya, dmix0, name="out_proj_dw_a"), _mm_tn(yb, dmix0, name="out_proj_dw_b")], axis=0)
    g_wire.update(gate0=gf0["d_wg"], up0=gf0["d_wu"], down0=gf0["d_wd"], out_w=d_out_w)
    partials_ffn0 = packed_partials(2)
    du, dvg, d_ws, d_bs, acc_ln, theirs_ffn0 = _gmlp_bwd(pm, dyb, lnw, lnb, ws, bs_exp, "gmlp_bwd",
                                                         side=(_SiblingOps, partials_ffn0))
    pair_ffn0 = _pair_sum(partials_ffn0, theirs_ffn0, "grads_pair_sum_mix")
    dz, dcpre, ddtr, acc_ssd, d_ssm_nw, from_chips[2] = _ssd_bwd(
        cpre, dtr, pm, ypre, sprev, dya, dtb, alog, dskip, ssm_nw, head_ind, "ssd_bwd", side=(_ChipsOps, pair_ffn0))
    acc_f0, acc_f1 = gf0["acc"], gf1["acc"]
    row = lambda i: slice(i, i + 1)
    views = dict(ada_b=(DEPTH * 6, D), norm_mix_w=(DEPTH, D), norm_ffn_w=(DEPTH, D), conv_b=(1, CONV_DIM),
                 dt_bias=(1, SSM_HEADS), a_log=(1, SSM_HEADS), d_skip=(1, SSM_HEADS), ssm_norm_w=(1, D), gmlp_ln_w=(1, D),
                 gmlp_ln_b=(1, D), gmlp_ws=(GMLP_GROUPS * CHUNK, CHUNK), gmlp_bs=(GMLP_GROUPS, CHUNK),
                 sinks=(1, ATTN_HEADS), rel_table=(REL_BUCKETS, ATTN_HEADS), final_norm_w=(1, D),
                 conv_w=(SSM_CONV, CONV_DIM), qkv_b=(1, QKV_DIM), o_b=(1, D), loss=(1, D))
    late_keys = ["ada_b", "norm_mix_w", "conv_b", "dt_bias", "a_log", "d_skip", "ssm_norm_w", "conv_w"]
    canvas_early, canvas_late = _Canvas(), _Canvas()
    for key, (r, cdim) in views.items():
        (canvas_late if key in late_keys else canvas_early).add(key, r, cdim, blocks=GMLP_GROUPS if key == "gmlp_ws" else 1)
    early_sources = [
        (acc_f0, [("norm_ffn_w", row(2), 0)]),
        (acc_f1, [("norm_ffn_w", row(2), 1), ("o_b", row(4), 0)]),
        (acc_f, [("final_norm_w", row(0), 0), ("loss", row(1), 0)]),
        (acc_ln, [("gmlp_ln_w", row(0), 0), ("gmlp_ln_b", row(1), 0)]),
        (d_ws.reshape(GMLP_GROUPS * CHUNK, CHUNK), [("gmlp_ws", slice(0, GMLP_GROUPS * CHUNK), 0)]),
        (d_bs.T, [("gmlp_bs", slice(0, GMLP_GROUPS), 0)]),
        (dsinks, [("sinks", row(0), 0)]),
        (d_table, [("rel_table", slice(0, REL_BUCKETS), 0)]),
        (d_qkv_b, [("qkv_b", row(0), 0)]),
    ]
    dxbc, acc_conv, parts_early = _conv_bwd(dcpre, pm, conv_w_full, "conv_bwd",
                                            side=(_GatherOps, _canvas_fill(canvas_early, early_sources, "small_grads_early")))
    d_in_t = jnp.concatenate([
        _mm_tn(dz, h0, name="in_dw_z"), _mm_tn(dxbc, h0, name="in_dw_xbc"),
        _mm_tn(ddtr, h0, name="in_dw_dt")[:SSM_HEADS], _mm_tn(du, h0, name="in_dw_u"), _mm_tn(dvg, h0, name="in_dw_v")], axis=0)
    g_wire.update(in_w=d_in_t)
    partials_mix = packed_partials(3)
    theirs_mix = _comm_call(_SiblingOps, partials_mix, "exchange_grads_sibling")
    pair_mix = _pair_sum(partials_mix, theirs_mix, "grads_pair_sum")
    dh0, from_chips[3] = _mm([(dz, w_z), (dxbc, w_xbc), (ddtr, w_dtp), (du, w_u), (dvg, w_v)], "nn", name="in_proj_dx",
                             side=(_ChipsOps, pair_mix))
    grad_x, acc_n0 = _norm_mod_bwd(x0, dh0, dx1, nmw[0:1], mod0[1:2], None, None, "mix_norm_bwd_0")

    g_mine = {}
    for group in range(len(_GRAD_GROUPS)):
        g_mine.update(grad_packs[group].unpack(_sum_parts(from_chips[group], f"grads_chip_sum_{group}")))
    res_big = [{}, {}, {}, {}]
    for key, name, layer, kind in _BIG:
        g_nat = _to_wire(g_mine[key], kind)
        outs = _adamw(g_nat[None], W[name][layer], M[name][layer], V[name][layer], f"adamw_{key}")
        for res, out in zip(res_big, outs):
            res[key] = out

    late_sources = [
        (acc_n0, [("ada_b", row(1), 0), ("ada_b", row(0), 1), ("norm_mix_w", row(2), 0)]),
        (acc_f0, [("ada_b", row(3), 2), ("ada_b", row(1), 3), ("ada_b", row(0), 4)]),
        (acc_n1, [("ada_b", row(3), 5), ("ada_b", row(1), 6), ("ada_b", row(0), 7), ("norm_mix_w", row(2), 1)]),
        (acc_f1, [("ada_b", row(3), 8), ("ada_b", row(1), 9), ("ada_b", row(0), 10)]),
        (acc_f, [("ada_b", row(3), 11)]),
        (acc_conv, [("conv_w", slice(0, SSM_CONV), 0), ("conv_b", row(4), 0)]),
        (acc_ssd, [("dt_bias", row(0), 0), ("a_log", row(1), 0), ("d_skip", row(2), 0)]),
        (d_ssm_nw, [("ssm_norm_w", row(0), 0)]),
    ]
    parts_small = _all_gather(_canvas_fill(canvas_late, late_sources, "small_grads_late"), "gather_small_grads")
    view = lambda a, key: a.reshape(views[key])
    params = lambda keys: [(n, view(W[n], n), view(M[n], n), view(V[n], n)) for n in _REPLICATED if (n in late_keys) == keys]
    small_out = _adamw_canvas(canvas_early, parts_early, params(False), ["qkv_b", "o_b", "loss"], "adamw_small_early")
    small_out.update(_adamw_canvas(canvas_late, parts_small, params(True), ["conv_w"], "adamw_small_late"))
    loss = small_out["loss"][0][0, 0]
    res_small = [{name: small_out[name][k].reshape(W[name].shape) for name in _REPLICATED} for k in range(4)]

    n_cw, n_qb, n_ob = W["conv_w"].shape[2], W["qkv_b"].shape[1], W["o_b"].shape[1]
    g_tiny = dict(conv_w=lax.dynamic_slice_in_dim(small_out["conv_w"][0], me * n_cw, n_cw, axis=1)[None],
                  qkv_b=lax.dynamic_slice_in_dim(small_out["qkv_b"][0], me * n_qb, n_qb, axis=1),
                  o_b=lax.dynamic_slice_in_dim(small_out["o_b"][0], me * n_ob, n_ob, axis=1))
    tiny = _Pack(D, 8, 8)
    for name in _TINY_SHARDED:
        tiny.add(name, W[name].shape)
    pkt = lambda S: tiny.pack({name: S[name] for name in _TINY_SHARDED}, F32)
    res_tiny = [tiny.unpack(r) for r in _adamw(pkt(g_tiny)[None], pkt(W), pkt(M), pkt(V), "adamw_tiny")]

    dmod_all = parts_small[:, canvas_late.offset["ada_b"]:canvas_late.offset["ada_b"] + DEPTH * 6].reshape(N_DEV, DEPTH, 6 * D)
    dmod_cols = jnp.transpose(lax.dynamic_slice_in_dim(dmod_all, me * ncol, ncol, axis=2), (1, 0, 2))
    g_ada_w = _ada_w_grad(cond, dmod_cols, "ada_w_grad")
    flat = lambda a: a.reshape(DEPTH * D, ncol)
    res_ada = [r.reshape(DEPTH, D, ncol) for r in _adamw(flat(g_ada_w)[None], flat(W["ada_w"]), flat(M["ada_w"]), flat(V["ada_w"]), "adamw_ada_w")]

    def result(kind_idx, name):
        if name == "ada_w":
            return res_ada[kind_idx]
        if name in _REPLICATED:
            return res_small[kind_idx][name]
        if name in _TINY_SHARDED:
            return res_tiny[kind_idx][name]
        pieces = [res_big[kind_idx][key] for key, nm, layer, kind in _BIG if nm == name]
        return jnp.stack(pieces)

    outs = [loss, grad_x[None]]
    for kind_idx in range(4):
        outs += [result(kind_idx, name) for name in _WEIGHTS]
    return tuple(outs)


def kernel(x, c, ada_w, ada_b, norm_mix_w, norm_ffn_w, in_w_even, conv_w, conv_b, dt_bias, a_log, d_skip, ssm_norm_w, gmlp_ln_w, gmlp_ln_b, gmlp_ws, gmlp_bs, out_w_even, qkv_w, qkv_b, o_w, o_b, sinks, rel_table, ffn_gate_w, ffn_up_w, ffn_down_w, final_norm_w, loss_target, m_ada_w, m_ada_b, m_norm_mix_w, m_norm_ffn_w, m_in_w_even, m_conv_w, m_conv_b, m_dt_bias, m_a_log, m_d_skip, m_ssm_norm_w, m_gmlp_ln_w, m_gmlp_ln_b, m_gmlp_ws, m_gmlp_bs, m_out_w_even, m_qkv_w, m_qkv_b, m_o_w, m_o_b, m_sinks, m_rel_table, m_ffn_gate_w, m_ffn_up_w, m_ffn_down_w, m_final_norm_w, v_ada_w, v_ada_b, v_norm_mix_w, v_norm_ffn_w, v_in_w_even, v_conv_w, v_conv_b, v_dt_bias, v_a_log, v_d_skip, v_ssm_norm_w, v_gmlp_ln_w, v_gmlp_ln_b, v_gmlp_ws, v_gmlp_bs, v_out_w_even, v_qkv_w, v_qkv_b, v_o_w, v_o_b, v_sinks, v_rel_table, v_ffn_gate_w, v_ffn_up_w, v_ffn_down_w, v_final_norm_w):
    args = locals()
    W = {n: args[n] for n in _WEIGHTS}
    M = {n: args["m_" + n] for n in _WEIGHTS}
    V = {n: args["v_" + n] for n in _WEIGHTS}
    return _step(x, c, loss_target, W, M, V)
```

```python
import functools
import math

import numpy as np
import jax
import jax.numpy as jnp
from jax import lax
from jax.experimental import pallas as pl
from jax.experimental.pallas import tpu as pltpu

F32 = jnp.float32
BF16 = jnp.bfloat16
HIGHEST = lax.Precision.HIGHEST
MESH = pl.DeviceIdType.MESH

N_DEV = 8
D = 1024
DEPTH = 2
SSM_HEADS = 16
SSM_HEAD_DIM = 64
SSM_INNER = 1024
SSM_GROUPS = 2
SSM_STATE = 128
SSM_CONV = 4
CHUNK = 128
CONV_DIM = SSM_INNER + 2 * SSM_GROUPS * SSM_STATE
GMLP_GROUPS = 8
GMLP_INNER = 1024
IN_EVEN = 4624
ATTN_HEADS = 16
ATTN_KV = 2
ATTN_DH = 64
QKV_DIM = 1280
REL_BUCKETS = 32
REL_MAX_DIST = 128
FFN = 2816
EPS = 1e-6
NEG_INF = -1e30
LANES = 128

ADAM_LR = 0.001
ADAM_B1 = 0.9
ADAM_B2 = 0.999
ADAM_EPS = 1e-08
ADAM_WD = 0.01
ADAM_STEP = 10

VMEM_LIMIT_BYTES = 56 * 1024 * 1024
ROW_TILE = 512


def _pcall(body, *, name, out_shape, grid=(), in_specs=None, out_specs=None, scratch=(), sem=None):
    params = dict(vmem_limit_bytes=VMEM_LIMIT_BYTES)
    if sem is not None:
        params["dimension_semantics"] = sem
    specs = {} if in_specs is None else dict(in_specs=in_specs, out_specs=out_specs)
    return pl.pallas_call(
        body, name=name, out_shape=out_shape, grid=grid, **specs,
        scratch_shapes=list(scratch), compiler_params=pltpu.CompilerParams(**params))


def _call(body, args, side=None, *, name, out_shape, grid, in_specs, out_specs, scratch=(), sem=None):
    if side is None:
        return _pcall(body, name=name, out_shape=out_shape, grid=grid, in_specs=in_specs, out_specs=out_specs,
                      scratch=scratch, sem=sem)(*args)
    ops_cls, x = side
    n_in, n_out, n_scr = len(in_specs), len(out_shape), len(scratch)
    steps = int(np.prod(grid))
    hbm = pl.BlockSpec(memory_space=pl.ANY)

    def wrapped(*refs):
        ins, x_ref = refs[:n_in], refs[n_in]
        outs, r_ref = refs[n_in + 1:n_in + 1 + n_out], refs[n_in + 1 + n_out]
        scr, sems = refs[n_in + 2 + n_out:n_in + 2 + n_out + n_scr], refs[n_in + 2 + n_out + n_scr:]
        ops = ops_cls(x_ref, r_ref, *sems)
        step = pl.program_id(0)
        for axis in range(1, len(grid)):
            step = step * grid[axis] + pl.program_id(axis)
        pl.when(step == 0)(ops.start)
        body(*ins, *outs, *scr)
        pl.when(step == (3 * steps) // 4)(ops.forward)
        pl.when(step == steps - 1)(ops.finish)

    return _pcall(
        wrapped, name=name, out_shape=list(out_shape) + [ops_cls.result(x)], grid=grid,
        in_specs=list(in_specs) + [hbm], out_specs=list(out_specs) + [hbm],
        scratch=list(scratch) + ops_cls.scratch(), sem=("arbitrary",) * len(grid))(*args, x)


def _tile(n, pref):
    if n <= pref:
        return n
    best = None
    for t in range(LANES, pref + 1, LANES):
        if n % t == 0:
            best = t
    assert best is not None, (n, pref)
    return best


def _rows(T):
    return min(ROW_TILE, T)


def _sds(shape, dtype=F32):
    return jax.ShapeDtypeStruct(shape, dtype)


def _row_spec(tm, c, col=0):
    return pl.BlockSpec((tm, c), lambda i, col=col: (i, col))


def _vec_spec(c, r=1):
    return pl.BlockSpec((r, c), lambda i: (0, 0))


def _sigmoid(x):
    return jax.nn.sigmoid(x)


def _silu(x):
    return x * _sigmoid(x)


def _dsilu(x):
    s = _sigmoid(x)
    return s * (1.0 + x * (1.0 - s))


def _silu_and_grad(x):
    s = _sigmoid(x)
    return x * s, s * (1.0 + x * (1.0 - s))


def _gelu(x):
    return 0.5 * x * (1.0 + lax.erf(x * 0.7071067811865476))


def _gelu_and_grad(x):
    phi = 0.5 * (1.0 + lax.erf(x * 0.7071067811865476))
    return x * phi, phi + x * jnp.exp(-0.5 * x * x) * 0.3989422804014327


def _dot(a, b, dims, precision=None):
    return lax.dot_general(a, b, (dims, ((), ())), precision=precision, preferred_element_type=F32)


def _nn(a, b, precision=None):
    return _dot(a, b, ((1,), (0,)), precision)


def _nt(a, b, precision=None):
    return _dot(a, b, ((1,), (1,)), precision)


def _tn(a, b, precision=None):
    return _dot(a, b, ((0,), (0,)), precision)


def _bf(x):
    return x.astype(BF16)


def _colsum(x):
    return jnp.sum(x, axis=0, keepdims=True)


def _rowsum(x):
    return jnp.sum(x, axis=1, keepdims=True)


def _allsum(x):
    return _colsum(_rowsum(x))


def _comm_call(ops_cls, x, name, from_vmem=False):
    def body(x_ref, out_ref, *sems):
        ops = ops_cls(x_ref, out_ref, *sems)
        ops.start()
        ops.forward()
        ops.finish()

    return pl.pallas_call(
        body, name=name, out_shape=ops_cls.result(x),
        in_specs=[pl.BlockSpec(memory_space=pltpu.VMEM if from_vmem else pl.ANY)],
        out_specs=pl.BlockSpec(memory_space=pl.ANY), scratch_shapes=ops_cls.scratch(),
    )(x)


def _all_gather(x, name):
    return _comm_call(_GatherOps, x, name, from_vmem=True)


class _GatherOps:
    def __init__(self, x_ref, out_ref, send_sems, recv_sems, local_sem):
        self.x_ref, self.out_ref = x_ref, out_ref
        self.send_sems, self.recv_sems, self.local_sem = send_sems, recv_sems, local_sem
        mx, my, mc = lax.axis_index("x"), lax.axis_index("y"), lax.axis_index("c")
        self.mc = mc
        self.me, self.sibling = (mx, my, mc), (mx, my, 1 - mc)
        self.chips = [(1 - mx, my), (mx, 1 - my), (1 - mx, 1 - my)]

    @staticmethod
    def result(x):
        return _sds((N_DEV,) + x.shape, x.dtype)

    @staticmethod
    def scratch():
        return [pltpu.SemaphoreType.DMA((7,)), pltpu.SemaphoreType.DMA((7,)), pltpu.SemaphoreType.DMA(())]

    def _slot(self, px, py, pc):
        return self.out_ref.at[4 * px + 2 * py + pc]

    def _copy(self, k, block, to, own=False):
        return pltpu.make_async_remote_copy(
            src_ref=self.x_ref if own else self._slot(*block), dst_ref=self._slot(*block),
            send_sem=self.send_sems.at[k], recv_sem=self.recv_sems.at[k], device_id=to, device_id_type=MESH)

    def _mine(self):
        return pltpu.make_async_copy(self.x_ref, self._slot(*self.me), self.local_sem)

    def _first(self):
        return [self._copy(0, self.me, self.sibling, own=True)] + [
            self._copy(1 + j, self.me, (*chip, self.mc), own=True) for j, chip in enumerate(self.chips)]

    def _passed(self):
        return [self._copy(4 + j, (*chip, self.mc), self.sibling) for j, chip in enumerate(self.chips)]

    def start(self):
        self._mine().start()
        for cp in self._first():
            cp.start()

    def forward(self):
        passed = self._passed()
        for j, chip in enumerate(self.chips):
            self._copy(1 + j, (*chip, self.mc), self.me).wait_recv()
            passed[j].start()

    def finish(self):
        self._copy(0, self.sibling, self.me).wait_recv()
        for j, chip in enumerate(self.chips):
            self._copy(4 + j, (*chip, 1 - self.mc), self.me).wait_recv()
        for cp in self._first() + self._passed():
            cp.wait_send()
        self._mine().wait()


N_CHIP = 4


class _SiblingOps:
    def __init__(self, p_ref, theirs_ref, send_sems, recv_sems):
        self.p_ref, self.theirs_ref, self.send_sems, self.recv_sems = p_ref, theirs_ref, send_sems, recv_sems

    @staticmethod
    def result(p):
        return _sds((N_CHIP,) + p.shape[1:], p.dtype)

    @staticmethod
    def scratch():
        return [pltpu.SemaphoreType.DMA((N_CHIP,))] * 2

    def _copies(self):
        mx, my, mc = lax.axis_index("x"), lax.axis_index("y"), lax.axis_index("c")
        return [pltpu.make_async_remote_copy(
            src_ref=self.p_ref.at[2 * chip + 1 - mc], dst_ref=self.theirs_ref.at[chip],
            send_sem=self.send_sems.at[chip], recv_sem=self.recv_sems.at[chip],
            device_id=(mx, my, 1 - mc), device_id_type=MESH) for chip in range(N_CHIP)]

    def start(self):
        for cp in self._copies():
            cp.start()

    def forward(self):
        pass

    def finish(self):
        for cp in self._copies():
            cp.wait()


class _ChipsOps:
    def __init__(self, q_ref, out_ref, send_sems, recv_sems, local_sem):
        self.q_ref, self.out_ref = q_ref, out_ref
        self.send_sems, self.recv_sems, self.local_sem = send_sems, recv_sems, local_sem

    @staticmethod
    def result(q):
        return _sds(q.shape, q.dtype)

    @staticmethod
    def scratch():
        return [pltpu.SemaphoreType.DMA((N_CHIP - 1,)), pltpu.SemaphoreType.DMA((N_CHIP - 1,)), pltpu.SemaphoreType.DMA(())]

    def _copies(self):
        mx, my, mc = lax.axis_index("x"), lax.axis_index("y"), lax.axis_index("c")
        me = 2 * mx + my
        local = pltpu.make_async_copy(self.q_ref.at[me], self.out_ref.at[me], self.local_sem)
        remote = []
        for r in range(1, N_CHIP):
            px = 1 - mx if r & 2 else mx
            py = 1 - my if r & 1 else my
            remote.append(pltpu.make_async_remote_copy(
                src_ref=self.q_ref.at[2 * px + py], dst_ref=self.out_ref.at[me],
                send_sem=self.send_sems.at[r - 1], recv_sem=self.recv_sems.at[r - 1],
                device_id=(px, py, mc), device_id_type=MESH))
        return local, remote

    def start(self):
        local, remote = self._copies()
        local.start()
        for cp in remote:
            cp.start()

    def forward(self):
        pass

    def finish(self):
        local, remote = self._copies()
        for cp in remote:
            cp.wait()
        local.wait()


def _pair_sum(p, theirs, name):
    n, R, C = theirs.shape
    tr = _tile_rows(R, 256)

    def body(p_ref, t_ref, o_ref):
        mc = lax.axis_index("c")
        o_ref[0] = (p_ref[0, mc] + t_ref[0]).astype(BF16)

    blk = pl.BlockSpec((1, tr, C), lambda s, i: (s, i, 0))
    return _pcall(body, name=name, out_shape=_sds((n, R, C), BF16), grid=(n, R // tr),
                  in_specs=[pl.BlockSpec((1, 2, tr, C), lambda s, i: (s, 0, i, 0)), blk],
                  out_specs=blk, sem=("parallel", "parallel"))(p.reshape(n, 2, R, C), theirs)


def _sum_parts(parts, name):
    P, R, C = parts.shape
    tr = _tile_rows(R, 256)

    def body(p_ref, o_ref):
        g = p_ref[0].astype(F32)
        for k in range(1, P):
            g = g + p_ref[k].astype(F32)
        o_ref[...] = g

    return _pcall(body, name=name, out_shape=_sds((R, C)), grid=(R // tr,),
                  in_specs=[pl.BlockSpec((P, tr, C), lambda i: (0, i, 0))],
                  out_specs=pl.BlockSpec((tr, C), lambda i: (i, 0)), sem=("parallel",))(parts)


def _adamw(parts, w, m, v, name):
    P, R, C = parts.shape
    tr = R if R <= 256 else _tile_rows(R, 256)

    def body(p_ref, w_ref, m_ref, v_ref, g_ref, d_ref, nm_ref, nv_ref):
        g = p_ref[0]
        for k in range(1, P):
            g = g + p_ref[k]
        nm = ADAM_B1 * m_ref[...] + (1.0 - ADAM_B1) * g
        nv = ADAM_B2 * v_ref[...] + (1.0 - ADAM_B2) * (g * g)
        m_hat = nm / (1.0 - ADAM_B1 ** ADAM_STEP)
        v_hat = nv / (1.0 - ADAM_B2 ** ADAM_STEP)
        g_ref[...] = g
        d_ref[...] = -ADAM_LR * (m_hat / (jnp.sqrt(v_hat) + ADAM_EPS) + ADAM_WD * w_ref[...])
        nm_ref[...] = nm
        nv_ref[...] = nv

    blk = pl.BlockSpec((tr, C), lambda i: (i, 0))
    return _pcall(
        body, name=name, out_shape=[_sds((R, C))] * 4, grid=(R // tr,),
        in_specs=[pl.BlockSpec((P, tr, C), lambda i: (0, i, 0)), blk, blk, blk],
        out_specs=[blk] * 4, sem=("parallel",))(parts, w, m, v)


def _adam_update(g, w, m, v):
    nm = ADAM_B1 * m + (1.0 - ADAM_B1) * g
    nv = ADAM_B2 * v + (1.0 - ADAM_B2) * (g * g)
    m_hat = nm / (1.0 - ADAM_B1 ** ADAM_STEP)
    v_hat = nv / (1.0 - ADAM_B2 ** ADAM_STEP)
    return -ADAM_LR * (m_hat / (jnp.sqrt(v_hat) + ADAM_EPS) + ADAM_WD * w), nm, nv


class _Canvas:
    def __init__(self):
        self.offset, self.views, self.rows = {}, {}, 0

    def add(self, key, r, c, blocks=1):
        need = r // blocks if blocks > 1 else r * (-(-c // D))
        if need >= 8:
            self.rows = -(-self.rows // 8) * 8
        self.offset[key], self.views[key] = self.rows, (r, c, blocks)
        self.rows += need

    @property
    def total(self):
        return -(-self.rows // 8) * 8

    def cells(self, key):
        (r, c, blocks), off = self.views[key], self.offset[key]
        if blocks > 1:
            n = r // blocks
            return [(off, n, b * c, c, slice(b * n, (b + 1) * n), slice(0, c)) for b in range(blocks)]
        if c <= D:
            return [(off, r, 0, c, slice(0, r), slice(0, c))]
        chunks = -(-c // D)
        return [(off + i * chunks + j, 1, 0, min(D, c - j * D), slice(i, i + 1), slice(j * D, min(c, (j + 1) * D)))
                for i in range(r) for j in range(chunks)]


def _canvas_fill(canvas, sources, name):
    arrays = [a for a, _ in sources]

    def body(*refs):
        out_ref = refs[-1]
        out_ref[...] = jnp.zeros_like(out_ref)
        for ref, (_, items) in zip(refs[:-1], sources):
            for key, src_rows, view_row in items:
                n = src_rows.stop - src_rows.start
                for row, count, lane, width, vrows, vcols in canvas.cells(key):
                    lo, hi = max(vrows.start, view_row), min(vrows.stop, view_row + n)
                    if lo < hi:
                        src = slice(src_rows.start + lo - view_row, src_rows.start + hi - view_row)
                        out_ref[row + lo - vrows.start:row + hi - vrows.start, lane:lane + width] = ref[src, vcols]

    return _pcall(body, name=name, out_shape=_sds((canvas.total, D)))(*arrays)


def _adamw_canvas(canvas, parts, params, sum_only, name):
    n = len(params)

    def body(p_ref, *refs):
        sum_ref = refs[-1]
        g_all = p_ref[0]
        for k in range(1, N_DEV):
            g_all = g_all + p_ref[k]
        sum_ref[...] = g_all
        for i, (key, _, _, _) in enumerate(params):
            w_ref, m_ref, v_ref = refs[3 * i:3 * i + 3]
            outs = refs[3 * n + 4 * i:3 * n + 4 * i + 4]
            for row, count, lane, width, vrows, vcols in canvas.cells(key):
                g = sum_ref[row:row + count, lane:lane + width]
                delta, nm, nv = _adam_update(g, w_ref[vrows, vcols], m_ref[vrows, vcols], v_ref[vrows, vcols])
                for ref, val in zip(outs, (g, delta, nm, nv)):
                    ref[vrows, vcols] = val
        for i, key in enumerate(sum_only):
            for row, count, lane, width, vrows, vcols in canvas.cells(key):
                refs[7 * n + i][vrows, vcols] = sum_ref[row:row + count, lane:lane + width]

    args = [a for _, w, m, v in params for a in (w, m, v)]
    out_shape = [_sds(w.shape) for _, w, _, _ in params for _ in range(4)] + [_sds(canvas.views[k][:2]) for k in sum_only]
    res = _pcall(body, name=name, out_shape=out_shape, scratch=[pltpu.VMEM(parts.shape[1:], F32)])(parts, *args)
    out = {key: res[4 * i:4 * i + 4] for i, (key, _, _, _) in enumerate(params)}
    out.update({key: [res[4 * n + i]] for i, key in enumerate(sum_only)})
    return out


def _tile_rows(n, pref):
    best = None
    for t in range(8, pref + 1, 8):
        if n % t == 0:
            best = t
    assert best is not None, (n, pref)
    return best


def _mm(pairs, mode, *, name, out_dtype=F32, bias=None, tn_pref=1024, side=None):
    M = pairs[0][0].shape[0]
    N = pairs[0][1].shape[1] if mode == "nn" else pairs[0][1].shape[0]
    tm, tn = _rows(M), _tile(N, tn_pref)
    n_pairs = len(pairs)
    has_bias = bias is not None

    def body(*refs):
        acc = _pairs_dot(refs[:2 * n_pairs], mode)
        if has_bias:
            acc = acc + refs[2 * n_pairs][...]
        refs[-1][...] = acc.astype(refs[-1].dtype)

    in_specs, args = _pair_specs(pairs, mode, tm, tn)
    if has_bias:
        in_specs.append(pl.BlockSpec((1, tn), lambda j, i: (0, j)))
        args.append(bias)
    res = _call(body, args, side, name=name, out_shape=[_sds((M, N), out_dtype)], grid=(N // tn, M // tm),
                in_specs=in_specs, out_specs=[pl.BlockSpec((tm, tn), lambda j, i: (i, j))], sem=("parallel", "parallel"))
    return res[0] if side is None else (res[0], res[1])


def _pairs_dot(ab, mode):
    acc = None
    for p in range(len(ab) // 2):
        a, b = _bf(ab[2 * p][...]), _bf(ab[2 * p + 1][...])
        d = _nn(a, b) if mode == "nn" else _nt(a, b)
        acc = d if acc is None else acc + d
    return acc


def _pair_specs(pairs, mode, tm, tn):
    in_specs, args = [], []
    for a, b in pairs:
        K = a.shape[1]
        in_specs.append(pl.BlockSpec((tm, K), lambda j, i: (i, 0)))
        if mode == "nn":
            in_specs.append(pl.BlockSpec((K, tn), lambda j, i: (0, j)))
        else:
            in_specs.append(pl.BlockSpec((tn, K), lambda j, i: (j, 0)))
        args += [a, b]
    return in_specs, args


def _mm_resid(pairs, resid, gvec, *, name, bias=None, norm=None):
    M, N = resid.shape
    tm = _rows(M)
    n_pairs = len(pairs)
    has_bias, has_norm = bias is not None, norm is not None

    def body(*refs):
        acc = _pairs_dot(refs[:2 * n_pairs], "nn")
        pos = 2 * n_pairs
        if has_bias:
            acc = acc + refs[pos][...]
            pos += 1
        xv = refs[pos][...] + refs[pos + 1][...] * acc
        outs = refs[pos + 2 + 3 * has_norm:]
        outs[0][...] = xv
        outs[1][...] = acc.astype(BF16)
        if has_norm:
            w_ref, sc_ref, sh_ref = refs[pos + 2:pos + 5]
            r = lax.rsqrt(jnp.mean(xv * xv, axis=-1, keepdims=True) + EPS)
            outs[2][...] = ((xv * r * w_ref[...]) * (1.0 + sc_ref[...]) + sh_ref[...]).astype(BF16)

    in_specs, args = _pair_specs(pairs, "nn", tm, N)
    vec = pl.BlockSpec((1, N), lambda j, i: (0, 0))
    row = pl.BlockSpec((tm, N), lambda j, i: (i, 0))
    if has_bias:
        in_specs.append(vec)
        args.append(bias)
    in_specs += [row, vec] + [vec] * (3 * has_norm)
    args += [resid, gvec] + (list(norm) if has_norm else [])
    return _pcall(body, name=name, out_shape=[_sds((M, N)), _sds((M, N), BF16)] + [_sds((M, N), BF16)] * has_norm,
                  grid=(1, M // tm), in_specs=in_specs, out_specs=[row] * (2 + has_norm),
                  sem=("parallel", "parallel"))(*args)


def _mm_tn(a, b, *, name, tm_pref=1408, tn_pref=1536):
    K, M = a.shape
    N = b.shape[1]
    tm, tn = _tile(M, tm_pref), _tile(N, tn_pref)
    tk = K if K <= 2 * ROW_TILE else 2 * ROW_TILE

    def body(a_ref, b_ref, o_ref):
        @pl.when(pl.program_id(2) == 0)
        def _():
            o_ref[...] = jnp.zeros_like(o_ref)
        o_ref[...] += _tn(_bf(a_ref[...]), _bf(b_ref[...]))

    return _pcall(
        body, name=name, out_shape=_sds((M, N)), grid=(M // tm, N // tn, K // tk),
        in_specs=[pl.BlockSpec((tk, tm), lambda i, j, k: (k, i)), pl.BlockSpec((tk, tn), lambda i, j, k: (k, j))],
        out_specs=pl.BlockSpec((tm, tn), lambda i, j, k: (i, j)),
        sem=("parallel", "parallel", "arbitrary"))(a, b)


def _mm_swiglu(h, wg_t, wu_t, name, side=None):
    M, K = h.shape
    N = wg_t.shape[0]
    tm, tn = _rows(M), _tile(N, 1408)

    def body(h_ref, wg_ref, wu_ref, gate_ref, up_ref, act_ref):
        hv = _bf(h_ref[...])
        gate = _nt(hv, wg_ref[...])
        up = _nt(hv, wu_ref[...])
        gate_ref[...] = gate.astype(BF16)
        up_ref[...] = up.astype(BF16)
        act_ref[...] = (_silu(gate) * up).astype(BF16)

    w_spec = pl.BlockSpec((tn, K), lambda j, i: (j, 0))
    o_spec = pl.BlockSpec((tm, tn), lambda j, i: (i, j))
    return _call(body, (h, wg_t, wu_t), side, name=name,
                 out_shape=[_sds((M, N), BF16)] * 3, grid=(N // tn, M // tm),
                 in_specs=[pl.BlockSpec((tm, K), lambda j, i: (i, 0)), w_spec, w_spec], out_specs=[o_spec] * 3,
                 sem=("parallel", "parallel"))


def _mm_swiglu_bwd(dout, wd, gate, up, name, side=None):
    M, K = dout.shape
    N = wd.shape[0]
    tm, tn = _rows(M), _tile(N, 1408)

    def body(d_ref, wd_ref, gate_ref, up_ref, dg_ref, du_ref):
        dact = _nt(_bf(d_ref[...]), wd_ref[...])
        act, dact_dg = _silu_and_grad(gate_ref[...].astype(F32))
        dg_ref[...] = (dact * up_ref[...].astype(F32) * dact_dg).astype(BF16)
        du_ref[...] = (dact * act).astype(BF16)

    t_spec = pl.BlockSpec((tm, tn), lambda j, i: (i, j))
    return _call(
        body, (dout, wd, gate, up), side, name=name, out_shape=[_sds((M, N), BF16)] * 2, grid=(N // tn, M // tm),
        in_specs=[pl.BlockSpec((tm, K), lambda j, i: (i, 0)), pl.BlockSpec((tn, K), lambda j, i: (j, 0)), t_spec, t_spec],
        out_specs=[t_spec] * 2, sem=("parallel", "parallel"))


def _norm_mod(x, w, sc, sh, name, side=None):
    T = x.shape[0]
    tm = _rows(T)

    def body(x_ref, w_ref, sc_ref, sh_ref, o_ref):
        xv = x_ref[...]
        r = lax.rsqrt(jnp.mean(xv * xv, axis=-1, keepdims=True) + EPS)
        o_ref[...] = ((xv * r * w_ref[...]) * (1.0 + sc_ref[...]) + sh_ref[...]).astype(BF16)

    return _call(body, (x, w, sc, sh), side, name=name, out_shape=[_sds((T, D), BF16)], grid=(T // tm,),
                 in_specs=[_row_spec(tm, D), _vec_spec(D), _vec_spec(D), _vec_spec(D)],
                 out_specs=[_row_spec(tm, D)], sem=("parallel",))


def _gate_rows(dxv, br_ref, g_ref, db_ref, acc_ref):
    db = g_ref[...] * dxv
    db_ref[...] = db.astype(BF16)
    acc_ref[3:4, :] += _colsum(dxv * br_ref[...].astype(F32))
    acc_ref[4:5, :] += _colsum(db)


def _norm_mod_bwd(x, dh, dres, w, sc, branch, g, name, side=None):
    T = x.shape[0]
    tm = _rows(T)
    gated = branch is not None

    def body(x_ref, dh_ref, dres_ref, w_ref, sc_ref, *rest):
        (br_ref, g_ref, dx_ref, db_ref, acc_ref) = rest if gated else (None, None, rest[0], None, rest[1])

        @pl.when(pl.program_id(0) == 0)
        def _():
            acc_ref[...] = jnp.zeros_like(acc_ref)
        xv, dh_v, wv = x_ref[...], dh_ref[...], w_ref[...]
        r = lax.rsqrt(jnp.mean(xv * xv, axis=-1, keepdims=True) + EPS)
        n = xv * r
        dnw = dh_v * (1.0 + sc_ref[...])
        dn = dnw * wv
        dxv = dres_ref[...] + r * (dn - n * jnp.mean(dn * n, axis=-1, keepdims=True))
        dx_ref[...] = dxv
        acc_ref[0:1, :] += _colsum(dh_v * (n * wv))
        acc_ref[1:2, :] += _colsum(dh_v)
        acc_ref[2:3, :] += _colsum(dnw * n)
        if gated:
            _gate_rows(dxv, br_ref, g_ref, db_ref, acc_ref)

    row = _row_spec(tm, D)
    args = (x, dh, dres, w, sc) + ((branch, g) if gated else ())
    return _call(body, args, side, name=name,
                 out_shape=[_sds((T, D))] + ([_sds((T, D), BF16)] if gated else []) + [_sds((8, D))], grid=(T // tm,),
                 in_specs=[row, row, row, _vec_spec(D), _vec_spec(D)] + ([row, _vec_spec(D)] if gated else []),
                 out_specs=[row] + ([row] if gated else []) + [_vec_spec(D, 8)], sem=("arbitrary",))


def _final_loss(x, wf, target, branch, g, name):
    T = x.shape[0]
    tm = _rows(T)

    def body(x_ref, w_ref, t_ref, br_ref, g_ref, dx_ref, db_ref, acc_ref):
        @pl.when(pl.program_id(0) == 0)
        def _():
            acc_ref[...] = jnp.zeros_like(acc_ref)
        xv, wv = x_ref[...], w_ref[...]
        r = lax.rsqrt(jnp.mean(xv * xv, axis=-1, keepdims=True) + EPS)
        n = xv * r
        err = n * wv - t_ref[...]
        dy = err * (1.0 / D)
        dn = dy * wv
        dxv = r * (dn - n * jnp.mean(dn * n, axis=-1, keepdims=True))
        dx_ref[...] = dxv
        acc_ref[0:1, :] += _colsum(dy * n)
        acc_ref[1:2, :] += jnp.broadcast_to(_allsum(err * err) * (0.5 / D), (1, D))
        _gate_rows(dxv, br_ref, g_ref, db_ref, acc_ref)

    row = _row_spec(tm, D)
    return _pcall(body, name=name, out_shape=[_sds((T, D)), _sds((T, D), BF16), _sds((8, D))], grid=(T // tm,),
                  in_specs=[row, _vec_spec(D), row, row, _vec_spec(D)],
                  out_specs=[row, row, _vec_spec(D, 8)], sem=("arbitrary",))(x, wf, target, branch, g)


def _mod_matmul(c_all, ada_w, name):
    n = ada_w.shape[2]

    def body(c_ref, w_ref, cond_ref, o_ref):
        cond = _silu(c_ref[...])
        cond_ref[...] = cond
        o_ref[0] = _nn(cond, w_ref[0])

    return _pcall(body, name=name, out_shape=[_sds((N_DEV, D)), _sds((DEPTH, N_DEV, n))], grid=(DEPTH,),
                  in_specs=[pl.BlockSpec((N_DEV, D), lambda l: (0, 0)), pl.BlockSpec((1, D, n), lambda l: (l, 0, 0))],
                  out_specs=[pl.BlockSpec((N_DEV, D), lambda l: (0, 0)), pl.BlockSpec((1, N_DEV, n), lambda l: (l, 0, 0))],
                  sem=("arbitrary",))(c_all, ada_w)


def _add_rows(a, b, name):
    def body(a_ref, b_ref, o_ref):
        o_ref[...] = a_ref[...] + b_ref[...]

    return _pcall(body, name=name, out_shape=_sds(a.shape))(a, b)


def _ada_w_grad(cond, dmod_cols, name):
    n = dmod_cols.shape[2]

    def body(c_ref, d_ref, o_ref):
        o_ref[0] = _tn(c_ref[...], d_ref[0])

    return _pcall(body, name=name, out_shape=_sds((DEPTH, D, n)), grid=(DEPTH,),
                  in_specs=[pl.BlockSpec((N_DEV, D), lambda l: (0, 0)), pl.BlockSpec((1, N_DEV, n), lambda l: (l, 0, 0))],
                  out_specs=pl.BlockSpec((1, D, n), lambda l: (l, 0, 0)), sem=("parallel",))(cond, dmod_cols)


def _conv_fwd(pm, conv_w, conv_b, name):
    T = pm.shape[0]
    tm = _rows(T)
    C = CONV_DIM

    def body(x_ref, prev_ref, w_ref, b_ref, o_ref):
        cur = x_ref[...].astype(F32)
        prev = jnp.where(pl.program_id(0) > 0, prev_ref[...].astype(F32)[8:16], 0.0)
        cur8 = cur[0:8]
        row8 = lax.broadcasted_iota(jnp.int32, (8, C), 0)
        full = w_ref[3:4, :] * cur
        head = w_ref[3:4, :] * cur8
        for k in range(1, SSM_CONV):
            wk = w_ref[3 - k:4 - k, :]
            full = full + wk * pltpu.roll(cur, k, 0)
            head = head + wk * jnp.where(row8 < k, pltpu.roll(prev, k, 0), pltpu.roll(cur8, k, 0))
        o_ref[...] = full + b_ref[...]
        o_ref[0:8, :] = head + b_ref[...]

    return _pcall(
        body, name=name, out_shape=_sds((T, C)), grid=(T // tm,),
        in_specs=[pl.BlockSpec((tm, C), lambda i: (i, 2)),
                  pl.BlockSpec((16, C), lambda i: (jnp.maximum(i * (tm // 16) - 1, 0), 2)),
                  _vec_spec(C, SSM_CONV), _vec_spec(C)],
        out_specs=_row_spec(tm, C), sem=("parallel",))(pm, pm, conv_w, conv_b)


def _conv_bwd(dc, pm, conv_w, name, side=None):
    T = dc.shape[0]
    tm = _rows(T)
    C = CONV_DIM
    nt = T // tm

    def body(dc_ref, nxt_ref, x_ref, w_ref, dx_ref, acc_ref):
        i = pl.program_id(0)

        @pl.when(i == 0)
        def _():
            acc_ref[...] = jnp.zeros_like(acc_ref)
        dcv = dc_ref[...]
        nxt = jnp.where(i < nt - 1, nxt_ref[...], 0.0)
        xc = x_ref[...].astype(F32)
        dc8t, x8t = dcv[tm - 8:tm], xc[tm - 8:tm]
        row8 = lax.broadcasted_iota(jnp.int32, (8, C), 0)
        full = w_ref[3:4, :] * dcv
        tail = w_ref[3:4, :] * dc8t
        acc_ref[3:4, :] += _colsum(dcv * xc)
        for k in range(1, SSM_CONV):
            wk = w_ref[3 - k:4 - k, :]
            up = pltpu.roll(dcv, tm - k, 0)
            up_tail = jnp.where(row8 + k >= 8, pltpu.roll(nxt, 8 - k, 0), pltpu.roll(dc8t, 8 - k, 0))
            full = full + wk * up
            tail = tail + wk * up_tail
            prod = up * xc
            acc_ref[3 - k:4 - k, :] += _colsum(prod) - _colsum(prod[tm - 8:tm]) + _colsum(up_tail * x8t)
        acc_ref[4:5, :] += _colsum(dcv)
        dx_ref[...] = jnp.concatenate([full[0:tm - 8], tail], axis=0).astype(BF16)

    return _call(
        body, (dc, dc, pm, conv_w), side, name=name, out_shape=[_sds((T, C), BF16), _sds((8, C))], grid=(nt,),
        in_specs=[_row_spec(tm, C),
                  pl.BlockSpec((8, C), lambda i: (jnp.minimum((i + 1) * (tm // 8), T // 8 - 1), 0)),
                  pl.BlockSpec((tm, C), lambda i: (i, 2)),
                  _vec_spec(C, SSM_CONV)],
        out_specs=[_row_spec(tm, C), _vec_spec(C, 8)], sem=("arbitrary",))


def _ssd_prologue(cpre, dtr, dtb, alog):
    L = CHUNK
    xc = _silu(cpre)
    pre = dtr + dtb
    dt = jnp.maximum(pre, 0.0) + jnp.log1p(jnp.exp(-jnp.abs(pre)))
    a = -jnp.exp(alog)
    la = dt * a
    row = lax.broadcasted_iota(jnp.int32, (L, L), 0)
    col = lax.broadcasted_iota(jnp.int32, (L, L), 1)
    causal = row >= col
    tri = causal.astype(F32)
    lc = _nn(tri, la, HIGHEST)
    return xc, pre, dt, a, causal, tri, lc, row, col


def _head_indicator():
    m = np.zeros((LANES, SSM_INNER), np.float32)
    for h in range(SSM_HEADS):
        m[h, h * SSM_HEAD_DIM:(h + 1) * SSM_HEAD_DIM] = 1.0
    return jnp.asarray(m, dtype=BF16)


def _split_dot(x, ind, dims):
    hi = x.astype(BF16)
    lo = (x - hi.astype(F32)).astype(BF16)
    return _dot(hi, ind, dims) + _dot(lo, ind, dims)


def _expand(x16, ind):
    return _split_dot(x16, ind, ((1,), (0,)))


def _headsum(x, ind, single_pass=False):
    if single_pass:
        return _dot(x.astype(BF16), ind, ((1,), (1,)))
    return _split_dot(x, ind, ((1,), (1,)))


def _ssd_fwd(cpre, dtr, pm, dtb, alog, dskip, normw, ind, name, side=None):
    T = cpre.shape[0]
    nc = T // CHUNK
    L, P, H, HPG, N = CHUNK, SSM_HEAD_DIM, SSM_HEADS, SSM_HEADS // SSM_GROUPS, SSM_STATE
    half = SSM_INNER // SSM_GROUPS

    def body(cp_ref, dtr_ref, z_ref, dtb_ref, alog_ref, dskip_ref, nw_ref, ind_ref, ya_ref, y_ref, sp_ref, st_ref):
        @pl.when(pl.program_id(0) == 0)
        def _():
            st_ref[...] = jnp.zeros_like(st_ref)
        xc, _, dt, _, causal, _, lc, _, _ = _ssd_prologue(cp_ref[...], dtr_ref[...], dtb_ref[...], alog_ref[...])
        lct = lc.T
        ind = ind_ref[...]
        llast = lc[L - 1:L, :]
        xs = xc[:, :SSM_INNER]
        xd = xs * _expand(dt, ind)
        ex = _expand(jnp.exp(lc), ind)
        xd_end = _bf(xd * _expand(jnp.exp(llast - lc), ind))
        cdx = _expand(jnp.broadcast_to(jnp.exp(llast), (8, LANES)), ind)[0:1]
        xdb = _bf(xd)
        sp_ref[0] = st_ref[...]
        for g in range(SSM_GROUPS):
            sl = slice(g * half, (g + 1) * half)
            bm = _bf(xc[:, SSM_INNER + g * N:SSM_INNER + (g + 1) * N])
            cm = _bf(xc[:, SSM_INNER + (SSM_GROUPS + g) * N:SSM_INNER + (SSM_GROUPS + g + 1) * N])
            cb = _nt(cm, bm)
            st = st_ref[g]
            y_ref[:, sl] = ex[:, sl] * _nn(cm, _bf(st)) + dskip_ref[:, sl] * xs[:, sl]
            st_ref[g] = st * cdx[:, sl] + _tn(bm, xd_end[:, sl])
            for j in range(HPG):
                h = g * HPG + j
                decay = jnp.where(causal, jnp.exp(jnp.where(causal, lc[:, h:h + 1] - lct[h:h + 1, :], 0.0)), 0.0)
                y_ref[:, h * P:(h + 1) * P] += _nn(_bf(cb * decay), xdb[:, h * P:(h + 1) * P])
        y2 = y_ref[...] * _silu(z_ref[...].astype(F32))
        for g in range(SSM_GROUPS):
            yg = y2[:, g * half:(g + 1) * half]
            r = lax.rsqrt(jnp.mean(yg * yg, axis=-1, keepdims=True) + EPS)
            ya_ref[:, g * half:(g + 1) * half] = (yg * r * nw_ref[:, g * half:(g + 1) * half]).astype(BF16)

    return _call(
        body, (cpre, dtr, pm, dtb, alog, dskip, normw, ind), side, name=name,
        out_shape=[_sds((T, SSM_INNER), BF16), _sds((T, SSM_INNER)), _sds((nc, SSM_GROUPS, N, half))], grid=(nc,),
        in_specs=[_row_spec(L, CONV_DIM), _row_spec(L, LANES), _row_spec(L, SSM_INNER, 0),
                  _vec_spec(LANES), _vec_spec(LANES), _vec_spec(SSM_INNER), _vec_spec(SSM_INNER), _vec_spec(SSM_INNER, LANES)],
        out_specs=[_row_spec(L, SSM_INNER), _row_spec(L, SSM_INNER),
                   pl.BlockSpec((1, SSM_GROUPS, N, half), lambda i: (i, 0, 0, 0))],
        scratch=[pltpu.VMEM((SSM_GROUPS, N, half), F32)], sem=("arbitrary",))


def _ssd_bwd(cpre, dtr, pm, ypre, sprev, dya, dtb, alog, dskip, normw, ind, name, side=None):
    T = cpre.shape[0]
    nc = T // CHUNK
    L, P, H, HPG, N = CHUNK, SSM_HEAD_DIM, SSM_HEADS, SSM_HEADS // SSM_GROUPS, SSM_STATE
    half = SSM_INNER // SSM_GROUPS

    def body(cp_ref, dtr_ref, z_ref, y_ref, sp_ref, dya_ref, dtb_ref, alog_ref, dskip_ref, nw_ref, ind_ref,
             dz_ref, dcp_ref, ddtr_ref, acc_ref, dnw_ref, ds_ref, dy_ref, dxd_ref, rr_ref, yoff_ref, dcd_ref):
        @pl.when(pl.program_id(0) == 0)
        def _():
            ds_ref[...] = jnp.zeros_like(ds_ref)
            acc_ref[...] = jnp.zeros_like(acc_ref)
            dnw_ref[...] = jnp.zeros_like(dnw_ref)
        cpre_v = cp_ref[...]
        xc, pre, dt, a, causal, tri, lc, row, col = _ssd_prologue(cpre_v, dtr_ref[...], dtb_ref[...], alog_ref[...])
        lct = lc.T
        zv, yv = z_ref[...].astype(F32), y_ref[...]
        sz, dsz = _silu_and_grad(zv)
        y2 = yv * sz
        dya_v = dya_ref[...]
        nwv = nw_ref[...]
        for g in range(SSM_GROUPS):
            sl = slice(g * half, (g + 1) * half)
            yg = y2[:, sl]
            r = lax.rsqrt(jnp.mean(yg * yg, axis=-1, keepdims=True) + EPS)
            nrm = yg * r
            dnw_ref[:, sl] += _colsum(dya_v[:, sl] * nrm)
            dn = dya_v[:, sl] * nwv[:, sl]
            dy2 = r * (dn - nrm * jnp.mean(dn * nrm, axis=-1, keepdims=True))
            dy_ref[:, sl] = dy2 * sz[:, sl]
            dz_ref[:, sl] = (dy2 * yv[:, sl] * dsz[:, sl]).astype(BF16)
        ind = ind_ref[...]
        llast = lc[L - 1:L, :]
        dte16 = jnp.exp(llast - lc)
        cd16 = jnp.exp(llast)
        xs = xc[:, :SSM_INNER]
        dtx = _expand(dt, ind)
        ex = _expand(jnp.exp(lc), ind)
        dtex = _expand(dte16, ind)
        cdx = _expand(jnp.broadcast_to(cd16, (8, LANES)), ind)[0:1]
        xd = xs * dtx
        xdb = _bf(xd)
        xd_end = _bf(xd * dtex)
        dyv = dy_ref[...]
        dy_off = _bf(ex * dyv)
        dyb = _bf(dyv)
        dskx = dskip_ref[...]
        lane_c = lax.broadcasted_iota(jnp.int32, (L, LANES), 1)
        lane1 = lax.broadcasted_iota(jnp.int32, (1, LANES), 1)
        sub16 = lax.broadcasted_iota(jnp.int32, (H, L), 0)
        dlc_c = jnp.zeros((L, LANES), F32)
        dlc_r = jnp.zeros((H, L), F32)
        for g in range(SSM_GROUPS):
            sl = slice(g * half, (g + 1) * half)
            b_lo = SSM_INNER + g * N
            c_lo = SSM_INNER + (SSM_GROUPS + g) * N
            bm, cm = _bf(xc[:, b_lo:b_lo + N]), _bf(xc[:, c_lo:c_lo + N])
            cb = _nt(cm, bm)
            st, dst = sp_ref[0, g], ds_ref[g]
            stb, dstb = _bf(st), _bf(dst)
            dcm = _nt(dy_off[:, sl], stb)
            ds_ref[g] = _tn(cm, dy_off[:, sl]) + dst * cdx[:, sl]
            rr_ref[:, sl] = _nn(bm, dstb)
            yoff_ref[:, sl] = ex[:, sl] * _nn(cm, stb)
            db = _nt(xd_end[:, sl], dstb)
            dcd_ref[:, sl] = _colsum(dst * st)
            dcb = jnp.zeros((L, L), F32)
            for j in range(HPG):
                h = g * HPG + j
                hs = slice(h * P, (h + 1) * P)
                decay = jnp.where(causal, jnp.exp(jnp.where(causal, lc[:, h:h + 1] - lct[h:h + 1, :], 0.0)), 0.0)
                m = cb * decay
                dxd_ref[:, hs] = _tn(_bf(m), dyb[:, hs])
                dm = _nt(dyb[:, hs], xdb[:, hs])
                dcb = dcb + dm * decay
                gm = dm * m
                dlc_c = dlc_c + jnp.where(lane_c == h, _rowsum(gm), 0.0)
                dlc_r = dlc_r + jnp.where(sub16 == h, _colsum(gm), 0.0)
            dcbb = _bf(dcb)
            dcp_ref[:, c_lo:c_lo + N] = dcm + _nn(dcbb, bm)
            dcp_ref[:, b_lo:b_lo + N] = db + _tn(dcbb, cm)
        dxd_diag, rr = dxd_ref[...], rr_ref[...]
        tt = _headsum(rr * xd, ind, single_pass=True) * dte16
        dlc_rt = jnp.concatenate([dlc_r, jnp.zeros((LANES - H, L), F32)], axis=0).T
        dlc = dlc_c - dlc_rt + _headsum(dyv * yoff_ref[...], ind, single_pass=True) - tt
        dcd = _headsum(jnp.broadcast_to(dcd_ref[...], (8, SSM_INNER)), ind)[0:1]
        dlc = dlc + jnp.where(row == L - 1, _colsum(tt) + dcd * cd16, 0.0)
        dla = _tn(tri, dlc, HIGHEST)
        dxd = dxd_diag + dtex * rr
        ddt = _headsum(dxd * xs, ind, single_pass=True) + dla * a
        ddtr = jnp.where(lane_c < H, ddt * _sigmoid(pre), 0.0)
        ddtr_ref[...] = ddtr
        acc_ref[0:1, :] += _colsum(ddtr)
        acc_ref[1:2, :] += jnp.where(lane1 < H, _colsum(dla * dt) * a, 0.0)
        acc_ref[2:3, :] += _headsum(jnp.broadcast_to(_colsum(dyv * xs), (8, SSM_INNER)), ind)[0:1]
        dcp_ref[:, 0:SSM_INNER] = dxd * dtx + dskx * dyv
        dcp_ref[...] = dcp_ref[...] * _dsilu(cpre_v)

    rev = lambda i: (nc - 1 - i, 0)
    rspec = lambda c: pl.BlockSpec((L, c), rev)
    return _call(
        body, (cpre, dtr, pm, ypre, sprev, dya, dtb, alog, dskip, normw, ind), side, name=name,
        out_shape=[_sds((T, SSM_INNER), BF16), _sds((T, CONV_DIM)), _sds((T, LANES)), _sds((8, LANES)), _sds((1, SSM_INNER))],
        grid=(nc,),
        in_specs=[rspec(CONV_DIM), rspec(LANES), rspec(SSM_INNER), rspec(SSM_INNER),
                  pl.BlockSpec((1, SSM_GROUPS, N, half), lambda i: (nc - 1 - i, 0, 0, 0)), rspec(SSM_INNER),
                  _vec_spec(LANES), _vec_spec(LANES), _vec_spec(SSM_INNER), _vec_spec(SSM_INNER), _vec_spec(SSM_INNER, LANES)],
        out_specs=[rspec(SSM_INNER), rspec(CONV_DIM), rspec(LANES), _vec_spec(LANES, 8), _vec_spec(SSM_INNER)],
        scratch=[pltpu.VMEM((SSM_GROUPS, N, half), F32), pltpu.VMEM((L, SSM_INNER), F32), pltpu.VMEM((L, SSM_INNER), F32),
                 pltpu.VMEM((L, SSM_INNER), F32), pltpu.VMEM((L, SSM_INNER), F32), pltpu.VMEM((1, SSM_INNER), F32)],
        sem=("arbitrary",))


def _gmlp_common(u, v, lnw, lnb, with_grads=False):
    (ug, dug), (vg, dvg) = (_gelu_and_grad(u), _gelu_and_grad(v)) if with_grads else ((_gelu(u), None), (_gelu(v), None))
    mu = jnp.mean(vg, axis=-1, keepdims=True)
    cen = vg - mu
    rstd = lax.rsqrt(jnp.mean(cen * cen, axis=-1, keepdims=True) + EPS)
    vhat = cen * rstd
    out = (ug, rstd, vhat, vhat * lnw + lnb)
    return out + (dug, dvg) if with_grads else out


def _causal_mask():
    row = lax.broadcasted_iota(jnp.int32, (CHUNK, CHUNK), 0)
    col = lax.broadcasted_iota(jnp.int32, (CHUNK, CHUNK), 1)
    return row >= col


def _gmlp_rows(T):
    return max(r for r in (4 * CHUNK, 2 * CHUNK, CHUNK) if T % r == 0)


def _gmlp_fwd(pm, lnw, lnb, ws, bs_exp, name, side=None):
    T = pm.shape[0]
    L, G = CHUNK, GMLP_GROUPS
    R = _gmlp_rows(T)

    def body(u_ref, v_ref, lnw_ref, lnb_ref, ws_ref, bs_ref, o_ref):
        ug, _, _, vn = _gmlp_common(u_ref[...].astype(F32), v_ref[...].astype(F32), lnw_ref[...], lnb_ref[...])
        causal = _causal_mask()
        for g in range(G):
            sl = slice(g * L, (g + 1) * L)
            wm = _bf(jnp.where(causal, ws_ref[g], 0.0))
            for c in range(R // L):
                rows = slice(c * L, (c + 1) * L)
                sv = _nn(wm, _bf(vn[rows, sl])) + bs_ref[:, sl]
                o_ref[rows, sl] = (ug[rows, sl] * sv).astype(BF16)

    return _call(
        body, (pm, pm, lnw, lnb, ws, bs_exp), side, name=name, out_shape=[_sds((T, GMLP_INNER), BF16)], grid=(T // R,),
        in_specs=[_row_spec(R, GMLP_INNER, 1), _row_spec(R, GMLP_INNER, 2), _vec_spec(GMLP_INNER), _vec_spec(GMLP_INNER),
                  pl.BlockSpec((G, L, L), lambda i: (0, 0, 0)), _vec_spec(GMLP_INNER, L)],
        out_specs=[_row_spec(R, GMLP_INNER)], sem=("parallel",))


def _gmlp_bwd(pm, dyb, lnw, lnb, ws, bs_exp, name, side=None):
    T = pm.shape[0]
    L, G = CHUNK, GMLP_GROUPS
    R = _gmlp_rows(T)

    def body(u_ref, v_ref, dy_ref, lnw_ref, lnb_ref, ws_ref, bs_ref, du_ref, dv_ref, dws_ref, dbs_ref, acc_ref, dvn_ref):
        @pl.when(pl.program_id(0) == 0)
        def _():
            dws_ref[...] = jnp.zeros_like(dws_ref)
            dbs_ref[...] = jnp.zeros_like(dbs_ref)
            acc_ref[...] = jnp.zeros_like(acc_ref)
        uv, vv, dyv, lnwv = u_ref[...].astype(F32), v_ref[...].astype(F32), dy_ref[...], lnw_ref[...]
        ug, rstd, vhat, vn, dug, dvg_act = _gmlp_common(uv, vv, lnwv, lnb_ref[...], with_grads=True)
        causal = _causal_mask()
        lane = lax.broadcasted_iota(jnp.int32, (L, LANES), 1)
        dbs = jnp.zeros((L, LANES), F32)
        for g in range(G):
            sl = slice(g * L, (g + 1) * L)
            wm = _bf(jnp.where(causal, ws_ref[g], 0.0))
            for c in range(R // L):
                rows = slice(c * L, (c + 1) * L)
                vng = _bf(vn[rows, sl])
                sv = _nn(wm, vng) + bs_ref[:, sl]
                du_ref[rows, sl] = (dyv[rows, sl] * sv * dug[rows, sl]).astype(BF16)
                dsv = dyv[rows, sl] * ug[rows, sl]
                dsvb = _bf(dsv)
                dws_ref[g] += jnp.where(causal, _nt(dsvb, vng), 0.0)
                dbs = dbs + jnp.where(lane == g, _rowsum(dsv), 0.0)
                dvn_ref[rows, sl] = _tn(wm, dsvb)
        dbs_ref[...] += dbs
        dvn = dvn_ref[...]
        acc_ref[0:1, :] += _colsum(dvn * vhat)
        acc_ref[1:2, :] += _colsum(dvn)
        dvh = dvn * lnwv
        dvg = rstd * (dvh - jnp.mean(dvh, axis=-1, keepdims=True) - vhat * jnp.mean(dvh * vhat, axis=-1, keepdims=True))
        dv_ref[...] = (dvg * dvg_act).astype(BF16)

    return _call(
        body, (pm, pm, dyb, lnw, lnb, ws, bs_exp), side, name=name,
        out_shape=[_sds((T, GMLP_INNER), BF16), _sds((T, GMLP_INNER), BF16), _sds((G, L, L)), _sds((L, LANES)), _sds((8, GMLP_INNER))],
        grid=(T // R,),
        in_specs=[_row_spec(R, GMLP_INNER, 1), _row_spec(R, GMLP_INNER, 2), _row_spec(R, GMLP_INNER),
                  _vec_spec(GMLP_INNER), _vec_spec(GMLP_INNER), pl.BlockSpec((G, L, L), lambda i: (0, 0, 0)),
                  _vec_spec(GMLP_INNER, L)],
        out_specs=[_row_spec(R, GMLP_INNER), _row_spec(R, GMLP_INNER), pl.BlockSpec((G, L, L), lambda i: (0, 0, 0)),
                   _vec_spec(LANES, L), _vec_spec(GMLP_INNER, 8)],
        scratch=[pltpu.VMEM((R, GMLP_INNER), F32)], sem=("arbitrary",))


def _rel_buckets():
    qi = np.arange(CHUNK)[:, None]
    sj = np.arange(2 * CHUNK)[None, :]
    dist = np.maximum(qi + CHUNK - sj, 0)
    max_exact = REL_BUCKETS // 2
    log_ratio = (np.log(np.maximum(dist, 1).astype(np.float32) / np.float32(max_exact))
                 / np.float32(math.log(REL_MAX_DIST / max_exact))).astype(np.float32)
    large = max_exact + (log_ratio * np.float32(REL_BUCKETS - max_exact)).astype(np.int32)
    return np.where(dist < max_exact, dist, np.minimum(large, REL_BUCKETS - 1))


def _bucket_onehot_t():
    bucket = _rel_buckets().reshape(-1)
    return jnp.asarray((np.arange(REL_BUCKETS)[:, None] == bucket[None, :]).astype(np.float32))


def _bias_from_table(table_t, onehot_t, window, name):
    def body(t_ref, o_ref, w_ref, out_ref):
        out_ref[...] = jnp.where(w_ref[...] > 0.5, _nn(t_ref[...], o_ref[...], HIGHEST), NEG_INF)

    return _pcall(body, name=name, out_shape=_sds((ATTN_HEADS, onehot_t.shape[1])))(table_t, onehot_t, window)


def _table_from_dbias(dbias, onehot_t, name):
    def body(d_ref, o_ref, out_ref):
        out_ref[...] = _nt(o_ref[...], d_ref[...], HIGHEST)

    return _pcall(body, name=name, out_shape=_sds((REL_BUCKETS, ATTN_HEADS)))(dbias, onehot_t)


def _softmax_sink(logits, sink):
    mx = jnp.maximum(jnp.max(logits, axis=-1, keepdims=True), sink)
    e = jnp.exp(logits - mx)
    es = jnp.exp(sink - mx)
    inv = 1.0 / (_rowsum(e) + es)
    return e * inv, es * inv


def _first_block_penalty(n):
    sj = lax.broadcasted_iota(jnp.int32, (1, 2 * CHUNK), 1)
    return jnp.where((sj < CHUNK) & (n == 0), NEG_INF, 0.0)


def _window_mask_flat():
    qi = np.arange(CHUNK)[:, None]
    sj = np.arange(2 * CHUNK)[None, :]
    rel = qi + CHUNK - sj
    return jnp.asarray(((rel >= 0) & (rel < CHUNK)).astype(np.float32).reshape(1, -1))


def _stack_heads(ref, first, count, width, scale=None):
    x = jnp.concatenate([_bf(ref[:, (first + j) * width:(first + j + 1) * width]) for j in range(count)], axis=0)
    return x if scale is None else x * jnp.asarray(scale, x.dtype)


def _attn_fwd(qkv, bias, sinks, name):
    T = qkv.shape[0]
    nb = T // CHUNK
    L, DH, HPK = CHUNK, ATTN_DH, ATTN_HEADS // ATTN_KV
    scale = DH ** -0.5
    kcol, vcol = ATTN_HEADS * DH // LANES, ATTN_HEADS * DH // LANES + 1

    def body(q_ref, k_ref, v_ref, kp_ref, vp_ref, bias_ref, sink_ref, o_ref, lg_ref, p_ref):
        n = pl.program_id(0)
        pen = _first_block_penalty(n)
        kband = _bf(jnp.concatenate([kp_ref[...], k_ref[...]], axis=0))
        vband = _bf(jnp.concatenate([vp_ref[...], v_ref[...]], axis=0))
        for kv in range(ATTN_KV):
            lg_ref[...] = _nt(_stack_heads(q_ref, kv * HPK, HPK, DH, scale), kband[:, kv * DH:(kv + 1) * DH])
            for j in range(HPK):
                h = kv * HPK + j
                p, _ = _softmax_sink(lg_ref[j * L:(j + 1) * L, :] + bias_ref[h] + pen, sink_ref[h])
                p_ref[j * L:(j + 1) * L, :] = _bf(p)
            og = _nn(p_ref[...], vband[:, kv * DH:(kv + 1) * DH])
            for j in range(HPK):
                h = kv * HPK + j
                o_ref[:, h * DH:(h + 1) * DH] = og[j * L:(j + 1) * L].astype(BF16)

    prev = lambda i: jnp.maximum(i - 1, 0)
    return _pcall(
        body, name=name, out_shape=_sds((T, ATTN_HEADS * DH), BF16), grid=(nb,),
        in_specs=[_row_spec(L, ATTN_HEADS * DH, 0), _row_spec(L, LANES, kcol), _row_spec(L, LANES, vcol),
                  pl.BlockSpec((L, LANES), lambda i: (prev(i), kcol)), pl.BlockSpec((L, LANES), lambda i: (prev(i), vcol)),
                  pl.BlockSpec((ATTN_HEADS, L, 2 * L), lambda i: (0, 0, 0)),
                  pl.BlockSpec(memory_space=pltpu.SMEM)],
        out_specs=_row_spec(L, ATTN_HEADS * DH),
        scratch=[pltpu.VMEM((HPK * L, 2 * L), F32), pltpu.VMEM((HPK * L, 2 * L), BF16)],
        sem=("parallel",))(qkv, qkv, qkv, qkv, qkv, bias, sinks)


def _attn_bwd(qkv, datt, bias, sinks, name, side=None):
    T = qkv.shape[0]
    nb = T // CHUNK
    L, DH, HPK = CHUNK, ATTN_DH, ATTN_HEADS // ATTN_KV
    scale = DH ** -0.5
    kcol, vcol = ATTN_HEADS * DH // LANES, ATTN_HEADS * DH // LANES + 1

    def body(q_ref, k_ref, v_ref, kp_ref, vp_ref, do_ref, bias_ref, sink_ref,
             dq_ref, dk_ref, dv_ref, bsum_ref, dbias_ref, dsink_ref, pend_k, pend_v, band_k, band_v, lg_ref, dp_ref, p_ref, dl_ref):
        n = pl.program_id(0)

        @pl.when(n == 0)
        def _():
            dbias_ref[...] = jnp.zeros_like(dbias_ref)
            dsink_ref[...] = jnp.zeros_like(dsink_ref)
            bsum_ref[...] = jnp.zeros_like(bsum_ref)

        def emit_kv(dk, dv):
            dk_ref[...] = dk.astype(BF16)
            dv_ref[...] = dv.astype(BF16)
            bsum_ref[:, ATTN_HEADS * DH:ATTN_HEADS * DH + LANES] += _colsum(dk)
            bsum_ref[:, ATTN_HEADS * DH + LANES:] += _colsum(dv)

        @pl.when(n < nb)
        def _():
            pen = _first_block_penalty(n)
            kband = _bf(jnp.concatenate([kp_ref[...], k_ref[...]], axis=0))
            vband = _bf(jnp.concatenate([vp_ref[...], v_ref[...]], axis=0))
            lane1 = lax.broadcasted_iota(jnp.int32, (1, LANES), 1)
            dsink = jnp.zeros((1, LANES), F32)
            for kv in range(ATTN_KV):
                kb, vb = kband[:, kv * DH:(kv + 1) * DH], vband[:, kv * DH:(kv + 1) * DH]
                qg = _stack_heads(q_ref, kv * HPK, HPK, DH, scale)
                dog = _stack_heads(do_ref, kv * HPK, HPK, DH)
                lg_ref[...] = _nt(qg, kb)
                dp_ref[...] = _nt(dog, vb)
                for j in range(HPK):
                    h = kv * HPK + j
                    rows = slice(j * L, (j + 1) * L)
                    p, ps = _softmax_sink(lg_ref[rows, :] + bias_ref[h] + pen, sink_ref[h])
                    dp = dp_ref[rows, :]
                    delta = _rowsum(p * dp)
                    dl = p * (dp - delta)
                    dbias_ref[h] += dl
                    p_ref[rows, :] = _bf(p)
                    dl_ref[rows, :] = _bf(dl)
                    dsink = dsink + jnp.where(lane1 == h, -_colsum(ps * delta), 0.0)
                band_v[:, kv * DH:(kv + 1) * DH] = _tn(p_ref[...], dog)
                dqg = _nn(dl_ref[...], kb) * scale
                band_k[:, kv * DH:(kv + 1) * DH] = _tn(dl_ref[...], qg)
                for j in range(HPK):
                    h = kv * HPK + j
                    dq_ref[:, h * DH:(h + 1) * DH] = dqg[j * L:(j + 1) * L].astype(BF16)
                    bsum_ref[:, h * DH:(h + 1) * DH] += _colsum(dqg[j * L:(j + 1) * L])
            dsink_ref[...] += dsink

            @pl.when(n > 0)
            def _():
                emit_kv(pend_k[...] + band_k[0:L, :], pend_v[...] + band_v[0:L, :])
            pend_k[...] = band_k[L:2 * L, :]
            pend_v[...] = band_v[L:2 * L, :]

        @pl.when(n == nb)
        def _():
            emit_kv(pend_k[...], pend_v[...])

    cur = lambda i: jnp.minimum(i, nb - 1)
    prev = lambda i: jnp.maximum(jnp.minimum(i, nb - 1) - 1, 0)
    lag = lambda i: jnp.maximum(i - 1, 0)
    return _call(
        body, (qkv, qkv, qkv, qkv, qkv, datt, bias, sinks), side, name=name,
        out_shape=[_sds((T, ATTN_HEADS * DH), BF16), _sds((T, LANES), BF16), _sds((T, LANES), BF16), _sds((1, QKV_DIM)),
                   _sds((ATTN_HEADS, L, 2 * L)), _sds((1, LANES))],
        grid=(nb + 1,),
        in_specs=[pl.BlockSpec((L, ATTN_HEADS * DH), lambda i: (cur(i), 0)),
                  pl.BlockSpec((L, LANES), lambda i: (cur(i), kcol)), pl.BlockSpec((L, LANES), lambda i: (cur(i), vcol)),
                  pl.BlockSpec((L, LANES), lambda i: (prev(i), kcol)), pl.BlockSpec((L, LANES), lambda i: (prev(i), vcol)),
                  pl.BlockSpec((L, ATTN_HEADS * DH), lambda i: (cur(i), 0)),
                  pl.BlockSpec((ATTN_HEADS, L, 2 * L), lambda i: (0, 0, 0)),
                  pl.BlockSpec(memory_space=pltpu.SMEM)],
        out_specs=[pl.BlockSpec((L, ATTN_HEADS * DH), lambda i: (cur(i), 0)),
                   pl.BlockSpec((L, LANES), lambda i: (lag(i), 0)), pl.BlockSpec((L, LANES), lambda i: (lag(i), 0)),
                   _vec_spec(QKV_DIM), pl.BlockSpec((ATTN_HEADS, L, 2 * L), lambda i: (0, 0, 0)), _vec_spec(LANES)],
        scratch=[pltpu.VMEM((L, LANES), F32), pltpu.VMEM((L, LANES), F32),
                 pltpu.VMEM((2 * L, LANES), F32), pltpu.VMEM((2 * L, LANES), F32),
                 pltpu.VMEM((HPK * L, 2 * L), F32), pltpu.VMEM((HPK * L, 2 * L), F32),
                 pltpu.VMEM((HPK * L, 2 * L), BF16), pltpu.VMEM((HPK * L, 2 * L), BF16)],
        sem=("arbitrary",))


def _pad_rows(a, mult):
    pad = (-a.shape[-2]) % mult
    if pad == 0:
        return a
    cfg = [(0, 0)] * (a.ndim - 2) + [(0, pad), (0, 0)]
    return jnp.pad(a, cfg)


class _Pack:
    def __init__(self, width, mult, total_mult):
        self.width, self.mult, self.total_mult = width, mult, total_mult
        self.entries = []
        self.rows = 0

    def add(self, key, shape):
        n = int(np.prod(shape))
        rows = -(-n // self.width)
        self.entries.append((key, self.rows, rows, tuple(shape)))
        self.rows += -(-rows // self.mult) * self.mult

    @property
    def total(self):
        return -(-self.rows // self.total_mult) * self.total_mult

    def pack(self, pieces, dtype, lead=()):
        parts = []
        for key, _, rows, shape in self.entries:
            a = pieces[key].astype(dtype).reshape(lead + (-1,))
            n = int(np.prod(shape))
            a = jnp.pad(a, [(0, 0)] * len(lead) + [(0, rows * self.width - n)])
            a = a.reshape(lead + (rows, self.width))
            parts.append(_pad_rows(a, self.mult))
        out = jnp.concatenate(parts, axis=len(lead))
        return _pad_rows(out, self.total_mult)

    def unpack(self, packed, lead=()):
        out = {}
        for key, off, rows, shape in self.entries:
            a = lax.slice_in_dim(packed, off, off + rows, axis=len(lead))
            a = a.reshape(lead + (-1,))
            n = int(np.prod(shape))
            out[key] = lax.slice_in_dim(a, 0, n, axis=len(lead)).reshape(lead + shape)
        return out


def _ffn_fwd(x, h, mod, wg_t, wu_t, wd, tag, next_norm=None, gather=None):
    side = None if gather is None else (_GatherOps, gather)
    gate, up, act, *gathered = _mm_swiglu(h, wg_t, wu_t, f"ffn_gateup_{tag}", side=side)
    x_out, ffn_out, *h_next = _mm_resid([(act, wd)], x, mod[5:6], name=f"ffn_down_{tag}", norm=next_norm)
    return (x_out, dict(h=h, gate=gate, up=up, act=act, out=ffn_out), *h_next, *gathered)


def _ffn_bwd(dx_out, dffn, x_in, saved, mod, norm_w, wg_t, wu_t, wd, below, tag, exchange=None):
    side = None if exchange is None else (_ChipsOps, exchange)
    dgate, dup, *from_chips = _mm_swiglu_bwd(dffn, wd, saved["gate"], saved["up"], f"ffn_act_bwd_{tag}", side=side)
    d_wd = _mm_tn(saved["act"], dffn, name=f"ffn_dwd_{tag}")
    d_wg_t = _mm_tn(dgate, saved["h"], name=f"ffn_dwg_{tag}")
    d_wu_t = _mm_tn(dup, saved["h"], name=f"ffn_dwu_{tag}")
    dh = _mm([(dgate, wg_t), (dup, wu_t)], "nn", name=f"ffn_dh_{tag}")
    dx, d_below, acc = _norm_mod_bwd(x_in, dh, dx_out, norm_w, mod[4:5], below[0], below[1], f"ffn_norm_bwd_{tag}")
    return (dx, d_below, dict(d_wg=d_wg_t, d_wu=d_wu_t, d_wd=d_wd, acc=acc), *from_chips)


_BIG = [
    ("out_w", "out_w_even", 0, "row"), ("qkv_w", "qkv_w", 0, "col"), ("o_w", "o_w", 0, "row"),
    ("gate0", "ffn_gate_w", 0, "col"), ("up0", "ffn_up_w", 0, "col"), ("down0", "ffn_down_w", 0, "row"),
    ("gate1", "ffn_gate_w", 1, "col"), ("up1", "ffn_up_w", 1, "col"), ("down1", "ffn_down_w", 1, "row"),
    ("in_w", "in_w_even", 0, "col"),
]


def _to_wire(a, kind):
    return a.T if kind == "col" else a


_GATHER_GROUPS = [["in_w"], ["out_w"], ["gate0", "up0", "down0"], ["qkv_w", "o_w"], ["gate1", "up1", "down1"]]
_GRAD_GROUPS = [["gate1", "up1", "down1"], ["qkv_w", "o_w"], ["out_w", "gate0", "up0", "down0"], ["in_w"]]

_REPLICATED = ["ada_b", "norm_mix_w", "norm_ffn_w", "conv_b", "dt_bias", "a_log", "d_skip", "ssm_norm_w", "gmlp_ln_w",
               "gmlp_ln_b", "gmlp_ws", "gmlp_bs", "sinks", "rel_table", "final_norm_w"]
_TINY_SHARDED = ["conv_w", "qkv_b", "o_b"]

_WEIGHTS = ['ada_w', 'ada_b', 'norm_mix_w', 'norm_ffn_w', 'in_w_even', 'conv_w', 'conv_b', 'dt_bias', 'a_log', 'd_skip',
            'ssm_norm_w', 'gmlp_ln_w', 'gmlp_ln_b', 'gmlp_ws', 'gmlp_bs', 'out_w_even', 'qkv_w', 'qkv_b', 'o_w', 'o_b',
            'sinks', 'rel_table', 'ffn_gate_w', 'ffn_up_w', 'ffn_down_w', 'final_norm_w']


def _step(x, c, loss_target, W, M, V):
    T = x.shape[1]
    x0 = x[0]
    target = loss_target[0]
    me = 4 * lax.axis_index("x") + 2 * lax.axis_index("y") + lax.axis_index("c")

    w_wire_local = {key: _to_wire(W[name][layer].astype(BF16), kind) for key, name, layer, kind in _BIG}

    def wire_pack(keys, mult):
        gp = _Pack(D, 1, mult)
        for key in keys:
            gp.add(key, w_wire_local[key].shape)
        return gp

    gather_packs = [(gp, gp.pack(w_wire_local, BF16)) for gp in (wire_pack(keys, 16) for keys in _GATHER_GROUPS)]
    grad_packs = [wire_pack(keys, 64) for keys in _GRAD_GROUPS]
    full = {}

    def gathered_weights(group, gathered):
        shards = gather_packs[group][0].unpack(gathered, lead=(N_DEV,))
        full.update({key: a.reshape(-1, D) for key, a in shards.items()})


    small_in = _Pack(D, 8, 8)
    small_in.add("c", (1, D))
    small_in.add("conv_w", W["conv_w"][0].shape)
    small_in.add("qkv_b", W["qkv_b"][0].shape)
    small_in.add("o_b", W["o_b"][0].shape)
    sm = small_in.unpack(_all_gather(small_in.pack(
        dict(c=c, conv_w=W["conv_w"][0], qkv_b=W["qkv_b"][0], o_b=W["o_b"][0]), F32), "gather_small"), lead=(N_DEV,))
    c_all = sm["c"].reshape(N_DEV, D)
    conv_w_full = jnp.transpose(sm["conv_w"], (1, 0, 2)).reshape(SSM_CONV, CONV_DIM)
    qkv_b_full = sm["qkv_b"].reshape(1, QKV_DIM)
    o_b_full = sm["o_b"].reshape(1, D)

    ncol = W["ada_w"].shape[2]
    cond, mod_cols = _mod_matmul(c_all, W["ada_w"], "mod_matmul")
    mod_g = _all_gather(mod_cols.reshape(DEPTH * N_DEV, ncol), "gather_mod").reshape(N_DEV, DEPTH, N_DEV, ncol)
    mod_me = lax.dynamic_index_in_dim(mod_g, me, axis=2, keepdims=False)
    mod_me = jnp.transpose(mod_me, (1, 0, 2)).reshape(DEPTH, 6, D)
    mod_me = jnp.pad(mod_me, ((0, 0), (0, 2), (0, 0))).reshape(DEPTH * 8, D)
    ada_b_rows = jnp.pad(W["ada_b"].reshape(DEPTH, 6, D), ((0, 0), (0, 2), (0, 0))).reshape(DEPTH * 8, D)
    mod_all = _add_rows(mod_me, ada_b_rows, "mod_bias").reshape(DEPTH, 8, D)
    mod0, mod1 = mod_all[0], mod_all[1]
    h0, gathered_in = _norm_mod(x0, W["norm_mix_w"][0:1], mod0[1:2], mod0[0:1], "mix_norm_0",
                                side=(_GatherOps, gather_packs[0][1]))
    gathered_weights(0, gathered_in)

    in_t = full["in_w"]
    o1, o2, o3, o4 = SSM_INNER, SSM_INNER + CONV_DIM, SSM_INNER + CONV_DIM + SSM_HEADS, SSM_INNER + CONV_DIM + SSM_HEADS + GMLP_INNER
    w_z, w_xbc, w_dt, w_u, w_v = in_t[:o1], in_t[o1:o2], in_t[o2:o3], in_t[o3:o4], in_t[o4:]
    w_main = jnp.concatenate([w_z, w_u, w_v, w_xbc], axis=0)
    w_dtp = jnp.pad(w_dt, ((0, LANES - SSM_HEADS), (0, 0)))

    pad16 = lambda a: jnp.pad(a.reshape(1, SSM_HEADS), ((0, 0), (0, LANES - SSM_HEADS)))
    dtb, alog = pad16(W["dt_bias"][0]), pad16(W["a_log"][0])
    dskip = jnp.repeat(W["d_skip"][0], SSM_HEAD_DIM).reshape(1, SSM_INNER)
    ssm_nw = W["ssm_norm_w"]
    lnw, lnb = W["gmlp_ln_w"], W["gmlp_ln_b"]
    ws = W["gmlp_ws"][0]
    bs_exp = jnp.repeat(W["gmlp_bs"][0].T, CHUNK, axis=1)
    conv_b = W["conv_b"]
    nmw, nfw = W["norm_mix_w"], W["norm_ffn_w"]
    onehot_t = _bucket_onehot_t()
    head_ind = _head_indicator()
    bias = _bias_from_table(W["rel_table"].T, onehot_t, _window_mask_flat(), "rel_bias").reshape(ATTN_HEADS, CHUNK, 2 * CHUNK)
    sinks = W["sinks"][0]

    pm, gathered_a = _mm([(h0, w_main)], "nt", name="in_proj", tn_pref=1536, out_dtype=BF16,
                         side=(_GatherOps, gather_packs[1][1]))
    gathered_weights(1, gathered_a)
    out_w = full["out_w"]
    dtr = _mm([(h0, w_dtp)], "nt", name="in_proj_dt")
    cpre = _conv_fwd(pm, conv_w_full, conv_b, "conv_fwd")
    ya, ypre, sprev, gathered_b = _ssd_fwd(cpre, dtr, pm, dtb, alog, dskip, ssm_nw, head_ind, "ssd_fwd",
                                           side=(_GatherOps, gather_packs[2][1]))
    gathered_weights(2, gathered_b)
    yb, gathered_c = _gmlp_fwd(pm, lnw, lnb, ws, bs_exp, "gmlp_fwd", side=(_GatherOps, gather_packs[3][1]))
    gathered_weights(3, gathered_c)
    x1, mix0, hf0 = _mm_resid([(ya, out_w[:SSM_INNER]), (yb, out_w[SSM_INNER:])], x0, mod0[2:3], name="out_proj",
                              norm=(nfw[0:1], mod0[4:5], mod0[3:4]))
    x2, ffn0, h1, gathered_d = _ffn_fwd(x1, hf0, mod0, full["gate0"], full["up0"], full["down0"], "0",
                                        next_norm=(nmw[1:2], mod1[1:2], mod1[0:1]), gather=gather_packs[4][1])
    gathered_weights(4, gathered_d)
    qkv_t, o_w = full["qkv_w"], full["o_w"]
    w_q, w_k, w_v_att = qkv_t[:D], qkv_t[D:D + LANES], qkv_t[D + LANES:]

    qkv = _mm([(h1, qkv_t)], "nt", name="qkv_proj", bias=qkv_b_full, tn_pref=1280, out_dtype=BF16)
    att = _attn_fwd(qkv, bias, sinks, "attn_fwd")
    x3, mix1, hf1 = _mm_resid([(att, o_w)], x2, mod1[2:3], name="o_proj", bias=o_b_full,
                              norm=(nfw[1:2], mod1[4:5], mod1[3:4]))
    x4, ffn1 = _ffn_fwd(x3, hf1, mod1, full["gate1"], full["up1"], full["down1"], "1")

    dx4, dffn1, acc_f = _final_loss(x4, W["final_norm_w"].reshape(1, D), target, ffn1["out"], mod1[5:6], "final_loss")
    dx3, dmix1, gf1 = _ffn_bwd(dx4, dffn1, x3, ffn1, mod1, nfw[1:2], full["gate1"], full["up1"], full["down1"],
                               (mix1, mod1[2:3]), "1")
    g_wire = dict(gate1=gf1["d_wg"], up1=gf1["d_wu"], down1=gf1["d_wd"])

    def packed_partials(group):
        return grad_packs[group].pack({key: g_wire[key].reshape(N_DEV, -1, D) for key in _GRAD_GROUPS[group]},
                                      F32, lead=(N_DEV,))

    from_chips = {}
    partials_ffn1 = packed_partials(0)
    datt, theirs_ffn1 = _mm([(dmix1, o_w)], "nt", name="o_proj_dx", out_dtype=BF16, side=(_SiblingOps, partials_ffn1))
    pair_ffn1 = _pair_sum(partials_ffn1, theirs_ffn1, "grads_pair_sum_ffn1")
    d_o_w = _mm_tn(att, dmix1, name="o_proj_dw")
    dq, dk, dv, d_qkv_b, dbias, dsinks, from_chips[0] = _attn_bwd(qkv, datt, bias, sinks, "attn_bwd",
                                                                 side=(_ChipsOps, pair_ffn1))
    d_table = _table_from_dbias(dbias.reshape(ATTN_HEADS, -1), onehot_t, "rel_table_grad")
    d_qkv_t = jnp.concatenate([_mm_tn(dq, h1, name="qkv_dw_q"), _mm_tn(dk, h1, name="qkv_dw_k"), _mm_tn(dv, h1, name="qkv_dw_v")], axis=0)
    g_wire.update(qkv_w=d_qkv_t, o_w=d_o_w)
    partials_l1 = packed_partials(1)
    dh1, theirs_l1 = _mm([(dq, w_q), (dk, w_k), (dv, w_v_att)], "nn", name="qkv_proj_dx", side=(_SiblingOps, partials_l1))
    pair_l1 = _pair_sum(partials_l1, theirs_l1, "grads_pair_sum_l1")
    dx2, dffn0, acc_n1 = _norm_mod_bwd(x2, dh1, dx3, nmw[1:2], mod1[1:2], ffn0["out"], mod0[5:6], "mix_norm_bwd_1")

    dx1, dmix0, gf0, from_chips[1] = _ffn_bwd(dx2, dffn0, x1, ffn0, mod0, nfw[0:1], full["gate0"], full["up0"], full["down0"],
                                              (mix0, mod0[2:3]), "0", exchange=pair_l1)

    dya = _mm([(dmix0, out_w[:SSM_INNER])], "nt", name="out_proj_dx_a")
    dyb = _mm([(dmix0, out_w[SSM_INNER:])], "nt", name="out_proj_dx_b")
    d_out_w = jnp.concatenate([_mm_tn(ya, dmix0, name="out_proj_dw_a"), _mm_tn(yb, dmix0, name="out_proj_dw_b")], axis=0)
    g_wire.update(gate0=gf0["d_wg"], up0=gf0["d_wu"], down0=gf0["d_wd"], out_w=d_out_w)
    partials_ffn0 = packed_partials(2)
    du, dvg, d_ws, d_bs, acc_ln, theirs_ffn0 = _gmlp_bwd(pm, dyb, lnw, lnb, ws, bs_exp, "gmlp_bwd",
                                                         side=(_SiblingOps, partials_ffn0))
    pair_ffn0 = _pair_sum(partials_ffn0, theirs_ffn0, "grads_pair_sum_mix")
    dz, dcpre, ddtr, acc_ssd, d_ssm_nw, from_chips[2] = _ssd_bwd(
        cpre, dtr, pm, ypre, sprev, dya, dtb, alog, dskip, ssm_nw, head_ind, "ssd_bwd", side=(_ChipsOps, pair_ffn0))
    acc_f0, acc_f1 = gf0["acc"], gf1["acc"]
    row = lambda i: slice(i, i + 1)
    views = dict(ada_b=(DEPTH * 6, D), norm_mix_w=(DEPTH, D), norm_ffn_w=(DEPTH, D), conv_b=(1, CONV_DIM),
                 dt_bias=(1, SSM_HEADS), a_log=(1, SSM_HEADS), d_skip=(1, SSM_HEADS), ssm_norm_w=(1, D), gmlp_ln_w=(1, D),
                 gmlp_ln_b=(1, D), gmlp_ws=(GMLP_GROUPS * CHUNK, CHUNK), gmlp_bs=(GMLP_GROUPS, CHUNK),
                 sinks=(1, ATTN_HEADS), rel_table=(REL_BUCKETS, ATTN_HEADS), final_norm_w=(1, D),
                 conv_w=(SSM_CONV, CONV_DIM), qkv_b=(1, QKV_DIM), o_b=(1, D), loss=(1, D))
    late_keys = ["ada_b", "norm_mix_w", "conv_b", "dt_bias", "a_log", "d_skip", "ssm_norm_w", "conv_w"]
    canvas_early, canvas_late = _Canvas(), _Canvas()
    for key, (r, cdim) in views.items():
        (canvas_late if key in late_keys else canvas_early).add(key, r, cdim, blocks=GMLP_GROUPS if key == "gmlp_ws" else 1)
    early_sources = [
        (acc_f0, [("norm_ffn_w", row(2), 0)]),
        (acc_f1, [("norm_ffn_w", row(2), 1), ("o_b", row(4), 0)]),
        (acc_f, [("final_norm_w", row(0), 0), ("loss", row(1), 0)]),
        (acc_ln, [("gmlp_ln_w", row(0), 0), ("gmlp_ln_b", row(1), 0)]),
        (d_ws.reshape(GMLP_GROUPS * CHUNK, CHUNK), [("gmlp_ws", slice(0, GMLP_GROUPS * CHUNK), 0)]),
        (d_bs.T, [("gmlp_bs", slice(0, GMLP_GROUPS), 0)]),
        (dsinks, [("sinks", row(0), 0)]),
        (d_table, [("rel_table", slice(0, REL_BUCKETS), 0)]),
        (d_qkv_b, [("qkv_b", row(0), 0)]),
    ]
    dxbc, acc_conv, parts_early = _conv_bwd(dcpre, pm, conv_w_full, "conv_bwd",
                                            side=(_GatherOps, _canvas_fill(canvas_early, early_sources, "small_grads_early")))
    d_in_t = jnp.concatenate([
        _mm_tn(dz, h0, name="in_dw_z"), _mm_tn(dxbc, h0, name="in_dw_xbc"),
        _mm_tn(ddtr, h0, name="in_dw_dt")[:SSM_HEADS], _mm_tn(du, h0, name="in_dw_u"), _mm_tn(dvg, h0, name="in_dw_v")], axis=0)
    g_wire.update(in_w=d_in_t)
    partials_mix = packed_partials(3)
    theirs_mix = _comm_call(_SiblingOps, partials_mix, "exchange_grads_sibling")
    pair_mix = _pair_sum(partials_mix, theirs_mix, "grads_pair_sum")
    dh0, from_chips[3] = _mm([(dz, w_z), (dxbc, w_xbc), (ddtr, w_dtp), (du, w_u), (dvg, w_v)], "nn", name="in_proj_dx",
                             side=(_ChipsOps, pair_mix))
    grad_x, acc_n0 = _norm_mod_bwd(x0, dh0, dx1, nmw[0:1], mod0[1:2], None, None, "mix_norm_bwd_0")

    g_mine = {}
    for group in range(len(_GRAD_GROUPS)):
        g_mine.update(grad_packs[group].unpack(_sum_parts(from_chips[group], f"grads_chip_sum_{group}")))
    res_big = [{}, {}, {}, {}]
    for key, name, layer, kind in _BIG:
        g_nat = _to_wire(g_mine[key], kind)
        outs = _adamw(g_nat[None], W[name][layer], M[name][layer], V[name][layer], f"adamw_{key}")
        for res, out in zip(res_big, outs):
            res[key] = out

    late_sources = [
        (acc_n0, [("ada_b", row(1), 0), ("ada_b", row(0), 1), ("norm_mix_w", row(2), 0)]),
        (acc_f0, [("ada_b", row(3), 2), ("ada_b", row(1), 3), ("ada_b", row(0), 4)]),
        (acc_n1, [("ada_b", row(3), 5), ("ada_b", row(1), 6), ("ada_b", row(0), 7), ("norm_mix_w", row(2), 1)]),
        (acc_f1, [("ada_b", row(3), 8), ("ada_b", row(1), 9), ("ada_b", row(0), 10)]),
        (acc_f, [("ada_b", row(3), 11)]),
        (acc_conv, [("conv_w", slice(0, SSM_CONV), 0), ("conv_b", row(4), 0)]),
        (acc_ssd, [("dt_bias", row(0), 0), ("a_log", row(1), 0), ("d_skip", row(2), 0)]),
        (d_ssm_nw, [("ssm_norm_w", row(0), 0)]),
    ]
    parts_small = _all_gather(_canvas_fill(canvas_late, late_sources, "small_grads_late"), "gather_small_grads")
    view = lambda a, key: a.reshape(views[key])
    params = lambda keys: [(n, view(W[n], n), view(M[n], n), view(V[n], n)) for n in _REPLICATED if (n in late_keys) == keys]
    small_out = _adamw_canvas(canvas_early, parts_early, params(False), ["qkv_b", "o_b", "loss"], "adamw_small_early")
    small_out.update(_adamw_canvas(canvas_late, parts_small, params(True), ["conv_w"], "adamw_small_late"))
    loss = small_out["loss"][0][0, 0]
    res_small = [{name: small_out[name][k].reshape(W[name].shape) for name in _REPLICATED} for k in range(4)]

    n_cw, n_qb, n_ob = W["conv_w"].shape[2], W["qkv_b"].shape[1], W["o_b"].shape[1]
    g_tiny = dict(conv_w=lax.dynamic_slice_in_dim(small_out["conv_w"][0], me * n_cw, n_cw, axis=1)[None],
                  qkv_b=lax.dynamic_slice_in_dim(small_out["qkv_b"][0], me * n_qb, n_qb, axis=1),
                  o_b=lax.dynamic_slice_in_dim(small_out["o_b"][0], me * n_ob, n_ob, axis=1))
    tiny = _Pack(D, 8, 8)
    for name in _TINY_SHARDED:
        tiny.add(name, W[name].shape)
    pkt = lambda S: tiny.pack({name: S[name] for name in _TINY_SHARDED}, F32)
    res_tiny = [tiny.unpack(r) for r in _adamw(pkt(g_tiny)[None], pkt(W), pkt(M), pkt(V), "adamw_tiny")]

    dmod_all = parts_small[:, canvas_late.offset["ada_b"]:canvas_late.offset["ada_b"] + DEPTH * 6].reshape(N_DEV, DEPTH, 6 * D)
    dmod_cols = jnp.transpose(lax.dynamic_slice_in_dim(dmod_all, me * ncol, ncol, axis=2), (1, 0, 2))
    g_ada_w = _ada_w_grad(cond, dmod_cols, "ada_w_grad")
    flat = lambda a: a.reshape(DEPTH * D, ncol)
    res_ada = [r.reshape(DEPTH, D, ncol) for r in _adamw(flat(g_ada_w)[None], flat(W["ada_w"]), flat(M["ada_w"]), flat(V["ada_w"]), "adamw_ada_w")]

    def result(kind_idx, name):
        if name == "ada_w":
            return res_ada[kind_idx]
        if name in _REPLICATED:
            return res_small[kind_idx][name]
        if name in _TINY_SHARDED:
            return res_tiny[kind_idx][name]
        pieces = [res_big[kind_idx][key] for key, nm, layer, kind in _BIG if nm == name]
        return jnp.stack(pieces)

    outs = [loss, grad_x[None]]
    for kind_idx in range(4):
        outs += [result(kind_idx, name) for name in _WEIGHTS]
    return tuple(outs)


def kernel(x, c, ada_w, ada_b, norm_mix_w, norm_ffn_w, in_w_even, conv_w, conv_b, dt_bias, a_log, d_skip, ssm_norm_w, gmlp_ln_w, gmlp_ln_b, gmlp_ws, gmlp_bs, out_w_even, qkv_w, qkv_b, o_w, o_b, sinks, rel_table, ffn_gate_w, ffn_up_w, ffn_down_w, final_norm_w, loss_target, m_ada_w, m_ada_b, m_norm_mix_w, m_norm_ffn_w, m_in_w_even, m_conv_w, m_conv_b, m_dt_bias, m_a_log, m_d_skip, m_ssm_norm_w, m_gmlp_ln_w, m_gmlp_ln_b, m_gmlp_ws, m_gmlp_bs, m_out_w_even, m_qkv_w, m_qkv_b, m_o_w, m_o_b, m_sinks, m_rel_table, m_ffn_gate_w, m_ffn_up_w, m_ffn_down_w, m_final_norm_w, v_ada_w, v_ada_b, v_norm_mix_w, v_norm_ffn_w, v_in_w_even, v_conv_w, v_conv_b, v_dt_bias, v_a_log, v_d_skip, v_ssm_norm_w, v_gmlp_ln_w, v_gmlp_ln_b, v_gmlp_ws, v_gmlp_bs, v_out_w_even, v_qkv_w, v_qkv_b, v_o_w, v_o_b, v_sinks, v_rel_table, v_ffn_gate_w, v_ffn_up_w, v_ffn_down_w, v_final_norm_w):
    args = locals()
    W = {n: args[n] for n in _WEIGHTS}
    M = {n: args["m_" + n] for n in _WEIGHTS}
    V = {n: args["v_" + n] for n in _WEIGHTS}
    return _step(x, c, loss_target, W, M, V)
```

```python
import functools
import math

import numpy as np
import jax
import jax.numpy as jnp
from jax import lax
from jax.experimental import pallas as pl
from jax.experimental.pallas import tpu as pltpu

F32 = jnp.float32
BF16 = jnp.bfloat16
HIGHEST = lax.Precision.HIGHEST
MESH = pl.DeviceIdType.MESH

N_DEV = 8
D = 1024
DEPTH = 2
SSM_HEADS = 16
SSM_HEAD_DIM = 64
SSM_INNER = 1024
SSM_GROUPS = 2
SSM_STATE = 128
SSM_CONV = 4
CHUNK = 128
CONV_DIM = SSM_INNER + 2 * SSM_GROUPS * SSM_STATE
GMLP_GROUPS = 8
GMLP_INNER = 1024
IN_EVEN = 4624
ATTN_HEADS = 16
ATTN_KV = 2
ATTN_DH = 64
QKV_DIM = 1280
REL_BUCKETS = 32
REL_MAX_DIST = 128
FFN = 2816
EPS = 1e-6
NEG_INF = -1e30
LANES = 128

ADAM_LR = 0.001
ADAM_B1 = 0.9
ADAM_B2 = 0.999
ADAM_EPS = 1e-08
ADAM_WD = 0.01
ADAM_STEP = 10

VMEM_LIMIT_BYTES = 56 * 1024 * 1024
ROW_TILE = 512


def _pcall(body, *, name, out_shape, grid=(), in_specs=None, out_specs=None, scratch=(), sem=None):
    params = dict(vmem_limit_bytes=VMEM_LIMIT_BYTES)
    if sem is not None:
        params["dimension_semantics"] = sem
    specs = {} if in_specs is None else dict(in_specs=in_specs, out_specs=out_specs)
    return pl.pallas_call(
        body, name=name, out_shape=out_shape, grid=grid, **specs,
        scratch_shapes=list(scratch), compiler_params=pltpu.CompilerParams(**params))


def _call(body, args, side=None, *, name, out_shape, grid, in_specs, out_specs, scratch=(), sem=None):
    if side is None:
        return _pcall(body, name=name, out_shape=out_shape, grid=grid, in_specs=in_specs, out_specs=out_specs,
                      scratch=scratch, sem=sem)(*args)
    ops_cls, x = side
    n_in, n_out, n_scr = len(in_specs), len(out_shape), len(scratch)
    steps = int(np.prod(grid))
    hbm = pl.BlockSpec(memory_space=pl.ANY)

    def wrapped(*refs):
        ins, x_ref = refs[:n_in], refs[n_in]
        outs, r_ref = refs[n_in + 1:n_in + 1 + n_out], refs[n_in + 1 + n_out]
        scr, sems = refs[n_in + 2 + n_out:n_in + 2 + n_out + n_scr], refs[n_in + 2 + n_out + n_scr:]
        ops = ops_cls(x_ref, r_ref, *sems)
        step = pl.program_id(0)
        for axis in range(1, len(grid)):
            step = step * grid[axis] + pl.program_id(axis)
        pl.when(step == 0)(ops.start)
        body(*ins, *outs, *scr)
        pl.when(step == (3 * steps) // 4)(ops.forward)
        pl.when(step == steps - 1)(ops.finish)

    return _pcall(
        wrapped, name=name, out_shape=list(out_shape) + [ops_cls.result(x)], grid=grid,
        in_specs=list(in_specs) + [hbm], out_specs=list(out_specs) + [hbm],
        scratch=list(scratch) + ops_cls.scratch(), sem=("arbitrary",) * len(grid))(*args, x)


def _tile(n, pref):
    if n <= pref:
        return n
    best = None
    for t in range(LANES, pref + 1, LANES):
        if n % t == 0:
            best = t
    assert best is not None, (n, pref)
    return best


def _rows(T):
    return min(ROW_TILE, T)


def _sds(shape, dtype=F32):
    return jax.ShapeDtypeStruct(shape, dtype)


def _row_spec(tm, c, col=0):
    return pl.BlockSpec((tm, c), lambda i, col=col: (i, col))


def _vec_spec(c, r=1):
    return pl.BlockSpec((r, c), lambda i: (0, 0))


def _sigmoid(x):
    return jax.nn.sigmoid(x)


def _silu(x):
    return x * _sigmoid(x)


def _dsilu(x):
    s = _sigmoid(x)
    return s * (1.0 + x * (1.0 - s))


def _silu_and_grad(x):
    s = _sigmoid(x)
    return x * s, s * (1.0 + x * (1.0 - s))


def _gelu(x):
    return 0.5 * x * (1.0 + lax.erf(x * 0.7071067811865476))


def _gelu_and_grad(x):
    phi = 0.5 * (1.0 + lax.erf(x * 0.7071067811865476))
    return x * phi, phi + x * jnp.exp(-0.5 * x * x) * 0.3989422804014327


def _dot(a, b, dims, precision=None):
    return lax.dot_general(a, b, (dims, ((), ())), precision=precision, preferred_element_type=F32)


def _nn(a, b, precision=None):
    return _dot(a, b, ((1,), (0,)), precision)


def _nt(a, b, precision=None):
    return _dot(a, b, ((1,), (1,)), precision)


def _tn(a, b, precision=None):
    return _dot(a, b, ((0,), (0,)), precision)


def _bf(x):
    return x.astype(BF16)


def _colsum(x):
    return jnp.sum(x, axis=0, keepdims=True)


def _rowsum(x):
    return jnp.sum(x, axis=1, keepdims=True)


def _allsum(x):
    return _colsum(_rowsum(x))


def _comm_call(ops_cls, x, name, from_vmem=False):
    def body(x_ref, out_ref, *sems):
        ops = ops_cls(x_ref, out_ref, *sems)
        ops.start()
        ops.forward()
        ops.finish()

    return pl.pallas_call(
        body, name=name, out_shape=ops_cls.result(x),
        in_specs=[pl.BlockSpec(memory_space=pltpu.VMEM if from_vmem else pl.ANY)],
        out_specs=pl.BlockSpec(memory_space=pl.ANY), scratch_shapes=ops_cls.scratch(),
    )(x)


def _all_gather(x, name):
    return _comm_call(_GatherOps, x, name, from_vmem=True)


class _GatherOps:
    def __init__(self, x_ref, out_ref, send_sems, recv_sems, local_sem):
        self.x_ref, self.out_ref = x_ref, out_ref
        self.send_sems, self.recv_sems, self.local_sem = send_sems, recv_sems, local_sem
        mx, my, mc = lax.axis_index("x"), lax.axis_index("y"), lax.axis_index("c")
        self.mc = mc
        self.me, self.sibling = (mx, my, mc), (mx, my, 1 - mc)
        self.chips = [(1 - mx, my), (mx, 1 - my), (1 - mx, 1 - my)]

    @staticmethod
    def result(x):
        return _sds((N_DEV,) + x.shape, x.dtype)

    @staticmethod
    def scratch():
        return [pltpu.SemaphoreType.DMA((7,)), pltpu.SemaphoreType.DMA((7,)), pltpu.SemaphoreType.DMA(())]

    def _slot(self, px, py, pc):
        return self.out_ref.at[4 * px + 2 * py + pc]

    def _copy(self, k, block, to, own=False):
        return pltpu.make_async_remote_copy(
            src_ref=self.x_ref if own else self._slot(*block), dst_ref=self._slot(*block),
            send_sem=self.send_sems.at[k], recv_sem=self.recv_sems.at[k], device_id=to, device_id_type=MESH)

    def _mine(self):
        return pltpu.make_async_copy(self.x_ref, self._slot(*self.me), self.local_sem)

    def _first(self):
        return [self._copy(0, self.me, self.sibling, own=True)] + [
            self._copy(1 + j, self.me, (*chip, self.mc), own=True) for j, chip in enumerate(self.chips)]

    def _passed(self):
        return [self._copy(4 + j, (*chip, self.mc), self.sibling) for j, chip in enumerate(self.chips)]

    def start(self):
        self._mine().start()
        for cp in self._first():
            cp.start()

    def forward(self):
        passed = self._passed()
        for j, chip in enumerate(self.chips):
            self._copy(1 + j, (*chip, self.mc), self.me).wait_recv()
            passed[j].start()

    def finish(self):
        self._copy(0, self.sibling, self.me).wait_recv()
        for j, chip in enumerate(self.chips):
            self._copy(4 + j, (*chip, 1 - self.mc), self.me).wait_recv()
        for cp in self._first() + self._passed():
            cp.wait_send()
        self._mine().wait()


N_CHIP = 4


class _SiblingOps:
    def __init__(self, p_ref, theirs_ref, send_sems, recv_sems):
        self.p_ref, self.theirs_ref, self.send_sems, self.recv_sems = p_ref, theirs_ref, send_sems, recv_sems

    @staticmethod
    def result(p):
        return _sds((N_CHIP,) + p.shape[1:], p.dtype)

    @staticmethod
    def scratch():
        return [pltpu.SemaphoreType.DMA((N_CHIP,))] * 2

    def _copies(self):
        mx, my, mc = lax.axis_index("x"), lax.axis_index("y"), lax.axis_index("c")
        return [pltpu.make_async_remote_copy(
            src_ref=self.p_ref.at[2 * chip + 1 - mc], dst_ref=self.theirs_ref.at[chip],
            send_sem=self.send_sems.at[chip], recv_sem=self.recv_sems.at[chip],
            device_id=(mx, my, 1 - mc), device_id_type=MESH) for chip in range(N_CHIP)]

    def start(self):
        for cp in self._copies():
            cp.start()

    def forward(self):
        pass

    def finish(self):
        for cp in self._copies():
            cp.wait()


class _ChipsOps:
    def __init__(self, q_ref, out_ref, send_sems, recv_sems, local_sem):
        self.q_ref, self.out_ref = q_ref, out_ref
        self.send_sems, self.recv_sems, self.local_sem = send_sems, recv_sems, local_sem

    @staticmethod
    def result(q):
        return _sds(q.shape, q.dtype)

    @staticmethod
    def scratch():
        return [pltpu.SemaphoreType.DMA((N_CHIP - 1,)), pltpu.SemaphoreType.DMA((N_CHIP - 1,)), pltpu.SemaphoreType.DMA(())]

    def _copies(self):
        mx, my, mc = lax.axis_index("x"), lax.axis_index("y"), lax.axis_index("c")
        me = 2 * mx + my
        local = pltpu.make_async_copy(self.q_ref.at[me], self.out_ref.at[me], self.local_sem)
        remote = []
        for r in range(1, N_CHIP):
            px = 1 - mx if r & 2 else mx
            py = 1 - my if r & 1 else my
            remote.append(pltpu.make_async_remote_copy(
                src_ref=self.q_ref.at[2 * px + py], dst_ref=self.out_ref.at[me],
                send_sem=self.send_sems.at[r - 1], recv_sem=self.recv_sems.at[r - 1],
                device_id=(px, py, mc), device_id_type=MESH))
        return local, remote

    def start(self):
        local, remote = self._copies()
        local.start()
        for cp in remote:
            cp.start()

    def forward(self):
        pass

    def finish(self):
        local, remote = self._copies()
        for cp in remote:
            cp.wait()
        local.wait()


def _pair_sum(p, theirs, name):
    n, R, C = theirs.shape
    tr = _tile_rows(R, 256)

    def body(p_ref, t_ref, o_ref):
        mc = lax.axis_index("c")
        o_ref[0] = (p_ref[0, mc] + t_ref[0]).astype(BF16)

    blk = pl.BlockSpec((1, tr, C), lambda s, i: (s, i, 0))
    return _pcall(body, name=name, out_shape=_sds((n, R, C), BF16), grid=(n, R // tr),
                  in_specs=[pl.BlockSpec((1, 2, tr, C), lambda s, i: (s, 0, i, 0)), blk],
                  out_specs=blk, sem=("parallel", "parallel"))(p.reshape(n, 2, R, C), theirs)


def _sum_parts(parts, name):
    P, R, C = parts.shape
    tr = _tile_rows(R, 256)

    def body(p_ref, o_ref):
        g = p_ref[0].astype(F32)
        for k in range(1, P):
            g = g + p_ref[k].astype(F32)
        o_ref[...] = g

    return _pcall(body, name=name, out_shape=_sds((R, C)), grid=(R // tr,),
                  in_specs=[pl.BlockSpec((P, tr, C), lambda i: (0, i, 0))],
                  out_specs=pl.BlockSpec((tr, C), lambda i: (i, 0)), sem=("parallel",))(parts)


def _adamw(parts, w, m, v, name):
    P, R, C = parts.shape
    tr = R if R <= 256 else _tile_rows(R, 256)

    def body(p_ref, w_ref, m_ref, v_ref, g_ref, d_ref, nm_ref, nv_ref):
        g = p_ref[0]
        for k in range(1, P):
            g = g + p_ref[k]
        nm = ADAM_B1 * m_ref[...] + (1.0 - ADAM_B1) * g
        nv = ADAM_B2 * v_ref[...] + (1.0 - ADAM_B2) * (g * g)
        m_hat = nm / (1.0 - ADAM_B1 ** ADAM_STEP)
        v_hat = nv / (1.0 - ADAM_B2 ** ADAM_STEP)
        g_ref[...] = g
        d_ref[...] = -ADAM_LR * (m_hat / (jnp.sqrt(v_hat) + ADAM_EPS) + ADAM_WD * w_ref[...])
        nm_ref[...] = nm
        nv_ref[...] = nv

    blk = pl.BlockSpec((tr, C), lambda i: (i, 0))
    return _pcall(
        body, name=name, out_shape=[_sds((R, C))] * 4, grid=(R // tr,),
        in_specs=[pl.BlockSpec((P, tr, C), lambda i: (0, i, 0)), blk, blk, blk],
        out_specs=[blk] * 4, sem=("parallel",))(parts, w, m, v)


def _adam_update(g, w, m, v):
    nm = ADAM_B1 * m + (1.0 - ADAM_B1) * g
    nv = ADAM_B2 * v + (1.0 - ADAM_B2) * (g * g)
    m_hat = nm / (1.0 - ADAM_B1 ** ADAM_STEP)
    v_hat = nv / (1.0 - ADAM_B2 ** ADAM_STEP)
    return -ADAM_LR * (m_hat / (jnp.sqrt(v_hat) + ADAM_EPS) + ADAM_WD * w), nm, nv


class _Canvas:
    def __init__(self):
        self.offset, self.views, self.rows = {}, {}, 0

    def add(self, key, r, c, blocks=1):
        need = r // blocks if blocks > 1 else r * (-(-c // D))
        if need >= 8:
            self.rows = -(-self.rows // 8) * 8
        self.offset[key], self.views[key] = self.rows, (r, c, blocks)
        self.rows += need

    @property
    def total(self):
        return -(-self.rows // 8) * 8

    def cells(self, key):
        (r, c, blocks), off = self.views[key], self.offset[key]
        if blocks > 1:
            n = r // blocks
            return [(off, n, b * c, c, slice(b * n, (b + 1) * n), slice(0, c)) for b in range(blocks)]
        if c <= D:
            return [(off, r, 0, c, slice(0, r), slice(0, c))]
        chunks = -(-c // D)
        return [(off + i * chunks + j, 1, 0, min(D, c - j * D), slice(i, i + 1), slice(j * D, min(c, (j + 1) * D)))
                for i in range(r) for j in range(chunks)]


def _canvas_fill(canvas, sources, name):
    arrays = [a for a, _ in sources]

    def body(*refs):
        out_ref = refs[-1]
        out_ref[...] = jnp.zeros_like(out_ref)
        for ref, (_, items) in zip(refs[:-1], sources):
            for key, src_rows, view_row in items:
                n = src_rows.stop - src_rows.start
                for row, count, lane, width, vrows, vcols in canvas.cells(key):
                    lo, hi = max(vrows.start, view_row), min(vrows.stop, view_row + n)
                    if lo < hi:
                        src = slice(src_rows.start + lo - view_row, src_rows.start + hi - view_row)
                        out_ref[row + lo - vrows.start:row + hi - vrows.start, lane:lane + width] = ref[src, vcols]

    return _pcall(body, name=name, out_shape=_sds((canvas.total, D)))(*arrays)


def _adamw_canvas(canvas, parts, params, sum_only, name):
    n = len(params)

    def body(p_ref, *refs):
        sum_ref = refs[-1]
        g_all = p_ref[0]
        for k in range(1, N_DEV):
            g_all = g_all + p_ref[k]
        sum_ref[...] = g_all
        for i, (key, _, _, _) in enumerate(params):
            w_ref, m_ref, v_ref = refs[3 * i:3 * i + 3]
            outs = refs[3 * n + 4 * i:3 * n + 4 * i + 4]
            for row, count, lane, width, vrows, vcols in canvas.cells(key):
                g = sum_ref[row:row + count, lane:lane + width]
                delta, nm, nv = _adam_update(g, w_ref[vrows, vcols], m_ref[vrows, vcols], v_ref[vrows, vcols])
                for ref, val in zip(outs, (g, delta, nm, nv)):
                    ref[vrows, vcols] = val
        for i, key in enumerate(sum_only):
            for row, count, lane, width, vrows, vcols in canvas.cells(key):
                refs[7 * n + i][vrows, vcols] = sum_ref[row:row + count, lane:lane + width]

    args = [a for _, w, m, v in params for a in (w, m, v)]
    out_shape = [_sds(w.shape) for _, w, _, _ in params for _ in range(4)] + [_sds(canvas.views[k][:2]) for k in sum_only]
    res = _pcall(body, name=name, out_shape=out_shape, scratch=[pltpu.VMEM(parts.shape[1:], F32)])(parts, *args)
    out = {key: res[4 * i:4 * i + 4] for i, (key, _, _, _) in enumerate(params)}
    out.update({key: [res[4 * n + i]] for i, key in enumerate(sum_only)})
    return out


def _tile_rows(n, pref):
    best = None
    for t in range(8, pref + 1, 8):
        if n % t == 0:
            best = t
    assert best is not None, (n, pref)
    return best


def _mm(pairs, mode, *, name, out_dtype=F32, bias=None, tn_pref=1024, side=None):
    M = pairs[0][0].shape[0]
    N = pairs[0][1].shape[1] if mode == "nn" else pairs[0][1].shape[0]
    tm, tn = _rows(M), _tile(N, tn_pref)
    n_pairs = len(pairs)
    has_bias = bias is not None

    def body(*refs):
        acc = _pairs_dot(refs[:2 * n_pairs], mode)
        if has_bias:
            acc = acc + refs[2 * n_pairs][...]
        refs[-1][...] = acc.astype(refs[-1].dtype)

    in_specs, args = _pair_specs(pairs, mode, tm, tn)
    if has_bias:
        in_specs.append(pl.BlockSpec((1, tn), lambda j, i: (0, j)))
        args.append(bias)
    res = _call(body, args, side, name=name, out_shape=[_sds((M, N), out_dtype)], grid=(N // tn, M // tm),
                in_specs=in_specs, out_specs=[pl.BlockSpec((tm, tn), lambda j, i: (i, j))], sem=("parallel", "parallel"))
    return res[0] if side is None else (res[0], res[1])


def _pairs_dot(ab, mode):
    acc = None
    for p in range(len(ab) // 2):
        a, b = _bf(ab[2 * p][...]), _bf(ab[2 * p + 1][...])
        d = _nn(a, b) if mode == "nn" else _nt(a, b)
        acc = d if acc is None else acc + d
    return acc


def _pair_specs(pairs, mode, tm, tn):
    in_specs, args = [], []
    for a, b in pairs:
        K = a.shape[1]
        in_specs.append(pl.BlockSpec((tm, K), lambda j, i: (i, 0)))
        if mode == "nn":
            in_specs.append(pl.BlockSpec((K, tn), lambda j, i: (0, j)))
        else:
            in_specs.append(pl.BlockSpec((tn, K), lambda j, i: (j, 0)))
        args += [a, b]
    return in_specs, args


def _mm_resid(pairs, resid, gvec, *, name, bias=None, norm=None):
    M, N = resid.shape
    tm = _rows(M)
    n_pairs = len(pairs)
    has_bias, has_norm = bias is not None, norm is not None

    def body(*refs):
        acc = _pairs_dot(refs[:2 * n_pairs], "nn")
        pos = 2 * n_pairs
        if has_bias:
            acc = acc + refs[pos][...]
            pos += 1
        xv = refs[pos][...] + refs[pos + 1][...] * acc
        outs = refs[pos + 2 + 3 * has_norm:]
        outs[0][...] = xv
        outs[1][...] = acc.astype(BF16)
        if has_norm:
            w_ref, sc_ref, sh_ref = refs[pos + 2:pos + 5]
            r = lax.rsqrt(jnp.mean(xv * xv, axis=-1, keepdims=True) + EPS)
            outs[2][...] = ((xv * r * w_ref[...]) * (1.0 + sc_ref[...]) + sh_ref[...]).astype(BF16)

    in_specs, args = _pair_specs(pairs, "nn", tm, N)
    vec = pl.BlockSpec((1, N), lambda j, i: (0, 0))
    row = pl.BlockSpec((tm, N), lambda j, i: (i, 0))
    if has_bias:
        in_specs.append(vec)
        args.append(bias)
    in_specs += [row, vec] + [vec] * (3 * has_norm)
    args += [resid, gvec] + (list(norm) if has_norm else [])
    return _pcall(body, name=name, out_shape=[_sds((M, N)), _sds((M, N), BF16)] + [_sds((M, N), BF16)] * has_norm,
                  grid=(1, M // tm), in_specs=in_specs, out_specs=[row] * (2 + has_norm),
                  sem=("parallel", "parallel"))(*args)


def _mm_tn(a, b, *, name, tm_pref=1408, tn_pref=1536):
    K, M = a.shape
    N = b.shape[1]
    tm, tn = _tile(M, tm_pref), _tile(N, tn_pref)
    tk = K if K <= 4 * ROW_TILE else 4 * ROW_TILE

    def body(a_ref, b_ref, o_ref):
        @pl.when(pl.program_id(2) == 0)
        def _():
            o_ref[...] = jnp.zeros_like(o_ref)
        o_ref[...] += _tn(_bf(a_ref[...]), _bf(b_ref[...]))

    return _pcall(
        body, name=name, out_shape=_sds((M, N)), grid=(M // tm, N // tn, K // tk),
        in_specs=[pl.BlockSpec((tk, tm), lambda i, j, k: (k, i)), pl.BlockSpec((tk, tn), lambda i, j, k: (k, j))],
        out_specs=pl.BlockSpec((tm, tn), lambda i, j, k: (i, j)),
        sem=("parallel", "parallel", "arbitrary"))(a, b)


def _mm_swiglu(h, wg_t, wu_t, name, side=None):
    M, K = h.shape
    N = wg_t.shape[0]
    tm, tn = _rows(M), _tile(N, 1408)

    def body(h_ref, wg_ref, wu_ref, gate_ref, up_ref, act_ref):
        hv = _bf(h_ref[...])
        gate = _nt(hv, wg_ref[...])
        up = _nt(hv, wu_ref[...])
        gate_ref[...] = gate.astype(BF16)
        up_ref[...] = up.astype(BF16)
        act_ref[...] = (_silu(gate) * up).astype(BF16)

    w_spec = pl.BlockSpec((tn, K), lambda j, i: (j, 0))
    o_spec = pl.BlockSpec((tm, tn), lambda j, i: (i, j))
    return _call(body, (h, wg_t, wu_t), side, name=name,
                 out_shape=[_sds((M, N), BF16)] * 3, grid=(N // tn, M // tm),
                 in_specs=[pl.BlockSpec((tm, K), lambda j, i: (i, 0)), w_spec, w_spec], out_specs=[o_spec] * 3,
                 sem=("parallel", "parallel"))


def _mm_swiglu_bwd(dout, wd, gate, up, name, side=None):
    M, K = dout.shape
    N = wd.shape[0]
    tm, tn = _rows(M), _tile(N, 1408)

    def body(d_ref, wd_ref, gate_ref, up_ref, dg_ref, du_ref):
        dact = _nt(_bf(d_ref[...]), wd_ref[...])
        act, dact_dg = _silu_and_grad(gate_ref[...].astype(F32))
        dg_ref[...] = (dact * up_ref[...].astype(F32) * dact_dg).astype(BF16)
        du_ref[...] = (dact * act).astype(BF16)

    t_spec = pl.BlockSpec((tm, tn), lambda j, i: (i, j))
    return _call(
        body, (dout, wd, gate, up), side, name=name, out_shape=[_sds((M, N), BF16)] * 2, grid=(N // tn, M // tm),
        in_specs=[pl.BlockSpec((tm, K), lambda j, i: (i, 0)), pl.BlockSpec((tn, K), lambda j, i: (j, 0)), t_spec, t_spec],
        out_specs=[t_spec] * 2, sem=("parallel", "parallel"))


def _norm_mod(x, w, sc, sh, name, side=None):
    T = x.shape[0]
    tm = _rows(T)

    def body(x_ref, w_ref, sc_ref, sh_ref, o_ref):
        xv = x_ref[...]
        r = lax.rsqrt(jnp.mean(xv * xv, axis=-1, keepdims=True) + EPS)
        o_ref[...] = ((xv * r * w_ref[...]) * (1.0 + sc_ref[...]) + sh_ref[...]).astype(BF16)

    return _call(body, (x, w, sc, sh), side, name=name, out_shape=[_sds((T, D), BF16)], grid=(T // tm,),
                 in_specs=[_row_spec(tm, D), _vec_spec(D), _vec_spec(D), _vec_spec(D)],
                 out_specs=[_row_spec(tm, D)], sem=("parallel",))


def _gate_rows(dxv, br_ref, g_ref, db_ref, acc_ref):
    db = g_ref[...] * dxv
    db_ref[...] = db.astype(BF16)
    acc_ref[3:4, :] += _colsum(dxv * br_ref[...].astype(F32))
    acc_ref[4:5, :] += _colsum(db)


def _norm_mod_bwd(x, dh, dres, w, sc, branch, g, name, side=None):
    T = x.shape[0]
    tm = _rows(T)
    gated = branch is not None

    def body(x_ref, dh_ref, dres_ref, w_ref, sc_ref, *rest):
        (br_ref, g_ref, dx_ref, db_ref, acc_ref) = rest if gated else (None, None, rest[0], None, rest[1])

        @pl.when(pl.program_id(0) == 0)
        def _():
            acc_ref[...] = jnp.zeros_like(acc_ref)
        xv, dh_v, wv = x_ref[...], dh_ref[...], w_ref[...]
        r = lax.rsqrt(jnp.mean(xv * xv, axis=-1, keepdims=True) + EPS)
        n = xv * r
        dnw = dh_v * (1.0 + sc_ref[...])
        dn = dnw * wv
        dxv = dres_ref[...] + r * (dn - n * jnp.mean(dn * n, axis=-1, keepdims=True))
        dx_ref[...] = dxv
        acc_ref[0:1, :] += _colsum(dh_v * (n * wv))
        acc_ref[1:2, :] += _colsum(dh_v)
        acc_ref[2:3, :] += _colsum(dnw * n)
        if gated:
            _gate_rows(dxv, br_ref, g_ref, db_ref, acc_ref)

    row = _row_spec(tm, D)
    args = (x, dh, dres, w, sc) + ((branch, g) if gated else ())
    return _call(body, args, side, name=name,
                 out_shape=[_sds((T, D))] + ([_sds((T, D), BF16)] if gated else []) + [_sds((8, D))], grid=(T // tm,),
                 in_specs=[row, row, row, _vec_spec(D), _vec_spec(D)] + ([row, _vec_spec(D)] if gated else []),
                 out_specs=[row] + ([row] if gated else []) + [_vec_spec(D, 8)], sem=("arbitrary",))


def _final_loss(x, wf, target, branch, g, name):
    T = x.shape[0]
    tm = _rows(T)

    def body(x_ref, w_ref, t_ref, br_ref, g_ref, dx_ref, db_ref, acc_ref):
        @pl.when(pl.program_id(0) == 0)
        def _():
            acc_ref[...] = jnp.zeros_like(acc_ref)
        xv, wv = x_ref[...], w_ref[...]
        r = lax.rsqrt(jnp.mean(xv * xv, axis=-1, keepdims=True) + EPS)
        n = xv * r
        err = n * wv - t_ref[...]
        dy = err * (1.0 / D)
        dn = dy * wv
        dxv = r * (dn - n * jnp.mean(dn * n, axis=-1, keepdims=True))
        dx_ref[...] = dxv
        acc_ref[0:1, :] += _colsum(dy * n)
        acc_ref[1:2, :] += jnp.broadcast_to(_allsum(err * err) * (0.5 / D), (1, D))
        _gate_rows(dxv, br_ref, g_ref, db_ref, acc_ref)

    row = _row_spec(tm, D)
    return _pcall(body, name=name, out_shape=[_sds((T, D)), _sds((T, D), BF16), _sds((8, D))], grid=(T // tm,),
                  in_specs=[row, _vec_spec(D), row, row, _vec_spec(D)],
                  out_specs=[row, row, _vec_spec(D, 8)], sem=("arbitrary",))(x, wf, target, branch, g)


def _mod_matmul(c_all, ada_w, name):
    n = ada_w.shape[2]

    def body(c_ref, w_ref, cond_ref, o_ref):
        cond = _silu(c_ref[...])
        cond_ref[...] = cond
        o_ref[0] = _nn(cond, w_ref[0])

    return _pcall(body, name=name, out_shape=[_sds((N_DEV, D)), _sds((DEPTH, N_DEV, n))], grid=(DEPTH,),
                  in_specs=[pl.BlockSpec((N_DEV, D), lambda l: (0, 0)), pl.BlockSpec((1, D, n), lambda l: (l, 0, 0))],
                  out_specs=[pl.BlockSpec((N_DEV, D), lambda l: (0, 0)), pl.BlockSpec((1, N_DEV, n), lambda l: (l, 0, 0))],
                  sem=("arbitrary",))(c_all, ada_w)


def _add_rows(a, b, name):
    def body(a_ref, b_ref, o_ref):
        o_ref[...] = a_ref[...] + b_ref[...]

    return _pcall(body, name=name, out_shape=_sds(a.shape))(a, b)


def _ada_w_grad(cond, dmod_cols, name):
    n = dmod_cols.shape[2]

    def body(c_ref, d_ref, o_ref):
        o_ref[0] = _tn(c_ref[...], d_ref[0])

    return _pcall(body, name=name, out_shape=_sds((DEPTH, D, n)), grid=(DEPTH,),
                  in_specs=[pl.BlockSpec((N_DEV, D), lambda l: (0, 0)), pl.BlockSpec((1, N_DEV, n), lambda l: (l, 0, 0))],
                  out_specs=pl.BlockSpec((1, D, n), lambda l: (l, 0, 0)), sem=("parallel",))(cond, dmod_cols)


def _conv_fwd(pm, conv_w, conv_b, name):
    T = pm.shape[0]
    tm = _rows(T)
    C = CONV_DIM

    def body(x_ref, prev_ref, w_ref, b_ref, o_ref):
        cur = x_ref[...].astype(F32)
        prev = jnp.where(pl.program_id(0) > 0, prev_ref[...].astype(F32)[8:16], 0.0)
        cur8 = cur[0:8]
        row8 = lax.broadcasted_iota(jnp.int32, (8, C), 0)
        full = w_ref[3:4, :] * cur
        head = w_ref[3:4, :] * cur8
        for k in range(1, SSM_CONV):
            wk = w_ref[3 - k:4 - k, :]
            full = full + wk * pltpu.roll(cur, k, 0)
            head = head + wk * jnp.where(row8 < k, pltpu.roll(prev, k, 0), pltpu.roll(cur8, k, 0))
        o_ref[...] = full + b_ref[...]
        o_ref[0:8, :] = head + b_ref[...]

    return _pcall(
        body, name=name, out_shape=_sds((T, C)), grid=(T // tm,),
        in_specs=[pl.BlockSpec((tm, C), lambda i: (i, 2)),
                  pl.BlockSpec((16, C), lambda i: (jnp.maximum(i * (tm // 16) - 1, 0), 2)),
                  _vec_spec(C, SSM_CONV), _vec_spec(C)],
        out_specs=_row_spec(tm, C), sem=("parallel",))(pm, pm, conv_w, conv_b)


def _conv_bwd(dc, pm, conv_w, name, side=None):
    T = dc.shape[0]
    tm = _rows(T)
    C = CONV_DIM
    nt = T // tm

    def body(dc_ref, nxt_ref, x_ref, w_ref, dx_ref, acc_ref):
        i = pl.program_id(0)

        @pl.when(i == 0)
        def _():
            acc_ref[...] = jnp.zeros_like(acc_ref)
        dcv = dc_ref[...]
        nxt = jnp.where(i < nt - 1, nxt_ref[...], 0.0)
        xc = x_ref[...].astype(F32)
        dc8t, x8t = dcv[tm - 8:tm], xc[tm - 8:tm]
        row8 = lax.broadcasted_iota(jnp.int32, (8, C), 0)
        full = w_ref[3:4, :] * dcv
        tail = w_ref[3:4, :] * dc8t
        acc_ref[3:4, :] += _colsum(dcv * xc)
        for k in range(1, SSM_CONV):
            wk = w_ref[3 - k:4 - k, :]
            up = pltpu.roll(dcv, tm - k, 0)
            up_tail = jnp.where(row8 + k >= 8, pltpu.roll(nxt, 8 - k, 0), pltpu.roll(dc8t, 8 - k, 0))
            full = full + wk * up
            tail = tail + wk * up_tail
            prod = up * xc
            acc_ref[3 - k:4 - k, :] += _colsum(prod) - _colsum(prod[tm - 8:tm]) + _colsum(up_tail * x8t)
        acc_ref[4:5, :] += _colsum(dcv)
        dx_ref[...] = jnp.concatenate([full[0:tm - 8], tail], axis=0).astype(BF16)

    return _call(
        body, (dc, dc, pm, conv_w), side, name=name, out_shape=[_sds((T, C), BF16), _sds((8, C))], grid=(nt,),
        in_specs=[_row_spec(tm, C),
                  pl.BlockSpec((8, C), lambda i: (jnp.minimum((i + 1) * (tm // 8), T // 8 - 1), 0)),
                  pl.BlockSpec((tm, C), lambda i: (i, 2)),
                  _vec_spec(C, SSM_CONV)],
        out_specs=[_row_spec(tm, C), _vec_spec(C, 8)], sem=("arbitrary",))


def _ssd_prologue(cpre, dtr, dtb, alog):
    L = CHUNK
    xc = _silu(cpre)
    pre = dtr + dtb
    dt = jnp.maximum(pre, 0.0) + jnp.log1p(jnp.exp(-jnp.abs(pre)))
    a = -jnp.exp(alog)
    la = dt * a
    row = lax.broadcasted_iota(jnp.int32, (L, L), 0)
    col = lax.broadcasted_iota(jnp.int32, (L, L), 1)
    causal = row >= col
    tri = causal.astype(F32)
    lc = _nn(tri, la, HIGHEST)
    return xc, pre, dt, a, causal, tri, lc, row, col


def _head_indicator():
    m = np.zeros((LANES, SSM_INNER), np.float32)
    for h in range(SSM_HEADS):
        m[h, h * SSM_HEAD_DIM:(h + 1) * SSM_HEAD_DIM] = 1.0
    return jnp.asarray(m, dtype=BF16)


def _split_dot(x, ind, dims):
    hi = x.astype(BF16)
    lo = (x - hi.astype(F32)).astype(BF16)
    return _dot(hi, ind, dims) + _dot(lo, ind, dims)


def _expand(x16, ind):
    return _split_dot(x16, ind, ((1,), (0,)))


def _headsum(x, ind, single_pass=False):
    if single_pass:
        return _dot(x.astype(BF16), ind, ((1,), (1,)))
    return _split_dot(x, ind, ((1,), (1,)))


def _ssd_fwd(cpre, dtr, pm, dtb, alog, dskip, normw, ind, name, side=None):
    T = cpre.shape[0]
    nc = T // CHUNK
    L, P, H, HPG, N = CHUNK, SSM_HEAD_DIM, SSM_HEADS, SSM_HEADS // SSM_GROUPS, SSM_STATE
    half = SSM_INNER // SSM_GROUPS

    def body(cp_ref, dtr_ref, z_ref, dtb_ref, alog_ref, dskip_ref, nw_ref, ind_ref, ya_ref, y_ref, sp_ref, st_ref):
        @pl.when(pl.program_id(0) == 0)
        def _():
            st_ref[...] = jnp.zeros_like(st_ref)
        xc, _, dt, _, causal, _, lc, _, _ = _ssd_prologue(cp_ref[...], dtr_ref[...], dtb_ref[...], alog_ref[...])
        lct = lc.T
        ind = ind_ref[...]
        llast = lc[L - 1:L, :]
        xs = xc[:, :SSM_INNER]
        xd = xs * _expand(dt, ind)
        ex = _expand(jnp.exp(lc), ind)
        xd_end = _bf(xd * _expand(jnp.exp(llast - lc), ind))
        cdx = _expand(jnp.broadcast_to(jnp.exp(llast), (8, LANES)), ind)[0:1]
        xdb = _bf(xd)
        sp_ref[0] = st_ref[...]
        for g in range(SSM_GROUPS):
            sl = slice(g * half, (g + 1) * half)
            bm = _bf(xc[:, SSM_INNER + g * N:SSM_INNER + (g + 1) * N])
            cm = _bf(xc[:, SSM_INNER + (SSM_GROUPS + g) * N:SSM_INNER + (SSM_GROUPS + g + 1) * N])
            cb = _nt(cm, bm)
            st = st_ref[g]
            y_ref[:, sl] = ex[:, sl] * _nn(cm, _bf(st)) + dskip_ref[:, sl] * xs[:, sl]
            st_ref[g] = st * cdx[:, sl] + _tn(bm, xd_end[:, sl])
            for j in range(HPG):
                h = g * HPG + j
                decay = jnp.where(causal, jnp.exp(jnp.where(causal, lc[:, h:h + 1] - lct[h:h + 1, :], 0.0)), 0.0)
                y_ref[:, h * P:(h + 1) * P] += _nn(_bf(cb * decay), xdb[:, h * P:(h + 1) * P])
        y2 = y_ref[...] * _silu(z_ref[...].astype(F32))
        for g in range(SSM_GROUPS):
            yg = y2[:, g * half:(g + 1) * half]
            r = lax.rsqrt(jnp.mean(yg * yg, axis=-1, keepdims=True) + EPS)
            ya_ref[:, g * half:(g + 1) * half] = (yg * r * nw_ref[:, g * half:(g + 1) * half]).astype(BF16)

    return _call(
        body, (cpre, dtr, pm, dtb, alog, dskip, normw, ind), side, name=name,
        out_shape=[_sds((T, SSM_INNER), BF16), _sds((T, SSM_INNER)), _sds((nc, SSM_GROUPS, N, half))], grid=(nc,),
        in_specs=[_row_spec(L, CONV_DIM), _row_spec(L, LANES), _row_spec(L, SSM_INNER, 0),
                  _vec_spec(LANES), _vec_spec(LANES), _vec_spec(SSM_INNER), _vec_spec(SSM_INNER), _vec_spec(SSM_INNER, LANES)],
        out_specs=[_row_spec(L, SSM_INNER), _row_spec(L, SSM_INNER),
                   pl.BlockSpec((1, SSM_GROUPS, N, half), lambda i: (i, 0, 0, 0))],
        scratch=[pltpu.VMEM((SSM_GROUPS, N, half), F32)], sem=("arbitrary",))


def _ssd_bwd(cpre, dtr, pm, ypre, sprev, dya, dtb, alog, dskip, normw, ind, name, side=None):
    T = cpre.shape[0]
    nc = T // CHUNK
    L, P, H, HPG, N = CHUNK, SSM_HEAD_DIM, SSM_HEADS, SSM_HEADS // SSM_GROUPS, SSM_STATE
    half = SSM_INNER // SSM_GROUPS

    def body(cp_ref, dtr_ref, z_ref, y_ref, sp_ref, dya_ref, dtb_ref, alog_ref, dskip_ref, nw_ref, ind_ref,
             dz_ref, dcp_ref, ddtr_ref, acc_ref, dnw_ref, ds_ref, dy_ref, dxd_ref, rr_ref, yoff_ref, dcd_ref):
        @pl.when(pl.program_id(0) == 0)
        def _():
            ds_ref[...] = jnp.zeros_like(ds_ref)
            acc_ref[...] = jnp.zeros_like(acc_ref)
            dnw_ref[...] = jnp.zeros_like(dnw_ref)
        cpre_v = cp_ref[...]
        xc, pre, dt, a, causal, tri, lc, row, col = _ssd_prologue(cpre_v, dtr_ref[...], dtb_ref[...], alog_ref[...])
        lct = lc.T
        zv, yv = z_ref[...].astype(F32), y_ref[...]
        sz, dsz = _silu_and_grad(zv)
        y2 = yv * sz
        dya_v = dya_ref[...]
        nwv = nw_ref[...]
        for g in range(SSM_GROUPS):
            sl = slice(g * half, (g + 1) * half)
            yg = y2[:, sl]
            r = lax.rsqrt(jnp.mean(yg * yg, axis=-1, keepdims=True) + EPS)
            nrm = yg * r
            dnw_ref[:, sl] += _colsum(dya_v[:, sl] * nrm)
            dn = dya_v[:, sl] * nwv[:, sl]
            dy2 = r * (dn - nrm * jnp.mean(dn * nrm, axis=-1, keepdims=True))
            dy_ref[:, sl] = dy2 * sz[:, sl]
            dz_ref[:, sl] = (dy2 * yv[:, sl] * dsz[:, sl]).astype(BF16)
        ind = ind_ref[...]
        llast = lc[L - 1:L, :]
        dte16 = jnp.exp(llast - lc)
        cd16 = jnp.exp(llast)
        xs = xc[:, :SSM_INNER]
        dtx = _expand(dt, ind)
        ex = _expand(jnp.exp(lc), ind)
        dtex = _expand(dte16, ind)
        cdx = _expand(jnp.broadcast_to(cd16, (8, LANES)), ind)[0:1]
        xd = xs * dtx
        xdb = _bf(xd)
        xd_end = _bf(xd * dtex)
        dyv = dy_ref[...]
        dy_off = _bf(ex * dyv)
        dyb = _bf(dyv)
        dskx = dskip_ref[...]
        lane_c = lax.broadcasted_iota(jnp.int32, (L, LANES), 1)
        lane1 = lax.broadcasted_iota(jnp.int32, (1, LANES), 1)
        sub16 = lax.broadcasted_iota(jnp.int32, (H, L), 0)
        dlc_c = jnp.zeros((L, LANES), F32)
        dlc_r = jnp.zeros((H, L), F32)
        for g in range(SSM_GROUPS):
            sl = slice(g * half, (g + 1) * half)
            b_lo = SSM_INNER + g * N
            c_lo = SSM_INNER + (SSM_GROUPS + g) * N
            bm, cm = _bf(xc[:, b_lo:b_lo + N]), _bf(xc[:, c_lo:c_lo + N])
            cb = _nt(cm, bm)
            st, dst = sp_ref[0, g], ds_ref[g]
            stb, dstb = _bf(st), _bf(dst)
            dcm = _nt(dy_off[:, sl], stb)
            ds_ref[g] = _tn(cm, dy_off[:, sl]) + dst * cdx[:, sl]
            rr_ref[:, sl] = _nn(bm, dstb)
            yoff_ref[:, sl] = ex[:, sl] * _nn(cm, stb)
            db = _nt(xd_end[:, sl], dstb)
            dcd_ref[:, sl] = _colsum(dst * st)
            dcb = jnp.zeros((L, L), F32)
            for j in range(HPG):
                h = g * HPG + j
                hs = slice(h * P, (h + 1) * P)
                decay = jnp.where(causal, jnp.exp(jnp.where(causal, lc[:, h:h + 1] - lct[h:h + 1, :], 0.0)), 0.0)
                m = cb * decay
                dxd_ref[:, hs] = _tn(_bf(m), dyb[:, hs])
                dm = _nt(dyb[:, hs], xdb[:, hs])
                dcb = dcb + dm * decay
                gm = dm * m
                dlc_c = dlc_c + jnp.where(lane_c == h, _rowsum(gm), 0.0)
                dlc_r = dlc_r + jnp.where(sub16 == h, _colsum(gm), 0.0)
            dcbb = _bf(dcb)
            dcp_ref[:, c_lo:c_lo + N] = dcm + _nn(dcbb, bm)
            dcp_ref[:, b_lo:b_lo + N] = db + _tn(dcbb, cm)
        dxd_diag, rr = dxd_ref[...], rr_ref[...]
        tt = _headsum(rr * xd, ind, single_pass=True) * dte16
        dlc_rt = jnp.concatenate([dlc_r, jnp.zeros((LANES - H, L), F32)], axis=0).T
        dlc = dlc_c - dlc_rt + _headsum(dyv * yoff_ref[...], ind, single_pass=True) - tt
        dcd = _headsum(jnp.broadcast_to(dcd_ref[...], (8, SSM_INNER)), ind)[0:1]
        dlc = dlc + jnp.where(row == L - 1, _colsum(tt) + dcd * cd16, 0.0)
        dla = _tn(tri, dlc, HIGHEST)
        dxd = dxd_diag + dtex * rr
        ddt = _headsum(dxd * xs, ind, single_pass=True) + dla * a
        ddtr = jnp.where(lane_c < H, ddt * _sigmoid(pre), 0.0)
        ddtr_ref[...] = ddtr
        acc_ref[0:1, :] += _colsum(ddtr)
        acc_ref[1:2, :] += jnp.where(lane1 < H, _colsum(dla * dt) * a, 0.0)
        acc_ref[2:3, :] += _headsum(jnp.broadcast_to(_colsum(dyv * xs), (8, SSM_INNER)), ind)[0:1]
        dcp_ref[:, 0:SSM_INNER] = dxd * dtx + dskx * dyv
        dcp_ref[...] = dcp_ref[...] * _dsilu(cpre_v)

    rev = lambda i: (nc - 1 - i, 0)
    rspec = lambda c: pl.BlockSpec((L, c), rev)
    return _call(
        body, (cpre, dtr, pm, ypre, sprev, dya, dtb, alog, dskip, normw, ind), side, name=name,
        out_shape=[_sds((T, SSM_INNER), BF16), _sds((T, CONV_DIM)), _sds((T, LANES)), _sds((8, LANES)), _sds((1, SSM_INNER))],
        grid=(nc,),
        in_specs=[rspec(CONV_DIM), rspec(LANES), rspec(SSM_INNER), rspec(SSM_INNER),
                  pl.BlockSpec((1, SSM_GROUPS, N, half), lambda i: (nc - 1 - i, 0, 0, 0)), rspec(SSM_INNER),
                  _vec_spec(LANES), _vec_spec(LANES), _vec_spec(SSM_INNER), _vec_spec(SSM_INNER), _vec_spec(SSM_INNER, LANES)],
        out_specs=[rspec(SSM_INNER), rspec(CONV_DIM), rspec(LANES), _vec_spec(LANES, 8), _vec_spec(SSM_INNER)],
        scratch=[pltpu.VMEM((SSM_GROUPS, N, half), F32), pltpu.VMEM((L, SSM_INNER), F32), pltpu.VMEM((L, SSM_INNER), F32),
                 pltpu.VMEM((L, SSM_INNER), F32), pltpu.VMEM((L, SSM_INNER), F32), pltpu.VMEM((1, SSM_INNER), F32)],
        sem=("arbitrary",))


def _gmlp_common(u, v, lnw, lnb, with_grads=False):
    (ug, dug), (vg, dvg) = (_gelu_and_grad(u), _gelu_and_grad(v)) if with_grads else ((_gelu(u), None), (_gelu(v), None))
    mu = jnp.mean(vg, axis=-1, keepdims=True)
    cen = vg - mu
    rstd = lax.rsqrt(jnp.mean(cen * cen, axis=-1, keepdims=True) + EPS)
    vhat = cen * rstd
    out = (ug, rstd, vhat, vhat * lnw + lnb)
    return out + (dug, dvg) if with_grads else out


def _causal_mask():
    row = lax.broadcasted_iota(jnp.int32, (CHUNK, CHUNK), 0)
    col = lax.broadcasted_iota(jnp.int32, (CHUNK, CHUNK), 1)
    return row >= col


def _gmlp_rows(T):
    return max(r for r in (4 * CHUNK, 2 * CHUNK, CHUNK) if T % r == 0)


def _gmlp_fwd(pm, lnw, lnb, ws, bs_exp, name, side=None):
    T = pm.shape[0]
    L, G = CHUNK, GMLP_GROUPS
    R = _gmlp_rows(T)

    def body(u_ref, v_ref, lnw_ref, lnb_ref, ws_ref, bs_ref, o_ref):
        ug, _, _, vn = _gmlp_common(u_ref[...].astype(F32), v_ref[...].astype(F32), lnw_ref[...], lnb_ref[...])
        causal = _causal_mask()
        for g in range(G):
            sl = slice(g * L, (g + 1) * L)
            wm = _bf(jnp.where(causal, ws_ref[g], 0.0))
            for c in range(R // L):
                rows = slice(c * L, (c + 1) * L)
                sv = _nn(wm, _bf(vn[rows, sl])) + bs_ref[:, sl]
                o_ref[rows, sl] = (ug[rows, sl] * sv).astype(BF16)

    return _call(
        body, (pm, pm, lnw, lnb, ws, bs_exp), side, name=name, out_shape=[_sds((T, GMLP_INNER), BF16)], grid=(T // R,),
        in_specs=[_row_spec(R, GMLP_INNER, 1), _row_spec(R, GMLP_INNER, 2), _vec_spec(GMLP_INNER), _vec_spec(GMLP_INNER),
                  pl.BlockSpec((G, L, L), lambda i: (0, 0, 0)), _vec_spec(GMLP_INNER, L)],
        out_specs=[_row_spec(R, GMLP_INNER)], sem=("parallel",))


def _gmlp_bwd(pm, dyb, lnw, lnb, ws, bs_exp, name, side=None):
    T = pm.shape[0]
    L, G = CHUNK, GMLP_GROUPS
    R = _gmlp_rows(T)

    def body(u_ref, v_ref, dy_ref, lnw_ref, lnb_ref, ws_ref, bs_ref, du_ref, dv_ref, dws_ref, dbs_ref, acc_ref, dvn_ref):
        @pl.when(pl.program_id(0) == 0)
        def _():
            dws_ref[...] = jnp.zeros_like(dws_ref)
            dbs_ref[...] = jnp.zeros_like(dbs_ref)
            acc_ref[...] = jnp.zeros_like(acc_ref)
        uv, vv, dyv, lnwv = u_ref[...].astype(F32), v_ref[...].astype(F32), dy_ref[...], lnw_ref[...]
        ug, rstd, vhat, vn, dug, dvg_act = _gmlp_common(uv, vv, lnwv, lnb_ref[...], with_grads=True)
        causal = _causal_mask()
        lane = lax.broadcasted_iota(jnp.int32, (L, LANES), 1)
        dbs = jnp.zeros((L, LANES), F32)
        for g in range(G):
            sl = slice(g * L, (g + 1) * L)
            wm = _bf(jnp.where(causal, ws_ref[g], 0.0))
            for c in range(R // L):
                rows = slice(c * L, (c + 1) * L)
                vng = _bf(vn[rows, sl])
                sv = _nn(wm, vng) + bs_ref[:, sl]
                du_ref[rows, sl] = (dyv[rows, sl] * sv * dug[rows, sl]).astype(BF16)
                dsv = dyv[rows, sl] * ug[rows, sl]
                dsvb = _bf(dsv)
                dws_ref[g] += jnp.where(causal, _nt(dsvb, vng), 0.0)
                dbs = dbs + jnp.where(lane == g, _rowsum(dsv), 0.0)
                dvn_ref[rows, sl] = _tn(wm, dsvb)
        dbs_ref[...] += dbs
        dvn = dvn_ref[...]
        acc_ref[0:1, :] += _colsum(dvn * vhat)
        acc_ref[1:2, :] += _colsum(dvn)
        dvh = dvn * lnwv
        dvg = rstd * (dvh - jnp.mean(dvh, axis=-1, keepdims=True) - vhat * jnp.mean(dvh * vhat, axis=-1, keepdims=True))
        dv_ref[...] = (dvg * dvg_act).astype(BF16)

    return _call(
        body, (pm, pm, dyb, lnw, lnb, ws, bs_exp), side, name=name,
        out_shape=[_sds((T, GMLP_INNER), BF16), _sds((T, GMLP_INNER), BF16), _sds((G, L, L)), _sds((L, LANES)), _sds((8, GMLP_INNER))],
        grid=(T // R,),
        in_specs=[_row_spec(R, GMLP_INNER, 1), _row_spec(R, GMLP_INNER, 2), _row_spec(R, GMLP_INNER),
                  _vec_spec(GMLP_INNER), _vec_spec(GMLP_INNER), pl.BlockSpec((G, L, L), lambda i: (0, 0, 0)),
                  _vec_spec(GMLP_INNER, L)],
        out_specs=[_row_spec(R, GMLP_INNER), _row_spec(R, GMLP_INNER), pl.BlockSpec((G, L, L), lambda i: (0, 0, 0)),
                   _vec_spec(LANES, L), _vec_spec(GMLP_INNER, 8)],
        scratch=[pltpu.VMEM((R, GMLP_INNER), F32)], sem=("arbitrary",))


def _rel_buckets():
    qi = np.arange(CHUNK)[:, None]
    sj = np.arange(2 * CHUNK)[None, :]
    dist = np.maximum(qi + CHUNK - sj, 0)
    max_exact = REL_BUCKETS // 2
    log_ratio = (np.log(np.maximum(dist, 1).astype(np.float32) / np.float32(max_exact))
                 / np.float32(math.log(REL_MAX_DIST / max_exact))).astype(np.float32)
    large = max_exact + (log_ratio * np.float32(REL_BUCKETS - max_exact)).astype(np.int32)
    return np.where(dist < max_exact, dist, np.minimum(large, REL_BUCKETS - 1))


def _bucket_onehot_t():
    bucket = _rel_buckets().reshape(-1)
    return jnp.asarray((np.arange(REL_BUCKETS)[:, None] == bucket[None, :]).astype(np.float32))


def _bias_from_table(table_t, onehot_t, window, name):
    def body(t_ref, o_ref, w_ref, out_ref):
        out_ref[...] = jnp.where(w_ref[...] > 0.5, _nn(t_ref[...], o_ref[...], HIGHEST), NEG_INF)

    return _pcall(body, name=name, out_shape=_sds((ATTN_HEADS, onehot_t.shape[1])))(table_t, onehot_t, window)


def _table_from_dbias(dbias, onehot_t, name):
    def body(d_ref, o_ref, out_ref):
        out_ref[...] = _nt(o_ref[...], d_ref[...], HIGHEST)

    return _pcall(body, name=name, out_shape=_sds((REL_BUCKETS, ATTN_HEADS)))(dbias, onehot_t)


def _softmax_sink(logits, sink):
    mx = jnp.maximum(jnp.max(logits, axis=-1, keepdims=True), sink)
    e = jnp.exp(logits - mx)
    es = jnp.exp(sink - mx)
    inv = 1.0 / (_rowsum(e) + es)
    return e * inv, es * inv


def _first_block_penalty(n):
    sj = lax.broadcasted_iota(jnp.int32, (1, 2 * CHUNK), 1)
    return jnp.where((sj < CHUNK) & (n == 0), NEG_INF, 0.0)


def _window_mask_flat():
    qi = np.arange(CHUNK)[:, None]
    sj = np.arange(2 * CHUNK)[None, :]
    rel = qi + CHUNK - sj
    return jnp.asarray(((rel >= 0) & (rel < CHUNK)).astype(np.float32).reshape(1, -1))


def _stack_heads(ref, first, count, width, scale=None):
    x = jnp.concatenate([_bf(ref[:, (first + j) * width:(first + j + 1) * width]) for j in range(count)], axis=0)
    return x if scale is None else x * jnp.asarray(scale, x.dtype)


def _attn_fwd(qkv, bias, sinks, name):
    T = qkv.shape[0]
    nb = T // CHUNK
    L, DH, HPK = CHUNK, ATTN_DH, ATTN_HEADS // ATTN_KV
    scale = DH ** -0.5
    kcol, vcol = ATTN_HEADS * DH // LANES, ATTN_HEADS * DH // LANES + 1

    def body(q_ref, k_ref, v_ref, kp_ref, vp_ref, bias_ref, sink_ref, o_ref, lg_ref, p_ref):
        n = pl.program_id(0)
        pen = _first_block_penalty(n)
        kband = _bf(jnp.concatenate([kp_ref[...], k_ref[...]], axis=0))
        vband = _bf(jnp.concatenate([vp_ref[...], v_ref[...]], axis=0))
        for kv in range(ATTN_KV):
            lg_ref[...] = _nt(_stack_heads(q_ref, kv * HPK, HPK, DH, scale), kband[:, kv * DH:(kv + 1) * DH])
            for j in range(HPK):
                h = kv * HPK + j
                p, _ = _softmax_sink(lg_ref[j * L:(j + 1) * L, :] + bias_ref[h] + pen, sink_ref[h])
                p_ref[j * L:(j + 1) * L, :] = _bf(p)
            og = _nn(p_ref[...], vband[:, kv * DH:(kv + 1) * DH])
            for j in range(HPK):
                h = kv * HPK + j
                o_ref[:, h * DH:(h + 1) * DH] = og[j * L:(j + 1) * L].astype(BF16)

    prev = lambda i: jnp.maximum(i - 1, 0)
    return _pcall(
        body, name=name, out_shape=_sds((T, ATTN_HEADS * DH), BF16), grid=(nb,),
        in_specs=[_row_spec(L, ATTN_HEADS * DH, 0), _row_spec(L, LANES, kcol), _row_spec(L, LANES, vcol),
                  pl.BlockSpec((L, LANES), lambda i: (prev(i), kcol)), pl.BlockSpec((L, LANES), lambda i: (prev(i), vcol)),
                  pl.BlockSpec((ATTN_HEADS, L, 2 * L), lambda i: (0, 0, 0)),
                  pl.BlockSpec(memory_space=pltpu.SMEM)],
        out_specs=_row_spec(L, ATTN_HEADS * DH),
        scratch=[pltpu.VMEM((HPK * L, 2 * L), F32), pltpu.VMEM((HPK * L, 2 * L), BF16)],
        sem=("parallel",))(qkv, qkv, qkv, qkv, qkv, bias, sinks)


def _attn_bwd(qkv, datt, bias, sinks, name, side=None):
    T = qkv.shape[0]
    nb = T // CHUNK
    L, DH, HPK = CHUNK, ATTN_DH, ATTN_HEADS // ATTN_KV
    scale = DH ** -0.5
    kcol, vcol = ATTN_HEADS * DH // LANES, ATTN_HEADS * DH // LANES + 1

    def body(q_ref, k_ref, v_ref, kp_ref, vp_ref, do_ref, bias_ref, sink_ref,
             dq_ref, dk_ref, dv_ref, bsum_ref, dbias_ref, dsink_ref, pend_k, pend_v, band_k, band_v, lg_ref, dp_ref, p_ref, dl_ref):
        n = pl.program_id(0)

        @pl.when(n == 0)
        def _():
            dbias_ref[...] = jnp.zeros_like(dbias_ref)
            dsink_ref[...] = jnp.zeros_like(dsink_ref)
            bsum_ref[...] = jnp.zeros_like(bsum_ref)

        def emit_kv(dk, dv):
            dk_ref[...] = dk.astype(BF16)
            dv_ref[...] = dv.astype(BF16)
            bsum_ref[:, ATTN_HEADS * DH:ATTN_HEADS * DH + LANES] += _colsum(dk)
            bsum_ref[:, ATTN_HEADS * DH + LANES:] += _colsum(dv)

        @pl.when(n < nb)
        def _():
            pen = _first_block_penalty(n)
            kband = _bf(jnp.concatenate([kp_ref[...], k_ref[...]], axis=0))
            vband = _bf(jnp.concatenate([vp_ref[...], v_ref[...]], axis=0))
            lane1 = lax.broadcasted_iota(jnp.int32, (1, LANES), 1)
            dsink = jnp.zeros((1, LANES), F32)
            for kv in range(ATTN_KV):
                kb, vb = kband[:, kv * DH:(kv + 1) * DH], vband[:, kv * DH:(kv + 1) * DH]
                qg = _stack_heads(q_ref, kv * HPK, HPK, DH, scale)
                dog = _stack_heads(do_ref, kv * HPK, HPK, DH)
                lg_ref[...] = _nt(qg, kb)
                dp_ref[...] = _nt(dog, vb)
                for j in range(HPK):
                    h = kv * HPK + j
                    rows = slice(j * L, (j + 1) * L)
                    p, ps = _softmax_sink(lg_ref[rows, :] + bias_ref[h] + pen, sink_ref[h])
                    dp = dp_ref[rows, :]
                    delta = _rowsum(p * dp)
                    dl = p * (dp - delta)
                    dbias_ref[h] += dl
                    p_ref[rows, :] = _bf(p)
                    dl_ref[rows, :] = _bf(dl)
                    dsink = dsink + jnp.where(lane1 == h, -_colsum(ps * delta), 0.0)
                band_v[:, kv * DH:(kv + 1) * DH] = _tn(p_ref[...], dog)
                dqg = _nn(dl_ref[...], kb) * scale
                band_k[:, kv * DH:(kv + 1) * DH] = _tn(dl_ref[...], qg)
                for j in range(HPK):
                    h = kv * HPK + j
                    dq_ref[:, h * DH:(h + 1) * DH] = dqg[j * L:(j + 1) * L].astype(BF16)
                    bsum_ref[:, h * DH:(h + 1) * DH] += _colsum(dqg[j * L:(j + 1) * L])
            dsink_ref[...] += dsink

            @pl.when(n > 0)
            def _():
                emit_kv(pend_k[...] + band_k[0:L, :], pend_v[...] + band_v[0:L, :])
            pend_k[...] = band_k[L:2 * L, :]
            pend_v[...] = band_v[L:2 * L, :]

        @pl.when(n == nb)
        def _():
            emit_kv(pend_k[...], pend_v[...])

    cur = lambda i: jnp.minimum(i, nb - 1)
    prev = lambda i: jnp.maximum(jnp.minimum(i, nb - 1) - 1, 0)
    lag = lambda i: jnp.maximum(i - 1, 0)
    return _call(
        body, (qkv, qkv, qkv, qkv, qkv, datt, bias, sinks), side, name=name,
        out_shape=[_sds((T, ATTN_HEADS * DH), BF16), _sds((T, LANES), BF16), _sds((T, LANES), BF16), _sds((1, QKV_DIM)),
                   _sds((ATTN_HEADS, L, 2 * L)), _sds((1, LANES))],
        grid=(nb + 1,),
        in_specs=[pl.BlockSpec((L, ATTN_HEADS * DH), lambda i: (cur(i), 0)),
                  pl.BlockSpec((L, LANES), lambda i: (cur(i), kcol)), pl.BlockSpec((L, LANES), lambda i: (cur(i), vcol)),
                  pl.BlockSpec((L, LANES), lambda i: (prev(i), kcol)), pl.BlockSpec((L, LANES), lambda i: (prev(i), vcol)),
                  pl.BlockSpec((L, ATTN_HEADS * DH), lambda i: (cur(i), 0)),
                  pl.BlockSpec((ATTN_HEADS, L, 2 * L), lambda i: (0, 0, 0)),
                  pl.BlockSpec(memory_space=pltpu.SMEM)],
        out_specs=[pl.BlockSpec((L, ATTN_HEADS * DH), lambda i: (cur(i), 0)),
                   pl.BlockSpec((L, LANES), lambda i: (lag(i), 0)), pl.BlockSpec((L, LANES), lambda i: (lag(i), 0)),
                   _vec_spec(QKV_DIM), pl.BlockSpec((ATTN_HEADS, L, 2 * L), lambda i: (0, 0, 0)), _vec_spec(LANES)],
        scratch=[pltpu.VMEM((L, LANES), F32), pltpu.VMEM((L, LANES), F32),
                 pltpu.VMEM((2 * L, LANES), F32), pltpu.VMEM((2 * L, LANES), F32),
                 pltpu.VMEM((HPK * L, 2 * L), F32), pltpu.VMEM((HPK * L, 2 * L), F32),
                 pltpu.VMEM((HPK * L, 2 * L), BF16), pltpu.VMEM((HPK * L, 2 * L), BF16)],
        sem=("arbitrary",))


def _pad_rows(a, mult):
    pad = (-a.shape[-2]) % mult
    if pad == 0:
        return a
    cfg = [(0, 0)] * (a.ndim - 2) + [(0, pad), (0, 0)]
    return jnp.pad(a, cfg)


class _Pack:
    def __init__(self, width, mult, total_mult):
        self.width, self.mult, self.total_mult = width, mult, total_mult
        self.entries = []
        self.rows = 0

    def add(self, key, shape):
        n = int(np.prod(shape))
        rows = -(-n // self.width)
        self.entries.append((key, self.rows, rows, tuple(shape)))
        self.rows += -(-rows // self.mult) * self.mult

    @property
    def total(self):
        return -(-self.rows // self.total_mult) * self.total_mult

    def pack(self, pieces, dtype, lead=()):
        parts = []
        for key, _, rows, shape in self.entries:
            a = pieces[key].astype(dtype).reshape(lead + (-1,))
            n = int(np.prod(shape))
            a = jnp.pad(a, [(0, 0)] * len(lead) + [(0, rows * self.width - n)])
            a = a.reshape(lead + (rows, self.width))
            parts.append(_pad_rows(a, self.mult))
        out = jnp.concatenate(parts, axis=len(lead))
        return _pad_rows(out, self.total_mult)

    def unpack(self, packed, lead=()):
        out = {}
        for key, off, rows, shape in self.entries:
            a = lax.slice_in_dim(packed, off, off + rows, axis=len(lead))
            a = a.reshape(lead + (-1,))
            n = int(np.prod(shape))
            out[key] = lax.slice_in_dim(a, 0, n, axis=len(lead)).reshape(lead + shape)
        return out


def _ffn_fwd(x, h, mod, wg_t, wu_t, wd, tag, next_norm=None, gather=None):
    side = None if gather is None else (_GatherOps, gather)
    gate, up, act, *gathered = _mm_swiglu(h, wg_t, wu_t, f"ffn_gateup_{tag}", side=side)
    x_out, ffn_out, *h_next = _mm_resid([(act, wd)], x, mod[5:6], name=f"ffn_down_{tag}", norm=next_norm)
    return (x_out, dict(h=h, gate=gate, up=up, act=act, out=ffn_out), *h_next, *gathered)


def _ffn_bwd(dx_out, dffn, x_in, saved, mod, norm_w, wg_t, wu_t, wd, below, tag, exchange=None):
    side = None if exchange is None else (_ChipsOps, exchange)
    dgate, dup, *from_chips = _mm_swiglu_bwd(dffn, wd, saved["gate"], saved["up"], f"ffn_act_bwd_{tag}", side=side)
    d_wd = _mm_tn(saved["act"], dffn, name=f"ffn_dwd_{tag}")
    d_wg_t = _mm_tn(dgate, saved["h"], name=f"ffn_dwg_{tag}")
    d_wu_t = _mm_tn(dup, saved["h"], name=f"ffn_dwu_{tag}")
    dh = _mm([(dgate, wg_t), (dup, wu_t)], "nn", name=f"ffn_dh_{tag}")
    dx, d_below, acc = _norm_mod_bwd(x_in, dh, dx_out, norm_w, mod[4:5], below[0], below[1], f"ffn_norm_bwd_{tag}")
    return (dx, d_below, dict(d_wg=d_wg_t, d_wu=d_wu_t, d_wd=d_wd, acc=acc), *from_chips)


_BIG = [
    ("out_w", "out_w_even", 0, "row"), ("qkv_w", "qkv_w", 0, "col"), ("o_w", "o_w", 0, "row"),
    ("gate0", "ffn_gate_w", 0, "col"), ("up0", "ffn_up_w", 0, "col"), ("down0", "ffn_down_w", 0, "row"),
    ("gate1", "ffn_gate_w", 1, "col"), ("up1", "ffn_up_w", 1, "col"), ("down1", "ffn_down_w", 1, "row"),
    ("in_w", "in_w_even", 0, "col"),
]


def _to_wire(a, kind):
    return a.T if kind == "col" else a


_GATHER_GROUPS = [["in_w"], ["out_w"], ["gate0", "up0", "down0"], ["qkv_w", "o_w"], ["gate1", "up1", "down1"]]
_GRAD_GROUPS = [["gate1", "up1", "down1"], ["qkv_w", "o_w"], ["out_w", "gate0", "up0", "down0"], ["in_w"]]

_REPLICATED = ["ada_b", "norm_mix_w", "norm_ffn_w", "conv_b", "dt_bias", "a_log", "d_skip", "ssm_norm_w", "gmlp_ln_w",
               "gmlp_ln_b", "gmlp_ws", "gmlp_bs", "sinks", "rel_table", "final_norm_w"]
_TINY_SHARDED = ["conv_w", "qkv_b", "o_b"]

_WEIGHTS = ['ada_w', 'ada_b', 'norm_mix_w', 'norm_ffn_w', 'in_w_even', 'conv_w', 'conv_b', 'dt_bias', 'a_log', 'd_skip',
            'ssm_norm_w', 'gmlp_ln_w', 'gmlp_ln_b', 'gmlp_ws', 'gmlp_bs', 'out_w_even', 'qkv_w', 'qkv_b', 'o_w', 'o_b',
            'sinks', 'rel_table', 'ffn_gate_w', 'ffn_up_w', 'ffn_down_w', 'final_norm_w']


def _step(x, c, loss_target, W, M, V):
    T = x.shape[1]
    x0 = x[0]
    target = loss_target[0]
    me = 4 * lax.axis_index("x") + 2 * lax.axis_index("y") + lax.axis_index("c")

    w_wire_local = {key: _to_wire(W[name][layer].astype(BF16), kind) for key, name, layer, kind in _BIG}

    def wire_pack(keys, mult):
        gp = _Pack(D, 1, mult)
        for key in keys:
            gp.add(key, w_wire_local[key].shape)
        return gp

    gather_packs = [(gp, gp.pack(w_wire_local, BF16)) for gp in (wire_pack(keys, 16) for keys in _GATHER_GROUPS)]
    grad_packs = [wire_pack(keys, 64) for keys in _GRAD_GROUPS]
    full = {}

    def gathered_weights(group, gathered):
        shards = gather_packs[group][0].unpack(gathered, lead=(N_DEV,))
        full.update({key: a.reshape(-1, D) for key, a in shards.items()})


    small_in = _Pack(D, 8, 8)
    small_in.add("c", (1, D))
    small_in.add("conv_w", W["conv_w"][0].shape)
    small_in.add("qkv_b", W["qkv_b"][0].shape)
    small_in.add("o_b", W["o_b"][0].shape)
    sm = small_in.unpack(_all_gather(small_in.pack(
        dict(c=c, conv_w=W["conv_w"][0], qkv_b=W["qkv_b"][0], o_b=W["o_b"][0]), F32), "gather_small"), lead=(N_DEV,))
    c_all = sm["c"].reshape(N_DEV, D)
    conv_w_full = jnp.transpose(sm["conv_w"], (1, 0, 2)).reshape(SSM_CONV, CONV_DIM)
    qkv_b_full = sm["qkv_b"].reshape(1, QKV_DIM)
    o_b_full = sm["o_b"].reshape(1, D)

    ncol = W["ada_w"].shape[2]
    cond, mod_cols = _mod_matmul(c_all, W["ada_w"], "mod_matmul")
    mod_g = _all_gather(mod_cols.reshape(DEPTH * N_DEV, ncol), "gather_mod").reshape(N_DEV, DEPTH, N_DEV, ncol)
    mod_me = lax.dynamic_index_in_dim(mod_g, me, axis=2, keepdims=False)
    mod_me = jnp.transpose(mod_me, (1, 0, 2)).reshape(DEPTH, 6, D)
    mod_me = jnp.pad(mod_me, ((0, 0), (0, 2), (0, 0))).reshape(DEPTH * 8, D)
    ada_b_rows = jnp.pad(W["ada_b"].reshape(DEPTH, 6, D), ((0, 0), (0, 2), (0, 0))).reshape(DEPTH * 8, D)
    mod_all = _add_rows(mod_me, ada_b_rows, "mod_bias").reshape(DEPTH, 8, D)
    mod0, mod1 = mod_all[0], mod_all[1]
    h0, gathered_in = _norm_mod(x0, W["norm_mix_w"][0:1], mod0[1:2], mod0[0:1], "mix_norm_0",
                                side=(_GatherOps, gather_packs[0][1]))
    gathered_weights(0, gathered_in)

    in_t = full["in_w"]
    o1, o2, o3, o4 = SSM_INNER, SSM_INNER + CONV_DIM, SSM_INNER + CONV_DIM + SSM_HEADS, SSM_INNER + CONV_DIM + SSM_HEADS + GMLP_INNER
    w_z, w_xbc, w_dt, w_u, w_v = in_t[:o1], in_t[o1:o2], in_t[o2:o3], in_t[o3:o4], in_t[o4:]
    w_main = jnp.concatenate([w_z, w_u, w_v, w_xbc], axis=0)
    w_dtp = jnp.pad(w_dt, ((0, LANES - SSM_HEADS), (0, 0)))

    pad16 = lambda a: jnp.pad(a.reshape(1, SSM_HEADS), ((0, 0), (0, LANES - SSM_HEADS)))
    dtb, alog = pad16(W["dt_bias"][0]), pad16(W["a_log"][0])
    dskip = jnp.repeat(W["d_skip"][0], SSM_HEAD_DIM).reshape(1, SSM_INNER)
    ssm_nw = W["ssm_norm_w"]
    lnw, lnb = W["gmlp_ln_w"], W["gmlp_ln_b"]
    ws = W["gmlp_ws"][0]
    bs_exp = jnp.repeat(W["gmlp_bs"][0].T, CHUNK, axis=1)
    conv_b = W["conv_b"]
    nmw, nfw = W["norm_mix_w"], W["norm_ffn_w"]
    onehot_t = _bucket_onehot_t()
    head_ind = _head_indicator()
    bias = _bias_from_table(W["rel_table"].T, onehot_t, _window_mask_flat(), "rel_bias").reshape(ATTN_HEADS, CHUNK, 2 * CHUNK)
    sinks = W["sinks"][0]

    pm, gathered_a = _mm([(h0, w_main)], "nt", name="in_proj", tn_pref=1536, out_dtype=BF16,
                         side=(_GatherOps, gather_packs[1][1]))
    gathered_weights(1, gathered_a)
    out_w = full["out_w"]
    dtr = _mm([(h0, w_dtp)], "nt", name="in_proj_dt")
    cpre = _conv_fwd(pm, conv_w_full, conv_b, "conv_fwd")
    ya, ypre, sprev, gathered_b = _ssd_fwd(cpre, dtr, pm, dtb, alog, dskip, ssm_nw, head_ind, "ssd_fwd",
                                           side=(_GatherOps, gather_packs[2][1]))
    gathered_weights(2, gathered_b)
    yb, gathered_c = _gmlp_fwd(pm, lnw, lnb, ws, bs_exp, "gmlp_fwd", side=(_GatherOps, gather_packs[3][1]))
    gathered_weights(3, gathered_c)
    x1, mix0, hf0 = _mm_resid([(ya, out_w[:SSM_INNER]), (yb, out_w[SSM_INNER:])], x0, mod0[2:3], name="out_proj",
                              norm=(nfw[0:1], mod0[4:5], mod0[3:4]))
    x2, ffn0, h1, gathered_d = _ffn_fwd(x1, hf0, mod0, full["gate0"], full["up0"], full["down0"], "0",
                                        next_norm=(nmw[1:2], mod1[1:2], mod1[0:1]), gather=gather_packs[4][1])
    gathered_weights(4, gathered_d)
    qkv_t, o_w = full["qkv_w"], full["o_w"]
    w_q, w_k, w_v_att = qkv_t[:D], qkv_t[D:D + LANES], qkv_t[D + LANES:]

    qkv = _mm([(h1, qkv_t)], "nt", name="qkv_proj", bias=qkv_b_full, tn_pref=1280, out_dtype=BF16)
    att = _attn_fwd(qkv, bias, sinks, "attn_fwd")
    x3, mix1, hf1 = _mm_resid([(att, o_w)], x2, mod1[2:3], name="o_proj", bias=o_b_full,
                              norm=(nfw[1:2], mod1[4:5], mod1[3:4]))
    x4, ffn1 = _ffn_fwd(x3, hf1, mod1, full["gate1"], full["up1"], full["down1"], "1")

    dx4, dffn1, acc_f = _final_loss(x4, W["final_norm_w"].reshape(1, D), target, ffn1["out"], mod1[5:6], "final_loss")
    dx3, dmix1, gf1 = _ffn_bwd(dx4, dffn1, x3, ffn1, mod1, nfw[1:2], full["gate1"], full["up1"], full["down1"],
                               (mix1, mod1[2:3]), "1")
    g_wire = dict(gate1=gf1["d_wg"], up1=gf1["d_wu"], down1=gf1["d_wd"])

    def packed_partials(group):
        return grad_packs[group].pack({key: g_wire[key].reshape(N_DEV, -1, D) for key in _GRAD_GROUPS[group]},
                                      F32, lead=(N_DEV,))

    from_chips = {}
    partials_ffn1 = packed_partials(0)
    datt, theirs_ffn1 = _mm([(dmix1, o_w)], "nt", name="o_proj_dx", out_dtype=BF16, side=(_SiblingOps, partials_ffn1))
    pair_ffn1 = _pair_sum(partials_ffn1, theirs_ffn1, "grads_pair_sum_ffn1")
    d_o_w = _mm_tn(att, dmix1, name="o_proj_dw")
    dq, dk, dv, d_qkv_b, dbias, dsinks, from_chips[0] = _attn_bwd(qkv, datt, bias, sinks, "attn_bwd",
                                                                 side=(_ChipsOps, pair_ffn1))
    d_table = _table_from_dbias(dbias.reshape(ATTN_HEADS, -1), onehot_t, "rel_table_grad")
    d_qkv_t = jnp.concatenate([_mm_tn(dq, h1, name="qkv_dw_q"), _mm_tn(dk, h1, name="qkv_dw_k"), _mm_tn(dv, h1, name="qkv_dw_v")], axis=0)
    g_wire.update(qkv_w=d_qkv_t, o_w=d_o_w)
    partials_l1 = packed_partials(1)
    dh1, theirs_l1 = _mm([(dq, w_q), (dk, w_k), (dv, w_v_att)], "nn", name="qkv_proj_dx", side=(_SiblingOps, partials_l1))
    pair_l1 = _pair_sum(partials_l1, theirs_l1, "grads_pair_sum_l1")
    dx2, dffn0, acc_n1 = _norm_mod_bwd(x2, dh1, dx3, nmw[1:2], mod1[1:2], ffn0["out"], mod0[5:6], "mix_norm_bwd_1")

    dx1, dmix0, gf0, from_chips[1] = _ffn_bwd(dx2, dffn0, x1, ffn0, mod0, nfw[0:1], full["gate0"], full["up0"], full["down0"],
                                              (mix0, mod0[2:3]), "0", exchange=pair_l1)

    dya = _mm([(dmix0, out_w[:SSM_INNER])], "nt", name="out_proj_dx_a")
    dyb = _mm([(dmix0, out_w[SSM_INNER:])], "nt", name="out_proj_dx_b")
    d_out_w = jnp.concatenate([_mm_tn(ya, dmix0, name="out_proj_dw_a"), _mm_tn(yb, dmix0, name="out_proj_dw_b")], axis=0)
    g_wire.update(gate0=gf0["d_wg"], up0=gf0["d_wu"], down0=gf0["d_wd"], out_w=d_out_w)
    partials_ffn0 = packed_partials(2)
    du, dvg, d_ws, d_bs, acc_ln, theirs_ffn0 = _gmlp_bwd(pm, dyb, lnw, lnb, ws, bs_exp, "gmlp_bwd",
                                                         side=(_SiblingOps, partials_ffn0))
    pair_ffn0 = _pair_sum(partials_ffn0, theirs_ffn0, "grads_pair_sum_mix")
    dz, dcpre, ddtr, acc_ssd, d_ssm_nw, from_chips[2] = _ssd_bwd(
        cpre, dtr, pm, ypre, sprev, dya, dtb, alog, dskip, ssm_nw, head_ind, "ssd_bwd", side=(_ChipsOps, pair_ffn0))
    acc_f0, acc_f1 = gf0["acc"], gf1["acc"]
    row = lambda i: slice(i, i + 1)
    views = dict(ada_b=(DEPTH * 6, D), norm_mix_w=(DEPTH, D), norm_ffn_w=(DEPTH, D), conv_b=(1, CONV_DIM),
                 dt_bias=(1, SSM_HEADS), a_log=(1, SSM_HEADS), d_skip=(1, SSM_HEADS), ssm_norm_w=(1, D), gmlp_ln_w=(1, D),
                 gmlp_ln_b=(1, D), gmlp_ws=(GMLP_GROUPS * CHUNK, CHUNK), gmlp_bs=(GMLP_GROUPS, CHUNK),
                 sinks=(1, ATTN_HEADS), rel_table=(REL_BUCKETS, ATTN_HEADS), final_norm_w=(1, D),
                 conv_w=(SSM_CONV, CONV_DIM), qkv_b=(1, QKV_DIM), o_b=(1, D), loss=(1, D))
    late_keys = ["ada_b", "norm_mix_w", "conv_b", "dt_bias", "a_log", "d_skip", "ssm_norm_w", "conv_w"]
    canvas_early, canvas_late = _Canvas(), _Canvas()
    for key, (r, cdim) in views.items():
        (canvas_late if key in late_keys else canvas_early).add(key, r, cdim, blocks=GMLP_GROUPS if key == "gmlp_ws" else 1)
    early_sources = [
        (acc_f0, [("norm_ffn_w", row(2), 0)]),
        (acc_f1, [("norm_ffn_w", row(2), 1), ("o_b", row(4), 0)]),
        (acc_f, [("final_norm_w", row(0), 0), ("loss", row(1), 0)]),
        (acc_ln, [("gmlp_ln_w", row(0), 0), ("gmlp_ln_b", row(1), 0)]),
        (d_ws.reshape(GMLP_GROUPS * CHUNK, CHUNK), [("gmlp_ws", slice(0, GMLP_GROUPS * CHUNK), 0)]),
        (d_bs.T, [("gmlp_bs", slice(0, GMLP_GROUPS), 0)]),
        (dsinks, [("sinks", row(0), 0)]),
        (d_table, [("rel_table", slice(0, REL_BUCKETS), 0)]),
        (d_qkv_b, [("qkv_b", row(0), 0)]),
    ]
    dxbc, acc_conv, parts_early = _conv_bwd(dcpre, pm, conv_w_full, "conv_bwd",
                                            side=(_GatherOps, _canvas_fill(canvas_early, early_sources, "small_grads_early")))
    d_in_t = jnp.concatenate([
        _mm_tn(dz, h0, name="in_dw_z"), _mm_tn(dxbc, h0, name="in_dw_xbc"),
        _mm_tn(ddtr, h0, name="in_dw_dt")[:SSM_HEADS], _mm_tn(du, h0, name="in_dw_u"), _mm_tn(dvg, h0, name="in_dw_v")], axis=0)
    g_wire.update(in_w=d_in_t)
    partials_mix = packed_partials(3)
    theirs_mix = _comm_call(_SiblingOps, partials_mix, "exchange_grads_sibling")
    pair_mix = _pair_sum(partials_mix, theirs_mix, "grads_pair_sum")
    dh0, from_chips[3] = _mm([(dz, w_z), (dxbc, w_xbc), (ddtr, w_dtp), (du, w_u), (dvg, w_v)], "nn", name="in_proj_dx",
                             side=(_ChipsOps, pair_mix))
    grad_x, acc_n0 = _norm_mod_bwd(x0, dh0, dx1, nmw[0:1], mod0[1:2], None, None, "mix_norm_bwd_0")

    g_mine = {}
    for group in range(len(_GRAD_GROUPS)):
        g_mine.update(grad_packs[group].unpack(_sum_parts(from_chips[group], f"grads_chip_sum_{group}")))
    res_big = [{}, {}, {}, {}]
    for key, name, layer, kind in _BIG:
        g_nat = _to_wire(g_mine[key], kind)
        outs = _adamw(g_nat[None], W[name][layer], M[name][layer], V[name][layer], f"adamw_{key}")
        for res, out in zip(res_big, outs):
            res[key] = out

    late_sources = [
        (acc_n0, [("ada_b", row(1), 0), ("ada_b", row(0), 1), ("norm_mix_w", row(2), 0)]),
        (acc_f0, [("ada_b", row(3), 2), ("ada_b", row(1), 3), ("ada_b", row(0), 4)]),
        (acc_n1, [("ada_b", row(3), 5), ("ada_b", row(1), 6), ("ada_b", row(0), 7), ("norm_mix_w", row(2), 1)]),
        (acc_f1, [("ada_b", row(3), 8), ("ada_b", row(1), 9), ("ada_b", row(0), 10)]),
        (acc_f, [("ada_b", row(3), 11)]),
        (acc_conv, [("conv_w", slice(0, SSM_CONV), 0), ("conv_b", row(4), 0)]),
        (acc_ssd, [("dt_bias", row(0), 0), ("a_log", row(1), 0), ("d_skip", row(2), 0)]),
        (d_ssm_nw, [("ssm_norm_w", row(0), 0)]),
    ]
    parts_small = _all_gather(_canvas_fill(canvas_late, late_sources, "small_grads_late"), "gather_small_grads")
    view = lambda a, key: a.reshape(views[key])
    params = lambda keys: [(n, view(W[n], n), view(M[n], n), view(V[n], n)) for n in _REPLICATED if (n in late_keys) == keys]
    small_out = _adamw_canvas(canvas_early, parts_early, params(False), ["qkv_b", "o_b", "loss"], "adamw_small_early")
    small_out.update(_adamw_canvas(canvas_late, parts_small, params(True), ["conv_w"], "adamw_small_late"))
    loss = small_out["loss"][0][0, 0]
    res_small = [{name: small_out[name][k].reshape(W[name].shape) for name in _REPLICATED} for k in range(4)]

    n_cw, n_qb, n_ob = W["conv_w"].shape[2], W["qkv_b"].shape[1], W["o_b"].shape[1]
    g_tiny = dict(conv_w=lax.dynamic_slice_in_dim(small_out["conv_w"][0], me * n_cw, n_cw, axis=1)[None],
                  qkv_b=lax.dynamic_slice_in_dim(small_out["qkv_b"][0], me * n_qb, n_qb, axis=1),
                  o_b=lax.dynamic_slice_in_dim(small_out["o_b"][0], me * n_ob, n_ob, axis=1))
    tiny = _Pack(D, 8, 8)
    for name in _TINY_SHARDED:
        tiny.add(name, W[name].shape)
    pkt = lambda S: tiny.pack({name: S[name] for name in _TINY_SHARDED}, F32)
    res_tiny = [tiny.unpack(r) for r in _adamw(pkt(g_tiny)[None], pkt(W), pkt(M), pkt(V), "adamw_tiny")]

    dmod_all = parts_small[:, canvas_late.offset["ada_b"]:canvas_late.offset["ada_b"] + DEPTH * 6].reshape(N_DEV, DEPTH, 6 * D)
    dmod_cols = jnp.transpose(lax.dynamic_slice_in_dim(dmod_all, me * ncol, ncol, axis=2), (1, 0, 2))
    g_ada_w = _ada_w_grad(cond, dmod_cols, "ada_w_grad")
    flat = lambda a: a.reshape(DEPTH * D, ncol)
    res_ada = [r.reshape(DEPTH, D, ncol) for r in _adamw(flat(g_ada_w)[None], flat(W["ada_w"]), flat(M["ada_w"]), flat(V["ada_w"]), "adamw_ada_w")]

    def result(kind_idx, name):
        if name == "ada_w":
            return res_ada[kind_idx]
        if name in _REPLICATED:
            return res_small[kind_idx][name]
        if name in _TINY_SHARDED:
            return res_tiny[kind_idx][name]
        pieces = [res_big[kind_idx][key] for key, nm, layer, kind in _BIG if nm == name]
        return jnp.stack(pieces)

    outs = [loss, grad_x[None]]
    for kind_idx in range(4):
        outs += [result(kind_idx, name) for name in _WEIGHTS]
    return tuple(outs)


def kernel(x, c, ada_w, ada_b, norm_mix_w, norm_ffn_w, in_w_even, conv_w, conv_b, dt_bias, a_log, d_skip, ssm_norm_w, gmlp_ln_w, gmlp_ln_b, gmlp_ws, gmlp_bs, out_w_even, qkv_w, qkv_b, o_w, o_b, sinks, rel_table, ffn_gate_w, ffn_up_w, ffn_down_w, final_norm_w, loss_target, m_ada_w, m_ada_b, m_norm_mix_w, m_norm_ffn_w, m_in_w_even, m_conv_w, m_conv_b, m_dt_bias, m_a_log, m_d_skip, m_ssm_norm_w, m_gmlp_ln_w, m_gmlp_ln_b, m_gmlp_ws, m_gmlp_bs, m_out_w_even, m_qkv_w, m_qkv_b, m_o_w, m_o_b, m_sinks, m_rel_table, m_ffn_gate_w, m_ffn_up_w, m_ffn_down_w, m_final_norm_w, v_ada_w, v_ada_b, v_norm_mix_w, v_norm_ffn_w, v_in_w_even, v_conv_w, v_conv_b, v_dt_bias, v_a_log, v_d_skip, v_ssm_norm_w, v_gmlp_ln_w, v_gmlp_ln_b, v_gmlp_ws, v_gmlp_bs, v_out_w_even, v_qkv_w, v_qkv_b, v_o_w, v_o_b, v_sinks, v_rel_table, v_ffn_gate_w, v_ffn_up_w, v_ffn_down_w, v_final_norm_w):
    args = locals()
    W = {n: args[n] for n in _WEIGHTS}
    M = {n: args["m_" + n] for n in _WEIGHTS}
    V = {n: args["v_" + n] for n in _WEIGHTS}
    return _step(x, c, loss_target, W, M, V)
```

```python
import functools
import math

import numpy as np
import jax
import jax.numpy as jnp
from jax import lax
from jax.experimental import pallas as pl
from jax.experimental.pallas import tpu as pltpu

F32 = jnp.float32
BF16 = jnp.bfloat16
HIGHEST = lax.Precision.HIGHEST
MESH = pl.DeviceIdType.MESH

N_DEV = 8
D = 1024
DEPTH = 2
SSM_HEADS = 16
SSM_HEAD_DIM = 64
SSM_INNER = 1024
SSM_GROUPS = 2
SSM_STATE = 128
SSM_CONV = 4
CHUNK = 128
CONV_DIM = SSM_INNER + 2 * SSM_GROUPS * SSM_STATE
GMLP_GROUPS = 8
GMLP_INNER = 1024
IN_EVEN = 4624
ATTN_HEADS = 16
ATTN_KV = 2
ATTN_DH = 64
QKV_DIM = 1280
REL_BUCKETS = 32
REL_MAX_DIST = 128
FFN = 2816
EPS = 1e-6
NEG_INF = -1e30
LANES = 128

ADAM_LR = 0.001
ADAM_B1 = 0.9
ADAM_B2 = 0.999
ADAM_EPS = 1e-08
ADAM_WD = 0.01
ADAM_STEP = 10

VMEM_LIMIT_BYTES = 56 * 1024 * 1024
ROW_TILE = 512


def _pcall(body, *, name, out_shape, grid=(), in_specs=None, out_specs=None, scratch=(), sem=None):
    params = dict(vmem_limit_bytes=VMEM_LIMIT_BYTES)
    if sem is not None:
        params["dimension_semantics"] = sem
    specs = {} if in_specs is None else dict(in_specs=in_specs, out_specs=out_specs)
    return pl.pallas_call(
        body, name=name, out_shape=out_shape, grid=grid, **specs,
        scratch_shapes=list(scratch), compiler_params=pltpu.CompilerParams(**params))


def _call(body, args, side=None, *, name, out_shape, grid, in_specs, out_specs, scratch=(), sem=None):
    if side is None:
        return _pcall(body, name=name, out_shape=out_shape, grid=grid, in_specs=in_specs, out_specs=out_specs,
                      scratch=scratch, sem=sem)(*args)
    ops_cls, x = side
    n_in, n_out, n_scr = len(in_specs), len(out_shape), len(scratch)
    steps = int(np.prod(grid))
    hbm = pl.BlockSpec(memory_space=pl.ANY)

    def wrapped(*refs):
        ins, x_ref = refs[:n_in], refs[n_in]
        outs, r_ref = refs[n_in + 1:n_in + 1 + n_out], refs[n_in + 1 + n_out]
        scr, sems = refs[n_in + 2 + n_out:n_in + 2 + n_out + n_scr], refs[n_in + 2 + n_out + n_scr:]
        ops = ops_cls(x_ref, r_ref, *sems)
        step = pl.program_id(0)
        for axis in range(1, len(grid)):
            step = step * grid[axis] + pl.program_id(axis)
        pl.when(step == 0)(ops.start)
        body(*ins, *outs, *scr)
        pl.when(step == (3 * steps) // 4)(ops.forward)
        pl.when(step == steps - 1)(ops.finish)

    return _pcall(
        wrapped, name=name, out_shape=list(out_shape) + [ops_cls.result(x)], grid=grid,
        in_specs=list(in_specs) + [hbm], out_specs=list(out_specs) + [hbm],
        scratch=list(scratch) + ops_cls.scratch(), sem=("arbitrary",) * len(grid))(*args, x)


def _tile(n, pref):
    if n <= pref:
        return n
    best = None
    for t in range(LANES, pref + 1, LANES):
        if n % t == 0:
            best = t
    assert best is not None, (n, pref)
    return best


def _rows(T):
    return min(ROW_TILE, T)


def _sds(shape, dtype=F32):
    return jax.ShapeDtypeStruct(shape, dtype)


def _row_spec(tm, c, col=0):
    return pl.BlockSpec((tm, c), lambda i, col=col: (i, col))


def _vec_spec(c, r=1):
    return pl.BlockSpec((r, c), lambda i: (0, 0))


def _sigmoid(x):
    return jax.nn.sigmoid(x)


def _silu(x):
    return x * _sigmoid(x)


def _dsilu(x):
    s = _sigmoid(x)
    return s * (1.0 + x * (1.0 - s))


def _silu_and_grad(x):
    s = _sigmoid(x)
    return x * s, s * (1.0 + x * (1.0 - s))


def _gelu(x):
    return 0.5 * x * (1.0 + lax.erf(x * 0.7071067811865476))


def _gelu_and_grad(x):
    phi = 0.5 * (1.0 + lax.erf(x * 0.7071067811865476))
    return x * phi, phi + x * jnp.exp(-0.5 * x * x) * 0.3989422804014327


def _dot(a, b, dims, precision=None):
    return lax.dot_general(a, b, (dims, ((), ())), precision=precision, preferred_element_type=F32)


def _nn(a, b, precision=None):
    return _dot(a, b, ((1,), (0,)), precision)


def _nt(a, b, precision=None):
    return _dot(a, b, ((1,), (1,)), precision)


def _tn(a, b, precision=None):
    return _dot(a, b, ((0,), (0,)), precision)


def _bf(x):
    return x.astype(BF16)


def _colsum(x):
    return jnp.sum(x, axis=0, keepdims=True)


def _rowsum(x):
    return jnp.sum(x, axis=1, keepdims=True)


def _allsum(x):
    return _colsum(_rowsum(x))


def _comm_call(ops_cls, x, name, from_vmem=False):
    def body(x_ref, out_ref, *sems):
        ops = ops_cls(x_ref, out_ref, *sems)
        ops.start()
        ops.forward()
        ops.finish()

    return pl.pallas_call(
        body, name=name, out_shape=ops_cls.result(x),
        in_specs=[pl.BlockSpec(memory_space=pltpu.VMEM if from_vmem else pl.ANY)],
        out_specs=pl.BlockSpec(memory_space=pl.ANY), scratch_shapes=ops_cls.scratch(),
    )(x)


def _all_gather(x, name):
    return _comm_call(_GatherOps, x, name, from_vmem=True)


class _GatherOps:
    def __init__(self, x_ref, out_ref, send_sems, recv_sems, local_sem):
        self.x_ref, self.out_ref = x_ref, out_ref
        self.send_sems, self.recv_sems, self.local_sem = send_sems, recv_sems, local_sem
        mx, my, mc = lax.axis_index("x"), lax.axis_index("y"), lax.axis_index("c")
        self.mc = mc
        self.me, self.sibling = (mx, my, mc), (mx, my, 1 - mc)
        self.chips = [(1 - mx, my), (mx, 1 - my), (1 - mx, 1 - my)]

    @staticmethod
    def result(x):
        return _sds((N_DEV,) + x.shape, x.dtype)

    @staticmethod
    def scratch():
        return [pltpu.SemaphoreType.DMA((7,)), pltpu.SemaphoreType.DMA((7,)), pltpu.SemaphoreType.DMA(())]

    def _slot(self, px, py, pc):
        return self.out_ref.at[4 * px + 2 * py + pc]

    def _copy(self, k, block, to, own=False):
        return pltpu.make_async_remote_copy(
            src_ref=self.x_ref if own else self._slot(*block), dst_ref=self._slot(*block),
            send_sem=self.send_sems.at[k], recv_sem=self.recv_sems.at[k], device_id=to, device_id_type=MESH)

    def _mine(self):
        return pltpu.make_async_copy(self.x_ref, self._slot(*self.me), self.local_sem)

    def _first(self):
        return [self._copy(0, self.me, self.sibling, own=True)] + [
            self._copy(1 + j, self.me, (*chip, self.mc), own=True) for j, chip in enumerate(self.chips)]

    def _passed(self):
        return [self._copy(4 + j, (*chip, self.mc), self.sibling) for j, chip in enumerate(self.chips)]

    def start(self):
        self._mine().start()
        for cp in self._first():
            cp.start()

    def forward(self):
        passed = self._passed()
        for j, chip in enumerate(self.chips):
            self._copy(1 + j, (*chip, self.mc), self.me).wait_recv()
            passed[j].start()

    def finish(self):
        self._copy(0, self.sibling, self.me).wait_recv()
        for j, chip in enumerate(self.chips):
            self._copy(4 + j, (*chip, 1 - self.mc), self.me).wait_recv()
        for cp in self._first() + self._passed():
            cp.wait_send()
        self._mine().wait()


N_CHIP = 4


class _SiblingOps:
    def __init__(self, p_ref, theirs_ref, send_sems, recv_sems):
        self.p_ref, self.theirs_ref, self.send_sems, self.recv_sems = p_ref, theirs_ref, send_sems, recv_sems

    @staticmethod
    def result(p):
        return _sds((N_CHIP,) + p.shape[1:], p.dtype)

    @staticmethod
    def scratch():
        return [pltpu.SemaphoreType.DMA((N_CHIP,))] * 2

    def _copies(self):
        mx, my, mc = lax.axis_index("x"), lax.axis_index("y"), lax.axis_index("c")
        return [pltpu.make_async_remote_copy(
            src_ref=self.p_ref.at[2 * chip + 1 - mc], dst_ref=self.theirs_ref.at[chip],
            send_sem=self.send_sems.at[chip], recv_sem=self.recv_sems.at[chip],
            device_id=(mx, my, 1 - mc), device_id_type=MESH) for chip in range(N_CHIP)]

    def start(self):
        for cp in self._copies():
            cp.start()

    def forward(self):
        pass

    def finish(self):
        for cp in self._copies():
            cp.wait()


class _ChipsOps:
    def __init__(self, q_ref, out_ref, send_sems, recv_sems, local_sem):
        self.q_ref, self.out_ref = q_ref, out_ref
        self.send_sems, self.recv_sems, self.local_sem = send_sems, recv_sems, local_sem

    @staticmethod
    def result(q):
        return _sds(q.shape, q.dtype)

    @staticmethod
    def scratch():
        return [pltpu.SemaphoreType.DMA((N_CHIP - 1,)), pltpu.SemaphoreType.DMA((N_CHIP - 1,)), pltpu.SemaphoreType.DMA(())]

    def _copies(self):
        mx, my, mc = lax.axis_index("x"), lax.axis_index("y"), lax.axis_index("c")
        me = 2 * mx + my
        local = pltpu.make_async_copy(self.q_ref.at[me], self.out_ref.at[me], self.local_sem)
        remote = []
        for r in range(1, N_CHIP):
            px = 1 - mx if r & 2 else mx
            py = 1 - my if r & 1 else my
            remote.append(pltpu.make_async_remote_copy(
                src_ref=self.q_ref.at[2 * px + py], dst_ref=self.out_ref.at[me],
                send_sem=self.send_sems.at[r - 1], recv_sem=self.recv_sems.at[r - 1],
                device_id=(px, py, mc), device_id_type=MESH))
        return local, remote

    def start(self):
        local, remote = self._copies()
        local.start()
        for cp in remote:
            cp.start()

    def forward(self):
        pass

    def finish(self):
        local, remote = self._copies()
        for cp in remote:
            cp.wait()
        local.wait()


def _pair_sum(p, theirs, name):
    n, R, C = theirs.shape
    tr = _tile_rows(R, 256)

    def body(p_ref, t_ref, o_ref):
        mc = lax.axis_index("c")
        o_ref[0] = (p_ref[0, mc] + t_ref[0]).astype(BF16)

    blk = pl.BlockSpec((1, tr, C), lambda s, i: (s, i, 0))
    return _pcall(body, name=name, out_shape=_sds((n, R, C), BF16), grid=(n, R // tr),
                  in_specs=[pl.BlockSpec((1, 2, tr, C), lambda s, i: (s, 0, i, 0)), blk],
                  out_specs=blk, sem=("parallel", "parallel"))(p.reshape(n, 2, R, C), theirs)


def _sum_parts(parts, name):
    P, R, C = parts.shape
    tr = _tile_rows(R, 256)

    def body(p_ref, o_ref):
        g = p_ref[0].astype(F32)
        for k in range(1, P):
            g = g + p_ref[k].astype(F32)
        o_ref[...] = g

    return _pcall(body, name=name, out_shape=_sds((R, C)), grid=(R // tr,),
                  in_specs=[pl.BlockSpec((P, tr, C), lambda i: (0, i, 0))],
                  out_specs=pl.BlockSpec((tr, C), lambda i: (i, 0)), sem=("parallel",))(parts)


def _adamw(parts, w, m, v, name):
    P, R, C = parts.shape
    tr = R if R <= 256 else _tile_rows(R, 256)

    def body(p_ref, w_ref, m_ref, v_ref, g_ref, d_ref, nm_ref, nv_ref):
        g = p_ref[0]
        for k in range(1, P):
            g = g + p_ref[k]
        nm = ADAM_B1 * m_ref[...] + (1.0 - ADAM_B1) * g
        nv = ADAM_B2 * v_ref[...] + (1.0 - ADAM_B2) * (g * g)
        m_hat = nm / (1.0 - ADAM_B1 ** ADAM_STEP)
        v_hat = nv / (1.0 - ADAM_B2 ** ADAM_STEP)
        g_ref[...] = g
        d_ref[...] = -ADAM_LR * (m_hat / (jnp.sqrt(v_hat) + ADAM_EPS) + ADAM_WD * w_ref[...])
        nm_ref[...] = nm
        nv_ref[...] = nv

    blk = pl.BlockSpec((tr, C), lambda i: (i, 0))
    return _pcall(
        body, name=name, out_shape=[_sds((R, C))] * 4, grid=(R // tr,),
        in_specs=[pl.BlockSpec((P, tr, C), lambda i: (0, i, 0)), blk, blk, blk],
        out_specs=[blk] * 4, sem=("parallel",))(parts, w, m, v)


def _adam_update(g, w, m, v):
    nm = ADAM_B1 * m + (1.0 - ADAM_B1) * g
    nv = ADAM_B2 * v + (1.0 - ADAM_B2) * (g * g)
    m_hat = nm / (1.0 - ADAM_B1 ** ADAM_STEP)
    v_hat = nv / (1.0 - ADAM_B2 ** ADAM_STEP)
    return -ADAM_LR * (m_hat / (jnp.sqrt(v_hat) + ADAM_EPS) + ADAM_WD * w), nm, nv


class _Canvas:
    def __init__(self):
        self.offset, self.views, self.rows = {}, {}, 0

    def add(self, key, r, c, blocks=1):
        need = r // blocks if blocks > 1 else r * (-(-c // D))
        if need >= 8:
            self.rows = -(-self.rows // 8) * 8
        self.offset[key], self.views[key] = self.rows, (r, c, blocks)
        self.rows += need

    @property
    def total(self):
        return -(-self.rows // 8) * 8

    def cells(self, key):
        (r, c, blocks), off = self.views[key], self.offset[key]
        if blocks > 1:
            n = r // blocks
            return [(off, n, b * c, c, slice(b * n, (b + 1) * n), slice(0, c)) for b in range(blocks)]
        if c <= D:
            return [(off, r, 0, c, slice(0, r), slice(0, c))]
        chunks = -(-c // D)
        return [(off + i * chunks + j, 1, 0, min(D, c - j * D), slice(i, i + 1), slice(j * D, min(c, (j + 1) * D)))
                for i in range(r) for j in range(chunks)]


def _canvas_fill(canvas, sources, name):
    arrays = [a for a, _ in sources]

    def body(*refs):
        out_ref = refs[-1]
        out_ref[...] = jnp.zeros_like(out_ref)
        for ref, (_, items) in zip(refs[:-1], sources):
            for key, src_rows, view_row in items:
                n = src_rows.stop - src_rows.start
                for row, count, lane, width, vrows, vcols in canvas.cells(key):
                    lo, hi = max(vrows.start, view_row), min(vrows.stop, view_row + n)
                    if lo < hi:
                        src = slice(src_rows.start + lo - view_row, src_rows.start + hi - view_row)
                        out_ref[row + lo - vrows.start:row + hi - vrows.start, lane:lane + width] = ref[src, vcols]

    return _pcall(body, name=name, out_shape=_sds((canvas.total, D)))(*arrays)


def _adamw_canvas(canvas, parts, params, sum_only, name):
    n = len(params)

    def body(p_ref, *refs):
        sum_ref = refs[-1]
        g_all = p_ref[0]
        for k in range(1, N_DEV):
            g_all = g_all + p_ref[k]
        sum_ref[...] = g_all
        for i, (key, _, _, _) in enumerate(params):
            w_ref, m_ref, v_ref = refs[3 * i:3 * i + 3]
            outs = refs[3 * n + 4 * i:3 * n + 4 * i + 4]
            for row, count, lane, width, vrows, vcols in canvas.cells(key):
                g = sum_ref[row:row + count, lane:lane + width]
                delta, nm, nv = _adam_update(g, w_ref[vrows, vcols], m_ref[vrows, vcols], v_ref[vrows, vcols])
                for ref, val in zip(outs, (g, delta, nm, nv)):
                    ref[vrows, vcols] = val
        for i, key in enumerate(sum_only):
            for row, count, lane, width, vrows, vcols in canvas.cells(key):
                refs[7 * n + i][vrows, vcols] = sum_ref[row:row + count, lane:lane + width]

    args = [a for _, w, m, v in params for a in (w, m, v)]
    out_shape = [_sds(w.shape) for _, w, _, _ in params for _ in range(4)] + [_sds(canvas.views[k][:2]) for k in sum_only]
    res = _pcall(body, name=name, out_shape=out_shape, scratch=[pltpu.VMEM(parts.shape[1:], F32)])(parts, *args)
    out = {key: res[4 * i:4 * i + 4] for i, (key, _, _, _) in enumerate(params)}
    out.update({key: [res[4 * n + i]] for i, key in enumerate(sum_only)})
    return out


def _tile_rows(n, pref):
    best = None
    for t in range(8, pref + 1, 8):
        if n % t == 0:
            best = t
    assert best is not None, (n, pref)
    return best


def _mm(pairs, mode, *, name, out_dtype=F32, bias=None, tn_pref=1024, side=None):
    M = pairs[0][0].shape[0]
    N = pairs[0][1].shape[1] if mode == "nn" else pairs[0][1].shape[0]
    tm, tn = _rows(M), _tile(N, tn_pref)
    n_pairs = len(pairs)
    has_bias = bias is not None

    def body(*refs):
        acc = _pairs_dot(refs[:2 * n_pairs], mode)
        if has_bias:
            acc = acc + refs[2 * n_pairs][...]
        refs[-1][...] = acc.astype(refs[-1].dtype)

    in_specs, args = _pair_specs(pairs, mode, tm, tn)
    if has_bias:
        in_specs.append(pl.BlockSpec((1, tn), lambda j, i: (0, j)))
        args.append(bias)
    res = _call(body, args, side, name=name, out_shape=[_sds((M, N), out_dtype)], grid=(N // tn, M // tm),
                in_specs=in_specs, out_specs=[pl.BlockSpec((tm, tn), lambda j, i: (i, j))], sem=("parallel", "parallel"))
    return res[0] if side is None else (res[0], res[1])


def _pairs_dot(ab, mode):
    acc = None
    for p in range(len(ab) // 2):
        a, b = _bf(ab[2 * p][...]), _bf(ab[2 * p + 1][...])
        d = _nn(a, b) if mode == "nn" else _nt(a, b)
        acc = d if acc is None else acc + d
    return acc


def _pair_specs(pairs, mode, tm, tn):
    in_specs, args = [], []
    for a, b in pairs:
        K = a.shape[1]
        in_specs.append(pl.BlockSpec((tm, K), lambda j, i: (i, 0)))
        if mode == "nn":
            in_specs.append(pl.BlockSpec((K, tn), lambda j, i: (0, j)))
        else:
            in_specs.append(pl.BlockSpec((tn, K), lambda j, i: (j, 0)))
        args += [a, b]
    return in_specs, args


def _mm_resid(pairs, resid, gvec, *, name, bias=None, norm=None):
    M, N = resid.shape
    tm = _rows(M)
    n_pairs = len(pairs)
    has_bias, has_norm = bias is not None, norm is not None

    def body(*refs):
        acc = _pairs_dot(refs[:2 * n_pairs], "nn")
        pos = 2 * n_pairs
        if has_bias:
            acc = acc + refs[pos][...]
            pos += 1
        xv = refs[pos][...] + refs[pos + 1][...] * acc
        outs = refs[pos + 2 + 3 * has_norm:]
        outs[0][...] = xv
        outs[1][...] = acc.astype(BF16)
        if has_norm:
            w_ref, sc_ref, sh_ref = refs[pos + 2:pos + 5]
            r = lax.rsqrt(jnp.mean(xv * xv, axis=-1, keepdims=True) + EPS)
            outs[2][...] = ((xv * r * w_ref[...]) * (1.0 + sc_ref[...]) + sh_ref[...]).astype(BF16)

    in_specs, args = _pair_specs(pairs, "nn", tm, N)
    vec = pl.BlockSpec((1, N), lambda j, i: (0, 0))
    row = pl.BlockSpec((tm, N), lambda j, i: (i, 0))
    if has_bias:
        in_specs.append(vec)
        args.append(bias)
    in_specs += [row, vec] + [vec] * (3 * has_norm)
    args += [resid, gvec] + (list(norm) if has_norm else [])
    return _pcall(body, name=name, out_shape=[_sds((M, N)), _sds((M, N), BF16)] + [_sds((M, N), BF16)] * has_norm,
                  grid=(1, M // tm), in_specs=in_specs, out_specs=[row] * (2 + has_norm),
                  sem=("parallel", "parallel"))(*args)


def _mm_tn(a, b, *, name, tm_pref=1408, tn_pref=1536):
    K, M = a.shape
    N = b.shape[1]
    tm, tn = _tile(M, tm_pref), _tile(N, tn_pref)
    tk = K if K <= 4 * ROW_TILE else 4 * ROW_TILE

    def body(a_ref, b_ref, o_ref):
        @pl.when(pl.program_id(2) == 0)
        def _():
            o_ref[...] = jnp.zeros_like(o_ref)
        o_ref[...] += _tn(_bf(a_ref[...]), _bf(b_ref[...]))

    return _pcall(
        body, name=name, out_shape=_sds((M, N)), grid=(M // tm, N // tn, K // tk),
        in_specs=[pl.BlockSpec((tk, tm), lambda i, j, k: (k, i)), pl.BlockSpec((tk, tn), lambda i, j, k: (k, j))],
        out_specs=pl.BlockSpec((tm, tn), lambda i, j, k: (i, j)),
        sem=("parallel", "parallel", "arbitrary"))(a, b)


def _mm_swiglu(h, wg_t, wu_t, name, side=None):
    M, K = h.shape
    N = wg_t.shape[0]
    tm, tn = _rows(M), _tile(N, 1408)

    def body(h_ref, wg_ref, wu_ref, gate_ref, up_ref, act_ref):
        hv = _bf(h_ref[...])
        gate = _nt(hv, wg_ref[...])
        up = _nt(hv, wu_ref[...])
        gate_ref[...] = gate.astype(BF16)
        up_ref[...] = up.astype(BF16)
        act_ref[...] = (_silu(gate) * up).astype(BF16)

    w_spec = pl.BlockSpec((tn, K), lambda j, i: (j, 0))
    o_spec = pl.BlockSpec((tm, tn), lambda j, i: (i, j))
    return _call(body, (h, wg_t, wu_t), side, name=name,
                 out_shape=[_sds((M, N), BF16)] * 3, grid=(N // tn, M // tm),
                 in_specs=[pl.BlockSpec((tm, K), lambda j, i: (i, 0)), w_spec, w_spec], out_specs=[o_spec] * 3,
                 sem=("parallel", "parallel"))


def _mm_swiglu_bwd(dout, wd, gate, up, name, side=None):
    M, K = dout.shape
    N = wd.shape[0]
    tm, tn = _rows(M), _tile(N, 1408)

    def body(d_ref, wd_ref, gate_ref, up_ref, dg_ref, du_ref):
        dact = _nt(_bf(d_ref[...]), wd_ref[...])
        act, dact_dg = _silu_and_grad(gate_ref[...].astype(F32))
        dg_ref[...] = (dact * up_ref[...].astype(F32) * dact_dg).astype(BF16)
        du_ref[...] = (dact * act).astype(BF16)

    t_spec = pl.BlockSpec((tm, tn), lambda j, i: (i, j))
    return _call(
        body, (dout, wd, gate, up), side, name=name, out_shape=[_sds((M, N), BF16)] * 2, grid=(N // tn, M // tm),
        in_specs=[pl.BlockSpec((tm, K), lambda j, i: (i, 0)), pl.BlockSpec((tn, K), lambda j, i: (j, 0)), t_spec, t_spec],
        out_specs=[t_spec] * 2, sem=("parallel", "parallel"))


def _norm_mod(x, w, sc, sh, name, side=None):
    T = x.shape[0]
    tm = _rows(T)

    def body(x_ref, w_ref, sc_ref, sh_ref, o_ref):
        xv = x_ref[...]
        r = lax.rsqrt(jnp.mean(xv * xv, axis=-1, keepdims=True) + EPS)
        o_ref[...] = ((xv * r * w_ref[...]) * (1.0 + sc_ref[...]) + sh_ref[...]).astype(BF16)

    return _call(body, (x, w, sc, sh), side, name=name, out_shape=[_sds((T, D), BF16)], grid=(T // tm,),
                 in_specs=[_row_spec(tm, D), _vec_spec(D), _vec_spec(D), _vec_spec(D)],
                 out_specs=[_row_spec(tm, D)], sem=("parallel",))


def _gate_rows(dxv, br_ref, g_ref, db_ref, acc_ref):
    db = g_ref[...] * dxv
    db_ref[...] = db.astype(BF16)
    acc_ref[3:4, :] += _colsum(dxv * br_ref[...].astype(F32))
    acc_ref[4:5, :] += _colsum(db)


def _norm_mod_bwd(x, dh, dres, w, sc, branch, g, name, side=None):
    T = x.shape[0]
    tm = _rows(T)
    gated = branch is not None

    def body(x_ref, dh_ref, dres_ref, w_ref, sc_ref, *rest):
        (br_ref, g_ref, dx_ref, db_ref, acc_ref) = rest if gated else (None, None, rest[0], None, rest[1])

        @pl.when(pl.program_id(0) == 0)
        def _():
            acc_ref[...] = jnp.zeros_like(acc_ref)
        xv, dh_v, wv = x_ref[...], dh_ref[...], w_ref[...]
        r = lax.rsqrt(jnp.mean(xv * xv, axis=-1, keepdims=True) + EPS)
        n = xv * r
        dnw = dh_v * (1.0 + sc_ref[...])
        dn = dnw * wv
        dxv = dres_ref[...] + r * (dn - n * jnp.mean(dn * n, axis=-1, keepdims=True))
        dx_ref[...] = dxv
        acc_ref[0:1, :] += _colsum(dh_v * (n * wv))
        acc_ref[1:2, :] += _colsum(dh_v)
        acc_ref[2:3, :] += _colsum(dnw * n)
        if gated:
            _gate_rows(dxv, br_ref, g_ref, db_ref, acc_ref)

    row = _row_spec(tm, D)
    args = (x, dh, dres, w, sc) + ((branch, g) if gated else ())
    return _call(body, args, side, name=name,
                 out_shape=[_sds((T, D))] + ([_sds((T, D), BF16)] if gated else []) + [_sds((8, D))], grid=(T // tm,),
                 in_specs=[row, row, row, _vec_spec(D), _vec_spec(D)] + ([row, _vec_spec(D)] if gated else []),
                 out_specs=[row] + ([row] if gated else []) + [_vec_spec(D, 8)], sem=("arbitrary",))


def _final_loss(x, wf, target, branch, g, name):
    T = x.shape[0]
    tm = _rows(T)

    def body(x_ref, w_ref, t_ref, br_ref, g_ref, dx_ref, db_ref, acc_ref):
        @pl.when(pl.program_id(0) == 0)
        def _():
            acc_ref[...] = jnp.zeros_like(acc_ref)
        xv, wv = x_ref[...], w_ref[...]
        r = lax.rsqrt(jnp.mean(xv * xv, axis=-1, keepdims=True) + EPS)
        n = xv * r
        err = n * wv - t_ref[...]
        dy = err * (1.0 / D)
        dn = dy * wv
        dxv = r * (dn - n * jnp.mean(dn * n, axis=-1, keepdims=True))
        dx_ref[...] = dxv
        acc_ref[0:1, :] += _colsum(dy * n)
        acc_ref[1:2, :] += jnp.broadcast_to(_allsum(err * err) * (0.5 / D), (1, D))
        _gate_rows(dxv, br_ref, g_ref, db_ref, acc_ref)

    row = _row_spec(tm, D)
    return _pcall(body, name=name, out_shape=[_sds((T, D)), _sds((T, D), BF16), _sds((8, D))], grid=(T // tm,),
                  in_specs=[row, _vec_spec(D), row, row, _vec_spec(D)],
                  out_specs=[row, row, _vec_spec(D, 8)], sem=("arbitrary",))(x, wf, target, branch, g)


def _mod_matmul(c_all, ada_w, name):
    n = ada_w.shape[2]

    def body(c_ref, w_ref, cond_ref, o_ref):
        cond = _silu(c_ref[...])
        cond_ref[...] = cond
        o_ref[0] = _nn(cond, w_ref[0])

    return _pcall(body, name=name, out_shape=[_sds((N_DEV, D)), _sds((DEPTH, N_DEV, n))], grid=(DEPTH,),
                  in_specs=[pl.BlockSpec((N_DEV, D), lambda l: (0, 0)), pl.BlockSpec((1, D, n), lambda l: (l, 0, 0))],
                  out_specs=[pl.BlockSpec((N_DEV, D), lambda l: (0, 0)), pl.BlockSpec((1, N_DEV, n), lambda l: (l, 0, 0))],
                  sem=("arbitrary",))(c_all, ada_w)


def _add_rows(a, b, name):
    def body(a_ref, b_ref, o_ref):
        o_ref[...] = a_ref[...] + b_ref[...]

    return _pcall(body, name=name, out_shape=_sds(a.shape))(a, b)


def _ada_w_grad(cond, dmod_cols, name):
    n = dmod_cols.shape[2]

    def body(c_ref, d_ref, o_ref):
        o_ref[0] = _tn(c_ref[...], d_ref[0])

    return _pcall(body, name=name, out_shape=_sds((DEPTH, D, n)), grid=(DEPTH,),
                  in_specs=[pl.BlockSpec((N_DEV, D), lambda l: (0, 0)), pl.BlockSpec((1, N_DEV, n), lambda l: (l, 0, 0))],
                  out_specs=pl.BlockSpec((1, D, n), lambda l: (l, 0, 0)), sem=("parallel",))(cond, dmod_cols)


def _conv_fwd(pm, conv_w, conv_b, name):
    T = pm.shape[0]
    tm = _rows(T)
    C = CONV_DIM

    def body(x_ref, prev_ref, w_ref, b_ref, o_ref):
        cur = x_ref[...].astype(F32)
        prev = jnp.where(pl.program_id(0) > 0, prev_ref[...].astype(F32)[8:16], 0.0)
        cur8 = cur[0:8]
        row8 = lax.broadcasted_iota(jnp.int32, (8, C), 0)
        full = w_ref[3:4, :] * cur
        head = w_ref[3:4, :] * cur8
        for k in range(1, SSM_CONV):
            wk = w_ref[3 - k:4 - k, :]
            full = full + wk * pltpu.roll(cur, k, 0)
            head = head + wk * jnp.where(row8 < k, pltpu.roll(prev, k, 0), pltpu.roll(cur8, k, 0))
        o_ref[...] = full + b_ref[...]
        o_ref[0:8, :] = head + b_ref[...]

    return _pcall(
        body, name=name, out_shape=_sds((T, C)), grid=(T // tm,),
        in_specs=[pl.BlockSpec((tm, C), lambda i: (i, 2)),
                  pl.BlockSpec((16, C), lambda i: (jnp.maximum(i * (tm // 16) - 1, 0), 2)),
                  _vec_spec(C, SSM_CONV), _vec_spec(C)],
        out_specs=_row_spec(tm, C), sem=("parallel",))(pm, pm, conv_w, conv_b)


def _conv_bwd(dc, pm, conv_w, name, side=None):
    T = dc.shape[0]
    tm = _rows(T)
    C = CONV_DIM
    nt = T // tm

    def body(dc_ref, nxt_ref, x_ref, w_ref, dx_ref, acc_ref):
        i = pl.program_id(0)

        @pl.when(i == 0)
        def _():
            acc_ref[...] = jnp.zeros_like(acc_ref)
        dcv = dc_ref[...]
        nxt = jnp.where(i < nt - 1, nxt_ref[...], 0.0)
        xc = x_ref[...].astype(F32)
        dc8t, x8t = dcv[tm - 8:tm], xc[tm - 8:tm]
        row8 = lax.broadcasted_iota(jnp.int32, (8, C), 0)
        full = w_ref[3:4, :] * dcv
        tail = w_ref[3:4, :] * dc8t
        acc_ref[3:4, :] += _colsum(dcv * xc)
        for k in range(1, SSM_CONV):
            wk = w_ref[3 - k:4 - k, :]
            up = pltpu.roll(dcv, tm - k, 0)
            up_tail = jnp.where(row8 + k >= 8, pltpu.roll(nxt, 8 - k, 0), pltpu.roll(dc8t, 8 - k, 0))
            full = full + wk * up
            tail = tail + wk * up_tail
            prod = up * xc
            acc_ref[3 - k:4 - k, :] += _colsum(prod) - _colsum(prod[tm - 8:tm]) + _colsum(up_tail * x8t)
        acc_ref[4:5, :] += _colsum(dcv)
        dx_ref[...] = jnp.concatenate([full[0:tm - 8], tail], axis=0).astype(BF16)

    return _call(
        body, (dc, dc, pm, conv_w), side, name=name, out_shape=[_sds((T, C), BF16), _sds((8, C))], grid=(nt,),
        in_specs=[_row_spec(tm, C),
                  pl.BlockSpec((8, C), lambda i: (jnp.minimum((i + 1) * (tm // 8), T // 8 - 1), 0)),
                  pl.BlockSpec((tm, C), lambda i: (i, 2)),
                  _vec_spec(C, SSM_CONV)],
        out_specs=[_row_spec(tm, C), _vec_spec(C, 8)], sem=("arbitrary",))


def _ssd_prologue(cpre, dtr, dtb, alog):
    L = CHUNK
    xc = _silu(cpre)
    pre = dtr + dtb
    dt = jnp.maximum(pre, 0.0) + jnp.log1p(jnp.exp(-jnp.abs(pre)))
    a = -jnp.exp(alog)
    la = dt * a
    row = lax.broadcasted_iota(jnp.int32, (L, L), 0)
    col = lax.broadcasted_iota(jnp.int32, (L, L), 1)
    causal = row >= col
    tri = causal.astype(F32)
    lc = _nn(tri, la, HIGHEST)
    return xc, pre, dt, a, causal, tri, lc, row, col


def _head_indicator():
    m = np.zeros((LANES, SSM_INNER), np.float32)
    for h in range(SSM_HEADS):
        m[h, h * SSM_HEAD_DIM:(h + 1) * SSM_HEAD_DIM] = 1.0
    return jnp.asarray(m, dtype=BF16)


def _split_dot(x, ind, dims):
    hi = x.astype(BF16)
    lo = (x - hi.astype(F32)).astype(BF16)
    return _dot(hi, ind, dims) + _dot(lo, ind, dims)


def _expand(x16, ind):
    return _split_dot(x16, ind, ((1,), (0,)))


def _headsum(x, ind, single_pass=False):
    if single_pass:
        return _dot(x.astype(BF16), ind, ((1,), (1,)))
    return _split_dot(x, ind, ((1,), (1,)))


def _ssd_fwd(cpre, dtr, pm, dtb, alog, dskip, normw, ind, name, side=None):
    T = cpre.shape[0]
    nc = T // CHUNK
    L, P, H, HPG, N = CHUNK, SSM_HEAD_DIM, SSM_HEADS, SSM_HEADS // SSM_GROUPS, SSM_STATE
    half = SSM_INNER // SSM_GROUPS

    def body(cp_ref, dtr_ref, z_ref, dtb_ref, alog_ref, dskip_ref, nw_ref, ind_ref, ya_ref, y_ref, sp_ref, st_ref):
        @pl.when(pl.program_id(0) == 0)
        def _():
            st_ref[...] = jnp.zeros_like(st_ref)
        xc, _, dt, _, causal, _, lc, _, _ = _ssd_prologue(cp_ref[...], dtr_ref[...], dtb_ref[...], alog_ref[...])
        lct = lc.T
        ind = ind_ref[...]
        llast = lc[L - 1:L, :]
        xs = xc[:, :SSM_INNER]
        xd = xs * _expand(dt, ind)
        ex = _expand(jnp.exp(lc), ind)
        xd_end = _bf(xd * _expand(jnp.exp(llast - lc), ind))
        cdx = _expand(jnp.broadcast_to(jnp.exp(llast), (8, LANES)), ind)[0:1]
        xdb = _bf(xd)
        sp_ref[0] = st_ref[...]
        for g in range(SSM_GROUPS):
            sl = slice(g * half, (g + 1) * half)
            bm = _bf(xc[:, SSM_INNER + g * N:SSM_INNER + (g + 1) * N])
            cm = _bf(xc[:, SSM_INNER + (SSM_GROUPS + g) * N:SSM_INNER + (SSM_GROUPS + g + 1) * N])
            cb = _nt(cm, bm)
            st = st_ref[g]
            y_ref[:, sl] = ex[:, sl] * _nn(cm, _bf(st)) + dskip_ref[:, sl] * xs[:, sl]
            st_ref[g] = st * cdx[:, sl] + _tn(bm, xd_end[:, sl])
            for j in range(HPG):
                h = g * HPG + j
                decay = jnp.where(causal, jnp.exp(jnp.where(causal, lc[:, h:h + 1] - lct[h:h + 1, :], 0.0)), 0.0)
                y_ref[:, h * P:(h + 1) * P] += _nn(_bf(cb * decay), xdb[:, h * P:(h + 1) * P])
        y2 = y_ref[...] * _silu(z_ref[...].astype(F32))
        for g in range(SSM_GROUPS):
            yg = y2[:, g * half:(g + 1) * half]
            r = lax.rsqrt(jnp.mean(yg * yg, axis=-1, keepdims=True) + EPS)
            ya_ref[:, g * half:(g + 1) * half] = (yg * r * nw_ref[:, g * half:(g + 1) * half]).astype(BF16)

    return _call(
        body, (cpre, dtr, pm, dtb, alog, dskip, normw, ind), side, name=name,
        out_shape=[_sds((T, SSM_INNER), BF16), _sds((T, SSM_INNER)), _sds((nc, SSM_GROUPS, N, half))], grid=(nc,),
        in_specs=[_row_spec(L, CONV_DIM), _row_spec(L, LANES), _row_spec(L, SSM_INNER, 0),
                  _vec_spec(LANES), _vec_spec(LANES), _vec_spec(SSM_INNER), _vec_spec(SSM_INNER), _vec_spec(SSM_INNER, LANES)],
        out_specs=[_row_spec(L, SSM_INNER), _row_spec(L, SSM_INNER),
                   pl.BlockSpec((1, SSM_GROUPS, N, half), lambda i: (i, 0, 0, 0))],
        scratch=[pltpu.VMEM((SSM_GROUPS, N, half), F32)], sem=("arbitrary",))


def _ssd_bwd(cpre, dtr, pm, ypre, sprev, dya, dtb, alog, dskip, normw, ind, name, side=None):
    T = cpre.shape[0]
    nc = T // CHUNK
    L, P, H, HPG, N = CHUNK, SSM_HEAD_DIM, SSM_HEADS, SSM_HEADS // SSM_GROUPS, SSM_STATE
    half = SSM_INNER // SSM_GROUPS

    def body(cp_ref, dtr_ref, z_ref, y_ref, sp_ref, dya_ref, dtb_ref, alog_ref, dskip_ref, nw_ref, ind_ref,
             dz_ref, dcp_ref, ddtr_ref, acc_ref, dnw_ref, ds_ref, dy_ref, dxd_ref, rr_ref, yoff_ref, dcd_ref):
        @pl.when(pl.program_id(0) == 0)
        def _():
            ds_ref[...] = jnp.zeros_like(ds_ref)
            acc_ref[...] = jnp.zeros_like(acc_ref)
            dnw_ref[...] = jnp.zeros_like(dnw_ref)
        cpre_v = cp_ref[...]
        xc, pre, dt, a, causal, tri, lc, row, col = _ssd_prologue(cpre_v, dtr_ref[...], dtb_ref[...], alog_ref[...])
        lct = lc.T
        zv, yv = z_ref[...].astype(F32), y_ref[...]
        sz, dsz = _silu_and_grad(zv)
        y2 = yv * sz
        dya_v = dya_ref[...]
        nwv = nw_ref[...]
        for g in range(SSM_GROUPS):
            sl = slice(g * half, (g + 1) * half)
            yg = y2[:, sl]
            r = lax.rsqrt(jnp.mean(yg * yg, axis=-1, keepdims=True) + EPS)
            nrm = yg * r
            dnw_ref[:, sl] += _colsum(dya_v[:, sl] * nrm)
            dn = dya_v[:, sl] * nwv[:, sl]
            dy2 = r * (dn - nrm * jnp.mean(dn * nrm, axis=-1, keepdims=True))
            dy_ref[:, sl] = dy2 * sz[:, sl]
            dz_ref[:, sl] = (dy2 * yv[:, sl] * dsz[:, sl]).astype(BF16)
        ind = ind_ref[...]
        llast = lc[L - 1:L, :]
        dte16 = jnp.exp(llast - lc)
        cd16 = jnp.exp(llast)
        xs = xc[:, :SSM_INNER]
        dtx = _expand(dt, ind)
        ex = _expand(jnp.exp(lc), ind)
        dtex = _expand(dte16, ind)
        cdx = _expand(jnp.broadcast_to(cd16, (8, LANES)), ind)[0:1]
        xd = xs * dtx
        xdb = _bf(xd)
        xd_end = _bf(xd * dtex)
        dyv = dy_ref[...]
        dy_off = _bf(ex * dyv)
        dyb = _bf(dyv)
        dskx = dskip_ref[...]
        lane_c = lax.broadcasted_iota(jnp.int32, (L, LANES), 1)
        lane1 = lax.broadcasted_iota(jnp.int32, (1, LANES), 1)
        sub16 = lax.broadcasted_iota(jnp.int32, (H, L), 0)
        dlc_c = jnp.zeros((L, LANES), F32)
        dlc_r = jnp.zeros((H, L), F32)
        for g in range(SSM_GROUPS):
            sl = slice(g * half, (g + 1) * half)
            b_lo = SSM_INNER + g * N
            c_lo = SSM_INNER + (SSM_GROUPS + g) * N
            bm, cm = _bf(xc[:, b_lo:b_lo + N]), _bf(xc[:, c_lo:c_lo + N])
            cb = _nt(cm, bm)
            st, dst = sp_ref[0, g], ds_ref[g]
            stb, dstb = _bf(st), _bf(dst)
            dcm = _nt(dy_off[:, sl], stb)
            ds_ref[g] = _tn(cm, dy_off[:, sl]) + dst * cdx[:, sl]
            rr_ref[:, sl] = _nn(bm, dstb)
            yoff_ref[:, sl] = ex[:, sl] * _nn(cm, stb)
            db = _nt(xd_end[:, sl], dstb)
            dcd_ref[:, sl] = _colsum(dst * st)
            dcb = jnp.zeros((L, L), F32)
            for j in range(HPG):
                h = g * HPG + j
                hs = slice(h * P, (h + 1) * P)
                decay = jnp.where(causal, jnp.exp(jnp.where(causal, lc[:, h:h + 1] - lct[h:h + 1, :], 0.0)), 0.0)
                m = cb * decay
                dxd_ref[:, hs] = _tn(_bf(m), dyb[:, hs])
                dm = _nt(dyb[:, hs], xdb[:, hs])
                dcb = dcb + dm * decay
                gm = dm * m
                dlc_c = dlc_c + jnp.where(lane_c == h, _rowsum(gm), 0.0)
                dlc_r = dlc_r + jnp.where(sub16 == h, _colsum(gm), 0.0)
            dcbb = _bf(dcb)
            dcp_ref[:, c_lo:c_lo + N] = dcm + _nn(dcbb, bm)
            dcp_ref[:, b_lo:b_lo + N] = db + _tn(dcbb, cm)
        dxd_diag, rr = dxd_ref[...], rr_ref[...]
        tt = _headsum(rr * xd, ind, single_pass=True) * dte16
        dlc_rt = jnp.concatenate([dlc_r, jnp.zeros((LANES - H, L), F32)], axis=0).T
        dlc = dlc_c - dlc_rt + _headsum(dyv * yoff_ref[...], ind, single_pass=True) - tt
        dcd = _headsum(jnp.broadcast_to(dcd_ref[...], (8, SSM_INNER)), ind)[0:1]
        dlc = dlc + jnp.where(row == L - 1, _colsum(tt) + dcd * cd16, 0.0)
        dla = _tn(tri, dlc, HIGHEST)
        dxd = dxd_diag + dtex * rr
        ddt = _headsum(dxd * xs, ind, single_pass=True) + dla * a
        ddtr = jnp.where(lane_c < H, ddt * _sigmoid(pre), 0.0)
        ddtr_ref[...] = ddtr
        acc_ref[0:1, :] += _colsum(ddtr)
        acc_ref[1:2, :] += jnp.where(lane1 < H, _colsum(dla * dt) * a, 0.0)
        acc_ref[2:3, :] += _headsum(jnp.broadcast_to(_colsum(dyv * xs), (8, SSM_INNER)), ind)[0:1]
        dcp_ref[:, 0:SSM_INNER] = dxd * dtx + dskx * dyv
        dcp_ref[...] = dcp_ref[...] * _dsilu(cpre_v)

    rev = lambda i: (nc - 1 - i, 0)
    rspec = lambda c: pl.BlockSpec((L, c), rev)
    return _call(
        body, (cpre, dtr, pm, ypre, sprev, dya, dtb, alog, dskip, normw, ind), side, name=name,
        out_shape=[_sds((T, SSM_INNER), BF16), _sds((T, CONV_DIM)), _sds((T, LANES)), _sds((8, LANES)), _sds((1, SSM_INNER))],
        grid=(nc,),
        in_specs=[rspec(CONV_DIM), rspec(LANES), rspec(SSM_INNER), rspec(SSM_INNER),
                  pl.BlockSpec((1, SSM_GROUPS, N, half), lambda i: (nc - 1 - i, 0, 0, 0)), rspec(SSM_INNER),
                  _vec_spec(LANES), _vec_spec(LANES), _vec_spec(SSM_INNER), _vec_spec(SSM_INNER), _vec_spec(SSM_INNER, LANES)],
        out_specs=[rspec(SSM_INNER), rspec(CONV_DIM), rspec(LANES), _vec_spec(LANES, 8), _vec_spec(SSM_INNER)],
        scratch=[pltpu.VMEM((SSM_GROUPS, N, half), F32), pltpu.VMEM((L, SSM_INNER), F32), pltpu.VMEM((L, SSM_INNER), F32),
                 pltpu.VMEM((L, SSM_INNER), F32), pltpu.VMEM((L, SSM_INNER), F32), pltpu.VMEM((1, SSM_INNER), F32)],
        sem=("arbitrary",))


def _gmlp_common(u, v, lnw, lnb, with_grads=False):
    (ug, dug), (vg, dvg) = (_gelu_and_grad(u), _gelu_and_grad(v)) if with_grads else ((_gelu(u), None), (_gelu(v), None))
    mu = jnp.mean(vg, axis=-1, keepdims=True)
    cen = vg - mu
    rstd = lax.rsqrt(jnp.mean(cen * cen, axis=-1, keepdims=True) + EPS)
    vhat = cen * rstd
    out = (ug, rstd, vhat, vhat * lnw + lnb)
    return out + (dug, dvg) if with_grads else out


def _causal_mask():
    row = lax.broadcasted_iota(jnp.int32, (CHUNK, CHUNK), 0)
    col = lax.broadcasted_iota(jnp.int32, (CHUNK, CHUNK), 1)
    return row >= col


def _gmlp_rows(T):
    return max(r for r in (8 * CHUNK, 4 * CHUNK, 2 * CHUNK, CHUNK) if T % r == 0)


def _gmlp_fwd(pm, lnw, lnb, ws, bs_exp, name, side=None):
    T = pm.shape[0]
    L, G = CHUNK, GMLP_GROUPS
    R = _gmlp_rows(T)

    def body(u_ref, v_ref, lnw_ref, lnb_ref, ws_ref, bs_ref, o_ref):
        ug, _, _, vn = _gmlp_common(u_ref[...].astype(F32), v_ref[...].astype(F32), lnw_ref[...], lnb_ref[...])
        causal = _causal_mask()
        for g in range(G):
            sl = slice(g * L, (g + 1) * L)
            wm = _bf(jnp.where(causal, ws_ref[g], 0.0))
            for c in range(R // L):
                rows = slice(c * L, (c + 1) * L)
                sv = _nn(wm, _bf(vn[rows, sl])) + bs_ref[:, sl]
                o_ref[rows, sl] = (ug[rows, sl] * sv).astype(BF16)

    return _call(
        body, (pm, pm, lnw, lnb, ws, bs_exp), side, name=name, out_shape=[_sds((T, GMLP_INNER), BF16)], grid=(T // R,),
        in_specs=[_row_spec(R, GMLP_INNER, 1), _row_spec(R, GMLP_INNER, 2), _vec_spec(GMLP_INNER), _vec_spec(GMLP_INNER),
                  pl.BlockSpec((G, L, L), lambda i: (0, 0, 0)), _vec_spec(GMLP_INNER, L)],
        out_specs=[_row_spec(R, GMLP_INNER)], sem=("parallel",))


def _gmlp_bwd(pm, dyb, lnw, lnb, ws, bs_exp, name, side=None):
    T = pm.shape[0]
    L, G = CHUNK, GMLP_GROUPS
    R = _gmlp_rows(T)

    def body(u_ref, v_ref, dy_ref, lnw_ref, lnb_ref, ws_ref, bs_ref, du_ref, dv_ref, dws_ref, dbs_ref, acc_ref, dvn_ref):
        @pl.when(pl.program_id(0) == 0)
        def _():
            dws_ref[...] = jnp.zeros_like(dws_ref)
            dbs_ref[...] = jnp.zeros_like(dbs_ref)
            acc_ref[...] = jnp.zeros_like(acc_ref)
        uv, vv, dyv, lnwv = u_ref[...].astype(F32), v_ref[...].astype(F32), dy_ref[...], lnw_ref[...]
        ug, rstd, vhat, vn, dug, dvg_act = _gmlp_common(uv, vv, lnwv, lnb_ref[...], with_grads=True)
        causal = _causal_mask()
        lane = lax.broadcasted_iota(jnp.int32, (L, LANES), 1)
        dbs = jnp.zeros((L, LANES), F32)
        for g in range(G):
            sl = slice(g * L, (g + 1) * L)
            wm = _bf(jnp.where(causal, ws_ref[g], 0.0))
            for c in range(R // L):
                rows = slice(c * L, (c + 1) * L)
                vng = _bf(vn[rows, sl])
                sv = _nn(wm, vng) + bs_ref[:, sl]
                du_ref[rows, sl] = (dyv[rows, sl] * sv * dug[rows, sl]).astype(BF16)
                dsv = dyv[rows, sl] * ug[rows, sl]
                dsvb = _bf(dsv)
                dws_ref[g] += jnp.where(causal, _nt(dsvb, vng), 0.0)
                dbs = dbs + jnp.where(lane == g, _rowsum(dsv), 0.0)
                dvn_ref[rows, sl] = _tn(wm, dsvb)
        dbs_ref[...] += dbs
        dvn = dvn_ref[...]
        acc_ref[0:1, :] += _colsum(dvn * vhat)
        acc_ref[1:2, :] += _colsum(dvn)
        dvh = dvn * lnwv
        dvg = rstd * (dvh - jnp.mean(dvh, axis=-1, keepdims=True) - vhat * jnp.mean(dvh * vhat, axis=-1, keepdims=True))
        dv_ref[...] = (dvg * dvg_act).astype(BF16)

    return _call(
        body, (pm, pm, dyb, lnw, lnb, ws, bs_exp), side, name=name,
        out_shape=[_sds((T, GMLP_INNER), BF16), _sds((T, GMLP_INNER), BF16), _sds((G, L, L)), _sds((L, LANES)), _sds((8, GMLP_INNER))],
        grid=(T // R,),
        in_specs=[_row_spec(R, GMLP_INNER, 1), _row_spec(R, GMLP_INNER, 2), _row_spec(R, GMLP_INNER),
                  _vec_spec(GMLP_INNER), _vec_spec(GMLP_INNER), pl.BlockSpec((G, L, L), lambda i: (0, 0, 0)),
                  _vec_spec(GMLP_INNER, L)],
        out_specs=[_row_spec(R, GMLP_INNER), _row_spec(R, GMLP_INNER), pl.BlockSpec((G, L, L), lambda i: (0, 0, 0)),
                   _vec_spec(LANES, L), _vec_spec(GMLP_INNER, 8)],
        scratch=[pltpu.VMEM((R, GMLP_INNER), F32)], sem=("arbitrary",))


def _rel_buckets():
    qi = np.arange(CHUNK)[:, None]
    sj = np.arange(2 * CHUNK)[None, :]
    dist = np.maximum(qi + CHUNK - sj, 0)
    max_exact = REL_BUCKETS // 2
    log_ratio = (np.log(np.maximum(dist, 1).astype(np.float32) / np.float32(max_exact))
                 / np.float32(math.log(REL_MAX_DIST / max_exact))).astype(np.float32)
    large = max_exact + (log_ratio * np.float32(REL_BUCKETS - max_exact)).astype(np.int32)
    return np.where(dist < max_exact, dist, np.minimum(large, REL_BUCKETS - 1))


def _bucket_onehot_t():
    bucket = _rel_buckets().reshape(-1)
    return jnp.asarray((np.arange(REL_BUCKETS)[:, None] == bucket[None, :]).astype(np.float32))


def _bias_from_table(table_t, onehot_t, window, name):
    def body(t_ref, o_ref, w_ref, out_ref):
        out_ref[...] = jnp.where(w_ref[...] > 0.5, _nn(t_ref[...], o_ref[...], HIGHEST), NEG_INF)

    return _pcall(body, name=name, out_shape=_sds((ATTN_HEADS, onehot_t.shape[1])))(table_t, onehot_t, window)


def _table_from_dbias(dbias, onehot_t, name):
    def body(d_ref, o_ref, out_ref):
        out_ref[...] = _nt(o_ref[...], d_ref[...], HIGHEST)

    return _pcall(body, name=name, out_shape=_sds((REL_BUCKETS, ATTN_HEADS)))(dbias, onehot_t)


def _softmax_sink(logits, sink):
    mx = jnp.maximum(jnp.max(logits, axis=-1, keepdims=True), sink)
    e = jnp.exp(logits - mx)
    es = jnp.exp(sink - mx)
    inv = 1.0 / (_rowsum(e) + es)
    return e * inv, es * inv


def _first_block_penalty(n):
    sj = lax.broadcasted_iota(jnp.int32, (1, 2 * CHUNK), 1)
    return jnp.where((sj < CHUNK) & (n == 0), NEG_INF, 0.0)


def _window_mask_flat():
    qi = np.arange(CHUNK)[:, None]
    sj = np.arange(2 * CHUNK)[None, :]
    rel = qi + CHUNK - sj
    return jnp.asarray(((rel >= 0) & (rel < CHUNK)).astype(np.float32).reshape(1, -1))


def _stack_heads(ref, first, count, width, scale=None):
    x = jnp.concatenate([_bf(ref[:, (first + j) * width:(first + j + 1) * width]) for j in range(count)], axis=0)
    return x if scale is None else x * jnp.asarray(scale, x.dtype)


def _attn_fwd(qkv, bias, sinks, name):
    T = qkv.shape[0]
    nb = T // CHUNK
    L, DH, HPK = CHUNK, ATTN_DH, ATTN_HEADS // ATTN_KV
    scale = DH ** -0.5
    kcol, vcol = ATTN_HEADS * DH // LANES, ATTN_HEADS * DH // LANES + 1

    def body(q_ref, k_ref, v_ref, kp_ref, vp_ref, bias_ref, sink_ref, o_ref, lg_ref, p_ref):
        n = pl.program_id(0)
        pen = _first_block_penalty(n)
        kband = _bf(jnp.concatenate([kp_ref[...], k_ref[...]], axis=0))
        vband = _bf(jnp.concatenate([vp_ref[...], v_ref[...]], axis=0))
        for kv in range(ATTN_KV):
            lg_ref[...] = _nt(_stack_heads(q_ref, kv * HPK, HPK, DH, scale), kband[:, kv * DH:(kv + 1) * DH])
            for j in range(HPK):
                h = kv * HPK + j
                p, _ = _softmax_sink(lg_ref[j * L:(j + 1) * L, :] + bias_ref[h] + pen, sink_ref[h])
                p_ref[j * L:(j + 1) * L, :] = _bf(p)
            og = _nn(p_ref[...], vband[:, kv * DH:(kv + 1) * DH])
            for j in range(HPK):
                h = kv * HPK + j
                o_ref[:, h * DH:(h + 1) * DH] = og[j * L:(j + 1) * L].astype(BF16)

    prev = lambda i: jnp.maximum(i - 1, 0)
    return _pcall(
        body, name=name, out_shape=_sds((T, ATTN_HEADS * DH), BF16), grid=(nb,),
        in_specs=[_row_spec(L, ATTN_HEADS * DH, 0), _row_spec(L, LANES, kcol), _row_spec(L, LANES, vcol),
                  pl.BlockSpec((L, LANES), lambda i: (prev(i), kcol)), pl.BlockSpec((L, LANES), lambda i: (prev(i), vcol)),
                  pl.BlockSpec((ATTN_HEADS, L, 2 * L), lambda i: (0, 0, 0)),
                  pl.BlockSpec(memory_space=pltpu.SMEM)],
        out_specs=_row_spec(L, ATTN_HEADS * DH),
        scratch=[pltpu.VMEM((HPK * L, 2 * L), F32), pltpu.VMEM((HPK * L, 2 * L), BF16)],
        sem=("parallel",))(qkv, qkv, qkv, qkv, qkv, bias, sinks)


def _attn_bwd(qkv, datt, bias, sinks, name, side=None):
    T = qkv.shape[0]
    nb = T // CHUNK
    L, DH, HPK = CHUNK, ATTN_DH, ATTN_HEADS // ATTN_KV
    scale = DH ** -0.5
    kcol, vcol = ATTN_HEADS * DH // LANES, ATTN_HEADS * DH // LANES + 1

    def body(q_ref, k_ref, v_ref, kp_ref, vp_ref, do_ref, bias_ref, sink_ref,
             dq_ref, dk_ref, dv_ref, bsum_ref, dbias_ref, dsink_ref, pend_k, pend_v, band_k, band_v, lg_ref, dp_ref, p_ref, dl_ref):
        n = pl.program_id(0)

        @pl.when(n == 0)
        def _():
            dbias_ref[...] = jnp.zeros_like(dbias_ref)
            dsink_ref[...] = jnp.zeros_like(dsink_ref)
            bsum_ref[...] = jnp.zeros_like(bsum_ref)

        def emit_kv(dk, dv):
            dk_ref[...] = dk.astype(BF16)
            dv_ref[...] = dv.astype(BF16)
            bsum_ref[:, ATTN_HEADS * DH:ATTN_HEADS * DH + LANES] += _colsum(dk)
            bsum_ref[:, ATTN_HEADS * DH + LANES:] += _colsum(dv)

        @pl.when(n < nb)
        def _():
            pen = _first_block_penalty(n)
            kband = _bf(jnp.concatenate([kp_ref[...], k_ref[...]], axis=0))
            vband = _bf(jnp.concatenate([vp_ref[...], v_ref[...]], axis=0))
            lane1 = lax.broadcasted_iota(jnp.int32, (1, LANES), 1)
            dsink = jnp.zeros((1, LANES), F32)
            for kv in range(ATTN_KV):
                kb, vb = kband[:, kv * DH:(kv + 1) * DH], vband[:, kv * DH:(kv + 1) * DH]
                qg = _stack_heads(q_ref, kv * HPK, HPK, DH, scale)
                dog = _stack_heads(do_ref, kv * HPK, HPK, DH)
                lg_ref[...] = _nt(qg, kb)
                dp_ref[...] = _nt(dog, vb)
                for j in range(HPK):
                    h = kv * HPK + j
                    rows = slice(j * L, (j + 1) * L)
                    p, ps = _softmax_sink(lg_ref[rows, :] + bias_ref[h] + pen, sink_ref[h])
                    dp = dp_ref[rows, :]
                    delta = _rowsum(p * dp)
                    dl = p * (dp - delta)
                    dbias_ref[h] += dl
                    p_ref[rows, :] = _bf(p)
                    dl_ref[rows, :] = _bf(dl)
                    dsink = dsink + jnp.where(lane1 == h, -_colsum(ps * delta), 0.0)
                band_v[:, kv * DH:(kv + 1) * DH] = _tn(p_ref[...], dog)
                dqg = _nn(dl_ref[...], kb) * scale
                band_k[:, kv * DH:(kv + 1) * DH] = _tn(dl_ref[...], qg)
                for j in range(HPK):
                    h = kv * HPK + j
                    dq_ref[:, h * DH:(h + 1) * DH] = dqg[j * L:(j + 1) * L].astype(BF16)
                    bsum_ref[:, h * DH:(h + 1) * DH] += _colsum(dqg[j * L:(j + 1) * L])
            dsink_ref[...] += dsink

            @pl.when(n > 0)
            def _():
                emit_kv(pend_k[...] + band_k[0:L, :], pend_v[...] + band_v[0:L, :])
            pend_k[...] = band_k[L:2 * L, :]
            pend_v[...] = band_v[L:2 * L, :]

        @pl.when(n == nb)
        def _():
            emit_kv(pend_k[...], pend_v[...])

    cur = lambda i: jnp.minimum(i, nb - 1)
    prev = lambda i: jnp.maximum(jnp.minimum(i, nb - 1) - 1, 0)
    lag = lambda i: jnp.maximum(i - 1, 0)
    return _call(
        body, (qkv, qkv, qkv, qkv, qkv, datt, bias, sinks), side, name=name,
        out_shape=[_sds((T, ATTN_HEADS * DH), BF16), _sds((T, LANES), BF16), _sds((T, LANES), BF16), _sds((1, QKV_DIM)),
                   _sds((ATTN_HEADS, L, 2 * L)), _sds((1, LANES))],
        grid=(nb + 1,),
        in_specs=[pl.BlockSpec((L, ATTN_HEADS * DH), lambda i: (cur(i), 0)),
                  pl.BlockSpec((L, LANES), lambda i: (cur(i), kcol)), pl.BlockSpec((L, LANES), lambda i: (cur(i), vcol)),
                  pl.BlockSpec((L, LANES), lambda i: (prev(i), kcol)), pl.BlockSpec((L, LANES), lambda i: (prev(i), vcol)),
                  pl.BlockSpec((L, ATTN_HEADS * DH), lambda i: (cur(i), 0)),
                  pl.BlockSpec((ATTN_HEADS, L, 2 * L), lambda i: (0, 0, 0)),
                  pl.BlockSpec(memory_space=pltpu.SMEM)],
        out_specs=[pl.BlockSpec((L, ATTN_HEADS * DH), lambda i: (cur(i), 0)),
                   pl.BlockSpec((L, LANES), lambda i: (lag(i), 0)), pl.BlockSpec((L, LANES), lambda i: (lag(i), 0)),
                   _vec_spec(QKV_DIM), pl.BlockSpec((ATTN_HEADS, L, 2 * L), lambda i: (0, 0, 0)), _vec_spec(LANES)],
        scratch=[pltpu.VMEM((L, LANES), F32), pltpu.VMEM((L, LANES), F32),
                 pltpu.VMEM((2 * L, LANES), F32), pltpu.VMEM((2 * L, LANES), F32),
                 pltpu.VMEM((HPK * L, 2 * L), F32), pltpu.VMEM((HPK * L, 2 * L), F32),
                 pltpu.VMEM((HPK * L, 2 * L), BF16), pltpu.VMEM((HPK * L, 2 * L), BF16)],
        sem=("arbitrary",))


def _pad_rows(a, mult):
    pad = (-a.shape[-2]) % mult
    if pad == 0:
        return a
    cfg = [(0, 0)] * (a.ndim - 2) + [(0, pad), (0, 0)]
    return jnp.pad(a, cfg)


class _Pack:
    def __init__(self, width, mult, total_mult):
        self.width, self.mult, self.total_mult = width, mult, total_mult
        self.entries = []
        self.rows = 0

    def add(self, key, shape):
        n = int(np.prod(shape))
        rows = -(-n // self.width)
        self.entries.append((key, self.rows, rows, tuple(shape)))
        self.rows += -(-rows // self.mult) * self.mult

    @property
    def total(self):
        return -(-self.rows // self.total_mult) * self.total_mult

    def pack(self, pieces, dtype, lead=()):
        parts = []
        for key, _, rows, shape in self.entries:
            a = pieces[key].astype(dtype).reshape(lead + (-1,))
            n = int(np.prod(shape))
            a = jnp.pad(a, [(0, 0)] * len(lead) + [(0, rows * self.width - n)])
            a = a.reshape(lead + (rows, self.width))
            parts.append(_pad_rows(a, self.mult))
        out = jnp.concatenate(parts, axis=len(lead))
        return _pad_rows(out, self.total_mult)

    def unpack(self, packed, lead=()):
        out = {}
        for key, off, rows, shape in self.entries:
            a = lax.slice_in_dim(packed, off, off + rows, axis=len(lead))
            a = a.reshape(lead + (-1,))
            n = int(np.prod(shape))
            out[key] = lax.slice_in_dim(a, 0, n, axis=len(lead)).reshape(lead + shape)
        return out


def _ffn_fwd(x, h, mod, wg_t, wu_t, wd, tag, next_norm=None, gather=None):
    side = None if gather is None else (_GatherOps, gather)
    gate, up, act, *gathered = _mm_swiglu(h, wg_t, wu_t, f"ffn_gateup_{tag}", side=side)
    x_out, ffn_out, *h_next = _mm_resid([(act, wd)], x, mod[5:6], name=f"ffn_down_{tag}", norm=next_norm)
    return (x_out, dict(h=h, gate=gate, up=up, act=act, out=ffn_out), *h_next, *gathered)


def _ffn_bwd(dx_out, dffn, x_in, saved, mod, norm_w, wg_t, wu_t, wd, below, tag, exchange=None):
    side = None if exchange is None else (_ChipsOps, exchange)
    dgate, dup, *from_chips = _mm_swiglu_bwd(dffn, wd, saved["gate"], saved["up"], f"ffn_act_bwd_{tag}", side=side)
    d_wd = _mm_tn(saved["act"], dffn, name=f"ffn_dwd_{tag}")
    d_wg_t = _mm_tn(dgate, saved["h"], name=f"ffn_dwg_{tag}")
    d_wu_t = _mm_tn(dup, saved["h"], name=f"ffn_dwu_{tag}")
    dh = _mm([(dgate, wg_t), (dup, wu_t)], "nn", name=f"ffn_dh_{tag}")
    dx, d_below, acc = _norm_mod_bwd(x_in, dh, dx_out, norm_w, mod[4:5], below[0], below[1], f"ffn_norm_bwd_{tag}")
    return (dx, d_below, dict(d_wg=d_wg_t, d_wu=d_wu_t, d_wd=d_wd, acc=acc), *from_chips)


_BIG = [
    ("out_w", "out_w_even", 0, "row"), ("qkv_w", "qkv_w", 0, "col"), ("o_w", "o_w", 0, "row"),
    ("gate0", "ffn_gate_w", 0, "col"), ("up0", "ffn_up_w", 0, "col"), ("down0", "ffn_down_w", 0, "row"),
    ("gate1", "ffn_gate_w", 1, "col"), ("up1", "ffn_up_w", 1, "col"), ("down1", "ffn_down_w", 1, "row"),
    ("in_w", "in_w_even", 0, "col"),
]


def _to_wire(a, kind):
    return a.T if kind == "col" else a


_GATHER_GROUPS = [["in_w"], ["out_w"], ["gate0", "up0", "down0"], ["qkv_w", "o_w"], ["gate1", "up1", "down1"]]
_GRAD_GROUPS = [["gate1", "up1", "down1"], ["qkv_w", "o_w"], ["out_w", "gate0", "up0", "down0"], ["in_w"]]

_REPLICATED = ["ada_b", "norm_mix_w", "norm_ffn_w", "conv_b", "dt_bias", "a_log", "d_skip", "ssm_norm_w", "gmlp_ln_w",
               "gmlp_ln_b", "gmlp_ws", "gmlp_bs", "sinks", "rel_table", "final_norm_w"]
_TINY_SHARDED = ["conv_w", "qkv_b", "o_b"]

_WEIGHTS = ['ada_w', 'ada_b', 'norm_mix_w', 'norm_ffn_w', 'in_w_even', 'conv_w', 'conv_b', 'dt_bias', 'a_log', 'd_skip',
            'ssm_norm_w', 'gmlp_ln_w', 'gmlp_ln_b', 'gmlp_ws', 'gmlp_bs', 'out_w_even', 'qkv_w', 'qkv_b', 'o_w', 'o_b',
            'sinks', 'rel_table', 'ffn_gate_w', 'ffn_up_w', 'ffn_down_w', 'final_norm_w']


def _step(x, c, loss_target, W, M, V):
    T = x.shape[1]
    x0 = x[0]
    target = loss_target[0]
    me = 4 * lax.axis_index("x") + 2 * lax.axis_index("y") + lax.axis_index("c")

    w_wire_local = {key: _to_wire(W[name][layer].astype(BF16), kind) for key, name, layer, kind in _BIG}

    def wire_pack(keys, mult):
        gp = _Pack(D, 1, mult)
        for key in keys:
            gp.add(key, w_wire_local[key].shape)
        return gp

    gather_packs = [(gp, gp.pack(w_wire_local, BF16)) for gp in (wire_pack(keys, 16) for keys in _GATHER_GROUPS)]
    grad_packs = [wire_pack(keys, 64) for keys in _GRAD_GROUPS]
    full = {}

    def gathered_weights(group, gathered):
        shards = gather_packs[group][0].unpack(gathered, lead=(N_DEV,))
        full.update({key: a.reshape(-1, D) for key, a in shards.items()})


    small_in = _Pack(D, 8, 8)
    small_in.add("c", (1, D))
    small_in.add("conv_w", W["conv_w"][0].shape)
    small_in.add("qkv_b", W["qkv_b"][0].shape)
    small_in.add("o_b", W["o_b"][0].shape)
    sm = small_in.unpack(_all_gather(small_in.pack(
        dict(c=c, conv_w=W["conv_w"][0], qkv_b=W["qkv_b"][0], o_b=W["o_b"][0]), F32), "gather_small"), lead=(N_DEV,))
    c_all = sm["c"].reshape(N_DEV, D)
    conv_w_full = jnp.transpose(sm["conv_w"], (1, 0, 2)).reshape(SSM_CONV, CONV_DIM)
    qkv_b_full = sm["qkv_b"].reshape(1, QKV_DIM)
    o_b_full = sm["o_b"].reshape(1, D)

    ncol = W["ada_w"].shape[2]
    cond, mod_cols = _mod_matmul(c_all, W["ada_w"], "mod_matmul")
    mod_g = _all_gather(mod_cols.reshape(DEPTH * N_DEV, ncol), "gather_mod").reshape(N_DEV, DEPTH, N_DEV, ncol)
    mod_me = lax.dynamic_index_in_dim(mod_g, me, axis=2, keepdims=False)
    mod_me = jnp.transpose(mod_me, (1, 0, 2)).reshape(DEPTH, 6, D)
    mod_me = jnp.pad(mod_me, ((0, 0), (0, 2), (0, 0))).reshape(DEPTH * 8, D)
    ada_b_rows = jnp.pad(W["ada_b"].reshape(DEPTH, 6, D), ((0, 0), (0, 2), (0, 0))).reshape(DEPTH * 8, D)
    mod_all = _add_rows(mod_me, ada_b_rows, "mod_bias").reshape(DEPTH, 8, D)
    mod0, mod1 = mod_all[0], mod_all[1]
    h0, gathered_in = _norm_mod(x0, W["norm_mix_w"][0:1], mod0[1:2], mod0[0:1], "mix_norm_0",
                                side=(_GatherOps, gather_packs[0][1]))
    gathered_weights(0, gathered_in)

    in_t = full["in_w"]
    o1, o2, o3, o4 = SSM_INNER, SSM_INNER + CONV_DIM, SSM_INNER + CONV_DIM + SSM_HEADS, SSM_INNER + CONV_DIM + SSM_HEADS + GMLP_INNER
    w_z, w_xbc, w_dt, w_u, w_v = in_t[:o1], in_t[o1:o2], in_t[o2:o3], in_t[o3:o4], in_t[o4:]
    w_main = jnp.concatenate([w_z, w_u, w_v, w_xbc], axis=0)
    w_dtp = jnp.pad(w_dt, ((0, LANES - SSM_HEADS), (0, 0)))

    pad16 = lambda a: jnp.pad(a.reshape(1, SSM_HEADS), ((0, 0), (0, LANES - SSM_HEADS)))
    dtb, alog = pad16(W["dt_bias"][0]), pad16(W["a_log"][0])
    dskip = jnp.repeat(W["d_skip"][0], SSM_HEAD_DIM).reshape(1, SSM_INNER)
    ssm_nw = W["ssm_norm_w"]
    lnw, lnb = W["gmlp_ln_w"], W["gmlp_ln_b"]
    ws = W["gmlp_ws"][0]
    bs_exp = jnp.repeat(W["gmlp_bs"][0].T, CHUNK, axis=1)
    conv_b = W["conv_b"]
    nmw, nfw = W["norm_mix_w"], W["norm_ffn_w"]
    onehot_t = _bucket_onehot_t()
    head_ind = _head_indicator()
    bias = _bias_from_table(W["rel_table"].T, onehot_t, _window_mask_flat(), "rel_bias").reshape(ATTN_HEADS, CHUNK, 2 * CHUNK)
    sinks = W["sinks"][0]

    pm, gathered_a = _mm([(h0, w_main)], "nt", name="in_proj", tn_pref=1536, out_dtype=BF16,
                         side=(_GatherOps, gather_packs[1][1]))
    gathered_weights(1, gathered_a)
    out_w = full["out_w"]
    dtr = _mm([(h0, w_dtp)], "nt", name="in_proj_dt")
    cpre = _conv_fwd(pm, conv_w_full, conv_b, "conv_fwd")
    ya, ypre, sprev, gathered_b = _ssd_fwd(cpre, dtr, pm, dtb, alog, dskip, ssm_nw, head_ind, "ssd_fwd",
                                           side=(_GatherOps, gather_packs[2][1]))
    gathered_weights(2, gathered_b)
    yb, gathered_c = _gmlp_fwd(pm, lnw, lnb, ws, bs_exp, "gmlp_fwd", side=(_GatherOps, gather_packs[3][1]))
    gathered_weights(3, gathered_c)
    x1, mix0, hf0 = _mm_resid([(ya, out_w[:SSM_INNER]), (yb, out_w[SSM_INNER:])], x0, mod0[2:3], name="out_proj",
                              norm=(nfw[0:1], mod0[4:5], mod0[3:4]))
    x2, ffn0, h1, gathered_d = _ffn_fwd(x1, hf0, mod0, full["gate0"], full["up0"], full["down0"], "0",
                                        next_norm=(nmw[1:2], mod1[1:2], mod1[0:1]), gather=gather_packs[4][1])
    gathered_weights(4, gathered_d)
    qkv_t, o_w = full["qkv_w"], full["o_w"]
    w_q, w_k, w_v_att = qkv_t[:D], qkv_t[D:D + LANES], qkv_t[D + LANES:]

    qkv = _mm([(h1, qkv_t)], "nt", name="qkv_proj", bias=qkv_b_full, tn_pref=1280, out_dtype=BF16)
    att = _attn_fwd(qkv, bias, sinks, "attn_fwd")
    x3, mix1, hf1 = _mm_resid([(att, o_w)], x2, mod1[2:3], name="o_proj", bias=o_b_full,
                              norm=(nfw[1:2], mod1[4:5], mod1[3:4]))
    x4, ffn1 = _ffn_fwd(x3, hf1, mod1, full["gate1"], full["up1"], full["down1"], "1")

    dx4, dffn1, acc_f = _final_loss(x4, W["final_norm_w"].reshape(1, D), target, ffn1["out"], mod1[5:6], "final_loss")
    dx3, dmix1, gf1 = _ffn_bwd(dx4, dffn1, x3, ffn1, mod1, nfw[1:2], full["gate1"], full["up1"], full["down1"],
                               (mix1, mod1[2:3]), "1")
    g_wire = dict(gate1=gf1["d_wg"], up1=gf1["d_wu"], down1=gf1["d_wd"])

    def packed_partials(group):
        return grad_packs[group].pack({key: g_wire[key].reshape(N_DEV, -1, D) for key in _GRAD_GROUPS[group]},
                                      F32, lead=(N_DEV,))

    from_chips = {}
    partials_ffn1 = packed_partials(0)
    datt, theirs_ffn1 = _mm([(dmix1, o_w)], "nt", name="o_proj_dx", out_dtype=BF16, side=(_SiblingOps, partials_ffn1))
    pair_ffn1 = _pair_sum(partials_ffn1, theirs_ffn1, "grads_pair_sum_ffn1")
    d_o_w = _mm_tn(att, dmix1, name="o_proj_dw")
    dq, dk, dv, d_qkv_b, dbias, dsinks, from_chips[0] = _attn_bwd(qkv, datt, bias, sinks, "attn_bwd",
                                                                 side=(_ChipsOps, pair_ffn1))
    d_table = _table_from_dbias(dbias.reshape(ATTN_HEADS, -1), onehot_t, "rel_table_grad")
    d_qkv_t = jnp.concatenate([_mm_tn(dq, h1, name="qkv_dw_q"), _mm_tn(dk, h1, name="qkv_dw_k"), _mm_tn(dv, h1, name="qkv_dw_v")], axis=0)
    g_wire.update(qkv_w=d_qkv_t, o_w=d_o_w)
    partials_l1 = packed_partials(1)
    dh1, theirs_l1 = _mm([(dq, w_q), (dk, w_k), (dv, w_v_att)], "nn", name="qkv_proj_dx", side=(_SiblingOps, partials_l1))
    pair_l1 = _pair_sum(partials_l1, theirs_l1, "grads_pair_sum_l1")
    dx2, dffn0, acc_n1 = _norm_mod_bwd(x2, dh1, dx3, nmw[1:2], mod1[1:2], ffn0["out"], mod0[5:6], "mix_norm_bwd_1")

    dx1, dmix0, gf0, from_chips[1] = _ffn_bwd(dx2, dffn0, x1, ffn0, mod0, nfw[0:1], full["gate0"], full["up0"], full["down0"],
                                              (mix0, mod0[2:3]), "0", exchange=pair_l1)

    dya = _mm([(dmix0, out_w[:SSM_INNER])], "nt", name="out_proj_dx_a")
    dyb = _mm([(dmix0, out_w[SSM_INNER:])], "nt", name="out_proj_dx_b")
    d_out_w = jnp.concatenate([_mm_tn(ya, dmix0, name="out_proj_dw_a"), _mm_tn(yb, dmix0, name="out_proj_dw_b")], axis=0)
    g_wire.update(gate0=gf0["d_wg"], up0=gf0["d_wu"], down0=gf0["d_wd"], out_w=d_out_w)
    partials_ffn0 = packed_partials(2)
    du, dvg, d_ws, d_bs, acc_ln, theirs_ffn0 = _gmlp_bwd(pm, dyb, lnw, lnb, ws, bs_exp, "gmlp_bwd",
                                                         side=(_SiblingOps, partials_ffn0))
    pair_ffn0 = _pair_sum(partials_ffn0, theirs_ffn0, "grads_pair_sum_mix")
    dz, dcpre, ddtr, acc_ssd, d_ssm_nw, from_chips[2] = _ssd_bwd(
        cpre, dtr, pm, ypre, sprev, dya, dtb, alog, dskip, ssm_nw, head_ind, "ssd_bwd", side=(_ChipsOps, pair_ffn0))
    acc_f0, acc_f1 = gf0["acc"], gf1["acc"]
    row = lambda i: slice(i, i + 1)
    views = dict(ada_b=(DEPTH * 6, D), norm_mix_w=(DEPTH, D), norm_ffn_w=(DEPTH, D), conv_b=(1, CONV_DIM),
                 dt_bias=(1, SSM_HEADS), a_log=(1, SSM_HEADS), d_skip=(1, SSM_HEADS), ssm_norm_w=(1, D), gmlp_ln_w=(1, D),
                 gmlp_ln_b=(1, D), gmlp_ws=(GMLP_GROUPS * CHUNK, CHUNK), gmlp_bs=(GMLP_GROUPS, CHUNK),
                 sinks=(1, ATTN_HEADS), rel_table=(REL_BUCKETS, ATTN_HEADS), final_norm_w=(1, D),
                 conv_w=(SSM_CONV, CONV_DIM), qkv_b=(1, QKV_DIM), o_b=(1, D), loss=(1, D))
    late_keys = ["ada_b", "norm_mix_w", "conv_b", "dt_bias", "a_log", "d_skip", "ssm_norm_w", "conv_w"]
    canvas_early, canvas_late = _Canvas(), _Canvas()
    for key, (r, cdim) in views.items():
        (canvas_late if key in late_keys else canvas_early).add(key, r, cdim, blocks=GMLP_GROUPS if key == "gmlp_ws" else 1)
    early_sources = [
        (acc_f0, [("norm_ffn_w", row(2), 0)]),
        (acc_f1, [("norm_ffn_w", row(2), 1), ("o_b", row(4), 0)]),
        (acc_f, [("final_norm_w", row(0), 0), ("loss", row(1), 0)]),
        (acc_ln, [("gmlp_ln_w", row(0), 0), ("gmlp_ln_b", row(1), 0)]),
        (d_ws.reshape(GMLP_GROUPS * CHUNK, CHUNK), [("gmlp_ws", slice(0, GMLP_GROUPS * CHUNK), 0)]),
        (d_bs.T, [("gmlp_bs", slice(0, GMLP_GROUPS), 0)]),
        (dsinks, [("sinks", row(0), 0)]),
        (d_table, [("rel_table", slice(0, REL_BUCKETS), 0)]),
        (d_qkv_b, [("qkv_b", row(0), 0)]),
    ]
    dxbc, acc_conv, parts_early = _conv_bwd(dcpre, pm, conv_w_full, "conv_bwd",
                                            side=(_GatherOps, _canvas_fill(canvas_early, early_sources, "small_grads_early")))
    d_in_t = jnp.concatenate([
        _mm_tn(dz, h0, name="in_dw_z"), _mm_tn(dxbc, h0, name="in_dw_xbc"),
        _mm_tn(ddtr, h0, name="in_dw_dt")[:SSM_HEADS], _mm_tn(du, h0, name="in_dw_u"), _mm_tn(dvg, h0, name="in_dw_v")], axis=0)
    g_wire.update(in_w=d_in_t)
    partials_mix = packed_partials(3)
    theirs_mix = _comm_call(_SiblingOps, partials_mix, "exchange_grads_sibling")
    pair_mix = _pair_sum(partials_mix, theirs_mix, "grads_pair_sum")
    dh0, from_chips[3] = _mm([(dz, w_z), (dxbc, w_xbc), (ddtr, w_dtp), (du, w_u), (dvg, w_v)], "nn", name="in_proj_dx",
                             side=(_ChipsOps, pair_mix))
    grad_x, acc_n0 = _norm_mod_bwd(x0, dh0, dx1, nmw[0:1], mod0[1:2], None, None, "mix_norm_bwd_0")

    g_mine = {}
    for group in range(len(_GRAD_GROUPS)):
        g_mine.update(grad_packs[group].unpack(_sum_parts(from_chips[group], f"grads_chip_sum_{group}")))
    res_big = [{}, {}, {}, {}]
    for key, name, layer, kind in _BIG:
        g_nat = _to_wire(g_mine[key], kind)
        outs = _adamw(g_nat[None], W[name][layer], M[name][layer], V[name][layer], f"adamw_{key}")
        for res, out in zip(res_big, outs):
            res[key] = out

    late_sources = [
        (acc_n0, [("ada_b", row(1), 0), ("ada_b", row(0), 1), ("norm_mix_w", row(2), 0)]),
        (acc_f0, [("ada_b", row(3), 2), ("ada_b", row(1), 3), ("ada_b", row(0), 4)]),
        (acc_n1, [("ada_b", row(3), 5), ("ada_b", row(1), 6), ("ada_b", row(0), 7), ("norm_mix_w", row(2), 1)]),
        (acc_f1, [("ada_b", row(3), 8), ("ada_b", row(1), 9), ("ada_b", row(0), 10)]),
        (acc_f, [("ada_b", row(3), 11)]),
        (acc_conv, [("conv_w", slice(0, SSM_CONV), 0), ("conv_b", row(4), 0)]),
        (acc_ssd, [("dt_bias", row(0), 0), ("a_log", row(1), 0), ("d_skip", row(2), 0)]),
        (d_ssm_nw, [("ssm_norm_w", row(0), 0)]),
    ]
    parts_small = _all_gather(_canvas_fill(canvas_late, late_sources, "small_grads_late"), "gather_small_grads")
    view = lambda a, key: a.reshape(views[key])
    params = lambda keys: [(n, view(W[n], n), view(M[n], n), view(V[n], n)) for n in _REPLICATED if (n in late_keys) == keys]
    small_out = _adamw_canvas(canvas_early, parts_early, params(False), ["qkv_b", "o_b", "loss"], "adamw_small_early")
    small_out.update(_adamw_canvas(canvas_late, parts_small, params(True), ["conv_w"], "adamw_small_late"))
    loss = small_out["loss"][0][0, 0]
    res_small = [{name: small_out[name][k].reshape(W[name].shape) for name in _REPLICATED} for k in range(4)]

    n_cw, n_qb, n_ob = W["conv_w"].shape[2], W["qkv_b"].shape[1], W["o_b"].shape[1]
    g_tiny = dict(conv_w=lax.dynamic_slice_in_dim(small_out["conv_w"][0], me * n_cw, n_cw, axis=1)[None],
                  qkv_b=lax.dynamic_slice_in_dim(small_out["qkv_b"][0], me * n_qb, n_qb, axis=1),
                  o_b=lax.dynamic_slice_in_dim(small_out["o_b"][0], me * n_ob, n_ob, axis=1))
    tiny = _Pack(D, 8, 8)
    for name in _TINY_SHARDED:
        tiny.add(name, W[name].shape)
    pkt = lambda S: tiny.pack({name: S[name] for name in _TINY_SHARDED}, F32)
    res_tiny = [tiny.unpack(r) for r in _adamw(pkt(g_tiny)[None], pkt(W), pkt(M), pkt(V), "adamw_tiny")]

    dmod_all = parts_small[:, canvas_late.offset["ada_b"]:canvas_late.offset["ada_b"] + DEPTH * 6].reshape(N_DEV, DEPTH, 6 * D)
    dmod_cols = jnp.transpose(lax.dynamic_slice_in_dim(dmod_all, me * ncol, ncol, axis=2), (1, 0, 2))
    g_ada_w = _ada_w_grad(cond, dmod_cols, "ada_w_grad")
    flat = lambda a: a.reshape(DEPTH * D, ncol)
    res_ada = [r.reshape(DEPTH, D, ncol) for r in _adamw(flat(g_ada_w)[None], flat(W["ada_w"]), flat(M["ada_w"]), flat(V["ada_w"]), "adamw_ada_w")]

    def result(kind_idx, name):
        if name == "ada_w":
            return res_ada[kind_idx]
        if name in _REPLICATED:
            return res_small[kind_idx][name]
        if name in _TINY_SHARDED:
            return res_tiny[kind_idx][name]
        pieces = [res_big[kind_idx][key] for key, nm, layer, kind in _BIG if nm == name]
        return jnp.stack(pieces)

    outs = [loss, grad_x[None]]
    for kind_idx in range(4):
        outs += [result(kind_idx, name) for name in _WEIGHTS]
    return tuple(outs)


def kernel(x, c, ada_w, ada_b, norm_mix_w, norm_ffn_w, in_w_even, conv_w, conv_b, dt_bias, a_log, d_skip, ssm_norm_w, gmlp_ln_w, gmlp_ln_b, gmlp_ws, gmlp_bs, out_w_even, qkv_w, qkv_b, o_w, o_b, sinks, rel_table, ffn_gate_w, ffn_up_w, ffn_down_w, final_norm_w, loss_target, m_ada_w, m_ada_b, m_norm_mix_w, m_norm_ffn_w, m_in_w_even, m_conv_w, m_conv_b, m_dt_bias, m_a_log, m_d_skip, m_ssm_norm_w, m_gmlp_ln_w, m_gmlp_ln_b, m_gmlp_ws, m_gmlp_bs, m_out_w_even, m_qkv_w, m_qkv_b, m_o_w, m_o_b, m_sinks, m_rel_table, m_ffn_gate_w, m_ffn_up_w, m_ffn_down_w, m_final_norm_w, v_ada_w, v_ada_b, v_norm_mix_w, v_norm_ffn_w, v_in_w_even, v_conv_w, v_conv_b, v_dt_bias, v_a_log, v_d_skip, v_ssm_norm_w, v_gmlp_ln_w, v_gmlp_ln_b, v_gmlp_ws, v_gmlp_bs, v_out_w_even, v_qkv_w, v_qkv_b, v_o_w, v_o_b, v_sinks, v_rel_table, v_ffn_gate_w, v_ffn_up_w, v_ffn_down_w, v_final_norm_w):
    args = locals()
    W = {n: args[n] for n in _WEIGHTS}
    M = {n: args["m_" + n] for n in _WEIGHTS}
    V = {n: args["v_" + n] for n in _WEIGHTS}
    return _step(x, c, loss_target, W, M, V)
```
